```python
import jax, jax.numpy as jnp
from jax import lax
import numpy as np

D_MODEL = 1024
BATCH = 8
SEQ = 4096
DEPTH = 1

CHUNK = 64
Q_BLOCK = 128
MIX_WIDTH = D_MODEL
MLA_HEADS = 4
QK_NOPE_DIM = 128
QK_ROPE_DIM = 64
V_HEAD_DIM = 128
Q_LORA_RANK = 256
KV_LORA_RANK = 128
MLA_WIDTH = MLA_HEADS * V_HEAD_DIM
POOL_WIDTH = MIX_WIDTH - MLA_WIDTH
POOL_WINDOWS = (2, 4, 8, 16)
POOL_GROUPS = len(POOL_WINDOWS)
POOL_GROUP_DIM = POOL_WIDTH // POOL_GROUPS
IN_COLS = Q_LORA_RANK + KV_LORA_RANK + QK_ROPE_DIM + POOL_WIDTH
D_FF = ((8 * D_MODEL + 3 * 256 - 1) // (3 * 256)) * 256
ROPE_THETA = 10000.0
EPS = 1e-6
SM_SCALE = (QK_NOPE_DIM + QK_ROPE_DIM) ** -0.5
N_MOD = 6

kernel_name = "hybrid_mla_pool_adaln_block"


def rmsnorm(x, g):
    xf = x.astype(jnp.float32)
    y = xf * lax.rsqrt(jnp.mean(xf * xf, axis=-1, keepdims=True) + EPS)
    return (y * g.astype(jnp.float32)).astype(x.dtype)


def rope_tables(positions, dtype):
    half = QK_ROPE_DIM // 2
    freqs = jnp.power(ROPE_THETA, -jnp.arange(half, dtype=jnp.float32) / half)
    ang = positions.astype(jnp.float32)[..., None] * freqs
    return jnp.cos(ang).astype(dtype), jnp.sin(ang).astype(dtype)


def apply_rope(x, cos, sin):
    x1, x2 = jnp.split(x, 2, axis=-1)
    return jnp.concatenate([x1 * cos - x2 * sin, x1 * sin + x2 * cos], axis=-1)


def mla_mixer(cq_raw, ckv_raw, kr_raw, positions, g_q, g_kv, w_uq, w_uk, w_uv):
    B, S, _ = ckv_raw.shape
    c_q = rmsnorm(cq_raw, g_q)
    c_kv = rmsnorm(ckv_raw, g_kv)
    q = jnp.einsum('bsr,rhd->bshd', c_q, w_uq)
    q_nope, q_rope = q[..., :QK_NOPE_DIM], q[..., QK_NOPE_DIM:]
    cos, sin = rope_tables(positions, q.dtype)
    q_rope = apply_rope(q_rope, cos[:, :, None, :], sin[:, :, None, :])
    k_rope = apply_rope(kr_raw, cos, sin)
    q_lat = jnp.einsum('bshd,chd->bshc', q_nope, w_uk)
    nblk = S // Q_BLOCK
    q_lat_b = q_lat.reshape(B, nblk, Q_BLOCK, MLA_HEADS, KV_LORA_RANK).transpose(1, 0, 2, 3, 4)
    q_rope_b = q_rope.reshape(B, nblk, Q_BLOCK, MLA_HEADS, QK_ROPE_DIM).transpose(1, 0, 2, 3, 4)
    key_chunk = jnp.arange(S) // CHUNK

    def block(args):
        ql, qr, blk = args
        s = (jnp.einsum('bqhc,bkc->bhqk', ql, c_kv)
             + jnp.einsum('bqhr,bkr->bhqk', qr, k_rope)).astype(jnp.float32) * SM_SCALE
        q_chunk = (blk * Q_BLOCK + jnp.arange(Q_BLOCK)) // CHUNK
        mask = key_chunk[None, :] <= q_chunk[:, None]
        s = jnp.where(mask[None, None], s, -jnp.inf)
        p = jax.nn.softmax(s, axis=-1).astype(c_kv.dtype)
        o_lat = jnp.einsum('bhqk,bkc->bqhc', p, c_kv)
        o = jnp.einsum('bqhc,chv->bqhv', o_lat, w_uv)
        return o.reshape(B, Q_BLOCK, MLA_WIDTH)

    out = lax.map(block, (q_lat_b, q_rope_b, jnp.arange(nblk)))
    return out.transpose(1, 0, 2, 3).reshape(B, S, MLA_WIDTH)


def pool_mixer(u, w_pool, pool_scale):
    B, S, _ = u.shape
    ug = u.reshape(B, S, POOL_GROUPS, POOL_GROUP_DIM)
    cs = jnp.cumsum(ug.astype(jnp.float32), axis=1)
    t = jnp.arange(1, S + 1, dtype=jnp.float32)
    pooled = []
    for g, w in enumerate(POOL_WINDOWS):
        csg = cs[:, :, g]
        lagged = jnp.pad(csg, ((0, 0), (w, 0), (0, 0)))[:, :S]
        count = jnp.minimum(t, float(w))[None, :, None]
        pooled.append((csg - lagged) / count)
    pooled = jnp.stack(pooled, axis=2).astype(u.dtype) - ug
    y = jnp.einsum('bsgc,gcd->bsgd', pooled, w_pool).reshape(B, S, POOL_WIDTH)
    return y * pool_scale


def _fwd_setup_inputs(seed: int = 0) -> dict:
    key = jax.random.key(seed)
    ks = jax.random.split(key, 24)
    f32 = jnp.float32
    nrm = lambda k, shape, s: jax.random.normal(k, shape, f32) * s
    gain = lambda k, shape: 1.0 + 0.05 * jax.random.normal(k, shape, f32)
    x = jax.random.normal(ks[0], (BATCH, SEQ, D_MODEL), f32)
    c = jax.random.normal(ks[1], (BATCH, D_MODEL), f32)
    offset = jax.random.randint(ks[2], (BATCH, 1), 0, 8192, dtype=jnp.int32)
    positions = offset + jnp.arange(SEQ, dtype=jnp.int32)[None, :]
    return {
        "x": x,
        "c": c,
        "positions": positions,
        "w_ada": nrm(ks[3], (DEPTH, D_MODEL, N_MOD * D_MODEL), 0.5 * D_MODEL ** -0.5),
        "b_ada": nrm(ks[4], (DEPTH, N_MOD * D_MODEL), 0.02),
        "g_mix": gain(ks[5], (DEPTH, D_MODEL)),
        "w_in": nrm(ks[6], (DEPTH, D_MODEL, IN_COLS), D_MODEL ** -0.5),
        "g_q": gain(ks[7], (DEPTH, Q_LORA_RANK)),
        "g_kv": gain(ks[8], (DEPTH, KV_LORA_RANK)),
        "w_uq": nrm(ks[9], (DEPTH, Q_LORA_RANK, MLA_HEADS, QK_NOPE_DIM + QK_ROPE_DIM), Q_LORA_RANK ** -0.5),
        "w_uk": nrm(ks[10], (DEPTH, KV_LORA_RANK, MLA_HEADS, QK_NOPE_DIM), KV_LORA_RANK ** -0.5),
        "w_uv": nrm(ks[11], (DEPTH, KV_LORA_RANK, MLA_HEADS, V_HEAD_DIM), KV_LORA_RANK ** -0.5),
        "w_pool": nrm(ks[12], (DEPTH, POOL_GROUPS, POOL_GROUP_DIM, POOL_GROUP_DIM), POOL_GROUP_DIM ** -0.5),
        "pool_scale": gain(ks[13], (DEPTH, POOL_WIDTH)),
        "w_o": nrm(ks[14], (DEPTH, MIX_WIDTH, D_MODEL), MIX_WIDTH ** -0.5),
        "g_ffn": gain(ks[15], (DEPTH, D_MODEL)),
        "w_gate": nrm(ks[16], (DEPTH, D_MODEL, D_FF), D_MODEL ** -0.5),
        "w_up": nrm(ks[17], (DEPTH, D_MODEL, D_FF), D_MODEL ** -0.5),
        "w_down": nrm(ks[18], (DEPTH, D_FF, D_MODEL), D_FF ** -0.5),
        "g_final": gain(ks[19], (D_MODEL,)),
    }


def _fwd_reference(x, c, positions, w_ada, b_ada, g_mix, w_in, g_q, g_kv, w_uq, w_uk,
              w_uv, w_pool, pool_scale, w_o, g_ffn, w_gate, w_up, w_down, g_final):
    c_act = jax.nn.silu(c)
    for l in range(DEPTH):
        mod = (c_act @ w_ada[l] + b_ada[l])[:, None, :]
        sh1, sc1, gt1, sh2, sc2, gt2 = jnp.split(mod, N_MOD, axis=-1)

        h = rmsnorm(x, g_mix[l]) * (1.0 + sc1) + sh1
        proj = h @ w_in[l]
        o1 = Q_LORA_RANK
        o2 = o1 + KV_LORA_RANK
        o3 = o2 + QK_ROPE_DIM
        y_mla = mla_mixer(proj[..., :o1], proj[..., o1:o2], proj[..., o2:o3], positions,
                          g_q[l], g_kv[l], w_uq[l], w_uk[l], w_uv[l])
        y_pool = pool_mixer(proj[..., o3:], w_pool[l], pool_scale[l])
        mix = jnp.concatenate([y_mla, y_pool], axis=-1) @ w_o[l]
        x = x + gt1 * mix

        h = rmsnorm(x, g_ffn[l]) * (1.0 + sc2) + sh2
        ff = (jax.nn.silu(h @ w_gate[l]) * (h @ w_up[l])) @ w_down[l]
        x = x + gt2 * ff
    return rmsnorm(x, g_final)


import jax as _jax
import jax.numpy as _jnp

TWIN_FORMAT = 'train_step'
FWD_PARAMS = ['x', 'c', 'positions', 'w_ada', 'b_ada', 'g_mix', 'w_in', 'g_q', 'g_kv', 'w_uq', 'w_uk', 'w_uv', 'w_pool', 'pool_scale', 'w_o', 'g_ffn', 'w_gate', 'w_up', 'w_down', 'g_final']
TWIN_WEIGHTS = ['w_ada', 'b_ada', 'g_mix', 'w_in', 'g_q', 'g_kv', 'w_uq', 'w_uk', 'w_uv', 'w_pool', 'pool_scale', 'w_o', 'g_ffn', 'w_gate', 'w_up', 'w_down', 'g_final']
TWIN_DIFF_INPUT = 'x'
TWIN_INPUTS = ['x', 'c', 'positions', 'w_ada', 'b_ada', 'g_mix', 'w_in', 'g_q', 'g_kv', 'w_uq', 'w_uk', 'w_uv', 'w_pool', 'pool_scale', 'w_o', 'g_ffn', 'w_gate', 'w_up', 'w_down', 'g_final', 'loss_target', 'm_w_ada', 'm_b_ada', 'm_g_mix', 'm_w_in', 'm_g_q', 'm_g_kv', 'm_w_uq', 'm_w_uk', 'm_w_uv', 'm_w_pool', 'm_pool_scale', 'm_w_o', 'm_g_ffn', 'm_w_gate', 'm_w_up', 'm_w_down', 'm_g_final', 'v_w_ada', 'v_b_ada', 'v_g_mix', 'v_w_in', 'v_g_q', 'v_g_kv', 'v_w_uq', 'v_w_uk', 'v_w_uv', 'v_w_pool', 'v_pool_scale', 'v_w_o', 'v_g_ffn', 'v_w_gate', 'v_w_up', 'v_w_down', 'v_g_final']
TWIN_OUTPUTS = ['loss', 'grad_x', 'grad_w_ada', 'grad_b_ada', 'grad_g_mix', 'grad_w_in', 'grad_g_q', 'grad_g_kv', 'grad_w_uq', 'grad_w_uk', 'grad_w_uv', 'grad_w_pool', 'grad_pool_scale', 'grad_w_o', 'grad_g_ffn', 'grad_w_gate', 'grad_w_up', 'grad_w_down', 'grad_g_final', 'delta_w_ada', 'delta_b_ada', 'delta_g_mix', 'delta_w_in', 'delta_g_q', 'delta_g_kv', 'delta_w_uq', 'delta_w_uk', 'delta_w_uv', 'delta_w_pool', 'delta_pool_scale', 'delta_w_o', 'delta_g_ffn', 'delta_w_gate', 'delta_w_up', 'delta_w_down', 'delta_g_final', 'new_m_w_ada', 'new_m_b_ada', 'new_m_g_mix', 'new_m_w_in', 'new_m_g_q', 'new_m_g_kv', 'new_m_w_uq', 'new_m_w_uk', 'new_m_w_uv', 'new_m_w_pool', 'new_m_pool_scale', 'new_m_w_o', 'new_m_g_ffn', 'new_m_w_gate', 'new_m_w_up', 'new_m_w_down', 'new_m_g_final', 'new_v_w_ada', 'new_v_b_ada', 'new_v_g_mix', 'new_v_w_in', 'new_v_g_q', 'new_v_g_kv', 'new_v_w_uq', 'new_v_w_uk', 'new_v_w_uv', 'new_v_w_pool', 'new_v_pool_scale', 'new_v_w_o', 'new_v_g_ffn', 'new_v_w_gate', 'new_v_w_up', 'new_v_w_down', 'new_v_g_final']
TWIN_LEAF_KINDS = {'loss': 'loss', 'grad_x': 'grad_x', 'grad_w_ada': 'grad_w', 'grad_b_ada': 'grad_w', 'grad_g_mix': 'grad_w', 'grad_w_in': 'grad_w', 'grad_g_q': 'grad_w', 'grad_g_kv': 'grad_w', 'grad_w_uq': 'grad_w', 'grad_w_uk': 'grad_w', 'grad_w_uv': 'grad_w', 'grad_w_pool': 'grad_w', 'grad_pool_scale': 'grad_w', 'grad_w_o': 'grad_w', 'grad_g_ffn': 'grad_w', 'grad_w_gate': 'grad_w', 'grad_w_up': 'grad_w', 'grad_w_down': 'grad_w', 'grad_g_final': 'grad_w', 'delta_w_ada': 'delta_w', 'delta_b_ada': 'delta_w', 'delta_g_mix': 'delta_w', 'delta_w_in': 'delta_w', 'delta_g_q': 'delta_w', 'delta_g_kv': 'delta_w', 'delta_w_uq': 'delta_w', 'delta_w_uk': 'delta_w', 'delta_w_uv': 'delta_w', 'delta_w_pool': 'delta_w', 'delta_pool_scale': 'delta_w', 'delta_w_o': 'delta_w', 'delta_g_ffn': 'delta_w', 'delta_w_gate': 'delta_w', 'delta_w_up': 'delta_w', 'delta_w_down': 'delta_w', 'delta_g_final': 'delta_w', 'new_m_w_ada': 'new_m', 'new_m_b_ada': 'new_m', 'new_m_g_mix': 'new_m', 'new_m_w_in': 'new_m', 'new_m_g_q': 'new_m', 'new_m_g_kv': 'new_m', 'new_m_w_uq': 'new_m', 'new_m_w_uk': 'new_m', 'new_m_w_uv': 'new_m', 'new_m_w_pool': 'new_m', 'new_m_pool_scale': 'new_m', 'new_m_w_o': 'new_m', 'new_m_g_ffn': 'new_m', 'new_m_w_gate': 'new_m', 'new_m_w_up': 'new_m', 'new_m_w_down': 'new_m', 'new_m_g_final': 'new_m', 'new_v_w_ada': 'new_v', 'new_v_b_ada': 'new_v', 'new_v_g_mix': 'new_v', 'new_v_w_in': 'new_v', 'new_v_g_q': 'new_v', 'new_v_g_kv': 'new_v', 'new_v_w_uq': 'new_v', 'new_v_w_uk': 'new_v', 'new_v_w_uv': 'new_v', 'new_v_w_pool': 'new_v', 'new_v_pool_scale': 'new_v', 'new_v_w_o': 'new_v', 'new_v_g_ffn': 'new_v', 'new_v_w_gate': 'new_v', 'new_v_w_up': 'new_v', 'new_v_w_down': 'new_v', 'new_v_g_final': 'new_v'}


def _forward(args):
    return _fwd_reference(*[args[k] for k in FWD_PARAMS])


def _output_shape():
    out = _jax.eval_shape(lambda: _forward(_fwd_setup_inputs(0)))
    return out.shape, out.dtype

N_MICROBATCH = 1
ADAM_LR = 0.001
ADAM_B1 = 0.9
ADAM_B2 = 0.999
ADAM_EPS = 1e-08
ADAM_WD = 0.01
ADAM_STEP = 10
PER_EXAMPLE_BATCH_AXIS = {'x': 0, 'c': 0, 'positions': 0, 'loss_target': 0}
SHARED_INPUTS = []
_WEIGHT_DTYPES = {'w_ada': _jnp.float32, 'b_ada': _jnp.float32, 'g_mix': _jnp.float32, 'w_in': _jnp.float32, 'g_q': _jnp.float32, 'g_kv': _jnp.float32, 'w_uq': _jnp.float32, 'w_uk': _jnp.float32, 'w_uv': _jnp.float32, 'w_pool': _jnp.float32, 'pool_scale': _jnp.float32, 'w_o': _jnp.float32, 'g_ffn': _jnp.float32, 'w_gate': _jnp.float32, 'w_up': _jnp.float32, 'w_down': _jnp.float32, 'g_final': _jnp.float32}
MOMENT_SCALE = {'w_ada': 6.609831e-02, 'b_ada': 1.144177e-01, 'g_mix': 3.726753e-02, 'w_in': 4.017162e-02, 'g_q': 1.130255e-02, 'g_kv': 4.417573e-02, 'w_uq': 6.742302e-03, 'w_uk': 6.849222e-03, 'w_uv': 1.867457e-02, 'w_pool': 5.171079e-02, 'pool_scale': 5.217932e-02, 'w_o': 3.827799e-02, 'g_ffn': 5.680220e-02, 'w_gate': 2.404717e-02, 'w_up': 2.341400e-02, 'w_down': 3.854101e-02, 'g_final': 3.203493e+01}


def _to_microbatches(a, axis):
    t = _jnp.moveaxis(a, axis, 0)
    t = t.reshape((N_MICROBATCH, t.shape[0] // N_MICROBATCH) + t.shape[1:])
    return _jnp.moveaxis(t, 1, axis + 1)


def setup_inputs(seed: int = 0) -> dict:
    inp = _fwd_setup_inputs(seed)
    key = _jax.random.fold_in(_jax.random.key(seed), 7919)
    shape, _ = _output_shape()
    out = dict(inp)
    out["loss_target"] = _jax.random.normal(_jax.random.fold_in(key, 0), shape, _jnp.float32)
    for i, name in enumerate(TWIN_WEIGHTS):
        w = inp[name].astype(_jnp.float32)
        if MOMENT_SCALE is None:
            s = _jnp.sqrt(_jnp.mean(_jnp.square(w)) + 1e-30)
        else:
            s = MOMENT_SCALE[name]
        km, kv = _jax.random.split(_jax.random.fold_in(key, i + 1))
        out[name] = w
        out["m_" + name] = s * _jax.random.normal(km, w.shape, _jnp.float32)
        out["v_" + name] = (s * s) * _jax.random.uniform(kv, w.shape, _jnp.float32, 0.5, 1.5)
    if N_MICROBATCH > 1:
        for name, axis in PER_EXAMPLE_BATCH_AXIS.items():
            out[name] = _to_microbatches(out[name], axis)
    return {'x': out['x'], 'c': out['c'], 'positions': out['positions'], 'w_ada': out['w_ada'], 'b_ada': out['b_ada'], 'g_mix': out['g_mix'], 'w_in': out['w_in'], 'g_q': out['g_q'], 'g_kv': out['g_kv'], 'w_uq': out['w_uq'], 'w_uk': out['w_uk'], 'w_uv': out['w_uv'], 'w_pool': out['w_pool'], 'pool_scale': out['pool_scale'], 'w_o': out['w_o'], 'g_ffn': out['g_ffn'], 'w_gate': out['w_gate'], 'w_up': out['w_up'], 'w_down': out['w_down'], 'g_final': out['g_final'], 'loss_target': out['loss_target'], 'm_w_ada': out['m_w_ada'], 'm_b_ada': out['m_b_ada'], 'm_g_mix': out['m_g_mix'], 'm_w_in': out['m_w_in'], 'm_g_q': out['m_g_q'], 'm_g_kv': out['m_g_kv'], 'm_w_uq': out['m_w_uq'], 'm_w_uk': out['m_w_uk'], 'm_w_uv': out['m_w_uv'], 'm_w_pool': out['m_w_pool'], 'm_pool_scale': out['m_pool_scale'], 'm_w_o': out['m_w_o'], 'm_g_ffn': out['m_g_ffn'], 'm_w_gate': out['m_w_gate'], 'm_w_up': out['m_w_up'], 'm_w_down': out['m_w_down'], 'm_g_final': out['m_g_final'], 'v_w_ada': out['v_w_ada'], 'v_b_ada': out['v_b_ada'], 'v_g_mix': out['v_g_mix'], 'v_w_in': out['v_w_in'], 'v_g_q': out['v_g_q'], 'v_g_kv': out['v_g_kv'], 'v_w_uq': out['v_w_uq'], 'v_w_uk': out['v_w_uk'], 'v_w_uv': out['v_w_uv'], 'v_w_pool': out['v_w_pool'], 'v_pool_scale': out['v_pool_scale'], 'v_w_o': out['v_w_o'], 'v_g_ffn': out['v_g_ffn'], 'v_w_gate': out['v_w_gate'], 'v_w_up': out['v_w_up'], 'v_w_down': out['v_w_down'], 'v_g_final': out['v_g_final']}


def _loss(weights, diff, rest, loss_target):
    with _jax.named_scope("forward"):
        args = {**rest, TWIN_DIFF_INPUT: diff, **{k: w.astype(_WEIGHT_DTYPES[k]) for k, w in weights.items()}}
        y = _forward(args)
    with _jax.named_scope("loss_head"):
        err = _jnp.square(y.astype(_jnp.float32) - loss_target)
        return 0.5 * _jnp.sum(_jnp.mean(err, axis=-1)) if err.ndim else 0.5 * err


def _adamw(w, g, m, v):
    m = ADAM_B1 * m + (1.0 - ADAM_B1) * g
    v = ADAM_B2 * v + (1.0 - ADAM_B2) * _jnp.square(g)
    m_hat = m / (1.0 - ADAM_B1 ** ADAM_STEP)
    v_hat = v / (1.0 - ADAM_B2 ** ADAM_STEP)
    delta = -ADAM_LR * (m_hat / (_jnp.sqrt(v_hat) + ADAM_EPS) + ADAM_WD * w)
    return delta, m, v


def reference(x, c, positions, w_ada, b_ada, g_mix, w_in, g_q, g_kv, w_uq, w_uk, w_uv, w_pool, pool_scale, w_o, g_ffn, w_gate, w_up, w_down, g_final, loss_target, m_w_ada, m_b_ada, m_g_mix, m_w_in, m_g_q, m_g_kv, m_w_uq, m_w_uk, m_w_uv, m_w_pool, m_pool_scale, m_w_o, m_g_ffn, m_w_gate, m_w_up, m_w_down, m_g_final, v_w_ada, v_b_ada, v_g_mix, v_w_in, v_g_q, v_g_kv, v_w_uq, v_w_uk, v_w_uv, v_w_pool, v_pool_scale, v_w_o, v_g_ffn, v_w_gate, v_w_up, v_w_down, v_g_final):
    given = dict(x=x, c=c, positions=positions, w_ada=w_ada, b_ada=b_ada, g_mix=g_mix, w_in=w_in, g_q=g_q, g_kv=g_kv, w_uq=w_uq, w_uk=w_uk, w_uv=w_uv, w_pool=w_pool, pool_scale=pool_scale, w_o=w_o, g_ffn=g_ffn, w_gate=w_gate, w_up=w_up, w_down=w_down, g_final=g_final, loss_target=loss_target, m_w_ada=m_w_ada, m_b_ada=m_b_ada, m_g_mix=m_g_mix, m_w_in=m_w_in, m_g_q=m_g_q, m_g_kv=m_g_kv, m_w_uq=m_w_uq, m_w_uk=m_w_uk, m_w_uv=m_w_uv, m_w_pool=m_w_pool, m_pool_scale=m_pool_scale, m_w_o=m_w_o, m_g_ffn=m_g_ffn, m_w_gate=m_w_gate, m_w_up=m_w_up, m_w_down=m_w_down, m_g_final=m_g_final, v_w_ada=v_w_ada, v_b_ada=v_b_ada, v_g_mix=v_g_mix, v_w_in=v_w_in, v_g_q=v_g_q, v_g_kv=v_g_kv, v_w_uq=v_w_uq, v_w_uk=v_w_uk, v_w_uv=v_w_uv, v_w_pool=v_w_pool, v_pool_scale=v_pool_scale, v_w_o=v_w_o, v_g_ffn=v_g_ffn, v_w_gate=v_w_gate, v_w_up=v_w_up, v_w_down=v_w_down, v_g_final=v_g_final)
    weights = {n: given[n] for n in TWIN_WEIGHTS}
    shared = {n: given[n] for n in SHARED_INPUTS}
    per_example = {n: given[n] for n in ['x', 'c', 'positions']}
    grad_fn = _jax.value_and_grad(_loss, argnums=(0, 1))

    def one_microbatch(ex, loss_target):
        ex = dict(ex)
        diff = ex.pop(TWIN_DIFF_INPUT)
        return grad_fn(weights, diff, {**shared, **ex}, loss_target)

    if N_MICROBATCH == 1:
        loss, (grad_w, grad_x) = one_microbatch(per_example, given["loss_target"])
    else:
        def body(carry, xs):
            loss_sum, grad_sum = carry
            l_k, (gw_k, gx_k) = one_microbatch(xs[0], xs[1])
            with _jax.named_scope("update"):
                return (loss_sum + l_k, _jax.tree.map(_jnp.add, grad_sum, gw_k)), gx_k

        init = (_jnp.zeros((), _jnp.float32), _jax.tree.map(_jnp.zeros_like, weights))
        (loss, grad_w), grad_x = _jax.lax.scan(body, init, (per_example, given["loss_target"]))
    with _jax.named_scope("update"):
        delta_w, new_m, new_v = {}, {}, {}
        for n in TWIN_WEIGHTS:
            delta_w[n], new_m[n], new_v[n] = _adamw(weights[n], grad_w[n], given["m_" + n], given["v_" + n])
    return (loss, grad_x, *[grad_w[n] for n in TWIN_WEIGHTS], *[delta_w[n] for n in TWIN_WEIGHTS],
            *[new_m[n] for n in TWIN_WEIGHTS], *[new_v[n] for n in TWIN_WEIGHTS])
```

```python
import functools

import numpy as np
import jax
import jax.numpy as jnp
from jax import lax
from jax.experimental import pallas as pl
from jax.experimental.pallas import tpu as pltpu

D = 1024
HEADS = 4
NOPE = 128
ROPE = 64
HALF = ROPE // 2
QL = 256
KVL = 128
FF = 2816
PW = 512
GROUPS = 4
GD = 128
N_MOD = 6
EPS = 1e-6
SM_SCALE = (NOPE + ROPE) ** -0.5
ROPE_THETA = 10000.0
NDEV = 8
MODC = N_MOD * D // NDEV

ADAM_LR = 0.001
ADAM_B1 = 0.9
ADAM_B2 = 0.999
ADAM_EPS = 1e-08
ADAM_WD = 0.01
ADAM_STEP = 10

BF = jnp.bfloat16
F32 = jnp.float32
VMEM_LIMIT_V7X = 60 * 1024 * 1024
MESH = pl.DeviceIdType.MESH

TQ = 256
TK = 256
QW = 256
MOD_ROWS = 8
REP_ROWS = 200
SMALL_ROWS = MOD_ROWS + REP_ROWS


def _params(sem=None):
    return pltpu.CompilerParams(dimension_semantics=sem, vmem_limit_bytes=VMEM_LIMIT_V7X)


def _dot(a, b):
    return jnp.dot(a, b, preferred_element_type=F32)


def _dot_nt(a, b):
    return lax.dot_general(a, b, (((1,), (1,)), ((), ())), preferred_element_type=F32)


def _dot_tn(a, b):
    return _dot(a.astype(F32).T.astype(BF), b)


def _full(shape):
    return pl.BlockSpec(shape, lambda *_: (0,) * len(shape))


def _rows(ts, cols):
    return pl.BlockSpec((ts, cols), lambda i: (i, 0))


def _vmem():
    return pl.BlockSpec(memory_space=pltpu.VMEM)


def _any():
    return pl.BlockSpec(memory_space=pl.ANY)


def _rms(v):
    return lax.rsqrt(jnp.mean(v * v, axis=-1, keepdims=True) + EPS)


def _rms_bwd(dn, n, r):
    return r * (dn - n * jnp.mean(dn * n, axis=-1, keepdims=True))


def _colsum(v):
    return jnp.sum(v, axis=0, keepdims=True)


def _swap_halves(v):
    lane = lax.broadcasted_iota(jnp.int32, v.shape, 1)
    return jnp.where(lane < HALF, pltpu.roll(v, 128 - HALF, 1), pltpu.roll(v, HALF, 1))


def _window_lane_width():
    lane = lax.broadcasted_iota(jnp.int32, (1, PW), 1)
    return jnp.where(lane < 128, 2.0, jnp.where(lane < 256, 4.0, jnp.where(lane < 384, 8.0, 16.0))).astype(F32)


def _window_sums(ext, back):
    n = ext.shape[0]

    def sh(v, k):
        return pltpu.roll(v, k if back else n - k, 0)

    s2 = ext + sh(ext, 1)
    e4 = s2[:, 128:]
    s4 = e4 + sh(e4, 2)
    e8 = s4[:, 128:]
    s8 = e8 + sh(e8, 4)
    e16 = s8[:, 128:]
    s16 = e16 + sh(e16, 8)
    return jnp.concatenate([s2[:, :128], s4[:, :128], s8[:, :128], s16], axis=1)


def _row_counts(first_row, ts):
    t1 = (first_row + lax.broadcasted_iota(jnp.int32, (ts, 1), 0) + 1).astype(F32)
    return jnp.minimum(t1, _window_lane_width())


def _fwd_in(x, mod, g_mix, w_in, g_q, g_kv, w_uq, wuk_dc, perm, cos4, sin4, csk, snk, w_pool, pool_scale):
    S = x.shape[0]
    ts = 512
    nsub = ts // TQ

    def body(x_ref, mod_ref, gmix_ref, win_ref, gq_ref, gkv_ref, wuq_ref, wuk_ref, perm_ref, cos_ref, sin_ref,
             csk_ref, snk_ref, wpool_ref, pscale_ref,
             h1_ref, raw_ref, qn_ref, qs_ref, kv_ref, pooled_ref, ypre_ref, ypool_ref, carry_ref):
        i = pl.program_id(0)

        @pl.when(i == 0)
        def _():
            carry_ref[...] = jnp.zeros_like(carry_ref)

        xv = x_ref[...]
        sh1 = mod_ref[0:1, 0:D]
        sc1 = mod_ref[0:1, D:2 * D]
        h = (xv * _rms(xv)) * gmix_ref[...] * (1.0 + sc1) + sh1
        hb = h.astype(BF)
        h1_ref[...] = hb
        proj = _dot(hb, win_ref[...])
        cq_raw = proj[:, 0:QL]
        ckv_raw = proj[:, QL:QL + KVL]
        kr = proj[:, 384:512]
        u = proj[:, 512:1024]
        raw_ref[...] = proj[:, 0:384]

        c_q = (cq_raw * _rms(cq_raw)) * gq_ref[...]
        c_kv = (ckv_raw * _rms(ckv_raw)) * gkv_ref[...]
        q = _dot(c_q.astype(BF), wuq_ref[...])
        qn = q[:, 0:HEADS * NOPE].astype(BF)
        qn_ref[...] = qn
        x1 = q[:, 512:640]
        x2 = q[:, 640:768]
        cosv = cos_ref[...]
        sinv = sin_ref[...]
        roped = jnp.concatenate([x1 * cosv - x2 * sinv, x1 * sinv + x2 * cosv], axis=1).astype(BF)
        for hd in range(HEADS):
            q_lat = _dot(qn[:, hd * NOPE:(hd + 1) * NOPE], wuk_ref[hd])
            q_rope = _dot(roped, perm_ref[hd])
            qh = jnp.concatenate([q_lat, q_rope], axis=1).astype(BF)
            for a in range(nsub):
                qs_ref[a, hd * TQ:(hd + 1) * TQ, :] = qh[a * TQ:(a + 1) * TQ, :]
        k_rope = kr * csk_ref[...] + _swap_halves(kr) * snk_ref[...]
        kv_ref[...] = jnp.concatenate([c_kv, k_rope], axis=1).astype(BF)

        ext = jnp.concatenate([carry_ref[...], u], axis=0)
        win = _window_sums(ext, True)[16:, :]
        pooled = (win / _row_counts(i * ts, ts) - u).astype(BF)
        pooled_ref[...] = pooled
        carry_ref[...] = u[ts - 16:ts, :]
        ypre = jnp.concatenate(
            [_dot(pooled[:, g * GD:(g + 1) * GD], wpool_ref[g]) for g in range(GROUPS)], axis=1)
        ypre_ref[...] = ypre
        ypool_ref[...] = (ypre * pscale_ref[...]).astype(BF)

    out_shape = (
        jax.ShapeDtypeStruct((S, D), BF),
        jax.ShapeDtypeStruct((S, 384), F32),
        jax.ShapeDtypeStruct((S, HEADS * NOPE), BF),
        jax.ShapeDtypeStruct((S // TQ, HEADS * TQ, QW), BF),
        jax.ShapeDtypeStruct((S, QW), BF),
        jax.ShapeDtypeStruct((S, PW), BF),
        jax.ShapeDtypeStruct((S, PW), F32),
        jax.ShapeDtypeStruct((S, PW), BF),
    )
    in_specs = [
        _rows(ts, D), _full(mod.shape), _full((1, D)), _full(w_in.shape), _full((1, QL)), _full((1, KVL)),
        _full(w_uq.shape), _full(wuk_dc.shape), _full(perm.shape), _rows(ts, 128), _rows(ts, 128), _rows(ts, 128),
        _rows(ts, 128), _full(w_pool.shape), _full((1, PW)),
    ]
    out_specs = (
        _rows(ts, D), _rows(ts, 384), _rows(ts, HEADS * NOPE),
        pl.BlockSpec((nsub, HEADS * TQ, QW), lambda i: (i, 0, 0)),
        _rows(ts, QW), _rows(ts, PW), _rows(ts, PW), _rows(ts, PW),
    )
    return pl.pallas_call(
        body, name="fwd_in", out_shape=out_shape, grid=(S // ts,), in_specs=in_specs, out_specs=out_specs,
        scratch_shapes=[pltpu.VMEM((16, PW), F32)], compiler_params=_params(("arbitrary",)),
    )(x, mod, g_mix, w_in, g_q, g_kv, w_uq, wuk_dc, perm, cos4, sin4, csk, snk, w_pool, pool_scale)


def _diag_mask(shape, q_axis):
    qi = (lax.broadcasted_iota(jnp.int32, shape, q_axis) & (TQ - 1)) >> 6
    ki = lax.broadcasted_iota(jnp.int32, shape, 1 - q_axis) >> 6
    return ki <= qi


def _attn_fwd(qs, kv, wuv_cv):
    nq = qs.shape[0]
    S = kv.shape[0]
    M = HEADS * TQ

    def body(qs_ref, kv_ref, wuv_ref, olat_ref, ymla_ref, lse_ref):
        i = pl.program_id(0)
        q = qs_ref[0]

        def step(kt, carry, masked):
            m, l, acc = carry
            k = kv_ref[pl.ds(pl.multiple_of(kt * TK, TK), TK), :]
            s = _dot_nt(q, k) * SM_SCALE
            if masked:
                s = jnp.where(_diag_mask((M, TK), 0), s, -jnp.inf)
            m_new = jnp.maximum(m, jnp.max(s, axis=-1, keepdims=True))
            alpha = jnp.exp(m - m_new)
            p = jnp.exp(s - m_new)
            l = alpha * l + jnp.sum(p, axis=-1, keepdims=True)
            acc = alpha * acc + _dot(p.astype(BF), k[:, 0:KVL])
            return m_new, l, acc

        init = (jnp.full((M, 1), -jnp.inf, F32), jnp.zeros((M, 1), F32), jnp.zeros((M, KVL), F32))
        carry = lax.fori_loop(0, i, lambda kt, c: step(kt, c, False), init)
        m, l, acc = step(i, carry, True)
        o_lat = acc / l
        olat_ref[0] = o_lat
        lse = m + jnp.log(l)
        lse_ref[0] = jnp.broadcast_to(lse, (M, 128)).T[0:8, :]
        for hd in range(HEADS):
            o = _dot(o_lat[hd * TQ:(hd + 1) * TQ, :].astype(BF), wuv_ref[hd])
            ymla_ref[:, hd * 128:(hd + 1) * 128] = o.astype(BF)

    out_shape = (
        jax.ShapeDtypeStruct((nq, M, KVL), F32),
        jax.ShapeDtypeStruct((S, HEADS * 128), BF),
        jax.ShapeDtypeStruct((nq, 8, M), F32),
    )
    return pl.pallas_call(
        body, name="attn_fwd", out_shape=out_shape, grid=(nq,),
        in_specs=[pl.BlockSpec((1, M, QW), lambda i: (i, 0, 0)), _full(kv.shape), _full(wuv_cv.shape)],
        out_specs=(pl.BlockSpec((1, M, KVL), lambda i: (i, 0, 0)), _rows(TQ, HEADS * 128),
                   pl.BlockSpec((1, 8, M), lambda i: (i, 0, 0))),
        compiler_params=_params(("arbitrary",)),
    )(qs, kv, wuv_cv)


def _silu_parts(a):
    sg = jax.nn.sigmoid(a)
    return sg, a * sg


def _ffn_fwd(x, ymla, ypool, mod, w_o, g_ffn, wg_t, wu_t, wd, g_final, target):
    S = x.shape[0]
    ts = 512
    tf = 256
    nj = FF // tf

    def body(x_ref, ymla_ref, ypool_ref, mod_ref, wo_ref, gffn_ref, wg_ref, wu_ref, wd_ref, gfin_ref, t_ref,
             x2_ref, mix_ref, h2_ref, a_ref, b_ref, dx3_ref, dff_ref, loss_ref, dgfin_ref, dgt2_ref, acc_ref):
        i = pl.program_id(0)
        j = pl.program_id(1)

        @pl.when(jnp.logical_and(i == 0, j == 0))
        def _():
            loss_ref[...] = jnp.zeros_like(loss_ref)
            dgfin_ref[...] = jnp.zeros_like(dgfin_ref)
            dgt2_ref[...] = jnp.zeros_like(dgt2_ref)

        @pl.when(j == 0)
        def _():
            gt1 = mod_ref[0:1, 2 * D:3 * D]
            sh2 = mod_ref[0:1, 3 * D:4 * D]
            sc2 = mod_ref[0:1, 4 * D:5 * D]
            cat = jnp.concatenate([ymla_ref[...], ypool_ref[...]], axis=1)
            mix = _dot(cat, wo_ref[...])
            mix_ref[...] = mix
            x2 = x_ref[...] + gt1 * mix
            x2_ref[...] = x2
            h2 = (x2 * _rms(x2)) * gffn_ref[...] * (1.0 + sc2) + sh2
            h2_ref[...] = h2.astype(BF)
            acc_ref[...] = jnp.zeros_like(acc_ref)

        h2b = h2_ref[...]
        a = _dot_nt(h2b, wg_ref[...])
        b = _dot_nt(h2b, wu_ref[...])
        a_ref[...] = a.astype(BF)
        b_ref[...] = b.astype(BF)
        f = _silu_parts(a)[1] * b
        acc_ref[...] += _dot(f.astype(BF), wd_ref[...])

        @pl.when(j == nj - 1)
        def _():
            gt2 = mod_ref[0:1, 5 * D:6 * D]
            ff = acc_ref[...]
            x3 = x2_ref[...] + gt2 * ff
            r3 = _rms(x3)
            xn3 = x3 * r3
            gfin = gfin_ref[...]
            e = xn3 * gfin - t_ref[...]
            loss_ref[...] += 0.5 * jnp.sum(jnp.mean(e * e, axis=-1, keepdims=True))
            dy = e * (1.0 / D)
            dgfin_ref[...] += _colsum(dy * xn3)
            dx3 = _rms_bwd(dy * gfin, xn3, r3)
            dx3_ref[...] = dx3
            dgt2_ref[...] += _colsum(dx3 * ff)
            dff_ref[...] = (dx3 * gt2).astype(BF)

    row = lambda c: pl.BlockSpec((ts, c), lambda i, j: (i, 0))
    wblk = pl.BlockSpec((tf, D), lambda i, j: (j, 0))
    act = pl.BlockSpec((ts, tf), lambda i, j: (i, j))
    const = lambda shape: pl.BlockSpec(shape, lambda i, j: (0,) * len(shape))
    out_shape = (
        jax.ShapeDtypeStruct((S, D), F32),
        jax.ShapeDtypeStruct((S, D), F32),
        jax.ShapeDtypeStruct((S, D), BF),
        jax.ShapeDtypeStruct((S, FF), BF),
        jax.ShapeDtypeStruct((S, FF), BF),
        jax.ShapeDtypeStruct((S, D), F32),
        jax.ShapeDtypeStruct((S, D), BF),
        jax.ShapeDtypeStruct((8, 128), F32),
        jax.ShapeDtypeStruct((1, D), F32),
        jax.ShapeDtypeStruct((1, D), F32),
    )
    return pl.pallas_call(
        body, name="ffn_fwd", out_shape=out_shape, grid=(S // ts, nj),
        in_specs=[row(D), row(PW), row(PW), const(mod.shape), const(w_o.shape), const((1, D)), wblk, wblk, wblk,
                  const((1, D)), row(D)],
        out_specs=(row(D), row(D), row(D), act, act, row(D), row(D), const((8, 128)), const((1, D)), const((1, D))),
        scratch_shapes=[pltpu.VMEM((ts, D), F32)],
        compiler_params=_params(("arbitrary", "arbitrary")),
    )(x, ymla, ypool, mod, w_o, g_ffn, wg_t, wu_t, wd, g_final, target)


def _ffn_bwd(dff, h2, a, b, wg_t, wu_t, wd):
    S = dff.shape[0]
    ts = 1024
    tf = 256
    ni = S // ts
    nj = FF // tf

    def body(dff_ref, h2_ref, a_ref, b_ref, wg_ref, wu_ref, wd_ref,
             dwg_ref, dwu_ref, dwd_ref, dh2_ref, gacc, uacc, dacc, dh2acc):
        j = pl.program_id(0)
        i = pl.program_id(1)
        dffb = dff_ref[...]
        h2b = h2_ref[...]
        av = a_ref[...].astype(F32)
        bv = b_ref[...].astype(F32)
        df = _dot_nt(dffb, wd_ref[...])
        sg, sa = _silu_parts(av)
        f = sa * bv
        db = df * sa
        da = df * bv * (sg * (1.0 + av * (1.0 - sg)))
        dab = da.astype(BF)
        dbb = db.astype(BF)

        @pl.when(i == 0)
        def _():
            gacc[...] = jnp.zeros_like(gacc)
            uacc[...] = jnp.zeros_like(uacc)
            dacc[...] = jnp.zeros_like(dacc)

        gacc[...] += _dot_tn(da, h2b)
        uacc[...] += _dot_tn(db, h2b)
        dacc[...] += _dot_tn(f, dffb)
        contrib = _dot(dab, wg_ref[...]) + _dot(dbb, wu_ref[...])
        rows = pl.ds(pl.multiple_of(i * ts, ts), ts)

        @pl.when(j == 0)
        def _():
            dh2acc[rows, :] = contrib

        @pl.when(j > 0)
        def _():
            dh2acc[rows, :] += contrib

        @pl.when(i == ni - 1)
        def _():
            dwg_ref[...] = gacc[...].astype(BF)
            dwu_ref[...] = uacc[...].astype(BF)
            dwd_ref[...] = dacc[...].astype(BF)

        @pl.when(j == nj - 1)
        def _():
            dh2_ref[...] = dh2acc[rows, :]

    row = lambda c: pl.BlockSpec((ts, c), lambda j, i: (i, 0))
    act = pl.BlockSpec((ts, tf), lambda j, i: (i, j))
    wblk = pl.BlockSpec((tf, D), lambda j, i: (j, 0))
    out_shape = (
        jax.ShapeDtypeStruct((FF, D), BF), jax.ShapeDtypeStruct((FF, D), BF), jax.ShapeDtypeStruct((FF, D), BF),
        jax.ShapeDtypeStruct((S, D), F32),
    )
    return pl.pallas_call(
        body, name="ffn_bwd", out_shape=out_shape, grid=(nj, ni),
        in_specs=[row(D), row(D), act, act, wblk, wblk, wblk],
        out_specs=(wblk, wblk, wblk, pl.BlockSpec((ts, D), lambda j, i: (jnp.where(j == nj - 1, i, 0), 0))),
        scratch_shapes=[pltpu.VMEM((tf, D), F32), pltpu.VMEM((tf, D), F32), pltpu.VMEM((tf, D), F32),
                        pltpu.VMEM((S, D), F32)],
        compiler_params=_params(("arbitrary", "arbitrary")),
    )(dff, h2, a, b, wg_t, wu_t, wd)


def _mix_bwd(dh2, dx3, x2, mix, mod, g_ffn, ymla, ypool, w_o, ypre, pooled, pool_scale, wpool_dc, olat, wuv_vc):
    S = dh2.shape[0]
    ts = 512
    n = S // ts
    nsub = ts // TQ
    M = HEADS * TQ

    def body(dh2_ref, dx3_ref, x2_ref, mix_ref, mod_ref, gffn_ref, ymla_ref, ypool_ref, wo_ref, ypre_ref, pooled_ref,
             pscale_ref, wpool_ref, olat_ref, wuv_ref,
             dx2_ref, du_ref, dolat_ref, delta_ref, dwo_ref, dwuv_ref, dwpool_ref, dpscale_ref, dgt1_ref, dsc2_ref,
             dsh2_ref, dgffn_ref, carry_ref, dwo_acc):
        i = pl.program_id(0)

        @pl.when(i == 0)
        def _():
            carry_ref[...] = jnp.zeros_like(carry_ref)
            dwo_acc[...] = jnp.zeros_like(dwo_acc)
            for r in (dwuv_ref, dwpool_ref, dpscale_ref, dgt1_ref, dsc2_ref, dsh2_ref, dgffn_ref):
                r[...] = jnp.zeros_like(r)

        gt1 = mod_ref[0:1, 2 * D:3 * D]
        sc2 = mod_ref[0:1, 4 * D:5 * D]
        gffn = gffn_ref[...]
        dh2 = dh2_ref[...]
        x2 = x2_ref[...]
        r2 = _rms(x2)
        xn2 = x2 * r2
        dsc2_ref[...] += _colsum(dh2 * (xn2 * gffn))
        dsh2_ref[...] += _colsum(dh2)
        dgffn_ref[...] += _colsum(dh2 * (1.0 + sc2) * xn2)
        dx2 = dx3_ref[...] + _rms_bwd(dh2 * gffn * (1.0 + sc2), xn2, r2)
        dx2_ref[...] = dx2
        dgt1_ref[...] += _colsum(dx2 * mix_ref[...])
        dmix = (dx2 * gt1).astype(BF)
        cat = jnp.concatenate([ymla_ref[...], ypool_ref[...]], axis=1)
        dwo_acc[...] += _dot_tn(cat, dmix)
        dcat = _dot_nt(dmix, wo_ref[...])
        dymla = dcat[:, 0:512]
        dypool = dcat[:, 512:1024]

        dpscale_ref[...] += _colsum(dypool * ypre_ref[...])
        dypre = (dypool * pscale_ref[...]).astype(BF)
        pooled = pooled_ref[...]
        dpooled = []
        for g in range(GROUPS):
            sl = slice(g * GD, (g + 1) * GD)
            dwpool_ref[g] += _dot_tn(pooled[:, sl], dypre[:, sl])
            dpooled.append(_dot(dypre[:, sl], wpool_ref[g]))
        dpooled = jnp.concatenate(dpooled, axis=1)
        tile = n - 1 - i
        e = dpooled / _row_counts(tile * ts, ts)
        ext = jnp.concatenate([e, carry_ref[...]], axis=0)
        du_ref[...] = _window_sums(ext, False)[0:ts, :] - dpooled
        carry_ref[...] = e[0:16, :]

        for hd in range(HEADS):
            do = dymla[:, hd * 128:(hd + 1) * 128]
            dob = do.astype(BF)
            dol = _dot(dob, wuv_ref[hd])
            for a in range(nsub):
                ol = olat_ref[a, hd * TQ:(hd + 1) * TQ, :]
                dl = dol[a * TQ:(a + 1) * TQ, :]
                dolat_ref[a, hd * TQ:(hd + 1) * TQ, :] = dl.astype(BF)
                dwuv_ref[hd] += _dot_tn(ol, dob[a * TQ:(a + 1) * TQ, :])
                delta = jnp.sum(dl * ol, axis=-1, keepdims=True)
                delta_ref[a, :, hd * TQ:(hd + 1) * TQ] = jnp.broadcast_to(delta, (TQ, 128)).T[0:8, :]

        @pl.when(i == n - 1)
        def _():
            dwo_ref[...] = dwo_acc[...].astype(BF)

    rev = lambda c: pl.BlockSpec((ts, c), lambda i: (n - 1 - i, 0))
    rev3 = lambda r, c: pl.BlockSpec((nsub, r, c), lambda i: (n - 1 - i, 0, 0))
    out_shape = (
        jax.ShapeDtypeStruct((S, D), F32),
        jax.ShapeDtypeStruct((S, PW), F32),
        jax.ShapeDtypeStruct((S // TQ, M, KVL), BF),
        jax.ShapeDtypeStruct((S // TQ, 8, M), F32),
        jax.ShapeDtypeStruct((D, D), BF),
        jax.ShapeDtypeStruct((HEADS, KVL, 128), F32),
        jax.ShapeDtypeStruct((GROUPS, GD, GD), F32),
        jax.ShapeDtypeStruct((1, PW), F32),
        jax.ShapeDtypeStruct((1, D), F32), jax.ShapeDtypeStruct((1, D), F32), jax.ShapeDtypeStruct((1, D), F32),
        jax.ShapeDtypeStruct((1, D), F32),
    )
    in_specs = [rev(D), rev(D), rev(D), rev(D), _full(mod.shape), _full((1, D)), rev(PW), rev(PW), _full(w_o.shape),
                rev(PW), rev(PW), _full((1, PW)), _full(wpool_dc.shape), rev3(M, KVL), _full(wuv_vc.shape)]
    out_specs = (rev(D), rev(PW), rev3(M, KVL), rev3(8, M), _full((D, D)), _full((HEADS, KVL, 128)),
                 _full((GROUPS, GD, GD)), _full((1, PW)), _full((1, D)), _full((1, D)), _full((1, D)), _full((1, D)))
    return pl.pallas_call(
        body, name="mix_bwd", out_shape=out_shape, grid=(n,), in_specs=in_specs, out_specs=out_specs,
        scratch_shapes=[pltpu.VMEM((16, PW), F32), pltpu.VMEM((D, D), F32)],
        compiler_params=_params(("arbitrary",)),
    )(dh2, dx3, x2, mix, mod, g_ffn, ymla, ypool, w_o, ypre, pooled, pool_scale, wpool_dc, olat, wuv_vc)


def _attn_bwd(qs, kv, dolat, lse, delta):
    nq = qs.shape[0]
    S = kv.shape[0]
    M = HEADS * TQ
    nk = S // TK

    def body(qs_ref, kv_ref, do_ref, lse_ref, delta_ref, dkv_ref, dqt_ref):
        kt = pl.program_id(0)
        k = kv_ref[...]
        v = k[:, 0:KVL]
        k_t = k.astype(F32).T.astype(BF)

        @pl.when(kt == 0)
        def _():
            dqt_ref[...] = jnp.zeros_like(dqt_ref)

        def step(qi, carry, masked):
            dk, dv = carry
            q = qs_ref[qi]
            do = do_ref[qi]
            s = _dot_nt(k, q) * SM_SCALE
            p = jnp.exp(s - lse_ref[qi, 0:1, :])
            if masked:
                p = jnp.where(_diag_mask((TK, M), 1), p, 0.0)
            dp = _dot_nt(v, do)
            ds = (p * (dp - delta_ref[qi, 0:1, :]) * SM_SCALE).astype(BF)
            dv = dv + _dot(p.astype(BF), do)
            dk = dk + _dot(ds, q)
            dqt_ref[qi] += _dot(k_t, ds)
            return dk, dv

        carry = step(kt, (jnp.zeros((TK, QW), F32), jnp.zeros((TK, KVL), F32)), True)
        dk, dv = lax.fori_loop(kt + 1, nq, lambda qi, c: step(qi, c, False), carry)
        dkv_ref[...] = dk + jnp.concatenate([dv, jnp.zeros((TK, QW - KVL), F32)], axis=1)

    out_shape = (jax.ShapeDtypeStruct((S, QW), F32), jax.ShapeDtypeStruct((nq, QW, M), F32))
    return pl.pallas_call(
        body, name="attn_bwd", out_shape=out_shape, grid=(nk,),
        in_specs=[_vmem(), _rows(TK, QW), _vmem(), _vmem(), _vmem()],
        out_specs=(_rows(TK, QW), _vmem()),
        compiler_params=_params(("arbitrary",)),
    )(qs, kv, dolat, lse, delta)


def _in_bwd(dqt, dkv, du, raw, qn, h1, x, dx2, mod, g_mix, w_in, g_q, g_kv, w_uq, wuk_cd, perm_t, cos4, sin4, csk,
            snk):
    S = x.shape[0]
    ts = 512
    n = S // ts
    nsub = ts // TQ
    M = HEADS * TQ

    def body(dqt_ref, dkv_ref, du_ref, raw_ref, qn_ref, h1_ref, x_ref, dx2_ref, mod_ref, gmix_ref, win_ref, gq_ref,
             gkv_ref, wuq_ref, wuk_ref, permt_ref, cos_ref, sin_ref, csk_ref, snk_ref,
             dx_ref, dwin_ref, dwuq_ref, dwuk_ref, dgq_ref, dgkv_ref, dsc1_ref, dsh1_ref, dgmix_ref, dwin_acc,
             dwuq_acc):
        i = pl.program_id(0)

        @pl.when(i == 0)
        def _():
            dwin_acc[...] = jnp.zeros_like(dwin_acc)
            dwuq_acc[...] = jnp.zeros_like(dwuq_acc)
            for r in (dwuk_ref, dgq_ref, dgkv_ref, dsc1_ref, dsh1_ref, dgmix_ref):
                r[...] = jnp.zeros_like(r)

        dq_blocks = [dqt_ref[a].T for a in range(nsub)]
        qn = qn_ref[...]
        dq_parts = []
        drope = jnp.zeros((ts, 2 * 128), F32)
        for hd in range(HEADS):
            dqh = jnp.concatenate([blk[hd * TQ:(hd + 1) * TQ, :] for blk in dq_blocks], axis=0)
            dq_lat = dqh[:, 0:KVL].astype(BF)
            dq_parts.append(_dot(dq_lat, wuk_ref[hd]))
            dwuk_ref[hd] += _dot_tn(dq_lat, qn[:, hd * NOPE:(hd + 1) * NOPE])
            drope = drope + _dot(dqh[:, KVL:QW].astype(BF), permt_ref[hd])
        do1 = drope[:, 0:128]
        do2 = drope[:, 128:256]
        cosv = cos_ref[...]
        sinv = sin_ref[...]
        dq_parts.append(do1 * cosv + do2 * sinv)
        dq_parts.append(do2 * cosv - do1 * sinv)
        dq = jnp.concatenate(dq_parts, axis=1).astype(BF)

        cq_raw = raw_ref[:, 0:QL]
        ckv_raw = raw_ref[:, QL:QL + KVL]
        rq = _rms(cq_raw)
        nq_ = cq_raw * rq
        gq = gq_ref[...]
        dwuq_acc[...] += _dot_tn((nq_ * gq).astype(BF), dq)
        dc_q = _dot_nt(dq, wuq_ref[...])
        dgq_ref[...] += _colsum(dc_q * nq_)
        dcq_raw = _rms_bwd(dc_q * gq, nq_, rq)

        dkv = dkv_ref[...]
        rk = _rms(ckv_raw)
        nk_ = ckv_raw * rk
        dc_kv = dkv[:, 0:KVL]
        dgkv_ref[...] += _colsum(dc_kv * nk_)
        dckv_raw = _rms_bwd(dc_kv * gkv_ref[...], nk_, rk)
        dkr_roped = dkv[:, KVL:QW]
        dkr = dkr_roped * csk_ref[...] - _swap_halves(dkr_roped) * snk_ref[...]

        dproj = jnp.concatenate([dcq_raw, dckv_raw, dkr, du_ref[...]], axis=1).astype(BF)
        dwin_acc[...] += _dot_tn(h1_ref[...], dproj)
        dh1 = _dot_nt(dproj, win_ref[...])

        sc1 = mod_ref[0:1, D:2 * D]
        gmix = gmix_ref[...]
        xv = x_ref[...]
        r1 = _rms(xv)
        xn1 = xv * r1
        dsc1_ref[...] += _colsum(dh1 * (xn1 * gmix))
        dsh1_ref[...] += _colsum(dh1)
        dgmix_ref[...] += _colsum(dh1 * (1.0 + sc1) * xn1)
        dx_ref[...] = dx2_ref[...] + _rms_bwd(dh1 * gmix * (1.0 + sc1), xn1, r1)

        @pl.when(i == n - 1)
        def _():
            dwin_ref[...] = dwin_acc[...].astype(BF)
            dwuq_ref[...] = dwuq_acc[...].astype(BF)

    out_shape = (
        jax.ShapeDtypeStruct((S, D), F32),
        jax.ShapeDtypeStruct((D, D), BF),
        jax.ShapeDtypeStruct((QL, 768), BF),
        jax.ShapeDtypeStruct((HEADS, KVL, NOPE), F32),
        jax.ShapeDtypeStruct((1, QL), F32), jax.ShapeDtypeStruct((1, KVL), F32),
        jax.ShapeDtypeStruct((1, D), F32), jax.ShapeDtypeStruct((1, D), F32), jax.ShapeDtypeStruct((1, D), F32),
    )
    in_specs = [pl.BlockSpec((nsub, QW, M), lambda i: (i, 0, 0)), _rows(ts, QW), _rows(ts, PW), _rows(ts, 384),
                _rows(ts, HEADS * NOPE), _rows(ts, D), _rows(ts, D), _rows(ts, D), _full(mod.shape), _full((1, D)),
                _full(w_in.shape), _full((1, QL)), _full((1, KVL)), _full(w_uq.shape), _full(wuk_cd.shape),
                _full(perm_t.shape), _rows(ts, 128), _rows(ts, 128), _rows(ts, 128), _rows(ts, 128)]
    out_specs = (_rows(ts, D), _full((D, D)), _full((QL, 768)), _full((HEADS, KVL, NOPE)), _full((1, QL)),
                 _full((1, KVL)), _full((1, D)), _full((1, D)), _full((1, D)))
    return pl.pallas_call(
        body, name="in_bwd", out_shape=out_shape, grid=(n,), in_specs=in_specs, out_specs=out_specs,
        scratch_shapes=[pltpu.VMEM((D, D), F32), pltpu.VMEM((QL, 768), F32)],
        compiler_params=_params(("arbitrary",)),
    )(dqt, dkv, du, raw, qn, h1, x, dx2, mod, g_mix, w_in, g_q, g_kv, w_uq, wuk_cd, perm_t, cos4, sin4, csk, snk)


def _rope_perm():
    p = np.zeros((HEADS, 2 * 128, 128), np.float32)
    for hd in range(HEADS):
        for t in range(HALF):
            p[hd, hd * HALF + t, t] = 1.0
            p[hd, 128 + hd * HALF + t, HALF + t] = 1.0
    return p


def _rope_tables(positions):
    freqs = jnp.power(ROPE_THETA, -jnp.arange(HALF, dtype=F32) / HALF)
    ang = positions.astype(F32)[:, None] * freqs
    cos = jnp.cos(ang)
    sin = jnp.sin(ang)
    zero = jnp.zeros_like(cos)
    cos4 = jnp.tile(cos, (1, HEADS))
    sin4 = jnp.tile(sin, (1, HEADS))
    csk = jnp.concatenate([cos, cos, zero, zero], axis=1)
    snk = jnp.concatenate([-sin, sin, zero, zero], axis=1)
    return cos4, sin4, csk, snk


def _local_step(x, positions, target, mod, g_mix, w_in_p, g_q, g_kv, w_uq_p, w_uk, w_uv, w_pool, pool_scale, w_o,
                g_ffn, wg_t, wu_t, wd, g_final):
    perm = jnp.asarray(_rope_perm(), BF)
    perm_t = jnp.asarray(_rope_perm().transpose(0, 2, 1), BF)
    cos4, sin4, csk, snk = _rope_tables(positions)
    wuk_dc = w_uk.transpose(1, 2, 0).astype(BF)
    wuk_cd = w_uk.transpose(1, 0, 2).astype(BF)
    wuv_cv = w_uv.transpose(1, 0, 2).astype(BF)
    wuv_vc = w_uv.transpose(1, 2, 0).astype(BF)
    wpool = w_pool.astype(BF)
    wpool_dc = w_pool.transpose(0, 2, 1).astype(BF)

    h1, raw, qn, qs, kv, pooled, ypre, ypool = _fwd_in(
        x, mod, g_mix, w_in_p, g_q, g_kv, w_uq_p, wuk_dc, perm, cos4, sin4, csk, snk, wpool, pool_scale)
    olat, ymla, lse = _attn_fwd(qs, kv, wuv_cv)
    x2, mix, h2, a, b, dx3, dff, loss, dgfin, dgt2 = _ffn_fwd(
        x, ymla, ypool, mod, w_o, g_ffn, wg_t, wu_t, wd, g_final, target)
    dwg_t, dwu_t, dwd, dh2 = _ffn_bwd(dff, h2, a, b, wg_t, wu_t, wd)
    (dx2, du, dolat, delta, dwo, dwuv, dwpool, dpscale, dgt1, dsc2, dsh2, dgffn) = _mix_bwd(
        dh2, dx3, x2, mix, mod, g_ffn, ymla, ypool, w_o, ypre, pooled, pool_scale, wpool_dc, olat, wuv_vc)
    dkv, dqt = _attn_bwd(qs, kv, dolat, lse, delta)
    dx, dwin, dwuq, dwuk, dgq, dgkv, dsc1, dsh1, dgmix = _in_bwd(
        dqt, dkv, du, raw, qn, h1, x, dx2, mod, g_mix, w_in_p, g_q, g_kv, w_uq_p, wuk_cd, perm_t, cos4, sin4, csk,
        snk)
    dmod = jnp.concatenate([dsh1, dsc1, dgt1, dsh2, dsc2, dgt2], axis=1)
    sharded = dict(w_in=dwin, w_uq=dwuq, w_o=dwo, w_gate=dwg_t, w_up=dwu_t, w_down=dwd)
    replicated = dict(
        w_uk=dwuk.transpose(1, 0, 2), w_uv=dwuv.transpose(1, 0, 2), w_pool=dwpool, g_mix=dgmix, g_q=dgq, g_kv=dgkv,
        pool_scale=dpscale, g_ffn=dgffn, g_final=dgfin)
    return loss[0, 0], dx, dmod, sharded, replicated


def _my_pos():
    return lax.axis_index("x"), lax.axis_index("y"), lax.axis_index("c")


def _peer(pos, k):
    x, y, c = pos
    return (1 - x if k & 4 else x, 1 - y if k & 2 else y, 1 - c if k & 1 else c)


def _index(pos):
    x, y, c = pos
    return 4 * x + 2 * y + c


def _remote(src, dst, send_sem, recv_sem, to):
    return pltpu.make_async_remote_copy(src_ref=src, dst_ref=dst, send_sem=send_sem, recv_sem=recv_sem,
                                        device_id=to, device_id_type=MESH)


def _ada_mod(c, w_ada, b_ada):
    def body(c_ref, w_ref, b_ref, mod_ref, call_ref, cbuf, sbuf, rbuf, send1, recv1, send2, recv2):
        me = _my_pos()
        mi = _index(me)
        cv = c_ref[...]
        cbuf[...] = jnp.broadcast_to(cv * jax.nn.sigmoid(cv), (8, D))
        call_ref[mi] = cbuf[...]
        first = [_remote(cbuf, call_ref.at[mi], send1.at[k - 1], recv1.at[k - 1], _peer(me, k)) for k in range(1, NDEV)]
        for cp in first:
            cp.start()
        for k in range(1, NDEV):
            _remote(cbuf, call_ref.at[_index(_peer(me, k))], send1.at[k - 1], recv1.at[k - 1], _peer(me, k)).wait_recv()
        c_all = jnp.concatenate([call_ref[b][0:1, :] for b in range(NDEV)], axis=0)
        blocks = _dot(c_all.astype(BF), w_ref[...].astype(BF))
        for b in range(NDEV):
            sbuf[b] = jnp.broadcast_to(blocks[b:b + 1, :], (8, MODC))
        second = []
        for k in range(1, NDEV):
            to = _peer(me, k)
            second.append(_remote(sbuf.at[_index(to)], rbuf.at[mi], send2.at[k - 1], recv2.at[k - 1], to))
        for cp in second:
            cp.start()
        rbuf[mi] = sbuf[mi]
        for k in range(1, NDEV):
            to = _peer(me, k)
            _remote(sbuf.at[_index(to)], rbuf.at[_index(to)], send2.at[k - 1], recv2.at[k - 1], to).wait_recv()
        for j in range(NDEV):
            mod_ref[:, j * MODC:(j + 1) * MODC] = rbuf[j] + b_ref[:, j * MODC:(j + 1) * MODC]
        for cp in first + second:
            cp.wait_send()

    return pl.pallas_call(
        body, name="ada_mod",
        out_shape=(jax.ShapeDtypeStruct((8, N_MOD * D), F32), jax.ShapeDtypeStruct((NDEV, 8, D), F32)),
        in_specs=[_vmem(), _vmem(), _vmem()], out_specs=(_vmem(), _vmem()),
        scratch_shapes=[pltpu.VMEM((8, D), F32), pltpu.VMEM((NDEV, 8, MODC), F32), pltpu.VMEM((NDEV, 8, MODC), F32),
                        pltpu.SemaphoreType.DMA((NDEV - 1,)), pltpu.SemaphoreType.DMA((NDEV - 1,)),
                        pltpu.SemaphoreType.DMA((NDEV - 1,)), pltpu.SemaphoreType.DMA((NDEV - 1,))],
        compiler_params=_params(),
    )(c, w_ada, b_ada)


def _all_gather(shards):
    n = len(shards)

    def body(*refs):
        ins, outs = refs[:n], refs[n:2 * n]
        send, recv, local = refs[2 * n:]
        me = _my_pos()
        mi = _index(me)
        own = [pltpu.make_async_copy(ins[a], outs[a].at[mi], local.at[a]) for a in range(n)]
        for cp in own:
            cp.start()
        sent = []
        for a in range(n):
            for k in range(1, NDEV):
                sent.append(_remote(ins[a], outs[a].at[mi], send.at[a, k - 1], recv.at[a, k - 1], _peer(me, k)))
        for cp in sent:
            cp.start()
        for a in range(n):
            for k in range(1, NDEV):
                to = _peer(me, k)
                _remote(ins[a], outs[a].at[_index(to)], send.at[a, k - 1], recv.at[a, k - 1], to).wait_recv()
        for cp in sent:
            cp.wait_send()
        for cp in own:
            cp.wait()

    return pl.pallas_call(
        body, name="gather_weights",
        out_shape=tuple(jax.ShapeDtypeStruct((NDEV,) + s.shape, s.dtype) for s in shards),
        in_specs=[_any()] * n, out_specs=tuple([_any()] * n),
        scratch_shapes=[pltpu.SemaphoreType.DMA((n, NDEV - 1)), pltpu.SemaphoreType.DMA((n, NDEV - 1)),
                        pltpu.SemaphoreType.DMA((n,))],
        compiler_params=_params(),
    )(*shards)


def _scatter_partials(grads):
    n = len(grads)

    def body(*refs):
        ins, outs = refs[:n], refs[n:2 * n]
        send, recv, local = refs[2 * n:]
        me = _my_pos()
        mi = _index(me)

        def rows_of(a, dev_index):
            r = ins[a].shape[0] // NDEV
            return ins[a].at[pl.ds(pl.multiple_of(dev_index * r, 16), r), :]

        own = [pltpu.make_async_copy(rows_of(a, mi), outs[a].at[mi], local.at[a]) for a in range(n)]
        for cp in own:
            cp.start()
        sent = []
        for a in range(n):
            for k in range(1, NDEV):
                to = _peer(me, k)
                sent.append(_remote(rows_of(a, _index(to)), outs[a].at[mi], send.at[a, k - 1], recv.at[a, k - 1], to))
        for cp in sent:
            cp.start()
        for a in range(n):
            for k in range(1, NDEV):
                to = _peer(me, k)
                _remote(rows_of(a, mi), outs[a].at[_index(to)], send.at[a, k - 1], recv.at[a, k - 1], to).wait_recv()
        for cp in sent:
            cp.wait_send()
        for cp in own:
            cp.wait()

    return pl.pallas_call(
        body, name="scatter_grads",
        out_shape=tuple(jax.ShapeDtypeStruct((NDEV, g.shape[0] // NDEV, g.shape[1]), g.dtype) for g in grads),
        in_specs=[_any()] * n, out_specs=tuple([_any()] * n),
        scratch_shapes=[pltpu.SemaphoreType.DMA((n, NDEV - 1)), pltpu.SemaphoreType.DMA((n, NDEV - 1)),
                        pltpu.SemaphoreType.DMA((n,))],
        compiler_params=_params(),
    )(*grads)


def _sum_partials(parts):
    n = len(parts)

    def body(*refs):
        for a in range(n):
            acc = refs[a][0].astype(F32)
            for p in range(1, NDEV):
                acc = acc + refs[a][p].astype(F32)
            refs[n + a][...] = acc

    return pl.pallas_call(
        body, name="sum_partials",
        out_shape=tuple(jax.ShapeDtypeStruct(p.shape[1:], F32) for p in parts),
        in_specs=[_vmem()] * n, out_specs=tuple([_vmem()] * n), compiler_params=_params(),
    )(*parts)


def _small_all_reduce(buf):
    def body(buf_ref, got_ref, red_ref, mine, send1, recv1, send2, recv2):
        me = _my_pos()
        mi = _index(me)
        first = []
        for k in range(1, NDEV):
            to = _peer(me, k)
            first.append(_remote(buf_ref.at[_index(to)], got_ref.at[mi], send1.at[k - 1], recv1.at[k - 1], to))
        for cp in first:
            cp.start()
        got_ref[mi] = buf_ref[mi]
        for k in range(1, NDEV):
            to = _peer(me, k)
            _remote(buf_ref.at[mi], got_ref.at[_index(to)], send1.at[k - 1], recv1.at[k - 1], to).wait_recv()
        acc = got_ref[0]
        for p in range(1, NDEV):
            acc = acc + got_ref[p]
        mine[...] = acc
        second = [_remote(mine, red_ref.at[mi], send2.at[k - 1], recv2.at[k - 1], _peer(me, k)) for k in range(1, NDEV)]
        for cp in second:
            cp.start()
        red_ref[mi] = acc
        for k in range(1, NDEV):
            to = _peer(me, k)
            _remote(mine, red_ref.at[_index(to)], send2.at[k - 1], recv2.at[k - 1], to).wait_recv()
        for cp in first + second:
            cp.wait_send()

    return pl.pallas_call(
        body, name="small_all_reduce",
        out_shape=(jax.ShapeDtypeStruct(buf.shape, F32), jax.ShapeDtypeStruct(buf.shape, F32)),
        in_specs=[_vmem()], out_specs=(_vmem(), _vmem()),
        scratch_shapes=[pltpu.VMEM(buf.shape[1:], F32),
                        pltpu.SemaphoreType.DMA((NDEV - 1,)), pltpu.SemaphoreType.DMA((NDEV - 1,)),
                        pltpu.SemaphoreType.DMA((NDEV - 1,)), pltpu.SemaphoreType.DMA((NDEV - 1,))],
        compiler_params=_params(),
    )(buf)


def _adamw_math(w, g, m, v):
    m = ADAM_B1 * m + (1.0 - ADAM_B1) * g
    v = ADAM_B2 * v + (1.0 - ADAM_B2) * jnp.square(g)
    m_hat = m / (1.0 - ADAM_B1 ** ADAM_STEP)
    v_hat = v / (1.0 - ADAM_B2 ** ADAM_STEP)
    delta = -ADAM_LR * (m_hat / (jnp.sqrt(v_hat) + ADAM_EPS) + ADAM_WD * w)
    return delta, m, v


def _adamw_group(name, ws, gs, ms, vs):
    n = len(ws)

    def body(*refs):
        for a in range(n):
            w, g, m, v = (refs[q * n + a][...] for q in range(4))
            delta, m2, v2 = _adamw_math(w, g, m, v)
            refs[4 * n + a][...] = delta
            refs[5 * n + a][...] = m2
            refs[6 * n + a][...] = v2

    shapes = tuple(jax.ShapeDtypeStruct(w.shape, F32) for w in ws)
    outs = pl.pallas_call(
        body, name=name, out_shape=shapes * 3, in_specs=[_vmem()] * (4 * n), out_specs=tuple([_vmem()] * (3 * n)),
        compiler_params=_params(),
    )(*ws, *gs, *ms, *vs)
    return outs[:n], outs[n:2 * n], outs[2 * n:]


def _adamw_ada(w, m, v, c_all, dmod_rows):
    def body(w_ref, m_ref, v_ref, c_ref, dm_ref, g_ref, d_ref, m2_ref, v2_ref):
        g = _dot_tn(c_ref[...], dm_ref[...].astype(BF))
        g_ref[...] = g
        delta, m2, v2 = _adamw_math(w_ref[...], g, m_ref[...], v_ref[...])
        d_ref[...] = delta
        m2_ref[...] = m2
        v2_ref[...] = v2

    shp = jax.ShapeDtypeStruct(w.shape, F32)
    return pl.pallas_call(
        body, name="adamw_ada", out_shape=(shp, shp, shp, shp), in_specs=[_vmem()] * 5,
        out_specs=tuple([_vmem()] * 4), compiler_params=_params(),
    )(w, m, v, c_all, dmod_rows)


def _w_in_to_kernel(w):
    return jnp.concatenate([w[:, 0:448], jnp.zeros((w.shape[0], 64), w.dtype), w[:, 448:960]], axis=1)


def _w_in_from_kernel(w):
    return jnp.concatenate([w[:, 0:448], w[:, 512:1024]], axis=1)


def _w_uq_to_kernel(w):
    r = w.shape[0]
    return jnp.concatenate([w[:, :, 0:NOPE].reshape(r, HEADS * NOPE),
                            w[:, :, NOPE:NOPE + HALF].reshape(r, HEADS * HALF),
                            w[:, :, NOPE + HALF:].reshape(r, HEADS * HALF)], axis=1)


def _w_uq_from_kernel(w):
    r = w.shape[0]
    return jnp.concatenate([w[:, 0:512].reshape(r, HEADS, NOPE), w[:, 512:640].reshape(r, HEADS, HALF),
                            w[:, 640:768].reshape(r, HEADS, HALF)], axis=2)


REP_NAMES = ("w_uk", "w_uv", "w_pool", "g_mix", "g_q", "g_kv", "pool_scale", "g_ffn", "g_final")


def kernel(x, c, positions, w_ada, b_ada, g_mix, w_in, g_q, g_kv, w_uq, w_uk, w_uv, w_pool, pool_scale, w_o, g_ffn, w_gate, w_up, w_down, g_final, loss_target, m_w_ada, m_b_ada, m_g_mix, m_w_in, m_g_q, m_g_kv, m_w_uq, m_w_uk, m_w_uv, m_w_pool, m_pool_scale, m_w_o, m_g_ffn, m_w_gate, m_w_up, m_w_down, m_g_final, v_w_ada, v_b_ada, v_g_mix, v_w_in, v_g_q, v_g_kv, v_w_uq, v_w_uk, v_w_uv, v_w_pool, v_pool_scale, v_w_o, v_g_ffn, v_w_gate, v_w_up, v_w_down, v_g_final):
    given = dict(locals())

    mod, c_all8 = _ada_mod(c, w_ada[0], b_ada)
    c_all = c_all8[:, 0, :]

    shards = (
        _w_in_to_kernel(w_in[0]).astype(BF), _w_uq_to_kernel(w_uq[0]).astype(BF), w_o[0].astype(BF),
        w_gate[0].T.astype(BF), w_up[0].T.astype(BF), w_down[0].astype(BF))
    gathered = _all_gather(shards)
    w_in_p, w_uq_p, w_o_f, wg_t, wu_t, wd = (g.reshape(NDEV * g.shape[1], g.shape[2]) for g in gathered)

    loss, dx, dmod, sharded, replicated = _local_step(
        x[0], positions[0], loss_target[0], mod, g_mix, w_in_p, g_q, g_kv, w_uq_p, w_uk[0], w_uv[0], w_pool[0],
        pool_scale, w_o_f, g_ffn, wg_t, wu_t, wd, g_final.reshape(1, D))

    parts = _scatter_partials(tuple(sharded[k] for k in ("w_in", "w_uq", "w_o", "w_gate", "w_up", "w_down")))
    g_in_p, g_uq_p, g_o, g_gate_t, g_up_t, g_down = _sum_partials(parts)
    grads = dict(w_in=_w_in_from_kernel(g_in_p), w_uq=_w_uq_from_kernel(g_uq_p).reshape(QL // NDEV, HEADS * 192),
                 w_o=g_o, w_gate=g_gate_t.T, w_up=g_up_t.T, w_down=g_down)

    flat = jnp.concatenate([replicated[k].reshape(-1) for k in REP_NAMES])
    flat = jnp.pad(flat, (0, NDEV * REP_ROWS * 128 - flat.shape[0])).reshape(NDEV, REP_ROWS, 128)
    dmod_blocks = jnp.pad(dmod.reshape(NDEV, MODC // 128, 128), ((0, 0), (0, MOD_ROWS - MODC // 128), (0, 0)))
    got, red = _small_all_reduce(jnp.concatenate([dmod_blocks, flat], axis=1))
    dmod_rows = got[:, 0:MODC // 128, :].reshape(NDEV, MODC)
    grads["b_ada"] = red[:, 0:MODC // 128, :].reshape(1, N_MOD * D)
    rep_flat = red[:, MOD_ROWS:, :].reshape(-1)
    off = 0
    for k in REP_NAMES:
        size = int(np.prod(given[k].shape))
        grads[k] = rep_flat[off:off + size]
        off += size

    view = dict(w_ada=(D, MODC), b_ada=(1, N_MOD * D), g_mix=(1, D), w_in=(D // NDEV, 960), g_q=(1, QL),
                g_kv=(1, KVL), w_uq=(QL // NDEV, HEADS * 192), w_uk=(KVL, HEADS * NOPE), w_uv=(KVL, HEADS * 128),
                w_pool=(GROUPS * GD, GD), pool_scale=(1, PW), w_o=(D // NDEV, D), g_ffn=(1, D),
                w_gate=(D, FF // NDEV), w_up=(D, FF // NDEV), w_down=(FF // NDEV, D), g_final=(1, D))
    names = list(view)
    g_ada, d_ada, m_ada, v_ada = _adamw_ada(w_ada[0], m_w_ada[0], v_w_ada[0], c_all.astype(BF), dmod_rows)
    out_g, out_d, out_m, out_v = dict(w_ada=g_ada), dict(w_ada=d_ada), dict(w_ada=m_ada), dict(w_ada=v_ada)
    groups = (("adamw_ffn", ("w_gate", "w_up", "w_down")),
              ("adamw_rest", tuple(k for k in names if k not in ("w_ada", "w_gate", "w_up", "w_down"))))
    for gname, members in groups:
        ws = [given[k].reshape(view[k]) for k in members]
        gs = [grads[k].reshape(view[k]) for k in members]
        ms = [given["m_" + k].reshape(view[k]) for k in members]
        vs = [given["v_" + k].reshape(view[k]) for k in members]
        ds, m2, v2 = _adamw_group(gname, ws, gs, ms, vs)
        for k, g, d, mm, vv in zip(members, gs, ds, m2, v2):
            out_g[k], out_d[k], out_m[k], out_v[k] = g, d, mm, vv

    total = lax.psum(loss, ("x", "y", "c"))
    shaped = lambda d: [d[k].reshape(given[k].shape) for k in names]
    return (total, dx[None], *shaped(out_g), *shaped(out_d), *shaped(out_m), *shaped(out_v))
```

```python
import functools

import numpy as np
import jax
import jax.numpy as jnp
from jax import lax
from jax.experimental import pallas as pl
from jax.experimental.pallas import tpu as pltpu

D = 1024
HEADS = 4
NOPE = 128
ROPE = 64
HALF = ROPE // 2
QL = 256
KVL = 128
FF = 2816
PW = 512
GROUPS = 4
GD = 128
N_MOD = 6
EPS = 1e-6
SM_SCALE = (NOPE + ROPE) ** -0.5
ROPE_THETA = 10000.0
NDEV = 8
MODC = N_MOD * D // NDEV

ADAM_LR = 0.001
ADAM_B1 = 0.9
ADAM_B2 = 0.999
ADAM_EPS = 1e-08
ADAM_WD = 0.01
ADAM_STEP = 10

BF = jnp.bfloat16
F32 = jnp.float32
VMEM_LIMIT_V7X = 60 * 1024 * 1024
MESH = pl.DeviceIdType.MESH

TQ = 256
TK = 256
QW = 256
MOD_ROWS = 8
REP_ROWS = 200
SMALL_ROWS = MOD_ROWS + REP_ROWS


def _params(sem=None):
    return pltpu.CompilerParams(dimension_semantics=sem, vmem_limit_bytes=VMEM_LIMIT_V7X)


def _dot(a, b):
    return jnp.dot(a, b, preferred_element_type=F32)


def _dot_nt(a, b):
    return lax.dot_general(a, b, (((1,), (1,)), ((), ())), preferred_element_type=F32)


def _dot_tn(a, b):
    return _dot(a.astype(F32).T.astype(BF), b)


def _full(shape):
    return pl.BlockSpec(shape, lambda *_: (0,) * len(shape))


def _rows(ts, cols):
    return pl.BlockSpec((ts, cols), lambda i: (i, 0))


def _vmem():
    return pl.BlockSpec(memory_space=pltpu.VMEM)


def _any():
    return pl.BlockSpec(memory_space=pl.ANY)


def _rms(v):
    return lax.rsqrt(jnp.mean(v * v, axis=-1, keepdims=True) + EPS)


def _rms_bwd(dn, n, r):
    return r * (dn - n * jnp.mean(dn * n, axis=-1, keepdims=True))


def _colsum(v):
    return jnp.sum(v, axis=0, keepdims=True)


def _swap_halves(v):
    lane = lax.broadcasted_iota(jnp.int32, v.shape, 1)
    return jnp.where(lane < HALF, pltpu.roll(v, 128 - HALF, 1), pltpu.roll(v, HALF, 1))


def _window_lane_width():
    lane = lax.broadcasted_iota(jnp.int32, (1, PW), 1)
    return jnp.where(lane < 128, 2.0, jnp.where(lane < 256, 4.0, jnp.where(lane < 384, 8.0, 16.0))).astype(F32)


def _window_sums(ext, back):
    n = ext.shape[0]

    def sh(v, k):
        return pltpu.roll(v, k if back else n - k, 0)

    s2 = ext + sh(ext, 1)
    e4 = s2[:, 128:]
    s4 = e4 + sh(e4, 2)
    e8 = s4[:, 128:]
    s8 = e8 + sh(e8, 4)
    e16 = s8[:, 128:]
    s16 = e16 + sh(e16, 8)
    return jnp.concatenate([s2[:, :128], s4[:, :128], s8[:, :128], s16], axis=1)


def _row_counts(first_row, ts):
    t1 = (first_row + lax.broadcasted_iota(jnp.int32, (ts, 1), 0) + 1).astype(F32)
    return jnp.minimum(t1, _window_lane_width())


def _fwd_in(x, mod, g_mix, w_in, g_q, g_kv, w_uq, wuk_dc, perm, cos4, sin4, csk, snk, w_pool, pool_scale):
    S = x.shape[0]
    ts = 512
    nsub = ts // TQ

    def body(x_ref, mod_ref, gmix_ref, win_ref, gq_ref, gkv_ref, wuq_ref, wuk_ref, perm_ref, cos_ref, sin_ref,
             csk_ref, snk_ref, wpool_ref, pscale_ref,
             h1_ref, raw_ref, qn_ref, qs_ref, kv_ref, pooled_ref, ypre_ref, ypool_ref, carry_ref):
        i = pl.program_id(0)

        @pl.when(i == 0)
        def _():
            carry_ref[...] = jnp.zeros_like(carry_ref)

        xv = x_ref[...]
        sh1 = mod_ref[0:1, 0:D]
        sc1 = mod_ref[0:1, D:2 * D]
        h = (xv * _rms(xv)) * gmix_ref[...] * (1.0 + sc1) + sh1
        hb = h.astype(BF)
        h1_ref[...] = hb
        proj = _dot(hb, win_ref[...])
        cq_raw = proj[:, 0:QL]
        ckv_raw = proj[:, QL:QL + KVL]
        kr = proj[:, 384:512]
        u = proj[:, 512:1024]
        raw_ref[...] = proj[:, 0:384]

        c_q = (cq_raw * _rms(cq_raw)) * gq_ref[...]
        c_kv = (ckv_raw * _rms(ckv_raw)) * gkv_ref[...]
        q = _dot(c_q.astype(BF), wuq_ref[...])
        qn = q[:, 0:HEADS * NOPE].astype(BF)
        qn_ref[...] = qn
        x1 = q[:, 512:640]
        x2 = q[:, 640:768]
        cosv = cos_ref[...]
        sinv = sin_ref[...]
        roped = jnp.concatenate([x1 * cosv - x2 * sinv, x1 * sinv + x2 * cosv], axis=1).astype(BF)
        for hd in range(HEADS):
            q_lat = _dot(qn[:, hd * NOPE:(hd + 1) * NOPE], wuk_ref[hd])
            q_rope = _dot(roped, perm_ref[hd])
            qh = jnp.concatenate([q_lat, q_rope], axis=1).astype(BF)
            for a in range(nsub):
                qs_ref[a, hd * TQ:(hd + 1) * TQ, :] = qh[a * TQ:(a + 1) * TQ, :]
        k_rope = kr * csk_ref[...] + _swap_halves(kr) * snk_ref[...]
        kv_ref[...] = jnp.concatenate([c_kv, k_rope], axis=1).astype(BF)

        ext = jnp.concatenate([carry_ref[...], u], axis=0)
        win = _window_sums(ext, True)[16:, :]
        pooled = (win / _row_counts(i * ts, ts) - u).astype(BF)
        pooled_ref[...] = pooled
        carry_ref[...] = u[ts - 16:ts, :]
        ypre = jnp.concatenate(
            [_dot(pooled[:, g * GD:(g + 1) * GD], wpool_ref[g]) for g in range(GROUPS)], axis=1)
        ypre_ref[...] = ypre
        ypool_ref[...] = (ypre * pscale_ref[...]).astype(BF)

    out_shape = (
        jax.ShapeDtypeStruct((S, D), BF),
        jax.ShapeDtypeStruct((S, 384), F32),
        jax.ShapeDtypeStruct((S, HEADS * NOPE), BF),
        jax.ShapeDtypeStruct((S // TQ, HEADS * TQ, QW), BF),
        jax.ShapeDtypeStruct((S, QW), BF),
        jax.ShapeDtypeStruct((S, PW), BF),
        jax.ShapeDtypeStruct((S, PW), F32),
        jax.ShapeDtypeStruct((S, PW), BF),
    )
    in_specs = [
        _rows(ts, D), _full(mod.shape), _full((1, D)), _full(w_in.shape), _full((1, QL)), _full((1, KVL)),
        _full(w_uq.shape), _full(wuk_dc.shape), _full(perm.shape), _rows(ts, 128), _rows(ts, 128), _rows(ts, 128),
        _rows(ts, 128), _full(w_pool.shape), _full((1, PW)),
    ]
    out_specs = (
        _rows(ts, D), _rows(ts, 384), _rows(ts, HEADS * NOPE),
        pl.BlockSpec((nsub, HEADS * TQ, QW), lambda i: (i, 0, 0)),
        _rows(ts, QW), _rows(ts, PW), _rows(ts, PW), _rows(ts, PW),
    )
    return pl.pallas_call(
        body, name="fwd_in", out_shape=out_shape, grid=(S // ts,), in_specs=in_specs, out_specs=out_specs,
        scratch_shapes=[pltpu.VMEM((16, PW), F32)], compiler_params=_params(("arbitrary",)),
    )(x, mod, g_mix, w_in, g_q, g_kv, w_uq, wuk_dc, perm, cos4, sin4, csk, snk, w_pool, pool_scale)


def _diag_mask(shape, q_axis):
    qi = (lax.broadcasted_iota(jnp.int32, shape, q_axis) & (TQ - 1)) >> 6
    ki = lax.broadcasted_iota(jnp.int32, shape, 1 - q_axis) >> 6
    return ki <= qi


def _attn_fwd(qs, kv, wuv_cv):
    nq = qs.shape[0]
    S = kv.shape[0]
    M = HEADS * TQ

    def body(qs_ref, kv_ref, wuv_ref, olat_ref, ymla_ref, lse_ref):
        i = pl.program_id(0)
        q = qs_ref[0]

        def step(kt, carry, masked):
            m, l, acc = carry
            k = kv_ref[pl.ds(pl.multiple_of(kt * TK, TK), TK), :]
            s = _dot_nt(q, k) * SM_SCALE
            if masked:
                s = jnp.where(_diag_mask((M, TK), 0), s, -jnp.inf)
            m_new = jnp.maximum(m, jnp.max(s, axis=-1, keepdims=True))
            alpha = jnp.exp(m - m_new)
            p = jnp.exp(s - m_new)
            l = alpha * l + jnp.sum(p, axis=-1, keepdims=True)
            acc = alpha * acc + _dot(p.astype(BF), k[:, 0:KVL])
            return m_new, l, acc

        init = (jnp.full((M, 1), -jnp.inf, F32), jnp.zeros((M, 1), F32), jnp.zeros((M, KVL), F32))
        carry = lax.fori_loop(0, i, lambda kt, c: step(kt, c, False), init)
        m, l, acc = step(i, carry, True)
        o_lat = acc / l
        olat_ref[0] = o_lat
        lse = m + jnp.log(l)
        lse_ref[0] = jnp.broadcast_to(lse, (M, 128)).T[0:8, :]
        for hd in range(HEADS):
            o = _dot(o_lat[hd * TQ:(hd + 1) * TQ, :].astype(BF), wuv_ref[hd])
            ymla_ref[:, hd * 128:(hd + 1) * 128] = o.astype(BF)

    out_shape = (
        jax.ShapeDtypeStruct((nq, M, KVL), F32),
        jax.ShapeDtypeStruct((S, HEADS * 128), BF),
        jax.ShapeDtypeStruct((nq, 8, M), F32),
    )
    return pl.pallas_call(
        body, name="attn_fwd", out_shape=out_shape, grid=(nq,),
        in_specs=[pl.BlockSpec((1, M, QW), lambda i: (i, 0, 0)), _full(kv.shape), _full(wuv_cv.shape)],
        out_specs=(pl.BlockSpec((1, M, KVL), lambda i: (i, 0, 0)), _rows(TQ, HEADS * 128),
                   pl.BlockSpec((1, 8, M), lambda i: (i, 0, 0))),
        compiler_params=_params(("arbitrary",)),
    )(qs, kv, wuv_cv)


def _silu_parts(a):
    sg = jax.nn.sigmoid(a)
    return sg, a * sg


def _ffn_fwd(x, ymla, ypool, mod, w_o, g_ffn, wg_t, wu_t, wd, g_final, target):
    S = x.shape[0]
    ts = 512
    tf = 256
    nj = FF // tf

    def body(x_ref, ymla_ref, ypool_ref, mod_ref, wo_ref, gffn_ref, wg_ref, wu_ref, wd_ref, gfin_ref, t_ref,
             x2_ref, mix_ref, h2_ref, a_ref, b_ref, dx3_ref, dff_ref, loss_ref, dgfin_ref, dgt2_ref, acc_ref):
        i = pl.program_id(0)
        j = pl.program_id(1)

        @pl.when(jnp.logical_and(i == 0, j == 0))
        def _():
            loss_ref[...] = jnp.zeros_like(loss_ref)
            dgfin_ref[...] = jnp.zeros_like(dgfin_ref)
            dgt2_ref[...] = jnp.zeros_like(dgt2_ref)

        @pl.when(j == 0)
        def _():
            gt1 = mod_ref[0:1, 2 * D:3 * D]
            sh2 = mod_ref[0:1, 3 * D:4 * D]
            sc2 = mod_ref[0:1, 4 * D:5 * D]
            cat = jnp.concatenate([ymla_ref[...], ypool_ref[...]], axis=1)
            mix = _dot(cat, wo_ref[...])
            mix_ref[...] = mix
            x2 = x_ref[...] + gt1 * mix
            x2_ref[...] = x2
            h2 = (x2 * _rms(x2)) * gffn_ref[...] * (1.0 + sc2) + sh2
            h2_ref[...] = h2.astype(BF)
            acc_ref[...] = jnp.zeros_like(acc_ref)

        h2b = h2_ref[...]
        a = _dot_nt(h2b, wg_ref[...])
        b = _dot_nt(h2b, wu_ref[...])
        a_ref[...] = a.astype(BF)
        b_ref[...] = b.astype(BF)
        f = _silu_parts(a)[1] * b
        acc_ref[...] += _dot(f.astype(BF), wd_ref[...])

        @pl.when(j == nj - 1)
        def _():
            gt2 = mod_ref[0:1, 5 * D:6 * D]
            ff = acc_ref[...]
            x3 = x2_ref[...] + gt2 * ff
            r3 = _rms(x3)
            xn3 = x3 * r3
            gfin = gfin_ref[...]
            e = xn3 * gfin - t_ref[...]
            loss_ref[...] += 0.5 * jnp.sum(jnp.mean(e * e, axis=-1, keepdims=True))
            dy = e * (1.0 / D)
            dgfin_ref[...] += _colsum(dy * xn3)
            dx3 = _rms_bwd(dy * gfin, xn3, r3)
            dx3_ref[...] = dx3
            dgt2_ref[...] += _colsum(dx3 * ff)
            dff_ref[...] = (dx3 * gt2).astype(BF)

    row = lambda c: pl.BlockSpec((ts, c), lambda i, j: (i, 0))
    wblk = pl.BlockSpec((tf, D), lambda i, j: (j, 0))
    act = pl.BlockSpec((ts, tf), lambda i, j: (i, j))
    const = lambda shape: pl.BlockSpec(shape, lambda i, j: (0,) * len(shape))
    out_shape = (
        jax.ShapeDtypeStruct((S, D), F32),
        jax.ShapeDtypeStruct((S, D), F32),
        jax.ShapeDtypeStruct((S, D), BF),
        jax.ShapeDtypeStruct((S, FF), BF),
        jax.ShapeDtypeStruct((S, FF), BF),
        jax.ShapeDtypeStruct((S, D), F32),
        jax.ShapeDtypeStruct((S, D), BF),
        jax.ShapeDtypeStruct((8, 128), F32),
        jax.ShapeDtypeStruct((1, D), F32),
        jax.ShapeDtypeStruct((1, D), F32),
    )
    return pl.pallas_call(
        body, name="ffn_fwd", out_shape=out_shape, grid=(S // ts, nj),
        in_specs=[row(D), row(PW), row(PW), const(mod.shape), const(w_o.shape), const((1, D)), wblk, wblk, wblk,
                  const((1, D)), row(D)],
        out_specs=(row(D), row(D), row(D), act, act, row(D), row(D), const((8, 128)), const((1, D)), const((1, D))),
        scratch_shapes=[pltpu.VMEM((ts, D), F32)],
        compiler_params=_params(("arbitrary", "arbitrary")),
    )(x, ymla, ypool, mod, w_o, g_ffn, wg_t, wu_t, wd, g_final, target)


def _ffn_bwd(dff, h2, a, b, wg_t, wu_t, wd):
    S = dff.shape[0]
    ts = 1024
    tf = 256
    ni = S // ts
    nj = FF // tf

    def body(dff_ref, h2_ref, a_ref, b_ref, wg_ref, wu_ref, wd_ref,
             dwg_ref, dwu_ref, dwd_ref, dh2_ref, gacc, uacc, dacc, dh2acc):
        j = pl.program_id(0)
        i = pl.program_id(1)
        dffb = dff_ref[...]
        h2b = h2_ref[...]
        av = a_ref[...].astype(F32)
        bv = b_ref[...].astype(F32)
        df = _dot_nt(dffb, wd_ref[...])
        sg, sa = _silu_parts(av)
        f = sa * bv
        db = df * sa
        da = df * bv * (sg * (1.0 + av * (1.0 - sg)))
        dab = da.astype(BF)
        dbb = db.astype(BF)

        @pl.when(i == 0)
        def _():
            gacc[...] = jnp.zeros_like(gacc)
            uacc[...] = jnp.zeros_like(uacc)
            dacc[...] = jnp.zeros_like(dacc)

        gacc[...] += _dot_tn(da, h2b)
        uacc[...] += _dot_tn(db, h2b)
        dacc[...] += _dot_tn(f, dffb)
        contrib = _dot(dab, wg_ref[...]) + _dot(dbb, wu_ref[...])
        rows = pl.ds(pl.multiple_of(i * ts, ts), ts)

        @pl.when(j == 0)
        def _():
            dh2acc[rows, :] = contrib

        @pl.when(j > 0)
        def _():
            dh2acc[rows, :] += contrib

        @pl.when(i == ni - 1)
        def _():
            dwg_ref[...] = gacc[...].astype(BF)
            dwu_ref[...] = uacc[...].astype(BF)
            dwd_ref[...] = dacc[...].astype(BF)

        @pl.when(j == nj - 1)
        def _():
            dh2_ref[...] = dh2acc[rows, :]

    row = lambda c: pl.BlockSpec((ts, c), lambda j, i: (i, 0))
    act = pl.BlockSpec((ts, tf), lambda j, i: (i, j))
    wblk = pl.BlockSpec((tf, D), lambda j, i: (j, 0))
    out_shape = (
        jax.ShapeDtypeStruct((FF, D), BF), jax.ShapeDtypeStruct((FF, D), BF), jax.ShapeDtypeStruct((FF, D), BF),
        jax.ShapeDtypeStruct((S, D), F32),
    )
    return pl.pallas_call(
        body, name="ffn_bwd", out_shape=out_shape, grid=(nj, ni),
        in_specs=[row(D), row(D), act, act, wblk, wblk, wblk],
        out_specs=(wblk, wblk, wblk, pl.BlockSpec((ts, D), lambda j, i: (jnp.where(j == nj - 1, i, 0), 0))),
        scratch_shapes=[pltpu.VMEM((tf, D), F32), pltpu.VMEM((tf, D), F32), pltpu.VMEM((tf, D), F32),
                        pltpu.VMEM((S, D), F32)],
        compiler_params=_params(("arbitrary", "arbitrary")),
    )(dff, h2, a, b, wg_t, wu_t, wd)


def _mix_bwd(dh2, dx3, x2, mix, mod, g_ffn, ymla, ypool, w_o, ypre, pooled, pool_scale, wpool_dc, olat, wuv_vc):
    S = dh2.shape[0]
    ts = 512
    n = S // ts
    nsub = ts // TQ
    M = HEADS * TQ

    def body(dh2_ref, dx3_ref, x2_ref, mix_ref, mod_ref, gffn_ref, ymla_ref, ypool_ref, wo_ref, ypre_ref, pooled_ref,
             pscale_ref, wpool_ref, olat_ref, wuv_ref,
             dx2_ref, du_ref, dolat_ref, delta_ref, dwo_ref, dwuv_ref, dwpool_ref, dpscale_ref, dgt1_ref, dsc2_ref,
             dsh2_ref, dgffn_ref, carry_ref, dwo_acc):
        i = pl.program_id(0)

        @pl.when(i == 0)
        def _():
            carry_ref[...] = jnp.zeros_like(carry_ref)
            dwo_acc[...] = jnp.zeros_like(dwo_acc)
            for r in (dwuv_ref, dwpool_ref, dpscale_ref, dgt1_ref, dsc2_ref, dsh2_ref, dgffn_ref):
                r[...] = jnp.zeros_like(r)

        gt1 = mod_ref[0:1, 2 * D:3 * D]
        sc2 = mod_ref[0:1, 4 * D:5 * D]
        gffn = gffn_ref[...]
        dh2 = dh2_ref[...]
        x2 = x2_ref[...]
        r2 = _rms(x2)
        xn2 = x2 * r2
        dsc2_ref[...] += _colsum(dh2 * (xn2 * gffn))
        dsh2_ref[...] += _colsum(dh2)
        dgffn_ref[...] += _colsum(dh2 * (1.0 + sc2) * xn2)
        dx2 = dx3_ref[...] + _rms_bwd(dh2 * gffn * (1.0 + sc2), xn2, r2)
        dx2_ref[...] = dx2
        dgt1_ref[...] += _colsum(dx2 * mix_ref[...])
        dmix = (dx2 * gt1).astype(BF)
        cat = jnp.concatenate([ymla_ref[...], ypool_ref[...]], axis=1)
        dwo_acc[...] += _dot_tn(cat, dmix)
        dcat = _dot_nt(dmix, wo_ref[...])
        dymla = dcat[:, 0:512]
        dypool = dcat[:, 512:1024]

        dpscale_ref[...] += _colsum(dypool * ypre_ref[...])
        dypre = (dypool * pscale_ref[...]).astype(BF)
        pooled = pooled_ref[...]
        dpooled = []
        for g in range(GROUPS):
            sl = slice(g * GD, (g + 1) * GD)
            dwpool_ref[g] += _dot_tn(pooled[:, sl], dypre[:, sl])
            dpooled.append(_dot(dypre[:, sl], wpool_ref[g]))
        dpooled = jnp.concatenate(dpooled, axis=1)
        tile = n - 1 - i
        e = dpooled / _row_counts(tile * ts, ts)
        ext = jnp.concatenate([e, carry_ref[...]], axis=0)
        du_ref[...] = _window_sums(ext, False)[0:ts, :] - dpooled
        carry_ref[...] = e[0:16, :]

        for hd in range(HEADS):
            do = dymla[:, hd * 128:(hd + 1) * 128]
            dob = do.astype(BF)
            dol = _dot(dob, wuv_ref[hd])
            for a in range(nsub):
                ol = olat_ref[a, hd * TQ:(hd + 1) * TQ, :]
                dl = dol[a * TQ:(a + 1) * TQ, :]
                dolat_ref[a, hd * TQ:(hd + 1) * TQ, :] = dl.astype(BF)
                dwuv_ref[hd] += _dot_tn(ol, dob[a * TQ:(a + 1) * TQ, :])
                delta = jnp.sum(dl * ol, axis=-1, keepdims=True)
                delta_ref[a, :, hd * TQ:(hd + 1) * TQ] = jnp.broadcast_to(delta, (TQ, 128)).T[0:8, :]

        @pl.when(i == n - 1)
        def _():
            dwo_ref[...] = dwo_acc[...].astype(BF)

    rev = lambda c: pl.BlockSpec((ts, c), lambda i: (n - 1 - i, 0))
    rev3 = lambda r, c: pl.BlockSpec((nsub, r, c), lambda i: (n - 1 - i, 0, 0))
    out_shape = (
        jax.ShapeDtypeStruct((S, D), F32),
        jax.ShapeDtypeStruct((S, PW), F32),
        jax.ShapeDtypeStruct((S // TQ, M, KVL), BF),
        jax.ShapeDtypeStruct((S // TQ, 8, M), F32),
        jax.ShapeDtypeStruct((D, D), BF),
        jax.ShapeDtypeStruct((HEADS, KVL, 128), F32),
        jax.ShapeDtypeStruct((GROUPS, GD, GD), F32),
        jax.ShapeDtypeStruct((1, PW), F32),
        jax.ShapeDtypeStruct((1, D), F32), jax.ShapeDtypeStruct((1, D), F32), jax.ShapeDtypeStruct((1, D), F32),
        jax.ShapeDtypeStruct((1, D), F32),
    )
    in_specs = [rev(D), rev(D), rev(D), rev(D), _full(mod.shape), _full((1, D)), rev(PW), rev(PW), _full(w_o.shape),
                rev(PW), rev(PW), _full((1, PW)), _full(wpool_dc.shape), rev3(M, KVL), _full(wuv_vc.shape)]
    out_specs = (rev(D), rev(PW), rev3(M, KVL), rev3(8, M), _full((D, D)), _full((HEADS, KVL, 128)),
                 _full((GROUPS, GD, GD)), _full((1, PW)), _full((1, D)), _full((1, D)), _full((1, D)), _full((1, D)))
    return pl.pallas_call(
        body, name="mix_bwd", out_shape=out_shape, grid=(n,), in_specs=in_specs, out_specs=out_specs,
        scratch_shapes=[pltpu.VMEM((16, PW), F32), pltpu.VMEM((D, D), F32)],
        compiler_params=_params(("arbitrary",)),
    )(dh2, dx3, x2, mix, mod, g_ffn, ymla, ypool, w_o, ypre, pooled, pool_scale, wpool_dc, olat, wuv_vc)


def _attn_bwd(qs, kv, dolat, lse, delta):
    nq = qs.shape[0]
    S = kv.shape[0]
    M = HEADS * TQ
    nk = S // TK

    def body(qs_ref, kv_ref, do_ref, lse_ref, delta_ref, dkv_ref, dqt_ref):
        kt = pl.program_id(0)
        k = kv_ref[...]
        v = k[:, 0:KVL]
        k_t = k.astype(F32).T.astype(BF)

        @pl.when(kt == 0)
        def _():
            dqt_ref[...] = jnp.zeros_like(dqt_ref)

        def step(qi, carry, masked):
            dk, dv = carry
            q = qs_ref[qi]
            do = do_ref[qi]
            s = _dot_nt(k, q) * SM_SCALE
            p = jnp.exp(s - lse_ref[qi, 0:1, :])
            if masked:
                p = jnp.where(_diag_mask((TK, M), 1), p, 0.0)
            dp = _dot_nt(v, do)
            ds = (p * (dp - delta_ref[qi, 0:1, :]) * SM_SCALE).astype(BF)
            dv = dv + _dot(p.astype(BF), do)
            dk = dk + _dot(ds, q)
            dqt_ref[qi] += _dot(k_t, ds)
            return dk, dv

        carry = step(kt, (jnp.zeros((TK, QW), F32), jnp.zeros((TK, KVL), F32)), True)
        dk, dv = lax.fori_loop(kt + 1, nq, lambda qi, c: step(qi, c, False), carry)
        dkv_ref[...] = dk + jnp.concatenate([dv, jnp.zeros((TK, QW - KVL), F32)], axis=1)

    out_shape = (jax.ShapeDtypeStruct((S, QW), F32), jax.ShapeDtypeStruct((nq, QW, M), F32))
    return pl.pallas_call(
        body, name="attn_bwd", out_shape=out_shape, grid=(nk,),
        in_specs=[_vmem(), _rows(TK, QW), _vmem(), _vmem(), _vmem()],
        out_specs=(_rows(TK, QW), _vmem()),
        compiler_params=_params(("arbitrary",)),
    )(qs, kv, dolat, lse, delta)


def _in_bwd(dqt, dkv, du, raw, qn, h1, x, dx2, mod, g_mix, w_in, g_q, g_kv, w_uq, wuk_cd, perm_t, cos4, sin4, csk,
            snk):
    S = x.shape[0]
    ts = 512
    n = S // ts
    nsub = ts // TQ
    M = HEADS * TQ

    def body(dqt_ref, dkv_ref, du_ref, raw_ref, qn_ref, h1_ref, x_ref, dx2_ref, mod_ref, gmix_ref, win_ref, gq_ref,
             gkv_ref, wuq_ref, wuk_ref, permt_ref, cos_ref, sin_ref, csk_ref, snk_ref,
             dx_ref, dwin_ref, dwuq_ref, dwuk_ref, dgq_ref, dgkv_ref, dsc1_ref, dsh1_ref, dgmix_ref, dwin_acc,
             dwuq_acc):
        i = pl.program_id(0)

        @pl.when(i == 0)
        def _():
            dwin_acc[...] = jnp.zeros_like(dwin_acc)
            dwuq_acc[...] = jnp.zeros_like(dwuq_acc)
            for r in (dwuk_ref, dgq_ref, dgkv_ref, dsc1_ref, dsh1_ref, dgmix_ref):
                r[...] = jnp.zeros_like(r)

        dq_blocks = [dqt_ref[a].T for a in range(nsub)]
        qn = qn_ref[...]
        dq_parts = []
        drope = jnp.zeros((ts, 2 * 128), F32)
        for hd in range(HEADS):
            dqh = jnp.concatenate([blk[hd * TQ:(hd + 1) * TQ, :] for blk in dq_blocks], axis=0)
            dq_lat = dqh[:, 0:KVL].astype(BF)
            dq_parts.append(_dot(dq_lat, wuk_ref[hd]))
            dwuk_ref[hd] += _dot_tn(dq_lat, qn[:, hd * NOPE:(hd + 1) * NOPE])
            drope = drope + _dot(dqh[:, KVL:QW].astype(BF), permt_ref[hd])
        do1 = drope[:, 0:128]
        do2 = drope[:, 128:256]
        cosv = cos_ref[...]
        sinv = sin_ref[...]
        dq_parts.append(do1 * cosv + do2 * sinv)
        dq_parts.append(do2 * cosv - do1 * sinv)
        dq = jnp.concatenate(dq_parts, axis=1).astype(BF)

        cq_raw = raw_ref[:, 0:QL]
        ckv_raw = raw_ref[:, QL:QL + KVL]
        rq = _rms(cq_raw)
        nq_ = cq_raw * rq
        gq = gq_ref[...]
        dwuq_acc[...] += _dot_tn((nq_ * gq).astype(BF), dq)
        dc_q = _dot_nt(dq, wuq_ref[...])
        dgq_ref[...] += _colsum(dc_q * nq_)
        dcq_raw = _rms_bwd(dc_q * gq, nq_, rq)

        dkv = dkv_ref[...]
        rk = _rms(ckv_raw)
        nk_ = ckv_raw * rk
        dc_kv = dkv[:, 0:KVL]
        dgkv_ref[...] += _colsum(dc_kv * nk_)
        dckv_raw = _rms_bwd(dc_kv * gkv_ref[...], nk_, rk)
        dkr_roped = dkv[:, KVL:QW]
        dkr = dkr_roped * csk_ref[...] - _swap_halves(dkr_roped) * snk_ref[...]

        dproj = jnp.concatenate([dcq_raw, dckv_raw, dkr, du_ref[...]], axis=1).astype(BF)
        dwin_acc[...] += _dot_tn(h1_ref[...], dproj)
        dh1 = _dot_nt(dproj, win_ref[...])

        sc1 = mod_ref[0:1, D:2 * D]
        gmix = gmix_ref[...]
        xv = x_ref[...]
        r1 = _rms(xv)
        xn1 = xv * r1
        dsc1_ref[...] += _colsum(dh1 * (xn1 * gmix))
        dsh1_ref[...] += _colsum(dh1)
        dgmix_ref[...] += _colsum(dh1 * (1.0 + sc1) * xn1)
        dx_ref[...] = dx2_ref[...] + _rms_bwd(dh1 * gmix * (1.0 + sc1), xn1, r1)

        @pl.when(i == n - 1)
        def _():
            dwin_ref[...] = dwin_acc[...].astype(BF)
            dwuq_ref[...] = dwuq_acc[...].astype(BF)

    out_shape = (
        jax.ShapeDtypeStruct((S, D), F32),
        jax.ShapeDtypeStruct((D, D), BF),
        jax.ShapeDtypeStruct((QL, 768), BF),
        jax.ShapeDtypeStruct((HEADS, KVL, NOPE), F32),
        jax.ShapeDtypeStruct((1, QL), F32), jax.ShapeDtypeStruct((1, KVL), F32),
        jax.ShapeDtypeStruct((1, D), F32), jax.ShapeDtypeStruct((1, D), F32), jax.ShapeDtypeStruct((1, D), F32),
    )
    in_specs = [pl.BlockSpec((nsub, QW, M), lambda i: (i, 0, 0)), _rows(ts, QW), _rows(ts, PW), _rows(ts, 384),
                _rows(ts, HEADS * NOPE), _rows(ts, D), _rows(ts, D), _rows(ts, D), _full(mod.shape), _full((1, D)),
                _full(w_in.shape), _full((1, QL)), _full((1, KVL)), _full(w_uq.shape), _full(wuk_cd.shape),
                _full(perm_t.shape), _rows(ts, 128), _rows(ts, 128), _rows(ts, 128), _rows(ts, 128)]
    out_specs = (_rows(ts, D), _full((D, D)), _full((QL, 768)), _full((HEADS, KVL, NOPE)), _full((1, QL)),
                 _full((1, KVL)), _full((1, D)), _full((1, D)), _full((1, D)))
    return pl.pallas_call(
        body, name="in_bwd", out_shape=out_shape, grid=(n,), in_specs=in_specs, out_specs=out_specs,
        scratch_shapes=[pltpu.VMEM((D, D), F32), pltpu.VMEM((QL, 768), F32)],
        compiler_params=_params(("arbitrary",)),
    )(dqt, dkv, du, raw, qn, h1, x, dx2, mod, g_mix, w_in, g_q, g_kv, w_uq, wuk_cd, perm_t, cos4, sin4, csk, snk)


def _rope_perm():
    p = np.zeros((HEADS, 2 * 128, 128), np.float32)
    for hd in range(HEADS):
        for t in range(HALF):
            p[hd, hd * HALF + t, t] = 1.0
            p[hd, 128 + hd * HALF + t, HALF + t] = 1.0
    return p


def _rope_tables(positions):
    freqs = jnp.power(ROPE_THETA, -jnp.arange(HALF, dtype=F32) / HALF)
    ang = positions.astype(F32)[:, None] * freqs
    cos = jnp.cos(ang)
    sin = jnp.sin(ang)
    zero = jnp.zeros_like(cos)
    cos4 = jnp.tile(cos, (1, HEADS))
    sin4 = jnp.tile(sin, (1, HEADS))
    csk = jnp.concatenate([cos, cos, zero, zero], axis=1)
    snk = jnp.concatenate([-sin, sin, zero, zero], axis=1)
    return cos4, sin4, csk, snk


def _local_step(x, positions, target, mod, g_mix, w_in_p, g_q, g_kv, w_uq_p, w_uk, w_uv, w_pool, pool_scale, g_ffn,
                g_final, late_weights, ffn_grads_start):
    perm = jnp.asarray(_rope_perm(), BF)
    perm_t = jnp.asarray(_rope_perm().transpose(0, 2, 1), BF)
    cos4, sin4, csk, snk = _rope_tables(positions)
    wuk_dc = w_uk.transpose(1, 2, 0).astype(BF)
    wuk_cd = w_uk.transpose(1, 0, 2).astype(BF)
    wuv_cv = w_uv.transpose(1, 0, 2).astype(BF)
    wuv_vc = w_uv.transpose(1, 2, 0).astype(BF)
    wpool = w_pool.astype(BF)
    wpool_dc = w_pool.transpose(0, 2, 1).astype(BF)

    h1, raw, qn, qs, kv, pooled, ypre, ypool = _fwd_in(
        x, mod, g_mix, w_in_p, g_q, g_kv, w_uq_p, wuk_dc, perm, cos4, sin4, csk, snk, wpool, pool_scale)
    olat, ymla, lse = _attn_fwd(qs, kv, wuv_cv)
    w_o, wg_t, wu_t, wd = late_weights(ymla)
    x2, mix, h2, a, b, dx3, dff, loss, dgfin, dgt2 = _ffn_fwd(
        x, ymla, ypool, mod, w_o, g_ffn, wg_t, wu_t, wd, g_final, target)
    dwg_t, dwu_t, dwd, dh2 = _ffn_bwd(dff, h2, a, b, wg_t, wu_t, wd)
    ffn_parts = ffn_grads_start(dwg_t, dwu_t, dwd)
    (dx2, du, dolat, delta, dwo, dwuv, dwpool, dpscale, dgt1, dsc2, dsh2, dgffn) = _mix_bwd(
        dh2, dx3, x2, mix, mod, g_ffn + ffn_parts[-1][0:1, 0:1], ymla, ypool, w_o, ypre, pooled, pool_scale, wpool_dc,
        olat, wuv_vc)
    dkv, dqt = _attn_bwd(qs, kv, dolat, lse, delta)
    dx, dwin, dwuq, dwuk, dgq, dgkv, dsc1, dsh1, dgmix = _in_bwd(
        dqt, dkv, du, raw, qn, h1, x, dx2, mod, g_mix, w_in_p, g_q, g_kv, w_uq_p, wuk_cd, perm_t, cos4, sin4, csk,
        snk)
    dmod = jnp.concatenate([dsh1, dsc1, dgt1, dsh2, dsc2, dgt2], axis=1)
    sharded = dict(w_in=dwin, w_uq=dwuq, w_o=dwo)
    replicated = dict(
        w_uk=dwuk.transpose(1, 0, 2), w_uv=dwuv.transpose(1, 0, 2), w_pool=dwpool, g_mix=dgmix, g_q=dgq, g_kv=dgkv,
        pool_scale=dpscale, g_ffn=dgffn, g_final=dgfin)
    return loss[0, 0], dx, dmod, sharded, replicated, ffn_parts


def _my_pos():
    return lax.axis_index("x"), lax.axis_index("y"), lax.axis_index("c")


def _peer(pos, k):
    x, y, c = pos
    return (1 - x if k & 4 else x, 1 - y if k & 2 else y, 1 - c if k & 1 else c)


def _index(pos):
    x, y, c = pos
    return 4 * x + 2 * y + c


def _remote(src, dst, send_sem, recv_sem, to):
    return pltpu.make_async_remote_copy(src_ref=src, dst_ref=dst, send_sem=send_sem, recv_sem=recv_sem,
                                        device_id=to, device_id_type=MESH)


def _ada_mod(c, w_ada, b_ada):
    def body(c_ref, w_ref, b_ref, mod_ref, call_ref, cbuf, sbuf, rbuf, send1, recv1, send2, recv2):
        me = _my_pos()
        mi = _index(me)
        cv = c_ref[...]
        cbuf[...] = jnp.broadcast_to(cv * jax.nn.sigmoid(cv), (8, D))
        call_ref[mi] = cbuf[...]
        first = [_remote(cbuf, call_ref.at[mi], send1.at[k - 1], recv1.at[k - 1], _peer(me, k)) for k in range(1, NDEV)]
        for cp in first:
            cp.start()
        for k in range(1, NDEV):
            _remote(cbuf, call_ref.at[_index(_peer(me, k))], send1.at[k - 1], recv1.at[k - 1], _peer(me, k)).wait_recv()
        c_all = jnp.concatenate([call_ref[b][0:1, :] for b in range(NDEV)], axis=0)
        blocks = _dot(c_all.astype(BF), w_ref[...].astype(BF))
        for b in range(NDEV):
            sbuf[b] = jnp.broadcast_to(blocks[b:b + 1, :], (8, MODC))
        second = []
        for k in range(1, NDEV):
            to = _peer(me, k)
            second.append(_remote(sbuf.at[_index(to)], rbuf.at[mi], send2.at[k - 1], recv2.at[k - 1], to))
        for cp in second:
            cp.start()
        rbuf[mi] = sbuf[mi]
        for k in range(1, NDEV):
            to = _peer(me, k)
            _remote(sbuf.at[_index(to)], rbuf.at[_index(to)], send2.at[k - 1], recv2.at[k - 1], to).wait_recv()
        for j in range(NDEV):
            mod_ref[:, j * MODC:(j + 1) * MODC] = rbuf[j] + b_ref[:, j * MODC:(j + 1) * MODC]
        for cp in first + second:
            cp.wait_send()

    return pl.pallas_call(
        body, name="ada_mod",
        out_shape=(jax.ShapeDtypeStruct((8, N_MOD * D), F32), jax.ShapeDtypeStruct((NDEV, 8, D), F32)),
        in_specs=[_vmem(), _vmem(), _vmem()], out_specs=(_vmem(), _vmem()),
        scratch_shapes=[pltpu.VMEM((8, D), F32), pltpu.VMEM((NDEV, 8, MODC), F32), pltpu.VMEM((NDEV, 8, MODC), F32),
                        pltpu.SemaphoreType.DMA((NDEV - 1,)), pltpu.SemaphoreType.DMA((NDEV - 1,)),
                        pltpu.SemaphoreType.DMA((NDEV - 1,)), pltpu.SemaphoreType.DMA((NDEV - 1,))],
        compiler_params=_params(),
    )(c, w_ada, b_ada)


def _all_gather(shards):
    n = len(shards)

    def body(*refs):
        ins, outs = refs[:n], refs[n:2 * n]
        send, recv, local = refs[2 * n:]
        me = _my_pos()
        mi = _index(me)
        own = [pltpu.make_async_copy(ins[a], outs[a].at[mi], local.at[a]) for a in range(n)]
        for cp in own:
            cp.start()
        sent = []
        for a in range(n):
            for k in range(1, NDEV):
                sent.append(_remote(ins[a], outs[a].at[mi], send.at[a, k - 1], recv.at[a, k - 1], _peer(me, k)))
        for cp in sent:
            cp.start()
        for a in range(n):
            for k in range(1, NDEV):
                to = _peer(me, k)
                _remote(ins[a], outs[a].at[_index(to)], send.at[a, k - 1], recv.at[a, k - 1], to).wait_recv()
        for cp in sent:
            cp.wait_send()
        for cp in own:
            cp.wait()

    return pl.pallas_call(
        body, name="gather_weights",
        out_shape=tuple(jax.ShapeDtypeStruct((NDEV,) + s.shape, s.dtype) for s in shards),
        in_specs=[_any()] * n, out_specs=tuple([_any()] * n),
        scratch_shapes=[pltpu.SemaphoreType.DMA((n, NDEV - 1)), pltpu.SemaphoreType.DMA((n, NDEV - 1)),
                        pltpu.SemaphoreType.DMA((n,))],
        compiler_params=_params(),
    )(*shards)


def _scatter_partials(grads):
    n = len(grads)

    def body(*refs):
        ins, outs = refs[:n], refs[n:2 * n]
        send, recv, local = refs[2 * n:]
        me = _my_pos()
        mi = _index(me)

        def rows_of(a, dev_index):
            r = ins[a].shape[0] // NDEV
            return ins[a].at[pl.ds(pl.multiple_of(dev_index * r, 16), r), :]

        own = [pltpu.make_async_copy(rows_of(a, mi), outs[a].at[mi], local.at[a]) for a in range(n)]
        for cp in own:
            cp.start()
        sent = []
        for a in range(n):
            for k in range(1, NDEV):
                to = _peer(me, k)
                sent.append(_remote(rows_of(a, _index(to)), outs[a].at[mi], send.at[a, k - 1], recv.at[a, k - 1], to))
        for cp in sent:
            cp.start()
        for a in range(n):
            for k in range(1, NDEV):
                to = _peer(me, k)
                _remote(rows_of(a, mi), outs[a].at[_index(to)], send.at[a, k - 1], recv.at[a, k - 1], to).wait_recv()
        for cp in sent:
            cp.wait_send()
        for cp in own:
            cp.wait()

    return pl.pallas_call(
        body, name="scatter_grads",
        out_shape=tuple(jax.ShapeDtypeStruct((NDEV, g.shape[0] // NDEV, g.shape[1]), g.dtype) for g in grads),
        in_specs=[_any()] * n, out_specs=tuple([_any()] * n),
        scratch_shapes=[pltpu.SemaphoreType.DMA((n, NDEV - 1)), pltpu.SemaphoreType.DMA((n, NDEV - 1)),
                        pltpu.SemaphoreType.DMA((n,))],
        compiler_params=_params(),
    )(*grads)


_HBM = pl.BlockSpec(memory_space=pltpu.HBM)
_SEM = pl.BlockSpec(memory_space=pltpu.SEMAPHORE)
_EFFECT = pltpu.SideEffectType.DATAFLOW_SIDE_EFFECTING


def _exchange_slices(scatter):
    def of(src, to_index):
        if not scatter:
            return src
        r = src.shape[0] // NDEV
        return src.at[pl.ds(pl.multiple_of(to_index * r, 16), r), :]
    return of


def _exchange_start(name, srcs, scatter):
    n = len(srcs)
    of = _exchange_slices(scatter)
    zones = [lax.empty((NDEV, s.shape[0] // NDEV if scatter else s.shape[0], s.shape[1]), s.dtype) for s in srcs]

    def body(*refs):
        src, land = refs[:n], refs[n:2 * n]
        send, recv = refs[2 * n], refs[2 * n + 1]
        token, local = refs[4 * n + 2], refs[4 * n + 3]
        me = _my_pos()
        mi = _index(me)
        for a in range(n):
            for k in range(1, NDEV):
                to = _peer(me, k)
                s = a * (NDEV - 1) + k - 1
                _remote(of(src[a], _index(to)), land[a].at[mi], send.at[s], recv.at[s], to).start()
        own = [pltpu.make_async_copy(of(src[a], mi), land[a].at[mi], local.at[a]) for a in range(n)]
        for cp in own:
            cp.start()
        for cp in own:
            cp.wait()
        token[...] = jnp.zeros_like(token)

    sems = pltpu.SemaphoreType.DMA((n * (NDEV - 1),))
    return pl.pallas_call(
        body, name=name,
        out_shape=(sems, sems, *[pltpu.HBM(s.shape, s.dtype) for s in srcs],
                   *[pltpu.HBM(z.shape, z.dtype) for z in zones], jax.ShapeDtypeStruct((8, 128), F32)),
        in_specs=[_HBM] * (2 * n), out_specs=(_SEM, _SEM, *[_HBM] * (2 * n), _vmem()),
        input_output_aliases={i: 2 + i for i in range(2 * n)},
        scratch_shapes=[pltpu.SemaphoreType.DMA((n,))],
        compiler_params=pltpu.CompilerParams(has_side_effects=_EFFECT),
    )(*[pltpu.with_memory_space_constraint(s, pltpu.HBM) for s in srcs],
      *[pltpu.with_memory_space_constraint(z, pltpu.HBM) for z in zones])


def _exchange_wait(name, started, scatter, after):
    send, recv, *bufs, _ = started
    n = len(bufs) // 2
    of = _exchange_slices(scatter)

    def body(*refs):
        src, land = refs[:n], refs[n:2 * n]
        send_ref, recv_ref = refs[2 * n], refs[2 * n + 1]
        me = _my_pos()
        mi = _index(me)
        for a in range(n):
            for k in range(1, NDEV):
                to = _peer(me, k)
                s = a * (NDEV - 1) + k - 1
                cp = _remote(of(src[a], mi), land[a].at[_index(to)], send_ref.at[s], recv_ref.at[s], to)
                cp.wait_send()
                cp.wait_recv()

    outs = pl.pallas_call(
        body, name=name, out_shape=tuple(pltpu.HBM(b.shape, b.dtype) for b in bufs),
        in_specs=[_HBM] * (2 * n) + [_SEM, _SEM, _any()], out_specs=tuple([_HBM] * (2 * n)),
        input_output_aliases={i: i for i in range(2 * n)},
        compiler_params=pltpu.CompilerParams(has_side_effects=_EFFECT),
    )(*bufs, send, recv, after)
    return outs[n:]


def _sum_partials(parts):
    n = len(parts)

    def body(*refs):
        for a in range(n):
            acc = refs[a][0].astype(F32)
            for p in range(1, NDEV):
                acc = acc + refs[a][p].astype(F32)
            refs[n + a][...] = acc

    return pl.pallas_call(
        body, name="sum_partials",
        out_shape=tuple(jax.ShapeDtypeStruct(p.shape[1:], F32) for p in parts),
        in_specs=[_vmem()] * n, out_specs=tuple([_vmem()] * n), compiler_params=_params(),
    )(*parts)


def _small_all_reduce(buf):
    def body(buf_ref, got_ref, red_ref, mine, send1, recv1, send2, recv2):
        me = _my_pos()
        mi = _index(me)
        first = []
        for k in range(1, NDEV):
            to = _peer(me, k)
            first.append(_remote(buf_ref.at[_index(to)], got_ref.at[mi], send1.at[k - 1], recv1.at[k - 1], to))
        for cp in first:
            cp.start()
        got_ref[mi] = buf_ref[mi]
        for k in range(1, NDEV):
            to = _peer(me, k)
            _remote(buf_ref.at[mi], got_ref.at[_index(to)], send1.at[k - 1], recv1.at[k - 1], to).wait_recv()
        acc = got_ref[0]
        for p in range(1, NDEV):
            acc = acc + got_ref[p]
        mine[...] = acc
        second = [_remote(mine, red_ref.at[mi], send2.at[k - 1], recv2.at[k - 1], _peer(me, k)) for k in range(1, NDEV)]
        for cp in second:
            cp.start()
        red_ref[mi] = acc
        for k in range(1, NDEV):
            to = _peer(me, k)
            _remote(mine, red_ref.at[_index(to)], send2.at[k - 1], recv2.at[k - 1], to).wait_recv()
        for cp in first + second:
            cp.wait_send()

    return pl.pallas_call(
        body, name="small_all_reduce",
        out_shape=(jax.ShapeDtypeStruct(buf.shape, F32), jax.ShapeDtypeStruct(buf.shape, F32)),
        in_specs=[_vmem()], out_specs=(_vmem(), _vmem()),
        scratch_shapes=[pltpu.VMEM(buf.shape[1:], F32),
                        pltpu.SemaphoreType.DMA((NDEV - 1,)), pltpu.SemaphoreType.DMA((NDEV - 1,)),
                        pltpu.SemaphoreType.DMA((NDEV - 1,)), pltpu.SemaphoreType.DMA((NDEV - 1,))],
        compiler_params=_params(),
    )(buf)


def _adamw_math(w, g, m, v):
    m = ADAM_B1 * m + (1.0 - ADAM_B1) * g
    v = ADAM_B2 * v + (1.0 - ADAM_B2) * jnp.square(g)
    m_hat = m / (1.0 - ADAM_B1 ** ADAM_STEP)
    v_hat = v / (1.0 - ADAM_B2 ** ADAM_STEP)
    delta = -ADAM_LR * (m_hat / (jnp.sqrt(v_hat) + ADAM_EPS) + ADAM_WD * w)
    return delta, m, v


def _adamw_group(name, ws, gs, ms, vs):
    n = len(ws)

    def body(*refs):
        for a in range(n):
            w, g, m, v = (refs[q * n + a][...] for q in range(4))
            delta, m2, v2 = _adamw_math(w, g, m, v)
            refs[4 * n + a][...] = delta
            refs[5 * n + a][...] = m2
            refs[6 * n + a][...] = v2

    shapes = tuple(jax.ShapeDtypeStruct(w.shape, F32) for w in ws)
    outs = pl.pallas_call(
        body, name=name, out_shape=shapes * 3, in_specs=[_vmem()] * (4 * n), out_specs=tuple([_vmem()] * (3 * n)),
        compiler_params=_params(),
    )(*ws, *gs, *ms, *vs)
    return outs[:n], outs[n:2 * n], outs[2 * n:]


def _adamw_ada(w, m, v, c_all, dmod_rows):
    def body(w_ref, m_ref, v_ref, c_ref, dm_ref, g_ref, d_ref, m2_ref, v2_ref):
        g = _dot_tn(c_ref[...], dm_ref[...].astype(BF))
        g_ref[...] = g
        delta, m2, v2 = _adamw_math(w_ref[...], g, m_ref[...], v_ref[...])
        d_ref[...] = delta
        m2_ref[...] = m2
        v2_ref[...] = v2

    shp = jax.ShapeDtypeStruct(w.shape, F32)
    return pl.pallas_call(
        body, name="adamw_ada", out_shape=(shp, shp, shp, shp), in_specs=[_vmem()] * 5,
        out_specs=tuple([_vmem()] * 4), compiler_params=_params(),
    )(w, m, v, c_all, dmod_rows)


def _w_in_to_kernel(w):
    return jnp.concatenate([w[:, 0:448], jnp.zeros((w.shape[0], 64), w.dtype), w[:, 448:960]], axis=1)


def _w_in_from_kernel(w):
    return jnp.concatenate([w[:, 0:448], w[:, 512:1024]], axis=1)


def _w_uq_to_kernel(w):
    r = w.shape[0]
    return jnp.concatenate([w[:, :, 0:NOPE].reshape(r, HEADS * NOPE),
                            w[:, :, NOPE:NOPE + HALF].reshape(r, HEADS * HALF),
                            w[:, :, NOPE + HALF:].reshape(r, HEADS * HALF)], axis=1)


def _w_uq_from_kernel(w):
    r = w.shape[0]
    return jnp.concatenate([w[:, 0:512].reshape(r, HEADS, NOPE), w[:, 512:640].reshape(r, HEADS, HALF),
                            w[:, 640:768].reshape(r, HEADS, HALF)], axis=2)


REP_NAMES = ("w_uk", "w_uv", "w_pool", "g_mix", "g_q", "g_kv", "pool_scale", "g_ffn", "g_final")


def kernel(x, c, positions, w_ada, b_ada, g_mix, w_in, g_q, g_kv, w_uq, w_uk, w_uv, w_pool, pool_scale, w_o, g_ffn, w_gate, w_up, w_down, g_final, loss_target, m_w_ada, m_b_ada, m_g_mix, m_w_in, m_g_q, m_g_kv, m_w_uq, m_w_uk, m_w_uv, m_w_pool, m_pool_scale, m_w_o, m_g_ffn, m_w_gate, m_w_up, m_w_down, m_g_final, v_w_ada, v_b_ada, v_g_mix, v_w_in, v_g_q, v_g_kv, v_w_uq, v_w_uk, v_w_uv, v_w_pool, v_pool_scale, v_w_o, v_g_ffn, v_w_gate, v_w_up, v_w_down, v_g_final):
    given = dict(locals())

    merge = lambda g: g.reshape(NDEV * g.shape[1], g.shape[2])
    w_in_p, w_uq_p = (merge(g) for g in _all_gather(
        (_w_in_to_kernel(w_in[0]).astype(BF), _w_uq_to_kernel(w_uq[0]).astype(BF))))
    late = _exchange_start("gather_start", (w_o[0].astype(BF), w_gate[0].T.astype(BF), w_up[0].T.astype(BF),
                                            w_down[0].astype(BF)), False)
    weights_token = late[-1][0:1, 0:1]

    mod, c_all8 = _ada_mod(c + weights_token, w_ada[0], b_ada)
    c_all = c_all8[:, 0, :]

    def late_weights(after):
        return tuple(merge(g) for g in _exchange_wait("gather_wait", late, False, after))

    def ffn_grads_start(dwg_t, dwu_t, dwd):
        return _exchange_start("scatter_start", (dwg_t, dwu_t, dwd), True)

    loss, dx, dmod, sharded, replicated, ffn_parts = _local_step(
        x[0], positions[0], loss_target[0], mod, g_mix, w_in_p, g_q, g_kv, w_uq_p, w_uk[0], w_uv[0], w_pool[0],
        pool_scale, g_ffn, g_final.reshape(1, D), late_weights, ffn_grads_start)

    parts = _scatter_partials(tuple(sharded[k] for k in ("w_in", "w_uq", "w_o")))
    g_gate_p, g_up_p, g_down_p = _exchange_wait("scatter_wait", ffn_parts, True, parts[0])
    g_in_p, g_uq_p, g_o, g_gate_t, g_up_t, g_down = _sum_partials((*parts, g_gate_p, g_up_p, g_down_p))
    grads = dict(w_in=_w_in_from_kernel(g_in_p), w_uq=_w_uq_from_kernel(g_uq_p).reshape(QL // NDEV, HEADS * 192),
                 w_o=g_o, w_gate=g_gate_t.T, w_up=g_up_t.T, w_down=g_down)

    flat = jnp.concatenate([replicated[k].reshape(-1) for k in REP_NAMES])
    flat = jnp.pad(flat, (0, NDEV * REP_ROWS * 128 - flat.shape[0])).reshape(NDEV, REP_ROWS, 128)
    dmod_blocks = jnp.pad(dmod.reshape(NDEV, MODC // 128, 128), ((0, 0), (0, MOD_ROWS - MODC // 128), (0, 0)))
    got, red = _small_all_reduce(jnp.concatenate([dmod_blocks, flat], axis=1))
    dmod_rows = got[:, 0:MODC // 128, :].reshape(NDEV, MODC)
    grads["b_ada"] = red[:, 0:MODC // 128, :].reshape(1, N_MOD * D)
    rep_flat = red[:, MOD_ROWS:, :].reshape(-1)
    off = 0
    for k in REP_NAMES:
        size = int(np.prod(given[k].shape))
        grads[k] = rep_flat[off:off + size]
        off += size

    view = dict(w_ada=(D, MODC), b_ada=(1, N_MOD * D), g_mix=(1, D), w_in=(D // NDEV, 960), g_q=(1, QL),
                g_kv=(1, KVL), w_uq=(QL // NDEV, HEADS * 192), w_uk=(KVL, HEADS * NOPE), w_uv=(KVL, HEADS * 128),
                w_pool=(GROUPS * GD, GD), pool_scale=(1, PW), w_o=(D // NDEV, D), g_ffn=(1, D),
                w_gate=(D, FF // NDEV), w_up=(D, FF // NDEV), w_down=(FF // NDEV, D), g_final=(1, D))
    names = list(view)
    g_ada, d_ada, m_ada, v_ada = _adamw_ada(w_ada[0], m_w_ada[0], v_w_ada[0], c_all.astype(BF), dmod_rows)
    out_g, out_d, out_m, out_v = dict(w_ada=g_ada), dict(w_ada=d_ada), dict(w_ada=m_ada), dict(w_ada=v_ada)
    groups = (("adamw_ffn", ("w_gate", "w_up", "w_down")),
              ("adamw_rest", tuple(k for k in names if k not in ("w_ada", "w_gate", "w_up", "w_down"))))
    for gname, members in groups:
        ws = [given[k].reshape(view[k]) for k in members]
        gs = [grads[k].reshape(view[k]) for k in members]
        ms = [given["m_" + k].reshape(view[k]) for k in members]
        vs = [given["v_" + k].reshape(view[k]) for k in members]
        ds, m2, v2 = _adamw_group(gname, ws, gs, ms, vs)
        for k, g, d, mm, vv in zip(members, gs, ds, m2, v2):
            out_g[k], out_d[k], out_m[k], out_v[k] = g, d, mm, vv

    total = lax.psum(loss, ("x", "y", "c"))
    shaped = lambda d: [d[k].reshape(given[k].shape) for k in names]
    return (total, dx[None], *shaped(out_g), *shaped(out_d), *shaped(out_m), *shaped(out_v))
```

```python
import functools

import numpy as np
import jax
import jax.numpy as jnp
from jax import lax
from jax.experimental import pallas as pl
from jax.experimental.pallas import tpu as pltpu

D = 1024
HEADS = 4
NOPE = 128
ROPE = 64
HALF = ROPE // 2
QL = 256
KVL = 128
FF = 2816
PW = 512
GROUPS = 4
GD = 128
N_MOD = 6
EPS = 1e-6
SM_SCALE = (NOPE + ROPE) ** -0.5
ROPE_THETA = 10000.0
NDEV = 8
MODC = N_MOD * D // NDEV

ADAM_LR = 0.001
ADAM_B1 = 0.9
ADAM_B2 = 0.999
ADAM_EPS = 1e-08
ADAM_WD = 0.01
ADAM_STEP = 10

BF = jnp.bfloat16
F32 = jnp.float32
VMEM_LIMIT_V7X = 60 * 1024 * 1024
MESH = pl.DeviceIdType.MESH

TQ = 256
TK = 256
QW = 256
MOD_ROWS = 8
REP_ROWS = 200
SMALL_ROWS = MOD_ROWS + REP_ROWS


def _params(sem=None):
    return pltpu.CompilerParams(dimension_semantics=sem, vmem_limit_bytes=VMEM_LIMIT_V7X)


def _dot(a, b):
    return jnp.dot(a, b, preferred_element_type=F32)


def _dot_nt(a, b):
    return lax.dot_general(a, b, (((1,), (1,)), ((), ())), preferred_element_type=F32)


def _dot_tn(a, b):
    return _dot(a.astype(F32).T.astype(BF), b)


def _full(shape):
    return pl.BlockSpec(shape, lambda *_: (0,) * len(shape))


def _rows(ts, cols):
    return pl.BlockSpec((ts, cols), lambda i: (i, 0))


def _vmem():
    return pl.BlockSpec(memory_space=pltpu.VMEM)


def _any():
    return pl.BlockSpec(memory_space=pl.ANY)


def _rms(v):
    return lax.rsqrt(jnp.mean(v * v, axis=-1, keepdims=True) + EPS)


def _rms_bwd(dn, n, r):
    return r * (dn - n * jnp.mean(dn * n, axis=-1, keepdims=True))


def _colsum(v):
    return jnp.sum(v, axis=0, keepdims=True)


def _swap_halves(v):
    lane = lax.broadcasted_iota(jnp.int32, v.shape, 1)
    return jnp.where(lane < HALF, pltpu.roll(v, 128 - HALF, 1), pltpu.roll(v, HALF, 1))


def _window_lane_width():
    lane = lax.broadcasted_iota(jnp.int32, (1, PW), 1)
    return jnp.where(lane < 128, 2.0, jnp.where(lane < 256, 4.0, jnp.where(lane < 384, 8.0, 16.0))).astype(F32)


def _window_sums(ext, back):
    n = ext.shape[0]

    def sh(v, k):
        return pltpu.roll(v, k if back else n - k, 0)

    s2 = ext + sh(ext, 1)
    e4 = s2[:, 128:]
    s4 = e4 + sh(e4, 2)
    e8 = s4[:, 128:]
    s8 = e8 + sh(e8, 4)
    e16 = s8[:, 128:]
    s16 = e16 + sh(e16, 8)
    return jnp.concatenate([s2[:, :128], s4[:, :128], s8[:, :128], s16], axis=1)


def _row_counts(first_row, ts):
    t1 = (first_row + lax.broadcasted_iota(jnp.int32, (ts, 1), 0) + 1).astype(F32)
    return jnp.minimum(t1, _window_lane_width())


def _fwd_in(x, mod, g_mix, w_in, g_q, g_kv, w_uq, wuk_dc, perm, cos4, sin4, csk, snk, w_pool, pool_scale):
    S = x.shape[0]
    ts = 512
    nsub = ts // TQ

    def body(x_ref, mod_ref, gmix_ref, win_ref, gq_ref, gkv_ref, wuq_ref, wuk_ref, perm_ref, cos_ref, sin_ref,
             csk_ref, snk_ref, wpool_ref, pscale_ref,
             h1_ref, raw_ref, qn_ref, qs_ref, kv_ref, pooled_ref, ypre_ref, ypool_ref, carry_ref):
        i = pl.program_id(0)

        @pl.when(i == 0)
        def _():
            carry_ref[...] = jnp.zeros_like(carry_ref)

        xv = x_ref[...]
        sh1 = mod_ref[0:1, 0:D]
        sc1 = mod_ref[0:1, D:2 * D]
        h = (xv * _rms(xv)) * gmix_ref[...] * (1.0 + sc1) + sh1
        hb = h.astype(BF)
        h1_ref[...] = hb
        proj = _dot(hb, win_ref[...])
        cq_raw = proj[:, 0:QL]
        ckv_raw = proj[:, QL:QL + KVL]
        kr = proj[:, 384:512]
        u = proj[:, 512:1024]
        raw_ref[...] = proj[:, 0:384]

        c_q = (cq_raw * _rms(cq_raw)) * gq_ref[...]
        c_kv = (ckv_raw * _rms(ckv_raw)) * gkv_ref[...]
        q = _dot(c_q.astype(BF), wuq_ref[...])
        qn = q[:, 0:HEADS * NOPE].astype(BF)
        qn_ref[...] = qn
        x1 = q[:, 512:640]
        x2 = q[:, 640:768]
        cosv = cos_ref[...]
        sinv = sin_ref[...]
        roped = jnp.concatenate([x1 * cosv - x2 * sinv, x1 * sinv + x2 * cosv], axis=1).astype(BF)
        for hd in range(HEADS):
            q_lat = _dot(qn[:, hd * NOPE:(hd + 1) * NOPE], wuk_ref[hd])
            q_rope = _dot(roped, perm_ref[hd])
            qh = jnp.concatenate([q_lat, q_rope], axis=1).astype(BF)
            for a in range(nsub):
                qs_ref[a, hd * TQ:(hd + 1) * TQ, :] = qh[a * TQ:(a + 1) * TQ, :]
        k_rope = kr * csk_ref[...] + _swap_halves(kr) * snk_ref[...]
        kv_ref[...] = jnp.concatenate([c_kv, k_rope], axis=1).astype(BF)

        ext = jnp.concatenate([carry_ref[...], u], axis=0)
        win = _window_sums(ext, True)[16:, :]
        pooled = (win / _row_counts(i * ts, ts) - u).astype(BF)
        pooled_ref[...] = pooled
        carry_ref[...] = u[ts - 16:ts, :]
        ypre = jnp.concatenate(
            [_dot(pooled[:, g * GD:(g + 1) * GD], wpool_ref[g]) for g in range(GROUPS)], axis=1)
        ypre_ref[...] = ypre
        ypool_ref[...] = (ypre * pscale_ref[...]).astype(BF)

    out_shape = (
        jax.ShapeDtypeStruct((S, D), BF),
        jax.ShapeDtypeStruct((S, 384), F32),
        jax.ShapeDtypeStruct((S, HEADS * NOPE), BF),
        jax.ShapeDtypeStruct((S // TQ, HEADS * TQ, QW), BF),
        jax.ShapeDtypeStruct((S, QW), BF),
        jax.ShapeDtypeStruct((S, PW), BF),
        jax.ShapeDtypeStruct((S, PW), F32),
        jax.ShapeDtypeStruct((S, PW), BF),
    )
    in_specs = [
        _rows(ts, D), _full(mod.shape), _full((1, D)), _full(w_in.shape), _full((1, QL)), _full((1, KVL)),
        _full(w_uq.shape), _full(wuk_dc.shape), _full(perm.shape), _rows(ts, 128), _rows(ts, 128), _rows(ts, 128),
        _rows(ts, 128), _full(w_pool.shape), _full((1, PW)),
    ]
    out_specs = (
        _rows(ts, D), _rows(ts, 384), _rows(ts, HEADS * NOPE),
        pl.BlockSpec((nsub, HEADS * TQ, QW), lambda i: (i, 0, 0)),
        _rows(ts, QW), _rows(ts, PW), _rows(ts, PW), _rows(ts, PW),
    )
    return pl.pallas_call(
        body, name="fwd_in", out_shape=out_shape, grid=(S // ts,), in_specs=in_specs, out_specs=out_specs,
        scratch_shapes=[pltpu.VMEM((16, PW), F32)], compiler_params=_params(("arbitrary",)),
    )(x, mod, g_mix, w_in, g_q, g_kv, w_uq, wuk_dc, perm, cos4, sin4, csk, snk, w_pool, pool_scale)


def _diag_mask(shape, q_axis):
    qi = (lax.broadcasted_iota(jnp.int32, shape, q_axis) & (TQ - 1)) >> 6
    ki = lax.broadcasted_iota(jnp.int32, shape, 1 - q_axis) >> 6
    return ki <= qi


def _attn_fwd(qs, kv, wuv_cv):
    nq = qs.shape[0]
    S = kv.shape[0]
    M = HEADS * TQ

    def body(qs_ref, kv_ref, wuv_ref, olat_ref, ymla_ref, lse_ref):
        i = pl.program_id(0)
        q = qs_ref[0]

        def step(kt, carry, masked):
            m, l, acc = carry
            k = kv_ref[pl.ds(pl.multiple_of(kt * TK, TK), TK), :]
            s = _dot_nt(q, k) * SM_SCALE
            if masked:
                s = jnp.where(_diag_mask((M, TK), 0), s, -jnp.inf)
            m_new = jnp.maximum(m, jnp.max(s, axis=-1, keepdims=True))
            alpha = jnp.exp(m - m_new)
            p = jnp.exp(s - m_new)
            l = alpha * l + jnp.sum(p, axis=-1, keepdims=True)
            acc = alpha * acc + _dot(p.astype(BF), k[:, 0:KVL])
            return m_new, l, acc

        init = (jnp.full((M, 1), -jnp.inf, F32), jnp.zeros((M, 1), F32), jnp.zeros((M, KVL), F32))
        carry = lax.fori_loop(0, i, lambda kt, c: step(kt, c, False), init)
        m, l, acc = step(i, carry, True)
        o_lat = acc / l
        olat_ref[0] = o_lat
        lse = m + jnp.log(l)
        lse_ref[0] = jnp.broadcast_to(lse, (M, 128)).T[0:8, :]
        for hd in range(HEADS):
            o = _dot(o_lat[hd * TQ:(hd + 1) * TQ, :].astype(BF), wuv_ref[hd])
            ymla_ref[:, hd * 128:(hd + 1) * 128] = o.astype(BF)

    out_shape = (
        jax.ShapeDtypeStruct((nq, M, KVL), F32),
        jax.ShapeDtypeStruct((S, HEADS * 128), BF),
        jax.ShapeDtypeStruct((nq, 8, M), F32),
    )
    return pl.pallas_call(
        body, name="attn_fwd", out_shape=out_shape, grid=(nq,),
        in_specs=[pl.BlockSpec((1, M, QW), lambda i: (i, 0, 0)), _full(kv.shape), _full(wuv_cv.shape)],
        out_specs=(pl.BlockSpec((1, M, KVL), lambda i: (i, 0, 0)), _rows(TQ, HEADS * 128),
                   pl.BlockSpec((1, 8, M), lambda i: (i, 0, 0))),
        compiler_params=_params(("arbitrary",)),
    )(qs, kv, wuv_cv)


def _silu_parts(a):
    sg = jax.nn.sigmoid(a)
    return sg, a * sg


def _ffn_fwd(x, ymla, ypool, mod, w_o, g_ffn, wg_t, wu_t, wd, g_final, target):
    S = x.shape[0]
    ts = 512
    tf = 256
    nj = FF // tf

    def body(x_ref, ymla_ref, ypool_ref, mod_ref, wo_ref, gffn_ref, wg_ref, wu_ref, wd_ref, gfin_ref, t_ref,
             x2_ref, mix_ref, h2_ref, a_ref, b_ref, dx3_ref, dff_ref, loss_ref, dgfin_ref, dgt2_ref, acc_ref):
        i = pl.program_id(0)
        j = pl.program_id(1)

        @pl.when(jnp.logical_and(i == 0, j == 0))
        def _():
            loss_ref[...] = jnp.zeros_like(loss_ref)
            dgfin_ref[...] = jnp.zeros_like(dgfin_ref)
            dgt2_ref[...] = jnp.zeros_like(dgt2_ref)

        @pl.when(j == 0)
        def _():
            gt1 = mod_ref[0:1, 2 * D:3 * D]
            sh2 = mod_ref[0:1, 3 * D:4 * D]
            sc2 = mod_ref[0:1, 4 * D:5 * D]
            cat = jnp.concatenate([ymla_ref[...], ypool_ref[...]], axis=1)
            mix = _dot(cat, wo_ref[...])
            mix_ref[...] = mix
            x2 = x_ref[...] + gt1 * mix
            x2_ref[...] = x2
            h2 = (x2 * _rms(x2)) * gffn_ref[...] * (1.0 + sc2) + sh2
            h2_ref[...] = h2.astype(BF)
            acc_ref[...] = jnp.zeros_like(acc_ref)

        h2b = h2_ref[...]
        a = _dot_nt(h2b, wg_ref[...])
        b = _dot_nt(h2b, wu_ref[...])
        a_ref[...] = a.astype(BF)
        b_ref[...] = b.astype(BF)
        f = _silu_parts(a)[1] * b
        acc_ref[...] += _dot(f.astype(BF), wd_ref[...])

        @pl.when(j == nj - 1)
        def _():
            gt2 = mod_ref[0:1, 5 * D:6 * D]
            ff = acc_ref[...]
            x3 = x2_ref[...] + gt2 * ff
            r3 = _rms(x3)
            xn3 = x3 * r3
            gfin = gfin_ref[...]
            e = xn3 * gfin - t_ref[...]
            loss_ref[...] += 0.5 * jnp.sum(jnp.mean(e * e, axis=-1, keepdims=True))
            dy = e * (1.0 / D)
            dgfin_ref[...] += _colsum(dy * xn3)
            dx3 = _rms_bwd(dy * gfin, xn3, r3)
            dx3_ref[...] = dx3
            dgt2_ref[...] += _colsum(dx3 * ff)
            dff_ref[...] = (dx3 * gt2).astype(BF)

    row = lambda c: pl.BlockSpec((ts, c), lambda i, j: (i, 0))
    wblk = pl.BlockSpec((tf, D), lambda i, j: (j, 0))
    act = pl.BlockSpec((ts, tf), lambda i, j: (i, j))
    const = lambda shape: pl.BlockSpec(shape, lambda i, j: (0,) * len(shape))
    out_shape = (
        jax.ShapeDtypeStruct((S, D), F32),
        jax.ShapeDtypeStruct((S, D), F32),
        jax.ShapeDtypeStruct((S, D), BF),
        jax.ShapeDtypeStruct((S, FF), BF),
        jax.ShapeDtypeStruct((S, FF), BF),
        jax.ShapeDtypeStruct((S, D), F32),
        jax.ShapeDtypeStruct((S, D), BF),
        jax.ShapeDtypeStruct((8, 128), F32),
        jax.ShapeDtypeStruct((1, D), F32),
        jax.ShapeDtypeStruct((1, D), F32),
    )
    return pl.pallas_call(
        body, name="ffn_fwd", out_shape=out_shape, grid=(S // ts, nj),
        in_specs=[row(D), row(PW), row(PW), const(mod.shape), const(w_o.shape), const((1, D)), wblk, wblk, wblk,
                  const((1, D)), row(D)],
        out_specs=(row(D), row(D), row(D), act, act, row(D), row(D), const((8, 128)), const((1, D)), const((1, D))),
        scratch_shapes=[pltpu.VMEM((ts, D), F32)],
        compiler_params=_params(("arbitrary", "arbitrary")),
    )(x, ymla, ypool, mod, w_o, g_ffn, wg_t, wu_t, wd, g_final, target)


def _ffn_bwd(dff, h2, a, b, wg_t, wu_t, wd):
    S = dff.shape[0]
    ts = 1024
    tf = 256
    ni = S // ts
    nj = FF // tf

    def body(dff_ref, h2_ref, a_ref, b_ref, wg_ref, wu_ref, wd_ref,
             dwg_ref, dwu_ref, dwd_ref, dh2_ref, gacc, uacc, dacc, dh2acc):
        j = pl.program_id(0)
        i = pl.program_id(1)
        dffb = dff_ref[...]
        h2b = h2_ref[...]
        av = a_ref[...].astype(F32)
        bv = b_ref[...].astype(F32)
        df = _dot_nt(dffb, wd_ref[...])
        sg, sa = _silu_parts(av)
        f = sa * bv
        db = df * sa
        da = df * bv * (sg * (1.0 + av * (1.0 - sg)))
        dab = da.astype(BF)
        dbb = db.astype(BF)

        @pl.when(i == 0)
        def _():
            gacc[...] = jnp.zeros_like(gacc)
            uacc[...] = jnp.zeros_like(uacc)
            dacc[...] = jnp.zeros_like(dacc)

        gacc[...] += _dot_tn(da, h2b)
        uacc[...] += _dot_tn(db, h2b)
        dacc[...] += _dot_tn(f, dffb)
        contrib = _dot(dab, wg_ref[...]) + _dot(dbb, wu_ref[...])
        rows = pl.ds(pl.multiple_of(i * ts, ts), ts)

        @pl.when(j == 0)
        def _():
            dh2acc[rows, :] = contrib

        @pl.when(j > 0)
        def _():
            dh2acc[rows, :] += contrib

        @pl.when(i == ni - 1)
        def _():
            dwg_ref[...] = gacc[...].astype(BF)
            dwu_ref[...] = uacc[...].astype(BF)
            dwd_ref[...] = dacc[...].astype(BF)

        @pl.when(j == nj - 1)
        def _():
            dh2_ref[...] = dh2acc[rows, :]

    row = lambda c: pl.BlockSpec((ts, c), lambda j, i: (i, 0))
    act = pl.BlockSpec((ts, tf), lambda j, i: (i, j))
    wblk = pl.BlockSpec((tf, D), lambda j, i: (j, 0))
    out_shape = (
        jax.ShapeDtypeStruct((FF, D), BF), jax.ShapeDtypeStruct((FF, D), BF), jax.ShapeDtypeStruct((FF, D), BF),
        jax.ShapeDtypeStruct((S, D), F32),
    )
    return pl.pallas_call(
        body, name="ffn_bwd", out_shape=out_shape, grid=(nj, ni),
        in_specs=[row(D), row(D), act, act, wblk, wblk, wblk],
        out_specs=(wblk, wblk, wblk, pl.BlockSpec((ts, D), lambda j, i: (jnp.where(j == nj - 1, i, 0), 0))),
        scratch_shapes=[pltpu.VMEM((tf, D), F32), pltpu.VMEM((tf, D), F32), pltpu.VMEM((tf, D), F32),
                        pltpu.VMEM((S, D), F32)],
        compiler_params=_params(("arbitrary", "arbitrary")),
    )(dff, h2, a, b, wg_t, wu_t, wd)


def _mix_bwd(dh2, dx3, x2, mix, mod, g_ffn, ymla, ypool, w_o, ypre, pooled, pool_scale, wpool_dc, olat, wuv_vc):
    S = dh2.shape[0]
    ts = 512
    n = S // ts
    nsub = ts // TQ
    M = HEADS * TQ

    def body(dh2_ref, dx3_ref, x2_ref, mix_ref, mod_ref, gffn_ref, ymla_ref, ypool_ref, wo_ref, ypre_ref, pooled_ref,
             pscale_ref, wpool_ref, olat_ref, wuv_ref,
             dx2_ref, du_ref, dolat_ref, delta_ref, dwo_ref, dwuv_ref, dwpool_ref, dpscale_ref, dgt1_ref, dsc2_ref,
             dsh2_ref, dgffn_ref, carry_ref, dwo_acc):
        i = pl.program_id(0)

        @pl.when(i == 0)
        def _():
            carry_ref[...] = jnp.zeros_like(carry_ref)
            dwo_acc[...] = jnp.zeros_like(dwo_acc)
            for r in (dwuv_ref, dwpool_ref, dpscale_ref, dgt1_ref, dsc2_ref, dsh2_ref, dgffn_ref):
                r[...] = jnp.zeros_like(r)

        gt1 = mod_ref[0:1, 2 * D:3 * D]
        sc2 = mod_ref[0:1, 4 * D:5 * D]
        gffn = gffn_ref[...]
        dh2 = dh2_ref[...]
        x2 = x2_ref[...]
        r2 = _rms(x2)
        xn2 = x2 * r2
        dsc2_ref[...] += _colsum(dh2 * (xn2 * gffn))
        dsh2_ref[...] += _colsum(dh2)
        dgffn_ref[...] += _colsum(dh2 * (1.0 + sc2) * xn2)
        dx2 = dx3_ref[...] + _rms_bwd(dh2 * gffn * (1.0 + sc2), xn2, r2)
        dx2_ref[...] = dx2
        dgt1_ref[...] += _colsum(dx2 * mix_ref[...])
        dmix = (dx2 * gt1).astype(BF)
        cat = jnp.concatenate([ymla_ref[...], ypool_ref[...]], axis=1)
        dwo_acc[...] += _dot_tn(cat, dmix)
        dcat = _dot_nt(dmix, wo_ref[...])
        dymla = dcat[:, 0:512]
        dypool = dcat[:, 512:1024]

        dpscale_ref[...] += _colsum(dypool * ypre_ref[...])
        dypre = (dypool * pscale_ref[...]).astype(BF)
        pooled = pooled_ref[...]
        dpooled = []
        for g in range(GROUPS):
            sl = slice(g * GD, (g + 1) * GD)
            dwpool_ref[g] += _dot_tn(pooled[:, sl], dypre[:, sl])
            dpooled.append(_dot(dypre[:, sl], wpool_ref[g]))
        dpooled = jnp.concatenate(dpooled, axis=1)
        tile = n - 1 - i
        e = dpooled / _row_counts(tile * ts, ts)
        ext = jnp.concatenate([e, carry_ref[...]], axis=0)
        du_ref[...] = _window_sums(ext, False)[0:ts, :] - dpooled
        carry_ref[...] = e[0:16, :]

        for hd in range(HEADS):
            do = dymla[:, hd * 128:(hd + 1) * 128]
            dob = do.astype(BF)
            dol = _dot(dob, wuv_ref[hd])
            for a in range(nsub):
                ol = olat_ref[a, hd * TQ:(hd + 1) * TQ, :]
                dl = dol[a * TQ:(a + 1) * TQ, :]
                dolat_ref[a, hd * TQ:(hd + 1) * TQ, :] = dl.astype(BF)
                dwuv_ref[hd] += _dot_tn(ol, dob[a * TQ:(a + 1) * TQ, :])
                delta = jnp.sum(dl * ol, axis=-1, keepdims=True)
                delta_ref[a, :, hd * TQ:(hd + 1) * TQ] = jnp.broadcast_to(delta, (TQ, 128)).T[0:8, :]

        @pl.when(i == n - 1)
        def _():
            dwo_ref[...] = dwo_acc[...].astype(BF)

    rev = lambda c: pl.BlockSpec((ts, c), lambda i: (n - 1 - i, 0))
    rev3 = lambda r, c: pl.BlockSpec((nsub, r, c), lambda i: (n - 1 - i, 0, 0))
    out_shape = (
        jax.ShapeDtypeStruct((S, D), F32),
        jax.ShapeDtypeStruct((S, PW), F32),
        jax.ShapeDtypeStruct((S // TQ, M, KVL), BF),
        jax.ShapeDtypeStruct((S // TQ, 8, M), F32),
        jax.ShapeDtypeStruct((D, D), BF),
        jax.ShapeDtypeStruct((HEADS, KVL, 128), F32),
        jax.ShapeDtypeStruct((GROUPS, GD, GD), F32),
        jax.ShapeDtypeStruct((1, PW), F32),
        jax.ShapeDtypeStruct((1, D), F32), jax.ShapeDtypeStruct((1, D), F32), jax.ShapeDtypeStruct((1, D), F32),
        jax.ShapeDtypeStruct((1, D), F32),
    )
    in_specs = [rev(D), rev(D), rev(D), rev(D), _full(mod.shape), _full((1, D)), rev(PW), rev(PW), _full(w_o.shape),
                rev(PW), rev(PW), _full((1, PW)), _full(wpool_dc.shape), rev3(M, KVL), _full(wuv_vc.shape)]
    out_specs = (rev(D), rev(PW), rev3(M, KVL), rev3(8, M), _full((D, D)), _full((HEADS, KVL, 128)),
                 _full((GROUPS, GD, GD)), _full((1, PW)), _full((1, D)), _full((1, D)), _full((1, D)), _full((1, D)))
    return pl.pallas_call(
        body, name="mix_bwd", out_shape=out_shape, grid=(n,), in_specs=in_specs, out_specs=out_specs,
        scratch_shapes=[pltpu.VMEM((16, PW), F32), pltpu.VMEM((D, D), F32)],
        compiler_params=_params(("arbitrary",)),
    )(dh2, dx3, x2, mix, mod, g_ffn, ymla, ypool, w_o, ypre, pooled, pool_scale, wpool_dc, olat, wuv_vc)


def _attn_bwd(qs, kv, dolat, lse, delta):
    nq = qs.shape[0]
    S = kv.shape[0]
    M = HEADS * TQ
    nk = S // TK

    def body(qs_ref, kv_ref, do_ref, lse_ref, delta_ref, dkv_ref, dqt_ref):
        kt = pl.program_id(0)
        k = kv_ref[...]
        v = k[:, 0:KVL]
        k_t = k.astype(F32).T.astype(BF)

        @pl.when(kt == 0)
        def _():
            dqt_ref[...] = jnp.zeros_like(dqt_ref)

        def step(qi, carry, masked):
            dk, dv = carry
            q = qs_ref[qi]
            do = do_ref[qi]
            s = _dot_nt(k, q) * SM_SCALE
            p = jnp.exp(s - lse_ref[qi, 0:1, :])
            if masked:
                p = jnp.where(_diag_mask((TK, M), 1), p, 0.0)
            dp = _dot_nt(v, do)
            ds = (p * (dp - delta_ref[qi, 0:1, :]) * SM_SCALE).astype(BF)
            dv = dv + _dot(p.astype(BF), do)
            dk = dk + _dot(ds, q)
            dqt_ref[qi] += _dot(k_t, ds)
            return dk, dv

        carry = step(kt, (jnp.zeros((TK, QW), F32), jnp.zeros((TK, KVL), F32)), True)
        dk, dv = lax.fori_loop(kt + 1, nq, lambda qi, c: step(qi, c, False), carry)
        dkv_ref[...] = dk + jnp.concatenate([dv, jnp.zeros((TK, QW - KVL), F32)], axis=1)

    out_shape = (jax.ShapeDtypeStruct((S, QW), F32), jax.ShapeDtypeStruct((nq, QW, M), F32))
    return pl.pallas_call(
        body, name="attn_bwd", out_shape=out_shape, grid=(nk,),
        in_specs=[_vmem(), _rows(TK, QW), _vmem(), _vmem(), _vmem()],
        out_specs=(_rows(TK, QW), _vmem()),
        compiler_params=_params(("arbitrary",)),
    )(qs, kv, dolat, lse, delta)


def _in_bwd(dqt, dkv, du, raw, qn, h1, x, dx2, mod, g_mix, w_in, g_q, g_kv, w_uq, wuk_cd, perm_t, cos4, sin4, csk,
            snk):
    S = x.shape[0]
    ts = 512
    n = S // ts
    nsub = ts // TQ
    M = HEADS * TQ

    def body(dqt_ref, dkv_ref, du_ref, raw_ref, qn_ref, h1_ref, x_ref, dx2_ref, mod_ref, gmix_ref, win_ref, gq_ref,
             gkv_ref, wuq_ref, wuk_ref, permt_ref, cos_ref, sin_ref, csk_ref, snk_ref,
             dx_ref, dwin_ref, dwuq_ref, dwuk_ref, dgq_ref, dgkv_ref, dsc1_ref, dsh1_ref, dgmix_ref, dwin_acc,
             dwuq_acc):
        i = pl.program_id(0)

        @pl.when(i == 0)
        def _():
            dwin_acc[...] = jnp.zeros_like(dwin_acc)
            dwuq_acc[...] = jnp.zeros_like(dwuq_acc)
            for r in (dwuk_ref, dgq_ref, dgkv_ref, dsc1_ref, dsh1_ref, dgmix_ref):
                r[...] = jnp.zeros_like(r)

        dq_blocks = [dqt_ref[a].T for a in range(nsub)]
        qn = qn_ref[...]
        dq_parts = []
        drope = jnp.zeros((ts, 2 * 128), F32)
        for hd in range(HEADS):
            dqh = jnp.concatenate([blk[hd * TQ:(hd + 1) * TQ, :] for blk in dq_blocks], axis=0)
            dq_lat = dqh[:, 0:KVL].astype(BF)
            dq_parts.append(_dot(dq_lat, wuk_ref[hd]))
            dwuk_ref[hd] += _dot_tn(dq_lat, qn[:, hd * NOPE:(hd + 1) * NOPE])
            drope = drope + _dot(dqh[:, KVL:QW].astype(BF), permt_ref[hd])
        do1 = drope[:, 0:128]
        do2 = drope[:, 128:256]
        cosv = cos_ref[...]
        sinv = sin_ref[...]
        dq_parts.append(do1 * cosv + do2 * sinv)
        dq_parts.append(do2 * cosv - do1 * sinv)
        dq = jnp.concatenate(dq_parts, axis=1).astype(BF)

        cq_raw = raw_ref[:, 0:QL]
        ckv_raw = raw_ref[:, QL:QL + KVL]
        rq = _rms(cq_raw)
        nq_ = cq_raw * rq
        gq = gq_ref[...]
        dwuq_acc[...] += _dot_tn((nq_ * gq).astype(BF), dq)
        dc_q = _dot_nt(dq, wuq_ref[...])
        dgq_ref[...] += _colsum(dc_q * nq_)
        dcq_raw = _rms_bwd(dc_q * gq, nq_, rq)

        dkv = dkv_ref[...]
        rk = _rms(ckv_raw)
        nk_ = ckv_raw * rk
        dc_kv = dkv[:, 0:KVL]
        dgkv_ref[...] += _colsum(dc_kv * nk_)
        dckv_raw = _rms_bwd(dc_kv * gkv_ref[...], nk_, rk)
        dkr_roped = dkv[:, KVL:QW]
        dkr = dkr_roped * csk_ref[...] - _swap_halves(dkr_roped) * snk_ref[...]

        dproj = jnp.concatenate([dcq_raw, dckv_raw, dkr, du_ref[...]], axis=1).astype(BF)
        dwin_acc[...] += _dot_tn(h1_ref[...], dproj)
        dh1 = _dot_nt(dproj, win_ref[...])

        sc1 = mod_ref[0:1, D:2 * D]
        gmix = gmix_ref[...]
        xv = x_ref[...]
        r1 = _rms(xv)
        xn1 = xv * r1
        dsc1_ref[...] += _colsum(dh1 * (xn1 * gmix))
        dsh1_ref[...] += _colsum(dh1)
        dgmix_ref[...] += _colsum(dh1 * (1.0 + sc1) * xn1)
        dx_ref[...] = dx2_ref[...] + _rms_bwd(dh1 * gmix * (1.0 + sc1), xn1, r1)

        @pl.when(i == n - 1)
        def _():
            dwin_ref[...] = dwin_acc[...].astype(BF)
            dwuq_ref[...] = dwuq_acc[...].astype(BF)

    out_shape = (
        jax.ShapeDtypeStruct((S, D), F32),
        jax.ShapeDtypeStruct((D, D), BF),
        jax.ShapeDtypeStruct((QL, 768), BF),
        jax.ShapeDtypeStruct((HEADS, KVL, NOPE), F32),
        jax.ShapeDtypeStruct((1, QL), F32), jax.ShapeDtypeStruct((1, KVL), F32),
        jax.ShapeDtypeStruct((1, D), F32), jax.ShapeDtypeStruct((1, D), F32), jax.ShapeDtypeStruct((1, D), F32),
    )
    in_specs = [pl.BlockSpec((nsub, QW, M), lambda i: (i, 0, 0)), _rows(ts, QW), _rows(ts, PW), _rows(ts, 384),
                _rows(ts, HEADS * NOPE), _rows(ts, D), _rows(ts, D), _rows(ts, D), _full(mod.shape), _full((1, D)),
                _full(w_in.shape), _full((1, QL)), _full((1, KVL)), _full(w_uq.shape), _full(wuk_cd.shape),
                _full(perm_t.shape), _rows(ts, 128), _rows(ts, 128), _rows(ts, 128), _rows(ts, 128)]
    out_specs = (_rows(ts, D), _full((D, D)), _full((QL, 768)), _full((HEADS, KVL, NOPE)), _full((1, QL)),
                 _full((1, KVL)), _full((1, D)), _full((1, D)), _full((1, D)))
    return pl.pallas_call(
        body, name="in_bwd", out_shape=out_shape, grid=(n,), in_specs=in_specs, out_specs=out_specs,
        scratch_shapes=[pltpu.VMEM((D, D), F32), pltpu.VMEM((QL, 768), F32)],
        compiler_params=_params(("arbitrary",)),
    )(dqt, dkv, du, raw, qn, h1, x, dx2, mod, g_mix, w_in, g_q, g_kv, w_uq, wuk_cd, perm_t, cos4, sin4, csk, snk)


def _rope_perm():
    p = np.zeros((HEADS, 2 * 128, 128), np.float32)
    for hd in range(HEADS):
        for t in range(HALF):
            p[hd, hd * HALF + t, t] = 1.0
            p[hd, 128 + hd * HALF + t, HALF + t] = 1.0
    return p


def _rope_tables(positions):
    freqs = jnp.power(ROPE_THETA, -jnp.arange(HALF, dtype=F32) / HALF)
    ang = positions.astype(F32)[:, None] * freqs
    cos = jnp.cos(ang)
    sin = jnp.sin(ang)
    zero = jnp.zeros_like(cos)
    cos4 = jnp.tile(cos, (1, HEADS))
    sin4 = jnp.tile(sin, (1, HEADS))
    csk = jnp.concatenate([cos, cos, zero, zero], axis=1)
    snk = jnp.concatenate([-sin, sin, zero, zero], axis=1)
    return cos4, sin4, csk, snk


def _local_step(x, positions, target, mod, g_mix, w_in_p, g_q, g_kv, w_uq_p, w_uk, w_uv, w_pool, pool_scale, g_ffn,
                g_final, late_token, late_weights, grads_start):
    perm = jnp.asarray(_rope_perm(), BF)
    perm_t = jnp.asarray(_rope_perm().transpose(0, 2, 1), BF)
    cos4, sin4, csk, snk = _rope_tables(positions)
    wuk_dc = w_uk.transpose(1, 2, 0).astype(BF)
    wuk_cd = w_uk.transpose(1, 0, 2).astype(BF)
    wuv_cv = w_uv.transpose(1, 0, 2).astype(BF)
    wuv_vc = w_uv.transpose(1, 2, 0).astype(BF)
    wpool = w_pool.astype(BF)
    wpool_dc = w_pool.transpose(0, 2, 1).astype(BF)

    h1, raw, qn, qs, kv, pooled, ypre, ypool = _fwd_in(
        x, mod, g_mix + late_token, w_in_p, g_q, g_kv, w_uq_p, wuk_dc, perm, cos4, sin4, csk, snk, wpool, pool_scale)
    olat, ymla, lse = _attn_fwd(qs, kv, wuv_cv)
    w_o, wg_t, wu_t, wd = late_weights(ymla)
    x2, mix, h2, a, b, dx3, dff, loss, dgfin, dgt2 = _ffn_fwd(
        x, ymla, ypool, mod, w_o, g_ffn, wg_t, wu_t, wd, g_final, target)
    dwg_t, dwu_t, dwd, dh2 = _ffn_bwd(dff, h2, a, b, wg_t, wu_t, wd)
    ffn_parts = grads_start("scatter_ffn_start", (dwg_t, dwu_t, dwd))
    (dx2, du, dolat, delta, dwo, dwuv, dwpool, dpscale, dgt1, dsc2, dsh2, dgffn) = _mix_bwd(
        dh2, dx3, x2, mix, mod, g_ffn + ffn_parts[-1][0:1, 0:1], ymla, ypool, w_o, ypre, pooled, pool_scale, wpool_dc,
        olat, wuv_vc)
    wo_parts = grads_start("scatter_wo_start", (dwo,))
    dkv, dqt = _attn_bwd(qs, kv, dolat, lse, delta + wo_parts[-1][0:1, 0:1])
    dx, dwin, dwuq, dwuk, dgq, dgkv, dsc1, dsh1, dgmix = _in_bwd(
        dqt, dkv, du, raw, qn, h1, x, dx2, mod, g_mix, w_in_p, g_q, g_kv, w_uq_p, wuk_cd, perm_t, cos4, sin4, csk,
        snk)
    dmod = jnp.concatenate([dsh1, dsc1, dgt1, dsh2, dsc2, dgt2], axis=1)
    sharded = dict(w_in=dwin, w_uq=dwuq)
    replicated = dict(
        w_uk=dwuk.transpose(1, 0, 2), w_uv=dwuv.transpose(1, 0, 2), w_pool=dwpool, g_mix=dgmix, g_q=dgq, g_kv=dgkv,
        pool_scale=dpscale, g_ffn=dgffn, g_final=dgfin)
    return loss[0, 0], dx, dmod, sharded, replicated, ffn_parts, wo_parts


def _my_pos():
    return lax.axis_index("x"), lax.axis_index("y"), lax.axis_index("c")


def _peer(pos, k):
    x, y, c = pos
    return (1 - x if k & 4 else x, 1 - y if k & 2 else y, 1 - c if k & 1 else c)


def _index(pos):
    x, y, c = pos
    return 4 * x + 2 * y + c


def _remote(src, dst, send_sem, recv_sem, to):
    return pltpu.make_async_remote_copy(src_ref=src, dst_ref=dst, send_sem=send_sem, recv_sem=recv_sem,
                                        device_id=to, device_id_type=MESH)


def _ada_mod(c, w_ada, b_ada):
    def body(c_ref, w_ref, b_ref, mod_ref, call_ref, cbuf, sbuf, rbuf, send1, recv1, send2, recv2):
        me = _my_pos()
        mi = _index(me)
        cv = c_ref[...]
        cbuf[...] = jnp.broadcast_to(cv * jax.nn.sigmoid(cv), (8, D))
        call_ref[mi] = cbuf[...]
        first = [_remote(cbuf, call_ref.at[mi], send1.at[k - 1], recv1.at[k - 1], _peer(me, k)) for k in range(1, NDEV)]
        for cp in first:
            cp.start()
        for k in range(1, NDEV):
            _remote(cbuf, call_ref.at[_index(_peer(me, k))], send1.at[k - 1], recv1.at[k - 1], _peer(me, k)).wait_recv()
        c_all = jnp.concatenate([call_ref[b][0:1, :] for b in range(NDEV)], axis=0)
        blocks = _dot(c_all.astype(BF), w_ref[...].astype(BF))
        for b in range(NDEV):
            sbuf[b] = jnp.broadcast_to(blocks[b:b + 1, :], (8, MODC))
        second = []
        for k in range(1, NDEV):
            to = _peer(me, k)
            second.append(_remote(sbuf.at[_index(to)], rbuf.at[mi], send2.at[k - 1], recv2.at[k - 1], to))
        for cp in second:
            cp.start()
        rbuf[mi] = sbuf[mi]
        for k in range(1, NDEV):
            to = _peer(me, k)
            _remote(sbuf.at[_index(to)], rbuf.at[_index(to)], send2.at[k - 1], recv2.at[k - 1], to).wait_recv()
        for j in range(NDEV):
            mod_ref[:, j * MODC:(j + 1) * MODC] = rbuf[j] + b_ref[:, j * MODC:(j + 1) * MODC]
        for cp in first + second:
            cp.wait_send()

    return pl.pallas_call(
        body, name="ada_mod",
        out_shape=(jax.ShapeDtypeStruct((8, N_MOD * D), F32), jax.ShapeDtypeStruct((NDEV, 8, D), F32)),
        in_specs=[_vmem(), _vmem(), _vmem()], out_specs=(_vmem(), _vmem()),
        scratch_shapes=[pltpu.VMEM((8, D), F32), pltpu.VMEM((NDEV, 8, MODC), F32), pltpu.VMEM((NDEV, 8, MODC), F32),
                        pltpu.SemaphoreType.DMA((NDEV - 1,)), pltpu.SemaphoreType.DMA((NDEV - 1,)),
                        pltpu.SemaphoreType.DMA((NDEV - 1,)), pltpu.SemaphoreType.DMA((NDEV - 1,))],
        compiler_params=_params(),
    )(c, w_ada, b_ada)


def _all_gather(shards):
    n = len(shards)

    def body(*refs):
        ins, outs = refs[:n], refs[n:2 * n]
        send, recv, local = refs[2 * n:]
        me = _my_pos()
        mi = _index(me)
        own = [pltpu.make_async_copy(ins[a], outs[a].at[mi], local.at[a]) for a in range(n)]
        for cp in own:
            cp.start()
        sent = []
        for a in range(n):
            for k in range(1, NDEV):
                sent.append(_remote(ins[a], outs[a].at[mi], send.at[a, k - 1], recv.at[a, k - 1], _peer(me, k)))
        for cp in sent:
            cp.start()
        for a in range(n):
            for k in range(1, NDEV):
                to = _peer(me, k)
                _remote(ins[a], outs[a].at[_index(to)], send.at[a, k - 1], recv.at[a, k - 1], to).wait_recv()
        for cp in sent:
            cp.wait_send()
        for cp in own:
            cp.wait()

    return pl.pallas_call(
        body, name="gather_weights",
        out_shape=tuple(jax.ShapeDtypeStruct((NDEV,) + s.shape, s.dtype) for s in shards),
        in_specs=[_any()] * n, out_specs=tuple([_any()] * n),
        scratch_shapes=[pltpu.SemaphoreType.DMA((n, NDEV - 1)), pltpu.SemaphoreType.DMA((n, NDEV - 1)),
                        pltpu.SemaphoreType.DMA((n,))],
        compiler_params=_params(),
    )(*shards)


def _scatter_partials(grads):
    n = len(grads)

    def body(*refs):
        ins, outs = refs[:n], refs[n:2 * n]
        send, recv, local = refs[2 * n:]
        me = _my_pos()
        mi = _index(me)

        def rows_of(a, dev_index):
            r = ins[a].shape[0] // NDEV
            return ins[a].at[pl.ds(pl.multiple_of(dev_index * r, 16), r), :]

        own = [pltpu.make_async_copy(rows_of(a, mi), outs[a].at[mi], local.at[a]) for a in range(n)]
        for cp in own:
            cp.start()
        sent = []
        for a in range(n):
            for k in range(1, NDEV):
                to = _peer(me, k)
                sent.append(_remote(rows_of(a, _index(to)), outs[a].at[mi], send.at[a, k - 1], recv.at[a, k - 1], to))
        for cp in sent:
            cp.start()
        for a in range(n):
            for k in range(1, NDEV):
                to = _peer(me, k)
                _remote(rows_of(a, mi), outs[a].at[_index(to)], send.at[a, k - 1], recv.at[a, k - 1], to).wait_recv()
        for cp in sent:
            cp.wait_send()
        for cp in own:
            cp.wait()

    return pl.pallas_call(
        body, name="scatter_grads",
        out_shape=tuple(jax.ShapeDtypeStruct((NDEV, g.shape[0] // NDEV, g.shape[1]), g.dtype) for g in grads),
        in_specs=[_any()] * n, out_specs=tuple([_any()] * n),
        scratch_shapes=[pltpu.SemaphoreType.DMA((n, NDEV - 1)), pltpu.SemaphoreType.DMA((n, NDEV - 1)),
                        pltpu.SemaphoreType.DMA((n,))],
        compiler_params=_params(),
    )(*grads)


_HBM = pl.BlockSpec(memory_space=pltpu.HBM)
_SEM = pl.BlockSpec(memory_space=pltpu.SEMAPHORE)
_EFFECT = pltpu.SideEffectType.DATAFLOW_SIDE_EFFECTING


def _exchange_slices(scatter):
    def of(src, to_index):
        if not scatter:
            return src
        r = src.shape[0] // NDEV
        return src.at[pl.ds(pl.multiple_of(to_index * r, 16), r), :]
    return of


def _exchange_start(name, srcs, scatter, after):
    n = len(srcs)
    of = _exchange_slices(scatter)
    zones = [lax.empty((NDEV, s.shape[0] // NDEV if scatter else s.shape[0], s.shape[1]), s.dtype) for s in srcs]

    def body(*refs):
        src, land = refs[:n], refs[n:2 * n]
        send, recv = refs[2 * n + 1], refs[2 * n + 2]
        token, local = refs[4 * n + 3], refs[4 * n + 4]
        me = _my_pos()
        mi = _index(me)
        own = [pltpu.make_async_copy(of(src[a], mi), land[a].at[mi], local.at[a]) for a in range(n)]
        for cp in own:
            cp.start()
        for cp in own:
            cp.wait()
        for a in range(n):
            for k in range(1, NDEV):
                to = _peer(me, k)
                s = a * (NDEV - 1) + k - 1
                _remote(of(src[a], _index(to)), land[a].at[mi], send.at[s], recv.at[s], to).start()
        token[...] = jnp.zeros_like(token)

    sems = pltpu.SemaphoreType.DMA((n * (NDEV - 1),))
    return pl.pallas_call(
        body, name=name,
        out_shape=(sems, sems, *[pltpu.HBM(s.shape, s.dtype) for s in srcs],
                   *[pltpu.HBM(z.shape, z.dtype) for z in zones], jax.ShapeDtypeStruct((8, 128), F32)),
        in_specs=[_HBM] * (2 * n) + [_any()], out_specs=(_SEM, _SEM, *[_HBM] * (2 * n), _vmem()),
        input_output_aliases={i: 2 + i for i in range(2 * n)},
        scratch_shapes=[pltpu.SemaphoreType.DMA((n,))],
        compiler_params=pltpu.CompilerParams(has_side_effects=_EFFECT),
    )(*[pltpu.with_memory_space_constraint(s, pltpu.HBM) for s in srcs],
      *[pltpu.with_memory_space_constraint(z, pltpu.HBM) for z in zones], after)


def _exchange_wait(name, started, scatter, after):
    send, recv, *bufs, _ = started
    n = len(bufs) // 2
    of = _exchange_slices(scatter)

    def body(*refs):
        src, land = refs[:n], refs[n:2 * n]
        send_ref, recv_ref = refs[2 * n], refs[2 * n + 1]
        me = _my_pos()
        mi = _index(me)
        for a in range(n):
            for k in range(1, NDEV):
                to = _peer(me, k)
                s = a * (NDEV - 1) + k - 1
                cp = _remote(of(src[a], mi), land[a].at[_index(to)], send_ref.at[s], recv_ref.at[s], to)
                cp.wait_send()
                cp.wait_recv()

    outs = pl.pallas_call(
        body, name=name, out_shape=tuple(pltpu.HBM(b.shape, b.dtype) for b in bufs),
        in_specs=[_HBM] * (2 * n) + [_SEM, _SEM, _any()], out_specs=tuple([_HBM] * (2 * n)),
        input_output_aliases={i: i for i in range(2 * n)},
        compiler_params=pltpu.CompilerParams(has_side_effects=_EFFECT),
    )(*bufs, send, recv, after)
    return outs[n:]


def _sum_partials(parts):
    n = len(parts)

    def body(*refs):
        for a in range(n):
            acc = refs[a][0].astype(F32)
            for p in range(1, NDEV):
                acc = acc + refs[a][p].astype(F32)
            refs[n + a][...] = acc

    return pl.pallas_call(
        body, name="sum_partials",
        out_shape=tuple(jax.ShapeDtypeStruct(p.shape[1:], F32) for p in parts),
        in_specs=[_vmem()] * n, out_specs=tuple([_vmem()] * n), compiler_params=_params(),
    )(*parts)


def _small_all_reduce(buf):
    def body(buf_ref, got_ref, red_ref, mine, send1, recv1, send2, recv2):
        me = _my_pos()
        mi = _index(me)
        first = []
        for k in range(1, NDEV):
            to = _peer(me, k)
            first.append(_remote(buf_ref.at[_index(to)], got_ref.at[mi], send1.at[k - 1], recv1.at[k - 1], to))
        for cp in first:
            cp.start()
        got_ref[mi] = buf_ref[mi]
        for k in range(1, NDEV):
            to = _peer(me, k)
            _remote(buf_ref.at[mi], got_ref.at[_index(to)], send1.at[k - 1], recv1.at[k - 1], to).wait_recv()
        acc = got_ref[0]
        for p in range(1, NDEV):
            acc = acc + got_ref[p]
        mine[...] = acc
        second = [_remote(mine, red_ref.at[mi], send2.at[k - 1], recv2.at[k - 1], _peer(me, k)) for k in range(1, NDEV)]
        for cp in second:
            cp.start()
        red_ref[mi] = acc
        for k in range(1, NDEV):
            to = _peer(me, k)
            _remote(mine, red_ref.at[_index(to)], send2.at[k - 1], recv2.at[k - 1], to).wait_recv()
        for cp in first + second:
            cp.wait_send()

    return pl.pallas_call(
        body, name="small_all_reduce",
        out_shape=(jax.ShapeDtypeStruct(buf.shape, F32), jax.ShapeDtypeStruct(buf.shape, F32)),
        in_specs=[_vmem()], out_specs=(_vmem(), _vmem()),
        scratch_shapes=[pltpu.VMEM(buf.shape[1:], F32),
                        pltpu.SemaphoreType.DMA((NDEV - 1,)), pltpu.SemaphoreType.DMA((NDEV - 1,)),
                        pltpu.SemaphoreType.DMA((NDEV - 1,)), pltpu.SemaphoreType.DMA((NDEV - 1,))],
        compiler_params=_params(),
    )(buf)


def _adamw_math(w, g, m, v):
    m = ADAM_B1 * m + (1.0 - ADAM_B1) * g
    v = ADAM_B2 * v + (1.0 - ADAM_B2) * jnp.square(g)
    m_hat = m / (1.0 - ADAM_B1 ** ADAM_STEP)
    v_hat = v / (1.0 - ADAM_B2 ** ADAM_STEP)
    delta = -ADAM_LR * (m_hat / (jnp.sqrt(v_hat) + ADAM_EPS) + ADAM_WD * w)
    return delta, m, v


def _adamw_group(name, ws, gs, ms, vs):
    n = len(ws)

    def body(*refs):
        for a in range(n):
            w, g, m, v = (refs[q * n + a][...] for q in range(4))
            delta, m2, v2 = _adamw_math(w, g, m, v)
            refs[4 * n + a][...] = delta
            refs[5 * n + a][...] = m2
            refs[6 * n + a][...] = v2

    shapes = tuple(jax.ShapeDtypeStruct(w.shape, F32) for w in ws)
    outs = pl.pallas_call(
        body, name=name, out_shape=shapes * 3, in_specs=[_vmem()] * (4 * n), out_specs=tuple([_vmem()] * (3 * n)),
        compiler_params=_params(),
    )(*ws, *gs, *ms, *vs)
    return outs[:n], outs[n:2 * n], outs[2 * n:]


def _adamw_ada(w, m, v, c_all, dmod_rows):
    def body(w_ref, m_ref, v_ref, c_ref, dm_ref, g_ref, d_ref, m2_ref, v2_ref):
        g = _dot_tn(c_ref[...], dm_ref[...].astype(BF))
        g_ref[...] = g
        delta, m2, v2 = _adamw_math(w_ref[...], g, m_ref[...], v_ref[...])
        d_ref[...] = delta
        m2_ref[...] = m2
        v2_ref[...] = v2

    shp = jax.ShapeDtypeStruct(w.shape, F32)
    return pl.pallas_call(
        body, name="adamw_ada", out_shape=(shp, shp, shp, shp), in_specs=[_vmem()] * 5,
        out_specs=tuple([_vmem()] * 4), compiler_params=_params(),
    )(w, m, v, c_all, dmod_rows)


def _w_in_to_kernel(w):
    return jnp.concatenate([w[:, 0:448], jnp.zeros((w.shape[0], 64), w.dtype), w[:, 448:960]], axis=1)


def _w_in_from_kernel(w):
    return jnp.concatenate([w[:, 0:448], w[:, 512:1024]], axis=1)


def _w_uq_to_kernel(w):
    r = w.shape[0]
    return jnp.concatenate([w[:, :, 0:NOPE].reshape(r, HEADS * NOPE),
                            w[:, :, NOPE:NOPE + HALF].reshape(r, HEADS * HALF),
                            w[:, :, NOPE + HALF:].reshape(r, HEADS * HALF)], axis=1)


def _w_uq_from_kernel(w):
    r = w.shape[0]
    return jnp.concatenate([w[:, 0:512].reshape(r, HEADS, NOPE), w[:, 512:640].reshape(r, HEADS, HALF),
                            w[:, 640:768].reshape(r, HEADS, HALF)], axis=2)


REP_NAMES = ("w_uk", "w_uv", "w_pool", "g_mix", "g_q", "g_kv", "pool_scale", "g_ffn", "g_final")


def kernel(x, c, positions, w_ada, b_ada, g_mix, w_in, g_q, g_kv, w_uq, w_uk, w_uv, w_pool, pool_scale, w_o, g_ffn, w_gate, w_up, w_down, g_final, loss_target, m_w_ada, m_b_ada, m_g_mix, m_w_in, m_g_q, m_g_kv, m_w_uq, m_w_uk, m_w_uv, m_w_pool, m_pool_scale, m_w_o, m_g_ffn, m_w_gate, m_w_up, m_w_down, m_g_final, v_w_ada, v_b_ada, v_g_mix, v_w_in, v_g_q, v_g_kv, v_w_uq, v_w_uk, v_w_uv, v_w_pool, v_pool_scale, v_w_o, v_g_ffn, v_w_gate, v_w_up, v_w_down, v_g_final):
    given = dict(locals())

    merge = lambda g: g.reshape(NDEV * g.shape[1], g.shape[2])
    w_in_p, w_uq_p = (merge(g) for g in _all_gather(
        (_w_in_to_kernel(w_in[0]).astype(BF), _w_uq_to_kernel(w_uq[0]).astype(BF))))

    mod, c_all8 = _ada_mod(c, w_ada[0], b_ada)
    c_all = c_all8[:, 0, :]
    late = _exchange_start("gather_start", (w_o[0].astype(BF), w_gate[0].T.astype(BF), w_up[0].T.astype(BF),
                                            w_down[0].astype(BF)), False, mod)

    def late_weights(after):
        return tuple(merge(g) for g in _exchange_wait("gather_wait", late, False, after))

    def grads_start(name, arrays):
        return _exchange_start(name, arrays, True, mod)

    loss, dx, dmod, sharded, replicated, ffn_parts, wo_parts = _local_step(
        x[0], positions[0], loss_target[0], mod, g_mix, w_in_p, g_q, g_kv, w_uq_p, w_uk[0], w_uv[0], w_pool[0],
        pool_scale, g_ffn, g_final.reshape(1, D), late[-1][0:1, 0:1], late_weights, grads_start)

    parts = _scatter_partials(tuple(sharded[k] for k in ("w_in", "w_uq")))
    (g_o_p,) = _exchange_wait("scatter_wo_wait", wo_parts, True, parts[0])
    g_gate_p, g_up_p, g_down_p = _exchange_wait("scatter_ffn_wait", ffn_parts, True, g_o_p)
    g_in_p, g_uq_p, g_o, g_gate_t, g_up_t, g_down = _sum_partials((*parts, g_o_p, g_gate_p, g_up_p, g_down_p))
    grads = dict(w_in=_w_in_from_kernel(g_in_p), w_uq=_w_uq_from_kernel(g_uq_p).reshape(QL // NDEV, HEADS * 192),
                 w_o=g_o, w_gate=g_gate_t.T, w_up=g_up_t.T, w_down=g_down)

    flat = jnp.concatenate([replicated[k].reshape(-1) for k in REP_NAMES])
    flat = jnp.pad(flat, (0, NDEV * REP_ROWS * 128 - flat.shape[0])).reshape(NDEV, REP_ROWS, 128)
    dmod_blocks = jnp.pad(dmod.reshape(NDEV, MODC // 128, 128), ((0, 0), (0, MOD_ROWS - MODC // 128), (0, 0)))
    got, red = _small_all_reduce(jnp.concatenate([dmod_blocks, flat], axis=1))
    dmod_rows = got[:, 0:MODC // 128, :].reshape(NDEV, MODC)
    grads["b_ada"] = red[:, 0:MODC // 128, :].reshape(1, N_MOD * D)
    rep_flat = red[:, MOD_ROWS:, :].reshape(-1)
    off = 0
    for k in REP_NAMES:
        size = int(np.prod(given[k].shape))
        grads[k] = rep_flat[off:off + size]
        off += size

    view = dict(w_ada=(D, MODC), b_ada=(1, N_MOD * D), g_mix=(1, D), w_in=(D // NDEV, 960), g_q=(1, QL),
                g_kv=(1, KVL), w_uq=(QL // NDEV, HEADS * 192), w_uk=(KVL, HEADS * NOPE), w_uv=(KVL, HEADS * 128),
                w_pool=(GROUPS * GD, GD), pool_scale=(1, PW), w_o=(D // NDEV, D), g_ffn=(1, D),
                w_gate=(D, FF // NDEV), w_up=(D, FF // NDEV), w_down=(FF // NDEV, D), g_final=(1, D))
    names = list(view)
    g_ada, d_ada, m_ada, v_ada = _adamw_ada(w_ada[0], m_w_ada[0], v_w_ada[0], c_all.astype(BF), dmod_rows)
    out_g, out_d, out_m, out_v = dict(w_ada=g_ada), dict(w_ada=d_ada), dict(w_ada=m_ada), dict(w_ada=v_ada)
    groups = (("adamw_ffn", ("w_gate", "w_up", "w_down")),
              ("adamw_rest", tuple(k for k in names if k not in ("w_ada", "w_gate", "w_up", "w_down"))))
    for gname, members in groups:
        ws = [given[k].reshape(view[k]) for k in members]
        gs = [grads[k].reshape(view[k]) for k in members]
        ms = [given["m_" + k].reshape(view[k]) for k in members]
        vs = [given["v_" + k].reshape(view[k]) for k in members]
        ds, m2, v2 = _adamw_group(gname, ws, gs, ms, vs)
        for k, g, d, mm, vv in zip(members, gs, ds, m2, v2):
            out_g[k], out_d[k], out_m[k], out_v[k] = g, d, mm, vv

    total = lax.psum(loss, ("x", "y", "c"))
    shaped = lambda d: [d[k].reshape(given[k].shape) for k in names]
    return (total, dx[None], *shaped(out_g), *shaped(out_d), *shaped(out_m), *shaped(out_v))
```

```python
import functools

import numpy as np
import jax
import jax.numpy as jnp
from jax import lax
from jax.experimental import pallas as pl
from jax.experimental.pallas import tpu as pltpu

D = 1024
HEADS = 4
NOPE = 128
ROPE = 64
HALF = ROPE // 2
QL = 256
KVL = 128
FF = 2816
PW = 512
GROUPS = 4
GD = 128
N_MOD = 6
EPS = 1e-6
SM_SCALE = (NOPE + ROPE) ** -0.5
ROPE_THETA = 10000.0
NDEV = 8
MODC = N_MOD * D // NDEV

ADAM_LR = 0.001
ADAM_B1 = 0.9
ADAM_B2 = 0.999
ADAM_EPS = 1e-08
ADAM_WD = 0.01
ADAM_STEP = 10

BF = jnp.bfloat16
F32 = jnp.float32
VMEM_LIMIT_V7X = 60 * 1024 * 1024
MESH = pl.DeviceIdType.MESH

TQ = 256
TK = 256
QW = 256
MOD_ROWS = 8
REP_ROWS = 200
SMALL_ROWS = MOD_ROWS + REP_ROWS


def _params(sem=None):
    return pltpu.CompilerParams(dimension_semantics=sem, vmem_limit_bytes=VMEM_LIMIT_V7X)


def _dot(a, b):
    return jnp.dot(a, b, preferred_element_type=F32)


def _dot_nt(a, b):
    return lax.dot_general(a, b, (((1,), (1,)), ((), ())), preferred_element_type=F32)


def _dot_tn(a, b):
    return _dot(a.astype(F32).T.astype(BF), b)


def _full(shape):
    return pl.BlockSpec(shape, lambda *_: (0,) * len(shape))


def _rows(ts, cols):
    return pl.BlockSpec((ts, cols), lambda i: (i, 0))


def _vmem():
    return pl.BlockSpec(memory_space=pltpu.VMEM)


def _any():
    return pl.BlockSpec(memory_space=pl.ANY)


def _rms(v):
    return lax.rsqrt(jnp.mean(v * v, axis=-1, keepdims=True) + EPS)


def _rms_bwd(dn, n, r):
    return r * (dn - n * jnp.mean(dn * n, axis=-1, keepdims=True))


def _colsum(v):
    return jnp.sum(v, axis=0, keepdims=True)


def _swap_halves(v):
    lane = lax.broadcasted_iota(jnp.int32, v.shape, 1)
    return jnp.where(lane < HALF, pltpu.roll(v, 128 - HALF, 1), pltpu.roll(v, HALF, 1))


def _window_lane_width():
    lane = lax.broadcasted_iota(jnp.int32, (1, PW), 1)
    return jnp.where(lane < 128, 2.0, jnp.where(lane < 256, 4.0, jnp.where(lane < 384, 8.0, 16.0))).astype(F32)


def _window_sums(ext, back):
    n = ext.shape[0]

    def sh(v, k):
        return pltpu.roll(v, k if back else n - k, 0)

    s2 = ext + sh(ext, 1)
    e4 = s2[:, 128:]
    s4 = e4 + sh(e4, 2)
    e8 = s4[:, 128:]
    s8 = e8 + sh(e8, 4)
    e16 = s8[:, 128:]
    s16 = e16 + sh(e16, 8)
    return jnp.concatenate([s2[:, :128], s4[:, :128], s8[:, :128], s16], axis=1)


def _row_counts(first_row, ts):
    t1 = (first_row + lax.broadcasted_iota(jnp.int32, (ts, 1), 0) + 1).astype(F32)
    return jnp.minimum(t1, _window_lane_width())


def _fwd_in(x, mod, g_mix, w_in, g_q, g_kv, w_uq, wuk_dc, perm, cos4, sin4, csk, snk, w_pool, pool_scale):
    S = x.shape[0]
    ts = 512
    nsub = ts // TQ

    def body(x_ref, mod_ref, gmix_ref, win_ref, gq_ref, gkv_ref, wuq_ref, wuk_ref, perm_ref, cos_ref, sin_ref,
             csk_ref, snk_ref, wpool_ref, pscale_ref,
             h1_ref, raw_ref, qn_ref, qs_ref, kv_ref, pooled_ref, ypre_ref, ypool_ref, carry_ref):
        i = pl.program_id(0)

        @pl.when(i == 0)
        def _():
            carry_ref[...] = jnp.zeros_like(carry_ref)

        xv = x_ref[...]
        sh1 = mod_ref[0:1, 0:D]
        sc1 = mod_ref[0:1, D:2 * D]
        h = (xv * _rms(xv)) * gmix_ref[...] * (1.0 + sc1) + sh1
        hb = h.astype(BF)
        h1_ref[...] = hb
        proj = _dot(hb, win_ref[...])
        cq_raw = proj[:, 0:QL]
        ckv_raw = proj[:, QL:QL + KVL]
        kr = proj[:, 384:512]
        u = proj[:, 512:1024]
        raw_ref[...] = proj[:, 0:384]

        c_q = (cq_raw * _rms(cq_raw)) * gq_ref[...]
        c_kv = (ckv_raw * _rms(ckv_raw)) * gkv_ref[...]
        q = _dot(c_q.astype(BF), wuq_ref[...])
        qn = q[:, 0:HEADS * NOPE].astype(BF)
        qn_ref[...] = qn
        x1 = q[:, 512:640]
        x2 = q[:, 640:768]
        cosv = cos_ref[...]
        sinv = sin_ref[...]
        roped = jnp.concatenate([x1 * cosv - x2 * sinv, x1 * sinv + x2 * cosv], axis=1).astype(BF)
        for hd in range(HEADS):
            q_lat = _dot(qn[:, hd * NOPE:(hd + 1) * NOPE], wuk_ref[hd])
            q_rope = _dot(roped, perm_ref[hd])
            qh = jnp.concatenate([q_lat, q_rope], axis=1).astype(BF)
            for a in range(nsub):
                qs_ref[a, hd * TQ:(hd + 1) * TQ, :] = qh[a * TQ:(a + 1) * TQ, :]
        k_rope = kr * csk_ref[...] + _swap_halves(kr) * snk_ref[...]
        kv_ref[...] = jnp.concatenate([c_kv, k_rope], axis=1).astype(BF)

        ext = jnp.concatenate([carry_ref[...], u], axis=0)
        win = _window_sums(ext, True)[16:, :]
        pooled = (win / _row_counts(i * ts, ts) - u).astype(BF)
        pooled_ref[...] = pooled
        carry_ref[...] = u[ts - 16:ts, :]
        ypre = jnp.concatenate(
            [_dot(pooled[:, g * GD:(g + 1) * GD], wpool_ref[g]) for g in range(GROUPS)], axis=1)
        ypre_ref[...] = ypre
        ypool_ref[...] = (ypre * pscale_ref[...]).astype(BF)

    out_shape = (
        jax.ShapeDtypeStruct((S, D), BF),
        jax.ShapeDtypeStruct((S, 384), F32),
        jax.ShapeDtypeStruct((S, HEADS * NOPE), BF),
        jax.ShapeDtypeStruct((S // TQ, HEADS * TQ, QW), BF),
        jax.ShapeDtypeStruct((S, QW), BF),
        jax.ShapeDtypeStruct((S, PW), BF),
        jax.ShapeDtypeStruct((S, PW), F32),
        jax.ShapeDtypeStruct((S, PW), BF),
    )
    in_specs = [
        _rows(ts, D), _full(mod.shape), _full((1, D)), _full(w_in.shape), _full((1, QL)), _full((1, KVL)),
        _full(w_uq.shape), _full(wuk_dc.shape), _full(perm.shape), _rows(ts, 128), _rows(ts, 128), _rows(ts, 128),
        _rows(ts, 128), _full(w_pool.shape), _full((1, PW)),
    ]
    out_specs = (
        _rows(ts, D), _rows(ts, 384), _rows(ts, HEADS * NOPE),
        pl.BlockSpec((nsub, HEADS * TQ, QW), lambda i: (i, 0, 0)),
        _rows(ts, QW), _rows(ts, PW), _rows(ts, PW), _rows(ts, PW),
    )
    return pl.pallas_call(
        body, name="fwd_in", out_shape=out_shape, grid=(S // ts,), in_specs=in_specs, out_specs=out_specs,
        scratch_shapes=[pltpu.VMEM((16, PW), F32)], compiler_params=_params(("arbitrary",)),
    )(x, mod, g_mix, w_in, g_q, g_kv, w_uq, wuk_dc, perm, cos4, sin4, csk, snk, w_pool, pool_scale)


def _diag_mask(shape, q_axis):
    qi = (lax.broadcasted_iota(jnp.int32, shape, q_axis) & (TQ - 1)) >> 6
    ki = lax.broadcasted_iota(jnp.int32, shape, 1 - q_axis) >> 6
    return ki <= qi


def _attn_fwd(qs, kv, wuv_cv):
    nq = qs.shape[0]
    S = kv.shape[0]
    M = HEADS * TQ

    def body(qs_ref, kv_ref, wuv_ref, olat_ref, ymla_ref, lse_ref):
        i = pl.program_id(0)
        q = qs_ref[0]

        def step(kt, carry, masked):
            m, l, acc = carry
            k = kv_ref[pl.ds(pl.multiple_of(kt * TK, TK), TK), :]
            s = _dot_nt(q, k) * SM_SCALE
            if masked:
                s = jnp.where(_diag_mask((M, TK), 0), s, -jnp.inf)
            m_new = jnp.maximum(m, jnp.max(s, axis=-1, keepdims=True))
            alpha = jnp.exp(m - m_new)
            p = jnp.exp(s - m_new)
            l = alpha * l + jnp.sum(p, axis=-1, keepdims=True)
            acc = alpha * acc + _dot(p.astype(BF), k[:, 0:KVL])
            return m_new, l, acc

        init = (jnp.full((M, 1), -jnp.inf, F32), jnp.zeros((M, 1), F32), jnp.zeros((M, KVL), F32))
        carry = lax.fori_loop(0, i, lambda kt, c: step(kt, c, False), init)
        m, l, acc = step(i, carry, True)
        o_lat = acc / l
        olat_ref[0] = o_lat
        lse = m + jnp.log(l)
        lse_ref[0] = jnp.broadcast_to(lse, (M, 128)).T[0:8, :]
        for hd in range(HEADS):
            o = _dot(o_lat[hd * TQ:(hd + 1) * TQ, :].astype(BF), wuv_ref[hd])
            ymla_ref[:, hd * 128:(hd + 1) * 128] = o.astype(BF)

    out_shape = (
        jax.ShapeDtypeStruct((nq, M, KVL), F32),
        jax.ShapeDtypeStruct((S, HEADS * 128), BF),
        jax.ShapeDtypeStruct((nq, 8, M), F32),
    )
    return pl.pallas_call(
        body, name="attn_fwd", out_shape=out_shape, grid=(nq,),
        in_specs=[pl.BlockSpec((1, M, QW), lambda i: (i, 0, 0)), _full(kv.shape), _full(wuv_cv.shape)],
        out_specs=(pl.BlockSpec((1, M, KVL), lambda i: (i, 0, 0)), _rows(TQ, HEADS * 128),
                   pl.BlockSpec((1, 8, M), lambda i: (i, 0, 0))),
        compiler_params=_params(("arbitrary",)),
    )(qs, kv, wuv_cv)


def _silu_parts(a):
    sg = jax.nn.sigmoid(a)
    return sg, a * sg


def _ffn_fwd(x, ymla, ypool, mod, w_o, g_ffn, wg_t, wu_t, wd, g_final, target):
    S = x.shape[0]
    ts = 512
    tf = 256
    nj = FF // tf

    def body(x_ref, ymla_ref, ypool_ref, mod_ref, wo_ref, gffn_ref, wg_ref, wu_ref, wd_ref, gfin_ref, t_ref,
             x2_ref, mix_ref, h2_ref, a_ref, b_ref, dx3_ref, dff_ref, loss_ref, dgfin_ref, dgt2_ref, acc_ref):
        i = pl.program_id(0)
        j = pl.program_id(1)

        @pl.when(jnp.logical_and(i == 0, j == 0))
        def _():
            loss_ref[...] = jnp.zeros_like(loss_ref)
            dgfin_ref[...] = jnp.zeros_like(dgfin_ref)
            dgt2_ref[...] = jnp.zeros_like(dgt2_ref)

        @pl.when(j == 0)
        def _():
            gt1 = mod_ref[0:1, 2 * D:3 * D]
            sh2 = mod_ref[0:1, 3 * D:4 * D]
            sc2 = mod_ref[0:1, 4 * D:5 * D]
            cat = jnp.concatenate([ymla_ref[...], ypool_ref[...]], axis=1)
            mix = _dot(cat, wo_ref[...])
            mix_ref[...] = mix
            x2 = x_ref[...] + gt1 * mix
            x2_ref[...] = x2
            h2 = (x2 * _rms(x2)) * gffn_ref[...] * (1.0 + sc2) + sh2
            h2_ref[...] = h2.astype(BF)
            acc_ref[...] = jnp.zeros_like(acc_ref)

        h2b = h2_ref[...]
        a = _dot_nt(h2b, wg_ref[...])
        b = _dot_nt(h2b, wu_ref[...])
        a_ref[...] = a.astype(BF)
        b_ref[...] = b.astype(BF)
        f = _silu_parts(a)[1] * b
        acc_ref[...] += _dot(f.astype(BF), wd_ref[...])

        @pl.when(j == nj - 1)
        def _():
            gt2 = mod_ref[0:1, 5 * D:6 * D]
            ff = acc_ref[...]
            x3 = x2_ref[...] + gt2 * ff
            r3 = _rms(x3)
            xn3 = x3 * r3
            gfin = gfin_ref[...]
            e = xn3 * gfin - t_ref[...]
            loss_ref[...] += 0.5 * jnp.sum(jnp.mean(e * e, axis=-1, keepdims=True))
            dy = e * (1.0 / D)
            dgfin_ref[...] += _colsum(dy * xn3)
            dx3 = _rms_bwd(dy * gfin, xn3, r3)
            dx3_ref[...] = dx3
            dgt2_ref[...] += _colsum(dx3 * ff)
            dff_ref[...] = (dx3 * gt2).astype(BF)

    row = lambda c: pl.BlockSpec((ts, c), lambda i, j: (i, 0))
    wblk = pl.BlockSpec((tf, D), lambda i, j: (j, 0))
    act = pl.BlockSpec((ts, tf), lambda i, j: (i, j))
    const = lambda shape: pl.BlockSpec(shape, lambda i, j: (0,) * len(shape))
    out_shape = (
        jax.ShapeDtypeStruct((S, D), F32),
        jax.ShapeDtypeStruct((S, D), F32),
        jax.ShapeDtypeStruct((S, D), BF),
        jax.ShapeDtypeStruct((S, FF), BF),
        jax.ShapeDtypeStruct((S, FF), BF),
        jax.ShapeDtypeStruct((S, D), F32),
        jax.ShapeDtypeStruct((S, D), BF),
        jax.ShapeDtypeStruct((8, 128), F32),
        jax.ShapeDtypeStruct((1, D), F32),
        jax.ShapeDtypeStruct((1, D), F32),
    )
    return pl.pallas_call(
        body, name="ffn_fwd", out_shape=out_shape, grid=(S // ts, nj),
        in_specs=[row(D), row(PW), row(PW), const(mod.shape), const(w_o.shape), const((1, D)), wblk, wblk, wblk,
                  const((1, D)), row(D)],
        out_specs=(row(D), row(D), row(D), act, act, row(D), row(D), const((8, 128)), const((1, D)), const((1, D))),
        scratch_shapes=[pltpu.VMEM((ts, D), F32)],
        compiler_params=_params(("arbitrary", "arbitrary")),
    )(x, ymla, ypool, mod, w_o, g_ffn, wg_t, wu_t, wd, g_final, target)


def _ffn_bwd(dff, h2, a, b, wg_t, wu_t, wd):
    S = dff.shape[0]
    ts = 1024
    tf = 256
    ni = S // ts
    nj = FF // tf

    def body(dff_ref, h2_ref, a_ref, b_ref, wg_ref, wu_ref, wd_ref,
             dwg_ref, dwu_ref, dwd_ref, dh2_ref, gacc, uacc, dacc, dh2acc):
        j = pl.program_id(0)
        i = pl.program_id(1)
        dffb = dff_ref[...]
        h2b = h2_ref[...]
        av = a_ref[...].astype(F32)
        bv = b_ref[...].astype(F32)
        df = _dot_nt(dffb, wd_ref[...])
        sg, sa = _silu_parts(av)
        f = sa * bv
        db = df * sa
        da = df * bv * (sg * (1.0 + av * (1.0 - sg)))
        dab = da.astype(BF)
        dbb = db.astype(BF)

        @pl.when(i == 0)
        def _():
            gacc[...] = jnp.zeros_like(gacc)
            uacc[...] = jnp.zeros_like(uacc)
            dacc[...] = jnp.zeros_like(dacc)

        gacc[...] += _dot_tn(da, h2b)
        uacc[...] += _dot_tn(db, h2b)
        dacc[...] += _dot_tn(f, dffb)
        contrib = _dot(dab, wg_ref[...]) + _dot(dbb, wu_ref[...])
        rows = pl.ds(pl.multiple_of(i * ts, ts), ts)

        @pl.when(j == 0)
        def _():
            dh2acc[rows, :] = contrib

        @pl.when(j > 0)
        def _():
            dh2acc[rows, :] += contrib

        @pl.when(i == ni - 1)
        def _():
            dwg_ref[...] = gacc[...].astype(BF)
            dwu_ref[...] = uacc[...].astype(BF)
            dwd_ref[...] = dacc[...].astype(BF)

        @pl.when(j == nj - 1)
        def _():
            dh2_ref[...] = dh2acc[rows, :]

    row = lambda c: pl.BlockSpec((ts, c), lambda j, i: (i, 0))
    act = pl.BlockSpec((ts, tf), lambda j, i: (i, j))
    wblk = pl.BlockSpec((tf, D), lambda j, i: (j, 0))
    out_shape = (
        jax.ShapeDtypeStruct((FF, D), BF), jax.ShapeDtypeStruct((FF, D), BF), jax.ShapeDtypeStruct((FF, D), BF),
        jax.ShapeDtypeStruct((S, D), F32),
    )
    return pl.pallas_call(
        body, name="ffn_bwd", out_shape=out_shape, grid=(nj, ni),
        in_specs=[row(D), row(D), act, act, wblk, wblk, wblk],
        out_specs=(wblk, wblk, wblk, pl.BlockSpec((ts, D), lambda j, i: (jnp.where(j == nj - 1, i, 0), 0))),
        scratch_shapes=[pltpu.VMEM((tf, D), F32), pltpu.VMEM((tf, D), F32), pltpu.VMEM((tf, D), F32),
                        pltpu.VMEM((S, D), F32)],
        compiler_params=_params(("arbitrary", "arbitrary")),
    )(dff, h2, a, b, wg_t, wu_t, wd)


def _mix_bwd(dh2, dx3, x2, mix, mod, g_ffn, ymla, ypool, w_o, ypre, pooled, pool_scale, wpool_dc, olat, wuv_vc):
    S = dh2.shape[0]
    ts = 512
    n = S // ts
    nsub = ts // TQ
    M = HEADS * TQ

    def body(dh2_ref, dx3_ref, x2_ref, mix_ref, mod_ref, gffn_ref, ymla_ref, ypool_ref, wo_ref, ypre_ref, pooled_ref,
             pscale_ref, wpool_ref, olat_ref, wuv_ref,
             dx2_ref, du_ref, dolat_ref, delta_ref, dwo_ref, dwuv_ref, dwpool_ref, dpscale_ref, dgt1_ref, dsc2_ref,
             dsh2_ref, dgffn_ref, carry_ref, dwo_acc):
        i = pl.program_id(0)

        @pl.when(i == 0)
        def _():
            carry_ref[...] = jnp.zeros_like(carry_ref)
            dwo_acc[...] = jnp.zeros_like(dwo_acc)
            for r in (dwuv_ref, dwpool_ref, dpscale_ref, dgt1_ref, dsc2_ref, dsh2_ref, dgffn_ref):
                r[...] = jnp.zeros_like(r)

        gt1 = mod_ref[0:1, 2 * D:3 * D]
        sc2 = mod_ref[0:1, 4 * D:5 * D]
        gffn = gffn_ref[...]
        dh2 = dh2_ref[...]
        x2 = x2_ref[...]
        r2 = _rms(x2)
        xn2 = x2 * r2
        dsc2_ref[...] += _colsum(dh2 * (xn2 * gffn))
        dsh2_ref[...] += _colsum(dh2)
        dgffn_ref[...] += _colsum(dh2 * (1.0 + sc2) * xn2)
        dx2 = dx3_ref[...] + _rms_bwd(dh2 * gffn * (1.0 + sc2), xn2, r2)
        dx2_ref[...] = dx2
        dgt1_ref[...] += _colsum(dx2 * mix_ref[...])
        dmix = (dx2 * gt1).astype(BF)
        cat = jnp.concatenate([ymla_ref[...], ypool_ref[...]], axis=1)
        dwo_acc[...] += _dot_tn(cat, dmix)
        dcat = _dot_nt(dmix, wo_ref[...])
        dymla = dcat[:, 0:512]
        dypool = dcat[:, 512:1024]

        dpscale_ref[...] += _colsum(dypool * ypre_ref[...])
        dypre = (dypool * pscale_ref[...]).astype(BF)
        pooled = pooled_ref[...]
        dpooled = []
        for g in range(GROUPS):
            sl = slice(g * GD, (g + 1) * GD)
            dwpool_ref[g] += _dot_tn(pooled[:, sl], dypre[:, sl])
            dpooled.append(_dot(dypre[:, sl], wpool_ref[g]))
        dpooled = jnp.concatenate(dpooled, axis=1)
        tile = n - 1 - i
        e = dpooled / _row_counts(tile * ts, ts)
        ext = jnp.concatenate([e, carry_ref[...]], axis=0)
        du_ref[...] = _window_sums(ext, False)[0:ts, :] - dpooled
        carry_ref[...] = e[0:16, :]

        for hd in range(HEADS):
            do = dymla[:, hd * 128:(hd + 1) * 128]
            dob = do.astype(BF)
            dol = _dot(dob, wuv_ref[hd])
            for a in range(nsub):
                ol = olat_ref[a, hd * TQ:(hd + 1) * TQ, :]
                dl = dol[a * TQ:(a + 1) * TQ, :]
                dolat_ref[a, hd * TQ:(hd + 1) * TQ, :] = dl.astype(BF)
                dwuv_ref[hd] += _dot_tn(ol, dob[a * TQ:(a + 1) * TQ, :])
                delta = jnp.sum(dl * ol, axis=-1, keepdims=True)
                delta_ref[a, :, hd * TQ:(hd + 1) * TQ] = jnp.broadcast_to(delta, (TQ, 128)).T[0:8, :]

        @pl.when(i == n - 1)
        def _():
            dwo_ref[...] = dwo_acc[...].astype(BF)

    rev = lambda c: pl.BlockSpec((ts, c), lambda i: (n - 1 - i, 0))
    rev3 = lambda r, c: pl.BlockSpec((nsub, r, c), lambda i: (n - 1 - i, 0, 0))
    out_shape = (
        jax.ShapeDtypeStruct((S, D), F32),
        jax.ShapeDtypeStruct((S, PW), F32),
        jax.ShapeDtypeStruct((S // TQ, M, KVL), BF),
        jax.ShapeDtypeStruct((S // TQ, 8, M), F32),
        jax.ShapeDtypeStruct((D, D), BF),
        jax.ShapeDtypeStruct((HEADS, KVL, 128), F32),
        jax.ShapeDtypeStruct((GROUPS, GD, GD), F32),
        jax.ShapeDtypeStruct((1, PW), F32),
        jax.ShapeDtypeStruct((1, D), F32), jax.ShapeDtypeStruct((1, D), F32), jax.ShapeDtypeStruct((1, D), F32),
        jax.ShapeDtypeStruct((1, D), F32),
    )
    in_specs = [rev(D), rev(D), rev(D), rev(D), _full(mod.shape), _full((1, D)), rev(PW), rev(PW), _full(w_o.shape),
                rev(PW), rev(PW), _full((1, PW)), _full(wpool_dc.shape), rev3(M, KVL), _full(wuv_vc.shape)]
    out_specs = (rev(D), rev(PW), rev3(M, KVL), rev3(8, M), _full((D, D)), _full((HEADS, KVL, 128)),
                 _full((GROUPS, GD, GD)), _full((1, PW)), _full((1, D)), _full((1, D)), _full((1, D)), _full((1, D)))
    return pl.pallas_call(
        body, name="mix_bwd", out_shape=out_shape, grid=(n,), in_specs=in_specs, out_specs=out_specs,
        scratch_shapes=[pltpu.VMEM((16, PW), F32), pltpu.VMEM((D, D), F32)],
        compiler_params=_params(("arbitrary",)),
    )(dh2, dx3, x2, mix, mod, g_ffn, ymla, ypool, w_o, ypre, pooled, pool_scale, wpool_dc, olat, wuv_vc)


def _attn_bwd(qs, kv, dolat, lse, delta):
    nq = qs.shape[0]
    S = kv.shape[0]
    M = HEADS * TQ
    nk = S // TK

    def body(qs_ref, kv_ref, do_ref, lse_ref, delta_ref, dkv_ref, dqt_ref):
        kt = pl.program_id(0)
        k = kv_ref[...]
        v = k[:, 0:KVL]
        k_t = k.astype(F32).T.astype(BF)

        @pl.when(kt == 0)
        def _():
            dqt_ref[...] = jnp.zeros_like(dqt_ref)

        def step(qi, carry, masked):
            dk, dv = carry
            q = qs_ref[qi]
            do = do_ref[qi]
            s = _dot_nt(k, q) * SM_SCALE
            p = jnp.exp(s - lse_ref[qi, 0:1, :])
            if masked:
                p = jnp.where(_diag_mask((TK, M), 1), p, 0.0)
            dp = _dot_nt(v, do)
            ds = (p * (dp - delta_ref[qi, 0:1, :]) * SM_SCALE).astype(BF)
            dv = dv + _dot(p.astype(BF), do)
            dk = dk + _dot(ds, q)
            dqt_ref[qi] += _dot(k_t, ds)
            return dk, dv

        carry = step(kt, (jnp.zeros((TK, QW), F32), jnp.zeros((TK, KVL), F32)), True)
        dk, dv = lax.fori_loop(kt + 1, nq, lambda qi, c: step(qi, c, False), carry)
        dkv_ref[...] = dk + jnp.concatenate([dv, jnp.zeros((TK, QW - KVL), F32)], axis=1)

    out_shape = (jax.ShapeDtypeStruct((S, QW), F32), jax.ShapeDtypeStruct((nq, QW, M), F32))
    return pl.pallas_call(
        body, name="attn_bwd", out_shape=out_shape, grid=(nk,),
        in_specs=[_vmem(), _rows(TK, QW), _vmem(), _vmem(), _vmem()],
        out_specs=(_rows(TK, QW), _vmem()),
        compiler_params=_params(("arbitrary",)),
    )(qs, kv, dolat, lse, delta)


def _in_bwd(dqt, dkv, du, raw, qn, h1, x, dx2, mod, g_mix, w_in, g_q, g_kv, w_uq, wuk_cd, perm_t, cos4, sin4, csk,
            snk):
    S = x.shape[0]
    ts = 512
    n = S // ts
    nsub = ts // TQ
    M = HEADS * TQ

    def body(dqt_ref, dkv_ref, du_ref, raw_ref, qn_ref, h1_ref, x_ref, dx2_ref, mod_ref, gmix_ref, win_ref, gq_ref,
             gkv_ref, wuq_ref, wuk_ref, permt_ref, cos_ref, sin_ref, csk_ref, snk_ref,
             dx_ref, dwin_ref, dwuq_ref, dwuk_ref, dgq_ref, dgkv_ref, dsc1_ref, dsh1_ref, dgmix_ref, dwin_acc,
             dwuq_acc):
        i = pl.program_id(0)

        @pl.when(i == 0)
        def _():
            dwin_acc[...] = jnp.zeros_like(dwin_acc)
            dwuq_acc[...] = jnp.zeros_like(dwuq_acc)
            for r in (dwuk_ref, dgq_ref, dgkv_ref, dsc1_ref, dsh1_ref, dgmix_ref):
                r[...] = jnp.zeros_like(r)

        dq_blocks = [dqt_ref[a].T for a in range(nsub)]
        qn = qn_ref[...]
        dq_parts = []
        drope = jnp.zeros((ts, 2 * 128), F32)
        for hd in range(HEADS):
            dqh = jnp.concatenate([blk[hd * TQ:(hd + 1) * TQ, :] for blk in dq_blocks], axis=0)
            dq_lat = dqh[:, 0:KVL].astype(BF)
            dq_parts.append(_dot(dq_lat, wuk_ref[hd]))
            dwuk_ref[hd] += _dot_tn(dq_lat, qn[:, hd * NOPE:(hd + 1) * NOPE])
            drope = drope + _dot(dqh[:, KVL:QW].astype(BF), permt_ref[hd])
        do1 = drope[:, 0:128]
        do2 = drope[:, 128:256]
        cosv = cos_ref[...]
        sinv = sin_ref[...]
        dq_parts.append(do1 * cosv + do2 * sinv)
        dq_parts.append(do2 * cosv - do1 * sinv)
        dq = jnp.concatenate(dq_parts, axis=1).astype(BF)

        cq_raw = raw_ref[:, 0:QL]
        ckv_raw = raw_ref[:, QL:QL + KVL]
        rq = _rms(cq_raw)
        nq_ = cq_raw * rq
        gq = gq_ref[...]
        dwuq_acc[...] += _dot_tn((nq_ * gq).astype(BF), dq)
        dc_q = _dot_nt(dq, wuq_ref[...])
        dgq_ref[...] += _colsum(dc_q * nq_)
        dcq_raw = _rms_bwd(dc_q * gq, nq_, rq)

        dkv = dkv_ref[...]
        rk = _rms(ckv_raw)
        nk_ = ckv_raw * rk
        dc_kv = dkv[:, 0:KVL]
        dgkv_ref[...] += _colsum(dc_kv * nk_)
        dckv_raw = _rms_bwd(dc_kv * gkv_ref[...], nk_, rk)
        dkr_roped = dkv[:, KVL:QW]
        dkr = dkr_roped * csk_ref[...] - _swap_halves(dkr_roped) * snk_ref[...]

        dproj = jnp.concatenate([dcq_raw, dckv_raw, dkr, du_ref[...]], axis=1).astype(BF)
        dwin_acc[...] += _dot_tn(h1_ref[...], dproj)
        dh1 = _dot_nt(dproj, win_ref[...])

        sc1 = mod_ref[0:1, D:2 * D]
        gmix = gmix_ref[...]
        xv = x_ref[...]
        r1 = _rms(xv)
        xn1 = xv * r1
        dsc1_ref[...] += _colsum(dh1 * (xn1 * gmix))
        dsh1_ref[...] += _colsum(dh1)
        dgmix_ref[...] += _colsum(dh1 * (1.0 + sc1) * xn1)
        dx_ref[...] = dx2_ref[...] + _rms_bwd(dh1 * gmix * (1.0 + sc1), xn1, r1)

        @pl.when(i == n - 1)
        def _():
            dwin_ref[...] = dwin_acc[...].astype(BF)
            dwuq_ref[...] = dwuq_acc[...].astype(BF)

    out_shape = (
        jax.ShapeDtypeStruct((S, D), F32),
        jax.ShapeDtypeStruct((D, D), BF),
        jax.ShapeDtypeStruct((QL, 768), BF),
        jax.ShapeDtypeStruct((HEADS, KVL, NOPE), F32),
        jax.ShapeDtypeStruct((1, QL), F32), jax.ShapeDtypeStruct((1, KVL), F32),
        jax.ShapeDtypeStruct((1, D), F32), jax.ShapeDtypeStruct((1, D), F32), jax.ShapeDtypeStruct((1, D), F32),
    )
    in_specs = [pl.BlockSpec((nsub, QW, M), lambda i: (i, 0, 0)), _rows(ts, QW), _rows(ts, PW), _rows(ts, 384),
                _rows(ts, HEADS * NOPE), _rows(ts, D), _rows(ts, D), _rows(ts, D), _full(mod.shape), _full((1, D)),
                _full(w_in.shape), _full((1, QL)), _full((1, KVL)), _full(w_uq.shape), _full(wuk_cd.shape),
                _full(perm_t.shape), _rows(ts, 128), _rows(ts, 128), _rows(ts, 128), _rows(ts, 128)]
    out_specs = (_rows(ts, D), _full((D, D)), _full((QL, 768)), _full((HEADS, KVL, NOPE)), _full((1, QL)),
                 _full((1, KVL)), _full((1, D)), _full((1, D)), _full((1, D)))
    return pl.pallas_call(
        body, name="in_bwd", out_shape=out_shape, grid=(n,), in_specs=in_specs, out_specs=out_specs,
        scratch_shapes=[pltpu.VMEM((D, D), F32), pltpu.VMEM((QL, 768), F32)],
        compiler_params=_params(("arbitrary",)),
    )(dqt, dkv, du, raw, qn, h1, x, dx2, mod, g_mix, w_in, g_q, g_kv, w_uq, wuk_cd, perm_t, cos4, sin4, csk, snk)


def _rope_perm():
    p = np.zeros((HEADS, 2 * 128, 128), np.float32)
    for hd in range(HEADS):
        for t in range(HALF):
            p[hd, hd * HALF + t, t] = 1.0
            p[hd, 128 + hd * HALF + t, HALF + t] = 1.0
    return p


def _rope_tables(positions):
    freqs = jnp.power(ROPE_THETA, -jnp.arange(HALF, dtype=F32) / HALF)
    ang = positions.astype(F32)[:, None] * freqs
    cos = jnp.cos(ang)
    sin = jnp.sin(ang)
    zero = jnp.zeros_like(cos)
    cos4 = jnp.tile(cos, (1, HEADS))
    sin4 = jnp.tile(sin, (1, HEADS))
    csk = jnp.concatenate([cos, cos, zero, zero], axis=1)
    snk = jnp.concatenate([-sin, sin, zero, zero], axis=1)
    return cos4, sin4, csk, snk


def _local_step(x, positions, target, mod, g_mix, w_in_p, g_q, g_kv, w_uq_p, w_uk, w_uv, w_pool, pool_scale, g_ffn,
                g_final, late_token, late_weights, grads_start):
    perm = jnp.asarray(_rope_perm(), BF)
    perm_t = jnp.asarray(_rope_perm().transpose(0, 2, 1), BF)
    cos4, sin4, csk, snk = _rope_tables(positions)
    wuk_dc = w_uk.transpose(1, 2, 0).astype(BF)
    wuk_cd = w_uk.transpose(1, 0, 2).astype(BF)
    wuv_cv = w_uv.transpose(1, 0, 2).astype(BF)
    wuv_vc = w_uv.transpose(1, 2, 0).astype(BF)
    wpool = w_pool.astype(BF)
    wpool_dc = w_pool.transpose(0, 2, 1).astype(BF)

    h1, raw, qn, qs, kv, pooled, ypre, ypool = _fwd_in(
        x, mod, g_mix + late_token, w_in_p, g_q, g_kv, w_uq_p, wuk_dc, perm, cos4, sin4, csk, snk, wpool, pool_scale)
    olat, ymla, lse = _attn_fwd(qs, kv, wuv_cv)
    w_o, wg_t, wu_t, wd = late_weights(ymla)
    x2, mix, h2, a, b, dx3, dff, loss, dgfin, dgt2 = _ffn_fwd(
        x, ymla, ypool, mod, w_o, g_ffn, wg_t, wu_t, wd, g_final, target)
    dwg_t, dwu_t, dwd, dh2 = _ffn_bwd(dff, h2, a, b, wg_t, wu_t, wd)
    ffn_parts = grads_start("scatter_ffn_start", (dwg_t, dwu_t, dwd))
    (dx2, du, dolat, delta, dwo, dwuv, dwpool, dpscale, dgt1, dsc2, dsh2, dgffn) = _mix_bwd(
        dh2, dx3, x2, mix, mod, g_ffn + ffn_parts[-1][0:1, 0:1], ymla, ypool, w_o, ypre, pooled, pool_scale, wpool_dc,
        olat, wuv_vc)
    wo_parts = grads_start("scatter_wo_start", (dwo,))
    dkv, dqt = _attn_bwd(qs, kv, dolat, lse, delta + wo_parts[-1][0:1, 0:1])
    dx, dwin, dwuq, dwuk, dgq, dgkv, dsc1, dsh1, dgmix = _in_bwd(
        dqt, dkv, du, raw, qn, h1, x, dx2, mod, g_mix, w_in_p, g_q, g_kv, w_uq_p, wuk_cd, perm_t, cos4, sin4, csk,
        snk)
    dmod = jnp.concatenate([dsh1, dsc1, dgt1, dsh2, dsc2, dgt2], axis=1)
    sharded = dict(w_in=dwin, w_uq=dwuq)
    replicated = dict(
        w_uk=dwuk.transpose(1, 0, 2), w_uv=dwuv.transpose(1, 0, 2), w_pool=dwpool, g_mix=dgmix, g_q=dgq, g_kv=dgkv,
        pool_scale=dpscale, g_ffn=dgffn, g_final=dgfin)
    return loss[0, 0], dx, dmod, sharded, replicated, ffn_parts, wo_parts


def _my_pos():
    return lax.axis_index("x"), lax.axis_index("y"), lax.axis_index("c")


def _peer(pos, k):
    x, y, c = pos
    return (1 - x if k & 4 else x, 1 - y if k & 2 else y, 1 - c if k & 1 else c)


def _index(pos):
    x, y, c = pos
    return 4 * x + 2 * y + c


def _remote(src, dst, send_sem, recv_sem, to):
    return pltpu.make_async_remote_copy(src_ref=src, dst_ref=dst, send_sem=send_sem, recv_sem=recv_sem,
                                        device_id=to, device_id_type=MESH)


def _ada_mod(c, w_ada, b_ada):
    def body(c_ref, w_ref, b_ref, mod_ref, call_ref, cbuf, sbuf, rbuf, send1, recv1, send2, recv2):
        me = _my_pos()
        mi = _index(me)
        cv = c_ref[...]
        cbuf[...] = jnp.broadcast_to(cv * jax.nn.sigmoid(cv), (8, D))
        call_ref[mi] = cbuf[...]
        first = [_remote(cbuf, call_ref.at[mi], send1.at[k - 1], recv1.at[k - 1], _peer(me, k)) for k in range(1, NDEV)]
        for cp in first:
            cp.start()
        for k in range(1, NDEV):
            _remote(cbuf, call_ref.at[_index(_peer(me, k))], send1.at[k - 1], recv1.at[k - 1], _peer(me, k)).wait_recv()
        c_all = jnp.concatenate([call_ref[b][0:1, :] for b in range(NDEV)], axis=0)
        blocks = _dot(c_all.astype(BF), w_ref[...].astype(BF))
        for b in range(NDEV):
            sbuf[b] = jnp.broadcast_to(blocks[b:b + 1, :], (8, MODC))
        second = []
        for k in range(1, NDEV):
            to = _peer(me, k)
            second.append(_remote(sbuf.at[_index(to)], rbuf.at[mi], send2.at[k - 1], recv2.at[k - 1], to))
        for cp in second:
            cp.start()
        rbuf[mi] = sbuf[mi]
        for k in range(1, NDEV):
            to = _peer(me, k)
            _remote(sbuf.at[_index(to)], rbuf.at[_index(to)], send2.at[k - 1], recv2.at[k - 1], to).wait_recv()
        for j in range(NDEV):
            mod_ref[:, j * MODC:(j + 1) * MODC] = rbuf[j] + b_ref[:, j * MODC:(j + 1) * MODC]
        for cp in first + second:
            cp.wait_send()

    return pl.pallas_call(
        body, name="ada_mod",
        out_shape=(jax.ShapeDtypeStruct((8, N_MOD * D), F32), jax.ShapeDtypeStruct((NDEV, 8, D), F32)),
        in_specs=[_vmem(), _vmem(), _vmem()], out_specs=(_vmem(), _vmem()),
        scratch_shapes=[pltpu.VMEM((8, D), F32), pltpu.VMEM((NDEV, 8, MODC), F32), pltpu.VMEM((NDEV, 8, MODC), F32),
                        pltpu.SemaphoreType.DMA((NDEV - 1,)), pltpu.SemaphoreType.DMA((NDEV - 1,)),
                        pltpu.SemaphoreType.DMA((NDEV - 1,)), pltpu.SemaphoreType.DMA((NDEV - 1,))],
        compiler_params=_params(),
    )(c, w_ada, b_ada)


def _all_gather(shards):
    n = len(shards)

    def body(*refs):
        ins, outs = refs[:n], refs[n:2 * n]
        send, recv, local = refs[2 * n:]
        me = _my_pos()
        mi = _index(me)
        own = [pltpu.make_async_copy(ins[a], outs[a].at[mi], local.at[a]) for a in range(n)]
        for cp in own:
            cp.start()
        sent = []
        for a in range(n):
            for k in range(1, NDEV):
                sent.append(_remote(ins[a], outs[a].at[mi], send.at[a, k - 1], recv.at[a, k - 1], _peer(me, k)))
        for cp in sent:
            cp.start()
        for a in range(n):
            for k in range(1, NDEV):
                to = _peer(me, k)
                _remote(ins[a], outs[a].at[_index(to)], send.at[a, k - 1], recv.at[a, k - 1], to).wait_recv()
        for cp in sent:
            cp.wait_send()
        for cp in own:
            cp.wait()

    return pl.pallas_call(
        body, name="gather_weights",
        out_shape=tuple(jax.ShapeDtypeStruct((NDEV,) + s.shape, s.dtype) for s in shards),
        in_specs=[_any()] * n, out_specs=tuple([_any()] * n),
        scratch_shapes=[pltpu.SemaphoreType.DMA((n, NDEV - 1)), pltpu.SemaphoreType.DMA((n, NDEV - 1)),
                        pltpu.SemaphoreType.DMA((n,))],
        compiler_params=_params(),
    )(*shards)


def _scatter_partials(grads):
    n = len(grads)

    def body(*refs):
        ins, outs = refs[:n], refs[n:2 * n]
        send, recv, local = refs[2 * n:]
        me = _my_pos()
        mi = _index(me)

        def rows_of(a, dev_index):
            r = ins[a].shape[0] // NDEV
            return ins[a].at[pl.ds(pl.multiple_of(dev_index * r, 16), r), :]

        own = [pltpu.make_async_copy(rows_of(a, mi), outs[a].at[mi], local.at[a]) for a in range(n)]
        for cp in own:
            cp.start()
        sent = []
        for a in range(n):
            for k in range(1, NDEV):
                to = _peer(me, k)
                sent.append(_remote(rows_of(a, _index(to)), outs[a].at[mi], send.at[a, k - 1], recv.at[a, k - 1], to))
        for cp in sent:
            cp.start()
        for a in range(n):
            for k in range(1, NDEV):
                to = _peer(me, k)
                _remote(rows_of(a, mi), outs[a].at[_index(to)], send.at[a, k - 1], recv.at[a, k - 1], to).wait_recv()
        for cp in sent:
            cp.wait_send()
        for cp in own:
            cp.wait()

    return pl.pallas_call(
        body, name="scatter_grads",
        out_shape=tuple(jax.ShapeDtypeStruct((NDEV, g.shape[0] // NDEV, g.shape[1]), g.dtype) for g in grads),
        in_specs=[_any()] * n, out_specs=tuple([_any()] * n),
        scratch_shapes=[pltpu.SemaphoreType.DMA((n, NDEV - 1)), pltpu.SemaphoreType.DMA((n, NDEV - 1)),
                        pltpu.SemaphoreType.DMA((n,))],
        compiler_params=_params(),
    )(*grads)


_HBM = pl.BlockSpec(memory_space=pltpu.HBM)
_SEM = pl.BlockSpec(memory_space=pltpu.SEMAPHORE)
_EFFECT = pltpu.SideEffectType.DATAFLOW_SIDE_EFFECTING


def _exchange_slices(scatter):
    def of(src, to_index):
        if not scatter:
            return src
        r = src.shape[0] // NDEV
        return src.at[pl.ds(pl.multiple_of(to_index * r, 16), r), :]
    return of


def _exchange_start(name, srcs, scatter, after):
    n = len(srcs)
    of = _exchange_slices(scatter)
    zones = [lax.empty((NDEV, s.shape[0] // NDEV if scatter else s.shape[0], s.shape[1]), s.dtype) for s in srcs]

    def body(*refs):
        src, land = refs[:n], refs[n:2 * n]
        first_out = 2 * n + len(after)
        send, recv = refs[first_out], refs[first_out + 1]
        token, local = refs[first_out + 2 + 2 * n], refs[first_out + 3 + 2 * n]
        me = _my_pos()
        mi = _index(me)
        own = [pltpu.make_async_copy(of(src[a], mi), land[a].at[mi], local.at[a]) for a in range(n)]
        for cp in own:
            cp.start()
        for cp in own:
            cp.wait()
        for a in range(n):
            for k in range(1, NDEV):
                to = _peer(me, k)
                s = a * (NDEV - 1) + k - 1
                _remote(of(src[a], _index(to)), land[a].at[mi], send.at[s], recv.at[s], to).start()
        token[...] = jnp.zeros_like(token)

    sems = pltpu.SemaphoreType.DMA((n * (NDEV - 1),))
    return pl.pallas_call(
        body, name=name,
        out_shape=(sems, sems, *[pltpu.HBM(s.shape, s.dtype) for s in srcs],
                   *[pltpu.HBM(z.shape, z.dtype) for z in zones], jax.ShapeDtypeStruct((8, 128), F32)),
        in_specs=[_HBM] * (2 * n) + [_any()] * len(after), out_specs=(_SEM, _SEM, *[_HBM] * (2 * n), _vmem()),
        input_output_aliases={i: 2 + i for i in range(2 * n)},
        scratch_shapes=[pltpu.SemaphoreType.DMA((n,))],
        compiler_params=pltpu.CompilerParams(has_side_effects=_EFFECT),
    )(*[pltpu.with_memory_space_constraint(s, pltpu.HBM) for s in srcs],
      *[pltpu.with_memory_space_constraint(z, pltpu.HBM) for z in zones], *after)


def _exchange_wait(name, started, scatter, after):
    send, recv, *bufs, _ = started
    n = len(bufs) // 2
    of = _exchange_slices(scatter)

    def body(*refs):
        src, land = refs[:n], refs[n:2 * n]
        send_ref, recv_ref = refs[2 * n], refs[2 * n + 1]
        me = _my_pos()
        mi = _index(me)
        for a in range(n):
            for k in range(1, NDEV):
                to = _peer(me, k)
                s = a * (NDEV - 1) + k - 1
                cp = _remote(of(src[a], mi), land[a].at[_index(to)], send_ref.at[s], recv_ref.at[s], to)
                cp.wait_send()
                cp.wait_recv()

    outs = pl.pallas_call(
        body, name=name, out_shape=tuple(pltpu.HBM(b.shape, b.dtype) for b in bufs),
        in_specs=[_HBM] * (2 * n) + [_SEM, _SEM, _any()], out_specs=tuple([_HBM] * (2 * n)),
        input_output_aliases={i: i for i in range(2 * n)},
        compiler_params=pltpu.CompilerParams(has_side_effects=_EFFECT),
    )(*bufs, send, recv, after)
    return outs[n:]


def _sum_partials(parts):
    n = len(parts)

    def body(*refs):
        for a in range(n):
            acc = refs[a][0].astype(F32)
            for p in range(1, NDEV):
                acc = acc + refs[a][p].astype(F32)
            refs[n + a][...] = acc

    return pl.pallas_call(
        body, name="sum_partials",
        out_shape=tuple(jax.ShapeDtypeStruct(p.shape[1:], F32) for p in parts),
        in_specs=[_vmem()] * n, out_specs=tuple([_vmem()] * n), compiler_params=_params(),
    )(*parts)


def _small_all_reduce(buf):
    def body(buf_ref, got_ref, red_ref, mine, send1, recv1, send2, recv2):
        me = _my_pos()
        mi = _index(me)
        first = []
        for k in range(1, NDEV):
            to = _peer(me, k)
            first.append(_remote(buf_ref.at[_index(to)], got_ref.at[mi], send1.at[k - 1], recv1.at[k - 1], to))
        for cp in first:
            cp.start()
        got_ref[mi] = buf_ref[mi]
        for k in range(1, NDEV):
            to = _peer(me, k)
            _remote(buf_ref.at[mi], got_ref.at[_index(to)], send1.at[k - 1], recv1.at[k - 1], to).wait_recv()
        acc = got_ref[0]
        for p in range(1, NDEV):
            acc = acc + got_ref[p]
        mine[...] = acc
        second = [_remote(mine, red_ref.at[mi], send2.at[k - 1], recv2.at[k - 1], _peer(me, k)) for k in range(1, NDEV)]
        for cp in second:
            cp.start()
        red_ref[mi] = acc
        for k in range(1, NDEV):
            to = _peer(me, k)
            _remote(mine, red_ref.at[_index(to)], send2.at[k - 1], recv2.at[k - 1], to).wait_recv()
        for cp in first + second:
            cp.wait_send()

    return pl.pallas_call(
        body, name="small_all_reduce",
        out_shape=(jax.ShapeDtypeStruct(buf.shape, F32), jax.ShapeDtypeStruct(buf.shape, F32)),
        in_specs=[_vmem()], out_specs=(_vmem(), _vmem()),
        scratch_shapes=[pltpu.VMEM(buf.shape[1:], F32),
                        pltpu.SemaphoreType.DMA((NDEV - 1,)), pltpu.SemaphoreType.DMA((NDEV - 1,)),
                        pltpu.SemaphoreType.DMA((NDEV - 1,)), pltpu.SemaphoreType.DMA((NDEV - 1,))],
        compiler_params=_params(),
    )(buf)


def _adamw_math(w, g, m, v):
    m = ADAM_B1 * m + (1.0 - ADAM_B1) * g
    v = ADAM_B2 * v + (1.0 - ADAM_B2) * jnp.square(g)
    m_hat = m / (1.0 - ADAM_B1 ** ADAM_STEP)
    v_hat = v / (1.0 - ADAM_B2 ** ADAM_STEP)
    delta = -ADAM_LR * (m_hat / (jnp.sqrt(v_hat) + ADAM_EPS) + ADAM_WD * w)
    return delta, m, v


def _adamw_group(name, ws, gs, ms, vs):
    n = len(ws)

    def body(*refs):
        for a in range(n):
            w, g, m, v = (refs[q * n + a][...] for q in range(4))
            delta, m2, v2 = _adamw_math(w, g, m, v)
            refs[4 * n + a][...] = delta
            refs[5 * n + a][...] = m2
            refs[6 * n + a][...] = v2

    shapes = tuple(jax.ShapeDtypeStruct(w.shape, F32) for w in ws)
    outs = pl.pallas_call(
        body, name=name, out_shape=shapes * 3, in_specs=[_vmem()] * (4 * n), out_specs=tuple([_vmem()] * (3 * n)),
        compiler_params=_params(),
    )(*ws, *gs, *ms, *vs)
    return outs[:n], outs[n:2 * n], outs[2 * n:]


def _adamw_ada(w, m, v, c_all, dmod_rows):
    def body(w_ref, m_ref, v_ref, c_ref, dm_ref, g_ref, d_ref, m2_ref, v2_ref):
        g = _dot_tn(c_ref[...], dm_ref[...].astype(BF))
        g_ref[...] = g
        delta, m2, v2 = _adamw_math(w_ref[...], g, m_ref[...], v_ref[...])
        d_ref[...] = delta
        m2_ref[...] = m2
        v2_ref[...] = v2

    shp = jax.ShapeDtypeStruct(w.shape, F32)
    return pl.pallas_call(
        body, name="adamw_ada", out_shape=(shp, shp, shp, shp), in_specs=[_vmem()] * 5,
        out_specs=tuple([_vmem()] * 4), compiler_params=_params(),
    )(w, m, v, c_all, dmod_rows)


def _w_in_to_kernel(w):
    return jnp.concatenate([w[:, 0:448], jnp.zeros((w.shape[0], 64), w.dtype), w[:, 448:960]], axis=1)


def _w_in_from_kernel(w):
    return jnp.concatenate([w[:, 0:448], w[:, 512:1024]], axis=1)


def _w_uq_to_kernel(w):
    r = w.shape[0]
    return jnp.concatenate([w[:, :, 0:NOPE].reshape(r, HEADS * NOPE),
                            w[:, :, NOPE:NOPE + HALF].reshape(r, HEADS * HALF),
                            w[:, :, NOPE + HALF:].reshape(r, HEADS * HALF)], axis=1)


def _w_uq_from_kernel(w):
    r = w.shape[0]
    return jnp.concatenate([w[:, 0:512].reshape(r, HEADS, NOPE), w[:, 512:640].reshape(r, HEADS, HALF),
                            w[:, 640:768].reshape(r, HEADS, HALF)], axis=2)


REP_NAMES = ("w_uk", "w_uv", "w_pool", "g_mix", "g_q", "g_kv", "pool_scale", "g_ffn", "g_final")


def kernel(x, c, positions, w_ada, b_ada, g_mix, w_in, g_q, g_kv, w_uq, w_uk, w_uv, w_pool, pool_scale, w_o, g_ffn, w_gate, w_up, w_down, g_final, loss_target, m_w_ada, m_b_ada, m_g_mix, m_w_in, m_g_q, m_g_kv, m_w_uq, m_w_uk, m_w_uv, m_w_pool, m_pool_scale, m_w_o, m_g_ffn, m_w_gate, m_w_up, m_w_down, m_g_final, v_w_ada, v_b_ada, v_g_mix, v_w_in, v_g_q, v_g_kv, v_w_uq, v_w_uk, v_w_uv, v_w_pool, v_pool_scale, v_w_o, v_g_ffn, v_w_gate, v_w_up, v_w_down, v_g_final):
    given = dict(locals())

    merge = lambda g: g.reshape(NDEV * g.shape[1], g.shape[2])
    w_in_p, w_uq_p = (merge(g) for g in _all_gather(
        (_w_in_to_kernel(w_in[0]).astype(BF), _w_uq_to_kernel(w_uq[0]).astype(BF))))

    mod, c_all8 = _ada_mod(c, w_ada[0], b_ada)
    c_all = c_all8[:, 0, :]
    late = _exchange_start("gather_start", (w_o[0].astype(BF), w_gate[0].T.astype(BF), w_up[0].T.astype(BF),
                                            w_down[0].astype(BF)), False, (mod, w_in_p, w_uq_p))

    def late_weights(after):
        return tuple(merge(g) for g in _exchange_wait("gather_wait", late, False, after))

    def grads_start(name, arrays):
        return _exchange_start(name, arrays, True, (mod,))

    loss, dx, dmod, sharded, replicated, ffn_parts, wo_parts = _local_step(
        x[0], positions[0], loss_target[0], mod, g_mix, w_in_p, g_q, g_kv, w_uq_p, w_uk[0], w_uv[0], w_pool[0],
        pool_scale, g_ffn, g_final.reshape(1, D), late[-1][0:1, 0:1], late_weights, grads_start)

    parts = _scatter_partials(tuple(sharded[k] for k in ("w_in", "w_uq")))
    (g_o_p,) = _exchange_wait("scatter_wo_wait", wo_parts, True, parts[0])
    g_gate_p, g_up_p, g_down_p = _exchange_wait("scatter_ffn_wait", ffn_parts, True, g_o_p)
    g_in_p, g_uq_p, g_o, g_gate_t, g_up_t, g_down = _sum_partials((*parts, g_o_p, g_gate_p, g_up_p, g_down_p))
    grads = dict(w_in=_w_in_from_kernel(g_in_p), w_uq=_w_uq_from_kernel(g_uq_p).reshape(QL // NDEV, HEADS * 192),
                 w_o=g_o, w_gate=g_gate_t.T, w_up=g_up_t.T, w_down=g_down)

    flat = jnp.concatenate([replicated[k].reshape(-1) for k in REP_NAMES])
    flat = jnp.pad(flat, (0, NDEV * REP_ROWS * 128 - flat.shape[0])).reshape(NDEV, REP_ROWS, 128)
    dmod_blocks = jnp.pad(dmod.reshape(NDEV, MODC // 128, 128), ((0, 0), (0, MOD_ROWS - MODC // 128), (0, 0)))
    got, red = _small_all_reduce(jnp.concatenate([dmod_blocks, flat], axis=1))
    dmod_rows = got[:, 0:MODC // 128, :].reshape(NDEV, MODC)
    grads["b_ada"] = red[:, 0:MODC // 128, :].reshape(1, N_MOD * D)
    rep_flat = red[:, MOD_ROWS:, :].reshape(-1)
    off = 0
    for k in REP_NAMES:
        size = int(np.prod(given[k].shape))
        grads[k] = rep_flat[off:off + size]
        off += size

    view = dict(w_ada=(D, MODC), b_ada=(1, N_MOD * D), g_mix=(1, D), w_in=(D // NDEV, 960), g_q=(1, QL),
                g_kv=(1, KVL), w_uq=(QL // NDEV, HEADS * 192), w_uk=(KVL, HEADS * NOPE), w_uv=(KVL, HEADS * 128),
                w_pool=(GROUPS * GD, GD), pool_scale=(1, PW), w_o=(D // NDEV, D), g_ffn=(1, D),
                w_gate=(D, FF // NDEV), w_up=(D, FF // NDEV), w_down=(FF // NDEV, D), g_final=(1, D))
    names = list(view)
    g_ada, d_ada, m_ada, v_ada = _adamw_ada(w_ada[0], m_w_ada[0], v_w_ada[0], c_all.astype(BF), dmod_rows)
    out_g, out_d, out_m, out_v = dict(w_ada=g_ada), dict(w_ada=d_ada), dict(w_ada=m_ada), dict(w_ada=v_ada)
    groups = (("adamw_ffn", ("w_gate", "w_up", "w_down")),
              ("adamw_rest", tuple(k for k in names if k not in ("w_ada", "w_gate", "w_up", "w_down"))))
    for gname, members in groups:
        ws = [given[k].reshape(view[k]) for k in members]
        gs = [grads[k].reshape(view[k]) for k in members]
        ms = [given["m_" + k].reshape(view[k]) for k in members]
        vs = [given["v_" + k].reshape(view[k]) for k in members]
        ds, m2, v2 = _adamw_group(gname, ws, gs, ms, vs)
        for k, g, d, mm, vv in zip(members, gs, ds, m2, v2):
            out_g[k], out_d[k], out_m[k], out_v[k] = g, d, mm, vv

    total = lax.psum(loss, ("x", "y", "c"))
    shaped = lambda d: [d[k].reshape(given[k].shape) for k in names]
    return (total, dx[None], *shaped(out_g), *shaped(out_d), *shaped(out_m), *shaped(out_v))
```

```python
import numpy as np
import jax
import jax.numpy as jnp
from jax import lax
from jax.experimental import pallas as pl
from jax.experimental.pallas import tpu as pltpu
from jax.experimental.pallas import tpu_sc as plsc

D = 1024
HEADS = 4
NOPE = 128
ROPE = 64
HALF = ROPE // 2
QL = 256
KVL = 128
FF = 2816
PW = 512
GROUPS = 4
GD = 128
N_MOD = 6
EPS = 1e-6
SM_SCALE = (NOPE + ROPE) ** -0.5
ROPE_THETA = 10000.0
NDEV = 8
MODC = N_MOD * D // NDEV

ADAM_LR = 0.001
ADAM_B1 = 0.9
ADAM_B2 = 0.999
ADAM_EPS = 1e-08
ADAM_WD = 0.01
ADAM_STEP = 10

BF = jnp.bfloat16
F32 = jnp.float32
VMEM_LIMIT_V7X = 60 * 1024 * 1024
MESH = pl.DeviceIdType.MESH

TQ = 256
TK = 256
QW = 256
MOD_ROWS = 8
REP_ROWS = 200
SMALL_ROWS = MOD_ROWS + REP_ROWS


def _params(sem=None):
    return pltpu.CompilerParams(dimension_semantics=sem, vmem_limit_bytes=VMEM_LIMIT_V7X)


def _dot(a, b):
    return jnp.dot(a, b, preferred_element_type=F32)


def _dot_nt(a, b):
    return lax.dot_general(a, b, (((1,), (1,)), ((), ())), preferred_element_type=F32)


def _dot_tn(a, b):
    return _dot(a.astype(F32).T.astype(BF), b)


def _full(shape):
    return pl.BlockSpec(shape, lambda *_: (0,) * len(shape))


def _rows(ts, cols):
    return pl.BlockSpec((ts, cols), lambda i: (i, 0))


def _vmem():
    return pl.BlockSpec(memory_space=pltpu.VMEM)


def _any():
    return pl.BlockSpec(memory_space=pl.ANY)


def _rms(v):
    return lax.rsqrt(jnp.mean(v * v, axis=-1, keepdims=True) + EPS)


def _rms_bwd(dn, n, r):
    return r * (dn - n * jnp.mean(dn * n, axis=-1, keepdims=True))


def _colsum(v):
    return jnp.sum(v, axis=0, keepdims=True)


def _swap_halves(v):
    lane = lax.broadcasted_iota(jnp.int32, v.shape, 1)
    return jnp.where(lane < HALF, pltpu.roll(v, 128 - HALF, 1), pltpu.roll(v, HALF, 1))


def _window_lane_width():
    lane = lax.broadcasted_iota(jnp.int32, (1, PW), 1)
    return jnp.where(lane < 128, 2.0, jnp.where(lane < 256, 4.0, jnp.where(lane < 384, 8.0, 16.0))).astype(F32)


def _window_sums(ext, back):
    n = ext.shape[0]

    def sh(v, k):
        return pltpu.roll(v, k if back else n - k, 0)

    s2 = ext + sh(ext, 1)
    e4 = s2[:, 128:]
    s4 = e4 + sh(e4, 2)
    e8 = s4[:, 128:]
    s8 = e8 + sh(e8, 4)
    e16 = s8[:, 128:]
    s16 = e16 + sh(e16, 8)
    return jnp.concatenate([s2[:, :128], s4[:, :128], s8[:, :128], s16], axis=1)


def _row_counts(first_row, ts):
    t1 = (first_row + lax.broadcasted_iota(jnp.int32, (ts, 1), 0) + 1).astype(F32)
    return jnp.minimum(t1, _window_lane_width())


def _fwd_in(x, mod, g_mix, w_in, g_q, g_kv, w_uq, wuk_dc, perm, cos4, sin4, csk, snk, w_pool, pool_scale):
    S = x.shape[0]
    ts = 512
    nsub = ts // TQ

    def body(x_ref, mod_ref, gmix_ref, win_ref, gq_ref, gkv_ref, wuq_ref, wuk_ref, perm_ref, cos_ref, sin_ref,
             csk_ref, snk_ref, wpool_ref, pscale_ref,
             h1_ref, raw_ref, qn_ref, qs_ref, kv_ref, pooled_ref, ypre_ref, ypool_ref, carry_ref):
        i = pl.program_id(0)

        @pl.when(i == 0)
        def _():
            carry_ref[...] = jnp.zeros_like(carry_ref)

        xv = x_ref[...]
        sh1 = mod_ref[0:1, 0:D]
        sc1 = mod_ref[0:1, D:2 * D]
        h = (xv * _rms(xv)) * gmix_ref[...] * (1.0 + sc1) + sh1
        hb = h.astype(BF)
        h1_ref[...] = hb
        proj = _dot(hb, win_ref[...])
        cq_raw = proj[:, 0:QL]
        ckv_raw = proj[:, QL:QL + KVL]
        kr = proj[:, 384:512]
        u = proj[:, 512:1024]
        raw_ref[...] = proj[:, 0:384]

        c_q = (cq_raw * _rms(cq_raw)) * gq_ref[...]
        c_kv = (ckv_raw * _rms(ckv_raw)) * gkv_ref[...]
        q = _dot(c_q.astype(BF), wuq_ref[...])
        qn = q[:, 0:HEADS * NOPE].astype(BF)
        qn_ref[...] = qn
        x1 = q[:, 512:640]
        x2 = q[:, 640:768]
        cosv = cos_ref[...]
        sinv = sin_ref[...]
        roped = jnp.concatenate([x1 * cosv - x2 * sinv, x1 * sinv + x2 * cosv], axis=1).astype(BF)
        for hd in range(HEADS):
            q_lat = _dot(qn[:, hd * NOPE:(hd + 1) * NOPE], wuk_ref[hd])
            q_rope = _dot(roped, perm_ref[hd])
            qh = jnp.concatenate([q_lat, q_rope], axis=1).astype(BF)
            for a in range(nsub):
                qs_ref[a, hd * TQ:(hd + 1) * TQ, :] = qh[a * TQ:(a + 1) * TQ, :]
        k_rope = kr * csk_ref[...] + _swap_halves(kr) * snk_ref[...]
        kv_ref[...] = jnp.concatenate([c_kv, k_rope], axis=1).astype(BF)

        ext = jnp.concatenate([carry_ref[...], u], axis=0)
        win = _window_sums(ext, True)[16:, :]
        pooled = (win / _row_counts(i * ts, ts) - u).astype(BF)
        pooled_ref[...] = pooled
        carry_ref[...] = u[ts - 16:ts, :]
        ypre = jnp.concatenate(
            [_dot(pooled[:, g * GD:(g + 1) * GD], wpool_ref[g]) for g in range(GROUPS)], axis=1)
        ypre_ref[...] = ypre
        ypool_ref[...] = (ypre * pscale_ref[...]).astype(BF)

    out_shape = (
        jax.ShapeDtypeStruct((S, D), BF),
        jax.ShapeDtypeStruct((S, 384), F32),
        jax.ShapeDtypeStruct((S, HEADS * NOPE), BF),
        jax.ShapeDtypeStruct((S // TQ, HEADS * TQ, QW), BF),
        jax.ShapeDtypeStruct((S, QW), BF),
        jax.ShapeDtypeStruct((S, PW), BF),
        jax.ShapeDtypeStruct((S, PW), F32),
        jax.ShapeDtypeStruct((S, PW), BF),
    )
    in_specs = [
        _rows(ts, D), _full(mod.shape), _full((1, D)), _full(w_in.shape), _full((1, QL)), _full((1, KVL)),
        _full(w_uq.shape), _full(wuk_dc.shape), _full(perm.shape), _rows(ts, 128), _rows(ts, 128), _rows(ts, 128),
        _rows(ts, 128), _full(w_pool.shape), _full((1, PW)),
    ]
    out_specs = (
        _rows(ts, D), _rows(ts, 384), _rows(ts, HEADS * NOPE),
        pl.BlockSpec((nsub, HEADS * TQ, QW), lambda i: (i, 0, 0)),
        _rows(ts, QW), _rows(ts, PW), _rows(ts, PW), _rows(ts, PW),
    )
    return pl.pallas_call(
        body, name="fwd_in", out_shape=out_shape, grid=(S // ts,), in_specs=in_specs, out_specs=out_specs,
        scratch_shapes=[pltpu.VMEM((16, PW), F32)], compiler_params=_params(("arbitrary",)),
    )(x, mod, g_mix, w_in, g_q, g_kv, w_uq, wuk_dc, perm, cos4, sin4, csk, snk, w_pool, pool_scale)


def _diag_mask(shape, q_axis):
    qi = (lax.broadcasted_iota(jnp.int32, shape, q_axis) & (TQ - 1)) >> 6
    ki = lax.broadcasted_iota(jnp.int32, shape, 1 - q_axis) >> 6
    return ki <= qi


def _attn_fwd(qs, kv, wuv_cv):
    nq = qs.shape[0]
    S = kv.shape[0]
    M = HEADS * TQ

    def body(qs_ref, kv_ref, wuv_ref, olat_ref, ymla_ref, lse_ref):
        i = pl.program_id(0)
        q = qs_ref[0]

        def step(kt, carry, masked):
            m, l, acc = carry
            k = kv_ref[pl.ds(pl.multiple_of(kt * TK, TK), TK), :]
            s = _dot_nt(q, k) * SM_SCALE
            if masked:
                s = jnp.where(_diag_mask((M, TK), 0), s, -jnp.inf)
            m_new = jnp.maximum(m, jnp.max(s, axis=-1, keepdims=True))
            alpha = jnp.exp(m - m_new)
            p = jnp.exp(s - m_new)
            l = alpha * l + jnp.sum(p, axis=-1, keepdims=True)
            acc = alpha * acc + _dot(p.astype(BF), k[:, 0:KVL])
            return m_new, l, acc

        init = (jnp.full((M, 1), -jnp.inf, F32), jnp.zeros((M, 1), F32), jnp.zeros((M, KVL), F32))
        carry = lax.fori_loop(0, i, lambda kt, c: step(kt, c, False), init)
        m, l, acc = step(i, carry, True)
        o_lat = acc / l
        olat_ref[0] = o_lat
        lse = m + jnp.log(l)
        lse_ref[0] = jnp.broadcast_to(lse, (M, 128)).T[0:8, :]
        for hd in range(HEADS):
            o = _dot(o_lat[hd * TQ:(hd + 1) * TQ, :].astype(BF), wuv_ref[hd])
            ymla_ref[:, hd * 128:(hd + 1) * 128] = o.astype(BF)

    out_shape = (
        jax.ShapeDtypeStruct((nq, M, KVL), F32),
        jax.ShapeDtypeStruct((S, HEADS * 128), BF),
        jax.ShapeDtypeStruct((nq, 8, M), F32),
    )
    return pl.pallas_call(
        body, name="attn_fwd", out_shape=out_shape, grid=(nq,),
        in_specs=[pl.BlockSpec((1, M, QW), lambda i: (i, 0, 0)), _full(kv.shape), _full(wuv_cv.shape)],
        out_specs=(pl.BlockSpec((1, M, KVL), lambda i: (i, 0, 0)), _rows(TQ, HEADS * 128),
                   pl.BlockSpec((1, 8, M), lambda i: (i, 0, 0))),
        compiler_params=_params(("arbitrary",)),
    )(qs, kv, wuv_cv)


def _silu_parts(a):
    sg = jax.nn.sigmoid(a)
    return sg, a * sg


def _ffn_fwd(x, ymla, ypool, mod, w_o, g_ffn, wg_t, wu_t, wd, g_final, target):
    S = x.shape[0]
    ts = 512
    tf = 256
    nj = FF // tf

    def body(x_ref, ymla_ref, ypool_ref, mod_ref, wo_ref, gffn_ref, wg_ref, wu_ref, wd_ref, gfin_ref, t_ref,
             x2_ref, mix_ref, h2_ref, a_ref, b_ref, dx3_ref, dff_ref, loss_ref, dgfin_ref, dgt2_ref, acc_ref):
        i = pl.program_id(0)
        j = pl.program_id(1)

        @pl.when(jnp.logical_and(i == 0, j == 0))
        def _():
            loss_ref[...] = jnp.zeros_like(loss_ref)
            dgfin_ref[...] = jnp.zeros_like(dgfin_ref)
            dgt2_ref[...] = jnp.zeros_like(dgt2_ref)

        @pl.when(j == 0)
        def _():
            gt1 = mod_ref[0:1, 2 * D:3 * D]
            sh2 = mod_ref[0:1, 3 * D:4 * D]
            sc2 = mod_ref[0:1, 4 * D:5 * D]
            cat = jnp.concatenate([ymla_ref[...], ypool_ref[...]], axis=1)
            mix = _dot(cat, wo_ref[...])
            mix_ref[...] = mix
            x2 = x_ref[...] + gt1 * mix
            x2_ref[...] = x2
            h2 = (x2 * _rms(x2)) * gffn_ref[...] * (1.0 + sc2) + sh2
            h2_ref[...] = h2.astype(BF)
            acc_ref[...] = jnp.zeros_like(acc_ref)

        h2b = h2_ref[...]
        a = _dot_nt(h2b, wg_ref[...])
        b = _dot_nt(h2b, wu_ref[...])
        a_ref[...] = a.astype(BF)
        b_ref[...] = b.astype(BF)
        f = _silu_parts(a)[1] * b
        acc_ref[...] += _dot(f.astype(BF), wd_ref[...])

        @pl.when(j == nj - 1)
        def _():
            gt2 = mod_ref[0:1, 5 * D:6 * D]
            ff = acc_ref[...]
            x3 = x2_ref[...] + gt2 * ff
            r3 = _rms(x3)
            xn3 = x3 * r3
            gfin = gfin_ref[...]
            e = xn3 * gfin - t_ref[...]
            loss_ref[...] += 0.5 * jnp.sum(jnp.mean(e * e, axis=-1, keepdims=True))
            dy = e * (1.0 / D)
            dgfin_ref[...] += _colsum(dy * xn3)
            dx3 = _rms_bwd(dy * gfin, xn3, r3)
            dx3_ref[...] = dx3
            dgt2_ref[...] += _colsum(dx3 * ff)
            dff_ref[...] = (dx3 * gt2).astype(BF)

    row = lambda c: pl.BlockSpec((ts, c), lambda i, j: (i, 0))
    wblk = pl.BlockSpec((tf, D), lambda i, j: (j, 0))
    act = pl.BlockSpec((ts, tf), lambda i, j: (i, j))
    const = lambda shape: pl.BlockSpec(shape, lambda i, j: (0,) * len(shape))
    out_shape = (
        jax.ShapeDtypeStruct((S, D), F32),
        jax.ShapeDtypeStruct((S, D), F32),
        jax.ShapeDtypeStruct((S, D), BF),
        jax.ShapeDtypeStruct((S, FF), BF),
        jax.ShapeDtypeStruct((S, FF), BF),
        jax.ShapeDtypeStruct((S, D), F32),
        jax.ShapeDtypeStruct((S, D), BF),
        jax.ShapeDtypeStruct((8, 128), F32),
        jax.ShapeDtypeStruct((1, D), F32),
        jax.ShapeDtypeStruct((1, D), F32),
    )
    return pl.pallas_call(
        body, name="ffn_fwd", out_shape=out_shape, grid=(S // ts, nj),
        in_specs=[row(D), row(PW), row(PW), const(mod.shape), const(w_o.shape), const((1, D)), wblk, wblk, wblk,
                  const((1, D)), row(D)],
        out_specs=(row(D), row(D), row(D), act, act, row(D), row(D), const((8, 128)), const((1, D)), const((1, D))),
        scratch_shapes=[pltpu.VMEM((ts, D), F32)],
        compiler_params=_params(("arbitrary", "arbitrary")),
    )(x, ymla, ypool, mod, w_o, g_ffn, wg_t, wu_t, wd, g_final, target)


def _ffn_bwd(dff, h2, a, b, wg_t, wu_t, wd):
    S = dff.shape[0]
    ts = 1024
    tf = 256
    ni = S // ts
    nj = FF // tf

    def body(dff_ref, h2_ref, a_ref, b_ref, wg_ref, wu_ref, wd_ref,
             dwg_ref, dwu_ref, dwd_ref, dh2_ref, gacc, uacc, dacc, dh2acc):
        j = pl.program_id(0)
        i = pl.program_id(1)
        dffb = dff_ref[...]
        h2b = h2_ref[...]
        av = a_ref[...].astype(F32)
        bv = b_ref[...].astype(F32)
        df = _dot_nt(dffb, wd_ref[...])
        sg, sa = _silu_parts(av)
        f = sa * bv
        db = df * sa
        da = df * bv * (sg * (1.0 + av * (1.0 - sg)))
        dab = da.astype(BF)
        dbb = db.astype(BF)

        @pl.when(i == 0)
        def _():
            gacc[...] = jnp.zeros_like(gacc)
            uacc[...] = jnp.zeros_like(uacc)
            dacc[...] = jnp.zeros_like(dacc)

        gacc[...] += _dot_tn(da, h2b)
        uacc[...] += _dot_tn(db, h2b)
        dacc[...] += _dot_tn(f, dffb)
        contrib = _dot(dab, wg_ref[...]) + _dot(dbb, wu_ref[...])
        rows = pl.ds(pl.multiple_of(i * ts, ts), ts)

        @pl.when(j == 0)
        def _():
            dh2acc[rows, :] = contrib

        @pl.when(j > 0)
        def _():
            dh2acc[rows, :] += contrib

        @pl.when(i == ni - 1)
        def _():
            dwg_ref[...] = gacc[...].astype(BF)
            dwu_ref[...] = uacc[...].astype(BF)
            dwd_ref[...] = dacc[...].astype(BF)

        @pl.when(j == nj - 1)
        def _():
            dh2_ref[...] = dh2acc[rows, :]

    row = lambda c: pl.BlockSpec((ts, c), lambda j, i: (i, 0))
    act = pl.BlockSpec((ts, tf), lambda j, i: (i, j))
    wblk = pl.BlockSpec((tf, D), lambda j, i: (j, 0))
    out_shape = (
        jax.ShapeDtypeStruct((FF, D), BF), jax.ShapeDtypeStruct((FF, D), BF), jax.ShapeDtypeStruct((FF, D), BF),
        jax.ShapeDtypeStruct((S, D), F32),
    )
    return pl.pallas_call(
        body, name="ffn_bwd", out_shape=out_shape, grid=(nj, ni),
        in_specs=[row(D), row(D), act, act, wblk, wblk, wblk],
        out_specs=(wblk, wblk, wblk, pl.BlockSpec((ts, D), lambda j, i: (jnp.where(j == nj - 1, i, 0), 0))),
        scratch_shapes=[pltpu.VMEM((tf, D), F32), pltpu.VMEM((tf, D), F32), pltpu.VMEM((tf, D), F32),
                        pltpu.VMEM((S, D), F32)],
        compiler_params=_params(("arbitrary", "arbitrary")),
    )(dff, h2, a, b, wg_t, wu_t, wd)


def _mix_bwd(dh2, dx3, x2, mix, mod, g_ffn, ymla, ypool, w_o, ypre, pooled, pool_scale, wpool_dc, olat, wuv_vc):
    S = dh2.shape[0]
    ts = 512
    n = S // ts
    nsub = ts // TQ
    M = HEADS * TQ

    def body(dh2_ref, dx3_ref, x2_ref, mix_ref, mod_ref, gffn_ref, ymla_ref, ypool_ref, wo_ref, ypre_ref, pooled_ref,
             pscale_ref, wpool_ref, olat_ref, wuv_ref,
             dx2_ref, du_ref, dolat_ref, delta_ref, dwo_ref, dwuv_ref, dwpool_ref, dpscale_ref, dgt1_ref, dsc2_ref,
             dsh2_ref, dgffn_ref, carry_ref, dwo_acc):
        i = pl.program_id(0)

        @pl.when(i == 0)
        def _():
            carry_ref[...] = jnp.zeros_like(carry_ref)
            dwo_acc[...] = jnp.zeros_like(dwo_acc)
            for r in (dwuv_ref, dwpool_ref, dpscale_ref, dgt1_ref, dsc2_ref, dsh2_ref, dgffn_ref):
                r[...] = jnp.zeros_like(r)

        gt1 = mod_ref[0:1, 2 * D:3 * D]
        sc2 = mod_ref[0:1, 4 * D:5 * D]
        gffn = gffn_ref[...]
        dh2 = dh2_ref[...]
        x2 = x2_ref[...]
        r2 = _rms(x2)
        xn2 = x2 * r2
        dsc2_ref[...] += _colsum(dh2 * (xn2 * gffn))
        dsh2_ref[...] += _colsum(dh2)
        dgffn_ref[...] += _colsum(dh2 * (1.0 + sc2) * xn2)
        dx2 = dx3_ref[...] + _rms_bwd(dh2 * gffn * (1.0 + sc2), xn2, r2)
        dx2_ref[...] = dx2
        dgt1_ref[...] += _colsum(dx2 * mix_ref[...])
        dmix = (dx2 * gt1).astype(BF)
        cat = jnp.concatenate([ymla_ref[...], ypool_ref[...]], axis=1)
        dwo_acc[...] += _dot_tn(cat, dmix)
        dcat = _dot_nt(dmix, wo_ref[...])
        dymla = dcat[:, 0:512]
        dypool = dcat[:, 512:1024]

        dpscale_ref[...] += _colsum(dypool * ypre_ref[...])
        dypre = (dypool * pscale_ref[...]).astype(BF)
        pooled = pooled_ref[...]
        dpooled = []
        for g in range(GROUPS):
            sl = slice(g * GD, (g + 1) * GD)
            dwpool_ref[g] += _dot_tn(pooled[:, sl], dypre[:, sl])
            dpooled.append(_dot(dypre[:, sl], wpool_ref[g]))
        dpooled = jnp.concatenate(dpooled, axis=1)
        tile = n - 1 - i
        e = dpooled / _row_counts(tile * ts, ts)
        ext = jnp.concatenate([e, carry_ref[...]], axis=0)
        du_ref[...] = _window_sums(ext, False)[0:ts, :] - dpooled
        carry_ref[...] = e[0:16, :]

        for hd in range(HEADS):
            do = dymla[:, hd * 128:(hd + 1) * 128]
            dob = do.astype(BF)
            dol = _dot(dob, wuv_ref[hd])
            for a in range(nsub):
                ol = olat_ref[a, hd * TQ:(hd + 1) * TQ, :]
                dl = dol[a * TQ:(a + 1) * TQ, :]
                dolat_ref[a, hd * TQ:(hd + 1) * TQ, :] = dl.astype(BF)
                dwuv_ref[hd] += _dot_tn(ol, dob[a * TQ:(a + 1) * TQ, :])
                delta = jnp.sum(dl * ol, axis=-1, keepdims=True)
                delta_ref[a, :, hd * TQ:(hd + 1) * TQ] = jnp.broadcast_to(delta, (TQ, 128)).T[0:8, :]

        @pl.when(i == n - 1)
        def _():
            dwo_ref[...] = dwo_acc[...].astype(BF)

    rev = lambda c: pl.BlockSpec((ts, c), lambda i: (n - 1 - i, 0))
    rev3 = lambda r, c: pl.BlockSpec((nsub, r, c), lambda i: (n - 1 - i, 0, 0))
    out_shape = (
        jax.ShapeDtypeStruct((S, D), F32),
        jax.ShapeDtypeStruct((S, PW), F32),
        jax.ShapeDtypeStruct((S // TQ, M, KVL), BF),
        jax.ShapeDtypeStruct((S // TQ, 8, M), F32),
        jax.ShapeDtypeStruct((D, D), BF),
        jax.ShapeDtypeStruct((HEADS, KVL, 128), F32),
        jax.ShapeDtypeStruct((GROUPS, GD, GD), F32),
        jax.ShapeDtypeStruct((1, PW), F32),
        jax.ShapeDtypeStruct((1, D), F32), jax.ShapeDtypeStruct((1, D), F32), jax.ShapeDtypeStruct((1, D), F32),
        jax.ShapeDtypeStruct((1, D), F32),
    )
    in_specs = [rev(D), rev(D), rev(D), rev(D), _full(mod.shape), _full((1, D)), rev(PW), rev(PW), _full(w_o.shape),
                rev(PW), rev(PW), _full((1, PW)), _full(wpool_dc.shape), rev3(M, KVL), _full(wuv_vc.shape)]
    out_specs = (rev(D), rev(PW), rev3(M, KVL), rev3(8, M), _full((D, D)), _full((HEADS, KVL, 128)),
                 _full((GROUPS, GD, GD)), _full((1, PW)), _full((1, D)), _full((1, D)), _full((1, D)), _full((1, D)))
    return pl.pallas_call(
        body, name="mix_bwd", out_shape=out_shape, grid=(n,), in_specs=in_specs, out_specs=out_specs,
        scratch_shapes=[pltpu.VMEM((16, PW), F32), pltpu.VMEM((D, D), F32)],
        compiler_params=_params(("arbitrary",)),
    )(dh2, dx3, x2, mix, mod, g_ffn, ymla, ypool, w_o, ypre, pooled, pool_scale, wpool_dc, olat, wuv_vc)


def _attn_bwd(qs, kv, dolat, lse, delta):
    nq = qs.shape[0]
    S = kv.shape[0]
    M = HEADS * TQ
    nk = S // TK

    def body(qs_ref, kv_ref, do_ref, lse_ref, delta_ref, dkv_ref, dqt_ref):
        kt = pl.program_id(0)
        k = kv_ref[...]
        v = k[:, 0:KVL]
        k_t = k.astype(F32).T.astype(BF)

        @pl.when(kt == 0)
        def _():
            dqt_ref[...] = jnp.zeros_like(dqt_ref)

        def step(qi, carry, masked):
            dk, dv = carry
            q = qs_ref[qi]
            do = do_ref[qi]
            s = _dot_nt(k, q) * SM_SCALE
            p = jnp.exp(s - lse_ref[qi, 0:1, :])
            if masked:
                p = jnp.where(_diag_mask((TK, M), 1), p, 0.0)
            dp = _dot_nt(v, do)
            ds = (p * (dp - delta_ref[qi, 0:1, :]) * SM_SCALE).astype(BF)
            dv = dv + _dot(p.astype(BF), do)
            dk = dk + _dot(ds, q)
            dqt_ref[qi] += _dot(k_t, ds)
            return dk, dv

        carry = step(kt, (jnp.zeros((TK, QW), F32), jnp.zeros((TK, KVL), F32)), True)
        dk, dv = lax.fori_loop(kt + 1, nq, lambda qi, c: step(qi, c, False), carry)
        dkv_ref[...] = dk + jnp.concatenate([dv, jnp.zeros((TK, QW - KVL), F32)], axis=1)

    out_shape = (jax.ShapeDtypeStruct((S, QW), F32), jax.ShapeDtypeStruct((nq, QW, M), F32))
    return pl.pallas_call(
        body, name="attn_bwd", out_shape=out_shape, grid=(nk,),
        in_specs=[_vmem(), _rows(TK, QW), _vmem(), _vmem(), _vmem()],
        out_specs=(_rows(TK, QW), _vmem()),
        compiler_params=_params(("arbitrary",)),
    )(qs, kv, dolat, lse, delta)


def _in_bwd(dqt, dkv, du, raw, qn, h1, x, dx2, mod, g_mix, w_in, g_q, g_kv, w_uq, wuk_cd, perm_t, cos4, sin4, csk,
            snk):
    S = x.shape[0]
    ts = 512
    n = S // ts
    nsub = ts // TQ
    M = HEADS * TQ

    def body(dqt_ref, dkv_ref, du_ref, raw_ref, qn_ref, h1_ref, x_ref, dx2_ref, mod_ref, gmix_ref, win_ref, gq_ref,
             gkv_ref, wuq_ref, wuk_ref, permt_ref, cos_ref, sin_ref, csk_ref, snk_ref,
             dx_ref, dwin_ref, dwuq_ref, dwuk_ref, dgq_ref, dgkv_ref, dsc1_ref, dsh1_ref, dgmix_ref, dwin_acc,
             dwuq_acc):
        i = pl.program_id(0)

        @pl.when(i == 0)
        def _():
            dwin_acc[...] = jnp.zeros_like(dwin_acc)
            dwuq_acc[...] = jnp.zeros_like(dwuq_acc)
            for r in (dwuk_ref, dgq_ref, dgkv_ref, dsc1_ref, dsh1_ref, dgmix_ref):
                r[...] = jnp.zeros_like(r)

        dq_blocks = [dqt_ref[a].T for a in range(nsub)]
        qn = qn_ref[...]
        dq_parts = []
        drope = jnp.zeros((ts, 2 * 128), F32)
        for hd in range(HEADS):
            dqh = jnp.concatenate([blk[hd * TQ:(hd + 1) * TQ, :] for blk in dq_blocks], axis=0)
            dq_lat = dqh[:, 0:KVL].astype(BF)
            dq_parts.append(_dot(dq_lat, wuk_ref[hd]))
            dwuk_ref[hd] += _dot_tn(dq_lat, qn[:, hd * NOPE:(hd + 1) * NOPE])
            drope = drope + _dot(dqh[:, KVL:QW].astype(BF), permt_ref[hd])
        do1 = drope[:, 0:128]
        do2 = drope[:, 128:256]
        cosv = cos_ref[...]
        sinv = sin_ref[...]
        dq_parts.append(do1 * cosv + do2 * sinv)
        dq_parts.append(do2 * cosv - do1 * sinv)
        dq = jnp.concatenate(dq_parts, axis=1).astype(BF)

        cq_raw = raw_ref[:, 0:QL]
        ckv_raw = raw_ref[:, QL:QL + KVL]
        rq = _rms(cq_raw)
        nq_ = cq_raw * rq
        gq = gq_ref[...]
        dwuq_acc[...] += _dot_tn((nq_ * gq).astype(BF), dq)
        dc_q = _dot_nt(dq, wuq_ref[...])
        dgq_ref[...] += _colsum(dc_q * nq_)
        dcq_raw = _rms_bwd(dc_q * gq, nq_, rq)

        dkv = dkv_ref[...]
        rk = _rms(ckv_raw)
        nk_ = ckv_raw * rk
        dc_kv = dkv[:, 0:KVL]
        dgkv_ref[...] += _colsum(dc_kv * nk_)
        dckv_raw = _rms_bwd(dc_kv * gkv_ref[...], nk_, rk)
        dkr_roped = dkv[:, KVL:QW]
        dkr = dkr_roped * csk_ref[...] - _swap_halves(dkr_roped) * snk_ref[...]

        dproj = jnp.concatenate([dcq_raw, dckv_raw, dkr, du_ref[...]], axis=1).astype(BF)
        dwin_acc[...] += _dot_tn(h1_ref[...], dproj)
        dh1 = _dot_nt(dproj, win_ref[...])

        sc1 = mod_ref[0:1, D:2 * D]
        gmix = gmix_ref[...]
        xv = x_ref[...]
        r1 = _rms(xv)
        xn1 = xv * r1
        dsc1_ref[...] += _colsum(dh1 * (xn1 * gmix))
        dsh1_ref[...] += _colsum(dh1)
        dgmix_ref[...] += _colsum(dh1 * (1.0 + sc1) * xn1)
        dx_ref[...] = dx2_ref[...] + _rms_bwd(dh1 * gmix * (1.0 + sc1), xn1, r1)

        @pl.when(i == n - 1)
        def _():
            dwin_ref[...] = dwin_acc[...].astype(BF)
            dwuq_ref[...] = dwuq_acc[...].astype(BF)

    out_shape = (
        jax.ShapeDtypeStruct((S, D), F32),
        jax.ShapeDtypeStruct((D, D), BF),
        jax.ShapeDtypeStruct((QL, 768), BF),
        jax.ShapeDtypeStruct((HEADS, KVL, NOPE), F32),
        jax.ShapeDtypeStruct((1, QL), F32), jax.ShapeDtypeStruct((1, KVL), F32),
        jax.ShapeDtypeStruct((1, D), F32), jax.ShapeDtypeStruct((1, D), F32), jax.ShapeDtypeStruct((1, D), F32),
    )
    in_specs = [pl.BlockSpec((nsub, QW, M), lambda i: (i, 0, 0)), _rows(ts, QW), _rows(ts, PW), _rows(ts, 384),
                _rows(ts, HEADS * NOPE), _rows(ts, D), _rows(ts, D), _rows(ts, D), _full(mod.shape), _full((1, D)),
                _full(w_in.shape), _full((1, QL)), _full((1, KVL)), _full(w_uq.shape), _full(wuk_cd.shape),
                _full(perm_t.shape), _rows(ts, 128), _rows(ts, 128), _rows(ts, 128), _rows(ts, 128)]
    out_specs = (_rows(ts, D), _full((D, D)), _full((QL, 768)), _full((HEADS, KVL, NOPE)), _full((1, QL)),
                 _full((1, KVL)), _full((1, D)), _full((1, D)), _full((1, D)))
    return pl.pallas_call(
        body, name="in_bwd", out_shape=out_shape, grid=(n,), in_specs=in_specs, out_specs=out_specs,
        scratch_shapes=[pltpu.VMEM((D, D), F32), pltpu.VMEM((QL, 768), F32)],
        compiler_params=_params(("arbitrary",)),
    )(dqt, dkv, du, raw, qn, h1, x, dx2, mod, g_mix, w_in, g_q, g_kv, w_uq, wuk_cd, perm_t, cos4, sin4, csk, snk)


def _rope_perm():
    p = np.zeros((HEADS, 2 * 128, 128), np.float32)
    for hd in range(HEADS):
        for t in range(HALF):
            p[hd, hd * HALF + t, t] = 1.0
            p[hd, 128 + hd * HALF + t, HALF + t] = 1.0
    return p


def _rope_tables(positions):
    freqs = jnp.power(ROPE_THETA, -jnp.arange(HALF, dtype=F32) / HALF)
    ang = positions.astype(F32)[:, None] * freqs
    cos = jnp.cos(ang)
    sin = jnp.sin(ang)
    zero = jnp.zeros_like(cos)
    cos4 = jnp.tile(cos, (1, HEADS))
    sin4 = jnp.tile(sin, (1, HEADS))
    csk = jnp.concatenate([cos, cos, zero, zero], axis=1)
    snk = jnp.concatenate([-sin, sin, zero, zero], axis=1)
    return cos4, sin4, csk, snk


def _local_step(x, positions, target, mod, g_mix, w_in_p, g_q, g_kv, w_uq_p, w_uk, w_uv, w_pool, pool_scale, g_ffn,
                g_final, late, grads_exchange):
    perm = jnp.asarray(_rope_perm(), BF)
    perm_t = jnp.asarray(_rope_perm().transpose(0, 2, 1), BF)
    cos4, sin4, csk, snk = _rope_tables(positions)
    wuk_dc = w_uk.transpose(1, 2, 0).astype(BF)
    wuk_cd = w_uk.transpose(1, 0, 2).astype(BF)
    wuv_cv = w_uv.transpose(1, 0, 2).astype(BF)
    wuv_vc = w_uv.transpose(1, 2, 0).astype(BF)
    wpool = w_pool.astype(BF)
    wpool_dc = w_pool.transpose(0, 2, 1).astype(BF)

    h1, raw, qn, qs, kv, pooled, ypre, ypool = _fwd_in(
        x, mod, g_mix, w_in_p, g_q, g_kv, w_uq_p, wuk_dc, perm, cos4, sin4, csk, snk, wpool, pool_scale)
    olat, ymla, lse = _attn_fwd(qs, kv, wuv_cv)
    w_o, wg_t, wu_t, wd = late
    x2, mix, h2, a, b, dx3, dff, loss, dgfin, dgt2 = _ffn_fwd(
        x, ymla, ypool, mod, w_o, g_ffn, wg_t, wu_t, wd, g_final, target)
    dwg_t, dwu_t, dwd, dh2 = _ffn_bwd(dff, h2, a, b, wg_t, wu_t, wd)
    ffn_parts = grads_exchange("scatter_ffn", 2, (dwg_t, dwu_t, dwd))
    (dx2, du, dolat, delta, dwo, dwuv, dwpool, dpscale, dgt1, dsc2, dsh2, dgffn) = _mix_bwd(
        dh2, dx3, x2, mix, mod, g_ffn, ymla, ypool, w_o, ypre, pooled, pool_scale, wpool_dc, olat, wuv_vc)
    wo_parts = grads_exchange("scatter_wo", 3, (dwo,))
    dkv, dqt = _attn_bwd(qs, kv, dolat, lse, delta)
    dx, dwin, dwuq, dwuk, dgq, dgkv, dsc1, dsh1, dgmix = _in_bwd(
        dqt, dkv, du, raw, qn, h1, x, dx2, mod, g_mix, w_in_p, g_q, g_kv, w_uq_p, wuk_cd, perm_t, cos4, sin4, csk,
        snk)
    dmod = jnp.concatenate([dsh1, dsc1, dgt1, dsh2, dsc2, dgt2], axis=1)
    in_parts = grads_exchange("scatter_in", 4, (dwin, dwuq))
    parts = (*in_parts, *wo_parts, *ffn_parts)
    replicated = dict(
        w_uk=dwuk.transpose(1, 0, 2), w_uv=dwuv.transpose(1, 0, 2), w_pool=dwpool, g_mix=dgmix, g_q=dgq, g_kv=dgkv,
        pool_scale=dpscale, g_ffn=dgffn, g_final=dgfin)
    return loss[0, 0], dx, dmod, parts, replicated


def _my_pos():
    return lax.axis_index("x"), lax.axis_index("y"), lax.axis_index("c")


def _peer(pos, k):
    x, y, c = pos
    return (1 - x if k & 4 else x, 1 - y if k & 2 else y, 1 - c if k & 1 else c)


def _index(pos):
    x, y, c = pos
    return 4 * x + 2 * y + c


def _remote(src, dst, send_sem, recv_sem, to):
    return pltpu.make_async_remote_copy(src_ref=src, dst_ref=dst, send_sem=send_sem, recv_sem=recv_sem,
                                        device_id=to, device_id_type=MESH)


def _ada_mod(c, w_ada, b_ada):
    def body(c_ref, w_ref, b_ref, mod_ref, call_ref, cbuf, sbuf, rbuf, send1, recv1, send2, recv2):
        me = _my_pos()
        mi = _index(me)
        cv = c_ref[...]
        cbuf[...] = jnp.broadcast_to(cv * jax.nn.sigmoid(cv), (8, D))
        call_ref[mi] = cbuf[...]
        first = [_remote(cbuf, call_ref.at[mi], send1.at[k - 1], recv1.at[k - 1], _peer(me, k)) for k in range(1, NDEV)]
        for cp in first:
            cp.start()
        for k in range(1, NDEV):
            _remote(cbuf, call_ref.at[_index(_peer(me, k))], send1.at[k - 1], recv1.at[k - 1], _peer(me, k)).wait_recv()
        c_all = jnp.concatenate([call_ref[b][0:1, :] for b in range(NDEV)], axis=0)
        blocks = _dot(c_all.astype(BF), w_ref[...].astype(BF))
        for b in range(NDEV):
            sbuf[b] = jnp.broadcast_to(blocks[b:b + 1, :], (8, MODC))
        second = []
        for k in range(1, NDEV):
            to = _peer(me, k)
            second.append(_remote(sbuf.at[_index(to)], rbuf.at[mi], send2.at[k - 1], recv2.at[k - 1], to))
        for cp in second:
            cp.start()
        rbuf[mi] = sbuf[mi]
        for k in range(1, NDEV):
            to = _peer(me, k)
            _remote(sbuf.at[_index(to)], rbuf.at[_index(to)], send2.at[k - 1], recv2.at[k - 1], to).wait_recv()
        for j in range(NDEV):
            mod_ref[:, j * MODC:(j + 1) * MODC] = rbuf[j] + b_ref[:, j * MODC:(j + 1) * MODC]
        for cp in first + second:
            cp.wait_send()

    return pl.pallas_call(
        body, name="ada_mod",
        out_shape=(jax.ShapeDtypeStruct((8, N_MOD * D), F32), jax.ShapeDtypeStruct((NDEV, 8, D), F32)),
        in_specs=[_vmem(), _vmem(), _vmem()], out_specs=(_vmem(), _vmem()),
        scratch_shapes=[pltpu.VMEM((8, D), F32), pltpu.VMEM((NDEV, 8, MODC), F32), pltpu.VMEM((NDEV, 8, MODC), F32),
                        pltpu.SemaphoreType.DMA((NDEV - 1,)), pltpu.SemaphoreType.DMA((NDEV - 1,)),
                        pltpu.SemaphoreType.DMA((NDEV - 1,)), pltpu.SemaphoreType.DMA((NDEV - 1,))],
        compiler_params=_params(),
    )(c, w_ada, b_ada)


def _all_gather(shards):
    n = len(shards)

    def body(*refs):
        ins, outs = refs[:n], refs[n:2 * n]
        send, recv, local = refs[2 * n:]
        me = _my_pos()
        mi = _index(me)
        own = [pltpu.make_async_copy(ins[a], outs[a].at[mi], local.at[a]) for a in range(n)]
        for cp in own:
            cp.start()
        sent = []
        for a in range(n):
            for k in range(1, NDEV):
                sent.append(_remote(ins[a], outs[a].at[mi], send.at[a, k - 1], recv.at[a, k - 1], _peer(me, k)))
        for cp in sent:
            cp.start()
        for a in range(n):
            for k in range(1, NDEV):
                to = _peer(me, k)
                _remote(ins[a], outs[a].at[_index(to)], send.at[a, k - 1], recv.at[a, k - 1], to).wait_recv()
        for cp in sent:
            cp.wait_send()
        for cp in own:
            cp.wait()

    return pl.pallas_call(
        body, name="gather_weights",
        out_shape=tuple(jax.ShapeDtypeStruct((NDEV,) + s.shape, s.dtype) for s in shards),
        in_specs=[_any()] * n, out_specs=tuple([_any()] * n),
        scratch_shapes=[pltpu.SemaphoreType.DMA((n, NDEV - 1)), pltpu.SemaphoreType.DMA((n, NDEV - 1)),
                        pltpu.SemaphoreType.DMA((n,))],
        compiler_params=_params(),
    )(*shards)


def _exchange_slices(scatter):
    def of(src, to_index):
        if not scatter:
            return src
        r = src.shape[0] // NDEV
        return src.at[pl.ds(pl.multiple_of(to_index * r, 16), r), :]
    return of


def _sequencer_exchange(name, collective_id, srcs, scatter, after=()):
    n = len(srcs)
    of = _exchange_slices(scatter)
    src_refs = [jax.new_ref(s, memory_space=pltpu.MemorySpace.HBM) for s in srcs]
    zone_refs = [
        jax.empty_ref(jax.ShapeDtypeStruct((NDEV, s.shape[0] // NDEV if scatter else s.shape[0], s.shape[1]), s.dtype),
                      memory_space=pltpu.MemorySpace.HBM) for s in srcs]
    after_refs = [jax.new_ref(t, memory_space=pltpu.MemorySpace.HBM) for t in after]
    seen_refs = [jax.empty_ref(jax.ShapeDtypeStruct(t.shape, t.dtype), memory_space=pltpu.MemorySpace.HBM)
                 for t in after]

    @pl.kernel(mesh=plsc.ScalarSubcoreMesh(axis_name="sequencer", num_cores=1), name=name,
               scratch_types=(pltpu.SemaphoreType.DMA((n * (NDEV - 1),)), pltpu.SemaphoreType.DMA((n * (NDEV - 1),)),
                              pltpu.SemaphoreType.DMA((n + len(after),))),
               compiler_params=pltpu.CompilerParams(collective_id=collective_id))
    def launch(send, recv, local):
        me = _my_pos()
        mi = _index(me)
        barrier = pltpu.get_barrier_semaphore()
        for k in range(1, NDEV):
            pl.semaphore_signal(barrier, inc=1, device_id=_peer(me, k), device_id_type=MESH)
        pl.semaphore_wait(barrier, NDEV - 1)
        own = [pltpu.make_async_copy(of(src_refs[a], mi), zone_refs[a].at[mi], local.at[a]) for a in range(n)]
        own += [pltpu.make_async_copy(t, seen, local.at[n + q])
                for q, (t, seen) in enumerate(zip(after_refs, seen_refs))]
        for cp in own:
            cp.start()
        for a in range(n):
            for k in range(1, NDEV):
                to = _peer(me, k)
                s = a * (NDEV - 1) + k - 1
                _remote(of(src_refs[a], _index(to)), zone_refs[a].at[mi], send.at[s], recv.at[s], to).start()
        for cp in own:
            cp.wait()
        for a in range(n):
            for k in range(1, NDEV):
                to = _peer(me, k)
                s = a * (NDEV - 1) + k - 1
                cp = _remote(of(src_refs[a], mi), zone_refs[a].at[_index(to)], send.at[s], recv.at[s], to)
                cp.wait_send()
                cp.wait_recv()

    launch()
    return tuple(z[...] for z in zone_refs)


def _sum_partials(parts):
    n = len(parts)

    def body(*refs):
        for a in range(n):
            acc = refs[a][0].astype(F32)
            for p in range(1, NDEV):
                acc = acc + refs[a][p].astype(F32)
            refs[n + a][...] = acc

    return pl.pallas_call(
        body, name="sum_partials",
        out_shape=tuple(jax.ShapeDtypeStruct(p.shape[1:], F32) for p in parts),
        in_specs=[_vmem()] * n, out_specs=tuple([_vmem()] * n), compiler_params=_params(),
    )(*parts)


def _small_all_reduce(buf):
    def body(buf_ref, got_ref, red_ref, mine, send1, recv1, send2, recv2):
        me = _my_pos()
        mi = _index(me)
        first = []
        for k in range(1, NDEV):
            to = _peer(me, k)
            first.append(_remote(buf_ref.at[_index(to)], got_ref.at[mi], send1.at[k - 1], recv1.at[k - 1], to))
        for cp in first:
            cp.start()
        got_ref[mi] = buf_ref[mi]
        for k in range(1, NDEV):
            to = _peer(me, k)
            _remote(buf_ref.at[mi], got_ref.at[_index(to)], send1.at[k - 1], recv1.at[k - 1], to).wait_recv()
        acc = got_ref[0]
        for p in range(1, NDEV):
            acc = acc + got_ref[p]
        mine[...] = acc
        second = [_remote(mine, red_ref.at[mi], send2.at[k - 1], recv2.at[k - 1], _peer(me, k)) for k in range(1, NDEV)]
        for cp in second:
            cp.start()
        red_ref[mi] = acc
        for k in range(1, NDEV):
            to = _peer(me, k)
            _remote(mine, red_ref.at[_index(to)], send2.at[k - 1], recv2.at[k - 1], to).wait_recv()
        for cp in first + second:
            cp.wait_send()

    return pl.pallas_call(
        body, name="small_all_reduce",
        out_shape=(jax.ShapeDtypeStruct(buf.shape, F32), jax.ShapeDtypeStruct(buf.shape, F32)),
        in_specs=[_vmem()], out_specs=(_vmem(), _vmem()),
        scratch_shapes=[pltpu.VMEM(buf.shape[1:], F32),
                        pltpu.SemaphoreType.DMA((NDEV - 1,)), pltpu.SemaphoreType.DMA((NDEV - 1,)),
                        pltpu.SemaphoreType.DMA((NDEV - 1,)), pltpu.SemaphoreType.DMA((NDEV - 1,))],
        compiler_params=_params(),
    )(buf)


def _adamw_math(w, g, m, v):
    m = ADAM_B1 * m + (1.0 - ADAM_B1) * g
    v = ADAM_B2 * v + (1.0 - ADAM_B2) * jnp.square(g)
    m_hat = m / (1.0 - ADAM_B1 ** ADAM_STEP)
    v_hat = v / (1.0 - ADAM_B2 ** ADAM_STEP)
    delta = -ADAM_LR * (m_hat / (jnp.sqrt(v_hat) + ADAM_EPS) + ADAM_WD * w)
    return delta, m, v


def _adamw_group(name, ws, gs, ms, vs):
    n = len(ws)

    def body(*refs):
        for a in range(n):
            w, g, m, v = (refs[q * n + a][...] for q in range(4))
            delta, m2, v2 = _adamw_math(w, g, m, v)
            refs[4 * n + a][...] = delta
            refs[5 * n + a][...] = m2
            refs[6 * n + a][...] = v2

    shapes = tuple(jax.ShapeDtypeStruct(w.shape, F32) for w in ws)
    outs = pl.pallas_call(
        body, name=name, out_shape=shapes * 3, in_specs=[_vmem()] * (4 * n), out_specs=tuple([_vmem()] * (3 * n)),
        compiler_params=_params(),
    )(*ws, *gs, *ms, *vs)
    return outs[:n], outs[n:2 * n], outs[2 * n:]


def _adamw_ada(w, m, v, c_all, dmod_rows):
    def body(w_ref, m_ref, v_ref, c_ref, dm_ref, g_ref, d_ref, m2_ref, v2_ref):
        g = _dot_tn(c_ref[...], dm_ref[...].astype(BF))
        g_ref[...] = g
        delta, m2, v2 = _adamw_math(w_ref[...], g, m_ref[...], v_ref[...])
        d_ref[...] = delta
        m2_ref[...] = m2
        v2_ref[...] = v2

    shp = jax.ShapeDtypeStruct(w.shape, F32)
    return pl.pallas_call(
        body, name="adamw_ada", out_shape=(shp, shp, shp, shp), in_specs=[_vmem()] * 5,
        out_specs=tuple([_vmem()] * 4), compiler_params=_params(),
    )(w, m, v, c_all, dmod_rows)


def _w_in_to_kernel(w):
    return jnp.concatenate([w[:, 0:448], jnp.zeros((w.shape[0], 64), w.dtype), w[:, 448:960]], axis=1)


def _w_in_from_kernel(w):
    return jnp.concatenate([w[:, 0:448], w[:, 512:1024]], axis=1)


def _w_uq_to_kernel(w):
    r = w.shape[0]
    return jnp.concatenate([w[:, :, 0:NOPE].reshape(r, HEADS * NOPE),
                            w[:, :, NOPE:NOPE + HALF].reshape(r, HEADS * HALF),
                            w[:, :, NOPE + HALF:].reshape(r, HEADS * HALF)], axis=1)


def _w_uq_from_kernel(w):
    r = w.shape[0]
    return jnp.concatenate([w[:, 0:512].reshape(r, HEADS, NOPE), w[:, 512:640].reshape(r, HEADS, HALF),
                            w[:, 640:768].reshape(r, HEADS, HALF)], axis=2)


REP_NAMES = ("w_uk", "w_uv", "w_pool", "g_mix", "g_q", "g_kv", "pool_scale", "g_ffn", "g_final")


def kernel(x, c, positions, w_ada, b_ada, g_mix, w_in, g_q, g_kv, w_uq, w_uk, w_uv, w_pool, pool_scale, w_o, g_ffn, w_gate, w_up, w_down, g_final, loss_target, m_w_ada, m_b_ada, m_g_mix, m_w_in, m_g_q, m_g_kv, m_w_uq, m_w_uk, m_w_uv, m_w_pool, m_pool_scale, m_w_o, m_g_ffn, m_w_gate, m_w_up, m_w_down, m_g_final, v_w_ada, v_b_ada, v_g_mix, v_w_in, v_g_q, v_g_kv, v_w_uq, v_w_uk, v_w_uv, v_w_pool, v_pool_scale, v_w_o, v_g_ffn, v_w_gate, v_w_up, v_w_down, v_g_final):
    given = dict(locals())

    merge = lambda g: g.reshape(NDEV * g.shape[1], g.shape[2])
    w_in_p, w_uq_p = (merge(g) for g in _all_gather(
        (_w_in_to_kernel(w_in[0]).astype(BF), _w_uq_to_kernel(w_uq[0]).astype(BF))))

    mod, c_all8 = _ada_mod(c, w_ada[0], b_ada)
    c_all = c_all8[:, 0, :]
    late = _sequencer_exchange(
        "gather_late", 1, (w_o[0].astype(BF), w_gate[0].T.astype(BF), w_up[0].T.astype(BF), w_down[0].astype(BF)),
        False, after=(mod[:, 0:128], w_in_p[0:16, 0:128], w_uq_p[0:16, 0:128]))

    def grads_exchange(name, collective_id, arrays):
        return _sequencer_exchange(name, collective_id, arrays, True)

    loss, dx, dmod, parts, replicated = _local_step(
        x[0], positions[0], loss_target[0], mod, g_mix, w_in_p, g_q, g_kv, w_uq_p, w_uk[0], w_uv[0], w_pool[0],
        pool_scale, g_ffn, g_final.reshape(1, D), tuple(merge(g) for g in late), grads_exchange)

    g_in_p, g_uq_p, g_o, g_gate_t, g_up_t, g_down = _sum_partials(parts)
    grads = dict(w_in=_w_in_from_kernel(g_in_p), w_uq=_w_uq_from_kernel(g_uq_p).reshape(QL // NDEV, HEADS * 192),
                 w_o=g_o, w_gate=g_gate_t.T, w_up=g_up_t.T, w_down=g_down)

    flat = jnp.concatenate([replicated[k].reshape(-1) for k in REP_NAMES])
    flat = jnp.pad(flat, (0, NDEV * REP_ROWS * 128 - flat.shape[0])).reshape(NDEV, REP_ROWS, 128)
    dmod_blocks = jnp.pad(dmod.reshape(NDEV, MODC // 128, 128), ((0, 0), (0, MOD_ROWS - MODC // 128), (0, 0)))
    got, red = _small_all_reduce(jnp.concatenate([dmod_blocks, flat], axis=1))
    dmod_rows = got[:, 0:MODC // 128, :].reshape(NDEV, MODC)
    grads["b_ada"] = red[:, 0:MODC // 128, :].reshape(1, N_MOD * D)
    rep_flat = red[:, MOD_ROWS:, :].reshape(-1)
    off = 0
    for k in REP_NAMES:
        size = int(np.prod(given[k].shape))
        grads[k] = rep_flat[off:off + size]
        off += size

    view = dict(w_ada=(D, MODC), b_ada=(1, N_MOD * D), g_mix=(1, D), w_in=(D // NDEV, 960), g_q=(1, QL),
                g_kv=(1, KVL), w_uq=(QL // NDEV, HEADS * 192), w_uk=(KVL, HEADS * NOPE), w_uv=(KVL, HEADS * 128),
                w_pool=(GROUPS * GD, GD), pool_scale=(1, PW), w_o=(D // NDEV, D), g_ffn=(1, D),
                w_gate=(D, FF // NDEV), w_up=(D, FF // NDEV), w_down=(FF // NDEV, D), g_final=(1, D))
    names = list(view)
    g_ada, d_ada, m_ada, v_ada = _adamw_ada(w_ada[0], m_w_ada[0], v_w_ada[0], c_all.astype(BF), dmod_rows)
    out_g, out_d, out_m, out_v = dict(w_ada=g_ada), dict(w_ada=d_ada), dict(w_ada=m_ada), dict(w_ada=v_ada)
    groups = (("adamw_ffn", ("w_gate", "w_up", "w_down")),
              ("adamw_rest", tuple(k for k in names if k not in ("w_ada", "w_gate", "w_up", "w_down"))))
    for gname, members in groups:
        ws = [given[k].reshape(view[k]) for k in members]
        gs = [grads[k].reshape(view[k]) for k in members]
        ms = [given["m_" + k].reshape(view[k]) for k in members]
        vs = [given["v_" + k].reshape(view[k]) for k in members]
        ds, m2, v2 = _adamw_group(gname, ws, gs, ms, vs)
        for k, g, d, mm, vv in zip(members, gs, ds, m2, v2):
            out_g[k], out_d[k], out_m[k], out_v[k] = g, d, mm, vv

    total = lax.psum(loss, ("x", "y", "c"))
    shaped = lambda d: [d[k].reshape(given[k].shape) for k in names]
    return (total, dx[None], *shaped(out_g), *shaped(out_d), *shaped(out_m), *shaped(out_v))
```

```python
import numpy as np
import jax
import jax.numpy as jnp
from jax import lax
from jax.experimental import pallas as pl
from jax.experimental.pallas import tpu as pltpu
from jax.experimental.pallas import tpu_sc as plsc

D = 1024
HEADS = 4
NOPE = 128
ROPE = 64
HALF = ROPE // 2
QL = 256
KVL = 128
FF = 2816
PW = 512
GROUPS = 4
GD = 128
N_MOD = 6
EPS = 1e-6
SM_SCALE = (NOPE + ROPE) ** -0.5
ROPE_THETA = 10000.0
NDEV = 8
MODC = N_MOD * D // NDEV

ADAM_LR = 0.001
ADAM_B1 = 0.9
ADAM_B2 = 0.999
ADAM_EPS = 1e-08
ADAM_WD = 0.01
ADAM_STEP = 10

BF = jnp.bfloat16
F32 = jnp.float32
VMEM_LIMIT_V7X = 60 * 1024 * 1024
MESH = pl.DeviceIdType.MESH

TQ = 256
TK = 256
QW = 256
MOD_ROWS = 8
REP_ROWS = 200
SMALL_ROWS = MOD_ROWS + REP_ROWS


def _params(sem=None):
    return pltpu.CompilerParams(dimension_semantics=sem, vmem_limit_bytes=VMEM_LIMIT_V7X)


def _dot(a, b):
    return jnp.dot(a, b, preferred_element_type=F32)


def _dot_nt(a, b):
    return lax.dot_general(a, b, (((1,), (1,)), ((), ())), preferred_element_type=F32)


def _dot_tn(a, b):
    return _dot(a.astype(F32).T.astype(BF), b)


def _full(shape):
    return pl.BlockSpec(shape, lambda *_: (0,) * len(shape))


def _rows(ts, cols):
    return pl.BlockSpec((ts, cols), lambda i: (i, 0))


def _vmem():
    return pl.BlockSpec(memory_space=pltpu.VMEM)


def _any():
    return pl.BlockSpec(memory_space=pl.ANY)


def _rms(v):
    return lax.rsqrt(jnp.mean(v * v, axis=-1, keepdims=True) + EPS)


def _rms_bwd(dn, n, r):
    return r * (dn - n * jnp.mean(dn * n, axis=-1, keepdims=True))


def _colsum(v):
    return jnp.sum(v, axis=0, keepdims=True)


def _swap_halves(v):
    lane = lax.broadcasted_iota(jnp.int32, v.shape, 1)
    return jnp.where(lane < HALF, pltpu.roll(v, 128 - HALF, 1), pltpu.roll(v, HALF, 1))


def _window_lane_width():
    lane = lax.broadcasted_iota(jnp.int32, (1, PW), 1)
    return jnp.where(lane < 128, 2.0, jnp.where(lane < 256, 4.0, jnp.where(lane < 384, 8.0, 16.0))).astype(F32)


def _window_sums(ext, back):
    n = ext.shape[0]

    def sh(v, k):
        return pltpu.roll(v, k if back else n - k, 0)

    s2 = ext + sh(ext, 1)
    e4 = s2[:, 128:]
    s4 = e4 + sh(e4, 2)
    e8 = s4[:, 128:]
    s8 = e8 + sh(e8, 4)
    e16 = s8[:, 128:]
    s16 = e16 + sh(e16, 8)
    return jnp.concatenate([s2[:, :128], s4[:, :128], s8[:, :128], s16], axis=1)


def _row_counts(first_row, ts):
    t1 = (first_row + lax.broadcasted_iota(jnp.int32, (ts, 1), 0) + 1).astype(F32)
    return jnp.minimum(t1, _window_lane_width())


def _fwd_in(x, mod, g_mix, w_in, g_q, g_kv, w_uq, wuk_dc, perm, cos4, sin4, csk, snk, w_pool, pool_scale):
    S = x.shape[0]
    ts = 512
    nsub = ts // TQ

    def body(x_ref, mod_ref, gmix_ref, win_ref, gq_ref, gkv_ref, wuq_ref, wuk_ref, perm_ref, cos_ref, sin_ref,
             csk_ref, snk_ref, wpool_ref, pscale_ref,
             h1_ref, raw_ref, qn_ref, qs_ref, kv_ref, pooled_ref, ypre_ref, ypool_ref, carry_ref):
        i = pl.program_id(0)

        @pl.when(i == 0)
        def _():
            carry_ref[...] = jnp.zeros_like(carry_ref)

        xv = x_ref[...]
        sh1 = mod_ref[0:1, 0:D]
        sc1 = mod_ref[0:1, D:2 * D]
        h = (xv * _rms(xv)) * gmix_ref[...] * (1.0 + sc1) + sh1
        hb = h.astype(BF)
        h1_ref[...] = hb
        proj = _dot(hb, win_ref[...])
        cq_raw = proj[:, 0:QL]
        ckv_raw = proj[:, QL:QL + KVL]
        kr = proj[:, 384:512]
        u = proj[:, 512:1024]
        raw_ref[...] = proj[:, 0:384]

        c_q = (cq_raw * _rms(cq_raw)) * gq_ref[...]
        c_kv = (ckv_raw * _rms(ckv_raw)) * gkv_ref[...]
        q = _dot(c_q.astype(BF), wuq_ref[...])
        qn = q[:, 0:HEADS * NOPE].astype(BF)
        qn_ref[...] = qn
        x1 = q[:, 512:640]
        x2 = q[:, 640:768]
        cosv = cos_ref[...]
        sinv = sin_ref[...]
        roped = jnp.concatenate([x1 * cosv - x2 * sinv, x1 * sinv + x2 * cosv], axis=1).astype(BF)
        for hd in range(HEADS):
            q_lat = _dot(qn[:, hd * NOPE:(hd + 1) * NOPE], wuk_ref[hd])
            q_rope = _dot(roped, perm_ref[hd])
            qh = jnp.concatenate([q_lat, q_rope], axis=1).astype(BF)
            for a in range(nsub):
                qs_ref[a, hd * TQ:(hd + 1) * TQ, :] = qh[a * TQ:(a + 1) * TQ, :]
        k_rope = kr * csk_ref[...] + _swap_halves(kr) * snk_ref[...]
        kv_ref[...] = jnp.concatenate([c_kv, k_rope], axis=1).astype(BF)

        ext = jnp.concatenate([carry_ref[...], u], axis=0)
        win = _window_sums(ext, True)[16:, :]
        pooled = (win / _row_counts(i * ts, ts) - u).astype(BF)
        pooled_ref[...] = pooled
        carry_ref[...] = u[ts - 16:ts, :]
        ypre = jnp.concatenate(
            [_dot(pooled[:, g * GD:(g + 1) * GD], wpool_ref[g]) for g in range(GROUPS)], axis=1)
        ypre_ref[...] = ypre
        ypool_ref[...] = (ypre * pscale_ref[...]).astype(BF)

    out_shape = (
        jax.ShapeDtypeStruct((S, D), BF),
        jax.ShapeDtypeStruct((S, 384), F32),
        jax.ShapeDtypeStruct((S, HEADS * NOPE), BF),
        jax.ShapeDtypeStruct((S // TQ, HEADS * TQ, QW), BF),
        jax.ShapeDtypeStruct((S, QW), BF),
        jax.ShapeDtypeStruct((S, PW), BF),
        jax.ShapeDtypeStruct((S, PW), F32),
        jax.ShapeDtypeStruct((S, PW), BF),
    )
    in_specs = [
        _rows(ts, D), _full(mod.shape), _full((1, D)), _full(w_in.shape), _full((1, QL)), _full((1, KVL)),
        _full(w_uq.shape), _full(wuk_dc.shape), _full(perm.shape), _rows(ts, 128), _rows(ts, 128), _rows(ts, 128),
        _rows(ts, 128), _full(w_pool.shape), _full((1, PW)),
    ]
    out_specs = (
        _rows(ts, D), _rows(ts, 384), _rows(ts, HEADS * NOPE),
        pl.BlockSpec((nsub, HEADS * TQ, QW), lambda i: (i, 0, 0)),
        _rows(ts, QW), _rows(ts, PW), _rows(ts, PW), _rows(ts, PW),
    )
    return pl.pallas_call(
        body, name="fwd_in", out_shape=out_shape, grid=(S // ts,), in_specs=in_specs, out_specs=out_specs,
        scratch_shapes=[pltpu.VMEM((16, PW), F32)], compiler_params=_params(("arbitrary",)),
    )(x, mod, g_mix, w_in, g_q, g_kv, w_uq, wuk_dc, perm, cos4, sin4, csk, snk, w_pool, pool_scale)


def _diag_mask(shape, q_axis):
    qi = (lax.broadcasted_iota(jnp.int32, shape, q_axis) & (TQ - 1)) >> 6
    ki = lax.broadcasted_iota(jnp.int32, shape, 1 - q_axis) >> 6
    return ki <= qi


def _attn_fwd(qs, kv, wuv_cv):
    nq = qs.shape[0]
    S = kv.shape[0]
    M = HEADS * TQ

    def body(qs_ref, kv_ref, wuv_ref, olat_ref, ymla_ref, lse_ref):
        i = pl.program_id(0)
        q = qs_ref[0]

        def step(kt, carry, masked):
            m, l, acc = carry
            k = kv_ref[pl.ds(pl.multiple_of(kt * TK, TK), TK), :]
            s = _dot_nt(q, k) * SM_SCALE
            if masked:
                s = jnp.where(_diag_mask((M, TK), 0), s, -jnp.inf)
            m_new = jnp.maximum(m, jnp.max(s, axis=-1, keepdims=True))
            alpha = jnp.exp(m - m_new)
            p = jnp.exp(s - m_new)
            l = alpha * l + jnp.sum(p, axis=-1, keepdims=True)
            acc = alpha * acc + _dot(p.astype(BF), k[:, 0:KVL])
            return m_new, l, acc

        init = (jnp.full((M, 1), -jnp.inf, F32), jnp.zeros((M, 1), F32), jnp.zeros((M, KVL), F32))
        carry = lax.fori_loop(0, i, lambda kt, c: step(kt, c, False), init)
        m, l, acc = step(i, carry, True)
        o_lat = acc / l
        olat_ref[0] = o_lat
        lse = m + jnp.log(l)
        lse_ref[0] = jnp.broadcast_to(lse, (M, 128)).T[0:8, :]
        for hd in range(HEADS):
            o = _dot(o_lat[hd * TQ:(hd + 1) * TQ, :].astype(BF), wuv_ref[hd])
            ymla_ref[:, hd * 128:(hd + 1) * 128] = o.astype(BF)

    out_shape = (
        jax.ShapeDtypeStruct((nq, M, KVL), F32),
        jax.ShapeDtypeStruct((S, HEADS * 128), BF),
        jax.ShapeDtypeStruct((nq, 8, M), F32),
    )
    return pl.pallas_call(
        body, name="attn_fwd", out_shape=out_shape, grid=(nq,),
        in_specs=[pl.BlockSpec((1, M, QW), lambda i: (i, 0, 0)), _full(kv.shape), _full(wuv_cv.shape)],
        out_specs=(pl.BlockSpec((1, M, KVL), lambda i: (i, 0, 0)), _rows(TQ, HEADS * 128),
                   pl.BlockSpec((1, 8, M), lambda i: (i, 0, 0))),
        compiler_params=_params(("arbitrary",)),
    )(qs, kv, wuv_cv)


def _silu_parts(a):
    sg = jax.nn.sigmoid(a)
    return sg, a * sg


def _ffn_fwd(x, ymla, ypool, mod, w_o, g_ffn, wg_t, wu_t, wd, g_final, target):
    S = x.shape[0]
    ts = 512
    tf = 256
    nj = FF // tf

    def body(x_ref, ymla_ref, ypool_ref, mod_ref, wo_ref, gffn_ref, wg_ref, wu_ref, wd_ref, gfin_ref, t_ref,
             x2_ref, mix_ref, h2_ref, a_ref, b_ref, dx3_ref, dff_ref, loss_ref, dgfin_ref, dgt2_ref, acc_ref):
        i = pl.program_id(0)
        j = pl.program_id(1)

        @pl.when(jnp.logical_and(i == 0, j == 0))
        def _():
            loss_ref[...] = jnp.zeros_like(loss_ref)
            dgfin_ref[...] = jnp.zeros_like(dgfin_ref)
            dgt2_ref[...] = jnp.zeros_like(dgt2_ref)

        @pl.when(j == 0)
        def _():
            gt1 = mod_ref[0:1, 2 * D:3 * D]
            sh2 = mod_ref[0:1, 3 * D:4 * D]
            sc2 = mod_ref[0:1, 4 * D:5 * D]
            cat = jnp.concatenate([ymla_ref[...], ypool_ref[...]], axis=1)
            mix = _dot(cat, wo_ref[...])
            mix_ref[...] = mix
            x2 = x_ref[...] + gt1 * mix
            x2_ref[...] = x2
            h2 = (x2 * _rms(x2)) * gffn_ref[...] * (1.0 + sc2) + sh2
            h2_ref[...] = h2.astype(BF)
            acc_ref[...] = jnp.zeros_like(acc_ref)

        h2b = h2_ref[...]
        a = _dot_nt(h2b, wg_ref[...])
        b = _dot_nt(h2b, wu_ref[...])
        a_ref[...] = a.astype(BF)
        b_ref[...] = b.astype(BF)
        f = _silu_parts(a)[1] * b
        acc_ref[...] += _dot(f.astype(BF), wd_ref[...])

        @pl.when(j == nj - 1)
        def _():
            gt2 = mod_ref[0:1, 5 * D:6 * D]
            ff = acc_ref[...]
            x3 = x2_ref[...] + gt2 * ff
            r3 = _rms(x3)
            xn3 = x3 * r3
            gfin = gfin_ref[...]
            e = xn3 * gfin - t_ref[...]
            loss_ref[...] += 0.5 * jnp.sum(jnp.mean(e * e, axis=-1, keepdims=True))
            dy = e * (1.0 / D)
            dgfin_ref[...] += _colsum(dy * xn3)
            dx3 = _rms_bwd(dy * gfin, xn3, r3)
            dx3_ref[...] = dx3
            dgt2_ref[...] += _colsum(dx3 * ff)
            dff_ref[...] = (dx3 * gt2).astype(BF)

    row = lambda c: pl.BlockSpec((ts, c), lambda i, j: (i, 0))
    wblk = pl.BlockSpec((tf, D), lambda i, j: (j, 0))
    act = pl.BlockSpec((ts, tf), lambda i, j: (i, j))
    const = lambda shape: pl.BlockSpec(shape, lambda i, j: (0,) * len(shape))
    out_shape = (
        jax.ShapeDtypeStruct((S, D), F32),
        jax.ShapeDtypeStruct((S, D), F32),
        jax.ShapeDtypeStruct((S, D), BF),
        jax.ShapeDtypeStruct((S, FF), BF),
        jax.ShapeDtypeStruct((S, FF), BF),
        jax.ShapeDtypeStruct((S, D), F32),
        jax.ShapeDtypeStruct((S, D), BF),
        jax.ShapeDtypeStruct((8, 128), F32),
        jax.ShapeDtypeStruct((1, D), F32),
        jax.ShapeDtypeStruct((1, D), F32),
    )
    return pl.pallas_call(
        body, name="ffn_fwd", out_shape=out_shape, grid=(S // ts, nj),
        in_specs=[row(D), row(PW), row(PW), const(mod.shape), const(w_o.shape), const((1, D)), wblk, wblk, wblk,
                  const((1, D)), row(D)],
        out_specs=(row(D), row(D), row(D), act, act, row(D), row(D), const((8, 128)), const((1, D)), const((1, D))),
        scratch_shapes=[pltpu.VMEM((ts, D), F32)],
        compiler_params=_params(("arbitrary", "arbitrary")),
    )(x, ymla, ypool, mod, w_o, g_ffn, wg_t, wu_t, wd, g_final, target)


def _ffn_bwd(dff, h2, a, b, wg_t, wu_t, wd):
    S = dff.shape[0]
    ts = 1024
    tf = 256
    ni = S // ts
    nj = FF // tf

    def body(dff_ref, h2_ref, a_ref, b_ref, wg_ref, wu_ref, wd_ref,
             dwg_ref, dwu_ref, dwd_ref, dh2_ref, gacc, uacc, dacc, dh2acc):
        j = pl.program_id(0)
        i = pl.program_id(1)
        dffb = dff_ref[...]
        h2b = h2_ref[...]
        av = a_ref[...].astype(F32)
        bv = b_ref[...].astype(F32)
        df = _dot_nt(dffb, wd_ref[...])
        sg, sa = _silu_parts(av)
        f = sa * bv
        db = df * sa
        da = df * bv * (sg * (1.0 + av * (1.0 - sg)))
        dab = da.astype(BF)
        dbb = db.astype(BF)

        @pl.when(i == 0)
        def _():
            gacc[...] = jnp.zeros_like(gacc)
            uacc[...] = jnp.zeros_like(uacc)
            dacc[...] = jnp.zeros_like(dacc)

        gacc[...] += _dot_tn(da, h2b)
        uacc[...] += _dot_tn(db, h2b)
        dacc[...] += _dot_tn(f, dffb)
        contrib = _dot(dab, wg_ref[...]) + _dot(dbb, wu_ref[...])
        rows = pl.ds(pl.multiple_of(i * ts, ts), ts)

        @pl.when(j == 0)
        def _():
            dh2acc[rows, :] = contrib

        @pl.when(j > 0)
        def _():
            dh2acc[rows, :] += contrib

        @pl.when(i == ni - 1)
        def _():
            dwg_ref[...] = gacc[...].astype(BF)
            dwu_ref[...] = uacc[...].astype(BF)
            dwd_ref[...] = dacc[...].astype(BF)

        @pl.when(j == nj - 1)
        def _():
            dh2_ref[...] = dh2acc[rows, :]

    row = lambda c: pl.BlockSpec((ts, c), lambda j, i: (i, 0))
    act = pl.BlockSpec((ts, tf), lambda j, i: (i, j))
    wblk = pl.BlockSpec((tf, D), lambda j, i: (j, 0))
    out_shape = (
        jax.ShapeDtypeStruct((FF, D), BF), jax.ShapeDtypeStruct((FF, D), BF), jax.ShapeDtypeStruct((FF, D), BF),
        jax.ShapeDtypeStruct((S, D), F32),
    )
    return pl.pallas_call(
        body, name="ffn_bwd", out_shape=out_shape, grid=(nj, ni),
        in_specs=[row(D), row(D), act, act, wblk, wblk, wblk],
        out_specs=(wblk, wblk, wblk, pl.BlockSpec((ts, D), lambda j, i: (jnp.where(j == nj - 1, i, 0), 0))),
        scratch_shapes=[pltpu.VMEM((tf, D), F32), pltpu.VMEM((tf, D), F32), pltpu.VMEM((tf, D), F32),
                        pltpu.VMEM((S, D), F32)],
        compiler_params=_params(("arbitrary", "arbitrary")),
    )(dff, h2, a, b, wg_t, wu_t, wd)


def _mix_bwd(dh2, dx3, x2, mix, mod, g_ffn, ymla, ypool, w_o, ypre, pooled, pool_scale, wpool_dc, olat, wuv_vc):
    S = dh2.shape[0]
    ts = 512
    n = S // ts
    nsub = ts // TQ
    M = HEADS * TQ

    def body(dh2_ref, dx3_ref, x2_ref, mix_ref, mod_ref, gffn_ref, ymla_ref, ypool_ref, wo_ref, ypre_ref, pooled_ref,
             pscale_ref, wpool_ref, olat_ref, wuv_ref,
             dx2_ref, du_ref, dolat_ref, delta_ref, dwo_ref, dwuv_ref, dwpool_ref, dpscale_ref, dgt1_ref, dsc2_ref,
             dsh2_ref, dgffn_ref, carry_ref, dwo_acc):
        i = pl.program_id(0)

        @pl.when(i == 0)
        def _():
            carry_ref[...] = jnp.zeros_like(carry_ref)
            dwo_acc[...] = jnp.zeros_like(dwo_acc)
            for r in (dwuv_ref, dwpool_ref, dpscale_ref, dgt1_ref, dsc2_ref, dsh2_ref, dgffn_ref):
                r[...] = jnp.zeros_like(r)

        gt1 = mod_ref[0:1, 2 * D:3 * D]
        sc2 = mod_ref[0:1, 4 * D:5 * D]
        gffn = gffn_ref[...]
        dh2 = dh2_ref[...]
        x2 = x2_ref[...]
        r2 = _rms(x2)
        xn2 = x2 * r2
        dsc2_ref[...] += _colsum(dh2 * (xn2 * gffn))
        dsh2_ref[...] += _colsum(dh2)
        dgffn_ref[...] += _colsum(dh2 * (1.0 + sc2) * xn2)
        dx2 = dx3_ref[...] + _rms_bwd(dh2 * gffn * (1.0 + sc2), xn2, r2)
        dx2_ref[...] = dx2
        dgt1_ref[...] += _colsum(dx2 * mix_ref[...])
        dmix = (dx2 * gt1).astype(BF)
        cat = jnp.concatenate([ymla_ref[...], ypool_ref[...]], axis=1)
        dwo_acc[...] += _dot_tn(cat, dmix)
        dcat = _dot_nt(dmix, wo_ref[...])
        dymla = dcat[:, 0:512]
        dypool = dcat[:, 512:1024]

        dpscale_ref[...] += _colsum(dypool * ypre_ref[...])
        dypre = (dypool * pscale_ref[...]).astype(BF)
        pooled = pooled_ref[...]
        dpooled = []
        for g in range(GROUPS):
            sl = slice(g * GD, (g + 1) * GD)
            dwpool_ref[g] += _dot_tn(pooled[:, sl], dypre[:, sl])
            dpooled.append(_dot(dypre[:, sl], wpool_ref[g]))
        dpooled = jnp.concatenate(dpooled, axis=1)
        tile = n - 1 - i
        e = dpooled / _row_counts(tile * ts, ts)
        ext = jnp.concatenate([e, carry_ref[...]], axis=0)
        du_ref[...] = _window_sums(ext, False)[0:ts, :] - dpooled
        carry_ref[...] = e[0:16, :]

        for hd in range(HEADS):
            do = dymla[:, hd * 128:(hd + 1) * 128]
            dob = do.astype(BF)
            dol = _dot(dob, wuv_ref[hd])
            for a in range(nsub):
                ol = olat_ref[a, hd * TQ:(hd + 1) * TQ, :]
                dl = dol[a * TQ:(a + 1) * TQ, :]
                dolat_ref[a, hd * TQ:(hd + 1) * TQ, :] = dl.astype(BF)
                dwuv_ref[hd] += _dot_tn(ol, dob[a * TQ:(a + 1) * TQ, :])
                delta = jnp.sum(dl * ol, axis=-1, keepdims=True)
                delta_ref[a, :, hd * TQ:(hd + 1) * TQ] = jnp.broadcast_to(delta, (TQ, 128)).T[0:8, :]

        @pl.when(i == n - 1)
        def _():
            dwo_ref[...] = dwo_acc[...].astype(BF)

    rev = lambda c: pl.BlockSpec((ts, c), lambda i: (n - 1 - i, 0))
    rev3 = lambda r, c: pl.BlockSpec((nsub, r, c), lambda i: (n - 1 - i, 0, 0))
    out_shape = (
        jax.ShapeDtypeStruct((S, D), F32),
        jax.ShapeDtypeStruct((S, PW), F32),
        jax.ShapeDtypeStruct((S // TQ, M, KVL), BF),
        jax.ShapeDtypeStruct((S // TQ, 8, M), F32),
        jax.ShapeDtypeStruct((D, D), BF),
        jax.ShapeDtypeStruct((HEADS, KVL, 128), F32),
        jax.ShapeDtypeStruct((GROUPS, GD, GD), F32),
        jax.ShapeDtypeStruct((1, PW), F32),
        jax.ShapeDtypeStruct((1, D), F32), jax.ShapeDtypeStruct((1, D), F32), jax.ShapeDtypeStruct((1, D), F32),
        jax.ShapeDtypeStruct((1, D), F32),
    )
    in_specs = [rev(D), rev(D), rev(D), rev(D), _full(mod.shape), _full((1, D)), rev(PW), rev(PW), _full(w_o.shape),
                rev(PW), rev(PW), _full((1, PW)), _full(wpool_dc.shape), rev3(M, KVL), _full(wuv_vc.shape)]
    out_specs = (rev(D), rev(PW), rev3(M, KVL), rev3(8, M), _full((D, D)), _full((HEADS, KVL, 128)),
                 _full((GROUPS, GD, GD)), _full((1, PW)), _full((1, D)), _full((1, D)), _full((1, D)), _full((1, D)))
    return pl.pallas_call(
        body, name="mix_bwd", out_shape=out_shape, grid=(n,), in_specs=in_specs, out_specs=out_specs,
        scratch_shapes=[pltpu.VMEM((16, PW), F32), pltpu.VMEM((D, D), F32)],
        compiler_params=_params(("arbitrary",)),
    )(dh2, dx3, x2, mix, mod, g_ffn, ymla, ypool, w_o, ypre, pooled, pool_scale, wpool_dc, olat, wuv_vc)


def _attn_bwd(qs, kv, dolat, lse, delta):
    nq = qs.shape[0]
    S = kv.shape[0]
    M = HEADS * TQ
    nk = S // TK

    def body(qs_ref, kv_ref, do_ref, lse_ref, delta_ref, dkv_ref, dqt_ref):
        kt = pl.program_id(0)
        k = kv_ref[...]
        v = k[:, 0:KVL]
        k_t = k.astype(F32).T.astype(BF)

        @pl.when(kt == 0)
        def _():
            dqt_ref[...] = jnp.zeros_like(dqt_ref)

        def step(qi, carry, masked):
            dk, dv = carry
            q = qs_ref[qi]
            do = do_ref[qi]
            s = _dot_nt(k, q) * SM_SCALE
            p = jnp.exp(s - lse_ref[qi, 0:1, :])
            if masked:
                p = jnp.where(_diag_mask((TK, M), 1), p, 0.0)
            dp = _dot_nt(v, do)
            ds = (p * (dp - delta_ref[qi, 0:1, :]) * SM_SCALE).astype(BF)
            dv = dv + _dot(p.astype(BF), do)
            dk = dk + _dot(ds, q)
            dqt_ref[qi] += _dot(k_t, ds)
            return dk, dv

        carry = step(kt, (jnp.zeros((TK, QW), F32), jnp.zeros((TK, KVL), F32)), True)
        dk, dv = lax.fori_loop(kt + 1, nq, lambda qi, c: step(qi, c, False), carry)
        dkv_ref[...] = dk + jnp.concatenate([dv, jnp.zeros((TK, QW - KVL), F32)], axis=1)

    out_shape = (jax.ShapeDtypeStruct((S, QW), F32), jax.ShapeDtypeStruct((nq, QW, M), F32))
    return pl.pallas_call(
        body, name="attn_bwd", out_shape=out_shape, grid=(nk,),
        in_specs=[_vmem(), _rows(TK, QW), _vmem(), _vmem(), _vmem()],
        out_specs=(_rows(TK, QW), _vmem()),
        compiler_params=_params(("arbitrary",)),
    )(qs, kv, dolat, lse, delta)


def _in_bwd(dqt, dkv, du, raw, qn, h1, x, dx2, mod, g_mix, w_in, g_q, g_kv, w_uq, wuk_cd, perm_t, cos4, sin4, csk,
            snk):
    S = x.shape[0]
    ts = 512
    n = S // ts
    nsub = ts // TQ
    M = HEADS * TQ

    def body(dqt_ref, dkv_ref, du_ref, raw_ref, qn_ref, h1_ref, x_ref, dx2_ref, mod_ref, gmix_ref, win_ref, gq_ref,
             gkv_ref, wuq_ref, wuk_ref, permt_ref, cos_ref, sin_ref, csk_ref, snk_ref,
             dx_ref, dwin_ref, dwuq_ref, dwuk_ref, dgq_ref, dgkv_ref, dsc1_ref, dsh1_ref, dgmix_ref, dwin_acc,
             dwuq_acc):
        i = pl.program_id(0)

        @pl.when(i == 0)
        def _():
            dwin_acc[...] = jnp.zeros_like(dwin_acc)
            dwuq_acc[...] = jnp.zeros_like(dwuq_acc)
            for r in (dwuk_ref, dgq_ref, dgkv_ref, dsc1_ref, dsh1_ref, dgmix_ref):
                r[...] = jnp.zeros_like(r)

        dq_blocks = [dqt_ref[a].T for a in range(nsub)]
        qn = qn_ref[...]
        dq_parts = []
        drope = jnp.zeros((ts, 2 * 128), F32)
        for hd in range(HEADS):
            dqh = jnp.concatenate([blk[hd * TQ:(hd + 1) * TQ, :] for blk in dq_blocks], axis=0)
            dq_lat = dqh[:, 0:KVL].astype(BF)
            dq_parts.append(_dot(dq_lat, wuk_ref[hd]))
            dwuk_ref[hd] += _dot_tn(dq_lat, qn[:, hd * NOPE:(hd + 1) * NOPE])
            drope = drope + _dot(dqh[:, KVL:QW].astype(BF), permt_ref[hd])
        do1 = drope[:, 0:128]
        do2 = drope[:, 128:256]
        cosv = cos_ref[...]
        sinv = sin_ref[...]
        dq_parts.append(do1 * cosv + do2 * sinv)
        dq_parts.append(do2 * cosv - do1 * sinv)
        dq = jnp.concatenate(dq_parts, axis=1).astype(BF)

        cq_raw = raw_ref[:, 0:QL]
        ckv_raw = raw_ref[:, QL:QL + KVL]
        rq = _rms(cq_raw)
        nq_ = cq_raw * rq
        gq = gq_ref[...]
        dwuq_acc[...] += _dot_tn((nq_ * gq).astype(BF), dq)
        dc_q = _dot_nt(dq, wuq_ref[...])
        dgq_ref[...] += _colsum(dc_q * nq_)
        dcq_raw = _rms_bwd(dc_q * gq, nq_, rq)

        dkv = dkv_ref[...]
        rk = _rms(ckv_raw)
        nk_ = ckv_raw * rk
        dc_kv = dkv[:, 0:KVL]
        dgkv_ref[...] += _colsum(dc_kv * nk_)
        dckv_raw = _rms_bwd(dc_kv * gkv_ref[...], nk_, rk)
        dkr_roped = dkv[:, KVL:QW]
        dkr = dkr_roped * csk_ref[...] - _swap_halves(dkr_roped) * snk_ref[...]

        dproj = jnp.concatenate([dcq_raw, dckv_raw, dkr, du_ref[...]], axis=1).astype(BF)
        dwin_acc[...] += _dot_tn(h1_ref[...], dproj)
        dh1 = _dot_nt(dproj, win_ref[...])

        sc1 = mod_ref[0:1, D:2 * D]
        gmix = gmix_ref[...]
        xv = x_ref[...]
        r1 = _rms(xv)
        xn1 = xv * r1
        dsc1_ref[...] += _colsum(dh1 * (xn1 * gmix))
        dsh1_ref[...] += _colsum(dh1)
        dgmix_ref[...] += _colsum(dh1 * (1.0 + sc1) * xn1)
        dx_ref[...] = dx2_ref[...] + _rms_bwd(dh1 * gmix * (1.0 + sc1), xn1, r1)

        @pl.when(i == n - 1)
        def _():
            dwin_ref[...] = dwin_acc[...].astype(BF)
            dwuq_ref[...] = dwuq_acc[...].astype(BF)

    out_shape = (
        jax.ShapeDtypeStruct((S, D), F32),
        jax.ShapeDtypeStruct((D, D), BF),
        jax.ShapeDtypeStruct((QL, 768), BF),
        jax.ShapeDtypeStruct((HEADS, KVL, NOPE), F32),
        jax.ShapeDtypeStruct((1, QL), F32), jax.ShapeDtypeStruct((1, KVL), F32),
        jax.ShapeDtypeStruct((1, D), F32), jax.ShapeDtypeStruct((1, D), F32), jax.ShapeDtypeStruct((1, D), F32),
    )
    in_specs = [pl.BlockSpec((nsub, QW, M), lambda i: (i, 0, 0)), _rows(ts, QW), _rows(ts, PW), _rows(ts, 384),
                _rows(ts, HEADS * NOPE), _rows(ts, D), _rows(ts, D), _rows(ts, D), _full(mod.shape), _full((1, D)),
                _full(w_in.shape), _full((1, QL)), _full((1, KVL)), _full(w_uq.shape), _full(wuk_cd.shape),
                _full(perm_t.shape), _rows(ts, 128), _rows(ts, 128), _rows(ts, 128), _rows(ts, 128)]
    out_specs = (_rows(ts, D), _full((D, D)), _full((QL, 768)), _full((HEADS, KVL, NOPE)), _full((1, QL)),
                 _full((1, KVL)), _full((1, D)), _full((1, D)), _full((1, D)))
    return pl.pallas_call(
        body, name="in_bwd", out_shape=out_shape, grid=(n,), in_specs=in_specs, out_specs=out_specs,
        scratch_shapes=[pltpu.VMEM((D, D), F32), pltpu.VMEM((QL, 768), F32)],
        compiler_params=_params(("arbitrary",)),
    )(dqt, dkv, du, raw, qn, h1, x, dx2, mod, g_mix, w_in, g_q, g_kv, w_uq, wuk_cd, perm_t, cos4, sin4, csk, snk)


def _rope_perm():
    p = np.zeros((HEADS, 2 * 128, 128), np.float32)
    for hd in range(HEADS):
        for t in range(HALF):
            p[hd, hd * HALF + t, t] = 1.0
            p[hd, 128 + hd * HALF + t, HALF + t] = 1.0
    return p


def _rope_tables(positions):
    freqs = jnp.power(ROPE_THETA, -jnp.arange(HALF, dtype=F32) / HALF)
    ang = positions.astype(F32)[:, None] * freqs
    cos = jnp.cos(ang)
    sin = jnp.sin(ang)
    zero = jnp.zeros_like(cos)
    cos4 = jnp.tile(cos, (1, HEADS))
    sin4 = jnp.tile(sin, (1, HEADS))
    csk = jnp.concatenate([cos, cos, zero, zero], axis=1)
    snk = jnp.concatenate([-sin, sin, zero, zero], axis=1)
    return cos4, sin4, csk, snk


def _local_step(x, positions, target, mod, g_mix, w_in_p, g_q, g_kv, w_uq_p, w_uk, w_uv, w_pool, pool_scale, g_ffn,
                g_final, late, ffn_grads_exchange):
    perm = jnp.asarray(_rope_perm(), BF)
    perm_t = jnp.asarray(_rope_perm().transpose(0, 2, 1), BF)
    cos4, sin4, csk, snk = _rope_tables(positions)
    wuk_dc = w_uk.transpose(1, 2, 0).astype(BF)
    wuk_cd = w_uk.transpose(1, 0, 2).astype(BF)
    wuv_cv = w_uv.transpose(1, 0, 2).astype(BF)
    wuv_vc = w_uv.transpose(1, 2, 0).astype(BF)
    wpool = w_pool.astype(BF)
    wpool_dc = w_pool.transpose(0, 2, 1).astype(BF)

    h1, raw, qn, qs, kv, pooled, ypre, ypool = _fwd_in(
        x, mod, g_mix, w_in_p, g_q, g_kv, w_uq_p, wuk_dc, perm, cos4, sin4, csk, snk, wpool, pool_scale)
    olat, ymla, lse = _attn_fwd(qs, kv, wuv_cv)
    w_o, wg_t, wu_t, wd = late
    x2, mix, h2, a, b, dx3, dff, loss, dgfin, dgt2 = _ffn_fwd(
        x, ymla, ypool, mod, w_o, g_ffn, wg_t, wu_t, wd, g_final, target)
    dwg_t, dwu_t, dwd, dh2 = _ffn_bwd(dff, h2, a, b, wg_t, wu_t, wd)
    ffn_parts = ffn_grads_exchange((dwg_t, dwu_t, dwd))
    (dx2, du, dolat, delta, dwo, dwuv, dwpool, dpscale, dgt1, dsc2, dsh2, dgffn) = _mix_bwd(
        dh2, dx3, x2, mix, mod, g_ffn, ymla, ypool, w_o, ypre, pooled, pool_scale, wpool_dc, olat, wuv_vc)
    dkv, dqt = _attn_bwd(qs, kv, dolat, lse, delta)
    dx, dwin, dwuq, dwuk, dgq, dgkv, dsc1, dsh1, dgmix = _in_bwd(
        dqt, dkv, du, raw, qn, h1, x, dx2, mod, g_mix, w_in_p, g_q, g_kv, w_uq_p, wuk_cd, perm_t, cos4, sin4, csk,
        snk)
    dmod = jnp.concatenate([dsh1, dsc1, dgt1, dsh2, dsc2, dgt2], axis=1)
    parts = (*_scatter_partials((dwin, dwuq, dwo)), *ffn_parts)
    replicated = dict(
        w_uk=dwuk.transpose(1, 0, 2), w_uv=dwuv.transpose(1, 0, 2), w_pool=dwpool, g_mix=dgmix, g_q=dgq, g_kv=dgkv,
        pool_scale=dpscale, g_ffn=dgffn, g_final=dgfin)
    return loss[0, 0], dx, dmod, parts, replicated


def _my_pos():
    return lax.axis_index("x"), lax.axis_index("y"), lax.axis_index("c")


def _peer(pos, k):
    x, y, c = pos
    return (1 - x if k & 4 else x, 1 - y if k & 2 else y, 1 - c if k & 1 else c)


def _index(pos):
    x, y, c = pos
    return 4 * x + 2 * y + c


def _remote(src, dst, send_sem, recv_sem, to):
    return pltpu.make_async_remote_copy(src_ref=src, dst_ref=dst, send_sem=send_sem, recv_sem=recv_sem,
                                        device_id=to, device_id_type=MESH)


def _ada_mod(c, w_ada, b_ada):
    def body(c_ref, w_ref, b_ref, mod_ref, call_ref, cbuf, sbuf, rbuf, send1, recv1, send2, recv2):
        me = _my_pos()
        mi = _index(me)
        cv = c_ref[...]
        cbuf[...] = jnp.broadcast_to(cv * jax.nn.sigmoid(cv), (8, D))
        call_ref[mi] = cbuf[...]
        first = [_remote(cbuf, call_ref.at[mi], send1.at[k - 1], recv1.at[k - 1], _peer(me, k)) for k in range(1, NDEV)]
        for cp in first:
            cp.start()
        for k in range(1, NDEV):
            _remote(cbuf, call_ref.at[_index(_peer(me, k))], send1.at[k - 1], recv1.at[k - 1], _peer(me, k)).wait_recv()
        c_all = jnp.concatenate([call_ref[b][0:1, :] for b in range(NDEV)], axis=0)
        blocks = _dot(c_all.astype(BF), w_ref[...].astype(BF))
        for b in range(NDEV):
            sbuf[b] = jnp.broadcast_to(blocks[b:b + 1, :], (8, MODC))
        second = []
        for k in range(1, NDEV):
            to = _peer(me, k)
            second.append(_remote(sbuf.at[_index(to)], rbuf.at[mi], send2.at[k - 1], recv2.at[k - 1], to))
        for cp in second:
            cp.start()
        rbuf[mi] = sbuf[mi]
        for k in range(1, NDEV):
            to = _peer(me, k)
            _remote(sbuf.at[_index(to)], rbuf.at[_index(to)], send2.at[k - 1], recv2.at[k - 1], to).wait_recv()
        for j in range(NDEV):
            mod_ref[:, j * MODC:(j + 1) * MODC] = rbuf[j] + b_ref[:, j * MODC:(j + 1) * MODC]
        for cp in first + second:
            cp.wait_send()

    return pl.pallas_call(
        body, name="ada_mod",
        out_shape=(jax.ShapeDtypeStruct((8, N_MOD * D), F32), jax.ShapeDtypeStruct((NDEV, 8, D), F32)),
        in_specs=[_vmem(), _vmem(), _vmem()], out_specs=(_vmem(), _vmem()),
        scratch_shapes=[pltpu.VMEM((8, D), F32), pltpu.VMEM((NDEV, 8, MODC), F32), pltpu.VMEM((NDEV, 8, MODC), F32),
                        pltpu.SemaphoreType.DMA((NDEV - 1,)), pltpu.SemaphoreType.DMA((NDEV - 1,)),
                        pltpu.SemaphoreType.DMA((NDEV - 1,)), pltpu.SemaphoreType.DMA((NDEV - 1,))],
        compiler_params=_params(),
    )(c, w_ada, b_ada)


def _all_gather(shards):
    n = len(shards)

    def body(*refs):
        ins, outs = refs[:n], refs[n:2 * n]
        send, recv, local = refs[2 * n:]
        me = _my_pos()
        mi = _index(me)
        own = [pltpu.make_async_copy(ins[a], outs[a].at[mi], local.at[a]) for a in range(n)]
        for cp in own:
            cp.start()
        sent = []
        for a in range(n):
            for k in range(1, NDEV):
                sent.append(_remote(ins[a], outs[a].at[mi], send.at[a, k - 1], recv.at[a, k - 1], _peer(me, k)))
        for cp in sent:
            cp.start()
        for a in range(n):
            for k in range(1, NDEV):
                to = _peer(me, k)
                _remote(ins[a], outs[a].at[_index(to)], send.at[a, k - 1], recv.at[a, k - 1], to).wait_recv()
        for cp in sent:
            cp.wait_send()
        for cp in own:
            cp.wait()

    return pl.pallas_call(
        body, name="gather_weights",
        out_shape=tuple(jax.ShapeDtypeStruct((NDEV,) + s.shape, s.dtype) for s in shards),
        in_specs=[_any()] * n, out_specs=tuple([_any()] * n),
        scratch_shapes=[pltpu.SemaphoreType.DMA((n, NDEV - 1)), pltpu.SemaphoreType.DMA((n, NDEV - 1)),
                        pltpu.SemaphoreType.DMA((n,))],
        compiler_params=_params(),
    )(*shards)


def _scatter_partials(grads):
    n = len(grads)
    of = _exchange_slices(True)

    def body(*refs):
        ins, outs = refs[:n], refs[n:2 * n]
        send, recv, local = refs[2 * n:]
        me = _my_pos()
        mi = _index(me)
        own = [pltpu.make_async_copy(of(ins[a], mi), outs[a].at[mi], local.at[a]) for a in range(n)]
        for cp in own:
            cp.start()
        sent = []
        for a in range(n):
            for k in range(1, NDEV):
                to = _peer(me, k)
                s = a * (NDEV - 1) + k - 1
                sent.append(_remote(of(ins[a], _index(to)), outs[a].at[mi], send.at[s], recv.at[s], to))
        for cp in sent:
            cp.start()
        for a in range(n):
            for k in range(1, NDEV):
                to = _peer(me, k)
                s = a * (NDEV - 1) + k - 1
                _remote(of(ins[a], mi), outs[a].at[_index(to)], send.at[s], recv.at[s], to).wait_recv()
        for cp in sent:
            cp.wait_send()
        for cp in own:
            cp.wait()

    return pl.pallas_call(
        body, name="scatter_grads",
        out_shape=tuple(jax.ShapeDtypeStruct((NDEV, g.shape[0] // NDEV, g.shape[1]), g.dtype) for g in grads),
        in_specs=[_any()] * n, out_specs=tuple([_any()] * n),
        scratch_shapes=[pltpu.SemaphoreType.DMA((n * (NDEV - 1),)), pltpu.SemaphoreType.DMA((n * (NDEV - 1),)),
                        pltpu.SemaphoreType.DMA((n,))],
        compiler_params=_params(),
    )(*grads)


def _exchange_slices(scatter):
    def of(src, to_index):
        if not scatter:
            return src
        r = src.shape[0] // NDEV
        return src.at[pl.ds(pl.multiple_of(to_index * r, 16), r), :]
    return of


def _sequencer_exchange(name, collective_id, srcs, scatter, after=()):
    n = len(srcs)
    of = _exchange_slices(scatter)
    src_refs = [jax.new_ref(s, memory_space=pltpu.MemorySpace.HBM) for s in srcs]
    zone_refs = [
        jax.empty_ref(jax.ShapeDtypeStruct((NDEV, s.shape[0] // NDEV if scatter else s.shape[0], s.shape[1]), s.dtype),
                      memory_space=pltpu.MemorySpace.HBM) for s in srcs]
    after_refs = [jax.new_ref(t, memory_space=pltpu.MemorySpace.HBM) for t in after]
    seen_refs = [jax.empty_ref(jax.ShapeDtypeStruct(t.shape, t.dtype), memory_space=pltpu.MemorySpace.HBM)
                 for t in after]

    @pl.kernel(mesh=plsc.ScalarSubcoreMesh(axis_name="sequencer", num_cores=1), name=name,
               scratch_types=(pltpu.SemaphoreType.DMA((n * (NDEV - 1),)), pltpu.SemaphoreType.DMA((n * (NDEV - 1),)),
                              pltpu.SemaphoreType.DMA((n + len(after),))),
               compiler_params=pltpu.CompilerParams(collective_id=collective_id))
    def launch(send, recv, local):
        me = _my_pos()
        mi = _index(me)
        barrier = pltpu.get_barrier_semaphore()
        for k in range(1, NDEV):
            pl.semaphore_signal(barrier, inc=1, device_id=_peer(me, k), device_id_type=MESH)
        pl.semaphore_wait(barrier, NDEV - 1)
        own = [pltpu.make_async_copy(of(src_refs[a], mi), zone_refs[a].at[mi], local.at[a]) for a in range(n)]
        own += [pltpu.make_async_copy(t, seen, local.at[n + q])
                for q, (t, seen) in enumerate(zip(after_refs, seen_refs))]
        for cp in own:
            cp.start()
        for a in range(n):
            for k in range(1, NDEV):
                to = _peer(me, k)
                s = a * (NDEV - 1) + k - 1
                _remote(of(src_refs[a], _index(to)), zone_refs[a].at[mi], send.at[s], recv.at[s], to).start()
        for cp in own:
            cp.wait()
        for a in range(n):
            for k in range(1, NDEV):
                to = _peer(me, k)
                s = a * (NDEV - 1) + k - 1
                cp = _remote(of(src_refs[a], mi), zone_refs[a].at[_index(to)], send.at[s], recv.at[s], to)
                cp.wait_send()
                cp.wait_recv()

    launch()
    return tuple(z[...] for z in zone_refs)


def _sum_partials(parts):
    n = len(parts)

    def body(*refs):
        for a in range(n):
            acc = refs[a][0].astype(F32)
            for p in range(1, NDEV):
                acc = acc + refs[a][p].astype(F32)
            refs[n + a][...] = acc

    return pl.pallas_call(
        body, name="sum_partials",
        out_shape=tuple(jax.ShapeDtypeStruct(p.shape[1:], F32) for p in parts),
        in_specs=[_vmem()] * n, out_specs=tuple([_vmem()] * n), compiler_params=_params(),
    )(*parts)


def _small_all_reduce(buf):
    def body(buf_ref, got_ref, red_ref, mine, send1, recv1, send2, recv2):
        me = _my_pos()
        mi = _index(me)
        first = []
        for k in range(1, NDEV):
            to = _peer(me, k)
            first.append(_remote(buf_ref.at[_index(to)], got_ref.at[mi], send1.at[k - 1], recv1.at[k - 1], to))
        for cp in first:
            cp.start()
        got_ref[mi] = buf_ref[mi]
        for k in range(1, NDEV):
            to = _peer(me, k)
            _remote(buf_ref.at[mi], got_ref.at[_index(to)], send1.at[k - 1], recv1.at[k - 1], to).wait_recv()
        acc = got_ref[0]
        for p in range(1, NDEV):
            acc = acc + got_ref[p]
        mine[...] = acc
        second = [_remote(mine, red_ref.at[mi], send2.at[k - 1], recv2.at[k - 1], _peer(me, k)) for k in range(1, NDEV)]
        for cp in second:
            cp.start()
        red_ref[mi] = acc
        for k in range(1, NDEV):
            to = _peer(me, k)
            _remote(mine, red_ref.at[_index(to)], send2.at[k - 1], recv2.at[k - 1], to).wait_recv()
        for cp in first + second:
            cp.wait_send()

    return pl.pallas_call(
        body, name="small_all_reduce",
        out_shape=(jax.ShapeDtypeStruct(buf.shape, F32), jax.ShapeDtypeStruct(buf.shape, F32)),
        in_specs=[_vmem()], out_specs=(_vmem(), _vmem()),
        scratch_shapes=[pltpu.VMEM(buf.shape[1:], F32),
                        pltpu.SemaphoreType.DMA((NDEV - 1,)), pltpu.SemaphoreType.DMA((NDEV - 1,)),
                        pltpu.SemaphoreType.DMA((NDEV - 1,)), pltpu.SemaphoreType.DMA((NDEV - 1,))],
        compiler_params=_params(),
    )(buf)


def _adamw_math(w, g, m, v):
    m = ADAM_B1 * m + (1.0 - ADAM_B1) * g
    v = ADAM_B2 * v + (1.0 - ADAM_B2) * jnp.square(g)
    m_hat = m / (1.0 - ADAM_B1 ** ADAM_STEP)
    v_hat = v / (1.0 - ADAM_B2 ** ADAM_STEP)
    delta = -ADAM_LR * (m_hat / (jnp.sqrt(v_hat) + ADAM_EPS) + ADAM_WD * w)
    return delta, m, v


def _adamw_group(name, ws, gs, ms, vs):
    n = len(ws)

    def body(*refs):
        for a in range(n):
            w, g, m, v = (refs[q * n + a][...] for q in range(4))
            delta, m2, v2 = _adamw_math(w, g, m, v)
            refs[4 * n + a][...] = delta
            refs[5 * n + a][...] = m2
            refs[6 * n + a][...] = v2

    shapes = tuple(jax.ShapeDtypeStruct(w.shape, F32) for w in ws)
    outs = pl.pallas_call(
        body, name=name, out_shape=shapes * 3, in_specs=[_vmem()] * (4 * n), out_specs=tuple([_vmem()] * (3 * n)),
        compiler_params=_params(),
    )(*ws, *gs, *ms, *vs)
    return outs[:n], outs[n:2 * n], outs[2 * n:]


def _adamw_ada(w, m, v, c_all, dmod_rows):
    def body(w_ref, m_ref, v_ref, c_ref, dm_ref, g_ref, d_ref, m2_ref, v2_ref):
        g = _dot_tn(c_ref[...], dm_ref[...].astype(BF))
        g_ref[...] = g
        delta, m2, v2 = _adamw_math(w_ref[...], g, m_ref[...], v_ref[...])
        d_ref[...] = delta
        m2_ref[...] = m2
        v2_ref[...] = v2

    shp = jax.ShapeDtypeStruct(w.shape, F32)
    return pl.pallas_call(
        body, name="adamw_ada", out_shape=(shp, shp, shp, shp), in_specs=[_vmem()] * 5,
        out_specs=tuple([_vmem()] * 4), compiler_params=_params(),
    )(w, m, v, c_all, dmod_rows)


def _w_in_to_kernel(w):
    return jnp.concatenate([w[:, 0:448], jnp.zeros((w.shape[0], 64), w.dtype), w[:, 448:960]], axis=1)


def _w_in_from_kernel(w):
    return jnp.concatenate([w[:, 0:448], w[:, 512:1024]], axis=1)


def _w_uq_to_kernel(w):
    r = w.shape[0]
    return jnp.concatenate([w[:, :, 0:NOPE].reshape(r, HEADS * NOPE),
                            w[:, :, NOPE:NOPE + HALF].reshape(r, HEADS * HALF),
                            w[:, :, NOPE + HALF:].reshape(r, HEADS * HALF)], axis=1)


def _w_uq_from_kernel(w):
    r = w.shape[0]
    return jnp.concatenate([w[:, 0:512].reshape(r, HEADS, NOPE), w[:, 512:640].reshape(r, HEADS, HALF),
                            w[:, 640:768].reshape(r, HEADS, HALF)], axis=2)


REP_NAMES = ("w_uk", "w_uv", "w_pool", "g_mix", "g_q", "g_kv", "pool_scale", "g_ffn", "g_final")


def kernel(x, c, positions, w_ada, b_ada, g_mix, w_in, g_q, g_kv, w_uq, w_uk, w_uv, w_pool, pool_scale, w_o, g_ffn, w_gate, w_up, w_down, g_final, loss_target, m_w_ada, m_b_ada, m_g_mix, m_w_in, m_g_q, m_g_kv, m_w_uq, m_w_uk, m_w_uv, m_w_pool, m_pool_scale, m_w_o, m_g_ffn, m_w_gate, m_w_up, m_w_down, m_g_final, v_w_ada, v_b_ada, v_g_mix, v_w_in, v_g_q, v_g_kv, v_w_uq, v_w_uk, v_w_uv, v_w_pool, v_pool_scale, v_w_o, v_g_ffn, v_w_gate, v_w_up, v_w_down, v_g_final):
    given = dict(locals())

    merge = lambda g: g.reshape(NDEV * g.shape[1], g.shape[2])
    w_in_p, w_uq_p = (merge(g) for g in _all_gather(
        (_w_in_to_kernel(w_in[0]).astype(BF), _w_uq_to_kernel(w_uq[0]).astype(BF))))

    mod, c_all8 = _ada_mod(c, w_ada[0], b_ada)
    c_all = c_all8[:, 0, :]
    late = _sequencer_exchange(
        "gather_late", 1, (w_o[0].astype(BF), w_gate[0].T.astype(BF), w_up[0].T.astype(BF), w_down[0].astype(BF)),
        False, after=(mod[:, 0:128], w_in_p[0:16, 0:128], w_uq_p[0:16, 0:128]))

    def ffn_grads_exchange(arrays):
        return _sequencer_exchange("scatter_ffn", 2, arrays, True)

    loss, dx, dmod, parts, replicated = _local_step(
        x[0], positions[0], loss_target[0], mod, g_mix, w_in_p, g_q, g_kv, w_uq_p, w_uk[0], w_uv[0], w_pool[0],
        pool_scale, g_ffn, g_final.reshape(1, D), tuple(merge(g) for g in late), ffn_grads_exchange)

    g_in_p, g_uq_p, g_o, g_gate_t, g_up_t, g_down = _sum_partials(parts)
    grads = dict(w_in=_w_in_from_kernel(g_in_p), w_uq=_w_uq_from_kernel(g_uq_p).reshape(QL // NDEV, HEADS * 192),
                 w_o=g_o, w_gate=g_gate_t.T, w_up=g_up_t.T, w_down=g_down)

    flat = jnp.concatenate([replicated[k].reshape(-1) for k in REP_NAMES])
    flat = jnp.pad(flat, (0, NDEV * REP_ROWS * 128 - flat.shape[0])).reshape(NDEV, REP_ROWS, 128)
    dmod_blocks = jnp.pad(dmod.reshape(NDEV, MODC // 128, 128), ((0, 0), (0, MOD_ROWS - MODC // 128), (0, 0)))
    got, red = _small_all_reduce(jnp.concatenate([dmod_blocks, flat], axis=1))
    dmod_rows = got[:, 0:MODC // 128, :].reshape(NDEV, MODC)
    grads["b_ada"] = red[:, 0:MODC // 128, :].reshape(1, N_MOD * D)
    rep_flat = red[:, MOD_ROWS:, :].reshape(-1)
    off = 0
    for k in REP_NAMES:
        size = int(np.prod(given[k].shape))
        grads[k] = rep_flat[off:off + size]
        off += size

    view = dict(w_ada=(D, MODC), b_ada=(1, N_MOD * D), g_mix=(1, D), w_in=(D // NDEV, 960), g_q=(1, QL),
                g_kv=(1, KVL), w_uq=(QL // NDEV, HEADS * 192), w_uk=(KVL, HEADS * NOPE), w_uv=(KVL, HEADS * 128),
                w_pool=(GROUPS * GD, GD), pool_scale=(1, PW), w_o=(D // NDEV, D), g_ffn=(1, D),
                w_gate=(D, FF // NDEV), w_up=(D, FF // NDEV), w_down=(FF // NDEV, D), g_final=(1, D))
    names = list(view)
    g_ada, d_ada, m_ada, v_ada = _adamw_ada(w_ada[0], m_w_ada[0], v_w_ada[0], c_all.astype(BF), dmod_rows)
    out_g, out_d, out_m, out_v = dict(w_ada=g_ada), dict(w_ada=d_ada), dict(w_ada=m_ada), dict(w_ada=v_ada)
    groups = (("adamw_ffn", ("w_gate", "w_up", "w_down")),
              ("adamw_rest", tuple(k for k in names if k not in ("w_ada", "w_gate", "w_up", "w_down"))))
    for gname, members in groups:
        ws = [given[k].reshape(view[k]) for k in members]
        gs = [grads[k].reshape(view[k]) for k in members]
        ms = [given["m_" + k].reshape(view[k]) for k in members]
        vs = [given["v_" + k].reshape(view[k]) for k in members]
        ds, m2, v2 = _adamw_group(gname, ws, gs, ms, vs)
        for k, g, d, mm, vv in zip(members, gs, ds, m2, v2):
            out_g[k], out_d[k], out_m[k], out_v[k] = g, d, mm, vv

    total = lax.psum(loss, ("x", "y", "c"))
    shaped = lambda d: [d[k].reshape(given[k].shape) for k in names]
    return (total, dx[None], *shaped(out_g), *shaped(out_d), *shaped(out_m), *shaped(out_v))
```

```python
import numpy as np
import jax
import jax.numpy as jnp
from jax import lax
from jax.experimental import pallas as pl
from jax.experimental.pallas import tpu as pltpu
from jax.experimental.pallas import tpu_sc as plsc

D = 1024
HEADS = 4
NOPE = 128
ROPE = 64
HALF = ROPE // 2
QL = 256
KVL = 128
FF = 2816
PW = 512
GROUPS = 4
GD = 128
N_MOD = 6
EPS = 1e-6
SM_SCALE = (NOPE + ROPE) ** -0.5
ROPE_THETA = 10000.0
NDEV = 8
MODC = N_MOD * D // NDEV

ADAM_LR = 0.001
ADAM_B1 = 0.9
ADAM_B2 = 0.999
ADAM_EPS = 1e-08
ADAM_WD = 0.01
ADAM_STEP = 10

BF = jnp.bfloat16
F32 = jnp.float32
VMEM_LIMIT_V7X = 60 * 1024 * 1024
MESH = pl.DeviceIdType.MESH

TQ = 256
TK = 256
QW = 256
MOD_ROWS = 8
REP_ROWS = 200
SMALL_ROWS = MOD_ROWS + REP_ROWS


def _params(sem=None):
    return pltpu.CompilerParams(dimension_semantics=sem, vmem_limit_bytes=VMEM_LIMIT_V7X)


def _dot(a, b):
    return jnp.dot(a, b, preferred_element_type=F32)


def _dot_nt(a, b):
    return lax.dot_general(a, b, (((1,), (1,)), ((), ())), preferred_element_type=F32)


def _dot_tn(a, b):
    return _dot(a.astype(F32).T.astype(BF), b)


def _full(shape):
    return pl.BlockSpec(shape, lambda *_: (0,) * len(shape))


def _rows(ts, cols):
    return pl.BlockSpec((ts, cols), lambda i: (i, 0))


def _vmem():
    return pl.BlockSpec(memory_space=pltpu.VMEM)


def _any():
    return pl.BlockSpec(memory_space=pl.ANY)


def _rms(v):
    return lax.rsqrt(jnp.mean(v * v, axis=-1, keepdims=True) + EPS)


def _rms_bwd(dn, n, r):
    return r * (dn - n * jnp.mean(dn * n, axis=-1, keepdims=True))


def _colsum(v):
    return jnp.sum(v, axis=0, keepdims=True)


def _swap_halves(v):
    lane = lax.broadcasted_iota(jnp.int32, v.shape, 1)
    return jnp.where(lane < HALF, pltpu.roll(v, 128 - HALF, 1), pltpu.roll(v, HALF, 1))


def _window_lane_width():
    lane = lax.broadcasted_iota(jnp.int32, (1, PW), 1)
    return jnp.where(lane < 128, 2.0, jnp.where(lane < 256, 4.0, jnp.where(lane < 384, 8.0, 16.0))).astype(F32)


def _window_sums(ext, back):
    n = ext.shape[0]

    def sh(v, k):
        return pltpu.roll(v, k if back else n - k, 0)

    s2 = ext + sh(ext, 1)
    e4 = s2[:, 128:]
    s4 = e4 + sh(e4, 2)
    e8 = s4[:, 128:]
    s8 = e8 + sh(e8, 4)
    e16 = s8[:, 128:]
    s16 = e16 + sh(e16, 8)
    return jnp.concatenate([s2[:, :128], s4[:, :128], s8[:, :128], s16], axis=1)


def _row_counts(first_row, ts):
    t1 = (first_row + lax.broadcasted_iota(jnp.int32, (ts, 1), 0) + 1).astype(F32)
    return jnp.minimum(t1, _window_lane_width())


def _fwd_in(x, mod, g_mix, w_in, g_q, g_kv, w_uq, wuk_dc, perm, cos4, sin4, csk, snk, w_pool, pool_scale):
    S = x.shape[0]
    ts = 512
    nsub = ts // TQ

    def body(x_ref, mod_ref, gmix_ref, win_ref, gq_ref, gkv_ref, wuq_ref, wuk_ref, perm_ref, cos_ref, sin_ref,
             csk_ref, snk_ref, wpool_ref, pscale_ref,
             h1_ref, raw_ref, qn_ref, qs_ref, kv_ref, pooled_ref, ypre_ref, ypool_ref, carry_ref):
        i = pl.program_id(0)

        @pl.when(i == 0)
        def _():
            carry_ref[...] = jnp.zeros_like(carry_ref)

        xv = x_ref[...]
        sh1 = mod_ref[0:1, 0:D]
        sc1 = mod_ref[0:1, D:2 * D]
        h = (xv * _rms(xv)) * gmix_ref[...] * (1.0 + sc1) + sh1
        hb = h.astype(BF)
        h1_ref[...] = hb
        proj = _dot(hb, win_ref[...])
        cq_raw = proj[:, 0:QL]
        ckv_raw = proj[:, QL:QL + KVL]
        kr = proj[:, 384:512]
        u = proj[:, 512:1024]
        raw_ref[...] = proj[:, 0:384]

        c_q = (cq_raw * _rms(cq_raw)) * gq_ref[...]
        c_kv = (ckv_raw * _rms(ckv_raw)) * gkv_ref[...]
        q = _dot(c_q.astype(BF), wuq_ref[...])
        qn = q[:, 0:HEADS * NOPE].astype(BF)
        qn_ref[...] = qn
        x1 = q[:, 512:640]
        x2 = q[:, 640:768]
        cosv = cos_ref[...]
        sinv = sin_ref[...]
        roped = jnp.concatenate([x1 * cosv - x2 * sinv, x1 * sinv + x2 * cosv], axis=1).astype(BF)
        for hd in range(HEADS):
            q_lat = _dot(qn[:, hd * NOPE:(hd + 1) * NOPE], wuk_ref[hd])
            q_rope = _dot(roped, perm_ref[hd])
            qh = jnp.concatenate([q_lat, q_rope], axis=1).astype(BF)
            for a in range(nsub):
                qs_ref[a, hd * TQ:(hd + 1) * TQ, :] = qh[a * TQ:(a + 1) * TQ, :]
        k_rope = kr * csk_ref[...] + _swap_halves(kr) * snk_ref[...]
        kv_ref[...] = jnp.concatenate([c_kv, k_rope], axis=1).astype(BF)

        ext = jnp.concatenate([carry_ref[...], u], axis=0)
        win = _window_sums(ext, True)[16:, :]
        pooled = (win / _row_counts(i * ts, ts) - u).astype(BF)
        pooled_ref[...] = pooled
        carry_ref[...] = u[ts - 16:ts, :]
        ypre = jnp.concatenate(
            [_dot(pooled[:, g * GD:(g + 1) * GD], wpool_ref[g]) for g in range(GROUPS)], axis=1)
        ypre_ref[...] = ypre
        ypool_ref[...] = (ypre * pscale_ref[...]).astype(BF)

    out_shape = (
        jax.ShapeDtypeStruct((S, D), BF),
        jax.ShapeDtypeStruct((S, 384), F32),
        jax.ShapeDtypeStruct((S, HEADS * NOPE), BF),
        jax.ShapeDtypeStruct((S // TQ, HEADS * TQ, QW), BF),
        jax.ShapeDtypeStruct((S, QW), BF),
        jax.ShapeDtypeStruct((S, PW), BF),
        jax.ShapeDtypeStruct((S, PW), F32),
        jax.ShapeDtypeStruct((S, PW), BF),
    )
    in_specs = [
        _rows(ts, D), _full(mod.shape), _full((1, D)), _full(w_in.shape), _full((1, QL)), _full((1, KVL)),
        _full(w_uq.shape), _full(wuk_dc.shape), _full(perm.shape), _rows(ts, 128), _rows(ts, 128), _rows(ts, 128),
        _rows(ts, 128), _full(w_pool.shape), _full((1, PW)),
    ]
    out_specs = (
        _rows(ts, D), _rows(ts, 384), _rows(ts, HEADS * NOPE),
        pl.BlockSpec((nsub, HEADS * TQ, QW), lambda i: (i, 0, 0)),
        _rows(ts, QW), _rows(ts, PW), _rows(ts, PW), _rows(ts, PW),
    )
    return pl.pallas_call(
        body, name="fwd_in", out_shape=out_shape, grid=(S // ts,), in_specs=in_specs, out_specs=out_specs,
        scratch_shapes=[pltpu.VMEM((16, PW), F32)], compiler_params=_params(("arbitrary",)),
    )(x, mod, g_mix, w_in, g_q, g_kv, w_uq, wuk_dc, perm, cos4, sin4, csk, snk, w_pool, pool_scale)


def _diag_mask(shape, q_axis):
    qi = (lax.broadcasted_iota(jnp.int32, shape, q_axis) & (TQ - 1)) >> 6
    ki = lax.broadcasted_iota(jnp.int32, shape, 1 - q_axis) >> 6
    return ki <= qi


def _attn_fwd(qs, kv, wuv_cv):
    nq = qs.shape[0]
    S = kv.shape[0]
    M = HEADS * TQ

    def body(qs_ref, kv_ref, wuv_ref, olat_ref, ymla_ref, lse_ref):
        i = pl.program_id(0)
        q = qs_ref[0]

        def step(kt, carry, masked):
            m, l, acc = carry
            k = kv_ref[pl.ds(pl.multiple_of(kt * TK, TK), TK), :]
            s = _dot_nt(q, k) * SM_SCALE
            if masked:
                s = jnp.where(_diag_mask((M, TK), 0), s, -jnp.inf)
            m_new = jnp.maximum(m, jnp.max(s, axis=-1, keepdims=True))
            alpha = jnp.exp(m - m_new)
            p = jnp.exp(s - m_new)
            l = alpha * l + jnp.sum(p, axis=-1, keepdims=True)
            acc = alpha * acc + _dot(p.astype(BF), k[:, 0:KVL])
            return m_new, l, acc

        init = (jnp.full((M, 1), -jnp.inf, F32), jnp.zeros((M, 1), F32), jnp.zeros((M, KVL), F32))
        carry = lax.fori_loop(0, i, lambda kt, c: step(kt, c, False), init)
        m, l, acc = step(i, carry, True)
        o_lat = acc / l
        olat_ref[0] = o_lat
        lse = m + jnp.log(l)
        lse_ref[0] = jnp.broadcast_to(lse, (M, 128)).T[0:8, :]
        for hd in range(HEADS):
            o = _dot(o_lat[hd * TQ:(hd + 1) * TQ, :].astype(BF), wuv_ref[hd])
            ymla_ref[:, hd * 128:(hd + 1) * 128] = o.astype(BF)

    out_shape = (
        jax.ShapeDtypeStruct((nq, M, KVL), F32),
        jax.ShapeDtypeStruct((S, HEADS * 128), BF),
        jax.ShapeDtypeStruct((nq, 8, M), F32),
    )
    return pl.pallas_call(
        body, name="attn_fwd", out_shape=out_shape, grid=(nq,),
        in_specs=[pl.BlockSpec((1, M, QW), lambda i: (i, 0, 0)), _full(kv.shape), _full(wuv_cv.shape)],
        out_specs=(pl.BlockSpec((1, M, KVL), lambda i: (i, 0, 0)), _rows(TQ, HEADS * 128),
                   pl.BlockSpec((1, 8, M), lambda i: (i, 0, 0))),
        compiler_params=_params(("arbitrary",)),
    )(qs, kv, wuv_cv)


def _silu_parts(a):
    sg = jax.nn.sigmoid(a)
    return sg, a * sg


def _ffn_fwd(x, ymla, ypool, mod, w_o, g_ffn, wg_t, wu_t, wd, g_final, target):
    S = x.shape[0]
    ts = 512
    tf = 256
    nj = FF // tf

    def body(x_ref, ymla_ref, ypool_ref, mod_ref, wo_ref, gffn_ref, wg_ref, wu_ref, wd_ref, gfin_ref, t_ref,
             x2_ref, mix_ref, h2t_ref, a_ref, b_ref, dx3_ref, dff_ref, dfft_ref, loss_ref, dgfin_ref, dgt2_ref,
             acc_ref, h2_ref):
        i = pl.program_id(0)
        j = pl.program_id(1)

        @pl.when(jnp.logical_and(i == 0, j == 0))
        def _():
            loss_ref[...] = jnp.zeros_like(loss_ref)
            dgfin_ref[...] = jnp.zeros_like(dgfin_ref)
            dgt2_ref[...] = jnp.zeros_like(dgt2_ref)

        @pl.when(j == 0)
        def _():
            gt1 = mod_ref[0:1, 2 * D:3 * D]
            sh2 = mod_ref[0:1, 3 * D:4 * D]
            sc2 = mod_ref[0:1, 4 * D:5 * D]
            cat = jnp.concatenate([ymla_ref[...], ypool_ref[...]], axis=1)
            mix = _dot(cat, wo_ref[...])
            mix_ref[...] = mix
            x2 = x_ref[...] + gt1 * mix
            x2_ref[...] = x2
            h2 = (x2 * _rms(x2)) * gffn_ref[...] * (1.0 + sc2) + sh2
            h2_ref[...] = h2.astype(BF)
            h2t_ref[...] = h2.T.astype(BF)
            acc_ref[...] = jnp.zeros_like(acc_ref)

        h2b = h2_ref[...]
        a = _dot_nt(h2b, wg_ref[...])
        b = _dot_nt(h2b, wu_ref[...])
        a_ref[...] = a.astype(BF)
        b_ref[...] = b.astype(BF)
        f = _silu_parts(a)[1] * b
        acc_ref[...] += _dot(f.astype(BF), wd_ref[...])

        @pl.when(j == nj - 1)
        def _():
            gt2 = mod_ref[0:1, 5 * D:6 * D]
            ff = acc_ref[...]
            x3 = x2_ref[...] + gt2 * ff
            r3 = _rms(x3)
            xn3 = x3 * r3
            gfin = gfin_ref[...]
            e = xn3 * gfin - t_ref[...]
            loss_ref[...] += 0.5 * jnp.sum(jnp.mean(e * e, axis=-1, keepdims=True))
            dy = e * (1.0 / D)
            dgfin_ref[...] += _colsum(dy * xn3)
            dx3 = _rms_bwd(dy * gfin, xn3, r3)
            dx3_ref[...] = dx3
            dgt2_ref[...] += _colsum(dx3 * ff)
            dff = dx3 * gt2
            dff_ref[...] = dff.astype(BF)
            dfft_ref[...] = dff.T.astype(BF)

    row = lambda c: pl.BlockSpec((ts, c), lambda i, j: (i, 0))
    col = pl.BlockSpec((D, ts), lambda i, j: (0, i))
    wblk = pl.BlockSpec((tf, D), lambda i, j: (j, 0))
    act = pl.BlockSpec((ts, tf), lambda i, j: (i, j))
    const = lambda shape: pl.BlockSpec(shape, lambda i, j: (0,) * len(shape))
    out_shape = (
        jax.ShapeDtypeStruct((S, D), F32),
        jax.ShapeDtypeStruct((S, D), F32),
        jax.ShapeDtypeStruct((D, S), BF),
        jax.ShapeDtypeStruct((S, FF), BF),
        jax.ShapeDtypeStruct((S, FF), BF),
        jax.ShapeDtypeStruct((S, D), F32),
        jax.ShapeDtypeStruct((S, D), BF),
        jax.ShapeDtypeStruct((D, S), BF),
        jax.ShapeDtypeStruct((8, 128), F32),
        jax.ShapeDtypeStruct((1, D), F32),
        jax.ShapeDtypeStruct((1, D), F32),
    )
    return pl.pallas_call(
        body, name="ffn_fwd", out_shape=out_shape, grid=(S // ts, nj),
        in_specs=[row(D), row(PW), row(PW), const(mod.shape), const(w_o.shape), const((1, D)), wblk, wblk, wblk,
                  const((1, D)), row(D)],
        out_specs=(row(D), row(D), col, act, act, row(D), row(D), col, const((8, 128)), const((1, D)),
                   const((1, D))),
        scratch_shapes=[pltpu.VMEM((ts, D), F32), pltpu.VMEM((ts, D), BF)],
        compiler_params=_params(("arbitrary", "arbitrary")),
    )(x, ymla, ypool, mod, w_o, g_ffn, wg_t, wu_t, wd, g_final, target)


def _ffn_bwd(dff, dff_t, h2_t, a, b, wg_t, wu_t, wd):
    S = dff.shape[0]
    ts = 1024
    tf = 256
    ni = S // ts
    nj = FF // tf

    def body(dff_ref, dfft_ref, h2t_ref, a_ref, b_ref, wg_ref, wu_ref, wd_ref,
             dwg_ref, dwu_ref, dwd_ref, dh2_ref, gacc, uacc, dacc, dh2acc):
        j = pl.program_id(0)
        i = pl.program_id(1)
        dffb = dff_ref[...]
        h2t = h2t_ref[...]
        av = a_ref[...].astype(F32)
        bv = b_ref[...].astype(F32)
        df = _dot_nt(dffb, wd_ref[...])
        sg, sa = _silu_parts(av)
        fb = (sa * bv).astype(BF)
        dbb = (df * sa).astype(BF)
        dab = (df * bv * (sg * (1.0 + av * (1.0 - sg)))).astype(BF)

        @pl.when(i == 0)
        def _():
            gacc[...] = jnp.zeros_like(gacc)
            uacc[...] = jnp.zeros_like(uacc)
            dacc[...] = jnp.zeros_like(dacc)

        gacc[...] += _dot(h2t, dab)
        uacc[...] += _dot(h2t, dbb)
        dacc[...] += _dot(dfft_ref[...], fb)
        contrib = _dot(dab, wg_ref[...]) + _dot(dbb, wu_ref[...])
        rows = pl.ds(pl.multiple_of(i * ts, ts), ts)

        @pl.when(j == 0)
        def _():
            dh2acc[rows, :] = contrib

        @pl.when(j > 0)
        def _():
            dh2acc[rows, :] += contrib

        @pl.when(i == ni - 1)
        def _():
            dwg_ref[...] = gacc[...].T.astype(BF)
            dwu_ref[...] = uacc[...].T.astype(BF)
            dwd_ref[...] = dacc[...].T.astype(BF)

        @pl.when(j == nj - 1)
        def _():
            dh2_ref[...] = dh2acc[rows, :]

    row = lambda c: pl.BlockSpec((ts, c), lambda j, i: (i, 0))
    col = pl.BlockSpec((D, ts), lambda j, i: (0, i))
    act = pl.BlockSpec((ts, tf), lambda j, i: (i, j))
    wblk = pl.BlockSpec((tf, D), lambda j, i: (j, 0))
    out_shape = (
        jax.ShapeDtypeStruct((FF, D), BF), jax.ShapeDtypeStruct((FF, D), BF), jax.ShapeDtypeStruct((FF, D), BF),
        jax.ShapeDtypeStruct((S, D), F32),
    )
    return pl.pallas_call(
        body, name="ffn_bwd", out_shape=out_shape, grid=(nj, ni),
        in_specs=[row(D), col, col, act, act, wblk, wblk, wblk],
        out_specs=(wblk, wblk, wblk, pl.BlockSpec((ts, D), lambda j, i: (jnp.where(j == nj - 1, i, 0), 0))),
        scratch_shapes=[pltpu.VMEM((D, tf), F32), pltpu.VMEM((D, tf), F32), pltpu.VMEM((D, tf), F32),
                        pltpu.VMEM((S, D), F32)],
        compiler_params=_params(("arbitrary", "arbitrary")),
    )(dff, dff_t, h2_t, a, b, wg_t, wu_t, wd)


def _mix_bwd(dh2, dx3, x2, mix, mod, g_ffn, ymla, ypool, w_o, ypre, pooled, pool_scale, wpool_dc, olat, wuv_vc):
    S = dh2.shape[0]
    ts = 512
    n = S // ts
    nsub = ts // TQ
    M = HEADS * TQ

    def body(dh2_ref, dx3_ref, x2_ref, mix_ref, mod_ref, gffn_ref, ymla_ref, ypool_ref, wo_ref, ypre_ref, pooled_ref,
             pscale_ref, wpool_ref, olat_ref, wuv_ref,
             dx2_ref, du_ref, dolat_ref, delta_ref, dwo_ref, dwuv_ref, dwpool_ref, dpscale_ref, dgt1_ref, dsc2_ref,
             dsh2_ref, dgffn_ref, carry_ref, dwo_acc):
        i = pl.program_id(0)

        @pl.when(i == 0)
        def _():
            carry_ref[...] = jnp.zeros_like(carry_ref)
            dwo_acc[...] = jnp.zeros_like(dwo_acc)
            for r in (dwuv_ref, dwpool_ref, dpscale_ref, dgt1_ref, dsc2_ref, dsh2_ref, dgffn_ref):
                r[...] = jnp.zeros_like(r)

        gt1 = mod_ref[0:1, 2 * D:3 * D]
        sc2 = mod_ref[0:1, 4 * D:5 * D]
        gffn = gffn_ref[...]
        dh2 = dh2_ref[...]
        x2 = x2_ref[...]
        r2 = _rms(x2)
        xn2 = x2 * r2
        dsc2_ref[...] += _colsum(dh2 * (xn2 * gffn))
        dsh2_ref[...] += _colsum(dh2)
        dgffn_ref[...] += _colsum(dh2 * (1.0 + sc2) * xn2)
        dx2 = dx3_ref[...] + _rms_bwd(dh2 * gffn * (1.0 + sc2), xn2, r2)
        dx2_ref[...] = dx2
        dgt1_ref[...] += _colsum(dx2 * mix_ref[...])
        dmix = (dx2 * gt1).astype(BF)
        cat = jnp.concatenate([ymla_ref[...], ypool_ref[...]], axis=1)
        dwo_acc[...] += _dot_tn(cat, dmix)
        dcat = _dot_nt(dmix, wo_ref[...])
        dymla = dcat[:, 0:512]
        dypool = dcat[:, 512:1024]

        dpscale_ref[...] += _colsum(dypool * ypre_ref[...])
        dypre = (dypool * pscale_ref[...]).astype(BF)
        pooled = pooled_ref[...]
        dpooled = []
        for g in range(GROUPS):
            sl = slice(g * GD, (g + 1) * GD)
            dwpool_ref[g] += _dot_tn(pooled[:, sl], dypre[:, sl])
            dpooled.append(_dot(dypre[:, sl], wpool_ref[g]))
        dpooled = jnp.concatenate(dpooled, axis=1)
        tile = n - 1 - i
        e = dpooled / _row_counts(tile * ts, ts)
        ext = jnp.concatenate([e, carry_ref[...]], axis=0)
        du_ref[...] = _window_sums(ext, False)[0:ts, :] - dpooled
        carry_ref[...] = e[0:16, :]

        for hd in range(HEADS):
            do = dymla[:, hd * 128:(hd + 1) * 128]
            dob = do.astype(BF)
            dol = _dot(dob, wuv_ref[hd])
            for a in range(nsub):
                ol = olat_ref[a, hd * TQ:(hd + 1) * TQ, :]
                dl = dol[a * TQ:(a + 1) * TQ, :]
                dolat_ref[a, hd * TQ:(hd + 1) * TQ, :] = dl.astype(BF)
                dwuv_ref[hd] += _dot_tn(ol, dob[a * TQ:(a + 1) * TQ, :])
                delta = jnp.sum(dl * ol, axis=-1, keepdims=True)
                delta_ref[a, :, hd * TQ:(hd + 1) * TQ] = jnp.broadcast_to(delta, (TQ, 128)).T[0:8, :]

        @pl.when(i == n - 1)
        def _():
            dwo_ref[...] = dwo_acc[...].astype(BF)

    rev = lambda c: pl.BlockSpec((ts, c), lambda i: (n - 1 - i, 0))
    rev3 = lambda r, c: pl.BlockSpec((nsub, r, c), lambda i: (n - 1 - i, 0, 0))
    out_shape = (
        jax.ShapeDtypeStruct((S, D), F32),
        jax.ShapeDtypeStruct((S, PW), F32),
        jax.ShapeDtypeStruct((S // TQ, M, KVL), BF),
        jax.ShapeDtypeStruct((S // TQ, 8, M), F32),
        jax.ShapeDtypeStruct((D, D), BF),
        jax.ShapeDtypeStruct((HEADS, KVL, 128), F32),
        jax.ShapeDtypeStruct((GROUPS, GD, GD), F32),
        jax.ShapeDtypeStruct((1, PW), F32),
        jax.ShapeDtypeStruct((1, D), F32), jax.ShapeDtypeStruct((1, D), F32), jax.ShapeDtypeStruct((1, D), F32),
        jax.ShapeDtypeStruct((1, D), F32),
    )
    in_specs = [rev(D), rev(D), rev(D), rev(D), _full(mod.shape), _full((1, D)), rev(PW), rev(PW), _full(w_o.shape),
                rev(PW), rev(PW), _full((1, PW)), _full(wpool_dc.shape), rev3(M, KVL), _full(wuv_vc.shape)]
    out_specs = (rev(D), rev(PW), rev3(M, KVL), rev3(8, M), _full((D, D)), _full((HEADS, KVL, 128)),
                 _full((GROUPS, GD, GD)), _full((1, PW)), _full((1, D)), _full((1, D)), _full((1, D)), _full((1, D)))
    return pl.pallas_call(
        body, name="mix_bwd", out_shape=out_shape, grid=(n,), in_specs=in_specs, out_specs=out_specs,
        scratch_shapes=[pltpu.VMEM((16, PW), F32), pltpu.VMEM((D, D), F32)],
        compiler_params=_params(("arbitrary",)),
    )(dh2, dx3, x2, mix, mod, g_ffn, ymla, ypool, w_o, ypre, pooled, pool_scale, wpool_dc, olat, wuv_vc)


def _attn_bwd(qs, kv, dolat, lse, delta):
    nq = qs.shape[0]
    S = kv.shape[0]
    M = HEADS * TQ
    nk = S // TK

    def body(qs_ref, kv_ref, do_ref, lse_ref, delta_ref, dkv_ref, dqt_ref):
        kt = pl.program_id(0)
        k = kv_ref[...]
        v = k[:, 0:KVL]
        k_t = k.astype(F32).T.astype(BF)

        @pl.when(kt == 0)
        def _():
            dqt_ref[...] = jnp.zeros_like(dqt_ref)

        def step(qi, carry, masked):
            dk, dv = carry
            q = qs_ref[qi]
            do = do_ref[qi]
            s = _dot_nt(k, q) * SM_SCALE
            p = jnp.exp(s - lse_ref[qi, 0:1, :])
            if masked:
                p = jnp.where(_diag_mask((TK, M), 1), p, 0.0)
            dp = _dot_nt(v, do)
            ds = (p * (dp - delta_ref[qi, 0:1, :]) * SM_SCALE).astype(BF)
            dv = dv + _dot(p.astype(BF), do)
            dk = dk + _dot(ds, q)
            dqt_ref[qi] += _dot(k_t, ds)
            return dk, dv

        carry = step(kt, (jnp.zeros((TK, QW), F32), jnp.zeros((TK, KVL), F32)), True)
        dk, dv = lax.fori_loop(kt + 1, nq, lambda qi, c: step(qi, c, False), carry)
        dkv_ref[...] = dk + jnp.concatenate([dv, jnp.zeros((TK, QW - KVL), F32)], axis=1)

    out_shape = (jax.ShapeDtypeStruct((S, QW), F32), jax.ShapeDtypeStruct((nq, QW, M), F32))
    return pl.pallas_call(
        body, name="attn_bwd", out_shape=out_shape, grid=(nk,),
        in_specs=[_vmem(), _rows(TK, QW), _vmem(), _vmem(), _vmem()],
        out_specs=(_rows(TK, QW), _vmem()),
        compiler_params=_params(("arbitrary",)),
    )(qs, kv, dolat, lse, delta)


def _in_bwd(dqt, dkv, du, raw, qn, h1, x, dx2, mod, g_mix, w_in, g_q, g_kv, w_uq, wuk_cd, perm_t, cos4, sin4, csk,
            snk):
    S = x.shape[0]
    ts = 512
    n = S // ts
    nsub = ts // TQ
    M = HEADS * TQ

    def body(dqt_ref, dkv_ref, du_ref, raw_ref, qn_ref, h1_ref, x_ref, dx2_ref, mod_ref, gmix_ref, win_ref, gq_ref,
             gkv_ref, wuq_ref, wuk_ref, permt_ref, cos_ref, sin_ref, csk_ref, snk_ref,
             dx_ref, dwin_ref, dwuq_ref, dwuk_ref, dgq_ref, dgkv_ref, dsc1_ref, dsh1_ref, dgmix_ref, dwin_acc,
             dwuq_acc):
        i = pl.program_id(0)

        @pl.when(i == 0)
        def _():
            dwin_acc[...] = jnp.zeros_like(dwin_acc)
            dwuq_acc[...] = jnp.zeros_like(dwuq_acc)
            for r in (dwuk_ref, dgq_ref, dgkv_ref, dsc1_ref, dsh1_ref, dgmix_ref):
                r[...] = jnp.zeros_like(r)

        dq_blocks = [dqt_ref[a].T for a in range(nsub)]
        qn = qn_ref[...]
        dq_parts = []
        drope = jnp.zeros((ts, 2 * 128), F32)
        for hd in range(HEADS):
            dqh = jnp.concatenate([blk[hd * TQ:(hd + 1) * TQ, :] for blk in dq_blocks], axis=0)
            dq_lat = dqh[:, 0:KVL].astype(BF)
            dq_parts.append(_dot(dq_lat, wuk_ref[hd]))
            dwuk_ref[hd] += _dot_tn(dq_lat, qn[:, hd * NOPE:(hd + 1) * NOPE])
            drope = drope + _dot(dqh[:, KVL:QW].astype(BF), permt_ref[hd])
        do1 = drope[:, 0:128]
        do2 = drope[:, 128:256]
        cosv = cos_ref[...]
        sinv = sin_ref[...]
        dq_parts.append(do1 * cosv + do2 * sinv)
        dq_parts.append(do2 * cosv - do1 * sinv)
        dq = jnp.concatenate(dq_parts, axis=1).astype(BF)

        cq_raw = raw_ref[:, 0:QL]
        ckv_raw = raw_ref[:, QL:QL + KVL]
        rq = _rms(cq_raw)
        nq_ = cq_raw * rq
        gq = gq_ref[...]
        dwuq_acc[...] += _dot_tn((nq_ * gq).astype(BF), dq)
        dc_q = _dot_nt(dq, wuq_ref[...])
        dgq_ref[...] += _colsum(dc_q * nq_)
        dcq_raw = _rms_bwd(dc_q * gq, nq_, rq)

        dkv = dkv_ref[...]
        rk = _rms(ckv_raw)
        nk_ = ckv_raw * rk
        dc_kv = dkv[:, 0:KVL]
        dgkv_ref[...] += _colsum(dc_kv * nk_)
        dckv_raw = _rms_bwd(dc_kv * gkv_ref[...], nk_, rk)
        dkr_roped = dkv[:, KVL:QW]
        dkr = dkr_roped * csk_ref[...] - _swap_halves(dkr_roped) * snk_ref[...]

        dproj = jnp.concatenate([dcq_raw, dckv_raw, dkr, du_ref[...]], axis=1).astype(BF)
        dwin_acc[...] += _dot_tn(h1_ref[...], dproj)
        dh1 = _dot_nt(dproj, win_ref[...])

        sc1 = mod_ref[0:1, D:2 * D]
        gmix = gmix_ref[...]
        xv = x_ref[...]
        r1 = _rms(xv)
        xn1 = xv * r1
        dsc1_ref[...] += _colsum(dh1 * (xn1 * gmix))
        dsh1_ref[...] += _colsum(dh1)
        dgmix_ref[...] += _colsum(dh1 * (1.0 + sc1) * xn1)
        dx_ref[...] = dx2_ref[...] + _rms_bwd(dh1 * gmix * (1.0 + sc1), xn1, r1)

        @pl.when(i == n - 1)
        def _():
            dwin_ref[...] = dwin_acc[...].astype(BF)
            dwuq_ref[...] = dwuq_acc[...].astype(BF)

    out_shape = (
        jax.ShapeDtypeStruct((S, D), F32),
        jax.ShapeDtypeStruct((D, D), BF),
        jax.ShapeDtypeStruct((QL, 768), BF),
        jax.ShapeDtypeStruct((HEADS, KVL, NOPE), F32),
        jax.ShapeDtypeStruct((1, QL), F32), jax.ShapeDtypeStruct((1, KVL), F32),
        jax.ShapeDtypeStruct((1, D), F32), jax.ShapeDtypeStruct((1, D), F32), jax.ShapeDtypeStruct((1, D), F32),
    )
    in_specs = [pl.BlockSpec((nsub, QW, M), lambda i: (i, 0, 0)), _rows(ts, QW), _rows(ts, PW), _rows(ts, 384),
                _rows(ts, HEADS * NOPE), _rows(ts, D), _rows(ts, D), _rows(ts, D), _full(mod.shape), _full((1, D)),
                _full(w_in.shape), _full((1, QL)), _full((1, KVL)), _full(w_uq.shape), _full(wuk_cd.shape),
                _full(perm_t.shape), _rows(ts, 128), _rows(ts, 128), _rows(ts, 128), _rows(ts, 128)]
    out_specs = (_rows(ts, D), _full((D, D)), _full((QL, 768)), _full((HEADS, KVL, NOPE)), _full((1, QL)),
                 _full((1, KVL)), _full((1, D)), _full((1, D)), _full((1, D)))
    return pl.pallas_call(
        body, name="in_bwd", out_shape=out_shape, grid=(n,), in_specs=in_specs, out_specs=out_specs,
        scratch_shapes=[pltpu.VMEM((D, D), F32), pltpu.VMEM((QL, 768), F32)],
        compiler_params=_params(("arbitrary",)),
    )(dqt, dkv, du, raw, qn, h1, x, dx2, mod, g_mix, w_in, g_q, g_kv, w_uq, wuk_cd, perm_t, cos4, sin4, csk, snk)


def _rope_perm():
    p = np.zeros((HEADS, 2 * 128, 128), np.float32)
    for hd in range(HEADS):
        for t in range(HALF):
            p[hd, hd * HALF + t, t] = 1.0
            p[hd, 128 + hd * HALF + t, HALF + t] = 1.0
    return p


def _rope_tables(positions):
    freqs = jnp.power(ROPE_THETA, -jnp.arange(HALF, dtype=F32) / HALF)
    ang = positions.astype(F32)[:, None] * jnp.tile(freqs, HEADS)[None, :]
    cos4 = jnp.cos(ang)
    sin4 = jnp.sin(ang)
    lane = jnp.arange(HEADS * HALF)[None, :]
    csk = jnp.where(lane < ROPE, cos4, 0.0)
    snk = jnp.where(lane < HALF, -sin4, jnp.where(lane < ROPE, sin4, 0.0))
    return cos4, sin4, csk, snk


def _local_step(x, positions, target, mod, g_mix, w_in_p, g_q, g_kv, w_uq_p, w_uk, w_uv, w_pool, pool_scale, g_ffn,
                g_final, late, ffn_grads_exchange):
    perm = jnp.asarray(_rope_perm(), BF)
    perm_t = jnp.asarray(_rope_perm().transpose(0, 2, 1), BF)
    cos4, sin4, csk, snk = _rope_tables(positions)
    wuk_dc = w_uk.transpose(1, 2, 0).astype(BF)
    wuk_cd = w_uk.transpose(1, 0, 2).astype(BF)
    wuv_cv = w_uv.transpose(1, 0, 2).astype(BF)
    wuv_vc = w_uv.transpose(1, 2, 0).astype(BF)
    wpool = w_pool.astype(BF)
    wpool_dc = w_pool.transpose(0, 2, 1).astype(BF)

    h1, raw, qn, qs, kv, pooled, ypre, ypool = _fwd_in(
        x, mod, g_mix, w_in_p, g_q, g_kv, w_uq_p, wuk_dc, perm, cos4, sin4, csk, snk, wpool, pool_scale)
    olat, ymla, lse = _attn_fwd(qs, kv, wuv_cv)
    w_o, wg_t, wu_t, wd = late
    x2, mix, h2_t, a, b, dx3, dff, dff_t, loss, dgfin, dgt2 = _ffn_fwd(
        x, ymla, ypool, mod, w_o, g_ffn, wg_t, wu_t, wd, g_final, target)
    dwg_t, dwu_t, dwd, dh2 = _ffn_bwd(dff, dff_t, h2_t, a, b, wg_t, wu_t, wd)
    ffn_parts = ffn_grads_exchange((dwg_t, dwu_t, dwd))
    (dx2, du, dolat, delta, dwo, dwuv, dwpool, dpscale, dgt1, dsc2, dsh2, dgffn) = _mix_bwd(
        dh2, dx3, x2, mix, mod, g_ffn, ymla, ypool, w_o, ypre, pooled, pool_scale, wpool_dc, olat, wuv_vc)
    dkv, dqt = _attn_bwd(qs, kv, dolat, lse, delta)
    dx, dwin, dwuq, dwuk, dgq, dgkv, dsc1, dsh1, dgmix = _in_bwd(
        dqt, dkv, du, raw, qn, h1, x, dx2, mod, g_mix, w_in_p, g_q, g_kv, w_uq_p, wuk_cd, perm_t, cos4, sin4, csk,
        snk)
    dmod = jnp.concatenate([dsh1, dsc1, dgt1, dsh2, dsc2, dgt2], axis=1)
    parts = (*_scatter_partials((dwin, dwuq, dwo)), *ffn_parts)
    replicated = dict(
        w_uk=dwuk.transpose(1, 0, 2), w_uv=dwuv.transpose(1, 0, 2), w_pool=dwpool, g_mix=dgmix, g_q=dgq, g_kv=dgkv,
        pool_scale=dpscale, g_ffn=dgffn, g_final=dgfin)
    return loss[0, 0], dx, dmod, parts, replicated


def _my_pos():
    return lax.axis_index("x"), lax.axis_index("y"), lax.axis_index("c")


def _peer(pos, k):
    x, y, c = pos
    return (1 - x if k & 4 else x, 1 - y if k & 2 else y, 1 - c if k & 1 else c)


def _index(pos):
    x, y, c = pos
    return 4 * x + 2 * y + c


def _remote(src, dst, send_sem, recv_sem, to):
    return pltpu.make_async_remote_copy(src_ref=src, dst_ref=dst, send_sem=send_sem, recv_sem=recv_sem,
                                        device_id=to, device_id_type=MESH)


def _ada_mod(c, w_ada, b_ada):
    def body(c_ref, w_ref, b_ref, mod_ref, call_ref, cbuf, sbuf, rbuf, send1, recv1, send2, recv2):
        me = _my_pos()
        mi = _index(me)
        cv = c_ref[...]
        cbuf[...] = jnp.broadcast_to(cv * jax.nn.sigmoid(cv), (8, D))
        call_ref[mi] = cbuf[...]
        first = [_remote(cbuf, call_ref.at[mi], send1.at[k - 1], recv1.at[k - 1], _peer(me, k)) for k in range(1, NDEV)]
        for cp in first:
            cp.start()
        for k in range(1, NDEV):
            _remote(cbuf, call_ref.at[_index(_peer(me, k))], send1.at[k - 1], recv1.at[k - 1], _peer(me, k)).wait_recv()
        c_all = jnp.concatenate([call_ref[b][0:1, :] for b in range(NDEV)], axis=0)
        blocks = _dot(c_all.astype(BF), w_ref[...].astype(BF))
        for b in range(NDEV):
            sbuf[b] = jnp.broadcast_to(blocks[b:b + 1, :], (8, MODC))
        second = []
        for k in range(1, NDEV):
            to = _peer(me, k)
            second.append(_remote(sbuf.at[_index(to)], rbuf.at[mi], send2.at[k - 1], recv2.at[k - 1], to))
        for cp in second:
            cp.start()
        rbuf[mi] = sbuf[mi]
        for k in range(1, NDEV):
            to = _peer(me, k)
            _remote(sbuf.at[_index(to)], rbuf.at[_index(to)], send2.at[k - 1], recv2.at[k - 1], to).wait_recv()
        for j in range(NDEV):
            mod_ref[:, j * MODC:(j + 1) * MODC] = rbuf[j] + b_ref[:, j * MODC:(j + 1) * MODC]
        for cp in first + second:
            cp.wait_send()

    return pl.pallas_call(
        body, name="ada_mod",
        out_shape=(jax.ShapeDtypeStruct((8, N_MOD * D), F32), jax.ShapeDtypeStruct((NDEV, 8, D), F32)),
        in_specs=[_vmem(), _vmem(), _vmem()], out_specs=(_vmem(), _vmem()),
        scratch_shapes=[pltpu.VMEM((8, D), F32), pltpu.VMEM((NDEV, 8, MODC), F32), pltpu.VMEM((NDEV, 8, MODC), F32),
                        pltpu.SemaphoreType.DMA((NDEV - 1,)), pltpu.SemaphoreType.DMA((NDEV - 1,)),
                        pltpu.SemaphoreType.DMA((NDEV - 1,)), pltpu.SemaphoreType.DMA((NDEV - 1,))],
        compiler_params=_params(),
    )(c, w_ada, b_ada)


def _all_gather(shards):
    n = len(shards)

    def body(*refs):
        ins, outs = refs[:n], refs[n:2 * n]
        send, recv, local = refs[2 * n:]
        me = _my_pos()
        mi = _index(me)
        own = [pltpu.make_async_copy(ins[a], outs[a].at[mi], local.at[a]) for a in range(n)]
        for cp in own:
            cp.start()
        sent = []
        for a in range(n):
            for k in range(1, NDEV):
                sent.append(_remote(ins[a], outs[a].at[mi], send.at[a, k - 1], recv.at[a, k - 1], _peer(me, k)))
        for cp in sent:
            cp.start()
        for a in range(n):
            for k in range(1, NDEV):
                to = _peer(me, k)
                _remote(ins[a], outs[a].at[_index(to)], send.at[a, k - 1], recv.at[a, k - 1], to).wait_recv()
        for cp in sent:
            cp.wait_send()
        for cp in own:
            cp.wait()

    return pl.pallas_call(
        body, name="gather_weights",
        out_shape=tuple(jax.ShapeDtypeStruct((NDEV,) + s.shape, s.dtype) for s in shards),
        in_specs=[_any()] * n, out_specs=tuple([_any()] * n),
        scratch_shapes=[pltpu.SemaphoreType.DMA((n, NDEV - 1)), pltpu.SemaphoreType.DMA((n, NDEV - 1)),
                        pltpu.SemaphoreType.DMA((n,))],
        compiler_params=_params(),
    )(*shards)


def _scatter_partials(grads):
    n = len(grads)
    of = _exchange_slices(True)

    def body(*refs):
        ins, outs = refs[:n], refs[n:2 * n]
        send, recv, local = refs[2 * n:]
        me = _my_pos()
        mi = _index(me)
        own = [pltpu.make_async_copy(of(ins[a], mi), outs[a].at[mi], local.at[a]) for a in range(n)]
        for cp in own:
            cp.start()
        sent = []
        for a in range(n):
            for k in range(1, NDEV):
                to = _peer(me, k)
                s = a * (NDEV - 1) + k - 1
                sent.append(_remote(of(ins[a], _index(to)), outs[a].at[mi], send.at[s], recv.at[s], to))
        for cp in sent:
            cp.start()
        for a in range(n):
            for k in range(1, NDEV):
                to = _peer(me, k)
                s = a * (NDEV - 1) + k - 1
                _remote(of(ins[a], mi), outs[a].at[_index(to)], send.at[s], recv.at[s], to).wait_recv()
        for cp in sent:
            cp.wait_send()
        for cp in own:
            cp.wait()

    return pl.pallas_call(
        body, name="scatter_grads",
        out_shape=tuple(jax.ShapeDtypeStruct((NDEV, g.shape[0] // NDEV, g.shape[1]), g.dtype) for g in grads),
        in_specs=[_any()] * n, out_specs=tuple([_any()] * n),
        scratch_shapes=[pltpu.SemaphoreType.DMA((n * (NDEV - 1),)), pltpu.SemaphoreType.DMA((n * (NDEV - 1),)),
                        pltpu.SemaphoreType.DMA((n,))],
        compiler_params=_params(),
    )(*grads)


def _exchange_slices(scatter):
    def of(src, to_index):
        if not scatter:
            return src
        r = src.shape[0] // NDEV
        return src.at[pl.ds(pl.multiple_of(to_index * r, 16), r), :]
    return of


def _sequencer_exchange(name, collective_id, srcs, scatter, after=()):
    n = len(srcs)
    of = _exchange_slices(scatter)
    src_refs = [jax.new_ref(s, memory_space=pltpu.MemorySpace.HBM) for s in srcs]
    zone_refs = [
        jax.empty_ref(jax.ShapeDtypeStruct((NDEV, s.shape[0] // NDEV if scatter else s.shape[0], s.shape[1]), s.dtype),
                      memory_space=pltpu.MemorySpace.HBM) for s in srcs]
    after_refs = [jax.new_ref(t, memory_space=pltpu.MemorySpace.HBM) for t in after]
    seen_refs = [jax.empty_ref(jax.ShapeDtypeStruct(t.shape, t.dtype), memory_space=pltpu.MemorySpace.HBM)
                 for t in after]

    @pl.kernel(mesh=plsc.ScalarSubcoreMesh(axis_name="sequencer", num_cores=1), name=name,
               scratch_types=(pltpu.SemaphoreType.DMA((n * (NDEV - 1),)), pltpu.SemaphoreType.DMA((n * (NDEV - 1),)),
                              pltpu.SemaphoreType.DMA((n + len(after),))),
               compiler_params=pltpu.CompilerParams(collective_id=collective_id))
    def launch(send, recv, local):
        me = _my_pos()
        mi = _index(me)
        barrier = pltpu.get_barrier_semaphore()
        for k in range(1, NDEV):
            pl.semaphore_signal(barrier, inc=1, device_id=_peer(me, k), device_id_type=MESH)
        pl.semaphore_wait(barrier, NDEV - 1)
        own = [pltpu.make_async_copy(of(src_refs[a], mi), zone_refs[a].at[mi], local.at[a]) for a in range(n)]
        own += [pltpu.make_async_copy(t, seen, local.at[n + q])
                for q, (t, seen) in enumerate(zip(after_refs, seen_refs))]
        for cp in own:
            cp.start()
        for a in range(n):
            for k in range(1, NDEV):
                to = _peer(me, k)
                s = a * (NDEV - 1) + k - 1
                _remote(of(src_refs[a], _index(to)), zone_refs[a].at[mi], send.at[s], recv.at[s], to).start()
        for cp in own:
            cp.wait()
        for a in range(n):
            for k in range(1, NDEV):
                to = _peer(me, k)
                s = a * (NDEV - 1) + k - 1
                cp = _remote(of(src_refs[a], mi), zone_refs[a].at[_index(to)], send.at[s], recv.at[s], to)
                cp.wait_send()
                cp.wait_recv()

    launch()
    return tuple(z[...] for z in zone_refs)


def _sum_partials(parts):
    n = len(parts)

    def body(*refs):
        for a in range(n):
            acc = refs[a][0].astype(F32)
            for p in range(1, NDEV):
                acc = acc + refs[a][p].astype(F32)
            refs[n + a][...] = acc

    return pl.pallas_call(
        body, name="sum_partials",
        out_shape=tuple(jax.ShapeDtypeStruct(p.shape[1:], F32) for p in parts),
        in_specs=[_vmem()] * n, out_specs=tuple([_vmem()] * n), compiler_params=_params(),
    )(*parts)


def _small_all_reduce(buf):
    def body(buf_ref, got_ref, red_ref, mine, send1, recv1, send2, recv2):
        me = _my_pos()
        mi = _index(me)
        first = []
        for k in range(1, NDEV):
            to = _peer(me, k)
            first.append(_remote(buf_ref.at[_index(to)], got_ref.at[mi], send1.at[k - 1], recv1.at[k - 1], to))
        for cp in first:
            cp.start()
        got_ref[mi] = buf_ref[mi]
        for k in range(1, NDEV):
            to = _peer(me, k)
            _remote(buf_ref.at[mi], got_ref.at[_index(to)], send1.at[k - 1], recv1.at[k - 1], to).wait_recv()
        acc = got_ref[0]
        for p in range(1, NDEV):
            acc = acc + got_ref[p]
        mine[...] = acc
        second = [_remote(mine, red_ref.at[mi], send2.at[k - 1], recv2.at[k - 1], _peer(me, k)) for k in range(1, NDEV)]
        for cp in second:
            cp.start()
        red_ref[mi] = acc
        for k in range(1, NDEV):
            to = _peer(me, k)
            _remote(mine, red_ref.at[_index(to)], send2.at[k - 1], recv2.at[k - 1], to).wait_recv()
        for cp in first + second:
            cp.wait_send()

    return pl.pallas_call(
        body, name="small_all_reduce",
        out_shape=(jax.ShapeDtypeStruct(buf.shape, F32), jax.ShapeDtypeStruct(buf.shape, F32)),
        in_specs=[_vmem()], out_specs=(_vmem(), _vmem()),
        scratch_shapes=[pltpu.VMEM(buf.shape[1:], F32),
                        pltpu.SemaphoreType.DMA((NDEV - 1,)), pltpu.SemaphoreType.DMA((NDEV - 1,)),
                        pltpu.SemaphoreType.DMA((NDEV - 1,)), pltpu.SemaphoreType.DMA((NDEV - 1,))],
        compiler_params=_params(),
    )(buf)


def _adamw_math(w, g, m, v):
    m = ADAM_B1 * m + (1.0 - ADAM_B1) * g
    v = ADAM_B2 * v + (1.0 - ADAM_B2) * jnp.square(g)
    m_hat = m / (1.0 - ADAM_B1 ** ADAM_STEP)
    v_hat = v / (1.0 - ADAM_B2 ** ADAM_STEP)
    delta = -ADAM_LR * (m_hat / (jnp.sqrt(v_hat) + ADAM_EPS) + ADAM_WD * w)
    return delta, m, v


def _adamw_group(name, ws, gs, ms, vs):
    n = len(ws)

    def body(*refs):
        for a in range(n):
            w, g, m, v = (refs[q * n + a][...] for q in range(4))
            delta, m2, v2 = _adamw_math(w, g, m, v)
            refs[4 * n + a][...] = delta
            refs[5 * n + a][...] = m2
            refs[6 * n + a][...] = v2

    shapes = tuple(jax.ShapeDtypeStruct(w.shape, F32) for w in ws)
    outs = pl.pallas_call(
        body, name=name, out_shape=shapes * 3, in_specs=[_vmem()] * (4 * n), out_specs=tuple([_vmem()] * (3 * n)),
        compiler_params=_params(),
    )(*ws, *gs, *ms, *vs)
    return outs[:n], outs[n:2 * n], outs[2 * n:]


def _adamw_ada(w, m, v, c_all, dmod_rows):
    def body(w_ref, m_ref, v_ref, c_ref, dm_ref, g_ref, d_ref, m2_ref, v2_ref):
        g = _dot_tn(c_ref[...], dm_ref[...].astype(BF))
        g_ref[...] = g
        delta, m2, v2 = _adamw_math(w_ref[...], g, m_ref[...], v_ref[...])
        d_ref[...] = delta
        m2_ref[...] = m2
        v2_ref[...] = v2

    shp = jax.ShapeDtypeStruct(w.shape, F32)
    return pl.pallas_call(
        body, name="adamw_ada", out_shape=(shp, shp, shp, shp), in_specs=[_vmem()] * 5,
        out_specs=tuple([_vmem()] * 4), compiler_params=_params(),
    )(w, m, v, c_all, dmod_rows)


def _w_in_to_kernel(w):
    return jnp.concatenate([w[:, 0:448], jnp.zeros((w.shape[0], 64), w.dtype), w[:, 448:960]], axis=1)


def _w_in_from_kernel(w):
    return jnp.concatenate([w[:, 0:448], w[:, 512:1024]], axis=1)


def _w_uq_to_kernel(w):
    r = w.shape[0]
    return jnp.concatenate([w[:, :, 0:NOPE].reshape(r, HEADS * NOPE),
                            w[:, :, NOPE:NOPE + HALF].reshape(r, HEADS * HALF),
                            w[:, :, NOPE + HALF:].reshape(r, HEADS * HALF)], axis=1)


def _w_uq_from_kernel(w):
    r = w.shape[0]
    return jnp.concatenate([w[:, 0:512].reshape(r, HEADS, NOPE), w[:, 512:640].reshape(r, HEADS, HALF),
                            w[:, 640:768].reshape(r, HEADS, HALF)], axis=2)


REP_NAMES = ("w_uk", "w_uv", "w_pool", "g_mix", "g_q", "g_kv", "pool_scale", "g_ffn", "g_final")


def kernel(x, c, positions, w_ada, b_ada, g_mix, w_in, g_q, g_kv, w_uq, w_uk, w_uv, w_pool, pool_scale, w_o, g_ffn, w_gate, w_up, w_down, g_final, loss_target, m_w_ada, m_b_ada, m_g_mix, m_w_in, m_g_q, m_g_kv, m_w_uq, m_w_uk, m_w_uv, m_w_pool, m_pool_scale, m_w_o, m_g_ffn, m_w_gate, m_w_up, m_w_down, m_g_final, v_w_ada, v_b_ada, v_g_mix, v_w_in, v_g_q, v_g_kv, v_w_uq, v_w_uk, v_w_uv, v_w_pool, v_pool_scale, v_w_o, v_g_ffn, v_w_gate, v_w_up, v_w_down, v_g_final):
    given = dict(locals())

    merge = lambda g: g.reshape(NDEV * g.shape[1], g.shape[2])
    w_in_p, w_uq_p = (merge(g) for g in _all_gather(
        (_w_in_to_kernel(w_in[0]).astype(BF), _w_uq_to_kernel(w_uq[0]).astype(BF))))

    mod, c_all8 = _ada_mod(c, w_ada[0], b_ada)
    c_all = c_all8[:, 0, :]
    late = _sequencer_exchange(
        "gather_late", 1, (w_o[0].astype(BF), w_gate[0].T.astype(BF), w_up[0].T.astype(BF), w_down[0].astype(BF)),
        False, after=(mod[:, 0:128], w_in_p[0:16, 0:128], w_uq_p[0:16, 0:128]))

    def ffn_grads_exchange(arrays):
        return _sequencer_exchange("scatter_ffn", 2, arrays, True)

    loss, dx, dmod, parts, replicated = _local_step(
        x[0], positions[0], loss_target[0], mod, g_mix, w_in_p, g_q, g_kv, w_uq_p, w_uk[0], w_uv[0], w_pool[0],
        pool_scale, g_ffn, g_final.reshape(1, D), tuple(merge(g) for g in late), ffn_grads_exchange)

    g_in_p, g_uq_p, g_o, g_gate_t, g_up_t, g_down = _sum_partials(parts)
    grads = dict(w_in=_w_in_from_kernel(g_in_p), w_uq=_w_uq_from_kernel(g_uq_p).reshape(QL // NDEV, HEADS * 192),
                 w_o=g_o, w_gate=g_gate_t.T, w_up=g_up_t.T, w_down=g_down)

    flat = jnp.concatenate([replicated[k].reshape(-1) for k in REP_NAMES] + [loss.reshape(1)])
    flat = jnp.pad(flat, (0, NDEV * REP_ROWS * 128 - flat.shape[0])).reshape(NDEV, REP_ROWS, 128)
    dmod_blocks = jnp.pad(dmod.reshape(NDEV, MODC // 128, 128), ((0, 0), (0, MOD_ROWS - MODC // 128), (0, 0)))
    got, red = _small_all_reduce(jnp.concatenate([dmod_blocks, flat], axis=1))
    dmod_rows = got[:, 0:MODC // 128, :].reshape(NDEV, MODC)
    grads["b_ada"] = red[:, 0:MODC // 128, :].reshape(1, N_MOD * D)
    rep_flat = red[:, MOD_ROWS:, :].reshape(-1)
    off = 0
    for k in REP_NAMES:
        size = int(np.prod(given[k].shape))
        grads[k] = rep_flat[off:off + size]
        off += size

    view = dict(w_ada=(D, MODC), b_ada=(1, N_MOD * D), g_mix=(1, D), w_in=(D // NDEV, 960), g_q=(1, QL),
                g_kv=(1, KVL), w_uq=(QL // NDEV, HEADS * 192), w_uk=(KVL, HEADS * NOPE), w_uv=(KVL, HEADS * 128),
                w_pool=(GROUPS * GD, GD), pool_scale=(1, PW), w_o=(D // NDEV, D), g_ffn=(1, D),
                w_gate=(D, FF // NDEV), w_up=(D, FF // NDEV), w_down=(FF // NDEV, D), g_final=(1, D))
    names = list(view)
    g_ada, d_ada, m_ada, v_ada = _adamw_ada(w_ada[0], m_w_ada[0], v_w_ada[0], c_all.astype(BF), dmod_rows)
    out_g, out_d, out_m, out_v = dict(w_ada=g_ada), dict(w_ada=d_ada), dict(w_ada=m_ada), dict(w_ada=v_ada)
    groups = (("adamw_ffn", ("w_gate", "w_up", "w_down")),
              ("adamw_rest", tuple(k for k in names if k not in ("w_ada", "w_gate", "w_up", "w_down"))))
    for gname, members in groups:
        ws = [given[k].reshape(view[k]) for k in members]
        gs = [grads[k].reshape(view[k]) for k in members]
        ms = [given["m_" + k].reshape(view[k]) for k in members]
        vs = [given["v_" + k].reshape(view[k]) for k in members]
        ds, m2, v2 = _adamw_group(gname, ws, gs, ms, vs)
        for k, g, d, mm, vv in zip(members, gs, ds, m2, v2):
            out_g[k], out_d[k], out_m[k], out_v[k] = g, d, mm, vv

    total = rep_flat[off]
    shaped = lambda d: [d[k].reshape(given[k].shape) for k in names]
    return (total, dx[None], *shaped(out_g), *shaped(out_d), *shaped(out_m), *shaped(out_v))
```

```python
import numpy as np
import jax
import jax.numpy as jnp
from jax import lax
from jax.experimental import pallas as pl
from jax.experimental.pallas import tpu as pltpu
from jax.experimental.pallas import tpu_sc as plsc

D = 1024
HEADS = 4
NOPE = 128
ROPE = 64
HALF = ROPE // 2
QL = 256
KVL = 128
FF = 2816
PW = 512
GROUPS = 4
GD = 128
N_MOD = 6
EPS = 1e-6
SM_SCALE = (NOPE + ROPE) ** -0.5
ROPE_THETA = 10000.0
NDEV = 8
MODC = N_MOD * D // NDEV

ADAM_LR = 0.001
ADAM_B1 = 0.9
ADAM_B2 = 0.999
ADAM_EPS = 1e-08
ADAM_WD = 0.01
ADAM_STEP = 10

BF = jnp.bfloat16
F32 = jnp.float32
VMEM_LIMIT_V7X = 60 * 1024 * 1024
MESH = pl.DeviceIdType.MESH

TQ = 256
TK = 256
QW = 256
MOD_ROWS = 8
REP_ROWS = 200
SMALL_ROWS = MOD_ROWS + REP_ROWS


def _params(sem=None):
    return pltpu.CompilerParams(dimension_semantics=sem, vmem_limit_bytes=VMEM_LIMIT_V7X)


def _dot(a, b):
    return jnp.dot(a, b, preferred_element_type=F32)


def _dot_nt(a, b):
    return lax.dot_general(a, b, (((1,), (1,)), ((), ())), preferred_element_type=F32)


def _dot_tn(a, b):
    return _dot(a.astype(F32).T.astype(BF), b)


def _full(shape):
    return pl.BlockSpec(shape, lambda *_: (0,) * len(shape))


def _rows(ts, cols):
    return pl.BlockSpec((ts, cols), lambda i: (i, 0))


def _vmem():
    return pl.BlockSpec(memory_space=pltpu.VMEM)


def _any():
    return pl.BlockSpec(memory_space=pl.ANY)


def _rms(v):
    return lax.rsqrt(jnp.mean(v * v, axis=-1, keepdims=True) + EPS)


def _rms_bwd(dn, n, r):
    return r * (dn - n * jnp.mean(dn * n, axis=-1, keepdims=True))


def _colsum(v):
    return jnp.sum(v, axis=0, keepdims=True)


def _swap_halves(v):
    lane = lax.broadcasted_iota(jnp.int32, v.shape, 1)
    return jnp.where(lane < HALF, pltpu.roll(v, 128 - HALF, 1), pltpu.roll(v, HALF, 1))


def _window_lane_width():
    lane = lax.broadcasted_iota(jnp.int32, (1, PW), 1)
    return jnp.where(lane < 128, 2.0, jnp.where(lane < 256, 4.0, jnp.where(lane < 384, 8.0, 16.0))).astype(F32)


def _window_sums(ext, back):
    n = ext.shape[0]

    def sh(v, k):
        return pltpu.roll(v, k if back else n - k, 0)

    s2 = ext + sh(ext, 1)
    e4 = s2[:, 128:]
    s4 = e4 + sh(e4, 2)
    e8 = s4[:, 128:]
    s8 = e8 + sh(e8, 4)
    e16 = s8[:, 128:]
    s16 = e16 + sh(e16, 8)
    return jnp.concatenate([s2[:, :128], s4[:, :128], s8[:, :128], s16], axis=1)


def _row_counts(first_row, ts):
    t1 = (first_row + lax.broadcasted_iota(jnp.int32, (ts, 1), 0) + 1).astype(F32)
    return jnp.minimum(t1, _window_lane_width())


def _fwd_in(x, mod, g_mix, w_in, g_q, g_kv, w_uq, wuk_dc, perm, cos4, sin4, csk, snk, w_pool, pool_scale):
    S = x.shape[0]
    ts = 512
    nsub = ts // TQ

    def body(x_ref, mod_ref, gmix_ref, win_ref, gq_ref, gkv_ref, wuq_ref, wuk_ref, perm_ref, cos_ref, sin_ref,
             csk_ref, snk_ref, wpool_ref, pscale_ref,
             h1_ref, raw_ref, qn_ref, qs_ref, kv_ref, pooled_ref, ypre_ref, ypool_ref, carry_ref):
        i = pl.program_id(0)

        @pl.when(i == 0)
        def _():
            carry_ref[...] = jnp.zeros_like(carry_ref)

        xv = x_ref[...]
        sh1 = mod_ref[0:1, 0:D]
        sc1 = mod_ref[0:1, D:2 * D]
        h = (xv * _rms(xv)) * gmix_ref[...] * (1.0 + sc1) + sh1
        hb = h.astype(BF)
        h1_ref[...] = hb
        proj = _dot(hb, win_ref[...])
        cq_raw = proj[:, 0:QL]
        ckv_raw = proj[:, QL:QL + KVL]
        kr = proj[:, 384:512]
        u = proj[:, 512:1024]
        raw_ref[...] = proj[:, 0:384]

        c_q = (cq_raw * _rms(cq_raw)) * gq_ref[...]
        c_kv = (ckv_raw * _rms(ckv_raw)) * gkv_ref[...]
        q = _dot(c_q.astype(BF), wuq_ref[...])
        qn = q[:, 0:HEADS * NOPE].astype(BF)
        qn_ref[...] = qn
        x1 = q[:, 512:640]
        x2 = q[:, 640:768]
        cosv = cos_ref[...]
        sinv = sin_ref[...]
        roped = jnp.concatenate([x1 * cosv - x2 * sinv, x1 * sinv + x2 * cosv], axis=1).astype(BF)
        for hd in range(HEADS):
            q_lat = _dot(qn[:, hd * NOPE:(hd + 1) * NOPE], wuk_ref[hd])
            q_rope = _dot(roped, perm_ref[hd])
            qh = jnp.concatenate([q_lat, q_rope], axis=1).astype(BF)
            for a in range(nsub):
                qs_ref[a, hd * TQ:(hd + 1) * TQ, :] = qh[a * TQ:(a + 1) * TQ, :]
        k_rope = kr * csk_ref[...] + _swap_halves(kr) * snk_ref[...]
        kv_ref[...] = jnp.concatenate([c_kv, k_rope], axis=1).astype(BF)

        ext = jnp.concatenate([carry_ref[...], u], axis=0)
        win = _window_sums(ext, True)[16:, :]
        pooled = (win / _row_counts(i * ts, ts) - u).astype(BF)
        pooled_ref[...] = pooled
        carry_ref[...] = u[ts - 16:ts, :]
        ypre = jnp.concatenate(
            [_dot(pooled[:, g * GD:(g + 1) * GD], wpool_ref[g]) for g in range(GROUPS)], axis=1)
        ypre_ref[...] = ypre
        ypool_ref[...] = (ypre * pscale_ref[...]).astype(BF)

    out_shape = (
        jax.ShapeDtypeStruct((S, D), BF),
        jax.ShapeDtypeStruct((S, 384), F32),
        jax.ShapeDtypeStruct((S, HEADS * NOPE), BF),
        jax.ShapeDtypeStruct((S // TQ, HEADS * TQ, QW), BF),
        jax.ShapeDtypeStruct((S, QW), BF),
        jax.ShapeDtypeStruct((S, PW), BF),
        jax.ShapeDtypeStruct((S, PW), F32),
        jax.ShapeDtypeStruct((S, PW), BF),
    )
    in_specs = [
        _rows(ts, D), _full(mod.shape), _full((1, D)), _full(w_in.shape), _full((1, QL)), _full((1, KVL)),
        _full(w_uq.shape), _full(wuk_dc.shape), _full(perm.shape), _rows(ts, 128), _rows(ts, 128), _rows(ts, 128),
        _rows(ts, 128), _full(w_pool.shape), _full((1, PW)),
    ]
    out_specs = (
        _rows(ts, D), _rows(ts, 384), _rows(ts, HEADS * NOPE),
        pl.BlockSpec((nsub, HEADS * TQ, QW), lambda i: (i, 0, 0)),
        _rows(ts, QW), _rows(ts, PW), _rows(ts, PW), _rows(ts, PW),
    )
    return pl.pallas_call(
        body, name="fwd_in", out_shape=out_shape, grid=(S // ts,), in_specs=in_specs, out_specs=out_specs,
        scratch_shapes=[pltpu.VMEM((16, PW), F32)], compiler_params=_params(("arbitrary",)),
    )(x, mod, g_mix, w_in, g_q, g_kv, w_uq, wuk_dc, perm, cos4, sin4, csk, snk, w_pool, pool_scale)


def _diag_mask(shape, q_axis):
    qi = (lax.broadcasted_iota(jnp.int32, shape, q_axis) & (TQ - 1)) >> 6
    ki = lax.broadcasted_iota(jnp.int32, shape, 1 - q_axis) >> 6
    return ki <= qi


def _attn_fwd(qs, kv, wuv_cv):
    nq = qs.shape[0]
    S = kv.shape[0]
    M = HEADS * TQ

    def body(qs_ref, kv_ref, wuv_ref, olat_ref, ymla_ref, lse_ref):
        i = pl.program_id(0)
        q = qs_ref[0]

        def step(kt, carry, masked):
            m, l, acc = carry
            k = kv_ref[pl.ds(pl.multiple_of(kt * TK, TK), TK), :]
            s = _dot_nt(q, k) * SM_SCALE
            if masked:
                s = jnp.where(_diag_mask((M, TK), 0), s, -jnp.inf)
            m_new = jnp.maximum(m, jnp.max(s, axis=-1, keepdims=True))
            alpha = jnp.exp(m - m_new)
            p = jnp.exp(s - m_new)
            l = alpha * l + jnp.sum(p, axis=-1, keepdims=True)
            acc = alpha * acc + _dot(p.astype(BF), k[:, 0:KVL])
            return m_new, l, acc

        init = (jnp.full((M, 1), -jnp.inf, F32), jnp.zeros((M, 1), F32), jnp.zeros((M, KVL), F32))
        carry = lax.fori_loop(0, i, lambda kt, c: step(kt, c, False), init)
        m, l, acc = step(i, carry, True)
        o_lat = acc / l
        olat_ref[0] = o_lat
        lse = m + jnp.log(l)
        lse_ref[0] = jnp.broadcast_to(lse, (M, 128)).T[0:8, :]
        for hd in range(HEADS):
            o = _dot(o_lat[hd * TQ:(hd + 1) * TQ, :].astype(BF), wuv_ref[hd])
            ymla_ref[:, hd * 128:(hd + 1) * 128] = o.astype(BF)

    out_shape = (
        jax.ShapeDtypeStruct((nq, M, KVL), F32),
        jax.ShapeDtypeStruct((S, HEADS * 128), BF),
        jax.ShapeDtypeStruct((nq, 8, M), F32),
    )
    return pl.pallas_call(
        body, name="attn_fwd", out_shape=out_shape, grid=(nq,),
        in_specs=[pl.BlockSpec((1, M, QW), lambda i: (i, 0, 0)), _full(kv.shape), _full(wuv_cv.shape)],
        out_specs=(pl.BlockSpec((1, M, KVL), lambda i: (i, 0, 0)), _rows(TQ, HEADS * 128),
                   pl.BlockSpec((1, 8, M), lambda i: (i, 0, 0))),
        compiler_params=_params(("arbitrary",)),
    )(qs, kv, wuv_cv)


def _silu_parts(a):
    sg = jax.nn.sigmoid(a)
    return sg, a * sg


def _ffn_fwd(x, ymla, ypool, mod, w_o, g_ffn, wg_t, wu_t, wd, g_final, target):
    S = x.shape[0]
    ts = 512
    tf = 256
    nj = FF // tf

    def body(x_ref, ymla_ref, ypool_ref, mod_ref, wo_ref, gffn_ref, wg_ref, wu_ref, wd_ref, gfin_ref, t_ref,
             x2_ref, mix_ref, h2t_ref, a_ref, b_ref, dx3_ref, dff_ref, dfft_ref, loss_ref, dgfin_ref, dgt2_ref,
             acc_ref, h2_ref):
        i = pl.program_id(0)
        j = pl.program_id(1)

        @pl.when(jnp.logical_and(i == 0, j == 0))
        def _():
            loss_ref[...] = jnp.zeros_like(loss_ref)
            dgfin_ref[...] = jnp.zeros_like(dgfin_ref)
            dgt2_ref[...] = jnp.zeros_like(dgt2_ref)

        @pl.when(j == 0)
        def _():
            gt1 = mod_ref[0:1, 2 * D:3 * D]
            sh2 = mod_ref[0:1, 3 * D:4 * D]
            sc2 = mod_ref[0:1, 4 * D:5 * D]
            cat = jnp.concatenate([ymla_ref[...], ypool_ref[...]], axis=1)
            mix = _dot(cat, wo_ref[...])
            mix_ref[...] = mix
            x2 = x_ref[...] + gt1 * mix
            x2_ref[...] = x2
            h2 = (x2 * _rms(x2)) * gffn_ref[...] * (1.0 + sc2) + sh2
            h2_ref[...] = h2.astype(BF)
            h2t_ref[...] = h2.T.astype(BF)
            acc_ref[...] = jnp.zeros_like(acc_ref)

        h2b = h2_ref[...]
        a = _dot_nt(h2b, wg_ref[...])
        b = _dot_nt(h2b, wu_ref[...])
        a_ref[...] = a.astype(BF)
        b_ref[...] = b.astype(BF)
        f = _silu_parts(a)[1] * b
        acc_ref[...] += _dot(f.astype(BF), wd_ref[...])

        @pl.when(j == nj - 1)
        def _():
            gt2 = mod_ref[0:1, 5 * D:6 * D]
            ff = acc_ref[...]
            x3 = x2_ref[...] + gt2 * ff
            r3 = _rms(x3)
            xn3 = x3 * r3
            gfin = gfin_ref[...]
            e = xn3 * gfin - t_ref[...]
            loss_ref[...] += 0.5 * jnp.sum(jnp.mean(e * e, axis=-1, keepdims=True))
            dy = e * (1.0 / D)
            dgfin_ref[...] += _colsum(dy * xn3)
            dx3 = _rms_bwd(dy * gfin, xn3, r3)
            dx3_ref[...] = dx3
            dgt2_ref[...] += _colsum(dx3 * ff)
            dff = dx3 * gt2
            dff_ref[...] = dff.astype(BF)
            dfft_ref[...] = dff.T.astype(BF)

    row = lambda c: pl.BlockSpec((ts, c), lambda i, j: (i, 0))
    col = pl.BlockSpec((D, ts), lambda i, j: (0, i))
    wblk = pl.BlockSpec((tf, D), lambda i, j: (j, 0))
    act = pl.BlockSpec((ts, tf), lambda i, j: (i, j))
    const = lambda shape: pl.BlockSpec(shape, lambda i, j: (0,) * len(shape))
    out_shape = (
        jax.ShapeDtypeStruct((S, D), F32),
        jax.ShapeDtypeStruct((S, D), F32),
        jax.ShapeDtypeStruct((D, S), BF),
        jax.ShapeDtypeStruct((S, FF), BF),
        jax.ShapeDtypeStruct((S, FF), BF),
        jax.ShapeDtypeStruct((S, D), F32),
        jax.ShapeDtypeStruct((S, D), BF),
        jax.ShapeDtypeStruct((D, S), BF),
        jax.ShapeDtypeStruct((8, 128), F32),
        jax.ShapeDtypeStruct((1, D), F32),
        jax.ShapeDtypeStruct((1, D), F32),
    )
    return pl.pallas_call(
        body, name="ffn_fwd", out_shape=out_shape, grid=(S // ts, nj),
        in_specs=[row(D), row(PW), row(PW), const(mod.shape), const(w_o.shape), const((1, D)), wblk, wblk, wblk,
                  const((1, D)), row(D)],
        out_specs=(row(D), row(D), col, act, act, row(D), row(D), col, const((8, 128)), const((1, D)),
                   const((1, D))),
        scratch_shapes=[pltpu.VMEM((ts, D), F32), pltpu.VMEM((ts, D), BF)],
        compiler_params=_params(("arbitrary", "arbitrary")),
    )(x, ymla, ypool, mod, w_o, g_ffn, wg_t, wu_t, wd, g_final, target)


def _ffn_bwd(dff, dff_t, h2_t, a, b, wg_t, wu_t, wd):
    S = dff.shape[0]
    ts = 1024
    tf = 256
    ni = S // ts
    nj = FF // tf

    def body(dff_ref, dfft_ref, h2t_ref, a_ref, b_ref, wg_ref, wu_ref, wd_ref,
             dwg_ref, dwu_ref, dwd_ref, dh2_ref, gacc, uacc, dacc, dh2acc):
        j = pl.program_id(0)
        i = pl.program_id(1)
        dffb = dff_ref[...]
        h2t = h2t_ref[...]
        av = a_ref[...].astype(F32)
        bv = b_ref[...].astype(F32)
        df = _dot_nt(dffb, wd_ref[...])
        sg, sa = _silu_parts(av)
        fb = (sa * bv).astype(BF)
        dbb = (df * sa).astype(BF)
        dab = (df * bv * (sg * (1.0 + av * (1.0 - sg)))).astype(BF)

        @pl.when(i == 0)
        def _():
            gacc[...] = jnp.zeros_like(gacc)
            uacc[...] = jnp.zeros_like(uacc)
            dacc[...] = jnp.zeros_like(dacc)

        gacc[...] += _dot(h2t, dab)
        uacc[...] += _dot(h2t, dbb)
        dacc[...] += _dot(dfft_ref[...], fb)
        contrib = _dot(dab, wg_ref[...]) + _dot(dbb, wu_ref[...])
        rows = pl.ds(pl.multiple_of(i * ts, ts), ts)

        @pl.when(j == 0)
        def _():
            dh2acc[rows, :] = contrib

        @pl.when(j > 0)
        def _():
            dh2acc[rows, :] += contrib

        @pl.when(i == ni - 1)
        def _():
            dwg_ref[...] = gacc[...].T.astype(BF)
            dwu_ref[...] = uacc[...].T.astype(BF)
            dwd_ref[...] = dacc[...].T.astype(BF)

        @pl.when(j == nj - 1)
        def _():
            dh2_ref[...] = dh2acc[rows, :]

    row = lambda c: pl.BlockSpec((ts, c), lambda j, i: (i, 0))
    col = pl.BlockSpec((D, ts), lambda j, i: (0, i))
    act = pl.BlockSpec((ts, tf), lambda j, i: (i, j))
    wblk = pl.BlockSpec((tf, D), lambda j, i: (j, 0))
    out_shape = (
        jax.ShapeDtypeStruct((FF, D), BF), jax.ShapeDtypeStruct((FF, D), BF), jax.ShapeDtypeStruct((FF, D), BF),
        jax.ShapeDtypeStruct((S, D), F32),
    )
    return pl.pallas_call(
        body, name="ffn_bwd", out_shape=out_shape, grid=(nj, ni),
        in_specs=[row(D), col, col, act, act, wblk, wblk, wblk],
        out_specs=(wblk, wblk, wblk, pl.BlockSpec((ts, D), lambda j, i: (jnp.where(j == nj - 1, i, 0), 0))),
        scratch_shapes=[pltpu.VMEM((D, tf), F32), pltpu.VMEM((D, tf), F32), pltpu.VMEM((D, tf), F32),
                        pltpu.VMEM((S, D), F32)],
        compiler_params=_params(("arbitrary", "arbitrary")),
    )(dff, dff_t, h2_t, a, b, wg_t, wu_t, wd)


def _mix_bwd(dh2, dx3, x2, mix, mod, g_ffn, ymla, ypool, w_o, ypre, pooled, pool_scale, wpool_dc, olat, wuv_vc):
    S = dh2.shape[0]
    ts = 512
    n = S // ts
    nsub = ts // TQ
    M = HEADS * TQ

    def body(dh2_ref, dx3_ref, x2_ref, mix_ref, mod_ref, gffn_ref, ymla_ref, ypool_ref, wo_ref, ypre_ref, pooled_ref,
             pscale_ref, wpool_ref, olat_ref, wuv_ref,
             dx2_ref, du_ref, dolat_ref, delta_ref, dwo_ref, dwuv_ref, dwpool_ref, dpscale_ref, dgt1_ref, dsc2_ref,
             dsh2_ref, dgffn_ref, carry_ref, dwo_acc):
        i = pl.program_id(0)

        @pl.when(i == 0)
        def _():
            carry_ref[...] = jnp.zeros_like(carry_ref)
            dwo_acc[...] = jnp.zeros_like(dwo_acc)
            for r in (dwuv_ref, dwpool_ref, dpscale_ref, dgt1_ref, dsc2_ref, dsh2_ref, dgffn_ref):
                r[...] = jnp.zeros_like(r)

        gt1 = mod_ref[0:1, 2 * D:3 * D]
        sc2 = mod_ref[0:1, 4 * D:5 * D]
        gffn = gffn_ref[...]
        dh2 = dh2_ref[...]
        x2 = x2_ref[...]
        r2 = _rms(x2)
        xn2 = x2 * r2
        dsc2_ref[...] += _colsum(dh2 * (xn2 * gffn))
        dsh2_ref[...] += _colsum(dh2)
        dgffn_ref[...] += _colsum(dh2 * (1.0 + sc2) * xn2)
        dx2 = dx3_ref[...] + _rms_bwd(dh2 * gffn * (1.0 + sc2), xn2, r2)
        dx2_ref[...] = dx2
        dgt1_ref[...] += _colsum(dx2 * mix_ref[...])
        dmix = (dx2 * gt1).astype(BF)
        cat = jnp.concatenate([ymla_ref[...], ypool_ref[...]], axis=1)
        dwo_acc[...] += _dot_tn(cat, dmix)
        dcat = _dot_nt(dmix, wo_ref[...])
        dymla = dcat[:, 0:512]
        dypool = dcat[:, 512:1024]

        dpscale_ref[...] += _colsum(dypool * ypre_ref[...])
        dypre = (dypool * pscale_ref[...]).astype(BF)
        pooled = pooled_ref[...]
        dpooled = []
        for g in range(GROUPS):
            sl = slice(g * GD, (g + 1) * GD)
            dwpool_ref[g] += _dot_tn(pooled[:, sl], dypre[:, sl])
            dpooled.append(_dot(dypre[:, sl], wpool_ref[g]))
        dpooled = jnp.concatenate(dpooled, axis=1)
        tile = n - 1 - i
        e = dpooled / _row_counts(tile * ts, ts)
        ext = jnp.concatenate([e, carry_ref[...]], axis=0)
        du_ref[...] = _window_sums(ext, False)[0:ts, :] - dpooled
        carry_ref[...] = e[0:16, :]

        for hd in range(HEADS):
            do = dymla[:, hd * 128:(hd + 1) * 128]
            dob = do.astype(BF)
            dol = _dot(dob, wuv_ref[hd])
            for a in range(nsub):
                ol = olat_ref[a, hd * TQ:(hd + 1) * TQ, :]
                dl = dol[a * TQ:(a + 1) * TQ, :]
                dolat_ref[a, hd * TQ:(hd + 1) * TQ, :] = dl.astype(BF)
                dwuv_ref[hd] += _dot_tn(ol, dob[a * TQ:(a + 1) * TQ, :])
                delta = jnp.sum(dl * ol, axis=-1, keepdims=True)
                delta_ref[a, :, hd * TQ:(hd + 1) * TQ] = jnp.broadcast_to(delta, (TQ, 128)).T[0:8, :]

        @pl.when(i == n - 1)
        def _():
            dwo_ref[...] = dwo_acc[...].astype(BF)

    rev = lambda c: pl.BlockSpec((ts, c), lambda i: (n - 1 - i, 0))
    rev3 = lambda r, c: pl.BlockSpec((nsub, r, c), lambda i: (n - 1 - i, 0, 0))
    out_shape = (
        jax.ShapeDtypeStruct((S, D), F32),
        jax.ShapeDtypeStruct((S, PW), F32),
        jax.ShapeDtypeStruct((S // TQ, M, KVL), BF),
        jax.ShapeDtypeStruct((S // TQ, 8, M), F32),
        jax.ShapeDtypeStruct((D, D), BF),
        jax.ShapeDtypeStruct((HEADS, KVL, 128), F32),
        jax.ShapeDtypeStruct((GROUPS, GD, GD), F32),
        jax.ShapeDtypeStruct((1, PW), F32),
        jax.ShapeDtypeStruct((1, D), F32), jax.ShapeDtypeStruct((1, D), F32), jax.ShapeDtypeStruct((1, D), F32),
        jax.ShapeDtypeStruct((1, D), F32),
    )
    in_specs = [rev(D), rev(D), rev(D), rev(D), _full(mod.shape), _full((1, D)), rev(PW), rev(PW), _full(w_o.shape),
                rev(PW), rev(PW), _full((1, PW)), _full(wpool_dc.shape), rev3(M, KVL), _full(wuv_vc.shape)]
    out_specs = (rev(D), rev(PW), rev3(M, KVL), rev3(8, M), _full((D, D)), _full((HEADS, KVL, 128)),
                 _full((GROUPS, GD, GD)), _full((1, PW)), _full((1, D)), _full((1, D)), _full((1, D)), _full((1, D)))
    return pl.pallas_call(
        body, name="mix_bwd", out_shape=out_shape, grid=(n,), in_specs=in_specs, out_specs=out_specs,
        scratch_shapes=[pltpu.VMEM((16, PW), F32), pltpu.VMEM((D, D), F32)],
        compiler_params=_params(("arbitrary",)),
    )(dh2, dx3, x2, mix, mod, g_ffn, ymla, ypool, w_o, ypre, pooled, pool_scale, wpool_dc, olat, wuv_vc)


def _attn_bwd(qs, kv, dolat, lse, delta):
    nq = qs.shape[0]
    S = kv.shape[0]
    M = HEADS * TQ
    nk = S // TK

    def body(qs_ref, kv_ref, do_ref, lse_ref, delta_ref, dkv_ref, dqt_ref):
        kt = pl.program_id(0)
        k = kv_ref[...]
        v = k[:, 0:KVL]
        k_t = k.astype(F32).T.astype(BF)

        @pl.when(kt == 0)
        def _():
            dqt_ref[...] = jnp.zeros_like(dqt_ref)

        def step(qi, carry, masked):
            dk, dv = carry
            q = qs_ref[qi]
            do = do_ref[qi]
            s = _dot_nt(k, q) * SM_SCALE
            p = jnp.exp(s - lse_ref[qi, 0:1, :])
            if masked:
                p = jnp.where(_diag_mask((TK, M), 1), p, 0.0)
            dp = _dot_nt(v, do)
            ds = (p * (dp - delta_ref[qi, 0:1, :]) * SM_SCALE).astype(BF)
            dv = dv + _dot(p.astype(BF), do)
            dk = dk + _dot(ds, q)
            dqt_ref[qi] += _dot(k_t, ds)
            return dk, dv

        carry = step(kt, (jnp.zeros((TK, QW), F32), jnp.zeros((TK, KVL), F32)), True)
        dk, dv = lax.fori_loop(kt + 1, nq, lambda qi, c: step(qi, c, False), carry)
        dkv_ref[...] = dk + jnp.concatenate([dv, jnp.zeros((TK, QW - KVL), F32)], axis=1)

    out_shape = (jax.ShapeDtypeStruct((S, QW), F32), jax.ShapeDtypeStruct((nq, QW, M), F32))
    return pl.pallas_call(
        body, name="attn_bwd", out_shape=out_shape, grid=(nk,),
        in_specs=[_vmem(), _rows(TK, QW), _vmem(), _vmem(), _vmem()],
        out_specs=(_rows(TK, QW), _vmem()),
        compiler_params=_params(("arbitrary",)),
    )(qs, kv, dolat, lse, delta)


def _in_bwd(dqt, dkv, du, raw, qn, h1, x, dx2, mod, g_mix, w_in, g_q, g_kv, w_uq, wuk_cd, perm_t, cos4, sin4, csk,
            snk):
    S = x.shape[0]
    ts = 512
    n = S // ts
    nsub = ts // TQ
    M = HEADS * TQ

    def body(dqt_ref, dkv_ref, du_ref, raw_ref, qn_ref, h1_ref, x_ref, dx2_ref, mod_ref, gmix_ref, win_ref, gq_ref,
             gkv_ref, wuq_ref, wuk_ref, permt_ref, cos_ref, sin_ref, csk_ref, snk_ref,
             dx_ref, dwin_ref, dwuq_ref, dwuk_ref, dgq_ref, dgkv_ref, dsc1_ref, dsh1_ref, dgmix_ref, dwin_acc,
             dwuq_acc):
        i = pl.program_id(0)

        @pl.when(i == 0)
        def _():
            dwin_acc[...] = jnp.zeros_like(dwin_acc)
            dwuq_acc[...] = jnp.zeros_like(dwuq_acc)
            for r in (dwuk_ref, dgq_ref, dgkv_ref, dsc1_ref, dsh1_ref, dgmix_ref):
                r[...] = jnp.zeros_like(r)

        dq_blocks = [dqt_ref[a].T for a in range(nsub)]
        qn = qn_ref[...]
        dq_parts = []
        drope = jnp.zeros((ts, 2 * 128), F32)
        for hd in range(HEADS):
            dqh = jnp.concatenate([blk[hd * TQ:(hd + 1) * TQ, :] for blk in dq_blocks], axis=0)
            dq_lat = dqh[:, 0:KVL].astype(BF)
            dq_parts.append(_dot(dq_lat, wuk_ref[hd]))
            dwuk_ref[hd] += _dot_tn(dq_lat, qn[:, hd * NOPE:(hd + 1) * NOPE])
            drope = drope + _dot(dqh[:, KVL:QW].astype(BF), permt_ref[hd])
        do1 = drope[:, 0:128]
        do2 = drope[:, 128:256]
        cosv = cos_ref[...]
        sinv = sin_ref[...]
        dq_parts.append(do1 * cosv + do2 * sinv)
        dq_parts.append(do2 * cosv - do1 * sinv)
        dq = jnp.concatenate(dq_parts, axis=1).astype(BF)

        cq_raw = raw_ref[:, 0:QL]
        ckv_raw = raw_ref[:, QL:QL + KVL]
        rq = _rms(cq_raw)
        nq_ = cq_raw * rq
        gq = gq_ref[...]
        dwuq_acc[...] += _dot_tn((nq_ * gq).astype(BF), dq)
        dc_q = _dot_nt(dq, wuq_ref[...])
        dgq_ref[...] += _colsum(dc_q * nq_)
        dcq_raw = _rms_bwd(dc_q * gq, nq_, rq)

        dkv = dkv_ref[...]
        rk = _rms(ckv_raw)
        nk_ = ckv_raw * rk
        dc_kv = dkv[:, 0:KVL]
        dgkv_ref[...] += _colsum(dc_kv * nk_)
        dckv_raw = _rms_bwd(dc_kv * gkv_ref[...], nk_, rk)
        dkr_roped = dkv[:, KVL:QW]
        dkr = dkr_roped * csk_ref[...] - _swap_halves(dkr_roped) * snk_ref[...]

        dproj = jnp.concatenate([dcq_raw, dckv_raw, dkr, du_ref[...]], axis=1).astype(BF)
        dwin_acc[...] += _dot_tn(h1_ref[...], dproj)
        dh1 = _dot_nt(dproj, win_ref[...])

        sc1 = mod_ref[0:1, D:2 * D]
        gmix = gmix_ref[...]
        xv = x_ref[...]
        r1 = _rms(xv)
        xn1 = xv * r1
        dsc1_ref[...] += _colsum(dh1 * (xn1 * gmix))
        dsh1_ref[...] += _colsum(dh1)
        dgmix_ref[...] += _colsum(dh1 * (1.0 + sc1) * xn1)
        dx_ref[...] = dx2_ref[...] + _rms_bwd(dh1 * gmix * (1.0 + sc1), xn1, r1)

        @pl.when(i == n - 1)
        def _():
            dwin_ref[...] = dwin_acc[...].astype(BF)
            dwuq_ref[...] = dwuq_acc[...].astype(BF)

    out_shape = (
        jax.ShapeDtypeStruct((S, D), F32),
        jax.ShapeDtypeStruct((D, D), BF),
        jax.ShapeDtypeStruct((QL, 768), BF),
        jax.ShapeDtypeStruct((HEADS, KVL, NOPE), F32),
        jax.ShapeDtypeStruct((1, QL), F32), jax.ShapeDtypeStruct((1, KVL), F32),
        jax.ShapeDtypeStruct((1, D), F32), jax.ShapeDtypeStruct((1, D), F32), jax.ShapeDtypeStruct((1, D), F32),
    )
    in_specs = [pl.BlockSpec((nsub, QW, M), lambda i: (i, 0, 0)), _rows(ts, QW), _rows(ts, PW), _rows(ts, 384),
                _rows(ts, HEADS * NOPE), _rows(ts, D), _rows(ts, D), _rows(ts, D), _full(mod.shape), _full((1, D)),
                _full(w_in.shape), _full((1, QL)), _full((1, KVL)), _full(w_uq.shape), _full(wuk_cd.shape),
                _full(perm_t.shape), _rows(ts, 128), _rows(ts, 128), _rows(ts, 128), _rows(ts, 128)]
    out_specs = (_rows(ts, D), _full((D, D)), _full((QL, 768)), _full((HEADS, KVL, NOPE)), _full((1, QL)),
                 _full((1, KVL)), _full((1, D)), _full((1, D)), _full((1, D)))
    return pl.pallas_call(
        body, name="in_bwd", out_shape=out_shape, grid=(n,), in_specs=in_specs, out_specs=out_specs,
        scratch_shapes=[pltpu.VMEM((D, D), F32), pltpu.VMEM((QL, 768), F32)],
        compiler_params=_params(("arbitrary",)),
    )(dqt, dkv, du, raw, qn, h1, x, dx2, mod, g_mix, w_in, g_q, g_kv, w_uq, wuk_cd, perm_t, cos4, sin4, csk, snk)


def _rope_perm():
    p = np.zeros((HEADS, 2 * 128, 128), np.float32)
    for hd in range(HEADS):
        for t in range(HALF):
            p[hd, hd * HALF + t, t] = 1.0
            p[hd, 128 + hd * HALF + t, HALF + t] = 1.0
    return p


def _rope_tables(positions):
    freqs = jnp.power(ROPE_THETA, -jnp.arange(HALF, dtype=F32) / HALF)
    ang = positions.astype(F32)[:, None] * jnp.tile(freqs, HEADS)[None, :]
    cos4 = jnp.cos(ang)
    sin4 = jnp.sin(ang)
    lane = jnp.arange(HEADS * HALF)[None, :]
    csk = jnp.where(lane < ROPE, cos4, 0.0)
    snk = jnp.where(lane < HALF, -sin4, jnp.where(lane < ROPE, sin4, 0.0))
    return cos4, sin4, csk, snk


def _local_step(x, positions, target, mod, g_mix, w_in_p, g_q, g_kv, w_uq_p, w_uk, w_uv, w_pool, pool_scale, g_ffn,
                g_final, late, ffn_grads_exchange, tail_grads_exchange):
    perm = jnp.asarray(_rope_perm(), BF)
    perm_t = jnp.asarray(_rope_perm().transpose(0, 2, 1), BF)
    cos4, sin4, csk, snk = _rope_tables(positions)
    wuk_dc = w_uk.transpose(1, 2, 0).astype(BF)
    wuk_cd = w_uk.transpose(1, 0, 2).astype(BF)
    wuv_cv = w_uv.transpose(1, 0, 2).astype(BF)
    wuv_vc = w_uv.transpose(1, 2, 0).astype(BF)
    wpool = w_pool.astype(BF)
    wpool_dc = w_pool.transpose(0, 2, 1).astype(BF)

    h1, raw, qn, qs, kv, pooled, ypre, ypool = _fwd_in(
        x, mod, g_mix, w_in_p, g_q, g_kv, w_uq_p, wuk_dc, perm, cos4, sin4, csk, snk, wpool, pool_scale)
    olat, ymla, lse = _attn_fwd(qs, kv, wuv_cv)
    w_o, wg_t, wu_t, wd = late
    x2, mix, h2_t, a, b, dx3, dff, dff_t, loss, dgfin, dgt2 = _ffn_fwd(
        x, ymla, ypool, mod, w_o, g_ffn, wg_t, wu_t, wd, g_final, target)
    dwg_t, dwu_t, dwd, dh2 = _ffn_bwd(dff, dff_t, h2_t, a, b, wg_t, wu_t, wd)
    ffn_parts = ffn_grads_exchange((dwg_t, dwu_t, dwd))
    (dx2, du, dolat, delta, dwo, dwuv, dwpool, dpscale, dgt1, dsc2, dsh2, dgffn) = _mix_bwd(
        dh2, dx3, x2, mix, mod, g_ffn, ymla, ypool, w_o, ypre, pooled, pool_scale, wpool_dc, olat, wuv_vc)
    dkv, dqt = _attn_bwd(qs, kv, dolat, lse, delta)
    dx, dwin, dwuq, dwuk, dgq, dgkv, dsc1, dsh1, dgmix = _in_bwd(
        dqt, dkv, du, raw, qn, h1, x, dx2, mod, g_mix, w_in_p, g_q, g_kv, w_uq_p, wuk_cd, perm_t, cos4, sin4, csk,
        snk)
    dmod = jnp.concatenate([dsh1, dsc1, dgt1, dsh2, dsc2, dgt2], axis=1)
    parts = (tail_grads_exchange((dwin, dwuq, dwo), ffn_parts), ffn_parts)
    replicated = dict(
        w_uk=dwuk.transpose(1, 0, 2), w_uv=dwuv.transpose(1, 0, 2), w_pool=dwpool, g_mix=dgmix, g_q=dgq, g_kv=dgkv,
        pool_scale=dpscale, g_ffn=dgffn, g_final=dgfin)
    return loss[0, 0], dx, dmod, parts, replicated


def _my_pos():
    return lax.axis_index("x"), lax.axis_index("y"), lax.axis_index("c")


def _peer(pos, k):
    x, y, c = pos
    return (1 - x if k & 4 else x, 1 - y if k & 2 else y, 1 - c if k & 1 else c)


def _index(pos):
    x, y, c = pos
    return 4 * x + 2 * y + c


def _remote(src, dst, send_sem, recv_sem, to):
    return pltpu.make_async_remote_copy(src_ref=src, dst_ref=dst, send_sem=send_sem, recv_sem=recv_sem,
                                        device_id=to, device_id_type=MESH)


def _ada_mod(c, w_ada, b_ada):
    def body(c_ref, w_ref, b_ref, mod_ref, call_ref, cbuf, sbuf, rbuf, send1, recv1, send2, recv2):
        me = _my_pos()
        mi = _index(me)
        cv = c_ref[...]
        cbuf[...] = jnp.broadcast_to(cv * jax.nn.sigmoid(cv), (8, D))
        call_ref[mi] = cbuf[...]
        first = [_remote(cbuf, call_ref.at[mi], send1.at[k - 1], recv1.at[k - 1], _peer(me, k)) for k in range(1, NDEV)]
        for cp in first:
            cp.start()
        for k in range(1, NDEV):
            _remote(cbuf, call_ref.at[_index(_peer(me, k))], send1.at[k - 1], recv1.at[k - 1], _peer(me, k)).wait_recv()
        c_all = jnp.concatenate([call_ref[b][0:1, :] for b in range(NDEV)], axis=0)
        blocks = _dot(c_all.astype(BF), w_ref[...].astype(BF))
        for b in range(NDEV):
            sbuf[b] = jnp.broadcast_to(blocks[b:b + 1, :], (8, MODC))
        second = []
        for k in range(1, NDEV):
            to = _peer(me, k)
            second.append(_remote(sbuf.at[_index(to)], rbuf.at[mi], send2.at[k - 1], recv2.at[k - 1], to))
        for cp in second:
            cp.start()
        rbuf[mi] = sbuf[mi]
        for k in range(1, NDEV):
            to = _peer(me, k)
            _remote(sbuf.at[_index(to)], rbuf.at[_index(to)], send2.at[k - 1], recv2.at[k - 1], to).wait_recv()
        for j in range(NDEV):
            mod_ref[:, j * MODC:(j + 1) * MODC] = rbuf[j] + b_ref[:, j * MODC:(j + 1) * MODC]
        for cp in first + second:
            cp.wait_send()

    return pl.pallas_call(
        body, name="ada_mod",
        out_shape=(jax.ShapeDtypeStruct((8, N_MOD * D), F32), jax.ShapeDtypeStruct((NDEV, 8, D), F32)),
        in_specs=[_vmem(), _vmem(), _vmem()], out_specs=(_vmem(), _vmem()),
        scratch_shapes=[pltpu.VMEM((8, D), F32), pltpu.VMEM((NDEV, 8, MODC), F32), pltpu.VMEM((NDEV, 8, MODC), F32),
                        pltpu.SemaphoreType.DMA((NDEV - 1,)), pltpu.SemaphoreType.DMA((NDEV - 1,)),
                        pltpu.SemaphoreType.DMA((NDEV - 1,)), pltpu.SemaphoreType.DMA((NDEV - 1,))],
        compiler_params=_params(),
    )(c, w_ada, b_ada)


def _exchange_slices(scatter):
    def of(src, to_index):
        if not scatter:
            return src
        r = src.shape[0] // NDEV
        return src.at[pl.ds(pl.multiple_of(to_index * r, 16), r), :]
    return of


def _sequencer_exchange(name, collective_id, srcs, scatter, after=()):
    n = len(srcs)
    of = _exchange_slices(scatter)
    src_refs = [jax.new_ref(s, memory_space=pltpu.MemorySpace.HBM) for s in srcs]
    zone_refs = [
        jax.empty_ref(jax.ShapeDtypeStruct((NDEV, s.shape[0] // NDEV if scatter else s.shape[0], s.shape[1]), s.dtype),
                      memory_space=pltpu.MemorySpace.HBM) for s in srcs]
    after_refs = [jax.new_ref(t, memory_space=pltpu.MemorySpace.HBM) for t in after]
    seen_refs = [jax.empty_ref(jax.ShapeDtypeStruct(t.shape, t.dtype), memory_space=pltpu.MemorySpace.HBM)
                 for t in after]

    @pl.kernel(mesh=plsc.ScalarSubcoreMesh(axis_name="sequencer", num_cores=1), name=name,
               scratch_types=(pltpu.SemaphoreType.DMA((n * (NDEV - 1),)), pltpu.SemaphoreType.DMA((n * (NDEV - 1),)),
                              pltpu.SemaphoreType.DMA((n + len(after),))),
               compiler_params=pltpu.CompilerParams(collective_id=collective_id))
    def launch(send, recv, local):
        me = _my_pos()
        mi = _index(me)
        barrier = pltpu.get_barrier_semaphore()
        for k in range(1, NDEV):
            pl.semaphore_signal(barrier, inc=1, device_id=_peer(me, k), device_id_type=MESH)
        pl.semaphore_wait(barrier, NDEV - 1)
        own = [pltpu.make_async_copy(of(src_refs[a], mi), zone_refs[a].at[mi], local.at[a]) for a in range(n)]
        own += [pltpu.make_async_copy(t, seen, local.at[n + q])
                for q, (t, seen) in enumerate(zip(after_refs, seen_refs))]
        for cp in own:
            cp.start()
        for a in range(n):
            for k in range(1, NDEV):
                to = _peer(me, k)
                s = a * (NDEV - 1) + k - 1
                _remote(of(src_refs[a], _index(to)), zone_refs[a].at[mi], send.at[s], recv.at[s], to).start()
        for cp in own:
            cp.wait()
        for a in range(n):
            for k in range(1, NDEV):
                to = _peer(me, k)
                s = a * (NDEV - 1) + k - 1
                cp = _remote(of(src_refs[a], mi), zone_refs[a].at[_index(to)], send.at[s], recv.at[s], to)
                cp.wait_send()
                cp.wait_recv()

    launch()
    return tuple(z[...] for z in zone_refs)


def _sum_partials(name, parts):
    n = len(parts)

    def body(*refs):
        for a in range(n):
            acc = refs[a][0].astype(F32)
            for p in range(1, NDEV):
                acc = acc + refs[a][p].astype(F32)
            refs[n + a][...] = acc

    return pl.pallas_call(
        body, name=name,
        out_shape=tuple(jax.ShapeDtypeStruct(p.shape[1:], F32) for p in parts),
        in_specs=[_vmem()] * n, out_specs=tuple([_vmem()] * n), compiler_params=_params(),
    )(*parts)


def _small_all_reduce(buf):
    def body(buf_ref, got_ref, red_ref, mine, send1, recv1, send2, recv2):
        me = _my_pos()
        mi = _index(me)
        first = []
        for k in range(1, NDEV):
            to = _peer(me, k)
            first.append(_remote(buf_ref.at[_index(to)], got_ref.at[mi], send1.at[k - 1], recv1.at[k - 1], to))
        for cp in first:
            cp.start()
        got_ref[mi] = buf_ref[mi]
        for k in range(1, NDEV):
            to = _peer(me, k)
            _remote(buf_ref.at[mi], got_ref.at[_index(to)], send1.at[k - 1], recv1.at[k - 1], to).wait_recv()
        acc = got_ref[0]
        for p in range(1, NDEV):
            acc = acc + got_ref[p]
        mine[...] = acc
        second = [_remote(mine, red_ref.at[mi], send2.at[k - 1], recv2.at[k - 1], _peer(me, k)) for k in range(1, NDEV)]
        for cp in second:
            cp.start()
        red_ref[mi] = acc
        for k in range(1, NDEV):
            to = _peer(me, k)
            _remote(mine, red_ref.at[_index(to)], send2.at[k - 1], recv2.at[k - 1], to).wait_recv()
        for cp in first + second:
            cp.wait_send()

    return pl.pallas_call(
        body, name="small_all_reduce",
        out_shape=(jax.ShapeDtypeStruct(buf.shape, F32), jax.ShapeDtypeStruct(buf.shape, F32)),
        in_specs=[_vmem()], out_specs=(_vmem(), _vmem()),
        scratch_shapes=[pltpu.VMEM(buf.shape[1:], F32),
                        pltpu.SemaphoreType.DMA((NDEV - 1,)), pltpu.SemaphoreType.DMA((NDEV - 1,)),
                        pltpu.SemaphoreType.DMA((NDEV - 1,)), pltpu.SemaphoreType.DMA((NDEV - 1,))],
        compiler_params=_params(),
    )(buf)


def _adamw_math(w, g, m, v):
    m = ADAM_B1 * m + (1.0 - ADAM_B1) * g
    v = ADAM_B2 * v + (1.0 - ADAM_B2) * jnp.square(g)
    m_hat = m / (1.0 - ADAM_B1 ** ADAM_STEP)
    v_hat = v / (1.0 - ADAM_B2 ** ADAM_STEP)
    delta = -ADAM_LR * (m_hat / (jnp.sqrt(v_hat) + ADAM_EPS) + ADAM_WD * w)
    return delta, m, v


def _adamw_group(name, ws, gs, ms, vs):
    n = len(ws)

    def body(*refs):
        for a in range(n):
            w, g, m, v = (refs[q * n + a][...] for q in range(4))
            delta, m2, v2 = _adamw_math(w, g, m, v)
            refs[4 * n + a][...] = delta
            refs[5 * n + a][...] = m2
            refs[6 * n + a][...] = v2

    shapes = tuple(jax.ShapeDtypeStruct(w.shape, F32) for w in ws)
    outs = pl.pallas_call(
        body, name=name, out_shape=shapes * 3, in_specs=[_vmem()] * (4 * n), out_specs=tuple([_vmem()] * (3 * n)),
        compiler_params=_params(),
    )(*ws, *gs, *ms, *vs)
    return outs[:n], outs[n:2 * n], outs[2 * n:]


def _adamw_ada(w, m, v, c_all, dmod_rows):
    def body(w_ref, m_ref, v_ref, c_ref, dm_ref, g_ref, d_ref, m2_ref, v2_ref):
        g = _dot_tn(c_ref[...], dm_ref[...].astype(BF))
        g_ref[...] = g
        delta, m2, v2 = _adamw_math(w_ref[...], g, m_ref[...], v_ref[...])
        d_ref[...] = delta
        m2_ref[...] = m2
        v2_ref[...] = v2

    shp = jax.ShapeDtypeStruct(w.shape, F32)
    return pl.pallas_call(
        body, name="adamw_ada", out_shape=(shp, shp, shp, shp), in_specs=[_vmem()] * 5,
        out_specs=tuple([_vmem()] * 4), compiler_params=_params(),
    )(w, m, v, c_all, dmod_rows)


def _w_in_to_kernel(w):
    return jnp.concatenate([w[:, 0:448], jnp.zeros((w.shape[0], 64), w.dtype), w[:, 448:960]], axis=1)


def _w_in_from_kernel(w):
    return jnp.concatenate([w[:, 0:448], w[:, 512:1024]], axis=1)


def _w_uq_to_kernel(w):
    r = w.shape[0]
    return jnp.concatenate([w[:, :, 0:NOPE].reshape(r, HEADS * NOPE),
                            w[:, :, NOPE:NOPE + HALF].reshape(r, HEADS * HALF),
                            w[:, :, NOPE + HALF:].reshape(r, HEADS * HALF)], axis=1)


def _w_uq_from_kernel(w):
    r = w.shape[0]
    return jnp.concatenate([w[:, 0:512].reshape(r, HEADS, NOPE), w[:, 512:640].reshape(r, HEADS, HALF),
                            w[:, 640:768].reshape(r, HEADS, HALF)], axis=2)


REP_NAMES = ("w_uk", "w_uv", "w_pool", "g_mix", "g_q", "g_kv", "pool_scale", "g_ffn", "g_final")


def kernel(x, c, positions, w_ada, b_ada, g_mix, w_in, g_q, g_kv, w_uq, w_uk, w_uv, w_pool, pool_scale, w_o, g_ffn, w_gate, w_up, w_down, g_final, loss_target, m_w_ada, m_b_ada, m_g_mix, m_w_in, m_g_q, m_g_kv, m_w_uq, m_w_uk, m_w_uv, m_w_pool, m_pool_scale, m_w_o, m_g_ffn, m_w_gate, m_w_up, m_w_down, m_g_final, v_w_ada, v_b_ada, v_g_mix, v_w_in, v_g_q, v_g_kv, v_w_uq, v_w_uk, v_w_uv, v_w_pool, v_pool_scale, v_w_o, v_g_ffn, v_w_gate, v_w_up, v_w_down, v_g_final):
    given = dict(locals())

    merge = lambda g: g.reshape(NDEV * g.shape[1], g.shape[2])
    w_in_p, w_uq_p = (merge(g) for g in _sequencer_exchange(
        "gather_in", 3, (_w_in_to_kernel(w_in[0]).astype(BF), _w_uq_to_kernel(w_uq[0]).astype(BF)), False))

    mod, c_all8 = _ada_mod(c, w_ada[0], b_ada)
    c_all = c_all8[:, 0, :]
    late = _sequencer_exchange(
        "gather_late", 1, (w_o[0].astype(BF), w_gate[0].T.astype(BF), w_up[0].T.astype(BF), w_down[0].astype(BF)),
        False, after=(mod[:, 0:128], w_in_p[0:16, 0:128], w_uq_p[0:16, 0:128]))

    def ffn_grads_exchange(arrays):
        return _sequencer_exchange("scatter_ffn", 2, arrays, True)

    def tail_grads_exchange(arrays, ffn_parts):
        return _sequencer_exchange("scatter_tail", 4, arrays, True, after=(ffn_parts[0][0, 0:16, 0:128],))

    loss, dx, dmod, (tail_parts, ffn_parts), replicated = _local_step(
        x[0], positions[0], loss_target[0], mod, g_mix, w_in_p, g_q, g_kv, w_uq_p, w_uk[0], w_uv[0], w_pool[0],
        pool_scale, g_ffn, g_final.reshape(1, D), tuple(merge(g) for g in late), ffn_grads_exchange,
        tail_grads_exchange)

    g_gate_t, g_up_t, g_down = _sum_partials("sum_ffn_partials", ffn_parts)
    g_in_p, g_uq_p, g_o = _sum_partials("sum_tail_partials", tail_parts)
    grads = dict(w_in=_w_in_from_kernel(g_in_p), w_uq=_w_uq_from_kernel(g_uq_p).reshape(QL // NDEV, HEADS * 192),
                 w_o=g_o, w_gate=g_gate_t.T, w_up=g_up_t.T, w_down=g_down)

    flat = jnp.concatenate([replicated[k].reshape(-1) for k in REP_NAMES] + [loss.reshape(1)])
    flat = jnp.pad(flat, (0, NDEV * REP_ROWS * 128 - flat.shape[0])).reshape(NDEV, REP_ROWS, 128)
    dmod_blocks = jnp.pad(dmod.reshape(NDEV, MODC // 128, 128), ((0, 0), (0, MOD_ROWS - MODC // 128), (0, 0)))
    got, red = _small_all_reduce(jnp.concatenate([dmod_blocks, flat], axis=1))
    dmod_rows = got[:, 0:MODC // 128, :].reshape(NDEV, MODC)
    grads["b_ada"] = red[:, 0:MODC // 128, :].reshape(1, N_MOD * D)
    rep_flat = red[:, MOD_ROWS:, :].reshape(-1)
    off = 0
    for k in REP_NAMES:
        size = int(np.prod(given[k].shape))
        grads[k] = rep_flat[off:off + size]
        off += size

    view = dict(w_ada=(D, MODC), b_ada=(1, N_MOD * D), g_mix=(1, D), w_in=(D // NDEV, 960), g_q=(1, QL),
                g_kv=(1, KVL), w_uq=(QL // NDEV, HEADS * 192), w_uk=(KVL, HEADS * NOPE), w_uv=(KVL, HEADS * 128),
                w_pool=(GROUPS * GD, GD), pool_scale=(1, PW), w_o=(D // NDEV, D), g_ffn=(1, D),
                w_gate=(D, FF // NDEV), w_up=(D, FF // NDEV), w_down=(FF // NDEV, D), g_final=(1, D))
    names = list(view)
    g_ada, d_ada, m_ada, v_ada = _adamw_ada(w_ada[0], m_w_ada[0], v_w_ada[0], c_all.astype(BF), dmod_rows)
    out_g, out_d, out_m, out_v = dict(w_ada=g_ada), dict(w_ada=d_ada), dict(w_ada=m_ada), dict(w_ada=v_ada)
    tail = ("w_in", "w_uq", "w_o")
    groups = (("adamw_ffn", ("w_gate", "w_up", "w_down")),
              ("adamw_replicated", tuple(k for k in names if k not in ("w_ada", "w_gate", "w_up", "w_down") + tail)),
              ("adamw_tail", tail))
    for gname, members in groups:
        ws = [given[k].reshape(view[k]) for k in members]
        gs = [grads[k].reshape(view[k]) for k in members]
        ms = [given["m_" + k].reshape(view[k]) for k in members]
        vs = [given["v_" + k].reshape(view[k]) for k in members]
        ds, m2, v2 = _adamw_group(gname, ws, gs, ms, vs)
        for k, g, d, mm, vv in zip(members, gs, ds, m2, v2):
            out_g[k], out_d[k], out_m[k], out_v[k] = g, d, mm, vv

    total = rep_flat[off]
    shaped = lambda d: [d[k].reshape(given[k].shape) for k in names]
    return (total, dx[None], *shaped(out_g), *shaped(out_d), *shaped(out_m), *shaped(out_v))
```

```python
import numpy as np
import jax
import jax.numpy as jnp
from jax import lax
from jax.experimental import pallas as pl
from jax.experimental.pallas import tpu as pltpu
from jax.experimental.pallas import tpu_sc as plsc

D = 1024
HEADS = 4
NOPE = 128
ROPE = 64
HALF = ROPE // 2
QL = 256
KVL = 128
FF = 2816
PW = 512
GROUPS = 4
GD = 128
N_MOD = 6
EPS = 1e-6
SM_SCALE = (NOPE + ROPE) ** -0.5
ROPE_THETA = 10000.0
NDEV = 8
MODC = N_MOD * D // NDEV

ADAM_LR = 0.001
ADAM_B1 = 0.9
ADAM_B2 = 0.999
ADAM_EPS = 1e-08
ADAM_WD = 0.01
ADAM_STEP = 10

BF = jnp.bfloat16
F32 = jnp.float32
VMEM_LIMIT_V7X = 60 * 1024 * 1024
MESH = pl.DeviceIdType.MESH

TQ = 256
TK = 256
QW = 256
MOD_ROWS = 8
REP_ROWS = 200
SMALL_ROWS = MOD_ROWS + REP_ROWS


def _params(sem=None):
    return pltpu.CompilerParams(dimension_semantics=sem, vmem_limit_bytes=VMEM_LIMIT_V7X)


def _dot(a, b):
    return jnp.dot(a, b, preferred_element_type=F32)


def _dot_nt(a, b):
    return lax.dot_general(a, b, (((1,), (1,)), ((), ())), preferred_element_type=F32)


def _dot_tn(a, b):
    return _dot(a.astype(F32).T.astype(BF), b)


def _full(shape):
    return pl.BlockSpec(shape, lambda *_: (0,) * len(shape))


def _rows(ts, cols):
    return pl.BlockSpec((ts, cols), lambda i: (i, 0))


def _vmem():
    return pl.BlockSpec(memory_space=pltpu.VMEM)


def _any():
    return pl.BlockSpec(memory_space=pl.ANY)


def _rms(v):
    return lax.rsqrt(jnp.mean(v * v, axis=-1, keepdims=True) + EPS)


def _rms_bwd(dn, n, r):
    return r * (dn - n * jnp.mean(dn * n, axis=-1, keepdims=True))


def _colsum(v):
    return jnp.sum(v, axis=0, keepdims=True)


def _swap_halves(v):
    lane = lax.broadcasted_iota(jnp.int32, v.shape, 1)
    return jnp.where(lane < HALF, pltpu.roll(v, 128 - HALF, 1), pltpu.roll(v, HALF, 1))


def _window_lane_width():
    lane = lax.broadcasted_iota(jnp.int32, (1, PW), 1)
    return jnp.where(lane < 128, 2.0, jnp.where(lane < 256, 4.0, jnp.where(lane < 384, 8.0, 16.0))).astype(F32)


def _window_sums(ext, back):
    n = ext.shape[0]

    def sh(v, k):
        return pltpu.roll(v, k if back else n - k, 0)

    s2 = ext + sh(ext, 1)
    e4 = s2[:, 128:]
    s4 = e4 + sh(e4, 2)
    e8 = s4[:, 128:]
    s8 = e8 + sh(e8, 4)
    e16 = s8[:, 128:]
    s16 = e16 + sh(e16, 8)
    return jnp.concatenate([s2[:, :128], s4[:, :128], s8[:, :128], s16], axis=1)


def _row_counts(first_row, ts):
    t1 = (first_row + lax.broadcasted_iota(jnp.int32, (ts, 1), 0) + 1).astype(F32)
    return jnp.minimum(t1, _window_lane_width())


def _fwd_in(x, mod, g_mix, w_in, g_q, g_kv, w_uq, wuk_dc, perm, cos4, sin4, csk, snk, w_pool, pool_scale):
    S = x.shape[0]
    ts = 512
    nsub = ts // TQ

    def body(x_ref, mod_ref, gmix_ref, win_ref, gq_ref, gkv_ref, wuq_ref, wuk_ref, perm_ref, cos_ref, sin_ref,
             csk_ref, snk_ref, wpool_ref, pscale_ref,
             h1_ref, raw_ref, qn_ref, qs_ref, kv_ref, pooled_ref, ypre_ref, ypool_ref, carry_ref):
        i = pl.program_id(0)

        @pl.when(i == 0)
        def _():
            carry_ref[...] = jnp.zeros_like(carry_ref)

        xv = x_ref[...]
        sh1 = mod_ref[0:1, 0:D]
        sc1 = mod_ref[0:1, D:2 * D]
        h = (xv * _rms(xv)) * gmix_ref[...] * (1.0 + sc1) + sh1
        hb = h.astype(BF)
        h1_ref[...] = hb
        proj = _dot(hb, win_ref[...])
        cq_raw = proj[:, 0:QL]
        ckv_raw = proj[:, QL:QL + KVL]
        kr = proj[:, 384:512]
        u = proj[:, 512:1024]
        raw_ref[...] = proj[:, 0:384]

        c_q = (cq_raw * _rms(cq_raw)) * gq_ref[...]
        c_kv = (ckv_raw * _rms(ckv_raw)) * gkv_ref[...]
        q = _dot(c_q.astype(BF), wuq_ref[...])
        qn = q[:, 0:HEADS * NOPE].astype(BF)
        qn_ref[...] = qn
        x1 = q[:, 512:640]
        x2 = q[:, 640:768]
        cosv = cos_ref[...]
        sinv = sin_ref[...]
        roped = jnp.concatenate([x1 * cosv - x2 * sinv, x1 * sinv + x2 * cosv], axis=1).astype(BF)
        for hd in range(HEADS):
            q_lat = _dot(qn[:, hd * NOPE:(hd + 1) * NOPE], wuk_ref[hd])
            q_rope = _dot(roped, perm_ref[hd])
            qh = jnp.concatenate([q_lat, q_rope], axis=1).astype(BF)
            for a in range(nsub):
                qs_ref[a, hd * TQ:(hd + 1) * TQ, :] = qh[a * TQ:(a + 1) * TQ, :]
        k_rope = kr * csk_ref[...] + _swap_halves(kr) * snk_ref[...]
        kv_ref[...] = jnp.concatenate([c_kv, k_rope], axis=1).astype(BF)

        ext = jnp.concatenate([carry_ref[...], u], axis=0)
        win = _window_sums(ext, True)[16:, :]
        pooled = (win / _row_counts(i * ts, ts) - u).astype(BF)
        pooled_ref[...] = pooled
        carry_ref[...] = u[ts - 16:ts, :]
        ypre = jnp.concatenate(
            [_dot(pooled[:, g * GD:(g + 1) * GD], wpool_ref[g]) for g in range(GROUPS)], axis=1)
        ypre_ref[...] = ypre
        ypool_ref[...] = (ypre * pscale_ref[...]).astype(BF)

    out_shape = (
        jax.ShapeDtypeStruct((S, D), BF),
        jax.ShapeDtypeStruct((S, 384), F32),
        jax.ShapeDtypeStruct((S, HEADS * NOPE), BF),
        jax.ShapeDtypeStruct((S // TQ, HEADS * TQ, QW), BF),
        jax.ShapeDtypeStruct((S, QW), BF),
        jax.ShapeDtypeStruct((S, PW), BF),
        jax.ShapeDtypeStruct((S, PW), F32),
        jax.ShapeDtypeStruct((S, PW), BF),
    )
    in_specs = [
        _rows(ts, D), _full(mod.shape), _full((1, D)), _full(w_in.shape), _full((1, QL)), _full((1, KVL)),
        _full(w_uq.shape), _full(wuk_dc.shape), _full(perm.shape), _rows(ts, 128), _rows(ts, 128), _rows(ts, 128),
        _rows(ts, 128), _full(w_pool.shape), _full((1, PW)),
    ]
    out_specs = (
        _rows(ts, D), _rows(ts, 384), _rows(ts, HEADS * NOPE),
        pl.BlockSpec((nsub, HEADS * TQ, QW), lambda i: (i, 0, 0)),
        _rows(ts, QW), _rows(ts, PW), _rows(ts, PW), _rows(ts, PW),
    )
    return pl.pallas_call(
        body, name="fwd_in", out_shape=out_shape, grid=(S // ts,), in_specs=in_specs, out_specs=out_specs,
        scratch_shapes=[pltpu.VMEM((16, PW), F32)], compiler_params=_params(("arbitrary",)),
    )(x, mod, g_mix, w_in, g_q, g_kv, w_uq, wuk_dc, perm, cos4, sin4, csk, snk, w_pool, pool_scale)


def _diag_mask(shape, q_axis):
    qi = (lax.broadcasted_iota(jnp.int32, shape, q_axis) & (TQ - 1)) >> 6
    ki = lax.broadcasted_iota(jnp.int32, shape, 1 - q_axis) >> 6
    return ki <= qi


def _attn_fwd(qs, kv, wuv_cv):
    nq = qs.shape[0]
    S = kv.shape[0]
    M = HEADS * TQ

    def body(qs_ref, kv_ref, wuv_ref, olat_ref, ymla_ref, lse_ref):
        i = pl.program_id(0)
        q = qs_ref[0]

        def step(kt, carry, masked):
            m, l, acc = carry
            k = kv_ref[pl.ds(pl.multiple_of(kt * TK, TK), TK), :]
            s = _dot_nt(q, k) * SM_SCALE
            if masked:
                s = jnp.where(_diag_mask((M, TK), 0), s, -jnp.inf)
            m_new = jnp.maximum(m, jnp.max(s, axis=-1, keepdims=True))
            alpha = jnp.exp(m - m_new)
            p = jnp.exp(s - m_new)
            l = alpha * l + jnp.sum(p, axis=-1, keepdims=True)
            acc = alpha * acc + _dot(p.astype(BF), k[:, 0:KVL])
            return m_new, l, acc

        init = (jnp.full((M, 1), -jnp.inf, F32), jnp.zeros((M, 1), F32), jnp.zeros((M, KVL), F32))
        carry = lax.fori_loop(0, i, lambda kt, c: step(kt, c, False), init)
        m, l, acc = step(i, carry, True)
        o_lat = acc / l
        olat_ref[0] = o_lat
        lse = m + jnp.log(l)
        lse_ref[0] = jnp.broadcast_to(lse, (M, 128)).T[0:8, :]
        for hd in range(HEADS):
            o = _dot(o_lat[hd * TQ:(hd + 1) * TQ, :].astype(BF), wuv_ref[hd])
            ymla_ref[:, hd * 128:(hd + 1) * 128] = o.astype(BF)

    out_shape = (
        jax.ShapeDtypeStruct((nq, M, KVL), F32),
        jax.ShapeDtypeStruct((S, HEADS * 128), BF),
        jax.ShapeDtypeStruct((nq, 8, M), F32),
    )
    return pl.pallas_call(
        body, name="attn_fwd", out_shape=out_shape, grid=(nq,),
        in_specs=[pl.BlockSpec((1, M, QW), lambda i: (i, 0, 0)), _full(kv.shape), _full(wuv_cv.shape)],
        out_specs=(pl.BlockSpec((1, M, KVL), lambda i: (i, 0, 0)), _rows(TQ, HEADS * 128),
                   pl.BlockSpec((1, 8, M), lambda i: (i, 0, 0))),
        compiler_params=_params(("arbitrary",)),
    )(qs, kv, wuv_cv)


def _silu_parts(a):
    sg = jax.nn.sigmoid(a)
    return sg, a * sg


def _ffn_fwd(x, ymla, ypool, mod, w_o, g_ffn, wg_t, wu_t, wd, g_final, target):
    S = x.shape[0]
    ts = 512
    tf = 256
    nj = FF // tf

    def body(x_ref, ymla_ref, ypool_ref, mod_ref, wo_ref, gffn_ref, wg_ref, wu_ref, wd_ref, gfin_ref, t_ref,
             x2_ref, mix_ref, h2t_ref, a_ref, b_ref, dx3_ref, dff_ref, dfft_ref, loss_ref, dgfin_ref, dgt2_ref,
             acc_ref, h2_ref):
        i = pl.program_id(0)
        j = pl.program_id(1)

        @pl.when(jnp.logical_and(i == 0, j == 0))
        def _():
            loss_ref[...] = jnp.zeros_like(loss_ref)
            dgfin_ref[...] = jnp.zeros_like(dgfin_ref)
            dgt2_ref[...] = jnp.zeros_like(dgt2_ref)

        @pl.when(j == 0)
        def _():
            gt1 = mod_ref[0:1, 2 * D:3 * D]
            sh2 = mod_ref[0:1, 3 * D:4 * D]
            sc2 = mod_ref[0:1, 4 * D:5 * D]
            cat = jnp.concatenate([ymla_ref[...], ypool_ref[...]], axis=1)
            mix = _dot(cat, wo_ref[...])
            mix_ref[...] = mix
            x2 = x_ref[...] + gt1 * mix
            x2_ref[...] = x2
            h2 = (x2 * _rms(x2)) * gffn_ref[...] * (1.0 + sc2) + sh2
            h2_ref[...] = h2.astype(BF)
            h2t_ref[...] = h2.T.astype(BF)
            acc_ref[...] = jnp.zeros_like(acc_ref)

        h2b = h2_ref[...]
        a = _dot_nt(h2b, wg_ref[...])
        b = _dot_nt(h2b, wu_ref[...])
        a_ref[...] = a.astype(BF)
        b_ref[...] = b.astype(BF)
        f = _silu_parts(a)[1] * b
        acc_ref[...] += _dot(f.astype(BF), wd_ref[...])

        @pl.when(j == nj - 1)
        def _():
            gt2 = mod_ref[0:1, 5 * D:6 * D]
            ff = acc_ref[...]
            x3 = x2_ref[...] + gt2 * ff
            r3 = _rms(x3)
            xn3 = x3 * r3
            gfin = gfin_ref[...]
            e = xn3 * gfin - t_ref[...]
            loss_ref[...] += 0.5 * jnp.sum(jnp.mean(e * e, axis=-1, keepdims=True))
            dy = e * (1.0 / D)
            dgfin_ref[...] += _colsum(dy * xn3)
            dx3 = _rms_bwd(dy * gfin, xn3, r3)
            dx3_ref[...] = dx3
            dgt2_ref[...] += _colsum(dx3 * ff)
            dff = dx3 * gt2
            dff_ref[...] = dff.astype(BF)
            dfft_ref[...] = dff.T.astype(BF)

    row = lambda c: pl.BlockSpec((ts, c), lambda i, j: (i, 0))
    col = pl.BlockSpec((D, ts), lambda i, j: (0, i))
    wblk = pl.BlockSpec((tf, D), lambda i, j: (j, 0))
    act = pl.BlockSpec((ts, tf), lambda i, j: (i, j))
    const = lambda shape: pl.BlockSpec(shape, lambda i, j: (0,) * len(shape))
    out_shape = (
        jax.ShapeDtypeStruct((S, D), F32),
        jax.ShapeDtypeStruct((S, D), F32),
        jax.ShapeDtypeStruct((D, S), BF),
        jax.ShapeDtypeStruct((S, FF), BF),
        jax.ShapeDtypeStruct((S, FF), BF),
        jax.ShapeDtypeStruct((S, D), F32),
        jax.ShapeDtypeStruct((S, D), BF),
        jax.ShapeDtypeStruct((D, S), BF),
        jax.ShapeDtypeStruct((8, 128), F32),
        jax.ShapeDtypeStruct((1, D), F32),
        jax.ShapeDtypeStruct((1, D), F32),
    )
    return pl.pallas_call(
        body, name="ffn_fwd", out_shape=out_shape, grid=(S // ts, nj),
        in_specs=[row(D), row(PW), row(PW), const(mod.shape), const(w_o.shape), const((1, D)), wblk, wblk, wblk,
                  const((1, D)), row(D)],
        out_specs=(row(D), row(D), col, act, act, row(D), row(D), col, const((8, 128)), const((1, D)),
                   const((1, D))),
        scratch_shapes=[pltpu.VMEM((ts, D), F32), pltpu.VMEM((ts, D), BF)],
        compiler_params=_params(("arbitrary", "arbitrary")),
    )(x, ymla, ypool, mod, w_o, g_ffn, wg_t, wu_t, wd, g_final, target)


def _ffn_bwd(dff, dff_t, h2_t, a, b, wg_t, wu_t, wd):
    S = dff.shape[0]
    ts = 1024
    tf = 256
    ni = S // ts
    nj = FF // tf

    def body(dff_ref, dfft_ref, h2t_ref, a_ref, b_ref, wg_ref, wu_ref, wd_ref,
             dwg_ref, dwu_ref, dwd_ref, dh2_ref, gacc, uacc, dacc, dh2acc):
        j = pl.program_id(0)
        i = pl.program_id(1)
        dffb = dff_ref[...]
        h2t = h2t_ref[...]
        av = a_ref[...].astype(F32)
        bv = b_ref[...].astype(F32)
        df = _dot_nt(dffb, wd_ref[...])
        sg, sa = _silu_parts(av)
        fb = (sa * bv).astype(BF)
        dbb = (df * sa).astype(BF)
        dab = (df * bv * (sg * (1.0 + av * (1.0 - sg)))).astype(BF)

        @pl.when(i == 0)
        def _():
            gacc[...] = jnp.zeros_like(gacc)
            uacc[...] = jnp.zeros_like(uacc)
            dacc[...] = jnp.zeros_like(dacc)

        gacc[...] += _dot(h2t, dab)
        uacc[...] += _dot(h2t, dbb)
        dacc[...] += _dot(dfft_ref[...], fb)
        contrib = _dot(dab, wg_ref[...]) + _dot(dbb, wu_ref[...])
        rows = pl.ds(pl.multiple_of(i * ts, ts), ts)

        @pl.when(j == 0)
        def _():
            dh2acc[rows, :] = contrib

        @pl.when(j > 0)
        def _():
            dh2acc[rows, :] += contrib

        @pl.when(i == ni - 1)
        def _():
            dwg_ref[...] = gacc[...].T.astype(BF)
            dwu_ref[...] = uacc[...].T.astype(BF)
            dwd_ref[...] = dacc[...].T.astype(BF)

        @pl.when(j == nj - 1)
        def _():
            dh2_ref[...] = dh2acc[rows, :]

    row = lambda c: pl.BlockSpec((ts, c), lambda j, i: (i, 0))
    col = pl.BlockSpec((D, ts), lambda j, i: (0, i))
    act = pl.BlockSpec((ts, tf), lambda j, i: (i, j))
    wblk = pl.BlockSpec((tf, D), lambda j, i: (j, 0))
    out_shape = (
        jax.ShapeDtypeStruct((FF, D), BF), jax.ShapeDtypeStruct((FF, D), BF), jax.ShapeDtypeStruct((FF, D), BF),
        jax.ShapeDtypeStruct((S, D), F32),
    )
    return pl.pallas_call(
        body, name="ffn_bwd", out_shape=out_shape, grid=(nj, ni),
        in_specs=[row(D), col, col, act, act, wblk, wblk, wblk],
        out_specs=(wblk, wblk, wblk, pl.BlockSpec((ts, D), lambda j, i: (jnp.where(j == nj - 1, i, 0), 0))),
        scratch_shapes=[pltpu.VMEM((D, tf), F32), pltpu.VMEM((D, tf), F32), pltpu.VMEM((D, tf), F32),
                        pltpu.VMEM((S, D), F32)],
        compiler_params=_params(("arbitrary", "arbitrary")),
    )(dff, dff_t, h2_t, a, b, wg_t, wu_t, wd)


def _mix_bwd(dh2, dx3, x2, mix, mod, g_ffn, ymla, ypool, w_o, ypre, pooled, pool_scale, wpool_dc, olat, wuv_vc):
    S = dh2.shape[0]
    ts = 512
    n = S // ts
    nsub = ts // TQ
    M = HEADS * TQ

    def body(dh2_ref, dx3_ref, x2_ref, mix_ref, mod_ref, gffn_ref, ymla_ref, ypool_ref, wo_ref, ypre_ref, pooled_ref,
             pscale_ref, wpool_ref, olat_ref, wuv_ref,
             dx2_ref, du_ref, dolat_ref, delta_ref, dwo_ref, dwuv_ref, dwpool_ref, dpscale_ref, dgt1_ref, dsc2_ref,
             dsh2_ref, dgffn_ref, carry_ref, dwo_acc):
        i = pl.program_id(0)

        @pl.when(i == 0)
        def _():
            carry_ref[...] = jnp.zeros_like(carry_ref)
            dwo_acc[...] = jnp.zeros_like(dwo_acc)
            for r in (dwuv_ref, dwpool_ref, dpscale_ref, dgt1_ref, dsc2_ref, dsh2_ref, dgffn_ref):
                r[...] = jnp.zeros_like(r)

        gt1 = mod_ref[0:1, 2 * D:3 * D]
        sc2 = mod_ref[0:1, 4 * D:5 * D]
        gffn = gffn_ref[...]
        dh2 = dh2_ref[...]
        x2 = x2_ref[...]
        r2 = _rms(x2)
        xn2 = x2 * r2
        dsc2_ref[...] += _colsum(dh2 * (xn2 * gffn))
        dsh2_ref[...] += _colsum(dh2)
        dgffn_ref[...] += _colsum(dh2 * (1.0 + sc2) * xn2)
        dx2 = dx3_ref[...] + _rms_bwd(dh2 * gffn * (1.0 + sc2), xn2, r2)
        dx2_ref[...] = dx2
        dgt1_ref[...] += _colsum(dx2 * mix_ref[...])
        dmix = (dx2 * gt1).astype(BF)
        cat = jnp.concatenate([ymla_ref[...], ypool_ref[...]], axis=1)
        dwo_acc[...] += _dot_tn(cat, dmix)
        dcat = _dot_nt(dmix, wo_ref[...])
        dymla = dcat[:, 0:512]
        dypool = dcat[:, 512:1024]

        dpscale_ref[...] += _colsum(dypool * ypre_ref[...])
        dypre = (dypool * pscale_ref[...]).astype(BF)
        pooled = pooled_ref[...]
        dpooled = []
        for g in range(GROUPS):
            sl = slice(g * GD, (g + 1) * GD)
            dwpool_ref[g] += _dot_tn(pooled[:, sl], dypre[:, sl])
            dpooled.append(_dot(dypre[:, sl], wpool_ref[g]))
        dpooled = jnp.concatenate(dpooled, axis=1)
        tile = n - 1 - i
        e = dpooled / _row_counts(tile * ts, ts)
        ext = jnp.concatenate([e, carry_ref[...]], axis=0)
        du_ref[...] = _window_sums(ext, False)[0:ts, :] - dpooled
        carry_ref[...] = e[0:16, :]

        for hd in range(HEADS):
            do = dymla[:, hd * 128:(hd + 1) * 128]
            dob = do.astype(BF)
            dol = _dot(dob, wuv_ref[hd])
            for a in range(nsub):
                ol = olat_ref[a, hd * TQ:(hd + 1) * TQ, :]
                dl = dol[a * TQ:(a + 1) * TQ, :]
                dolat_ref[a, hd * TQ:(hd + 1) * TQ, :] = dl.astype(BF)
                dwuv_ref[hd] += _dot_tn(ol, dob[a * TQ:(a + 1) * TQ, :])
                delta = jnp.sum(dl * ol, axis=-1, keepdims=True)
                delta_ref[a, :, hd * TQ:(hd + 1) * TQ] = jnp.broadcast_to(delta, (TQ, 128)).T[0:8, :]

        @pl.when(i == n - 1)
        def _():
            dwo_ref[...] = dwo_acc[...].astype(BF)

    rev = lambda c: pl.BlockSpec((ts, c), lambda i: (n - 1 - i, 0))
    rev3 = lambda r, c: pl.BlockSpec((nsub, r, c), lambda i: (n - 1 - i, 0, 0))
    out_shape = (
        jax.ShapeDtypeStruct((S, D), F32),
        jax.ShapeDtypeStruct((S, PW), F32),
        jax.ShapeDtypeStruct((S // TQ, M, KVL), BF),
        jax.ShapeDtypeStruct((S // TQ, 8, M), F32),
        jax.ShapeDtypeStruct((D, D), BF),
        jax.ShapeDtypeStruct((HEADS, KVL, 128), F32),
        jax.ShapeDtypeStruct((GROUPS, GD, GD), F32),
        jax.ShapeDtypeStruct((1, PW), F32),
        jax.ShapeDtypeStruct((1, D), F32), jax.ShapeDtypeStruct((1, D), F32), jax.ShapeDtypeStruct((1, D), F32),
        jax.ShapeDtypeStruct((1, D), F32),
    )
    in_specs = [rev(D), rev(D), rev(D), rev(D), _full(mod.shape), _full((1, D)), rev(PW), rev(PW), _full(w_o.shape),
                rev(PW), rev(PW), _full((1, PW)), _full(wpool_dc.shape), rev3(M, KVL), _full(wuv_vc.shape)]
    out_specs = (rev(D), rev(PW), rev3(M, KVL), rev3(8, M), _full((D, D)), _full((HEADS, KVL, 128)),
                 _full((GROUPS, GD, GD)), _full((1, PW)), _full((1, D)), _full((1, D)), _full((1, D)), _full((1, D)))
    return pl.pallas_call(
        body, name="mix_bwd", out_shape=out_shape, grid=(n,), in_specs=in_specs, out_specs=out_specs,
        scratch_shapes=[pltpu.VMEM((16, PW), F32), pltpu.VMEM((D, D), F32)],
        compiler_params=_params(("arbitrary",)),
    )(dh2, dx3, x2, mix, mod, g_ffn, ymla, ypool, w_o, ypre, pooled, pool_scale, wpool_dc, olat, wuv_vc)


def _attn_bwd(qs, kv, dolat, lse, delta):
    nq = qs.shape[0]
    S = kv.shape[0]
    M = HEADS * TQ
    nk = S // TK

    def body(qs_ref, kv_ref, do_ref, lse_ref, delta_ref, dkv_ref, dqt_ref):
        kt = pl.program_id(0)
        k = kv_ref[...]
        v = k[:, 0:KVL]
        k_t = k.astype(F32).T.astype(BF)

        @pl.when(kt == 0)
        def _():
            dqt_ref[...] = jnp.zeros_like(dqt_ref)

        def step(qi, carry, masked):
            dk, dv = carry
            q = qs_ref[qi]
            do = do_ref[qi]
            s = _dot_nt(k, q) * SM_SCALE
            p = jnp.exp(s - lse_ref[qi, 0:1, :])
            if masked:
                p = jnp.where(_diag_mask((TK, M), 1), p, 0.0)
            dp = _dot_nt(v, do)
            ds = (p * (dp - delta_ref[qi, 0:1, :]) * SM_SCALE).astype(BF)
            dv = dv + _dot(p.astype(BF), do)
            dk = dk + _dot(ds, q)
            dqt_ref[qi] += _dot(k_t, ds)
            return dk, dv

        carry = step(kt, (jnp.zeros((TK, QW), F32), jnp.zeros((TK, KVL), F32)), True)
        dk, dv = lax.fori_loop(kt + 1, nq, lambda qi, c: step(qi, c, False), carry)
        dkv_ref[...] = dk + jnp.concatenate([dv, jnp.zeros((TK, QW - KVL), F32)], axis=1)

    out_shape = (jax.ShapeDtypeStruct((S, QW), F32), jax.ShapeDtypeStruct((nq, QW, M), F32))
    return pl.pallas_call(
        body, name="attn_bwd", out_shape=out_shape, grid=(nk,),
        in_specs=[_vmem(), _rows(TK, QW), _vmem(), _vmem(), _vmem()],
        out_specs=(_rows(TK, QW), _vmem()),
        compiler_params=_params(("arbitrary",)),
    )(qs, kv, dolat, lse, delta)


def _in_bwd(dqt, dkv, du, raw, qn, h1, x, dx2, mod, g_mix, w_in, g_q, g_kv, w_uq, wuk_cd, perm_t, cos4, sin4, csk,
            snk):
    S = x.shape[0]
    ts = 512
    n = S // ts
    nsub = ts // TQ
    M = HEADS * TQ

    def body(dqt_ref, dkv_ref, du_ref, raw_ref, qn_ref, h1_ref, x_ref, dx2_ref, mod_ref, gmix_ref, win_ref, gq_ref,
             gkv_ref, wuq_ref, wuk_ref, permt_ref, cos_ref, sin_ref, csk_ref, snk_ref,
             dx_ref, dwin_ref, dwuq_ref, dwuk_ref, dgq_ref, dgkv_ref, dsc1_ref, dsh1_ref, dgmix_ref, dwin_acc,
             dwuq_acc):
        i = pl.program_id(0)

        @pl.when(i == 0)
        def _():
            dwin_acc[...] = jnp.zeros_like(dwin_acc)
            dwuq_acc[...] = jnp.zeros_like(dwuq_acc)
            for r in (dwuk_ref, dgq_ref, dgkv_ref, dsc1_ref, dsh1_ref, dgmix_ref):
                r[...] = jnp.zeros_like(r)

        dq_blocks = [dqt_ref[a].T for a in range(nsub)]
        qn = qn_ref[...]
        dq_parts = []
        drope = jnp.zeros((ts, 2 * 128), F32)
        for hd in range(HEADS):
            dqh = jnp.concatenate([blk[hd * TQ:(hd + 1) * TQ, :] for blk in dq_blocks], axis=0)
            dq_lat = dqh[:, 0:KVL].astype(BF)
            dq_parts.append(_dot(dq_lat, wuk_ref[hd]))
            dwuk_ref[hd] += _dot_tn(dq_lat, qn[:, hd * NOPE:(hd + 1) * NOPE])
            drope = drope + _dot(dqh[:, KVL:QW].astype(BF), permt_ref[hd])
        do1 = drope[:, 0:128]
        do2 = drope[:, 128:256]
        cosv = cos_ref[...]
        sinv = sin_ref[...]
        dq_parts.append(do1 * cosv + do2 * sinv)
        dq_parts.append(do2 * cosv - do1 * sinv)
        dq = jnp.concatenate(dq_parts, axis=1).astype(BF)

        cq_raw = raw_ref[:, 0:QL]
        ckv_raw = raw_ref[:, QL:QL + KVL]
        rq = _rms(cq_raw)
        nq_ = cq_raw * rq
        gq = gq_ref[...]
        dwuq_acc[...] += _dot_tn((nq_ * gq).astype(BF), dq)
        dc_q = _dot_nt(dq, wuq_ref[...])
        dgq_ref[...] += _colsum(dc_q * nq_)
        dcq_raw = _rms_bwd(dc_q * gq, nq_, rq)

        dkv = dkv_ref[...]
        rk = _rms(ckv_raw)
        nk_ = ckv_raw * rk
        dc_kv = dkv[:, 0:KVL]
        dgkv_ref[...] += _colsum(dc_kv * nk_)
        dckv_raw = _rms_bwd(dc_kv * gkv_ref[...], nk_, rk)
        dkr_roped = dkv[:, KVL:QW]
        dkr = dkr_roped * csk_ref[...] - _swap_halves(dkr_roped) * snk_ref[...]

        dproj = jnp.concatenate([dcq_raw, dckv_raw, dkr, du_ref[...]], axis=1).astype(BF)
        dwin_acc[...] += _dot_tn(h1_ref[...], dproj)
        dh1 = _dot_nt(dproj, win_ref[...])

        sc1 = mod_ref[0:1, D:2 * D]
        gmix = gmix_ref[...]
        xv = x_ref[...]
        r1 = _rms(xv)
        xn1 = xv * r1
        dsc1_ref[...] += _colsum(dh1 * (xn1 * gmix))
        dsh1_ref[...] += _colsum(dh1)
        dgmix_ref[...] += _colsum(dh1 * (1.0 + sc1) * xn1)
        dx_ref[...] = dx2_ref[...] + _rms_bwd(dh1 * gmix * (1.0 + sc1), xn1, r1)

        @pl.when(i == n - 1)
        def _():
            dwin_ref[...] = dwin_acc[...].astype(BF)
            dwuq_ref[...] = dwuq_acc[...].astype(BF)

    out_shape = (
        jax.ShapeDtypeStruct((S, D), F32),
        jax.ShapeDtypeStruct((D, D), BF),
        jax.ShapeDtypeStruct((QL, 768), BF),
        jax.ShapeDtypeStruct((HEADS, KVL, NOPE), F32),
        jax.ShapeDtypeStruct((1, QL), F32), jax.ShapeDtypeStruct((1, KVL), F32),
        jax.ShapeDtypeStruct((1, D), F32), jax.ShapeDtypeStruct((1, D), F32), jax.ShapeDtypeStruct((1, D), F32),
    )
    in_specs = [pl.BlockSpec((nsub, QW, M), lambda i: (i, 0, 0)), _rows(ts, QW), _rows(ts, PW), _rows(ts, 384),
                _rows(ts, HEADS * NOPE), _rows(ts, D), _rows(ts, D), _rows(ts, D), _full(mod.shape), _full((1, D)),
                _full(w_in.shape), _full((1, QL)), _full((1, KVL)), _full(w_uq.shape), _full(wuk_cd.shape),
                _full(perm_t.shape), _rows(ts, 128), _rows(ts, 128), _rows(ts, 128), _rows(ts, 128)]
    out_specs = (_rows(ts, D), _full((D, D)), _full((QL, 768)), _full((HEADS, KVL, NOPE)), _full((1, QL)),
                 _full((1, KVL)), _full((1, D)), _full((1, D)), _full((1, D)))
    return pl.pallas_call(
        body, name="in_bwd", out_shape=out_shape, grid=(n,), in_specs=in_specs, out_specs=out_specs,
        scratch_shapes=[pltpu.VMEM((D, D), F32), pltpu.VMEM((QL, 768), F32)],
        compiler_params=_params(("arbitrary",)),
    )(dqt, dkv, du, raw, qn, h1, x, dx2, mod, g_mix, w_in, g_q, g_kv, w_uq, wuk_cd, perm_t, cos4, sin4, csk, snk)


def _rope_perm():
    p = np.zeros((HEADS, 2 * 128, 128), np.float32)
    for hd in range(HEADS):
        for t in range(HALF):
            p[hd, hd * HALF + t, t] = 1.0
            p[hd, 128 + hd * HALF + t, HALF + t] = 1.0
    return p


def _rope_tables(positions):
    freqs = jnp.power(ROPE_THETA, -jnp.arange(HALF, dtype=F32) / HALF)
    ang = positions.astype(F32)[:, None] * jnp.tile(freqs, HEADS)[None, :]
    cos4 = jnp.cos(ang)
    sin4 = jnp.sin(ang)
    lane = jnp.arange(HEADS * HALF)[None, :]
    csk = jnp.where(lane < ROPE, cos4, 0.0)
    snk = jnp.where(lane < HALF, -sin4, jnp.where(lane < ROPE, sin4, 0.0))
    return cos4, sin4, csk, snk


def _local_step(x, positions, target, mod, g_mix, w_in_p, g_q, g_kv, w_uq_p, w_uk, w_uv, w_pool, pool_scale, g_ffn,
                g_final, late, ffn_grads_exchange):
    perm = jnp.asarray(_rope_perm(), BF)
    perm_t = jnp.asarray(_rope_perm().transpose(0, 2, 1), BF)
    cos4, sin4, csk, snk = _rope_tables(positions)
    wuk_dc = w_uk.transpose(1, 2, 0).astype(BF)
    wuk_cd = w_uk.transpose(1, 0, 2).astype(BF)
    wuv_cv = w_uv.transpose(1, 0, 2).astype(BF)
    wuv_vc = w_uv.transpose(1, 2, 0).astype(BF)
    wpool = w_pool.astype(BF)
    wpool_dc = w_pool.transpose(0, 2, 1).astype(BF)

    h1, raw, qn, qs, kv, pooled, ypre, ypool = _fwd_in(
        x, mod, g_mix, w_in_p, g_q, g_kv, w_uq_p, wuk_dc, perm, cos4, sin4, csk, snk, wpool, pool_scale)
    olat, ymla, lse = _attn_fwd(qs, kv, wuv_cv)
    w_o, wg_t, wu_t, wd = late
    x2, mix, h2_t, a, b, dx3, dff, dff_t, loss, dgfin, dgt2 = _ffn_fwd(
        x, ymla, ypool, mod, w_o, g_ffn, wg_t, wu_t, wd, g_final, target)
    dwg_t, dwu_t, dwd, dh2 = _ffn_bwd(dff, dff_t, h2_t, a, b, wg_t, wu_t, wd)
    ffn_parts = ffn_grads_exchange((dwg_t, dwu_t, dwd))
    (dx2, du, dolat, delta, dwo, dwuv, dwpool, dpscale, dgt1, dsc2, dsh2, dgffn) = _mix_bwd(
        dh2, dx3, x2, mix, mod, g_ffn, ymla, ypool, w_o, ypre, pooled, pool_scale, wpool_dc, olat, wuv_vc)
    dkv, dqt = _attn_bwd(qs, kv, dolat, lse, delta)
    dx, dwin, dwuq, dwuk, dgq, dgkv, dsc1, dsh1, dgmix = _in_bwd(
        dqt, dkv, du, raw, qn, h1, x, dx2, mod, g_mix, w_in_p, g_q, g_kv, w_uq_p, wuk_cd, perm_t, cos4, sin4, csk,
        snk)
    dmod = jnp.concatenate([dsh1, dsc1, dgt1, dsh2, dsc2, dgt2], axis=1)
    replicated = dict(
        w_uk=dwuk.transpose(1, 0, 2), w_uv=dwuv.transpose(1, 0, 2), w_pool=dwpool, g_mix=dgmix, g_q=dgq, g_kv=dgkv,
        pool_scale=dpscale, g_ffn=dgffn, g_final=dgfin)
    return loss[0, 0], dx, dmod, (dwin, dwuq, dwo), ffn_parts, replicated


def _my_pos():
    return lax.axis_index("x"), lax.axis_index("y"), lax.axis_index("c")


def _peer(pos, k):
    x, y, c = pos
    return (1 - x if k & 4 else x, 1 - y if k & 2 else y, 1 - c if k & 1 else c)


def _index(pos):
    x, y, c = pos
    return 4 * x + 2 * y + c


def _remote(src, dst, send_sem, recv_sem, to):
    return pltpu.make_async_remote_copy(src_ref=src, dst_ref=dst, send_sem=send_sem, recv_sem=recv_sem,
                                        device_id=to, device_id_type=MESH)


def _ada_mod(c, w_ada, b_ada):
    def body(c_ref, w_ref, b_ref, mod_ref, call_ref, cbuf, sbuf, rbuf, send1, recv1, send2, recv2):
        me = _my_pos()
        mi = _index(me)
        cv = c_ref[...]
        cbuf[...] = jnp.broadcast_to(cv * jax.nn.sigmoid(cv), (8, D))
        call_ref[mi] = cbuf[...]
        first = [_remote(cbuf, call_ref.at[mi], send1.at[k - 1], recv1.at[k - 1], _peer(me, k)) for k in range(1, NDEV)]
        for cp in first:
            cp.start()
        for k in range(1, NDEV):
            _remote(cbuf, call_ref.at[_index(_peer(me, k))], send1.at[k - 1], recv1.at[k - 1], _peer(me, k)).wait_recv()
        c_all = jnp.concatenate([call_ref[b][0:1, :] for b in range(NDEV)], axis=0)
        blocks = _dot(c_all.astype(BF), w_ref[...].astype(BF))
        for b in range(NDEV):
            sbuf[b] = jnp.broadcast_to(blocks[b:b + 1, :], (8, MODC))
        second = []
        for k in range(1, NDEV):
            to = _peer(me, k)
            second.append(_remote(sbuf.at[_index(to)], rbuf.at[mi], send2.at[k - 1], recv2.at[k - 1], to))
        for cp in second:
            cp.start()
        rbuf[mi] = sbuf[mi]
        for k in range(1, NDEV):
            to = _peer(me, k)
            _remote(sbuf.at[_index(to)], rbuf.at[_index(to)], send2.at[k - 1], recv2.at[k - 1], to).wait_recv()
        for j in range(NDEV):
            mod_ref[:, j * MODC:(j + 1) * MODC] = rbuf[j] + b_ref[:, j * MODC:(j + 1) * MODC]
        for cp in first + second:
            cp.wait_send()

    return pl.pallas_call(
        body, name="ada_mod",
        out_shape=(jax.ShapeDtypeStruct((8, N_MOD * D), F32), jax.ShapeDtypeStruct((NDEV, 8, D), F32)),
        in_specs=[_vmem(), _vmem(), _vmem()], out_specs=(_vmem(), _vmem()),
        scratch_shapes=[pltpu.VMEM((8, D), F32), pltpu.VMEM((NDEV, 8, MODC), F32), pltpu.VMEM((NDEV, 8, MODC), F32),
                        pltpu.SemaphoreType.DMA((NDEV - 1,)), pltpu.SemaphoreType.DMA((NDEV - 1,)),
                        pltpu.SemaphoreType.DMA((NDEV - 1,)), pltpu.SemaphoreType.DMA((NDEV - 1,))],
        compiler_params=_params(),
    )(c, w_ada, b_ada)


def _exchange_slices(scatter):
    def of(src, to_index):
        if not scatter:
            return src
        r = src.shape[0] // NDEV
        return src.at[pl.ds(pl.multiple_of(to_index * r, 16), r), :]
    return of


def _sequencer_exchange(name, collective_id, srcs, scatter, after=()):
    n = len(srcs)
    of = _exchange_slices(scatter)

    def body(*refs):
        src, zone = refs[:n], refs[n + len(after):2 * n + len(after)]
        send, recv, local = refs[2 * n + len(after):]
        me = _my_pos()
        mi = _index(me)
        barrier = pltpu.get_barrier_semaphore()
        for k in range(1, NDEV):
            pl.semaphore_signal(barrier, inc=1, device_id=_peer(me, k), device_id_type=MESH)
        pl.semaphore_wait(barrier, NDEV - 1)
        own = [pltpu.make_async_copy(of(src[a], mi), zone[a].at[mi], local.at[a]) for a in range(n)]
        for cp in own:
            cp.start()
        for a in range(n):
            for k in range(1, NDEV):
                to = _peer(me, k)
                s = a * (NDEV - 1) + k - 1
                _remote(of(src[a], _index(to)), zone[a].at[mi], send.at[s], recv.at[s], to).start()
        for cp in own:
            cp.wait()
        for a in range(n):
            for k in range(1, NDEV):
                to = _peer(me, k)
                s = a * (NDEV - 1) + k - 1
                cp = _remote(of(src[a], mi), zone[a].at[_index(to)], send.at[s], recv.at[s], to)
                cp.wait_send()
                cp.wait_recv()

    return pl.kernel(
        body, name=name, mesh=plsc.ScalarSubcoreMesh(axis_name="sequencer", num_cores=1),
        out_type=tuple(jax.ShapeDtypeStruct((NDEV, s.shape[0] // NDEV if scatter else s.shape[0], s.shape[1]), s.dtype)
                       for s in srcs),
        scratch_types=[pltpu.SemaphoreType.DMA((n * (NDEV - 1),)), pltpu.SemaphoreType.DMA((n * (NDEV - 1),)),
                       pltpu.SemaphoreType.DMA((n,))],
        compiler_params=pltpu.CompilerParams(collective_id=collective_id),
    )(*srcs, *after)


def _sum_partials(name, parts):
    n = len(parts)

    def body(*refs):
        for a in range(n):
            acc = refs[a][0].astype(F32)
            for p in range(1, NDEV):
                acc = acc + refs[a][p].astype(F32)
            refs[n + a][...] = acc

    return pl.pallas_call(
        body, name=name,
        out_shape=tuple(jax.ShapeDtypeStruct(p.shape[1:], F32) for p in parts),
        in_specs=[_vmem()] * n, out_specs=tuple([_vmem()] * n), compiler_params=_params(),
    )(*parts)


def _small_all_reduce(buf):
    def body(buf_ref, got_ref, red_ref, mine, send1, recv1, send2, recv2):
        me = _my_pos()
        mi = _index(me)
        first = []
        for k in range(1, NDEV):
            to = _peer(me, k)
            first.append(_remote(buf_ref.at[_index(to)], got_ref.at[mi], send1.at[k - 1], recv1.at[k - 1], to))
        for cp in first:
            cp.start()
        got_ref[mi] = buf_ref[mi]
        for k in range(1, NDEV):
            to = _peer(me, k)
            _remote(buf_ref.at[mi], got_ref.at[_index(to)], send1.at[k - 1], recv1.at[k - 1], to).wait_recv()
        acc = got_ref[0]
        for p in range(1, NDEV):
            acc = acc + got_ref[p]
        mine[...] = acc
        second = [_remote(mine, red_ref.at[mi], send2.at[k - 1], recv2.at[k - 1], _peer(me, k)) for k in range(1, NDEV)]
        for cp in second:
            cp.start()
        red_ref[mi] = acc
        for k in range(1, NDEV):
            to = _peer(me, k)
            _remote(mine, red_ref.at[_index(to)], send2.at[k - 1], recv2.at[k - 1], to).wait_recv()
        for cp in first + second:
            cp.wait_send()

    return pl.pallas_call(
        body, name="small_all_reduce",
        out_shape=(jax.ShapeDtypeStruct(buf.shape, F32), jax.ShapeDtypeStruct(buf.shape, F32)),
        in_specs=[_vmem()], out_specs=(_vmem(), _vmem()),
        scratch_shapes=[pltpu.VMEM(buf.shape[1:], F32),
                        pltpu.SemaphoreType.DMA((NDEV - 1,)), pltpu.SemaphoreType.DMA((NDEV - 1,)),
                        pltpu.SemaphoreType.DMA((NDEV - 1,)), pltpu.SemaphoreType.DMA((NDEV - 1,))],
        compiler_params=_params(),
    )(buf)


def _adamw_math(w, g, m, v):
    m = ADAM_B1 * m + (1.0 - ADAM_B1) * g
    v = ADAM_B2 * v + (1.0 - ADAM_B2) * jnp.square(g)
    m_hat = m / (1.0 - ADAM_B1 ** ADAM_STEP)
    v_hat = v / (1.0 - ADAM_B2 ** ADAM_STEP)
    delta = -ADAM_LR * (m_hat / (jnp.sqrt(v_hat) + ADAM_EPS) + ADAM_WD * w)
    return delta, m, v


def _adamw_group(name, ws, gs, ms, vs):
    n = len(ws)

    def body(*refs):
        for a in range(n):
            w, g, m, v = (refs[q * n + a][...] for q in range(4))
            delta, m2, v2 = _adamw_math(w, g, m, v)
            refs[4 * n + a][...] = delta
            refs[5 * n + a][...] = m2
            refs[6 * n + a][...] = v2

    shapes = tuple(jax.ShapeDtypeStruct(w.shape, F32) for w in ws)
    outs = pl.pallas_call(
        body, name=name, out_shape=shapes * 3, in_specs=[_vmem()] * (4 * n), out_specs=tuple([_vmem()] * (3 * n)),
        compiler_params=_params(),
    )(*ws, *gs, *ms, *vs)
    return outs[:n], outs[n:2 * n], outs[2 * n:]


def _adamw_ada(w, m, v, c_all, dmod_rows):
    def body(w_ref, m_ref, v_ref, c_ref, dm_ref, g_ref, d_ref, m2_ref, v2_ref):
        g = _dot_tn(c_ref[...], dm_ref[...].astype(BF))
        g_ref[...] = g
        delta, m2, v2 = _adamw_math(w_ref[...], g, m_ref[...], v_ref[...])
        d_ref[...] = delta
        m2_ref[...] = m2
        v2_ref[...] = v2

    shp = jax.ShapeDtypeStruct(w.shape, F32)
    return pl.pallas_call(
        body, name="adamw_ada", out_shape=(shp, shp, shp, shp), in_specs=[_vmem()] * 5,
        out_specs=tuple([_vmem()] * 4), compiler_params=_params(),
    )(w, m, v, c_all, dmod_rows)


def _w_in_to_kernel(w):
    return jnp.concatenate([w[:, 0:448], jnp.zeros((w.shape[0], 64), w.dtype), w[:, 448:960]], axis=1)


def _w_in_from_kernel(w):
    return jnp.concatenate([w[:, 0:448], w[:, 512:1024]], axis=1)


def _w_uq_to_kernel(w):
    r = w.shape[0]
    return jnp.concatenate([w[:, :, 0:NOPE].reshape(r, HEADS * NOPE),
                            w[:, :, NOPE:NOPE + HALF].reshape(r, HEADS * HALF),
                            w[:, :, NOPE + HALF:].reshape(r, HEADS * HALF)], axis=1)


def _w_uq_from_kernel(w):
    r = w.shape[0]
    return jnp.concatenate([w[:, 0:512].reshape(r, HEADS, NOPE), w[:, 512:640].reshape(r, HEADS, HALF),
                            w[:, 640:768].reshape(r, HEADS, HALF)], axis=2)


REP_NAMES = ("w_uk", "w_uv", "w_pool", "g_mix", "g_q", "g_kv", "pool_scale", "g_ffn", "g_final")


def kernel(x, c, positions, w_ada, b_ada, g_mix, w_in, g_q, g_kv, w_uq, w_uk, w_uv, w_pool, pool_scale, w_o, g_ffn, w_gate, w_up, w_down, g_final, loss_target, m_w_ada, m_b_ada, m_g_mix, m_w_in, m_g_q, m_g_kv, m_w_uq, m_w_uk, m_w_uv, m_w_pool, m_pool_scale, m_w_o, m_g_ffn, m_w_gate, m_w_up, m_w_down, m_g_final, v_w_ada, v_b_ada, v_g_mix, v_w_in, v_g_q, v_g_kv, v_w_uq, v_w_uk, v_w_uv, v_w_pool, v_pool_scale, v_w_o, v_g_ffn, v_w_gate, v_w_up, v_w_down, v_g_final):
    given = dict(locals())

    merge = lambda g: g.reshape(NDEV * g.shape[1], g.shape[2])
    w_in_p, w_uq_p = (merge(g) for g in _sequencer_exchange(
        "gather_in", 3, (_w_in_to_kernel(w_in[0]).astype(BF), _w_uq_to_kernel(w_uq[0]).astype(BF)), False))

    mod, c_all8 = _ada_mod(c, w_ada[0], b_ada)
    c_all = c_all8[:, 0, :]
    late = _sequencer_exchange(
        "gather_late", 1, (w_o[0].astype(BF), w_gate[0].T.astype(BF), w_up[0].T.astype(BF), w_down[0].astype(BF)),
        False, after=(mod[:, 0:128], w_in_p[0:16, 0:128], w_uq_p[0:16, 0:128]))

    def ffn_grads_exchange(arrays):
        return _sequencer_exchange("scatter_ffn", 2, arrays, True)

    loss, dx, dmod, tail_grads, ffn_parts, replicated = _local_step(
        x[0], positions[0], loss_target[0], mod, g_mix, w_in_p, g_q, g_kv, w_uq_p, w_uk[0], w_uv[0], w_pool[0],
        pool_scale, g_ffn, g_final.reshape(1, D), tuple(merge(g) for g in late), ffn_grads_exchange)

    flat = jnp.concatenate([replicated[k].reshape(-1) for k in REP_NAMES] + [loss.reshape(1)])
    flat = jnp.pad(flat, (0, NDEV * REP_ROWS * 128 - flat.shape[0])).reshape(NDEV, REP_ROWS, 128)
    dmod_blocks = jnp.pad(dmod.reshape(NDEV, MODC // 128, 128), ((0, 0), (0, MOD_ROWS - MODC // 128), (0, 0)))
    got, red = _small_all_reduce(jnp.concatenate([dmod_blocks, flat], axis=1))

    tail_parts = _sequencer_exchange("scatter_tail", 4, tail_grads, True,
                                     after=(ffn_parts[0][0, 0:16, 0:128], red[0, 0:8, :]))
    g_gate_t, g_up_t, g_down = _sum_partials("sum_ffn_partials", ffn_parts)
    g_in_p, g_uq_p, g_o = _sum_partials("sum_tail_partials", tail_parts)
    grads = dict(w_in=_w_in_from_kernel(g_in_p), w_uq=_w_uq_from_kernel(g_uq_p).reshape(QL // NDEV, HEADS * 192),
                 w_o=g_o, w_gate=g_gate_t.T, w_up=g_up_t.T, w_down=g_down)
    dmod_rows = got[:, 0:MODC // 128, :].reshape(NDEV, MODC)
    grads["b_ada"] = red[:, 0:MODC // 128, :].reshape(1, N_MOD * D)
    rep_flat = red[:, MOD_ROWS:, :].reshape(-1)
    off = 0
    for k in REP_NAMES:
        size = int(np.prod(given[k].shape))
        grads[k] = rep_flat[off:off + size]
        off += size

    view = dict(w_ada=(D, MODC), b_ada=(1, N_MOD * D), g_mix=(1, D), w_in=(D // NDEV, 960), g_q=(1, QL),
                g_kv=(1, KVL), w_uq=(QL // NDEV, HEADS * 192), w_uk=(KVL, HEADS * NOPE), w_uv=(KVL, HEADS * 128),
                w_pool=(GROUPS * GD, GD), pool_scale=(1, PW), w_o=(D // NDEV, D), g_ffn=(1, D),
                w_gate=(D, FF // NDEV), w_up=(D, FF // NDEV), w_down=(FF // NDEV, D), g_final=(1, D))
    names = list(view)
    g_ada, d_ada, m_ada, v_ada = _adamw_ada(w_ada[0], m_w_ada[0], v_w_ada[0], c_all.astype(BF), dmod_rows)
    out_g, out_d, out_m, out_v = dict(w_ada=g_ada), dict(w_ada=d_ada), dict(w_ada=m_ada), dict(w_ada=v_ada)
    tail = ("w_in", "w_uq", "w_o")
    groups = (("adamw_ffn", ("w_gate", "w_up", "w_down")),
              ("adamw_replicated", tuple(k for k in names if k not in ("w_ada", "w_gate", "w_up", "w_down") + tail)),
              ("adamw_tail", tail))
    for gname, members in groups:
        ws = [given[k].reshape(view[k]) for k in members]
        gs = [grads[k].reshape(view[k]) for k in members]
        ms = [given["m_" + k].reshape(view[k]) for k in members]
        vs = [given["v_" + k].reshape(view[k]) for k in members]
        ds, m2, v2 = _adamw_group(gname, ws, gs, ms, vs)
        for k, g, d, mm, vv in zip(members, gs, ds, m2, v2):
            out_g[k], out_d[k], out_m[k], out_v[k] = g, d, mm, vv

    total = rep_flat[off]
    shaped = lambda d: [d[k].reshape(given[k].shape) for k in names]
    return (total, dx[None], *shaped(out_g), *shaped(out_d), *shaped(out_m), *shaped(out_v))
```

```python
import numpy as np
import jax
import jax.numpy as jnp
from jax import lax
from jax.experimental import pallas as pl
from jax.experimental.pallas import tpu as pltpu
from jax.experimental.pallas import tpu_sc as plsc

D = 1024
HEADS = 4
NOPE = 128
ROPE = 64
HALF = ROPE // 2
QL = 256
KVL = 128
FF = 2816
PW = 512
GROUPS = 4
GD = 128
N_MOD = 6
EPS = 1e-6
SM_SCALE = (NOPE + ROPE) ** -0.5
ROPE_THETA = 10000.0
NDEV = 8
MODC = N_MOD * D // NDEV

ADAM_LR = 0.001
ADAM_B1 = 0.9
ADAM_B2 = 0.999
ADAM_EPS = 1e-08
ADAM_WD = 0.01
ADAM_STEP = 10

BF = jnp.bfloat16
F32 = jnp.float32
VMEM_LIMIT_V7X = 60 * 1024 * 1024
MESH = pl.DeviceIdType.MESH

TQ = 256
TK = 256
QW = 256
MOD_ROWS = 8
REP_ROWS = 200
SMALL_ROWS = MOD_ROWS + REP_ROWS


def _params(sem=None):
    return pltpu.CompilerParams(dimension_semantics=sem, vmem_limit_bytes=VMEM_LIMIT_V7X)


def _dot(a, b):
    return jnp.dot(a, b, preferred_element_type=F32)


def _dot_nt(a, b):
    return lax.dot_general(a, b, (((1,), (1,)), ((), ())), preferred_element_type=F32)


def _dot_tn(a, b):
    return _dot(a.astype(F32).T.astype(BF), b)


def _full(shape):
    return pl.BlockSpec(shape, lambda *_: (0,) * len(shape))


def _rows(ts, cols):
    return pl.BlockSpec((ts, cols), lambda i: (i, 0))


def _vmem():
    return pl.BlockSpec(memory_space=pltpu.VMEM)


def _any():
    return pl.BlockSpec(memory_space=pl.ANY)


def _rms(v):
    return lax.rsqrt(jnp.mean(v * v, axis=-1, keepdims=True) + EPS)


def _rms_bwd(dn, n, r):
    return r * (dn - n * jnp.mean(dn * n, axis=-1, keepdims=True))


def _colsum(v):
    return jnp.sum(v, axis=0, keepdims=True)


def _swap_halves(v):
    lane = lax.broadcasted_iota(jnp.int32, v.shape, 1)
    return jnp.where(lane < HALF, pltpu.roll(v, 128 - HALF, 1), pltpu.roll(v, HALF, 1))


def _window_lane_width():
    lane = lax.broadcasted_iota(jnp.int32, (1, PW), 1)
    return jnp.where(lane < 128, 2.0, jnp.where(lane < 256, 4.0, jnp.where(lane < 384, 8.0, 16.0))).astype(F32)


def _window_sums(ext, back):
    n = ext.shape[0]

    def sh(v, k):
        return pltpu.roll(v, k if back else n - k, 0)

    s2 = ext + sh(ext, 1)
    e4 = s2[:, 128:]
    s4 = e4 + sh(e4, 2)
    e8 = s4[:, 128:]
    s8 = e8 + sh(e8, 4)
    e16 = s8[:, 128:]
    s16 = e16 + sh(e16, 8)
    return jnp.concatenate([s2[:, :128], s4[:, :128], s8[:, :128], s16], axis=1)


def _row_counts(first_row, ts):
    t1 = (first_row + lax.broadcasted_iota(jnp.int32, (ts, 1), 0) + 1).astype(F32)
    return jnp.minimum(t1, _window_lane_width())


def _fwd_in(x, mod, g_mix, w_in, g_q, g_kv, w_uq, wuk_dc, perm, cos4, sin4, csk, snk, w_pool, pool_scale):
    S = x.shape[0]
    ts = 512
    nsub = ts // TQ

    def body(x_ref, mod_ref, gmix_ref, win_ref, gq_ref, gkv_ref, wuq_ref, wuk_ref, perm_ref, cos_ref, sin_ref,
             csk_ref, snk_ref, wpool_ref, pscale_ref,
             h1_ref, raw_ref, qn_ref, qs_ref, kv_ref, pooled_ref, ypre_ref, ypool_ref, carry_ref):
        i = pl.program_id(0)

        @pl.when(i == 0)
        def _():
            carry_ref[...] = jnp.zeros_like(carry_ref)

        xv = x_ref[...]
        sh1 = mod_ref[0:1, 0:D]
        sc1 = mod_ref[0:1, D:2 * D]
        h = (xv * _rms(xv)) * gmix_ref[...] * (1.0 + sc1) + sh1
        hb = h.astype(BF)
        h1_ref[...] = hb
        proj = _dot(hb, win_ref[...])
        cq_raw = proj[:, 0:QL]
        ckv_raw = proj[:, QL:QL + KVL]
        kr = proj[:, 384:512]
        u = proj[:, 512:1024]
        raw_ref[...] = proj[:, 0:384]

        c_q = (cq_raw * _rms(cq_raw)) * gq_ref[...]
        c_kv = (ckv_raw * _rms(ckv_raw)) * gkv_ref[...]
        q = _dot(c_q.astype(BF), wuq_ref[...])
        qn = q[:, 0:HEADS * NOPE].astype(BF)
        qn_ref[...] = qn
        x1 = q[:, 512:640]
        x2 = q[:, 640:768]
        cosv = cos_ref[...]
        sinv = sin_ref[...]
        roped = jnp.concatenate([x1 * cosv - x2 * sinv, x1 * sinv + x2 * cosv], axis=1).astype(BF)
        for hd in range(HEADS):
            q_lat = _dot(qn[:, hd * NOPE:(hd + 1) * NOPE], wuk_ref[hd])
            q_rope = _dot(roped, perm_ref[hd])
            qh = jnp.concatenate([q_lat, q_rope], axis=1).astype(BF)
            for a in range(nsub):
                qs_ref[a, hd * TQ:(hd + 1) * TQ, :] = qh[a * TQ:(a + 1) * TQ, :]
        k_rope = kr * csk_ref[...] + _swap_halves(kr) * snk_ref[...]
        kv_ref[...] = jnp.concatenate([c_kv, k_rope], axis=1).astype(BF)

        ext = jnp.concatenate([carry_ref[...], u], axis=0)
        win = _window_sums(ext, True)[16:, :]
        pooled = (win / _row_counts(i * ts, ts) - u).astype(BF)
        pooled_ref[...] = pooled
        carry_ref[...] = u[ts - 16:ts, :]
        ypre = jnp.concatenate(
            [_dot(pooled[:, g * GD:(g + 1) * GD], wpool_ref[g]) for g in range(GROUPS)], axis=1)
        ypre_ref[...] = ypre
        ypool_ref[...] = (ypre * pscale_ref[...]).astype(BF)

    out_shape = (
        jax.ShapeDtypeStruct((S, D), BF),
        jax.ShapeDtypeStruct((S, 384), F32),
        jax.ShapeDtypeStruct((S, HEADS * NOPE), BF),
        jax.ShapeDtypeStruct((S // TQ, HEADS * TQ, QW), BF),
        jax.ShapeDtypeStruct((S, QW), BF),
        jax.ShapeDtypeStruct((S, PW), BF),
        jax.ShapeDtypeStruct((S, PW), F32),
        jax.ShapeDtypeStruct((S, PW), BF),
    )
    in_specs = [
        _rows(ts, D), _full(mod.shape), _full((1, D)), _full(w_in.shape), _full((1, QL)), _full((1, KVL)),
        _full(w_uq.shape), _full(wuk_dc.shape), _full(perm.shape), _rows(ts, 128), _rows(ts, 128), _rows(ts, 128),
        _rows(ts, 128), _full(w_pool.shape), _full((1, PW)),
    ]
    out_specs = (
        _rows(ts, D), _rows(ts, 384), _rows(ts, HEADS * NOPE),
        pl.BlockSpec((nsub, HEADS * TQ, QW), lambda i: (i, 0, 0)),
        _rows(ts, QW), _rows(ts, PW), _rows(ts, PW), _rows(ts, PW),
    )
    return pl.pallas_call(
        body, name="fwd_in", out_shape=out_shape, grid=(S // ts,), in_specs=in_specs, out_specs=out_specs,
        scratch_shapes=[pltpu.VMEM((16, PW), F32)], compiler_params=_params(("arbitrary",)),
    )(x, mod, g_mix, w_in, g_q, g_kv, w_uq, wuk_dc, perm, cos4, sin4, csk, snk, w_pool, pool_scale)


def _diag_mask(shape, q_axis):
    qi = (lax.broadcasted_iota(jnp.int32, shape, q_axis) & (TQ - 1)) >> 6
    ki = lax.broadcasted_iota(jnp.int32, shape, 1 - q_axis) >> 6
    return ki <= qi


def _attn_fwd(qs, kv, wuv_cv):
    nq = qs.shape[0]
    S = kv.shape[0]
    M = HEADS * TQ

    def body(qs_ref, kv_ref, wuv_ref, olat_ref, ymla_ref, lse_ref):
        i = pl.program_id(0)
        q = qs_ref[0]

        def step(kt, carry, masked):
            m, l, acc = carry
            k = kv_ref[pl.ds(pl.multiple_of(kt * TK, TK), TK), :]
            s = _dot_nt(q, k) * SM_SCALE
            if masked:
                s = jnp.where(_diag_mask((M, TK), 0), s, -jnp.inf)
            m_new = jnp.maximum(m, jnp.max(s, axis=-1, keepdims=True))
            alpha = jnp.exp(m - m_new)
            p = jnp.exp(s - m_new)
            l = alpha * l + jnp.sum(p, axis=-1, keepdims=True)
            acc = alpha * acc + _dot(p.astype(BF), k[:, 0:KVL])
            return m_new, l, acc

        init = (jnp.full((M, 1), -jnp.inf, F32), jnp.zeros((M, 1), F32), jnp.zeros((M, KVL), F32))
        carry = lax.fori_loop(0, i, lambda kt, c: step(kt, c, False), init)
        m, l, acc = step(i, carry, True)
        o_lat = acc / l
        olat_ref[0] = o_lat
        lse = m + jnp.log(l)
        lse_ref[0] = jnp.broadcast_to(lse, (M, 128)).T[0:8, :]
        for hd in range(HEADS):
            o = _dot(o_lat[hd * TQ:(hd + 1) * TQ, :].astype(BF), wuv_ref[hd])
            ymla_ref[:, hd * 128:(hd + 1) * 128] = o.astype(BF)

    out_shape = (
        jax.ShapeDtypeStruct((nq, M, KVL), F32),
        jax.ShapeDtypeStruct((S, HEADS * 128), BF),
        jax.ShapeDtypeStruct((nq, 8, M), F32),
    )
    return pl.pallas_call(
        body, name="attn_fwd", out_shape=out_shape, grid=(nq,),
        in_specs=[pl.BlockSpec((1, M, QW), lambda i: (i, 0, 0)), _full(kv.shape), _full(wuv_cv.shape)],
        out_specs=(pl.BlockSpec((1, M, KVL), lambda i: (i, 0, 0)), _rows(TQ, HEADS * 128),
                   pl.BlockSpec((1, 8, M), lambda i: (i, 0, 0))),
        compiler_params=_params(("arbitrary",)),
    )(qs, kv, wuv_cv)


def _silu_parts(a):
    sg = jax.nn.sigmoid(a)
    return sg, a * sg


def _ffn_fwd(x, ymla, ypool, mod, w_o, g_ffn, wg_t, wu_t, wd, g_final, target):
    S = x.shape[0]
    ts = 512
    tf = 256
    nj = FF // tf

    def body(x_ref, ymla_ref, ypool_ref, mod_ref, wo_ref, gffn_ref, wg_ref, wu_ref, wd_ref, gfin_ref, t_ref,
             x2_ref, mix_ref, h2t_ref, a_ref, b_ref, dx3_ref, dff_ref, dfft_ref, loss_ref, dgfin_ref, dgt2_ref,
             acc_ref, h2_ref):
        i = pl.program_id(0)
        j = pl.program_id(1)

        @pl.when(jnp.logical_and(i == 0, j == 0))
        def _():
            loss_ref[...] = jnp.zeros_like(loss_ref)
            dgfin_ref[...] = jnp.zeros_like(dgfin_ref)
            dgt2_ref[...] = jnp.zeros_like(dgt2_ref)

        @pl.when(j == 0)
        def _():
            gt1 = mod_ref[0:1, 2 * D:3 * D]
            sh2 = mod_ref[0:1, 3 * D:4 * D]
            sc2 = mod_ref[0:1, 4 * D:5 * D]
            cat = jnp.concatenate([ymla_ref[...], ypool_ref[...]], axis=1)
            mix = _dot(cat, wo_ref[...])
            mix_ref[...] = mix
            x2 = x_ref[...] + gt1 * mix
            x2_ref[...] = x2
            h2 = (x2 * _rms(x2)) * gffn_ref[...] * (1.0 + sc2) + sh2
            h2_ref[...] = h2.astype(BF)
            h2t_ref[...] = h2.T.astype(BF)
            acc_ref[...] = jnp.zeros_like(acc_ref)

        h2b = h2_ref[...]
        a = _dot_nt(h2b, wg_ref[...])
        b = _dot_nt(h2b, wu_ref[...])
        a_ref[...] = a.astype(BF)
        b_ref[...] = b.astype(BF)
        f = _silu_parts(a)[1] * b
        acc_ref[...] += _dot(f.astype(BF), wd_ref[...])

        @pl.when(j == nj - 1)
        def _():
            gt2 = mod_ref[0:1, 5 * D:6 * D]
            ff = acc_ref[...]
            x3 = x2_ref[...] + gt2 * ff
            r3 = _rms(x3)
            xn3 = x3 * r3
            gfin = gfin_ref[...]
            e = xn3 * gfin - t_ref[...]
            loss_ref[...] += 0.5 * jnp.sum(jnp.mean(e * e, axis=-1, keepdims=True))
            dy = e * (1.0 / D)
            dgfin_ref[...] += _colsum(dy * xn3)
            dx3 = _rms_bwd(dy * gfin, xn3, r3)
            dx3_ref[...] = dx3
            dgt2_ref[...] += _colsum(dx3 * ff)
            dff = dx3 * gt2
            dff_ref[...] = dff.astype(BF)
            dfft_ref[...] = dff.T.astype(BF)

    row = lambda c: pl.BlockSpec((ts, c), lambda i, j: (i, 0))
    col = pl.BlockSpec((D, ts), lambda i, j: (0, i))
    wblk = pl.BlockSpec((tf, D), lambda i, j: (j, 0))
    act = pl.BlockSpec((ts, tf), lambda i, j: (i, j))
    const = lambda shape: pl.BlockSpec(shape, lambda i, j: (0,) * len(shape))
    out_shape = (
        jax.ShapeDtypeStruct((S, D), F32),
        jax.ShapeDtypeStruct((S, D), F32),
        jax.ShapeDtypeStruct((D, S), BF),
        jax.ShapeDtypeStruct((S, FF), BF),
        jax.ShapeDtypeStruct((S, FF), BF),
        jax.ShapeDtypeStruct((S, D), F32),
        jax.ShapeDtypeStruct((S, D), BF),
        jax.ShapeDtypeStruct((D, S), BF),
        jax.ShapeDtypeStruct((8, 128), F32),
        jax.ShapeDtypeStruct((1, D), F32),
        jax.ShapeDtypeStruct((1, D), F32),
    )
    return pl.pallas_call(
        body, name="ffn_fwd", out_shape=out_shape, grid=(S // ts, nj),
        in_specs=[row(D), row(PW), row(PW), const(mod.shape), const(w_o.shape), const((1, D)), wblk, wblk, wblk,
                  const((1, D)), row(D)],
        out_specs=(row(D), row(D), col, act, act, row(D), row(D), col, const((8, 128)), const((1, D)),
                   const((1, D))),
        scratch_shapes=[pltpu.VMEM((ts, D), F32), pltpu.VMEM((ts, D), BF)],
        compiler_params=_params(("arbitrary", "arbitrary")),
    )(x, ymla, ypool, mod, w_o, g_ffn, wg_t, wu_t, wd, g_final, target)


FCHUNK = 256


def _ffn_bwd_acts(dff, a, b, wg_t, wu_t, wd):
    S = dff.shape[0]
    ts = 512

    def body(dff_ref, a_ref, b_ref, wg_ref, wu_ref, wd_ref, da_ref, db_ref, dh2_ref):
        dffb = dff_ref[...]
        for c in range(FF // FCHUNK):
            cols = slice(c * FCHUNK, (c + 1) * FCHUNK)
            df = _dot_nt(dffb, wd_ref[cols, :])
            av = a_ref[:, cols].astype(F32)
            bv = b_ref[:, cols].astype(F32)
            sg, sa = _silu_parts(av)
            db_ref[:, cols] = (df * sa).astype(BF)
            da_ref[:, cols] = (df * bv * (sg * (1.0 + av * (1.0 - sg)))).astype(BF)
        dh2_ref[...] = _dot(da_ref[...], wg_ref[...]) + _dot(db_ref[...], wu_ref[...])

    act = _rows(ts, FF)
    return pl.pallas_call(
        body, name="ffn_bwd_acts",
        out_shape=(jax.ShapeDtypeStruct((S, FF), BF), jax.ShapeDtypeStruct((S, FF), BF),
                   jax.ShapeDtypeStruct((S, D), F32)),
        grid=(S // ts,), in_specs=[_rows(ts, D), act, act, _vmem(), _vmem(), _vmem()],
        out_specs=(act, act, _rows(ts, D)), compiler_params=_params(("arbitrary",)),
    )(dff, a, b, wg_t, wu_t, wd)


def _ffn_bwd_weights(dff_t, h2_t, da, db, a, b):
    S = da.shape[0]

    def body(dfft_ref, h2t_ref, da_ref, db_ref, a_ref, b_ref, dwg_ref, dwu_ref, dwd_ref):
        h2t = h2t_ref[...]
        dwg_ref[...] = _dot(h2t, da_ref[...]).T.astype(BF)
        dwu_ref[...] = _dot(h2t, db_ref[...]).T.astype(BF)
        f = (_silu_parts(a_ref[...].astype(F32))[1] * b_ref[...].astype(F32)).astype(BF)
        dwd_ref[...] = _dot(dfft_ref[...], f).T.astype(BF)

    act = pl.BlockSpec((S, FCHUNK), lambda j: (0, j))
    wblk = _rows(FCHUNK, D)
    shp = jax.ShapeDtypeStruct((FF, D), BF)
    return pl.pallas_call(
        body, name="ffn_bwd_weights", out_shape=(shp, shp, shp), grid=(FF // FCHUNK,),
        in_specs=[_vmem(), _vmem(), act, act, act, act], out_specs=(wblk, wblk, wblk),
        compiler_params=_params(("arbitrary",)),
    )(dff_t, h2_t, da, db, a, b)


def _mix_bwd(dh2, dx3, x2, mix, mod, g_ffn, ymla, ypool, w_o, ypre, pooled, pool_scale, wpool_dc, olat, wuv_vc):
    S = dh2.shape[0]
    ts = 512
    n = S // ts
    nsub = ts // TQ
    M = HEADS * TQ

    def body(dh2_ref, dx3_ref, x2_ref, mix_ref, mod_ref, gffn_ref, ymla_ref, ypool_ref, wo_ref, ypre_ref, pooled_ref,
             pscale_ref, wpool_ref, olat_ref, wuv_ref,
             dx2_ref, du_ref, dolat_ref, delta_ref, dwo_ref, dwuv_ref, dwpool_ref, dpscale_ref, dgt1_ref, dsc2_ref,
             dsh2_ref, dgffn_ref, carry_ref, dwo_acc):
        i = pl.program_id(0)

        @pl.when(i == 0)
        def _():
            carry_ref[...] = jnp.zeros_like(carry_ref)
            dwo_acc[...] = jnp.zeros_like(dwo_acc)
            for r in (dwuv_ref, dwpool_ref, dpscale_ref, dgt1_ref, dsc2_ref, dsh2_ref, dgffn_ref):
                r[...] = jnp.zeros_like(r)

        gt1 = mod_ref[0:1, 2 * D:3 * D]
        sc2 = mod_ref[0:1, 4 * D:5 * D]
        gffn = gffn_ref[...]
        dh2 = dh2_ref[...]
        x2 = x2_ref[...]
        r2 = _rms(x2)
        xn2 = x2 * r2
        dsc2_ref[...] += _colsum(dh2 * (xn2 * gffn))
        dsh2_ref[...] += _colsum(dh2)
        dgffn_ref[...] += _colsum(dh2 * (1.0 + sc2) * xn2)
        dx2 = dx3_ref[...] + _rms_bwd(dh2 * gffn * (1.0 + sc2), xn2, r2)
        dx2_ref[...] = dx2
        dgt1_ref[...] += _colsum(dx2 * mix_ref[...])
        dmix = (dx2 * gt1).astype(BF)
        cat = jnp.concatenate([ymla_ref[...], ypool_ref[...]], axis=1)
        dwo_acc[...] += _dot_tn(cat, dmix)
        dcat = _dot_nt(dmix, wo_ref[...])
        dymla = dcat[:, 0:512]
        dypool = dcat[:, 512:1024]

        dpscale_ref[...] += _colsum(dypool * ypre_ref[...])
        dypre = (dypool * pscale_ref[...]).astype(BF)
        pooled = pooled_ref[...]
        dpooled = []
        for g in range(GROUPS):
            sl = slice(g * GD, (g + 1) * GD)
            dwpool_ref[g] += _dot_tn(pooled[:, sl], dypre[:, sl])
            dpooled.append(_dot(dypre[:, sl], wpool_ref[g]))
        dpooled = jnp.concatenate(dpooled, axis=1)
        tile = n - 1 - i
        e = dpooled / _row_counts(tile * ts, ts)
        ext = jnp.concatenate([e, carry_ref[...]], axis=0)
        du_ref[...] = _window_sums(ext, False)[0:ts, :] - dpooled
        carry_ref[...] = e[0:16, :]

        for hd in range(HEADS):
            do = dymla[:, hd * 128:(hd + 1) * 128]
            dob = do.astype(BF)
            dol = _dot(dob, wuv_ref[hd])
            for a in range(nsub):
                ol = olat_ref[a, hd * TQ:(hd + 1) * TQ, :]
                dl = dol[a * TQ:(a + 1) * TQ, :]
                dolat_ref[a, hd * TQ:(hd + 1) * TQ, :] = dl.astype(BF)
                dwuv_ref[hd] += _dot_tn(ol, dob[a * TQ:(a + 1) * TQ, :])
                delta = jnp.sum(dl * ol, axis=-1, keepdims=True)
                delta_ref[a, :, hd * TQ:(hd + 1) * TQ] = jnp.broadcast_to(delta, (TQ, 128)).T[0:8, :]

        @pl.when(i == n - 1)
        def _():
            dwo_ref[...] = dwo_acc[...].astype(BF)

    rev = lambda c: pl.BlockSpec((ts, c), lambda i: (n - 1 - i, 0))
    rev3 = lambda r, c: pl.BlockSpec((nsub, r, c), lambda i: (n - 1 - i, 0, 0))
    out_shape = (
        jax.ShapeDtypeStruct((S, D), F32),
        jax.ShapeDtypeStruct((S, PW), F32),
        jax.ShapeDtypeStruct((S // TQ, M, KVL), BF),
        jax.ShapeDtypeStruct((S // TQ, 8, M), F32),
        jax.ShapeDtypeStruct((D, D), BF),
        jax.ShapeDtypeStruct((HEADS, KVL, 128), F32),
        jax.ShapeDtypeStruct((GROUPS, GD, GD), F32),
        jax.ShapeDtypeStruct((1, PW), F32),
        jax.ShapeDtypeStruct((1, D), F32), jax.ShapeDtypeStruct((1, D), F32), jax.ShapeDtypeStruct((1, D), F32),
        jax.ShapeDtypeStruct((1, D), F32),
    )
    in_specs = [rev(D), rev(D), rev(D), rev(D), _full(mod.shape), _full((1, D)), rev(PW), rev(PW), _full(w_o.shape),
                rev(PW), rev(PW), _full((1, PW)), _full(wpool_dc.shape), rev3(M, KVL), _full(wuv_vc.shape)]
    out_specs = (rev(D), rev(PW), rev3(M, KVL), rev3(8, M), _full((D, D)), _full((HEADS, KVL, 128)),
                 _full((GROUPS, GD, GD)), _full((1, PW)), _full((1, D)), _full((1, D)), _full((1, D)), _full((1, D)))
    return pl.pallas_call(
        body, name="mix_bwd", out_shape=out_shape, grid=(n,), in_specs=in_specs, out_specs=out_specs,
        scratch_shapes=[pltpu.VMEM((16, PW), F32), pltpu.VMEM((D, D), F32)],
        compiler_params=_params(("arbitrary",)),
    )(dh2, dx3, x2, mix, mod, g_ffn, ymla, ypool, w_o, ypre, pooled, pool_scale, wpool_dc, olat, wuv_vc)


def _attn_bwd(qs, kv, dolat, lse, delta):
    nq = qs.shape[0]
    S = kv.shape[0]
    M = HEADS * TQ
    nk = S // TK

    def body(qs_ref, kv_ref, do_ref, lse_ref, delta_ref, dkv_ref, dqt_ref):
        kt = pl.program_id(0)
        k = kv_ref[...]
        v = k[:, 0:KVL]
        k_t = k.astype(F32).T.astype(BF)

        @pl.when(kt == 0)
        def _():
            dqt_ref[...] = jnp.zeros_like(dqt_ref)

        def step(qi, carry, masked):
            dk, dv = carry
            q = qs_ref[qi]
            do = do_ref[qi]
            s = _dot_nt(k, q) * SM_SCALE
            p = jnp.exp(s - lse_ref[qi, 0:1, :])
            if masked:
                p = jnp.where(_diag_mask((TK, M), 1), p, 0.0)
            dp = _dot_nt(v, do)
            ds = (p * (dp - delta_ref[qi, 0:1, :]) * SM_SCALE).astype(BF)
            dv = dv + _dot(p.astype(BF), do)
            dk = dk + _dot(ds, q)
            dqt_ref[qi] += _dot(k_t, ds)
            return dk, dv

        carry = step(kt, (jnp.zeros((TK, QW), F32), jnp.zeros((TK, KVL), F32)), True)
        dk, dv = lax.fori_loop(kt + 1, nq, lambda qi, c: step(qi, c, False), carry)
        dkv_ref[...] = dk + jnp.concatenate([dv, jnp.zeros((TK, QW - KVL), F32)], axis=1)

    out_shape = (jax.ShapeDtypeStruct((S, QW), F32), jax.ShapeDtypeStruct((nq, QW, M), F32))
    return pl.pallas_call(
        body, name="attn_bwd", out_shape=out_shape, grid=(nk,),
        in_specs=[_vmem(), _rows(TK, QW), _vmem(), _vmem(), _vmem()],
        out_specs=(_rows(TK, QW), _vmem()),
        compiler_params=_params(("arbitrary",)),
    )(qs, kv, dolat, lse, delta)


def _in_bwd(dqt, dkv, du, raw, qn, h1, x, dx2, mod, g_mix, w_in, g_q, g_kv, w_uq, wuk_cd, perm_t, cos4, sin4, csk,
            snk):
    S = x.shape[0]
    ts = 512
    n = S // ts
    nsub = ts // TQ
    M = HEADS * TQ

    def body(dqt_ref, dkv_ref, du_ref, raw_ref, qn_ref, h1_ref, x_ref, dx2_ref, mod_ref, gmix_ref, win_ref, gq_ref,
             gkv_ref, wuq_ref, wuk_ref, permt_ref, cos_ref, sin_ref, csk_ref, snk_ref,
             dx_ref, dwin_ref, dwuq_ref, dwuk_ref, dgq_ref, dgkv_ref, dsc1_ref, dsh1_ref, dgmix_ref, dwin_acc,
             dwuq_acc):
        i = pl.program_id(0)

        @pl.when(i == 0)
        def _():
            dwin_acc[...] = jnp.zeros_like(dwin_acc)
            dwuq_acc[...] = jnp.zeros_like(dwuq_acc)
            for r in (dwuk_ref, dgq_ref, dgkv_ref, dsc1_ref, dsh1_ref, dgmix_ref):
                r[...] = jnp.zeros_like(r)

        dq_blocks = [dqt_ref[a].T for a in range(nsub)]
        qn = qn_ref[...]
        dq_parts = []
        drope = jnp.zeros((ts, 2 * 128), F32)
        for hd in range(HEADS):
            dqh = jnp.concatenate([blk[hd * TQ:(hd + 1) * TQ, :] for blk in dq_blocks], axis=0)
            dq_lat = dqh[:, 0:KVL].astype(BF)
            dq_parts.append(_dot(dq_lat, wuk_ref[hd]))
            dwuk_ref[hd] += _dot_tn(dq_lat, qn[:, hd * NOPE:(hd + 1) * NOPE])
            drope = drope + _dot(dqh[:, KVL:QW].astype(BF), permt_ref[hd])
        do1 = drope[:, 0:128]
        do2 = drope[:, 128:256]
        cosv = cos_ref[...]
        sinv = sin_ref[...]
        dq_parts.append(do1 * cosv + do2 * sinv)
        dq_parts.append(do2 * cosv - do1 * sinv)
        dq = jnp.concatenate(dq_parts, axis=1).astype(BF)

        cq_raw = raw_ref[:, 0:QL]
        ckv_raw = raw_ref[:, QL:QL + KVL]
        rq = _rms(cq_raw)
        nq_ = cq_raw * rq
        gq = gq_ref[...]
        dwuq_acc[...] += _dot_tn((nq_ * gq).astype(BF), dq)
        dc_q = _dot_nt(dq, wuq_ref[...])
        dgq_ref[...] += _colsum(dc_q * nq_)
        dcq_raw = _rms_bwd(dc_q * gq, nq_, rq)

        dkv = dkv_ref[...]
        rk = _rms(ckv_raw)
        nk_ = ckv_raw * rk
        dc_kv = dkv[:, 0:KVL]
        dgkv_ref[...] += _colsum(dc_kv * nk_)
        dckv_raw = _rms_bwd(dc_kv * gkv_ref[...], nk_, rk)
        dkr_roped = dkv[:, KVL:QW]
        dkr = dkr_roped * csk_ref[...] - _swap_halves(dkr_roped) * snk_ref[...]

        dproj = jnp.concatenate([dcq_raw, dckv_raw, dkr, du_ref[...]], axis=1).astype(BF)
        dwin_acc[...] += _dot_tn(h1_ref[...], dproj)
        dh1 = _dot_nt(dproj, win_ref[...])

        sc1 = mod_ref[0:1, D:2 * D]
        gmix = gmix_ref[...]
        xv = x_ref[...]
        r1 = _rms(xv)
        xn1 = xv * r1
        dsc1_ref[...] += _colsum(dh1 * (xn1 * gmix))
        dsh1_ref[...] += _colsum(dh1)
        dgmix_ref[...] += _colsum(dh1 * (1.0 + sc1) * xn1)
        dx_ref[...] = dx2_ref[...] + _rms_bwd(dh1 * gmix * (1.0 + sc1), xn1, r1)

        @pl.when(i == n - 1)
        def _():
            dwin_ref[...] = dwin_acc[...].astype(BF)
            dwuq_ref[...] = dwuq_acc[...].astype(BF)

    out_shape = (
        jax.ShapeDtypeStruct((S, D), F32),
        jax.ShapeDtypeStruct((D, D), BF),
        jax.ShapeDtypeStruct((QL, 768), BF),
        jax.ShapeDtypeStruct((HEADS, KVL, NOPE), F32),
        jax.ShapeDtypeStruct((1, QL), F32), jax.ShapeDtypeStruct((1, KVL), F32),
        jax.ShapeDtypeStruct((1, D), F32), jax.ShapeDtypeStruct((1, D), F32), jax.ShapeDtypeStruct((1, D), F32),
    )
    in_specs = [pl.BlockSpec((nsub, QW, M), lambda i: (i, 0, 0)), _rows(ts, QW), _rows(ts, PW), _rows(ts, 384),
                _rows(ts, HEADS * NOPE), _rows(ts, D), _rows(ts, D), _rows(ts, D), _full(mod.shape), _full((1, D)),
                _full(w_in.shape), _full((1, QL)), _full((1, KVL)), _full(w_uq.shape), _full(wuk_cd.shape),
                _full(perm_t.shape), _rows(ts, 128), _rows(ts, 128), _rows(ts, 128), _rows(ts, 128)]
    out_specs = (_rows(ts, D), _full((D, D)), _full((QL, 768)), _full((HEADS, KVL, NOPE)), _full((1, QL)),
                 _full((1, KVL)), _full((1, D)), _full((1, D)), _full((1, D)))
    return pl.pallas_call(
        body, name="in_bwd", out_shape=out_shape, grid=(n,), in_specs=in_specs, out_specs=out_specs,
        scratch_shapes=[pltpu.VMEM((D, D), F32), pltpu.VMEM((QL, 768), F32)],
        compiler_params=_params(("arbitrary",)),
    )(dqt, dkv, du, raw, qn, h1, x, dx2, mod, g_mix, w_in, g_q, g_kv, w_uq, wuk_cd, perm_t, cos4, sin4, csk, snk)


def _rope_perm():
    p = np.zeros((HEADS, 2 * 128, 128), np.float32)
    for hd in range(HEADS):
        for t in range(HALF):
            p[hd, hd * HALF + t, t] = 1.0
            p[hd, 128 + hd * HALF + t, HALF + t] = 1.0
    return p


def _rope_tables(positions):
    freqs = jnp.power(ROPE_THETA, -jnp.arange(HALF, dtype=F32) / HALF)
    ang = positions.astype(F32)[:, None] * jnp.tile(freqs, HEADS)[None, :]
    cos4 = jnp.cos(ang)
    sin4 = jnp.sin(ang)
    lane = jnp.arange(HEADS * HALF)[None, :]
    csk = jnp.where(lane < ROPE, cos4, 0.0)
    snk = jnp.where(lane < HALF, -sin4, jnp.where(lane < ROPE, sin4, 0.0))
    return cos4, sin4, csk, snk


def _local_step(x, positions, target, mod, g_mix, w_in_p, g_q, g_kv, w_uq_p, w_uk, w_uv, w_pool, pool_scale, g_ffn,
                g_final, late, ffn_grads_exchange):
    perm = jnp.asarray(_rope_perm(), BF)
    perm_t = jnp.asarray(_rope_perm().transpose(0, 2, 1), BF)
    cos4, sin4, csk, snk = _rope_tables(positions)
    wuk_dc = w_uk.transpose(1, 2, 0).astype(BF)
    wuk_cd = w_uk.transpose(1, 0, 2).astype(BF)
    wuv_cv = w_uv.transpose(1, 0, 2).astype(BF)
    wuv_vc = w_uv.transpose(1, 2, 0).astype(BF)
    wpool = w_pool.astype(BF)
    wpool_dc = w_pool.transpose(0, 2, 1).astype(BF)

    h1, raw, qn, qs, kv, pooled, ypre, ypool = _fwd_in(
        x, mod, g_mix, w_in_p, g_q, g_kv, w_uq_p, wuk_dc, perm, cos4, sin4, csk, snk, wpool, pool_scale)
    olat, ymla, lse = _attn_fwd(qs, kv, wuv_cv)
    w_o, wg_t, wu_t, wd = late
    x2, mix, h2_t, a, b, dx3, dff, dff_t, loss, dgfin, dgt2 = _ffn_fwd(
        x, ymla, ypool, mod, w_o, g_ffn, wg_t, wu_t, wd, g_final, target)
    da, db, dh2 = _ffn_bwd_acts(dff, a, b, wg_t, wu_t, wd)
    dwg_t, dwu_t, dwd = _ffn_bwd_weights(dff_t, h2_t, da, db, a, b)
    ffn_parts = ffn_grads_exchange((dwg_t, dwu_t, dwd))
    (dx2, du, dolat, delta, dwo, dwuv, dwpool, dpscale, dgt1, dsc2, dsh2, dgffn) = _mix_bwd(
        dh2, dx3, x2, mix, mod, g_ffn, ymla, ypool, w_o, ypre, pooled, pool_scale, wpool_dc, olat, wuv_vc)
    dkv, dqt = _attn_bwd(qs, kv, dolat, lse, delta)
    dx, dwin, dwuq, dwuk, dgq, dgkv, dsc1, dsh1, dgmix = _in_bwd(
        dqt, dkv, du, raw, qn, h1, x, dx2, mod, g_mix, w_in_p, g_q, g_kv, w_uq_p, wuk_cd, perm_t, cos4, sin4, csk,
        snk)
    dmod = jnp.concatenate([dsh1, dsc1, dgt1, dsh2, dsc2, dgt2], axis=1)
    replicated = dict(
        w_uk=dwuk.transpose(1, 0, 2), w_uv=dwuv.transpose(1, 0, 2), w_pool=dwpool, g_mix=dgmix, g_q=dgq, g_kv=dgkv,
        pool_scale=dpscale, g_ffn=dgffn, g_final=dgfin)
    return loss[0, 0], dx, dmod, (dwin, dwuq, dwo), ffn_parts, replicated


def _my_pos():
    return lax.axis_index("x"), lax.axis_index("y"), lax.axis_index("c")


def _peer(pos, k):
    x, y, c = pos
    return (1 - x if k & 4 else x, 1 - y if k & 2 else y, 1 - c if k & 1 else c)


def _index(pos):
    x, y, c = pos
    return 4 * x + 2 * y + c


def _remote(src, dst, send_sem, recv_sem, to):
    return pltpu.make_async_remote_copy(src_ref=src, dst_ref=dst, send_sem=send_sem, recv_sem=recv_sem,
                                        device_id=to, device_id_type=MESH)


def _ada_mod(c, w_ada, b_ada):
    def body(c_ref, w_ref, b_ref, mod_ref, call_ref, cbuf, sbuf, rbuf, send1, recv1, send2, recv2):
        me = _my_pos()
        mi = _index(me)
        cv = c_ref[...]
        cbuf[...] = jnp.broadcast_to(cv * jax.nn.sigmoid(cv), (8, D))
        call_ref[mi] = cbuf[...]
        first = [_remote(cbuf, call_ref.at[mi], send1.at[k - 1], recv1.at[k - 1], _peer(me, k)) for k in range(1, NDEV)]
        for cp in first:
            cp.start()
        for k in range(1, NDEV):
            _remote(cbuf, call_ref.at[_index(_peer(me, k))], send1.at[k - 1], recv1.at[k - 1], _peer(me, k)).wait_recv()
        c_all = jnp.concatenate([call_ref[b][0:1, :] for b in range(NDEV)], axis=0)
        blocks = _dot(c_all.astype(BF), w_ref[...].astype(BF))
        for b in range(NDEV):
            sbuf[b] = jnp.broadcast_to(blocks[b:b + 1, :], (8, MODC))
        second = []
        for k in range(1, NDEV):
            to = _peer(me, k)
            second.append(_remote(sbuf.at[_index(to)], rbuf.at[mi], send2.at[k - 1], recv2.at[k - 1], to))
        for cp in second:
            cp.start()
        rbuf[mi] = sbuf[mi]
        for k in range(1, NDEV):
            to = _peer(me, k)
            _remote(sbuf.at[_index(to)], rbuf.at[_index(to)], send2.at[k - 1], recv2.at[k - 1], to).wait_recv()
        for j in range(NDEV):
            mod_ref[:, j * MODC:(j + 1) * MODC] = rbuf[j] + b_ref[:, j * MODC:(j + 1) * MODC]
        for cp in first + second:
            cp.wait_send()

    return pl.pallas_call(
        body, name="ada_mod",
        out_shape=(jax.ShapeDtypeStruct((8, N_MOD * D), F32), jax.ShapeDtypeStruct((NDEV, 8, D), F32)),
        in_specs=[_vmem(), _vmem(), _vmem()], out_specs=(_vmem(), _vmem()),
        scratch_shapes=[pltpu.VMEM((8, D), F32), pltpu.VMEM((NDEV, 8, MODC), F32), pltpu.VMEM((NDEV, 8, MODC), F32),
                        pltpu.SemaphoreType.DMA((NDEV - 1,)), pltpu.SemaphoreType.DMA((NDEV - 1,)),
                        pltpu.SemaphoreType.DMA((NDEV - 1,)), pltpu.SemaphoreType.DMA((NDEV - 1,))],
        compiler_params=_params(),
    )(c, w_ada, b_ada)


def _exchange_slices(scatter):
    def of(src, to_index):
        if not scatter:
            return src
        r = src.shape[0] // NDEV
        return src.at[pl.ds(pl.multiple_of(to_index * r, 16), r), :]
    return of


def _sequencer_exchange(name, collective_id, srcs, scatter, after=()):
    n = len(srcs)
    of = _exchange_slices(scatter)

    def body(*refs):
        src, zone = refs[:n], refs[n + len(after):2 * n + len(after)]
        send, recv, local = refs[2 * n + len(after):]
        me = _my_pos()
        mi = _index(me)
        barrier = pltpu.get_barrier_semaphore()
        for k in range(1, NDEV):
            pl.semaphore_signal(barrier, inc=1, device_id=_peer(me, k), device_id_type=MESH)
        pl.semaphore_wait(barrier, NDEV - 1)
        own = [pltpu.make_async_copy(of(src[a], mi), zone[a].at[mi], local.at[a]) for a in range(n)]
        for cp in own:
            cp.start()
        for a in range(n):
            for k in range(1, NDEV):
                to = _peer(me, k)
                s = a * (NDEV - 1) + k - 1
                _remote(of(src[a], _index(to)), zone[a].at[mi], send.at[s], recv.at[s], to).start()
        for cp in own:
            cp.wait()
        for a in range(n):
            for k in range(1, NDEV):
                to = _peer(me, k)
                s = a * (NDEV - 1) + k - 1
                cp = _remote(of(src[a], mi), zone[a].at[_index(to)], send.at[s], recv.at[s], to)
                cp.wait_send()
                cp.wait_recv()

    return pl.kernel(
        body, name=name, mesh=plsc.ScalarSubcoreMesh(axis_name="sequencer", num_cores=1),
        out_type=tuple(jax.ShapeDtypeStruct((NDEV, s.shape[0] // NDEV if scatter else s.shape[0], s.shape[1]), s.dtype)
                       for s in srcs),
        scratch_types=[pltpu.SemaphoreType.DMA((n * (NDEV - 1),)), pltpu.SemaphoreType.DMA((n * (NDEV - 1),)),
                       pltpu.SemaphoreType.DMA((n,))],
        compiler_params=pltpu.CompilerParams(collective_id=collective_id),
    )(*srcs, *after)


def _sum_partials(name, parts):
    n = len(parts)

    def body(*refs):
        for a in range(n):
            acc = refs[a][0].astype(F32)
            for p in range(1, NDEV):
                acc = acc + refs[a][p].astype(F32)
            refs[n + a][...] = acc

    return pl.pallas_call(
        body, name=name,
        out_shape=tuple(jax.ShapeDtypeStruct(p.shape[1:], F32) for p in parts),
        in_specs=[_vmem()] * n, out_specs=tuple([_vmem()] * n), compiler_params=_params(),
    )(*parts)


def _small_all_reduce(buf):
    def body(buf_ref, got_ref, red_ref, mine, send1, recv1, send2, recv2):
        me = _my_pos()
        mi = _index(me)
        first = []
        for k in range(1, NDEV):
            to = _peer(me, k)
            first.append(_remote(buf_ref.at[_index(to)], got_ref.at[mi], send1.at[k - 1], recv1.at[k - 1], to))
        for cp in first:
            cp.start()
        got_ref[mi] = buf_ref[mi]
        for k in range(1, NDEV):
            to = _peer(me, k)
            _remote(buf_ref.at[mi], got_ref.at[_index(to)], send1.at[k - 1], recv1.at[k - 1], to).wait_recv()
        acc = got_ref[0]
        for p in range(1, NDEV):
            acc = acc + got_ref[p]
        mine[...] = acc
        second = [_remote(mine, red_ref.at[mi], send2.at[k - 1], recv2.at[k - 1], _peer(me, k)) for k in range(1, NDEV)]
        for cp in second:
            cp.start()
        red_ref[mi] = acc
        for k in range(1, NDEV):
            to = _peer(me, k)
            _remote(mine, red_ref.at[_index(to)], send2.at[k - 1], recv2.at[k - 1], to).wait_recv()
        for cp in first + second:
            cp.wait_send()

    return pl.pallas_call(
        body, name="small_all_reduce",
        out_shape=(jax.ShapeDtypeStruct(buf.shape, F32), jax.ShapeDtypeStruct(buf.shape, F32)),
        in_specs=[_vmem()], out_specs=(_vmem(), _vmem()),
        scratch_shapes=[pltpu.VMEM(buf.shape[1:], F32),
                        pltpu.SemaphoreType.DMA((NDEV - 1,)), pltpu.SemaphoreType.DMA((NDEV - 1,)),
                        pltpu.SemaphoreType.DMA((NDEV - 1,)), pltpu.SemaphoreType.DMA((NDEV - 1,))],
        compiler_params=_params(),
    )(buf)


def _adamw_math(w, g, m, v):
    m = ADAM_B1 * m + (1.0 - ADAM_B1) * g
    v = ADAM_B2 * v + (1.0 - ADAM_B2) * jnp.square(g)
    m_hat = m / (1.0 - ADAM_B1 ** ADAM_STEP)
    v_hat = v / (1.0 - ADAM_B2 ** ADAM_STEP)
    delta = -ADAM_LR * (m_hat / (jnp.sqrt(v_hat) + ADAM_EPS) + ADAM_WD * w)
    return delta, m, v


def _adamw_group(name, ws, gs, ms, vs):
    n = len(ws)

    def body(*refs):
        for a in range(n):
            w, g, m, v = (refs[q * n + a][...] for q in range(4))
            delta, m2, v2 = _adamw_math(w, g, m, v)
            refs[4 * n + a][...] = delta
            refs[5 * n + a][...] = m2
            refs[6 * n + a][...] = v2

    shapes = tuple(jax.ShapeDtypeStruct(w.shape, F32) for w in ws)
    outs = pl.pallas_call(
        body, name=name, out_shape=shapes * 3, in_specs=[_vmem()] * (4 * n), out_specs=tuple([_vmem()] * (3 * n)),
        compiler_params=_params(),
    )(*ws, *gs, *ms, *vs)
    return outs[:n], outs[n:2 * n], outs[2 * n:]


def _adamw_ada(w, m, v, c_all, dmod_rows):
    def body(w_ref, m_ref, v_ref, c_ref, dm_ref, g_ref, d_ref, m2_ref, v2_ref):
        g = _dot_tn(c_ref[...], dm_ref[...].astype(BF))
        g_ref[...] = g
        delta, m2, v2 = _adamw_math(w_ref[...], g, m_ref[...], v_ref[...])
        d_ref[...] = delta
        m2_ref[...] = m2
        v2_ref[...] = v2

    shp = jax.ShapeDtypeStruct(w.shape, F32)
    return pl.pallas_call(
        body, name="adamw_ada", out_shape=(shp, shp, shp, shp), in_specs=[_vmem()] * 5,
        out_specs=tuple([_vmem()] * 4), compiler_params=_params(),
    )(w, m, v, c_all, dmod_rows)


def _w_in_to_kernel(w):
    return jnp.concatenate([w[:, 0:448], jnp.zeros((w.shape[0], 64), w.dtype), w[:, 448:960]], axis=1)


def _w_in_from_kernel(w):
    return jnp.concatenate([w[:, 0:448], w[:, 512:1024]], axis=1)


def _w_uq_to_kernel(w):
    r = w.shape[0]
    return jnp.concatenate([w[:, :, 0:NOPE].reshape(r, HEADS * NOPE),
                            w[:, :, NOPE:NOPE + HALF].reshape(r, HEADS * HALF),
                            w[:, :, NOPE + HALF:].reshape(r, HEADS * HALF)], axis=1)


def _w_uq_from_kernel(w):
    r = w.shape[0]
    return jnp.concatenate([w[:, 0:512].reshape(r, HEADS, NOPE), w[:, 512:640].reshape(r, HEADS, HALF),
                            w[:, 640:768].reshape(r, HEADS, HALF)], axis=2)


REP_NAMES = ("w_uk", "w_uv", "w_pool", "g_mix", "g_q", "g_kv", "pool_scale", "g_ffn", "g_final")


def kernel(x, c, positions, w_ada, b_ada, g_mix, w_in, g_q, g_kv, w_uq, w_uk, w_uv, w_pool, pool_scale, w_o, g_ffn, w_gate, w_up, w_down, g_final, loss_target, m_w_ada, m_b_ada, m_g_mix, m_w_in, m_g_q, m_g_kv, m_w_uq, m_w_uk, m_w_uv, m_w_pool, m_pool_scale, m_w_o, m_g_ffn, m_w_gate, m_w_up, m_w_down, m_g_final, v_w_ada, v_b_ada, v_g_mix, v_w_in, v_g_q, v_g_kv, v_w_uq, v_w_uk, v_w_uv, v_w_pool, v_pool_scale, v_w_o, v_g_ffn, v_w_gate, v_w_up, v_w_down, v_g_final):
    given = dict(locals())

    merge = lambda g: g.reshape(NDEV * g.shape[1], g.shape[2])
    w_in_p, w_uq_p = (merge(g) for g in _sequencer_exchange(
        "gather_in", 3, (_w_in_to_kernel(w_in[0]).astype(BF), _w_uq_to_kernel(w_uq[0]).astype(BF)), False))

    mod, c_all8 = _ada_mod(c, w_ada[0], b_ada)
    c_all = c_all8[:, 0, :]
    late = _sequencer_exchange(
        "gather_late", 1, (w_o[0].astype(BF), w_gate[0].T.astype(BF), w_up[0].T.astype(BF), w_down[0].astype(BF)),
        False, after=(mod[:, 0:128], w_in_p[0:16, 0:128], w_uq_p[0:16, 0:128]))

    def ffn_grads_exchange(arrays):
        return _sequencer_exchange("scatter_ffn", 2, arrays, True)

    loss, dx, dmod, tail_grads, ffn_parts, replicated = _local_step(
        x[0], positions[0], loss_target[0], mod, g_mix, w_in_p, g_q, g_kv, w_uq_p, w_uk[0], w_uv[0], w_pool[0],
        pool_scale, g_ffn, g_final.reshape(1, D), tuple(merge(g) for g in late), ffn_grads_exchange)

    flat = jnp.concatenate([replicated[k].reshape(-1) for k in REP_NAMES] + [loss.reshape(1)])
    flat = jnp.pad(flat, (0, NDEV * REP_ROWS * 128 - flat.shape[0])).reshape(NDEV, REP_ROWS, 128)
    dmod_blocks = jnp.pad(dmod.reshape(NDEV, MODC // 128, 128), ((0, 0), (0, MOD_ROWS - MODC // 128), (0, 0)))
    got, red = _small_all_reduce(jnp.concatenate([dmod_blocks, flat], axis=1))

    tail_parts = _sequencer_exchange("scatter_tail", 4, tail_grads, True,
                                     after=(ffn_parts[0][0, 0:16, 0:128], red[0, 0:8, :]))
    g_gate_t, g_up_t, g_down = _sum_partials("sum_ffn_partials", ffn_parts)
    g_in_p, g_uq_p, g_o = _sum_partials("sum_tail_partials", tail_parts)
    grads = dict(w_in=_w_in_from_kernel(g_in_p), w_uq=_w_uq_from_kernel(g_uq_p).reshape(QL // NDEV, HEADS * 192),
                 w_o=g_o, w_gate=g_gate_t.T, w_up=g_up_t.T, w_down=g_down)
    dmod_rows = got[:, 0:MODC // 128, :].reshape(NDEV, MODC)
    grads["b_ada"] = red[:, 0:MODC // 128, :].reshape(1, N_MOD * D)
    rep_flat = red[:, MOD_ROWS:, :].reshape(-1)
    off = 0
    for k in REP_NAMES:
        size = int(np.prod(given[k].shape))
        grads[k] = rep_flat[off:off + size]
        off += size

    view = dict(w_ada=(D, MODC), b_ada=(1, N_MOD * D), g_mix=(1, D), w_in=(D // NDEV, 960), g_q=(1, QL),
                g_kv=(1, KVL), w_uq=(QL // NDEV, HEADS * 192), w_uk=(KVL, HEADS * NOPE), w_uv=(KVL, HEADS * 128),
                w_pool=(GROUPS * GD, GD), pool_scale=(1, PW), w_o=(D // NDEV, D), g_ffn=(1, D),
                w_gate=(D, FF // NDEV), w_up=(D, FF // NDEV), w_down=(FF // NDEV, D), g_final=(1, D))
    names = list(view)
    g_ada, d_ada, m_ada, v_ada = _adamw_ada(w_ada[0], m_w_ada[0], v_w_ada[0], c_all.astype(BF), dmod_rows)
    out_g, out_d, out_m, out_v = dict(w_ada=g_ada), dict(w_ada=d_ada), dict(w_ada=m_ada), dict(w_ada=v_ada)
    tail = ("w_in", "w_uq", "w_o")
    groups = (("adamw_ffn", ("w_gate", "w_up", "w_down")),
              ("adamw_replicated", tuple(k for k in names if k not in ("w_ada", "w_gate", "w_up", "w_down") + tail)),
              ("adamw_tail", tail))
    for gname, members in groups:
        ws = [given[k].reshape(view[k]) for k in members]
        gs = [grads[k].reshape(view[k]) for k in members]
        ms = [given["m_" + k].reshape(view[k]) for k in members]
        vs = [given["v_" + k].reshape(view[k]) for k in members]
        ds, m2, v2 = _adamw_group(gname, ws, gs, ms, vs)
        for k, g, d, mm, vv in zip(members, gs, ds, m2, v2):
            out_g[k], out_d[k], out_m[k], out_v[k] = g, d, mm, vv

    total = rep_flat[off]
    shaped = lambda d: [d[k].reshape(given[k].shape) for k in names]
    return (total, dx[None], *shaped(out_g), *shaped(out_d), *shaped(out_m), *shaped(out_v))
```

```python
import numpy as np
import jax
import jax.numpy as jnp
from jax import lax
from jax.experimental import pallas as pl
from jax.experimental.pallas import tpu as pltpu
from jax.experimental.pallas import tpu_sc as plsc

D = 1024
HEADS = 4
NOPE = 128
ROPE = 64
HALF = ROPE // 2
QL = 256
KVL = 128
FF = 2816
PW = 512
GROUPS = 4
GD = 128
N_MOD = 6
EPS = 1e-6
SM_SCALE = (NOPE + ROPE) ** -0.5
ROPE_THETA = 10000.0
NDEV = 8
MODC = N_MOD * D // NDEV

ADAM_LR = 0.001
ADAM_B1 = 0.9
ADAM_B2 = 0.999
ADAM_EPS = 1e-08
ADAM_WD = 0.01
ADAM_STEP = 10

BF = jnp.bfloat16
F32 = jnp.float32
VMEM_LIMIT_V7X = 60 * 1024 * 1024
MESH = pl.DeviceIdType.MESH

TQ = 256
TK = 256
QW = 256
MOD_ROWS = 8
REP_ROWS = 200
SMALL_ROWS = MOD_ROWS + REP_ROWS


def _params(sem=None):
    return pltpu.CompilerParams(dimension_semantics=sem, vmem_limit_bytes=VMEM_LIMIT_V7X)


def _dot(a, b):
    return jnp.dot(a, b, preferred_element_type=F32)


def _dot_nt(a, b):
    return lax.dot_general(a, b, (((1,), (1,)), ((), ())), preferred_element_type=F32)


def _dot_tn(a, b):
    return _dot(a.astype(F32).T.astype(BF), b)


def _full(shape):
    return pl.BlockSpec(shape, lambda *_: (0,) * len(shape))


def _rows(ts, cols):
    return pl.BlockSpec((ts, cols), lambda i: (i, 0))


def _vmem():
    return pl.BlockSpec(memory_space=pltpu.VMEM)


def _any():
    return pl.BlockSpec(memory_space=pl.ANY)


def _rms(v):
    return lax.rsqrt(jnp.mean(v * v, axis=-1, keepdims=True) + EPS)


def _rms_bwd(dn, n, r):
    return r * (dn - n * jnp.mean(dn * n, axis=-1, keepdims=True))


def _colsum(v):
    return jnp.sum(v, axis=0, keepdims=True)


def _swap_halves(v):
    lane = lax.broadcasted_iota(jnp.int32, v.shape, 1)
    return jnp.where(lane < HALF, pltpu.roll(v, 128 - HALF, 1), pltpu.roll(v, HALF, 1))


def _window_lane_width():
    lane = lax.broadcasted_iota(jnp.int32, (1, PW), 1)
    return jnp.where(lane < 128, 2.0, jnp.where(lane < 256, 4.0, jnp.where(lane < 384, 8.0, 16.0))).astype(F32)


def _window_sums(ext, back):
    n = ext.shape[0]

    def sh(v, k):
        return pltpu.roll(v, k if back else n - k, 0)

    s2 = ext + sh(ext, 1)
    e4 = s2[:, 128:]
    s4 = e4 + sh(e4, 2)
    e8 = s4[:, 128:]
    s8 = e8 + sh(e8, 4)
    e16 = s8[:, 128:]
    s16 = e16 + sh(e16, 8)
    return jnp.concatenate([s2[:, :128], s4[:, :128], s8[:, :128], s16], axis=1)


def _row_counts(first_row, ts):
    t1 = (first_row + lax.broadcasted_iota(jnp.int32, (ts, 1), 0) + 1).astype(F32)
    return jnp.minimum(t1, _window_lane_width())


def _fwd_in(x, mod, g_mix, w_in, g_q, g_kv, w_uq, wuk_dc, perm, cos4, sin4, csk, snk, w_pool, pool_scale):
    S = x.shape[0]
    ts = 512
    nsub = ts // TQ

    def body(x_ref, mod_ref, gmix_ref, win_ref, gq_ref, gkv_ref, wuq_ref, wuk_ref, perm_ref, cos_ref, sin_ref,
             csk_ref, snk_ref, wpool_ref, pscale_ref,
             h1_ref, raw_ref, qn_ref, qs_ref, kv_ref, kvt_ref, pooled_ref, ypre_ref, ypool_ref, carry_ref):
        i = pl.program_id(0)

        @pl.when(i == 0)
        def _():
            carry_ref[...] = jnp.zeros_like(carry_ref)

        xv = x_ref[...]
        sh1 = mod_ref[0:1, 0:D]
        sc1 = mod_ref[0:1, D:2 * D]
        h = (xv * _rms(xv)) * gmix_ref[...] * (1.0 + sc1) + sh1
        hb = h.astype(BF)
        h1_ref[...] = hb
        proj = _dot(hb, win_ref[...])
        cq_raw = proj[:, 0:QL]
        ckv_raw = proj[:, QL:QL + KVL]
        kr = proj[:, 384:512]
        u = proj[:, 512:1024]
        raw_ref[...] = proj[:, 0:384]

        c_q = (cq_raw * _rms(cq_raw)) * gq_ref[...]
        c_kv = (ckv_raw * _rms(ckv_raw)) * gkv_ref[...]
        q = _dot(c_q.astype(BF), wuq_ref[...])
        qn = q[:, 0:HEADS * NOPE].astype(BF)
        qn_ref[...] = qn
        x1 = q[:, 512:640]
        x2 = q[:, 640:768]
        cosv = cos_ref[...]
        sinv = sin_ref[...]
        roped = jnp.concatenate([x1 * cosv - x2 * sinv, x1 * sinv + x2 * cosv], axis=1).astype(BF)
        for hd in range(HEADS):
            q_lat = _dot(qn[:, hd * NOPE:(hd + 1) * NOPE], wuk_ref[hd])
            q_rope = _dot(roped, perm_ref[hd])
            qh = jnp.concatenate([q_lat, q_rope], axis=1).astype(BF)
            for a in range(nsub):
                qs_ref[a, hd * TQ:(hd + 1) * TQ, :] = qh[a * TQ:(a + 1) * TQ, :]
        k_rope = kr * csk_ref[...] + _swap_halves(kr) * snk_ref[...]
        keys = jnp.concatenate([c_kv, k_rope], axis=1)
        kv_ref[...] = keys.astype(BF)
        for a in range(ts // TK):
            kvt_ref[a] = keys[a * TK:(a + 1) * TK, :].T.astype(BF)

        ext = jnp.concatenate([carry_ref[...], u], axis=0)
        win = _window_sums(ext, True)[16:, :]
        pooled = (win / _row_counts(i * ts, ts) - u).astype(BF)
        pooled_ref[...] = pooled
        carry_ref[...] = u[ts - 16:ts, :]
        ypre = jnp.concatenate(
            [_dot(pooled[:, g * GD:(g + 1) * GD], wpool_ref[g]) for g in range(GROUPS)], axis=1)
        ypre_ref[...] = ypre
        ypool_ref[...] = (ypre * pscale_ref[...]).astype(BF)

    out_shape = (
        jax.ShapeDtypeStruct((S, D), BF),
        jax.ShapeDtypeStruct((S, 384), F32),
        jax.ShapeDtypeStruct((S, HEADS * NOPE), BF),
        jax.ShapeDtypeStruct((S // TQ, HEADS * TQ, QW), BF),
        jax.ShapeDtypeStruct((S, QW), BF),
        jax.ShapeDtypeStruct((S // TK, QW, TK), BF),
        jax.ShapeDtypeStruct((S, PW), BF),
        jax.ShapeDtypeStruct((S, PW), F32),
        jax.ShapeDtypeStruct((S, PW), BF),
    )
    in_specs = [
        _rows(ts, D), _full(mod.shape), _full((1, D)), _full(w_in.shape), _full((1, QL)), _full((1, KVL)),
        _full(w_uq.shape), _full(wuk_dc.shape), _full(perm.shape), _rows(ts, 128), _rows(ts, 128), _rows(ts, 128),
        _rows(ts, 128), _full(w_pool.shape), _full((1, PW)),
    ]
    out_specs = (
        _rows(ts, D), _rows(ts, 384), _rows(ts, HEADS * NOPE),
        pl.BlockSpec((nsub, HEADS * TQ, QW), lambda i: (i, 0, 0)),
        _rows(ts, QW), pl.BlockSpec((ts // TK, QW, TK), lambda i: (i, 0, 0)), _rows(ts, PW), _rows(ts, PW),
        _rows(ts, PW),
    )
    return pl.pallas_call(
        body, name="fwd_in", out_shape=out_shape, grid=(S // ts,), in_specs=in_specs, out_specs=out_specs,
        scratch_shapes=[pltpu.VMEM((16, PW), F32)], compiler_params=_params(("arbitrary",)),
    )(x, mod, g_mix, w_in, g_q, g_kv, w_uq, wuk_dc, perm, cos4, sin4, csk, snk, w_pool, pool_scale)


def _diag_mask(shape, q_axis):
    qi = (lax.broadcasted_iota(jnp.int32, shape, q_axis) & (TQ - 1)) >> 6
    ki = lax.broadcasted_iota(jnp.int32, shape, 1 - q_axis) >> 6
    return ki <= qi


def _attn_fwd(qs, kv, kvt, wuv_vc):
    nq = qs.shape[0]
    S = kv.shape[0]
    M = HEADS * TQ

    def body(qs_ref, kv_ref, kvt_ref, wuv_ref, olat_ref, ymla_ref, lse_ref):
        i = pl.program_id(0)
        q = qs_ref[0]

        def step(kt, carry, masked):
            m, l, acc = carry
            k = kv_ref[pl.ds(pl.multiple_of(kt * TK, TK), TK), :]
            v_t = kvt_ref[kt][0:KVL, :]
            s = _dot_nt(k, q) * SM_SCALE
            if masked:
                s = jnp.where(_diag_mask((TK, M), 1), s, -jnp.inf)
            m_new = jnp.maximum(m, jnp.max(s, axis=0, keepdims=True))
            alpha = jnp.exp(m - m_new)
            p = jnp.exp(s - m_new)
            l = alpha * l + jnp.sum(p, axis=0, keepdims=True)
            acc = alpha * acc + _dot(v_t, p.astype(BF))
            return m_new, l, acc

        init = (jnp.full((1, M), -jnp.inf, F32), jnp.zeros((1, M), F32), jnp.zeros((KVL, M), F32))
        carry = lax.fori_loop(0, i, lambda kt, c: step(kt, c, False), init)
        m, l, acc = step(i, carry, True)
        o_lat = acc / l
        olat_ref[0] = o_lat
        lse_ref[0] = jnp.broadcast_to(m + jnp.log(l), (8, M))
        for hd in range(HEADS):
            o_t = _dot(wuv_ref[hd], o_lat[:, hd * TQ:(hd + 1) * TQ].astype(BF))
            ymla_ref[:, hd * 128:(hd + 1) * 128] = o_t.T.astype(BF)

    out_shape = (
        jax.ShapeDtypeStruct((nq, KVL, M), F32),
        jax.ShapeDtypeStruct((S, HEADS * 128), BF),
        jax.ShapeDtypeStruct((nq, 8, M), F32),
    )
    return pl.pallas_call(
        body, name="attn_fwd", out_shape=out_shape, grid=(nq,),
        in_specs=[pl.BlockSpec((1, M, QW), lambda i: (i, 0, 0)), _full(kv.shape), _full(kvt.shape),
                  _full(wuv_vc.shape)],
        out_specs=(pl.BlockSpec((1, KVL, M), lambda i: (i, 0, 0)), _rows(TQ, HEADS * 128),
                   pl.BlockSpec((1, 8, M), lambda i: (i, 0, 0))),
        compiler_params=_params(("arbitrary",)),
    )(qs, kv, kvt, wuv_vc)


def _silu_parts(a):
    sg = jax.nn.sigmoid(a)
    return sg, a * sg


def _ffn_fwd(x, ymla, ypool, mod, w_o, g_ffn, wg_t, wu_t, wd, g_final, target):
    S = x.shape[0]
    ts = 512
    tf = 256
    nj = FF // tf

    def body(x_ref, ymla_ref, ypool_ref, mod_ref, wo_ref, gffn_ref, wg_ref, wu_ref, wd_ref, gfin_ref, t_ref,
             x2_ref, mix_ref, h2t_ref, a_ref, b_ref, dx3_ref, dff_ref, dfft_ref, loss_ref, dgfin_ref, dgt2_ref,
             acc_ref, h2_ref):
        i = pl.program_id(0)
        j = pl.program_id(1)

        @pl.when(jnp.logical_and(i == 0, j == 0))
        def _():
            loss_ref[...] = jnp.zeros_like(loss_ref)
            dgfin_ref[...] = jnp.zeros_like(dgfin_ref)
            dgt2_ref[...] = jnp.zeros_like(dgt2_ref)

        @pl.when(j == 0)
        def _():
            gt1 = mod_ref[0:1, 2 * D:3 * D]
            sh2 = mod_ref[0:1, 3 * D:4 * D]
            sc2 = mod_ref[0:1, 4 * D:5 * D]
            cat = jnp.concatenate([ymla_ref[...], ypool_ref[...]], axis=1)
            mix = _dot(cat, wo_ref[...])
            mix_ref[...] = mix
            x2 = x_ref[...] + gt1 * mix
            x2_ref[...] = x2
            h2 = (x2 * _rms(x2)) * gffn_ref[...] * (1.0 + sc2) + sh2
            h2_ref[...] = h2.astype(BF)
            h2t_ref[...] = h2.T.astype(BF)
            acc_ref[...] = jnp.zeros_like(acc_ref)

        h2b = h2_ref[...]
        a = _dot_nt(h2b, wg_ref[...])
        b = _dot_nt(h2b, wu_ref[...])
        a_ref[...] = a.astype(BF)
        b_ref[...] = b.astype(BF)
        f = _silu_parts(a)[1] * b
        acc_ref[...] += _dot(f.astype(BF), wd_ref[...])

        @pl.when(j == nj - 1)
        def _():
            gt2 = mod_ref[0:1, 5 * D:6 * D]
            ff = acc_ref[...]
            x3 = x2_ref[...] + gt2 * ff
            r3 = _rms(x3)
            xn3 = x3 * r3
            gfin = gfin_ref[...]
            e = xn3 * gfin - t_ref[...]
            loss_ref[...] += 0.5 * jnp.sum(jnp.mean(e * e, axis=-1, keepdims=True))
            dy = e * (1.0 / D)
            dgfin_ref[...] += _colsum(dy * xn3)
            dx3 = _rms_bwd(dy * gfin, xn3, r3)
            dx3_ref[...] = dx3
            dgt2_ref[...] += _colsum(dx3 * ff)
            dff = dx3 * gt2
            dff_ref[...] = dff.astype(BF)
            dfft_ref[...] = dff.T.astype(BF)

    row = lambda c: pl.BlockSpec((ts, c), lambda i, j: (i, 0))
    col = pl.BlockSpec((D, ts), lambda i, j: (0, i))
    wblk = pl.BlockSpec((tf, D), lambda i, j: (j, 0))
    act = pl.BlockSpec((ts, tf), lambda i, j: (i, j))
    const = lambda shape: pl.BlockSpec(shape, lambda i, j: (0,) * len(shape))
    out_shape = (
        jax.ShapeDtypeStruct((S, D), F32),
        jax.ShapeDtypeStruct((S, D), F32),
        jax.ShapeDtypeStruct((D, S), BF),
        jax.ShapeDtypeStruct((S, FF), BF),
        jax.ShapeDtypeStruct((S, FF), BF),
        jax.ShapeDtypeStruct((S, D), F32),
        jax.ShapeDtypeStruct((S, D), BF),
        jax.ShapeDtypeStruct((D, S), BF),
        jax.ShapeDtypeStruct((8, 128), F32),
        jax.ShapeDtypeStruct((1, D), F32),
        jax.ShapeDtypeStruct((1, D), F32),
    )
    return pl.pallas_call(
        body, name="ffn_fwd", out_shape=out_shape, grid=(S // ts, nj),
        in_specs=[row(D), row(PW), row(PW), const(mod.shape), const(w_o.shape), const((1, D)), wblk, wblk, wblk,
                  const((1, D)), row(D)],
        out_specs=(row(D), row(D), col, act, act, row(D), row(D), col, const((8, 128)), const((1, D)),
                   const((1, D))),
        scratch_shapes=[pltpu.VMEM((ts, D), F32), pltpu.VMEM((ts, D), BF)],
        compiler_params=_params(("arbitrary", "arbitrary")),
    )(x, ymla, ypool, mod, w_o, g_ffn, wg_t, wu_t, wd, g_final, target)


FCHUNK = 256


def _ffn_bwd_acts(dff, a, b, wg_t, wu_t, wd):
    S = dff.shape[0]
    ts = 512

    def body(dff_ref, a_ref, b_ref, wg_ref, wu_ref, wd_ref, da_ref, db_ref, dh2_ref):
        dffb = dff_ref[...]
        for c in range(FF // FCHUNK):
            cols = slice(c * FCHUNK, (c + 1) * FCHUNK)
            df = _dot_nt(dffb, wd_ref[cols, :])
            av = a_ref[:, cols].astype(F32)
            bv = b_ref[:, cols].astype(F32)
            sg, sa = _silu_parts(av)
            db_ref[:, cols] = (df * sa).astype(BF)
            da_ref[:, cols] = (df * bv * (sg * (1.0 + av * (1.0 - sg)))).astype(BF)
        dh2_ref[...] = _dot(da_ref[...], wg_ref[...]) + _dot(db_ref[...], wu_ref[...])

    act = _rows(ts, FF)
    return pl.pallas_call(
        body, name="ffn_bwd_acts",
        out_shape=(jax.ShapeDtypeStruct((S, FF), BF), jax.ShapeDtypeStruct((S, FF), BF),
                   jax.ShapeDtypeStruct((S, D), F32)),
        grid=(S // ts,), in_specs=[_rows(ts, D), act, act, _vmem(), _vmem(), _vmem()],
        out_specs=(act, act, _rows(ts, D)), compiler_params=_params(("arbitrary",)),
    )(dff, a, b, wg_t, wu_t, wd)


def _ffn_bwd_weights(dff_t, h2_t, da, db, a, b):
    S = da.shape[0]

    def body(dfft_ref, h2t_ref, da_ref, db_ref, a_ref, b_ref, dwg_ref, dwu_ref, dwd_ref):
        h2t = h2t_ref[...]
        dwg_ref[...] = _dot(h2t, da_ref[...]).T.astype(BF)
        dwu_ref[...] = _dot(h2t, db_ref[...]).T.astype(BF)
        f = (_silu_parts(a_ref[...].astype(F32))[1] * b_ref[...].astype(F32)).astype(BF)
        dwd_ref[...] = _dot(dfft_ref[...], f).T.astype(BF)

    act = pl.BlockSpec((S, FCHUNK), lambda j: (0, j))
    wblk = _rows(FCHUNK, D)
    shp = jax.ShapeDtypeStruct((FF, D), BF)
    return pl.pallas_call(
        body, name="ffn_bwd_weights", out_shape=(shp, shp, shp), grid=(FF // FCHUNK,),
        in_specs=[_vmem(), _vmem(), act, act, act, act], out_specs=(wblk, wblk, wblk),
        compiler_params=_params(("arbitrary",)),
    )(dff_t, h2_t, da, db, a, b)


def _mix_bwd(dh2, dx3, x2, mix, mod, g_ffn, ymla, ypool, w_o, ypre, pooled, pool_scale, wpool_dc, olat, wuv_vc):
    S = dh2.shape[0]
    ts = 512
    n = S // ts
    nsub = ts // TQ
    M = HEADS * TQ

    def body(dh2_ref, dx3_ref, x2_ref, mix_ref, mod_ref, gffn_ref, ymla_ref, ypool_ref, wo_ref, ypre_ref, pooled_ref,
             pscale_ref, wpool_ref, olat_ref, wuv_ref,
             dx2_ref, du_ref, dolat_ref, delta_ref, dwo_ref, dwuv_ref, dwpool_ref, dpscale_ref, dgt1_ref, dsc2_ref,
             dsh2_ref, dgffn_ref, carry_ref, dwo_acc):
        i = pl.program_id(0)

        @pl.when(i == 0)
        def _():
            carry_ref[...] = jnp.zeros_like(carry_ref)
            dwo_acc[...] = jnp.zeros_like(dwo_acc)
            for r in (dwuv_ref, dwpool_ref, dpscale_ref, dgt1_ref, dsc2_ref, dsh2_ref, dgffn_ref):
                r[...] = jnp.zeros_like(r)

        gt1 = mod_ref[0:1, 2 * D:3 * D]
        sc2 = mod_ref[0:1, 4 * D:5 * D]
        gffn = gffn_ref[...]
        dh2 = dh2_ref[...]
        x2 = x2_ref[...]
        r2 = _rms(x2)
        xn2 = x2 * r2
        dsc2_ref[...] += _colsum(dh2 * (xn2 * gffn))
        dsh2_ref[...] += _colsum(dh2)
        dgffn_ref[...] += _colsum(dh2 * (1.0 + sc2) * xn2)
        dx2 = dx3_ref[...] + _rms_bwd(dh2 * gffn * (1.0 + sc2), xn2, r2)
        dx2_ref[...] = dx2
        dgt1_ref[...] += _colsum(dx2 * mix_ref[...])
        dmix = (dx2 * gt1).astype(BF)
        cat = jnp.concatenate([ymla_ref[...], ypool_ref[...]], axis=1)
        dwo_acc[...] += _dot_tn(cat, dmix)
        dcat = _dot_nt(dmix, wo_ref[...])
        dymla = dcat[:, 0:512]
        dypool = dcat[:, 512:1024]

        dpscale_ref[...] += _colsum(dypool * ypre_ref[...])
        dypre = (dypool * pscale_ref[...]).astype(BF)
        pooled = pooled_ref[...]
        dpooled = []
        for g in range(GROUPS):
            sl = slice(g * GD, (g + 1) * GD)
            dwpool_ref[g] += _dot_tn(pooled[:, sl], dypre[:, sl])
            dpooled.append(_dot(dypre[:, sl], wpool_ref[g]))
        dpooled = jnp.concatenate(dpooled, axis=1)
        tile = n - 1 - i
        e = dpooled / _row_counts(tile * ts, ts)
        ext = jnp.concatenate([e, carry_ref[...]], axis=0)
        du_ref[...] = _window_sums(ext, False)[0:ts, :] - dpooled
        carry_ref[...] = e[0:16, :]

        for hd in range(HEADS):
            do = dymla[:, hd * 128:(hd + 1) * 128]
            dob = do.astype(BF)
            dol = _dot(dob, wuv_ref[hd])
            for a in range(nsub):
                ol_t = olat_ref[a, :, hd * TQ:(hd + 1) * TQ]
                dl = dol[a * TQ:(a + 1) * TQ, :]
                dolat_ref[a, hd * TQ:(hd + 1) * TQ, :] = dl.astype(BF)
                dwuv_ref[hd] += _dot(ol_t.astype(BF), dob[a * TQ:(a + 1) * TQ, :])
                delta = jnp.sum(dl * ol_t.T, axis=-1, keepdims=True)
                delta_ref[a, :, hd * TQ:(hd + 1) * TQ] = jnp.broadcast_to(delta, (TQ, 128)).T[0:8, :]

        @pl.when(i == n - 1)
        def _():
            dwo_ref[...] = dwo_acc[...].astype(BF)

    rev = lambda c: pl.BlockSpec((ts, c), lambda i: (n - 1 - i, 0))
    rev3 = lambda r, c: pl.BlockSpec((nsub, r, c), lambda i: (n - 1 - i, 0, 0))
    out_shape = (
        jax.ShapeDtypeStruct((S, D), F32),
        jax.ShapeDtypeStruct((S, PW), F32),
        jax.ShapeDtypeStruct((S // TQ, M, KVL), BF),
        jax.ShapeDtypeStruct((S // TQ, 8, M), F32),
        jax.ShapeDtypeStruct((D, D), BF),
        jax.ShapeDtypeStruct((HEADS, KVL, 128), F32),
        jax.ShapeDtypeStruct((GROUPS, GD, GD), F32),
        jax.ShapeDtypeStruct((1, PW), F32),
        jax.ShapeDtypeStruct((1, D), F32), jax.ShapeDtypeStruct((1, D), F32), jax.ShapeDtypeStruct((1, D), F32),
        jax.ShapeDtypeStruct((1, D), F32),
    )
    in_specs = [rev(D), rev(D), rev(D), rev(D), _full(mod.shape), _full((1, D)), rev(PW), rev(PW), _full(w_o.shape),
                rev(PW), rev(PW), _full((1, PW)), _full(wpool_dc.shape), rev3(KVL, M), _full(wuv_vc.shape)]
    out_specs = (rev(D), rev(PW), rev3(M, KVL), rev3(8, M), _full((D, D)), _full((HEADS, KVL, 128)),
                 _full((GROUPS, GD, GD)), _full((1, PW)), _full((1, D)), _full((1, D)), _full((1, D)), _full((1, D)))
    return pl.pallas_call(
        body, name="mix_bwd", out_shape=out_shape, grid=(n,), in_specs=in_specs, out_specs=out_specs,
        scratch_shapes=[pltpu.VMEM((16, PW), F32), pltpu.VMEM((D, D), F32)],
        compiler_params=_params(("arbitrary",)),
    )(dh2, dx3, x2, mix, mod, g_ffn, ymla, ypool, w_o, ypre, pooled, pool_scale, wpool_dc, olat, wuv_vc)


def _attn_bwd(qs, kv, dolat, lse, delta):
    nq = qs.shape[0]
    S = kv.shape[0]
    M = HEADS * TQ
    nk = S // TK

    def body(qs_ref, kv_ref, do_ref, lse_ref, delta_ref, dkv_ref, dqt_ref):
        kt = pl.program_id(0)
        k = kv_ref[...]
        v = k[:, 0:KVL]
        k_t = k.astype(F32).T.astype(BF)

        @pl.when(kt == 0)
        def _():
            dqt_ref[...] = jnp.zeros_like(dqt_ref)

        def step(qi, carry, masked):
            dk, dv = carry
            q = qs_ref[qi]
            do = do_ref[qi]
            s = _dot_nt(k, q) * SM_SCALE
            p = jnp.exp(s - lse_ref[qi, 0:1, :])
            if masked:
                p = jnp.where(_diag_mask((TK, M), 1), p, 0.0)
            dp = _dot_nt(v, do)
            ds = (p * (dp - delta_ref[qi, 0:1, :]) * SM_SCALE).astype(BF)
            dv = dv + _dot(p.astype(BF), do)
            dk = dk + _dot(ds, q)
            dqt_ref[qi] += _dot(k_t, ds)
            return dk, dv

        carry = step(kt, (jnp.zeros((TK, QW), F32), jnp.zeros((TK, KVL), F32)), True)
        dk, dv = lax.fori_loop(kt + 1, nq, lambda qi, c: step(qi, c, False), carry)
        dkv_ref[...] = dk + jnp.concatenate([dv, jnp.zeros((TK, QW - KVL), F32)], axis=1)

    out_shape = (jax.ShapeDtypeStruct((S, QW), F32), jax.ShapeDtypeStruct((nq, QW, M), F32))
    return pl.pallas_call(
        body, name="attn_bwd", out_shape=out_shape, grid=(nk,),
        in_specs=[_vmem(), _rows(TK, QW), _vmem(), _vmem(), _vmem()],
        out_specs=(_rows(TK, QW), _vmem()),
        compiler_params=_params(("arbitrary",)),
    )(qs, kv, dolat, lse, delta)


def _in_bwd(dqt, dkv, du, raw, qn, h1, x, dx2, mod, g_mix, w_in, g_q, g_kv, w_uq, wuk_cd, perm_t, cos4, sin4, csk,
            snk):
    S = x.shape[0]
    ts = 512
    n = S // ts
    nsub = ts // TQ
    M = HEADS * TQ

    def body(dqt_ref, dkv_ref, du_ref, raw_ref, qn_ref, h1_ref, x_ref, dx2_ref, mod_ref, gmix_ref, win_ref, gq_ref,
             gkv_ref, wuq_ref, wuk_ref, permt_ref, cos_ref, sin_ref, csk_ref, snk_ref,
             dx_ref, dwin_ref, dwuq_ref, dwuk_ref, dgq_ref, dgkv_ref, dsc1_ref, dsh1_ref, dgmix_ref, dwin_acc,
             dwuq_acc):
        i = pl.program_id(0)

        @pl.when(i == 0)
        def _():
            dwin_acc[...] = jnp.zeros_like(dwin_acc)
            dwuq_acc[...] = jnp.zeros_like(dwuq_acc)
            for r in (dwuk_ref, dgq_ref, dgkv_ref, dsc1_ref, dsh1_ref, dgmix_ref):
                r[...] = jnp.zeros_like(r)

        dq_blocks = [dqt_ref[a].T for a in range(nsub)]
        qn = qn_ref[...]
        dq_parts = []
        drope = jnp.zeros((ts, 2 * 128), F32)
        for hd in range(HEADS):
            dqh = jnp.concatenate([blk[hd * TQ:(hd + 1) * TQ, :] for blk in dq_blocks], axis=0)
            dq_lat = dqh[:, 0:KVL].astype(BF)
            dq_parts.append(_dot(dq_lat, wuk_ref[hd]))
            dwuk_ref[hd] += _dot_tn(dq_lat, qn[:, hd * NOPE:(hd + 1) * NOPE])
            drope = drope + _dot(dqh[:, KVL:QW].astype(BF), permt_ref[hd])
        do1 = drope[:, 0:128]
        do2 = drope[:, 128:256]
        cosv = cos_ref[...]
        sinv = sin_ref[...]
        dq_parts.append(do1 * cosv + do2 * sinv)
        dq_parts.append(do2 * cosv - do1 * sinv)
        dq = jnp.concatenate(dq_parts, axis=1).astype(BF)

        cq_raw = raw_ref[:, 0:QL]
        ckv_raw = raw_ref[:, QL:QL + KVL]
        rq = _rms(cq_raw)
        nq_ = cq_raw * rq
        gq = gq_ref[...]
        dwuq_acc[...] += _dot_tn((nq_ * gq).astype(BF), dq)
        dc_q = _dot_nt(dq, wuq_ref[...])
        dgq_ref[...] += _colsum(dc_q * nq_)
        dcq_raw = _rms_bwd(dc_q * gq, nq_, rq)

        dkv = dkv_ref[...]
        rk = _rms(ckv_raw)
        nk_ = ckv_raw * rk
        dc_kv = dkv[:, 0:KVL]
        dgkv_ref[...] += _colsum(dc_kv * nk_)
        dckv_raw = _rms_bwd(dc_kv * gkv_ref[...], nk_, rk)
        dkr_roped = dkv[:, KVL:QW]
        dkr = dkr_roped * csk_ref[...] - _swap_halves(dkr_roped) * snk_ref[...]

        dproj = jnp.concatenate([dcq_raw, dckv_raw, dkr, du_ref[...]], axis=1).astype(BF)
        dwin_acc[...] += _dot_tn(h1_ref[...], dproj)
        dh1 = _dot_nt(dproj, win_ref[...])

        sc1 = mod_ref[0:1, D:2 * D]
        gmix = gmix_ref[...]
        xv = x_ref[...]
        r1 = _rms(xv)
        xn1 = xv * r1
        dsc1_ref[...] += _colsum(dh1 * (xn1 * gmix))
        dsh1_ref[...] += _colsum(dh1)
        dgmix_ref[...] += _colsum(dh1 * (1.0 + sc1) * xn1)
        dx_ref[...] = dx2_ref[...] + _rms_bwd(dh1 * gmix * (1.0 + sc1), xn1, r1)

        @pl.when(i == n - 1)
        def _():
            dwin_ref[...] = dwin_acc[...].astype(BF)
            dwuq_ref[...] = dwuq_acc[...].astype(BF)

    out_shape = (
        jax.ShapeDtypeStruct((S, D), F32),
        jax.ShapeDtypeStruct((D, D), BF),
        jax.ShapeDtypeStruct((QL, 768), BF),
        jax.ShapeDtypeStruct((HEADS, KVL, NOPE), F32),
        jax.ShapeDtypeStruct((1, QL), F32), jax.ShapeDtypeStruct((1, KVL), F32),
        jax.ShapeDtypeStruct((1, D), F32), jax.ShapeDtypeStruct((1, D), F32), jax.ShapeDtypeStruct((1, D), F32),
    )
    in_specs = [pl.BlockSpec((nsub, QW, M), lambda i: (i, 0, 0)), _rows(ts, QW), _rows(ts, PW), _rows(ts, 384),
                _rows(ts, HEADS * NOPE), _rows(ts, D), _rows(ts, D), _rows(ts, D), _full(mod.shape), _full((1, D)),
                _full(w_in.shape), _full((1, QL)), _full((1, KVL)), _full(w_uq.shape), _full(wuk_cd.shape),
                _full(perm_t.shape), _rows(ts, 128), _rows(ts, 128), _rows(ts, 128), _rows(ts, 128)]
    out_specs = (_rows(ts, D), _full((D, D)), _full((QL, 768)), _full((HEADS, KVL, NOPE)), _full((1, QL)),
                 _full((1, KVL)), _full((1, D)), _full((1, D)), _full((1, D)))
    return pl.pallas_call(
        body, name="in_bwd", out_shape=out_shape, grid=(n,), in_specs=in_specs, out_specs=out_specs,
        scratch_shapes=[pltpu.VMEM((D, D), F32), pltpu.VMEM((QL, 768), F32)],
        compiler_params=_params(("arbitrary",)),
    )(dqt, dkv, du, raw, qn, h1, x, dx2, mod, g_mix, w_in, g_q, g_kv, w_uq, wuk_cd, perm_t, cos4, sin4, csk, snk)


def _rope_perm():
    p = np.zeros((HEADS, 2 * 128, 128), np.float32)
    for hd in range(HEADS):
        for t in range(HALF):
            p[hd, hd * HALF + t, t] = 1.0
            p[hd, 128 + hd * HALF + t, HALF + t] = 1.0
    return p


def _rope_tables(positions):
    freqs = jnp.power(ROPE_THETA, -jnp.arange(HALF, dtype=F32) / HALF)
    ang = positions.astype(F32)[:, None] * jnp.tile(freqs, HEADS)[None, :]
    cos4 = jnp.cos(ang)
    sin4 = jnp.sin(ang)
    lane = jnp.arange(HEADS * HALF)[None, :]
    csk = jnp.where(lane < ROPE, cos4, 0.0)
    snk = jnp.where(lane < HALF, -sin4, jnp.where(lane < ROPE, sin4, 0.0))
    return cos4, sin4, csk, snk


def _local_step(x, positions, target, mod, g_mix, w_in_p, g_q, g_kv, w_uq_p, w_uk, w_uv, w_pool, pool_scale, g_ffn,
                g_final, late, ffn_grads_exchange):
    perm = jnp.asarray(_rope_perm(), BF)
    perm_t = jnp.asarray(_rope_perm().transpose(0, 2, 1), BF)
    cos4, sin4, csk, snk = _rope_tables(positions)
    wuk_dc = w_uk.transpose(1, 2, 0).astype(BF)
    wuk_cd = w_uk.transpose(1, 0, 2).astype(BF)
    wuv_vc = w_uv.transpose(1, 2, 0).astype(BF)
    wpool = w_pool.astype(BF)
    wpool_dc = w_pool.transpose(0, 2, 1).astype(BF)

    h1, raw, qn, qs, kv, kvt, pooled, ypre, ypool = _fwd_in(
        x, mod, g_mix, w_in_p, g_q, g_kv, w_uq_p, wuk_dc, perm, cos4, sin4, csk, snk, wpool, pool_scale)
    olat, ymla, lse = _attn_fwd(qs, kv, kvt, wuv_vc)
    w_o, wg_t, wu_t, wd = late
    x2, mix, h2_t, a, b, dx3, dff, dff_t, loss, dgfin, dgt2 = _ffn_fwd(
        x, ymla, ypool, mod, w_o, g_ffn, wg_t, wu_t, wd, g_final, target)
    da, db, dh2 = _ffn_bwd_acts(dff, a, b, wg_t, wu_t, wd)
    dwg_t, dwu_t, dwd = _ffn_bwd_weights(dff_t, h2_t, da, db, a, b)
    ffn_parts = ffn_grads_exchange((dwg_t, dwu_t, dwd))
    (dx2, du, dolat, delta, dwo, dwuv, dwpool, dpscale, dgt1, dsc2, dsh2, dgffn) = _mix_bwd(
        dh2, dx3, x2, mix, mod, g_ffn, ymla, ypool, w_o, ypre, pooled, pool_scale, wpool_dc, olat, wuv_vc)
    dkv, dqt = _attn_bwd(qs, kv, dolat, lse, delta)
    dx, dwin, dwuq, dwuk, dgq, dgkv, dsc1, dsh1, dgmix = _in_bwd(
        dqt, dkv, du, raw, qn, h1, x, dx2, mod, g_mix, w_in_p, g_q, g_kv, w_uq_p, wuk_cd, perm_t, cos4, sin4, csk,
        snk)
    dmod = jnp.concatenate([dsh1, dsc1, dgt1, dsh2, dsc2, dgt2], axis=1)
    replicated = dict(
        w_uk=dwuk.transpose(1, 0, 2), w_uv=dwuv.transpose(1, 0, 2), w_pool=dwpool, g_mix=dgmix, g_q=dgq, g_kv=dgkv,
        pool_scale=dpscale, g_ffn=dgffn, g_final=dgfin)
    return loss[0, 0], dx, dmod, (dwin, dwuq, dwo), ffn_parts, replicated


def _my_pos():
    return lax.axis_index("x"), lax.axis_index("y"), lax.axis_index("c")


def _peer(pos, k):
    x, y, c = pos
    return (1 - x if k & 4 else x, 1 - y if k & 2 else y, 1 - c if k & 1 else c)


def _index(pos):
    x, y, c = pos
    return 4 * x + 2 * y + c


def _remote(src, dst, send_sem, recv_sem, to):
    return pltpu.make_async_remote_copy(src_ref=src, dst_ref=dst, send_sem=send_sem, recv_sem=recv_sem,
                                        device_id=to, device_id_type=MESH)


def _ada_mod(c, w_ada, b_ada):
    def body(c_ref, w_ref, b_ref, mod_ref, call_ref, cbuf, sbuf, rbuf, send1, recv1, send2, recv2):
        me = _my_pos()
        mi = _index(me)
        cv = c_ref[...]
        cbuf[...] = jnp.broadcast_to(cv * jax.nn.sigmoid(cv), (8, D))
        call_ref[mi] = cbuf[...]
        first = [_remote(cbuf, call_ref.at[mi], send1.at[k - 1], recv1.at[k - 1], _peer(me, k)) for k in range(1, NDEV)]
        for cp in first:
            cp.start()
        for k in range(1, NDEV):
            _remote(cbuf, call_ref.at[_index(_peer(me, k))], send1.at[k - 1], recv1.at[k - 1], _peer(me, k)).wait_recv()
        c_all = jnp.concatenate([call_ref[b][0:1, :] for b in range(NDEV)], axis=0)
        blocks = _dot(c_all.astype(BF), w_ref[...].astype(BF))
        for b in range(NDEV):
            sbuf[b] = jnp.broadcast_to(blocks[b:b + 1, :], (8, MODC))
        second = []
        for k in range(1, NDEV):
            to = _peer(me, k)
            second.append(_remote(sbuf.at[_index(to)], rbuf.at[mi], send2.at[k - 1], recv2.at[k - 1], to))
        for cp in second:
            cp.start()
        rbuf[mi] = sbuf[mi]
        for k in range(1, NDEV):
            to = _peer(me, k)
            _remote(sbuf.at[_index(to)], rbuf.at[_index(to)], send2.at[k - 1], recv2.at[k - 1], to).wait_recv()
        for j in range(NDEV):
            mod_ref[:, j * MODC:(j + 1) * MODC] = rbuf[j] + b_ref[:, j * MODC:(j + 1) * MODC]
        for cp in first + second:
            cp.wait_send()

    return pl.pallas_call(
        body, name="ada_mod",
        out_shape=(jax.ShapeDtypeStruct((8, N_MOD * D), F32), jax.ShapeDtypeStruct((NDEV, 8, D), F32)),
        in_specs=[_vmem(), _vmem(), _vmem()], out_specs=(_vmem(), _vmem()),
        scratch_shapes=[pltpu.VMEM((8, D), F32), pltpu.VMEM((NDEV, 8, MODC), F32), pltpu.VMEM((NDEV, 8, MODC), F32),
                        pltpu.SemaphoreType.DMA((NDEV - 1,)), pltpu.SemaphoreType.DMA((NDEV - 1,)),
                        pltpu.SemaphoreType.DMA((NDEV - 1,)), pltpu.SemaphoreType.DMA((NDEV - 1,))],
        compiler_params=_params(),
    )(c, w_ada, b_ada)


def _exchange_slices(scatter):
    def of(src, to_index):
        if not scatter:
            return src
        r = src.shape[0] // NDEV
        return src.at[pl.ds(pl.multiple_of(to_index * r, 16), r), :]
    return of


def _sequencer_exchange(name, collective_id, srcs, scatter, after=()):
    n = len(srcs)
    of = _exchange_slices(scatter)

    def body(*refs):
        src, zone = refs[:n], refs[n + len(after):2 * n + len(after)]
        send, recv, local = refs[2 * n + len(after):]
        me = _my_pos()
        mi = _index(me)
        barrier = pltpu.get_barrier_semaphore()
        for k in range(1, NDEV):
            pl.semaphore_signal(barrier, inc=1, device_id=_peer(me, k), device_id_type=MESH)
        pl.semaphore_wait(barrier, NDEV - 1)
        own = [pltpu.make_async_copy(of(src[a], mi), zone[a].at[mi], local.at[a]) for a in range(n)]
        for cp in own:
            cp.start()
        for a in range(n):
            for k in range(1, NDEV):
                to = _peer(me, k)
                s = a * (NDEV - 1) + k - 1
                _remote(of(src[a], _index(to)), zone[a].at[mi], send.at[s], recv.at[s], to).start()
        for cp in own:
            cp.wait()
        for a in range(n):
            for k in range(1, NDEV):
                to = _peer(me, k)
                s = a * (NDEV - 1) + k - 1
                cp = _remote(of(src[a], mi), zone[a].at[_index(to)], send.at[s], recv.at[s], to)
                cp.wait_send()
                cp.wait_recv()

    return pl.kernel(
        body, name=name, mesh=plsc.ScalarSubcoreMesh(axis_name="sequencer", num_cores=1),
        out_type=tuple(jax.ShapeDtypeStruct((NDEV, s.shape[0] // NDEV if scatter else s.shape[0], s.shape[1]), s.dtype)
                       for s in srcs),
        scratch_types=[pltpu.SemaphoreType.DMA((n * (NDEV - 1),)), pltpu.SemaphoreType.DMA((n * (NDEV - 1),)),
                       pltpu.SemaphoreType.DMA((n,))],
        compiler_params=pltpu.CompilerParams(collective_id=collective_id),
    )(*srcs, *after)


def _sum_partials(name, parts):
    n = len(parts)

    def body(*refs):
        for a in range(n):
            acc = refs[a][0].astype(F32)
            for p in range(1, NDEV):
                acc = acc + refs[a][p].astype(F32)
            refs[n + a][...] = acc

    return pl.pallas_call(
        body, name=name,
        out_shape=tuple(jax.ShapeDtypeStruct(p.shape[1:], F32) for p in parts),
        in_specs=[_vmem()] * n, out_specs=tuple([_vmem()] * n), compiler_params=_params(),
    )(*parts)


def _small_all_reduce(buf):
    def body(buf_ref, got_ref, red_ref, mine, send1, recv1, send2, recv2):
        me = _my_pos()
        mi = _index(me)
        first = []
        for k in range(1, NDEV):
            to = _peer(me, k)
            first.append(_remote(buf_ref.at[_index(to)], got_ref.at[mi], send1.at[k - 1], recv1.at[k - 1], to))
        for cp in first:
            cp.start()
        got_ref[mi] = buf_ref[mi]
        for k in range(1, NDEV):
            to = _peer(me, k)
            _remote(buf_ref.at[mi], got_ref.at[_index(to)], send1.at[k - 1], recv1.at[k - 1], to).wait_recv()
        acc = got_ref[0]
        for p in range(1, NDEV):
            acc = acc + got_ref[p]
        mine[...] = acc
        second = [_remote(mine, red_ref.at[mi], send2.at[k - 1], recv2.at[k - 1], _peer(me, k)) for k in range(1, NDEV)]
        for cp in second:
            cp.start()
        red_ref[mi] = acc
        for k in range(1, NDEV):
            to = _peer(me, k)
            _remote(mine, red_ref.at[_index(to)], send2.at[k - 1], recv2.at[k - 1], to).wait_recv()
        for cp in first + second:
            cp.wait_send()

    return pl.pallas_call(
        body, name="small_all_reduce",
        out_shape=(jax.ShapeDtypeStruct(buf.shape, F32), jax.ShapeDtypeStruct(buf.shape, F32)),
        in_specs=[_vmem()], out_specs=(_vmem(), _vmem()),
        scratch_shapes=[pltpu.VMEM(buf.shape[1:], F32),
                        pltpu.SemaphoreType.DMA((NDEV - 1,)), pltpu.SemaphoreType.DMA((NDEV - 1,)),
                        pltpu.SemaphoreType.DMA((NDEV - 1,)), pltpu.SemaphoreType.DMA((NDEV - 1,))],
        compiler_params=_params(),
    )(buf)


def _adamw_math(w, g, m, v):
    m = ADAM_B1 * m + (1.0 - ADAM_B1) * g
    v = ADAM_B2 * v + (1.0 - ADAM_B2) * jnp.square(g)
    m_hat = m / (1.0 - ADAM_B1 ** ADAM_STEP)
    v_hat = v / (1.0 - ADAM_B2 ** ADAM_STEP)
    delta = -ADAM_LR * (m_hat / (jnp.sqrt(v_hat) + ADAM_EPS) + ADAM_WD * w)
    return delta, m, v


def _adamw_group(name, ws, gs, ms, vs):
    n = len(ws)

    def body(*refs):
        for a in range(n):
            w, g, m, v = (refs[q * n + a][...] for q in range(4))
            delta, m2, v2 = _adamw_math(w, g, m, v)
            refs[4 * n + a][...] = delta
            refs[5 * n + a][...] = m2
            refs[6 * n + a][...] = v2

    shapes = tuple(jax.ShapeDtypeStruct(w.shape, F32) for w in ws)
    outs = pl.pallas_call(
        body, name=name, out_shape=shapes * 3, in_specs=[_vmem()] * (4 * n), out_specs=tuple([_vmem()] * (3 * n)),
        compiler_params=_params(),
    )(*ws, *gs, *ms, *vs)
    return outs[:n], outs[n:2 * n], outs[2 * n:]


def _adamw_ada(w, m, v, c_all, dmod_rows):
    def body(w_ref, m_ref, v_ref, c_ref, dm_ref, g_ref, d_ref, m2_ref, v2_ref):
        g = _dot_tn(c_ref[...], dm_ref[...].astype(BF))
        g_ref[...] = g
        delta, m2, v2 = _adamw_math(w_ref[...], g, m_ref[...], v_ref[...])
        d_ref[...] = delta
        m2_ref[...] = m2
        v2_ref[...] = v2

    shp = jax.ShapeDtypeStruct(w.shape, F32)
    return pl.pallas_call(
        body, name="adamw_ada", out_shape=(shp, shp, shp, shp), in_specs=[_vmem()] * 5,
        out_specs=tuple([_vmem()] * 4), compiler_params=_params(),
    )(w, m, v, c_all, dmod_rows)


def _w_in_to_kernel(w):
    return jnp.concatenate([w[:, 0:448], jnp.zeros((w.shape[0], 64), w.dtype), w[:, 448:960]], axis=1)


def _w_in_from_kernel(w):
    return jnp.concatenate([w[:, 0:448], w[:, 512:1024]], axis=1)


def _w_uq_to_kernel(w):
    r = w.shape[0]
    return jnp.concatenate([w[:, :, 0:NOPE].reshape(r, HEADS * NOPE),
                            w[:, :, NOPE:NOPE + HALF].reshape(r, HEADS * HALF),
                            w[:, :, NOPE + HALF:].reshape(r, HEADS * HALF)], axis=1)


def _w_uq_from_kernel(w):
    r = w.shape[0]
    return jnp.concatenate([w[:, 0:512].reshape(r, HEADS, NOPE), w[:, 512:640].reshape(r, HEADS, HALF),
                            w[:, 640:768].reshape(r, HEADS, HALF)], axis=2)


REP_NAMES = ("w_uk", "w_uv", "w_pool", "g_mix", "g_q", "g_kv", "pool_scale", "g_ffn", "g_final")


def kernel(x, c, positions, w_ada, b_ada, g_mix, w_in, g_q, g_kv, w_uq, w_uk, w_uv, w_pool, pool_scale, w_o, g_ffn, w_gate, w_up, w_down, g_final, loss_target, m_w_ada, m_b_ada, m_g_mix, m_w_in, m_g_q, m_g_kv, m_w_uq, m_w_uk, m_w_uv, m_w_pool, m_pool_scale, m_w_o, m_g_ffn, m_w_gate, m_w_up, m_w_down, m_g_final, v_w_ada, v_b_ada, v_g_mix, v_w_in, v_g_q, v_g_kv, v_w_uq, v_w_uk, v_w_uv, v_w_pool, v_pool_scale, v_w_o, v_g_ffn, v_w_gate, v_w_up, v_w_down, v_g_final):
    given = dict(locals())

    merge = lambda g: g.reshape(NDEV * g.shape[1], g.shape[2])
    w_in_p, w_uq_p = (merge(g) for g in _sequencer_exchange(
        "gather_in", 3, (_w_in_to_kernel(w_in[0]).astype(BF), _w_uq_to_kernel(w_uq[0]).astype(BF)), False))

    mod, c_all8 = _ada_mod(c, w_ada[0], b_ada)
    c_all = c_all8[:, 0, :]
    late = _sequencer_exchange(
        "gather_late", 1, (w_o[0].astype(BF), w_gate[0].T.astype(BF), w_up[0].T.astype(BF), w_down[0].astype(BF)),
        False, after=(mod[:, 0:128], w_in_p[0:16, 0:128], w_uq_p[0:16, 0:128]))

    def ffn_grads_exchange(arrays):
        return _sequencer_exchange("scatter_ffn", 2, arrays, True)

    loss, dx, dmod, tail_grads, ffn_parts, replicated = _local_step(
        x[0], positions[0], loss_target[0], mod, g_mix, w_in_p, g_q, g_kv, w_uq_p, w_uk[0], w_uv[0], w_pool[0],
        pool_scale, g_ffn, g_final.reshape(1, D), tuple(merge(g) for g in late), ffn_grads_exchange)

    flat = jnp.concatenate([replicated[k].reshape(-1) for k in REP_NAMES] + [loss.reshape(1)])
    flat = jnp.pad(flat, (0, NDEV * REP_ROWS * 128 - flat.shape[0])).reshape(NDEV, REP_ROWS, 128)
    dmod_blocks = jnp.pad(dmod.reshape(NDEV, MODC // 128, 128), ((0, 0), (0, MOD_ROWS - MODC // 128), (0, 0)))
    got, red = _small_all_reduce(jnp.concatenate([dmod_blocks, flat], axis=1))

    tail_parts = _sequencer_exchange("scatter_tail", 4, tail_grads, True,
                                     after=(ffn_parts[0][0, 0:16, 0:128], red[0, 0:8, :]))
    g_gate_t, g_up_t, g_down = _sum_partials("sum_ffn_partials", ffn_parts)
    g_in_p, g_uq_p, g_o = _sum_partials("sum_tail_partials", tail_parts)
    grads = dict(w_in=_w_in_from_kernel(g_in_p), w_uq=_w_uq_from_kernel(g_uq_p).reshape(QL // NDEV, HEADS * 192),
                 w_o=g_o, w_gate=g_gate_t.T, w_up=g_up_t.T, w_down=g_down)
    dmod_rows = got[:, 0:MODC // 128, :].reshape(NDEV, MODC)
    grads["b_ada"] = red[:, 0:MODC // 128, :].reshape(1, N_MOD * D)
    rep_flat = red[:, MOD_ROWS:, :].reshape(-1)
    off = 0
    for k in REP_NAMES:
        size = int(np.prod(given[k].shape))
        grads[k] = rep_flat[off:off + size]
        off += size

    view = dict(w_ada=(D, MODC), b_ada=(1, N_MOD * D), g_mix=(1, D), w_in=(D // NDEV, 960), g_q=(1, QL),
                g_kv=(1, KVL), w_uq=(QL // NDEV, HEADS * 192), w_uk=(KVL, HEADS * NOPE), w_uv=(KVL, HEADS * 128),
                w_pool=(GROUPS * GD, GD), pool_scale=(1, PW), w_o=(D // NDEV, D), g_ffn=(1, D),
                w_gate=(D, FF // NDEV), w_up=(D, FF // NDEV), w_down=(FF // NDEV, D), g_final=(1, D))
    names = list(view)
    g_ada, d_ada, m_ada, v_ada = _adamw_ada(w_ada[0], m_w_ada[0], v_w_ada[0], c_all.astype(BF), dmod_rows)
    out_g, out_d, out_m, out_v = dict(w_ada=g_ada), dict(w_ada=d_ada), dict(w_ada=m_ada), dict(w_ada=v_ada)
    tail = ("w_in", "w_uq", "w_o")
    groups = (("adamw_ffn", ("w_gate", "w_up", "w_down")),
              ("adamw_replicated", tuple(k for k in names if k not in ("w_ada", "w_gate", "w_up", "w_down") + tail)),
              ("adamw_tail", tail))
    for gname, members in groups:
        ws = [given[k].reshape(view[k]) for k in members]
        gs = [grads[k].reshape(view[k]) for k in members]
        ms = [given["m_" + k].reshape(view[k]) for k in members]
        vs = [given["v_" + k].reshape(view[k]) for k in members]
        ds, m2, v2 = _adamw_group(gname, ws, gs, ms, vs)
        for k, g, d, mm, vv in zip(members, gs, ds, m2, v2):
            out_g[k], out_d[k], out_m[k], out_v[k] = g, d, mm, vv

    total = rep_flat[off]
    shaped = lambda d: [d[k].reshape(given[k].shape) for k in names]
    return (total, dx[None], *shaped(out_g), *shaped(out_d), *shaped(out_m), *shaped(out_v))
```

```python
import numpy as np
import jax
import jax.numpy as jnp
from jax import lax
from jax.experimental import pallas as pl
from jax.experimental.pallas import tpu as pltpu
from jax.experimental.pallas import tpu_sc as plsc

D = 1024
HEADS = 4
NOPE = 128
ROPE = 64
HALF = ROPE // 2
QL = 256
KVL = 128
FF = 2816
PW = 512
GROUPS = 4
GD = 128
N_MOD = 6
EPS = 1e-6
SM_SCALE = (NOPE + ROPE) ** -0.5
ROPE_THETA = 10000.0
NDEV = 8
MODC = N_MOD * D // NDEV

ADAM_LR = 0.001
ADAM_B1 = 0.9
ADAM_B2 = 0.999
ADAM_EPS = 1e-08
ADAM_WD = 0.01
ADAM_STEP = 10

BF = jnp.bfloat16
F32 = jnp.float32
VMEM_LIMIT_V7X = 60 * 1024 * 1024
MESH = pl.DeviceIdType.MESH

TQ = 256
TK = 256
QW = 256
MOD_ROWS = 8
REP_ROWS = 200
SMALL_ROWS = MOD_ROWS + REP_ROWS


def _params(sem=None):
    return pltpu.CompilerParams(dimension_semantics=sem, vmem_limit_bytes=VMEM_LIMIT_V7X)


def _dot(a, b):
    return jnp.dot(a, b, preferred_element_type=F32)


def _dot_nt(a, b):
    return lax.dot_general(a, b, (((1,), (1,)), ((), ())), preferred_element_type=F32)


def _dot_tn(a, b):
    return _dot(a.astype(F32).T.astype(BF), b)


def _full(shape):
    return pl.BlockSpec(shape, lambda *_: (0,) * len(shape))


def _rows(ts, cols):
    return pl.BlockSpec((ts, cols), lambda i: (i, 0))


def _vmem():
    return pl.BlockSpec(memory_space=pltpu.VMEM)


def _any():
    return pl.BlockSpec(memory_space=pl.ANY)


def _rms(v):
    return lax.rsqrt(jnp.mean(v * v, axis=-1, keepdims=True) + EPS)


def _rms_bwd(dn, n, r):
    return r * (dn - n * jnp.mean(dn * n, axis=-1, keepdims=True))


def _colsum(v):
    return jnp.sum(v, axis=0, keepdims=True)


def _swap_halves(v):
    lane = lax.broadcasted_iota(jnp.int32, v.shape, 1)
    return jnp.where(lane < HALF, pltpu.roll(v, 128 - HALF, 1), pltpu.roll(v, HALF, 1))


def _window_lane_width():
    lane = lax.broadcasted_iota(jnp.int32, (1, PW), 1)
    return jnp.where(lane < 128, 2.0, jnp.where(lane < 256, 4.0, jnp.where(lane < 384, 8.0, 16.0))).astype(F32)


def _window_sums(ext, back):
    n = ext.shape[0]

    def sh(v, k):
        return pltpu.roll(v, k if back else n - k, 0)

    s2 = ext + sh(ext, 1)
    e4 = s2[:, 128:]
    s4 = e4 + sh(e4, 2)
    e8 = s4[:, 128:]
    s8 = e8 + sh(e8, 4)
    e16 = s8[:, 128:]
    s16 = e16 + sh(e16, 8)
    return jnp.concatenate([s2[:, :128], s4[:, :128], s8[:, :128], s16], axis=1)


def _row_counts(first_row, ts):
    t1 = (first_row + lax.broadcasted_iota(jnp.int32, (ts, 1), 0) + 1).astype(F32)
    return jnp.minimum(t1, _window_lane_width())


def _fwd_in(x, mod, g_mix, w_in, g_q, g_kv, w_uq, wuk_dc, perm, cos4, sin4, csk, snk, w_pool, pool_scale):
    S = x.shape[0]
    ts = 512
    nsub = ts // TQ

    def body(x_ref, mod_ref, gmix_ref, win_ref, gq_ref, gkv_ref, wuq_ref, wuk_ref, perm_ref, cos_ref, sin_ref,
             csk_ref, snk_ref, wpool_ref, pscale_ref,
             h1_ref, raw_ref, qn_ref, qs_ref, kv_ref, kvt_ref, pooled_ref, ypre_ref, ypool_ref, carry_ref):
        i = pl.program_id(0)

        @pl.when(i == 0)
        def _():
            carry_ref[...] = jnp.zeros_like(carry_ref)

        xv = x_ref[...]
        sh1 = mod_ref[0:1, 0:D]
        sc1 = mod_ref[0:1, D:2 * D]
        h = (xv * _rms(xv)) * gmix_ref[...] * (1.0 + sc1) + sh1
        hb = h.astype(BF)
        h1_ref[...] = hb
        proj = _dot(hb, win_ref[...])
        cq_raw = proj[:, 0:QL]
        ckv_raw = proj[:, QL:QL + KVL]
        kr = proj[:, 384:512]
        u = proj[:, 512:1024]
        raw_ref[...] = proj[:, 0:384]

        c_q = (cq_raw * _rms(cq_raw)) * gq_ref[...]
        c_kv = (ckv_raw * _rms(ckv_raw)) * gkv_ref[...]
        q = _dot(c_q.astype(BF), wuq_ref[...])
        qn = q[:, 0:HEADS * NOPE].astype(BF)
        qn_ref[...] = qn
        x1 = q[:, 512:640]
        x2 = q[:, 640:768]
        cosv = cos_ref[...]
        sinv = sin_ref[...]
        roped = jnp.concatenate([x1 * cosv - x2 * sinv, x1 * sinv + x2 * cosv], axis=1).astype(BF)
        for hd in range(HEADS):
            q_lat = _dot(qn[:, hd * NOPE:(hd + 1) * NOPE], wuk_ref[hd])
            q_rope = _dot(roped, perm_ref[hd])
            qh = jnp.concatenate([q_lat, q_rope], axis=1).astype(BF)
            for a in range(nsub):
                qs_ref[a, hd * TQ:(hd + 1) * TQ, :] = qh[a * TQ:(a + 1) * TQ, :]
        k_rope = kr * csk_ref[...] + _swap_halves(kr) * snk_ref[...]
        keys = jnp.concatenate([c_kv, k_rope], axis=1)
        kv_ref[...] = keys.astype(BF)
        for a in range(ts // TK):
            kvt_ref[a] = keys[a * TK:(a + 1) * TK, :].T.astype(BF)

        ext = jnp.concatenate([carry_ref[...], u], axis=0)
        win = _window_sums(ext, True)[16:, :]
        pooled = (win / _row_counts(i * ts, ts) - u).astype(BF)
        pooled_ref[...] = pooled
        carry_ref[...] = u[ts - 16:ts, :]
        ypre = jnp.concatenate(
            [_dot(pooled[:, g * GD:(g + 1) * GD], wpool_ref[g]) for g in range(GROUPS)], axis=1)
        ypre_ref[...] = ypre
        ypool_ref[...] = (ypre * pscale_ref[...]).astype(BF)

    out_shape = (
        jax.ShapeDtypeStruct((S, D), BF),
        jax.ShapeDtypeStruct((S, 384), F32),
        jax.ShapeDtypeStruct((S, HEADS * NOPE), BF),
        jax.ShapeDtypeStruct((S // TQ, HEADS * TQ, QW), BF),
        jax.ShapeDtypeStruct((S, QW), BF),
        jax.ShapeDtypeStruct((S // TK, QW, TK), BF),
        jax.ShapeDtypeStruct((S, PW), BF),
        jax.ShapeDtypeStruct((S, PW), F32),
        jax.ShapeDtypeStruct((S, PW), BF),
    )
    in_specs = [
        _rows(ts, D), _full(mod.shape), _full((1, D)), _full(w_in.shape), _full((1, QL)), _full((1, KVL)),
        _full(w_uq.shape), _full(wuk_dc.shape), _full(perm.shape), _rows(ts, 128), _rows(ts, 128), _rows(ts, 128),
        _rows(ts, 128), _full(w_pool.shape), _full((1, PW)),
    ]
    out_specs = (
        _rows(ts, D), _rows(ts, 384), _rows(ts, HEADS * NOPE),
        pl.BlockSpec((nsub, HEADS * TQ, QW), lambda i: (i, 0, 0)),
        _rows(ts, QW), pl.BlockSpec((ts // TK, QW, TK), lambda i: (i, 0, 0)), _rows(ts, PW), _rows(ts, PW),
        _rows(ts, PW),
    )
    return pl.pallas_call(
        body, name="fwd_in", out_shape=out_shape, grid=(S // ts,), in_specs=in_specs, out_specs=out_specs,
        scratch_shapes=[pltpu.VMEM((16, PW), F32)], compiler_params=_params(("arbitrary",)),
    )(x, mod, g_mix, w_in, g_q, g_kv, w_uq, wuk_dc, perm, cos4, sin4, csk, snk, w_pool, pool_scale)


def _diag_mask(shape, q_axis):
    qi = (lax.broadcasted_iota(jnp.int32, shape, q_axis) & (TQ - 1)) >> 6
    ki = lax.broadcasted_iota(jnp.int32, shape, 1 - q_axis) >> 6
    return ki <= qi


def _attn_fwd(qs, kv, kvt, wuv_vc):
    nq = qs.shape[0]
    S = kv.shape[0]
    M = HEADS * TQ

    def body(qs_ref, kv_ref, kvt_ref, wuv_ref, olat_ref, ymla_ref, lse_ref):
        i = pl.program_id(0)
        q = qs_ref[0]

        def step(kt, carry, masked):
            m, l, acc = carry
            k = kv_ref[pl.ds(pl.multiple_of(kt * TK, TK), TK), :]
            v_t = kvt_ref[kt][0:KVL, :]
            s = _dot_nt(k, q) * SM_SCALE
            if masked:
                s = jnp.where(_diag_mask((TK, M), 1), s, -jnp.inf)
            m_new = jnp.maximum(m, jnp.max(s, axis=0, keepdims=True))
            alpha = jnp.exp(m - m_new)
            p = jnp.exp(s - m_new)
            l = alpha * l + jnp.sum(p, axis=0, keepdims=True)
            acc = alpha * acc + _dot(v_t, p.astype(BF))
            return m_new, l, acc

        init = (jnp.full((1, M), -jnp.inf, F32), jnp.zeros((1, M), F32), jnp.zeros((KVL, M), F32))
        carry = lax.fori_loop(0, i, lambda kt, c: step(kt, c, False), init)
        m, l, acc = step(i, carry, True)
        o_lat = acc / l
        olat_ref[0] = o_lat
        lse_ref[0] = jnp.broadcast_to(m + jnp.log(l), (8, M))
        for hd in range(HEADS):
            o_t = _dot(wuv_ref[hd], o_lat[:, hd * TQ:(hd + 1) * TQ].astype(BF))
            ymla_ref[:, hd * 128:(hd + 1) * 128] = o_t.T.astype(BF)

    out_shape = (
        jax.ShapeDtypeStruct((nq, KVL, M), F32),
        jax.ShapeDtypeStruct((S, HEADS * 128), BF),
        jax.ShapeDtypeStruct((nq, 8, M), F32),
    )
    return pl.pallas_call(
        body, name="attn_fwd", out_shape=out_shape, grid=(nq,),
        in_specs=[pl.BlockSpec((1, M, QW), lambda i: (i, 0, 0)), _full(kv.shape), _full(kvt.shape),
                  _full(wuv_vc.shape)],
        out_specs=(pl.BlockSpec((1, KVL, M), lambda i: (i, 0, 0)), _rows(TQ, HEADS * 128),
                   pl.BlockSpec((1, 8, M), lambda i: (i, 0, 0))),
        compiler_params=_params(("arbitrary",)),
    )(qs, kv, kvt, wuv_vc)


def _silu_parts(a):
    sg = jax.nn.sigmoid(a)
    return sg, a * sg


def _ffn_fwd(x, ymla, ypool, mod, w_o, g_ffn, wg_t, wu_t, wd, g_final, target):
    S = x.shape[0]
    ts = 512
    tf = 256
    nj = FF // tf

    def body(x_ref, ymla_ref, ypool_ref, mod_ref, wo_ref, gffn_ref, wg_ref, wu_ref, wd_ref, gfin_ref, t_ref,
             x2_ref, mix_ref, h2t_ref, a_ref, b_ref, dx3_ref, dff_ref, dfft_ref, loss_ref, dgfin_ref, dgt2_ref,
             acc_ref, h2_ref):
        i = pl.program_id(0)
        j = pl.program_id(1)

        @pl.when(jnp.logical_and(i == 0, j == 0))
        def _():
            loss_ref[...] = jnp.zeros_like(loss_ref)
            dgfin_ref[...] = jnp.zeros_like(dgfin_ref)
            dgt2_ref[...] = jnp.zeros_like(dgt2_ref)

        @pl.when(j == 0)
        def _():
            gt1 = mod_ref[0:1, 2 * D:3 * D]
            sh2 = mod_ref[0:1, 3 * D:4 * D]
            sc2 = mod_ref[0:1, 4 * D:5 * D]
            cat = jnp.concatenate([ymla_ref[...], ypool_ref[...]], axis=1)
            mix = _dot(cat, wo_ref[...])
            mix_ref[...] = mix
            x2 = x_ref[...] + gt1 * mix
            x2_ref[...] = x2
            h2 = (x2 * _rms(x2)) * gffn_ref[...] * (1.0 + sc2) + sh2
            h2_ref[...] = h2.astype(BF)
            h2t_ref[...] = h2.T.astype(BF)
            acc_ref[...] = jnp.zeros_like(acc_ref)

        h2b = h2_ref[...]
        a = _dot_nt(h2b, wg_ref[...])
        b = _dot_nt(h2b, wu_ref[...])
        a_ref[...] = a.astype(BF)
        b_ref[...] = b.astype(BF)
        f = _silu_parts(a)[1] * b
        acc_ref[...] += _dot(f.astype(BF), wd_ref[...])

        @pl.when(j == nj - 1)
        def _():
            gt2 = mod_ref[0:1, 5 * D:6 * D]
            ff = acc_ref[...]
            x3 = x2_ref[...] + gt2 * ff
            r3 = _rms(x3)
            xn3 = x3 * r3
            gfin = gfin_ref[...]
            e = xn3 * gfin - t_ref[...]
            loss_ref[...] += 0.5 * jnp.sum(jnp.mean(e * e, axis=-1, keepdims=True))
            dy = e * (1.0 / D)
            dgfin_ref[...] += _colsum(dy * xn3)
            dx3 = _rms_bwd(dy * gfin, xn3, r3)
            dx3_ref[...] = dx3
            dgt2_ref[...] += _colsum(dx3 * ff)
            dff = dx3 * gt2
            dff_ref[...] = dff.astype(BF)
            dfft_ref[...] = dff.T.astype(BF)

    row = lambda c: pl.BlockSpec((ts, c), lambda i, j: (i, 0))
    col = pl.BlockSpec((D, ts), lambda i, j: (0, i))
    wblk = pl.BlockSpec((tf, D), lambda i, j: (j, 0))
    act = pl.BlockSpec((ts, tf), lambda i, j: (i, j))
    const = lambda shape: pl.BlockSpec(shape, lambda i, j: (0,) * len(shape))
    out_shape = (
        jax.ShapeDtypeStruct((S, D), F32),
        jax.ShapeDtypeStruct((S, D), F32),
        jax.ShapeDtypeStruct((D, S), BF),
        jax.ShapeDtypeStruct((S, FF), BF),
        jax.ShapeDtypeStruct((S, FF), BF),
        jax.ShapeDtypeStruct((S, D), F32),
        jax.ShapeDtypeStruct((S, D), BF),
        jax.ShapeDtypeStruct((D, S), BF),
        jax.ShapeDtypeStruct((8, 128), F32),
        jax.ShapeDtypeStruct((1, D), F32),
        jax.ShapeDtypeStruct((1, D), F32),
    )
    return pl.pallas_call(
        body, name="ffn_fwd", out_shape=out_shape, grid=(S // ts, nj),
        in_specs=[row(D), row(PW), row(PW), const(mod.shape), const(w_o.shape), const((1, D)), wblk, wblk, wblk,
                  const((1, D)), row(D)],
        out_specs=(row(D), row(D), col, act, act, row(D), row(D), col, const((8, 128)), const((1, D)),
                   const((1, D))),
        scratch_shapes=[pltpu.VMEM((ts, D), F32), pltpu.VMEM((ts, D), BF)],
        compiler_params=_params(("arbitrary", "arbitrary")),
    )(x, ymla, ypool, mod, w_o, g_ffn, wg_t, wu_t, wd, g_final, target)


FCHUNK = 256


def _ffn_bwd_acts(dff, a, b, wg_t, wu_t, wd):
    S = dff.shape[0]
    ts = 512

    def body(dff_ref, a_ref, b_ref, wg_ref, wu_ref, wd_ref, da_ref, db_ref, dh2_ref):
        dffb = dff_ref[...]
        for c in range(FF // FCHUNK):
            cols = slice(c * FCHUNK, (c + 1) * FCHUNK)
            df = _dot_nt(dffb, wd_ref[cols, :])
            av = a_ref[:, cols].astype(F32)
            bv = b_ref[:, cols].astype(F32)
            sg, sa = _silu_parts(av)
            db_ref[:, cols] = (df * sa).astype(BF)
            da_ref[:, cols] = (df * bv * (sg * (1.0 + av * (1.0 - sg)))).astype(BF)
        dh2_ref[...] = _dot(da_ref[...], wg_ref[...]) + _dot(db_ref[...], wu_ref[...])

    act = _rows(ts, FF)
    return pl.pallas_call(
        body, name="ffn_bwd_acts",
        out_shape=(jax.ShapeDtypeStruct((S, FF), BF), jax.ShapeDtypeStruct((S, FF), BF),
                   jax.ShapeDtypeStruct((S, D), F32)),
        grid=(S // ts,), in_specs=[_rows(ts, D), act, act, _vmem(), _vmem(), _vmem()],
        out_specs=(act, act, _rows(ts, D)), compiler_params=_params(("arbitrary",)),
    )(dff, a, b, wg_t, wu_t, wd)


def _ffn_bwd_weights(dff_t, h2_t, da, db, a, b):
    S = da.shape[0]

    def body(dfft_ref, h2t_ref, da_ref, db_ref, a_ref, b_ref, dwg_ref, dwu_ref, dwd_ref):
        h2t = h2t_ref[...]
        dwg_ref[...] = _dot(h2t, da_ref[...]).T.astype(BF)
        dwu_ref[...] = _dot(h2t, db_ref[...]).T.astype(BF)
        f = (_silu_parts(a_ref[...].astype(F32))[1] * b_ref[...].astype(F32)).astype(BF)
        dwd_ref[...] = _dot(dfft_ref[...], f).T.astype(BF)

    act = pl.BlockSpec((S, FCHUNK), lambda j: (0, j))
    wblk = _rows(FCHUNK, D)
    shp = jax.ShapeDtypeStruct((FF, D), BF)
    return pl.pallas_call(
        body, name="ffn_bwd_weights", out_shape=(shp, shp, shp), grid=(FF // FCHUNK,),
        in_specs=[_vmem(), _vmem(), act, act, act, act], out_specs=(wblk, wblk, wblk),
        compiler_params=_params(("arbitrary",)),
    )(dff_t, h2_t, da, db, a, b)


def _mix_bwd(dh2, dx3, x2, mix, mod, g_ffn, ymla, ypool, w_o, ypre, pooled, pool_scale, wpool_dc, olat, wuv_vc):
    S = dh2.shape[0]
    ts = 512
    n = S // ts
    nsub = ts // TQ
    M = HEADS * TQ

    def body(dh2_ref, dx3_ref, x2_ref, mix_ref, mod_ref, gffn_ref, ymla_ref, ypool_ref, wo_ref, ypre_ref, pooled_ref,
             pscale_ref, wpool_ref, olat_ref, wuv_ref,
             dx2_ref, du_ref, dolat_ref, delta_ref, dwo_ref, dwuv_ref, dwpool_ref, dpscale_ref, dgt1_ref, dsc2_ref,
             dsh2_ref, dgffn_ref, carry_ref, dwo_acc):
        i = pl.program_id(0)

        @pl.when(i == 0)
        def _():
            carry_ref[...] = jnp.zeros_like(carry_ref)
            dwo_acc[...] = jnp.zeros_like(dwo_acc)
            for r in (dwuv_ref, dwpool_ref, dpscale_ref, dgt1_ref, dsc2_ref, dsh2_ref, dgffn_ref):
                r[...] = jnp.zeros_like(r)

        gt1 = mod_ref[0:1, 2 * D:3 * D]
        sc2 = mod_ref[0:1, 4 * D:5 * D]
        gffn = gffn_ref[...]
        dh2 = dh2_ref[...]
        x2 = x2_ref[...]
        r2 = _rms(x2)
        xn2 = x2 * r2
        dsc2_ref[...] += _colsum(dh2 * (xn2 * gffn))
        dsh2_ref[...] += _colsum(dh2)
        dgffn_ref[...] += _colsum(dh2 * (1.0 + sc2) * xn2)
        dx2 = dx3_ref[...] + _rms_bwd(dh2 * gffn * (1.0 + sc2), xn2, r2)
        dx2_ref[...] = dx2
        dgt1_ref[...] += _colsum(dx2 * mix_ref[...])
        dmix = (dx2 * gt1).astype(BF)
        cat = jnp.concatenate([ymla_ref[...], ypool_ref[...]], axis=1)
        dwo_acc[...] += _dot_tn(cat, dmix)
        dcat = _dot_nt(dmix, wo_ref[...])
        dymla = dcat[:, 0:512]
        dypool = dcat[:, 512:1024]

        dpscale_ref[...] += _colsum(dypool * ypre_ref[...])
        dypre = (dypool * pscale_ref[...]).astype(BF)
        pooled = pooled_ref[...]
        dpooled = []
        for g in range(GROUPS):
            sl = slice(g * GD, (g + 1) * GD)
            dwpool_ref[g] += _dot_tn(pooled[:, sl], dypre[:, sl])
            dpooled.append(_dot(dypre[:, sl], wpool_ref[g]))
        dpooled = jnp.concatenate(dpooled, axis=1)
        tile = n - 1 - i
        e = dpooled / _row_counts(tile * ts, ts)
        ext = jnp.concatenate([e, carry_ref[...]], axis=0)
        du_ref[...] = _window_sums(ext, False)[0:ts, :] - dpooled
        carry_ref[...] = e[0:16, :]

        for hd in range(HEADS):
            do = dymla[:, hd * 128:(hd + 1) * 128]
            dob = do.astype(BF)
            dol = _dot(dob, wuv_ref[hd])
            for a in range(nsub):
                ol_t = olat_ref[a, :, hd * TQ:(hd + 1) * TQ]
                dl = dol[a * TQ:(a + 1) * TQ, :]
                dolat_ref[a, hd * TQ:(hd + 1) * TQ, :] = dl.astype(BF)
                dwuv_ref[hd] += _dot(ol_t.astype(BF), dob[a * TQ:(a + 1) * TQ, :])
                delta = jnp.sum(dl * ol_t.T, axis=-1, keepdims=True)
                delta_ref[a, :, hd * TQ:(hd + 1) * TQ] = jnp.broadcast_to(delta, (TQ, 128)).T[0:8, :]

        @pl.when(i == n - 1)
        def _():
            dwo_ref[...] = dwo_acc[...].astype(BF)

    rev = lambda c: pl.BlockSpec((ts, c), lambda i: (n - 1 - i, 0))
    rev3 = lambda r, c: pl.BlockSpec((nsub, r, c), lambda i: (n - 1 - i, 0, 0))
    out_shape = (
        jax.ShapeDtypeStruct((S, D), F32),
        jax.ShapeDtypeStruct((S, PW), F32),
        jax.ShapeDtypeStruct((S // TQ, M, KVL), BF),
        jax.ShapeDtypeStruct((S // TQ, 8, M), F32),
        jax.ShapeDtypeStruct((D, D), BF),
        jax.ShapeDtypeStruct((HEADS, KVL, 128), F32),
        jax.ShapeDtypeStruct((GROUPS, GD, GD), F32),
        jax.ShapeDtypeStruct((1, PW), F32),
        jax.ShapeDtypeStruct((1, D), F32), jax.ShapeDtypeStruct((1, D), F32), jax.ShapeDtypeStruct((1, D), F32),
        jax.ShapeDtypeStruct((1, D), F32),
    )
    in_specs = [rev(D), rev(D), rev(D), rev(D), _full(mod.shape), _full((1, D)), rev(PW), rev(PW), _full(w_o.shape),
                rev(PW), rev(PW), _full((1, PW)), _full(wpool_dc.shape), rev3(KVL, M), _full(wuv_vc.shape)]
    out_specs = (rev(D), rev(PW), rev3(M, KVL), rev3(8, M), _full((D, D)), _full((HEADS, KVL, 128)),
                 _full((GROUPS, GD, GD)), _full((1, PW)), _full((1, D)), _full((1, D)), _full((1, D)), _full((1, D)))
    return pl.pallas_call(
        body, name="mix_bwd", out_shape=out_shape, grid=(n,), in_specs=in_specs, out_specs=out_specs,
        scratch_shapes=[pltpu.VMEM((16, PW), F32), pltpu.VMEM((D, D), F32)],
        compiler_params=_params(("arbitrary",)),
    )(dh2, dx3, x2, mix, mod, g_ffn, ymla, ypool, w_o, ypre, pooled, pool_scale, wpool_dc, olat, wuv_vc)


def _attn_bwd(qs, kv, dolat, lse, delta):
    nq = qs.shape[0]
    S = kv.shape[0]
    M = HEADS * TQ
    nk = S // TK

    def body(qs_ref, kv_ref, do_ref, lse_ref, delta_ref, dkv_ref, dqt_ref):
        kt = pl.program_id(0)
        k = kv_ref[...]
        v = k[:, 0:KVL]
        k_t = k.astype(F32).T.astype(BF)

        @pl.when(kt == 0)
        def _():
            dqt_ref[...] = jnp.zeros_like(dqt_ref)

        def step(qi, carry, masked):
            dk, dv = carry
            q = qs_ref[qi]
            do = do_ref[qi]
            s = _dot_nt(k, q) * SM_SCALE
            p = jnp.exp(s - lse_ref[qi, 0:1, :])
            if masked:
                p = jnp.where(_diag_mask((TK, M), 1), p, 0.0)
            dp = _dot_nt(v, do)
            ds = (p * (dp - delta_ref[qi, 0:1, :]) * SM_SCALE).astype(BF)
            dv = dv + _dot(p.astype(BF), do)
            dk = dk + _dot(ds, q)
            dqt_ref[qi] += _dot(k_t, ds)
            return dk, dv

        carry = step(kt, (jnp.zeros((TK, QW), F32), jnp.zeros((TK, KVL), F32)), True)
        dk, dv = lax.fori_loop(kt + 1, nq, lambda qi, c: step(qi, c, False), carry)
        dkv_ref[...] = dk + jnp.concatenate([dv, jnp.zeros((TK, QW - KVL), F32)], axis=1)

    out_shape = (jax.ShapeDtypeStruct((S, QW), F32), jax.ShapeDtypeStruct((nq, QW, M), F32))
    return pl.pallas_call(
        body, name="attn_bwd", out_shape=out_shape, grid=(nk,),
        in_specs=[_vmem(), _rows(TK, QW), _vmem(), _vmem(), _vmem()],
        out_specs=(_rows(TK, QW), _vmem()),
        compiler_params=_params(("arbitrary",)),
    )(qs, kv, dolat, lse, delta)


def _in_bwd(dqt, dkv, du, raw, qn, h1, x, dx2, mod, g_mix, w_in, g_q, g_kv, w_uq, wuk_cd, perm_t, cos4, sin4, csk,
            snk):
    S = x.shape[0]
    ts = 512
    n = S // ts
    nsub = ts // TQ
    M = HEADS * TQ

    def body(dqt_ref, dkv_ref, du_ref, raw_ref, qn_ref, h1_ref, x_ref, dx2_ref, mod_ref, gmix_ref, win_ref, gq_ref,
             gkv_ref, wuq_ref, wuk_ref, permt_ref, cos_ref, sin_ref, csk_ref, snk_ref,
             dx_ref, dwin_ref, dwuq_ref, dwuk_ref, dgq_ref, dgkv_ref, dsc1_ref, dsh1_ref, dgmix_ref, dwin_acc,
             dwuq_acc):
        i = pl.program_id(0)

        @pl.when(i == 0)
        def _():
            dwin_acc[...] = jnp.zeros_like(dwin_acc)
            dwuq_acc[...] = jnp.zeros_like(dwuq_acc)
            for r in (dwuk_ref, dgq_ref, dgkv_ref, dsc1_ref, dsh1_ref, dgmix_ref):
                r[...] = jnp.zeros_like(r)

        dq_blocks = [dqt_ref[a].T for a in range(nsub)]
        qn = qn_ref[...]
        dq_parts = []
        drope = jnp.zeros((ts, 2 * 128), F32)
        for hd in range(HEADS):
            dqh = jnp.concatenate([blk[hd * TQ:(hd + 1) * TQ, :] for blk in dq_blocks], axis=0)
            dq_lat = dqh[:, 0:KVL].astype(BF)
            dq_parts.append(_dot(dq_lat, wuk_ref[hd]))
            dwuk_ref[hd] += _dot_tn(dq_lat, qn[:, hd * NOPE:(hd + 1) * NOPE])
            drope = drope + _dot(dqh[:, KVL:QW].astype(BF), permt_ref[hd])
        do1 = drope[:, 0:128]
        do2 = drope[:, 128:256]
        cosv = cos_ref[...]
        sinv = sin_ref[...]
        dq_parts.append(do1 * cosv + do2 * sinv)
        dq_parts.append(do2 * cosv - do1 * sinv)
        dq = jnp.concatenate(dq_parts, axis=1).astype(BF)

        cq_raw = raw_ref[:, 0:QL]
        ckv_raw = raw_ref[:, QL:QL + KVL]
        rq = _rms(cq_raw)
        nq_ = cq_raw * rq
        gq = gq_ref[...]
        dwuq_acc[...] += _dot_tn((nq_ * gq).astype(BF), dq)
        dc_q = _dot_nt(dq, wuq_ref[...])
        dgq_ref[...] += _colsum(dc_q * nq_)
        dcq_raw = _rms_bwd(dc_q * gq, nq_, rq)

        dkv = dkv_ref[...]
        rk = _rms(ckv_raw)
        nk_ = ckv_raw * rk
        dc_kv = dkv[:, 0:KVL]
        dgkv_ref[...] += _colsum(dc_kv * nk_)
        dckv_raw = _rms_bwd(dc_kv * gkv_ref[...], nk_, rk)
        dkr_roped = dkv[:, KVL:QW]
        dkr = dkr_roped * csk_ref[...] - _swap_halves(dkr_roped) * snk_ref[...]

        dproj = jnp.concatenate([dcq_raw, dckv_raw, dkr, du_ref[...]], axis=1).astype(BF)
        dwin_acc[...] += _dot_tn(h1_ref[...], dproj)
        dh1 = _dot_nt(dproj, win_ref[...])

        sc1 = mod_ref[0:1, D:2 * D]
        gmix = gmix_ref[...]
        xv = x_ref[...]
        r1 = _rms(xv)
        xn1 = xv * r1
        dsc1_ref[...] += _colsum(dh1 * (xn1 * gmix))
        dsh1_ref[...] += _colsum(dh1)
        dgmix_ref[...] += _colsum(dh1 * (1.0 + sc1) * xn1)
        dx_ref[...] = dx2_ref[...] + _rms_bwd(dh1 * gmix * (1.0 + sc1), xn1, r1)

        @pl.when(i == n - 1)
        def _():
            dwin_ref[...] = dwin_acc[...].astype(BF)
            dwuq_ref[...] = dwuq_acc[...].astype(BF)

    out_shape = (
        jax.ShapeDtypeStruct((S, D), F32),
        jax.ShapeDtypeStruct((D, D), BF),
        jax.ShapeDtypeStruct((QL, 768), BF),
        jax.ShapeDtypeStruct((HEADS, KVL, NOPE), F32),
        jax.ShapeDtypeStruct((1, QL), F32), jax.ShapeDtypeStruct((1, KVL), F32),
        jax.ShapeDtypeStruct((1, D), F32), jax.ShapeDtypeStruct((1, D), F32), jax.ShapeDtypeStruct((1, D), F32),
    )
    in_specs = [pl.BlockSpec((nsub, QW, M), lambda i: (i, 0, 0)), _rows(ts, QW), _rows(ts, PW), _rows(ts, 384),
                _rows(ts, HEADS * NOPE), _rows(ts, D), _rows(ts, D), _rows(ts, D), _full(mod.shape), _full((1, D)),
                _full(w_in.shape), _full((1, QL)), _full((1, KVL)), _full(w_uq.shape), _full(wuk_cd.shape),
                _full(perm_t.shape), _rows(ts, 128), _rows(ts, 128), _rows(ts, 128), _rows(ts, 128)]
    out_specs = (_rows(ts, D), _full((D, D)), _full((QL, 768)), _full((HEADS, KVL, NOPE)), _full((1, QL)),
                 _full((1, KVL)), _full((1, D)), _full((1, D)), _full((1, D)))
    return pl.pallas_call(
        body, name="in_bwd", out_shape=out_shape, grid=(n,), in_specs=in_specs, out_specs=out_specs,
        scratch_shapes=[pltpu.VMEM((D, D), F32), pltpu.VMEM((QL, 768), F32)],
        compiler_params=_params(("arbitrary",)),
    )(dqt, dkv, du, raw, qn, h1, x, dx2, mod, g_mix, w_in, g_q, g_kv, w_uq, wuk_cd, perm_t, cos4, sin4, csk, snk)


def _rope_perm():
    p = np.zeros((HEADS, 2 * 128, 128), np.float32)
    for hd in range(HEADS):
        for t in range(HALF):
            p[hd, hd * HALF + t, t] = 1.0
            p[hd, 128 + hd * HALF + t, HALF + t] = 1.0
    return p


def _rope_tables(positions):
    freqs = jnp.power(ROPE_THETA, -jnp.arange(HALF, dtype=F32) / HALF)
    ang = positions.astype(F32)[:, None] * jnp.tile(freqs, HEADS)[None, :]
    cos4 = jnp.cos(ang)
    sin4 = jnp.sin(ang)
    lane = jnp.arange(HEADS * HALF)[None, :]
    csk = jnp.where(lane < ROPE, cos4, 0.0)
    snk = jnp.where(lane < HALF, -sin4, jnp.where(lane < ROPE, sin4, 0.0))
    return cos4, sin4, csk, snk


def _local_step(x, positions, target, mod, g_mix, w_in_p, g_q, g_kv, w_uq_p, w_uk, w_uv, w_pool, pool_scale, g_ffn,
                g_final, late, ffn_grads_exchange):
    perm = jnp.asarray(_rope_perm(), BF)
    perm_t = jnp.asarray(_rope_perm().transpose(0, 2, 1), BF)
    cos4, sin4, csk, snk = _rope_tables(positions)
    wuk_dc = w_uk.transpose(1, 2, 0).astype(BF)
    wuk_cd = w_uk.transpose(1, 0, 2).astype(BF)
    wuv_vc = w_uv.transpose(1, 2, 0).astype(BF)
    wpool = w_pool.astype(BF)
    wpool_dc = w_pool.transpose(0, 2, 1).astype(BF)

    h1, raw, qn, qs, kv, kvt, pooled, ypre, ypool = _fwd_in(
        x, mod, g_mix, w_in_p, g_q, g_kv, w_uq_p, wuk_dc, perm, cos4, sin4, csk, snk, wpool, pool_scale)
    olat, ymla, lse = _attn_fwd(qs, kv, kvt, wuv_vc)
    w_o, wg_t, wu_t, wd = late
    x2, mix, h2_t, a, b, dx3, dff, dff_t, loss, dgfin, dgt2 = _ffn_fwd(
        x, ymla, ypool, mod, w_o, g_ffn, wg_t, wu_t, wd, g_final, target)
    da, db, dh2 = _ffn_bwd_acts(dff, a, b, wg_t, wu_t, wd)
    dwg_t, dwu_t, dwd = _ffn_bwd_weights(dff_t, h2_t, da, db, a, b)
    ffn_parts = ffn_grads_exchange((dwg_t, dwu_t, dwd))
    (dx2, du, dolat, delta, dwo, dwuv, dwpool, dpscale, dgt1, dsc2, dsh2, dgffn) = _mix_bwd(
        dh2, dx3, x2, mix, mod, g_ffn, ymla, ypool, w_o, ypre, pooled, pool_scale, wpool_dc, olat, wuv_vc)
    dkv, dqt = _attn_bwd(qs, kv, dolat, lse, delta)
    dx, dwin, dwuq, dwuk, dgq, dgkv, dsc1, dsh1, dgmix = _in_bwd(
        dqt, dkv, du, raw, qn, h1, x, dx2, mod, g_mix, w_in_p, g_q, g_kv, w_uq_p, wuk_cd, perm_t, cos4, sin4, csk,
        snk)
    dmod = jnp.concatenate([dsh1, dsc1, dgt1, dsh2, dsc2, dgt2], axis=1)
    replicated = dict(
        w_uk=dwuk.transpose(1, 0, 2), w_uv=dwuv.transpose(1, 0, 2), w_pool=dwpool, g_mix=dgmix, g_q=dgq, g_kv=dgkv,
        pool_scale=dpscale, g_ffn=dgffn, g_final=dgfin)
    return loss[0, 0], dx, dmod, (dwin, dwuq, dwo), ffn_parts, replicated


def _my_pos():
    return lax.axis_index("x"), lax.axis_index("y"), lax.axis_index("c")


def _peer(pos, k):
    x, y, c = pos
    return (1 - x if k & 4 else x, 1 - y if k & 2 else y, 1 - c if k & 1 else c)


def _index(pos):
    x, y, c = pos
    return 4 * x + 2 * y + c


def _remote(src, dst, send_sem, recv_sem, to):
    return pltpu.make_async_remote_copy(src_ref=src, dst_ref=dst, send_sem=send_sem, recv_sem=recv_sem,
                                        device_id=to, device_id_type=MESH)


def _ada_mod(c, w_ada, b_ada):
    def body(c_ref, w_ref, b_ref, mod_ref, call_ref, cbuf, sbuf, rbuf, send1, recv1, send2, recv2):
        me = _my_pos()
        mi = _index(me)
        cv = c_ref[...]
        cbuf[...] = jnp.broadcast_to(cv * jax.nn.sigmoid(cv), (8, D))
        call_ref[mi] = cbuf[...]
        first = [_remote(cbuf, call_ref.at[mi], send1.at[k - 1], recv1.at[k - 1], _peer(me, k)) for k in range(1, NDEV)]
        for cp in first:
            cp.start()
        for k in range(1, NDEV):
            _remote(cbuf, call_ref.at[_index(_peer(me, k))], send1.at[k - 1], recv1.at[k - 1], _peer(me, k)).wait_recv()
        c_all = jnp.concatenate([call_ref[b][0:1, :] for b in range(NDEV)], axis=0)
        blocks = _dot(c_all.astype(BF), w_ref[...].astype(BF))
        for b in range(NDEV):
            sbuf[b] = jnp.broadcast_to(blocks[b:b + 1, :], (8, MODC))
        second = []
        for k in range(1, NDEV):
            to = _peer(me, k)
            second.append(_remote(sbuf.at[_index(to)], rbuf.at[mi], send2.at[k - 1], recv2.at[k - 1], to))
        for cp in second:
            cp.start()
        rbuf[mi] = sbuf[mi]
        for k in range(1, NDEV):
            to = _peer(me, k)
            _remote(sbuf.at[_index(to)], rbuf.at[_index(to)], send2.at[k - 1], recv2.at[k - 1], to).wait_recv()
        for j in range(NDEV):
            mod_ref[:, j * MODC:(j + 1) * MODC] = rbuf[j] + b_ref[:, j * MODC:(j + 1) * MODC]
        for cp in first + second:
            cp.wait_send()

    return pl.pallas_call(
        body, name="ada_mod",
        out_shape=(jax.ShapeDtypeStruct((8, N_MOD * D), F32), jax.ShapeDtypeStruct((NDEV, 8, D), F32)),
        in_specs=[_vmem(), _vmem(), _vmem()], out_specs=(_vmem(), _vmem()),
        scratch_shapes=[pltpu.VMEM((8, D), F32), pltpu.VMEM((NDEV, 8, MODC), F32), pltpu.VMEM((NDEV, 8, MODC), F32),
                        pltpu.SemaphoreType.DMA((NDEV - 1,)), pltpu.SemaphoreType.DMA((NDEV - 1,)),
                        pltpu.SemaphoreType.DMA((NDEV - 1,)), pltpu.SemaphoreType.DMA((NDEV - 1,))],
        compiler_params=_params(),
    )(c, w_ada, b_ada)


def _sequencer_scatter(name, collective_id, srcs, after=()):
    n = len(srcs)

    def of(src, to_index):
        r = src.shape[0] // NDEV
        return src.at[pl.ds(pl.multiple_of(to_index * r, 16), r), :]

    def body(*refs):
        src, zone = refs[:n], refs[n + len(after):2 * n + len(after)]
        send, recv, local = refs[2 * n + len(after):]
        me = _my_pos()
        mi = _index(me)
        barrier = pltpu.get_barrier_semaphore()
        for k in range(1, NDEV):
            pl.semaphore_signal(barrier, inc=1, device_id=_peer(me, k), device_id_type=MESH)
        pl.semaphore_wait(barrier, NDEV - 1)
        own = [pltpu.make_async_copy(of(src[a], mi), zone[a].at[mi], local.at[a]) for a in range(n)]
        for cp in own:
            cp.start()
        for a in range(n):
            for k in range(1, NDEV):
                to = _peer(me, k)
                s = a * (NDEV - 1) + k - 1
                _remote(of(src[a], _index(to)), zone[a].at[mi], send.at[s], recv.at[s], to).start()
        for cp in own:
            cp.wait()
        for a in range(n):
            for k in range(1, NDEV):
                to = _peer(me, k)
                s = a * (NDEV - 1) + k - 1
                cp = _remote(of(src[a], mi), zone[a].at[_index(to)], send.at[s], recv.at[s], to)
                cp.wait_send()
                cp.wait_recv()

    return pl.kernel(
        body, name=name, mesh=plsc.ScalarSubcoreMesh(axis_name="sequencer", num_cores=1),
        out_type=tuple(jax.ShapeDtypeStruct((NDEV, s.shape[0] // NDEV, s.shape[1]), s.dtype) for s in srcs),
        scratch_types=[pltpu.SemaphoreType.DMA((n * (NDEV - 1),)), pltpu.SemaphoreType.DMA((n * (NDEV - 1),)),
                       pltpu.SemaphoreType.DMA((n,))],
        compiler_params=pltpu.CompilerParams(collective_id=collective_id),
    )(*srcs, *after)


CHIP_PEERS = (2, 4, 6)


def _sequencer_gather(name, collective_id, srcs, after=()):
    n = len(srcs)
    per = NDEV - 1

    def body(*refs):
        src, zone = refs[:n], refs[n + len(after):2 * n + len(after)]
        send, recv, local = refs[2 * n + len(after):]
        me = _my_pos()
        mi = _index(me)
        sibling = _peer(me, 1)
        talk_to = (sibling,) + tuple(_peer(me, k) for k in CHIP_PEERS)
        barrier = pltpu.get_barrier_semaphore()
        for to in talk_to:
            pl.semaphore_signal(barrier, inc=1, device_id=to, device_id_type=MESH)
        pl.semaphore_wait(barrier, len(talk_to))

        def copy(a, slot, block_of, to, from_src=False):
            rows = zone[a].at[_index(block_of)]
            return _remote(src[a] if from_src else rows, rows, send.at[a * per + slot], recv.at[a * per + slot], to)

        own = [pltpu.make_async_copy(src[a], zone[a].at[mi], local.at[a]) for a in range(n)]
        for cp in own:
            cp.start()
        started = []
        for a in range(n):
            started.append(copy(a, 0, me, sibling, from_src=True))
            started += [copy(a, 1 + j, me, _peer(me, k), from_src=True) for j, k in enumerate(CHIP_PEERS)]
        for cp in started:
            cp.start()
        for a in range(n):
            for j, k in enumerate(CHIP_PEERS):
                copy(a, 1 + j, _peer(me, k), me).wait_recv()
                passed = copy(a, 4 + j, _peer(me, k), sibling)
                passed.start()
                started.append(passed)
        for a in range(n):
            copy(a, 0, sibling, me).wait_recv()
            for j, k in enumerate(CHIP_PEERS):
                copy(a, 4 + j, _peer(me, k | 1), me).wait_recv()
        for cp in started:
            cp.wait_send()
        for cp in own:
            cp.wait()

    return pl.kernel(
        body, name=name, mesh=plsc.ScalarSubcoreMesh(axis_name="sequencer", num_cores=1),
        out_type=tuple(jax.ShapeDtypeStruct((NDEV,) + s.shape, s.dtype) for s in srcs),
        scratch_types=[pltpu.SemaphoreType.DMA((n * per,)), pltpu.SemaphoreType.DMA((n * per,)),
                       pltpu.SemaphoreType.DMA((n,))],
        compiler_params=pltpu.CompilerParams(collective_id=collective_id),
    )(*srcs, *after)


def _sum_partials(name, parts):
    n = len(parts)

    def body(*refs):
        for a in range(n):
            acc = refs[a][0].astype(F32)
            for p in range(1, NDEV):
                acc = acc + refs[a][p].astype(F32)
            refs[n + a][...] = acc

    return pl.pallas_call(
        body, name=name,
        out_shape=tuple(jax.ShapeDtypeStruct(p.shape[1:], F32) for p in parts),
        in_specs=[_vmem()] * n, out_specs=tuple([_vmem()] * n), compiler_params=_params(),
    )(*parts)


def _small_all_reduce(buf):
    def body(buf_ref, got_ref, red_ref, mine, send1, recv1, send2, recv2):
        me = _my_pos()
        mi = _index(me)
        first = []
        for k in range(1, NDEV):
            to = _peer(me, k)
            first.append(_remote(buf_ref.at[_index(to)], got_ref.at[mi], send1.at[k - 1], recv1.at[k - 1], to))
        for cp in first:
            cp.start()
        got_ref[mi] = buf_ref[mi]
        for k in range(1, NDEV):
            to = _peer(me, k)
            _remote(buf_ref.at[mi], got_ref.at[_index(to)], send1.at[k - 1], recv1.at[k - 1], to).wait_recv()
        acc = got_ref[0]
        for p in range(1, NDEV):
            acc = acc + got_ref[p]
        mine[...] = acc
        second = [_remote(mine, red_ref.at[mi], send2.at[k - 1], recv2.at[k - 1], _peer(me, k)) for k in range(1, NDEV)]
        for cp in second:
            cp.start()
        red_ref[mi] = acc
        for k in range(1, NDEV):
            to = _peer(me, k)
            _remote(mine, red_ref.at[_index(to)], send2.at[k - 1], recv2.at[k - 1], to).wait_recv()
        for cp in first + second:
            cp.wait_send()

    return pl.pallas_call(
        body, name="small_all_reduce",
        out_shape=(jax.ShapeDtypeStruct(buf.shape, F32), jax.ShapeDtypeStruct(buf.shape, F32)),
        in_specs=[_vmem()], out_specs=(_vmem(), _vmem()),
        scratch_shapes=[pltpu.VMEM(buf.shape[1:], F32),
                        pltpu.SemaphoreType.DMA((NDEV - 1,)), pltpu.SemaphoreType.DMA((NDEV - 1,)),
                        pltpu.SemaphoreType.DMA((NDEV - 1,)), pltpu.SemaphoreType.DMA((NDEV - 1,))],
        compiler_params=_params(),
    )(buf)


def _adamw_math(w, g, m, v):
    m = ADAM_B1 * m + (1.0 - ADAM_B1) * g
    v = ADAM_B2 * v + (1.0 - ADAM_B2) * jnp.square(g)
    m_hat = m / (1.0 - ADAM_B1 ** ADAM_STEP)
    v_hat = v / (1.0 - ADAM_B2 ** ADAM_STEP)
    delta = -ADAM_LR * (m_hat / (jnp.sqrt(v_hat) + ADAM_EPS) + ADAM_WD * w)
    return delta, m, v


def _adamw_group(name, ws, gs, ms, vs):
    n = len(ws)

    def body(*refs):
        for a in range(n):
            w, g, m, v = (refs[q * n + a][...] for q in range(4))
            delta, m2, v2 = _adamw_math(w, g, m, v)
            refs[4 * n + a][...] = delta
            refs[5 * n + a][...] = m2
            refs[6 * n + a][...] = v2

    shapes = tuple(jax.ShapeDtypeStruct(w.shape, F32) for w in ws)
    outs = pl.pallas_call(
        body, name=name, out_shape=shapes * 3, in_specs=[_vmem()] * (4 * n), out_specs=tuple([_vmem()] * (3 * n)),
        compiler_params=_params(),
    )(*ws, *gs, *ms, *vs)
    return outs[:n], outs[n:2 * n], outs[2 * n:]


def _adamw_ada(w, m, v, c_all, dmod_rows):
    def body(w_ref, m_ref, v_ref, c_ref, dm_ref, g_ref, d_ref, m2_ref, v2_ref):
        g = _dot_tn(c_ref[...], dm_ref[...].astype(BF))
        g_ref[...] = g
        delta, m2, v2 = _adamw_math(w_ref[...], g, m_ref[...], v_ref[...])
        d_ref[...] = delta
        m2_ref[...] = m2
        v2_ref[...] = v2

    shp = jax.ShapeDtypeStruct(w.shape, F32)
    return pl.pallas_call(
        body, name="adamw_ada", out_shape=(shp, shp, shp, shp), in_specs=[_vmem()] * 5,
        out_specs=tuple([_vmem()] * 4), compiler_params=_params(),
    )(w, m, v, c_all, dmod_rows)


def _w_in_to_kernel(w):
    return jnp.concatenate([w[:, 0:448], jnp.zeros((w.shape[0], 64), w.dtype), w[:, 448:960]], axis=1)


def _w_in_from_kernel(w):
    return jnp.concatenate([w[:, 0:448], w[:, 512:1024]], axis=1)


def _w_uq_to_kernel(w):
    r = w.shape[0]
    return jnp.concatenate([w[:, :, 0:NOPE].reshape(r, HEADS * NOPE),
                            w[:, :, NOPE:NOPE + HALF].reshape(r, HEADS * HALF),
                            w[:, :, NOPE + HALF:].reshape(r, HEADS * HALF)], axis=1)


def _w_uq_from_kernel(w):
    r = w.shape[0]
    return jnp.concatenate([w[:, 0:512].reshape(r, HEADS, NOPE), w[:, 512:640].reshape(r, HEADS, HALF),
                            w[:, 640:768].reshape(r, HEADS, HALF)], axis=2)


REP_NAMES = ("w_uk", "w_uv", "w_pool", "g_mix", "g_q", "g_kv", "pool_scale", "g_ffn", "g_final")


def kernel(x, c, positions, w_ada, b_ada, g_mix, w_in, g_q, g_kv, w_uq, w_uk, w_uv, w_pool, pool_scale, w_o, g_ffn, w_gate, w_up, w_down, g_final, loss_target, m_w_ada, m_b_ada, m_g_mix, m_w_in, m_g_q, m_g_kv, m_w_uq, m_w_uk, m_w_uv, m_w_pool, m_pool_scale, m_w_o, m_g_ffn, m_w_gate, m_w_up, m_w_down, m_g_final, v_w_ada, v_b_ada, v_g_mix, v_w_in, v_g_q, v_g_kv, v_w_uq, v_w_uk, v_w_uv, v_w_pool, v_pool_scale, v_w_o, v_g_ffn, v_w_gate, v_w_up, v_w_down, v_g_final):
    given = dict(locals())

    merge = lambda g: g.reshape(NDEV * g.shape[1], g.shape[2])
    w_in_p, w_uq_p = (merge(g) for g in _sequencer_gather(
        "gather_in", 3, (_w_in_to_kernel(w_in[0]).astype(BF), _w_uq_to_kernel(w_uq[0]).astype(BF))))

    mod, c_all8 = _ada_mod(c, w_ada[0], b_ada)
    c_all = c_all8[:, 0, :]
    late = _sequencer_gather(
        "gather_late", 1, (w_o[0].astype(BF), w_gate[0].T.astype(BF), w_up[0].T.astype(BF), w_down[0].astype(BF)),
        after=(mod[:, 0:128], w_in_p[0:16, 0:128], w_uq_p[0:16, 0:128]))

    def ffn_grads_exchange(arrays):
        return _sequencer_scatter("scatter_ffn", 2, arrays)

    loss, dx, dmod, tail_grads, ffn_parts, replicated = _local_step(
        x[0], positions[0], loss_target[0], mod, g_mix, w_in_p, g_q, g_kv, w_uq_p, w_uk[0], w_uv[0], w_pool[0],
        pool_scale, g_ffn, g_final.reshape(1, D), tuple(merge(g) for g in late), ffn_grads_exchange)

    flat = jnp.concatenate([replicated[k].reshape(-1) for k in REP_NAMES] + [loss.reshape(1)])
    flat = jnp.pad(flat, (0, NDEV * REP_ROWS * 128 - flat.shape[0])).reshape(NDEV, REP_ROWS, 128)
    dmod_blocks = jnp.pad(dmod.reshape(NDEV, MODC // 128, 128), ((0, 0), (0, MOD_ROWS - MODC // 128), (0, 0)))
    got, red = _small_all_reduce(jnp.concatenate([dmod_blocks, flat], axis=1))

    tail_parts = _sequencer_scatter("scatter_tail", 4, tail_grads,
                                    after=(ffn_parts[0][0, 0:16, 0:128], red[0, 0:8, :]))
    g_gate_t, g_up_t, g_down = _sum_partials("sum_ffn_partials", ffn_parts)
    g_in_p, g_uq_p, g_o = _sum_partials("sum_tail_partials", tail_parts)
    grads = dict(w_in=_w_in_from_kernel(g_in_p), w_uq=_w_uq_from_kernel(g_uq_p).reshape(QL // NDEV, HEADS * 192),
                 w_o=g_o, w_gate=g_gate_t.T, w_up=g_up_t.T, w_down=g_down)
    dmod_rows = got[:, 0:MODC // 128, :].reshape(NDEV, MODC)
    grads["b_ada"] = red[:, 0:MODC // 128, :].reshape(1, N_MOD * D)
    rep_flat = red[:, MOD_ROWS:, :].reshape(-1)
    off = 0
    for k in REP_NAMES:
        size = int(np.prod(given[k].shape))
        grads[k] = rep_flat[off:off + size]
        off += size

    view = dict(w_ada=(D, MODC), b_ada=(1, N_MOD * D), g_mix=(1, D), w_in=(D // NDEV, 960), g_q=(1, QL),
                g_kv=(1, KVL), w_uq=(QL // NDEV, HEADS * 192), w_uk=(KVL, HEADS * NOPE), w_uv=(KVL, HEADS * 128),
                w_pool=(GROUPS * GD, GD), pool_scale=(1, PW), w_o=(D // NDEV, D), g_ffn=(1, D),
                w_gate=(D, FF // NDEV), w_up=(D, FF // NDEV), w_down=(FF // NDEV, D), g_final=(1, D))
    names = list(view)
    g_ada, d_ada, m_ada, v_ada = _adamw_ada(w_ada[0], m_w_ada[0], v_w_ada[0], c_all.astype(BF), dmod_rows)
    out_g, out_d, out_m, out_v = dict(w_ada=g_ada), dict(w_ada=d_ada), dict(w_ada=m_ada), dict(w_ada=v_ada)
    tail = ("w_in", "w_uq", "w_o")
    groups = (("adamw_ffn", ("w_gate", "w_up", "w_down")),
              ("adamw_replicated", tuple(k for k in names if k not in ("w_ada", "w_gate", "w_up", "w_down") + tail)),
              ("adamw_tail", tail))
    for gname, members in groups:
        ws = [given[k].reshape(view[k]) for k in members]
        gs = [grads[k].reshape(view[k]) for k in members]
        ms = [given["m_" + k].reshape(view[k]) for k in members]
        vs = [given["v_" + k].reshape(view[k]) for k in members]
        ds, m2, v2 = _adamw_group(gname, ws, gs, ms, vs)
        for k, g, d, mm, vv in zip(members, gs, ds, m2, v2):
            out_g[k], out_d[k], out_m[k], out_v[k] = g, d, mm, vv

    total = rep_flat[off]
    shaped = lambda d: [d[k].reshape(given[k].shape) for k in names]
    return (total, dx[None], *shaped(out_g), *shaped(out_d), *shaped(out_m), *shaped(out_v))
```

```python
import numpy as np
import jax
import jax.numpy as jnp
from jax import lax
from jax.experimental import pallas as pl
from jax.experimental.pallas import tpu as pltpu
from jax.experimental.pallas import tpu_sc as plsc

D = 1024
HEADS = 4
NOPE = 128
ROPE = 64
HALF = ROPE // 2
QL = 256
KVL = 128
FF = 2816
PW = 512
GROUPS = 4
GD = 128
N_MOD = 6
EPS = 1e-6
SM_SCALE = (NOPE + ROPE) ** -0.5
ROPE_THETA = 10000.0
NDEV = 8
MODC = N_MOD * D // NDEV

ADAM_LR = 0.001
ADAM_B1 = 0.9
ADAM_B2 = 0.999
ADAM_EPS = 1e-08
ADAM_WD = 0.01
ADAM_STEP = 10

BF = jnp.bfloat16
F32 = jnp.float32
VMEM_LIMIT_V7X = 60 * 1024 * 1024
MESH = pl.DeviceIdType.MESH

TQ = 256
TK = 256
QW = 256
MOD_ROWS = 8
REP_ROWS = 200
SMALL_ROWS = MOD_ROWS + REP_ROWS


def _params(sem=None):
    return pltpu.CompilerParams(dimension_semantics=sem, vmem_limit_bytes=VMEM_LIMIT_V7X)


def _dot(a, b):
    return jnp.dot(a, b, preferred_element_type=F32)


def _dot_nt(a, b):
    return lax.dot_general(a, b, (((1,), (1,)), ((), ())), preferred_element_type=F32)


def _dot_tn(a, b):
    return _dot(a.astype(F32).T.astype(BF), b)


def _full(shape):
    return pl.BlockSpec(shape, lambda *_: (0,) * len(shape))


def _rows(ts, cols):
    return pl.BlockSpec((ts, cols), lambda i: (i, 0))


def _vmem():
    return pl.BlockSpec(memory_space=pltpu.VMEM)


def _any():
    return pl.BlockSpec(memory_space=pl.ANY)


def _rms(v):
    return lax.rsqrt(jnp.mean(v * v, axis=-1, keepdims=True) + EPS)


def _rms_bwd(dn, n, r):
    return r * (dn - n * jnp.mean(dn * n, axis=-1, keepdims=True))


def _colsum(v):
    return jnp.sum(v, axis=0, keepdims=True)


def _swap_halves(v):
    lane = lax.broadcasted_iota(jnp.int32, v.shape, 1)
    return jnp.where(lane < HALF, pltpu.roll(v, 128 - HALF, 1), pltpu.roll(v, HALF, 1))


def _window_lane_width():
    lane = lax.broadcasted_iota(jnp.int32, (1, PW), 1)
    return jnp.where(lane < 128, 2.0, jnp.where(lane < 256, 4.0, jnp.where(lane < 384, 8.0, 16.0))).astype(F32)


def _window_sums(ext, back):
    n = ext.shape[0]

    def sh(v, k):
        return pltpu.roll(v, k if back else n - k, 0)

    s2 = ext + sh(ext, 1)
    e4 = s2[:, 128:]
    s4 = e4 + sh(e4, 2)
    e8 = s4[:, 128:]
    s8 = e8 + sh(e8, 4)
    e16 = s8[:, 128:]
    s16 = e16 + sh(e16, 8)
    return jnp.concatenate([s2[:, :128], s4[:, :128], s8[:, :128], s16], axis=1)


def _row_counts(first_row, ts):
    t1 = (first_row + lax.broadcasted_iota(jnp.int32, (ts, 1), 0) + 1).astype(F32)
    return jnp.minimum(t1, _window_lane_width())


def _fwd_in(x, mod, g_mix, w_in, g_q, g_kv, w_uq, wuk_dc, perm, cos4, sin4, csk, snk, w_pool, pool_scale):
    S = x.shape[0]
    ts = 512
    nsub = ts // TQ

    def body(x_ref, mod_ref, gmix_ref, win_ref, gq_ref, gkv_ref, wuq_ref, wuk_ref, perm_ref, cos_ref, sin_ref,
             csk_ref, snk_ref, wpool_ref, pscale_ref,
             h1_ref, raw_ref, qn_ref, qs_ref, kv_ref, kvt_ref, pooled_ref, ypre_ref, ypool_ref, carry_ref):
        i = pl.program_id(0)

        @pl.when(i == 0)
        def _():
            carry_ref[...] = jnp.zeros_like(carry_ref)

        xv = x_ref[...]
        sh1 = mod_ref[0:1, 0:D]
        sc1 = mod_ref[0:1, D:2 * D]
        h = (xv * _rms(xv)) * gmix_ref[...] * (1.0 + sc1) + sh1
        hb = h.astype(BF)
        h1_ref[...] = hb
        proj = _dot(hb, win_ref[...])
        cq_raw = proj[:, 0:QL]
        ckv_raw = proj[:, QL:QL + KVL]
        kr = proj[:, 384:512]
        u = proj[:, 512:1024]
        raw_ref[...] = proj[:, 0:384]

        c_q = (cq_raw * _rms(cq_raw)) * gq_ref[...]
        c_kv = (ckv_raw * _rms(ckv_raw)) * gkv_ref[...]
        q = _dot(c_q.astype(BF), wuq_ref[...])
        qn = q[:, 0:HEADS * NOPE].astype(BF)
        qn_ref[...] = qn
        x1 = q[:, 512:640]
        x2 = q[:, 640:768]
        cosv = cos_ref[...]
        sinv = sin_ref[...]
        roped = jnp.concatenate([x1 * cosv - x2 * sinv, x1 * sinv + x2 * cosv], axis=1).astype(BF)
        for hd in range(HEADS):
            q_lat = _dot(qn[:, hd * NOPE:(hd + 1) * NOPE], wuk_ref[hd])
            q_rope = _dot(roped, perm_ref[hd])
            qh = jnp.concatenate([q_lat, q_rope], axis=1).astype(BF)
            for a in range(nsub):
                qs_ref[a, hd * TQ:(hd + 1) * TQ, :] = qh[a * TQ:(a + 1) * TQ, :]
        k_rope = kr * csk_ref[...] + _swap_halves(kr) * snk_ref[...]
        keys = jnp.concatenate([c_kv, k_rope], axis=1)
        kv_ref[...] = keys.astype(BF)
        for a in range(ts // TK):
            kvt_ref[a] = keys[a * TK:(a + 1) * TK, :].T.astype(BF)

        ext = jnp.concatenate([carry_ref[...], u], axis=0)
        win = _window_sums(ext, True)[16:, :]
        pooled = (win / _row_counts(i * ts, ts) - u).astype(BF)
        pooled_ref[...] = pooled
        carry_ref[...] = u[ts - 16:ts, :]
        ypre = jnp.concatenate(
            [_dot(pooled[:, g * GD:(g + 1) * GD], wpool_ref[g]) for g in range(GROUPS)], axis=1)
        ypre_ref[...] = ypre
        ypool_ref[...] = (ypre * pscale_ref[...]).astype(BF)

    out_shape = (
        jax.ShapeDtypeStruct((S, D), BF),
        jax.ShapeDtypeStruct((S, 384), F32),
        jax.ShapeDtypeStruct((S, HEADS * NOPE), BF),
        jax.ShapeDtypeStruct((S // TQ, HEADS * TQ, QW), BF),
        jax.ShapeDtypeStruct((S, QW), BF),
        jax.ShapeDtypeStruct((S // TK, QW, TK), BF),
        jax.ShapeDtypeStruct((S, PW), BF),
        jax.ShapeDtypeStruct((S, PW), F32),
        jax.ShapeDtypeStruct((S, PW), BF),
    )
    in_specs = [
        _rows(ts, D), _full(mod.shape), _full((1, D)), _full(w_in.shape), _full((1, QL)), _full((1, KVL)),
        _full(w_uq.shape), _full(wuk_dc.shape), _full(perm.shape), _rows(ts, 128), _rows(ts, 128), _rows(ts, 128),
        _rows(ts, 128), _full(w_pool.shape), _full((1, PW)),
    ]
    out_specs = (
        _rows(ts, D), _rows(ts, 384), _rows(ts, HEADS * NOPE),
        pl.BlockSpec((nsub, HEADS * TQ, QW), lambda i: (i, 0, 0)),
        _rows(ts, QW), pl.BlockSpec((ts // TK, QW, TK), lambda i: (i, 0, 0)), _rows(ts, PW), _rows(ts, PW),
        _rows(ts, PW),
    )
    return pl.pallas_call(
        body, name="fwd_in", out_shape=out_shape, grid=(S // ts,), in_specs=in_specs, out_specs=out_specs,
        scratch_shapes=[pltpu.VMEM((16, PW), F32)], compiler_params=_params(("arbitrary",)),
    )(x, mod, g_mix, w_in, g_q, g_kv, w_uq, wuk_dc, perm, cos4, sin4, csk, snk, w_pool, pool_scale)


def _diag_mask(shape, q_axis):
    qi = (lax.broadcasted_iota(jnp.int32, shape, q_axis) & (TQ - 1)) >> 6
    ki = lax.broadcasted_iota(jnp.int32, shape, 1 - q_axis) >> 6
    return ki <= qi


def _attn_fwd(qs, kv, kvt, wuv_vc):
    nq = qs.shape[0]
    S = kv.shape[0]
    M = HEADS * TQ

    def body(qs_ref, kv_ref, kvt_ref, wuv_ref, olat_ref, ymla_ref, lse_ref):
        i = pl.program_id(0)
        q = qs_ref[0]

        def step(kt, carry, masked):
            m, l, acc = carry
            k = kv_ref[pl.ds(pl.multiple_of(kt * TK, TK), TK), :]
            v_t = kvt_ref[kt][0:KVL, :]
            s = _dot_nt(k, q) * SM_SCALE
            if masked:
                s = jnp.where(_diag_mask((TK, M), 1), s, -jnp.inf)
            m_new = jnp.maximum(m, jnp.max(s, axis=0, keepdims=True))
            alpha = jnp.exp(m - m_new)
            p = jnp.exp(s - m_new)
            l = alpha * l + jnp.sum(p, axis=0, keepdims=True)
            acc = alpha * acc + _dot(v_t, p.astype(BF))
            return m_new, l, acc

        init = (jnp.full((1, M), -jnp.inf, F32), jnp.zeros((1, M), F32), jnp.zeros((KVL, M), F32))
        carry = lax.fori_loop(0, i, lambda kt, c: step(kt, c, False), init)
        m, l, acc = step(i, carry, True)
        o_lat = acc / l
        olat_ref[0] = o_lat
        lse_ref[0] = jnp.broadcast_to(m + jnp.log(l), (8, M))
        for hd in range(HEADS):
            o_t = _dot(wuv_ref[hd], o_lat[:, hd * TQ:(hd + 1) * TQ].astype(BF))
            ymla_ref[:, hd * 128:(hd + 1) * 128] = o_t.T.astype(BF)

    out_shape = (
        jax.ShapeDtypeStruct((nq, KVL, M), F32),
        jax.ShapeDtypeStruct((S, HEADS * 128), BF),
        jax.ShapeDtypeStruct((nq, 8, M), F32),
    )
    return pl.pallas_call(
        body, name="attn_fwd", out_shape=out_shape, grid=(nq,),
        in_specs=[pl.BlockSpec((1, M, QW), lambda i: (i, 0, 0)), _full(kv.shape), _full(kvt.shape),
                  _full(wuv_vc.shape)],
        out_specs=(pl.BlockSpec((1, KVL, M), lambda i: (i, 0, 0)), _rows(TQ, HEADS * 128),
                   pl.BlockSpec((1, 8, M), lambda i: (i, 0, 0))),
        compiler_params=_params(("arbitrary",)),
    )(qs, kv, kvt, wuv_vc)


def _silu_parts(a):
    sg = jax.nn.sigmoid(a)
    return sg, a * sg


def _ffn_fwd(x, ymla, ypool, mod, w_o, g_ffn, wg_t, wu_t, wd, g_final, target):
    S = x.shape[0]
    ts = 256

    def body(x_ref, ymla_ref, ypool_ref, mod_ref, wo_ref, gffn_ref, wg_ref, wu_ref, wd_ref, gfin_ref, t_ref,
             x2_ref, mix_ref, h2t_ref, a_ref, b_ref, dx3_ref, dff_ref, dfft_ref, loss_ref, dgfin_ref, dgt2_ref,
             f_ref):
        i = pl.program_id(0)

        @pl.when(i == 0)
        def _():
            loss_ref[...] = jnp.zeros_like(loss_ref)
            dgfin_ref[...] = jnp.zeros_like(dgfin_ref)
            dgt2_ref[...] = jnp.zeros_like(dgt2_ref)

        gt1 = mod_ref[0:1, 2 * D:3 * D]
        sh2 = mod_ref[0:1, 3 * D:4 * D]
        sc2 = mod_ref[0:1, 4 * D:5 * D]
        gt2 = mod_ref[0:1, 5 * D:6 * D]
        cat = jnp.concatenate([ymla_ref[...], ypool_ref[...]], axis=1)
        mix = _dot(cat, wo_ref[...])
        mix_ref[...] = mix
        x2 = x_ref[...] + gt1 * mix
        x2_ref[...] = x2
        h2 = (x2 * _rms(x2)) * gffn_ref[...] * (1.0 + sc2) + sh2
        h2b = h2.astype(BF)
        h2t_ref[...] = h2.T.astype(BF)

        for c in range(FF // FCHUNK):
            cols = slice(c * FCHUNK, (c + 1) * FCHUNK)
            a = _dot_nt(h2b, wg_ref[cols, :])
            b = _dot_nt(h2b, wu_ref[cols, :])
            a_ref[:, cols] = a.astype(BF)
            b_ref[:, cols] = b.astype(BF)
            f_ref[:, cols] = (_silu_parts(a)[1] * b).astype(BF)
        ff = _dot(f_ref[...], wd_ref[...])

        x3 = x2 + gt2 * ff
        r3 = _rms(x3)
        xn3 = x3 * r3
        gfin = gfin_ref[...]
        e = xn3 * gfin - t_ref[...]
        loss_ref[...] += 0.5 * jnp.sum(jnp.mean(e * e, axis=-1, keepdims=True))
        dy = e * (1.0 / D)
        dgfin_ref[...] += _colsum(dy * xn3)
        dx3 = _rms_bwd(dy * gfin, xn3, r3)
        dx3_ref[...] = dx3
        dgt2_ref[...] += _colsum(dx3 * ff)
        dff = dx3 * gt2
        dff_ref[...] = dff.astype(BF)
        dfft_ref[...] = dff.T.astype(BF)

    row = lambda c: _rows(ts, c)
    col = pl.BlockSpec((D, ts), lambda i: (0, i))
    const = _full
    out_shape = (
        jax.ShapeDtypeStruct((S, D), F32),
        jax.ShapeDtypeStruct((S, D), F32),
        jax.ShapeDtypeStruct((D, S), BF),
        jax.ShapeDtypeStruct((S, FF), BF),
        jax.ShapeDtypeStruct((S, FF), BF),
        jax.ShapeDtypeStruct((S, D), F32),
        jax.ShapeDtypeStruct((S, D), BF),
        jax.ShapeDtypeStruct((D, S), BF),
        jax.ShapeDtypeStruct((8, 128), F32),
        jax.ShapeDtypeStruct((1, D), F32),
        jax.ShapeDtypeStruct((1, D), F32),
    )
    return pl.pallas_call(
        body, name="ffn_fwd", out_shape=out_shape, grid=(S // ts,),
        in_specs=[row(D), row(PW), row(PW), const(mod.shape), _vmem(), const((1, D)), _vmem(), _vmem(), _vmem(),
                  const((1, D)), row(D)],
        out_specs=(row(D), row(D), col, row(FF), row(FF), row(D), row(D), col, const((8, 128)), const((1, D)),
                   const((1, D))),
        scratch_shapes=[pltpu.VMEM((ts, FF), BF)],
        compiler_params=_params(("arbitrary",)),
    )(x, ymla, ypool, mod, w_o, g_ffn, wg_t, wu_t, wd, g_final, target)


FCHUNK = 256


def _ffn_bwd_acts(dff, a, b, wg_t, wu_t, wd):
    S = dff.shape[0]
    ts = 512

    def body(dff_ref, a_ref, b_ref, wg_ref, wu_ref, wd_ref, da_ref, db_ref, dh2_ref):
        dffb = dff_ref[...]
        for c in range(FF // FCHUNK):
            cols = slice(c * FCHUNK, (c + 1) * FCHUNK)
            df = _dot_nt(dffb, wd_ref[cols, :])
            av = a_ref[:, cols].astype(F32)
            bv = b_ref[:, cols].astype(F32)
            sg, sa = _silu_parts(av)
            db_ref[:, cols] = (df * sa).astype(BF)
            da_ref[:, cols] = (df * bv * (sg * (1.0 + av * (1.0 - sg)))).astype(BF)
        dh2_ref[...] = _dot(da_ref[...], wg_ref[...]) + _dot(db_ref[...], wu_ref[...])

    act = _rows(ts, FF)
    return pl.pallas_call(
        body, name="ffn_bwd_acts",
        out_shape=(jax.ShapeDtypeStruct((S, FF), BF), jax.ShapeDtypeStruct((S, FF), BF),
                   jax.ShapeDtypeStruct((S, D), F32)),
        grid=(S // ts,), in_specs=[_rows(ts, D), act, act, _vmem(), _vmem(), _vmem()],
        out_specs=(act, act, _rows(ts, D)), compiler_params=_params(("arbitrary",)),
    )(dff, a, b, wg_t, wu_t, wd)


def _ffn_bwd_weights(dff_t, h2_t, da, db, a, b):
    S = da.shape[0]

    def body(dfft_ref, h2t_ref, da_ref, db_ref, a_ref, b_ref, dwg_ref, dwu_ref, dwd_ref):
        h2t = h2t_ref[...]
        dwg_ref[...] = _dot(h2t, da_ref[...]).T.astype(BF)
        dwu_ref[...] = _dot(h2t, db_ref[...]).T.astype(BF)
        f = (_silu_parts(a_ref[...].astype(F32))[1] * b_ref[...].astype(F32)).astype(BF)
        dwd_ref[...] = _dot(dfft_ref[...], f).T.astype(BF)

    act = pl.BlockSpec((S, FCHUNK), lambda j: (0, j))
    wblk = _rows(FCHUNK, D)
    shp = jax.ShapeDtypeStruct((FF, D), BF)
    return pl.pallas_call(
        body, name="ffn_bwd_weights", out_shape=(shp, shp, shp), grid=(FF // FCHUNK,),
        in_specs=[_vmem(), _vmem(), act, act, act, act], out_specs=(wblk, wblk, wblk),
        compiler_params=_params(("arbitrary",)),
    )(dff_t, h2_t, da, db, a, b)


def _mix_bwd(dh2, dx3, x2, mix, mod, g_ffn, ymla, ypool, w_o, ypre, pooled, pool_scale, wpool_dc, olat, wuv_vc):
    S = dh2.shape[0]
    ts = 512
    n = S // ts
    nsub = ts // TQ
    M = HEADS * TQ

    def body(dh2_ref, dx3_ref, x2_ref, mix_ref, mod_ref, gffn_ref, ymla_ref, ypool_ref, wo_ref, ypre_ref, pooled_ref,
             pscale_ref, wpool_ref, olat_ref, wuv_ref,
             dx2_ref, du_ref, dolat_ref, delta_ref, dwo_ref, dwuv_ref, dwpool_ref, dpscale_ref, dgt1_ref, dsc2_ref,
             dsh2_ref, dgffn_ref, carry_ref, dwo_acc):
        i = pl.program_id(0)

        @pl.when(i == 0)
        def _():
            carry_ref[...] = jnp.zeros_like(carry_ref)
            dwo_acc[...] = jnp.zeros_like(dwo_acc)
            for r in (dwuv_ref, dwpool_ref, dpscale_ref, dgt1_ref, dsc2_ref, dsh2_ref, dgffn_ref):
                r[...] = jnp.zeros_like(r)

        gt1 = mod_ref[0:1, 2 * D:3 * D]
        sc2 = mod_ref[0:1, 4 * D:5 * D]
        gffn = gffn_ref[...]
        dh2 = dh2_ref[...]
        x2 = x2_ref[...]
        r2 = _rms(x2)
        xn2 = x2 * r2
        dsc2_ref[...] += _colsum(dh2 * (xn2 * gffn))
        dsh2_ref[...] += _colsum(dh2)
        dgffn_ref[...] += _colsum(dh2 * (1.0 + sc2) * xn2)
        dx2 = dx3_ref[...] + _rms_bwd(dh2 * gffn * (1.0 + sc2), xn2, r2)
        dx2_ref[...] = dx2
        dgt1_ref[...] += _colsum(dx2 * mix_ref[...])
        dmix = (dx2 * gt1).astype(BF)
        cat = jnp.concatenate([ymla_ref[...], ypool_ref[...]], axis=1)
        dwo_acc[...] += _dot_tn(cat, dmix)
        dcat = _dot_nt(dmix, wo_ref[...])
        dymla = dcat[:, 0:512]
        dypool = dcat[:, 512:1024]

        dpscale_ref[...] += _colsum(dypool * ypre_ref[...])
        dypre = (dypool * pscale_ref[...]).astype(BF)
        pooled = pooled_ref[...]
        dpooled = []
        for g in range(GROUPS):
            sl = slice(g * GD, (g + 1) * GD)
            dwpool_ref[g] += _dot_tn(pooled[:, sl], dypre[:, sl])
            dpooled.append(_dot(dypre[:, sl], wpool_ref[g]))
        dpooled = jnp.concatenate(dpooled, axis=1)
        tile = n - 1 - i
        e = dpooled / _row_counts(tile * ts, ts)
        ext = jnp.concatenate([e, carry_ref[...]], axis=0)
        du_ref[...] = _window_sums(ext, False)[0:ts, :] - dpooled
        carry_ref[...] = e[0:16, :]

        for hd in range(HEADS):
            do = dymla[:, hd * 128:(hd + 1) * 128]
            dob = do.astype(BF)
            dol = _dot(dob, wuv_ref[hd])
            for a in range(nsub):
                ol_t = olat_ref[a, :, hd * TQ:(hd + 1) * TQ]
                dl = dol[a * TQ:(a + 1) * TQ, :]
                dolat_ref[a, hd * TQ:(hd + 1) * TQ, :] = dl.astype(BF)
                dwuv_ref[hd] += _dot(ol_t.astype(BF), dob[a * TQ:(a + 1) * TQ, :])
                delta = jnp.sum(dl * ol_t.T, axis=-1, keepdims=True)
                delta_ref[a, :, hd * TQ:(hd + 1) * TQ] = jnp.broadcast_to(delta, (TQ, 128)).T[0:8, :]

        @pl.when(i == n - 1)
        def _():
            dwo_ref[...] = dwo_acc[...].astype(BF)

    rev = lambda c: pl.BlockSpec((ts, c), lambda i: (n - 1 - i, 0))
    rev3 = lambda r, c: pl.BlockSpec((nsub, r, c), lambda i: (n - 1 - i, 0, 0))
    out_shape = (
        jax.ShapeDtypeStruct((S, D), F32),
        jax.ShapeDtypeStruct((S, PW), F32),
        jax.ShapeDtypeStruct((S // TQ, M, KVL), BF),
        jax.ShapeDtypeStruct((S // TQ, 8, M), F32),
        jax.ShapeDtypeStruct((D, D), BF),
        jax.ShapeDtypeStruct((HEADS, KVL, 128), F32),
        jax.ShapeDtypeStruct((GROUPS, GD, GD), F32),
        jax.ShapeDtypeStruct((1, PW), F32),
        jax.ShapeDtypeStruct((1, D), F32), jax.ShapeDtypeStruct((1, D), F32), jax.ShapeDtypeStruct((1, D), F32),
        jax.ShapeDtypeStruct((1, D), F32),
    )
    in_specs = [rev(D), rev(D), rev(D), rev(D), _full(mod.shape), _full((1, D)), rev(PW), rev(PW), _full(w_o.shape),
                rev(PW), rev(PW), _full((1, PW)), _full(wpool_dc.shape), rev3(KVL, M), _full(wuv_vc.shape)]
    out_specs = (rev(D), rev(PW), rev3(M, KVL), rev3(8, M), _full((D, D)), _full((HEADS, KVL, 128)),
                 _full((GROUPS, GD, GD)), _full((1, PW)), _full((1, D)), _full((1, D)), _full((1, D)), _full((1, D)))
    return pl.pallas_call(
        body, name="mix_bwd", out_shape=out_shape, grid=(n,), in_specs=in_specs, out_specs=out_specs,
        scratch_shapes=[pltpu.VMEM((16, PW), F32), pltpu.VMEM((D, D), F32)],
        compiler_params=_params(("arbitrary",)),
    )(dh2, dx3, x2, mix, mod, g_ffn, ymla, ypool, w_o, ypre, pooled, pool_scale, wpool_dc, olat, wuv_vc)


def _attn_bwd(qs, kv, dolat, lse, delta):
    nq = qs.shape[0]
    S = kv.shape[0]
    M = HEADS * TQ
    nk = S // TK

    def body(qs_ref, kv_ref, do_ref, lse_ref, delta_ref, dkv_ref, dqt_ref):
        kt = pl.program_id(0)
        k = kv_ref[...]
        v = k[:, 0:KVL]
        k_t = k.astype(F32).T.astype(BF)

        @pl.when(kt == 0)
        def _():
            dqt_ref[...] = jnp.zeros_like(dqt_ref)

        def step(qi, carry, masked):
            dk, dv = carry
            q = qs_ref[qi]
            do = do_ref[qi]
            s = _dot_nt(k, q) * SM_SCALE
            p = jnp.exp(s - lse_ref[qi, 0:1, :])
            if masked:
                p = jnp.where(_diag_mask((TK, M), 1), p, 0.0)
            dp = _dot_nt(v, do)
            ds = (p * (dp - delta_ref[qi, 0:1, :]) * SM_SCALE).astype(BF)
            dv = dv + _dot(p.astype(BF), do)
            dk = dk + _dot(ds, q)
            dqt_ref[qi] += _dot(k_t, ds)
            return dk, dv

        carry = step(kt, (jnp.zeros((TK, QW), F32), jnp.zeros((TK, KVL), F32)), True)
        dk, dv = lax.fori_loop(kt + 1, nq, lambda qi, c: step(qi, c, False), carry)
        dkv_ref[...] = dk + jnp.concatenate([dv, jnp.zeros((TK, QW - KVL), F32)], axis=1)

    out_shape = (jax.ShapeDtypeStruct((S, QW), F32), jax.ShapeDtypeStruct((nq, QW, M), F32))
    return pl.pallas_call(
        body, name="attn_bwd", out_shape=out_shape, grid=(nk,),
        in_specs=[_vmem(), _rows(TK, QW), _vmem(), _vmem(), _vmem()],
        out_specs=(_rows(TK, QW), _vmem()),
        compiler_params=_params(("arbitrary",)),
    )(qs, kv, dolat, lse, delta)


def _in_bwd(dqt, dkv, du, raw, qn, h1, x, dx2, mod, g_mix, w_in, g_q, g_kv, w_uq, wuk_cd, perm_t, cos4, sin4, csk,
            snk):
    S = x.shape[0]
    ts = 512
    n = S // ts
    nsub = ts // TQ
    M = HEADS * TQ

    def body(dqt_ref, dkv_ref, du_ref, raw_ref, qn_ref, h1_ref, x_ref, dx2_ref, mod_ref, gmix_ref, win_ref, gq_ref,
             gkv_ref, wuq_ref, wuk_ref, permt_ref, cos_ref, sin_ref, csk_ref, snk_ref,
             dx_ref, dwin_ref, dwuq_ref, dwuk_ref, dgq_ref, dgkv_ref, dsc1_ref, dsh1_ref, dgmix_ref, dwin_acc,
             dwuq_acc):
        i = pl.program_id(0)

        @pl.when(i == 0)
        def _():
            dwin_acc[...] = jnp.zeros_like(dwin_acc)
            dwuq_acc[...] = jnp.zeros_like(dwuq_acc)
            for r in (dwuk_ref, dgq_ref, dgkv_ref, dsc1_ref, dsh1_ref, dgmix_ref):
                r[...] = jnp.zeros_like(r)

        dq_blocks = [dqt_ref[a].T for a in range(nsub)]
        qn = qn_ref[...]
        dq_parts = []
        drope = jnp.zeros((ts, 2 * 128), F32)
        for hd in range(HEADS):
            dqh = jnp.concatenate([blk[hd * TQ:(hd + 1) * TQ, :] for blk in dq_blocks], axis=0)
            dq_lat = dqh[:, 0:KVL].astype(BF)
            dq_parts.append(_dot(dq_lat, wuk_ref[hd]))
            dwuk_ref[hd] += _dot_tn(dq_lat, qn[:, hd * NOPE:(hd + 1) * NOPE])
            drope = drope + _dot(dqh[:, KVL:QW].astype(BF), permt_ref[hd])
        do1 = drope[:, 0:128]
        do2 = drope[:, 128:256]
        cosv = cos_ref[...]
        sinv = sin_ref[...]
        dq_parts.append(do1 * cosv + do2 * sinv)
        dq_parts.append(do2 * cosv - do1 * sinv)
        dq = jnp.concatenate(dq_parts, axis=1).astype(BF)

        cq_raw = raw_ref[:, 0:QL]
        ckv_raw = raw_ref[:, QL:QL + KVL]
        rq = _rms(cq_raw)
        nq_ = cq_raw * rq
        gq = gq_ref[...]
        dwuq_acc[...] += _dot_tn((nq_ * gq).astype(BF), dq)
        dc_q = _dot_nt(dq, wuq_ref[...])
        dgq_ref[...] += _colsum(dc_q * nq_)
        dcq_raw = _rms_bwd(dc_q * gq, nq_, rq)

        dkv = dkv_ref[...]
        rk = _rms(ckv_raw)
        nk_ = ckv_raw * rk
        dc_kv = dkv[:, 0:KVL]
        dgkv_ref[...] += _colsum(dc_kv * nk_)
        dckv_raw = _rms_bwd(dc_kv * gkv_ref[...], nk_, rk)
        dkr_roped = dkv[:, KVL:QW]
        dkr = dkr_roped * csk_ref[...] - _swap_halves(dkr_roped) * snk_ref[...]

        dproj = jnp.concatenate([dcq_raw, dckv_raw, dkr, du_ref[...]], axis=1).astype(BF)
        dwin_acc[...] += _dot_tn(h1_ref[...], dproj)
        dh1 = _dot_nt(dproj, win_ref[...])

        sc1 = mod_ref[0:1, D:2 * D]
        gmix = gmix_ref[...]
        xv = x_ref[...]
        r1 = _rms(xv)
        xn1 = xv * r1
        dsc1_ref[...] += _colsum(dh1 * (xn1 * gmix))
        dsh1_ref[...] += _colsum(dh1)
        dgmix_ref[...] += _colsum(dh1 * (1.0 + sc1) * xn1)
        dx_ref[...] = dx2_ref[...] + _rms_bwd(dh1 * gmix * (1.0 + sc1), xn1, r1)

        @pl.when(i == n - 1)
        def _():
            dwin_ref[...] = dwin_acc[...].astype(BF)
            dwuq_ref[...] = dwuq_acc[...].astype(BF)

    out_shape = (
        jax.ShapeDtypeStruct((S, D), F32),
        jax.ShapeDtypeStruct((D, D), BF),
        jax.ShapeDtypeStruct((QL, 768), BF),
        jax.ShapeDtypeStruct((HEADS, KVL, NOPE), F32),
        jax.ShapeDtypeStruct((1, QL), F32), jax.ShapeDtypeStruct((1, KVL), F32),
        jax.ShapeDtypeStruct((1, D), F32), jax.ShapeDtypeStruct((1, D), F32), jax.ShapeDtypeStruct((1, D), F32),
    )
    in_specs = [pl.BlockSpec((nsub, QW, M), lambda i: (i, 0, 0)), _rows(ts, QW), _rows(ts, PW), _rows(ts, 384),
                _rows(ts, HEADS * NOPE), _rows(ts, D), _rows(ts, D), _rows(ts, D), _full(mod.shape), _full((1, D)),
                _full(w_in.shape), _full((1, QL)), _full((1, KVL)), _full(w_uq.shape), _full(wuk_cd.shape),
                _full(perm_t.shape), _rows(ts, 128), _rows(ts, 128), _rows(ts, 128), _rows(ts, 128)]
    out_specs = (_rows(ts, D), _full((D, D)), _full((QL, 768)), _full((HEADS, KVL, NOPE)), _full((1, QL)),
                 _full((1, KVL)), _full((1, D)), _full((1, D)), _full((1, D)))
    return pl.pallas_call(
        body, name="in_bwd", out_shape=out_shape, grid=(n,), in_specs=in_specs, out_specs=out_specs,
        scratch_shapes=[pltpu.VMEM((D, D), F32), pltpu.VMEM((QL, 768), F32)],
        compiler_params=_params(("arbitrary",)),
    )(dqt, dkv, du, raw, qn, h1, x, dx2, mod, g_mix, w_in, g_q, g_kv, w_uq, wuk_cd, perm_t, cos4, sin4, csk, snk)


def _rope_perm():
    p = np.zeros((HEADS, 2 * 128, 128), np.float32)
    for hd in range(HEADS):
        for t in range(HALF):
            p[hd, hd * HALF + t, t] = 1.0
            p[hd, 128 + hd * HALF + t, HALF + t] = 1.0
    return p


def _rope_tables(positions):
    freqs = jnp.power(ROPE_THETA, -jnp.arange(HALF, dtype=F32) / HALF)
    ang = positions.astype(F32)[:, None] * jnp.tile(freqs, HEADS)[None, :]
    cos4 = jnp.cos(ang)
    sin4 = jnp.sin(ang)
    lane = jnp.arange(HEADS * HALF)[None, :]
    csk = jnp.where(lane < ROPE, cos4, 0.0)
    snk = jnp.where(lane < HALF, -sin4, jnp.where(lane < ROPE, sin4, 0.0))
    return cos4, sin4, csk, snk


def _local_step(x, positions, target, mod, g_mix, w_in_p, g_q, g_kv, w_uq_p, w_uk, w_uv, w_pool, pool_scale, g_ffn,
                g_final, late, ffn_grads_exchange):
    perm = jnp.asarray(_rope_perm(), BF)
    perm_t = jnp.asarray(_rope_perm().transpose(0, 2, 1), BF)
    cos4, sin4, csk, snk = _rope_tables(positions)
    wuk_dc = w_uk.transpose(1, 2, 0).astype(BF)
    wuk_cd = w_uk.transpose(1, 0, 2).astype(BF)
    wuv_vc = w_uv.transpose(1, 2, 0).astype(BF)
    wpool = w_pool.astype(BF)
    wpool_dc = w_pool.transpose(0, 2, 1).astype(BF)

    h1, raw, qn, qs, kv, kvt, pooled, ypre, ypool = _fwd_in(
        x, mod, g_mix, w_in_p, g_q, g_kv, w_uq_p, wuk_dc, perm, cos4, sin4, csk, snk, wpool, pool_scale)
    olat, ymla, lse = _attn_fwd(qs, kv, kvt, wuv_vc)
    w_o, wg_t, wu_t, wd = late
    x2, mix, h2_t, a, b, dx3, dff, dff_t, loss, dgfin, dgt2 = _ffn_fwd(
        x, ymla, ypool, mod, w_o, g_ffn, wg_t, wu_t, wd, g_final, target)
    da, db, dh2 = _ffn_bwd_acts(dff, a, b, wg_t, wu_t, wd)
    dwg_t, dwu_t, dwd = _ffn_bwd_weights(dff_t, h2_t, da, db, a, b)
    ffn_parts = ffn_grads_exchange((dwg_t, dwu_t, dwd))
    (dx2, du, dolat, delta, dwo, dwuv, dwpool, dpscale, dgt1, dsc2, dsh2, dgffn) = _mix_bwd(
        dh2, dx3, x2, mix, mod, g_ffn, ymla, ypool, w_o, ypre, pooled, pool_scale, wpool_dc, olat, wuv_vc)
    dkv, dqt = _attn_bwd(qs, kv, dolat, lse, delta)
    dx, dwin, dwuq, dwuk, dgq, dgkv, dsc1, dsh1, dgmix = _in_bwd(
        dqt, dkv, du, raw, qn, h1, x, dx2, mod, g_mix, w_in_p, g_q, g_kv, w_uq_p, wuk_cd, perm_t, cos4, sin4, csk,
        snk)
    dmod = jnp.concatenate([dsh1, dsc1, dgt1, dsh2, dsc2, dgt2], axis=1)
    replicated = dict(
        w_uk=dwuk.transpose(1, 0, 2), w_uv=dwuv.transpose(1, 0, 2), w_pool=dwpool, g_mix=dgmix, g_q=dgq, g_kv=dgkv,
        pool_scale=dpscale, g_ffn=dgffn, g_final=dgfin)
    return loss[0, 0], dx, dmod, (dwin, dwuq, dwo), ffn_parts, replicated


def _my_pos():
    return lax.axis_index("x"), lax.axis_index("y"), lax.axis_index("c")


def _peer(pos, k):
    x, y, c = pos
    return (1 - x if k & 4 else x, 1 - y if k & 2 else y, 1 - c if k & 1 else c)


def _index(pos):
    x, y, c = pos
    return 4 * x + 2 * y + c


def _remote(src, dst, send_sem, recv_sem, to):
    return pltpu.make_async_remote_copy(src_ref=src, dst_ref=dst, send_sem=send_sem, recv_sem=recv_sem,
                                        device_id=to, device_id_type=MESH)


def _ada_mod(c, w_ada, b_ada):
    def body(c_ref, w_ref, b_ref, mod_ref, call_ref, cbuf, sbuf, rbuf, send1, recv1, send2, recv2):
        me = _my_pos()
        mi = _index(me)
        cv = c_ref[...]
        cbuf[...] = jnp.broadcast_to(cv * jax.nn.sigmoid(cv), (8, D))
        call_ref[mi] = cbuf[...]
        first = [_remote(cbuf, call_ref.at[mi], send1.at[k - 1], recv1.at[k - 1], _peer(me, k)) for k in range(1, NDEV)]
        for cp in first:
            cp.start()
        for k in range(1, NDEV):
            _remote(cbuf, call_ref.at[_index(_peer(me, k))], send1.at[k - 1], recv1.at[k - 1], _peer(me, k)).wait_recv()
        c_all = jnp.concatenate([call_ref[b][0:1, :] for b in range(NDEV)], axis=0)
        blocks = _dot(c_all.astype(BF), w_ref[...].astype(BF))
        for b in range(NDEV):
            sbuf[b] = jnp.broadcast_to(blocks[b:b + 1, :], (8, MODC))
        second = []
        for k in range(1, NDEV):
            to = _peer(me, k)
            second.append(_remote(sbuf.at[_index(to)], rbuf.at[mi], send2.at[k - 1], recv2.at[k - 1], to))
        for cp in second:
            cp.start()
        rbuf[mi] = sbuf[mi]
        for k in range(1, NDEV):
            to = _peer(me, k)
            _remote(sbuf.at[_index(to)], rbuf.at[_index(to)], send2.at[k - 1], recv2.at[k - 1], to).wait_recv()
        for j in range(NDEV):
            mod_ref[:, j * MODC:(j + 1) * MODC] = rbuf[j] + b_ref[:, j * MODC:(j + 1) * MODC]
        for cp in first + second:
            cp.wait_send()

    return pl.pallas_call(
        body, name="ada_mod",
        out_shape=(jax.ShapeDtypeStruct((8, N_MOD * D), F32), jax.ShapeDtypeStruct((NDEV, 8, D), F32)),
        in_specs=[_vmem(), _vmem(), _vmem()], out_specs=(_vmem(), _vmem()),
        scratch_shapes=[pltpu.VMEM((8, D), F32), pltpu.VMEM((NDEV, 8, MODC), F32), pltpu.VMEM((NDEV, 8, MODC), F32),
                        pltpu.SemaphoreType.DMA((NDEV - 1,)), pltpu.SemaphoreType.DMA((NDEV - 1,)),
                        pltpu.SemaphoreType.DMA((NDEV - 1,)), pltpu.SemaphoreType.DMA((NDEV - 1,))],
        compiler_params=_params(),
    )(c, w_ada, b_ada)


def _sequencer_scatter(name, collective_id, srcs, after=()):
    n = len(srcs)

    def of(src, to_index):
        r = src.shape[0] // NDEV
        return src.at[pl.ds(pl.multiple_of(to_index * r, 16), r), :]

    def body(*refs):
        src, zone = refs[:n], refs[n + len(after):2 * n + len(after)]
        send, recv, local = refs[2 * n + len(after):]
        me = _my_pos()
        mi = _index(me)
        barrier = pltpu.get_barrier_semaphore()
        for k in range(1, NDEV):
            pl.semaphore_signal(barrier, inc=1, device_id=_peer(me, k), device_id_type=MESH)
        pl.semaphore_wait(barrier, NDEV - 1)
        own = [pltpu.make_async_copy(of(src[a], mi), zone[a].at[mi], local.at[a]) for a in range(n)]
        for cp in own:
            cp.start()
        for a in range(n):
            for k in range(1, NDEV):
                to = _peer(me, k)
                s = a * (NDEV - 1) + k - 1
                _remote(of(src[a], _index(to)), zone[a].at[mi], send.at[s], recv.at[s], to).start()
        for cp in own:
            cp.wait()
        for a in range(n):
            for k in range(1, NDEV):
                to = _peer(me, k)
                s = a * (NDEV - 1) + k - 1
                cp = _remote(of(src[a], mi), zone[a].at[_index(to)], send.at[s], recv.at[s], to)
                cp.wait_send()
                cp.wait_recv()

    return pl.kernel(
        body, name=name, mesh=plsc.ScalarSubcoreMesh(axis_name="sequencer", num_cores=1),
        out_type=tuple(jax.ShapeDtypeStruct((NDEV, s.shape[0] // NDEV, s.shape[1]), s.dtype) for s in srcs),
        scratch_types=[pltpu.SemaphoreType.DMA((n * (NDEV - 1),)), pltpu.SemaphoreType.DMA((n * (NDEV - 1),)),
                       pltpu.SemaphoreType.DMA((n,))],
        compiler_params=pltpu.CompilerParams(collective_id=collective_id),
    )(*srcs, *after)


CHIP_PEERS = (2, 4, 6)


def _sequencer_gather(name, collective_id, srcs, after=()):
    n = len(srcs)
    per = NDEV - 1

    def body(*refs):
        src, zone = refs[:n], refs[n + len(after):2 * n + len(after)]
        send, recv, local = refs[2 * n + len(after):]
        me = _my_pos()
        mi = _index(me)
        sibling = _peer(me, 1)
        talk_to = (sibling,) + tuple(_peer(me, k) for k in CHIP_PEERS)
        barrier = pltpu.get_barrier_semaphore()
        for to in talk_to:
            pl.semaphore_signal(barrier, inc=1, device_id=to, device_id_type=MESH)
        pl.semaphore_wait(barrier, len(talk_to))

        def copy(a, slot, block_of, to, from_src=False):
            rows = zone[a].at[_index(block_of)]
            return _remote(src[a] if from_src else rows, rows, send.at[a * per + slot], recv.at[a * per + slot], to)

        own = [pltpu.make_async_copy(src[a], zone[a].at[mi], local.at[a]) for a in range(n)]
        for cp in own:
            cp.start()
        started = []
        for a in range(n):
            started.append(copy(a, 0, me, sibling, from_src=True))
            started += [copy(a, 1 + j, me, _peer(me, k), from_src=True) for j, k in enumerate(CHIP_PEERS)]
        for cp in started:
            cp.start()
        for a in range(n):
            for j, k in enumerate(CHIP_PEERS):
                copy(a, 1 + j, _peer(me, k), me).wait_recv()
                passed = copy(a, 4 + j, _peer(me, k), sibling)
                passed.start()
                started.append(passed)
        for a in range(n):
            copy(a, 0, sibling, me).wait_recv()
            for j, k in enumerate(CHIP_PEERS):
                copy(a, 4 + j, _peer(me, k | 1), me).wait_recv()
        for cp in started:
            cp.wait_send()
        for cp in own:
            cp.wait()

    return pl.kernel(
        body, name=name, mesh=plsc.ScalarSubcoreMesh(axis_name="sequencer", num_cores=1),
        out_type=tuple(jax.ShapeDtypeStruct((NDEV,) + s.shape, s.dtype) for s in srcs),
        scratch_types=[pltpu.SemaphoreType.DMA((n * per,)), pltpu.SemaphoreType.DMA((n * per,)),
                       pltpu.SemaphoreType.DMA((n,))],
        compiler_params=pltpu.CompilerParams(collective_id=collective_id),
    )(*srcs, *after)


def _sum_partials(name, parts):
    n = len(parts)

    def body(*refs):
        for a in range(n):
            acc = refs[a][0].astype(F32)
            for p in range(1, NDEV):
                acc = acc + refs[a][p].astype(F32)
            refs[n + a][...] = acc

    return pl.pallas_call(
        body, name=name,
        out_shape=tuple(jax.ShapeDtypeStruct(p.shape[1:], F32) for p in parts),
        in_specs=[_vmem()] * n, out_specs=tuple([_vmem()] * n), compiler_params=_params(),
    )(*parts)


def _small_all_reduce(buf):
    def body(buf_ref, got_ref, red_ref, mine, send1, recv1, send2, recv2):
        me = _my_pos()
        mi = _index(me)
        first = []
        for k in range(1, NDEV):
            to = _peer(me, k)
            first.append(_remote(buf_ref.at[_index(to)], got_ref.at[mi], send1.at[k - 1], recv1.at[k - 1], to))
        for cp in first:
            cp.start()
        got_ref[mi] = buf_ref[mi]
        for k in range(1, NDEV):
            to = _peer(me, k)
            _remote(buf_ref.at[mi], got_ref.at[_index(to)], send1.at[k - 1], recv1.at[k - 1], to).wait_recv()
        acc = got_ref[0]
        for p in range(1, NDEV):
            acc = acc + got_ref[p]
        mine[...] = acc
        second = [_remote(mine, red_ref.at[mi], send2.at[k - 1], recv2.at[k - 1], _peer(me, k)) for k in range(1, NDEV)]
        for cp in second:
            cp.start()
        red_ref[mi] = acc
        for k in range(1, NDEV):
            to = _peer(me, k)
            _remote(mine, red_ref.at[_index(to)], send2.at[k - 1], recv2.at[k - 1], to).wait_recv()
        for cp in first + second:
            cp.wait_send()

    return pl.pallas_call(
        body, name="small_all_reduce",
        out_shape=(jax.ShapeDtypeStruct(buf.shape, F32), jax.ShapeDtypeStruct(buf.shape, F32)),
        in_specs=[_vmem()], out_specs=(_vmem(), _vmem()),
        scratch_shapes=[pltpu.VMEM(buf.shape[1:], F32),
                        pltpu.SemaphoreType.DMA((NDEV - 1,)), pltpu.SemaphoreType.DMA((NDEV - 1,)),
                        pltpu.SemaphoreType.DMA((NDEV - 1,)), pltpu.SemaphoreType.DMA((NDEV - 1,))],
        compiler_params=_params(),
    )(buf)


def _adamw_math(w, g, m, v):
    m = ADAM_B1 * m + (1.0 - ADAM_B1) * g
    v = ADAM_B2 * v + (1.0 - ADAM_B2) * jnp.square(g)
    m_hat = m / (1.0 - ADAM_B1 ** ADAM_STEP)
    v_hat = v / (1.0 - ADAM_B2 ** ADAM_STEP)
    delta = -ADAM_LR * (m_hat / (jnp.sqrt(v_hat) + ADAM_EPS) + ADAM_WD * w)
    return delta, m, v


def _adamw_group(name, ws, gs, ms, vs):
    n = len(ws)

    def body(*refs):
        for a in range(n):
            w, g, m, v = (refs[q * n + a][...] for q in range(4))
            delta, m2, v2 = _adamw_math(w, g, m, v)
            refs[4 * n + a][...] = delta
            refs[5 * n + a][...] = m2
            refs[6 * n + a][...] = v2

    shapes = tuple(jax.ShapeDtypeStruct(w.shape, F32) for w in ws)
    outs = pl.pallas_call(
        body, name=name, out_shape=shapes * 3, in_specs=[_vmem()] * (4 * n), out_specs=tuple([_vmem()] * (3 * n)),
        compiler_params=_params(),
    )(*ws, *gs, *ms, *vs)
    return outs[:n], outs[n:2 * n], outs[2 * n:]


def _adamw_ada(w, m, v, c_all, dmod_rows):
    def body(w_ref, m_ref, v_ref, c_ref, dm_ref, g_ref, d_ref, m2_ref, v2_ref):
        g = _dot_tn(c_ref[...], dm_ref[...].astype(BF))
        g_ref[...] = g
        delta, m2, v2 = _adamw_math(w_ref[...], g, m_ref[...], v_ref[...])
        d_ref[...] = delta
        m2_ref[...] = m2
        v2_ref[...] = v2

    shp = jax.ShapeDtypeStruct(w.shape, F32)
    return pl.pallas_call(
        body, name="adamw_ada", out_shape=(shp, shp, shp, shp), in_specs=[_vmem()] * 5,
        out_specs=tuple([_vmem()] * 4), compiler_params=_params(),
    )(w, m, v, c_all, dmod_rows)


def _w_in_to_kernel(w):
    return jnp.concatenate([w[:, 0:448], jnp.zeros((w.shape[0], 64), w.dtype), w[:, 448:960]], axis=1)


def _w_in_from_kernel(w):
    return jnp.concatenate([w[:, 0:448], w[:, 512:1024]], axis=1)


def _w_uq_to_kernel(w):
    r = w.shape[0]
    return jnp.concatenate([w[:, :, 0:NOPE].reshape(r, HEADS * NOPE),
                            w[:, :, NOPE:NOPE + HALF].reshape(r, HEADS * HALF),
                            w[:, :, NOPE + HALF:].reshape(r, HEADS * HALF)], axis=1)


def _w_uq_from_kernel(w):
    r = w.shape[0]
    return jnp.concatenate([w[:, 0:512].reshape(r, HEADS, NOPE), w[:, 512:640].reshape(r, HEADS, HALF),
                            w[:, 640:768].reshape(r, HEADS, HALF)], axis=2)


REP_NAMES = ("w_uk", "w_uv", "w_pool", "g_mix", "g_q", "g_kv", "pool_scale", "g_ffn", "g_final")


def kernel(x, c, positions, w_ada, b_ada, g_mix, w_in, g_q, g_kv, w_uq, w_uk, w_uv, w_pool, pool_scale, w_o, g_ffn, w_gate, w_up, w_down, g_final, loss_target, m_w_ada, m_b_ada, m_g_mix, m_w_in, m_g_q, m_g_kv, m_w_uq, m_w_uk, m_w_uv, m_w_pool, m_pool_scale, m_w_o, m_g_ffn, m_w_gate, m_w_up, m_w_down, m_g_final, v_w_ada, v_b_ada, v_g_mix, v_w_in, v_g_q, v_g_kv, v_w_uq, v_w_uk, v_w_uv, v_w_pool, v_pool_scale, v_w_o, v_g_ffn, v_w_gate, v_w_up, v_w_down, v_g_final):
    given = dict(locals())

    merge = lambda g: g.reshape(NDEV * g.shape[1], g.shape[2])
    w_in_p, w_uq_p = (merge(g) for g in _sequencer_gather(
        "gather_in", 3, (_w_in_to_kernel(w_in[0]).astype(BF), _w_uq_to_kernel(w_uq[0]).astype(BF))))

    mod, c_all8 = _ada_mod(c, w_ada[0], b_ada)
    c_all = c_all8[:, 0, :]
    late = _sequencer_gather(
        "gather_late", 1, (w_o[0].astype(BF), w_gate[0].T.astype(BF), w_up[0].T.astype(BF), w_down[0].astype(BF)),
        after=(mod[:, 0:128], w_in_p[0:16, 0:128], w_uq_p[0:16, 0:128]))

    def ffn_grads_exchange(arrays):
        return _sequencer_scatter("scatter_ffn", 2, arrays)

    loss, dx, dmod, tail_grads, ffn_parts, replicated = _local_step(
        x[0], positions[0], loss_target[0], mod, g_mix, w_in_p, g_q, g_kv, w_uq_p, w_uk[0], w_uv[0], w_pool[0],
        pool_scale, g_ffn, g_final.reshape(1, D), tuple(merge(g) for g in late), ffn_grads_exchange)

    flat = jnp.concatenate([replicated[k].reshape(-1) for k in REP_NAMES] + [loss.reshape(1)])
    flat = jnp.pad(flat, (0, NDEV * REP_ROWS * 128 - flat.shape[0])).reshape(NDEV, REP_ROWS, 128)
    dmod_blocks = jnp.pad(dmod.reshape(NDEV, MODC // 128, 128), ((0, 0), (0, MOD_ROWS - MODC // 128), (0, 0)))
    got, red = _small_all_reduce(jnp.concatenate([dmod_blocks, flat], axis=1))

    tail_parts = _sequencer_scatter("scatter_tail", 4, tail_grads,
                                    after=(ffn_parts[0][0, 0:16, 0:128], red[0, 0:8, :]))
    g_gate_t, g_up_t, g_down = _sum_partials("sum_ffn_partials", ffn_parts)
    g_in_p, g_uq_p, g_o = _sum_partials("sum_tail_partials", tail_parts)
    grads = dict(w_in=_w_in_from_kernel(g_in_p), w_uq=_w_uq_from_kernel(g_uq_p).reshape(QL // NDEV, HEADS * 192),
                 w_o=g_o, w_gate=g_gate_t.T, w_up=g_up_t.T, w_down=g_down)
    dmod_rows = got[:, 0:MODC // 128, :].reshape(NDEV, MODC)
    grads["b_ada"] = red[:, 0:MODC // 128, :].reshape(1, N_MOD * D)
    rep_flat = red[:, MOD_ROWS:, :].reshape(-1)
    off = 0
    for k in REP_NAMES:
        size = int(np.prod(given[k].shape))
        grads[k] = rep_flat[off:off + size]
        off += size

    view = dict(w_ada=(D, MODC), b_ada=(1, N_MOD * D), g_mix=(1, D), w_in=(D // NDEV, 960), g_q=(1, QL),
                g_kv=(1, KVL), w_uq=(QL // NDEV, HEADS * 192), w_uk=(KVL, HEADS * NOPE), w_uv=(KVL, HEADS * 128),
                w_pool=(GROUPS * GD, GD), pool_scale=(1, PW), w_o=(D // NDEV, D), g_ffn=(1, D),
                w_gate=(D, FF // NDEV), w_up=(D, FF // NDEV), w_down=(FF // NDEV, D), g_final=(1, D))
    names = list(view)
    g_ada, d_ada, m_ada, v_ada = _adamw_ada(w_ada[0], m_w_ada[0], v_w_ada[0], c_all.astype(BF), dmod_rows)
    out_g, out_d, out_m, out_v = dict(w_ada=g_ada), dict(w_ada=d_ada), dict(w_ada=m_ada), dict(w_ada=v_ada)
    tail = ("w_in", "w_uq", "w_o")
    groups = (("adamw_ffn", ("w_gate", "w_up", "w_down")),
              ("adamw_replicated", tuple(k for k in names if k not in ("w_ada", "w_gate", "w_up", "w_down") + tail)),
              ("adamw_tail", tail))
    for gname, members in groups:
        ws = [given[k].reshape(view[k]) for k in members]
        gs = [grads[k].reshape(view[k]) for k in members]
        ms = [given["m_" + k].reshape(view[k]) for k in members]
        vs = [given["v_" + k].reshape(view[k]) for k in members]
        ds, m2, v2 = _adamw_group(gname, ws, gs, ms, vs)
        for k, g, d, mm, vv in zip(members, gs, ds, m2, v2):
            out_g[k], out_d[k], out_m[k], out_v[k] = g, d, mm, vv

    total = rep_flat[off]
    shaped = lambda d: [d[k].reshape(given[k].shape) for k in names]
    return (total, dx[None], *shaped(out_g), *shaped(out_d), *shaped(out_m), *shaped(out_v))
```

```python
import numpy as np
import jax
import jax.numpy as jnp
from jax import lax
from jax.experimental import pallas as pl
from jax.experimental.pallas import tpu as pltpu
from jax.experimental.pallas import tpu_sc as plsc

D = 1024
HEADS = 4
NOPE = 128
ROPE = 64
HALF = ROPE // 2
QL = 256
KVL = 128
FF = 2816
PW = 512
GROUPS = 4
GD = 128
N_MOD = 6
EPS = 1e-6
SM_SCALE = (NOPE + ROPE) ** -0.5
ROPE_THETA = 10000.0
NDEV = 8
MODC = N_MOD * D // NDEV

ADAM_LR = 0.001
ADAM_B1 = 0.9
ADAM_B2 = 0.999
ADAM_EPS = 1e-08
ADAM_WD = 0.01
ADAM_STEP = 10

BF = jnp.bfloat16
F32 = jnp.float32
VMEM_LIMIT_V7X = 60 * 1024 * 1024
MESH = pl.DeviceIdType.MESH

TQ = 256
TK = 256
QW = 256
MOD_ROWS = 8
REP_ROWS = 200
SMALL_ROWS = MOD_ROWS + REP_ROWS


def _params(sem=None):
    return pltpu.CompilerParams(dimension_semantics=sem, vmem_limit_bytes=VMEM_LIMIT_V7X)


def _dot(a, b):
    return jnp.dot(a, b, preferred_element_type=F32)


def _dot_nt(a, b):
    return lax.dot_general(a, b, (((1,), (1,)), ((), ())), preferred_element_type=F32)


def _dot_tn(a, b):
    return _dot(a.astype(F32).T.astype(BF), b)


def _full(shape):
    return pl.BlockSpec(shape, lambda *_: (0,) * len(shape))


def _rows(ts, cols):
    return pl.BlockSpec((ts, cols), lambda i: (i, 0))


def _vmem():
    return pl.BlockSpec(memory_space=pltpu.VMEM)


def _any():
    return pl.BlockSpec(memory_space=pl.ANY)


def _rms(v):
    return lax.rsqrt(jnp.mean(v * v, axis=-1, keepdims=True) + EPS)


def _rms_bwd(dn, n, r):
    return r * (dn - n * jnp.mean(dn * n, axis=-1, keepdims=True))


def _colsum(v):
    return jnp.sum(v, axis=0, keepdims=True)


def _swap_halves(v):
    lane = lax.broadcasted_iota(jnp.int32, v.shape, 1)
    return jnp.where(lane < HALF, pltpu.roll(v, 128 - HALF, 1), pltpu.roll(v, HALF, 1))


def _window_lane_width():
    lane = lax.broadcasted_iota(jnp.int32, (1, PW), 1)
    return jnp.where(lane < 128, 2.0, jnp.where(lane < 256, 4.0, jnp.where(lane < 384, 8.0, 16.0))).astype(F32)


def _window_sums(ext, back):
    n = ext.shape[0]

    def sh(v, k):
        return pltpu.roll(v, k if back else n - k, 0)

    s2 = ext + sh(ext, 1)
    e4 = s2[:, 128:]
    s4 = e4 + sh(e4, 2)
    e8 = s4[:, 128:]
    s8 = e8 + sh(e8, 4)
    e16 = s8[:, 128:]
    s16 = e16 + sh(e16, 8)
    return jnp.concatenate([s2[:, :128], s4[:, :128], s8[:, :128], s16], axis=1)


def _row_counts(first_row, ts):
    t1 = (first_row + lax.broadcasted_iota(jnp.int32, (ts, 1), 0) + 1).astype(F32)
    return jnp.minimum(t1, _window_lane_width())


def _fwd_in(x, mod, g_mix, w_in, g_q, g_kv, w_uq, wuk_dc, perm, cos4, sin4, csk, snk, w_pool, pool_scale):
    S = x.shape[0]
    ts = 512
    nsub = ts // TQ

    def body(x_ref, mod_ref, gmix_ref, win_ref, gq_ref, gkv_ref, wuq_ref, wuk_ref, perm_ref, cos_ref, sin_ref,
             csk_ref, snk_ref, wpool_ref, pscale_ref,
             h1_ref, raw_ref, qn_ref, qs_ref, kv_ref, kvt_ref, pooled_ref, ypre_ref, ypool_ref, carry_ref):
        i = pl.program_id(0)

        @pl.when(i == 0)
        def _():
            carry_ref[...] = jnp.zeros_like(carry_ref)

        xv = x_ref[...]
        sh1 = mod_ref[0:1, 0:D]
        sc1 = mod_ref[0:1, D:2 * D]
        h = (xv * _rms(xv)) * gmix_ref[...] * (1.0 + sc1) + sh1
        hb = h.astype(BF)
        h1_ref[...] = hb
        proj = _dot(hb, win_ref[...])
        cq_raw = proj[:, 0:QL]
        ckv_raw = proj[:, QL:QL + KVL]
        kr = proj[:, 384:512]
        u = proj[:, 512:1024]
        raw_ref[...] = proj[:, 0:384]

        c_q = (cq_raw * _rms(cq_raw)) * gq_ref[...]
        c_kv = (ckv_raw * _rms(ckv_raw)) * gkv_ref[...]
        q = _dot(c_q.astype(BF), wuq_ref[...])
        qn = q[:, 0:HEADS * NOPE].astype(BF)
        qn_ref[...] = qn
        x1 = q[:, 512:640]
        x2 = q[:, 640:768]
        cosv = cos_ref[...]
        sinv = sin_ref[...]
        roped = jnp.concatenate([x1 * cosv - x2 * sinv, x1 * sinv + x2 * cosv], axis=1).astype(BF)
        for hd in range(HEADS):
            q_lat = _dot(qn[:, hd * NOPE:(hd + 1) * NOPE], wuk_ref[hd])
            q_rope = _dot(roped, perm_ref[hd])
            qh = jnp.concatenate([q_lat, q_rope], axis=1).astype(BF)
            for a in range(nsub):
                qs_ref[a, hd * TQ:(hd + 1) * TQ, :] = qh[a * TQ:(a + 1) * TQ, :]
        k_rope = kr * csk_ref[...] + _swap_halves(kr) * snk_ref[...]
        keys = jnp.concatenate([c_kv, k_rope], axis=1)
        kv_ref[...] = keys.astype(BF)
        for a in range(ts // TK):
            kvt_ref[a] = keys[a * TK:(a + 1) * TK, :].T.astype(BF)

        ext = jnp.concatenate([carry_ref[...], u], axis=0)
        win = _window_sums(ext, True)[16:, :]
        pooled = (win / _row_counts(i * ts, ts) - u).astype(BF)
        pooled_ref[...] = pooled
        carry_ref[...] = u[ts - 16:ts, :]
        ypre = jnp.concatenate(
            [_dot(pooled[:, g * GD:(g + 1) * GD], wpool_ref[g]) for g in range(GROUPS)], axis=1)
        ypre_ref[...] = ypre
        ypool_ref[...] = (ypre * pscale_ref[...]).astype(BF)

    out_shape = (
        jax.ShapeDtypeStruct((S, D), BF),
        jax.ShapeDtypeStruct((S, 384), F32),
        jax.ShapeDtypeStruct((S, HEADS * NOPE), BF),
        jax.ShapeDtypeStruct((S // TQ, HEADS * TQ, QW), BF),
        jax.ShapeDtypeStruct((S, QW), BF),
        jax.ShapeDtypeStruct((S // TK, QW, TK), BF),
        jax.ShapeDtypeStruct((S, PW), BF),
        jax.ShapeDtypeStruct((S, PW), F32),
        jax.ShapeDtypeStruct((S, PW), BF),
    )
    in_specs = [
        _rows(ts, D), _full(mod.shape), _full((1, D)), _full(w_in.shape), _full((1, QL)), _full((1, KVL)),
        _full(w_uq.shape), _full(wuk_dc.shape), _full(perm.shape), _rows(ts, 128), _rows(ts, 128), _rows(ts, 128),
        _rows(ts, 128), _full(w_pool.shape), _full((1, PW)),
    ]
    out_specs = (
        _rows(ts, D), _rows(ts, 384), _rows(ts, HEADS * NOPE),
        pl.BlockSpec((nsub, HEADS * TQ, QW), lambda i: (i, 0, 0)),
        _rows(ts, QW), pl.BlockSpec((ts // TK, QW, TK), lambda i: (i, 0, 0)), _rows(ts, PW), _rows(ts, PW),
        _rows(ts, PW),
    )
    return pl.pallas_call(
        body, name="fwd_in", out_shape=out_shape, grid=(S // ts,), in_specs=in_specs, out_specs=out_specs,
        scratch_shapes=[pltpu.VMEM((16, PW), F32)], compiler_params=_params(("arbitrary",)),
    )(x, mod, g_mix, w_in, g_q, g_kv, w_uq, wuk_dc, perm, cos4, sin4, csk, snk, w_pool, pool_scale)


def _diag_mask(shape, q_axis):
    qi = (lax.broadcasted_iota(jnp.int32, shape, q_axis) & (TQ - 1)) >> 6
    ki = lax.broadcasted_iota(jnp.int32, shape, 1 - q_axis) >> 6
    return ki <= qi


def _attn_fwd(qs, kv, kvt, wuv_vc):
    nq = qs.shape[0]
    S = kv.shape[0]
    M = HEADS * TQ

    def body(qs_ref, kv_ref, kvt_ref, wuv_ref, olat_ref, ymla_ref, lse_ref):
        i = pl.program_id(0)
        q = qs_ref[0]

        def step(kt, carry, masked):
            m, l, acc = carry
            k = kv_ref[pl.ds(pl.multiple_of(kt * TK, TK), TK), :]
            v_t = kvt_ref[kt][0:KVL, :]
            s = _dot_nt(k, q) * SM_SCALE
            if masked:
                s = jnp.where(_diag_mask((TK, M), 1), s, -jnp.inf)
            m_new = jnp.maximum(m, jnp.max(s, axis=0, keepdims=True))
            alpha = jnp.exp(m - m_new)
            p = jnp.exp(s - m_new)
            l = alpha * l + jnp.sum(p, axis=0, keepdims=True)
            acc = alpha * acc + _dot(v_t, p.astype(BF))
            return m_new, l, acc

        init = (jnp.full((1, M), -jnp.inf, F32), jnp.zeros((1, M), F32), jnp.zeros((KVL, M), F32))
        carry = lax.fori_loop(0, i, lambda kt, c: step(kt, c, False), init)
        m, l, acc = step(i, carry, True)
        o_lat = acc / l
        olat_ref[0] = o_lat
        lse_ref[0] = jnp.broadcast_to(m + jnp.log(l), (8, M))
        for hd in range(HEADS):
            o_t = _dot(wuv_ref[hd], o_lat[:, hd * TQ:(hd + 1) * TQ].astype(BF))
            ymla_ref[:, hd * 128:(hd + 1) * 128] = o_t.T.astype(BF)

    out_shape = (
        jax.ShapeDtypeStruct((nq, KVL, M), F32),
        jax.ShapeDtypeStruct((S, HEADS * 128), BF),
        jax.ShapeDtypeStruct((nq, 8, M), F32),
    )
    return pl.pallas_call(
        body, name="attn_fwd", out_shape=out_shape, grid=(nq,),
        in_specs=[pl.BlockSpec((1, M, QW), lambda i: (i, 0, 0)), _full(kv.shape), _full(kvt.shape),
                  _full(wuv_vc.shape)],
        out_specs=(pl.BlockSpec((1, KVL, M), lambda i: (i, 0, 0)), _rows(TQ, HEADS * 128),
                   pl.BlockSpec((1, 8, M), lambda i: (i, 0, 0))),
        compiler_params=_params(("arbitrary",)),
    )(qs, kv, kvt, wuv_vc)


def _silu_parts(a):
    sg = jax.nn.sigmoid(a)
    return sg, a * sg


def _ffn_fwd(x, ymla, ypool, mod, w_o, g_ffn, wg_t, wu_t, wd, g_final, target):
    S = x.shape[0]
    ts = 256

    def body(x_ref, ymla_ref, ypool_ref, mod_ref, wo_ref, gffn_ref, wg_ref, wu_ref, wd_ref, gfin_ref, t_ref,
             x2_ref, mix_ref, h2t_ref, a_ref, b_ref, dx3_ref, dff_ref, dfft_ref, loss_ref, dgfin_ref, dgt2_ref,
             f_ref):
        i = pl.program_id(0)

        @pl.when(i == 0)
        def _():
            loss_ref[...] = jnp.zeros_like(loss_ref)
            dgfin_ref[...] = jnp.zeros_like(dgfin_ref)
            dgt2_ref[...] = jnp.zeros_like(dgt2_ref)

        gt1 = mod_ref[0:1, 2 * D:3 * D]
        sh2 = mod_ref[0:1, 3 * D:4 * D]
        sc2 = mod_ref[0:1, 4 * D:5 * D]
        gt2 = mod_ref[0:1, 5 * D:6 * D]
        cat = jnp.concatenate([ymla_ref[...], ypool_ref[...]], axis=1)
        mix = _dot(cat, wo_ref[...])
        mix_ref[...] = mix
        x2 = x_ref[...] + gt1 * mix
        x2_ref[...] = x2
        h2 = (x2 * _rms(x2)) * gffn_ref[...] * (1.0 + sc2) + sh2
        h2b = h2.astype(BF)
        h2t_ref[...] = h2.T.astype(BF)

        for c in range(FF // FCHUNK):
            cols = slice(c * FCHUNK, (c + 1) * FCHUNK)
            a = _dot_nt(h2b, wg_ref[cols, :])
            b = _dot_nt(h2b, wu_ref[cols, :])
            a_ref[:, cols] = a.astype(BF)
            b_ref[:, cols] = b.astype(BF)
            f_ref[:, cols] = (_silu_parts(a)[1] * b).astype(BF)
        ff = _dot(f_ref[...], wd_ref[...])

        x3 = x2 + gt2 * ff
        r3 = _rms(x3)
        xn3 = x3 * r3
        gfin = gfin_ref[...]
        e = xn3 * gfin - t_ref[...]
        loss_ref[...] += 0.5 * jnp.sum(jnp.mean(e * e, axis=-1, keepdims=True))
        dy = e * (1.0 / D)
        dgfin_ref[...] += _colsum(dy * xn3)
        dx3 = _rms_bwd(dy * gfin, xn3, r3)
        dx3_ref[...] = dx3
        dgt2_ref[...] += _colsum(dx3 * ff)
        dff = dx3 * gt2
        dff_ref[...] = dff.astype(BF)
        dfft_ref[...] = dff.T.astype(BF)

    row = lambda c: _rows(ts, c)
    col = pl.BlockSpec((D, ts), lambda i: (0, i))
    const = _full
    out_shape = (
        jax.ShapeDtypeStruct((S, D), F32),
        jax.ShapeDtypeStruct((S, D), F32),
        jax.ShapeDtypeStruct((D, S), BF),
        jax.ShapeDtypeStruct((S, FF), BF),
        jax.ShapeDtypeStruct((S, FF), BF),
        jax.ShapeDtypeStruct((S, D), F32),
        jax.ShapeDtypeStruct((S, D), BF),
        jax.ShapeDtypeStruct((D, S), BF),
        jax.ShapeDtypeStruct((8, 128), F32),
        jax.ShapeDtypeStruct((1, D), F32),
        jax.ShapeDtypeStruct((1, D), F32),
    )
    return pl.pallas_call(
        body, name="ffn_fwd", out_shape=out_shape, grid=(S // ts,),
        in_specs=[row(D), row(PW), row(PW), const(mod.shape), _vmem(), const((1, D)), _vmem(), _vmem(), _vmem(),
                  const((1, D)), row(D)],
        out_specs=(row(D), row(D), col, row(FF), row(FF), row(D), row(D), col, const((8, 128)), const((1, D)),
                   const((1, D))),
        scratch_shapes=[pltpu.VMEM((ts, FF), BF)],
        compiler_params=_params(("arbitrary",)),
    )(x, ymla, ypool, mod, w_o, g_ffn, wg_t, wu_t, wd, g_final, target)


FCHUNK = 256


def _ffn_bwd_acts(dff, a, b, wg_t, wu_t, wd):
    S = dff.shape[0]
    ts = 512

    def body(dff_ref, a_ref, b_ref, wg_ref, wu_ref, wd_ref, da_ref, db_ref, dh2_ref):
        dffb = dff_ref[...]
        for c in range(FF // FCHUNK):
            cols = slice(c * FCHUNK, (c + 1) * FCHUNK)
            df = _dot_nt(dffb, wd_ref[cols, :])
            av = a_ref[:, cols].astype(F32)
            bv = b_ref[:, cols].astype(F32)
            sg, sa = _silu_parts(av)
            db_ref[:, cols] = (df * sa).astype(BF)
            da_ref[:, cols] = (df * bv * (sg * (1.0 + av * (1.0 - sg)))).astype(BF)
        dh2_ref[...] = _dot(da_ref[...], wg_ref[...]) + _dot(db_ref[...], wu_ref[...])

    act = _rows(ts, FF)
    return pl.pallas_call(
        body, name="ffn_bwd_acts",
        out_shape=(jax.ShapeDtypeStruct((S, FF), BF), jax.ShapeDtypeStruct((S, FF), BF),
                   jax.ShapeDtypeStruct((S, D), F32)),
        grid=(S // ts,), in_specs=[_rows(ts, D), act, act, _vmem(), _vmem(), _vmem()],
        out_specs=(act, act, _rows(ts, D)), compiler_params=_params(("arbitrary",)),
    )(dff, a, b, wg_t, wu_t, wd)


def _ffn_bwd_weights(dff_t, h2_t, da, db, a, b):
    S = da.shape[0]

    def body(dfft_ref, h2t_ref, da_ref, db_ref, a_ref, b_ref, dwg_ref, dwu_ref, dwd_ref):
        h2t = h2t_ref[...]
        dwg_ref[...] = _dot(h2t, da_ref[...]).T.astype(BF)
        dwu_ref[...] = _dot(h2t, db_ref[...]).T.astype(BF)
        f = (_silu_parts(a_ref[...].astype(F32))[1] * b_ref[...].astype(F32)).astype(BF)
        dwd_ref[...] = _dot(dfft_ref[...], f).T.astype(BF)

    act = pl.BlockSpec((S, FCHUNK), lambda j: (0, j))
    wblk = _rows(FCHUNK, D)
    shp = jax.ShapeDtypeStruct((FF, D), BF)
    return pl.pallas_call(
        body, name="ffn_bwd_weights", out_shape=(shp, shp, shp), grid=(FF // FCHUNK,),
        in_specs=[_vmem(), _vmem(), act, act, act, act], out_specs=(wblk, wblk, wblk),
        compiler_params=_params(("arbitrary",)),
    )(dff_t, h2_t, da, db, a, b)


def _mix_bwd(dh2, dx3, x2, mix, mod, g_ffn, ymla, ypool, w_o, ypre, pooled, pool_scale, wpool_dc, olat, wuv_vc):
    S = dh2.shape[0]
    ts = 512
    n = S // ts
    nsub = ts // TQ
    M = HEADS * TQ

    def body(dh2_ref, dx3_ref, x2_ref, mix_ref, mod_ref, gffn_ref, ymla_ref, ypool_ref, wo_ref, ypre_ref, pooled_ref,
             pscale_ref, wpool_ref, olat_ref, wuv_ref,
             dx2_ref, du_ref, dolat_ref, delta_ref, dwo_ref, dwuv_ref, dwpool_ref, dpscale_ref, dgt1_ref, dsc2_ref,
             dsh2_ref, dgffn_ref, carry_ref, dwo_acc):
        i = pl.program_id(0)

        @pl.when(i == 0)
        def _():
            carry_ref[...] = jnp.zeros_like(carry_ref)
            dwo_acc[...] = jnp.zeros_like(dwo_acc)
            for r in (dwuv_ref, dwpool_ref, dpscale_ref, dgt1_ref, dsc2_ref, dsh2_ref, dgffn_ref):
                r[...] = jnp.zeros_like(r)

        gt1 = mod_ref[0:1, 2 * D:3 * D]
        sc2 = mod_ref[0:1, 4 * D:5 * D]
        gffn = gffn_ref[...]
        dh2 = dh2_ref[...]
        x2 = x2_ref[...]
        r2 = _rms(x2)
        xn2 = x2 * r2
        dsc2_ref[...] += _colsum(dh2 * (xn2 * gffn))
        dsh2_ref[...] += _colsum(dh2)
        dgffn_ref[...] += _colsum(dh2 * (1.0 + sc2) * xn2)
        dx2 = dx3_ref[...] + _rms_bwd(dh2 * gffn * (1.0 + sc2), xn2, r2)
        dx2_ref[...] = dx2
        dgt1_ref[...] += _colsum(dx2 * mix_ref[...])
        dmix = (dx2 * gt1).astype(BF)
        cat = jnp.concatenate([ymla_ref[...], ypool_ref[...]], axis=1)
        dwo_acc[...] += _dot_tn(cat, dmix)
        dcat = _dot_nt(dmix, wo_ref[...])
        dymla = dcat[:, 0:512]
        dypool = dcat[:, 512:1024]

        dpscale_ref[...] += _colsum(dypool * ypre_ref[...])
        dypre = (dypool * pscale_ref[...]).astype(BF)
        pooled = pooled_ref[...]
        dpooled = []
        for g in range(GROUPS):
            sl = slice(g * GD, (g + 1) * GD)
            dwpool_ref[g] += _dot_tn(pooled[:, sl], dypre[:, sl])
            dpooled.append(_dot(dypre[:, sl], wpool_ref[g]))
        dpooled = jnp.concatenate(dpooled, axis=1)
        tile = n - 1 - i
        e = dpooled / _row_counts(tile * ts, ts)
        ext = jnp.concatenate([e, carry_ref[...]], axis=0)
        du_ref[...] = _window_sums(ext, False)[0:ts, :] - dpooled
        carry_ref[...] = e[0:16, :]

        for hd in range(HEADS):
            do = dymla[:, hd * 128:(hd + 1) * 128]
            dob = do.astype(BF)
            dol = _dot(dob, wuv_ref[hd])
            for a in range(nsub):
                ol_t = olat_ref[a, :, hd * TQ:(hd + 1) * TQ]
                dl = dol[a * TQ:(a + 1) * TQ, :]
                dolat_ref[a, hd * TQ:(hd + 1) * TQ, :] = dl.astype(BF)
                dwuv_ref[hd] += _dot(ol_t.astype(BF), dob[a * TQ:(a + 1) * TQ, :])
                delta = jnp.sum(dl * ol_t.T, axis=-1, keepdims=True)
                delta_ref[a, :, hd * TQ:(hd + 1) * TQ] = jnp.broadcast_to(delta, (TQ, 128)).T[0:8, :]

        @pl.when(i == n - 1)
        def _():
            dwo_ref[...] = dwo_acc[...].astype(BF)

    rev = lambda c: pl.BlockSpec((ts, c), lambda i: (n - 1 - i, 0))
    rev3 = lambda r, c: pl.BlockSpec((nsub, r, c), lambda i: (n - 1 - i, 0, 0))
    out_shape = (
        jax.ShapeDtypeStruct((S, D), F32),
        jax.ShapeDtypeStruct((S, PW), F32),
        jax.ShapeDtypeStruct((S // TQ, M, KVL), BF),
        jax.ShapeDtypeStruct((S // TQ, 8, M), F32),
        jax.ShapeDtypeStruct((D, D), BF),
        jax.ShapeDtypeStruct((HEADS, KVL, 128), F32),
        jax.ShapeDtypeStruct((GROUPS, GD, GD), F32),
        jax.ShapeDtypeStruct((1, PW), F32),
        jax.ShapeDtypeStruct((1, D), F32), jax.ShapeDtypeStruct((1, D), F32), jax.ShapeDtypeStruct((1, D), F32),
        jax.ShapeDtypeStruct((1, D), F32),
    )
    in_specs = [rev(D), rev(D), rev(D), rev(D), _full(mod.shape), _full((1, D)), rev(PW), rev(PW), _full(w_o.shape),
                rev(PW), rev(PW), _full((1, PW)), _full(wpool_dc.shape), rev3(KVL, M), _full(wuv_vc.shape)]
    out_specs = (rev(D), rev(PW), rev3(M, KVL), rev3(8, M), _full((D, D)), _full((HEADS, KVL, 128)),
                 _full((GROUPS, GD, GD)), _full((1, PW)), _full((1, D)), _full((1, D)), _full((1, D)), _full((1, D)))
    return pl.pallas_call(
        body, name="mix_bwd", out_shape=out_shape, grid=(n,), in_specs=in_specs, out_specs=out_specs,
        scratch_shapes=[pltpu.VMEM((16, PW), F32), pltpu.VMEM((D, D), F32)],
        compiler_params=_params(("arbitrary",)),
    )(dh2, dx3, x2, mix, mod, g_ffn, ymla, ypool, w_o, ypre, pooled, pool_scale, wpool_dc, olat, wuv_vc)


def _attn_bwd(qs, kv, dolat, lse, delta):
    nq = qs.shape[0]
    S = kv.shape[0]
    M = HEADS * TQ
    nk = S // TK

    def body(qs_ref, kv_ref, do_ref, lse_ref, delta_ref, dkv_ref, dqt_ref):
        kt = pl.program_id(0)
        k = kv_ref[...]
        v = k[:, 0:KVL]
        k_t = k.astype(F32).T.astype(BF)

        @pl.when(kt == 0)
        def _():
            dqt_ref[...] = jnp.zeros_like(dqt_ref)

        def step(qi, carry, masked):
            dk, dv = carry
            q = qs_ref[qi]
            do = do_ref[qi]
            s = _dot_nt(k, q) * SM_SCALE
            p = jnp.exp(s - lse_ref[qi, 0:1, :])
            if masked:
                p = jnp.where(_diag_mask((TK, M), 1), p, 0.0)
            dp = _dot_nt(v, do)
            ds = (p * (dp - delta_ref[qi, 0:1, :]) * SM_SCALE).astype(BF)
            dv = dv + _dot(p.astype(BF), do)
            dk = dk + _dot(ds, q)
            dqt_ref[qi] += _dot(k_t, ds)
            return dk, dv

        carry = step(kt, (jnp.zeros((TK, QW), F32), jnp.zeros((TK, KVL), F32)), True)
        dk, dv = lax.fori_loop(kt + 1, nq, lambda qi, c: step(qi, c, False), carry)
        dkv_ref[...] = dk + jnp.concatenate([dv, jnp.zeros((TK, QW - KVL), F32)], axis=1)

    out_shape = (jax.ShapeDtypeStruct((S, QW), F32), jax.ShapeDtypeStruct((nq, QW, M), F32))
    return pl.pallas_call(
        body, name="attn_bwd", out_shape=out_shape, grid=(nk,),
        in_specs=[_vmem(), _rows(TK, QW), _vmem(), _vmem(), _vmem()],
        out_specs=(_rows(TK, QW), _vmem()),
        compiler_params=_params(("arbitrary",)),
    )(qs, kv, dolat, lse, delta)


def _in_bwd(dqt, dkv, du, raw, qn, h1, x, dx2, mod, g_mix, w_in, g_q, g_kv, w_uq, wuk_cd, perm_t, cos4, sin4, csk,
            snk):
    S = x.shape[0]
    ts = 512
    n = S // ts
    nsub = ts // TQ
    M = HEADS * TQ

    def body(dqt_ref, dkv_ref, du_ref, raw_ref, qn_ref, h1_ref, x_ref, dx2_ref, mod_ref, gmix_ref, win_ref, gq_ref,
             gkv_ref, wuq_ref, wuk_ref, permt_ref, cos_ref, sin_ref, csk_ref, snk_ref,
             dx_ref, dwin_ref, dwuq_ref, dwuk_ref, dgq_ref, dgkv_ref, dsc1_ref, dsh1_ref, dgmix_ref, dwin_acc,
             dwuq_acc):
        i = pl.program_id(0)

        @pl.when(i == 0)
        def _():
            dwin_acc[...] = jnp.zeros_like(dwin_acc)
            dwuq_acc[...] = jnp.zeros_like(dwuq_acc)
            for r in (dwuk_ref, dgq_ref, dgkv_ref, dsc1_ref, dsh1_ref, dgmix_ref):
                r[...] = jnp.zeros_like(r)

        dq_blocks = [dqt_ref[a].T for a in range(nsub)]
        qn = qn_ref[...]
        dq_parts = []
        drope = jnp.zeros((ts, 2 * 128), F32)
        for hd in range(HEADS):
            dqh = jnp.concatenate([blk[hd * TQ:(hd + 1) * TQ, :] for blk in dq_blocks], axis=0)
            dq_lat = dqh[:, 0:KVL].astype(BF)
            dq_parts.append(_dot(dq_lat, wuk_ref[hd]))
            dwuk_ref[hd] += _dot_tn(dq_lat, qn[:, hd * NOPE:(hd + 1) * NOPE])
            drope = drope + _dot(dqh[:, KVL:QW].astype(BF), permt_ref[hd])
        do1 = drope[:, 0:128]
        do2 = drope[:, 128:256]
        cosv = cos_ref[...]
        sinv = sin_ref[...]
        dq_parts.append(do1 * cosv + do2 * sinv)
        dq_parts.append(do2 * cosv - do1 * sinv)
        dq = jnp.concatenate(dq_parts, axis=1).astype(BF)

        cq_raw = raw_ref[:, 0:QL]
        ckv_raw = raw_ref[:, QL:QL + KVL]
        rq = _rms(cq_raw)
        nq_ = cq_raw * rq
        gq = gq_ref[...]
        dwuq_acc[...] += _dot_tn((nq_ * gq).astype(BF), dq)
        dc_q = _dot_nt(dq, wuq_ref[...])
        dgq_ref[...] += _colsum(dc_q * nq_)
        dcq_raw = _rms_bwd(dc_q * gq, nq_, rq)

        dkv = dkv_ref[...]
        rk = _rms(ckv_raw)
        nk_ = ckv_raw * rk
        dc_kv = dkv[:, 0:KVL]
        dgkv_ref[...] += _colsum(dc_kv * nk_)
        dckv_raw = _rms_bwd(dc_kv * gkv_ref[...], nk_, rk)
        dkr_roped = dkv[:, KVL:QW]
        dkr = dkr_roped * csk_ref[...] - _swap_halves(dkr_roped) * snk_ref[...]

        dproj = jnp.concatenate([dcq_raw, dckv_raw, dkr, du_ref[...]], axis=1).astype(BF)
        dwin_acc[...] += _dot_tn(h1_ref[...], dproj)
        dh1 = _dot_nt(dproj, win_ref[...])

        sc1 = mod_ref[0:1, D:2 * D]
        gmix = gmix_ref[...]
        xv = x_ref[...]
        r1 = _rms(xv)
        xn1 = xv * r1
        dsc1_ref[...] += _colsum(dh1 * (xn1 * gmix))
        dsh1_ref[...] += _colsum(dh1)
        dgmix_ref[...] += _colsum(dh1 * (1.0 + sc1) * xn1)
        dx_ref[...] = dx2_ref[...] + _rms_bwd(dh1 * gmix * (1.0 + sc1), xn1, r1)

        @pl.when(i == n - 1)
        def _():
            dwin_ref[...] = dwin_acc[...].astype(BF)
            dwuq_ref[...] = dwuq_acc[...].astype(BF)

    out_shape = (
        jax.ShapeDtypeStruct((S, D), F32),
        jax.ShapeDtypeStruct((D, D), BF),
        jax.ShapeDtypeStruct((QL, 768), BF),
        jax.ShapeDtypeStruct((HEADS, KVL, NOPE), F32),
        jax.ShapeDtypeStruct((1, QL), F32), jax.ShapeDtypeStruct((1, KVL), F32),
        jax.ShapeDtypeStruct((1, D), F32), jax.ShapeDtypeStruct((1, D), F32), jax.ShapeDtypeStruct((1, D), F32),
    )
    in_specs = [pl.BlockSpec((nsub, QW, M), lambda i: (i, 0, 0)), _rows(ts, QW), _rows(ts, PW), _rows(ts, 384),
                _rows(ts, HEADS * NOPE), _rows(ts, D), _rows(ts, D), _rows(ts, D), _full(mod.shape), _full((1, D)),
                _full(w_in.shape), _full((1, QL)), _full((1, KVL)), _full(w_uq.shape), _full(wuk_cd.shape),
                _full(perm_t.shape), _rows(ts, 128), _rows(ts, 128), _rows(ts, 128), _rows(ts, 128)]
    out_specs = (_rows(ts, D), _full((D, D)), _full((QL, 768)), _full((HEADS, KVL, NOPE)), _full((1, QL)),
                 _full((1, KVL)), _full((1, D)), _full((1, D)), _full((1, D)))
    return pl.pallas_call(
        body, name="in_bwd", out_shape=out_shape, grid=(n,), in_specs=in_specs, out_specs=out_specs,
        scratch_shapes=[pltpu.VMEM((D, D), F32), pltpu.VMEM((QL, 768), F32)],
        compiler_params=_params(("arbitrary",)),
    )(dqt, dkv, du, raw, qn, h1, x, dx2, mod, g_mix, w_in, g_q, g_kv, w_uq, wuk_cd, perm_t, cos4, sin4, csk, snk)


def _rope_perm():
    p = np.zeros((HEADS, 2 * 128, 128), np.float32)
    for hd in range(HEADS):
        for t in range(HALF):
            p[hd, hd * HALF + t, t] = 1.0
            p[hd, 128 + hd * HALF + t, HALF + t] = 1.0
    return p


def _rope_tables(positions):
    freqs = jnp.power(ROPE_THETA, -jnp.arange(HALF, dtype=F32) / HALF)
    ang = positions.astype(F32)[:, None] * jnp.tile(freqs, HEADS)[None, :]
    cos4 = jnp.cos(ang)
    sin4 = jnp.sin(ang)
    lane = jnp.arange(HEADS * HALF)[None, :]
    csk = jnp.where(lane < ROPE, cos4, 0.0)
    snk = jnp.where(lane < HALF, -sin4, jnp.where(lane < ROPE, sin4, 0.0))
    return cos4, sin4, csk, snk


def _local_step(x, positions, target, mod, g_mix, w_in_p, g_q, g_kv, w_uq_p, w_uk, w_uv, w_pool, pool_scale, g_ffn,
                g_final, late, ffn_grads_exchange):
    perm = jnp.asarray(_rope_perm(), BF)
    perm_t = jnp.asarray(_rope_perm().transpose(0, 2, 1), BF)
    cos4, sin4, csk, snk = _rope_tables(positions)
    wuk_dc = w_uk.transpose(1, 2, 0).astype(BF)
    wuk_cd = w_uk.transpose(1, 0, 2).astype(BF)
    wuv_vc = w_uv.transpose(1, 2, 0).astype(BF)
    wpool = w_pool.astype(BF)
    wpool_dc = w_pool.transpose(0, 2, 1).astype(BF)

    h1, raw, qn, qs, kv, kvt, pooled, ypre, ypool = _fwd_in(
        x, mod, g_mix, w_in_p, g_q, g_kv, w_uq_p, wuk_dc, perm, cos4, sin4, csk, snk, wpool, pool_scale)
    olat, ymla, lse = _attn_fwd(qs, kv, kvt, wuv_vc)
    w_o, wg_t, wu_t, wd = late
    x2, mix, h2_t, a, b, dx3, dff, dff_t, loss, dgfin, dgt2 = _ffn_fwd(
        x, ymla, ypool, mod, w_o, g_ffn, wg_t, wu_t, wd, g_final, target)
    da, db, dh2 = _ffn_bwd_acts(dff, a, b, wg_t, wu_t, wd)
    dwg_t, dwu_t, dwd = _ffn_bwd_weights(dff_t, h2_t, da, db, a, b)
    ffn_parts = ffn_grads_exchange((dwg_t, dwu_t, dwd))
    (dx2, du, dolat, delta, dwo, dwuv, dwpool, dpscale, dgt1, dsc2, dsh2, dgffn) = _mix_bwd(
        dh2, dx3, x2, mix, mod, g_ffn, ymla, ypool, w_o, ypre, pooled, pool_scale, wpool_dc, olat, wuv_vc)
    dkv, dqt = _attn_bwd(qs, kv, dolat, lse, delta)
    dx, dwin, dwuq, dwuk, dgq, dgkv, dsc1, dsh1, dgmix = _in_bwd(
        dqt, dkv, du, raw, qn, h1, x, dx2, mod, g_mix, w_in_p, g_q, g_kv, w_uq_p, wuk_cd, perm_t, cos4, sin4, csk,
        snk)
    dmod = jnp.concatenate([dsh1, dsc1, dgt1, dsh2, dsc2, dgt2], axis=1)
    replicated = dict(
        w_uk=dwuk.transpose(1, 0, 2), w_uv=dwuv.transpose(1, 0, 2), w_pool=dwpool, g_mix=dgmix, g_q=dgq, g_kv=dgkv,
        pool_scale=dpscale, g_ffn=dgffn, g_final=dgfin)
    return loss[0, 0], dx, dmod, (dwin, dwuq, dwo), ffn_parts, replicated


def _my_pos():
    return lax.axis_index("x"), lax.axis_index("y"), lax.axis_index("c")


def _peer(pos, k):
    x, y, c = pos
    return (1 - x if k & 4 else x, 1 - y if k & 2 else y, 1 - c if k & 1 else c)


def _index(pos):
    x, y, c = pos
    return 4 * x + 2 * y + c


def _remote(src, dst, send_sem, recv_sem, to):
    return pltpu.make_async_remote_copy(src_ref=src, dst_ref=dst, send_sem=send_sem, recv_sem=recv_sem,
                                        device_id=to, device_id_type=MESH)


def _ada_mod(c, w_ada, b_ada):
    def body(c_ref, w_ref, b_ref, mod_ref, call_ref, cbuf, sbuf, rbuf, send1, recv1, send2, recv2):
        me = _my_pos()
        mi = _index(me)
        cv = c_ref[...]
        cbuf[...] = jnp.broadcast_to(cv * jax.nn.sigmoid(cv), (8, D))
        call_ref[mi] = cbuf[...]
        first = [_remote(cbuf, call_ref.at[mi], send1.at[k - 1], recv1.at[k - 1], _peer(me, k)) for k in range(1, NDEV)]
        for cp in first:
            cp.start()
        for k in range(1, NDEV):
            _remote(cbuf, call_ref.at[_index(_peer(me, k))], send1.at[k - 1], recv1.at[k - 1], _peer(me, k)).wait_recv()
        c_all = jnp.concatenate([call_ref[b][0:1, :] for b in range(NDEV)], axis=0)
        blocks = _dot(c_all.astype(BF), w_ref[...].astype(BF))
        for b in range(NDEV):
            sbuf[b] = jnp.broadcast_to(blocks[b:b + 1, :], (8, MODC))
        second = []
        for k in range(1, NDEV):
            to = _peer(me, k)
            second.append(_remote(sbuf.at[_index(to)], rbuf.at[mi], send2.at[k - 1], recv2.at[k - 1], to))
        for cp in second:
            cp.start()
        rbuf[mi] = sbuf[mi]
        for k in range(1, NDEV):
            to = _peer(me, k)
            _remote(sbuf.at[_index(to)], rbuf.at[_index(to)], send2.at[k - 1], recv2.at[k - 1], to).wait_recv()
        for j in range(NDEV):
            mod_ref[:, j * MODC:(j + 1) * MODC] = rbuf[j] + b_ref[:, j * MODC:(j + 1) * MODC]
        for cp in first + second:
            cp.wait_send()

    return pl.pallas_call(
        body, name="ada_mod",
        out_shape=(jax.ShapeDtypeStruct((8, N_MOD * D), F32), jax.ShapeDtypeStruct((NDEV, 8, D), F32)),
        in_specs=[_vmem(), _vmem(), _vmem()], out_specs=(_vmem(), _vmem()),
        scratch_shapes=[pltpu.VMEM((8, D), F32), pltpu.VMEM((NDEV, 8, MODC), F32), pltpu.VMEM((NDEV, 8, MODC), F32),
                        pltpu.SemaphoreType.DMA((NDEV - 1,)), pltpu.SemaphoreType.DMA((NDEV - 1,)),
                        pltpu.SemaphoreType.DMA((NDEV - 1,)), pltpu.SemaphoreType.DMA((NDEV - 1,))],
        compiler_params=_params(),
    )(c, w_ada, b_ada)


def _sequencer_scatter(name, collective_id, srcs, after=()):
    n = len(srcs)

    def of(src, to_index):
        r = src.shape[0] // NDEV
        return src.at[pl.ds(pl.multiple_of(to_index * r, 16), r), :]

    def body(*refs):
        src, zone = refs[:n], refs[n + len(after):2 * n + len(after)]
        send, recv, local = refs[2 * n + len(after):]
        me = _my_pos()
        mi = _index(me)
        barrier = pltpu.get_barrier_semaphore()
        for k in range(1, NDEV):
            pl.semaphore_signal(barrier, inc=1, device_id=_peer(me, k), device_id_type=MESH)
        pl.semaphore_wait(barrier, NDEV - 1)
        own = [pltpu.make_async_copy(of(src[a], mi), zone[a].at[mi], local.at[a]) for a in range(n)]
        for cp in own:
            cp.start()
        for a in range(n):
            for k in range(1, NDEV):
                to = _peer(me, k)
                s = a * (NDEV - 1) + k - 1
                _remote(of(src[a], _index(to)), zone[a].at[mi], send.at[s], recv.at[s], to).start()
        for cp in own:
            cp.wait()
        for a in range(n):
            for k in range(1, NDEV):
                to = _peer(me, k)
                s = a * (NDEV - 1) + k - 1
                cp = _remote(of(src[a], mi), zone[a].at[_index(to)], send.at[s], recv.at[s], to)
                cp.wait_send()
                cp.wait_recv()

    return pl.kernel(
        body, name=name, mesh=plsc.ScalarSubcoreMesh(axis_name="sequencer", num_cores=1),
        out_type=tuple(jax.ShapeDtypeStruct((NDEV, s.shape[0] // NDEV, s.shape[1]), s.dtype) for s in srcs),
        scratch_types=[pltpu.SemaphoreType.DMA((n * (NDEV - 1),)), pltpu.SemaphoreType.DMA((n * (NDEV - 1),)),
                       pltpu.SemaphoreType.DMA((n,))],
        compiler_params=pltpu.CompilerParams(collective_id=collective_id),
    )(*srcs, *after)


CHIP_PEERS = (2, 4, 6)


def _sequencer_gather(name, collective_id, srcs, after=()):
    n = len(srcs)
    per = NDEV - 1

    def body(*refs):
        src, zone = refs[:n], refs[n + len(after):2 * n + len(after)]
        send, recv, local = refs[2 * n + len(after):]
        me = _my_pos()
        mi = _index(me)
        sibling = _peer(me, 1)
        talk_to = (sibling,) + tuple(_peer(me, k) for k in CHIP_PEERS)
        barrier = pltpu.get_barrier_semaphore()
        for to in talk_to:
            pl.semaphore_signal(barrier, inc=1, device_id=to, device_id_type=MESH)
        pl.semaphore_wait(barrier, len(talk_to))

        def copy(a, slot, block_of, to, from_src=False):
            rows = zone[a].at[_index(block_of)]
            return _remote(src[a] if from_src else rows, rows, send.at[a * per + slot], recv.at[a * per + slot], to)

        own = [pltpu.make_async_copy(src[a], zone[a].at[mi], local.at[a]) for a in range(n)]
        for cp in own:
            cp.start()
        started = []
        for a in range(n):
            started.append(copy(a, 0, me, sibling, from_src=True))
            started += [copy(a, 1 + j, me, _peer(me, k), from_src=True) for j, k in enumerate(CHIP_PEERS)]
        for cp in started:
            cp.start()
        for a in range(n):
            for j, k in enumerate(CHIP_PEERS):
                copy(a, 1 + j, _peer(me, k), me).wait_recv()
                passed = copy(a, 4 + j, _peer(me, k), sibling)
                passed.start()
                started.append(passed)
        for a in range(n):
            copy(a, 0, sibling, me).wait_recv()
            for j, k in enumerate(CHIP_PEERS):
                copy(a, 4 + j, _peer(me, k | 1), me).wait_recv()
        for cp in started:
            cp.wait_send()
        for cp in own:
            cp.wait()

    return pl.kernel(
        body, name=name, mesh=plsc.ScalarSubcoreMesh(axis_name="sequencer", num_cores=1),
        out_type=tuple(jax.ShapeDtypeStruct((NDEV,) + s.shape, s.dtype) for s in srcs),
        scratch_types=[pltpu.SemaphoreType.DMA((n * per,)), pltpu.SemaphoreType.DMA((n * per,)),
                       pltpu.SemaphoreType.DMA((n,))],
        compiler_params=pltpu.CompilerParams(collective_id=collective_id),
    )(*srcs, *after)


def _blocked(shape, nb, axis=0):
    block = tuple(s // nb if d == axis else s for d, s in enumerate(shape))
    return pl.BlockSpec(block, lambda i: tuple(i if d == axis else 0 for d in range(len(shape))))


def _sum_partials(name, parts, nb):
    n = len(parts)

    def body(*refs):
        for a in range(n):
            acc = refs[a][0].astype(F32)
            for p in range(1, NDEV):
                acc = acc + refs[a][p].astype(F32)
            refs[n + a][...] = acc

    return pl.pallas_call(
        body, name=name, grid=(nb,),
        out_shape=tuple(jax.ShapeDtypeStruct(p.shape[1:], F32) for p in parts),
        in_specs=[_blocked(p.shape, nb, 1) for p in parts],
        out_specs=tuple(_blocked(p.shape[1:], nb) for p in parts), compiler_params=_params(("arbitrary",)),
    )(*parts)


def _small_all_reduce(buf):
    def body(buf_ref, got_ref, red_ref, mine, send1, recv1, send2, recv2):
        me = _my_pos()
        mi = _index(me)
        first = []
        for k in range(1, NDEV):
            to = _peer(me, k)
            first.append(_remote(buf_ref.at[_index(to)], got_ref.at[mi], send1.at[k - 1], recv1.at[k - 1], to))
        for cp in first:
            cp.start()
        got_ref[mi] = buf_ref[mi]
        for k in range(1, NDEV):
            to = _peer(me, k)
            _remote(buf_ref.at[mi], got_ref.at[_index(to)], send1.at[k - 1], recv1.at[k - 1], to).wait_recv()
        acc = got_ref[0]
        for p in range(1, NDEV):
            acc = acc + got_ref[p]
        mine[...] = acc
        second = [_remote(mine, red_ref.at[mi], send2.at[k - 1], recv2.at[k - 1], _peer(me, k)) for k in range(1, NDEV)]
        for cp in second:
            cp.start()
        red_ref[mi] = acc
        for k in range(1, NDEV):
            to = _peer(me, k)
            _remote(mine, red_ref.at[_index(to)], send2.at[k - 1], recv2.at[k - 1], to).wait_recv()
        for cp in first + second:
            cp.wait_send()

    return pl.pallas_call(
        body, name="small_all_reduce",
        out_shape=(jax.ShapeDtypeStruct(buf.shape, F32), jax.ShapeDtypeStruct(buf.shape, F32)),
        in_specs=[_vmem()], out_specs=(_vmem(), _vmem()),
        scratch_shapes=[pltpu.VMEM(buf.shape[1:], F32),
                        pltpu.SemaphoreType.DMA((NDEV - 1,)), pltpu.SemaphoreType.DMA((NDEV - 1,)),
                        pltpu.SemaphoreType.DMA((NDEV - 1,)), pltpu.SemaphoreType.DMA((NDEV - 1,))],
        compiler_params=_params(),
    )(buf)


def _adamw_math(w, g, m, v):
    m = ADAM_B1 * m + (1.0 - ADAM_B1) * g
    v = ADAM_B2 * v + (1.0 - ADAM_B2) * jnp.square(g)
    m_hat = m / (1.0 - ADAM_B1 ** ADAM_STEP)
    v_hat = v / (1.0 - ADAM_B2 ** ADAM_STEP)
    delta = -ADAM_LR * (m_hat / (jnp.sqrt(v_hat) + ADAM_EPS) + ADAM_WD * w)
    return delta, m, v


def _adamw_group(name, ws, gs, ms, vs, nb):
    n = len(ws)

    def body(*refs):
        for a in range(n):
            w, g, m, v = (refs[q * n + a][...] for q in range(4))
            delta, m2, v2 = _adamw_math(w, g, m, v)
            refs[4 * n + a][...] = delta
            refs[5 * n + a][...] = m2
            refs[6 * n + a][...] = v2

    shapes = tuple(jax.ShapeDtypeStruct(w.shape, F32) for w in ws)
    specs = [_blocked(w.shape, nb) for w in ws]
    outs = pl.pallas_call(
        body, name=name, grid=(nb,), out_shape=shapes * 3, in_specs=specs * 4, out_specs=tuple(specs * 3),
        compiler_params=_params(("arbitrary",)),
    )(*ws, *gs, *ms, *vs)
    return outs[:n], outs[n:2 * n], outs[2 * n:]


def _adamw_ada(w, m, v, c_all_t, dmod_rows):
    nb = 4

    def body(w_ref, m_ref, v_ref, c_ref, dm_ref, g_ref, d_ref, m2_ref, v2_ref):
        g = _dot(c_ref[...], dm_ref[...].astype(BF))
        g_ref[...] = g
        delta, m2, v2 = _adamw_math(w_ref[...], g, m_ref[...], v_ref[...])
        d_ref[...] = delta
        m2_ref[...] = m2
        v2_ref[...] = v2

    shp = jax.ShapeDtypeStruct(w.shape, F32)
    spec = _blocked(w.shape, nb)
    return pl.pallas_call(
        body, name="adamw_ada", grid=(nb,), out_shape=(shp, shp, shp, shp),
        in_specs=[spec, spec, spec, _blocked(c_all_t.shape, nb), _full(dmod_rows.shape)],
        out_specs=(spec, spec, spec, spec), compiler_params=_params(("arbitrary",)),
    )(w, m, v, c_all_t, dmod_rows)


def _w_in_to_kernel(w):
    return jnp.concatenate([w[:, 0:448], jnp.zeros((w.shape[0], 64), w.dtype), w[:, 448:960]], axis=1)


def _w_in_from_kernel(w):
    return jnp.concatenate([w[:, 0:448], w[:, 512:1024]], axis=1)


def _w_uq_to_kernel(w):
    r = w.shape[0]
    return jnp.concatenate([w[:, :, 0:NOPE].reshape(r, HEADS * NOPE),
                            w[:, :, NOPE:NOPE + HALF].reshape(r, HEADS * HALF),
                            w[:, :, NOPE + HALF:].reshape(r, HEADS * HALF)], axis=1)


def _w_uq_from_kernel(w):
    r = w.shape[0]
    return jnp.concatenate([w[:, 0:512].reshape(r, HEADS, NOPE), w[:, 512:640].reshape(r, HEADS, HALF),
                            w[:, 640:768].reshape(r, HEADS, HALF)], axis=2)


REP_NAMES = ("w_uk", "w_uv", "w_pool", "g_mix", "g_q", "g_kv", "pool_scale", "g_ffn", "g_final")


def kernel(x, c, positions, w_ada, b_ada, g_mix, w_in, g_q, g_kv, w_uq, w_uk, w_uv, w_pool, pool_scale, w_o, g_ffn, w_gate, w_up, w_down, g_final, loss_target, m_w_ada, m_b_ada, m_g_mix, m_w_in, m_g_q, m_g_kv, m_w_uq, m_w_uk, m_w_uv, m_w_pool, m_pool_scale, m_w_o, m_g_ffn, m_w_gate, m_w_up, m_w_down, m_g_final, v_w_ada, v_b_ada, v_g_mix, v_w_in, v_g_q, v_g_kv, v_w_uq, v_w_uk, v_w_uv, v_w_pool, v_pool_scale, v_w_o, v_g_ffn, v_w_gate, v_w_up, v_w_down, v_g_final):
    given = dict(locals())

    merge = lambda g: g.reshape(NDEV * g.shape[1], g.shape[2])
    w_in_p, w_uq_p = (merge(g) for g in _sequencer_gather(
        "gather_in", 3, (_w_in_to_kernel(w_in[0]).astype(BF), _w_uq_to_kernel(w_uq[0]).astype(BF))))

    mod, c_all8 = _ada_mod(c, w_ada[0], b_ada)
    c_all = c_all8[:, 0, :]
    late = _sequencer_gather(
        "gather_late", 1, (w_o[0].astype(BF), w_gate[0].T.astype(BF), w_up[0].T.astype(BF), w_down[0].astype(BF)),
        after=(mod[:, 0:128], w_in_p[0:16, 0:128], w_uq_p[0:16, 0:128]))

    def ffn_grads_exchange(arrays):
        return _sequencer_scatter("scatter_ffn", 2, arrays)

    loss, dx, dmod, tail_grads, ffn_parts, replicated = _local_step(
        x[0], positions[0], loss_target[0], mod, g_mix, w_in_p, g_q, g_kv, w_uq_p, w_uk[0], w_uv[0], w_pool[0],
        pool_scale, g_ffn, g_final.reshape(1, D), tuple(merge(g) for g in late), ffn_grads_exchange)

    flat = jnp.concatenate([replicated[k].reshape(-1) for k in REP_NAMES] + [loss.reshape(1)])
    flat = jnp.pad(flat, (0, NDEV * REP_ROWS * 128 - flat.shape[0])).reshape(NDEV, REP_ROWS, 128)
    dmod_blocks = jnp.pad(dmod.reshape(NDEV, MODC // 128, 128), ((0, 0), (0, MOD_ROWS - MODC // 128), (0, 0)))
    got, red = _small_all_reduce(jnp.concatenate([dmod_blocks, flat], axis=1))

    tail_parts = _sequencer_scatter("scatter_tail", 4, tail_grads,
                                    after=(ffn_parts[0][0, 0:16, 0:128], red[0, 0:8, :]))
    g_gate_t, g_up_t, g_down = _sum_partials("sum_ffn_partials", ffn_parts, 4)
    g_in_p, g_uq_p, g_o = _sum_partials("sum_tail_partials", tail_parts, 1)
    grads = dict(w_in=_w_in_from_kernel(g_in_p), w_uq=_w_uq_from_kernel(g_uq_p), w_o=g_o, w_gate=g_gate_t.T,
                 w_up=g_up_t.T, w_down=g_down)
    dmod_rows = got[:, 0:MODC // 128, :].reshape(NDEV, MODC)
    grads["b_ada"] = red[:, 0:MODC // 128, :].reshape(1, N_MOD * D)
    rep_flat = red[:, MOD_ROWS:, :].reshape(-1)
    off = 0
    for k in REP_NAMES:
        size = int(np.prod(given[k].shape))
        grads[k] = rep_flat[off:off + size]
        off += size

    view = {k: (given[k].shape[1:] if given[k].ndim > 2 else given[k].shape)
            for k in REP_NAMES + ("b_ada", "w_ada", "w_in", "w_uq", "w_o", "w_gate", "w_up", "w_down")}
    view.update(g_final=(1, D))
    names = ["w_ada", "b_ada", "g_mix", "w_in", "g_q", "g_kv", "w_uq", "w_uk", "w_uv", "w_pool", "pool_scale",
             "w_o", "g_ffn", "w_gate", "w_up", "w_down", "g_final"]
    g_ada, d_ada, m_ada, v_ada = _adamw_ada(w_ada[0], m_w_ada[0], v_w_ada[0], c_all.T.astype(BF), dmod_rows)
    out_g, out_d, out_m, out_v = dict(w_ada=g_ada), dict(w_ada=d_ada), dict(w_ada=m_ada), dict(w_ada=v_ada)
    tail = ("w_in", "w_uq", "w_o")
    groups = (("adamw_ffn", ("w_gate", "w_up", "w_down"), 4),
              ("adamw_replicated", REP_NAMES + ("b_ada",), 1),
              ("adamw_tail", tail, 1))
    for gname, members, nb in groups:
        ws = [given[k].reshape(view[k]) for k in members]
        gs = [grads[k].reshape(view[k]) for k in members]
        ms = [given["m_" + k].reshape(view[k]) for k in members]
        vs = [given["v_" + k].reshape(view[k]) for k in members]
        ds, m2, v2 = _adamw_group(gname, ws, gs, ms, vs, nb)
        for k, g, d, mm, vv in zip(members, gs, ds, m2, v2):
            out_g[k], out_d[k], out_m[k], out_v[k] = g, d, mm, vv

    total = rep_flat[off]
    shaped = lambda d: [d[k].reshape(given[k].shape) for k in names]
    return (total, dx[None], *shaped(out_g), *shaped(out_d), *shaped(out_m), *shaped(out_v))
```

```python
import numpy as np
import jax
import jax.numpy as jnp
from jax import lax
from jax.experimental import pallas as pl
from jax.experimental.pallas import tpu as pltpu
from jax.experimental.pallas import tpu_sc as plsc

D = 1024
HEADS = 4
NOPE = 128
ROPE = 64
HALF = ROPE // 2
QL = 256
KVL = 128
FF = 2816
PW = 512
GROUPS = 4
GD = 128
N_MOD = 6
EPS = 1e-6
SM_SCALE = (NOPE + ROPE) ** -0.5
ROPE_THETA = 10000.0
NDEV = 8
MODC = N_MOD * D // NDEV

ADAM_LR = 0.001
ADAM_B1 = 0.9
ADAM_B2 = 0.999
ADAM_EPS = 1e-08
ADAM_WD = 0.01
ADAM_STEP = 10

BF = jnp.bfloat16
F32 = jnp.float32
VMEM_LIMIT_V7X = 60 * 1024 * 1024
MESH = pl.DeviceIdType.MESH

TQ = 256
TK = 256
QW = 256
MOD_ROWS = 8
REP_ROWS = 200
SMALL_ROWS = MOD_ROWS + REP_ROWS


def _params(sem=None):
    return pltpu.CompilerParams(dimension_semantics=sem, vmem_limit_bytes=VMEM_LIMIT_V7X)


def _dot(a, b):
    return jnp.dot(a, b, preferred_element_type=F32)


def _dot_nt(a, b):
    return lax.dot_general(a, b, (((1,), (1,)), ((), ())), preferred_element_type=F32)


def _dot_tn(a, b):
    return _dot(a.astype(F32).T.astype(BF), b)


def _full(shape):
    return pl.BlockSpec(shape, lambda *_: (0,) * len(shape))


def _rows(ts, cols):
    return pl.BlockSpec((ts, cols), lambda i: (i, 0))


def _vmem():
    return pl.BlockSpec(memory_space=pltpu.VMEM)


def _any():
    return pl.BlockSpec(memory_space=pl.ANY)


def _rms(v):
    return lax.rsqrt(jnp.mean(v * v, axis=-1, keepdims=True) + EPS)


def _rms_bwd(dn, n, r):
    return r * (dn - n * jnp.mean(dn * n, axis=-1, keepdims=True))


def _colsum(v):
    return jnp.sum(v, axis=0, keepdims=True)


def _swap_halves(v):
    lane = lax.broadcasted_iota(jnp.int32, v.shape, 1)
    return jnp.where(lane < HALF, pltpu.roll(v, 128 - HALF, 1), pltpu.roll(v, HALF, 1))


def _window_lane_width():
    lane = lax.broadcasted_iota(jnp.int32, (1, PW), 1)
    return jnp.where(lane < 128, 2.0, jnp.where(lane < 256, 4.0, jnp.where(lane < 384, 8.0, 16.0))).astype(F32)


def _window_sums(ext, back):
    n = ext.shape[0]

    def sh(v, k):
        return pltpu.roll(v, k if back else n - k, 0)

    s2 = ext + sh(ext, 1)
    e4 = s2[:, 128:]
    s4 = e4 + sh(e4, 2)
    e8 = s4[:, 128:]
    s8 = e8 + sh(e8, 4)
    e16 = s8[:, 128:]
    s16 = e16 + sh(e16, 8)
    return jnp.concatenate([s2[:, :128], s4[:, :128], s8[:, :128], s16], axis=1)


def _row_counts(first_row, ts):
    t1 = (first_row + lax.broadcasted_iota(jnp.int32, (ts, 1), 0) + 1).astype(F32)
    return jnp.minimum(t1, _window_lane_width())


def _fwd_in(x, mod, g_mix, w_in, g_q, g_kv, w_uq, wuk_dc, perm, cos4, sin4, csk, snk, w_pool, pool_scale):
    S = x.shape[0]
    ts = 512
    nsub = ts // TQ

    def body(x_ref, mod_ref, gmix_ref, win_ref, gq_ref, gkv_ref, wuq_ref, wuk_ref, perm_ref, cos_ref, sin_ref,
             csk_ref, snk_ref, wpool_ref, pscale_ref,
             h1_ref, raw_ref, qn_ref, qs_ref, kv_ref, kvt_ref, pooled_ref, ypre_ref, ypool_ref, carry_ref):
        i = pl.program_id(0)

        @pl.when(i == 0)
        def _():
            carry_ref[...] = jnp.zeros_like(carry_ref)

        xv = x_ref[...]
        sh1 = mod_ref[0:1, 0:D]
        sc1 = mod_ref[0:1, D:2 * D]
        h = (xv * _rms(xv)) * gmix_ref[...] * (1.0 + sc1) + sh1
        hb = h.astype(BF)
        h1_ref[...] = hb
        proj = _dot(hb, win_ref[...])
        cq_raw = proj[:, 0:QL]
        ckv_raw = proj[:, QL:QL + KVL]
        kr = proj[:, 384:512]
        u = proj[:, 512:1024]
        raw_ref[...] = proj[:, 0:384]

        c_q = (cq_raw * _rms(cq_raw)) * gq_ref[...]
        c_kv = (ckv_raw * _rms(ckv_raw)) * gkv_ref[...]
        q = _dot(c_q.astype(BF), wuq_ref[...])
        qn = q[:, 0:HEADS * NOPE].astype(BF)
        qn_ref[...] = qn
        x1 = q[:, 512:640]
        x2 = q[:, 640:768]
        cosv = cos_ref[...]
        sinv = sin_ref[...]
        roped = jnp.concatenate([x1 * cosv - x2 * sinv, x1 * sinv + x2 * cosv], axis=1).astype(BF)
        for hd in range(HEADS):
            q_lat = _dot(qn[:, hd * NOPE:(hd + 1) * NOPE], wuk_ref[hd])
            q_rope = _dot(roped, perm_ref[hd])
            qh = jnp.concatenate([q_lat, q_rope], axis=1).astype(BF)
            for a in range(nsub):
                qs_ref[a, hd * TQ:(hd + 1) * TQ, :] = qh[a * TQ:(a + 1) * TQ, :]
        k_rope = kr * csk_ref[...] + _swap_halves(kr) * snk_ref[...]
        keys = jnp.concatenate([c_kv, k_rope], axis=1)
        kv_ref[...] = keys.astype(BF)
        for a in range(ts // TK):
            kvt_ref[a] = keys[a * TK:(a + 1) * TK, :].T.astype(BF)

        ext = jnp.concatenate([carry_ref[...], u], axis=0)
        win = _window_sums(ext, True)[16:, :]
        pooled = (win / _row_counts(i * ts, ts) - u).astype(BF)
        pooled_ref[...] = pooled
        carry_ref[...] = u[ts - 16:ts, :]
        ypre = jnp.concatenate(
            [_dot(pooled[:, g * GD:(g + 1) * GD], wpool_ref[g]) for g in range(GROUPS)], axis=1)
        ypre_ref[...] = ypre
        ypool_ref[...] = (ypre * pscale_ref[...]).astype(BF)

    out_shape = (
        jax.ShapeDtypeStruct((S, D), BF),
        jax.ShapeDtypeStruct((S, 384), F32),
        jax.ShapeDtypeStruct((S, HEADS * NOPE), BF),
        jax.ShapeDtypeStruct((S // TQ, HEADS * TQ, QW), BF),
        jax.ShapeDtypeStruct((S, QW), BF),
        jax.ShapeDtypeStruct((S // TK, QW, TK), BF),
        jax.ShapeDtypeStruct((S, PW), BF),
        jax.ShapeDtypeStruct((S, PW), F32),
        jax.ShapeDtypeStruct((S, PW), BF),
    )
    in_specs = [
        _rows(ts, D), _full(mod.shape), _full((1, D)), _full(w_in.shape), _full((1, QL)), _full((1, KVL)),
        _full(w_uq.shape), _full(wuk_dc.shape), _full(perm.shape), _rows(ts, 128), _rows(ts, 128), _rows(ts, 128),
        _rows(ts, 128), _full(w_pool.shape), _full((1, PW)),
    ]
    out_specs = (
        _rows(ts, D), _rows(ts, 384), _rows(ts, HEADS * NOPE),
        pl.BlockSpec((nsub, HEADS * TQ, QW), lambda i: (i, 0, 0)),
        _rows(ts, QW), pl.BlockSpec((ts // TK, QW, TK), lambda i: (i, 0, 0)), _rows(ts, PW), _rows(ts, PW),
        _rows(ts, PW),
    )
    return pl.pallas_call(
        body, name="fwd_in", out_shape=out_shape, grid=(S // ts,), in_specs=in_specs, out_specs=out_specs,
        scratch_shapes=[pltpu.VMEM((16, PW), F32)], compiler_params=_params(("arbitrary",)),
    )(x, mod, g_mix, w_in, g_q, g_kv, w_uq, wuk_dc, perm, cos4, sin4, csk, snk, w_pool, pool_scale)


def _diag_mask(shape, q_axis):
    qi = (lax.broadcasted_iota(jnp.int32, shape, q_axis) & (TQ - 1)) >> 6
    ki = lax.broadcasted_iota(jnp.int32, shape, 1 - q_axis) >> 6
    return ki <= qi


def _attn_fwd(qs, kv, kvt, wuv_vc):
    nq = qs.shape[0]
    S = kv.shape[0]
    M = HEADS * TQ

    def body(qs_ref, kv_ref, kvt_ref, wuv_ref, olat_ref, ymla_ref, lse_ref):
        i = pl.program_id(0)
        q = qs_ref[0]

        def step(kt, carry, masked):
            m, l, acc = carry
            k = kv_ref[pl.ds(pl.multiple_of(kt * TK, TK), TK), :]
            v_t = kvt_ref[kt][0:KVL, :]
            s = _dot_nt(k, q) * SM_SCALE
            if masked:
                s = jnp.where(_diag_mask((TK, M), 1), s, -jnp.inf)
            m_new = jnp.maximum(m, jnp.max(s, axis=0, keepdims=True))
            alpha = jnp.exp(m - m_new)
            p = jnp.exp(s - m_new)
            l = alpha * l + jnp.sum(p, axis=0, keepdims=True)
            acc = alpha * acc + _dot(v_t, p.astype(BF))
            return m_new, l, acc

        init = (jnp.full((1, M), -jnp.inf, F32), jnp.zeros((1, M), F32), jnp.zeros((KVL, M), F32))
        carry = lax.fori_loop(0, i, lambda kt, c: step(kt, c, False), init)
        m, l, acc = step(i, carry, True)
        o_lat = acc / l
        olat_ref[0] = o_lat
        lse_ref[0] = jnp.broadcast_to(m + jnp.log(l), (8, M))
        for hd in range(HEADS):
            o_t = _dot(wuv_ref[hd], o_lat[:, hd * TQ:(hd + 1) * TQ].astype(BF))
            ymla_ref[:, hd * 128:(hd + 1) * 128] = o_t.T.astype(BF)

    out_shape = (
        jax.ShapeDtypeStruct((nq, KVL, M), F32),
        jax.ShapeDtypeStruct((S, HEADS * 128), BF),
        jax.ShapeDtypeStruct((nq, 8, M), F32),
    )
    return pl.pallas_call(
        body, name="attn_fwd", out_shape=out_shape, grid=(nq,),
        in_specs=[pl.BlockSpec((1, M, QW), lambda i: (i, 0, 0)), _full(kv.shape), _full(kvt.shape),
                  _full(wuv_vc.shape)],
        out_specs=(pl.BlockSpec((1, KVL, M), lambda i: (i, 0, 0)), _rows(TQ, HEADS * 128),
                   pl.BlockSpec((1, 8, M), lambda i: (i, 0, 0))),
        compiler_params=_params(("arbitrary",)),
    )(qs, kv, kvt, wuv_vc)


def _silu_parts(a):
    sg = jax.nn.sigmoid(a)
    return sg, a * sg


def _ffn_fwd(x, ymla, ypool, mod, w_o, g_ffn, wg_t, wu_t, wd, g_final, target):
    S = x.shape[0]
    ts = 256

    def body(x_ref, ymla_ref, ypool_ref, mod_ref, wo_ref, gffn_ref, wg_ref, wu_ref, wd_ref, gfin_ref, t_ref,
             x2_ref, mix_ref, h2t_ref, a_ref, b_ref, dx3_ref, dff_ref, dfft_ref, loss_ref, dgfin_ref, dgt2_ref,
             f_ref):
        i = pl.program_id(0)

        @pl.when(i == 0)
        def _():
            loss_ref[...] = jnp.zeros_like(loss_ref)
            dgfin_ref[...] = jnp.zeros_like(dgfin_ref)
            dgt2_ref[...] = jnp.zeros_like(dgt2_ref)

        gt1 = mod_ref[0:1, 2 * D:3 * D]
        sh2 = mod_ref[0:1, 3 * D:4 * D]
        sc2 = mod_ref[0:1, 4 * D:5 * D]
        gt2 = mod_ref[0:1, 5 * D:6 * D]
        cat = jnp.concatenate([ymla_ref[...], ypool_ref[...]], axis=1)
        mix = _dot(cat, wo_ref[...])
        mix_ref[...] = mix
        x2 = x_ref[...] + gt1 * mix
        x2_ref[...] = x2
        h2 = (x2 * _rms(x2)) * gffn_ref[...] * (1.0 + sc2) + sh2
        h2b = h2.astype(BF)
        h2t_ref[...] = h2.T.astype(BF)

        for c in range(FF // FCHUNK):
            cols = slice(c * FCHUNK, (c + 1) * FCHUNK)
            a = _dot_nt(h2b, wg_ref[cols, :])
            b = _dot_nt(h2b, wu_ref[cols, :])
            a_ref[:, cols] = a.astype(BF)
            b_ref[:, cols] = b.astype(BF)
            f_ref[:, cols] = (_silu_parts(a)[1] * b).astype(BF)
        ff = _dot(f_ref[...], wd_ref[...])

        x3 = x2 + gt2 * ff
        r3 = _rms(x3)
        xn3 = x3 * r3
        gfin = gfin_ref[...]
        e = xn3 * gfin - t_ref[...]
        loss_ref[...] += 0.5 * jnp.sum(jnp.mean(e * e, axis=-1, keepdims=True))
        dy = e * (1.0 / D)
        dgfin_ref[...] += _colsum(dy * xn3)
        dx3 = _rms_bwd(dy * gfin, xn3, r3)
        dx3_ref[...] = dx3
        dgt2_ref[...] += _colsum(dx3 * ff)
        dff = dx3 * gt2
        dff_ref[...] = dff.astype(BF)
        dfft_ref[...] = dff.T.astype(BF)

    row = lambda c: _rows(ts, c)
    col = pl.BlockSpec((D, ts), lambda i: (0, i))
    const = _full
    out_shape = (
        jax.ShapeDtypeStruct((S, D), F32),
        jax.ShapeDtypeStruct((S, D), F32),
        jax.ShapeDtypeStruct((D, S), BF),
        jax.ShapeDtypeStruct((S, FF), BF),
        jax.ShapeDtypeStruct((S, FF), BF),
        jax.ShapeDtypeStruct((S, D), F32),
        jax.ShapeDtypeStruct((S, D), BF),
        jax.ShapeDtypeStruct((D, S), BF),
        jax.ShapeDtypeStruct((8, 128), F32),
        jax.ShapeDtypeStruct((1, D), F32),
        jax.ShapeDtypeStruct((1, D), F32),
    )
    return pl.pallas_call(
        body, name="ffn_fwd", out_shape=out_shape, grid=(S // ts,),
        in_specs=[row(D), row(PW), row(PW), const(mod.shape), _vmem(), const((1, D)), _vmem(), _vmem(), _vmem(),
                  const((1, D)), row(D)],
        out_specs=(row(D), row(D), col, row(FF), row(FF), row(D), row(D), col, const((8, 128)), const((1, D)),
                   const((1, D))),
        scratch_shapes=[pltpu.VMEM((ts, FF), BF)],
        compiler_params=_params(("arbitrary",)),
    )(x, ymla, ypool, mod, w_o, g_ffn, wg_t, wu_t, wd, g_final, target)


FCHUNK = 256


def _ffn_bwd_acts(dff, a, b, wg_t, wu_t, wd):
    S = dff.shape[0]
    ts = 512

    def body(dff_ref, a_ref, b_ref, wg_ref, wu_ref, wd_ref, da_ref, db_ref, dh2_ref):
        dffb = dff_ref[...]
        for c in range(FF // FCHUNK):
            cols = slice(c * FCHUNK, (c + 1) * FCHUNK)
            df = _dot_nt(dffb, wd_ref[cols, :])
            av = a_ref[:, cols].astype(F32)
            bv = b_ref[:, cols].astype(F32)
            sg, sa = _silu_parts(av)
            db_ref[:, cols] = (df * sa).astype(BF)
            da_ref[:, cols] = (df * bv * (sg * (1.0 + av * (1.0 - sg)))).astype(BF)
        dh2_ref[...] = _dot(da_ref[...], wg_ref[...]) + _dot(db_ref[...], wu_ref[...])

    act = _rows(ts, FF)
    return pl.pallas_call(
        body, name="ffn_bwd_acts",
        out_shape=(jax.ShapeDtypeStruct((S, FF), BF), jax.ShapeDtypeStruct((S, FF), BF),
                   jax.ShapeDtypeStruct((S, D), F32)),
        grid=(S // ts,), in_specs=[_rows(ts, D), act, act, _vmem(), _vmem(), _vmem()],
        out_specs=(act, act, _rows(ts, D)), compiler_params=_params(("arbitrary",)),
    )(dff, a, b, wg_t, wu_t, wd)


def _ffn_bwd_weights(dff_t, h2_t, da, db, a, b):
    S = da.shape[0]

    def body(dfft_ref, h2t_ref, da_ref, db_ref, a_ref, b_ref, dwg_ref, dwu_ref, dwd_ref):
        h2t = h2t_ref[...]
        dwg_ref[...] = _dot(h2t, da_ref[...]).T.astype(BF)
        dwu_ref[...] = _dot(h2t, db_ref[...]).T.astype(BF)
        f = (_silu_parts(a_ref[...].astype(F32))[1] * b_ref[...].astype(F32)).astype(BF)
        dwd_ref[...] = _dot(dfft_ref[...], f).T.astype(BF)

    act = pl.BlockSpec((S, FCHUNK), lambda j: (0, j))
    wblk = _rows(FCHUNK, D)
    shp = jax.ShapeDtypeStruct((FF, D), BF)
    return pl.pallas_call(
        body, name="ffn_bwd_weights", out_shape=(shp, shp, shp), grid=(FF // FCHUNK,),
        in_specs=[_vmem(), _vmem(), act, act, act, act], out_specs=(wblk, wblk, wblk),
        compiler_params=_params(("arbitrary",)),
    )(dff_t, h2_t, da, db, a, b)


def _mix_bwd(dh2, dx3, x2, mix, mod, g_ffn, ymla, ypool, w_o, ypre, pooled, pool_scale, wpool_dc, olat, wuv_vc):
    S = dh2.shape[0]
    ts = 512
    n = S // ts
    nsub = ts // TQ
    M = HEADS * TQ

    def body(dh2_ref, dx3_ref, x2_ref, mix_ref, mod_ref, gffn_ref, ymla_ref, ypool_ref, wo_ref, ypre_ref, pooled_ref,
             pscale_ref, wpool_ref, olat_ref, wuv_ref,
             dx2_ref, du_ref, dolat_ref, delta_ref, dwo_ref, dwuv_ref, dwpool_ref, dpscale_ref, dgt1_ref, dsc2_ref,
             dsh2_ref, dgffn_ref, carry_ref, dwo_acc):
        i = pl.program_id(0)

        @pl.when(i == 0)
        def _():
            carry_ref[...] = jnp.zeros_like(carry_ref)
            dwo_acc[...] = jnp.zeros_like(dwo_acc)
            for r in (dwuv_ref, dwpool_ref, dpscale_ref, dgt1_ref, dsc2_ref, dsh2_ref, dgffn_ref):
                r[...] = jnp.zeros_like(r)

        gt1 = mod_ref[0:1, 2 * D:3 * D]
        sc2 = mod_ref[0:1, 4 * D:5 * D]
        gffn = gffn_ref[...]
        dh2 = dh2_ref[...]
        x2 = x2_ref[...]
        r2 = _rms(x2)
        xn2 = x2 * r2
        dsc2_ref[...] += _colsum(dh2 * (xn2 * gffn))
        dsh2_ref[...] += _colsum(dh2)
        dgffn_ref[...] += _colsum(dh2 * (1.0 + sc2) * xn2)
        dx2 = dx3_ref[...] + _rms_bwd(dh2 * gffn * (1.0 + sc2), xn2, r2)
        dx2_ref[...] = dx2
        dgt1_ref[...] += _colsum(dx2 * mix_ref[...])
        dmix = (dx2 * gt1).astype(BF)
        cat = jnp.concatenate([ymla_ref[...], ypool_ref[...]], axis=1)
        dwo_acc[...] += _dot_tn(cat, dmix)
        dcat = _dot_nt(dmix, wo_ref[...])
        dymla = dcat[:, 0:512]
        dypool = dcat[:, 512:1024]

        dpscale_ref[...] += _colsum(dypool * ypre_ref[...])
        dypre = (dypool * pscale_ref[...]).astype(BF)
        pooled = pooled_ref[...]
        dpooled = []
        for g in range(GROUPS):
            sl = slice(g * GD, (g + 1) * GD)
            dwpool_ref[g] += _dot_tn(pooled[:, sl], dypre[:, sl])
            dpooled.append(_dot(dypre[:, sl], wpool_ref[g]))
        dpooled = jnp.concatenate(dpooled, axis=1)
        tile = n - 1 - i
        e = dpooled / _row_counts(tile * ts, ts)
        ext = jnp.concatenate([e, carry_ref[...]], axis=0)
        du_ref[...] = _window_sums(ext, False)[0:ts, :] - dpooled
        carry_ref[...] = e[0:16, :]

        for hd in range(HEADS):
            do = dymla[:, hd * 128:(hd + 1) * 128]
            dob = do.astype(BF)
            dol = _dot(dob, wuv_ref[hd])
            for a in range(nsub):
                ol_t = olat_ref[a, :, hd * TQ:(hd + 1) * TQ]
                dl = dol[a * TQ:(a + 1) * TQ, :]
                dolat_ref[a, hd * TQ:(hd + 1) * TQ, :] = dl.astype(BF)
                dwuv_ref[hd] += _dot(ol_t.astype(BF), dob[a * TQ:(a + 1) * TQ, :])
                delta = jnp.sum(dl * ol_t.T, axis=-1, keepdims=True)
                delta_ref[a, :, hd * TQ:(hd + 1) * TQ] = jnp.broadcast_to(delta, (TQ, 128)).T[0:8, :]

        @pl.when(i == n - 1)
        def _():
            dwo_ref[...] = dwo_acc[...].astype(BF)

    rev = lambda c: pl.BlockSpec((ts, c), lambda i: (n - 1 - i, 0))
    rev3 = lambda r, c: pl.BlockSpec((nsub, r, c), lambda i: (n - 1 - i, 0, 0))
    out_shape = (
        jax.ShapeDtypeStruct((S, D), F32),
        jax.ShapeDtypeStruct((S, PW), F32),
        jax.ShapeDtypeStruct((S // TQ, M, KVL), BF),
        jax.ShapeDtypeStruct((S // TQ, 8, M), F32),
        jax.ShapeDtypeStruct((D, D), BF),
        jax.ShapeDtypeStruct((HEADS, KVL, 128), F32),
        jax.ShapeDtypeStruct((GROUPS, GD, GD), F32),
        jax.ShapeDtypeStruct((1, PW), F32),
        jax.ShapeDtypeStruct((1, D), F32), jax.ShapeDtypeStruct((1, D), F32), jax.ShapeDtypeStruct((1, D), F32),
        jax.ShapeDtypeStruct((1, D), F32),
    )
    in_specs = [rev(D), rev(D), rev(D), rev(D), _full(mod.shape), _full((1, D)), rev(PW), rev(PW), _full(w_o.shape),
                rev(PW), rev(PW), _full((1, PW)), _full(wpool_dc.shape), rev3(KVL, M), _full(wuv_vc.shape)]
    out_specs = (rev(D), rev(PW), rev3(M, KVL), rev3(8, M), _full((D, D)), _full((HEADS, KVL, 128)),
                 _full((GROUPS, GD, GD)), _full((1, PW)), _full((1, D)), _full((1, D)), _full((1, D)), _full((1, D)))
    return pl.pallas_call(
        body, name="mix_bwd", out_shape=out_shape, grid=(n,), in_specs=in_specs, out_specs=out_specs,
        scratch_shapes=[pltpu.VMEM((16, PW), F32), pltpu.VMEM((D, D), F32)],
        compiler_params=_params(("arbitrary",)),
    )(dh2, dx3, x2, mix, mod, g_ffn, ymla, ypool, w_o, ypre, pooled, pool_scale, wpool_dc, olat, wuv_vc)


def _attn_bwd(qs, kv, dolat, lse, delta):
    nq = qs.shape[0]
    S = kv.shape[0]
    M = HEADS * TQ
    nk = S // TK

    def body(qs_ref, kv_ref, do_ref, lse_ref, delta_ref, dkv_ref, dqt_ref):
        kt = pl.program_id(0)
        k = kv_ref[...]
        v = k[:, 0:KVL]
        k_t = k.astype(F32).T.astype(BF)

        @pl.when(kt == 0)
        def _():
            dqt_ref[...] = jnp.zeros_like(dqt_ref)

        def step(qi, carry, masked):
            dk, dv = carry
            q = qs_ref[qi]
            do = do_ref[qi]
            s = _dot_nt(k, q) * SM_SCALE
            p = jnp.exp(s - lse_ref[qi, 0:1, :])
            if masked:
                p = jnp.where(_diag_mask((TK, M), 1), p, 0.0)
            dp = _dot_nt(v, do)
            ds = (p * (dp - delta_ref[qi, 0:1, :]) * SM_SCALE).astype(BF)
            dv = dv + _dot(p.astype(BF), do)
            dk = dk + _dot(ds, q)
            dqt_ref[qi] += _dot(k_t, ds)
            return dk, dv

        carry = step(kt, (jnp.zeros((TK, QW), F32), jnp.zeros((TK, KVL), F32)), True)
        dk, dv = lax.fori_loop(kt + 1, nq, lambda qi, c: step(qi, c, False), carry)
        dkv_ref[...] = dk + jnp.concatenate([dv, jnp.zeros((TK, QW - KVL), F32)], axis=1)

    out_shape = (jax.ShapeDtypeStruct((S, QW), F32), jax.ShapeDtypeStruct((nq, QW, M), F32))
    return pl.pallas_call(
        body, name="attn_bwd", out_shape=out_shape, grid=(nk,),
        in_specs=[_vmem(), _rows(TK, QW), _vmem(), _vmem(), _vmem()],
        out_specs=(_rows(TK, QW), _vmem()),
        compiler_params=_params(("arbitrary",)),
    )(qs, kv, dolat, lse, delta)


def _in_bwd(dqt, dkv, du, raw, qn, h1, x, dx2, mod, g_mix, w_in, g_q, g_kv, w_uq, wuk_cd, perm_t, cos4, sin4, csk,
            snk):
    S = x.shape[0]
    ts = 512
    n = S // ts
    nsub = ts // TQ
    M = HEADS * TQ

    def body(dqt_ref, dkv_ref, du_ref, raw_ref, qn_ref, h1_ref, x_ref, dx2_ref, mod_ref, gmix_ref, win_ref, gq_ref,
             gkv_ref, wuq_ref, wuk_ref, permt_ref, cos_ref, sin_ref, csk_ref, snk_ref,
             dx_ref, dwin_ref, dwuq_ref, dwuk_ref, dgq_ref, dgkv_ref, dsc1_ref, dsh1_ref, dgmix_ref, dwin_acc,
             dwuq_acc):
        i = pl.program_id(0)

        @pl.when(i == 0)
        def _():
            dwin_acc[...] = jnp.zeros_like(dwin_acc)
            dwuq_acc[...] = jnp.zeros_like(dwuq_acc)
            for r in (dwuk_ref, dgq_ref, dgkv_ref, dsc1_ref, dsh1_ref, dgmix_ref):
                r[...] = jnp.zeros_like(r)

        dq_blocks = [dqt_ref[a].T for a in range(nsub)]
        qn = qn_ref[...]
        dq_parts = []
        drope = jnp.zeros((ts, 2 * 128), F32)
        for hd in range(HEADS):
            dqh = jnp.concatenate([blk[hd * TQ:(hd + 1) * TQ, :] for blk in dq_blocks], axis=0)
            dq_lat = dqh[:, 0:KVL].astype(BF)
            dq_parts.append(_dot(dq_lat, wuk_ref[hd]))
            dwuk_ref[hd] += _dot_tn(dq_lat, qn[:, hd * NOPE:(hd + 1) * NOPE])
            drope = drope + _dot(dqh[:, KVL:QW].astype(BF), permt_ref[hd])
        do1 = drope[:, 0:128]
        do2 = drope[:, 128:256]
        cosv = cos_ref[...]
        sinv = sin_ref[...]
        dq_parts.append(do1 * cosv + do2 * sinv)
        dq_parts.append(do2 * cosv - do1 * sinv)
        dq = jnp.concatenate(dq_parts, axis=1).astype(BF)

        cq_raw = raw_ref[:, 0:QL]
        ckv_raw = raw_ref[:, QL:QL + KVL]
        rq = _rms(cq_raw)
        nq_ = cq_raw * rq
        gq = gq_ref[...]
        dwuq_acc[...] += _dot_tn((nq_ * gq).astype(BF), dq)
        dc_q = _dot_nt(dq, wuq_ref[...])
        dgq_ref[...] += _colsum(dc_q * nq_)
        dcq_raw = _rms_bwd(dc_q * gq, nq_, rq)

        dkv = dkv_ref[...]
        rk = _rms(ckv_raw)
        nk_ = ckv_raw * rk
        dc_kv = dkv[:, 0:KVL]
        dgkv_ref[...] += _colsum(dc_kv * nk_)
        dckv_raw = _rms_bwd(dc_kv * gkv_ref[...], nk_, rk)
        dkr_roped = dkv[:, KVL:QW]
        dkr = dkr_roped * csk_ref[...] - _swap_halves(dkr_roped) * snk_ref[...]

        dproj = jnp.concatenate([dcq_raw, dckv_raw, dkr, du_ref[...]], axis=1).astype(BF)
        dwin_acc[...] += _dot_tn(h1_ref[...], dproj)
        dh1 = _dot_nt(dproj, win_ref[...])

        sc1 = mod_ref[0:1, D:2 * D]
        gmix = gmix_ref[...]
        xv = x_ref[...]
        r1 = _rms(xv)
        xn1 = xv * r1
        dsc1_ref[...] += _colsum(dh1 * (xn1 * gmix))
        dsh1_ref[...] += _colsum(dh1)
        dgmix_ref[...] += _colsum(dh1 * (1.0 + sc1) * xn1)
        dx_ref[...] = dx2_ref[...] + _rms_bwd(dh1 * gmix * (1.0 + sc1), xn1, r1)

        @pl.when(i == n - 1)
        def _():
            dwin_ref[...] = dwin_acc[...].astype(BF)
            dwuq_ref[...] = dwuq_acc[...].astype(BF)

    out_shape = (
        jax.ShapeDtypeStruct((S, D), F32),
        jax.ShapeDtypeStruct((D, D), BF),
        jax.ShapeDtypeStruct((QL, 768), BF),
        jax.ShapeDtypeStruct((HEADS, KVL, NOPE), F32),
        jax.ShapeDtypeStruct((1, QL), F32), jax.ShapeDtypeStruct((1, KVL), F32),
        jax.ShapeDtypeStruct((1, D), F32), jax.ShapeDtypeStruct((1, D), F32), jax.ShapeDtypeStruct((1, D), F32),
    )
    in_specs = [pl.BlockSpec((nsub, QW, M), lambda i: (i, 0, 0)), _rows(ts, QW), _rows(ts, PW), _rows(ts, 384),
                _rows(ts, HEADS * NOPE), _rows(ts, D), _rows(ts, D), _rows(ts, D), _full(mod.shape), _full((1, D)),
                _full(w_in.shape), _full((1, QL)), _full((1, KVL)), _full(w_uq.shape), _full(wuk_cd.shape),
                _full(perm_t.shape), _rows(ts, 128), _rows(ts, 128), _rows(ts, 128), _rows(ts, 128)]
    out_specs = (_rows(ts, D), _full((D, D)), _full((QL, 768)), _full((HEADS, KVL, NOPE)), _full((1, QL)),
                 _full((1, KVL)), _full((1, D)), _full((1, D)), _full((1, D)))
    return pl.pallas_call(
        body, name="in_bwd", out_shape=out_shape, grid=(n,), in_specs=in_specs, out_specs=out_specs,
        scratch_shapes=[pltpu.VMEM((D, D), F32), pltpu.VMEM((QL, 768), F32)],
        compiler_params=_params(("arbitrary",)),
    )(dqt, dkv, du, raw, qn, h1, x, dx2, mod, g_mix, w_in, g_q, g_kv, w_uq, wuk_cd, perm_t, cos4, sin4, csk, snk)


def _rope_perm():
    p = np.zeros((HEADS, 2 * 128, 128), np.float32)
    for hd in range(HEADS):
        for t in range(HALF):
            p[hd, hd * HALF + t, t] = 1.0
            p[hd, 128 + hd * HALF + t, HALF + t] = 1.0
    return p


def _rope_tables(positions):
    freqs = jnp.power(ROPE_THETA, -jnp.arange(HALF, dtype=F32) / HALF)
    ang = positions.astype(F32)[:, None] * jnp.tile(freqs, HEADS)[None, :]
    cos4 = jnp.cos(ang)
    sin4 = jnp.sin(ang)
    lane = jnp.arange(HEADS * HALF)[None, :]
    csk = jnp.where(lane < ROPE, cos4, 0.0)
    snk = jnp.where(lane < HALF, -sin4, jnp.where(lane < ROPE, sin4, 0.0))
    return cos4, sin4, csk, snk


def _local_step(x, positions, target, mod, g_mix, w_in_p, g_q, g_kv, w_uq_p, w_uk, w_uv, w_pool, pool_scale, g_ffn,
                g_final, late, ffn_grads_exchange):
    perm = jnp.asarray(_rope_perm(), BF)
    perm_t = jnp.asarray(_rope_perm().transpose(0, 2, 1), BF)
    cos4, sin4, csk, snk = _rope_tables(positions)
    wuk_dc = w_uk.transpose(1, 2, 0).astype(BF)
    wuk_cd = w_uk.transpose(1, 0, 2).astype(BF)
    wuv_vc = w_uv.transpose(1, 2, 0).astype(BF)
    wpool = w_pool.astype(BF)
    wpool_dc = w_pool.transpose(0, 2, 1).astype(BF)

    h1, raw, qn, qs, kv, kvt, pooled, ypre, ypool = _fwd_in(
        x, mod, g_mix, w_in_p, g_q, g_kv, w_uq_p, wuk_dc, perm, cos4, sin4, csk, snk, wpool, pool_scale)
    olat, ymla, lse = _attn_fwd(qs, kv, kvt, wuv_vc)
    w_o, wg_t, wu_t, wd = late
    x2, mix, h2_t, a, b, dx3, dff, dff_t, loss, dgfin, dgt2 = _ffn_fwd(
        x, ymla, ypool, mod, w_o, g_ffn, wg_t, wu_t, wd, g_final, target)
    da, db, dh2 = _ffn_bwd_acts(dff, a, b, wg_t, wu_t, wd)
    dwg_t, dwu_t, dwd = _ffn_bwd_weights(dff_t, h2_t, da, db, a, b)
    ffn_parts = ffn_grads_exchange((dwg_t, dwu_t, dwd))
    (dx2, du, dolat, delta, dwo, dwuv, dwpool, dpscale, dgt1, dsc2, dsh2, dgffn) = _mix_bwd(
        dh2, dx3, x2, mix, mod, g_ffn, ymla, ypool, w_o, ypre, pooled, pool_scale, wpool_dc, olat, wuv_vc)
    dkv, dqt = _attn_bwd(qs, kv, dolat, lse, delta)
    dx, dwin, dwuq, dwuk, dgq, dgkv, dsc1, dsh1, dgmix = _in_bwd(
        dqt, dkv, du, raw, qn, h1, x, dx2, mod, g_mix, w_in_p, g_q, g_kv, w_uq_p, wuk_cd, perm_t, cos4, sin4, csk,
        snk)
    dmod = jnp.concatenate([dsh1, dsc1, dgt1, dsh2, dsc2, dgt2], axis=1)
    replicated = dict(
        w_uk=dwuk.transpose(1, 0, 2), w_uv=dwuv.transpose(1, 0, 2), w_pool=dwpool, g_mix=dgmix, g_q=dgq, g_kv=dgkv,
        pool_scale=dpscale, g_ffn=dgffn, g_final=dgfin)
    return loss[0, 0], dx, dmod, (dwin, dwuq, dwo), ffn_parts, replicated


def _my_pos():
    return lax.axis_index("x"), lax.axis_index("y"), lax.axis_index("c")


def _peer(pos, k):
    x, y, c = pos
    return (1 - x if k & 4 else x, 1 - y if k & 2 else y, 1 - c if k & 1 else c)


def _index(pos):
    x, y, c = pos
    return 4 * x + 2 * y + c


def _remote(src, dst, send_sem, recv_sem, to):
    return pltpu.make_async_remote_copy(src_ref=src, dst_ref=dst, send_sem=send_sem, recv_sem=recv_sem,
                                        device_id=to, device_id_type=MESH)


def _ada_mod(c, w_ada, b_ada):
    def body(c_ref, w_ref, b_ref, mod_ref, call_ref, cbuf, sbuf, rbuf, send1, recv1, send2, recv2):
        me = _my_pos()
        mi = _index(me)
        cv = c_ref[...]
        cbuf[...] = jnp.broadcast_to(cv * jax.nn.sigmoid(cv), (8, D))
        call_ref[mi] = cbuf[...]
        first = [_remote(cbuf, call_ref.at[mi], send1.at[k - 1], recv1.at[k - 1], _peer(me, k)) for k in range(1, NDEV)]
        for cp in first:
            cp.start()
        for k in range(1, NDEV):
            _remote(cbuf, call_ref.at[_index(_peer(me, k))], send1.at[k - 1], recv1.at[k - 1], _peer(me, k)).wait_recv()
        c_all = jnp.concatenate([call_ref[b][0:1, :] for b in range(NDEV)], axis=0)
        blocks = _dot(c_all.astype(BF), w_ref[...].astype(BF))
        for b in range(NDEV):
            sbuf[b] = jnp.broadcast_to(blocks[b:b + 1, :], (8, MODC))
        second = []
        for k in range(1, NDEV):
            to = _peer(me, k)
            second.append(_remote(sbuf.at[_index(to)], rbuf.at[mi], send2.at[k - 1], recv2.at[k - 1], to))
        for cp in second:
            cp.start()
        rbuf[mi] = sbuf[mi]
        for k in range(1, NDEV):
            to = _peer(me, k)
            _remote(sbuf.at[_index(to)], rbuf.at[_index(to)], send2.at[k - 1], recv2.at[k - 1], to).wait_recv()
        for j in range(NDEV):
            mod_ref[:, j * MODC:(j + 1) * MODC] = rbuf[j] + b_ref[:, j * MODC:(j + 1) * MODC]
        for cp in first + second:
            cp.wait_send()

    return pl.pallas_call(
        body, name="ada_mod",
        out_shape=(jax.ShapeDtypeStruct((8, N_MOD * D), F32), jax.ShapeDtypeStruct((NDEV, 8, D), F32)),
        in_specs=[_vmem(), _vmem(), _vmem()], out_specs=(_vmem(), _vmem()),
        scratch_shapes=[pltpu.VMEM((8, D), F32), pltpu.VMEM((NDEV, 8, MODC), F32), pltpu.VMEM((NDEV, 8, MODC), F32),
                        pltpu.SemaphoreType.DMA((NDEV - 1,)), pltpu.SemaphoreType.DMA((NDEV - 1,)),
                        pltpu.SemaphoreType.DMA((NDEV - 1,)), pltpu.SemaphoreType.DMA((NDEV - 1,))],
        compiler_params=_params(),
    )(c, w_ada, b_ada)


def _sequencer_scatter(name, collective_id, srcs, after=()):
    n = len(srcs)

    def of(src, to_index):
        r = src.shape[0] // NDEV
        return src.at[pl.ds(pl.multiple_of(to_index * r, 16), r), :]

    def body(*refs):
        src, zone = refs[:n], refs[n + len(after):2 * n + len(after)]
        send, recv, local = refs[2 * n + len(after):]
        me = _my_pos()
        mi = _index(me)
        barrier = pltpu.get_barrier_semaphore()
        for k in range(1, NDEV):
            pl.semaphore_signal(barrier, inc=1, device_id=_peer(me, k), device_id_type=MESH)
        pl.semaphore_wait(barrier, NDEV - 1)
        own = [pltpu.make_async_copy(of(src[a], mi), zone[a].at[mi], local.at[a]) for a in range(n)]
        for cp in own:
            cp.start()
        for a in range(n):
            for k in range(1, NDEV):
                to = _peer(me, k)
                s = a * (NDEV - 1) + k - 1
                _remote(of(src[a], _index(to)), zone[a].at[mi], send.at[s], recv.at[s], to).start()
        for cp in own:
            cp.wait()
        for a in range(n):
            for k in range(1, NDEV):
                to = _peer(me, k)
                s = a * (NDEV - 1) + k - 1
                cp = _remote(of(src[a], mi), zone[a].at[_index(to)], send.at[s], recv.at[s], to)
                cp.wait_send()
                cp.wait_recv()

    return pl.kernel(
        body, name=name, mesh=plsc.ScalarSubcoreMesh(axis_name="sequencer", num_cores=1),
        out_type=tuple(jax.ShapeDtypeStruct((NDEV, s.shape[0] // NDEV, s.shape[1]), s.dtype) for s in srcs),
        scratch_types=[pltpu.SemaphoreType.DMA((n * (NDEV - 1),)), pltpu.SemaphoreType.DMA((n * (NDEV - 1),)),
                       pltpu.SemaphoreType.DMA((n,))],
        compiler_params=pltpu.CompilerParams(collective_id=collective_id),
    )(*srcs, *after)


CHIP_PEERS = (2, 4, 6)


def _sequencer_gather(name, collective_id, srcs, after=()):
    n = len(srcs)
    per = NDEV - 1

    def body(*refs):
        src, zone = refs[:n], refs[n + len(after):2 * n + len(after)]
        send, recv, local = refs[2 * n + len(after):]
        me = _my_pos()
        mi = _index(me)
        sibling = _peer(me, 1)
        talk_to = (sibling,) + tuple(_peer(me, k) for k in CHIP_PEERS)
        barrier = pltpu.get_barrier_semaphore()
        for to in talk_to:
            pl.semaphore_signal(barrier, inc=1, device_id=to, device_id_type=MESH)
        pl.semaphore_wait(barrier, len(talk_to))

        def copy(a, slot, block_of, to, from_src=False):
            rows = zone[a].at[_index(block_of)]
            return _remote(src[a] if from_src else rows, rows, send.at[a * per + slot], recv.at[a * per + slot], to)

        own = [pltpu.make_async_copy(src[a], zone[a].at[mi], local.at[a]) for a in range(n)]
        for cp in own:
            cp.start()
        started = []
        for a in range(n):
            started.append(copy(a, 0, me, sibling, from_src=True))
            started += [copy(a, 1 + j, me, _peer(me, k), from_src=True) for j, k in enumerate(CHIP_PEERS)]
        for cp in started:
            cp.start()
        for a in range(n):
            for j, k in enumerate(CHIP_PEERS):
                copy(a, 1 + j, _peer(me, k), me).wait_recv()
                passed = copy(a, 4 + j, _peer(me, k), sibling)
                passed.start()
                started.append(passed)
        for a in range(n):
            copy(a, 0, sibling, me).wait_recv()
            for j, k in enumerate(CHIP_PEERS):
                copy(a, 4 + j, _peer(me, k | 1), me).wait_recv()
        for cp in started:
            cp.wait_send()
        for cp in own:
            cp.wait()

    return pl.kernel(
        body, name=name, mesh=plsc.ScalarSubcoreMesh(axis_name="sequencer", num_cores=1),
        out_type=tuple(jax.ShapeDtypeStruct((NDEV,) + s.shape, s.dtype) for s in srcs),
        scratch_types=[pltpu.SemaphoreType.DMA((n * per,)), pltpu.SemaphoreType.DMA((n * per,)),
                       pltpu.SemaphoreType.DMA((n,))],
        compiler_params=pltpu.CompilerParams(collective_id=collective_id),
    )(*srcs, *after)


def _blocked(shape, nb, axis=0):
    block = tuple(s // nb if d == axis else s for d, s in enumerate(shape))
    return pl.BlockSpec(block, lambda i: tuple(i if d == axis else 0 for d in range(len(shape))))


def _sum_partials(name, parts, nb):
    n = len(parts)

    def body(*refs):
        for a in range(n):
            acc = refs[a][0].astype(F32)
            for p in range(1, NDEV):
                acc = acc + refs[a][p].astype(F32)
            refs[n + a][...] = acc

    return pl.pallas_call(
        body, name=name, grid=(nb,),
        out_shape=tuple(jax.ShapeDtypeStruct(p.shape[1:], F32) for p in parts),
        in_specs=[_blocked(p.shape, nb, 1) for p in parts],
        out_specs=tuple(_blocked(p.shape[1:], nb) for p in parts), compiler_params=_params(("arbitrary",)),
    )(*parts)


def _small_all_reduce(buf):
    def body(buf_ref, got_ref, red_ref, mine, send1, recv1, send2, recv2):
        me = _my_pos()
        mi = _index(me)
        first = []
        for k in range(1, NDEV):
            to = _peer(me, k)
            first.append(_remote(buf_ref.at[_index(to)], got_ref.at[mi], send1.at[k - 1], recv1.at[k - 1], to))
        for cp in first:
            cp.start()
        got_ref[mi] = buf_ref[mi]
        for k in range(1, NDEV):
            to = _peer(me, k)
            _remote(buf_ref.at[mi], got_ref.at[_index(to)], send1.at[k - 1], recv1.at[k - 1], to).wait_recv()
        acc = got_ref[0]
        for p in range(1, NDEV):
            acc = acc + got_ref[p]
        mine[...] = acc
        second = [_remote(mine, red_ref.at[mi], send2.at[k - 1], recv2.at[k - 1], _peer(me, k)) for k in range(1, NDEV)]
        for cp in second:
            cp.start()
        red_ref[mi] = acc
        for k in range(1, NDEV):
            to = _peer(me, k)
            _remote(mine, red_ref.at[_index(to)], send2.at[k - 1], recv2.at[k - 1], to).wait_recv()
        for cp in first + second:
            cp.wait_send()

    return pl.pallas_call(
        body, name="small_all_reduce",
        out_shape=(jax.ShapeDtypeStruct(buf.shape, F32), jax.ShapeDtypeStruct(buf.shape, F32)),
        in_specs=[_vmem()], out_specs=(_vmem(), _vmem()),
        scratch_shapes=[pltpu.VMEM(buf.shape[1:], F32),
                        pltpu.SemaphoreType.DMA((NDEV - 1,)), pltpu.SemaphoreType.DMA((NDEV - 1,)),
                        pltpu.SemaphoreType.DMA((NDEV - 1,)), pltpu.SemaphoreType.DMA((NDEV - 1,))],
        compiler_params=_params(),
    )(buf)


def _adamw_math(w, g, m, v):
    m = ADAM_B1 * m + (1.0 - ADAM_B1) * g
    v = ADAM_B2 * v + (1.0 - ADAM_B2) * jnp.square(g)
    m_hat = m / (1.0 - ADAM_B1 ** ADAM_STEP)
    v_hat = v / (1.0 - ADAM_B2 ** ADAM_STEP)
    delta = -ADAM_LR * (m_hat / (jnp.sqrt(v_hat) + ADAM_EPS) + ADAM_WD * w)
    return delta, m, v


def _adamw_group(name, ws, gs, ms, vs, nb):
    n = len(ws)

    def body(*refs):
        for a in range(n):
            w, g, m, v = (refs[q * n + a][...] for q in range(4))
            delta, m2, v2 = _adamw_math(w, g, m, v)
            refs[4 * n + a][...] = delta
            refs[5 * n + a][...] = m2
            refs[6 * n + a][...] = v2

    shapes = tuple(jax.ShapeDtypeStruct(w.shape, F32) for w in ws)
    specs = [_blocked(w.shape, nb) for w in ws]
    outs = pl.pallas_call(
        body, name=name, grid=(nb,), out_shape=shapes * 3, in_specs=specs * 4, out_specs=tuple(specs * 3),
        compiler_params=_params(("arbitrary",)),
    )(*ws, *gs, *ms, *vs)
    return outs[:n], outs[n:2 * n], outs[2 * n:]


def _adamw_ada(w, m, v, c_all_t, dmod_rows):
    nb = 4

    def body(w_ref, m_ref, v_ref, c_ref, dm_ref, g_ref, d_ref, m2_ref, v2_ref):
        g = _dot(c_ref[...], dm_ref[...].astype(BF))
        g_ref[...] = g
        delta, m2, v2 = _adamw_math(w_ref[...], g, m_ref[...], v_ref[...])
        d_ref[...] = delta
        m2_ref[...] = m2
        v2_ref[...] = v2

    shp = jax.ShapeDtypeStruct(w.shape, F32)
    spec = _blocked(w.shape, nb)
    return pl.pallas_call(
        body, name="adamw_ada", grid=(nb,), out_shape=(shp, shp, shp, shp),
        in_specs=[spec, spec, spec, _blocked(c_all_t.shape, nb), _full(dmod_rows.shape)],
        out_specs=(spec, spec, spec, spec), compiler_params=_params(("arbitrary",)),
    )(w, m, v, c_all_t, dmod_rows)


def _w_in_to_kernel(w):
    return jnp.concatenate([w[:, 0:448], jnp.zeros((w.shape[0], 64), w.dtype), w[:, 448:960]], axis=1)


def _w_in_from_kernel(w):
    return jnp.concatenate([w[:, 0:448], w[:, 512:1024]], axis=1)


def _w_uq_to_kernel(w):
    r = w.shape[0]
    return jnp.concatenate([w[:, :, 0:NOPE].reshape(r, HEADS * NOPE),
                            w[:, :, NOPE:NOPE + HALF].reshape(r, HEADS * HALF),
                            w[:, :, NOPE + HALF:].reshape(r, HEADS * HALF)], axis=1)


def _w_uq_from_kernel(w):
    r = w.shape[0]
    return jnp.concatenate([w[:, 0:512].reshape(r, HEADS, NOPE), w[:, 512:640].reshape(r, HEADS, HALF),
                            w[:, 640:768].reshape(r, HEADS, HALF)], axis=2)


REP_NAMES = ("w_uk", "w_uv", "w_pool", "g_mix", "g_q", "g_kv", "pool_scale", "g_ffn", "g_final")


def kernel(x, c, positions, w_ada, b_ada, g_mix, w_in, g_q, g_kv, w_uq, w_uk, w_uv, w_pool, pool_scale, w_o, g_ffn, w_gate, w_up, w_down, g_final, loss_target, m_w_ada, m_b_ada, m_g_mix, m_w_in, m_g_q, m_g_kv, m_w_uq, m_w_uk, m_w_uv, m_w_pool, m_pool_scale, m_w_o, m_g_ffn, m_w_gate, m_w_up, m_w_down, m_g_final, v_w_ada, v_b_ada, v_g_mix, v_w_in, v_g_q, v_g_kv, v_w_uq, v_w_uk, v_w_uv, v_w_pool, v_pool_scale, v_w_o, v_g_ffn, v_w_gate, v_w_up, v_w_down, v_g_final):
    given = dict(locals())

    merge = lambda g: g.reshape(NDEV * g.shape[1], g.shape[2])
    w_in_p, w_uq_p = (merge(g) for g in _sequencer_gather(
        "gather_in", 3, (_w_in_to_kernel(w_in[0]).astype(BF), _w_uq_to_kernel(w_uq[0]).astype(BF))))

    mod, c_all8 = _ada_mod(c, w_ada[0], b_ada)
    c_all = c_all8[:, 0, :]
    late = _sequencer_gather(
        "gather_late", 1, (w_o[0].astype(BF), w_gate[0].T.astype(BF), w_up[0].T.astype(BF), w_down[0].astype(BF)),
        after=(mod[:, 0:128], w_in_p[0:16, 0:128], w_uq_p[0:16, 0:128]))

    def ffn_grads_exchange(arrays):
        return _sequencer_scatter("scatter_ffn", 2, arrays)

    loss, dx, dmod, tail_grads, ffn_parts, replicated = _local_step(
        x[0], positions[0], loss_target[0], mod, g_mix, w_in_p, g_q, g_kv, w_uq_p, w_uk[0], w_uv[0], w_pool[0],
        pool_scale, g_ffn, g_final.reshape(1, D), tuple(merge(g) for g in late), ffn_grads_exchange)

    flat = jnp.concatenate([replicated[k].reshape(-1) for k in REP_NAMES] + [loss.reshape(1)])
    flat = jnp.pad(flat, (0, NDEV * REP_ROWS * 128 - flat.shape[0])).reshape(NDEV, REP_ROWS, 128)
    dmod_blocks = jnp.pad(dmod.reshape(NDEV, MODC // 128, 128), ((0, 0), (0, MOD_ROWS - MODC // 128), (0, 0)))
    got, red = _small_all_reduce(jnp.concatenate([dmod_blocks, flat], axis=1))

    tail_parts = _sequencer_scatter("scatter_tail", 4, tail_grads,
                                    after=(ffn_parts[0][0, 0:16, 0:128], red[0, 0:8, :]))
    g_gate_t, g_up_t, g_down = _sum_partials("sum_ffn_partials", ffn_parts, 4)
    g_in_p, g_uq_p, g_o = _sum_partials("sum_tail_partials", tail_parts, 1)
    as_transpose = ("w_in", "w_gate", "w_up")
    grads = dict(w_in=_w_in_from_kernel(g_in_p).T, w_uq=_w_uq_from_kernel(g_uq_p), w_o=g_o, w_gate=g_gate_t,
                 w_up=g_up_t, w_down=g_down)
    dmod_rows = got[:, 0:MODC // 128, :].reshape(NDEV, MODC)
    grads["b_ada"] = red[:, 0:MODC // 128, :].reshape(1, N_MOD * D)
    rep_flat = red[:, MOD_ROWS:, :].reshape(-1)
    off = 0
    for k in REP_NAMES:
        size = int(np.prod(given[k].shape))
        grads[k] = rep_flat[off:off + size]
        off += size

    view = {k: (given[k].shape[1:] if given[k].ndim > 2 else given[k].shape)
            for k in REP_NAMES + ("b_ada", "w_ada", "w_in", "w_uq", "w_o", "w_gate", "w_up", "w_down")}
    view.update(g_final=(1, D))
    names = ["w_ada", "b_ada", "g_mix", "w_in", "g_q", "g_kv", "w_uq", "w_uk", "w_uv", "w_pool", "pool_scale",
             "w_o", "g_ffn", "w_gate", "w_up", "w_down", "g_final"]
    g_ada, d_ada, m_ada, v_ada = _adamw_ada(w_ada[0], m_w_ada[0], v_w_ada[0], c_all.T.astype(BF), dmod_rows)
    out_g, out_d, out_m, out_v = dict(w_ada=g_ada), dict(w_ada=d_ada), dict(w_ada=m_ada), dict(w_ada=v_ada)
    tail = ("w_in", "w_uq", "w_o")
    groups = (("adamw_ffn", ("w_gate", "w_up", "w_down"), 4),
              ("adamw_replicated", REP_NAMES + ("b_ada",), 1),
              ("adamw_tail", tail, 1))
    for gname, members, nb in groups:
        turn = lambda k, t: t.T if k in as_transpose else t
        ws = [turn(k, given[k].reshape(view[k])) for k in members]
        gs = [grads[k] if k in as_transpose else grads[k].reshape(view[k]) for k in members]
        ms = [turn(k, given["m_" + k].reshape(view[k])) for k in members]
        vs = [turn(k, given["v_" + k].reshape(view[k])) for k in members]
        ds, m2, v2 = _adamw_group(gname, ws, gs, ms, vs, nb)
        for k, g, d, mm, vv in zip(members, gs, ds, m2, v2):
            out_g[k], out_d[k], out_m[k], out_v[k] = turn(k, g), turn(k, d), turn(k, mm), turn(k, vv)

    total = rep_flat[off]
    shaped = lambda d: [d[k].reshape(given[k].shape) for k in names]
    return (total, dx[None], *shaped(out_g), *shaped(out_d), *shaped(out_m), *shaped(out_v))
```

```python
import numpy as np
import jax
import jax.numpy as jnp
from jax import lax
from jax.experimental import pallas as pl
from jax.experimental.pallas import tpu as pltpu
from jax.experimental.pallas import tpu_sc as plsc

D = 1024
HEADS = 4
NOPE = 128
ROPE = 64
HALF = ROPE // 2
QL = 256
KVL = 128
FF = 2816
PW = 512
GROUPS = 4
GD = 128
N_MOD = 6
EPS = 1e-6
SM_SCALE = (NOPE + ROPE) ** -0.5
LOG2_E = 1.4426950408889634
EXP2_SCALE = SM_SCALE * LOG2_E
ROPE_THETA = 10000.0
NDEV = 8
MODC = N_MOD * D // NDEV

ADAM_LR = 0.001
ADAM_B1 = 0.9
ADAM_B2 = 0.999
ADAM_EPS = 1e-08
ADAM_WD = 0.01
ADAM_STEP = 10

BF = jnp.bfloat16
F32 = jnp.float32
VMEM_LIMIT_V7X = 60 * 1024 * 1024
MESH = pl.DeviceIdType.MESH

TQ = 256
TK = 256
QW = 256
VPU_ROWS = 16
MOD_ROWS = 8
REP_ROWS = 200
SMALL_ROWS = MOD_ROWS + REP_ROWS


def _params(sem=None):
    return pltpu.CompilerParams(dimension_semantics=sem, vmem_limit_bytes=VMEM_LIMIT_V7X)


def _dot(a, b):
    return jnp.dot(a, b, preferred_element_type=F32)


def _dot_nt(a, b):
    return lax.dot_general(a, b, (((1,), (1,)), ((), ())), preferred_element_type=F32)


def _dot_tn(a, b):
    return _dot(a.astype(F32).T.astype(BF), b)


def _full(shape):
    return pl.BlockSpec(shape, lambda *_: (0,) * len(shape))


def _rows(ts, cols):
    return pl.BlockSpec((ts, cols), lambda i: (i, 0))


def _vmem():
    return pl.BlockSpec(memory_space=pltpu.VMEM)


def _any():
    return pl.BlockSpec(memory_space=pl.ANY)


def _rms(v):
    return lax.rsqrt(jnp.mean(v * v, axis=-1, keepdims=True) + EPS)


def _rms_bwd(dn, n, r):
    return r * (dn - n * jnp.mean(dn * n, axis=-1, keepdims=True))


def _colsum(v):
    return jnp.sum(v, axis=0, keepdims=True)


def _swap_halves(v):
    lane = lax.broadcasted_iota(jnp.int32, v.shape, 1)
    return jnp.where(lane < HALF, pltpu.roll(v, 128 - HALF, 1), pltpu.roll(v, HALF, 1))


def _window_lane_width():
    lane = lax.broadcasted_iota(jnp.int32, (1, PW), 1)
    return jnp.where(lane < 128, 2.0, jnp.where(lane < 256, 4.0, jnp.where(lane < 384, 8.0, 16.0))).astype(F32)


def _window_sums(ext, back):
    n = ext.shape[0]

    def sh(v, k):
        return pltpu.roll(v, k if back else n - k, 0)

    s2 = ext + sh(ext, 1)
    e4 = s2[:, 128:]
    s4 = e4 + sh(e4, 2)
    e8 = s4[:, 128:]
    s8 = e8 + sh(e8, 4)
    e16 = s8[:, 128:]
    s16 = e16 + sh(e16, 8)
    return jnp.concatenate([s2[:, :128], s4[:, :128], s8[:, :128], s16], axis=1)


def _row_counts(first_row, ts):
    t1 = (first_row + lax.broadcasted_iota(jnp.int32, (ts, 1), 0) + 1).astype(F32)
    return jnp.minimum(t1, _window_lane_width())


def _fwd_in(x, mod, g_mix, w_in, g_q, g_kv, w_uq, wuk_dc, perm, cos4, sin4, csk, snk, w_pool, pool_scale):
    S = x.shape[0]
    ts = 512
    nsub = ts // TQ

    def body(x_ref, mod_ref, gmix_ref, win_ref, gq_ref, gkv_ref, wuq_ref, wuk_ref, perm_ref, cos_ref, sin_ref,
             csk_ref, snk_ref, wpool_ref, pscale_ref,
             h1_ref, raw_ref, qn_ref, qs_ref, kv_ref, kvt_ref, pooled_ref, ypre_ref, ypool_ref, carry_ref):
        i = pl.program_id(0)

        @pl.when(i == 0)
        def _():
            carry_ref[...] = jnp.zeros_like(carry_ref)

        xv = x_ref[...]
        sh1 = mod_ref[0:1, 0:D]
        sc1 = mod_ref[0:1, D:2 * D]
        h = (xv * _rms(xv)) * gmix_ref[...] * (1.0 + sc1) + sh1
        hb = h.astype(BF)
        h1_ref[...] = hb
        proj = _dot(hb, win_ref[...])
        cq_raw = proj[:, 0:QL]
        ckv_raw = proj[:, QL:QL + KVL]
        kr = proj[:, 384:512]
        u = proj[:, 512:1024]
        raw_ref[...] = proj[:, 0:384]

        c_q = (cq_raw * _rms(cq_raw)) * gq_ref[...]
        c_kv = (ckv_raw * _rms(ckv_raw)) * gkv_ref[...]
        q = _dot(c_q.astype(BF), wuq_ref[...])
        qn = q[:, 0:HEADS * NOPE].astype(BF)
        qn_ref[...] = qn
        x1 = q[:, 512:640]
        x2 = q[:, 640:768]
        cosv = cos_ref[...]
        sinv = sin_ref[...]
        roped = jnp.concatenate([x1 * cosv - x2 * sinv, x1 * sinv + x2 * cosv], axis=1).astype(BF)
        for hd in range(HEADS):
            q_lat = _dot(qn[:, hd * NOPE:(hd + 1) * NOPE], wuk_ref[hd])
            q_rope = _dot(roped, perm_ref[hd])
            qh = jnp.concatenate([q_lat, q_rope], axis=1).astype(BF)
            for a in range(nsub):
                qs_ref[a, hd * TQ:(hd + 1) * TQ, :] = qh[a * TQ:(a + 1) * TQ, :]
        k_rope = kr * csk_ref[...] + _swap_halves(kr) * snk_ref[...]
        keys = jnp.concatenate([c_kv, k_rope], axis=1)
        kv_ref[...] = keys.astype(BF)
        for a in range(ts // TK):
            kvt_ref[a] = keys[a * TK:(a + 1) * TK, :].T.astype(BF)

        ext = jnp.concatenate([carry_ref[...], u], axis=0)
        win = _window_sums(ext, True)[16:, :]
        pooled = (win / _row_counts(i * ts, ts) - u).astype(BF)
        pooled_ref[...] = pooled
        carry_ref[...] = u[ts - 16:ts, :]
        ypre = jnp.concatenate(
            [_dot(pooled[:, g * GD:(g + 1) * GD], wpool_ref[g]) for g in range(GROUPS)], axis=1)
        ypre_ref[...] = ypre
        ypool_ref[...] = (ypre * pscale_ref[...]).astype(BF)

    out_shape = (
        jax.ShapeDtypeStruct((S, D), BF),
        jax.ShapeDtypeStruct((S, 384), F32),
        jax.ShapeDtypeStruct((S, HEADS * NOPE), BF),
        jax.ShapeDtypeStruct((S // TQ, HEADS * TQ, QW), BF),
        jax.ShapeDtypeStruct((S, QW), BF),
        jax.ShapeDtypeStruct((S // TK, QW, TK), BF),
        jax.ShapeDtypeStruct((S, PW), BF),
        jax.ShapeDtypeStruct((S, PW), F32),
        jax.ShapeDtypeStruct((S, PW), BF),
    )
    in_specs = [
        _rows(ts, D), _full(mod.shape), _full((1, D)), _full(w_in.shape), _full((1, QL)), _full((1, KVL)),
        _full(w_uq.shape), _full(wuk_dc.shape), _full(perm.shape), _rows(ts, 128), _rows(ts, 128), _rows(ts, 128),
        _rows(ts, 128), _full(w_pool.shape), _full((1, PW)),
    ]
    out_specs = (
        _rows(ts, D), _rows(ts, 384), _rows(ts, HEADS * NOPE),
        pl.BlockSpec((nsub, HEADS * TQ, QW), lambda i: (i, 0, 0)),
        _rows(ts, QW), pl.BlockSpec((ts // TK, QW, TK), lambda i: (i, 0, 0)), _rows(ts, PW), _rows(ts, PW),
        _rows(ts, PW),
    )
    return pl.pallas_call(
        body, name="fwd_in", out_shape=out_shape, grid=(S // ts,), in_specs=in_specs, out_specs=out_specs,
        scratch_shapes=[pltpu.VMEM((16, PW), F32)], compiler_params=_params(("arbitrary",)),
    )(x, mod, g_mix, w_in, g_q, g_kv, w_uq, wuk_dc, perm, cos4, sin4, csk, snk, w_pool, pool_scale)


def _diag_mask(shape, q_axis):
    qi = (lax.broadcasted_iota(jnp.int32, shape, q_axis) & (TQ - 1)) >> 6
    ki = lax.broadcasted_iota(jnp.int32, shape, 1 - q_axis) >> 6
    return ki <= qi


def _attn_fwd(qs, kv, kvt, wuv_vc):
    nq = qs.shape[0]
    S = kv.shape[0]
    M = HEADS * TQ

    def body(qs_ref, kv_ref, kvt_ref, wuv_ref, olat_ref, ymla_ref, lse_ref):
        i = pl.program_id(0)
        q = qs_ref[0]

        def step(kt, carry, masked):
            m, l, acc = carry
            k = kv_ref[pl.ds(pl.multiple_of(kt * TK, TK), TK), :]
            v_t = kvt_ref[kt][0:KVL, :]
            s = _dot_nt(k, q)
            if masked:
                s = jnp.where(_diag_mask((TK, M), 1), s, -jnp.inf)
            m_new = jnp.maximum(m, jnp.max(s, axis=0, keepdims=True))
            alpha = jnp.exp2((m - m_new) * EXP2_SCALE)
            p = jnp.exp2((s - m_new) * EXP2_SCALE)
            l = alpha * l + jnp.sum(p, axis=0, keepdims=True)
            acc = alpha * acc + _dot(v_t, p.astype(BF))
            return m_new, l, acc

        init = (jnp.full((1, M), -jnp.inf, F32), jnp.zeros((1, M), F32), jnp.zeros((KVL, M), F32))
        carry = lax.fori_loop(0, i, lambda kt, c: step(kt, c, False), init)
        m, l, acc = step(i, carry, True)
        o_lat = acc / l
        olat_ref[0] = o_lat
        lse_ref[0] = jnp.broadcast_to(m * SM_SCALE + jnp.log(l), (8, M))
        for hd in range(HEADS):
            o_t = _dot(wuv_ref[hd], o_lat[:, hd * TQ:(hd + 1) * TQ].astype(BF))
            ymla_ref[:, hd * 128:(hd + 1) * 128] = o_t.T.astype(BF)

    out_shape = (
        jax.ShapeDtypeStruct((nq, KVL, M), F32),
        jax.ShapeDtypeStruct((S, HEADS * 128), BF),
        jax.ShapeDtypeStruct((nq, 8, M), F32),
    )
    return pl.pallas_call(
        body, name="attn_fwd", out_shape=out_shape, grid=(nq,),
        in_specs=[pl.BlockSpec((1, M, QW), lambda i: (i, 0, 0)), _full(kv.shape), _full(kvt.shape),
                  _full(wuv_vc.shape)],
        out_specs=(pl.BlockSpec((1, KVL, M), lambda i: (i, 0, 0)), _rows(TQ, HEADS * 128),
                   pl.BlockSpec((1, 8, M), lambda i: (i, 0, 0))),
        compiler_params=_params(("arbitrary",)),
    )(qs, kv, kvt, wuv_vc)


def _silu_parts(a):
    sg = jax.nn.sigmoid(a)
    return sg, a * sg


def _ffn_fwd(x, ymla, ypool, mod, w_o, g_ffn, wg_t, wu_t, wd, g_final, target):
    S = x.shape[0]
    ts = 256

    def body(x_ref, ymla_ref, ypool_ref, mod_ref, wo_ref, gffn_ref, wg_ref, wu_ref, wd_ref, gfin_ref, t_ref,
             x2_ref, mix_ref, h2t_ref, a_ref, b_ref, dx3_ref, dff_ref, dfft_ref, loss_ref, dgfin_ref, dgt2_ref,
             f_ref):
        i = pl.program_id(0)

        @pl.when(i == 0)
        def _():
            loss_ref[...] = jnp.zeros_like(loss_ref)
            dgfin_ref[...] = jnp.zeros_like(dgfin_ref)
            dgt2_ref[...] = jnp.zeros_like(dgt2_ref)

        gt1 = mod_ref[0:1, 2 * D:3 * D]
        sh2 = mod_ref[0:1, 3 * D:4 * D]
        sc2 = mod_ref[0:1, 4 * D:5 * D]
        gt2 = mod_ref[0:1, 5 * D:6 * D]
        cat = jnp.concatenate([ymla_ref[...], ypool_ref[...]], axis=1)
        mix = _dot(cat, wo_ref[...])
        mix_ref[...] = mix
        x2 = x_ref[...] + gt1 * mix
        x2_ref[...] = x2
        h2 = (x2 * _rms(x2)) * gffn_ref[...] * (1.0 + sc2) + sh2
        h2b = h2.astype(BF)
        h2t_ref[...] = h2.T.astype(BF)

        for c in range(FF // FCHUNK):
            cols = slice(c * FCHUNK, (c + 1) * FCHUNK)
            a = _dot_nt(h2b, wg_ref[cols, :])
            b = _dot_nt(h2b, wu_ref[cols, :])
            a_ref[:, cols] = a.astype(BF)
            b_ref[:, cols] = b.astype(BF)
            f_ref[:, cols] = (_silu_parts(a)[1] * b).astype(BF)
        ff = _dot(f_ref[...], wd_ref[...])

        x3 = x2 + gt2 * ff
        r3 = _rms(x3)
        xn3 = x3 * r3
        gfin = gfin_ref[...]
        e = xn3 * gfin - t_ref[...]
        loss_ref[...] += 0.5 * jnp.sum(jnp.mean(e * e, axis=-1, keepdims=True))
        dy = e * (1.0 / D)
        dgfin_ref[...] += _colsum(dy * xn3)
        dx3 = _rms_bwd(dy * gfin, xn3, r3)
        dx3_ref[...] = dx3
        dgt2_ref[...] += _colsum(dx3 * ff)
        dff = dx3 * gt2
        dff_ref[...] = dff.astype(BF)
        dfft_ref[...] = dff.T.astype(BF)

    row = lambda c: _rows(ts, c)
    col = pl.BlockSpec((D, ts), lambda i: (0, i))
    const = _full
    out_shape = (
        jax.ShapeDtypeStruct((S, D), F32),
        jax.ShapeDtypeStruct((S, D), F32),
        jax.ShapeDtypeStruct((D, S), BF),
        jax.ShapeDtypeStruct((S, FF), BF),
        jax.ShapeDtypeStruct((S, FF), BF),
        jax.ShapeDtypeStruct((S, D), F32),
        jax.ShapeDtypeStruct((S, D), BF),
        jax.ShapeDtypeStruct((D, S), BF),
        jax.ShapeDtypeStruct((8, 128), F32),
        jax.ShapeDtypeStruct((1, D), F32),
        jax.ShapeDtypeStruct((1, D), F32),
    )
    return pl.pallas_call(
        body, name="ffn_fwd", out_shape=out_shape, grid=(S // ts,),
        in_specs=[row(D), row(PW), row(PW), const(mod.shape), _vmem(), const((1, D)), _vmem(), _vmem(), _vmem(),
                  const((1, D)), row(D)],
        out_specs=(row(D), row(D), col, row(FF), row(FF), row(D), row(D), col, const((8, 128)), const((1, D)),
                   const((1, D))),
        scratch_shapes=[pltpu.VMEM((ts, FF), BF)],
        compiler_params=_params(("arbitrary",)),
    )(x, ymla, ypool, mod, w_o, g_ffn, wg_t, wu_t, wd, g_final, target)


FCHUNK = 256


def _ffn_bwd_acts(dff, a, b, wg_t, wu_t, wd):
    S = dff.shape[0]
    ts = 512

    def body(dff_ref, a_ref, b_ref, wg_ref, wu_ref, wd_ref, da_ref, db_ref, dh2_ref):
        dffb = dff_ref[...]
        for c in range(FF // FCHUNK):
            cols = slice(c * FCHUNK, (c + 1) * FCHUNK)
            df = _dot_nt(dffb, wd_ref[cols, :])
            av = a_ref[:, cols].astype(F32)
            bv = b_ref[:, cols].astype(F32)
            sg, sa = _silu_parts(av)
            db_ref[:, cols] = (df * sa).astype(BF)
            da_ref[:, cols] = (df * bv * (sg * (1.0 + av * (1.0 - sg)))).astype(BF)
        dh2_ref[...] = _dot(da_ref[...], wg_ref[...]) + _dot(db_ref[...], wu_ref[...])

    act = _rows(ts, FF)
    return pl.pallas_call(
        body, name="ffn_bwd_acts",
        out_shape=(jax.ShapeDtypeStruct((S, FF), BF), jax.ShapeDtypeStruct((S, FF), BF),
                   jax.ShapeDtypeStruct((S, D), F32)),
        grid=(S // ts,), in_specs=[_rows(ts, D), act, act, _vmem(), _vmem(), _vmem()],
        out_specs=(act, act, _rows(ts, D)), compiler_params=_params(("arbitrary",)),
    )(dff, a, b, wg_t, wu_t, wd)


def _ffn_bwd_weights(dff_t, h2_t, da, db, a, b):
    S = da.shape[0]

    def body(dfft_ref, h2t_ref, da_ref, db_ref, a_ref, b_ref, dwg_ref, dwu_ref, dwd_ref):
        h2t = h2t_ref[...]
        dwg_ref[...] = _dot(h2t, da_ref[...]).T.astype(BF)
        dwu_ref[...] = _dot(h2t, db_ref[...]).T.astype(BF)
        f = (_silu_parts(a_ref[...].astype(F32))[1] * b_ref[...].astype(F32)).astype(BF)
        dwd_ref[...] = _dot(dfft_ref[...], f).T.astype(BF)

    act = pl.BlockSpec((S, FCHUNK), lambda j: (0, j))
    wblk = _rows(FCHUNK, D)
    shp = jax.ShapeDtypeStruct((FF, D), BF)
    return pl.pallas_call(
        body, name="ffn_bwd_weights", out_shape=(shp, shp, shp), grid=(FF // FCHUNK,),
        in_specs=[_vmem(), _vmem(), act, act, act, act], out_specs=(wblk, wblk, wblk),
        compiler_params=_params(("arbitrary",)),
    )(dff_t, h2_t, da, db, a, b)


def _mix_bwd(dh2, dx3, x2, mix, mod, g_ffn, ymla, ypool, w_o, ypre, pooled, pool_scale, wpool_dc, olat, wuv_vc):
    S = dh2.shape[0]
    ts = 512
    n = S // ts
    nsub = ts // TQ
    M = HEADS * TQ

    def body(dh2_ref, dx3_ref, x2_ref, mix_ref, mod_ref, gffn_ref, ymla_ref, ypool_ref, wo_ref, ypre_ref, pooled_ref,
             pscale_ref, wpool_ref, olat_ref, wuv_ref,
             dx2_ref, du_ref, dolat_ref, delta_ref, dwo_ref, dwuv_ref, dwpool_ref, dpscale_ref, dgt1_ref, dsc2_ref,
             dsh2_ref, dgffn_ref, carry_ref, dwo_acc):
        i = pl.program_id(0)

        @pl.when(i == 0)
        def _():
            carry_ref[...] = jnp.zeros_like(carry_ref)
            dwo_acc[...] = jnp.zeros_like(dwo_acc)
            for r in (dwuv_ref, dwpool_ref, dpscale_ref, dgt1_ref, dsc2_ref, dsh2_ref, dgffn_ref):
                r[...] = jnp.zeros_like(r)

        gt1 = mod_ref[0:1, 2 * D:3 * D]
        sc2 = mod_ref[0:1, 4 * D:5 * D]
        gffn = gffn_ref[...]
        dh2 = dh2_ref[...]
        x2 = x2_ref[...]
        r2 = _rms(x2)
        xn2 = x2 * r2
        dsc2_ref[...] += _colsum(dh2 * (xn2 * gffn))
        dsh2_ref[...] += _colsum(dh2)
        dgffn_ref[...] += _colsum(dh2 * (1.0 + sc2) * xn2)
        dx2 = dx3_ref[...] + _rms_bwd(dh2 * gffn * (1.0 + sc2), xn2, r2)
        dx2_ref[...] = dx2
        dgt1_ref[...] += _colsum(dx2 * mix_ref[...])
        dmix = (dx2 * gt1).astype(BF)
        cat = jnp.concatenate([ymla_ref[...], ypool_ref[...]], axis=1)
        dwo_acc[...] += _dot_tn(cat, dmix)
        dcat = _dot_nt(dmix, wo_ref[...])
        dymla = dcat[:, 0:512]
        dypool = dcat[:, 512:1024]

        dpscale_ref[...] += _colsum(dypool * ypre_ref[...])
        dypre = (dypool * pscale_ref[...]).astype(BF)
        pooled = pooled_ref[...]
        dpooled = []
        for g in range(GROUPS):
            sl = slice(g * GD, (g + 1) * GD)
            dwpool_ref[g] += _dot_tn(pooled[:, sl], dypre[:, sl])
            dpooled.append(_dot(dypre[:, sl], wpool_ref[g]))
        dpooled = jnp.concatenate(dpooled, axis=1)
        tile = n - 1 - i
        e = dpooled / _row_counts(tile * ts, ts)
        ext = jnp.concatenate([e, carry_ref[...]], axis=0)
        du_ref[...] = _window_sums(ext, False)[0:ts, :] - dpooled
        carry_ref[...] = e[0:16, :]

        for hd in range(HEADS):
            do = dymla[:, hd * 128:(hd + 1) * 128]
            dob = do.astype(BF)
            dol = _dot(dob, wuv_ref[hd])
            for a in range(nsub):
                ol_t = olat_ref[a, :, hd * TQ:(hd + 1) * TQ]
                dl = dol[a * TQ:(a + 1) * TQ, :]
                dolat_ref[a, hd * TQ:(hd + 1) * TQ, :] = dl.astype(BF)
                dwuv_ref[hd] += _dot(ol_t.astype(BF), dob[a * TQ:(a + 1) * TQ, :])
                delta = jnp.sum(dl * ol_t.T, axis=-1, keepdims=True)
                delta_ref[a, :, hd * TQ:(hd + 1) * TQ] = jnp.broadcast_to(delta, (TQ, 128)).T[0:8, :]

        @pl.when(i == n - 1)
        def _():
            dwo_ref[...] = dwo_acc[...].astype(BF)

    rev = lambda c: pl.BlockSpec((ts, c), lambda i: (n - 1 - i, 0))
    rev3 = lambda r, c: pl.BlockSpec((nsub, r, c), lambda i: (n - 1 - i, 0, 0))
    out_shape = (
        jax.ShapeDtypeStruct((S, D), F32),
        jax.ShapeDtypeStruct((S, PW), F32),
        jax.ShapeDtypeStruct((S // TQ, M, KVL), BF),
        jax.ShapeDtypeStruct((S // TQ, 8, M), F32),
        jax.ShapeDtypeStruct((D, D), BF),
        jax.ShapeDtypeStruct((HEADS, KVL, 128), F32),
        jax.ShapeDtypeStruct((GROUPS, GD, GD), F32),
        jax.ShapeDtypeStruct((1, PW), F32),
        jax.ShapeDtypeStruct((1, D), F32), jax.ShapeDtypeStruct((1, D), F32), jax.ShapeDtypeStruct((1, D), F32),
        jax.ShapeDtypeStruct((1, D), F32),
    )
    in_specs = [rev(D), rev(D), rev(D), rev(D), _full(mod.shape), _full((1, D)), rev(PW), rev(PW), _full(w_o.shape),
                rev(PW), rev(PW), _full((1, PW)), _full(wpool_dc.shape), rev3(KVL, M), _full(wuv_vc.shape)]
    out_specs = (rev(D), rev(PW), rev3(M, KVL), rev3(8, M), _full((D, D)), _full((HEADS, KVL, 128)),
                 _full((GROUPS, GD, GD)), _full((1, PW)), _full((1, D)), _full((1, D)), _full((1, D)), _full((1, D)))
    return pl.pallas_call(
        body, name="mix_bwd", out_shape=out_shape, grid=(n,), in_specs=in_specs, out_specs=out_specs,
        scratch_shapes=[pltpu.VMEM((16, PW), F32), pltpu.VMEM((D, D), F32)],
        compiler_params=_params(("arbitrary",)),
    )(dh2, dx3, x2, mix, mod, g_ffn, ymla, ypool, w_o, ypre, pooled, pool_scale, wpool_dc, olat, wuv_vc)


def _attn_bwd(qs, kv, dolat, lse, delta):
    nq = qs.shape[0]
    S = kv.shape[0]
    M = HEADS * TQ
    nk = S // TK

    def body(qs_ref, kv_ref, do_ref, lse_ref, delta_ref, dkv_ref, dqt_ref, p_ref, ds_ref):
        kt = pl.program_id(0)
        k = kv_ref[...]
        v = k[:, 0:KVL]
        k_t = k.astype(F32).T.astype(BF)

        @pl.when(kt == 0)
        def _():
            dqt_ref[...] = jnp.zeros_like(dqt_ref)

        def step(qi, carry, masked):
            dk, dv = carry
            q = qs_ref[qi]
            do = do_ref[qi]
            s = _dot_nt(k, q)
            dp = _dot_nt(v, do)
            lse_row = lse_ref[qi, 0:1, :] * LOG2_E
            delta_row = delta_ref[qi, 0:1, :]
            q_chunk = (lax.broadcasted_iota(jnp.int32, (1, M), 1) & (TQ - 1)) >> 6
            for r in range(0, TK, VPU_ROWS):
                rows = slice(r, r + VPU_ROWS)
                p = jnp.exp2(s[rows, :] * EXP2_SCALE - lse_row)
                if masked:
                    p = jnp.where((r >> 6) <= q_chunk, p, 0.0)
                p_ref[rows, :] = p.astype(BF)
                ds_ref[rows, :] = (p * (dp[rows, :] - delta_row) * SM_SCALE).astype(BF)
            ds = ds_ref[...]
            dv = dv + _dot(p_ref[...], do)
            dk = dk + _dot(ds, q)
            dqt_ref[qi] += _dot(k_t, ds)
            return dk, dv

        carry = step(kt, (jnp.zeros((TK, QW), F32), jnp.zeros((TK, KVL), F32)), True)
        dk, dv = lax.fori_loop(kt + 1, nq, lambda qi, c: step(qi, c, False), carry)
        dkv_ref[...] = dk + jnp.concatenate([dv, jnp.zeros((TK, QW - KVL), F32)], axis=1)

    out_shape = (jax.ShapeDtypeStruct((S, QW), F32), jax.ShapeDtypeStruct((nq, QW, M), F32))
    return pl.pallas_call(
        body, name="attn_bwd", out_shape=out_shape, grid=(nk,),
        in_specs=[_vmem(), _rows(TK, QW), _vmem(), _vmem(), _vmem()],
        out_specs=(_rows(TK, QW), _vmem()),
        scratch_shapes=[pltpu.VMEM((TK, M), BF), pltpu.VMEM((TK, M), BF)],
        compiler_params=_params(("arbitrary",)),
    )(qs, kv, dolat, lse, delta)


def _in_bwd(dqt, dkv, du, raw, qn, h1, x, dx2, mod, g_mix, w_in, g_q, g_kv, w_uq, wuk_cd, perm_t, cos4, sin4, csk,
            snk):
    S = x.shape[0]
    ts = 512
    n = S // ts
    nsub = ts // TQ
    M = HEADS * TQ

    def body(dqt_ref, dkv_ref, du_ref, raw_ref, qn_ref, h1_ref, x_ref, dx2_ref, mod_ref, gmix_ref, win_ref, gq_ref,
             gkv_ref, wuq_ref, wuk_ref, permt_ref, cos_ref, sin_ref, csk_ref, snk_ref,
             dx_ref, dwin_ref, dwuq_ref, dwuk_ref, dgq_ref, dgkv_ref, dsc1_ref, dsh1_ref, dgmix_ref, dwin_acc,
             dwuq_acc):
        i = pl.program_id(0)

        @pl.when(i == 0)
        def _():
            dwin_acc[...] = jnp.zeros_like(dwin_acc)
            dwuq_acc[...] = jnp.zeros_like(dwuq_acc)
            for r in (dwuk_ref, dgq_ref, dgkv_ref, dsc1_ref, dsh1_ref, dgmix_ref):
                r[...] = jnp.zeros_like(r)

        dq_blocks = [dqt_ref[a].T for a in range(nsub)]
        qn = qn_ref[...]
        dq_parts = []
        drope = jnp.zeros((ts, 2 * 128), F32)
        for hd in range(HEADS):
            dqh = jnp.concatenate([blk[hd * TQ:(hd + 1) * TQ, :] for blk in dq_blocks], axis=0)
            dq_lat = dqh[:, 0:KVL].astype(BF)
            dq_parts.append(_dot(dq_lat, wuk_ref[hd]))
            dwuk_ref[hd] += _dot_tn(dq_lat, qn[:, hd * NOPE:(hd + 1) * NOPE])
            drope = drope + _dot(dqh[:, KVL:QW].astype(BF), permt_ref[hd])
        do1 = drope[:, 0:128]
        do2 = drope[:, 128:256]
        cosv = cos_ref[...]
        sinv = sin_ref[...]
        dq_parts.append(do1 * cosv + do2 * sinv)
        dq_parts.append(do2 * cosv - do1 * sinv)
        dq = jnp.concatenate(dq_parts, axis=1).astype(BF)

        cq_raw = raw_ref[:, 0:QL]
        ckv_raw = raw_ref[:, QL:QL + KVL]
        rq = _rms(cq_raw)
        nq_ = cq_raw * rq
        gq = gq_ref[...]
        dwuq_acc[...] += _dot_tn((nq_ * gq).astype(BF), dq)
        dc_q = _dot_nt(dq, wuq_ref[...])
        dgq_ref[...] += _colsum(dc_q * nq_)
        dcq_raw = _rms_bwd(dc_q * gq, nq_, rq)

        dkv = dkv_ref[...]
        rk = _rms(ckv_raw)
        nk_ = ckv_raw * rk
        dc_kv = dkv[:, 0:KVL]
        dgkv_ref[...] += _colsum(dc_kv * nk_)
        dckv_raw = _rms_bwd(dc_kv * gkv_ref[...], nk_, rk)
        dkr_roped = dkv[:, KVL:QW]
        dkr = dkr_roped * csk_ref[...] - _swap_halves(dkr_roped) * snk_ref[...]

        dproj = jnp.concatenate([dcq_raw, dckv_raw, dkr, du_ref[...]], axis=1).astype(BF)
        dwin_acc[...] += _dot_tn(h1_ref[...], dproj)
        dh1 = _dot_nt(dproj, win_ref[...])

        sc1 = mod_ref[0:1, D:2 * D]
        gmix = gmix_ref[...]
        xv = x_ref[...]
        r1 = _rms(xv)
        xn1 = xv * r1
        dsc1_ref[...] += _colsum(dh1 * (xn1 * gmix))
        dsh1_ref[...] += _colsum(dh1)
        dgmix_ref[...] += _colsum(dh1 * (1.0 + sc1) * xn1)
        dx_ref[...] = dx2_ref[...] + _rms_bwd(dh1 * gmix * (1.0 + sc1), xn1, r1)

        @pl.when(i == n - 1)
        def _():
            dwin_ref[...] = dwin_acc[...].astype(BF)
            dwuq_ref[...] = dwuq_acc[...].astype(BF)

    out_shape = (
        jax.ShapeDtypeStruct((S, D), F32),
        jax.ShapeDtypeStruct((D, D), BF),
        jax.ShapeDtypeStruct((QL, 768), BF),
        jax.ShapeDtypeStruct((HEADS, KVL, NOPE), F32),
        jax.ShapeDtypeStruct((1, QL), F32), jax.ShapeDtypeStruct((1, KVL), F32),
        jax.ShapeDtypeStruct((1, D), F32), jax.ShapeDtypeStruct((1, D), F32), jax.ShapeDtypeStruct((1, D), F32),
    )
    in_specs = [pl.BlockSpec((nsub, QW, M), lambda i: (i, 0, 0)), _rows(ts, QW), _rows(ts, PW), _rows(ts, 384),
                _rows(ts, HEADS * NOPE), _rows(ts, D), _rows(ts, D), _rows(ts, D), _full(mod.shape), _full((1, D)),
                _full(w_in.shape), _full((1, QL)), _full((1, KVL)), _full(w_uq.shape), _full(wuk_cd.shape),
                _full(perm_t.shape), _rows(ts, 128), _rows(ts, 128), _rows(ts, 128), _rows(ts, 128)]
    out_specs = (_rows(ts, D), _full((D, D)), _full((QL, 768)), _full((HEADS, KVL, NOPE)), _full((1, QL)),
                 _full((1, KVL)), _full((1, D)), _full((1, D)), _full((1, D)))
    return pl.pallas_call(
        body, name="in_bwd", out_shape=out_shape, grid=(n,), in_specs=in_specs, out_specs=out_specs,
        scratch_shapes=[pltpu.VMEM((D, D), F32), pltpu.VMEM((QL, 768), F32)],
        compiler_params=_params(("arbitrary",)),
    )(dqt, dkv, du, raw, qn, h1, x, dx2, mod, g_mix, w_in, g_q, g_kv, w_uq, wuk_cd, perm_t, cos4, sin4, csk, snk)


def _rope_perm():
    p = np.zeros((HEADS, 2 * 128, 128), np.float32)
    for hd in range(HEADS):
        for t in range(HALF):
            p[hd, hd * HALF + t, t] = 1.0
            p[hd, 128 + hd * HALF + t, HALF + t] = 1.0
    return p


def _rope_tables(positions):
    freqs = jnp.power(ROPE_THETA, -jnp.arange(HALF, dtype=F32) / HALF)
    ang = positions.astype(F32)[:, None] * jnp.tile(freqs, HEADS)[None, :]
    cos4 = jnp.cos(ang)
    sin4 = jnp.sin(ang)
    lane = jnp.arange(HEADS * HALF)[None, :]
    csk = jnp.where(lane < ROPE, cos4, 0.0)
    snk = jnp.where(lane < HALF, -sin4, jnp.where(lane < ROPE, sin4, 0.0))
    return cos4, sin4, csk, snk


def _local_step(x, rope, target, mod, g_mix, w_in_p, g_q, g_kv, w_uq_p, w_uk, w_uv, w_pool, pool_scale, g_ffn,
                g_final, late, ffn_grads_exchange):
    perm = jnp.asarray(_rope_perm(), BF)
    perm_t = jnp.asarray(_rope_perm().transpose(0, 2, 1), BF)
    cos4, sin4, csk, snk = rope
    wuk_dc = w_uk.transpose(1, 2, 0).astype(BF)
    wuk_cd = w_uk.transpose(1, 0, 2).astype(BF)
    wuv_vc = w_uv.transpose(1, 2, 0).astype(BF)
    wpool = w_pool.astype(BF)
    wpool_dc = w_pool.transpose(0, 2, 1).astype(BF)

    h1, raw, qn, qs, kv, kvt, pooled, ypre, ypool = _fwd_in(
        x, mod, g_mix, w_in_p, g_q, g_kv, w_uq_p, wuk_dc, perm, cos4, sin4, csk, snk, wpool, pool_scale)
    olat, ymla, lse = _attn_fwd(qs, kv, kvt, wuv_vc)
    w_o, wg_t, wu_t, wd = late
    x2, mix, h2_t, a, b, dx3, dff, dff_t, loss, dgfin, dgt2 = _ffn_fwd(
        x, ymla, ypool, mod, w_o, g_ffn, wg_t, wu_t, wd, g_final, target)
    da, db, dh2 = _ffn_bwd_acts(dff, a, b, wg_t, wu_t, wd)
    dwg_t, dwu_t, dwd = _ffn_bwd_weights(dff_t, h2_t, da, db, a, b)
    ffn_parts = ffn_grads_exchange((dwg_t, dwu_t, dwd))
    (dx2, du, dolat, delta, dwo, dwuv, dwpool, dpscale, dgt1, dsc2, dsh2, dgffn) = _mix_bwd(
        dh2, dx3, x2, mix, mod, g_ffn, ymla, ypool, w_o, ypre, pooled, pool_scale, wpool_dc, olat, wuv_vc)
    dkv, dqt = _attn_bwd(qs, kv, dolat, lse, delta)
    dx, dwin, dwuq, dwuk, dgq, dgkv, dsc1, dsh1, dgmix = _in_bwd(
        dqt, dkv, du, raw, qn, h1, x, dx2, mod, g_mix, w_in_p, g_q, g_kv, w_uq_p, wuk_cd, perm_t, cos4, sin4, csk,
        snk)
    dmod = jnp.concatenate([dsh1, dsc1, dgt1, dsh2, dsc2, dgt2], axis=1)
    replicated = dict(
        w_uk=dwuk.transpose(1, 0, 2), w_uv=dwuv.transpose(1, 0, 2), w_pool=dwpool, g_mix=dgmix, g_q=dgq, g_kv=dgkv,
        pool_scale=dpscale, g_ffn=dgffn, g_final=dgfin)
    return loss[0, 0], dx, dmod, (dwin, dwuq, dwo), ffn_parts, replicated


def _my_pos():
    return lax.axis_index("x"), lax.axis_index("y"), lax.axis_index("c")


def _peer(pos, k):
    x, y, c = pos
    return (1 - x if k & 4 else x, 1 - y if k & 2 else y, 1 - c if k & 1 else c)


def _index(pos):
    x, y, c = pos
    return 4 * x + 2 * y + c


def _remote(src, dst, send_sem, recv_sem, to):
    return pltpu.make_async_remote_copy(src_ref=src, dst_ref=dst, send_sem=send_sem, recv_sem=recv_sem,
                                        device_id=to, device_id_type=MESH)


def _ada_mod(c, w_ada, b_ada, after):
    def body(c_ref, w_ref, b_ref, after_ref, mod_ref, call_ref, cbuf, sbuf, rbuf, send1, recv1, send2, recv2):
        me = _my_pos()
        mi = _index(me)
        cv = c_ref[...]
        cbuf[...] = jnp.broadcast_to(cv * jax.nn.sigmoid(cv), (8, D))
        call_ref[mi] = cbuf[...]
        first = [_remote(cbuf, call_ref.at[mi], send1.at[k - 1], recv1.at[k - 1], _peer(me, k)) for k in range(1, NDEV)]
        for cp in first:
            cp.start()
        for k in range(1, NDEV):
            _remote(cbuf, call_ref.at[_index(_peer(me, k))], send1.at[k - 1], recv1.at[k - 1], _peer(me, k)).wait_recv()
        c_all = jnp.concatenate([call_ref[b][0:1, :] for b in range(NDEV)], axis=0)
        blocks = _dot(c_all.astype(BF), w_ref[...].astype(BF))
        for b in range(NDEV):
            sbuf[b] = jnp.broadcast_to(blocks[b:b + 1, :], (8, MODC))
        second = []
        for k in range(1, NDEV):
            to = _peer(me, k)
            second.append(_remote(sbuf.at[_index(to)], rbuf.at[mi], send2.at[k - 1], recv2.at[k - 1], to))
        for cp in second:
            cp.start()
        rbuf[mi] = sbuf[mi]
        for k in range(1, NDEV):
            to = _peer(me, k)
            _remote(sbuf.at[_index(to)], rbuf.at[_index(to)], send2.at[k - 1], recv2.at[k - 1], to).wait_recv()
        for j in range(NDEV):
            mod_ref[:, j * MODC:(j + 1) * MODC] = rbuf[j] + b_ref[:, j * MODC:(j + 1) * MODC]
        for cp in first + second:
            cp.wait_send()

    return pl.pallas_call(
        body, name="ada_mod",
        out_shape=(jax.ShapeDtypeStruct((8, N_MOD * D), F32), jax.ShapeDtypeStruct((NDEV, 8, D), F32)),
        in_specs=[_vmem(), _vmem(), _vmem(), _any()], out_specs=(_vmem(), _vmem()),
        scratch_shapes=[pltpu.VMEM((8, D), F32), pltpu.VMEM((NDEV, 8, MODC), F32), pltpu.VMEM((NDEV, 8, MODC), F32),
                        pltpu.SemaphoreType.DMA((NDEV - 1,)), pltpu.SemaphoreType.DMA((NDEV - 1,)),
                        pltpu.SemaphoreType.DMA((NDEV - 1,)), pltpu.SemaphoreType.DMA((NDEV - 1,))],
        compiler_params=_params(),
    )(c, w_ada, b_ada, after)


def _sequencer_scatter(name, collective_id, srcs, after=()):
    n = len(srcs)

    def of(src, to_index):
        r = src.shape[0] // NDEV
        return src.at[pl.ds(pl.multiple_of(to_index * r, 16), r), :]

    def body(*refs):
        src, zone = refs[:n], refs[n + len(after):2 * n + len(after)]
        send, recv, local = refs[2 * n + len(after):]
        me = _my_pos()
        mi = _index(me)
        barrier = pltpu.get_barrier_semaphore()
        for k in range(1, NDEV):
            pl.semaphore_signal(barrier, inc=1, device_id=_peer(me, k), device_id_type=MESH)
        pl.semaphore_wait(barrier, NDEV - 1)
        own = [pltpu.make_async_copy(of(src[a], mi), zone[a].at[mi], local.at[a]) for a in range(n)]
        for cp in own:
            cp.start()
        for a in range(n):
            for k in range(1, NDEV):
                to = _peer(me, k)
                s = a * (NDEV - 1) + k - 1
                _remote(of(src[a], _index(to)), zone[a].at[mi], send.at[s], recv.at[s], to).start()
        for cp in own:
            cp.wait()
        for a in range(n):
            for k in range(1, NDEV):
                to = _peer(me, k)
                s = a * (NDEV - 1) + k - 1
                cp = _remote(of(src[a], mi), zone[a].at[_index(to)], send.at[s], recv.at[s], to)
                cp.wait_send()
                cp.wait_recv()

    return pl.kernel(
        body, name=name, mesh=plsc.ScalarSubcoreMesh(axis_name="sequencer", num_cores=1),
        out_type=tuple(jax.ShapeDtypeStruct((NDEV, s.shape[0] // NDEV, s.shape[1]), s.dtype) for s in srcs),
        scratch_types=[pltpu.SemaphoreType.DMA((n * (NDEV - 1),)), pltpu.SemaphoreType.DMA((n * (NDEV - 1),)),
                       pltpu.SemaphoreType.DMA((n,))],
        compiler_params=pltpu.CompilerParams(collective_id=collective_id),
    )(*srcs, *after)


CHIP_PEERS = (2, 4, 6)


def _sequencer_gather(name, collective_id, srcs, after=()):
    n = len(srcs)
    per = NDEV - 1

    def body(*refs):
        src, zone = refs[:n], refs[n + len(after):2 * n + len(after)]
        send, recv, local = refs[2 * n + len(after):]
        me = _my_pos()
        mi = _index(me)
        sibling = _peer(me, 1)
        talk_to = (sibling,) + tuple(_peer(me, k) for k in CHIP_PEERS)
        barrier = pltpu.get_barrier_semaphore()
        for to in talk_to:
            pl.semaphore_signal(barrier, inc=1, device_id=to, device_id_type=MESH)
        pl.semaphore_wait(barrier, len(talk_to))

        def copy(a, slot, block_of, to, from_src=False):
            rows = zone[a].at[_index(block_of)]
            return _remote(src[a] if from_src else rows, rows, send.at[a * per + slot], recv.at[a * per + slot], to)

        own = [pltpu.make_async_copy(src[a], zone[a].at[mi], local.at[a]) for a in range(n)]
        for cp in own:
            cp.start()
        started = []
        for a in range(n):
            started.append(copy(a, 0, me, sibling, from_src=True))
            started += [copy(a, 1 + j, me, _peer(me, k), from_src=True) for j, k in enumerate(CHIP_PEERS)]
        for cp in started:
            cp.start()
        for a in range(n):
            for j, k in enumerate(CHIP_PEERS):
                copy(a, 1 + j, _peer(me, k), me).wait_recv()
                passed = copy(a, 4 + j, _peer(me, k), sibling)
                passed.start()
                started.append(passed)
        for a in range(n):
            copy(a, 0, sibling, me).wait_recv()
            for j, k in enumerate(CHIP_PEERS):
                copy(a, 4 + j, _peer(me, k | 1), me).wait_recv()
        for cp in started:
            cp.wait_send()
        for cp in own:
            cp.wait()

    return pl.kernel(
        body, name=name, mesh=plsc.ScalarSubcoreMesh(axis_name="sequencer", num_cores=1),
        out_type=tuple(jax.ShapeDtypeStruct((NDEV,) + s.shape, s.dtype) for s in srcs),
        scratch_types=[pltpu.SemaphoreType.DMA((n * per,)), pltpu.SemaphoreType.DMA((n * per,)),
                       pltpu.SemaphoreType.DMA((n,))],
        compiler_params=pltpu.CompilerParams(collective_id=collective_id),
    )(*srcs, *after)


def _blocked(shape, nb, axis=0):
    block = tuple(s // nb if d == axis else s for d, s in enumerate(shape))
    return pl.BlockSpec(block, lambda i: tuple(i if d == axis else 0 for d in range(len(shape))))


def _sum_partials(name, parts, nb):
    n = len(parts)

    def body(*refs):
        for a in range(n):
            acc = refs[a][0].astype(F32)
            for p in range(1, NDEV):
                acc = acc + refs[a][p].astype(F32)
            refs[n + a][...] = acc

    return pl.pallas_call(
        body, name=name, grid=(nb,),
        out_shape=tuple(jax.ShapeDtypeStruct(p.shape[1:], F32) for p in parts),
        in_specs=[_blocked(p.shape, nb, 1) for p in parts],
        out_specs=tuple(_blocked(p.shape[1:], nb) for p in parts), compiler_params=_params(("arbitrary",)),
    )(*parts)


def _small_all_reduce(buf):
    def body(buf_ref, got_ref, red_ref, mine, send1, recv1, send2, recv2):
        me = _my_pos()
        mi = _index(me)
        first = []
        for k in range(1, NDEV):
            to = _peer(me, k)
            first.append(_remote(buf_ref.at[_index(to)], got_ref.at[mi], send1.at[k - 1], recv1.at[k - 1], to))
        for cp in first:
            cp.start()
        got_ref[mi] = buf_ref[mi]
        for k in range(1, NDEV):
            to = _peer(me, k)
            _remote(buf_ref.at[mi], got_ref.at[_index(to)], send1.at[k - 1], recv1.at[k - 1], to).wait_recv()
        acc = got_ref[0]
        for p in range(1, NDEV):
            acc = acc + got_ref[p]
        mine[...] = acc
        second = [_remote(mine, red_ref.at[mi], send2.at[k - 1], recv2.at[k - 1], _peer(me, k)) for k in range(1, NDEV)]
        for cp in second:
            cp.start()
        red_ref[mi] = acc
        for k in range(1, NDEV):
            to = _peer(me, k)
            _remote(mine, red_ref.at[_index(to)], send2.at[k - 1], recv2.at[k - 1], to).wait_recv()
        for cp in first + second:
            cp.wait_send()

    return pl.pallas_call(
        body, name="small_all_reduce",
        out_shape=(jax.ShapeDtypeStruct(buf.shape, F32), jax.ShapeDtypeStruct(buf.shape, F32)),
        in_specs=[_vmem()], out_specs=(_vmem(), _vmem()),
        scratch_shapes=[pltpu.VMEM(buf.shape[1:], F32),
                        pltpu.SemaphoreType.DMA((NDEV - 1,)), pltpu.SemaphoreType.DMA((NDEV - 1,)),
                        pltpu.SemaphoreType.DMA((NDEV - 1,)), pltpu.SemaphoreType.DMA((NDEV - 1,))],
        compiler_params=_params(),
    )(buf)


def _adamw_math(w, g, m, v):
    m = ADAM_B1 * m + (1.0 - ADAM_B1) * g
    v = ADAM_B2 * v + (1.0 - ADAM_B2) * jnp.square(g)
    m_hat = m / (1.0 - ADAM_B1 ** ADAM_STEP)
    v_hat = v / (1.0 - ADAM_B2 ** ADAM_STEP)
    delta = -ADAM_LR * (m_hat / (jnp.sqrt(v_hat) + ADAM_EPS) + ADAM_WD * w)
    return delta, m, v


def _adamw_group(name, ws, gs, ms, vs, nb):
    n = len(ws)

    def body(*refs):
        for a in range(n):
            w, g, m, v = (refs[q * n + a][...] for q in range(4))
            delta, m2, v2 = _adamw_math(w, g, m, v)
            refs[4 * n + a][...] = delta
            refs[5 * n + a][...] = m2
            refs[6 * n + a][...] = v2

    shapes = tuple(jax.ShapeDtypeStruct(w.shape, F32) for w in ws)
    specs = [_blocked(w.shape, nb) for w in ws]
    outs = pl.pallas_call(
        body, name=name, grid=(nb,), out_shape=shapes * 3, in_specs=specs * 4, out_specs=tuple(specs * 3),
        compiler_params=_params(("arbitrary",)),
    )(*ws, *gs, *ms, *vs)
    return outs[:n], outs[n:2 * n], outs[2 * n:]


def _adamw_from_partials(name, ws, parts, ms, vs, nb):
    n = len(ws)

    def body(*refs):
        for a in range(n):
            part = refs[n + a]
            g = part[0].astype(F32)
            for p in range(1, NDEV):
                g = g + part[p].astype(F32)
            delta, m2, v2 = _adamw_math(refs[a][...], g, refs[2 * n + a][...], refs[3 * n + a][...])
            refs[4 * n + a][...] = g
            refs[5 * n + a][...] = delta
            refs[6 * n + a][...] = m2
            refs[7 * n + a][...] = v2

    shapes = tuple(jax.ShapeDtypeStruct(w.shape, F32) for w in ws)
    specs = [_blocked(w.shape, nb) for w in ws]
    outs = pl.pallas_call(
        body, name=name, grid=(nb,), out_shape=shapes * 4,
        in_specs=specs + [_blocked(p.shape, nb, 1) for p in parts] + specs * 2, out_specs=tuple(specs * 4),
        compiler_params=_params(("arbitrary",)),
    )(*ws, *parts, *ms, *vs)
    return outs[:n], outs[n:2 * n], outs[2 * n:3 * n], outs[3 * n:]


def _adamw_ada(w, m, v, c_all_t, dmod_rows):
    nb = 4

    def body(w_ref, m_ref, v_ref, c_ref, dm_ref, g_ref, d_ref, m2_ref, v2_ref):
        g = _dot(c_ref[...], dm_ref[...].astype(BF))
        g_ref[...] = g
        delta, m2, v2 = _adamw_math(w_ref[...], g, m_ref[...], v_ref[...])
        d_ref[...] = delta
        m2_ref[...] = m2
        v2_ref[...] = v2

    shp = jax.ShapeDtypeStruct(w.shape, F32)
    spec = _blocked(w.shape, nb)
    return pl.pallas_call(
        body, name="adamw_ada", grid=(nb,), out_shape=(shp, shp, shp, shp),
        in_specs=[spec, spec, spec, _blocked(c_all_t.shape, nb), _full(dmod_rows.shape)],
        out_specs=(spec, spec, spec, spec), compiler_params=_params(("arbitrary",)),
    )(w, m, v, c_all_t, dmod_rows)


def _w_in_to_kernel(w):
    return jnp.concatenate([w[:, 0:448], jnp.zeros((w.shape[0], 64), w.dtype), w[:, 448:960]], axis=1)


def _w_in_from_kernel(w):
    return jnp.concatenate([w[:, 0:448], w[:, 512:1024]], axis=1)


def _w_uq_to_kernel(w):
    r = w.shape[0]
    return jnp.concatenate([w[:, :, 0:NOPE].reshape(r, HEADS * NOPE),
                            w[:, :, NOPE:NOPE + HALF].reshape(r, HEADS * HALF),
                            w[:, :, NOPE + HALF:].reshape(r, HEADS * HALF)], axis=1)


def _w_uq_from_kernel(w):
    r = w.shape[0]
    return jnp.concatenate([w[:, 0:512].reshape(r, HEADS, NOPE), w[:, 512:640].reshape(r, HEADS, HALF),
                            w[:, 640:768].reshape(r, HEADS, HALF)], axis=2)


REP_NAMES = ("w_uk", "w_uv", "w_pool", "g_mix", "g_q", "g_kv", "pool_scale", "g_ffn", "g_final")


def kernel(x, c, positions, w_ada, b_ada, g_mix, w_in, g_q, g_kv, w_uq, w_uk, w_uv, w_pool, pool_scale, w_o, g_ffn, w_gate, w_up, w_down, g_final, loss_target, m_w_ada, m_b_ada, m_g_mix, m_w_in, m_g_q, m_g_kv, m_w_uq, m_w_uk, m_w_uv, m_w_pool, m_pool_scale, m_w_o, m_g_ffn, m_w_gate, m_w_up, m_w_down, m_g_final, v_w_ada, v_b_ada, v_g_mix, v_w_in, v_g_q, v_g_kv, v_w_uq, v_w_uk, v_w_uv, v_w_pool, v_pool_scale, v_w_o, v_g_ffn, v_w_gate, v_w_up, v_w_down, v_g_final):
    given = dict(locals())

    merge = lambda g: g.reshape(NDEV * g.shape[1], g.shape[2])
    w_in_p, w_uq_p = (merge(g) for g in _sequencer_gather(
        "gather_in", 3, (_w_in_to_kernel(w_in[0]).astype(BF), _w_uq_to_kernel(w_uq[0]).astype(BF))))

    rope = _rope_tables(positions[0])
    mod, c_all8 = _ada_mod(c, w_ada[0], b_ada, rope[3][0:8, :])
    c_all = c_all8[:, 0, :]
    late = _sequencer_gather(
        "gather_late", 1, (w_o[0].astype(BF), w_gate[0].T.astype(BF), w_up[0].T.astype(BF), w_down[0].astype(BF)),
        after=(mod[:, 0:128], w_in_p[0:16, 0:128], w_uq_p[0:16, 0:128]))

    def ffn_grads_exchange(arrays):
        return _sequencer_scatter("scatter_ffn", 2, arrays)

    loss, dx, dmod, tail_grads, ffn_parts, replicated = _local_step(
        x[0], rope, loss_target[0], mod, g_mix, w_in_p, g_q, g_kv, w_uq_p, w_uk[0], w_uv[0], w_pool[0],
        pool_scale, g_ffn, g_final.reshape(1, D), tuple(merge(g) for g in late), ffn_grads_exchange)

    flat = jnp.concatenate([replicated[k].reshape(-1) for k in REP_NAMES] + [loss.reshape(1)])
    flat = jnp.pad(flat, (0, NDEV * REP_ROWS * 128 - flat.shape[0])).reshape(NDEV, REP_ROWS, 128)
    dmod_blocks = jnp.pad(dmod.reshape(NDEV, MODC // 128, 128), ((0, 0), (0, MOD_ROWS - MODC // 128), (0, 0)))
    got, red = _small_all_reduce(jnp.concatenate([dmod_blocks, flat], axis=1))

    tail_parts = _sequencer_scatter("scatter_tail", 4, tail_grads,
                                    after=(ffn_parts[0][0, 0:16, 0:128], red[0, 0:8, :]))
    g_in_p, g_uq_p = _sum_partials("sum_tail_partials", tail_parts[0:2], 1)
    as_transpose = ("w_in", "w_gate", "w_up")
    grads = dict(w_in=_w_in_from_kernel(g_in_p).T, w_uq=_w_uq_from_kernel(g_uq_p))
    partials = dict(w_gate=ffn_parts[0], w_up=ffn_parts[1], w_down=ffn_parts[2], w_o=tail_parts[2])
    dmod_rows = got[:, 0:MODC // 128, :].reshape(NDEV, MODC)
    grads["b_ada"] = red[:, 0:MODC // 128, :].reshape(1, N_MOD * D)
    rep_flat = red[:, MOD_ROWS:, :].reshape(-1)
    off = 0
    for k in REP_NAMES:
        size = int(np.prod(given[k].shape))
        grads[k] = rep_flat[off:off + size]
        off += size

    view = {k: (given[k].shape[1:] if given[k].ndim > 2 else given[k].shape)
            for k in REP_NAMES + ("b_ada", "w_ada", "w_in", "w_uq", "w_o", "w_gate", "w_up", "w_down")}
    view.update(g_final=(1, D))
    names = ["w_ada", "b_ada", "g_mix", "w_in", "g_q", "g_kv", "w_uq", "w_uk", "w_uv", "w_pool", "pool_scale",
             "w_o", "g_ffn", "w_gate", "w_up", "w_down", "g_final"]
    g_ada, d_ada, m_ada, v_ada = _adamw_ada(w_ada[0], m_w_ada[0], v_w_ada[0], c_all.T.astype(BF), dmod_rows)
    out_g, out_d, out_m, out_v = dict(w_ada=g_ada), dict(w_ada=d_ada), dict(w_ada=m_ada), dict(w_ada=v_ada)
    groups = (("adamw_ffn", ("w_gate", "w_up", "w_down"), 4),
              ("adamw_replicated", REP_NAMES + ("b_ada",), 1),
              ("adamw_w_o", ("w_o",), 1),
              ("adamw_tail", ("w_in", "w_uq"), 1))
    for gname, members, nb in groups:
        turn = lambda k, t: t.T if k in as_transpose else t
        ws = [turn(k, given[k].reshape(view[k])) for k in members]
        ms = [turn(k, given["m_" + k].reshape(view[k])) for k in members]
        vs = [turn(k, given["v_" + k].reshape(view[k])) for k in members]
        if members[0] in partials:
            gs, ds, m2, v2 = _adamw_from_partials(gname, ws, [partials[k] for k in members], ms, vs, nb)
        else:
            gs = [grads[k] if k in as_transpose else grads[k].reshape(view[k]) for k in members]
            ds, m2, v2 = _adamw_group(gname, ws, gs, ms, vs, nb)
        for k, g, d, mm, vv in zip(members, gs, ds, m2, v2):
            out_g[k], out_d[k], out_m[k], out_v[k] = turn(k, g), turn(k, d), turn(k, mm), turn(k, vv)

    total = rep_flat[off]
    shaped = lambda d: [d[k].reshape(given[k].shape) for k in names]
    return (total, dx[None], *shaped(out_g), *shaped(out_d), *shaped(out_m), *shaped(out_v))
```

```python
import numpy as np
import jax
import jax.numpy as jnp
from jax import lax
from jax.experimental import pallas as pl
from jax.experimental.pallas import tpu as pltpu
from jax.experimental.pallas import tpu_sc as plsc

D = 1024
HEADS = 4
NOPE = 128
ROPE = 64
HALF = ROPE // 2
QL = 256
KVL = 128
FF = 2816
PW = 512
GROUPS = 4
GD = 128
N_MOD = 6
EPS = 1e-6
SM_SCALE = (NOPE + ROPE) ** -0.5
LOG2_E = 1.4426950408889634
EXP2_SCALE = SM_SCALE * LOG2_E
ROPE_THETA = 10000.0
NDEV = 8
MODC = N_MOD * D // NDEV

ADAM_LR = 0.001
ADAM_B1 = 0.9
ADAM_B2 = 0.999
ADAM_EPS = 1e-08
ADAM_WD = 0.01
ADAM_STEP = 10

BF = jnp.bfloat16
F32 = jnp.float32
VMEM_LIMIT_V7X = 60 * 1024 * 1024
MESH = pl.DeviceIdType.MESH

TQ = 256
TK = 256
QW = 256
VPU_ROWS = 16
MOD_ROWS = 8
REP_ROWS = 200
SMALL_ROWS = MOD_ROWS + REP_ROWS


def _params(sem=None):
    return pltpu.CompilerParams(dimension_semantics=sem, vmem_limit_bytes=VMEM_LIMIT_V7X)


def _dot(a, b):
    return jnp.dot(a, b, preferred_element_type=F32)


def _dot_nt(a, b):
    return lax.dot_general(a, b, (((1,), (1,)), ((), ())), preferred_element_type=F32)


def _dot_tn(a, b):
    return _dot(a.astype(F32).T.astype(BF), b)


def _full(shape):
    return pl.BlockSpec(shape, lambda *_: (0,) * len(shape))


def _rows(ts, cols):
    return pl.BlockSpec((ts, cols), lambda i: (i, 0))


def _vmem():
    return pl.BlockSpec(memory_space=pltpu.VMEM)


def _any():
    return pl.BlockSpec(memory_space=pl.ANY)


def _rms(v):
    return lax.rsqrt(jnp.mean(v * v, axis=-1, keepdims=True) + EPS)


def _rms_bwd(dn, n, r):
    return r * (dn - n * jnp.mean(dn * n, axis=-1, keepdims=True))


def _colsum(v):
    return jnp.sum(v, axis=0, keepdims=True)


def _swap_halves(v):
    lane = lax.broadcasted_iota(jnp.int32, v.shape, 1)
    return jnp.where(lane < HALF, pltpu.roll(v, 128 - HALF, 1), pltpu.roll(v, HALF, 1))


def _window_lane_width():
    lane = lax.broadcasted_iota(jnp.int32, (1, PW), 1)
    return jnp.where(lane < 128, 2.0, jnp.where(lane < 256, 4.0, jnp.where(lane < 384, 8.0, 16.0))).astype(F32)


def _window_sums(ext, back):
    n = ext.shape[0]

    def sh(v, k):
        return pltpu.roll(v, k if back else n - k, 0)

    s2 = ext + sh(ext, 1)
    e4 = s2[:, 128:]
    s4 = e4 + sh(e4, 2)
    e8 = s4[:, 128:]
    s8 = e8 + sh(e8, 4)
    e16 = s8[:, 128:]
    s16 = e16 + sh(e16, 8)
    return jnp.concatenate([s2[:, :128], s4[:, :128], s8[:, :128], s16], axis=1)


def _row_counts(first_row, ts):
    t1 = (first_row + lax.broadcasted_iota(jnp.int32, (ts, 1), 0) + 1).astype(F32)
    return jnp.minimum(t1, _window_lane_width())


def _fwd_in(x, mod, g_mix, w_in, g_q, g_kv, w_uq, wuk_dc, perm, cos4, sin4, csk, snk, w_pool, pool_scale):
    S = x.shape[0]
    ts = 512
    nsub = ts // TQ

    def body(x_ref, mod_ref, gmix_ref, win_ref, gq_ref, gkv_ref, wuq_ref, wuk_ref, perm_ref, cos_ref, sin_ref,
             csk_ref, snk_ref, wpool_ref, pscale_ref,
             h1_ref, raw_ref, qn_ref, qs_ref, kv_ref, kvt_ref, pooled_ref, ypre_ref, ypool_ref, carry_ref):
        i = pl.program_id(0)

        @pl.when(i == 0)
        def _():
            carry_ref[...] = jnp.zeros_like(carry_ref)

        xv = x_ref[...]
        sh1 = mod_ref[0:1, 0:D]
        sc1 = mod_ref[0:1, D:2 * D]
        h = (xv * _rms(xv)) * gmix_ref[...] * (1.0 + sc1) + sh1
        hb = h.astype(BF)
        h1_ref[...] = hb
        proj = _dot(hb, win_ref[...])
        cq_raw = proj[:, 0:QL]
        ckv_raw = proj[:, QL:QL + KVL]
        kr = proj[:, 384:512]
        u = proj[:, 512:1024]
        raw_ref[...] = proj[:, 0:384]

        c_q = (cq_raw * _rms(cq_raw)) * gq_ref[...]
        c_kv = (ckv_raw * _rms(ckv_raw)) * gkv_ref[...]
        q = _dot(c_q.astype(BF), wuq_ref[...])
        qn = q[:, 0:HEADS * NOPE].astype(BF)
        qn_ref[...] = qn
        x1 = q[:, 512:640]
        x2 = q[:, 640:768]
        cosv = cos_ref[...]
        sinv = sin_ref[...]
        roped = jnp.concatenate([x1 * cosv - x2 * sinv, x1 * sinv + x2 * cosv], axis=1).astype(BF)
        for hd in range(HEADS):
            q_lat = _dot(qn[:, hd * NOPE:(hd + 1) * NOPE], wuk_ref[hd])
            q_rope = _dot(roped, perm_ref[hd])
            qh = jnp.concatenate([q_lat, q_rope], axis=1).astype(BF)
            for a in range(nsub):
                qs_ref[a, hd * TQ:(hd + 1) * TQ, :] = qh[a * TQ:(a + 1) * TQ, :]
        k_rope = kr * csk_ref[...] + _swap_halves(kr) * snk_ref[...]
        keys = jnp.concatenate([c_kv, k_rope], axis=1)
        kv_ref[...] = keys.astype(BF)
        for a in range(ts // TK):
            kvt_ref[a] = keys[a * TK:(a + 1) * TK, :].T.astype(BF)

        ext = jnp.concatenate([carry_ref[...], u], axis=0)
        win = _window_sums(ext, True)[16:, :]
        pooled = (win / _row_counts(i * ts, ts) - u).astype(BF)
        pooled_ref[...] = pooled
        carry_ref[...] = u[ts - 16:ts, :]
        ypre = jnp.concatenate(
            [_dot(pooled[:, g * GD:(g + 1) * GD], wpool_ref[g]) for g in range(GROUPS)], axis=1)
        ypre_ref[...] = ypre
        ypool_ref[...] = (ypre * pscale_ref[...]).astype(BF)

    out_shape = (
        jax.ShapeDtypeStruct((S, D), BF),
        jax.ShapeDtypeStruct((S, 384), F32),
        jax.ShapeDtypeStruct((S, HEADS * NOPE), BF),
        jax.ShapeDtypeStruct((S // TQ, HEADS * TQ, QW), BF),
        jax.ShapeDtypeStruct((S, QW), BF),
        jax.ShapeDtypeStruct((S // TK, QW, TK), BF),
        jax.ShapeDtypeStruct((S, PW), BF),
        jax.ShapeDtypeStruct((S, PW), F32),
        jax.ShapeDtypeStruct((S, PW), BF),
    )
    in_specs = [
        _rows(ts, D), _full(mod.shape), _full((1, D)), _full(w_in.shape), _full((1, QL)), _full((1, KVL)),
        _full(w_uq.shape), _full(wuk_dc.shape), _full(perm.shape), _rows(ts, 128), _rows(ts, 128), _rows(ts, 128),
        _rows(ts, 128), _full(w_pool.shape), _full((1, PW)),
    ]
    out_specs = (
        _rows(ts, D), _rows(ts, 384), _rows(ts, HEADS * NOPE),
        pl.BlockSpec((nsub, HEADS * TQ, QW), lambda i: (i, 0, 0)),
        _rows(ts, QW), pl.BlockSpec((ts // TK, QW, TK), lambda i: (i, 0, 0)), _rows(ts, PW), _rows(ts, PW),
        _rows(ts, PW),
    )
    return pl.pallas_call(
        body, name="fwd_in", out_shape=out_shape, grid=(S // ts,), in_specs=in_specs, out_specs=out_specs,
        scratch_shapes=[pltpu.VMEM((16, PW), F32)], compiler_params=_params(("arbitrary",)),
    )(x, mod, g_mix, w_in, g_q, g_kv, w_uq, wuk_dc, perm, cos4, sin4, csk, snk, w_pool, pool_scale)


def _diag_mask(shape, q_axis):
    qi = (lax.broadcasted_iota(jnp.int32, shape, q_axis) & (TQ - 1)) >> 6
    ki = lax.broadcasted_iota(jnp.int32, shape, 1 - q_axis) >> 6
    return ki <= qi


def _attn_fwd(qs, kv, kvt, wuv_vc):
    nq = qs.shape[0]
    S = kv.shape[0]
    M = HEADS * TQ

    def body(qs_ref, kv_ref, kvt_ref, wuv_ref, olat_ref, ymla_ref, lse_ref):
        i = pl.program_id(0)
        q = qs_ref[0]

        def step(kt, carry, masked):
            m, l, acc = carry
            k = kv_ref[pl.ds(pl.multiple_of(kt * TK, TK), TK), :]
            v_t = kvt_ref[kt][0:KVL, :]
            s = _dot_nt(k, q)
            if masked:
                s = jnp.where(_diag_mask((TK, M), 1), s, -jnp.inf)
            m_new = jnp.maximum(m, jnp.max(s, axis=0, keepdims=True))
            alpha = jnp.exp2((m - m_new) * EXP2_SCALE)
            p = jnp.exp2((s - m_new) * EXP2_SCALE)
            l = alpha * l + jnp.sum(p, axis=0, keepdims=True)
            acc = alpha * acc + _dot(v_t, p.astype(BF))
            return m_new, l, acc

        init = (jnp.full((1, M), -jnp.inf, F32), jnp.zeros((1, M), F32), jnp.zeros((KVL, M), F32))
        carry = lax.fori_loop(0, i, lambda kt, c: step(kt, c, False), init)
        m, l, acc = step(i, carry, True)
        o_lat = acc / l
        olat_ref[0] = o_lat
        lse_ref[0] = jnp.broadcast_to(m * SM_SCALE + jnp.log(l), (8, M))
        for hd in range(HEADS):
            o_t = _dot(wuv_ref[hd], o_lat[:, hd * TQ:(hd + 1) * TQ].astype(BF))
            ymla_ref[:, hd * 128:(hd + 1) * 128] = o_t.T.astype(BF)

    out_shape = (
        jax.ShapeDtypeStruct((nq, KVL, M), F32),
        jax.ShapeDtypeStruct((S, HEADS * 128), BF),
        jax.ShapeDtypeStruct((nq, 8, M), F32),
    )
    return pl.pallas_call(
        body, name="attn_fwd", out_shape=out_shape, grid=(nq,),
        in_specs=[pl.BlockSpec((1, M, QW), lambda i: (i, 0, 0)), _full(kv.shape), _full(kvt.shape),
                  _full(wuv_vc.shape)],
        out_specs=(pl.BlockSpec((1, KVL, M), lambda i: (i, 0, 0)), _rows(TQ, HEADS * 128),
                   pl.BlockSpec((1, 8, M), lambda i: (i, 0, 0))),
        compiler_params=_params(("arbitrary",)),
    )(qs, kv, kvt, wuv_vc)


def _silu_parts(a):
    sg = jax.nn.sigmoid(a)
    return sg, a * sg


def _ffn_fwd(x, ymla, ypool, mod, w_o, g_ffn, wg_t, wu_t, wd, g_final, target):
    S = x.shape[0]
    ts = 256

    def body(x_ref, ymla_ref, ypool_ref, mod_ref, wo_ref, gffn_ref, wg_ref, wu_ref, wd_ref, gfin_ref, t_ref,
             x2_ref, mix_ref, h2t_ref, a_ref, b_ref, dx3_ref, dff_ref, dfft_ref, loss_ref, dgfin_ref, dgt2_ref,
             f_ref):
        i = pl.program_id(0)

        @pl.when(i == 0)
        def _():
            loss_ref[...] = jnp.zeros_like(loss_ref)
            dgfin_ref[...] = jnp.zeros_like(dgfin_ref)
            dgt2_ref[...] = jnp.zeros_like(dgt2_ref)

        gt1 = mod_ref[0:1, 2 * D:3 * D]
        sh2 = mod_ref[0:1, 3 * D:4 * D]
        sc2 = mod_ref[0:1, 4 * D:5 * D]
        gt2 = mod_ref[0:1, 5 * D:6 * D]
        cat = jnp.concatenate([ymla_ref[...], ypool_ref[...]], axis=1)
        mix = _dot(cat, wo_ref[...])
        mix_ref[...] = mix
        x2 = x_ref[...] + gt1 * mix
        x2_ref[...] = x2
        h2 = (x2 * _rms(x2)) * gffn_ref[...] * (1.0 + sc2) + sh2
        h2b = h2.astype(BF)
        h2t_ref[...] = h2.T.astype(BF)

        for c in range(FF // FCHUNK):
            cols = slice(c * FCHUNK, (c + 1) * FCHUNK)
            a = _dot_nt(h2b, wg_ref[cols, :])
            b = _dot_nt(h2b, wu_ref[cols, :])
            a_ref[:, cols] = a.astype(BF)
            b_ref[:, cols] = b.astype(BF)
            f_ref[:, cols] = (_silu_parts(a)[1] * b).astype(BF)
        ff = _dot(f_ref[...], wd_ref[...])

        x3 = x2 + gt2 * ff
        r3 = _rms(x3)
        xn3 = x3 * r3
        gfin = gfin_ref[...]
        e = xn3 * gfin - t_ref[...]
        loss_ref[...] += 0.5 * jnp.sum(jnp.mean(e * e, axis=-1, keepdims=True))
        dy = e * (1.0 / D)
        dgfin_ref[...] += _colsum(dy * xn3)
        dx3 = _rms_bwd(dy * gfin, xn3, r3)
        dx3_ref[...] = dx3
        dgt2_ref[...] += _colsum(dx3 * ff)
        dff = dx3 * gt2
        dff_ref[...] = dff.astype(BF)
        dfft_ref[...] = dff.T.astype(BF)

    row = lambda c: _rows(ts, c)
    col = pl.BlockSpec((D, ts), lambda i: (0, i))
    const = _full
    out_shape = (
        jax.ShapeDtypeStruct((S, D), F32),
        jax.ShapeDtypeStruct((S, D), F32),
        jax.ShapeDtypeStruct((D, S), BF),
        jax.ShapeDtypeStruct((S, FF), BF),
        jax.ShapeDtypeStruct((S, FF), BF),
        jax.ShapeDtypeStruct((S, D), F32),
        jax.ShapeDtypeStruct((S, D), BF),
        jax.ShapeDtypeStruct((D, S), BF),
        jax.ShapeDtypeStruct((8, 128), F32),
        jax.ShapeDtypeStruct((1, D), F32),
        jax.ShapeDtypeStruct((1, D), F32),
    )
    return pl.pallas_call(
        body, name="ffn_fwd", out_shape=out_shape, grid=(S // ts,),
        in_specs=[row(D), row(PW), row(PW), const(mod.shape), _vmem(), const((1, D)), _vmem(), _vmem(), _vmem(),
                  const((1, D)), row(D)],
        out_specs=(row(D), row(D), col, row(FF), row(FF), row(D), row(D), col, const((8, 128)), const((1, D)),
                   const((1, D))),
        scratch_shapes=[pltpu.VMEM((ts, FF), BF)],
        compiler_params=_params(("arbitrary",)),
    )(x, ymla, ypool, mod, w_o, g_ffn, wg_t, wu_t, wd, g_final, target)


FCHUNK = 256


def _ffn_bwd_acts(dff, a, b, wg_t, wu_t, wd):
    S = dff.shape[0]
    ts = 512

    def body(dff_ref, a_ref, b_ref, wg_ref, wu_ref, wd_ref, da_ref, db_ref, dh2_ref):
        dffb = dff_ref[...]
        for c in range(FF // FCHUNK):
            cols = slice(c * FCHUNK, (c + 1) * FCHUNK)
            df = _dot_nt(dffb, wd_ref[cols, :])
            av = a_ref[:, cols].astype(F32)
            bv = b_ref[:, cols].astype(F32)
            sg, sa = _silu_parts(av)
            db_ref[:, cols] = (df * sa).astype(BF)
            da_ref[:, cols] = (df * bv * (sg * (1.0 + av * (1.0 - sg)))).astype(BF)
        dh2_ref[...] = _dot(da_ref[...], wg_ref[...]) + _dot(db_ref[...], wu_ref[...])

    act = _rows(ts, FF)
    return pl.pallas_call(
        body, name="ffn_bwd_acts",
        out_shape=(jax.ShapeDtypeStruct((S, FF), BF), jax.ShapeDtypeStruct((S, FF), BF),
                   jax.ShapeDtypeStruct((S, D), F32)),
        grid=(S // ts,), in_specs=[_rows(ts, D), act, act, _vmem(), _vmem(), _vmem()],
        out_specs=(act, act, _rows(ts, D)), compiler_params=_params(("arbitrary",)),
    )(dff, a, b, wg_t, wu_t, wd)


def _ffn_bwd_weights(dff_t, h2_t, da, db, a, b):
    S = da.shape[0]

    def body(dfft_ref, h2t_ref, da_ref, db_ref, a_ref, b_ref, dwg_ref, dwu_ref, dwd_ref):
        h2t = h2t_ref[...]
        dwg_ref[...] = _dot(h2t, da_ref[...]).T.astype(BF)
        dwu_ref[...] = _dot(h2t, db_ref[...]).T.astype(BF)
        f = (_silu_parts(a_ref[...].astype(F32))[1] * b_ref[...].astype(F32)).astype(BF)
        dwd_ref[...] = _dot(dfft_ref[...], f).T.astype(BF)

    act = pl.BlockSpec((S, FCHUNK), lambda j: (0, j))
    wblk = _rows(FCHUNK, D)
    shp = jax.ShapeDtypeStruct((FF, D), BF)
    return pl.pallas_call(
        body, name="ffn_bwd_weights", out_shape=(shp, shp, shp), grid=(FF // FCHUNK,),
        in_specs=[_vmem(), _vmem(), act, act, act, act], out_specs=(wblk, wblk, wblk),
        compiler_params=_params(("arbitrary",)),
    )(dff_t, h2_t, da, db, a, b)


def _mix_bwd(dh2, dx3, x2, mix, mod, g_ffn, ymla, ypool, w_o, ypre, pooled, pool_scale, wpool_dc, olat, wuv_vc):
    S = dh2.shape[0]
    ts = 512
    n = S // ts
    nsub = ts // TQ
    M = HEADS * TQ

    def body(dh2_ref, dx3_ref, x2_ref, mix_ref, mod_ref, gffn_ref, ymla_ref, ypool_ref, wo_ref, ypre_ref, pooled_ref,
             pscale_ref, wpool_ref, olat_ref, wuv_ref,
             dx2_ref, du_ref, dolat_ref, delta_ref, dwo_ref, dwuv_ref, dwpool_ref, dpscale_ref, dgt1_ref, dsc2_ref,
             dsh2_ref, dgffn_ref, carry_ref, dwo_acc):
        i = pl.program_id(0)

        @pl.when(i == 0)
        def _():
            carry_ref[...] = jnp.zeros_like(carry_ref)
            dwo_acc[...] = jnp.zeros_like(dwo_acc)
            for r in (dwuv_ref, dwpool_ref, dpscale_ref, dgt1_ref, dsc2_ref, dsh2_ref, dgffn_ref):
                r[...] = jnp.zeros_like(r)

        gt1 = mod_ref[0:1, 2 * D:3 * D]
        sc2 = mod_ref[0:1, 4 * D:5 * D]
        gffn = gffn_ref[...]
        dh2 = dh2_ref[...]
        x2 = x2_ref[...]
        r2 = _rms(x2)
        xn2 = x2 * r2
        dsc2_ref[...] += _colsum(dh2 * (xn2 * gffn))
        dsh2_ref[...] += _colsum(dh2)
        dgffn_ref[...] += _colsum(dh2 * (1.0 + sc2) * xn2)
        dx2 = dx3_ref[...] + _rms_bwd(dh2 * gffn * (1.0 + sc2), xn2, r2)
        dx2_ref[...] = dx2
        dgt1_ref[...] += _colsum(dx2 * mix_ref[...])
        dmix = (dx2 * gt1).astype(BF)
        cat = jnp.concatenate([ymla_ref[...], ypool_ref[...]], axis=1)
        dwo_acc[...] += _dot_tn(cat, dmix)
        dcat = _dot_nt(dmix, wo_ref[...])
        dymla = dcat[:, 0:512]
        dypool = dcat[:, 512:1024]

        dpscale_ref[...] += _colsum(dypool * ypre_ref[...])
        dypre = (dypool * pscale_ref[...]).astype(BF)
        pooled = pooled_ref[...]
        dpooled = []
        for g in range(GROUPS):
            sl = slice(g * GD, (g + 1) * GD)
            dwpool_ref[g] += _dot_tn(pooled[:, sl], dypre[:, sl])
            dpooled.append(_dot(dypre[:, sl], wpool_ref[g]))
        dpooled = jnp.concatenate(dpooled, axis=1)
        tile = n - 1 - i
        e = dpooled / _row_counts(tile * ts, ts)
        ext = jnp.concatenate([e, carry_ref[...]], axis=0)
        du_ref[...] = _window_sums(ext, False)[0:ts, :] - dpooled
        carry_ref[...] = e[0:16, :]

        for hd in range(HEADS):
            do = dymla[:, hd * 128:(hd + 1) * 128]
            dob = do.astype(BF)
            dol = _dot(dob, wuv_ref[hd])
            for a in range(nsub):
                ol_t = olat_ref[a, :, hd * TQ:(hd + 1) * TQ]
                dl = dol[a * TQ:(a + 1) * TQ, :]
                dolat_ref[a, hd * TQ:(hd + 1) * TQ, :] = dl.astype(BF)
                dwuv_ref[hd] += _dot(ol_t.astype(BF), dob[a * TQ:(a + 1) * TQ, :])
                delta = jnp.sum(dl * ol_t.T, axis=-1, keepdims=True)
                delta_ref[a, :, hd * TQ:(hd + 1) * TQ] = jnp.broadcast_to(delta, (TQ, 128)).T[0:8, :]

        @pl.when(i == n - 1)
        def _():
            dwo_ref[...] = dwo_acc[...].astype(BF)

    rev = lambda c: pl.BlockSpec((ts, c), lambda i: (n - 1 - i, 0))
    rev3 = lambda r, c: pl.BlockSpec((nsub, r, c), lambda i: (n - 1 - i, 0, 0))
    out_shape = (
        jax.ShapeDtypeStruct((S, D), F32),
        jax.ShapeDtypeStruct((S, PW), F32),
        jax.ShapeDtypeStruct((S // TQ, M, KVL), BF),
        jax.ShapeDtypeStruct((S // TQ, 8, M), F32),
        jax.ShapeDtypeStruct((D, D), BF),
        jax.ShapeDtypeStruct((HEADS, KVL, 128), F32),
        jax.ShapeDtypeStruct((GROUPS, GD, GD), F32),
        jax.ShapeDtypeStruct((1, PW), F32),
        jax.ShapeDtypeStruct((1, D), F32), jax.ShapeDtypeStruct((1, D), F32), jax.ShapeDtypeStruct((1, D), F32),
        jax.ShapeDtypeStruct((1, D), F32),
    )
    in_specs = [rev(D), rev(D), rev(D), rev(D), _full(mod.shape), _full((1, D)), rev(PW), rev(PW), _full(w_o.shape),
                rev(PW), rev(PW), _full((1, PW)), _full(wpool_dc.shape), rev3(KVL, M), _full(wuv_vc.shape)]
    out_specs = (rev(D), rev(PW), rev3(M, KVL), rev3(8, M), _full((D, D)), _full((HEADS, KVL, 128)),
                 _full((GROUPS, GD, GD)), _full((1, PW)), _full((1, D)), _full((1, D)), _full((1, D)), _full((1, D)))
    return pl.pallas_call(
        body, name="mix_bwd", out_shape=out_shape, grid=(n,), in_specs=in_specs, out_specs=out_specs,
        scratch_shapes=[pltpu.VMEM((16, PW), F32), pltpu.VMEM((D, D), F32)],
        compiler_params=_params(("arbitrary",)),
    )(dh2, dx3, x2, mix, mod, g_ffn, ymla, ypool, w_o, ypre, pooled, pool_scale, wpool_dc, olat, wuv_vc)


def _attn_bwd(qs, kv, dolat, lse, delta):
    nq = qs.shape[0]
    S = kv.shape[0]
    M = HEADS * TQ
    nk = S // TK

    def body(qs_ref, kv_ref, do_ref, lse_ref, delta_ref, dkv_ref, dqt_ref, p_ref, ds_ref):
        kt = pl.program_id(0)
        k = kv_ref[...]
        v = k[:, 0:KVL]
        k_t = k.astype(F32).T.astype(BF)

        @pl.when(kt == 0)
        def _():
            dqt_ref[...] = jnp.zeros_like(dqt_ref)

        def step(qi, carry, masked):
            dk, dv = carry
            q = qs_ref[qi]
            do = do_ref[qi]
            s = _dot_nt(k, q)
            dp = _dot_nt(v, do)
            lse_row = lse_ref[qi, 0:1, :] * LOG2_E
            delta_row = delta_ref[qi, 0:1, :]
            q_chunk = (lax.broadcasted_iota(jnp.int32, (1, M), 1) & (TQ - 1)) >> 6
            for r in range(0, TK, VPU_ROWS):
                rows = slice(r, r + VPU_ROWS)
                p = jnp.exp2(s[rows, :] * EXP2_SCALE - lse_row)
                if masked:
                    p = jnp.where((r >> 6) <= q_chunk, p, 0.0)
                p_ref[rows, :] = p.astype(BF)
                ds_ref[rows, :] = (p * (dp[rows, :] - delta_row) * SM_SCALE).astype(BF)
            ds = ds_ref[...]
            dv = dv + _dot(p_ref[...], do)
            dk = dk + _dot(ds, q)
            dqt_ref[qi] += _dot(k_t, ds)
            return dk, dv

        carry = step(kt, (jnp.zeros((TK, QW), F32), jnp.zeros((TK, KVL), F32)), True)
        dk, dv = lax.fori_loop(kt + 1, nq, lambda qi, c: step(qi, c, False), carry)
        dkv_ref[...] = dk + jnp.concatenate([dv, jnp.zeros((TK, QW - KVL), F32)], axis=1)

    out_shape = (jax.ShapeDtypeStruct((S, QW), F32), jax.ShapeDtypeStruct((nq, QW, M), F32))
    return pl.pallas_call(
        body, name="attn_bwd", out_shape=out_shape, grid=(nk,),
        in_specs=[_vmem(), _rows(TK, QW), _vmem(), _vmem(), _vmem()],
        out_specs=(_rows(TK, QW), _vmem()),
        scratch_shapes=[pltpu.VMEM((TK, M), BF), pltpu.VMEM((TK, M), BF)],
        compiler_params=_params(("arbitrary",)),
    )(qs, kv, dolat, lse, delta)


def _in_bwd(dqt, dkv, du, raw, qn, h1, x, dx2, mod, g_mix, w_in, g_q, g_kv, w_uq, wuk_cd, perm_t, cos4, sin4, csk,
            snk):
    S = x.shape[0]
    ts = 512
    n = S // ts
    nsub = ts // TQ
    M = HEADS * TQ

    def body(dqt_ref, dkv_ref, du_ref, raw_ref, qn_ref, h1_ref, x_ref, dx2_ref, mod_ref, gmix_ref, win_ref, gq_ref,
             gkv_ref, wuq_ref, wuk_ref, permt_ref, cos_ref, sin_ref, csk_ref, snk_ref,
             dx_ref, dwin_ref, dwuq_ref, dwuk_ref, dgq_ref, dgkv_ref, dsc1_ref, dsh1_ref, dgmix_ref, dwin_acc,
             dwuq_acc):
        i = pl.program_id(0)

        @pl.when(i == 0)
        def _():
            dwin_acc[...] = jnp.zeros_like(dwin_acc)
            dwuq_acc[...] = jnp.zeros_like(dwuq_acc)
            for r in (dwuk_ref, dgq_ref, dgkv_ref, dsc1_ref, dsh1_ref, dgmix_ref):
                r[...] = jnp.zeros_like(r)

        dq_blocks = [dqt_ref[a].T for a in range(nsub)]
        qn = qn_ref[...]
        dq_parts = []
        drope = jnp.zeros((ts, 2 * 128), F32)
        for hd in range(HEADS):
            dqh = jnp.concatenate([blk[hd * TQ:(hd + 1) * TQ, :] for blk in dq_blocks], axis=0)
            dq_lat = dqh[:, 0:KVL].astype(BF)
            dq_parts.append(_dot(dq_lat, wuk_ref[hd]))
            dwuk_ref[hd] += _dot_tn(dq_lat, qn[:, hd * NOPE:(hd + 1) * NOPE])
            drope = drope + _dot(dqh[:, KVL:QW].astype(BF), permt_ref[hd])
        do1 = drope[:, 0:128]
        do2 = drope[:, 128:256]
        cosv = cos_ref[...]
        sinv = sin_ref[...]
        dq_parts.append(do1 * cosv + do2 * sinv)
        dq_parts.append(do2 * cosv - do1 * sinv)
        dq = jnp.concatenate(dq_parts, axis=1).astype(BF)

        cq_raw = raw_ref[:, 0:QL]
        ckv_raw = raw_ref[:, QL:QL + KVL]
        rq = _rms(cq_raw)
        nq_ = cq_raw * rq
        gq = gq_ref[...]
        dwuq_acc[...] += _dot_tn((nq_ * gq).astype(BF), dq)
        dc_q = _dot_nt(dq, wuq_ref[...])
        dgq_ref[...] += _colsum(dc_q * nq_)
        dcq_raw = _rms_bwd(dc_q * gq, nq_, rq)

        dkv = dkv_ref[...]
        rk = _rms(ckv_raw)
        nk_ = ckv_raw * rk
        dc_kv = dkv[:, 0:KVL]
        dgkv_ref[...] += _colsum(dc_kv * nk_)
        dckv_raw = _rms_bwd(dc_kv * gkv_ref[...], nk_, rk)
        dkr_roped = dkv[:, KVL:QW]
        dkr = dkr_roped * csk_ref[...] - _swap_halves(dkr_roped) * snk_ref[...]

        dproj = jnp.concatenate([dcq_raw, dckv_raw, dkr, du_ref[...]], axis=1).astype(BF)
        dwin_acc[...] += _dot_tn(h1_ref[...], dproj)
        dh1 = _dot_nt(dproj, win_ref[...])

        sc1 = mod_ref[0:1, D:2 * D]
        gmix = gmix_ref[...]
        xv = x_ref[...]
        r1 = _rms(xv)
        xn1 = xv * r1
        dsc1_ref[...] += _colsum(dh1 * (xn1 * gmix))
        dsh1_ref[...] += _colsum(dh1)
        dgmix_ref[...] += _colsum(dh1 * (1.0 + sc1) * xn1)
        dx_ref[...] = dx2_ref[...] + _rms_bwd(dh1 * gmix * (1.0 + sc1), xn1, r1)

        @pl.when(i == n - 1)
        def _():
            dwin_ref[...] = dwin_acc[...].astype(BF)
            dwuq_ref[...] = dwuq_acc[...].astype(BF)

    out_shape = (
        jax.ShapeDtypeStruct((S, D), F32),
        jax.ShapeDtypeStruct((D, D), BF),
        jax.ShapeDtypeStruct((QL, 768), BF),
        jax.ShapeDtypeStruct((HEADS, KVL, NOPE), F32),
        jax.ShapeDtypeStruct((1, QL), F32), jax.ShapeDtypeStruct((1, KVL), F32),
        jax.ShapeDtypeStruct((1, D), F32), jax.ShapeDtypeStruct((1, D), F32), jax.ShapeDtypeStruct((1, D), F32),
    )
    in_specs = [pl.BlockSpec((nsub, QW, M), lambda i: (i, 0, 0)), _rows(ts, QW), _rows(ts, PW), _rows(ts, 384),
                _rows(ts, HEADS * NOPE), _rows(ts, D), _rows(ts, D), _rows(ts, D), _full(mod.shape), _full((1, D)),
                _full(w_in.shape), _full((1, QL)), _full((1, KVL)), _full(w_uq.shape), _full(wuk_cd.shape),
                _full(perm_t.shape), _rows(ts, 128), _rows(ts, 128), _rows(ts, 128), _rows(ts, 128)]
    out_specs = (_rows(ts, D), _full((D, D)), _full((QL, 768)), _full((HEADS, KVL, NOPE)), _full((1, QL)),
                 _full((1, KVL)), _full((1, D)), _full((1, D)), _full((1, D)))
    return pl.pallas_call(
        body, name="in_bwd", out_shape=out_shape, grid=(n,), in_specs=in_specs, out_specs=out_specs,
        scratch_shapes=[pltpu.VMEM((D, D), F32), pltpu.VMEM((QL, 768), F32)],
        compiler_params=_params(("arbitrary",)),
    )(dqt, dkv, du, raw, qn, h1, x, dx2, mod, g_mix, w_in, g_q, g_kv, w_uq, wuk_cd, perm_t, cos4, sin4, csk, snk)


def _rope_perm():
    p = np.zeros((HEADS, 2 * 128, 128), np.float32)
    for hd in range(HEADS):
        for t in range(HALF):
            p[hd, hd * HALF + t, t] = 1.0
            p[hd, 128 + hd * HALF + t, HALF + t] = 1.0
    return p


def _rope_tables(positions):
    freqs = jnp.power(ROPE_THETA, -jnp.arange(HALF, dtype=F32) / HALF)
    ang = positions.astype(F32)[:, None] * jnp.tile(freqs, HEADS)[None, :]
    cos4 = jnp.cos(ang)
    sin4 = jnp.sin(ang)
    lane = jnp.arange(HEADS * HALF)[None, :]
    csk = jnp.where(lane < ROPE, cos4, 0.0)
    snk = jnp.where(lane < HALF, -sin4, jnp.where(lane < ROPE, sin4, 0.0))
    return cos4, sin4, csk, snk


def _local_step(x, rope, target, mod, g_mix, w_in_p, g_q, g_kv, w_uq_p, w_uk, w_uv, w_pool, pool_scale, g_ffn,
                g_final, late, ffn_grads_exchange):
    perm = jnp.asarray(_rope_perm(), BF)
    perm_t = jnp.asarray(_rope_perm().transpose(0, 2, 1), BF)
    cos4, sin4, csk, snk = rope
    wuk_dc = w_uk.transpose(1, 2, 0).astype(BF)
    wuk_cd = w_uk.transpose(1, 0, 2).astype(BF)
    wuv_vc = w_uv.transpose(1, 2, 0).astype(BF)
    wpool = w_pool.astype(BF)
    wpool_dc = w_pool.transpose(0, 2, 1).astype(BF)

    h1, raw, qn, qs, kv, kvt, pooled, ypre, ypool = _fwd_in(
        x, mod, g_mix, w_in_p, g_q, g_kv, w_uq_p, wuk_dc, perm, cos4, sin4, csk, snk, wpool, pool_scale)
    olat, ymla, lse = _attn_fwd(qs, kv, kvt, wuv_vc)
    w_o, wg_t, wu_t, wd = late
    x2, mix, h2_t, a, b, dx3, dff, dff_t, loss, dgfin, dgt2 = _ffn_fwd(
        x, ymla, ypool, mod, w_o, g_ffn, wg_t, wu_t, wd, g_final, target)
    da, db, dh2 = _ffn_bwd_acts(dff, a, b, wg_t, wu_t, wd)
    (dx2, du, dolat, delta, dwo, dwuv, dwpool, dpscale, dgt1, dsc2, dsh2, dgffn) = _mix_bwd(
        dh2, dx3, x2, mix, mod, g_ffn, ymla, ypool, w_o, ypre, pooled, pool_scale, wpool_dc, olat, wuv_vc)
    dwg_t, dwu_t, dwd = _ffn_bwd_weights(dff_t, h2_t, da, db, a, b)
    ffn_parts = ffn_grads_exchange((dwg_t, dwu_t, dwd, dwo))
    dkv, dqt = _attn_bwd(qs, kv, dolat, lse, delta)
    dx, dwin, dwuq, dwuk, dgq, dgkv, dsc1, dsh1, dgmix = _in_bwd(
        dqt, dkv, du, raw, qn, h1, x, dx2, mod, g_mix, w_in_p, g_q, g_kv, w_uq_p, wuk_cd, perm_t, cos4, sin4, csk,
        snk)
    dmod = jnp.concatenate([dsh1, dsc1, dgt1, dsh2, dsc2, dgt2], axis=1)
    replicated = dict(
        w_uk=dwuk.transpose(1, 0, 2), w_uv=dwuv.transpose(1, 0, 2), w_pool=dwpool, g_mix=dgmix, g_q=dgq, g_kv=dgkv,
        pool_scale=dpscale, g_ffn=dgffn, g_final=dgfin)
    return loss[0, 0], dx, dmod, (dwin, dwuq), ffn_parts, replicated


def _my_pos():
    return lax.axis_index("x"), lax.axis_index("y"), lax.axis_index("c")


def _peer(pos, k):
    x, y, c = pos
    return (1 - x if k & 4 else x, 1 - y if k & 2 else y, 1 - c if k & 1 else c)


def _index(pos):
    x, y, c = pos
    return 4 * x + 2 * y + c


def _remote(src, dst, send_sem, recv_sem, to):
    return pltpu.make_async_remote_copy(src_ref=src, dst_ref=dst, send_sem=send_sem, recv_sem=recv_sem,
                                        device_id=to, device_id_type=MESH)


def _ada_mod(c, w_ada, b_ada, after):
    def body(c_ref, w_ref, b_ref, after_ref, mod_ref, call_ref, cbuf, sbuf, rbuf, send1, recv1, send2, recv2):
        me = _my_pos()
        mi = _index(me)
        cv = c_ref[...]
        cbuf[...] = jnp.broadcast_to(cv * jax.nn.sigmoid(cv), (8, D))
        call_ref[mi] = cbuf[...]
        first = [_remote(cbuf, call_ref.at[mi], send1.at[k - 1], recv1.at[k - 1], _peer(me, k)) for k in range(1, NDEV)]
        for cp in first:
            cp.start()
        for k in range(1, NDEV):
            _remote(cbuf, call_ref.at[_index(_peer(me, k))], send1.at[k - 1], recv1.at[k - 1], _peer(me, k)).wait_recv()
        c_all = jnp.concatenate([call_ref[b][0:1, :] for b in range(NDEV)], axis=0)
        blocks = _dot(c_all.astype(BF), w_ref[...].astype(BF))
        for b in range(NDEV):
            sbuf[b] = jnp.broadcast_to(blocks[b:b + 1, :], (8, MODC))
        second = []
        for k in range(1, NDEV):
            to = _peer(me, k)
            second.append(_remote(sbuf.at[_index(to)], rbuf.at[mi], send2.at[k - 1], recv2.at[k - 1], to))
        for cp in second:
            cp.start()
        rbuf[mi] = sbuf[mi]
        for k in range(1, NDEV):
            to = _peer(me, k)
            _remote(sbuf.at[_index(to)], rbuf.at[_index(to)], send2.at[k - 1], recv2.at[k - 1], to).wait_recv()
        for j in range(NDEV):
            mod_ref[:, j * MODC:(j + 1) * MODC] = rbuf[j] + b_ref[:, j * MODC:(j + 1) * MODC]
        for cp in first + second:
            cp.wait_send()

    return pl.pallas_call(
        body, name="ada_mod",
        out_shape=(jax.ShapeDtypeStruct((8, N_MOD * D), F32), jax.ShapeDtypeStruct((NDEV, 8, D), F32)),
        in_specs=[_vmem(), _vmem(), _vmem(), _any()], out_specs=(_vmem(), _vmem()),
        scratch_shapes=[pltpu.VMEM((8, D), F32), pltpu.VMEM((NDEV, 8, MODC), F32), pltpu.VMEM((NDEV, 8, MODC), F32),
                        pltpu.SemaphoreType.DMA((NDEV - 1,)), pltpu.SemaphoreType.DMA((NDEV - 1,)),
                        pltpu.SemaphoreType.DMA((NDEV - 1,)), pltpu.SemaphoreType.DMA((NDEV - 1,))],
        compiler_params=_params(),
    )(c, w_ada, b_ada, after)


def _sequencer_scatter(name, collective_id, srcs, after=()):
    n = len(srcs)

    def of(src, to_index):
        r = src.shape[0] // NDEV
        return src.at[pl.ds(pl.multiple_of(to_index * r, 16), r), :]

    def body(*refs):
        src, zone = refs[:n], refs[n + len(after):2 * n + len(after)]
        send, recv, local = refs[2 * n + len(after):]
        me = _my_pos()
        mi = _index(me)
        barrier = pltpu.get_barrier_semaphore()
        for k in range(1, NDEV):
            pl.semaphore_signal(barrier, inc=1, device_id=_peer(me, k), device_id_type=MESH)
        pl.semaphore_wait(barrier, NDEV - 1)
        own = [pltpu.make_async_copy(of(src[a], mi), zone[a].at[mi], local.at[a]) for a in range(n)]
        for cp in own:
            cp.start()
        for a in range(n):
            for k in range(1, NDEV):
                to = _peer(me, k)
                s = a * (NDEV - 1) + k - 1
                _remote(of(src[a], _index(to)), zone[a].at[mi], send.at[s], recv.at[s], to).start()
        for cp in own:
            cp.wait()
        for a in range(n):
            for k in range(1, NDEV):
                to = _peer(me, k)
                s = a * (NDEV - 1) + k - 1
                cp = _remote(of(src[a], mi), zone[a].at[_index(to)], send.at[s], recv.at[s], to)
                cp.wait_send()
                cp.wait_recv()

    return pl.kernel(
        body, name=name, mesh=plsc.ScalarSubcoreMesh(axis_name="sequencer", num_cores=1),
        out_type=tuple(jax.ShapeDtypeStruct((NDEV, s.shape[0] // NDEV, s.shape[1]), s.dtype) for s in srcs),
        scratch_types=[pltpu.SemaphoreType.DMA((n * (NDEV - 1),)), pltpu.SemaphoreType.DMA((n * (NDEV - 1),)),
                       pltpu.SemaphoreType.DMA((n,))],
        compiler_params=pltpu.CompilerParams(collective_id=collective_id),
    )(*srcs, *after)


CHIP_PEERS = (2, 4, 6)


def _sequencer_gather(name, collective_id, srcs, after=()):
    n = len(srcs)
    per = NDEV - 1

    def body(*refs):
        src, zone = refs[:n], refs[n + len(after):2 * n + len(after)]
        send, recv, local = refs[2 * n + len(after):]
        me = _my_pos()
        mi = _index(me)
        sibling = _peer(me, 1)
        talk_to = (sibling,) + tuple(_peer(me, k) for k in CHIP_PEERS)
        barrier = pltpu.get_barrier_semaphore()
        for to in talk_to:
            pl.semaphore_signal(barrier, inc=1, device_id=to, device_id_type=MESH)
        pl.semaphore_wait(barrier, len(talk_to))

        def copy(a, slot, block_of, to, from_src=False):
            rows = zone[a].at[_index(block_of)]
            return _remote(src[a] if from_src else rows, rows, send.at[a * per + slot], recv.at[a * per + slot], to)

        own = [pltpu.make_async_copy(src[a], zone[a].at[mi], local.at[a]) for a in range(n)]
        for cp in own:
            cp.start()
        started = []
        for a in range(n):
            started.append(copy(a, 0, me, sibling, from_src=True))
            started += [copy(a, 1 + j, me, _peer(me, k), from_src=True) for j, k in enumerate(CHIP_PEERS)]
        for cp in started:
            cp.start()
        for a in range(n):
            for j, k in enumerate(CHIP_PEERS):
                copy(a, 1 + j, _peer(me, k), me).wait_recv()
                passed = copy(a, 4 + j, _peer(me, k), sibling)
                passed.start()
                started.append(passed)
        for a in range(n):
            copy(a, 0, sibling, me).wait_recv()
            for j, k in enumerate(CHIP_PEERS):
                copy(a, 4 + j, _peer(me, k | 1), me).wait_recv()
        for cp in started:
            cp.wait_send()
        for cp in own:
            cp.wait()

    return pl.kernel(
        body, name=name, mesh=plsc.ScalarSubcoreMesh(axis_name="sequencer", num_cores=1),
        out_type=tuple(jax.ShapeDtypeStruct((NDEV,) + s.shape, s.dtype) for s in srcs),
        scratch_types=[pltpu.SemaphoreType.DMA((n * per,)), pltpu.SemaphoreType.DMA((n * per,)),
                       pltpu.SemaphoreType.DMA((n,))],
        compiler_params=pltpu.CompilerParams(collective_id=collective_id),
    )(*srcs, *after)


def _blocked(shape, nb, axis=0):
    block = tuple(s // nb if d == axis else s for d, s in enumerate(shape))
    return pl.BlockSpec(block, lambda i: tuple(i if d == axis else 0 for d in range(len(shape))))


def _sum_partials(name, parts, nb):
    n = len(parts)

    def body(*refs):
        for a in range(n):
            acc = refs[a][0].astype(F32)
            for p in range(1, NDEV):
                acc = acc + refs[a][p].astype(F32)
            refs[n + a][...] = acc

    return pl.pallas_call(
        body, name=name, grid=(nb,),
        out_shape=tuple(jax.ShapeDtypeStruct(p.shape[1:], F32) for p in parts),
        in_specs=[_blocked(p.shape, nb, 1) for p in parts],
        out_specs=tuple(_blocked(p.shape[1:], nb) for p in parts), compiler_params=_params(("arbitrary",)),
    )(*parts)


def _small_all_reduce(buf):
    def body(buf_ref, got_ref, red_ref, mine, send1, recv1, send2, recv2):
        me = _my_pos()
        mi = _index(me)
        first = []
        for k in range(1, NDEV):
            to = _peer(me, k)
            first.append(_remote(buf_ref.at[_index(to)], got_ref.at[mi], send1.at[k - 1], recv1.at[k - 1], to))
        for cp in first:
            cp.start()
        got_ref[mi] = buf_ref[mi]
        for k in range(1, NDEV):
            to = _peer(me, k)
            _remote(buf_ref.at[mi], got_ref.at[_index(to)], send1.at[k - 1], recv1.at[k - 1], to).wait_recv()
        acc = got_ref[0]
        for p in range(1, NDEV):
            acc = acc + got_ref[p]
        mine[...] = acc
        second = [_remote(mine, red_ref.at[mi], send2.at[k - 1], recv2.at[k - 1], _peer(me, k)) for k in range(1, NDEV)]
        for cp in second:
            cp.start()
        red_ref[mi] = acc
        for k in range(1, NDEV):
            to = _peer(me, k)
            _remote(mine, red_ref.at[_index(to)], send2.at[k - 1], recv2.at[k - 1], to).wait_recv()
        for cp in first + second:
            cp.wait_send()

    return pl.pallas_call(
        body, name="small_all_reduce",
        out_shape=(jax.ShapeDtypeStruct(buf.shape, F32), jax.ShapeDtypeStruct(buf.shape, F32)),
        in_specs=[_vmem()], out_specs=(_vmem(), _vmem()),
        scratch_shapes=[pltpu.VMEM(buf.shape[1:], F32),
                        pltpu.SemaphoreType.DMA((NDEV - 1,)), pltpu.SemaphoreType.DMA((NDEV - 1,)),
                        pltpu.SemaphoreType.DMA((NDEV - 1,)), pltpu.SemaphoreType.DMA((NDEV - 1,))],
        compiler_params=_params(),
    )(buf)


def _adamw_math(w, g, m, v):
    m = ADAM_B1 * m + (1.0 - ADAM_B1) * g
    v = ADAM_B2 * v + (1.0 - ADAM_B2) * jnp.square(g)
    m_hat = m / (1.0 - ADAM_B1 ** ADAM_STEP)
    v_hat = v / (1.0 - ADAM_B2 ** ADAM_STEP)
    delta = -ADAM_LR * (m_hat / (jnp.sqrt(v_hat) + ADAM_EPS) + ADAM_WD * w)
    return delta, m, v


def _adamw_group(name, ws, gs, ms, vs, nb):
    n = len(ws)

    def body(*refs):
        for a in range(n):
            w, g, m, v = (refs[q * n + a][...] for q in range(4))
            delta, m2, v2 = _adamw_math(w, g, m, v)
            refs[4 * n + a][...] = delta
            refs[5 * n + a][...] = m2
            refs[6 * n + a][...] = v2

    shapes = tuple(jax.ShapeDtypeStruct(w.shape, F32) for w in ws)
    specs = [_blocked(w.shape, nb) for w in ws]
    outs = pl.pallas_call(
        body, name=name, grid=(nb,), out_shape=shapes * 3, in_specs=specs * 4, out_specs=tuple(specs * 3),
        compiler_params=_params(("arbitrary",)),
    )(*ws, *gs, *ms, *vs)
    return outs[:n], outs[n:2 * n], outs[2 * n:]


def _adamw_from_partials(name, ws, parts, ms, vs, nb):
    n = len(ws)

    def body(*refs):
        for a in range(n):
            part = refs[n + a]
            g = part[0].astype(F32)
            for p in range(1, NDEV):
                g = g + part[p].astype(F32)
            delta, m2, v2 = _adamw_math(refs[a][...], g, refs[2 * n + a][...], refs[3 * n + a][...])
            refs[4 * n + a][...] = g
            refs[5 * n + a][...] = delta
            refs[6 * n + a][...] = m2
            refs[7 * n + a][...] = v2

    shapes = tuple(jax.ShapeDtypeStruct(w.shape, F32) for w in ws)
    specs = [_blocked(w.shape, nb) for w in ws]
    outs = pl.pallas_call(
        body, name=name, grid=(nb,), out_shape=shapes * 4,
        in_specs=specs + [_blocked(p.shape, nb, 1) for p in parts] + specs * 2, out_specs=tuple(specs * 4),
        compiler_params=_params(("arbitrary",)),
    )(*ws, *parts, *ms, *vs)
    return outs[:n], outs[n:2 * n], outs[2 * n:3 * n], outs[3 * n:]


def _adamw_ada(w, m, v, c_all_t, dmod_rows):
    nb = 4

    def body(w_ref, m_ref, v_ref, c_ref, dm_ref, g_ref, d_ref, m2_ref, v2_ref):
        g = _dot(c_ref[...], dm_ref[...].astype(BF))
        g_ref[...] = g
        delta, m2, v2 = _adamw_math(w_ref[...], g, m_ref[...], v_ref[...])
        d_ref[...] = delta
        m2_ref[...] = m2
        v2_ref[...] = v2

    shp = jax.ShapeDtypeStruct(w.shape, F32)
    spec = _blocked(w.shape, nb)
    return pl.pallas_call(
        body, name="adamw_ada", grid=(nb,), out_shape=(shp, shp, shp, shp),
        in_specs=[spec, spec, spec, _blocked(c_all_t.shape, nb), _full(dmod_rows.shape)],
        out_specs=(spec, spec, spec, spec), compiler_params=_params(("arbitrary",)),
    )(w, m, v, c_all_t, dmod_rows)


def _w_in_to_kernel(w):
    return jnp.concatenate([w[:, 0:448], jnp.zeros((w.shape[0], 64), w.dtype), w[:, 448:960]], axis=1)


def _w_in_from_kernel(w):
    return jnp.concatenate([w[:, 0:448], w[:, 512:1024]], axis=1)


def _w_uq_to_kernel(w):
    r = w.shape[0]
    return jnp.concatenate([w[:, :, 0:NOPE].reshape(r, HEADS * NOPE),
                            w[:, :, NOPE:NOPE + HALF].reshape(r, HEADS * HALF),
                            w[:, :, NOPE + HALF:].reshape(r, HEADS * HALF)], axis=1)


def _w_uq_from_kernel(w):
    r = w.shape[0]
    return jnp.concatenate([w[:, 0:512].reshape(r, HEADS, NOPE), w[:, 512:640].reshape(r, HEADS, HALF),
                            w[:, 640:768].reshape(r, HEADS, HALF)], axis=2)


REP_NAMES = ("w_uk", "w_uv", "w_pool", "g_mix", "g_q", "g_kv", "pool_scale", "g_ffn", "g_final")


def kernel(x, c, positions, w_ada, b_ada, g_mix, w_in, g_q, g_kv, w_uq, w_uk, w_uv, w_pool, pool_scale, w_o, g_ffn, w_gate, w_up, w_down, g_final, loss_target, m_w_ada, m_b_ada, m_g_mix, m_w_in, m_g_q, m_g_kv, m_w_uq, m_w_uk, m_w_uv, m_w_pool, m_pool_scale, m_w_o, m_g_ffn, m_w_gate, m_w_up, m_w_down, m_g_final, v_w_ada, v_b_ada, v_g_mix, v_w_in, v_g_q, v_g_kv, v_w_uq, v_w_uk, v_w_uv, v_w_pool, v_pool_scale, v_w_o, v_g_ffn, v_w_gate, v_w_up, v_w_down, v_g_final):
    given = dict(locals())

    merge = lambda g: g.reshape(NDEV * g.shape[1], g.shape[2])
    w_in_p, w_uq_p = (merge(g) for g in _sequencer_gather(
        "gather_in", 3, (_w_in_to_kernel(w_in[0]).astype(BF), _w_uq_to_kernel(w_uq[0]).astype(BF))))

    rope = _rope_tables(positions[0])
    mod, c_all8 = _ada_mod(c, w_ada[0], b_ada, rope[3][0:8, :])
    c_all = c_all8[:, 0, :]
    late = _sequencer_gather(
        "gather_late", 1, (w_o[0].astype(BF), w_gate[0].T.astype(BF), w_up[0].T.astype(BF), w_down[0].astype(BF)),
        after=(mod[:, 0:128], w_in_p[0:16, 0:128], w_uq_p[0:16, 0:128]))

    def ffn_grads_exchange(arrays):
        return _sequencer_scatter("scatter_ffn", 2, arrays)

    loss, dx, dmod, tail_grads, ffn_parts, replicated = _local_step(
        x[0], rope, loss_target[0], mod, g_mix, w_in_p, g_q, g_kv, w_uq_p, w_uk[0], w_uv[0], w_pool[0],
        pool_scale, g_ffn, g_final.reshape(1, D), tuple(merge(g) for g in late), ffn_grads_exchange)

    flat = jnp.concatenate([replicated[k].reshape(-1) for k in REP_NAMES] + [loss.reshape(1)])
    flat = jnp.pad(flat, (0, NDEV * REP_ROWS * 128 - flat.shape[0])).reshape(NDEV, REP_ROWS, 128)
    dmod_blocks = jnp.pad(dmod.reshape(NDEV, MODC // 128, 128), ((0, 0), (0, MOD_ROWS - MODC // 128), (0, 0)))
    got, red = _small_all_reduce(jnp.concatenate([dmod_blocks, flat], axis=1))

    tail_parts = _sequencer_scatter("scatter_tail", 4, tail_grads,
                                    after=(ffn_parts[0][0, 0:16, 0:128], red[0, 0:8, :]))
    g_in_p, g_uq_p = _sum_partials("sum_tail_partials", tail_parts, 1)
    as_transpose = ("w_in", "w_gate", "w_up")
    grads = dict(w_in=_w_in_from_kernel(g_in_p).T, w_uq=_w_uq_from_kernel(g_uq_p))
    partials = dict(w_gate=ffn_parts[0], w_up=ffn_parts[1], w_down=ffn_parts[2], w_o=ffn_parts[3])
    dmod_rows = got[:, 0:MODC // 128, :].reshape(NDEV, MODC)
    grads["b_ada"] = red[:, 0:MODC // 128, :].reshape(1, N_MOD * D)
    rep_flat = red[:, MOD_ROWS:, :].reshape(-1)
    off = 0
    for k in REP_NAMES:
        size = int(np.prod(given[k].shape))
        grads[k] = rep_flat[off:off + size]
        off += size

    view = {k: (given[k].shape[1:] if given[k].ndim > 2 else given[k].shape)
            for k in REP_NAMES + ("b_ada", "w_ada", "w_in", "w_uq", "w_o", "w_gate", "w_up", "w_down")}
    view.update(g_final=(1, D))
    names = ["w_ada", "b_ada", "g_mix", "w_in", "g_q", "g_kv", "w_uq", "w_uk", "w_uv", "w_pool", "pool_scale",
             "w_o", "g_ffn", "w_gate", "w_up", "w_down", "g_final"]
    g_ada, d_ada, m_ada, v_ada = _adamw_ada(w_ada[0], m_w_ada[0], v_w_ada[0], c_all.T.astype(BF), dmod_rows)
    out_g, out_d, out_m, out_v = dict(w_ada=g_ada), dict(w_ada=d_ada), dict(w_ada=m_ada), dict(w_ada=v_ada)
    groups = (("adamw_ffn", ("w_gate", "w_up", "w_down", "w_o"), 4),
              ("adamw_replicated", REP_NAMES + ("b_ada",), 1),
              ("adamw_tail", ("w_in", "w_uq"), 1))
    for gname, members, nb in groups:
        turn = lambda k, t: t.T if k in as_transpose else t
        ws = [turn(k, given[k].reshape(view[k])) for k in members]
        ms = [turn(k, given["m_" + k].reshape(view[k])) for k in members]
        vs = [turn(k, given["v_" + k].reshape(view[k])) for k in members]
        if members[0] in partials:
            gs, ds, m2, v2 = _adamw_from_partials(gname, ws, [partials[k] for k in members], ms, vs, nb)
        else:
            gs = [grads[k] if k in as_transpose else grads[k].reshape(view[k]) for k in members]
            ds, m2, v2 = _adamw_group(gname, ws, gs, ms, vs, nb)
        for k, g, d, mm, vv in zip(members, gs, ds, m2, v2):
            out_g[k], out_d[k], out_m[k], out_v[k] = turn(k, g), turn(k, d), turn(k, mm), turn(k, vv)

    total = rep_flat[off]
    shaped = lambda d: [d[k].reshape(given[k].shape) for k in names]
    return (total, dx[None], *shaped(out_g), *shaped(out_d), *shaped(out_m), *shaped(out_v))
```

```python
import numpy as np
import jax
import jax.numpy as jnp
from jax import lax
from jax.experimental import pallas as pl
from jax.experimental.pallas import tpu as pltpu
from jax.experimental.pallas import tpu_sc as plsc

D = 1024
HEADS = 4
NOPE = 128
ROPE = 64
HALF = ROPE // 2
QL = 256
KVL = 128
FF = 2816
PW = 512
GROUPS = 4
GD = 128
N_MOD = 6
EPS = 1e-6
SM_SCALE = (NOPE + ROPE) ** -0.5
LOG2_E = 1.4426950408889634
EXP2_SCALE = SM_SCALE * LOG2_E
ROPE_THETA = 10000.0
NDEV = 8
MODC = N_MOD * D // NDEV

ADAM_LR = 0.001
ADAM_B1 = 0.9
ADAM_B2 = 0.999
ADAM_EPS = 1e-08
ADAM_WD = 0.01
ADAM_STEP = 10

BF = jnp.bfloat16
F32 = jnp.float32
VMEM_LIMIT_V7X = 60 * 1024 * 1024
MESH = pl.DeviceIdType.MESH

TQ = 256
TK = 256
QW = 256
VPU_ROWS = 16
MOD_ROWS = 8
REP_ROWS = 200
SMALL_ROWS = MOD_ROWS + REP_ROWS


def _params(sem=None):
    return pltpu.CompilerParams(dimension_semantics=sem, vmem_limit_bytes=VMEM_LIMIT_V7X)


def _dot(a, b):
    return jnp.dot(a, b, preferred_element_type=F32)


def _dot_nt(a, b):
    return lax.dot_general(a, b, (((1,), (1,)), ((), ())), preferred_element_type=F32)


def _dot_tn(a, b):
    return _dot(a.astype(F32).T.astype(BF), b)


def _full(shape):
    return pl.BlockSpec(shape, lambda *_: (0,) * len(shape))


def _rows(ts, cols):
    return pl.BlockSpec((ts, cols), lambda i: (i, 0))


def _vmem():
    return pl.BlockSpec(memory_space=pltpu.VMEM)


def _any():
    return pl.BlockSpec(memory_space=pl.ANY)


def _rms(v):
    return lax.rsqrt(jnp.mean(v * v, axis=-1, keepdims=True) + EPS)


def _rms_bwd(dn, n, r):
    return r * (dn - n * jnp.mean(dn * n, axis=-1, keepdims=True))


def _colsum(v):
    return jnp.sum(v, axis=0, keepdims=True)


def _swap_halves(v):
    lane = lax.broadcasted_iota(jnp.int32, v.shape, 1)
    return jnp.where(lane < HALF, pltpu.roll(v, 128 - HALF, 1), pltpu.roll(v, HALF, 1))


def _window_lane_width():
    lane = lax.broadcasted_iota(jnp.int32, (1, PW), 1)
    return jnp.where(lane < 128, 2.0, jnp.where(lane < 256, 4.0, jnp.where(lane < 384, 8.0, 16.0))).astype(F32)


def _window_sums(ext, back):
    n = ext.shape[0]

    def sh(v, k):
        return pltpu.roll(v, k if back else n - k, 0)

    s2 = ext + sh(ext, 1)
    e4 = s2[:, 128:]
    s4 = e4 + sh(e4, 2)
    e8 = s4[:, 128:]
    s8 = e8 + sh(e8, 4)
    e16 = s8[:, 128:]
    s16 = e16 + sh(e16, 8)
    return jnp.concatenate([s2[:, :128], s4[:, :128], s8[:, :128], s16], axis=1)


def _row_counts(first_row, ts):
    t1 = (first_row + lax.broadcasted_iota(jnp.int32, (ts, 1), 0) + 1).astype(F32)
    return jnp.minimum(t1, _window_lane_width())


def _fwd_in(x, mod, g_mix, w_in, g_q, g_kv, w_uq, wuk_dc, perm, cos4, sin4, csk, snk, w_pool, pool_scale):
    S = x.shape[0]
    ts = 512
    nsub = ts // TQ

    def body(x_ref, mod_ref, gmix_ref, win_ref, gq_ref, gkv_ref, wuq_ref, wuk_ref, perm_ref, cos_ref, sin_ref,
             csk_ref, snk_ref, wpool_ref, pscale_ref,
             h1_ref, raw_ref, qn_ref, qs_ref, kv_ref, kvt_ref, pooled_ref, ypre_ref, ypool_ref, carry_ref):
        i = pl.program_id(0)

        @pl.when(i == 0)
        def _():
            carry_ref[...] = jnp.zeros_like(carry_ref)

        xv = x_ref[...]
        sh1 = mod_ref[0:1, 0:D]
        sc1 = mod_ref[0:1, D:2 * D]
        h = (xv * _rms(xv)) * gmix_ref[...] * (1.0 + sc1) + sh1
        hb = h.astype(BF)
        h1_ref[...] = hb
        proj = _dot(hb, win_ref[...])
        cq_raw = proj[:, 0:QL]
        ckv_raw = proj[:, QL:QL + KVL]
        kr = proj[:, 384:512]
        u = proj[:, 512:1024]
        raw_ref[...] = proj[:, 0:384]

        c_q = (cq_raw * _rms(cq_raw)) * gq_ref[...]
        c_kv = (ckv_raw * _rms(ckv_raw)) * gkv_ref[...]
        q = _dot(c_q.astype(BF), wuq_ref[...])
        qn = q[:, 0:HEADS * NOPE].astype(BF)
        qn_ref[...] = qn
        x1 = q[:, 512:640]
        x2 = q[:, 640:768]
        cosv = cos_ref[...]
        sinv = sin_ref[...]
        roped = jnp.concatenate([x1 * cosv - x2 * sinv, x1 * sinv + x2 * cosv], axis=1).astype(BF)
        for hd in range(HEADS):
            q_lat = _dot(qn[:, hd * NOPE:(hd + 1) * NOPE], wuk_ref[hd])
            q_rope = _dot(roped, perm_ref[hd])
            qh = jnp.concatenate([q_lat, q_rope], axis=1).astype(BF)
            for a in range(nsub):
                qs_ref[a, hd * TQ:(hd + 1) * TQ, :] = qh[a * TQ:(a + 1) * TQ, :]
        k_rope = kr * csk_ref[...] + _swap_halves(kr) * snk_ref[...]
        keys = jnp.concatenate([c_kv, k_rope], axis=1)
        kv_ref[...] = keys.astype(BF)
        for a in range(ts // TK):
            kvt_ref[a] = keys[a * TK:(a + 1) * TK, :].T.astype(BF)

        ext = jnp.concatenate([carry_ref[...], u], axis=0)
        win = _window_sums(ext, True)[16:, :]
        pooled = (win / _row_counts(i * ts, ts) - u).astype(BF)
        pooled_ref[...] = pooled
        carry_ref[...] = u[ts - 16:ts, :]
        ypre = jnp.concatenate(
            [_dot(pooled[:, g * GD:(g + 1) * GD], wpool_ref[g]) for g in range(GROUPS)], axis=1)
        ypre_ref[...] = ypre
        ypool_ref[...] = (ypre * pscale_ref[...]).astype(BF)

    out_shape = (
        jax.ShapeDtypeStruct((S, D), BF),
        jax.ShapeDtypeStruct((S, 384), F32),
        jax.ShapeDtypeStruct((S, HEADS * NOPE), BF),
        jax.ShapeDtypeStruct((S // TQ, HEADS * TQ, QW), BF),
        jax.ShapeDtypeStruct((S, QW), BF),
        jax.ShapeDtypeStruct((S // TK, QW, TK), BF),
        jax.ShapeDtypeStruct((S, PW), BF),
        jax.ShapeDtypeStruct((S, PW), F32),
        jax.ShapeDtypeStruct((S, PW), BF),
    )
    in_specs = [
        _rows(ts, D), _full(mod.shape), _full((1, D)), _full(w_in.shape), _full((1, QL)), _full((1, KVL)),
        _full(w_uq.shape), _full(wuk_dc.shape), _full(perm.shape), _rows(ts, 128), _rows(ts, 128), _rows(ts, 128),
        _rows(ts, 128), _full(w_pool.shape), _full((1, PW)),
    ]
    out_specs = (
        _rows(ts, D), _rows(ts, 384), _rows(ts, HEADS * NOPE),
        pl.BlockSpec((nsub, HEADS * TQ, QW), lambda i: (i, 0, 0)),
        _rows(ts, QW), pl.BlockSpec((ts // TK, QW, TK), lambda i: (i, 0, 0)), _rows(ts, PW), _rows(ts, PW),
        _rows(ts, PW),
    )
    return pl.pallas_call(
        body, name="fwd_in", out_shape=out_shape, grid=(S // ts,), in_specs=in_specs, out_specs=out_specs,
        scratch_shapes=[pltpu.VMEM((16, PW), F32)], compiler_params=_params(("arbitrary",)),
    )(x, mod, g_mix, w_in, g_q, g_kv, w_uq, wuk_dc, perm, cos4, sin4, csk, snk, w_pool, pool_scale)


def _diag_mask(shape, q_axis):
    qi = (lax.broadcasted_iota(jnp.int32, shape, q_axis) & (TQ - 1)) >> 6
    ki = lax.broadcasted_iota(jnp.int32, shape, 1 - q_axis) >> 6
    return ki <= qi


def _attn_fwd(qs, kv, kvt, wuv_vc):
    nq = qs.shape[0]
    S = kv.shape[0]
    M = HEADS * TQ

    def body(qs_ref, kv_ref, kvt_ref, wuv_ref, olat_ref, ymla_ref, lse_ref):
        i = pl.program_id(0)
        q = qs_ref[0]

        def scores(tile, masked):
            s = _dot_nt(kv_ref[pl.ds(pl.multiple_of(tile * TK, TK), TK), :], q)
            return jnp.where(_diag_mask((TK, M), 1), s, -jnp.inf) if masked else s

        def softmax(s, m, l):
            m_new = jnp.maximum(m, jnp.max(s, axis=0, keepdims=True))
            alpha = jnp.exp2((m - m_new) * EXP2_SCALE)
            p = jnp.exp2((s - m_new) * EXP2_SCALE)
            return m_new, alpha * l + jnp.sum(p, axis=0, keepdims=True), alpha, p.astype(BF)

        def weighted_values(tile, p):
            return _dot(kvt_ref[tile][0:KVL, :], p)

        def body(n, carry):
            m, l, acc, s_cur, p_prev = carry
            s_next = scores(n, False)
            prev_tile = jnp.where(n == 1, i, jnp.maximum(n - 2, 0))
            acc = acc + weighted_values(prev_tile, p_prev)
            m, l, alpha, p = softmax(s_cur, m, l)
            return m, l, alpha * acc, s_next, p

        init = (jnp.full((1, M), -jnp.inf, F32), jnp.zeros((1, M), F32), jnp.zeros((KVL, M), F32),
                scores(i, True), jnp.zeros((TK, M), BF))
        m, l, acc, s_cur, p_prev = lax.fori_loop(0, i, body, init)
        acc = acc + weighted_values(jnp.where(i == 1, i, jnp.maximum(i - 2, 0)), p_prev)
        m, l, alpha, p = softmax(s_cur, m, l)
        acc = alpha * acc + weighted_values(jnp.maximum(i - 1, 0), p)
        o_lat = acc / l
        olat_ref[0] = o_lat
        lse_ref[0] = jnp.broadcast_to(m * SM_SCALE + jnp.log(l), (8, M))
        for hd in range(HEADS):
            o_t = _dot(wuv_ref[hd], o_lat[:, hd * TQ:(hd + 1) * TQ].astype(BF))
            ymla_ref[:, hd * 128:(hd + 1) * 128] = o_t.T.astype(BF)

    out_shape = (
        jax.ShapeDtypeStruct((nq, KVL, M), F32),
        jax.ShapeDtypeStruct((S, HEADS * 128), BF),
        jax.ShapeDtypeStruct((nq, 8, M), F32),
    )
    return pl.pallas_call(
        body, name="attn_fwd", out_shape=out_shape, grid=(nq,),
        in_specs=[pl.BlockSpec((1, M, QW), lambda i: (i, 0, 0)), _full(kv.shape), _full(kvt.shape),
                  _full(wuv_vc.shape)],
        out_specs=(pl.BlockSpec((1, KVL, M), lambda i: (i, 0, 0)), _rows(TQ, HEADS * 128),
                   pl.BlockSpec((1, 8, M), lambda i: (i, 0, 0))),
        compiler_params=_params(("arbitrary",)),
    )(qs, kv, kvt, wuv_vc)


def _silu_parts(a):
    sg = jax.nn.sigmoid(a)
    return sg, a * sg


def _ffn_fwd(x, ymla, ypool, mod, w_o, g_ffn, wg_t, wu_t, wd, g_final, target):
    S = x.shape[0]
    ts = 256

    def body(x_ref, ymla_ref, ypool_ref, mod_ref, wo_ref, gffn_ref, wg_ref, wu_ref, wd_ref, gfin_ref, t_ref,
             x2_ref, mix_ref, h2t_ref, a_ref, b_ref, dx3_ref, dff_ref, dfft_ref, loss_ref, dgfin_ref, dgt2_ref,
             f_ref):
        i = pl.program_id(0)

        @pl.when(i == 0)
        def _():
            loss_ref[...] = jnp.zeros_like(loss_ref)
            dgfin_ref[...] = jnp.zeros_like(dgfin_ref)
            dgt2_ref[...] = jnp.zeros_like(dgt2_ref)

        gt1 = mod_ref[0:1, 2 * D:3 * D]
        sh2 = mod_ref[0:1, 3 * D:4 * D]
        sc2 = mod_ref[0:1, 4 * D:5 * D]
        gt2 = mod_ref[0:1, 5 * D:6 * D]
        cat = jnp.concatenate([ymla_ref[...], ypool_ref[...]], axis=1)
        mix = _dot(cat, wo_ref[...])
        mix_ref[...] = mix
        x2 = x_ref[...] + gt1 * mix
        x2_ref[...] = x2
        h2 = (x2 * _rms(x2)) * gffn_ref[...] * (1.0 + sc2) + sh2
        h2b = h2.astype(BF)
        h2t_ref[...] = h2.T.astype(BF)

        for c in range(FF // FCHUNK):
            cols = slice(c * FCHUNK, (c + 1) * FCHUNK)
            a = _dot_nt(h2b, wg_ref[cols, :])
            b = _dot_nt(h2b, wu_ref[cols, :])
            a_ref[:, cols] = a.astype(BF)
            b_ref[:, cols] = b.astype(BF)
            f_ref[:, cols] = (_silu_parts(a)[1] * b).astype(BF)
        ff = _dot(f_ref[...], wd_ref[...])

        x3 = x2 + gt2 * ff
        r3 = _rms(x3)
        xn3 = x3 * r3
        gfin = gfin_ref[...]
        e = xn3 * gfin - t_ref[...]
        loss_ref[...] += 0.5 * jnp.sum(jnp.mean(e * e, axis=-1, keepdims=True))
        dy = e * (1.0 / D)
        dgfin_ref[...] += _colsum(dy * xn3)
        dx3 = _rms_bwd(dy * gfin, xn3, r3)
        dx3_ref[...] = dx3
        dgt2_ref[...] += _colsum(dx3 * ff)
        dff = dx3 * gt2
        dff_ref[...] = dff.astype(BF)
        dfft_ref[...] = dff.T.astype(BF)

    row = lambda c: _rows(ts, c)
    col = pl.BlockSpec((D, ts), lambda i: (0, i))
    const = _full
    out_shape = (
        jax.ShapeDtypeStruct((S, D), F32),
        jax.ShapeDtypeStruct((S, D), F32),
        jax.ShapeDtypeStruct((D, S), BF),
        jax.ShapeDtypeStruct((S, FF), BF),
        jax.ShapeDtypeStruct((S, FF), BF),
        jax.ShapeDtypeStruct((S, D), F32),
        jax.ShapeDtypeStruct((S, D), BF),
        jax.ShapeDtypeStruct((D, S), BF),
        jax.ShapeDtypeStruct((8, 128), F32),
        jax.ShapeDtypeStruct((1, D), F32),
        jax.ShapeDtypeStruct((1, D), F32),
    )
    return pl.pallas_call(
        body, name="ffn_fwd", out_shape=out_shape, grid=(S // ts,),
        in_specs=[row(D), row(PW), row(PW), const(mod.shape), _vmem(), const((1, D)), _vmem(), _vmem(), _vmem(),
                  const((1, D)), row(D)],
        out_specs=(row(D), row(D), col, row(FF), row(FF), row(D), row(D), col, const((8, 128)), const((1, D)),
                   const((1, D))),
        scratch_shapes=[pltpu.VMEM((ts, FF), BF)],
        compiler_params=_params(("arbitrary",)),
    )(x, ymla, ypool, mod, w_o, g_ffn, wg_t, wu_t, wd, g_final, target)


FCHUNK = 256


def _ffn_bwd_acts(dff, a, b, wg_t, wu_t, wd):
    S = dff.shape[0]
    ts = 512

    def body(dff_ref, a_ref, b_ref, wg_ref, wu_ref, wd_ref, da_ref, db_ref, dh2_ref):
        dffb = dff_ref[...]
        for c in range(FF // FCHUNK):
            cols = slice(c * FCHUNK, (c + 1) * FCHUNK)
            df = _dot_nt(dffb, wd_ref[cols, :])
            av = a_ref[:, cols].astype(F32)
            bv = b_ref[:, cols].astype(F32)
            sg, sa = _silu_parts(av)
            db_ref[:, cols] = (df * sa).astype(BF)
            da_ref[:, cols] = (df * bv * (sg * (1.0 + av * (1.0 - sg)))).astype(BF)
        dh2_ref[...] = _dot(da_ref[...], wg_ref[...]) + _dot(db_ref[...], wu_ref[...])

    act = _rows(ts, FF)
    return pl.pallas_call(
        body, name="ffn_bwd_acts",
        out_shape=(jax.ShapeDtypeStruct((S, FF), BF), jax.ShapeDtypeStruct((S, FF), BF),
                   jax.ShapeDtypeStruct((S, D), F32)),
        grid=(S // ts,), in_specs=[_rows(ts, D), act, act, _vmem(), _vmem(), _vmem()],
        out_specs=(act, act, _rows(ts, D)), compiler_params=_params(("arbitrary",)),
    )(dff, a, b, wg_t, wu_t, wd)


def _ffn_bwd_weights(dff_t, h2_t, da, db, a, b):
    S = da.shape[0]

    def body(dfft_ref, h2t_ref, da_ref, db_ref, a_ref, b_ref, dwg_ref, dwu_ref, dwd_ref):
        h2t = h2t_ref[...]
        dwg_ref[...] = _dot(h2t, da_ref[...]).T.astype(BF)
        dwu_ref[...] = _dot(h2t, db_ref[...]).T.astype(BF)
        f = (_silu_parts(a_ref[...].astype(F32))[1] * b_ref[...].astype(F32)).astype(BF)
        dwd_ref[...] = _dot(dfft_ref[...], f).T.astype(BF)

    act = pl.BlockSpec((S, FCHUNK), lambda j: (0, j))
    wblk = _rows(FCHUNK, D)
    shp = jax.ShapeDtypeStruct((FF, D), BF)
    return pl.pallas_call(
        body, name="ffn_bwd_weights", out_shape=(shp, shp, shp), grid=(FF // FCHUNK,),
        in_specs=[_vmem(), _vmem(), act, act, act, act], out_specs=(wblk, wblk, wblk),
        compiler_params=_params(("arbitrary",)),
    )(dff_t, h2_t, da, db, a, b)


def _mix_bwd(dh2, dx3, x2, mix, mod, g_ffn, ymla, ypool, w_o, ypre, pooled, pool_scale, wpool_dc, olat, wuv_vc):
    S = dh2.shape[0]
    ts = 512
    n = S // ts
    nsub = ts // TQ
    M = HEADS * TQ

    def body(dh2_ref, dx3_ref, x2_ref, mix_ref, mod_ref, gffn_ref, ymla_ref, ypool_ref, wo_ref, ypre_ref, pooled_ref,
             pscale_ref, wpool_ref, olat_ref, wuv_ref,
             dx2_ref, du_ref, dolat_ref, delta_ref, dwo_ref, dwuv_ref, dwpool_ref, dpscale_ref, dgt1_ref, dsc2_ref,
             dsh2_ref, dgffn_ref, carry_ref, dwo_acc):
        i = pl.program_id(0)

        @pl.when(i == 0)
        def _():
            carry_ref[...] = jnp.zeros_like(carry_ref)
            dwo_acc[...] = jnp.zeros_like(dwo_acc)
            for r in (dwuv_ref, dwpool_ref, dpscale_ref, dgt1_ref, dsc2_ref, dsh2_ref, dgffn_ref):
                r[...] = jnp.zeros_like(r)

        gt1 = mod_ref[0:1, 2 * D:3 * D]
        sc2 = mod_ref[0:1, 4 * D:5 * D]
        gffn = gffn_ref[...]
        dh2 = dh2_ref[...]
        x2 = x2_ref[...]
        r2 = _rms(x2)
        xn2 = x2 * r2
        dsc2_ref[...] += _colsum(dh2 * (xn2 * gffn))
        dsh2_ref[...] += _colsum(dh2)
        dgffn_ref[...] += _colsum(dh2 * (1.0 + sc2) * xn2)
        dx2 = dx3_ref[...] + _rms_bwd(dh2 * gffn * (1.0 + sc2), xn2, r2)
        dx2_ref[...] = dx2
        dgt1_ref[...] += _colsum(dx2 * mix_ref[...])
        dmix = (dx2 * gt1).astype(BF)
        cat = jnp.concatenate([ymla_ref[...], ypool_ref[...]], axis=1)
        dwo_acc[...] += _dot_tn(cat, dmix)
        dcat = _dot_nt(dmix, wo_ref[...])
        dymla = dcat[:, 0:512]
        dypool = dcat[:, 512:1024]

        dpscale_ref[...] += _colsum(dypool * ypre_ref[...])
        dypre = (dypool * pscale_ref[...]).astype(BF)
        pooled = pooled_ref[...]
        dpooled = []
        for g in range(GROUPS):
            sl = slice(g * GD, (g + 1) * GD)
            dwpool_ref[g] += _dot_tn(pooled[:, sl], dypre[:, sl])
            dpooled.append(_dot(dypre[:, sl], wpool_ref[g]))
        dpooled = jnp.concatenate(dpooled, axis=1)
        tile = n - 1 - i
        e = dpooled / _row_counts(tile * ts, ts)
        ext = jnp.concatenate([e, carry_ref[...]], axis=0)
        du_ref[...] = _window_sums(ext, False)[0:ts, :] - dpooled
        carry_ref[...] = e[0:16, :]

        for hd in range(HEADS):
            do = dymla[:, hd * 128:(hd + 1) * 128]
            dob = do.astype(BF)
            dol = _dot(dob, wuv_ref[hd])
            for a in range(nsub):
                ol_t = olat_ref[a, :, hd * TQ:(hd + 1) * TQ]
                dl = dol[a * TQ:(a + 1) * TQ, :]
                dolat_ref[a, hd * TQ:(hd + 1) * TQ, :] = dl.astype(BF)
                dwuv_ref[hd] += _dot(ol_t.astype(BF), dob[a * TQ:(a + 1) * TQ, :])
                delta = jnp.sum(dl * ol_t.T, axis=-1, keepdims=True)
                delta_ref[a, :, hd * TQ:(hd + 1) * TQ] = jnp.broadcast_to(delta, (TQ, 128)).T[0:8, :]

        @pl.when(i == n - 1)
        def _():
            dwo_ref[...] = dwo_acc[...].astype(BF)

    rev = lambda c: pl.BlockSpec((ts, c), lambda i: (n - 1 - i, 0))
    rev3 = lambda r, c: pl.BlockSpec((nsub, r, c), lambda i: (n - 1 - i, 0, 0))
    out_shape = (
        jax.ShapeDtypeStruct((S, D), F32),
        jax.ShapeDtypeStruct((S, PW), F32),
        jax.ShapeDtypeStruct((S // TQ, M, KVL), BF),
        jax.ShapeDtypeStruct((S // TQ, 8, M), F32),
        jax.ShapeDtypeStruct((D, D), BF),
        jax.ShapeDtypeStruct((HEADS, KVL, 128), F32),
        jax.ShapeDtypeStruct((GROUPS, GD, GD), F32),
        jax.ShapeDtypeStruct((1, PW), F32),
        jax.ShapeDtypeStruct((1, D), F32), jax.ShapeDtypeStruct((1, D), F32), jax.ShapeDtypeStruct((1, D), F32),
        jax.ShapeDtypeStruct((1, D), F32),
    )
    in_specs = [rev(D), rev(D), rev(D), rev(D), _full(mod.shape), _full((1, D)), rev(PW), rev(PW), _full(w_o.shape),
                rev(PW), rev(PW), _full((1, PW)), _full(wpool_dc.shape), rev3(KVL, M), _full(wuv_vc.shape)]
    out_specs = (rev(D), rev(PW), rev3(M, KVL), rev3(8, M), _full((D, D)), _full((HEADS, KVL, 128)),
                 _full((GROUPS, GD, GD)), _full((1, PW)), _full((1, D)), _full((1, D)), _full((1, D)), _full((1, D)))
    return pl.pallas_call(
        body, name="mix_bwd", out_shape=out_shape, grid=(n,), in_specs=in_specs, out_specs=out_specs,
        scratch_shapes=[pltpu.VMEM((16, PW), F32), pltpu.VMEM((D, D), F32)],
        compiler_params=_params(("arbitrary",)),
    )(dh2, dx3, x2, mix, mod, g_ffn, ymla, ypool, w_o, ypre, pooled, pool_scale, wpool_dc, olat, wuv_vc)


def _attn_bwd(qs, kv, dolat, lse, delta):
    nq = qs.shape[0]
    S = kv.shape[0]
    M = HEADS * TQ
    nk = S // TK

    def body(qs_ref, kv_ref, do_ref, lse_ref, delta_ref, dkv_ref, dqt_ref, p_ref, ds_ref):
        kt = pl.program_id(0)
        k = kv_ref[...]
        v = k[:, 0:KVL]
        k_t = k.astype(F32).T.astype(BF)

        @pl.when(kt == 0)
        def _():
            dqt_ref[...] = jnp.zeros_like(dqt_ref)

        def step(qi, carry, masked):
            dk, dv = carry
            q = qs_ref[qi]
            do = do_ref[qi]
            s = _dot_nt(k, q)
            dp = _dot_nt(v, do)
            lse_row = lse_ref[qi, 0:1, :] * LOG2_E
            delta_row = delta_ref[qi, 0:1, :]
            q_chunk = (lax.broadcasted_iota(jnp.int32, (1, M), 1) & (TQ - 1)) >> 6
            for r in range(0, TK, VPU_ROWS):
                rows = slice(r, r + VPU_ROWS)
                p = jnp.exp2(s[rows, :] * EXP2_SCALE - lse_row)
                if masked:
                    p = jnp.where((r >> 6) <= q_chunk, p, 0.0)
                p_ref[rows, :] = p.astype(BF)
                ds_ref[rows, :] = (p * (dp[rows, :] - delta_row) * SM_SCALE).astype(BF)
            ds = ds_ref[...]
            dv = dv + _dot(p_ref[...], do)
            dk = dk + _dot(ds, q)
            dqt_ref[qi] += _dot(k_t, ds)
            return dk, dv

        carry = step(kt, (jnp.zeros((TK, QW), F32), jnp.zeros((TK, KVL), F32)), True)
        dk, dv = lax.fori_loop(kt + 1, nq, lambda qi, c: step(qi, c, False), carry)
        dkv_ref[...] = dk + jnp.concatenate([dv, jnp.zeros((TK, QW - KVL), F32)], axis=1)

    out_shape = (jax.ShapeDtypeStruct((S, QW), F32), jax.ShapeDtypeStruct((nq, QW, M), F32))
    return pl.pallas_call(
        body, name="attn_bwd", out_shape=out_shape, grid=(nk,),
        in_specs=[_vmem(), _rows(TK, QW), _vmem(), _vmem(), _vmem()],
        out_specs=(_rows(TK, QW), _vmem()),
        scratch_shapes=[pltpu.VMEM((TK, M), BF), pltpu.VMEM((TK, M), BF)],
        compiler_params=_params(("arbitrary",)),
    )(qs, kv, dolat, lse, delta)


def _in_bwd(dqt, dkv, du, raw, qn, h1, x, dx2, mod, g_mix, w_in, g_q, g_kv, w_uq, wuk_cd, perm_t, cos4, sin4, csk,
            snk):
    S = x.shape[0]
    ts = 512
    n = S // ts
    nsub = ts // TQ
    M = HEADS * TQ

    def body(dqt_ref, dkv_ref, du_ref, raw_ref, qn_ref, h1_ref, x_ref, dx2_ref, mod_ref, gmix_ref, win_ref, gq_ref,
             gkv_ref, wuq_ref, wuk_ref, permt_ref, cos_ref, sin_ref, csk_ref, snk_ref,
             dx_ref, dwin_ref, dwuq_ref, dwuk_ref, dgq_ref, dgkv_ref, dsc1_ref, dsh1_ref, dgmix_ref, dwin_acc,
             dwuq_acc):
        i = pl.program_id(0)

        @pl.when(i == 0)
        def _():
            dwin_acc[...] = jnp.zeros_like(dwin_acc)
            dwuq_acc[...] = jnp.zeros_like(dwuq_acc)
            for r in (dwuk_ref, dgq_ref, dgkv_ref, dsc1_ref, dsh1_ref, dgmix_ref):
                r[...] = jnp.zeros_like(r)

        dq_blocks = [dqt_ref[a].T for a in range(nsub)]
        qn = qn_ref[...]
        dq_parts = []
        drope = jnp.zeros((ts, 2 * 128), F32)
        for hd in range(HEADS):
            dqh = jnp.concatenate([blk[hd * TQ:(hd + 1) * TQ, :] for blk in dq_blocks], axis=0)
            dq_lat = dqh[:, 0:KVL].astype(BF)
            dq_parts.append(_dot(dq_lat, wuk_ref[hd]))
            dwuk_ref[hd] += _dot_tn(dq_lat, qn[:, hd * NOPE:(hd + 1) * NOPE])
            drope = drope + _dot(dqh[:, KVL:QW].astype(BF), permt_ref[hd])
        do1 = drope[:, 0:128]
        do2 = drope[:, 128:256]
        cosv = cos_ref[...]
        sinv = sin_ref[...]
        dq_parts.append(do1 * cosv + do2 * sinv)
        dq_parts.append(do2 * cosv - do1 * sinv)
        dq = jnp.concatenate(dq_parts, axis=1).astype(BF)

        cq_raw = raw_ref[:, 0:QL]
        ckv_raw = raw_ref[:, QL:QL + KVL]
        rq = _rms(cq_raw)
        nq_ = cq_raw * rq
        gq = gq_ref[...]
        dwuq_acc[...] += _dot_tn((nq_ * gq).astype(BF), dq)
        dc_q = _dot_nt(dq, wuq_ref[...])
        dgq_ref[...] += _colsum(dc_q * nq_)
        dcq_raw = _rms_bwd(dc_q * gq, nq_, rq)

        dkv = dkv_ref[...]
        rk = _rms(ckv_raw)
        nk_ = ckv_raw * rk
        dc_kv = dkv[:, 0:KVL]
        dgkv_ref[...] += _colsum(dc_kv * nk_)
        dckv_raw = _rms_bwd(dc_kv * gkv_ref[...], nk_, rk)
        dkr_roped = dkv[:, KVL:QW]
        dkr = dkr_roped * csk_ref[...] - _swap_halves(dkr_roped) * snk_ref[...]

        dproj = jnp.concatenate([dcq_raw, dckv_raw, dkr, du_ref[...]], axis=1).astype(BF)
        dwin_acc[...] += _dot_tn(h1_ref[...], dproj)
        dh1 = _dot_nt(dproj, win_ref[...])

        sc1 = mod_ref[0:1, D:2 * D]
        gmix = gmix_ref[...]
        xv = x_ref[...]
        r1 = _rms(xv)
        xn1 = xv * r1
        dsc1_ref[...] += _colsum(dh1 * (xn1 * gmix))
        dsh1_ref[...] += _colsum(dh1)
        dgmix_ref[...] += _colsum(dh1 * (1.0 + sc1) * xn1)
        dx_ref[...] = dx2_ref[...] + _rms_bwd(dh1 * gmix * (1.0 + sc1), xn1, r1)

        @pl.when(i == n - 1)
        def _():
            dwin_ref[...] = dwin_acc[...].astype(BF)
            dwuq_ref[...] = dwuq_acc[...].astype(BF)

    out_shape = (
        jax.ShapeDtypeStruct((S, D), F32),
        jax.ShapeDtypeStruct((D, D), BF),
        jax.ShapeDtypeStruct((QL, 768), BF),
        jax.ShapeDtypeStruct((HEADS, KVL, NOPE), F32),
        jax.ShapeDtypeStruct((1, QL), F32), jax.ShapeDtypeStruct((1, KVL), F32),
        jax.ShapeDtypeStruct((1, D), F32), jax.ShapeDtypeStruct((1, D), F32), jax.ShapeDtypeStruct((1, D), F32),
    )
    in_specs = [pl.BlockSpec((nsub, QW, M), lambda i: (i, 0, 0)), _rows(ts, QW), _rows(ts, PW), _rows(ts, 384),
                _rows(ts, HEADS * NOPE), _rows(ts, D), _rows(ts, D), _rows(ts, D), _full(mod.shape), _full((1, D)),
                _full(w_in.shape), _full((1, QL)), _full((1, KVL)), _full(w_uq.shape), _full(wuk_cd.shape),
                _full(perm_t.shape), _rows(ts, 128), _rows(ts, 128), _rows(ts, 128), _rows(ts, 128)]
    out_specs = (_rows(ts, D), _full((D, D)), _full((QL, 768)), _full((HEADS, KVL, NOPE)), _full((1, QL)),
                 _full((1, KVL)), _full((1, D)), _full((1, D)), _full((1, D)))
    return pl.pallas_call(
        body, name="in_bwd", out_shape=out_shape, grid=(n,), in_specs=in_specs, out_specs=out_specs,
        scratch_shapes=[pltpu.VMEM((D, D), F32), pltpu.VMEM((QL, 768), F32)],
        compiler_params=_params(("arbitrary",)),
    )(dqt, dkv, du, raw, qn, h1, x, dx2, mod, g_mix, w_in, g_q, g_kv, w_uq, wuk_cd, perm_t, cos4, sin4, csk, snk)


def _rope_perm():
    p = np.zeros((HEADS, 2 * 128, 128), np.float32)
    for hd in range(HEADS):
        for t in range(HALF):
            p[hd, hd * HALF + t, t] = 1.0
            p[hd, 128 + hd * HALF + t, HALF + t] = 1.0
    return p


def _rope_tables(positions):
    freqs = jnp.power(ROPE_THETA, -jnp.arange(HALF, dtype=F32) / HALF)
    ang = positions.astype(F32)[:, None] * jnp.tile(freqs, HEADS)[None, :]
    cos4 = jnp.cos(ang)
    sin4 = jnp.sin(ang)
    lane = jnp.arange(HEADS * HALF)[None, :]
    csk = jnp.where(lane < ROPE, cos4, 0.0)
    snk = jnp.where(lane < HALF, -sin4, jnp.where(lane < ROPE, sin4, 0.0))
    return cos4, sin4, csk, snk


def _local_step(x, rope, target, mod, g_mix, w_in_p, g_q, g_kv, w_uq_p, w_uk, w_uv, w_pool, pool_scale, g_ffn,
                g_final, late, ffn_grads_exchange):
    perm = jnp.asarray(_rope_perm(), BF)
    perm_t = jnp.asarray(_rope_perm().transpose(0, 2, 1), BF)
    cos4, sin4, csk, snk = rope
    wuk_dc = w_uk.transpose(1, 2, 0).astype(BF)
    wuk_cd = w_uk.transpose(1, 0, 2).astype(BF)
    wuv_vc = w_uv.transpose(1, 2, 0).astype(BF)
    wpool = w_pool.astype(BF)
    wpool_dc = w_pool.transpose(0, 2, 1).astype(BF)

    h1, raw, qn, qs, kv, kvt, pooled, ypre, ypool = _fwd_in(
        x, mod, g_mix, w_in_p, g_q, g_kv, w_uq_p, wuk_dc, perm, cos4, sin4, csk, snk, wpool, pool_scale)
    olat, ymla, lse = _attn_fwd(qs, kv, kvt, wuv_vc)
    w_o, wg_t, wu_t, wd = late
    x2, mix, h2_t, a, b, dx3, dff, dff_t, loss, dgfin, dgt2 = _ffn_fwd(
        x, ymla, ypool, mod, w_o, g_ffn, wg_t, wu_t, wd, g_final, target)
    da, db, dh2 = _ffn_bwd_acts(dff, a, b, wg_t, wu_t, wd)
    (dx2, du, dolat, delta, dwo, dwuv, dwpool, dpscale, dgt1, dsc2, dsh2, dgffn) = _mix_bwd(
        dh2, dx3, x2, mix, mod, g_ffn, ymla, ypool, w_o, ypre, pooled, pool_scale, wpool_dc, olat, wuv_vc)
    dwg_t, dwu_t, dwd = _ffn_bwd_weights(dff_t, h2_t, da, db, a, b)
    ffn_parts = ffn_grads_exchange((dwg_t, dwu_t, dwd, dwo))
    dkv, dqt = _attn_bwd(qs, kv, dolat, lse, delta)
    dx, dwin, dwuq, dwuk, dgq, dgkv, dsc1, dsh1, dgmix = _in_bwd(
        dqt, dkv, du, raw, qn, h1, x, dx2, mod, g_mix, w_in_p, g_q, g_kv, w_uq_p, wuk_cd, perm_t, cos4, sin4, csk,
        snk)
    dmod = jnp.concatenate([dsh1, dsc1, dgt1, dsh2, dsc2, dgt2], axis=1)
    replicated = dict(
        w_uk=dwuk.transpose(1, 0, 2), w_uv=dwuv.transpose(1, 0, 2), w_pool=dwpool, g_mix=dgmix, g_q=dgq, g_kv=dgkv,
        pool_scale=dpscale, g_ffn=dgffn, g_final=dgfin)
    return loss[0, 0], dx, dmod, (dwin, dwuq), ffn_parts, replicated


def _my_pos():
    return lax.axis_index("x"), lax.axis_index("y"), lax.axis_index("c")


def _peer(pos, k):
    x, y, c = pos
    return (1 - x if k & 4 else x, 1 - y if k & 2 else y, 1 - c if k & 1 else c)


def _index(pos):
    x, y, c = pos
    return 4 * x + 2 * y + c


def _remote(src, dst, send_sem, recv_sem, to):
    return pltpu.make_async_remote_copy(src_ref=src, dst_ref=dst, send_sem=send_sem, recv_sem=recv_sem,
                                        device_id=to, device_id_type=MESH)


def _ada_mod(c, w_ada, b_ada, after):
    def body(c_ref, w_ref, b_ref, after_ref, mod_ref, call_ref, cbuf, sbuf, rbuf, send1, recv1, send2, recv2):
        me = _my_pos()
        mi = _index(me)
        cv = c_ref[...]
        cbuf[...] = jnp.broadcast_to(cv * jax.nn.sigmoid(cv), (8, D))
        call_ref[mi] = cbuf[...]
        first = [_remote(cbuf, call_ref.at[mi], send1.at[k - 1], recv1.at[k - 1], _peer(me, k)) for k in range(1, NDEV)]
        for cp in first:
            cp.start()
        for k in range(1, NDEV):
            _remote(cbuf, call_ref.at[_index(_peer(me, k))], send1.at[k - 1], recv1.at[k - 1], _peer(me, k)).wait_recv()
        c_all = jnp.concatenate([call_ref[b][0:1, :] for b in range(NDEV)], axis=0)
        blocks = _dot(c_all.astype(BF), w_ref[...].astype(BF))
        for b in range(NDEV):
            sbuf[b] = jnp.broadcast_to(blocks[b:b + 1, :], (8, MODC))
        second = []
        for k in range(1, NDEV):
            to = _peer(me, k)
            second.append(_remote(sbuf.at[_index(to)], rbuf.at[mi], send2.at[k - 1], recv2.at[k - 1], to))
        for cp in second:
            cp.start()
        rbuf[mi] = sbuf[mi]
        for k in range(1, NDEV):
            to = _peer(me, k)
            _remote(sbuf.at[_index(to)], rbuf.at[_index(to)], send2.at[k - 1], recv2.at[k - 1], to).wait_recv()
        for j in range(NDEV):
            mod_ref[:, j * MODC:(j + 1) * MODC] = rbuf[j] + b_ref[:, j * MODC:(j + 1) * MODC]
        for cp in first + second:
            cp.wait_send()

    return pl.pallas_call(
        body, name="ada_mod",
        out_shape=(jax.ShapeDtypeStruct((8, N_MOD * D), F32), jax.ShapeDtypeStruct((NDEV, 8, D), F32)),
        in_specs=[_vmem(), _vmem(), _vmem(), _any()], out_specs=(_vmem(), _vmem()),
        scratch_shapes=[pltpu.VMEM((8, D), F32), pltpu.VMEM((NDEV, 8, MODC), F32), pltpu.VMEM((NDEV, 8, MODC), F32),
                        pltpu.SemaphoreType.DMA((NDEV - 1,)), pltpu.SemaphoreType.DMA((NDEV - 1,)),
                        pltpu.SemaphoreType.DMA((NDEV - 1,)), pltpu.SemaphoreType.DMA((NDEV - 1,))],
        compiler_params=_params(),
    )(c, w_ada, b_ada, after)


def _sequencer_scatter(name, collective_id, srcs, after=()):
    n = len(srcs)

    def of(src, to_index):
        r = src.shape[0] // NDEV
        return src.at[pl.ds(pl.multiple_of(to_index * r, 16), r), :]

    def body(*refs):
        src, zone = refs[:n], refs[n + len(after):2 * n + len(after)]
        send, recv, local = refs[2 * n + len(after):]
        me = _my_pos()
        mi = _index(me)
        barrier = pltpu.get_barrier_semaphore()
        for k in range(1, NDEV):
            pl.semaphore_signal(barrier, inc=1, device_id=_peer(me, k), device_id_type=MESH)
        pl.semaphore_wait(barrier, NDEV - 1)
        own = [pltpu.make_async_copy(of(src[a], mi), zone[a].at[mi], local.at[a]) for a in range(n)]
        for cp in own:
            cp.start()
        for a in range(n):
            for k in range(1, NDEV):
                to = _peer(me, k)
                s = a * (NDEV - 1) + k - 1
                _remote(of(src[a], _index(to)), zone[a].at[mi], send.at[s], recv.at[s], to).start()
        for cp in own:
            cp.wait()
        for a in range(n):
            for k in range(1, NDEV):
                to = _peer(me, k)
                s = a * (NDEV - 1) + k - 1
                cp = _remote(of(src[a], mi), zone[a].at[_index(to)], send.at[s], recv.at[s], to)
                cp.wait_send()
                cp.wait_recv()

    return pl.kernel(
        body, name=name, mesh=plsc.ScalarSubcoreMesh(axis_name="sequencer", num_cores=1),
        out_type=tuple(jax.ShapeDtypeStruct((NDEV, s.shape[0] // NDEV, s.shape[1]), s.dtype) for s in srcs),
        scratch_types=[pltpu.SemaphoreType.DMA((n * (NDEV - 1),)), pltpu.SemaphoreType.DMA((n * (NDEV - 1),)),
                       pltpu.SemaphoreType.DMA((n,))],
        compiler_params=pltpu.CompilerParams(collective_id=collective_id),
    )(*srcs, *after)


CHIP_PEERS = (2, 4, 6)


def _sequencer_gather(name, collective_id, srcs, after=()):
    n = len(srcs)
    per = NDEV - 1

    def body(*refs):
        src, zone = refs[:n], refs[n + len(after):2 * n + len(after)]
        send, recv, local = refs[2 * n + len(after):]
        me = _my_pos()
        mi = _index(me)
        sibling = _peer(me, 1)
        talk_to = (sibling,) + tuple(_peer(me, k) for k in CHIP_PEERS)
        barrier = pltpu.get_barrier_semaphore()
        for to in talk_to:
            pl.semaphore_signal(barrier, inc=1, device_id=to, device_id_type=MESH)
        pl.semaphore_wait(barrier, len(talk_to))

        def copy(a, slot, block_of, to, from_src=False):
            rows = zone[a].at[_index(block_of)]
            return _remote(src[a] if from_src else rows, rows, send.at[a * per + slot], recv.at[a * per + slot], to)

        own = [pltpu.make_async_copy(src[a], zone[a].at[mi], local.at[a]) for a in range(n)]
        for cp in own:
            cp.start()
        started = []
        for a in range(n):
            started.append(copy(a, 0, me, sibling, from_src=True))
            started += [copy(a, 1 + j, me, _peer(me, k), from_src=True) for j, k in enumerate(CHIP_PEERS)]
        for cp in started:
            cp.start()
        for a in range(n):
            for j, k in enumerate(CHIP_PEERS):
                copy(a, 1 + j, _peer(me, k), me).wait_recv()
                passed = copy(a, 4 + j, _peer(me, k), sibling)
                passed.start()
                started.append(passed)
        for a in range(n):
            copy(a, 0, sibling, me).wait_recv()
            for j, k in enumerate(CHIP_PEERS):
                copy(a, 4 + j, _peer(me, k | 1), me).wait_recv()
        for cp in started:
            cp.wait_send()
        for cp in own:
            cp.wait()

    return pl.kernel(
        body, name=name, mesh=plsc.ScalarSubcoreMesh(axis_name="sequencer", num_cores=1),
        out_type=tuple(jax.ShapeDtypeStruct((NDEV,) + s.shape, s.dtype) for s in srcs),
        scratch_types=[pltpu.SemaphoreType.DMA((n * per,)), pltpu.SemaphoreType.DMA((n * per,)),
                       pltpu.SemaphoreType.DMA((n,))],
        compiler_params=pltpu.CompilerParams(collective_id=collective_id),
    )(*srcs, *after)


def _blocked(shape, nb, axis=0):
    block = tuple(s // nb if d == axis else s for d, s in enumerate(shape))
    return pl.BlockSpec(block, lambda i: tuple(i if d == axis else 0 for d in range(len(shape))))


def _sum_partials(name, parts, nb):
    n = len(parts)

    def body(*refs):
        for a in range(n):
            acc = refs[a][0].astype(F32)
            for p in range(1, NDEV):
                acc = acc + refs[a][p].astype(F32)
            refs[n + a][...] = acc

    return pl.pallas_call(
        body, name=name, grid=(nb,),
        out_shape=tuple(jax.ShapeDtypeStruct(p.shape[1:], F32) for p in parts),
        in_specs=[_blocked(p.shape, nb, 1) for p in parts],
        out_specs=tuple(_blocked(p.shape[1:], nb) for p in parts), compiler_params=_params(("arbitrary",)),
    )(*parts)


def _small_all_reduce(buf):
    def body(buf_ref, got_ref, red_ref, mine, send1, recv1, send2, recv2):
        me = _my_pos()
        mi = _index(me)
        first = []
        for k in range(1, NDEV):
            to = _peer(me, k)
            first.append(_remote(buf_ref.at[_index(to)], got_ref.at[mi], send1.at[k - 1], recv1.at[k - 1], to))
        for cp in first:
            cp.start()
        got_ref[mi] = buf_ref[mi]
        for k in range(1, NDEV):
            to = _peer(me, k)
            _remote(buf_ref.at[mi], got_ref.at[_index(to)], send1.at[k - 1], recv1.at[k - 1], to).wait_recv()
        acc = got_ref[0]
        for p in range(1, NDEV):
            acc = acc + got_ref[p]
        mine[...] = acc
        second = [_remote(mine, red_ref.at[mi], send2.at[k - 1], recv2.at[k - 1], _peer(me, k)) for k in range(1, NDEV)]
        for cp in second:
            cp.start()
        red_ref[mi] = acc
        for k in range(1, NDEV):
            to = _peer(me, k)
            _remote(mine, red_ref.at[_index(to)], send2.at[k - 1], recv2.at[k - 1], to).wait_recv()
        for cp in first + second:
            cp.wait_send()

    return pl.pallas_call(
        body, name="small_all_reduce",
        out_shape=(jax.ShapeDtypeStruct(buf.shape, F32), jax.ShapeDtypeStruct(buf.shape, F32)),
        in_specs=[_vmem()], out_specs=(_vmem(), _vmem()),
        scratch_shapes=[pltpu.VMEM(buf.shape[1:], F32),
                        pltpu.SemaphoreType.DMA((NDEV - 1,)), pltpu.SemaphoreType.DMA((NDEV - 1,)),
                        pltpu.SemaphoreType.DMA((NDEV - 1,)), pltpu.SemaphoreType.DMA((NDEV - 1,))],
        compiler_params=_params(),
    )(buf)


def _adamw_math(w, g, m, v):
    m = ADAM_B1 * m + (1.0 - ADAM_B1) * g
    v = ADAM_B2 * v + (1.0 - ADAM_B2) * jnp.square(g)
    m_hat = m / (1.0 - ADAM_B1 ** ADAM_STEP)
    v_hat = v / (1.0 - ADAM_B2 ** ADAM_STEP)
    delta = -ADAM_LR * (m_hat / (jnp.sqrt(v_hat) + ADAM_EPS) + ADAM_WD * w)
    return delta, m, v


def _adamw_group(name, ws, gs, ms, vs, nb):
    n = len(ws)

    def body(*refs):
        for a in range(n):
            w, g, m, v = (refs[q * n + a][...] for q in range(4))
            delta, m2, v2 = _adamw_math(w, g, m, v)
            refs[4 * n + a][...] = delta
            refs[5 * n + a][...] = m2
            refs[6 * n + a][...] = v2

    shapes = tuple(jax.ShapeDtypeStruct(w.shape, F32) for w in ws)
    specs = [_blocked(w.shape, nb) for w in ws]
    outs = pl.pallas_call(
        body, name=name, grid=(nb,), out_shape=shapes * 3, in_specs=specs * 4, out_specs=tuple(specs * 3),
        compiler_params=_params(("arbitrary",)),
    )(*ws, *gs, *ms, *vs)
    return outs[:n], outs[n:2 * n], outs[2 * n:]


def _adamw_from_partials(name, ws, parts, ms, vs, nb):
    n = len(ws)

    def body(*refs):
        for a in range(n):
            part = refs[n + a]
            g = part[0].astype(F32)
            for p in range(1, NDEV):
                g = g + part[p].astype(F32)
            delta, m2, v2 = _adamw_math(refs[a][...], g, refs[2 * n + a][...], refs[3 * n + a][...])
            refs[4 * n + a][...] = g
            refs[5 * n + a][...] = delta
            refs[6 * n + a][...] = m2
            refs[7 * n + a][...] = v2

    shapes = tuple(jax.ShapeDtypeStruct(w.shape, F32) for w in ws)
    specs = [_blocked(w.shape, nb) for w in ws]
    outs = pl.pallas_call(
        body, name=name, grid=(nb,), out_shape=shapes * 4,
        in_specs=specs + [_blocked(p.shape, nb, 1) for p in parts] + specs * 2, out_specs=tuple(specs * 4),
        compiler_params=_params(("arbitrary",)),
    )(*ws, *parts, *ms, *vs)
    return outs[:n], outs[n:2 * n], outs[2 * n:3 * n], outs[3 * n:]


def _adamw_ada(w, m, v, c_all_t, dmod_rows):
    nb = 4

    def body(w_ref, m_ref, v_ref, c_ref, dm_ref, g_ref, d_ref, m2_ref, v2_ref):
        g = _dot(c_ref[...], dm_ref[...].astype(BF))
        g_ref[...] = g
        delta, m2, v2 = _adamw_math(w_ref[...], g, m_ref[...], v_ref[...])
        d_ref[...] = delta
        m2_ref[...] = m2
        v2_ref[...] = v2

    shp = jax.ShapeDtypeStruct(w.shape, F32)
    spec = _blocked(w.shape, nb)
    return pl.pallas_call(
        body, name="adamw_ada", grid=(nb,), out_shape=(shp, shp, shp, shp),
        in_specs=[spec, spec, spec, _blocked(c_all_t.shape, nb), _full(dmod_rows.shape)],
        out_specs=(spec, spec, spec, spec), compiler_params=_params(("arbitrary",)),
    )(w, m, v, c_all_t, dmod_rows)


def _w_in_to_kernel(w):
    return jnp.concatenate([w[:, 0:448], jnp.zeros((w.shape[0], 64), w.dtype), w[:, 448:960]], axis=1)


def _w_in_from_kernel(w):
    return jnp.concatenate([w[:, 0:448], w[:, 512:1024]], axis=1)


def _w_uq_to_kernel(w):
    r = w.shape[0]
    return jnp.concatenate([w[:, :, 0:NOPE].reshape(r, HEADS * NOPE),
                            w[:, :, NOPE:NOPE + HALF].reshape(r, HEADS * HALF),
                            w[:, :, NOPE + HALF:].reshape(r, HEADS * HALF)], axis=1)


def _w_uq_from_kernel(w):
    r = w.shape[0]
    return jnp.concatenate([w[:, 0:512].reshape(r, HEADS, NOPE), w[:, 512:640].reshape(r, HEADS, HALF),
                            w[:, 640:768].reshape(r, HEADS, HALF)], axis=2)


REP_NAMES = ("w_uk", "w_uv", "w_pool", "g_mix", "g_q", "g_kv", "pool_scale", "g_ffn", "g_final")


def kernel(x, c, positions, w_ada, b_ada, g_mix, w_in, g_q, g_kv, w_uq, w_uk, w_uv, w_pool, pool_scale, w_o, g_ffn, w_gate, w_up, w_down, g_final, loss_target, m_w_ada, m_b_ada, m_g_mix, m_w_in, m_g_q, m_g_kv, m_w_uq, m_w_uk, m_w_uv, m_w_pool, m_pool_scale, m_w_o, m_g_ffn, m_w_gate, m_w_up, m_w_down, m_g_final, v_w_ada, v_b_ada, v_g_mix, v_w_in, v_g_q, v_g_kv, v_w_uq, v_w_uk, v_w_uv, v_w_pool, v_pool_scale, v_w_o, v_g_ffn, v_w_gate, v_w_up, v_w_down, v_g_final):
    given = dict(locals())

    merge = lambda g: g.reshape(NDEV * g.shape[1], g.shape[2])
    w_in_p, w_uq_p = (merge(g) for g in _sequencer_gather(
        "gather_in", 3, (_w_in_to_kernel(w_in[0]).astype(BF), _w_uq_to_kernel(w_uq[0]).astype(BF))))

    rope = _rope_tables(positions[0])
    mod, c_all8 = _ada_mod(c, w_ada[0], b_ada, rope[3][0:8, :])
    c_all = c_all8[:, 0, :]
    late = _sequencer_gather(
        "gather_late", 1, (w_o[0].astype(BF), w_gate[0].T.astype(BF), w_up[0].T.astype(BF), w_down[0].astype(BF)),
        after=(mod[:, 0:128], w_in_p[0:16, 0:128], w_uq_p[0:16, 0:128]))

    def ffn_grads_exchange(arrays):
        return _sequencer_scatter("scatter_ffn", 2, arrays)

    loss, dx, dmod, tail_grads, ffn_parts, replicated = _local_step(
        x[0], rope, loss_target[0], mod, g_mix, w_in_p, g_q, g_kv, w_uq_p, w_uk[0], w_uv[0], w_pool[0],
        pool_scale, g_ffn, g_final.reshape(1, D), tuple(merge(g) for g in late), ffn_grads_exchange)

    flat = jnp.concatenate([replicated[k].reshape(-1) for k in REP_NAMES] + [loss.reshape(1)])
    flat = jnp.pad(flat, (0, NDEV * REP_ROWS * 128 - flat.shape[0])).reshape(NDEV, REP_ROWS, 128)
    dmod_blocks = jnp.pad(dmod.reshape(NDEV, MODC // 128, 128), ((0, 0), (0, MOD_ROWS - MODC // 128), (0, 0)))
    got, red = _small_all_reduce(jnp.concatenate([dmod_blocks, flat], axis=1))

    tail_parts = _sequencer_scatter("scatter_tail", 4, tail_grads,
                                    after=(ffn_parts[0][0, 0:16, 0:128], red[0, 0:8, :]))
    g_in_p, g_uq_p = _sum_partials("sum_tail_partials", tail_parts, 1)
    as_transpose = ("w_in", "w_gate", "w_up")
    grads = dict(w_in=_w_in_from_kernel(g_in_p).T, w_uq=_w_uq_from_kernel(g_uq_p))
    partials = dict(w_gate=ffn_parts[0], w_up=ffn_parts[1], w_down=ffn_parts[2], w_o=ffn_parts[3])
    dmod_rows = got[:, 0:MODC // 128, :].reshape(NDEV, MODC)
    grads["b_ada"] = red[:, 0:MODC // 128, :].reshape(1, N_MOD * D)
    rep_flat = red[:, MOD_ROWS:, :].reshape(-1)
    off = 0
    for k in REP_NAMES:
        size = int(np.prod(given[k].shape))
        grads[k] = rep_flat[off:off + size]
        off += size

    view = {k: (given[k].shape[1:] if given[k].ndim > 2 else given[k].shape)
            for k in REP_NAMES + ("b_ada", "w_ada", "w_in", "w_uq", "w_o", "w_gate", "w_up", "w_down")}
    view.update(g_final=(1, D))
    names = ["w_ada", "b_ada", "g_mix", "w_in", "g_q", "g_kv", "w_uq", "w_uk", "w_uv", "w_pool", "pool_scale",
             "w_o", "g_ffn", "w_gate", "w_up", "w_down", "g_final"]
    g_ada, d_ada, m_ada, v_ada = _adamw_ada(w_ada[0], m_w_ada[0], v_w_ada[0], c_all.T.astype(BF), dmod_rows)
    out_g, out_d, out_m, out_v = dict(w_ada=g_ada), dict(w_ada=d_ada), dict(w_ada=m_ada), dict(w_ada=v_ada)
    groups = (("adamw_ffn", ("w_gate", "w_up", "w_down", "w_o"), 4),
              ("adamw_replicated", REP_NAMES + ("b_ada",), 1),
              ("adamw_tail", ("w_in", "w_uq"), 1))
    for gname, members, nb in groups:
        turn = lambda k, t: t.T if k in as_transpose else t
        ws = [turn(k, given[k].reshape(view[k])) for k in members]
        ms = [turn(k, given["m_" + k].reshape(view[k])) for k in members]
        vs = [turn(k, given["v_" + k].reshape(view[k])) for k in members]
        if members[0] in partials:
            gs, ds, m2, v2 = _adamw_from_partials(gname, ws, [partials[k] for k in members], ms, vs, nb)
        else:
            gs = [grads[k] if k in as_transpose else grads[k].reshape(view[k]) for k in members]
            ds, m2, v2 = _adamw_group(gname, ws, gs, ms, vs, nb)
        for k, g, d, mm, vv in zip(members, gs, ds, m2, v2):
            out_g[k], out_d[k], out_m[k], out_v[k] = turn(k, g), turn(k, d), turn(k, mm), turn(k, vv)

    total = rep_flat[off]
    shaped = lambda d: [d[k].reshape(given[k].shape) for k in names]
    return (total, dx[None], *shaped(out_g), *shaped(out_d), *shaped(out_m), *shaped(out_v))
```

```python
import numpy as np
import jax
import jax.numpy as jnp
from jax import lax
from jax.experimental import pallas as pl
from jax.experimental.pallas import tpu as pltpu
from jax.experimental.pallas import tpu_sc as plsc

D = 1024
HEADS = 4
NOPE = 128
ROPE = 64
HALF = ROPE // 2
QL = 256
KVL = 128
FF = 2816
PW = 512
GROUPS = 4
GD = 128
N_MOD = 6
EPS = 1e-6
SM_SCALE = (NOPE + ROPE) ** -0.5
LOG2_E = 1.4426950408889634
EXP2_SCALE = SM_SCALE * LOG2_E
ROPE_THETA = 10000.0
NDEV = 8
MODC = N_MOD * D // NDEV

ADAM_LR = 0.001
ADAM_B1 = 0.9
ADAM_B2 = 0.999
ADAM_EPS = 1e-08
ADAM_WD = 0.01
ADAM_STEP = 10

BF = jnp.bfloat16
F32 = jnp.float32
VMEM_LIMIT_V7X = 60 * 1024 * 1024
MESH = pl.DeviceIdType.MESH

TQ = 256
TK = 256
QW = 256
VPU_ROWS = 16
MOD_ROWS = 8
REP_ROWS = 200
SMALL_ROWS = MOD_ROWS + REP_ROWS


def _params(sem=None):
    return pltpu.CompilerParams(dimension_semantics=sem, vmem_limit_bytes=VMEM_LIMIT_V7X)


def _dot(a, b):
    return jnp.dot(a, b, preferred_element_type=F32)


def _dot_nt(a, b):
    return lax.dot_general(a, b, (((1,), (1,)), ((), ())), preferred_element_type=F32)


def _dot_tn(a, b):
    return _dot(a.astype(F32).T.astype(BF), b)


def _full(shape):
    return pl.BlockSpec(shape, lambda *_: (0,) * len(shape))


def _rows(ts, cols):
    return pl.BlockSpec((ts, cols), lambda i: (i, 0))


def _vmem():
    return pl.BlockSpec(memory_space=pltpu.VMEM)


def _any():
    return pl.BlockSpec(memory_space=pl.ANY)


def _rms(v):
    return lax.rsqrt(jnp.mean(v * v, axis=-1, keepdims=True) + EPS)


def _rms_bwd(dn, n, r):
    return r * (dn - n * jnp.mean(dn * n, axis=-1, keepdims=True))


def _colsum(v):
    return jnp.sum(v, axis=0, keepdims=True)


def _swap_halves(v):
    lane = lax.broadcasted_iota(jnp.int32, v.shape, 1)
    return jnp.where(lane < HALF, pltpu.roll(v, 128 - HALF, 1), pltpu.roll(v, HALF, 1))


def _window_lane_width():
    lane = lax.broadcasted_iota(jnp.int32, (1, PW), 1)
    return jnp.where(lane < 128, 2.0, jnp.where(lane < 256, 4.0, jnp.where(lane < 384, 8.0, 16.0))).astype(F32)


def _window_sums(ext, back):
    n = ext.shape[0]

    def sh(v, k):
        return pltpu.roll(v, k if back else n - k, 0)

    s2 = ext + sh(ext, 1)
    e4 = s2[:, 128:]
    s4 = e4 + sh(e4, 2)
    e8 = s4[:, 128:]
    s8 = e8 + sh(e8, 4)
    e16 = s8[:, 128:]
    s16 = e16 + sh(e16, 8)
    return jnp.concatenate([s2[:, :128], s4[:, :128], s8[:, :128], s16], axis=1)


def _fill_block_diagonal(dst_ref, blocks_ref):
    n, r, c = blocks_ref.shape
    dst_ref[...] = jnp.zeros_like(dst_ref)
    for b in range(n):
        dst_ref[b * r:(b + 1) * r, b * c:(b + 1) * c] = blocks_ref[b]


def _row_counts(first_row, ts):
    t1 = (first_row + lax.broadcasted_iota(jnp.int32, (ts, 1), 0) + 1).astype(F32)
    return jnp.minimum(t1, _window_lane_width())


def _fwd_in(x, mod, g_mix, w_in, g_q, g_kv, w_uq, wuk_dc, perm, cos4, sin4, csk, snk, w_pool, pool_scale):
    S = x.shape[0]
    ts = 512
    nsub = ts // TQ

    def body(x_ref, mod_ref, gmix_ref, win_ref, gq_ref, gkv_ref, wuq_ref, wuk_ref, perm_ref, cos_ref, sin_ref,
             csk_ref, snk_ref, wpool_ref, pscale_ref,
             h1_ref, raw_ref, qn_ref, qs_ref, kv_ref, kvt_ref, pooled_ref, ypre_ref, ypool_ref, carry_ref, wuk_bd,
             wpool_bd):
        i = pl.program_id(0)

        @pl.when(i == 0)
        def _():
            carry_ref[...] = jnp.zeros_like(carry_ref)
            _fill_block_diagonal(wuk_bd, wuk_ref)
            _fill_block_diagonal(wpool_bd, wpool_ref)

        xv = x_ref[...]
        sh1 = mod_ref[0:1, 0:D]
        sc1 = mod_ref[0:1, D:2 * D]
        h = (xv * _rms(xv)) * gmix_ref[...] * (1.0 + sc1) + sh1
        hb = h.astype(BF)
        h1_ref[...] = hb
        proj = _dot(hb, win_ref[...])
        cq_raw = proj[:, 0:QL]
        ckv_raw = proj[:, QL:QL + KVL]
        kr = proj[:, 384:512]
        u = proj[:, 512:1024]
        raw_ref[...] = proj[:, 0:384]

        c_q = (cq_raw * _rms(cq_raw)) * gq_ref[...]
        c_kv = (ckv_raw * _rms(ckv_raw)) * gkv_ref[...]
        q = _dot(c_q.astype(BF), wuq_ref[...])
        qn = q[:, 0:HEADS * NOPE].astype(BF)
        qn_ref[...] = qn
        x1 = q[:, 512:640]
        x2 = q[:, 640:768]
        cosv = cos_ref[...]
        sinv = sin_ref[...]
        roped = jnp.concatenate([x1 * cosv - x2 * sinv, x1 * sinv + x2 * cosv], axis=1).astype(BF)
        q_lat = _dot(qn, wuk_bd[...])
        q_rope = _dot(roped, perm_ref[...])
        for hd in range(HEADS):
            cols = slice(hd * 128, (hd + 1) * 128)
            qh = jnp.concatenate([q_lat[:, cols], q_rope[:, cols]], axis=1).astype(BF)
            for a in range(nsub):
                qs_ref[a, hd * TQ:(hd + 1) * TQ, :] = qh[a * TQ:(a + 1) * TQ, :]
        k_rope = kr * csk_ref[...] + _swap_halves(kr) * snk_ref[...]
        keys = jnp.concatenate([c_kv, k_rope], axis=1)
        kv_ref[...] = keys.astype(BF)
        for a in range(ts // TK):
            kvt_ref[a] = keys[a * TK:(a + 1) * TK, :].T.astype(BF)

        ext = jnp.concatenate([carry_ref[...], u], axis=0)
        win = _window_sums(ext, True)[16:, :]
        pooled = (win / _row_counts(i * ts, ts) - u).astype(BF)
        pooled_ref[...] = pooled
        carry_ref[...] = u[ts - 16:ts, :]
        ypre = _dot(pooled, wpool_bd[...])
        ypre_ref[...] = ypre
        ypool_ref[...] = (ypre * pscale_ref[...]).astype(BF)

    out_shape = (
        jax.ShapeDtypeStruct((S, D), BF),
        jax.ShapeDtypeStruct((S, 384), F32),
        jax.ShapeDtypeStruct((S, HEADS * NOPE), BF),
        jax.ShapeDtypeStruct((S // TQ, HEADS * TQ, QW), BF),
        jax.ShapeDtypeStruct((S, QW), BF),
        jax.ShapeDtypeStruct((S // TK, QW, TK), BF),
        jax.ShapeDtypeStruct((S, PW), BF),
        jax.ShapeDtypeStruct((S, PW), F32),
        jax.ShapeDtypeStruct((S, PW), BF),
    )
    in_specs = [
        _rows(ts, D), _full(mod.shape), _full((1, D)), _full(w_in.shape), _full((1, QL)), _full((1, KVL)),
        _full(w_uq.shape), _full(wuk_dc.shape), _full(perm.shape), _rows(ts, 128), _rows(ts, 128), _rows(ts, 128),
        _rows(ts, 128), _full(w_pool.shape), _full((1, PW)),
    ]
    out_specs = (
        _rows(ts, D), _rows(ts, 384), _rows(ts, HEADS * NOPE),
        pl.BlockSpec((nsub, HEADS * TQ, QW), lambda i: (i, 0, 0)),
        _rows(ts, QW), pl.BlockSpec((ts // TK, QW, TK), lambda i: (i, 0, 0)), _rows(ts, PW), _rows(ts, PW),
        _rows(ts, PW),
    )
    return pl.pallas_call(
        body, name="fwd_in", out_shape=out_shape, grid=(S // ts,), in_specs=in_specs, out_specs=out_specs,
        scratch_shapes=[pltpu.VMEM((16, PW), F32), pltpu.VMEM((HEADS * NOPE, HEADS * KVL), BF),
                        pltpu.VMEM((PW, PW), BF)],
        compiler_params=_params(("arbitrary",)),
    )(x, mod, g_mix, w_in, g_q, g_kv, w_uq, wuk_dc, perm, cos4, sin4, csk, snk, w_pool, pool_scale)


def _diag_mask(shape, q_axis):
    qi = (lax.broadcasted_iota(jnp.int32, shape, q_axis) & (TQ - 1)) >> 6
    ki = lax.broadcasted_iota(jnp.int32, shape, 1 - q_axis) >> 6
    return ki <= qi


def _attn_fwd(qs, kv, kvt, wuv_vc):
    nq = qs.shape[0]
    S = kv.shape[0]
    M = HEADS * TQ

    def body(qs_ref, kv_ref, kvt_ref, wuv_ref, olat_ref, ymla_ref, lse_ref):
        i = pl.program_id(0)
        q = qs_ref[0]

        def step(kt, carry, masked):
            m, l, acc = carry
            k = kv_ref[pl.ds(pl.multiple_of(kt * TK, TK), TK), :]
            v_t = kvt_ref[kt][0:KVL, :]
            s = _dot_nt(k, q)
            if masked:
                s = jnp.where(_diag_mask((TK, M), 1), s, -jnp.inf)
            m_new = jnp.maximum(m, jnp.max(s, axis=0, keepdims=True))
            alpha = jnp.exp2((m - m_new) * EXP2_SCALE)
            p = jnp.exp2((s - m_new) * EXP2_SCALE)
            l = alpha * l + jnp.sum(p, axis=0, keepdims=True)
            acc = alpha * acc + _dot(v_t, p.astype(BF))
            return m_new, l, acc

        init = (jnp.full((1, M), -jnp.inf, F32), jnp.zeros((1, M), F32), jnp.zeros((KVL, M), F32))
        carry = lax.fori_loop(0, i, lambda kt, c: step(kt, c, False), init)
        m, l, acc = step(i, carry, True)
        o_lat = acc / l
        olat_ref[0] = o_lat
        lse_ref[0] = jnp.broadcast_to(m * SM_SCALE + jnp.log(l), (8, M))
        for hd in range(HEADS):
            o_t = _dot(wuv_ref[hd], o_lat[:, hd * TQ:(hd + 1) * TQ].astype(BF))
            ymla_ref[:, hd * 128:(hd + 1) * 128] = o_t.T.astype(BF)

    out_shape = (
        jax.ShapeDtypeStruct((nq, KVL, M), F32),
        jax.ShapeDtypeStruct((S, HEADS * 128), BF),
        jax.ShapeDtypeStruct((nq, 8, M), F32),
    )
    return pl.pallas_call(
        body, name="attn_fwd", out_shape=out_shape, grid=(nq,),
        in_specs=[pl.BlockSpec((1, M, QW), lambda i: (i, 0, 0)), _full(kv.shape), _full(kvt.shape),
                  _full(wuv_vc.shape)],
        out_specs=(pl.BlockSpec((1, KVL, M), lambda i: (i, 0, 0)), _rows(TQ, HEADS * 128),
                   pl.BlockSpec((1, 8, M), lambda i: (i, 0, 0))),
        compiler_params=_params(("arbitrary",)),
    )(qs, kv, kvt, wuv_vc)


def _silu_parts(a):
    sg = jax.nn.sigmoid(a)
    return sg, a * sg


def _ffn_fwd(x, ymla, ypool, mod, w_o, g_ffn, wg_t, wu_t, wd, g_final, target):
    S = x.shape[0]
    ts = 256

    def body(x_ref, ymla_ref, ypool_ref, mod_ref, wo_ref, gffn_ref, wg_ref, wu_ref, wd_ref, gfin_ref, t_ref,
             x2_ref, mix_ref, h2t_ref, a_ref, b_ref, dx3_ref, dff_ref, dfft_ref, loss_ref, dgfin_ref, dgt2_ref,
             f_ref):
        i = pl.program_id(0)

        @pl.when(i == 0)
        def _():
            loss_ref[...] = jnp.zeros_like(loss_ref)
            dgfin_ref[...] = jnp.zeros_like(dgfin_ref)
            dgt2_ref[...] = jnp.zeros_like(dgt2_ref)

        gt1 = mod_ref[0:1, 2 * D:3 * D]
        sh2 = mod_ref[0:1, 3 * D:4 * D]
        sc2 = mod_ref[0:1, 4 * D:5 * D]
        gt2 = mod_ref[0:1, 5 * D:6 * D]
        cat = jnp.concatenate([ymla_ref[...], ypool_ref[...]], axis=1)
        mix = _dot(cat, wo_ref[...])
        mix_ref[...] = mix
        x2 = x_ref[...] + gt1 * mix
        x2_ref[...] = x2
        h2 = (x2 * _rms(x2)) * gffn_ref[...] * (1.0 + sc2) + sh2
        h2b = h2.astype(BF)
        h2t_ref[...] = h2.T.astype(BF)

        for c in range(FF // FCHUNK):
            cols = slice(c * FCHUNK, (c + 1) * FCHUNK)
            a = _dot_nt(h2b, wg_ref[cols, :])
            b = _dot_nt(h2b, wu_ref[cols, :])
            a_ref[:, cols] = a.astype(BF)
            b_ref[:, cols] = b.astype(BF)
            f_ref[:, cols] = (_silu_parts(a)[1] * b).astype(BF)
        ff = _dot(f_ref[...], wd_ref[...])

        x3 = x2 + gt2 * ff
        r3 = _rms(x3)
        xn3 = x3 * r3
        gfin = gfin_ref[...]
        e = xn3 * gfin - t_ref[...]
        loss_ref[...] += 0.5 * jnp.sum(jnp.mean(e * e, axis=-1, keepdims=True))
        dy = e * (1.0 / D)
        dgfin_ref[...] += _colsum(dy * xn3)
        dx3 = _rms_bwd(dy * gfin, xn3, r3)
        dx3_ref[...] = dx3
        dgt2_ref[...] += _colsum(dx3 * ff)
        dff = dx3 * gt2
        dff_ref[...] = dff.astype(BF)
        dfft_ref[...] = dff.T.astype(BF)

    row = lambda c: _rows(ts, c)
    col = pl.BlockSpec((D, ts), lambda i: (0, i))
    const = _full
    out_shape = (
        jax.ShapeDtypeStruct((S, D), F32),
        jax.ShapeDtypeStruct((S, D), F32),
        jax.ShapeDtypeStruct((D, S), BF),
        jax.ShapeDtypeStruct((S, FF), BF),
        jax.ShapeDtypeStruct((S, FF), BF),
        jax.ShapeDtypeStruct((S, D), F32),
        jax.ShapeDtypeStruct((S, D), BF),
        jax.ShapeDtypeStruct((D, S), BF),
        jax.ShapeDtypeStruct((8, 128), F32),
        jax.ShapeDtypeStruct((1, D), F32),
        jax.ShapeDtypeStruct((1, D), F32),
    )
    return pl.pallas_call(
        body, name="ffn_fwd", out_shape=out_shape, grid=(S // ts,),
        in_specs=[row(D), row(PW), row(PW), const(mod.shape), _vmem(), const((1, D)), _vmem(), _vmem(), _vmem(),
                  const((1, D)), row(D)],
        out_specs=(row(D), row(D), col, row(FF), row(FF), row(D), row(D), col, const((8, 128)), const((1, D)),
                   const((1, D))),
        scratch_shapes=[pltpu.VMEM((ts, FF), BF)],
        compiler_params=_params(("arbitrary",)),
    )(x, ymla, ypool, mod, w_o, g_ffn, wg_t, wu_t, wd, g_final, target)


FCHUNK = 256


def _ffn_bwd_acts(dff, a, b, wg_t, wu_t, wd):
    S = dff.shape[0]
    ts = 512

    def body(dff_ref, a_ref, b_ref, wg_ref, wu_ref, wd_ref, da_ref, db_ref, dh2_ref):
        dffb = dff_ref[...]
        for c in range(FF // FCHUNK):
            cols = slice(c * FCHUNK, (c + 1) * FCHUNK)
            df = _dot_nt(dffb, wd_ref[cols, :])
            av = a_ref[:, cols].astype(F32)
            bv = b_ref[:, cols].astype(F32)
            sg, sa = _silu_parts(av)
            db_ref[:, cols] = (df * sa).astype(BF)
            da_ref[:, cols] = (df * bv * (sg * (1.0 + av * (1.0 - sg)))).astype(BF)
        dh2_ref[...] = _dot(da_ref[...], wg_ref[...]) + _dot(db_ref[...], wu_ref[...])

    act = _rows(ts, FF)
    return pl.pallas_call(
        body, name="ffn_bwd_acts",
        out_shape=(jax.ShapeDtypeStruct((S, FF), BF), jax.ShapeDtypeStruct((S, FF), BF),
                   jax.ShapeDtypeStruct((S, D), F32)),
        grid=(S // ts,), in_specs=[_rows(ts, D), act, act, _vmem(), _vmem(), _vmem()],
        out_specs=(act, act, _rows(ts, D)), compiler_params=_params(("arbitrary",)),
    )(dff, a, b, wg_t, wu_t, wd)


def _ffn_bwd_weights(dff_t, h2_t, da, db, a, b):
    S = da.shape[0]

    def body(dfft_ref, h2t_ref, da_ref, db_ref, a_ref, b_ref, dwg_ref, dwu_ref, dwd_ref):
        h2t = h2t_ref[...]
        dwg_ref[...] = _dot(h2t, da_ref[...]).T.astype(BF)
        dwu_ref[...] = _dot(h2t, db_ref[...]).T.astype(BF)
        f = (_silu_parts(a_ref[...].astype(F32))[1] * b_ref[...].astype(F32)).astype(BF)
        dwd_ref[...] = _dot(dfft_ref[...], f).T.astype(BF)

    act = pl.BlockSpec((S, FCHUNK), lambda j: (0, j))
    wblk = _rows(FCHUNK, D)
    shp = jax.ShapeDtypeStruct((FF, D), BF)
    return pl.pallas_call(
        body, name="ffn_bwd_weights", out_shape=(shp, shp, shp), grid=(FF // FCHUNK,),
        in_specs=[_vmem(), _vmem(), act, act, act, act], out_specs=(wblk, wblk, wblk),
        compiler_params=_params(("arbitrary",)),
    )(dff_t, h2_t, da, db, a, b)


def _mix_bwd(dh2, dx3, x2, mix, mod, g_ffn, ymla, ypool, w_o, ypre, pooled, pool_scale, wpool_dc, olat, wuv_vc):
    S = dh2.shape[0]
    ts = 512
    n = S // ts
    nsub = ts // TQ
    M = HEADS * TQ

    def body(dh2_ref, dx3_ref, x2_ref, mix_ref, mod_ref, gffn_ref, ymla_ref, ypool_ref, wo_ref, ypre_ref, pooled_ref,
             pscale_ref, wpool_ref, olat_ref, wuv_ref,
             dx2_ref, du_ref, dolat_ref, delta_ref, dwo_ref, dwuv_ref, dwpool_ref, dpscale_ref, dgt1_ref, dsc2_ref,
             dsh2_ref, dgffn_ref, carry_ref, dwo_acc, dwpool_acc, wpool_bd, wuv_bd):
        i = pl.program_id(0)

        @pl.when(i == 0)
        def _():
            carry_ref[...] = jnp.zeros_like(carry_ref)
            dwo_acc[...] = jnp.zeros_like(dwo_acc)
            dwpool_acc[...] = jnp.zeros_like(dwpool_acc)
            _fill_block_diagonal(wpool_bd, wpool_ref)
            _fill_block_diagonal(wuv_bd, wuv_ref)
            for r in (dwuv_ref, dpscale_ref, dgt1_ref, dsc2_ref, dsh2_ref, dgffn_ref):
                r[...] = jnp.zeros_like(r)

        gt1 = mod_ref[0:1, 2 * D:3 * D]
        sc2 = mod_ref[0:1, 4 * D:5 * D]
        gffn = gffn_ref[...]
        dh2 = dh2_ref[...]
        x2 = x2_ref[...]
        r2 = _rms(x2)
        xn2 = x2 * r2
        along = _colsum(dh2 * xn2)
        dsc2_ref[...] += along * gffn
        dsh2_ref[...] += _colsum(dh2)
        dgffn_ref[...] += along * (1.0 + sc2)
        dx2 = dx3_ref[...] + _rms_bwd(dh2 * (gffn * (1.0 + sc2)), xn2, r2)
        dx2_ref[...] = dx2
        dgt1_ref[...] += _colsum(dx2 * mix_ref[...])
        dmix = (dx2 * gt1).astype(BF)
        cat = jnp.concatenate([ymla_ref[...], ypool_ref[...]], axis=1)
        dwo_acc[...] += _dot_tn(cat, dmix)
        dcat = _dot_nt(dmix, wo_ref[...])
        dymla = dcat[:, 0:512]
        dypool = dcat[:, 512:1024]

        dpscale_ref[...] += _colsum(dypool * ypre_ref[...])
        dypre = (dypool * pscale_ref[...]).astype(BF)
        dwpool_acc[...] += _dot_tn(pooled_ref[...], dypre)
        dpooled = _dot(dypre, wpool_bd[...])
        tile = n - 1 - i
        e = dpooled / _row_counts(tile * ts, ts)
        ext = jnp.concatenate([e, carry_ref[...]], axis=0)
        du_ref[...] = _window_sums(ext, False)[0:ts, :] - dpooled
        carry_ref[...] = e[0:16, :]

        dob_all = dymla.astype(BF)
        dol_all = _dot(dob_all, wuv_bd[...])
        for hd in range(HEADS):
            dob = dob_all[:, hd * 128:(hd + 1) * 128]
            dol = dol_all[:, hd * 128:(hd + 1) * 128]
            for a in range(nsub):
                ol_t = olat_ref[a, :, hd * TQ:(hd + 1) * TQ]
                dl = dol[a * TQ:(a + 1) * TQ, :]
                dolat_ref[a, hd * TQ:(hd + 1) * TQ, :] = dl.astype(BF)
                dwuv_ref[hd] += _dot(ol_t.astype(BF), dob[a * TQ:(a + 1) * TQ, :])
                delta = jnp.sum(dl * ol_t.T, axis=-1, keepdims=True)
                delta_ref[a, :, hd * TQ:(hd + 1) * TQ] = jnp.broadcast_to(delta, (TQ, 128)).T[0:8, :]

        @pl.when(i == n - 1)
        def _():
            dwo_ref[...] = dwo_acc[...].astype(BF)
            for g in range(GROUPS):
                dwpool_ref[g] = dwpool_acc[g * GD:(g + 1) * GD, g * GD:(g + 1) * GD]

    rev = lambda c: pl.BlockSpec((ts, c), lambda i: (n - 1 - i, 0))
    rev3 = lambda r, c: pl.BlockSpec((nsub, r, c), lambda i: (n - 1 - i, 0, 0))
    out_shape = (
        jax.ShapeDtypeStruct((S, D), F32),
        jax.ShapeDtypeStruct((S, PW), F32),
        jax.ShapeDtypeStruct((S // TQ, M, KVL), BF),
        jax.ShapeDtypeStruct((S // TQ, 8, M), F32),
        jax.ShapeDtypeStruct((D, D), BF),
        jax.ShapeDtypeStruct((HEADS, KVL, 128), F32),
        jax.ShapeDtypeStruct((GROUPS, GD, GD), F32),
        jax.ShapeDtypeStruct((1, PW), F32),
        jax.ShapeDtypeStruct((1, D), F32), jax.ShapeDtypeStruct((1, D), F32), jax.ShapeDtypeStruct((1, D), F32),
        jax.ShapeDtypeStruct((1, D), F32),
    )
    in_specs = [rev(D), rev(D), rev(D), rev(D), _full(mod.shape), _full((1, D)), rev(PW), rev(PW), _full(w_o.shape),
                rev(PW), rev(PW), _full((1, PW)), _full(wpool_dc.shape), rev3(KVL, M), _full(wuv_vc.shape)]
    out_specs = (rev(D), rev(PW), rev3(M, KVL), rev3(8, M), _full((D, D)), _full((HEADS, KVL, 128)),
                 _full((GROUPS, GD, GD)), _full((1, PW)), _full((1, D)), _full((1, D)), _full((1, D)), _full((1, D)))
    return pl.pallas_call(
        body, name="mix_bwd", out_shape=out_shape, grid=(n,), in_specs=in_specs, out_specs=out_specs,
        scratch_shapes=[pltpu.VMEM((16, PW), F32), pltpu.VMEM((D, D), F32), pltpu.VMEM((PW, PW), F32),
                        pltpu.VMEM((PW, PW), BF), pltpu.VMEM((HEADS * 128, HEADS * KVL), BF)],
        compiler_params=_params(("arbitrary",)),
    )(dh2, dx3, x2, mix, mod, g_ffn, ymla, ypool, w_o, ypre, pooled, pool_scale, wpool_dc, olat, wuv_vc)


def _attn_bwd(qs, kv, dolat, lse, delta):
    nq = qs.shape[0]
    S = kv.shape[0]
    M = HEADS * TQ
    nk = S // TK

    def body(qs_ref, kv_ref, do_ref, lse_ref, delta_ref, dkv_ref, dqt_ref, p_ref, ds_ref):
        kt = pl.program_id(0)
        k = kv_ref[...]
        v = k[:, 0:KVL]
        k_t = k.astype(F32).T.astype(BF)

        @pl.when(kt == 0)
        def _():
            dqt_ref[...] = jnp.zeros_like(dqt_ref)

        def step(qi, carry, masked):
            dk, dv = carry
            q = qs_ref[qi]
            do = do_ref[qi]
            s = _dot_nt(k, q)
            dp = _dot_nt(v, do)
            lse_row = lse_ref[qi, 0:1, :] * LOG2_E
            delta_row = delta_ref[qi, 0:1, :]
            q_chunk = (lax.broadcasted_iota(jnp.int32, (1, M), 1) & (TQ - 1)) >> 6
            for r in range(0, TK, VPU_ROWS):
                rows = slice(r, r + VPU_ROWS)
                p = jnp.exp2(s[rows, :] * EXP2_SCALE - lse_row)
                if masked:
                    p = jnp.where((r >> 6) <= q_chunk, p, 0.0)
                p_ref[rows, :] = p.astype(BF)
                ds_ref[rows, :] = (p * (dp[rows, :] - delta_row) * SM_SCALE).astype(BF)
            ds = ds_ref[...]
            dv = dv + _dot(p_ref[...], do)
            dk = dk + _dot(ds, q)
            dqt_ref[qi] += _dot(k_t, ds)
            return dk, dv

        carry = step(kt, (jnp.zeros((TK, QW), F32), jnp.zeros((TK, KVL), F32)), True)
        dk, dv = lax.fori_loop(kt + 1, nq, lambda qi, c: step(qi, c, False), carry)
        dkv_ref[...] = dk + jnp.concatenate([dv, jnp.zeros((TK, QW - KVL), F32)], axis=1)

    out_shape = (jax.ShapeDtypeStruct((S, QW), F32), jax.ShapeDtypeStruct((nq, QW, M), F32))
    return pl.pallas_call(
        body, name="attn_bwd", out_shape=out_shape, grid=(nk,),
        in_specs=[_vmem(), _rows(TK, QW), _vmem(), _vmem(), _vmem()],
        out_specs=(_rows(TK, QW), _vmem()),
        scratch_shapes=[pltpu.VMEM((TK, M), BF), pltpu.VMEM((TK, M), BF)],
        compiler_params=_params(("arbitrary",)),
    )(qs, kv, dolat, lse, delta)


def _in_bwd(dqt, dkv, du, raw, qn, h1, x, dx2, mod, g_mix, w_in, g_q, g_kv, w_uq, wuk_cd, perm_t, cos4, sin4, csk,
            snk):
    S = x.shape[0]
    ts = 512
    n = S // ts
    nsub = ts // TQ
    M = HEADS * TQ

    def body(dqt_ref, dkv_ref, du_ref, raw_ref, qn_ref, h1_ref, x_ref, dx2_ref, mod_ref, gmix_ref, win_ref, gq_ref,
             gkv_ref, wuq_ref, wuk_ref, permt_ref, cos_ref, sin_ref, csk_ref, snk_ref,
             dx_ref, dwin_ref, dwuq_ref, dwuk_ref, dgq_ref, dgkv_ref, dsc1_ref, dsh1_ref, dgmix_ref, dwin_acc,
             dwuq_acc, dwuk_acc, wuk_bd):
        i = pl.program_id(0)

        @pl.when(i == 0)
        def _():
            dwin_acc[...] = jnp.zeros_like(dwin_acc)
            dwuq_acc[...] = jnp.zeros_like(dwuq_acc)
            dwuk_acc[...] = jnp.zeros_like(dwuk_acc)
            _fill_block_diagonal(wuk_bd, wuk_ref)
            for r in (dgq_ref, dgkv_ref, dsc1_ref, dsh1_ref, dgmix_ref):
                r[...] = jnp.zeros_like(r)

        dq_blocks = [dqt_ref[a].T for a in range(nsub)]
        dq_heads = [jnp.concatenate([blk[hd * TQ:(hd + 1) * TQ, :] for blk in dq_blocks], axis=0)
                    for hd in range(HEADS)]
        dq_lat = jnp.concatenate([dqh[:, 0:KVL] for dqh in dq_heads], axis=1).astype(BF)
        dq_rope = jnp.concatenate([dqh[:, KVL:QW] for dqh in dq_heads], axis=1).astype(BF)
        dq_nope = _dot(dq_lat, wuk_bd[...])
        dwuk_acc[...] += _dot_tn(dq_lat, qn_ref[...])
        drope = _dot(dq_rope, permt_ref[...])
        do1 = drope[:, 0:128]
        do2 = drope[:, 128:256]
        cosv = cos_ref[...]
        sinv = sin_ref[...]
        dq = jnp.concatenate([dq_nope, do1 * cosv + do2 * sinv, do2 * cosv - do1 * sinv], axis=1).astype(BF)

        cq_raw = raw_ref[:, 0:QL]
        ckv_raw = raw_ref[:, QL:QL + KVL]
        rq = _rms(cq_raw)
        nq_ = cq_raw * rq
        gq = gq_ref[...]
        dwuq_acc[...] += _dot_tn((nq_ * gq).astype(BF), dq)
        dc_q = _dot_nt(dq, wuq_ref[...])
        dgq_ref[...] += _colsum(dc_q * nq_)
        dcq_raw = _rms_bwd(dc_q * gq, nq_, rq)

        dkv = dkv_ref[...]
        rk = _rms(ckv_raw)
        nk_ = ckv_raw * rk
        dc_kv = dkv[:, 0:KVL]
        dgkv_ref[...] += _colsum(dc_kv * nk_)
        dckv_raw = _rms_bwd(dc_kv * gkv_ref[...], nk_, rk)
        dkr_roped = dkv[:, KVL:QW]
        dkr = dkr_roped * csk_ref[...] - _swap_halves(dkr_roped) * snk_ref[...]

        dproj = jnp.concatenate([dcq_raw, dckv_raw, dkr, du_ref[...]], axis=1).astype(BF)
        dwin_acc[...] += _dot_tn(h1_ref[...], dproj)
        dh1 = _dot_nt(dproj, win_ref[...])

        sc1 = mod_ref[0:1, D:2 * D]
        gmix = gmix_ref[...]
        xv = x_ref[...]
        r1 = _rms(xv)
        xn1 = xv * r1
        along = _colsum(dh1 * xn1)
        dsc1_ref[...] += along * gmix
        dsh1_ref[...] += _colsum(dh1)
        dgmix_ref[...] += along * (1.0 + sc1)
        dx_ref[...] = dx2_ref[...] + _rms_bwd(dh1 * (gmix * (1.0 + sc1)), xn1, r1)

        @pl.when(i == n - 1)
        def _():
            dwin_ref[...] = dwin_acc[...].astype(BF)
            dwuq_ref[...] = dwuq_acc[...].astype(BF)
            for hd in range(HEADS):
                dwuk_ref[hd] = dwuk_acc[hd * KVL:(hd + 1) * KVL, hd * NOPE:(hd + 1) * NOPE]

    out_shape = (
        jax.ShapeDtypeStruct((S, D), F32),
        jax.ShapeDtypeStruct((D, D), BF),
        jax.ShapeDtypeStruct((QL, 768), BF),
        jax.ShapeDtypeStruct((HEADS, KVL, NOPE), F32),
        jax.ShapeDtypeStruct((1, QL), F32), jax.ShapeDtypeStruct((1, KVL), F32),
        jax.ShapeDtypeStruct((1, D), F32), jax.ShapeDtypeStruct((1, D), F32), jax.ShapeDtypeStruct((1, D), F32),
    )
    in_specs = [pl.BlockSpec((nsub, QW, M), lambda i: (i, 0, 0)), _rows(ts, QW), _rows(ts, PW), _rows(ts, 384),
                _rows(ts, HEADS * NOPE), _rows(ts, D), _rows(ts, D), _rows(ts, D), _full(mod.shape), _full((1, D)),
                _full(w_in.shape), _full((1, QL)), _full((1, KVL)), _full(w_uq.shape), _full(wuk_cd.shape),
                _full(perm_t.shape), _rows(ts, 128), _rows(ts, 128), _rows(ts, 128), _rows(ts, 128)]
    out_specs = (_rows(ts, D), _full((D, D)), _full((QL, 768)), _full((HEADS, KVL, NOPE)), _full((1, QL)),
                 _full((1, KVL)), _full((1, D)), _full((1, D)), _full((1, D)))
    return pl.pallas_call(
        body, name="in_bwd", out_shape=out_shape, grid=(n,), in_specs=in_specs, out_specs=out_specs,
        scratch_shapes=[pltpu.VMEM((D, D), F32), pltpu.VMEM((QL, 768), F32),
                        pltpu.VMEM((HEADS * KVL, HEADS * NOPE), F32), pltpu.VMEM((HEADS * KVL, HEADS * NOPE), BF)],
        compiler_params=_params(("arbitrary",)),
    )(dqt, dkv, du, raw, qn, h1, x, dx2, mod, g_mix, w_in, g_q, g_kv, w_uq, wuk_cd, perm_t, cos4, sin4, csk, snk)


def _rope_perm():
    p = np.zeros((HEADS, 2 * 128, 128), np.float32)
    for hd in range(HEADS):
        for t in range(HALF):
            p[hd, hd * HALF + t, t] = 1.0
            p[hd, 128 + hd * HALF + t, HALF + t] = 1.0
    return p


def _rope_tables(positions):
    freqs = jnp.power(ROPE_THETA, -jnp.arange(HALF, dtype=F32) / HALF)
    ang = positions.astype(F32)[:, None] * jnp.tile(freqs, HEADS)[None, :]
    cos4 = jnp.cos(ang)
    sin4 = jnp.sin(ang)
    lane = jnp.arange(HEADS * HALF)[None, :]
    csk = jnp.where(lane < ROPE, cos4, 0.0)
    snk = jnp.where(lane < HALF, -sin4, jnp.where(lane < ROPE, sin4, 0.0))
    return cos4, sin4, csk, snk


def _local_step(x, rope, target, mod, g_mix, w_in_p, g_q, g_kv, w_uq_p, w_uk, w_uv, w_pool, pool_scale, g_ffn,
                g_final, late, ffn_grads_exchange):
    perm = jnp.asarray(_rope_perm().transpose(1, 0, 2).reshape(2 * 128, HEADS * 128), BF)
    perm_t = jnp.asarray(_rope_perm().transpose(0, 2, 1).reshape(HEADS * 128, 2 * 128), BF)
    cos4, sin4, csk, snk = rope
    wuk_dc = w_uk.transpose(1, 2, 0).astype(BF)
    wuk_cd = w_uk.transpose(1, 0, 2).astype(BF)
    wuv_vc = w_uv.transpose(1, 2, 0).astype(BF)
    wpool = w_pool.astype(BF)
    wpool_dc = w_pool.transpose(0, 2, 1).astype(BF)

    h1, raw, qn, qs, kv, kvt, pooled, ypre, ypool = _fwd_in(
        x, mod, g_mix, w_in_p, g_q, g_kv, w_uq_p, wuk_dc, perm, cos4, sin4, csk, snk, wpool, pool_scale)
    olat, ymla, lse = _attn_fwd(qs, kv, kvt, wuv_vc)
    w_o, wg_t, wu_t, wd = late
    x2, mix, h2_t, a, b, dx3, dff, dff_t, loss, dgfin, dgt2 = _ffn_fwd(
        x, ymla, ypool, mod, w_o, g_ffn, wg_t, wu_t, wd, g_final, target)
    da, db, dh2 = _ffn_bwd_acts(dff, a, b, wg_t, wu_t, wd)
    (dx2, du, dolat, delta, dwo, dwuv, dwpool, dpscale, dgt1, dsc2, dsh2, dgffn) = _mix_bwd(
        dh2, dx3, x2, mix, mod, g_ffn, ymla, ypool, w_o, ypre, pooled, pool_scale, wpool_dc, olat, wuv_vc)
    dwg_t, dwu_t, dwd = _ffn_bwd_weights(dff_t, h2_t, da, db, a, b)
    ffn_parts = ffn_grads_exchange((dwg_t, dwu_t, dwd, dwo))
    dkv, dqt = _attn_bwd(qs, kv, dolat, lse, delta)
    dx, dwin, dwuq, dwuk, dgq, dgkv, dsc1, dsh1, dgmix = _in_bwd(
        dqt, dkv, du, raw, qn, h1, x, dx2, mod, g_mix, w_in_p, g_q, g_kv, w_uq_p, wuk_cd, perm_t, cos4, sin4, csk,
        snk)
    dmod = jnp.concatenate([dsh1, dsc1, dgt1, dsh2, dsc2, dgt2], axis=1)
    replicated = dict(
        w_uk=dwuk.transpose(1, 0, 2), w_uv=dwuv.transpose(1, 0, 2), w_pool=dwpool, g_mix=dgmix, g_q=dgq, g_kv=dgkv,
        pool_scale=dpscale, g_ffn=dgffn, g_final=dgfin)
    return loss[0, 0], dx, dmod, (dwin, dwuq), ffn_parts, replicated


def _my_pos():
    return lax.axis_index("x"), lax.axis_index("y"), lax.axis_index("c")


def _peer(pos, k):
    x, y, c = pos
    return (1 - x if k & 4 else x, 1 - y if k & 2 else y, 1 - c if k & 1 else c)


def _index(pos):
    x, y, c = pos
    return 4 * x + 2 * y + c


def _remote(src, dst, send_sem, recv_sem, to):
    return pltpu.make_async_remote_copy(src_ref=src, dst_ref=dst, send_sem=send_sem, recv_sem=recv_sem,
                                        device_id=to, device_id_type=MESH)


def _ada_mod(c, w_ada, b_ada, after):
    def body(c_ref, w_ref, b_ref, after_ref, mod_ref, call_ref, cbuf, sbuf, rbuf, send1, recv1, send2, recv2):
        me = _my_pos()
        mi = _index(me)
        cv = c_ref[...]
        cbuf[...] = jnp.broadcast_to(cv * jax.nn.sigmoid(cv), (8, D))
        call_ref[mi] = cbuf[...]
        first = [_remote(cbuf, call_ref.at[mi], send1.at[k - 1], recv1.at[k - 1], _peer(me, k)) for k in range(1, NDEV)]
        for cp in first:
            cp.start()
        for k in range(1, NDEV):
            _remote(cbuf, call_ref.at[_index(_peer(me, k))], send1.at[k - 1], recv1.at[k - 1], _peer(me, k)).wait_recv()
        c_all = jnp.concatenate([call_ref[b][0:1, :] for b in range(NDEV)], axis=0)
        blocks = _dot(c_all.astype(BF), w_ref[...].astype(BF))
        for b in range(NDEV):
            sbuf[b] = jnp.broadcast_to(blocks[b:b + 1, :], (8, MODC))
        second = []
        for k in range(1, NDEV):
            to = _peer(me, k)
            second.append(_remote(sbuf.at[_index(to)], rbuf.at[mi], send2.at[k - 1], recv2.at[k - 1], to))
        for cp in second:
            cp.start()
        rbuf[mi] = sbuf[mi]
        for k in range(1, NDEV):
            to = _peer(me, k)
            _remote(sbuf.at[_index(to)], rbuf.at[_index(to)], send2.at[k - 1], recv2.at[k - 1], to).wait_recv()
        for j in range(NDEV):
            mod_ref[:, j * MODC:(j + 1) * MODC] = rbuf[j] + b_ref[:, j * MODC:(j + 1) * MODC]
        for cp in first + second:
            cp.wait_send()

    return pl.pallas_call(
        body, name="ada_mod",
        out_shape=(jax.ShapeDtypeStruct((8, N_MOD * D), F32), jax.ShapeDtypeStruct((NDEV, 8, D), F32)),
        in_specs=[_vmem(), _vmem(), _vmem(), _any()], out_specs=(_vmem(), _vmem()),
        scratch_shapes=[pltpu.VMEM((8, D), F32), pltpu.VMEM((NDEV, 8, MODC), F32), pltpu.VMEM((NDEV, 8, MODC), F32),
                        pltpu.SemaphoreType.DMA((NDEV - 1,)), pltpu.SemaphoreType.DMA((NDEV - 1,)),
                        pltpu.SemaphoreType.DMA((NDEV - 1,)), pltpu.SemaphoreType.DMA((NDEV - 1,))],
        compiler_params=_params(),
    )(c, w_ada, b_ada, after)


def _sequencer_scatter(name, collective_id, srcs, after=()):
    n = len(srcs)

    def of(src, to_index):
        r = src.shape[0] // NDEV
        return src.at[pl.ds(pl.multiple_of(to_index * r, 16), r), :]

    def body(*refs):
        src, zone = refs[:n], refs[n + len(after):2 * n + len(after)]
        send, recv, local = refs[2 * n + len(after):]
        me = _my_pos()
        mi = _index(me)
        barrier = pltpu.get_barrier_semaphore()
        for k in range(1, NDEV):
            pl.semaphore_signal(barrier, inc=1, device_id=_peer(me, k), device_id_type=MESH)
        pl.semaphore_wait(barrier, NDEV - 1)
        own = [pltpu.make_async_copy(of(src[a], mi), zone[a].at[mi], local.at[a]) for a in range(n)]
        for cp in own:
            cp.start()
        for a in range(n):
            for k in range(1, NDEV):
                to = _peer(me, k)
                s = a * (NDEV - 1) + k - 1
                _remote(of(src[a], _index(to)), zone[a].at[mi], send.at[s], recv.at[s], to).start()
        for cp in own:
            cp.wait()
        for a in range(n):
            for k in range(1, NDEV):
                to = _peer(me, k)
                s = a * (NDEV - 1) + k - 1
                cp = _remote(of(src[a], mi), zone[a].at[_index(to)], send.at[s], recv.at[s], to)
                cp.wait_send()
                cp.wait_recv()

    return pl.kernel(
        body, name=name, mesh=plsc.ScalarSubcoreMesh(axis_name="sequencer", num_cores=1),
        out_type=tuple(jax.ShapeDtypeStruct((NDEV, s.shape[0] // NDEV, s.shape[1]), s.dtype) for s in srcs),
        scratch_types=[pltpu.SemaphoreType.DMA((n * (NDEV - 1),)), pltpu.SemaphoreType.DMA((n * (NDEV - 1),)),
                       pltpu.SemaphoreType.DMA((n,))],
        compiler_params=pltpu.CompilerParams(collective_id=collective_id),
    )(*srcs, *after)


CHIP_PEERS = (2, 4, 6)


def _sequencer_gather(name, collective_id, srcs, after=()):
    n = len(srcs)
    per = NDEV - 1

    def body(*refs):
        src, zone = refs[:n], refs[n + len(after):2 * n + len(after)]
        send, recv, local = refs[2 * n + len(after):]
        me = _my_pos()
        mi = _index(me)
        sibling = _peer(me, 1)
        talk_to = (sibling,) + tuple(_peer(me, k) for k in CHIP_PEERS)
        barrier = pltpu.get_barrier_semaphore()
        for to in talk_to:
            pl.semaphore_signal(barrier, inc=1, device_id=to, device_id_type=MESH)
        pl.semaphore_wait(barrier, len(talk_to))

        def copy(a, slot, block_of, to, from_src=False):
            rows = zone[a].at[_index(block_of)]
            return _remote(src[a] if from_src else rows, rows, send.at[a * per + slot], recv.at[a * per + slot], to)

        own = [pltpu.make_async_copy(src[a], zone[a].at[mi], local.at[a]) for a in range(n)]
        for cp in own:
            cp.start()
        started = []
        for a in range(n):
            started.append(copy(a, 0, me, sibling, from_src=True))
            started += [copy(a, 1 + j, me, _peer(me, k), from_src=True) for j, k in enumerate(CHIP_PEERS)]
        for cp in started:
            cp.start()
        for a in range(n):
            for j, k in enumerate(CHIP_PEERS):
                copy(a, 1 + j, _peer(me, k), me).wait_recv()
                passed = copy(a, 4 + j, _peer(me, k), sibling)
                passed.start()
                started.append(passed)
        for a in range(n):
            copy(a, 0, sibling, me).wait_recv()
            for j, k in enumerate(CHIP_PEERS):
                copy(a, 4 + j, _peer(me, k | 1), me).wait_recv()
        for cp in started:
            cp.wait_send()
        for cp in own:
            cp.wait()

    return pl.kernel(
        body, name=name, mesh=plsc.ScalarSubcoreMesh(axis_name="sequencer", num_cores=1),
        out_type=tuple(jax.ShapeDtypeStruct((NDEV,) + s.shape, s.dtype) for s in srcs),
        scratch_types=[pltpu.SemaphoreType.DMA((n * per,)), pltpu.SemaphoreType.DMA((n * per,)),
                       pltpu.SemaphoreType.DMA((n,))],
        compiler_params=pltpu.CompilerParams(collective_id=collective_id),
    )(*srcs, *after)


def _blocked(shape, nb, axis=0):
    block = tuple(s // nb if d == axis else s for d, s in enumerate(shape))
    return pl.BlockSpec(block, lambda i: tuple(i if d == axis else 0 for d in range(len(shape))))


def _sum_partials(name, parts, nb):
    n = len(parts)

    def body(*refs):
        for a in range(n):
            acc = refs[a][0].astype(F32)
            for p in range(1, NDEV):
                acc = acc + refs[a][p].astype(F32)
            refs[n + a][...] = acc

    return pl.pallas_call(
        body, name=name, grid=(nb,),
        out_shape=tuple(jax.ShapeDtypeStruct(p.shape[1:], F32) for p in parts),
        in_specs=[_blocked(p.shape, nb, 1) for p in parts],
        out_specs=tuple(_blocked(p.shape[1:], nb) for p in parts), compiler_params=_params(("arbitrary",)),
    )(*parts)


def _small_all_reduce(buf):
    def body(buf_ref, got_ref, red_ref, mine, send1, recv1, send2, recv2):
        me = _my_pos()
        mi = _index(me)
        first = []
        for k in range(1, NDEV):
            to = _peer(me, k)
            first.append(_remote(buf_ref.at[_index(to)], got_ref.at[mi], send1.at[k - 1], recv1.at[k - 1], to))
        for cp in first:
            cp.start()
        got_ref[mi] = buf_ref[mi]
        for k in range(1, NDEV):
            to = _peer(me, k)
            _remote(buf_ref.at[mi], got_ref.at[_index(to)], send1.at[k - 1], recv1.at[k - 1], to).wait_recv()
        acc = got_ref[0]
        for p in range(1, NDEV):
            acc = acc + got_ref[p]
        mine[...] = acc
        second = [_remote(mine, red_ref.at[mi], send2.at[k - 1], recv2.at[k - 1], _peer(me, k)) for k in range(1, NDEV)]
        for cp in second:
            cp.start()
        red_ref[mi] = acc
        for k in range(1, NDEV):
            to = _peer(me, k)
            _remote(mine, red_ref.at[_index(to)], send2.at[k - 1], recv2.at[k - 1], to).wait_recv()
        for cp in first + second:
            cp.wait_send()

    return pl.pallas_call(
        body, name="small_all_reduce",
        out_shape=(jax.ShapeDtypeStruct(buf.shape, F32), jax.ShapeDtypeStruct(buf.shape, F32)),
        in_specs=[_vmem()], out_specs=(_vmem(), _vmem()),
        scratch_shapes=[pltpu.VMEM(buf.shape[1:], F32),
                        pltpu.SemaphoreType.DMA((NDEV - 1,)), pltpu.SemaphoreType.DMA((NDEV - 1,)),
                        pltpu.SemaphoreType.DMA((NDEV - 1,)), pltpu.SemaphoreType.DMA((NDEV - 1,))],
        compiler_params=_params(),
    )(buf)


def _adamw_math(w, g, m, v):
    m = ADAM_B1 * m + (1.0 - ADAM_B1) * g
    v = ADAM_B2 * v + (1.0 - ADAM_B2) * jnp.square(g)
    m_hat = m / (1.0 - ADAM_B1 ** ADAM_STEP)
    v_hat = v / (1.0 - ADAM_B2 ** ADAM_STEP)
    delta = -ADAM_LR * (m_hat / (jnp.sqrt(v_hat) + ADAM_EPS) + ADAM_WD * w)
    return delta, m, v


def _adamw_group(name, ws, gs, ms, vs, nb):
    n = len(ws)

    def body(*refs):
        for a in range(n):
            w, g, m, v = (refs[q * n + a][...] for q in range(4))
            delta, m2, v2 = _adamw_math(w, g, m, v)
            refs[4 * n + a][...] = delta
            refs[5 * n + a][...] = m2
            refs[6 * n + a][...] = v2

    shapes = tuple(jax.ShapeDtypeStruct(w.shape, F32) for w in ws)
    specs = [_blocked(w.shape, nb) for w in ws]
    outs = pl.pallas_call(
        body, name=name, grid=(nb,), out_shape=shapes * 3, in_specs=specs * 4, out_specs=tuple(specs * 3),
        compiler_params=_params(("arbitrary",)),
    )(*ws, *gs, *ms, *vs)
    return outs[:n], outs[n:2 * n], outs[2 * n:]


def _adamw_from_partials(name, ws, parts, ms, vs, nb):
    n = len(ws)

    def body(*refs):
        for a in range(n):
            part = refs[n + a]
            g = part[0].astype(F32)
            for p in range(1, NDEV):
                g = g + part[p].astype(F32)
            delta, m2, v2 = _adamw_math(refs[a][...], g, refs[2 * n + a][...], refs[3 * n + a][...])
            refs[4 * n + a][...] = g
            refs[5 * n + a][...] = delta
            refs[6 * n + a][...] = m2
            refs[7 * n + a][...] = v2

    shapes = tuple(jax.ShapeDtypeStruct(w.shape, F32) for w in ws)
    specs = [_blocked(w.shape, nb) for w in ws]
    outs = pl.pallas_call(
        body, name=name, grid=(nb,), out_shape=shapes * 4,
        in_specs=specs + [_blocked(p.shape, nb, 1) for p in parts] + specs * 2, out_specs=tuple(specs * 4),
        compiler_params=_params(("arbitrary",)),
    )(*ws, *parts, *ms, *vs)
    return outs[:n], outs[n:2 * n], outs[2 * n:3 * n], outs[3 * n:]


def _adamw_ada(w, m, v, c_all_t, dmod_rows):
    nb = 4

    def body(w_ref, m_ref, v_ref, c_ref, dm_ref, g_ref, d_ref, m2_ref, v2_ref):
        g = _dot(c_ref[...], dm_ref[...].astype(BF))
        g_ref[...] = g
        delta, m2, v2 = _adamw_math(w_ref[...], g, m_ref[...], v_ref[...])
        d_ref[...] = delta
        m2_ref[...] = m2
        v2_ref[...] = v2

    shp = jax.ShapeDtypeStruct(w.shape, F32)
    spec = _blocked(w.shape, nb)
    return pl.pallas_call(
        body, name="adamw_ada", grid=(nb,), out_shape=(shp, shp, shp, shp),
        in_specs=[spec, spec, spec, _blocked(c_all_t.shape, nb), _full(dmod_rows.shape)],
        out_specs=(spec, spec, spec, spec), compiler_params=_params(("arbitrary",)),
    )(w, m, v, c_all_t, dmod_rows)


def _w_in_to_kernel(w):
    return jnp.concatenate([w[:, 0:448], jnp.zeros((w.shape[0], 64), w.dtype), w[:, 448:960]], axis=1)


def _w_in_from_kernel(w):
    return jnp.concatenate([w[:, 0:448], w[:, 512:1024]], axis=1)


def _w_uq_to_kernel(w):
    r = w.shape[0]
    return jnp.concatenate([w[:, :, 0:NOPE].reshape(r, HEADS * NOPE),
                            w[:, :, NOPE:NOPE + HALF].reshape(r, HEADS * HALF),
                            w[:, :, NOPE + HALF:].reshape(r, HEADS * HALF)], axis=1)


def _w_uq_from_kernel(w):
    r = w.shape[0]
    return jnp.concatenate([w[:, 0:512].reshape(r, HEADS, NOPE), w[:, 512:640].reshape(r, HEADS, HALF),
                            w[:, 640:768].reshape(r, HEADS, HALF)], axis=2)


REP_NAMES = ("w_uk", "w_uv", "w_pool", "g_mix", "g_q", "g_kv", "pool_scale", "g_ffn", "g_final")


def kernel(x, c, positions, w_ada, b_ada, g_mix, w_in, g_q, g_kv, w_uq, w_uk, w_uv, w_pool, pool_scale, w_o, g_ffn, w_gate, w_up, w_down, g_final, loss_target, m_w_ada, m_b_ada, m_g_mix, m_w_in, m_g_q, m_g_kv, m_w_uq, m_w_uk, m_w_uv, m_w_pool, m_pool_scale, m_w_o, m_g_ffn, m_w_gate, m_w_up, m_w_down, m_g_final, v_w_ada, v_b_ada, v_g_mix, v_w_in, v_g_q, v_g_kv, v_w_uq, v_w_uk, v_w_uv, v_w_pool, v_pool_scale, v_w_o, v_g_ffn, v_w_gate, v_w_up, v_w_down, v_g_final):
    given = dict(locals())

    merge = lambda g: g.reshape(NDEV * g.shape[1], g.shape[2])
    w_in_p, w_uq_p = (merge(g) for g in _sequencer_gather(
        "gather_in", 3, (_w_in_to_kernel(w_in[0]).astype(BF), _w_uq_to_kernel(w_uq[0]).astype(BF))))

    rope = _rope_tables(positions[0])
    mod, c_all8 = _ada_mod(c, w_ada[0], b_ada, rope[3][0:8, :])
    c_all = c_all8[:, 0, :]
    late = _sequencer_gather(
        "gather_late", 1, (w_o[0].astype(BF), w_gate[0].T.astype(BF), w_up[0].T.astype(BF), w_down[0].astype(BF)),
        after=(mod[:, 0:128], w_in_p[0:16, 0:128], w_uq_p[0:16, 0:128]))

    def ffn_grads_exchange(arrays):
        return _sequencer_scatter("scatter_ffn", 2, arrays)

    loss, dx, dmod, tail_grads, ffn_parts, replicated = _local_step(
        x[0], rope, loss_target[0], mod, g_mix, w_in_p, g_q, g_kv, w_uq_p, w_uk[0], w_uv[0], w_pool[0],
        pool_scale, g_ffn, g_final.reshape(1, D), tuple(merge(g) for g in late), ffn_grads_exchange)

    flat = jnp.concatenate([replicated[k].reshape(-1) for k in REP_NAMES] + [loss.reshape(1)])
    flat = jnp.pad(flat, (0, NDEV * REP_ROWS * 128 - flat.shape[0])).reshape(NDEV, REP_ROWS, 128)
    dmod_blocks = jnp.pad(dmod.reshape(NDEV, MODC // 128, 128), ((0, 0), (0, MOD_ROWS - MODC // 128), (0, 0)))
    got, red = _small_all_reduce(jnp.concatenate([dmod_blocks, flat], axis=1))

    tail_parts = _sequencer_scatter("scatter_tail", 4, tail_grads,
                                    after=(ffn_parts[0][0, 0:16, 0:128], red[0, 0:8, :]))
    g_in_p, g_uq_p = _sum_partials("sum_tail_partials", tail_parts, 1)
    as_transpose = ("w_in", "w_gate", "w_up")
    grads = dict(w_in=_w_in_from_kernel(g_in_p).T, w_uq=_w_uq_from_kernel(g_uq_p))
    partials = dict(w_gate=ffn_parts[0], w_up=ffn_parts[1], w_down=ffn_parts[2], w_o=ffn_parts[3])
    dmod_rows = got[:, 0:MODC // 128, :].reshape(NDEV, MODC)
    grads["b_ada"] = red[:, 0:MODC // 128, :].reshape(1, N_MOD * D)
    rep_flat = red[:, MOD_ROWS:, :].reshape(-1)
    off = 0
    for k in REP_NAMES:
        size = int(np.prod(given[k].shape))
        grads[k] = rep_flat[off:off + size]
        off += size

    view = {k: (given[k].shape[1:] if given[k].ndim > 2 else given[k].shape)
            for k in REP_NAMES + ("b_ada", "w_ada", "w_in", "w_uq", "w_o", "w_gate", "w_up", "w_down")}
    view.update(g_final=(1, D))
    names = ["w_ada", "b_ada", "g_mix", "w_in", "g_q", "g_kv", "w_uq", "w_uk", "w_uv", "w_pool", "pool_scale",
             "w_o", "g_ffn", "w_gate", "w_up", "w_down", "g_final"]
    g_ada, d_ada, m_ada, v_ada = _adamw_ada(w_ada[0], m_w_ada[0], v_w_ada[0], c_all.T.astype(BF), dmod_rows)
    out_g, out_d, out_m, out_v = dict(w_ada=g_ada), dict(w_ada=d_ada), dict(w_ada=m_ada), dict(w_ada=v_ada)
    groups = (("adamw_ffn", ("w_gate", "w_up", "w_down", "w_o"), 4),
              ("adamw_replicated", REP_NAMES + ("b_ada",), 1),
              ("adamw_tail", ("w_in", "w_uq"), 1))
    for gname, members, nb in groups:
        turn = lambda k, t: t.T if k in as_transpose else t
        ws = [turn(k, given[k].reshape(view[k])) for k in members]
        ms = [turn(k, given["m_" + k].reshape(view[k])) for k in members]
        vs = [turn(k, given["v_" + k].reshape(view[k])) for k in members]
        if members[0] in partials:
            gs, ds, m2, v2 = _adamw_from_partials(gname, ws, [partials[k] for k in members], ms, vs, nb)
        else:
            gs = [grads[k] if k in as_transpose else grads[k].reshape(view[k]) for k in members]
            ds, m2, v2 = _adamw_group(gname, ws, gs, ms, vs, nb)
        for k, g, d, mm, vv in zip(members, gs, ds, m2, v2):
            out_g[k], out_d[k], out_m[k], out_v[k] = turn(k, g), turn(k, d), turn(k, mm), turn(k, vv)

    total = rep_flat[off]
    shaped = lambda d: [d[k].reshape(given[k].shape) for k in names]
    return (total, dx[None], *shaped(out_g), *shaped(out_d), *shaped(out_m), *shaped(out_v))
```

```python
import numpy as np
import jax
import jax.numpy as jnp
from jax import lax
from jax.experimental import pallas as pl
from jax.experimental.pallas import tpu as pltpu
from jax.experimental.pallas import tpu_sc as plsc

D = 1024
HEADS = 4
NOPE = 128
ROPE = 64
HALF = ROPE // 2
QL = 256
KVL = 128
FF = 2816
PW = 512
GROUPS = 4
GD = 128
N_MOD = 6
EPS = 1e-6
SM_SCALE = (NOPE + ROPE) ** -0.5
LOG2_E = 1.4426950408889634
EXP2_SCALE = SM_SCALE * LOG2_E
ROPE_THETA = 10000.0
NDEV = 8
MODC = N_MOD * D // NDEV

ADAM_LR = 0.001
ADAM_B1 = 0.9
ADAM_B2 = 0.999
ADAM_EPS = 1e-08
ADAM_WD = 0.01
ADAM_STEP = 10

BF = jnp.bfloat16
F32 = jnp.float32
VMEM_LIMIT_V7X = 60 * 1024 * 1024
MESH = pl.DeviceIdType.MESH

TQ = 512
TK = 256
QW = 256
VPU_ROWS = 16
MOD_ROWS = 8
REP_ROWS = 200
SMALL_ROWS = MOD_ROWS + REP_ROWS


def _params(sem=None):
    return pltpu.CompilerParams(dimension_semantics=sem, vmem_limit_bytes=VMEM_LIMIT_V7X)


def _dot(a, b):
    return jnp.dot(a, b, preferred_element_type=F32)


def _dot_nt(a, b):
    return lax.dot_general(a, b, (((1,), (1,)), ((), ())), preferred_element_type=F32)


def _dot_tn(a, b):
    return _dot(a.astype(F32).T.astype(BF), b)


def _full(shape):
    return pl.BlockSpec(shape, lambda *_: (0,) * len(shape))


def _rows(ts, cols):
    return pl.BlockSpec((ts, cols), lambda i: (i, 0))


def _vmem():
    return pl.BlockSpec(memory_space=pltpu.VMEM)


def _any():
    return pl.BlockSpec(memory_space=pl.ANY)


def _rms(v):
    return lax.rsqrt(jnp.mean(v * v, axis=-1, keepdims=True) + EPS)


def _rms_bwd(dn, n, r):
    return r * (dn - n * jnp.mean(dn * n, axis=-1, keepdims=True))


def _colsum(v):
    return jnp.sum(v, axis=0, keepdims=True)


def _swap_halves(v):
    lane = lax.broadcasted_iota(jnp.int32, v.shape, 1)
    return jnp.where(lane < HALF, pltpu.roll(v, 128 - HALF, 1), pltpu.roll(v, HALF, 1))


def _window_lane_width():
    lane = lax.broadcasted_iota(jnp.int32, (1, PW), 1)
    return jnp.where(lane < 128, 2.0, jnp.where(lane < 256, 4.0, jnp.where(lane < 384, 8.0, 16.0))).astype(F32)


def _window_sums(ext, back):
    n = ext.shape[0]

    def sh(v, k):
        return pltpu.roll(v, k if back else n - k, 0)

    s2 = ext + sh(ext, 1)
    e4 = s2[:, 128:]
    s4 = e4 + sh(e4, 2)
    e8 = s4[:, 128:]
    s8 = e8 + sh(e8, 4)
    e16 = s8[:, 128:]
    s16 = e16 + sh(e16, 8)
    return jnp.concatenate([s2[:, :128], s4[:, :128], s8[:, :128], s16], axis=1)


def _fill_block_diagonal(dst_ref, blocks_ref):
    n, r, c = blocks_ref.shape
    dst_ref[...] = jnp.zeros_like(dst_ref)
    for b in range(n):
        dst_ref[b * r:(b + 1) * r, b * c:(b + 1) * c] = blocks_ref[b]


def _row_counts(first_row, ts):
    t1 = (first_row + lax.broadcasted_iota(jnp.int32, (ts, 1), 0) + 1).astype(F32)
    return jnp.minimum(t1, _window_lane_width())


def _fwd_in(x, mod, g_mix, w_in, g_q, g_kv, w_uq, wuk_dc, perm, cos4, sin4, csk, snk, w_pool, pool_scale):
    S = x.shape[0]
    ts = 512
    nsub = ts // TQ

    def body(x_ref, mod_ref, gmix_ref, win_ref, gq_ref, gkv_ref, wuq_ref, wuk_ref, perm_ref, cos_ref, sin_ref,
             csk_ref, snk_ref, wpool_ref, pscale_ref,
             h1_ref, raw_ref, qn_ref, qs_ref, kv_ref, kvt_ref, pooled_ref, ypre_ref, ypool_ref, carry_ref, wuk_bd,
             wpool_bd):
        i = pl.program_id(0)

        @pl.when(i == 0)
        def _():
            carry_ref[...] = jnp.zeros_like(carry_ref)
            _fill_block_diagonal(wuk_bd, wuk_ref)
            _fill_block_diagonal(wpool_bd, wpool_ref)

        xv = x_ref[...]
        sh1 = mod_ref[0:1, 0:D]
        sc1 = mod_ref[0:1, D:2 * D]
        h = (xv * _rms(xv)) * gmix_ref[...] * (1.0 + sc1) + sh1
        hb = h.astype(BF)
        h1_ref[...] = hb
        proj = _dot(hb, win_ref[...])
        cq_raw = proj[:, 0:QL]
        ckv_raw = proj[:, QL:QL + KVL]
        kr = proj[:, 384:512]
        u = proj[:, 512:1024]
        raw_ref[...] = proj[:, 0:384]

        c_q = (cq_raw * _rms(cq_raw)) * gq_ref[...]
        c_kv = (ckv_raw * _rms(ckv_raw)) * gkv_ref[...]
        q = _dot(c_q.astype(BF), wuq_ref[...])
        qn = q[:, 0:HEADS * NOPE].astype(BF)
        qn_ref[...] = qn
        x1 = q[:, 512:640]
        x2 = q[:, 640:768]
        cosv = cos_ref[...]
        sinv = sin_ref[...]
        roped = jnp.concatenate([x1 * cosv - x2 * sinv, x1 * sinv + x2 * cosv], axis=1).astype(BF)
        q_lat = _dot(qn, wuk_bd[...])
        q_rope = _dot(roped, perm_ref[...])
        for hd in range(HEADS):
            cols = slice(hd * 128, (hd + 1) * 128)
            qh = jnp.concatenate([q_lat[:, cols], q_rope[:, cols]], axis=1).astype(BF)
            for a in range(nsub):
                qs_ref[a, hd * TQ:(hd + 1) * TQ, :] = qh[a * TQ:(a + 1) * TQ, :]
        k_rope = kr * csk_ref[...] + _swap_halves(kr) * snk_ref[...]
        keys = jnp.concatenate([c_kv, k_rope], axis=1)
        kv_ref[...] = keys.astype(BF)
        for a in range(ts // TK):
            kvt_ref[a] = keys[a * TK:(a + 1) * TK, :].T.astype(BF)

        ext = jnp.concatenate([carry_ref[...], u], axis=0)
        win = _window_sums(ext, True)[16:, :]
        pooled = (win / _row_counts(i * ts, ts) - u).astype(BF)
        pooled_ref[...] = pooled
        carry_ref[...] = u[ts - 16:ts, :]
        ypre = _dot(pooled, wpool_bd[...])
        ypre_ref[...] = ypre
        ypool_ref[...] = (ypre * pscale_ref[...]).astype(BF)

    out_shape = (
        jax.ShapeDtypeStruct((S, D), BF),
        jax.ShapeDtypeStruct((S, 384), F32),
        jax.ShapeDtypeStruct((S, HEADS * NOPE), BF),
        jax.ShapeDtypeStruct((S // TQ, HEADS * TQ, QW), BF),
        jax.ShapeDtypeStruct((S, QW), BF),
        jax.ShapeDtypeStruct((S // TK, QW, TK), BF),
        jax.ShapeDtypeStruct((S, PW), BF),
        jax.ShapeDtypeStruct((S, PW), F32),
        jax.ShapeDtypeStruct((S, PW), BF),
    )
    in_specs = [
        _rows(ts, D), _full(mod.shape), _full((1, D)), _full(w_in.shape), _full((1, QL)), _full((1, KVL)),
        _full(w_uq.shape), _full(wuk_dc.shape), _full(perm.shape), _rows(ts, 128), _rows(ts, 128), _rows(ts, 128),
        _rows(ts, 128), _full(w_pool.shape), _full((1, PW)),
    ]
    out_specs = (
        _rows(ts, D), _rows(ts, 384), _rows(ts, HEADS * NOPE),
        pl.BlockSpec((nsub, HEADS * TQ, QW), lambda i: (i, 0, 0)),
        _rows(ts, QW), pl.BlockSpec((ts // TK, QW, TK), lambda i: (i, 0, 0)), _rows(ts, PW), _rows(ts, PW),
        _rows(ts, PW),
    )
    return pl.pallas_call(
        body, name="fwd_in", out_shape=out_shape, grid=(S // ts,), in_specs=in_specs, out_specs=out_specs,
        scratch_shapes=[pltpu.VMEM((16, PW), F32), pltpu.VMEM((HEADS * NOPE, HEADS * KVL), BF),
                        pltpu.VMEM((PW, PW), BF)],
        compiler_params=_params(("arbitrary",)),
    )(x, mod, g_mix, w_in, g_q, g_kv, w_uq, wuk_dc, perm, cos4, sin4, csk, snk, w_pool, pool_scale)


def _diag_mask(shape, q_axis, first_chunk):
    qi = (lax.broadcasted_iota(jnp.int32, shape, q_axis) & (TQ - 1)) >> 6
    ki = (lax.broadcasted_iota(jnp.int32, shape, 1 - q_axis) >> 6) + first_chunk
    return ki <= qi


def _attn_fwd(qs, kv, kvt, wuv_vc):
    nq = qs.shape[0]
    S = kv.shape[0]
    M = HEADS * TQ

    def body(qs_ref, kv_ref, kvt_ref, wuv_ref, olat_ref, ymla_ref, lse_ref):
        i = pl.program_id(0)
        q = qs_ref[0]

        def step(kt, carry, first_chunk=None):
            m, l, acc = carry
            k = kv_ref[pl.ds(pl.multiple_of(kt * TK, TK), TK), :]
            v_t = kvt_ref[kt][0:KVL, :]
            s = _dot_nt(k, q)
            if first_chunk is not None:
                s = jnp.where(_diag_mask((TK, M), 1, first_chunk), s, -jnp.inf)
            m_new = jnp.maximum(m, jnp.max(s, axis=0, keepdims=True))
            alpha = jnp.exp2((m - m_new) * EXP2_SCALE)
            p = jnp.exp2((s - m_new) * EXP2_SCALE)
            l = alpha * l + jnp.sum(p, axis=0, keepdims=True)
            acc = alpha * acc + _dot(v_t, p.astype(BF))
            return m_new, l, acc

        init = (jnp.full((1, M), -jnp.inf, F32), jnp.zeros((1, M), F32), jnp.zeros((KVL, M), F32))
        per = TQ // TK
        carry = lax.fori_loop(0, per * i, step, init)
        for j in range(per):
            carry = step(per * i + j, carry, j * (TK // 64))
        m, l, acc = carry
        o_lat = acc / l
        olat_ref[0] = o_lat
        lse_ref[0] = jnp.broadcast_to(m * SM_SCALE + jnp.log(l), (8, M))
        for hd in range(HEADS):
            o_t = _dot(wuv_ref[hd], o_lat[:, hd * TQ:(hd + 1) * TQ].astype(BF))
            ymla_ref[:, hd * 128:(hd + 1) * 128] = o_t.T.astype(BF)

    out_shape = (
        jax.ShapeDtypeStruct((nq, KVL, M), F32),
        jax.ShapeDtypeStruct((S, HEADS * 128), BF),
        jax.ShapeDtypeStruct((nq, 8, M), F32),
    )
    return pl.pallas_call(
        body, name="attn_fwd", out_shape=out_shape, grid=(nq,),
        in_specs=[pl.BlockSpec((1, M, QW), lambda i: (i, 0, 0)), _full(kv.shape), _full(kvt.shape),
                  _full(wuv_vc.shape)],
        out_specs=(pl.BlockSpec((1, KVL, M), lambda i: (i, 0, 0)), _rows(TQ, HEADS * 128),
                   pl.BlockSpec((1, 8, M), lambda i: (i, 0, 0))),
        compiler_params=_params(("arbitrary",)),
    )(qs, kv, kvt, wuv_vc)


def _silu_parts(a):
    sg = jax.nn.sigmoid(a)
    return sg, a * sg


def _ffn_fwd(x, ymla, ypool, mod, w_o, g_ffn, wg_t, wu_t, wd, g_final, target):
    S = x.shape[0]
    ts = 256

    def body(x_ref, ymla_ref, ypool_ref, mod_ref, wo_ref, gffn_ref, wg_ref, wu_ref, wd_ref, gfin_ref, t_ref,
             x2_ref, mix_ref, h2t_ref, a_ref, b_ref, dx3_ref, dff_ref, dfft_ref, loss_ref, dgfin_ref, dgt2_ref,
             f_ref):
        i = pl.program_id(0)

        @pl.when(i == 0)
        def _():
            loss_ref[...] = jnp.zeros_like(loss_ref)
            dgfin_ref[...] = jnp.zeros_like(dgfin_ref)
            dgt2_ref[...] = jnp.zeros_like(dgt2_ref)

        gt1 = mod_ref[0:1, 2 * D:3 * D]
        sh2 = mod_ref[0:1, 3 * D:4 * D]
        sc2 = mod_ref[0:1, 4 * D:5 * D]
        gt2 = mod_ref[0:1, 5 * D:6 * D]
        cat = jnp.concatenate([ymla_ref[...], ypool_ref[...]], axis=1)
        mix = _dot(cat, wo_ref[...])
        mix_ref[...] = mix
        x2 = x_ref[...] + gt1 * mix
        x2_ref[...] = x2
        h2 = (x2 * _rms(x2)) * gffn_ref[...] * (1.0 + sc2) + sh2
        h2b = h2.astype(BF)
        h2t_ref[...] = h2.T.astype(BF)

        for c in range(FF // FCHUNK):
            cols = slice(c * FCHUNK, (c + 1) * FCHUNK)
            a = _dot_nt(h2b, wg_ref[cols, :])
            b = _dot_nt(h2b, wu_ref[cols, :])
            a_ref[:, cols] = a.astype(BF)
            b_ref[:, cols] = b.astype(BF)
            f_ref[:, cols] = (_silu_parts(a)[1] * b).astype(BF)
        ff = _dot(f_ref[...], wd_ref[...])

        x3 = x2 + gt2 * ff
        r3 = _rms(x3)
        xn3 = x3 * r3
        gfin = gfin_ref[...]
        e = xn3 * gfin - t_ref[...]
        loss_ref[...] += 0.5 * jnp.sum(jnp.mean(e * e, axis=-1, keepdims=True))
        dy = e * (1.0 / D)
        dgfin_ref[...] += _colsum(dy * xn3)
        dx3 = _rms_bwd(dy * gfin, xn3, r3)
        dx3_ref[...] = dx3
        dgt2_ref[...] += _colsum(dx3 * ff)
        dff = dx3 * gt2
        dff_ref[...] = dff.astype(BF)
        dfft_ref[...] = dff.T.astype(BF)

    row = lambda c: _rows(ts, c)
    col = pl.BlockSpec((D, ts), lambda i: (0, i))
    const = _full
    out_shape = (
        jax.ShapeDtypeStruct((S, D), F32),
        jax.ShapeDtypeStruct((S, D), F32),
        jax.ShapeDtypeStruct((D, S), BF),
        jax.ShapeDtypeStruct((S, FF), BF),
        jax.ShapeDtypeStruct((S, FF), BF),
        jax.ShapeDtypeStruct((S, D), F32),
        jax.ShapeDtypeStruct((S, D), BF),
        jax.ShapeDtypeStruct((D, S), BF),
        jax.ShapeDtypeStruct((8, 128), F32),
        jax.ShapeDtypeStruct((1, D), F32),
        jax.ShapeDtypeStruct((1, D), F32),
    )
    return pl.pallas_call(
        body, name="ffn_fwd", out_shape=out_shape, grid=(S // ts,),
        in_specs=[row(D), row(PW), row(PW), const(mod.shape), _vmem(), const((1, D)), _vmem(), _vmem(), _vmem(),
                  const((1, D)), row(D)],
        out_specs=(row(D), row(D), col, row(FF), row(FF), row(D), row(D), col, const((8, 128)), const((1, D)),
                   const((1, D))),
        scratch_shapes=[pltpu.VMEM((ts, FF), BF)],
        compiler_params=_params(("arbitrary",)),
    )(x, ymla, ypool, mod, w_o, g_ffn, wg_t, wu_t, wd, g_final, target)


FCHUNK = 256


def _ffn_bwd_acts(dff, a, b, wg_t, wu_t, wd):
    S = dff.shape[0]
    ts = 512

    def body(dff_ref, a_ref, b_ref, wg_ref, wu_ref, wd_ref, da_ref, db_ref, dh2_ref):
        dffb = dff_ref[...]
        for c in range(FF // FCHUNK):
            cols = slice(c * FCHUNK, (c + 1) * FCHUNK)
            df = _dot_nt(dffb, wd_ref[cols, :])
            av = a_ref[:, cols].astype(F32)
            bv = b_ref[:, cols].astype(F32)
            sg, sa = _silu_parts(av)
            db_ref[:, cols] = (df * sa).astype(BF)
            da_ref[:, cols] = (df * bv * (sg * (1.0 + av * (1.0 - sg)))).astype(BF)
        dh2_ref[...] = _dot(da_ref[...], wg_ref[...]) + _dot(db_ref[...], wu_ref[...])

    act = _rows(ts, FF)
    return pl.pallas_call(
        body, name="ffn_bwd_acts",
        out_shape=(jax.ShapeDtypeStruct((S, FF), BF), jax.ShapeDtypeStruct((S, FF), BF),
                   jax.ShapeDtypeStruct((S, D), F32)),
        grid=(S // ts,), in_specs=[_rows(ts, D), act, act, _vmem(), _vmem(), _vmem()],
        out_specs=(act, act, _rows(ts, D)), compiler_params=_params(("arbitrary",)),
    )(dff, a, b, wg_t, wu_t, wd)


def _ffn_bwd_weights(dff_t, h2_t, da, db, a, b):
    S = da.shape[0]

    def body(dfft_ref, h2t_ref, da_ref, db_ref, a_ref, b_ref, dwg_ref, dwu_ref, dwd_ref):
        h2t = h2t_ref[...]
        dwg_ref[...] = _dot(h2t, da_ref[...]).T.astype(BF)
        dwu_ref[...] = _dot(h2t, db_ref[...]).T.astype(BF)
        f = (_silu_parts(a_ref[...].astype(F32))[1] * b_ref[...].astype(F32)).astype(BF)
        dwd_ref[...] = _dot(dfft_ref[...], f).T.astype(BF)

    act = pl.BlockSpec((S, FCHUNK), lambda j: (0, j))
    wblk = _rows(FCHUNK, D)
    shp = jax.ShapeDtypeStruct((FF, D), BF)
    return pl.pallas_call(
        body, name="ffn_bwd_weights", out_shape=(shp, shp, shp), grid=(FF // FCHUNK,),
        in_specs=[_vmem(), _vmem(), act, act, act, act], out_specs=(wblk, wblk, wblk),
        compiler_params=_params(("arbitrary",)),
    )(dff_t, h2_t, da, db, a, b)


def _mix_bwd(dh2, dx3, x2, mix, mod, g_ffn, ymla, ypool, w_o, ypre, pooled, pool_scale, wpool_dc, olat, wuv_vc):
    S = dh2.shape[0]
    ts = 512
    n = S // ts
    nsub = ts // TQ
    M = HEADS * TQ

    def body(dh2_ref, dx3_ref, x2_ref, mix_ref, mod_ref, gffn_ref, ymla_ref, ypool_ref, wo_ref, ypre_ref, pooled_ref,
             pscale_ref, wpool_ref, olat_ref, wuv_ref,
             dx2_ref, du_ref, dolat_ref, delta_ref, dwo_ref, dwuv_ref, dwpool_ref, dpscale_ref, dgt1_ref, dsc2_ref,
             dsh2_ref, dgffn_ref, carry_ref, dwo_acc, dwpool_acc, wpool_bd, wuv_bd):
        i = pl.program_id(0)

        @pl.when(i == 0)
        def _():
            carry_ref[...] = jnp.zeros_like(carry_ref)
            dwo_acc[...] = jnp.zeros_like(dwo_acc)
            dwpool_acc[...] = jnp.zeros_like(dwpool_acc)
            _fill_block_diagonal(wpool_bd, wpool_ref)
            _fill_block_diagonal(wuv_bd, wuv_ref)
            for r in (dwuv_ref, dpscale_ref, dgt1_ref, dsc2_ref, dsh2_ref, dgffn_ref):
                r[...] = jnp.zeros_like(r)

        gt1 = mod_ref[0:1, 2 * D:3 * D]
        sc2 = mod_ref[0:1, 4 * D:5 * D]
        gffn = gffn_ref[...]
        dh2 = dh2_ref[...]
        x2 = x2_ref[...]
        r2 = _rms(x2)
        xn2 = x2 * r2
        along = _colsum(dh2 * xn2)
        dsc2_ref[...] += along * gffn
        dsh2_ref[...] += _colsum(dh2)
        dgffn_ref[...] += along * (1.0 + sc2)
        dx2 = dx3_ref[...] + _rms_bwd(dh2 * (gffn * (1.0 + sc2)), xn2, r2)
        dx2_ref[...] = dx2
        dgt1_ref[...] += _colsum(dx2 * mix_ref[...])
        dmix = (dx2 * gt1).astype(BF)
        cat = jnp.concatenate([ymla_ref[...], ypool_ref[...]], axis=1)
        dwo_acc[...] += _dot_tn(cat, dmix)
        dcat = _dot_nt(dmix, wo_ref[...])
        dymla = dcat[:, 0:512]
        dypool = dcat[:, 512:1024]

        dpscale_ref[...] += _colsum(dypool * ypre_ref[...])
        dypre = (dypool * pscale_ref[...]).astype(BF)
        dwpool_acc[...] += _dot_tn(pooled_ref[...], dypre)
        dpooled = _dot(dypre, wpool_bd[...])
        tile = n - 1 - i
        e = dpooled / _row_counts(tile * ts, ts)
        ext = jnp.concatenate([e, carry_ref[...]], axis=0)
        du_ref[...] = _window_sums(ext, False)[0:ts, :] - dpooled
        carry_ref[...] = e[0:16, :]

        dob_all = dymla.astype(BF)
        dol_all = _dot(dob_all, wuv_bd[...])
        for hd in range(HEADS):
            dob = dob_all[:, hd * 128:(hd + 1) * 128]
            dol = dol_all[:, hd * 128:(hd + 1) * 128]
            for a in range(nsub):
                ol_t = olat_ref[a, :, hd * TQ:(hd + 1) * TQ]
                dl = dol[a * TQ:(a + 1) * TQ, :]
                dolat_ref[a, hd * TQ:(hd + 1) * TQ, :] = dl.astype(BF)
                dwuv_ref[hd] += _dot(ol_t.astype(BF), dob[a * TQ:(a + 1) * TQ, :])
                delta = jnp.sum(dl * ol_t.T, axis=-1, keepdims=True)
                delta_ref[a, :, hd * TQ:(hd + 1) * TQ] = jnp.broadcast_to(delta, (TQ, 128)).T[0:8, :]

        @pl.when(i == n - 1)
        def _():
            dwo_ref[...] = dwo_acc[...].astype(BF)
            for g in range(GROUPS):
                dwpool_ref[g] = dwpool_acc[g * GD:(g + 1) * GD, g * GD:(g + 1) * GD]

    rev = lambda c: pl.BlockSpec((ts, c), lambda i: (n - 1 - i, 0))
    rev3 = lambda r, c: pl.BlockSpec((nsub, r, c), lambda i: (n - 1 - i, 0, 0))
    out_shape = (
        jax.ShapeDtypeStruct((S, D), F32),
        jax.ShapeDtypeStruct((S, PW), F32),
        jax.ShapeDtypeStruct((S // TQ, M, KVL), BF),
        jax.ShapeDtypeStruct((S // TQ, 8, M), F32),
        jax.ShapeDtypeStruct((D, D), BF),
        jax.ShapeDtypeStruct((HEADS, KVL, 128), F32),
        jax.ShapeDtypeStruct((GROUPS, GD, GD), F32),
        jax.ShapeDtypeStruct((1, PW), F32),
        jax.ShapeDtypeStruct((1, D), F32), jax.ShapeDtypeStruct((1, D), F32), jax.ShapeDtypeStruct((1, D), F32),
        jax.ShapeDtypeStruct((1, D), F32),
    )
    in_specs = [rev(D), rev(D), rev(D), rev(D), _full(mod.shape), _full((1, D)), rev(PW), rev(PW), _full(w_o.shape),
                rev(PW), rev(PW), _full((1, PW)), _full(wpool_dc.shape), rev3(KVL, M), _full(wuv_vc.shape)]
    out_specs = (rev(D), rev(PW), rev3(M, KVL), rev3(8, M), _full((D, D)), _full((HEADS, KVL, 128)),
                 _full((GROUPS, GD, GD)), _full((1, PW)), _full((1, D)), _full((1, D)), _full((1, D)), _full((1, D)))
    return pl.pallas_call(
        body, name="mix_bwd", out_shape=out_shape, grid=(n,), in_specs=in_specs, out_specs=out_specs,
        scratch_shapes=[pltpu.VMEM((16, PW), F32), pltpu.VMEM((D, D), F32), pltpu.VMEM((PW, PW), F32),
                        pltpu.VMEM((PW, PW), BF), pltpu.VMEM((HEADS * 128, HEADS * KVL), BF)],
        compiler_params=_params(("arbitrary",)),
    )(dh2, dx3, x2, mix, mod, g_ffn, ymla, ypool, w_o, ypre, pooled, pool_scale, wpool_dc, olat, wuv_vc)


def _attn_bwd(qs, kv, dolat, lse, delta):
    nq = qs.shape[0]
    S = kv.shape[0]
    M = HEADS * TQ
    nk = S // TK

    def body(qs_ref, kv_ref, do_ref, lse_ref, delta_ref, dkv_ref, dqt_ref, p_ref, ds_ref):
        kt = pl.program_id(0)
        k = kv_ref[...]
        v = k[:, 0:KVL]
        k_t = k.astype(F32).T.astype(BF)

        @pl.when(kt == 0)
        def _():
            dqt_ref[...] = jnp.zeros_like(dqt_ref)

        def step(qi, carry, first_chunk=None):
            dk, dv = carry
            q = qs_ref[qi]
            do = do_ref[qi]
            s = _dot_nt(k, q)
            dp = _dot_nt(v, do)
            lse_row = lse_ref[qi, 0:1, :] * LOG2_E
            delta_row = delta_ref[qi, 0:1, :]
            q_chunk = (lax.broadcasted_iota(jnp.int32, (1, M), 1) & (TQ - 1)) >> 6
            for r in range(0, TK, VPU_ROWS):
                rows = slice(r, r + VPU_ROWS)
                p = jnp.exp2(s[rows, :] * EXP2_SCALE - lse_row)
                if first_chunk is not None:
                    p = jnp.where((r >> 6) + first_chunk <= q_chunk, p, 0.0)
                p_ref[rows, :] = p.astype(BF)
                ds_ref[rows, :] = (p * (dp[rows, :] - delta_row) * SM_SCALE).astype(BF)
            ds = ds_ref[...]
            dv = dv + _dot(p_ref[...], do)
            dk = dk + _dot(ds, q)
            dqt_ref[qi] += _dot(k_t, ds)
            return dk, dv

        per = TQ // TK
        first = kt // per
        carry = step(first, (jnp.zeros((TK, QW), F32), jnp.zeros((TK, KVL), F32)), (kt % per) * (TK // 64))
        dk, dv = lax.fori_loop(first + 1, nq, step, carry)
        dkv_ref[...] = dk + jnp.concatenate([dv, jnp.zeros((TK, QW - KVL), F32)], axis=1)

    out_shape = (jax.ShapeDtypeStruct((S, QW), F32), jax.ShapeDtypeStruct((nq, QW, M), F32))
    return pl.pallas_call(
        body, name="attn_bwd", out_shape=out_shape, grid=(nk,),
        in_specs=[_vmem(), _rows(TK, QW), _vmem(), _vmem(), _vmem()],
        out_specs=(_rows(TK, QW), _vmem()),
        scratch_shapes=[pltpu.VMEM((TK, M), BF), pltpu.VMEM((TK, M), BF)],
        compiler_params=_params(("arbitrary",)),
    )(qs, kv, dolat, lse, delta)


def _in_bwd(dqt, dkv, du, raw, qn, h1, x, dx2, mod, g_mix, w_in, g_q, g_kv, w_uq, wuk_cd, perm_t, cos4, sin4, csk,
            snk):
    S = x.shape[0]
    ts = 512
    n = S // ts
    nsub = ts // TQ
    M = HEADS * TQ

    def body(dqt_ref, dkv_ref, du_ref, raw_ref, qn_ref, h1_ref, x_ref, dx2_ref, mod_ref, gmix_ref, win_ref, gq_ref,
             gkv_ref, wuq_ref, wuk_ref, permt_ref, cos_ref, sin_ref, csk_ref, snk_ref,
             dx_ref, dwin_ref, dwuq_ref, dwuk_ref, dgq_ref, dgkv_ref, dsc1_ref, dsh1_ref, dgmix_ref, dwin_acc,
             dwuq_acc, dwuk_acc, wuk_bd):
        i = pl.program_id(0)

        @pl.when(i == 0)
        def _():
            dwin_acc[...] = jnp.zeros_like(dwin_acc)
            dwuq_acc[...] = jnp.zeros_like(dwuq_acc)
            dwuk_acc[...] = jnp.zeros_like(dwuk_acc)
            _fill_block_diagonal(wuk_bd, wuk_ref)
            for r in (dgq_ref, dgkv_ref, dsc1_ref, dsh1_ref, dgmix_ref):
                r[...] = jnp.zeros_like(r)

        dq_blocks = [dqt_ref[a].T for a in range(nsub)]
        dq_heads = [jnp.concatenate([blk[hd * TQ:(hd + 1) * TQ, :] for blk in dq_blocks], axis=0)
                    for hd in range(HEADS)]
        dq_lat = jnp.concatenate([dqh[:, 0:KVL] for dqh in dq_heads], axis=1).astype(BF)
        dq_rope = jnp.concatenate([dqh[:, KVL:QW] for dqh in dq_heads], axis=1).astype(BF)
        dq_nope = _dot(dq_lat, wuk_bd[...])
        dwuk_acc[...] += _dot_tn(dq_lat, qn_ref[...])
        drope = _dot(dq_rope, permt_ref[...])
        do1 = drope[:, 0:128]
        do2 = drope[:, 128:256]
        cosv = cos_ref[...]
        sinv = sin_ref[...]
        dq = jnp.concatenate([dq_nope, do1 * cosv + do2 * sinv, do2 * cosv - do1 * sinv], axis=1).astype(BF)

        cq_raw = raw_ref[:, 0:QL]
        ckv_raw = raw_ref[:, QL:QL + KVL]
        rq = _rms(cq_raw)
        nq_ = cq_raw * rq
        gq = gq_ref[...]
        dwuq_acc[...] += _dot_tn((nq_ * gq).astype(BF), dq)
        dc_q = _dot_nt(dq, wuq_ref[...])
        dgq_ref[...] += _colsum(dc_q * nq_)
        dcq_raw = _rms_bwd(dc_q * gq, nq_, rq)

        dkv = dkv_ref[...]
        rk = _rms(ckv_raw)
        nk_ = ckv_raw * rk
        dc_kv = dkv[:, 0:KVL]
        dgkv_ref[...] += _colsum(dc_kv * nk_)
        dckv_raw = _rms_bwd(dc_kv * gkv_ref[...], nk_, rk)
        dkr_roped = dkv[:, KVL:QW]
        dkr = dkr_roped * csk_ref[...] - _swap_halves(dkr_roped) * snk_ref[...]

        dproj = jnp.concatenate([dcq_raw, dckv_raw, dkr, du_ref[...]], axis=1).astype(BF)
        dwin_acc[...] += _dot_tn(h1_ref[...], dproj)
        dh1 = _dot_nt(dproj, win_ref[...])

        sc1 = mod_ref[0:1, D:2 * D]
        gmix = gmix_ref[...]
        xv = x_ref[...]
        r1 = _rms(xv)
        xn1 = xv * r1
        along = _colsum(dh1 * xn1)
        dsc1_ref[...] += along * gmix
        dsh1_ref[...] += _colsum(dh1)
        dgmix_ref[...] += along * (1.0 + sc1)
        dx_ref[...] = dx2_ref[...] + _rms_bwd(dh1 * (gmix * (1.0 + sc1)), xn1, r1)

        @pl.when(i == n - 1)
        def _():
            dwin_ref[...] = dwin_acc[...].astype(BF)
            dwuq_ref[...] = dwuq_acc[...].astype(BF)
            for hd in range(HEADS):
                dwuk_ref[hd] = dwuk_acc[hd * KVL:(hd + 1) * KVL, hd * NOPE:(hd + 1) * NOPE]

    out_shape = (
        jax.ShapeDtypeStruct((S, D), F32),
        jax.ShapeDtypeStruct((D, D), BF),
        jax.ShapeDtypeStruct((QL, 768), BF),
        jax.ShapeDtypeStruct((HEADS, KVL, NOPE), F32),
        jax.ShapeDtypeStruct((1, QL), F32), jax.ShapeDtypeStruct((1, KVL), F32),
        jax.ShapeDtypeStruct((1, D), F32), jax.ShapeDtypeStruct((1, D), F32), jax.ShapeDtypeStruct((1, D), F32),
    )
    in_specs = [pl.BlockSpec((nsub, QW, M), lambda i: (i, 0, 0)), _rows(ts, QW), _rows(ts, PW), _rows(ts, 384),
                _rows(ts, HEADS * NOPE), _rows(ts, D), _rows(ts, D), _rows(ts, D), _full(mod.shape), _full((1, D)),
                _full(w_in.shape), _full((1, QL)), _full((1, KVL)), _full(w_uq.shape), _full(wuk_cd.shape),
                _full(perm_t.shape), _rows(ts, 128), _rows(ts, 128), _rows(ts, 128), _rows(ts, 128)]
    out_specs = (_rows(ts, D), _full((D, D)), _full((QL, 768)), _full((HEADS, KVL, NOPE)), _full((1, QL)),
                 _full((1, KVL)), _full((1, D)), _full((1, D)), _full((1, D)))
    return pl.pallas_call(
        body, name="in_bwd", out_shape=out_shape, grid=(n,), in_specs=in_specs, out_specs=out_specs,
        scratch_shapes=[pltpu.VMEM((D, D), F32), pltpu.VMEM((QL, 768), F32),
                        pltpu.VMEM((HEADS * KVL, HEADS * NOPE), F32), pltpu.VMEM((HEADS * KVL, HEADS * NOPE), BF)],
        compiler_params=_params(("arbitrary",)),
    )(dqt, dkv, du, raw, qn, h1, x, dx2, mod, g_mix, w_in, g_q, g_kv, w_uq, wuk_cd, perm_t, cos4, sin4, csk, snk)


def _rope_perm():
    p = np.zeros((HEADS, 2 * 128, 128), np.float32)
    for hd in range(HEADS):
        for t in range(HALF):
            p[hd, hd * HALF + t, t] = 1.0
            p[hd, 128 + hd * HALF + t, HALF + t] = 1.0
    return p


def _rope_tables(positions):
    freqs = jnp.power(ROPE_THETA, -jnp.arange(HALF, dtype=F32) / HALF)
    ang = positions.astype(F32)[:, None] * jnp.tile(freqs, HEADS)[None, :]
    cos4 = jnp.cos(ang)
    sin4 = jnp.sin(ang)
    lane = jnp.arange(HEADS * HALF)[None, :]
    csk = jnp.where(lane < ROPE, cos4, 0.0)
    snk = jnp.where(lane < HALF, -sin4, jnp.where(lane < ROPE, sin4, 0.0))
    return cos4, sin4, csk, snk


def _local_step(x, rope, target, mod, g_mix, w_in_p, g_q, g_kv, w_uq_p, w_uk, w_uv, w_pool, pool_scale, g_ffn,
                g_final, late, ffn_grads_exchange):
    perm = jnp.asarray(_rope_perm().transpose(1, 0, 2).reshape(2 * 128, HEADS * 128), BF)
    perm_t = jnp.asarray(_rope_perm().transpose(0, 2, 1).reshape(HEADS * 128, 2 * 128), BF)
    cos4, sin4, csk, snk = rope
    wuk_dc = w_uk.transpose(1, 2, 0).astype(BF)
    wuk_cd = w_uk.transpose(1, 0, 2).astype(BF)
    wuv_vc = w_uv.transpose(1, 2, 0).astype(BF)
    wpool = w_pool.astype(BF)
    wpool_dc = w_pool.transpose(0, 2, 1).astype(BF)

    h1, raw, qn, qs, kv, kvt, pooled, ypre, ypool = _fwd_in(
        x, mod, g_mix, w_in_p, g_q, g_kv, w_uq_p, wuk_dc, perm, cos4, sin4, csk, snk, wpool, pool_scale)
    olat, ymla, lse = _attn_fwd(qs, kv, kvt, wuv_vc)
    w_o, wg_t, wu_t, wd = late
    x2, mix, h2_t, a, b, dx3, dff, dff_t, loss, dgfin, dgt2 = _ffn_fwd(
        x, ymla, ypool, mod, w_o, g_ffn, wg_t, wu_t, wd, g_final, target)
    da, db, dh2 = _ffn_bwd_acts(dff, a, b, wg_t, wu_t, wd)
    (dx2, du, dolat, delta, dwo, dwuv, dwpool, dpscale, dgt1, dsc2, dsh2, dgffn) = _mix_bwd(
        dh2, dx3, x2, mix, mod, g_ffn, ymla, ypool, w_o, ypre, pooled, pool_scale, wpool_dc, olat, wuv_vc)
    dwg_t, dwu_t, dwd = _ffn_bwd_weights(dff_t, h2_t, da, db, a, b)
    ffn_parts = ffn_grads_exchange((dwg_t, dwu_t, dwd, dwo))
    dkv, dqt = _attn_bwd(qs, kv, dolat, lse, delta)
    dx, dwin, dwuq, dwuk, dgq, dgkv, dsc1, dsh1, dgmix = _in_bwd(
        dqt, dkv, du, raw, qn, h1, x, dx2, mod, g_mix, w_in_p, g_q, g_kv, w_uq_p, wuk_cd, perm_t, cos4, sin4, csk,
        snk)
    dmod = jnp.concatenate([dsh1, dsc1, dgt1, dsh2, dsc2, dgt2], axis=1)
    replicated = dict(
        w_uk=dwuk.transpose(1, 0, 2), w_uv=dwuv.transpose(1, 0, 2), w_pool=dwpool, g_mix=dgmix, g_q=dgq, g_kv=dgkv,
        pool_scale=dpscale, g_ffn=dgffn, g_final=dgfin)
    return loss[0, 0], dx, dmod, (dwin, dwuq), ffn_parts, replicated


def _my_pos():
    return lax.axis_index("x"), lax.axis_index("y"), lax.axis_index("c")


def _peer(pos, k):
    x, y, c = pos
    return (1 - x if k & 4 else x, 1 - y if k & 2 else y, 1 - c if k & 1 else c)


def _index(pos):
    x, y, c = pos
    return 4 * x + 2 * y + c


def _remote(src, dst, send_sem, recv_sem, to):
    return pltpu.make_async_remote_copy(src_ref=src, dst_ref=dst, send_sem=send_sem, recv_sem=recv_sem,
                                        device_id=to, device_id_type=MESH)


def _ada_mod(c, w_ada, b_ada, after):
    def body(c_ref, w_ref, b_ref, after_ref, mod_ref, call_ref, cbuf, sbuf, rbuf, send1, recv1, send2, recv2):
        me = _my_pos()
        mi = _index(me)
        cv = c_ref[...]
        cbuf[...] = jnp.broadcast_to(cv * jax.nn.sigmoid(cv), (8, D))
        call_ref[mi] = cbuf[...]
        first = [_remote(cbuf, call_ref.at[mi], send1.at[k - 1], recv1.at[k - 1], _peer(me, k)) for k in range(1, NDEV)]
        for cp in first:
            cp.start()
        for k in range(1, NDEV):
            _remote(cbuf, call_ref.at[_index(_peer(me, k))], send1.at[k - 1], recv1.at[k - 1], _peer(me, k)).wait_recv()
        c_all = jnp.concatenate([call_ref[b][0:1, :] for b in range(NDEV)], axis=0)
        blocks = _dot(c_all.astype(BF), w_ref[...].astype(BF))
        for b in range(NDEV):
            sbuf[b] = jnp.broadcast_to(blocks[b:b + 1, :], (8, MODC))
        second = []
        for k in range(1, NDEV):
            to = _peer(me, k)
            second.append(_remote(sbuf.at[_index(to)], rbuf.at[mi], send2.at[k - 1], recv2.at[k - 1], to))
        for cp in second:
            cp.start()
        rbuf[mi] = sbuf[mi]
        for k in range(1, NDEV):
            to = _peer(me, k)
            _remote(sbuf.at[_index(to)], rbuf.at[_index(to)], send2.at[k - 1], recv2.at[k - 1], to).wait_recv()
        for j in range(NDEV):
            mod_ref[:, j * MODC:(j + 1) * MODC] = rbuf[j] + b_ref[:, j * MODC:(j + 1) * MODC]
        for cp in first + second:
            cp.wait_send()

    return pl.pallas_call(
        body, name="ada_mod",
        out_shape=(jax.ShapeDtypeStruct((8, N_MOD * D), F32), jax.ShapeDtypeStruct((NDEV, 8, D), F32)),
        in_specs=[_vmem(), _vmem(), _vmem(), _any()], out_specs=(_vmem(), _vmem()),
        scratch_shapes=[pltpu.VMEM((8, D), F32), pltpu.VMEM((NDEV, 8, MODC), F32), pltpu.VMEM((NDEV, 8, MODC), F32),
                        pltpu.SemaphoreType.DMA((NDEV - 1,)), pltpu.SemaphoreType.DMA((NDEV - 1,)),
                        pltpu.SemaphoreType.DMA((NDEV - 1,)), pltpu.SemaphoreType.DMA((NDEV - 1,))],
        compiler_params=_params(),
    )(c, w_ada, b_ada, after)


def _sequencer_scatter(name, collective_id, srcs, after=()):
    n = len(srcs)

    def of(src, to_index):
        r = src.shape[0] // NDEV
        return src.at[pl.ds(pl.multiple_of(to_index * r, 16), r), :]

    def body(*refs):
        src, zone = refs[:n], refs[n + len(after):2 * n + len(after)]
        send, recv, local = refs[2 * n + len(after):]
        me = _my_pos()
        mi = _index(me)
        barrier = pltpu.get_barrier_semaphore()
        for k in range(1, NDEV):
            pl.semaphore_signal(barrier, inc=1, device_id=_peer(me, k), device_id_type=MESH)
        pl.semaphore_wait(barrier, NDEV - 1)
        own = [pltpu.make_async_copy(of(src[a], mi), zone[a].at[mi], local.at[a]) for a in range(n)]
        for cp in own:
            cp.start()
        for a in range(n):
            for k in range(1, NDEV):
                to = _peer(me, k)
                s = a * (NDEV - 1) + k - 1
                _remote(of(src[a], _index(to)), zone[a].at[mi], send.at[s], recv.at[s], to).start()
        for cp in own:
            cp.wait()
        for a in range(n):
            for k in range(1, NDEV):
                to = _peer(me, k)
                s = a * (NDEV - 1) + k - 1
                cp = _remote(of(src[a], mi), zone[a].at[_index(to)], send.at[s], recv.at[s], to)
                cp.wait_send()
                cp.wait_recv()

    return pl.kernel(
        body, name=name, mesh=plsc.ScalarSubcoreMesh(axis_name="sequencer", num_cores=1),
        out_type=tuple(jax.ShapeDtypeStruct((NDEV, s.shape[0] // NDEV, s.shape[1]), s.dtype) for s in srcs),
        scratch_types=[pltpu.SemaphoreType.DMA((n * (NDEV - 1),)), pltpu.SemaphoreType.DMA((n * (NDEV - 1),)),
                       pltpu.SemaphoreType.DMA((n,))],
        compiler_params=pltpu.CompilerParams(collective_id=collective_id),
    )(*srcs, *after)


CHIP_PEERS = (2, 4, 6)


def _sequencer_gather(name, collective_id, srcs, after=()):
    n = len(srcs)
    per = NDEV - 1

    def body(*refs):
        src, zone = refs[:n], refs[n + len(after):2 * n + len(after)]
        send, recv, local = refs[2 * n + len(after):]
        me = _my_pos()
        mi = _index(me)
        sibling = _peer(me, 1)
        talk_to = (sibling,) + tuple(_peer(me, k) for k in CHIP_PEERS)
        barrier = pltpu.get_barrier_semaphore()
        for to in talk_to:
            pl.semaphore_signal(barrier, inc=1, device_id=to, device_id_type=MESH)
        pl.semaphore_wait(barrier, len(talk_to))

        def copy(a, slot, block_of, to, from_src=False):
            rows = zone[a].at[_index(block_of)]
            return _remote(src[a] if from_src else rows, rows, send.at[a * per + slot], recv.at[a * per + slot], to)

        own = [pltpu.make_async_copy(src[a], zone[a].at[mi], local.at[a]) for a in range(n)]
        for cp in own:
            cp.start()
        started = []
        for a in range(n):
            started.append(copy(a, 0, me, sibling, from_src=True))
            started += [copy(a, 1 + j, me, _peer(me, k), from_src=True) for j, k in enumerate(CHIP_PEERS)]
        for cp in started:
            cp.start()
        for a in range(n):
            for j, k in enumerate(CHIP_PEERS):
                copy(a, 1 + j, _peer(me, k), me).wait_recv()
                passed = copy(a, 4 + j, _peer(me, k), sibling)
                passed.start()
                started.append(passed)
        for a in range(n):
            copy(a, 0, sibling, me).wait_recv()
            for j, k in enumerate(CHIP_PEERS):
                copy(a, 4 + j, _peer(me, k | 1), me).wait_recv()
        for cp in started:
            cp.wait_send()
        for cp in own:
            cp.wait()

    return pl.kernel(
        body, name=name, mesh=plsc.ScalarSubcoreMesh(axis_name="sequencer", num_cores=1),
        out_type=tuple(jax.ShapeDtypeStruct((NDEV,) + s.shape, s.dtype) for s in srcs),
        scratch_types=[pltpu.SemaphoreType.DMA((n * per,)), pltpu.SemaphoreType.DMA((n * per,)),
                       pltpu.SemaphoreType.DMA((n,))],
        compiler_params=pltpu.CompilerParams(collective_id=collective_id),
    )(*srcs, *after)


def _blocked(shape, nb, axis=0):
    block = tuple(s // nb if d == axis else s for d, s in enumerate(shape))
    return pl.BlockSpec(block, lambda i: tuple(i if d == axis else 0 for d in range(len(shape))))


def _sum_partials(name, parts, nb):
    n = len(parts)

    def body(*refs):
        for a in range(n):
            acc = refs[a][0].astype(F32)
            for p in range(1, NDEV):
                acc = acc + refs[a][p].astype(F32)
            refs[n + a][...] = acc

    return pl.pallas_call(
        body, name=name, grid=(nb,),
        out_shape=tuple(jax.ShapeDtypeStruct(p.shape[1:], F32) for p in parts),
        in_specs=[_blocked(p.shape, nb, 1) for p in parts],
        out_specs=tuple(_blocked(p.shape[1:], nb) for p in parts), compiler_params=_params(("arbitrary",)),
    )(*parts)


def _small_all_reduce(buf):
    def body(buf_ref, got_ref, red_ref, mine, send1, recv1, send2, recv2):
        me = _my_pos()
        mi = _index(me)
        first = []
        for k in range(1, NDEV):
            to = _peer(me, k)
            first.append(_remote(buf_ref.at[_index(to)], got_ref.at[mi], send1.at[k - 1], recv1.at[k - 1], to))
        for cp in first:
            cp.start()
        got_ref[mi] = buf_ref[mi]
        for k in range(1, NDEV):
            to = _peer(me, k)
            _remote(buf_ref.at[mi], got_ref.at[_index(to)], send1.at[k - 1], recv1.at[k - 1], to).wait_recv()
        acc = got_ref[0]
        for p in range(1, NDEV):
            acc = acc + got_ref[p]
        mine[...] = acc
        second = [_remote(mine, red_ref.at[mi], send2.at[k - 1], recv2.at[k - 1], _peer(me, k)) for k in range(1, NDEV)]
        for cp in second:
            cp.start()
        red_ref[mi] = acc
        for k in range(1, NDEV):
            to = _peer(me, k)
            _remote(mine, red_ref.at[_index(to)], send2.at[k - 1], recv2.at[k - 1], to).wait_recv()
        for cp in first + second:
            cp.wait_send()

    return pl.pallas_call(
        body, name="small_all_reduce",
        out_shape=(jax.ShapeDtypeStruct(buf.shape, F32), jax.ShapeDtypeStruct(buf.shape, F32)),
        in_specs=[_vmem()], out_specs=(_vmem(), _vmem()),
        scratch_shapes=[pltpu.VMEM(buf.shape[1:], F32),
                        pltpu.SemaphoreType.DMA((NDEV - 1,)), pltpu.SemaphoreType.DMA((NDEV - 1,)),
                        pltpu.SemaphoreType.DMA((NDEV - 1,)), pltpu.SemaphoreType.DMA((NDEV - 1,))],
        compiler_params=_params(),
    )(buf)


def _adamw_math(w, g, m, v):
    m = ADAM_B1 * m + (1.0 - ADAM_B1) * g
    v = ADAM_B2 * v + (1.0 - ADAM_B2) * jnp.square(g)
    m_hat = m / (1.0 - ADAM_B1 ** ADAM_STEP)
    v_hat = v / (1.0 - ADAM_B2 ** ADAM_STEP)
    delta = -ADAM_LR * (m_hat / (jnp.sqrt(v_hat) + ADAM_EPS) + ADAM_WD * w)
    return delta, m, v


def _adamw_group(name, ws, gs, ms, vs, nb):
    n = len(ws)

    def body(*refs):
        for a in range(n):
            w, g, m, v = (refs[q * n + a][...] for q in range(4))
            delta, m2, v2 = _adamw_math(w, g, m, v)
            refs[4 * n + a][...] = delta
            refs[5 * n + a][...] = m2
            refs[6 * n + a][...] = v2

    shapes = tuple(jax.ShapeDtypeStruct(w.shape, F32) for w in ws)
    specs = [_blocked(w.shape, nb) for w in ws]
    outs = pl.pallas_call(
        body, name=name, grid=(nb,), out_shape=shapes * 3, in_specs=specs * 4, out_specs=tuple(specs * 3),
        compiler_params=_params(("arbitrary",)),
    )(*ws, *gs, *ms, *vs)
    return outs[:n], outs[n:2 * n], outs[2 * n:]


def _adamw_from_partials(name, ws, parts, ms, vs, nb):
    n = len(ws)

    def body(*refs):
        for a in range(n):
            part = refs[n + a]
            g = part[0].astype(F32)
            for p in range(1, NDEV):
                g = g + part[p].astype(F32)
            delta, m2, v2 = _adamw_math(refs[a][...], g, refs[2 * n + a][...], refs[3 * n + a][...])
            refs[4 * n + a][...] = g
            refs[5 * n + a][...] = delta
            refs[6 * n + a][...] = m2
            refs[7 * n + a][...] = v2

    shapes = tuple(jax.ShapeDtypeStruct(w.shape, F32) for w in ws)
    specs = [_blocked(w.shape, nb) for w in ws]
    outs = pl.pallas_call(
        body, name=name, grid=(nb,), out_shape=shapes * 4,
        in_specs=specs + [_blocked(p.shape, nb, 1) for p in parts] + specs * 2, out_specs=tuple(specs * 4),
        compiler_params=_params(("arbitrary",)),
    )(*ws, *parts, *ms, *vs)
    return outs[:n], outs[n:2 * n], outs[2 * n:3 * n], outs[3 * n:]


def _adamw_ada(w, m, v, c_all_t, dmod_rows):
    nb = 4

    def body(w_ref, m_ref, v_ref, c_ref, dm_ref, g_ref, d_ref, m2_ref, v2_ref):
        g = _dot(c_ref[...], dm_ref[...].astype(BF))
        g_ref[...] = g
        delta, m2, v2 = _adamw_math(w_ref[...], g, m_ref[...], v_ref[...])
        d_ref[...] = delta
        m2_ref[...] = m2
        v2_ref[...] = v2

    shp = jax.ShapeDtypeStruct(w.shape, F32)
    spec = _blocked(w.shape, nb)
    return pl.pallas_call(
        body, name="adamw_ada", grid=(nb,), out_shape=(shp, shp, shp, shp),
        in_specs=[spec, spec, spec, _blocked(c_all_t.shape, nb), _full(dmod_rows.shape)],
        out_specs=(spec, spec, spec, spec), compiler_params=_params(("arbitrary",)),
    )(w, m, v, c_all_t, dmod_rows)


def _w_in_to_kernel(w):
    return jnp.concatenate([w[:, 0:448], jnp.zeros((w.shape[0], 64), w.dtype), w[:, 448:960]], axis=1)


def _w_in_from_kernel(w):
    return jnp.concatenate([w[:, 0:448], w[:, 512:1024]], axis=1)


def _w_uq_to_kernel(w):
    r = w.shape[0]
    return jnp.concatenate([w[:, :, 0:NOPE].reshape(r, HEADS * NOPE),
                            w[:, :, NOPE:NOPE + HALF].reshape(r, HEADS * HALF),
                            w[:, :, NOPE + HALF:].reshape(r, HEADS * HALF)], axis=1)


def _w_uq_from_kernel(w):
    r = w.shape[0]
    return jnp.concatenate([w[:, 0:512].reshape(r, HEADS, NOPE), w[:, 512:640].reshape(r, HEADS, HALF),
                            w[:, 640:768].reshape(r, HEADS, HALF)], axis=2)


REP_NAMES = ("w_uk", "w_uv", "w_pool", "g_mix", "g_q", "g_kv", "pool_scale", "g_ffn", "g_final")


def kernel(x, c, positions, w_ada, b_ada, g_mix, w_in, g_q, g_kv, w_uq, w_uk, w_uv, w_pool, pool_scale, w_o, g_ffn, w_gate, w_up, w_down, g_final, loss_target, m_w_ada, m_b_ada, m_g_mix, m_w_in, m_g_q, m_g_kv, m_w_uq, m_w_uk, m_w_uv, m_w_pool, m_pool_scale, m_w_o, m_g_ffn, m_w_gate, m_w_up, m_w_down, m_g_final, v_w_ada, v_b_ada, v_g_mix, v_w_in, v_g_q, v_g_kv, v_w_uq, v_w_uk, v_w_uv, v_w_pool, v_pool_scale, v_w_o, v_g_ffn, v_w_gate, v_w_up, v_w_down, v_g_final):
    given = dict(locals())

    merge = lambda g: g.reshape(NDEV * g.shape[1], g.shape[2])
    w_in_p, w_uq_p = (merge(g) for g in _sequencer_gather(
        "gather_in", 3, (_w_in_to_kernel(w_in[0]).astype(BF), _w_uq_to_kernel(w_uq[0]).astype(BF))))

    rope = _rope_tables(positions[0])
    mod, c_all8 = _ada_mod(c, w_ada[0], b_ada, rope[3][0:8, :])
    c_all = c_all8[:, 0, :]
    late = _sequencer_gather(
        "gather_late", 1, (w_o[0].astype(BF), w_gate[0].T.astype(BF), w_up[0].T.astype(BF), w_down[0].astype(BF)),
        after=(mod[:, 0:128], w_in_p[0:16, 0:128], w_uq_p[0:16, 0:128]))

    def ffn_grads_exchange(arrays):
        return _sequencer_scatter("scatter_ffn", 2, arrays)

    loss, dx, dmod, tail_grads, ffn_parts, replicated = _local_step(
        x[0], rope, loss_target[0], mod, g_mix, w_in_p, g_q, g_kv, w_uq_p, w_uk[0], w_uv[0], w_pool[0],
        pool_scale, g_ffn, g_final.reshape(1, D), tuple(merge(g) for g in late), ffn_grads_exchange)

    flat = jnp.concatenate([replicated[k].reshape(-1) for k in REP_NAMES] + [loss.reshape(1)])
    flat = jnp.pad(flat, (0, NDEV * REP_ROWS * 128 - flat.shape[0])).reshape(NDEV, REP_ROWS, 128)
    dmod_blocks = jnp.pad(dmod.reshape(NDEV, MODC // 128, 128), ((0, 0), (0, MOD_ROWS - MODC // 128), (0, 0)))
    got, red = _small_all_reduce(jnp.concatenate([dmod_blocks, flat], axis=1))

    tail_parts = _sequencer_scatter("scatter_tail", 4, tail_grads,
                                    after=(ffn_parts[0][0, 0:16, 0:128], red[0, 0:8, :]))
    g_in_p, g_uq_p = _sum_partials("sum_tail_partials", tail_parts, 1)
    as_transpose = ("w_in", "w_gate", "w_up")
    grads = dict(w_in=_w_in_from_kernel(g_in_p).T, w_uq=_w_uq_from_kernel(g_uq_p))
    partials = dict(w_gate=ffn_parts[0], w_up=ffn_parts[1], w_down=ffn_parts[2], w_o=ffn_parts[3])
    dmod_rows = got[:, 0:MODC // 128, :].reshape(NDEV, MODC)
    grads["b_ada"] = red[:, 0:MODC // 128, :].reshape(1, N_MOD * D)
    rep_flat = red[:, MOD_ROWS:, :].reshape(-1)
    off = 0
    for k in REP_NAMES:
        size = int(np.prod(given[k].shape))
        grads[k] = rep_flat[off:off + size]
        off += size

    view = {k: (given[k].shape[1:] if given[k].ndim > 2 else given[k].shape)
            for k in REP_NAMES + ("b_ada", "w_ada", "w_in", "w_uq", "w_o", "w_gate", "w_up", "w_down")}
    view.update(g_final=(1, D))
    names = ["w_ada", "b_ada", "g_mix", "w_in", "g_q", "g_kv", "w_uq", "w_uk", "w_uv", "w_pool", "pool_scale",
             "w_o", "g_ffn", "w_gate", "w_up", "w_down", "g_final"]
    g_ada, d_ada, m_ada, v_ada = _adamw_ada(w_ada[0], m_w_ada[0], v_w_ada[0], c_all.T.astype(BF), dmod_rows)
    out_g, out_d, out_m, out_v = dict(w_ada=g_ada), dict(w_ada=d_ada), dict(w_ada=m_ada), dict(w_ada=v_ada)
    groups = (("adamw_ffn", ("w_gate", "w_up", "w_down", "w_o"), 4),
              ("adamw_replicated", REP_NAMES + ("b_ada",), 1),
              ("adamw_tail", ("w_in", "w_uq"), 1))
    for gname, members, nb in groups:
        turn = lambda k, t: t.T if k in as_transpose else t
        ws = [turn(k, given[k].reshape(view[k])) for k in members]
        ms = [turn(k, given["m_" + k].reshape(view[k])) for k in members]
        vs = [turn(k, given["v_" + k].reshape(view[k])) for k in members]
        if members[0] in partials:
            gs, ds, m2, v2 = _adamw_from_partials(gname, ws, [partials[k] for k in members], ms, vs, nb)
        else:
            gs = [grads[k] if k in as_transpose else grads[k].reshape(view[k]) for k in members]
            ds, m2, v2 = _adamw_group(gname, ws, gs, ms, vs, nb)
        for k, g, d, mm, vv in zip(members, gs, ds, m2, v2):
            out_g[k], out_d[k], out_m[k], out_v[k] = turn(k, g), turn(k, d), turn(k, mm), turn(k, vv)

    total = rep_flat[off]
    shaped = lambda d: [d[k].reshape(given[k].shape) for k in names]
    return (total, dx[None], *shaped(out_g), *shaped(out_d), *shaped(out_m), *shaped(out_v))
```

```python
import numpy as np
import jax
import jax.numpy as jnp
from jax import lax
from jax.experimental import pallas as pl
from jax.experimental.pallas import tpu as pltpu
from jax.experimental.pallas import tpu_sc as plsc

D = 1024
HEADS = 4
NOPE = 128
ROPE = 64
HALF = ROPE // 2
QL = 256
KVL = 128
FF = 2816
PW = 512
GROUPS = 4
GD = 128
N_MOD = 6
EPS = 1e-6
SM_SCALE = (NOPE + ROPE) ** -0.5
LOG2_E = 1.4426950408889634
EXP2_SCALE = SM_SCALE * LOG2_E
ROPE_THETA = 10000.0
NDEV = 8
MODC = N_MOD * D // NDEV

ADAM_LR = 0.001
ADAM_B1 = 0.9
ADAM_B2 = 0.999
ADAM_EPS = 1e-08
ADAM_WD = 0.01
ADAM_STEP = 10

BF = jnp.bfloat16
F32 = jnp.float32
VMEM_LIMIT_V7X = 60 * 1024 * 1024
MESH = pl.DeviceIdType.MESH

TQ = 512
TK = 512
QW = 256
VPU_ROWS = 16
MOD_ROWS = 8
REP_ROWS = 200
SMALL_ROWS = MOD_ROWS + REP_ROWS


def _params(sem=None):
    return pltpu.CompilerParams(dimension_semantics=sem, vmem_limit_bytes=VMEM_LIMIT_V7X)


def _dot(a, b):
    return jnp.dot(a, b, preferred_element_type=F32)


def _dot_nt(a, b):
    return lax.dot_general(a, b, (((1,), (1,)), ((), ())), preferred_element_type=F32)


def _dot_tn(a, b):
    return _dot(a.astype(F32).T.astype(BF), b)


def _full(shape):
    return pl.BlockSpec(shape, lambda *_: (0,) * len(shape))


def _rows(ts, cols):
    return pl.BlockSpec((ts, cols), lambda i: (i, 0))


def _vmem():
    return pl.BlockSpec(memory_space=pltpu.VMEM)


def _any():
    return pl.BlockSpec(memory_space=pl.ANY)


def _rms(v):
    return lax.rsqrt(jnp.mean(v * v, axis=-1, keepdims=True) + EPS)


def _rms_bwd(dn, n, r):
    return r * (dn - n * jnp.mean(dn * n, axis=-1, keepdims=True))


def _colsum(v):
    return jnp.sum(v, axis=0, keepdims=True)


def _swap_halves(v):
    lane = lax.broadcasted_iota(jnp.int32, v.shape, 1)
    return jnp.where(lane < HALF, pltpu.roll(v, 128 - HALF, 1), pltpu.roll(v, HALF, 1))


def _window_lane_width():
    lane = lax.broadcasted_iota(jnp.int32, (1, PW), 1)
    return jnp.where(lane < 128, 2.0, jnp.where(lane < 256, 4.0, jnp.where(lane < 384, 8.0, 16.0))).astype(F32)


def _window_sums(ext, back):
    n = ext.shape[0]

    def sh(v, k):
        return pltpu.roll(v, k if back else n - k, 0)

    s2 = ext + sh(ext, 1)
    e4 = s2[:, 128:]
    s4 = e4 + sh(e4, 2)
    e8 = s4[:, 128:]
    s8 = e8 + sh(e8, 4)
    e16 = s8[:, 128:]
    s16 = e16 + sh(e16, 8)
    return jnp.concatenate([s2[:, :128], s4[:, :128], s8[:, :128], s16], axis=1)


def _fill_block_diagonal(dst_ref, blocks_ref):
    n, r, c = blocks_ref.shape
    dst_ref[...] = jnp.zeros_like(dst_ref)
    for b in range(n):
        dst_ref[b * r:(b + 1) * r, b * c:(b + 1) * c] = blocks_ref[b]


def _row_counts(first_row, ts):
    t1 = (first_row + lax.broadcasted_iota(jnp.int32, (ts, 1), 0) + 1).astype(F32)
    return jnp.minimum(t1, _window_lane_width())


def _fwd_in(x, mod, g_mix, w_in, g_q, g_kv, w_uq, wuk_dc, perm, cos4, sin4, csk, snk, w_pool, pool_scale):
    S = x.shape[0]
    ts = 512
    nsub = ts // TQ

    def body(x_ref, mod_ref, gmix_ref, win_ref, gq_ref, gkv_ref, wuq_ref, wuk_ref, perm_ref, cos_ref, sin_ref,
             csk_ref, snk_ref, wpool_ref, pscale_ref,
             h1_ref, raw_ref, qn_ref, qs_ref, kv_ref, kvt_ref, pooled_ref, ypre_ref, ypool_ref, carry_ref, wuk_bd,
             wpool_bd):
        i = pl.program_id(0)

        @pl.when(i == 0)
        def _():
            carry_ref[...] = jnp.zeros_like(carry_ref)
            _fill_block_diagonal(wuk_bd, wuk_ref)
            _fill_block_diagonal(wpool_bd, wpool_ref)

        xv = x_ref[...]
        sh1 = mod_ref[0:1, 0:D]
        sc1 = mod_ref[0:1, D:2 * D]
        h = (xv * _rms(xv)) * gmix_ref[...] * (1.0 + sc1) + sh1
        hb = h.astype(BF)
        h1_ref[...] = hb
        proj = _dot(hb, win_ref[...])
        cq_raw = proj[:, 0:QL]
        ckv_raw = proj[:, QL:QL + KVL]
        kr = proj[:, 384:512]
        u = proj[:, 512:1024]
        raw_ref[...] = proj[:, 0:384]

        c_q = (cq_raw * _rms(cq_raw)) * gq_ref[...]
        c_kv = (ckv_raw * _rms(ckv_raw)) * gkv_ref[...]
        q = _dot(c_q.astype(BF), wuq_ref[...])
        qn = q[:, 0:HEADS * NOPE].astype(BF)
        qn_ref[...] = qn
        x1 = q[:, 512:640]
        x2 = q[:, 640:768]
        cosv = cos_ref[...]
        sinv = sin_ref[...]
        roped = jnp.concatenate([x1 * cosv - x2 * sinv, x1 * sinv + x2 * cosv], axis=1).astype(BF)
        q_lat = _dot(qn, wuk_bd[...])
        q_rope = _dot(roped, perm_ref[...])
        for hd in range(HEADS):
            cols = slice(hd * 128, (hd + 1) * 128)
            qh = jnp.concatenate([q_lat[:, cols], q_rope[:, cols]], axis=1).astype(BF)
            for a in range(nsub):
                qs_ref[a, hd * TQ:(hd + 1) * TQ, :] = qh[a * TQ:(a + 1) * TQ, :]
        k_rope = kr * csk_ref[...] + _swap_halves(kr) * snk_ref[...]
        keys = jnp.concatenate([c_kv, k_rope], axis=1)
        kv_ref[...] = keys.astype(BF)
        for a in range(ts // TK):
            kvt_ref[a] = keys[a * TK:(a + 1) * TK, :].T.astype(BF)

        ext = jnp.concatenate([carry_ref[...], u], axis=0)
        win = _window_sums(ext, True)[16:, :]
        pooled = (win / _row_counts(i * ts, ts) - u).astype(BF)
        pooled_ref[...] = pooled
        carry_ref[...] = u[ts - 16:ts, :]
        ypre = _dot(pooled, wpool_bd[...])
        ypre_ref[...] = ypre
        ypool_ref[...] = (ypre * pscale_ref[...]).astype(BF)

    out_shape = (
        jax.ShapeDtypeStruct((S, D), BF),
        jax.ShapeDtypeStruct((S, 384), F32),
        jax.ShapeDtypeStruct((S, HEADS * NOPE), BF),
        jax.ShapeDtypeStruct((S // TQ, HEADS * TQ, QW), BF),
        jax.ShapeDtypeStruct((S, QW), BF),
        jax.ShapeDtypeStruct((S // TK, QW, TK), BF),
        jax.ShapeDtypeStruct((S, PW), BF),
        jax.ShapeDtypeStruct((S, PW), F32),
        jax.ShapeDtypeStruct((S, PW), BF),
    )
    in_specs = [
        _rows(ts, D), _full(mod.shape), _full((1, D)), _full(w_in.shape), _full((1, QL)), _full((1, KVL)),
        _full(w_uq.shape), _full(wuk_dc.shape), _full(perm.shape), _rows(ts, 128), _rows(ts, 128), _rows(ts, 128),
        _rows(ts, 128), _full(w_pool.shape), _full((1, PW)),
    ]
    out_specs = (
        _rows(ts, D), _rows(ts, 384), _rows(ts, HEADS * NOPE),
        pl.BlockSpec((nsub, HEADS * TQ, QW), lambda i: (i, 0, 0)),
        _rows(ts, QW), pl.BlockSpec((ts // TK, QW, TK), lambda i: (i, 0, 0)), _rows(ts, PW), _rows(ts, PW),
        _rows(ts, PW),
    )
    return pl.pallas_call(
        body, name="fwd_in", out_shape=out_shape, grid=(S // ts,), in_specs=in_specs, out_specs=out_specs,
        scratch_shapes=[pltpu.VMEM((16, PW), F32), pltpu.VMEM((HEADS * NOPE, HEADS * KVL), BF),
                        pltpu.VMEM((PW, PW), BF)],
        compiler_params=_params(("arbitrary",)),
    )(x, mod, g_mix, w_in, g_q, g_kv, w_uq, wuk_dc, perm, cos4, sin4, csk, snk, w_pool, pool_scale)


def _diag_mask(shape, q_axis, first_chunk):
    qi = (lax.broadcasted_iota(jnp.int32, shape, q_axis) & (TQ - 1)) >> 6
    ki = (lax.broadcasted_iota(jnp.int32, shape, 1 - q_axis) >> 6) + first_chunk
    return ki <= qi


def _attn_fwd(qs, kv, kvt, wuv_vc):
    nq = qs.shape[0]
    S = kv.shape[0]
    M = HEADS * TQ

    def body(qs_ref, kv_ref, kvt_ref, wuv_ref, olat_ref, ymla_ref, lse_ref):
        i = pl.program_id(0)
        q = qs_ref[0]

        def step(kt, carry, first_chunk=None):
            m, l, acc = carry
            k = kv_ref[pl.ds(pl.multiple_of(kt * TK, TK), TK), :]
            v_t = kvt_ref[kt][0:KVL, :]
            s = _dot_nt(k, q)
            if first_chunk is not None:
                s = jnp.where(_diag_mask((TK, M), 1, first_chunk), s, -jnp.inf)
            m_new = jnp.maximum(m, jnp.max(s, axis=0, keepdims=True))
            alpha = jnp.exp2((m - m_new) * EXP2_SCALE)
            p = jnp.exp2((s - m_new) * EXP2_SCALE)
            l = alpha * l + jnp.sum(p, axis=0, keepdims=True)
            acc = alpha * acc + _dot(v_t, p.astype(BF))
            return m_new, l, acc

        init = (jnp.full((1, M), -jnp.inf, F32), jnp.zeros((1, M), F32), jnp.zeros((KVL, M), F32))
        per = TQ // TK
        carry = lax.fori_loop(0, per * i, step, init)
        for j in range(per):
            carry = step(per * i + j, carry, j * (TK // 64))
        m, l, acc = carry
        o_lat = acc / l
        olat_ref[0] = o_lat
        lse_ref[0] = jnp.broadcast_to(m * SM_SCALE + jnp.log(l), (8, M))
        for hd in range(HEADS):
            o_t = _dot(wuv_ref[hd], o_lat[:, hd * TQ:(hd + 1) * TQ].astype(BF))
            ymla_ref[:, hd * 128:(hd + 1) * 128] = o_t.T.astype(BF)

    out_shape = (
        jax.ShapeDtypeStruct((nq, KVL, M), F32),
        jax.ShapeDtypeStruct((S, HEADS * 128), BF),
        jax.ShapeDtypeStruct((nq, 8, M), F32),
    )
    return pl.pallas_call(
        body, name="attn_fwd", out_shape=out_shape, grid=(nq,),
        in_specs=[pl.BlockSpec((1, M, QW), lambda i: (i, 0, 0)), _full(kv.shape), _full(kvt.shape),
                  _full(wuv_vc.shape)],
        out_specs=(pl.BlockSpec((1, KVL, M), lambda i: (i, 0, 0)), _rows(TQ, HEADS * 128),
                   pl.BlockSpec((1, 8, M), lambda i: (i, 0, 0))),
        compiler_params=_params(("arbitrary",)),
    )(qs, kv, kvt, wuv_vc)


def _silu_parts(a):
    sg = jax.nn.sigmoid(a)
    return sg, a * sg


def _ffn_fwd(x, ymla, ypool, mod, w_o, g_ffn, wg_t, wu_t, wd, g_final, target):
    S = x.shape[0]
    ts = 256

    def body(x_ref, ymla_ref, ypool_ref, mod_ref, wo_ref, gffn_ref, wg_ref, wu_ref, wd_ref, gfin_ref, t_ref,
             x2_ref, mix_ref, h2t_ref, a_ref, b_ref, dx3_ref, dff_ref, dfft_ref, loss_ref, dgfin_ref, dgt2_ref,
             f_ref):
        i = pl.program_id(0)

        @pl.when(i == 0)
        def _():
            loss_ref[...] = jnp.zeros_like(loss_ref)
            dgfin_ref[...] = jnp.zeros_like(dgfin_ref)
            dgt2_ref[...] = jnp.zeros_like(dgt2_ref)

        gt1 = mod_ref[0:1, 2 * D:3 * D]
        sh2 = mod_ref[0:1, 3 * D:4 * D]
        sc2 = mod_ref[0:1, 4 * D:5 * D]
        gt2 = mod_ref[0:1, 5 * D:6 * D]
        cat = jnp.concatenate([ymla_ref[...], ypool_ref[...]], axis=1)
        mix = _dot(cat, wo_ref[...])
        mix_ref[...] = mix
        x2 = x_ref[...] + gt1 * mix
        x2_ref[...] = x2
        h2 = (x2 * _rms(x2)) * gffn_ref[...] * (1.0 + sc2) + sh2
        h2b = h2.astype(BF)
        h2t_ref[...] = h2.T.astype(BF)

        for c in range(FF // FCHUNK):
            cols = slice(c * FCHUNK, (c + 1) * FCHUNK)
            a = _dot_nt(h2b, wg_ref[cols, :])
            b = _dot_nt(h2b, wu_ref[cols, :])
            a_ref[:, cols] = a.astype(BF)
            b_ref[:, cols] = b.astype(BF)
            f_ref[:, cols] = (_silu_parts(a)[1] * b).astype(BF)
        ff = _dot(f_ref[...], wd_ref[...])

        x3 = x2 + gt2 * ff
        r3 = _rms(x3)
        xn3 = x3 * r3
        gfin = gfin_ref[...]
        e = xn3 * gfin - t_ref[...]
        loss_ref[...] += 0.5 * jnp.sum(jnp.mean(e * e, axis=-1, keepdims=True))
        dy = e * (1.0 / D)
        dgfin_ref[...] += _colsum(dy * xn3)
        dx3 = _rms_bwd(dy * gfin, xn3, r3)
        dx3_ref[...] = dx3
        dgt2_ref[...] += _colsum(dx3 * ff)
        dff = dx3 * gt2
        dff_ref[...] = dff.astype(BF)
        dfft_ref[...] = dff.T.astype(BF)

    row = lambda c: _rows(ts, c)
    col = pl.BlockSpec((D, ts), lambda i: (0, i))
    const = _full
    out_shape = (
        jax.ShapeDtypeStruct((S, D), F32),
        jax.ShapeDtypeStruct((S, D), F32),
        jax.ShapeDtypeStruct((D, S), BF),
        jax.ShapeDtypeStruct((S, FF), BF),
        jax.ShapeDtypeStruct((S, FF), BF),
        jax.ShapeDtypeStruct((S, D), F32),
        jax.ShapeDtypeStruct((S, D), BF),
        jax.ShapeDtypeStruct((D, S), BF),
        jax.ShapeDtypeStruct((8, 128), F32),
        jax.ShapeDtypeStruct((1, D), F32),
        jax.ShapeDtypeStruct((1, D), F32),
    )
    return pl.pallas_call(
        body, name="ffn_fwd", out_shape=out_shape, grid=(S // ts,),
        in_specs=[row(D), row(PW), row(PW), const(mod.shape), _vmem(), const((1, D)), _vmem(), _vmem(), _vmem(),
                  const((1, D)), row(D)],
        out_specs=(row(D), row(D), col, row(FF), row(FF), row(D), row(D), col, const((8, 128)), const((1, D)),
                   const((1, D))),
        scratch_shapes=[pltpu.VMEM((ts, FF), BF)],
        compiler_params=_params(("arbitrary",)),
    )(x, ymla, ypool, mod, w_o, g_ffn, wg_t, wu_t, wd, g_final, target)


FCHUNK = 256


def _ffn_bwd_acts(dff, a, b, wg_t, wu_t, wd):
    S = dff.shape[0]
    ts = 512

    def body(dff_ref, a_ref, b_ref, wg_ref, wu_ref, wd_ref, da_ref, db_ref, dh2_ref):
        dffb = dff_ref[...]
        for c in range(FF // FCHUNK):
            cols = slice(c * FCHUNK, (c + 1) * FCHUNK)
            df = _dot_nt(dffb, wd_ref[cols, :])
            av = a_ref[:, cols].astype(F32)
            bv = b_ref[:, cols].astype(F32)
            sg, sa = _silu_parts(av)
            db_ref[:, cols] = (df * sa).astype(BF)
            da_ref[:, cols] = (df * bv * (sg * (1.0 + av * (1.0 - sg)))).astype(BF)
        dh2_ref[...] = _dot(da_ref[...], wg_ref[...]) + _dot(db_ref[...], wu_ref[...])

    act = _rows(ts, FF)
    return pl.pallas_call(
        body, name="ffn_bwd_acts",
        out_shape=(jax.ShapeDtypeStruct((S, FF), BF), jax.ShapeDtypeStruct((S, FF), BF),
                   jax.ShapeDtypeStruct((S, D), F32)),
        grid=(S // ts,), in_specs=[_rows(ts, D), act, act, _vmem(), _vmem(), _vmem()],
        out_specs=(act, act, _rows(ts, D)), compiler_params=_params(("arbitrary",)),
    )(dff, a, b, wg_t, wu_t, wd)


def _ffn_bwd_weights(dff_t, h2_t, da, db, a, b):
    S = da.shape[0]

    def body(dfft_ref, h2t_ref, da_ref, db_ref, a_ref, b_ref, dwg_ref, dwu_ref, dwd_ref):
        h2t = h2t_ref[...]
        dwg_ref[...] = _dot(h2t, da_ref[...]).T.astype(BF)
        dwu_ref[...] = _dot(h2t, db_ref[...]).T.astype(BF)
        f = (_silu_parts(a_ref[...].astype(F32))[1] * b_ref[...].astype(F32)).astype(BF)
        dwd_ref[...] = _dot(dfft_ref[...], f).T.astype(BF)

    act = pl.BlockSpec((S, FCHUNK), lambda j: (0, j))
    wblk = _rows(FCHUNK, D)
    shp = jax.ShapeDtypeStruct((FF, D), BF)
    return pl.pallas_call(
        body, name="ffn_bwd_weights", out_shape=(shp, shp, shp), grid=(FF // FCHUNK,),
        in_specs=[_vmem(), _vmem(), act, act, act, act], out_specs=(wblk, wblk, wblk),
        compiler_params=_params(("arbitrary",)),
    )(dff_t, h2_t, da, db, a, b)


def _mix_bwd(dh2, dx3, x2, mix, mod, g_ffn, ymla, ypool, w_o, ypre, pooled, pool_scale, wpool_dc, olat, wuv_vc):
    S = dh2.shape[0]
    ts = 512
    n = S // ts
    nsub = ts // TQ
    M = HEADS * TQ

    def body(dh2_ref, dx3_ref, x2_ref, mix_ref, mod_ref, gffn_ref, ymla_ref, ypool_ref, wo_ref, ypre_ref, pooled_ref,
             pscale_ref, wpool_ref, olat_ref, wuv_ref,
             dx2_ref, du_ref, dolat_ref, delta_ref, dwo_ref, dwuv_ref, dwpool_ref, dpscale_ref, dgt1_ref, dsc2_ref,
             dsh2_ref, dgffn_ref, carry_ref, dwo_acc, dwpool_acc, wpool_bd, wuv_bd):
        i = pl.program_id(0)

        @pl.when(i == 0)
        def _():
            carry_ref[...] = jnp.zeros_like(carry_ref)
            dwo_acc[...] = jnp.zeros_like(dwo_acc)
            dwpool_acc[...] = jnp.zeros_like(dwpool_acc)
            _fill_block_diagonal(wpool_bd, wpool_ref)
            _fill_block_diagonal(wuv_bd, wuv_ref)
            for r in (dwuv_ref, dpscale_ref, dgt1_ref, dsc2_ref, dsh2_ref, dgffn_ref):
                r[...] = jnp.zeros_like(r)

        gt1 = mod_ref[0:1, 2 * D:3 * D]
        sc2 = mod_ref[0:1, 4 * D:5 * D]
        gffn = gffn_ref[...]
        dh2 = dh2_ref[...]
        x2 = x2_ref[...]
        r2 = _rms(x2)
        xn2 = x2 * r2
        along = _colsum(dh2 * xn2)
        dsc2_ref[...] += along * gffn
        dsh2_ref[...] += _colsum(dh2)
        dgffn_ref[...] += along * (1.0 + sc2)
        dx2 = dx3_ref[...] + _rms_bwd(dh2 * (gffn * (1.0 + sc2)), xn2, r2)
        dx2_ref[...] = dx2
        dgt1_ref[...] += _colsum(dx2 * mix_ref[...])
        dmix = (dx2 * gt1).astype(BF)
        cat = jnp.concatenate([ymla_ref[...], ypool_ref[...]], axis=1)
        dwo_acc[...] += _dot_tn(cat, dmix)
        dcat = _dot_nt(dmix, wo_ref[...])
        dymla = dcat[:, 0:512]
        dypool = dcat[:, 512:1024]

        dpscale_ref[...] += _colsum(dypool * ypre_ref[...])
        dypre = (dypool * pscale_ref[...]).astype(BF)
        dwpool_acc[...] += _dot_tn(pooled_ref[...], dypre)
        dpooled = _dot(dypre, wpool_bd[...])
        tile = n - 1 - i
        e = dpooled / _row_counts(tile * ts, ts)
        ext = jnp.concatenate([e, carry_ref[...]], axis=0)
        du_ref[...] = _window_sums(ext, False)[0:ts, :] - dpooled
        carry_ref[...] = e[0:16, :]

        dob_all = dymla.astype(BF)
        dol_all = _dot(dob_all, wuv_bd[...])
        for hd in range(HEADS):
            dob = dob_all[:, hd * 128:(hd + 1) * 128]
            dol = dol_all[:, hd * 128:(hd + 1) * 128]
            for a in range(nsub):
                ol_t = olat_ref[a, :, hd * TQ:(hd + 1) * TQ]
                dl = dol[a * TQ:(a + 1) * TQ, :]
                dolat_ref[a, hd * TQ:(hd + 1) * TQ, :] = dl.astype(BF)
                dwuv_ref[hd] += _dot(ol_t.astype(BF), dob[a * TQ:(a + 1) * TQ, :])
                delta = jnp.sum(dl * ol_t.T, axis=-1, keepdims=True)
                delta_ref[a, :, hd * TQ:(hd + 1) * TQ] = jnp.broadcast_to(delta, (TQ, 128)).T[0:8, :]

        @pl.when(i == n - 1)
        def _():
            dwo_ref[...] = dwo_acc[...].astype(BF)
            for g in range(GROUPS):
                dwpool_ref[g] = dwpool_acc[g * GD:(g + 1) * GD, g * GD:(g + 1) * GD]

    rev = lambda c: pl.BlockSpec((ts, c), lambda i: (n - 1 - i, 0))
    rev3 = lambda r, c: pl.BlockSpec((nsub, r, c), lambda i: (n - 1 - i, 0, 0))
    out_shape = (
        jax.ShapeDtypeStruct((S, D), F32),
        jax.ShapeDtypeStruct((S, PW), F32),
        jax.ShapeDtypeStruct((S // TQ, M, KVL), BF),
        jax.ShapeDtypeStruct((S // TQ, 8, M), F32),
        jax.ShapeDtypeStruct((D, D), BF),
        jax.ShapeDtypeStruct((HEADS, KVL, 128), F32),
        jax.ShapeDtypeStruct((GROUPS, GD, GD), F32),
        jax.ShapeDtypeStruct((1, PW), F32),
        jax.ShapeDtypeStruct((1, D), F32), jax.ShapeDtypeStruct((1, D), F32), jax.ShapeDtypeStruct((1, D), F32),
        jax.ShapeDtypeStruct((1, D), F32),
    )
    in_specs = [rev(D), rev(D), rev(D), rev(D), _full(mod.shape), _full((1, D)), rev(PW), rev(PW), _full(w_o.shape),
                rev(PW), rev(PW), _full((1, PW)), _full(wpool_dc.shape), rev3(KVL, M), _full(wuv_vc.shape)]
    out_specs = (rev(D), rev(PW), rev3(M, KVL), rev3(8, M), _full((D, D)), _full((HEADS, KVL, 128)),
                 _full((GROUPS, GD, GD)), _full((1, PW)), _full((1, D)), _full((1, D)), _full((1, D)), _full((1, D)))
    return pl.pallas_call(
        body, name="mix_bwd", out_shape=out_shape, grid=(n,), in_specs=in_specs, out_specs=out_specs,
        scratch_shapes=[pltpu.VMEM((16, PW), F32), pltpu.VMEM((D, D), F32), pltpu.VMEM((PW, PW), F32),
                        pltpu.VMEM((PW, PW), BF), pltpu.VMEM((HEADS * 128, HEADS * KVL), BF)],
        compiler_params=_params(("arbitrary",)),
    )(dh2, dx3, x2, mix, mod, g_ffn, ymla, ypool, w_o, ypre, pooled, pool_scale, wpool_dc, olat, wuv_vc)


def _attn_bwd(qs, kv, dolat, lse, delta):
    nq = qs.shape[0]
    S = kv.shape[0]
    M = HEADS * TQ
    nk = S // TK

    def body(qs_ref, kv_ref, do_ref, lse_ref, delta_ref, dkv_ref, dqt_ref, p_ref, ds_ref):
        kt = pl.program_id(0)
        k = kv_ref[...]
        v = k[:, 0:KVL]
        k_t = k.astype(F32).T.astype(BF)

        @pl.when(kt == 0)
        def _():
            dqt_ref[...] = jnp.zeros_like(dqt_ref)

        def step(qi, carry, first_chunk=None):
            dk, dv = carry
            q = qs_ref[qi]
            do = do_ref[qi]
            s = _dot_nt(k, q)
            dp = _dot_nt(v, do)
            lse_row = lse_ref[qi, 0:1, :] * LOG2_E
            delta_row = delta_ref[qi, 0:1, :]
            q_chunk = (lax.broadcasted_iota(jnp.int32, (1, M), 1) & (TQ - 1)) >> 6
            for r in range(0, TK, VPU_ROWS):
                rows = slice(r, r + VPU_ROWS)
                p = jnp.exp2(s[rows, :] * EXP2_SCALE - lse_row)
                if first_chunk is not None:
                    p = jnp.where((r >> 6) + first_chunk <= q_chunk, p, 0.0)
                p_ref[rows, :] = p.astype(BF)
                ds_ref[rows, :] = (p * (dp[rows, :] - delta_row) * SM_SCALE).astype(BF)
            ds = ds_ref[...]
            dv = dv + _dot(p_ref[...], do)
            dk = dk + _dot(ds, q)
            dqt_ref[qi] += _dot(k_t, ds)
            return dk, dv

        per = TQ // TK
        first = kt // per
        carry = step(first, (jnp.zeros((TK, QW), F32), jnp.zeros((TK, KVL), F32)), (kt % per) * (TK // 64))
        dk, dv = lax.fori_loop(first + 1, nq, step, carry)
        dkv_ref[...] = dk + jnp.concatenate([dv, jnp.zeros((TK, QW - KVL), F32)], axis=1)

    out_shape = (jax.ShapeDtypeStruct((S, QW), F32), jax.ShapeDtypeStruct((nq, QW, M), F32))
    return pl.pallas_call(
        body, name="attn_bwd", out_shape=out_shape, grid=(nk,),
        in_specs=[_vmem(), _rows(TK, QW), _vmem(), _vmem(), _vmem()],
        out_specs=(_rows(TK, QW), _vmem()),
        scratch_shapes=[pltpu.VMEM((TK, M), BF), pltpu.VMEM((TK, M), BF)],
        compiler_params=_params(("arbitrary",)),
    )(qs, kv, dolat, lse, delta)


def _in_bwd(dqt, dkv, du, raw, qn, h1, x, dx2, mod, g_mix, w_in, g_q, g_kv, w_uq, wuk_cd, perm_t, cos4, sin4, csk,
            snk):
    S = x.shape[0]
    ts = 512
    n = S // ts
    nsub = ts // TQ
    M = HEADS * TQ

    def body(dqt_ref, dkv_ref, du_ref, raw_ref, qn_ref, h1_ref, x_ref, dx2_ref, mod_ref, gmix_ref, win_ref, gq_ref,
             gkv_ref, wuq_ref, wuk_ref, permt_ref, cos_ref, sin_ref, csk_ref, snk_ref,
             dx_ref, dwin_ref, dwuq_ref, dwuk_ref, dgq_ref, dgkv_ref, dsc1_ref, dsh1_ref, dgmix_ref, dwin_acc,
             dwuq_acc, dwuk_acc, wuk_bd):
        i = pl.program_id(0)

        @pl.when(i == 0)
        def _():
            dwin_acc[...] = jnp.zeros_like(dwin_acc)
            dwuq_acc[...] = jnp.zeros_like(dwuq_acc)
            dwuk_acc[...] = jnp.zeros_like(dwuk_acc)
            _fill_block_diagonal(wuk_bd, wuk_ref)
            for r in (dgq_ref, dgkv_ref, dsc1_ref, dsh1_ref, dgmix_ref):
                r[...] = jnp.zeros_like(r)

        dq_blocks = [dqt_ref[a].T for a in range(nsub)]
        dq_heads = [jnp.concatenate([blk[hd * TQ:(hd + 1) * TQ, :] for blk in dq_blocks], axis=0)
                    for hd in range(HEADS)]
        dq_lat = jnp.concatenate([dqh[:, 0:KVL] for dqh in dq_heads], axis=1).astype(BF)
        dq_rope = jnp.concatenate([dqh[:, KVL:QW] for dqh in dq_heads], axis=1).astype(BF)
        dq_nope = _dot(dq_lat, wuk_bd[...])
        dwuk_acc[...] += _dot_tn(dq_lat, qn_ref[...])
        drope = _dot(dq_rope, permt_ref[...])
        do1 = drope[:, 0:128]
        do2 = drope[:, 128:256]
        cosv = cos_ref[...]
        sinv = sin_ref[...]
        dq = jnp.concatenate([dq_nope, do1 * cosv + do2 * sinv, do2 * cosv - do1 * sinv], axis=1).astype(BF)

        cq_raw = raw_ref[:, 0:QL]
        ckv_raw = raw_ref[:, QL:QL + KVL]
        rq = _rms(cq_raw)
        nq_ = cq_raw * rq
        gq = gq_ref[...]
        dwuq_acc[...] += _dot_tn((nq_ * gq).astype(BF), dq)
        dc_q = _dot_nt(dq, wuq_ref[...])
        dgq_ref[...] += _colsum(dc_q * nq_)
        dcq_raw = _rms_bwd(dc_q * gq, nq_, rq)

        dkv = dkv_ref[...]
        rk = _rms(ckv_raw)
        nk_ = ckv_raw * rk
        dc_kv = dkv[:, 0:KVL]
        dgkv_ref[...] += _colsum(dc_kv * nk_)
        dckv_raw = _rms_bwd(dc_kv * gkv_ref[...], nk_, rk)
        dkr_roped = dkv[:, KVL:QW]
        dkr = dkr_roped * csk_ref[...] - _swap_halves(dkr_roped) * snk_ref[...]

        dproj = jnp.concatenate([dcq_raw, dckv_raw, dkr, du_ref[...]], axis=1).astype(BF)
        dwin_acc[...] += _dot_tn(h1_ref[...], dproj)
        dh1 = _dot_nt(dproj, win_ref[...])

        sc1 = mod_ref[0:1, D:2 * D]
        gmix = gmix_ref[...]
        xv = x_ref[...]
        r1 = _rms(xv)
        xn1 = xv * r1
        along = _colsum(dh1 * xn1)
        dsc1_ref[...] += along * gmix
        dsh1_ref[...] += _colsum(dh1)
        dgmix_ref[...] += along * (1.0 + sc1)
        dx_ref[...] = dx2_ref[...] + _rms_bwd(dh1 * (gmix * (1.0 + sc1)), xn1, r1)

        @pl.when(i == n - 1)
        def _():
            dwin_ref[...] = dwin_acc[...].astype(BF)
            dwuq_ref[...] = dwuq_acc[...].astype(BF)
            for hd in range(HEADS):
                dwuk_ref[hd] = dwuk_acc[hd * KVL:(hd + 1) * KVL, hd * NOPE:(hd + 1) * NOPE]

    out_shape = (
        jax.ShapeDtypeStruct((S, D), F32),
        jax.ShapeDtypeStruct((D, D), BF),
        jax.ShapeDtypeStruct((QL, 768), BF),
        jax.ShapeDtypeStruct((HEADS, KVL, NOPE), F32),
        jax.ShapeDtypeStruct((1, QL), F32), jax.ShapeDtypeStruct((1, KVL), F32),
        jax.ShapeDtypeStruct((1, D), F32), jax.ShapeDtypeStruct((1, D), F32), jax.ShapeDtypeStruct((1, D), F32),
    )
    in_specs = [pl.BlockSpec((nsub, QW, M), lambda i: (i, 0, 0)), _rows(ts, QW), _rows(ts, PW), _rows(ts, 384),
                _rows(ts, HEADS * NOPE), _rows(ts, D), _rows(ts, D), _rows(ts, D), _full(mod.shape), _full((1, D)),
                _full(w_in.shape), _full((1, QL)), _full((1, KVL)), _full(w_uq.shape), _full(wuk_cd.shape),
                _full(perm_t.shape), _rows(ts, 128), _rows(ts, 128), _rows(ts, 128), _rows(ts, 128)]
    out_specs = (_rows(ts, D), _full((D, D)), _full((QL, 768)), _full((HEADS, KVL, NOPE)), _full((1, QL)),
                 _full((1, KVL)), _full((1, D)), _full((1, D)), _full((1, D)))
    return pl.pallas_call(
        body, name="in_bwd", out_shape=out_shape, grid=(n,), in_specs=in_specs, out_specs=out_specs,
        scratch_shapes=[pltpu.VMEM((D, D), F32), pltpu.VMEM((QL, 768), F32),
                        pltpu.VMEM((HEADS * KVL, HEADS * NOPE), F32), pltpu.VMEM((HEADS * KVL, HEADS * NOPE), BF)],
        compiler_params=_params(("arbitrary",)),
    )(dqt, dkv, du, raw, qn, h1, x, dx2, mod, g_mix, w_in, g_q, g_kv, w_uq, wuk_cd, perm_t, cos4, sin4, csk, snk)


def _rope_perm():
    p = np.zeros((HEADS, 2 * 128, 128), np.float32)
    for hd in range(HEADS):
        for t in range(HALF):
            p[hd, hd * HALF + t, t] = 1.0
            p[hd, 128 + hd * HALF + t, HALF + t] = 1.0
    return p


def _rope_tables(positions):
    freqs = jnp.power(ROPE_THETA, -jnp.arange(HALF, dtype=F32) / HALF)
    ang = positions.astype(F32)[:, None] * jnp.tile(freqs, HEADS)[None, :]
    cos4 = jnp.cos(ang)
    sin4 = jnp.sin(ang)
    lane = jnp.arange(HEADS * HALF)[None, :]
    csk = jnp.where(lane < ROPE, cos4, 0.0)
    snk = jnp.where(lane < HALF, -sin4, jnp.where(lane < ROPE, sin4, 0.0))
    return cos4, sin4, csk, snk


def _local_step(x, rope, target, mod, g_mix, w_in_p, g_q, g_kv, w_uq_p, w_uk, w_uv, w_pool, pool_scale, g_ffn,
                g_final, late, ffn_grads_exchange):
    perm = jnp.asarray(_rope_perm().transpose(1, 0, 2).reshape(2 * 128, HEADS * 128), BF)
    perm_t = jnp.asarray(_rope_perm().transpose(0, 2, 1).reshape(HEADS * 128, 2 * 128), BF)
    cos4, sin4, csk, snk = rope
    wuk_dc = w_uk.transpose(1, 2, 0).astype(BF)
    wuk_cd = w_uk.transpose(1, 0, 2).astype(BF)
    wuv_vc = w_uv.transpose(1, 2, 0).astype(BF)
    wpool = w_pool.astype(BF)
    wpool_dc = w_pool.transpose(0, 2, 1).astype(BF)

    h1, raw, qn, qs, kv, kvt, pooled, ypre, ypool = _fwd_in(
        x, mod, g_mix, w_in_p, g_q, g_kv, w_uq_p, wuk_dc, perm, cos4, sin4, csk, snk, wpool, pool_scale)
    olat, ymla, lse = _attn_fwd(qs, kv, kvt, wuv_vc)
    w_o, wg_t, wu_t, wd = late
    x2, mix, h2_t, a, b, dx3, dff, dff_t, loss, dgfin, dgt2 = _ffn_fwd(
        x, ymla, ypool, mod, w_o, g_ffn, wg_t, wu_t, wd, g_final, target)
    da, db, dh2 = _ffn_bwd_acts(dff, a, b, wg_t, wu_t, wd)
    (dx2, du, dolat, delta, dwo, dwuv, dwpool, dpscale, dgt1, dsc2, dsh2, dgffn) = _mix_bwd(
        dh2, dx3, x2, mix, mod, g_ffn, ymla, ypool, w_o, ypre, pooled, pool_scale, wpool_dc, olat, wuv_vc)
    dwg_t, dwu_t, dwd = _ffn_bwd_weights(dff_t, h2_t, da, db, a, b)
    ffn_parts = ffn_grads_exchange((dwg_t, dwu_t, dwd, dwo))
    dkv, dqt = _attn_bwd(qs, kv, dolat, lse, delta)
    dx, dwin, dwuq, dwuk, dgq, dgkv, dsc1, dsh1, dgmix = _in_bwd(
        dqt, dkv, du, raw, qn, h1, x, dx2, mod, g_mix, w_in_p, g_q, g_kv, w_uq_p, wuk_cd, perm_t, cos4, sin4, csk,
        snk)
    dmod = jnp.concatenate([dsh1, dsc1, dgt1, dsh2, dsc2, dgt2], axis=1)
    replicated = dict(
        w_uk=dwuk.transpose(1, 0, 2), w_uv=dwuv.transpose(1, 0, 2), w_pool=dwpool, g_mix=dgmix, g_q=dgq, g_kv=dgkv,
        pool_scale=dpscale, g_ffn=dgffn, g_final=dgfin)
    return loss[0, 0], dx, dmod, (dwin, dwuq), ffn_parts, replicated


def _my_pos():
    return lax.axis_index("x"), lax.axis_index("y"), lax.axis_index("c")


def _peer(pos, k):
    x, y, c = pos
    return (1 - x if k & 4 else x, 1 - y if k & 2 else y, 1 - c if k & 1 else c)


def _index(pos):
    x, y, c = pos
    return 4 * x + 2 * y + c


def _remote(src, dst, send_sem, recv_sem, to):
    return pltpu.make_async_remote_copy(src_ref=src, dst_ref=dst, send_sem=send_sem, recv_sem=recv_sem,
                                        device_id=to, device_id_type=MESH)


def _ada_mod(c, w_ada, b_ada, after):
    def body(c_ref, w_ref, b_ref, after_ref, mod_ref, call_ref, cbuf, sbuf, rbuf, send1, recv1, send2, recv2):
        me = _my_pos()
        mi = _index(me)
        cv = c_ref[...]
        cbuf[...] = jnp.broadcast_to(cv * jax.nn.sigmoid(cv), (8, D))
        call_ref[mi] = cbuf[...]
        first = [_remote(cbuf, call_ref.at[mi], send1.at[k - 1], recv1.at[k - 1], _peer(me, k)) for k in range(1, NDEV)]
        for cp in first:
            cp.start()
        for k in range(1, NDEV):
            _remote(cbuf, call_ref.at[_index(_peer(me, k))], send1.at[k - 1], recv1.at[k - 1], _peer(me, k)).wait_recv()
        c_all = jnp.concatenate([call_ref[b][0:1, :] for b in range(NDEV)], axis=0)
        blocks = _dot(c_all.astype(BF), w_ref[...].astype(BF))
        for b in range(NDEV):
            sbuf[b] = jnp.broadcast_to(blocks[b:b + 1, :], (8, MODC))
        second = []
        for k in range(1, NDEV):
            to = _peer(me, k)
            second.append(_remote(sbuf.at[_index(to)], rbuf.at[mi], send2.at[k - 1], recv2.at[k - 1], to))
        for cp in second:
            cp.start()
        rbuf[mi] = sbuf[mi]
        for k in range(1, NDEV):
            to = _peer(me, k)
            _remote(sbuf.at[_index(to)], rbuf.at[_index(to)], send2.at[k - 1], recv2.at[k - 1], to).wait_recv()
        for j in range(NDEV):
            mod_ref[:, j * MODC:(j + 1) * MODC] = rbuf[j] + b_ref[:, j * MODC:(j + 1) * MODC]
        for cp in first + second:
            cp.wait_send()

    return pl.pallas_call(
        body, name="ada_mod",
        out_shape=(jax.ShapeDtypeStruct((8, N_MOD * D), F32), jax.ShapeDtypeStruct((NDEV, 8, D), F32)),
        in_specs=[_vmem(), _vmem(), _vmem(), _any()], out_specs=(_vmem(), _vmem()),
        scratch_shapes=[pltpu.VMEM((8, D), F32), pltpu.VMEM((NDEV, 8, MODC), F32), pltpu.VMEM((NDEV, 8, MODC), F32),
                        pltpu.SemaphoreType.DMA((NDEV - 1,)), pltpu.SemaphoreType.DMA((NDEV - 1,)),
                        pltpu.SemaphoreType.DMA((NDEV - 1,)), pltpu.SemaphoreType.DMA((NDEV - 1,))],
        compiler_params=_params(),
    )(c, w_ada, b_ada, after)


def _sequencer_scatter(name, collective_id, srcs, after=()):
    n = len(srcs)

    def of(src, to_index):
        r = src.shape[0] // NDEV
        return src.at[pl.ds(pl.multiple_of(to_index * r, 16), r), :]

    def body(*refs):
        src, zone = refs[:n], refs[n + len(after):2 * n + len(after)]
        send, recv, local = refs[2 * n + len(after):]
        me = _my_pos()
        mi = _index(me)
        barrier = pltpu.get_barrier_semaphore()
        for k in range(1, NDEV):
            pl.semaphore_signal(barrier, inc=1, device_id=_peer(me, k), device_id_type=MESH)
        pl.semaphore_wait(barrier, NDEV - 1)
        own = [pltpu.make_async_copy(of(src[a], mi), zone[a].at[mi], local.at[a]) for a in range(n)]
        for cp in own:
            cp.start()
        for a in range(n):
            for k in range(1, NDEV):
                to = _peer(me, k)
                s = a * (NDEV - 1) + k - 1
                _remote(of(src[a], _index(to)), zone[a].at[mi], send.at[s], recv.at[s], to).start()
        for cp in own:
            cp.wait()
        for a in range(n):
            for k in range(1, NDEV):
                to = _peer(me, k)
                s = a * (NDEV - 1) + k - 1
                cp = _remote(of(src[a], mi), zone[a].at[_index(to)], send.at[s], recv.at[s], to)
                cp.wait_send()
                cp.wait_recv()

    return pl.kernel(
        body, name=name, mesh=plsc.ScalarSubcoreMesh(axis_name="sequencer", num_cores=1),
        out_type=tuple(jax.ShapeDtypeStruct((NDEV, s.shape[0] // NDEV, s.shape[1]), s.dtype) for s in srcs),
        scratch_types=[pltpu.SemaphoreType.DMA((n * (NDEV - 1),)), pltpu.SemaphoreType.DMA((n * (NDEV - 1),)),
                       pltpu.SemaphoreType.DMA((n,))],
        compiler_params=pltpu.CompilerParams(collective_id=collective_id),
    )(*srcs, *after)


CHIP_PEERS = (2, 4, 6)


def _sequencer_gather(name, collective_id, srcs, after=()):
    n = len(srcs)
    per = NDEV - 1

    def body(*refs):
        src, zone = refs[:n], refs[n + len(after):2 * n + len(after)]
        send, recv, local = refs[2 * n + len(after):]
        me = _my_pos()
        mi = _index(me)
        sibling = _peer(me, 1)
        talk_to = (sibling,) + tuple(_peer(me, k) for k in CHIP_PEERS)
        barrier = pltpu.get_barrier_semaphore()
        for to in talk_to:
            pl.semaphore_signal(barrier, inc=1, device_id=to, device_id_type=MESH)
        pl.semaphore_wait(barrier, len(talk_to))

        def copy(a, slot, block_of, to, from_src=False):
            rows = zone[a].at[_index(block_of)]
            return _remote(src[a] if from_src else rows, rows, send.at[a * per + slot], recv.at[a * per + slot], to)

        own = [pltpu.make_async_copy(src[a], zone[a].at[mi], local.at[a]) for a in range(n)]
        for cp in own:
            cp.start()
        started = []
        for a in range(n):
            started.append(copy(a, 0, me, sibling, from_src=True))
            started += [copy(a, 1 + j, me, _peer(me, k), from_src=True) for j, k in enumerate(CHIP_PEERS)]
        for cp in started:
            cp.start()
        for a in range(n):
            for j, k in enumerate(CHIP_PEERS):
                copy(a, 1 + j, _peer(me, k), me).wait_recv()
                passed = copy(a, 4 + j, _peer(me, k), sibling)
                passed.start()
                started.append(passed)
        for a in range(n):
            copy(a, 0, sibling, me).wait_recv()
            for j, k in enumerate(CHIP_PEERS):
                copy(a, 4 + j, _peer(me, k | 1), me).wait_recv()
        for cp in started:
            cp.wait_send()
        for cp in own:
            cp.wait()

    return pl.kernel(
        body, name=name, mesh=plsc.ScalarSubcoreMesh(axis_name="sequencer", num_cores=1),
        out_type=tuple(jax.ShapeDtypeStruct((NDEV,) + s.shape, s.dtype) for s in srcs),
        scratch_types=[pltpu.SemaphoreType.DMA((n * per,)), pltpu.SemaphoreType.DMA((n * per,)),
                       pltpu.SemaphoreType.DMA((n,))],
        compiler_params=pltpu.CompilerParams(collective_id=collective_id),
    )(*srcs, *after)


def _blocked(shape, nb, axis=0):
    block = tuple(s // nb if d == axis else s for d, s in enumerate(shape))
    return pl.BlockSpec(block, lambda i: tuple(i if d == axis else 0 for d in range(len(shape))))


def _sum_partials(name, parts, nb):
    n = len(parts)

    def body(*refs):
        for a in range(n):
            acc = refs[a][0].astype(F32)
            for p in range(1, NDEV):
                acc = acc + refs[a][p].astype(F32)
            refs[n + a][...] = acc

    return pl.pallas_call(
        body, name=name, grid=(nb,),
        out_shape=tuple(jax.ShapeDtypeStruct(p.shape[1:], F32) for p in parts),
        in_specs=[_blocked(p.shape, nb, 1) for p in parts],
        out_specs=tuple(_blocked(p.shape[1:], nb) for p in parts), compiler_params=_params(("arbitrary",)),
    )(*parts)


def _small_all_reduce(buf):
    def body(buf_ref, got_ref, red_ref, mine, send1, recv1, send2, recv2):
        me = _my_pos()
        mi = _index(me)
        first = []
        for k in range(1, NDEV):
            to = _peer(me, k)
            first.append(_remote(buf_ref.at[_index(to)], got_ref.at[mi], send1.at[k - 1], recv1.at[k - 1], to))
        for cp in first:
            cp.start()
        got_ref[mi] = buf_ref[mi]
        for k in range(1, NDEV):
            to = _peer(me, k)
            _remote(buf_ref.at[mi], got_ref.at[_index(to)], send1.at[k - 1], recv1.at[k - 1], to).wait_recv()
        acc = got_ref[0]
        for p in range(1, NDEV):
            acc = acc + got_ref[p]
        mine[...] = acc
        second = [_remote(mine, red_ref.at[mi], send2.at[k - 1], recv2.at[k - 1], _peer(me, k)) for k in range(1, NDEV)]
        for cp in second:
            cp.start()
        red_ref[mi] = acc
        for k in range(1, NDEV):
            to = _peer(me, k)
            _remote(mine, red_ref.at[_index(to)], send2.at[k - 1], recv2.at[k - 1], to).wait_recv()
        for cp in first + second:
            cp.wait_send()

    return pl.pallas_call(
        body, name="small_all_reduce",
        out_shape=(jax.ShapeDtypeStruct(buf.shape, F32), jax.ShapeDtypeStruct(buf.shape, F32)),
        in_specs=[_vmem()], out_specs=(_vmem(), _vmem()),
        scratch_shapes=[pltpu.VMEM(buf.shape[1:], F32),
                        pltpu.SemaphoreType.DMA((NDEV - 1,)), pltpu.SemaphoreType.DMA((NDEV - 1,)),
                        pltpu.SemaphoreType.DMA((NDEV - 1,)), pltpu.SemaphoreType.DMA((NDEV - 1,))],
        compiler_params=_params(),
    )(buf)


def _adamw_math(w, g, m, v):
    m = ADAM_B1 * m + (1.0 - ADAM_B1) * g
    v = ADAM_B2 * v + (1.0 - ADAM_B2) * jnp.square(g)
    m_hat = m / (1.0 - ADAM_B1 ** ADAM_STEP)
    v_hat = v / (1.0 - ADAM_B2 ** ADAM_STEP)
    delta = -ADAM_LR * (m_hat / (jnp.sqrt(v_hat) + ADAM_EPS) + ADAM_WD * w)
    return delta, m, v


def _adamw_group(name, ws, gs, ms, vs, nb):
    n = len(ws)

    def body(*refs):
        for a in range(n):
            w, g, m, v = (refs[q * n + a][...] for q in range(4))
            delta, m2, v2 = _adamw_math(w, g, m, v)
            refs[4 * n + a][...] = delta
            refs[5 * n + a][...] = m2
            refs[6 * n + a][...] = v2

    shapes = tuple(jax.ShapeDtypeStruct(w.shape, F32) for w in ws)
    specs = [_blocked(w.shape, nb) for w in ws]
    outs = pl.pallas_call(
        body, name=name, grid=(nb,), out_shape=shapes * 3, in_specs=specs * 4, out_specs=tuple(specs * 3),
        compiler_params=_params(("arbitrary",)),
    )(*ws, *gs, *ms, *vs)
    return outs[:n], outs[n:2 * n], outs[2 * n:]


def _adamw_from_partials(name, ws, parts, ms, vs, nb):
    n = len(ws)

    def body(*refs):
        for a in range(n):
            part = refs[n + a]
            g = part[0].astype(F32)
            for p in range(1, NDEV):
                g = g + part[p].astype(F32)
            delta, m2, v2 = _adamw_math(refs[a][...], g, refs[2 * n + a][...], refs[3 * n + a][...])
            refs[4 * n + a][...] = g
            refs[5 * n + a][...] = delta
            refs[6 * n + a][...] = m2
            refs[7 * n + a][...] = v2

    shapes = tuple(jax.ShapeDtypeStruct(w.shape, F32) for w in ws)
    specs = [_blocked(w.shape, nb) for w in ws]
    outs = pl.pallas_call(
        body, name=name, grid=(nb,), out_shape=shapes * 4,
        in_specs=specs + [_blocked(p.shape, nb, 1) for p in parts] + specs * 2, out_specs=tuple(specs * 4),
        compiler_params=_params(("arbitrary",)),
    )(*ws, *parts, *ms, *vs)
    return outs[:n], outs[n:2 * n], outs[2 * n:3 * n], outs[3 * n:]


def _adamw_ada(w, m, v, c_all_t, dmod_rows):
    nb = 4

    def body(w_ref, m_ref, v_ref, c_ref, dm_ref, g_ref, d_ref, m2_ref, v2_ref):
        g = _dot(c_ref[...], dm_ref[...].astype(BF))
        g_ref[...] = g
        delta, m2, v2 = _adamw_math(w_ref[...], g, m_ref[...], v_ref[...])
        d_ref[...] = delta
        m2_ref[...] = m2
        v2_ref[...] = v2

    shp = jax.ShapeDtypeStruct(w.shape, F32)
    spec = _blocked(w.shape, nb)
    return pl.pallas_call(
        body, name="adamw_ada", grid=(nb,), out_shape=(shp, shp, shp, shp),
        in_specs=[spec, spec, spec, _blocked(c_all_t.shape, nb), _full(dmod_rows.shape)],
        out_specs=(spec, spec, spec, spec), compiler_params=_params(("arbitrary",)),
    )(w, m, v, c_all_t, dmod_rows)


def _w_in_to_kernel(w):
    return jnp.concatenate([w[:, 0:448], jnp.zeros((w.shape[0], 64), w.dtype), w[:, 448:960]], axis=1)


def _w_in_from_kernel(w):
    return jnp.concatenate([w[:, 0:448], w[:, 512:1024]], axis=1)


def _w_uq_to_kernel(w):
    r = w.shape[0]
    return jnp.concatenate([w[:, :, 0:NOPE].reshape(r, HEADS * NOPE),
                            w[:, :, NOPE:NOPE + HALF].reshape(r, HEADS * HALF),
                            w[:, :, NOPE + HALF:].reshape(r, HEADS * HALF)], axis=1)


def _w_uq_from_kernel(w):
    r = w.shape[0]
    return jnp.concatenate([w[:, 0:512].reshape(r, HEADS, NOPE), w[:, 512:640].reshape(r, HEADS, HALF),
                            w[:, 640:768].reshape(r, HEADS, HALF)], axis=2)


REP_NAMES = ("w_uk", "w_uv", "w_pool", "g_mix", "g_q", "g_kv", "pool_scale", "g_ffn", "g_final")


def kernel(x, c, positions, w_ada, b_ada, g_mix, w_in, g_q, g_kv, w_uq, w_uk, w_uv, w_pool, pool_scale, w_o, g_ffn, w_gate, w_up, w_down, g_final, loss_target, m_w_ada, m_b_ada, m_g_mix, m_w_in, m_g_q, m_g_kv, m_w_uq, m_w_uk, m_w_uv, m_w_pool, m_pool_scale, m_w_o, m_g_ffn, m_w_gate, m_w_up, m_w_down, m_g_final, v_w_ada, v_b_ada, v_g_mix, v_w_in, v_g_q, v_g_kv, v_w_uq, v_w_uk, v_w_uv, v_w_pool, v_pool_scale, v_w_o, v_g_ffn, v_w_gate, v_w_up, v_w_down, v_g_final):
    given = dict(locals())

    merge = lambda g: g.reshape(NDEV * g.shape[1], g.shape[2])
    w_in_p, w_uq_p = (merge(g) for g in _sequencer_gather(
        "gather_in", 3, (_w_in_to_kernel(w_in[0]).astype(BF), _w_uq_to_kernel(w_uq[0]).astype(BF))))

    rope = _rope_tables(positions[0])
    mod, c_all8 = _ada_mod(c, w_ada[0], b_ada, rope[3][0:8, :])
    c_all = c_all8[:, 0, :]
    late = _sequencer_gather(
        "gather_late", 1, (w_o[0].astype(BF), w_gate[0].T.astype(BF), w_up[0].T.astype(BF), w_down[0].astype(BF)),
        after=(mod[:, 0:128], w_in_p[0:16, 0:128], w_uq_p[0:16, 0:128]))

    def ffn_grads_exchange(arrays):
        return _sequencer_scatter("scatter_ffn", 2, arrays)

    loss, dx, dmod, tail_grads, ffn_parts, replicated = _local_step(
        x[0], rope, loss_target[0], mod, g_mix, w_in_p, g_q, g_kv, w_uq_p, w_uk[0], w_uv[0], w_pool[0],
        pool_scale, g_ffn, g_final.reshape(1, D), tuple(merge(g) for g in late), ffn_grads_exchange)

    flat = jnp.concatenate([replicated[k].reshape(-1) for k in REP_NAMES] + [loss.reshape(1)])
    flat = jnp.pad(flat, (0, NDEV * REP_ROWS * 128 - flat.shape[0])).reshape(NDEV, REP_ROWS, 128)
    dmod_blocks = jnp.pad(dmod.reshape(NDEV, MODC // 128, 128), ((0, 0), (0, MOD_ROWS - MODC // 128), (0, 0)))
    got, red = _small_all_reduce(jnp.concatenate([dmod_blocks, flat], axis=1))

    tail_parts = _sequencer_scatter("scatter_tail", 4, tail_grads,
                                    after=(ffn_parts[0][0, 0:16, 0:128], red[0, 0:8, :]))
    g_in_p, g_uq_p = _sum_partials("sum_tail_partials", tail_parts, 1)
    as_transpose = ("w_in", "w_gate", "w_up")
    grads = dict(w_in=_w_in_from_kernel(g_in_p).T, w_uq=_w_uq_from_kernel(g_uq_p))
    partials = dict(w_gate=ffn_parts[0], w_up=ffn_parts[1], w_down=ffn_parts[2], w_o=ffn_parts[3])
    dmod_rows = got[:, 0:MODC // 128, :].reshape(NDEV, MODC)
    grads["b_ada"] = red[:, 0:MODC // 128, :].reshape(1, N_MOD * D)
    rep_flat = red[:, MOD_ROWS:, :].reshape(-1)
    off = 0
    for k in REP_NAMES:
        size = int(np.prod(given[k].shape))
        grads[k] = rep_flat[off:off + size]
        off += size

    view = {k: (given[k].shape[1:] if given[k].ndim > 2 else given[k].shape)
            for k in REP_NAMES + ("b_ada", "w_ada", "w_in", "w_uq", "w_o", "w_gate", "w_up", "w_down")}
    view.update(g_final=(1, D))
    names = ["w_ada", "b_ada", "g_mix", "w_in", "g_q", "g_kv", "w_uq", "w_uk", "w_uv", "w_pool", "pool_scale",
             "w_o", "g_ffn", "w_gate", "w_up", "w_down", "g_final"]
    g_ada, d_ada, m_ada, v_ada = _adamw_ada(w_ada[0], m_w_ada[0], v_w_ada[0], c_all.T.astype(BF), dmod_rows)
    out_g, out_d, out_m, out_v = dict(w_ada=g_ada), dict(w_ada=d_ada), dict(w_ada=m_ada), dict(w_ada=v_ada)
    groups = (("adamw_ffn", ("w_gate", "w_up", "w_down", "w_o"), 4),
              ("adamw_replicated", REP_NAMES + ("b_ada",), 1),
              ("adamw_tail", ("w_in", "w_uq"), 1))
    for gname, members, nb in groups:
        turn = lambda k, t: t.T if k in as_transpose else t
        ws = [turn(k, given[k].reshape(view[k])) for k in members]
        ms = [turn(k, given["m_" + k].reshape(view[k])) for k in members]
        vs = [turn(k, given["v_" + k].reshape(view[k])) for k in members]
        if members[0] in partials:
            gs, ds, m2, v2 = _adamw_from_partials(gname, ws, [partials[k] for k in members], ms, vs, nb)
        else:
            gs = [grads[k] if k in as_transpose else grads[k].reshape(view[k]) for k in members]
            ds, m2, v2 = _adamw_group(gname, ws, gs, ms, vs, nb)
        for k, g, d, mm, vv in zip(members, gs, ds, m2, v2):
            out_g[k], out_d[k], out_m[k], out_v[k] = turn(k, g), turn(k, d), turn(k, mm), turn(k, vv)

    total = rep_flat[off]
    shaped = lambda d: [d[k].reshape(given[k].shape) for k in names]
    return (total, dx[None], *shaped(out_g), *shaped(out_d), *shaped(out_m), *shaped(out_v))
```

```python
import numpy as np
import jax
import jax.numpy as jnp
from jax import lax
from jax.experimental import pallas as pl
from jax.experimental.pallas import tpu as pltpu
from jax.experimental.pallas import tpu_sc as plsc

D = 1024
HEADS = 4
NOPE = 128
ROPE = 64
HALF = ROPE // 2
QL = 256
KVL = 128
FF = 2816
PW = 512
GROUPS = 4
GD = 128
N_MOD = 6
EPS = 1e-6
SM_SCALE = (NOPE + ROPE) ** -0.5
LOG2_E = 1.4426950408889634
EXP2_SCALE = SM_SCALE * LOG2_E
ROPE_THETA = 10000.0
NDEV = 8
MODC = N_MOD * D // NDEV

ADAM_LR = 0.001
ADAM_B1 = 0.9
ADAM_B2 = 0.999
ADAM_EPS = 1e-08
ADAM_WD = 0.01
ADAM_STEP = 10

BF = jnp.bfloat16
F32 = jnp.float32
VMEM_LIMIT_V7X = 60 * 1024 * 1024
MESH = pl.DeviceIdType.MESH

TQ = 512
TK = 512
QW = 256
VPU_ROWS = 16
MOD_ROWS = 8
REP_ROWS = 200
SMALL_ROWS = MOD_ROWS + REP_ROWS


def _params(sem=None):
    return pltpu.CompilerParams(dimension_semantics=sem, vmem_limit_bytes=VMEM_LIMIT_V7X)


def _dot(a, b):
    return jnp.dot(a, b, preferred_element_type=F32)


def _dot_nt(a, b):
    return lax.dot_general(a, b, (((1,), (1,)), ((), ())), preferred_element_type=F32)


def _dot_tn(a, b):
    return _dot(a.astype(F32).T.astype(BF), b)


def _full(shape):
    return pl.BlockSpec(shape, lambda *_: (0,) * len(shape))


def _rows(ts, cols):
    return pl.BlockSpec((ts, cols), lambda i: (i, 0))


def _vmem():
    return pl.BlockSpec(memory_space=pltpu.VMEM)


def _any():
    return pl.BlockSpec(memory_space=pl.ANY)


def _rms(v):
    return lax.rsqrt(jnp.mean(v * v, axis=-1, keepdims=True) + EPS)


def _rms_bwd(dn, n, r):
    return r * (dn - n * jnp.mean(dn * n, axis=-1, keepdims=True))


def _colsum(v):
    return jnp.sum(v, axis=0, keepdims=True)


def _swap_halves(v):
    lane = lax.broadcasted_iota(jnp.int32, v.shape, 1)
    return jnp.where(lane < HALF, pltpu.roll(v, 128 - HALF, 1), pltpu.roll(v, HALF, 1))


def _window_lane_width():
    lane = lax.broadcasted_iota(jnp.int32, (1, PW), 1)
    return jnp.where(lane < 128, 2.0, jnp.where(lane < 256, 4.0, jnp.where(lane < 384, 8.0, 16.0))).astype(F32)


def _window_sums(ext, back):
    n = ext.shape[0]

    def sh(v, k):
        return pltpu.roll(v, k if back else n - k, 0)

    s2 = ext + sh(ext, 1)
    e4 = s2[:, 128:]
    s4 = e4 + sh(e4, 2)
    e8 = s4[:, 128:]
    s8 = e8 + sh(e8, 4)
    e16 = s8[:, 128:]
    s16 = e16 + sh(e16, 8)
    return jnp.concatenate([s2[:, :128], s4[:, :128], s8[:, :128], s16], axis=1)


def _fill_block_diagonal(dst_ref, blocks_ref):
    n, r, c = blocks_ref.shape
    dst_ref[...] = jnp.zeros_like(dst_ref)
    for b in range(n):
        dst_ref[b * r:(b + 1) * r, b * c:(b + 1) * c] = blocks_ref[b]


def _row_counts(first_row, ts):
    t1 = (first_row + lax.broadcasted_iota(jnp.int32, (ts, 1), 0) + 1).astype(F32)
    return jnp.minimum(t1, _window_lane_width())


def _fwd_in(x, mod, g_mix, w_in, g_q, g_kv, w_uq, wuk_dc, perm, cos4, sin4, csk, snk, w_pool, pool_scale):
    S = x.shape[0]
    ts = 512
    nsub = ts // TQ

    def body(x_ref, mod_ref, gmix_ref, win_ref, gq_ref, gkv_ref, wuq_ref, wuk_ref, perm_ref, cos_ref, sin_ref,
             csk_ref, snk_ref, wpool_ref, pscale_ref,
             h1_ref, raw_ref, qn_ref, qs_ref, kv_ref, kvt_ref, pooled_ref, ypre_ref, ypool_ref, carry_ref, wuk_bd,
             wpool_bd):
        i = pl.program_id(0)

        @pl.when(i == 0)
        def _():
            carry_ref[...] = jnp.zeros_like(carry_ref)
            _fill_block_diagonal(wuk_bd, wuk_ref)
            _fill_block_diagonal(wpool_bd, wpool_ref)

        xv = x_ref[...]
        sh1 = mod_ref[0:1, 0:D]
        sc1 = mod_ref[0:1, D:2 * D]
        h = (xv * _rms(xv)) * gmix_ref[...] * (1.0 + sc1) + sh1
        hb = h.astype(BF)
        h1_ref[...] = hb
        proj = _dot(hb, win_ref[...])
        cq_raw = proj[:, 0:QL]
        ckv_raw = proj[:, QL:QL + KVL]
        kr = proj[:, 384:512]
        u = proj[:, 512:1024]
        raw_ref[...] = proj[:, 0:384]

        c_q = (cq_raw * _rms(cq_raw)) * gq_ref[...]
        c_kv = (ckv_raw * _rms(ckv_raw)) * gkv_ref[...]
        q = _dot(c_q.astype(BF), wuq_ref[...])
        qn = q[:, 0:HEADS * NOPE].astype(BF)
        qn_ref[...] = qn
        x1 = q[:, 512:640]
        x2 = q[:, 640:768]
        cosv = cos_ref[...]
        sinv = sin_ref[...]
        roped = jnp.concatenate([x1 * cosv - x2 * sinv, x1 * sinv + x2 * cosv], axis=1).astype(BF)
        q_lat = _dot(qn, wuk_bd[...])
        q_rope = _dot(roped, perm_ref[...])
        for hd in range(HEADS):
            cols = slice(hd * 128, (hd + 1) * 128)
            qh = jnp.concatenate([q_lat[:, cols], q_rope[:, cols]], axis=1).astype(BF)
            for a in range(nsub):
                qs_ref[a, hd * TQ:(hd + 1) * TQ, :] = qh[a * TQ:(a + 1) * TQ, :]
        k_rope = kr * csk_ref[...] + _swap_halves(kr) * snk_ref[...]
        keys = jnp.concatenate([c_kv, k_rope], axis=1)
        kv_ref[...] = keys.astype(BF)
        for a in range(ts // TK):
            kvt_ref[a] = keys[a * TK:(a + 1) * TK, :].T.astype(BF)

        ext = jnp.concatenate([carry_ref[...], u], axis=0)
        win = _window_sums(ext, True)[16:, :]
        pooled = (win / _row_counts(i * ts, ts) - u).astype(BF)
        pooled_ref[...] = pooled
        carry_ref[...] = u[ts - 16:ts, :]
        ypre = _dot(pooled, wpool_bd[...])
        ypre_ref[...] = ypre.astype(BF)
        ypool_ref[...] = (ypre * pscale_ref[...]).astype(BF)

    out_shape = (
        jax.ShapeDtypeStruct((S, D), BF),
        jax.ShapeDtypeStruct((S, 384), F32),
        jax.ShapeDtypeStruct((S, HEADS * NOPE), BF),
        jax.ShapeDtypeStruct((S // TQ, HEADS * TQ, QW), BF),
        jax.ShapeDtypeStruct((S, QW), BF),
        jax.ShapeDtypeStruct((S // TK, QW, TK), BF),
        jax.ShapeDtypeStruct((S, PW), BF),
        jax.ShapeDtypeStruct((S, PW), BF),
        jax.ShapeDtypeStruct((S, PW), BF),
    )
    in_specs = [
        _rows(ts, D), _full(mod.shape), _full((1, D)), _full(w_in.shape), _full((1, QL)), _full((1, KVL)),
        _full(w_uq.shape), _full(wuk_dc.shape), _full(perm.shape), _rows(ts, 128), _rows(ts, 128), _rows(ts, 128),
        _rows(ts, 128), _full(w_pool.shape), _full((1, PW)),
    ]
    out_specs = (
        _rows(ts, D), _rows(ts, 384), _rows(ts, HEADS * NOPE),
        pl.BlockSpec((nsub, HEADS * TQ, QW), lambda i: (i, 0, 0)),
        _rows(ts, QW), pl.BlockSpec((ts // TK, QW, TK), lambda i: (i, 0, 0)), _rows(ts, PW), _rows(ts, PW),
        _rows(ts, PW),
    )
    return pl.pallas_call(
        body, name="fwd_in", out_shape=out_shape, grid=(S // ts,), in_specs=in_specs, out_specs=out_specs,
        scratch_shapes=[pltpu.VMEM((16, PW), F32), pltpu.VMEM((HEADS * NOPE, HEADS * KVL), BF),
                        pltpu.VMEM((PW, PW), BF)],
        compiler_params=_params(("arbitrary",)),
    )(x, mod, g_mix, w_in, g_q, g_kv, w_uq, wuk_dc, perm, cos4, sin4, csk, snk, w_pool, pool_scale)


def _diag_mask(shape, q_axis, first_chunk):
    qi = (lax.broadcasted_iota(jnp.int32, shape, q_axis) & (TQ - 1)) >> 6
    ki = (lax.broadcasted_iota(jnp.int32, shape, 1 - q_axis) >> 6) + first_chunk
    return ki <= qi


def _attn_fwd(qs, kv, kvt, wuv_vc):
    nq = qs.shape[0]
    S = kv.shape[0]
    M = HEADS * TQ

    def body(qs_ref, kv_ref, kvt_ref, wuv_ref, olat_ref, ymla_ref, lse_ref):
        i = pl.program_id(0)
        q = qs_ref[0]

        def step(kt, carry, first_chunk=None):
            m, l, acc = carry
            k = kv_ref[pl.ds(pl.multiple_of(kt * TK, TK), TK), :]
            v_t = kvt_ref[kt][0:KVL, :]
            s = _dot_nt(k, q)
            if first_chunk is not None:
                s = jnp.where(_diag_mask((TK, M), 1, first_chunk), s, -jnp.inf)
            m_new = jnp.maximum(m, jnp.max(s, axis=0, keepdims=True))
            alpha = jnp.exp2((m - m_new) * EXP2_SCALE)
            p = jnp.exp2((s - m_new) * EXP2_SCALE)
            l = alpha * l + jnp.sum(p, axis=0, keepdims=True)
            acc = alpha * acc + _dot(v_t, p.astype(BF))
            return m_new, l, acc

        init = (jnp.full((1, M), -jnp.inf, F32), jnp.zeros((1, M), F32), jnp.zeros((KVL, M), F32))
        per = TQ // TK
        carry = lax.fori_loop(0, per * i, step, init)
        for j in range(per):
            carry = step(per * i + j, carry, j * (TK // 64))
        m, l, acc = carry
        o_lat = acc / l
        olat_ref[0] = o_lat
        lse_ref[0] = jnp.broadcast_to(m * SM_SCALE + jnp.log(l), (8, M))
        for hd in range(HEADS):
            o_t = _dot(wuv_ref[hd], o_lat[:, hd * TQ:(hd + 1) * TQ].astype(BF))
            ymla_ref[:, hd * 128:(hd + 1) * 128] = o_t.T.astype(BF)

    out_shape = (
        jax.ShapeDtypeStruct((nq, KVL, M), F32),
        jax.ShapeDtypeStruct((S, HEADS * 128), BF),
        jax.ShapeDtypeStruct((nq, 8, M), F32),
    )
    return pl.pallas_call(
        body, name="attn_fwd", out_shape=out_shape, grid=(nq,),
        in_specs=[pl.BlockSpec((1, M, QW), lambda i: (i, 0, 0)), _full(kv.shape), _full(kvt.shape),
                  _full(wuv_vc.shape)],
        out_specs=(pl.BlockSpec((1, KVL, M), lambda i: (i, 0, 0)), _rows(TQ, HEADS * 128),
                   pl.BlockSpec((1, 8, M), lambda i: (i, 0, 0))),
        compiler_params=_params(("arbitrary",)),
    )(qs, kv, kvt, wuv_vc)


def _silu_parts(a):
    sg = jax.nn.sigmoid(a)
    return sg, a * sg


def _ffn_fwd(x, ymla, ypool, mod, w_o, g_ffn, wg_t, wu_t, wd, g_final, target):
    S = x.shape[0]
    ts = 256

    def body(x_ref, ymla_ref, ypool_ref, mod_ref, wo_ref, gffn_ref, wg_ref, wu_ref, wd_ref, gfin_ref, t_ref,
             x2_ref, mix_ref, h2t_ref, a_ref, b_ref, dx3_ref, dff_ref, dfft_ref, loss_ref, dgfin_ref, dgt2_ref,
             f_ref):
        i = pl.program_id(0)

        @pl.when(i == 0)
        def _():
            loss_ref[...] = jnp.zeros_like(loss_ref)
            dgfin_ref[...] = jnp.zeros_like(dgfin_ref)
            dgt2_ref[...] = jnp.zeros_like(dgt2_ref)

        gt1 = mod_ref[0:1, 2 * D:3 * D]
        sh2 = mod_ref[0:1, 3 * D:4 * D]
        sc2 = mod_ref[0:1, 4 * D:5 * D]
        gt2 = mod_ref[0:1, 5 * D:6 * D]
        cat = jnp.concatenate([ymla_ref[...], ypool_ref[...]], axis=1)
        mix = _dot(cat, wo_ref[...])
        mix_ref[...] = mix.astype(BF)
        x2 = x_ref[...] + gt1 * mix
        x2_ref[...] = x2
        h2 = (x2 * _rms(x2)) * gffn_ref[...] * (1.0 + sc2) + sh2
        h2b = h2.astype(BF)
        h2t_ref[...] = h2.T.astype(BF)

        for c in range(FF // FCHUNK):
            cols = slice(c * FCHUNK, (c + 1) * FCHUNK)
            a = _dot_nt(h2b, wg_ref[cols, :])
            b = _dot_nt(h2b, wu_ref[cols, :])
            a_ref[:, cols] = a.astype(BF)
            b_ref[:, cols] = b.astype(BF)
            f_ref[:, cols] = (_silu_parts(a)[1] * b).astype(BF)
        ff = _dot(f_ref[...], wd_ref[...])

        x3 = x2 + gt2 * ff
        r3 = _rms(x3)
        xn3 = x3 * r3
        gfin = gfin_ref[...]
        e = xn3 * gfin - t_ref[...]
        loss_ref[...] += 0.5 * jnp.sum(jnp.mean(e * e, axis=-1, keepdims=True))
        dy = e * (1.0 / D)
        dgfin_ref[...] += _colsum(dy * xn3)
        dx3 = _rms_bwd(dy * gfin, xn3, r3)
        dx3_ref[...] = dx3
        dgt2_ref[...] += _colsum(dx3 * ff)
        dff = dx3 * gt2
        dff_ref[...] = dff.astype(BF)
        dfft_ref[...] = dff.T.astype(BF)

    row = lambda c: _rows(ts, c)
    col = pl.BlockSpec((D, ts), lambda i: (0, i))
    const = _full
    out_shape = (
        jax.ShapeDtypeStruct((S, D), F32),
        jax.ShapeDtypeStruct((S, D), BF),
        jax.ShapeDtypeStruct((D, S), BF),
        jax.ShapeDtypeStruct((S, FF), BF),
        jax.ShapeDtypeStruct((S, FF), BF),
        jax.ShapeDtypeStruct((S, D), F32),
        jax.ShapeDtypeStruct((S, D), BF),
        jax.ShapeDtypeStruct((D, S), BF),
        jax.ShapeDtypeStruct((8, 128), F32),
        jax.ShapeDtypeStruct((1, D), F32),
        jax.ShapeDtypeStruct((1, D), F32),
    )
    return pl.pallas_call(
        body, name="ffn_fwd", out_shape=out_shape, grid=(S // ts,),
        in_specs=[row(D), row(PW), row(PW), const(mod.shape), _vmem(), const((1, D)), _vmem(), _vmem(), _vmem(),
                  const((1, D)), row(D)],
        out_specs=(row(D), row(D), col, row(FF), row(FF), row(D), row(D), col, const((8, 128)), const((1, D)),
                   const((1, D))),
        scratch_shapes=[pltpu.VMEM((ts, FF), BF)],
        compiler_params=_params(("arbitrary",)),
    )(x, ymla, ypool, mod, w_o, g_ffn, wg_t, wu_t, wd, g_final, target)


FCHUNK = 256


def _ffn_bwd_acts(dff, a, b, wg_t, wu_t, wd):
    S = dff.shape[0]
    ts = 512

    def body(dff_ref, a_ref, b_ref, wg_ref, wu_ref, wd_ref, da_ref, db_ref, dh2_ref):
        dffb = dff_ref[...]
        for c in range(FF // FCHUNK):
            cols = slice(c * FCHUNK, (c + 1) * FCHUNK)
            df = _dot_nt(dffb, wd_ref[cols, :])
            av = a_ref[:, cols].astype(F32)
            bv = b_ref[:, cols].astype(F32)
            sg, sa = _silu_parts(av)
            db_ref[:, cols] = (df * sa).astype(BF)
            da_ref[:, cols] = (df * bv * (sg * (1.0 + av * (1.0 - sg)))).astype(BF)
        dh2_ref[...] = _dot(da_ref[...], wg_ref[...]) + _dot(db_ref[...], wu_ref[...])

    act = _rows(ts, FF)
    return pl.pallas_call(
        body, name="ffn_bwd_acts",
        out_shape=(jax.ShapeDtypeStruct((S, FF), BF), jax.ShapeDtypeStruct((S, FF), BF),
                   jax.ShapeDtypeStruct((S, D), F32)),
        grid=(S // ts,), in_specs=[_rows(ts, D), act, act, _vmem(), _vmem(), _vmem()],
        out_specs=(act, act, _rows(ts, D)), compiler_params=_params(("arbitrary",)),
    )(dff, a, b, wg_t, wu_t, wd)


def _ffn_bwd_weights(dff_t, h2_t, da, db, a, b):
    S = da.shape[0]

    def body(dfft_ref, h2t_ref, da_ref, db_ref, a_ref, b_ref, dwg_ref, dwu_ref, dwd_ref):
        h2t = h2t_ref[...]
        dwg_ref[...] = _dot(h2t, da_ref[...]).T.astype(BF)
        dwu_ref[...] = _dot(h2t, db_ref[...]).T.astype(BF)
        f = (_silu_parts(a_ref[...].astype(F32))[1] * b_ref[...].astype(F32)).astype(BF)
        dwd_ref[...] = _dot(dfft_ref[...], f).T.astype(BF)

    act = pl.BlockSpec((S, FCHUNK), lambda j: (0, j))
    wblk = _rows(FCHUNK, D)
    shp = jax.ShapeDtypeStruct((FF, D), BF)
    return pl.pallas_call(
        body, name="ffn_bwd_weights", out_shape=(shp, shp, shp), grid=(FF // FCHUNK,),
        in_specs=[_vmem(), _vmem(), act, act, act, act], out_specs=(wblk, wblk, wblk),
        compiler_params=_params(("arbitrary",)),
    )(dff_t, h2_t, da, db, a, b)


def _mix_bwd(dh2, dx3, x2, mix, mod, g_ffn, ymla, ypool, w_o, ypre, pooled, pool_scale, wpool_dc, olat, wuv_vc):
    S = dh2.shape[0]
    ts = 512
    n = S // ts
    nsub = ts // TQ
    M = HEADS * TQ

    def body(dh2_ref, dx3_ref, x2_ref, mix_ref, mod_ref, gffn_ref, ymla_ref, ypool_ref, wo_ref, ypre_ref, pooled_ref,
             pscale_ref, wpool_ref, olat_ref, wuv_ref,
             dx2_ref, du_ref, dolat_ref, delta_ref, dwo_ref, dwuv_ref, dwpool_ref, dpscale_ref, dgt1_ref, dsc2_ref,
             dsh2_ref, dgffn_ref, carry_ref, dwo_acc, dwpool_acc, wpool_bd, wuv_bd):
        i = pl.program_id(0)

        @pl.when(i == 0)
        def _():
            carry_ref[...] = jnp.zeros_like(carry_ref)
            dwo_acc[...] = jnp.zeros_like(dwo_acc)
            dwpool_acc[...] = jnp.zeros_like(dwpool_acc)
            _fill_block_diagonal(wpool_bd, wpool_ref)
            _fill_block_diagonal(wuv_bd, wuv_ref)
            for r in (dwuv_ref, dpscale_ref, dgt1_ref, dsc2_ref, dsh2_ref, dgffn_ref):
                r[...] = jnp.zeros_like(r)

        gt1 = mod_ref[0:1, 2 * D:3 * D]
        sc2 = mod_ref[0:1, 4 * D:5 * D]
        gffn = gffn_ref[...]
        dh2 = dh2_ref[...]
        x2 = x2_ref[...]
        r2 = _rms(x2)
        xn2 = x2 * r2
        along = _colsum(dh2 * xn2)
        dsc2_ref[...] += along * gffn
        dsh2_ref[...] += _colsum(dh2)
        dgffn_ref[...] += along * (1.0 + sc2)
        dx2 = dx3_ref[...] + _rms_bwd(dh2 * (gffn * (1.0 + sc2)), xn2, r2)
        dx2_ref[...] = dx2
        dgt1_ref[...] += _colsum(dx2 * mix_ref[...].astype(F32))
        dmix = (dx2 * gt1).astype(BF)
        cat = jnp.concatenate([ymla_ref[...], ypool_ref[...]], axis=1)
        dwo_acc[...] += _dot_tn(cat, dmix)
        dcat = _dot_nt(dmix, wo_ref[...])
        dymla = dcat[:, 0:512]
        dypool = dcat[:, 512:1024]

        dpscale_ref[...] += _colsum(dypool * ypre_ref[...].astype(F32))
        dypre = (dypool * pscale_ref[...]).astype(BF)
        dwpool_acc[...] += _dot_tn(pooled_ref[...], dypre)
        dpooled = _dot(dypre, wpool_bd[...])
        tile = n - 1 - i
        e = dpooled / _row_counts(tile * ts, ts)
        ext = jnp.concatenate([e, carry_ref[...]], axis=0)
        du_ref[...] = (_window_sums(ext, False)[0:ts, :] - dpooled).astype(BF)
        carry_ref[...] = e[0:16, :]

        dob_all = dymla.astype(BF)
        dol_all = _dot(dob_all, wuv_bd[...])
        for hd in range(HEADS):
            dob = dob_all[:, hd * 128:(hd + 1) * 128]
            dol = dol_all[:, hd * 128:(hd + 1) * 128]
            for a in range(nsub):
                ol_t = olat_ref[a, :, hd * TQ:(hd + 1) * TQ]
                dl = dol[a * TQ:(a + 1) * TQ, :]
                dolat_ref[a, hd * TQ:(hd + 1) * TQ, :] = dl.astype(BF)
                dwuv_ref[hd] += _dot(ol_t.astype(BF), dob[a * TQ:(a + 1) * TQ, :])
                delta = jnp.sum(dl * ol_t.T, axis=-1, keepdims=True)
                delta_ref[a, :, hd * TQ:(hd + 1) * TQ] = jnp.broadcast_to(delta, (TQ, 128)).T[0:8, :]

        @pl.when(i == n - 1)
        def _():
            dwo_ref[...] = dwo_acc[...].astype(BF)
            for g in range(GROUPS):
                dwpool_ref[g] = dwpool_acc[g * GD:(g + 1) * GD, g * GD:(g + 1) * GD]

    rev = lambda c: pl.BlockSpec((ts, c), lambda i: (n - 1 - i, 0))
    rev3 = lambda r, c: pl.BlockSpec((nsub, r, c), lambda i: (n - 1 - i, 0, 0))
    out_shape = (
        jax.ShapeDtypeStruct((S, D), F32),
        jax.ShapeDtypeStruct((S, PW), BF),
        jax.ShapeDtypeStruct((S // TQ, M, KVL), BF),
        jax.ShapeDtypeStruct((S // TQ, 8, M), F32),
        jax.ShapeDtypeStruct((D, D), BF),
        jax.ShapeDtypeStruct((HEADS, KVL, 128), F32),
        jax.ShapeDtypeStruct((GROUPS, GD, GD), F32),
        jax.ShapeDtypeStruct((1, PW), F32),
        jax.ShapeDtypeStruct((1, D), F32), jax.ShapeDtypeStruct((1, D), F32), jax.ShapeDtypeStruct((1, D), F32),
        jax.ShapeDtypeStruct((1, D), F32),
    )
    in_specs = [rev(D), rev(D), rev(D), rev(D), _full(mod.shape), _full((1, D)), rev(PW), rev(PW), _full(w_o.shape),
                rev(PW), rev(PW), _full((1, PW)), _full(wpool_dc.shape), rev3(KVL, M), _full(wuv_vc.shape)]
    out_specs = (rev(D), rev(PW), rev3(M, KVL), rev3(8, M), _full((D, D)), _full((HEADS, KVL, 128)),
                 _full((GROUPS, GD, GD)), _full((1, PW)), _full((1, D)), _full((1, D)), _full((1, D)), _full((1, D)))
    return pl.pallas_call(
        body, name="mix_bwd", out_shape=out_shape, grid=(n,), in_specs=in_specs, out_specs=out_specs,
        scratch_shapes=[pltpu.VMEM((16, PW), F32), pltpu.VMEM((D, D), F32), pltpu.VMEM((PW, PW), F32),
                        pltpu.VMEM((PW, PW), BF), pltpu.VMEM((HEADS * 128, HEADS * KVL), BF)],
        compiler_params=_params(("arbitrary",)),
    )(dh2, dx3, x2, mix, mod, g_ffn, ymla, ypool, w_o, ypre, pooled, pool_scale, wpool_dc, olat, wuv_vc)


def _attn_bwd(qs, kv, dolat, lse, delta):
    nq = qs.shape[0]
    S = kv.shape[0]
    M = HEADS * TQ
    nk = S // TK

    def body(qs_ref, kv_ref, do_ref, lse_ref, delta_ref, dkv_ref, dqt_out_ref, dqt_ref, p_ref, ds_ref):
        kt = pl.program_id(0)
        k = kv_ref[...]
        v = k[:, 0:KVL]
        k_t = k.astype(F32).T.astype(BF)

        @pl.when(kt == 0)
        def _():
            dqt_ref[...] = jnp.zeros_like(dqt_ref)

        def step(qi, carry, first_chunk=None):
            dk, dv = carry
            q = qs_ref[qi]
            do = do_ref[qi]
            s = _dot_nt(k, q)
            dp = _dot_nt(v, do)
            lse_row = lse_ref[qi, 0:1, :] * LOG2_E
            delta_row = delta_ref[qi, 0:1, :]
            q_chunk = (lax.broadcasted_iota(jnp.int32, (1, M), 1) & (TQ - 1)) >> 6
            for r in range(0, TK, VPU_ROWS):
                rows = slice(r, r + VPU_ROWS)
                p = jnp.exp2(s[rows, :] * EXP2_SCALE - lse_row)
                if first_chunk is not None:
                    p = jnp.where((r >> 6) + first_chunk <= q_chunk, p, 0.0)
                p_ref[rows, :] = p.astype(BF)
                ds_ref[rows, :] = (p * (dp[rows, :] - delta_row) * SM_SCALE).astype(BF)
            ds = ds_ref[...]
            dv = dv + _dot(p_ref[...], do)
            dk = dk + _dot(ds, q)
            dqt_ref[qi] += _dot(k_t, ds)
            return dk, dv

        per = TQ // TK
        first = kt // per
        carry = step(first, (jnp.zeros((TK, QW), F32), jnp.zeros((TK, KVL), F32)), (kt % per) * (TK // 64))
        dk, dv = lax.fori_loop(first + 1, nq, step, carry)
        dkv_ref[...] = dk + jnp.concatenate([dv, jnp.zeros((TK, QW - KVL), F32)], axis=1)
        dqt_out_ref[0] = dqt_ref[first].astype(BF)

    out_shape = (jax.ShapeDtypeStruct((S, QW), F32), jax.ShapeDtypeStruct((nq, QW, M), BF))
    return pl.pallas_call(
        body, name="attn_bwd", out_shape=out_shape, grid=(nk,),
        in_specs=[_vmem(), _rows(TK, QW), _vmem(), _vmem(), _vmem()],
        out_specs=(_rows(TK, QW), pl.BlockSpec((1, QW, M), lambda kt: (kt // (TQ // TK), 0, 0))),
        scratch_shapes=[pltpu.VMEM((nq, QW, M), F32), pltpu.VMEM((TK, M), BF), pltpu.VMEM((TK, M), BF)],
        compiler_params=_params(("arbitrary",)),
    )(qs, kv, dolat, lse, delta)


def _in_bwd(dqt, dkv, du, raw, qn, h1, x, dx2, mod, g_mix, w_in, g_q, g_kv, w_uq, wuk_cd, perm_t, cos4, sin4, csk,
            snk):
    S = x.shape[0]
    ts = 512
    n = S // ts
    nsub = ts // TQ
    M = HEADS * TQ

    def body(dqt_ref, dkv_ref, du_ref, raw_ref, qn_ref, h1_ref, x_ref, dx2_ref, mod_ref, gmix_ref, win_ref, gq_ref,
             gkv_ref, wuq_ref, wuk_ref, permt_ref, cos_ref, sin_ref, csk_ref, snk_ref,
             dx_ref, dwin_ref, dwuq_ref, dwuk_ref, dgq_ref, dgkv_ref, dsc1_ref, dsh1_ref, dgmix_ref, dwin_acc,
             dwuq_acc, dwuk_acc, wuk_bd):
        i = pl.program_id(0)

        @pl.when(i == 0)
        def _():
            dwin_acc[...] = jnp.zeros_like(dwin_acc)
            dwuq_acc[...] = jnp.zeros_like(dwuq_acc)
            dwuk_acc[...] = jnp.zeros_like(dwuk_acc)
            _fill_block_diagonal(wuk_bd, wuk_ref)
            for r in (dgq_ref, dgkv_ref, dsc1_ref, dsh1_ref, dgmix_ref):
                r[...] = jnp.zeros_like(r)

        dq_blocks = [dqt_ref[a].astype(F32).T for a in range(nsub)]
        dq_heads = [jnp.concatenate([blk[hd * TQ:(hd + 1) * TQ, :] for blk in dq_blocks], axis=0)
                    for hd in range(HEADS)]
        dq_lat = jnp.concatenate([dqh[:, 0:KVL] for dqh in dq_heads], axis=1).astype(BF)
        dq_rope = jnp.concatenate([dqh[:, KVL:QW] for dqh in dq_heads], axis=1).astype(BF)
        dq_nope = _dot(dq_lat, wuk_bd[...])
        dwuk_acc[...] += _dot_tn(dq_lat, qn_ref[...])
        drope = _dot(dq_rope, permt_ref[...])
        do1 = drope[:, 0:128]
        do2 = drope[:, 128:256]
        cosv = cos_ref[...]
        sinv = sin_ref[...]
        dq = jnp.concatenate([dq_nope, do1 * cosv + do2 * sinv, do2 * cosv - do1 * sinv], axis=1).astype(BF)

        cq_raw = raw_ref[:, 0:QL]
        ckv_raw = raw_ref[:, QL:QL + KVL]
        rq = _rms(cq_raw)
        nq_ = cq_raw * rq
        gq = gq_ref[...]
        dwuq_acc[...] += _dot_tn((nq_ * gq).astype(BF), dq)
        dc_q = _dot_nt(dq, wuq_ref[...])
        dgq_ref[...] += _colsum(dc_q * nq_)
        dcq_raw = _rms_bwd(dc_q * gq, nq_, rq)

        dkv = dkv_ref[...]
        rk = _rms(ckv_raw)
        nk_ = ckv_raw * rk
        dc_kv = dkv[:, 0:KVL]
        dgkv_ref[...] += _colsum(dc_kv * nk_)
        dckv_raw = _rms_bwd(dc_kv * gkv_ref[...], nk_, rk)
        dkr_roped = dkv[:, KVL:QW]
        dkr = dkr_roped * csk_ref[...] - _swap_halves(dkr_roped) * snk_ref[...]

        dproj = jnp.concatenate([dcq_raw.astype(BF), dckv_raw.astype(BF), dkr.astype(BF), du_ref[...]], axis=1)
        dwin_acc[...] += _dot_tn(h1_ref[...], dproj)
        dh1 = _dot_nt(dproj, win_ref[...])

        sc1 = mod_ref[0:1, D:2 * D]
        gmix = gmix_ref[...]
        xv = x_ref[...]
        r1 = _rms(xv)
        xn1 = xv * r1
        along = _colsum(dh1 * xn1)
        dsc1_ref[...] += along * gmix
        dsh1_ref[...] += _colsum(dh1)
        dgmix_ref[...] += along * (1.0 + sc1)
        dx_ref[...] = dx2_ref[...] + _rms_bwd(dh1 * (gmix * (1.0 + sc1)), xn1, r1)

        @pl.when(i == n - 1)
        def _():
            dwin_ref[...] = dwin_acc[...].astype(BF)
            dwuq_ref[...] = dwuq_acc[...].astype(BF)
            for hd in range(HEADS):
                dwuk_ref[hd] = dwuk_acc[hd * KVL:(hd + 1) * KVL, hd * NOPE:(hd + 1) * NOPE]

    out_shape = (
        jax.ShapeDtypeStruct((S, D), F32),
        jax.ShapeDtypeStruct((D, D), BF),
        jax.ShapeDtypeStruct((QL, 768), BF),
        jax.ShapeDtypeStruct((HEADS, KVL, NOPE), F32),
        jax.ShapeDtypeStruct((1, QL), F32), jax.ShapeDtypeStruct((1, KVL), F32),
        jax.ShapeDtypeStruct((1, D), F32), jax.ShapeDtypeStruct((1, D), F32), jax.ShapeDtypeStruct((1, D), F32),
    )
    in_specs = [pl.BlockSpec((nsub, QW, M), lambda i: (i, 0, 0)), _rows(ts, QW), _rows(ts, PW), _rows(ts, 384),
                _rows(ts, HEADS * NOPE), _rows(ts, D), _rows(ts, D), _rows(ts, D), _full(mod.shape), _full((1, D)),
                _full(w_in.shape), _full((1, QL)), _full((1, KVL)), _full(w_uq.shape), _full(wuk_cd.shape),
                _full(perm_t.shape), _rows(ts, 128), _rows(ts, 128), _rows(ts, 128), _rows(ts, 128)]
    out_specs = (_rows(ts, D), _full((D, D)), _full((QL, 768)), _full((HEADS, KVL, NOPE)), _full((1, QL)),
                 _full((1, KVL)), _full((1, D)), _full((1, D)), _full((1, D)))
    return pl.pallas_call(
        body, name="in_bwd", out_shape=out_shape, grid=(n,), in_specs=in_specs, out_specs=out_specs,
        scratch_shapes=[pltpu.VMEM((D, D), F32), pltpu.VMEM((QL, 768), F32),
                        pltpu.VMEM((HEADS * KVL, HEADS * NOPE), F32), pltpu.VMEM((HEADS * KVL, HEADS * NOPE), BF)],
        compiler_params=_params(("arbitrary",)),
    )(dqt, dkv, du, raw, qn, h1, x, dx2, mod, g_mix, w_in, g_q, g_kv, w_uq, wuk_cd, perm_t, cos4, sin4, csk, snk)


def _rope_perm():
    p = np.zeros((HEADS, 2 * 128, 128), np.float32)
    for hd in range(HEADS):
        for t in range(HALF):
            p[hd, hd * HALF + t, t] = 1.0
            p[hd, 128 + hd * HALF + t, HALF + t] = 1.0
    return p


def _rope_tables(positions):
    freqs = jnp.power(ROPE_THETA, -jnp.arange(HALF, dtype=F32) / HALF)
    ang = positions.astype(F32)[:, None] * jnp.tile(freqs, HEADS)[None, :]
    cos4 = jnp.cos(ang)
    sin4 = jnp.sin(ang)
    lane = jnp.arange(HEADS * HALF)[None, :]
    csk = jnp.where(lane < ROPE, cos4, 0.0)
    snk = jnp.where(lane < HALF, -sin4, jnp.where(lane < ROPE, sin4, 0.0))
    return cos4, sin4, csk, snk


def _local_step(x, rope, target, mod, g_mix, w_in_p, g_q, g_kv, w_uq_p, w_uk, w_uv, w_pool, pool_scale, g_ffn,
                g_final, late, ffn_grads_exchange):
    perm = jnp.asarray(_rope_perm().transpose(1, 0, 2).reshape(2 * 128, HEADS * 128), BF)
    perm_t = jnp.asarray(_rope_perm().transpose(0, 2, 1).reshape(HEADS * 128, 2 * 128), BF)
    cos4, sin4, csk, snk = rope
    wuk_dc = w_uk.transpose(1, 2, 0).astype(BF)
    wuk_cd = w_uk.transpose(1, 0, 2).astype(BF)
    wuv_vc = w_uv.transpose(1, 2, 0).astype(BF)
    wpool = w_pool.astype(BF)
    wpool_dc = w_pool.transpose(0, 2, 1).astype(BF)

    h1, raw, qn, qs, kv, kvt, pooled, ypre, ypool = _fwd_in(
        x, mod, g_mix, w_in_p, g_q, g_kv, w_uq_p, wuk_dc, perm, cos4, sin4, csk, snk, wpool, pool_scale)
    olat, ymla, lse = _attn_fwd(qs, kv, kvt, wuv_vc)
    w_o, wg_t, wu_t, wd = late
    x2, mix, h2_t, a, b, dx3, dff, dff_t, loss, dgfin, dgt2 = _ffn_fwd(
        x, ymla, ypool, mod, w_o, g_ffn, wg_t, wu_t, wd, g_final, target)
    da, db, dh2 = _ffn_bwd_acts(dff, a, b, wg_t, wu_t, wd)
    (dx2, du, dolat, delta, dwo, dwuv, dwpool, dpscale, dgt1, dsc2, dsh2, dgffn) = _mix_bwd(
        dh2, dx3, x2, mix, mod, g_ffn, ymla, ypool, w_o, ypre, pooled, pool_scale, wpool_dc, olat, wuv_vc)
    dwg_t, dwu_t, dwd = _ffn_bwd_weights(dff_t, h2_t, da, db, a, b)
    ffn_parts = ffn_grads_exchange((dwg_t, dwu_t, dwd, dwo))
    dkv, dqt = _attn_bwd(qs, kv, dolat, lse, delta)
    dx, dwin, dwuq, dwuk, dgq, dgkv, dsc1, dsh1, dgmix = _in_bwd(
        dqt, dkv, du, raw, qn, h1, x, dx2, mod, g_mix, w_in_p, g_q, g_kv, w_uq_p, wuk_cd, perm_t, cos4, sin4, csk,
        snk)
    dmod = jnp.concatenate([dsh1, dsc1, dgt1, dsh2, dsc2, dgt2], axis=1)
    replicated = dict(
        w_uk=dwuk.transpose(1, 0, 2), w_uv=dwuv.transpose(1, 0, 2), w_pool=dwpool, g_mix=dgmix, g_q=dgq, g_kv=dgkv,
        pool_scale=dpscale, g_ffn=dgffn, g_final=dgfin)
    return loss[0, 0], dx, dmod, (dwin, dwuq), ffn_parts, replicated


def _my_pos():
    return lax.axis_index("x"), lax.axis_index("y"), lax.axis_index("c")


def _peer(pos, k):
    x, y, c = pos
    return (1 - x if k & 4 else x, 1 - y if k & 2 else y, 1 - c if k & 1 else c)


def _index(pos):
    x, y, c = pos
    return 4 * x + 2 * y + c


def _remote(src, dst, send_sem, recv_sem, to):
    return pltpu.make_async_remote_copy(src_ref=src, dst_ref=dst, send_sem=send_sem, recv_sem=recv_sem,
                                        device_id=to, device_id_type=MESH)


def _ada_mod(c, w_ada, b_ada, after):
    def body(c_ref, w_ref, b_ref, after_ref, mod_ref, call_ref, cbuf, sbuf, rbuf, send1, recv1, send2, recv2):
        me = _my_pos()
        mi = _index(me)
        cv = c_ref[...]
        cbuf[...] = jnp.broadcast_to(cv * jax.nn.sigmoid(cv), (8, D))
        call_ref[mi] = cbuf[...]
        first = [_remote(cbuf, call_ref.at[mi], send1.at[k - 1], recv1.at[k - 1], _peer(me, k)) for k in range(1, NDEV)]
        for cp in first:
            cp.start()
        for k in range(1, NDEV):
            _remote(cbuf, call_ref.at[_index(_peer(me, k))], send1.at[k - 1], recv1.at[k - 1], _peer(me, k)).wait_recv()
        c_all = jnp.concatenate([call_ref[b][0:1, :] for b in range(NDEV)], axis=0)
        blocks = _dot(c_all.astype(BF), w_ref[...].astype(BF))
        for b in range(NDEV):
            sbuf[b] = jnp.broadcast_to(blocks[b:b + 1, :], (8, MODC))
        second = []
        for k in range(1, NDEV):
            to = _peer(me, k)
            second.append(_remote(sbuf.at[_index(to)], rbuf.at[mi], send2.at[k - 1], recv2.at[k - 1], to))
        for cp in second:
            cp.start()
        rbuf[mi] = sbuf[mi]
        for k in range(1, NDEV):
            to = _peer(me, k)
            _remote(sbuf.at[_index(to)], rbuf.at[_index(to)], send2.at[k - 1], recv2.at[k - 1], to).wait_recv()
        for j in range(NDEV):
            mod_ref[:, j * MODC:(j + 1) * MODC] = rbuf[j] + b_ref[:, j * MODC:(j + 1) * MODC]
        for cp in first + second:
            cp.wait_send()

    return pl.pallas_call(
        body, name="ada_mod",
        out_shape=(jax.ShapeDtypeStruct((8, N_MOD * D), F32), jax.ShapeDtypeStruct((NDEV, 8, D), F32)),
        in_specs=[_vmem(), _vmem(), _vmem(), _any()], out_specs=(_vmem(), _vmem()),
        scratch_shapes=[pltpu.VMEM((8, D), F32), pltpu.VMEM((NDEV, 8, MODC), F32), pltpu.VMEM((NDEV, 8, MODC), F32),
                        pltpu.SemaphoreType.DMA((NDEV - 1,)), pltpu.SemaphoreType.DMA((NDEV - 1,)),
                        pltpu.SemaphoreType.DMA((NDEV - 1,)), pltpu.SemaphoreType.DMA((NDEV - 1,))],
        compiler_params=_params(),
    )(c, w_ada, b_ada, after)


def _sequencer_scatter(name, collective_id, srcs, after=()):
    n = len(srcs)

    def of(src, to_index):
        r = src.shape[0] // NDEV
        return src.at[pl.ds(pl.multiple_of(to_index * r, 16), r), :]

    def body(*refs):
        src, zone = refs[:n], refs[n + len(after):2 * n + len(after)]
        send, recv, local = refs[2 * n + len(after):]
        me = _my_pos()
        mi = _index(me)
        barrier = pltpu.get_barrier_semaphore()
        for k in range(1, NDEV):
            pl.semaphore_signal(barrier, inc=1, device_id=_peer(me, k), device_id_type=MESH)
        pl.semaphore_wait(barrier, NDEV - 1)
        own = [pltpu.make_async_copy(of(src[a], mi), zone[a].at[mi], local.at[a]) for a in range(n)]
        for cp in own:
            cp.start()
        for a in range(n):
            for k in range(1, NDEV):
                to = _peer(me, k)
                s = a * (NDEV - 1) + k - 1
                _remote(of(src[a], _index(to)), zone[a].at[mi], send.at[s], recv.at[s], to).start()
        for cp in own:
            cp.wait()
        for a in range(n):
            for k in range(1, NDEV):
                to = _peer(me, k)
                s = a * (NDEV - 1) + k - 1
                cp = _remote(of(src[a], mi), zone[a].at[_index(to)], send.at[s], recv.at[s], to)
                cp.wait_send()
                cp.wait_recv()

    return pl.kernel(
        body, name=name, mesh=plsc.ScalarSubcoreMesh(axis_name="sequencer", num_cores=1),
        out_type=tuple(jax.ShapeDtypeStruct((NDEV, s.shape[0] // NDEV, s.shape[1]), s.dtype) for s in srcs),
        scratch_types=[pltpu.SemaphoreType.DMA((n * (NDEV - 1),)), pltpu.SemaphoreType.DMA((n * (NDEV - 1),)),
                       pltpu.SemaphoreType.DMA((n,))],
        compiler_params=pltpu.CompilerParams(collective_id=collective_id),
    )(*srcs, *after)


CHIP_PEERS = (2, 4, 6)


def _sequencer_gather(name, collective_id, srcs, after=()):
    n = len(srcs)
    per = NDEV - 1

    def body(*refs):
        src, zone = refs[:n], refs[n + len(after):2 * n + len(after)]
        send, recv, local = refs[2 * n + len(after):]
        me = _my_pos()
        mi = _index(me)
        sibling = _peer(me, 1)
        talk_to = (sibling,) + tuple(_peer(me, k) for k in CHIP_PEERS)
        barrier = pltpu.get_barrier_semaphore()
        for to in talk_to:
            pl.semaphore_signal(barrier, inc=1, device_id=to, device_id_type=MESH)
        pl.semaphore_wait(barrier, len(talk_to))

        def copy(a, slot, block_of, to, from_src=False):
            rows = zone[a].at[_index(block_of)]
            return _remote(src[a] if from_src else rows, rows, send.at[a * per + slot], recv.at[a * per + slot], to)

        own = [pltpu.make_async_copy(src[a], zone[a].at[mi], local.at[a]) for a in range(n)]
        for cp in own:
            cp.start()
        started = []
        for a in range(n):
            started.append(copy(a, 0, me, sibling, from_src=True))
            started += [copy(a, 1 + j, me, _peer(me, k), from_src=True) for j, k in enumerate(CHIP_PEERS)]
        for cp in started:
            cp.start()
        for a in range(n):
            for j, k in enumerate(CHIP_PEERS):
                copy(a, 1 + j, _peer(me, k), me).wait_recv()
                passed = copy(a, 4 + j, _peer(me, k), sibling)
                passed.start()
                started.append(passed)
        for a in range(n):
            copy(a, 0, sibling, me).wait_recv()
            for j, k in enumerate(CHIP_PEERS):
                copy(a, 4 + j, _peer(me, k | 1), me).wait_recv()
        for cp in started:
            cp.wait_send()
        for cp in own:
            cp.wait()

    return pl.kernel(
        body, name=name, mesh=plsc.ScalarSubcoreMesh(axis_name="sequencer", num_cores=1),
        out_type=tuple(jax.ShapeDtypeStruct((NDEV,) + s.shape, s.dtype) for s in srcs),
        scratch_types=[pltpu.SemaphoreType.DMA((n * per,)), pltpu.SemaphoreType.DMA((n * per,)),
                       pltpu.SemaphoreType.DMA((n,))],
        compiler_params=pltpu.CompilerParams(collective_id=collective_id),
    )(*srcs, *after)


def _blocked(shape, nb, axis=0):
    block = tuple(s // nb if d == axis else s for d, s in enumerate(shape))
    return pl.BlockSpec(block, lambda i: tuple(i if d == axis else 0 for d in range(len(shape))))


def _sum_partials(name, parts, nb):
    n = len(parts)

    def body(*refs):
        for a in range(n):
            acc = refs[a][0].astype(F32)
            for p in range(1, NDEV):
                acc = acc + refs[a][p].astype(F32)
            refs[n + a][...] = acc

    return pl.pallas_call(
        body, name=name, grid=(nb,),
        out_shape=tuple(jax.ShapeDtypeStruct(p.shape[1:], F32) for p in parts),
        in_specs=[_blocked(p.shape, nb, 1) for p in parts],
        out_specs=tuple(_blocked(p.shape[1:], nb) for p in parts), compiler_params=_params(("arbitrary",)),
    )(*parts)


def _small_all_reduce(buf):
    def body(buf_ref, got_ref, red_ref, mine, send1, recv1, send2, recv2):
        me = _my_pos()
        mi = _index(me)
        first = []
        for k in range(1, NDEV):
            to = _peer(me, k)
            first.append(_remote(buf_ref.at[_index(to)], got_ref.at[mi], send1.at[k - 1], recv1.at[k - 1], to))
        for cp in first:
            cp.start()
        got_ref[mi] = buf_ref[mi]
        for k in range(1, NDEV):
            to = _peer(me, k)
            _remote(buf_ref.at[mi], got_ref.at[_index(to)], send1.at[k - 1], recv1.at[k - 1], to).wait_recv()
        acc = got_ref[0]
        for p in range(1, NDEV):
            acc = acc + got_ref[p]
        mine[...] = acc
        second = [_remote(mine, red_ref.at[mi], send2.at[k - 1], recv2.at[k - 1], _peer(me, k)) for k in range(1, NDEV)]
        for cp in second:
            cp.start()
        red_ref[mi] = acc
        for k in range(1, NDEV):
            to = _peer(me, k)
            _remote(mine, red_ref.at[_index(to)], send2.at[k - 1], recv2.at[k - 1], to).wait_recv()
        for cp in first + second:
            cp.wait_send()

    return pl.pallas_call(
        body, name="small_all_reduce",
        out_shape=(jax.ShapeDtypeStruct(buf.shape, F32), jax.ShapeDtypeStruct(buf.shape, F32)),
        in_specs=[_vmem()], out_specs=(_vmem(), _vmem()),
        scratch_shapes=[pltpu.VMEM(buf.shape[1:], F32),
                        pltpu.SemaphoreType.DMA((NDEV - 1,)), pltpu.SemaphoreType.DMA((NDEV - 1,)),
                        pltpu.SemaphoreType.DMA((NDEV - 1,)), pltpu.SemaphoreType.DMA((NDEV - 1,))],
        compiler_params=_params(),
    )(buf)


def _adamw_math(w, g, m, v):
    m = ADAM_B1 * m + (1.0 - ADAM_B1) * g
    v = ADAM_B2 * v + (1.0 - ADAM_B2) * jnp.square(g)
    m_hat = m / (1.0 - ADAM_B1 ** ADAM_STEP)
    v_hat = v / (1.0 - ADAM_B2 ** ADAM_STEP)
    delta = -ADAM_LR * (m_hat / (jnp.sqrt(v_hat) + ADAM_EPS) + ADAM_WD * w)
    return delta, m, v


def _adamw_group(name, ws, gs, ms, vs, nb):
    n = len(ws)

    def body(*refs):
        for a in range(n):
            w, g, m, v = (refs[q * n + a][...] for q in range(4))
            delta, m2, v2 = _adamw_math(w, g, m, v)
            refs[4 * n + a][...] = delta
            refs[5 * n + a][...] = m2
            refs[6 * n + a][...] = v2

    shapes = tuple(jax.ShapeDtypeStruct(w.shape, F32) for w in ws)
    specs = [_blocked(w.shape, nb) for w in ws]
    outs = pl.pallas_call(
        body, name=name, grid=(nb,), out_shape=shapes * 3, in_specs=specs * 4, out_specs=tuple(specs * 3),
        compiler_params=_params(("arbitrary",)),
    )(*ws, *gs, *ms, *vs)
    return outs[:n], outs[n:2 * n], outs[2 * n:]


def _adamw_from_partials(name, ws, parts, ms, vs, nb):
    n = len(ws)

    def body(*refs):
        for a in range(n):
            part = refs[n + a]
            g = part[0].astype(F32)
            for p in range(1, NDEV):
                g = g + part[p].astype(F32)
            delta, m2, v2 = _adamw_math(refs[a][...], g, refs[2 * n + a][...], refs[3 * n + a][...])
            refs[4 * n + a][...] = g
            refs[5 * n + a][...] = delta
            refs[6 * n + a][...] = m2
            refs[7 * n + a][...] = v2

    shapes = tuple(jax.ShapeDtypeStruct(w.shape, F32) for w in ws)
    specs = [_blocked(w.shape, nb) for w in ws]
    outs = pl.pallas_call(
        body, name=name, grid=(nb,), out_shape=shapes * 4,
        in_specs=specs + [_blocked(p.shape, nb, 1) for p in parts] + specs * 2, out_specs=tuple(specs * 4),
        compiler_params=_params(("arbitrary",)),
    )(*ws, *parts, *ms, *vs)
    return outs[:n], outs[n:2 * n], outs[2 * n:3 * n], outs[3 * n:]


def _adamw_ada(w, m, v, c_all_t, dmod_rows):
    nb = 4

    def body(w_ref, m_ref, v_ref, c_ref, dm_ref, g_ref, d_ref, m2_ref, v2_ref):
        g = _dot(c_ref[...], dm_ref[...].astype(BF))
        g_ref[...] = g
        delta, m2, v2 = _adamw_math(w_ref[...], g, m_ref[...], v_ref[...])
        d_ref[...] = delta
        m2_ref[...] = m2
        v2_ref[...] = v2

    shp = jax.ShapeDtypeStruct(w.shape, F32)
    spec = _blocked(w.shape, nb)
    return pl.pallas_call(
        body, name="adamw_ada", grid=(nb,), out_shape=(shp, shp, shp, shp),
        in_specs=[spec, spec, spec, _blocked(c_all_t.shape, nb), _full(dmod_rows.shape)],
        out_specs=(spec, spec, spec, spec), compiler_params=_params(("arbitrary",)),
    )(w, m, v, c_all_t, dmod_rows)


def _w_in_to_kernel(w):
    return jnp.concatenate([w[:, 0:448], jnp.zeros((w.shape[0], 64), w.dtype), w[:, 448:960]], axis=1)


def _w_in_from_kernel(w):
    return jnp.concatenate([w[:, 0:448], w[:, 512:1024]], axis=1)


def _w_uq_to_kernel(w):
    r = w.shape[0]
    return jnp.concatenate([w[:, :, 0:NOPE].reshape(r, HEADS * NOPE),
                            w[:, :, NOPE:NOPE + HALF].reshape(r, HEADS * HALF),
                            w[:, :, NOPE + HALF:].reshape(r, HEADS * HALF)], axis=1)


def _w_uq_from_kernel(w):
    r = w.shape[0]
    return jnp.concatenate([w[:, 0:512].reshape(r, HEADS, NOPE), w[:, 512:640].reshape(r, HEADS, HALF),
                            w[:, 640:768].reshape(r, HEADS, HALF)], axis=2)


REP_NAMES = ("w_uk", "w_uv", "w_pool", "g_mix", "g_q", "g_kv", "pool_scale", "g_ffn", "g_final")


def kernel(x, c, positions, w_ada, b_ada, g_mix, w_in, g_q, g_kv, w_uq, w_uk, w_uv, w_pool, pool_scale, w_o, g_ffn, w_gate, w_up, w_down, g_final, loss_target, m_w_ada, m_b_ada, m_g_mix, m_w_in, m_g_q, m_g_kv, m_w_uq, m_w_uk, m_w_uv, m_w_pool, m_pool_scale, m_w_o, m_g_ffn, m_w_gate, m_w_up, m_w_down, m_g_final, v_w_ada, v_b_ada, v_g_mix, v_w_in, v_g_q, v_g_kv, v_w_uq, v_w_uk, v_w_uv, v_w_pool, v_pool_scale, v_w_o, v_g_ffn, v_w_gate, v_w_up, v_w_down, v_g_final):
    given = dict(locals())

    merge = lambda g: g.reshape(NDEV * g.shape[1], g.shape[2])
    w_in_p, w_uq_p = (merge(g) for g in _sequencer_gather(
        "gather_in", 3, (_w_in_to_kernel(w_in[0]).astype(BF), _w_uq_to_kernel(w_uq[0]).astype(BF))))

    rope = _rope_tables(positions[0])
    mod, c_all8 = _ada_mod(c, w_ada[0], b_ada, rope[3][0:8, :])
    c_all = c_all8[:, 0, :]
    late = _sequencer_gather(
        "gather_late", 1, (w_o[0].astype(BF), w_gate[0].T.astype(BF), w_up[0].T.astype(BF), w_down[0].astype(BF)),
        after=(mod[:, 0:128], w_in_p[0:16, 0:128], w_uq_p[0:16, 0:128]))

    def ffn_grads_exchange(arrays):
        return _sequencer_scatter("scatter_ffn", 2, arrays)

    loss, dx, dmod, tail_grads, ffn_parts, replicated = _local_step(
        x[0], rope, loss_target[0], mod, g_mix, w_in_p, g_q, g_kv, w_uq_p, w_uk[0], w_uv[0], w_pool[0],
        pool_scale, g_ffn, g_final.reshape(1, D), tuple(merge(g) for g in late), ffn_grads_exchange)

    flat = jnp.concatenate([replicated[k].reshape(-1) for k in REP_NAMES] + [loss.reshape(1)])
    flat = jnp.pad(flat, (0, NDEV * REP_ROWS * 128 - flat.shape[0])).reshape(NDEV, REP_ROWS, 128)
    dmod_blocks = jnp.pad(dmod.reshape(NDEV, MODC // 128, 128), ((0, 0), (0, MOD_ROWS - MODC // 128), (0, 0)))
    got, red = _small_all_reduce(jnp.concatenate([dmod_blocks, flat], axis=1))

    tail_parts = _sequencer_scatter("scatter_tail", 4, tail_grads,
                                    after=(ffn_parts[0][0, 0:16, 0:128], red[0, 0:8, :]))
    g_in_p, g_uq_p = _sum_partials("sum_tail_partials", tail_parts, 1)
    as_transpose = ("w_in", "w_gate", "w_up")
    grads = dict(w_in=_w_in_from_kernel(g_in_p).T, w_uq=_w_uq_from_kernel(g_uq_p))
    partials = dict(w_gate=ffn_parts[0], w_up=ffn_parts[1], w_down=ffn_parts[2], w_o=ffn_parts[3])
    dmod_rows = got[:, 0:MODC // 128, :].reshape(NDEV, MODC)
    grads["b_ada"] = red[:, 0:MODC // 128, :].reshape(1, N_MOD * D)
    rep_flat = red[:, MOD_ROWS:, :].reshape(-1)
    off = 0
    for k in REP_NAMES:
        size = int(np.prod(given[k].shape))
        grads[k] = rep_flat[off:off + size]
        off += size

    view = {k: (given[k].shape[1:] if given[k].ndim > 2 else given[k].shape)
            for k in REP_NAMES + ("b_ada", "w_ada", "w_in", "w_uq", "w_o", "w_gate", "w_up", "w_down")}
    view.update(g_final=(1, D))
    names = ["w_ada", "b_ada", "g_mix", "w_in", "g_q", "g_kv", "w_uq", "w_uk", "w_uv", "w_pool", "pool_scale",
             "w_o", "g_ffn", "w_gate", "w_up", "w_down", "g_final"]
    g_ada, d_ada, m_ada, v_ada = _adamw_ada(w_ada[0], m_w_ada[0], v_w_ada[0], c_all.T.astype(BF), dmod_rows)
    out_g, out_d, out_m, out_v = dict(w_ada=g_ada), dict(w_ada=d_ada), dict(w_ada=m_ada), dict(w_ada=v_ada)
    groups = (("adamw_ffn", ("w_gate", "w_up", "w_down", "w_o"), 4),
              ("adamw_replicated", REP_NAMES + ("b_ada",), 1),
              ("adamw_tail", ("w_in", "w_uq"), 1))
    for gname, members, nb in groups:
        turn = lambda k, t: t.T if k in as_transpose else t
        ws = [turn(k, given[k].reshape(view[k])) for k in members]
        ms = [turn(k, given["m_" + k].reshape(view[k])) for k in members]
        vs = [turn(k, given["v_" + k].reshape(view[k])) for k in members]
        if members[0] in partials:
            gs, ds, m2, v2 = _adamw_from_partials(gname, ws, [partials[k] for k in members], ms, vs, nb)
        else:
            gs = [grads[k] if k in as_transpose else grads[k].reshape(view[k]) for k in members]
            ds, m2, v2 = _adamw_group(gname, ws, gs, ms, vs, nb)
        for k, g, d, mm, vv in zip(members, gs, ds, m2, v2):
            out_g[k], out_d[k], out_m[k], out_v[k] = turn(k, g), turn(k, d), turn(k, mm), turn(k, vv)

    total = rep_flat[off]
    shaped = lambda d: [d[k].reshape(given[k].shape) for k in names]
    return (total, dx[None], *shaped(out_g), *shaped(out_d), *shaped(out_m), *shaped(out_v))
```

```python
import numpy as np
import jax
import jax.numpy as jnp
from jax import lax
from jax.experimental import pallas as pl
from jax.experimental.pallas import tpu as pltpu
from jax.experimental.pallas import tpu_sc as plsc

D = 1024
HEADS = 4
NOPE = 128
ROPE = 64
HALF = ROPE // 2
QL = 256
KVL = 128
FF = 2816
PW = 512
GROUPS = 4
GD = 128
N_MOD = 6
EPS = 1e-6
SM_SCALE = (NOPE + ROPE) ** -0.5
LOG2_E = 1.4426950408889634
EXP2_SCALE = SM_SCALE * LOG2_E
ROPE_THETA = 10000.0
NDEV = 8
MODC = N_MOD * D // NDEV

ADAM_LR = 0.001
ADAM_B1 = 0.9
ADAM_B2 = 0.999
ADAM_EPS = 1e-08
ADAM_WD = 0.01
ADAM_STEP = 10

BF = jnp.bfloat16
F32 = jnp.float32
VMEM_LIMIT_V7X = 60 * 1024 * 1024
MESH = pl.DeviceIdType.MESH

TQ = 512
TK = 512
QW = 256
VPU_ROWS = 16
MOD_ROWS = 8
REP_ROWS = 200
SMALL_ROWS = MOD_ROWS + REP_ROWS


def _params(sem=None):
    return pltpu.CompilerParams(dimension_semantics=sem, vmem_limit_bytes=VMEM_LIMIT_V7X)


def _dot(a, b):
    return jnp.dot(a, b, preferred_element_type=F32)


def _dot_nt(a, b):
    return lax.dot_general(a, b, (((1,), (1,)), ((), ())), preferred_element_type=F32)


def _dot_tn(a, b):
    return _dot(a.astype(F32).T.astype(BF), b)


def _full(shape):
    return pl.BlockSpec(shape, lambda *_: (0,) * len(shape))


def _rows(ts, cols):
    return pl.BlockSpec((ts, cols), lambda i: (i, 0))


def _vmem():
    return pl.BlockSpec(memory_space=pltpu.VMEM)


def _any():
    return pl.BlockSpec(memory_space=pl.ANY)


def _rms(v):
    return lax.rsqrt(jnp.mean(v * v, axis=-1, keepdims=True) + EPS)


def _rms_bwd(dn, n, r):
    return r * (dn - n * jnp.mean(dn * n, axis=-1, keepdims=True))


def _colsum(v):
    return jnp.sum(v, axis=0, keepdims=True)


def _swap_halves(v):
    lane = lax.broadcasted_iota(jnp.int32, v.shape, 1)
    return jnp.where(lane < HALF, pltpu.roll(v, 128 - HALF, 1), pltpu.roll(v, HALF, 1))


def _window_lane_width():
    lane = lax.broadcasted_iota(jnp.int32, (1, PW), 1)
    return jnp.where(lane < 128, 2.0, jnp.where(lane < 256, 4.0, jnp.where(lane < 384, 8.0, 16.0))).astype(F32)


def _window_sums(ext, back):
    n = ext.shape[0]

    def sh(v, k):
        return pltpu.roll(v, k if back else n - k, 0)

    s2 = ext + sh(ext, 1)
    e4 = s2[:, 128:]
    s4 = e4 + sh(e4, 2)
    e8 = s4[:, 128:]
    s8 = e8 + sh(e8, 4)
    e16 = s8[:, 128:]
    s16 = e16 + sh(e16, 8)
    return jnp.concatenate([s2[:, :128], s4[:, :128], s8[:, :128], s16], axis=1)


def _fill_block_diagonal(dst_ref, blocks_ref):
    n, r, c = blocks_ref.shape
    dst_ref[...] = jnp.zeros_like(dst_ref)
    for b in range(n):
        dst_ref[b * r:(b + 1) * r, b * c:(b + 1) * c] = blocks_ref[b]


def _row_counts(first_row, ts):
    t1 = (first_row + lax.broadcasted_iota(jnp.int32, (ts, 1), 0) + 1).astype(F32)
    return jnp.minimum(t1, _window_lane_width())


def _fwd_in(x, mod, g_mix, w_in, g_q, g_kv, w_uq, wuk_dc, perm, cos4, sin4, csk, snk, w_pool, pool_scale):
    S = x.shape[0]
    ts = 1024
    nsub = ts // TQ

    def body(x_ref, mod_ref, gmix_ref, win_ref, gq_ref, gkv_ref, wuq_ref, wuk_ref, perm_ref, cos_ref, sin_ref,
             csk_ref, snk_ref, wpool_ref, pscale_ref,
             h1_ref, raw_ref, qn_ref, qs_ref, kv_ref, kvt_ref, pooled_ref, ypre_ref, ypool_ref, carry_ref, wuk_bd,
             wpool_bd):
        i = pl.program_id(0)

        @pl.when(i == 0)
        def _():
            carry_ref[...] = jnp.zeros_like(carry_ref)
            _fill_block_diagonal(wuk_bd, wuk_ref)
            _fill_block_diagonal(wpool_bd, wpool_ref)

        xv = x_ref[...]
        sh1 = mod_ref[0:1, 0:D]
        sc1 = mod_ref[0:1, D:2 * D]
        h = (xv * _rms(xv)) * gmix_ref[...] * (1.0 + sc1) + sh1
        hb = h.astype(BF)
        h1_ref[...] = hb
        proj = _dot(hb, win_ref[...])
        cq_raw = proj[:, 0:QL]
        ckv_raw = proj[:, QL:QL + KVL]
        kr = proj[:, 384:512]
        u = proj[:, 512:1024]
        raw_ref[...] = proj[:, 0:384]

        c_q = (cq_raw * _rms(cq_raw)) * gq_ref[...]
        c_kv = (ckv_raw * _rms(ckv_raw)) * gkv_ref[...]
        q = _dot(c_q.astype(BF), wuq_ref[...])
        qn = q[:, 0:HEADS * NOPE].astype(BF)
        qn_ref[...] = qn
        x1 = q[:, 512:640]
        x2 = q[:, 640:768]
        cosv = cos_ref[...]
        sinv = sin_ref[...]
        roped = jnp.concatenate([x1 * cosv - x2 * sinv, x1 * sinv + x2 * cosv], axis=1).astype(BF)
        q_lat = _dot(qn, wuk_bd[...])
        q_rope = _dot(roped, perm_ref[...])
        for hd in range(HEADS):
            cols = slice(hd * 128, (hd + 1) * 128)
            qh = jnp.concatenate([q_lat[:, cols], q_rope[:, cols]], axis=1).astype(BF)
            for a in range(nsub):
                qs_ref[a, hd * TQ:(hd + 1) * TQ, :] = qh[a * TQ:(a + 1) * TQ, :]
        k_rope = kr * csk_ref[...] + _swap_halves(kr) * snk_ref[...]
        keys = jnp.concatenate([c_kv, k_rope], axis=1)
        kv_ref[...] = keys.astype(BF)
        for a in range(ts // TK):
            kvt_ref[a] = keys[a * TK:(a + 1) * TK, :].T.astype(BF)

        ext = jnp.concatenate([carry_ref[...], u], axis=0)
        win = _window_sums(ext, True)[16:, :]
        pooled = (win / _row_counts(i * ts, ts) - u).astype(BF)
        pooled_ref[...] = pooled
        carry_ref[...] = u[ts - 16:ts, :]
        ypre = _dot(pooled, wpool_bd[...])
        ypre_ref[...] = ypre.astype(BF)
        ypool_ref[...] = (ypre * pscale_ref[...]).astype(BF)

    out_shape = (
        jax.ShapeDtypeStruct((S, D), BF),
        jax.ShapeDtypeStruct((S, 384), F32),
        jax.ShapeDtypeStruct((S, HEADS * NOPE), BF),
        jax.ShapeDtypeStruct((S // TQ, HEADS * TQ, QW), BF),
        jax.ShapeDtypeStruct((S, QW), BF),
        jax.ShapeDtypeStruct((S // TK, QW, TK), BF),
        jax.ShapeDtypeStruct((S, PW), BF),
        jax.ShapeDtypeStruct((S, PW), BF),
        jax.ShapeDtypeStruct((S, PW), BF),
    )
    in_specs = [
        _rows(ts, D), _full(mod.shape), _full((1, D)), _full(w_in.shape), _full((1, QL)), _full((1, KVL)),
        _full(w_uq.shape), _full(wuk_dc.shape), _full(perm.shape), _rows(ts, 128), _rows(ts, 128), _rows(ts, 128),
        _rows(ts, 128), _full(w_pool.shape), _full((1, PW)),
    ]
    out_specs = (
        _rows(ts, D), _rows(ts, 384), _rows(ts, HEADS * NOPE),
        pl.BlockSpec((nsub, HEADS * TQ, QW), lambda i: (i, 0, 0)),
        _rows(ts, QW), pl.BlockSpec((ts // TK, QW, TK), lambda i: (i, 0, 0)), _rows(ts, PW), _rows(ts, PW),
        _rows(ts, PW),
    )
    return pl.pallas_call(
        body, name="fwd_in", out_shape=out_shape, grid=(S // ts,), in_specs=in_specs, out_specs=out_specs,
        scratch_shapes=[pltpu.VMEM((16, PW), F32), pltpu.VMEM((HEADS * NOPE, HEADS * KVL), BF),
                        pltpu.VMEM((PW, PW), BF)],
        compiler_params=_params(("arbitrary",)),
    )(x, mod, g_mix, w_in, g_q, g_kv, w_uq, wuk_dc, perm, cos4, sin4, csk, snk, w_pool, pool_scale)


def _diag_mask(shape, q_axis, first_chunk):
    qi = (lax.broadcasted_iota(jnp.int32, shape, q_axis) & (TQ - 1)) >> 6
    ki = (lax.broadcasted_iota(jnp.int32, shape, 1 - q_axis) >> 6) + first_chunk
    return ki <= qi


def _attn_fwd(qs, kv, kvt, wuv_vc):
    nq = qs.shape[0]
    S = kv.shape[0]
    M = HEADS * TQ

    def body(qs_ref, kv_ref, kvt_ref, wuv_ref, olat_ref, ymla_ref, lse_ref):
        i = pl.program_id(0)
        q = qs_ref[0]

        def step(kt, carry, first_chunk=None):
            m, l, acc = carry
            k = kv_ref[pl.ds(pl.multiple_of(kt * TK, TK), TK), :]
            v_t = kvt_ref[kt][0:KVL, :]
            s = _dot_nt(k, q)
            if first_chunk is not None:
                s = jnp.where(_diag_mask((TK, M), 1, first_chunk), s, -jnp.inf)
            m_new = jnp.maximum(m, jnp.max(s, axis=0, keepdims=True))
            alpha = jnp.exp2((m - m_new) * EXP2_SCALE)
            p = jnp.exp2((s - m_new) * EXP2_SCALE)
            l = alpha * l + jnp.sum(p, axis=0, keepdims=True)
            acc = alpha * acc + _dot(v_t, p.astype(BF))
            return m_new, l, acc

        init = (jnp.full((1, M), -jnp.inf, F32), jnp.zeros((1, M), F32), jnp.zeros((KVL, M), F32))
        per = TQ // TK
        carry = lax.fori_loop(0, per * i, step, init)
        for j in range(per):
            carry = step(per * i + j, carry, j * (TK // 64))
        m, l, acc = carry
        o_lat = acc / l
        olat_ref[0] = o_lat
        lse_ref[0] = jnp.broadcast_to(m * SM_SCALE + jnp.log(l), (8, M))
        for hd in range(HEADS):
            o_t = _dot(wuv_ref[hd], o_lat[:, hd * TQ:(hd + 1) * TQ].astype(BF))
            ymla_ref[:, hd * 128:(hd + 1) * 128] = o_t.T.astype(BF)

    out_shape = (
        jax.ShapeDtypeStruct((nq, KVL, M), F32),
        jax.ShapeDtypeStruct((S, HEADS * 128), BF),
        jax.ShapeDtypeStruct((nq, 8, M), F32),
    )
    return pl.pallas_call(
        body, name="attn_fwd", out_shape=out_shape, grid=(nq,),
        in_specs=[pl.BlockSpec((1, M, QW), lambda i: (i, 0, 0)), _full(kv.shape), _full(kvt.shape),
                  _full(wuv_vc.shape)],
        out_specs=(pl.BlockSpec((1, KVL, M), lambda i: (i, 0, 0)), _rows(TQ, HEADS * 128),
                   pl.BlockSpec((1, 8, M), lambda i: (i, 0, 0))),
        compiler_params=_params(("arbitrary",)),
    )(qs, kv, kvt, wuv_vc)


def _silu_parts(a):
    sg = jax.nn.sigmoid(a)
    return sg, a * sg


def _ffn_fwd(x, ymla, ypool, mod, w_o, g_ffn, wg_t, wu_t, wd, g_final, target):
    S = x.shape[0]
    ts = 256

    def body(x_ref, ymla_ref, ypool_ref, mod_ref, wo_ref, gffn_ref, wg_ref, wu_ref, wd_ref, gfin_ref, t_ref,
             x2_ref, mix_ref, h2t_ref, a_ref, b_ref, dx3_ref, dff_ref, dfft_ref, loss_ref, dgfin_ref, dgt2_ref,
             f_ref):
        i = pl.program_id(0)

        @pl.when(i == 0)
        def _():
            loss_ref[...] = jnp.zeros_like(loss_ref)
            dgfin_ref[...] = jnp.zeros_like(dgfin_ref)
            dgt2_ref[...] = jnp.zeros_like(dgt2_ref)

        gt1 = mod_ref[0:1, 2 * D:3 * D]
        sh2 = mod_ref[0:1, 3 * D:4 * D]
        sc2 = mod_ref[0:1, 4 * D:5 * D]
        gt2 = mod_ref[0:1, 5 * D:6 * D]
        cat = jnp.concatenate([ymla_ref[...], ypool_ref[...]], axis=1)
        mix = _dot(cat, wo_ref[...])
        mix_ref[...] = mix.astype(BF)
        x2 = x_ref[...] + gt1 * mix
        x2_ref[...] = x2
        h2 = (x2 * _rms(x2)) * gffn_ref[...] * (1.0 + sc2) + sh2
        h2b = h2.astype(BF)
        h2t_ref[...] = h2.T.astype(BF)

        for c in range(FF // FCHUNK):
            cols = slice(c * FCHUNK, (c + 1) * FCHUNK)
            a = _dot_nt(h2b, wg_ref[cols, :])
            b = _dot_nt(h2b, wu_ref[cols, :])
            a_ref[:, cols] = a.astype(BF)
            b_ref[:, cols] = b.astype(BF)
            f_ref[:, cols] = (_silu_parts(a)[1] * b).astype(BF)
        ff = _dot(f_ref[...], wd_ref[...])

        x3 = x2 + gt2 * ff
        r3 = _rms(x3)
        xn3 = x3 * r3
        gfin = gfin_ref[...]
        e = xn3 * gfin - t_ref[...]
        loss_ref[...] += 0.5 * jnp.sum(jnp.mean(e * e, axis=-1, keepdims=True))
        dy = e * (1.0 / D)
        dgfin_ref[...] += _colsum(dy * xn3)
        dx3 = _rms_bwd(dy * gfin, xn3, r3)
        dx3_ref[...] = dx3
        dgt2_ref[...] += _colsum(dx3 * ff)
        dff = dx3 * gt2
        dff_ref[...] = dff.astype(BF)
        dfft_ref[...] = dff.T.astype(BF)

    row = lambda c: _rows(ts, c)
    col = pl.BlockSpec((D, ts), lambda i: (0, i))
    const = _full
    out_shape = (
        jax.ShapeDtypeStruct((S, D), F32),
        jax.ShapeDtypeStruct((S, D), BF),
        jax.ShapeDtypeStruct((D, S), BF),
        jax.ShapeDtypeStruct((S, FF), BF),
        jax.ShapeDtypeStruct((S, FF), BF),
        jax.ShapeDtypeStruct((S, D), F32),
        jax.ShapeDtypeStruct((S, D), BF),
        jax.ShapeDtypeStruct((D, S), BF),
        jax.ShapeDtypeStruct((8, 128), F32),
        jax.ShapeDtypeStruct((1, D), F32),
        jax.ShapeDtypeStruct((1, D), F32),
    )
    return pl.pallas_call(
        body, name="ffn_fwd", out_shape=out_shape, grid=(S // ts,),
        in_specs=[row(D), row(PW), row(PW), const(mod.shape), _vmem(), const((1, D)), _vmem(), _vmem(), _vmem(),
                  const((1, D)), row(D)],
        out_specs=(row(D), row(D), col, row(FF), row(FF), row(D), row(D), col, const((8, 128)), const((1, D)),
                   const((1, D))),
        scratch_shapes=[pltpu.VMEM((ts, FF), BF)],
        compiler_params=_params(("arbitrary",)),
    )(x, ymla, ypool, mod, w_o, g_ffn, wg_t, wu_t, wd, g_final, target)


FCHUNK = 256


def _ffn_bwd_acts(dff, a, b, wg_t, wu_t, wd):
    S = dff.shape[0]
    ts = 512

    def body(dff_ref, a_ref, b_ref, wg_ref, wu_ref, wd_ref, da_ref, db_ref, dh2_ref):
        dffb = dff_ref[...]
        for c in range(FF // FCHUNK):
            cols = slice(c * FCHUNK, (c + 1) * FCHUNK)
            df = _dot_nt(dffb, wd_ref[cols, :])
            av = a_ref[:, cols].astype(F32)
            bv = b_ref[:, cols].astype(F32)
            sg, sa = _silu_parts(av)
            db_ref[:, cols] = (df * sa).astype(BF)
            da_ref[:, cols] = (df * bv * (sg * (1.0 + av * (1.0 - sg)))).astype(BF)
        dh2_ref[...] = _dot(da_ref[...], wg_ref[...]) + _dot(db_ref[...], wu_ref[...])

    act = _rows(ts, FF)
    return pl.pallas_call(
        body, name="ffn_bwd_acts",
        out_shape=(jax.ShapeDtypeStruct((S, FF), BF), jax.ShapeDtypeStruct((S, FF), BF),
                   jax.ShapeDtypeStruct((S, D), F32)),
        grid=(S // ts,), in_specs=[_rows(ts, D), act, act, _vmem(), _vmem(), _vmem()],
        out_specs=(act, act, _rows(ts, D)), compiler_params=_params(("arbitrary",)),
    )(dff, a, b, wg_t, wu_t, wd)


def _ffn_bwd_weights(dff_t, h2_t, da, db, a, b):
    S = da.shape[0]

    def body(dfft_ref, h2t_ref, da_ref, db_ref, a_ref, b_ref, dwg_ref, dwu_ref, dwd_ref):
        h2t = h2t_ref[...]
        dwg_ref[...] = _dot(h2t, da_ref[...]).T.astype(BF)
        dwu_ref[...] = _dot(h2t, db_ref[...]).T.astype(BF)
        f = (_silu_parts(a_ref[...].astype(F32))[1] * b_ref[...].astype(F32)).astype(BF)
        dwd_ref[...] = _dot(dfft_ref[...], f).T.astype(BF)

    act = pl.BlockSpec((S, FCHUNK), lambda j: (0, j))
    wblk = _rows(FCHUNK, D)
    shp = jax.ShapeDtypeStruct((FF, D), BF)
    return pl.pallas_call(
        body, name="ffn_bwd_weights", out_shape=(shp, shp, shp), grid=(FF // FCHUNK,),
        in_specs=[_vmem(), _vmem(), act, act, act, act], out_specs=(wblk, wblk, wblk),
        compiler_params=_params(("arbitrary",)),
    )(dff_t, h2_t, da, db, a, b)


def _mix_bwd(dh2, dx3, x2, mix, mod, g_ffn, ymla, ypool, w_o, ypre, pooled, pool_scale, wpool_dc, olat, wuv_vc):
    S = dh2.shape[0]
    ts = 512
    n = S // ts
    nsub = ts // TQ
    M = HEADS * TQ

    def body(dh2_ref, dx3_ref, x2_ref, mix_ref, mod_ref, gffn_ref, ymla_ref, ypool_ref, wo_ref, ypre_ref, pooled_ref,
             pscale_ref, wpool_ref, olat_ref, wuv_ref,
             dx2_ref, du_ref, dolat_ref, delta_ref, dwo_ref, dwuv_ref, dwpool_ref, dpscale_ref, dgt1_ref, dsc2_ref,
             dsh2_ref, dgffn_ref, carry_ref, dwo_acc, dwpool_acc, wpool_bd, wuv_bd):
        i = pl.program_id(0)

        @pl.when(i == 0)
        def _():
            carry_ref[...] = jnp.zeros_like(carry_ref)
            dwo_acc[...] = jnp.zeros_like(dwo_acc)
            dwpool_acc[...] = jnp.zeros_like(dwpool_acc)
            _fill_block_diagonal(wpool_bd, wpool_ref)
            _fill_block_diagonal(wuv_bd, wuv_ref)
            for r in (dwuv_ref, dpscale_ref, dgt1_ref, dsc2_ref, dsh2_ref, dgffn_ref):
                r[...] = jnp.zeros_like(r)

        gt1 = mod_ref[0:1, 2 * D:3 * D]
        sc2 = mod_ref[0:1, 4 * D:5 * D]
        gffn = gffn_ref[...]
        dh2 = dh2_ref[...]
        x2 = x2_ref[...]
        r2 = _rms(x2)
        xn2 = x2 * r2
        along = _colsum(dh2 * xn2)
        dsc2_ref[...] += along * gffn
        dsh2_ref[...] += _colsum(dh2)
        dgffn_ref[...] += along * (1.0 + sc2)
        dx2 = dx3_ref[...] + _rms_bwd(dh2 * (gffn * (1.0 + sc2)), xn2, r2)
        dx2_ref[...] = dx2
        dgt1_ref[...] += _colsum(dx2 * mix_ref[...].astype(F32))
        dmix = (dx2 * gt1).astype(BF)
        cat = jnp.concatenate([ymla_ref[...], ypool_ref[...]], axis=1)
        dwo_acc[...] += _dot_tn(cat, dmix)
        dcat = _dot_nt(dmix, wo_ref[...])
        dymla = dcat[:, 0:512]
        dypool = dcat[:, 512:1024]

        dpscale_ref[...] += _colsum(dypool * ypre_ref[...].astype(F32))
        dypre = (dypool * pscale_ref[...]).astype(BF)
        dwpool_acc[...] += _dot_tn(pooled_ref[...], dypre)
        dpooled = _dot(dypre, wpool_bd[...])
        tile = n - 1 - i
        e = dpooled / _row_counts(tile * ts, ts)
        ext = jnp.concatenate([e, carry_ref[...]], axis=0)
        du_ref[...] = (_window_sums(ext, False)[0:ts, :] - dpooled).astype(BF)
        carry_ref[...] = e[0:16, :]

        dob_all = dymla.astype(BF)
        dol_all = _dot(dob_all, wuv_bd[...])
        for hd in range(HEADS):
            dob = dob_all[:, hd * 128:(hd + 1) * 128]
            dol = dol_all[:, hd * 128:(hd + 1) * 128]
            for a in range(nsub):
                ol_t = olat_ref[a, :, hd * TQ:(hd + 1) * TQ]
                dl = dol[a * TQ:(a + 1) * TQ, :]
                dolat_ref[a, hd * TQ:(hd + 1) * TQ, :] = dl.astype(BF)
                dwuv_ref[hd] += _dot(ol_t.astype(BF), dob[a * TQ:(a + 1) * TQ, :])
                delta = jnp.sum(dl * ol_t.T, axis=-1, keepdims=True)
                delta_ref[a, :, hd * TQ:(hd + 1) * TQ] = jnp.broadcast_to(delta, (TQ, 128)).T[0:8, :]

        @pl.when(i == n - 1)
        def _():
            dwo_ref[...] = dwo_acc[...].astype(BF)
            for g in range(GROUPS):
                dwpool_ref[g] = dwpool_acc[g * GD:(g + 1) * GD, g * GD:(g + 1) * GD]

    rev = lambda c: pl.BlockSpec((ts, c), lambda i: (n - 1 - i, 0))
    rev3 = lambda r, c: pl.BlockSpec((nsub, r, c), lambda i: (n - 1 - i, 0, 0))
    out_shape = (
        jax.ShapeDtypeStruct((S, D), F32),
        jax.ShapeDtypeStruct((S, PW), BF),
        jax.ShapeDtypeStruct((S // TQ, M, KVL), BF),
        jax.ShapeDtypeStruct((S // TQ, 8, M), F32),
        jax.ShapeDtypeStruct((D, D), BF),
        jax.ShapeDtypeStruct((HEADS, KVL, 128), F32),
        jax.ShapeDtypeStruct((GROUPS, GD, GD), F32),
        jax.ShapeDtypeStruct((1, PW), F32),
        jax.ShapeDtypeStruct((1, D), F32), jax.ShapeDtypeStruct((1, D), F32), jax.ShapeDtypeStruct((1, D), F32),
        jax.ShapeDtypeStruct((1, D), F32),
    )
    in_specs = [rev(D), rev(D), rev(D), rev(D), _full(mod.shape), _full((1, D)), rev(PW), rev(PW), _full(w_o.shape),
                rev(PW), rev(PW), _full((1, PW)), _full(wpool_dc.shape), rev3(KVL, M), _full(wuv_vc.shape)]
    out_specs = (rev(D), rev(PW), rev3(M, KVL), rev3(8, M), _full((D, D)), _full((HEADS, KVL, 128)),
                 _full((GROUPS, GD, GD)), _full((1, PW)), _full((1, D)), _full((1, D)), _full((1, D)), _full((1, D)))
    return pl.pallas_call(
        body, name="mix_bwd", out_shape=out_shape, grid=(n,), in_specs=in_specs, out_specs=out_specs,
        scratch_shapes=[pltpu.VMEM((16, PW), F32), pltpu.VMEM((D, D), F32), pltpu.VMEM((PW, PW), F32),
                        pltpu.VMEM((PW, PW), BF), pltpu.VMEM((HEADS * 128, HEADS * KVL), BF)],
        compiler_params=_params(("arbitrary",)),
    )(dh2, dx3, x2, mix, mod, g_ffn, ymla, ypool, w_o, ypre, pooled, pool_scale, wpool_dc, olat, wuv_vc)


def _attn_bwd(qs, kv, dolat, lse, delta):
    nq = qs.shape[0]
    S = kv.shape[0]
    M = HEADS * TQ
    nk = S // TK

    def body(qs_ref, kv_ref, do_ref, lse_ref, delta_ref, dkv_ref, dqt_out_ref, dqt_ref, p_ref, ds_ref):
        kt = pl.program_id(0)
        k = kv_ref[...]
        v = k[:, 0:KVL]
        k_t = k.astype(F32).T.astype(BF)

        @pl.when(kt == 0)
        def _():
            dqt_ref[...] = jnp.zeros_like(dqt_ref)

        def step(qi, carry, first_chunk=None):
            dk, dv = carry
            q = qs_ref[qi]
            do = do_ref[qi]
            s = _dot_nt(k, q)
            dp = _dot_nt(v, do)
            lse_row = lse_ref[qi, 0:1, :] * LOG2_E
            delta_row = delta_ref[qi, 0:1, :]
            q_chunk = (lax.broadcasted_iota(jnp.int32, (1, M), 1) & (TQ - 1)) >> 6
            for r in range(0, TK, VPU_ROWS):
                rows = slice(r, r + VPU_ROWS)
                p = jnp.exp2(s[rows, :] * EXP2_SCALE - lse_row)
                if first_chunk is not None:
                    p = jnp.where((r >> 6) + first_chunk <= q_chunk, p, 0.0)
                p_ref[rows, :] = p.astype(BF)
                ds_ref[rows, :] = (p * (dp[rows, :] - delta_row) * SM_SCALE).astype(BF)
            ds = ds_ref[...]
            dv = dv + _dot(p_ref[...], do)
            dk = dk + _dot(ds, q)
            dqt_ref[qi] += _dot(k_t, ds)
            return dk, dv

        per = TQ // TK
        first = kt // per
        carry = step(first, (jnp.zeros((TK, QW), F32), jnp.zeros((TK, KVL), F32)), (kt % per) * (TK // 64))
        dk, dv = lax.fori_loop(first + 1, nq, step, carry)
        dkv_ref[...] = dk + jnp.concatenate([dv, jnp.zeros((TK, QW - KVL), F32)], axis=1)
        dqt_out_ref[0] = dqt_ref[first].astype(BF)

    out_shape = (jax.ShapeDtypeStruct((S, QW), F32), jax.ShapeDtypeStruct((nq, QW, M), BF))
    return pl.pallas_call(
        body, name="attn_bwd", out_shape=out_shape, grid=(nk,),
        in_specs=[_vmem(), _rows(TK, QW), _vmem(), _vmem(), _vmem()],
        out_specs=(_rows(TK, QW), pl.BlockSpec((1, QW, M), lambda kt: (kt // (TQ // TK), 0, 0))),
        scratch_shapes=[pltpu.VMEM((nq, QW, M), F32), pltpu.VMEM((TK, M), BF), pltpu.VMEM((TK, M), BF)],
        compiler_params=_params(("arbitrary",)),
    )(qs, kv, dolat, lse, delta)


def _in_bwd(dqt, dkv, du, raw, qn, h1, x, dx2, mod, g_mix, w_in, g_q, g_kv, w_uq, wuk_cd, perm_t, cos4, sin4, csk,
            snk):
    S = x.shape[0]
    ts = 512
    n = S // ts
    nsub = ts // TQ
    M = HEADS * TQ

    def body(dqt_ref, dkv_ref, du_ref, raw_ref, qn_ref, h1_ref, x_ref, dx2_ref, mod_ref, gmix_ref, win_ref, gq_ref,
             gkv_ref, wuq_ref, wuk_ref, permt_ref, cos_ref, sin_ref, csk_ref, snk_ref,
             dx_ref, dwin_ref, dwuq_ref, dwuk_ref, dgq_ref, dgkv_ref, dsc1_ref, dsh1_ref, dgmix_ref, dwin_acc,
             dwuq_acc, dwuk_acc, wuk_bd):
        i = pl.program_id(0)

        @pl.when(i == 0)
        def _():
            dwin_acc[...] = jnp.zeros_like(dwin_acc)
            dwuq_acc[...] = jnp.zeros_like(dwuq_acc)
            dwuk_acc[...] = jnp.zeros_like(dwuk_acc)
            _fill_block_diagonal(wuk_bd, wuk_ref)
            for r in (dgq_ref, dgkv_ref, dsc1_ref, dsh1_ref, dgmix_ref):
                r[...] = jnp.zeros_like(r)

        dq_blocks = [dqt_ref[a].astype(F32).T for a in range(nsub)]
        dq_heads = [jnp.concatenate([blk[hd * TQ:(hd + 1) * TQ, :] for blk in dq_blocks], axis=0)
                    for hd in range(HEADS)]
        dq_lat = jnp.concatenate([dqh[:, 0:KVL] for dqh in dq_heads], axis=1).astype(BF)
        dq_rope = jnp.concatenate([dqh[:, KVL:QW] for dqh in dq_heads], axis=1).astype(BF)
        dq_nope = _dot(dq_lat, wuk_bd[...])
        dwuk_acc[...] += _dot_tn(dq_lat, qn_ref[...])
        drope = _dot(dq_rope, permt_ref[...])
        do1 = drope[:, 0:128]
        do2 = drope[:, 128:256]
        cosv = cos_ref[...]
        sinv = sin_ref[...]
        dq = jnp.concatenate([dq_nope, do1 * cosv + do2 * sinv, do2 * cosv - do1 * sinv], axis=1).astype(BF)

        cq_raw = raw_ref[:, 0:QL]
        ckv_raw = raw_ref[:, QL:QL + KVL]
        rq = _rms(cq_raw)
        nq_ = cq_raw * rq
        gq = gq_ref[...]
        dwuq_acc[...] += _dot_tn((nq_ * gq).astype(BF), dq)
        dc_q = _dot_nt(dq, wuq_ref[...])
        dgq_ref[...] += _colsum(dc_q * nq_)
        dcq_raw = _rms_bwd(dc_q * gq, nq_, rq)

        dkv = dkv_ref[...]
        rk = _rms(ckv_raw)
        nk_ = ckv_raw * rk
        dc_kv = dkv[:, 0:KVL]
        dgkv_ref[...] += _colsum(dc_kv * nk_)
        dckv_raw = _rms_bwd(dc_kv * gkv_ref[...], nk_, rk)
        dkr_roped = dkv[:, KVL:QW]
        dkr = dkr_roped * csk_ref[...] - _swap_halves(dkr_roped) * snk_ref[...]

        dproj = jnp.concatenate([dcq_raw.astype(BF), dckv_raw.astype(BF), dkr.astype(BF), du_ref[...]], axis=1)
        dwin_acc[...] += _dot_tn(h1_ref[...], dproj)
        dh1 = _dot_nt(dproj, win_ref[...])

        sc1 = mod_ref[0:1, D:2 * D]
        gmix = gmix_ref[...]
        xv = x_ref[...]
        r1 = _rms(xv)
        xn1 = xv * r1
        along = _colsum(dh1 * xn1)
        dsc1_ref[...] += along * gmix
        dsh1_ref[...] += _colsum(dh1)
        dgmix_ref[...] += along * (1.0 + sc1)
        dx_ref[...] = dx2_ref[...] + _rms_bwd(dh1 * (gmix * (1.0 + sc1)), xn1, r1)

        @pl.when(i == n - 1)
        def _():
            dwin_ref[...] = dwin_acc[...].astype(BF)
            dwuq_ref[...] = dwuq_acc[...].astype(BF)
            for hd in range(HEADS):
                dwuk_ref[hd] = dwuk_acc[hd * KVL:(hd + 1) * KVL, hd * NOPE:(hd + 1) * NOPE]

    out_shape = (
        jax.ShapeDtypeStruct((S, D), F32),
        jax.ShapeDtypeStruct((D, D), BF),
        jax.ShapeDtypeStruct((QL, 768), BF),
        jax.ShapeDtypeStruct((HEADS, KVL, NOPE), F32),
        jax.ShapeDtypeStruct((1, QL), F32), jax.ShapeDtypeStruct((1, KVL), F32),
        jax.ShapeDtypeStruct((1, D), F32), jax.ShapeDtypeStruct((1, D), F32), jax.ShapeDtypeStruct((1, D), F32),
    )
    in_specs = [pl.BlockSpec((nsub, QW, M), lambda i: (i, 0, 0)), _rows(ts, QW), _rows(ts, PW), _rows(ts, 384),
                _rows(ts, HEADS * NOPE), _rows(ts, D), _rows(ts, D), _rows(ts, D), _full(mod.shape), _full((1, D)),
                _full(w_in.shape), _full((1, QL)), _full((1, KVL)), _full(w_uq.shape), _full(wuk_cd.shape),
                _full(perm_t.shape), _rows(ts, 128), _rows(ts, 128), _rows(ts, 128), _rows(ts, 128)]
    out_specs = (_rows(ts, D), _full((D, D)), _full((QL, 768)), _full((HEADS, KVL, NOPE)), _full((1, QL)),
                 _full((1, KVL)), _full((1, D)), _full((1, D)), _full((1, D)))
    return pl.pallas_call(
        body, name="in_bwd", out_shape=out_shape, grid=(n,), in_specs=in_specs, out_specs=out_specs,
        scratch_shapes=[pltpu.VMEM((D, D), F32), pltpu.VMEM((QL, 768), F32),
                        pltpu.VMEM((HEADS * KVL, HEADS * NOPE), F32), pltpu.VMEM((HEADS * KVL, HEADS * NOPE), BF)],
        compiler_params=_params(("arbitrary",)),
    )(dqt, dkv, du, raw, qn, h1, x, dx2, mod, g_mix, w_in, g_q, g_kv, w_uq, wuk_cd, perm_t, cos4, sin4, csk, snk)


def _rope_perm():
    p = np.zeros((HEADS, 2 * 128, 128), np.float32)
    for hd in range(HEADS):
        for t in range(HALF):
            p[hd, hd * HALF + t, t] = 1.0
            p[hd, 128 + hd * HALF + t, HALF + t] = 1.0
    return p


def _rope_tables(positions):
    freqs = jnp.power(ROPE_THETA, -jnp.arange(HALF, dtype=F32) / HALF)
    ang = positions.astype(F32)[:, None] * jnp.tile(freqs, HEADS)[None, :]
    cos4 = jnp.cos(ang)
    sin4 = jnp.sin(ang)
    lane = jnp.arange(HEADS * HALF)[None, :]
    csk = jnp.where(lane < ROPE, cos4, 0.0)
    snk = jnp.where(lane < HALF, -sin4, jnp.where(lane < ROPE, sin4, 0.0))
    return cos4, sin4, csk, snk


def _local_step(x, rope, target, mod, g_mix, w_in_p, g_q, g_kv, w_uq_p, w_uk, w_uv, w_pool, pool_scale, g_ffn,
                g_final, late, ffn_grads_exchange):
    perm = jnp.asarray(_rope_perm().transpose(1, 0, 2).reshape(2 * 128, HEADS * 128), BF)
    perm_t = jnp.asarray(_rope_perm().transpose(0, 2, 1).reshape(HEADS * 128, 2 * 128), BF)
    cos4, sin4, csk, snk = rope
    wuk_dc = w_uk.transpose(1, 2, 0).astype(BF)
    wuk_cd = w_uk.transpose(1, 0, 2).astype(BF)
    wuv_vc = w_uv.transpose(1, 2, 0).astype(BF)
    wpool = w_pool.astype(BF)
    wpool_dc = w_pool.transpose(0, 2, 1).astype(BF)

    h1, raw, qn, qs, kv, kvt, pooled, ypre, ypool = _fwd_in(
        x, mod, g_mix, w_in_p, g_q, g_kv, w_uq_p, wuk_dc, perm, cos4, sin4, csk, snk, wpool, pool_scale)
    olat, ymla, lse = _attn_fwd(qs, kv, kvt, wuv_vc)
    w_o, wg_t, wu_t, wd = late
    x2, mix, h2_t, a, b, dx3, dff, dff_t, loss, dgfin, dgt2 = _ffn_fwd(
        x, ymla, ypool, mod, w_o, g_ffn, wg_t, wu_t, wd, g_final, target)
    da, db, dh2 = _ffn_bwd_acts(dff, a, b, wg_t, wu_t, wd)
    (dx2, du, dolat, delta, dwo, dwuv, dwpool, dpscale, dgt1, dsc2, dsh2, dgffn) = _mix_bwd(
        dh2, dx3, x2, mix, mod, g_ffn, ymla, ypool, w_o, ypre, pooled, pool_scale, wpool_dc, olat, wuv_vc)
    dwg_t, dwu_t, dwd = _ffn_bwd_weights(dff_t, h2_t, da, db, a, b)
    ffn_parts = ffn_grads_exchange((dwg_t, dwu_t, dwd, dwo))
    dkv, dqt = _attn_bwd(qs, kv, dolat, lse, delta)
    dx, dwin, dwuq, dwuk, dgq, dgkv, dsc1, dsh1, dgmix = _in_bwd(
        dqt, dkv, du, raw, qn, h1, x, dx2, mod, g_mix, w_in_p, g_q, g_kv, w_uq_p, wuk_cd, perm_t, cos4, sin4, csk,
        snk)
    dmod = jnp.concatenate([dsh1, dsc1, dgt1, dsh2, dsc2, dgt2], axis=1)
    replicated = dict(
        w_uk=dwuk.transpose(1, 0, 2), w_uv=dwuv.transpose(1, 0, 2), w_pool=dwpool, g_mix=dgmix, g_q=dgq, g_kv=dgkv,
        pool_scale=dpscale, g_ffn=dgffn, g_final=dgfin)
    return loss[0, 0], dx, dmod, (dwin, dwuq), ffn_parts, replicated


def _my_pos():
    return lax.axis_index("x"), lax.axis_index("y"), lax.axis_index("c")


def _peer(pos, k):
    x, y, c = pos
    return (1 - x if k & 4 else x, 1 - y if k & 2 else y, 1 - c if k & 1 else c)


def _index(pos):
    x, y, c = pos
    return 4 * x + 2 * y + c


def _remote(src, dst, send_sem, recv_sem, to):
    return pltpu.make_async_remote_copy(src_ref=src, dst_ref=dst, send_sem=send_sem, recv_sem=recv_sem,
                                        device_id=to, device_id_type=MESH)


def _ada_mod(c, w_ada, b_ada, after):
    def body(c_ref, w_ref, b_ref, after_ref, mod_ref, call_ref, cbuf, sbuf, rbuf, send1, recv1, send2, recv2):
        me = _my_pos()
        mi = _index(me)
        cv = c_ref[...]
        cbuf[...] = jnp.broadcast_to(cv * jax.nn.sigmoid(cv), (8, D))
        call_ref[mi] = cbuf[...]
        first = [_remote(cbuf, call_ref.at[mi], send1.at[k - 1], recv1.at[k - 1], _peer(me, k)) for k in range(1, NDEV)]
        for cp in first:
            cp.start()
        for k in range(1, NDEV):
            _remote(cbuf, call_ref.at[_index(_peer(me, k))], send1.at[k - 1], recv1.at[k - 1], _peer(me, k)).wait_recv()
        c_all = jnp.concatenate([call_ref[b][0:1, :] for b in range(NDEV)], axis=0)
        blocks = _dot(c_all.astype(BF), w_ref[...].astype(BF))
        for b in range(NDEV):
            sbuf[b] = jnp.broadcast_to(blocks[b:b + 1, :], (8, MODC))
        second = []
        for k in range(1, NDEV):
            to = _peer(me, k)
            second.append(_remote(sbuf.at[_index(to)], rbuf.at[mi], send2.at[k - 1], recv2.at[k - 1], to))
        for cp in second:
            cp.start()
        rbuf[mi] = sbuf[mi]
        for k in range(1, NDEV):
            to = _peer(me, k)
            _remote(sbuf.at[_index(to)], rbuf.at[_index(to)], send2.at[k - 1], recv2.at[k - 1], to).wait_recv()
        for j in range(NDEV):
            mod_ref[:, j * MODC:(j + 1) * MODC] = rbuf[j] + b_ref[:, j * MODC:(j + 1) * MODC]
        for cp in first + second:
            cp.wait_send()

    return pl.pallas_call(
        body, name="ada_mod",
        out_shape=(jax.ShapeDtypeStruct((8, N_MOD * D), F32), jax.ShapeDtypeStruct((NDEV, 8, D), F32)),
        in_specs=[_vmem(), _vmem(), _vmem(), _any()], out_specs=(_vmem(), _vmem()),
        scratch_shapes=[pltpu.VMEM((8, D), F32), pltpu.VMEM((NDEV, 8, MODC), F32), pltpu.VMEM((NDEV, 8, MODC), F32),
                        pltpu.SemaphoreType.DMA((NDEV - 1,)), pltpu.SemaphoreType.DMA((NDEV - 1,)),
                        pltpu.SemaphoreType.DMA((NDEV - 1,)), pltpu.SemaphoreType.DMA((NDEV - 1,))],
        compiler_params=_params(),
    )(c, w_ada, b_ada, after)


def _sequencer_scatter(name, collective_id, srcs, after=()):
    n = len(srcs)

    def of(src, to_index):
        r = src.shape[0] // NDEV
        return src.at[pl.ds(pl.multiple_of(to_index * r, 16), r), :]

    def body(*refs):
        src, zone = refs[:n], refs[n + len(after):2 * n + len(after)]
        send, recv, local = refs[2 * n + len(after):]
        me = _my_pos()
        mi = _index(me)
        barrier = pltpu.get_barrier_semaphore()
        for k in range(1, NDEV):
            pl.semaphore_signal(barrier, inc=1, device_id=_peer(me, k), device_id_type=MESH)
        pl.semaphore_wait(barrier, NDEV - 1)
        own = [pltpu.make_async_copy(of(src[a], mi), zone[a].at[mi], local.at[a]) for a in range(n)]
        for cp in own:
            cp.start()
        for a in range(n):
            for k in range(1, NDEV):
                to = _peer(me, k)
                s = a * (NDEV - 1) + k - 1
                _remote(of(src[a], _index(to)), zone[a].at[mi], send.at[s], recv.at[s], to).start()
        for cp in own:
            cp.wait()
        for a in range(n):
            for k in range(1, NDEV):
                to = _peer(me, k)
                s = a * (NDEV - 1) + k - 1
                cp = _remote(of(src[a], mi), zone[a].at[_index(to)], send.at[s], recv.at[s], to)
                cp.wait_send()
                cp.wait_recv()

    return pl.kernel(
        body, name=name, mesh=plsc.ScalarSubcoreMesh(axis_name="sequencer", num_cores=1),
        out_type=tuple(jax.ShapeDtypeStruct((NDEV, s.shape[0] // NDEV, s.shape[1]), s.dtype) for s in srcs),
        scratch_types=[pltpu.SemaphoreType.DMA((n * (NDEV - 1),)), pltpu.SemaphoreType.DMA((n * (NDEV - 1),)),
                       pltpu.SemaphoreType.DMA((n,))],
        compiler_params=pltpu.CompilerParams(collective_id=collective_id),
    )(*srcs, *after)


CHIP_PEERS = (2, 4, 6)


def _sequencer_gather(name, collective_id, srcs, after=()):
    n = len(srcs)
    per = NDEV - 1

    def body(*refs):
        src, zone = refs[:n], refs[n + len(after):2 * n + len(after)]
        send, recv, local = refs[2 * n + len(after):]
        me = _my_pos()
        mi = _index(me)
        sibling = _peer(me, 1)
        talk_to = (sibling,) + tuple(_peer(me, k) for k in CHIP_PEERS)
        barrier = pltpu.get_barrier_semaphore()
        for to in talk_to:
            pl.semaphore_signal(barrier, inc=1, device_id=to, device_id_type=MESH)
        pl.semaphore_wait(barrier, len(talk_to))

        def copy(a, slot, block_of, to, from_src=False):
            rows = zone[a].at[_index(block_of)]
            return _remote(src[a] if from_src else rows, rows, send.at[a * per + slot], recv.at[a * per + slot], to)

        own = [pltpu.make_async_copy(src[a], zone[a].at[mi], local.at[a]) for a in range(n)]
        for cp in own:
            cp.start()
        started = []
        for a in range(n):
            started.append(copy(a, 0, me, sibling, from_src=True))
            started += [copy(a, 1 + j, me, _peer(me, k), from_src=True) for j, k in enumerate(CHIP_PEERS)]
        for cp in started:
            cp.start()
        for a in range(n):
            for j, k in enumerate(CHIP_PEERS):
                copy(a, 1 + j, _peer(me, k), me).wait_recv()
                passed = copy(a, 4 + j, _peer(me, k), sibling)
                passed.start()
                started.append(passed)
        for a in range(n):
            copy(a, 0, sibling, me).wait_recv()
            for j, k in enumerate(CHIP_PEERS):
                copy(a, 4 + j, _peer(me, k | 1), me).wait_recv()
        for cp in started:
            cp.wait_send()
        for cp in own:
            cp.wait()

    return pl.kernel(
        body, name=name, mesh=plsc.ScalarSubcoreMesh(axis_name="sequencer", num_cores=1),
        out_type=tuple(jax.ShapeDtypeStruct((NDEV,) + s.shape, s.dtype) for s in srcs),
        scratch_types=[pltpu.SemaphoreType.DMA((n * per,)), pltpu.SemaphoreType.DMA((n * per,)),
                       pltpu.SemaphoreType.DMA((n,))],
        compiler_params=pltpu.CompilerParams(collective_id=collective_id),
    )(*srcs, *after)


def _blocked(shape, nb, axis=0):
    block = tuple(s // nb if d == axis else s for d, s in enumerate(shape))
    return pl.BlockSpec(block, lambda i: tuple(i if d == axis else 0 for d in range(len(shape))))


def _sum_partials(name, parts, nb):
    n = len(parts)

    def body(*refs):
        for a in range(n):
            acc = refs[a][0].astype(F32)
            for p in range(1, NDEV):
                acc = acc + refs[a][p].astype(F32)
            refs[n + a][...] = acc

    return pl.pallas_call(
        body, name=name, grid=(nb,),
        out_shape=tuple(jax.ShapeDtypeStruct(p.shape[1:], F32) for p in parts),
        in_specs=[_blocked(p.shape, nb, 1) for p in parts],
        out_specs=tuple(_blocked(p.shape[1:], nb) for p in parts), compiler_params=_params(("arbitrary",)),
    )(*parts)


def _small_all_reduce(buf):
    def body(buf_ref, got_ref, red_ref, mine, send1, recv1, send2, recv2):
        me = _my_pos()
        mi = _index(me)
        first = []
        for k in range(1, NDEV):
            to = _peer(me, k)
            first.append(_remote(buf_ref.at[_index(to)], got_ref.at[mi], send1.at[k - 1], recv1.at[k - 1], to))
        for cp in first:
            cp.start()
        got_ref[mi] = buf_ref[mi]
        for k in range(1, NDEV):
            to = _peer(me, k)
            _remote(buf_ref.at[mi], got_ref.at[_index(to)], send1.at[k - 1], recv1.at[k - 1], to).wait_recv()
        acc = got_ref[0]
        for p in range(1, NDEV):
            acc = acc + got_ref[p]
        mine[...] = acc
        second = [_remote(mine, red_ref.at[mi], send2.at[k - 1], recv2.at[k - 1], _peer(me, k)) for k in range(1, NDEV)]
        for cp in second:
            cp.start()
        red_ref[mi] = acc
        for k in range(1, NDEV):
            to = _peer(me, k)
            _remote(mine, red_ref.at[_index(to)], send2.at[k - 1], recv2.at[k - 1], to).wait_recv()
        for cp in first + second:
            cp.wait_send()

    return pl.pallas_call(
        body, name="small_all_reduce",
        out_shape=(jax.ShapeDtypeStruct(buf.shape, F32), jax.ShapeDtypeStruct(buf.shape, F32)),
        in_specs=[_vmem()], out_specs=(_vmem(), _vmem()),
        scratch_shapes=[pltpu.VMEM(buf.shape[1:], F32),
                        pltpu.SemaphoreType.DMA((NDEV - 1,)), pltpu.SemaphoreType.DMA((NDEV - 1,)),
                        pltpu.SemaphoreType.DMA((NDEV - 1,)), pltpu.SemaphoreType.DMA((NDEV - 1,))],
        compiler_params=_params(),
    )(buf)


def _adamw_math(w, g, m, v):
    m = ADAM_B1 * m + (1.0 - ADAM_B1) * g
    v = ADAM_B2 * v + (1.0 - ADAM_B2) * jnp.square(g)
    m_hat = m / (1.0 - ADAM_B1 ** ADAM_STEP)
    v_hat = v / (1.0 - ADAM_B2 ** ADAM_STEP)
    delta = -ADAM_LR * (m_hat / (jnp.sqrt(v_hat) + ADAM_EPS) + ADAM_WD * w)
    return delta, m, v


def _adamw_group(name, ws, gs, ms, vs, nb):
    n = len(ws)

    def body(*refs):
        for a in range(n):
            w, g, m, v = (refs[q * n + a][...] for q in range(4))
            delta, m2, v2 = _adamw_math(w, g, m, v)
            refs[4 * n + a][...] = delta
            refs[5 * n + a][...] = m2
            refs[6 * n + a][...] = v2

    shapes = tuple(jax.ShapeDtypeStruct(w.shape, F32) for w in ws)
    specs = [_blocked(w.shape, nb) for w in ws]
    outs = pl.pallas_call(
        body, name=name, grid=(nb,), out_shape=shapes * 3, in_specs=specs * 4, out_specs=tuple(specs * 3),
        compiler_params=_params(("arbitrary",)),
    )(*ws, *gs, *ms, *vs)
    return outs[:n], outs[n:2 * n], outs[2 * n:]


def _adamw_from_partials(name, ws, parts, ms, vs, nb):
    n = len(ws)

    def body(*refs):
        for a in range(n):
            part = refs[n + a]
            g = part[0].astype(F32)
            for p in range(1, NDEV):
                g = g + part[p].astype(F32)
            delta, m2, v2 = _adamw_math(refs[a][...], g, refs[2 * n + a][...], refs[3 * n + a][...])
            refs[4 * n + a][...] = g
            refs[5 * n + a][...] = delta
            refs[6 * n + a][...] = m2
            refs[7 * n + a][...] = v2

    shapes = tuple(jax.ShapeDtypeStruct(w.shape, F32) for w in ws)
    specs = [_blocked(w.shape, nb) for w in ws]
    outs = pl.pallas_call(
        body, name=name, grid=(nb,), out_shape=shapes * 4,
        in_specs=specs + [_blocked(p.shape, nb, 1) for p in parts] + specs * 2, out_specs=tuple(specs * 4),
        compiler_params=_params(("arbitrary",)),
    )(*ws, *parts, *ms, *vs)
    return outs[:n], outs[n:2 * n], outs[2 * n:3 * n], outs[3 * n:]


def _adamw_ada(w, m, v, c_all_t, dmod_rows):
    nb = 4

    def body(w_ref, m_ref, v_ref, c_ref, dm_ref, g_ref, d_ref, m2_ref, v2_ref):
        g = _dot(c_ref[...], dm_ref[...].astype(BF))
        g_ref[...] = g
        delta, m2, v2 = _adamw_math(w_ref[...], g, m_ref[...], v_ref[...])
        d_ref[...] = delta
        m2_ref[...] = m2
        v2_ref[...] = v2

    shp = jax.ShapeDtypeStruct(w.shape, F32)
    spec = _blocked(w.shape, nb)
    return pl.pallas_call(
        body, name="adamw_ada", grid=(nb,), out_shape=(shp, shp, shp, shp),
        in_specs=[spec, spec, spec, _blocked(c_all_t.shape, nb), _full(dmod_rows.shape)],
        out_specs=(spec, spec, spec, spec), compiler_params=_params(("arbitrary",)),
    )(w, m, v, c_all_t, dmod_rows)


def _w_in_to_kernel(w):
    return jnp.concatenate([w[:, 0:448], jnp.zeros((w.shape[0], 64), w.dtype), w[:, 448:960]], axis=1)


def _w_in_from_kernel(w):
    return jnp.concatenate([w[:, 0:448], w[:, 512:1024]], axis=1)


def _w_uq_to_kernel(w):
    r = w.shape[0]
    return jnp.concatenate([w[:, :, 0:NOPE].reshape(r, HEADS * NOPE),
                            w[:, :, NOPE:NOPE + HALF].reshape(r, HEADS * HALF),
                            w[:, :, NOPE + HALF:].reshape(r, HEADS * HALF)], axis=1)


def _w_uq_from_kernel(w):
    r = w.shape[0]
    return jnp.concatenate([w[:, 0:512].reshape(r, HEADS, NOPE), w[:, 512:640].reshape(r, HEADS, HALF),
                            w[:, 640:768].reshape(r, HEADS, HALF)], axis=2)


REP_NAMES = ("w_uk", "w_uv", "w_pool", "g_mix", "g_q", "g_kv", "pool_scale", "g_ffn", "g_final")


def kernel(x, c, positions, w_ada, b_ada, g_mix, w_in, g_q, g_kv, w_uq, w_uk, w_uv, w_pool, pool_scale, w_o, g_ffn, w_gate, w_up, w_down, g_final, loss_target, m_w_ada, m_b_ada, m_g_mix, m_w_in, m_g_q, m_g_kv, m_w_uq, m_w_uk, m_w_uv, m_w_pool, m_pool_scale, m_w_o, m_g_ffn, m_w_gate, m_w_up, m_w_down, m_g_final, v_w_ada, v_b_ada, v_g_mix, v_w_in, v_g_q, v_g_kv, v_w_uq, v_w_uk, v_w_uv, v_w_pool, v_pool_scale, v_w_o, v_g_ffn, v_w_gate, v_w_up, v_w_down, v_g_final):
    given = dict(locals())

    merge = lambda g: g.reshape(NDEV * g.shape[1], g.shape[2])
    w_in_p, w_uq_p = (merge(g) for g in _sequencer_gather(
        "gather_in", 3, (_w_in_to_kernel(w_in[0]).astype(BF), _w_uq_to_kernel(w_uq[0]).astype(BF))))

    rope = _rope_tables(positions[0])
    mod, c_all8 = _ada_mod(c, w_ada[0], b_ada, rope[3][0:8, :])
    c_all = c_all8[:, 0, :]
    late = _sequencer_gather(
        "gather_late", 1, (w_o[0].astype(BF), w_gate[0].T.astype(BF), w_up[0].T.astype(BF), w_down[0].astype(BF)),
        after=(mod[:, 0:128], w_in_p[0:16, 0:128], w_uq_p[0:16, 0:128]))

    def ffn_grads_exchange(arrays):
        return _sequencer_scatter("scatter_ffn", 2, arrays)

    loss, dx, dmod, tail_grads, ffn_parts, replicated = _local_step(
        x[0], rope, loss_target[0], mod, g_mix, w_in_p, g_q, g_kv, w_uq_p, w_uk[0], w_uv[0], w_pool[0],
        pool_scale, g_ffn, g_final.reshape(1, D), tuple(merge(g) for g in late), ffn_grads_exchange)

    flat = jnp.concatenate([replicated[k].reshape(-1) for k in REP_NAMES] + [loss.reshape(1)])
    flat = jnp.pad(flat, (0, NDEV * REP_ROWS * 128 - flat.shape[0])).reshape(NDEV, REP_ROWS, 128)
    dmod_blocks = jnp.pad(dmod.reshape(NDEV, MODC // 128, 128), ((0, 0), (0, MOD_ROWS - MODC // 128), (0, 0)))
    got, red = _small_all_reduce(jnp.concatenate([dmod_blocks, flat], axis=1))

    tail_parts = _sequencer_scatter("scatter_tail", 4, tail_grads,
                                    after=(ffn_parts[0][0, 0:16, 0:128], red[0, 0:8, :]))
    g_in_p, g_uq_p = _sum_partials("sum_tail_partials", tail_parts, 1)
    as_transpose = ("w_in", "w_gate", "w_up")
    grads = dict(w_in=_w_in_from_kernel(g_in_p).T, w_uq=_w_uq_from_kernel(g_uq_p))
    partials = dict(w_gate=ffn_parts[0], w_up=ffn_parts[1], w_down=ffn_parts[2], w_o=ffn_parts[3])
    dmod_rows = got[:, 0:MODC // 128, :].reshape(NDEV, MODC)
    grads["b_ada"] = red[:, 0:MODC // 128, :].reshape(1, N_MOD * D)
    rep_flat = red[:, MOD_ROWS:, :].reshape(-1)
    off = 0
    for k in REP_NAMES:
        size = int(np.prod(given[k].shape))
        grads[k] = rep_flat[off:off + size]
        off += size

    view = {k: (given[k].shape[1:] if given[k].ndim > 2 else given[k].shape)
            for k in REP_NAMES + ("b_ada", "w_ada", "w_in", "w_uq", "w_o", "w_gate", "w_up", "w_down")}
    view.update(g_final=(1, D))
    names = ["w_ada", "b_ada", "g_mix", "w_in", "g_q", "g_kv", "w_uq", "w_uk", "w_uv", "w_pool", "pool_scale",
             "w_o", "g_ffn", "w_gate", "w_up", "w_down", "g_final"]
    g_ada, d_ada, m_ada, v_ada = _adamw_ada(w_ada[0], m_w_ada[0], v_w_ada[0], c_all.T.astype(BF), dmod_rows)
    out_g, out_d, out_m, out_v = dict(w_ada=g_ada), dict(w_ada=d_ada), dict(w_ada=m_ada), dict(w_ada=v_ada)
    groups = (("adamw_ffn", ("w_gate", "w_up", "w_down", "w_o"), 4),
              ("adamw_replicated", REP_NAMES + ("b_ada",), 1),
              ("adamw_tail", ("w_in", "w_uq"), 1))
    for gname, members, nb in groups:
        turn = lambda k, t: t.T if k in as_transpose else t
        ws = [turn(k, given[k].reshape(view[k])) for k in members]
        ms = [turn(k, given["m_" + k].reshape(view[k])) for k in members]
        vs = [turn(k, given["v_" + k].reshape(view[k])) for k in members]
        if members[0] in partials:
            gs, ds, m2, v2 = _adamw_from_partials(gname, ws, [partials[k] for k in members], ms, vs, nb)
        else:
            gs = [grads[k] if k in as_transpose else grads[k].reshape(view[k]) for k in members]
            ds, m2, v2 = _adamw_group(gname, ws, gs, ms, vs, nb)
        for k, g, d, mm, vv in zip(members, gs, ds, m2, v2):
            out_g[k], out_d[k], out_m[k], out_v[k] = turn(k, g), turn(k, d), turn(k, mm), turn(k, vv)

    total = rep_flat[off]
    shaped = lambda d: [d[k].reshape(given[k].shape) for k in names]
    return (total, dx[None], *shaped(out_g), *shaped(out_d), *shaped(out_m), *shaped(out_v))
```

```python
import numpy as np
import jax
import jax.numpy as jnp
from jax import lax
from jax.experimental import pallas as pl
from jax.experimental.pallas import tpu as pltpu
from jax.experimental.pallas import tpu_sc as plsc

D = 1024
HEADS = 4
NOPE = 128
ROPE = 64
HALF = ROPE // 2
QL = 256
KVL = 128
FF = 2816
PW = 512
GROUPS = 4
GD = 128
N_MOD = 6
EPS = 1e-6
SM_SCALE = (NOPE + ROPE) ** -0.5
LOG2_E = 1.4426950408889634
EXP2_SCALE = SM_SCALE * LOG2_E
ROPE_THETA = 10000.0
NDEV = 8
MODC = N_MOD * D // NDEV

ADAM_LR = 0.001
ADAM_B1 = 0.9
ADAM_B2 = 0.999
ADAM_EPS = 1e-08
ADAM_WD = 0.01
ADAM_STEP = 10

BF = jnp.bfloat16
F32 = jnp.float32
VMEM_LIMIT_V7X = 60 * 1024 * 1024
MESH = pl.DeviceIdType.MESH

TQ = 512
TK = 512
QW = 256
VPU_ROWS = 16
MOD_ROWS = 8
REP_ROWS = 200
SMALL_ROWS = MOD_ROWS + REP_ROWS


def _params(sem=None):
    return pltpu.CompilerParams(dimension_semantics=sem, vmem_limit_bytes=VMEM_LIMIT_V7X)


def _dot(a, b):
    return jnp.dot(a, b, preferred_element_type=F32)


def _dot_nt(a, b):
    return lax.dot_general(a, b, (((1,), (1,)), ((), ())), preferred_element_type=F32)


def _dot_tn(a, b):
    return _dot(a.astype(F32).T.astype(BF), b)


def _full(shape):
    return pl.BlockSpec(shape, lambda *_: (0,) * len(shape))


def _rows(ts, cols):
    return pl.BlockSpec((ts, cols), lambda i: (i, 0))


def _vmem():
    return pl.BlockSpec(memory_space=pltpu.VMEM)


def _any():
    return pl.BlockSpec(memory_space=pl.ANY)


def _rms(v):
    return lax.rsqrt(jnp.mean(v * v, axis=-1, keepdims=True) + EPS)


def _rms_bwd(dn, n, r):
    return r * (dn - n * jnp.mean(dn * n, axis=-1, keepdims=True))


def _colsum(v):
    return jnp.sum(v, axis=0, keepdims=True)


def _swap_halves(v):
    lane = lax.broadcasted_iota(jnp.int32, v.shape, 1)
    return jnp.where(lane < HALF, pltpu.roll(v, 128 - HALF, 1), pltpu.roll(v, HALF, 1))


def _window_lane_width():
    lane = lax.broadcasted_iota(jnp.int32, (1, PW), 1)
    return jnp.where(lane < 128, 2.0, jnp.where(lane < 256, 4.0, jnp.where(lane < 384, 8.0, 16.0))).astype(F32)


def _window_sums(ext, back):
    n = ext.shape[0]

    def sh(v, k):
        return pltpu.roll(v, k if back else n - k, 0)

    s2 = ext + sh(ext, 1)
    e4 = s2[:, 128:]
    s4 = e4 + sh(e4, 2)
    e8 = s4[:, 128:]
    s8 = e8 + sh(e8, 4)
    e16 = s8[:, 128:]
    s16 = e16 + sh(e16, 8)
    return jnp.concatenate([s2[:, :128], s4[:, :128], s8[:, :128], s16], axis=1)


def _fill_block_diagonal(dst_ref, blocks_ref):
    n, r, c = blocks_ref.shape
    dst_ref[...] = jnp.zeros_like(dst_ref)
    for b in range(n):
        dst_ref[b * r:(b + 1) * r, b * c:(b + 1) * c] = blocks_ref[b]


def _row_counts(first_row, ts):
    t1 = (first_row + lax.broadcasted_iota(jnp.int32, (ts, 1), 0) + 1).astype(F32)
    return jnp.minimum(t1, _window_lane_width())


def _fwd_in(x, mod, g_mix, w_in, g_q, g_kv, w_uq, wuk_dc, perm, cos4, sin4, csk, snk, w_pool, pool_scale):
    S = x.shape[0]
    ts = 1024
    nsub = ts // TQ

    def body(x_ref, mod_ref, gmix_ref, win_ref, gq_ref, gkv_ref, wuq_ref, wuk_ref, perm_ref, cos_ref, sin_ref,
             csk_ref, snk_ref, wpool_ref, pscale_ref,
             h1_ref, raw_ref, qn_ref, qs_ref, kv_ref, kvt_ref, pooled_ref, ypre_ref, ypool_ref, carry_ref, wuk_bd,
             wpool_bd):
        i = pl.program_id(0)

        @pl.when(i == 0)
        def _():
            carry_ref[...] = jnp.zeros_like(carry_ref)
            _fill_block_diagonal(wuk_bd, wuk_ref)
            _fill_block_diagonal(wpool_bd, wpool_ref)

        xv = x_ref[...]
        sh1 = mod_ref[0:1, 0:D]
        sc1 = mod_ref[0:1, D:2 * D]
        h = (xv * _rms(xv)) * gmix_ref[...] * (1.0 + sc1) + sh1
        hb = h.astype(BF)
        h1_ref[...] = hb
        proj = _dot(hb, win_ref[...])
        cq_raw = proj[:, 0:QL]
        ckv_raw = proj[:, QL:QL + KVL]
        kr = proj[:, 384:512]
        u = proj[:, 512:1024]
        raw_ref[...] = proj[:, 0:384]

        c_q = (cq_raw * _rms(cq_raw)) * gq_ref[...]
        c_kv = (ckv_raw * _rms(ckv_raw)) * gkv_ref[...]
        q = _dot(c_q.astype(BF), wuq_ref[...])
        qn = q[:, 0:HEADS * NOPE].astype(BF)
        qn_ref[...] = qn
        x1 = q[:, 512:640]
        x2 = q[:, 640:768]
        cosv = cos_ref[...]
        sinv = sin_ref[...]
        roped = jnp.concatenate([x1 * cosv - x2 * sinv, x1 * sinv + x2 * cosv], axis=1).astype(BF)
        q_lat = _dot(qn, wuk_bd[...])
        q_rope = _dot(roped, perm_ref[...])
        for hd in range(HEADS):
            cols = slice(hd * 128, (hd + 1) * 128)
            qh = jnp.concatenate([q_lat[:, cols], q_rope[:, cols]], axis=1).astype(BF)
            for a in range(nsub):
                qs_ref[a, hd * TQ:(hd + 1) * TQ, :] = qh[a * TQ:(a + 1) * TQ, :]
        k_rope = kr * csk_ref[...] + _swap_halves(kr) * snk_ref[...]
        keys = jnp.concatenate([c_kv, k_rope], axis=1)
        kv_ref[...] = keys.astype(BF)
        for a in range(ts // TK):
            kvt_ref[a] = keys[a * TK:(a + 1) * TK, :].T.astype(BF)

        ext = jnp.concatenate([carry_ref[...], u], axis=0)
        win = _window_sums(ext, True)[16:, :]
        pooled = (win / _row_counts(i * ts, ts) - u).astype(BF)
        pooled_ref[...] = pooled
        carry_ref[...] = u[ts - 16:ts, :]
        ypre = _dot(pooled, wpool_bd[...])
        ypre_ref[...] = ypre.astype(BF)
        ypool_ref[...] = (ypre * pscale_ref[...]).astype(BF)

    out_shape = (
        jax.ShapeDtypeStruct((S, D), BF),
        jax.ShapeDtypeStruct((S, 384), F32),
        jax.ShapeDtypeStruct((S, HEADS * NOPE), BF),
        jax.ShapeDtypeStruct((S // TQ, HEADS * TQ, QW), BF),
        jax.ShapeDtypeStruct((S, QW), BF),
        jax.ShapeDtypeStruct((S // TK, QW, TK), BF),
        jax.ShapeDtypeStruct((S, PW), BF),
        jax.ShapeDtypeStruct((S, PW), BF),
        jax.ShapeDtypeStruct((S, PW), BF),
    )
    in_specs = [
        _rows(ts, D), _full(mod.shape), _full((1, D)), _full(w_in.shape), _full((1, QL)), _full((1, KVL)),
        _full(w_uq.shape), _full(wuk_dc.shape), _full(perm.shape), _rows(ts, 128), _rows(ts, 128), _rows(ts, 128),
        _rows(ts, 128), _full(w_pool.shape), _full((1, PW)),
    ]
    out_specs = (
        _rows(ts, D), _rows(ts, 384), _rows(ts, HEADS * NOPE),
        pl.BlockSpec((nsub, HEADS * TQ, QW), lambda i: (i, 0, 0)),
        _rows(ts, QW), pl.BlockSpec((ts // TK, QW, TK), lambda i: (i, 0, 0)), _rows(ts, PW), _rows(ts, PW),
        _rows(ts, PW),
    )
    return pl.pallas_call(
        body, name="fwd_in", out_shape=out_shape, grid=(S // ts,), in_specs=in_specs, out_specs=out_specs,
        scratch_shapes=[pltpu.VMEM((16, PW), F32), pltpu.VMEM((HEADS * NOPE, HEADS * KVL), BF),
                        pltpu.VMEM((PW, PW), BF)],
        compiler_params=_params(("arbitrary",)),
    )(x, mod, g_mix, w_in, g_q, g_kv, w_uq, wuk_dc, perm, cos4, sin4, csk, snk, w_pool, pool_scale)


def _diag_mask(shape, q_axis, first_chunk):
    qi = (lax.broadcasted_iota(jnp.int32, shape, q_axis) & (TQ - 1)) >> 6
    ki = (lax.broadcasted_iota(jnp.int32, shape, 1 - q_axis) >> 6) + first_chunk
    return ki <= qi


def _attn_fwd(qs, kv, kvt, wuv_vc):
    nq = qs.shape[0]
    S = kv.shape[0]
    M = HEADS * TQ

    def body(qs_ref, kv_ref, kvt_ref, wuv_ref, olat_ref, ymla_ref, lse_ref):
        i = pl.program_id(0)
        q = qs_ref[0]

        def step(kt, carry, first_chunk=None):
            m, l, acc = carry
            k = kv_ref[pl.ds(pl.multiple_of(kt * TK, TK), TK), :]
            v_t = kvt_ref[kt][0:KVL, :]
            s = _dot_nt(k, q)
            if first_chunk is not None:
                s = jnp.where(_diag_mask((TK, M), 1, first_chunk), s, -jnp.inf)
            m_new = jnp.maximum(m, jnp.max(s, axis=0, keepdims=True))
            alpha = jnp.exp2((m - m_new) * EXP2_SCALE)
            p = jnp.exp2((s - m_new) * EXP2_SCALE)
            l = alpha * l + jnp.sum(p, axis=0, keepdims=True)
            acc = alpha * acc + _dot(v_t, p.astype(BF))
            return m_new, l, acc

        init = (jnp.full((1, M), -jnp.inf, F32), jnp.zeros((1, M), F32), jnp.zeros((KVL, M), F32))
        per = TQ // TK
        carry = lax.fori_loop(0, per * i, step, init)
        for j in range(per):
            carry = step(per * i + j, carry, j * (TK // 64))
        m, l, acc = carry
        o_lat = acc / l
        olat_ref[0] = o_lat
        lse_ref[0] = jnp.broadcast_to(m * SM_SCALE + jnp.log(l), (8, M))
        for hd in range(HEADS):
            o_t = _dot(wuv_ref[hd], o_lat[:, hd * TQ:(hd + 1) * TQ].astype(BF))
            ymla_ref[:, hd * 128:(hd + 1) * 128] = o_t.T.astype(BF)

    out_shape = (
        jax.ShapeDtypeStruct((nq, KVL, M), F32),
        jax.ShapeDtypeStruct((S, HEADS * 128), BF),
        jax.ShapeDtypeStruct((nq, 8, M), F32),
    )
    return pl.pallas_call(
        body, name="attn_fwd", out_shape=out_shape, grid=(nq,),
        in_specs=[pl.BlockSpec((1, M, QW), lambda i: (i, 0, 0)), _full(kv.shape), _full(kvt.shape),
                  _full(wuv_vc.shape)],
        out_specs=(pl.BlockSpec((1, KVL, M), lambda i: (i, 0, 0)), _rows(TQ, HEADS * 128),
                   pl.BlockSpec((1, 8, M), lambda i: (i, 0, 0))),
        compiler_params=_params(("arbitrary",)),
    )(qs, kv, kvt, wuv_vc)


def _silu_parts(a):
    sg = jax.nn.sigmoid(a)
    return sg, a * sg


def _ffn_fwd(x, ymla, ypool, mod, w_o, g_ffn, wg_t, wu_t, wd, g_final, target):
    S = x.shape[0]
    ts = 256

    nc = FF // FCHUNK

    def body(x_ref, ymla_ref, ypool_ref, mod_ref, wo_hbm, gffn_ref, wg_hbm, wu_hbm, wd_hbm, gfin_ref, t_ref,
             x2_ref, mix_ref, h2t_ref, a_ref, b_ref, dx3_ref, dff_ref, dfft_ref, loss_ref, dgfin_ref, dgt2_ref,
             f_ref, wo_ref, wg_ref, wu_ref, wd_ref, sems):
        first = pl.program_id(0) == 0
        wo_in = _row_chunk_copies(wo_hbm, wo_ref, sems, 0, 1)
        wg_in = _row_chunk_copies(wg_hbm, wg_ref, sems, 1, nc)
        wu_in = _row_chunk_copies(wu_hbm, wu_ref, sems, 1 + nc, nc)
        wd_in = _row_chunk_copies(wd_hbm, wd_ref, sems, 1 + 2 * nc, 1)

        @pl.when(first)
        def _():
            loss_ref[...] = jnp.zeros_like(loss_ref)
            dgfin_ref[...] = jnp.zeros_like(dgfin_ref)
            dgt2_ref[...] = jnp.zeros_like(dgt2_ref)
            for cp in wo_in + [cp for pair in zip(wg_in, wu_in) for cp in pair] + wd_in:
                cp.start()

        gt1 = mod_ref[0:1, 2 * D:3 * D]
        sh2 = mod_ref[0:1, 3 * D:4 * D]
        sc2 = mod_ref[0:1, 4 * D:5 * D]
        gt2 = mod_ref[0:1, 5 * D:6 * D]
        cat = jnp.concatenate([ymla_ref[...], ypool_ref[...]], axis=1)
        pl.when(first)(wo_in[0].wait)
        mix = _dot(cat, wo_ref[...])
        mix_ref[...] = mix.astype(BF)
        x2 = x_ref[...] + gt1 * mix
        x2_ref[...] = x2
        h2 = (x2 * _rms(x2)) * gffn_ref[...] * (1.0 + sc2) + sh2
        h2b = h2.astype(BF)
        h2t_ref[...] = h2.T.astype(BF)

        for c in range(nc):
            cols = slice(c * FCHUNK, (c + 1) * FCHUNK)
            pl.when(first)(wg_in[c].wait)
            pl.when(first)(wu_in[c].wait)
            a = _dot_nt(h2b, wg_ref[cols, :])
            b = _dot_nt(h2b, wu_ref[cols, :])
            a_ref[:, cols] = a.astype(BF)
            b_ref[:, cols] = b.astype(BF)
            f_ref[:, cols] = (_silu_parts(a)[1] * b).astype(BF)
        pl.when(first)(wd_in[0].wait)
        ff = _dot(f_ref[...], wd_ref[...])

        x3 = x2 + gt2 * ff
        r3 = _rms(x3)
        xn3 = x3 * r3
        gfin = gfin_ref[...]
        e = xn3 * gfin - t_ref[...]
        loss_ref[...] += 0.5 * jnp.sum(jnp.mean(e * e, axis=-1, keepdims=True))
        dy = e * (1.0 / D)
        dgfin_ref[...] += _colsum(dy * xn3)
        dx3 = _rms_bwd(dy * gfin, xn3, r3)
        dx3_ref[...] = dx3
        dgt2_ref[...] += _colsum(dx3 * ff)
        dff = dx3 * gt2
        dff_ref[...] = dff.astype(BF)
        dfft_ref[...] = dff.T.astype(BF)

    row = lambda c: _rows(ts, c)
    col = pl.BlockSpec((D, ts), lambda i: (0, i))
    const = _full
    out_shape = (
        jax.ShapeDtypeStruct((S, D), F32),
        jax.ShapeDtypeStruct((S, D), BF),
        jax.ShapeDtypeStruct((D, S), BF),
        jax.ShapeDtypeStruct((S, FF), BF),
        jax.ShapeDtypeStruct((S, FF), BF),
        jax.ShapeDtypeStruct((S, D), F32),
        jax.ShapeDtypeStruct((S, D), BF),
        jax.ShapeDtypeStruct((D, S), BF),
        jax.ShapeDtypeStruct((8, 128), F32),
        jax.ShapeDtypeStruct((1, D), F32),
        jax.ShapeDtypeStruct((1, D), F32),
    )
    return pl.pallas_call(
        body, name="ffn_fwd", out_shape=out_shape, grid=(S // ts,),
        in_specs=[row(D), row(PW), row(PW), const(mod.shape), _any(), const((1, D)), _any(), _any(), _any(),
                  const((1, D)), row(D)],
        out_specs=(row(D), row(D), col, row(FF), row(FF), row(D), row(D), col, const((8, 128)), const((1, D)),
                   const((1, D))),
        scratch_shapes=[pltpu.VMEM((ts, FF), BF), pltpu.VMEM((D, D), BF), pltpu.VMEM((FF, D), BF),
                        pltpu.VMEM((FF, D), BF), pltpu.VMEM((FF, D), BF), pltpu.SemaphoreType.DMA((2 * nc + 2,))],
        compiler_params=_params(("arbitrary",)),
    )(x, ymla, ypool, mod, w_o, g_ffn, wg_t, wu_t, wd, g_final, target)


FCHUNK = 256


def _row_chunk_copies(src, dst, sems, first_sem, chunks):
    r = src.shape[0] // chunks
    return [pltpu.make_async_copy(src.at[pl.ds(c * r, r), :], dst.at[pl.ds(c * r, r), :], sems.at[first_sem + c])
            for c in range(chunks)]


def _ffn_bwd_acts(dff, a, b, wg_t, wu_t, wd):
    S = dff.shape[0]
    ts = 512
    nc = FF // FCHUNK

    def body(dff_ref, a_ref, b_ref, wg_hbm, wu_hbm, wd_hbm, da_ref, db_ref, dh2_ref, wg_ref, wu_ref, wd_ref, sems):
        first = pl.program_id(0) == 0
        wd_in = _row_chunk_copies(wd_hbm, wd_ref, sems, 0, nc)
        rest_in = _row_chunk_copies(wg_hbm, wg_ref, sems, nc, 1) + _row_chunk_copies(wu_hbm, wu_ref, sems, nc + 1, 1)

        @pl.when(first)
        def _():
            for cp in wd_in + rest_in:
                cp.start()

        dffb = dff_ref[...]
        for c in range(nc):
            cols = slice(c * FCHUNK, (c + 1) * FCHUNK)
            pl.when(first)(wd_in[c].wait)
            df = _dot_nt(dffb, wd_ref[cols, :])
            av = a_ref[:, cols].astype(F32)
            bv = b_ref[:, cols].astype(F32)
            sg, sa = _silu_parts(av)
            db_ref[:, cols] = (df * sa).astype(BF)
            da_ref[:, cols] = (df * bv * (sg * (1.0 + av * (1.0 - sg)))).astype(BF)
        for cp in rest_in:
            pl.when(first)(cp.wait)
        dh2_ref[...] = _dot(da_ref[...], wg_ref[...]) + _dot(db_ref[...], wu_ref[...])

    act = _rows(ts, FF)
    return pl.pallas_call(
        body, name="ffn_bwd_acts",
        out_shape=(jax.ShapeDtypeStruct((S, FF), BF), jax.ShapeDtypeStruct((S, FF), BF),
                   jax.ShapeDtypeStruct((S, D), F32)),
        grid=(S // ts,), in_specs=[_rows(ts, D), act, act, _any(), _any(), _any()],
        out_specs=(act, act, _rows(ts, D)),
        scratch_shapes=[pltpu.VMEM((FF, D), BF), pltpu.VMEM((FF, D), BF), pltpu.VMEM((FF, D), BF),
                        pltpu.SemaphoreType.DMA((nc + 2,))],
        compiler_params=_params(("arbitrary",)),
    )(dff, a, b, wg_t, wu_t, wd)


def _ffn_bwd_weights(dff_t, h2_t, da, db, a, b):
    S = da.shape[0]

    def body(dfft_ref, h2t_ref, da_ref, db_ref, a_ref, b_ref, dwg_ref, dwu_ref, dwd_ref):
        h2t = h2t_ref[...]
        dwg_ref[...] = _dot(h2t, da_ref[...]).T.astype(BF)
        dwu_ref[...] = _dot(h2t, db_ref[...]).T.astype(BF)
        f = (_silu_parts(a_ref[...].astype(F32))[1] * b_ref[...].astype(F32)).astype(BF)
        dwd_ref[...] = _dot(dfft_ref[...], f).T.astype(BF)

    act = pl.BlockSpec((S, FCHUNK), lambda j: (0, j))
    wblk = _rows(FCHUNK, D)
    shp = jax.ShapeDtypeStruct((FF, D), BF)
    return pl.pallas_call(
        body, name="ffn_bwd_weights", out_shape=(shp, shp, shp), grid=(FF // FCHUNK,),
        in_specs=[_vmem(), _vmem(), act, act, act, act], out_specs=(wblk, wblk, wblk),
        compiler_params=_params(("arbitrary",)),
    )(dff_t, h2_t, da, db, a, b)


def _mix_bwd(dh2, dx3, x2, mix, mod, g_ffn, ymla, ypool, w_o, ypre, pooled, pool_scale, wpool_dc, olat, wuv_vc):
    S = dh2.shape[0]
    ts = 512
    n = S // ts
    nsub = ts // TQ
    M = HEADS * TQ

    def body(dh2_ref, dx3_ref, x2_ref, mix_ref, mod_ref, gffn_ref, ymla_ref, ypool_ref, wo_ref, ypre_ref, pooled_ref,
             pscale_ref, wpool_ref, olat_ref, wuv_ref,
             dx2_ref, du_ref, dolat_ref, delta_ref, dwo_ref, dwuv_ref, dwpool_ref, dpscale_ref, dgt1_ref, dsc2_ref,
             dsh2_ref, dgffn_ref, carry_ref, dwo_acc, dwpool_acc, wpool_bd, wuv_bd):
        i = pl.program_id(0)

        @pl.when(i == 0)
        def _():
            carry_ref[...] = jnp.zeros_like(carry_ref)
            dwo_acc[...] = jnp.zeros_like(dwo_acc)
            dwpool_acc[...] = jnp.zeros_like(dwpool_acc)
            _fill_block_diagonal(wpool_bd, wpool_ref)
            _fill_block_diagonal(wuv_bd, wuv_ref)
            for r in (dwuv_ref, dpscale_ref, dgt1_ref, dsc2_ref, dsh2_ref, dgffn_ref):
                r[...] = jnp.zeros_like(r)

        gt1 = mod_ref[0:1, 2 * D:3 * D]
        sc2 = mod_ref[0:1, 4 * D:5 * D]
        gffn = gffn_ref[...]
        dh2 = dh2_ref[...]
        x2 = x2_ref[...]
        r2 = _rms(x2)
        xn2 = x2 * r2
        along = _colsum(dh2 * xn2)
        dsc2_ref[...] += along * gffn
        dsh2_ref[...] += _colsum(dh2)
        dgffn_ref[...] += along * (1.0 + sc2)
        dx2 = dx3_ref[...] + _rms_bwd(dh2 * (gffn * (1.0 + sc2)), xn2, r2)
        dx2_ref[...] = dx2
        dgt1_ref[...] += _colsum(dx2 * mix_ref[...].astype(F32))
        dmix = (dx2 * gt1).astype(BF)
        cat = jnp.concatenate([ymla_ref[...], ypool_ref[...]], axis=1)
        dwo_acc[...] += _dot_tn(cat, dmix)
        dcat = _dot_nt(dmix, wo_ref[...])
        dymla = dcat[:, 0:512]
        dypool = dcat[:, 512:1024]

        dpscale_ref[...] += _colsum(dypool * ypre_ref[...].astype(F32))
        dypre = (dypool * pscale_ref[...]).astype(BF)
        dwpool_acc[...] += _dot_tn(pooled_ref[...], dypre)
        dpooled = _dot(dypre, wpool_bd[...])
        tile = n - 1 - i
        e = dpooled / _row_counts(tile * ts, ts)
        ext = jnp.concatenate([e, carry_ref[...]], axis=0)
        du_ref[...] = (_window_sums(ext, False)[0:ts, :] - dpooled).astype(BF)
        carry_ref[...] = e[0:16, :]

        dob_all = dymla.astype(BF)
        dol_all = _dot(dob_all, wuv_bd[...])
        for hd in range(HEADS):
            dob = dob_all[:, hd * 128:(hd + 1) * 128]
            dol = dol_all[:, hd * 128:(hd + 1) * 128]
            for a in range(nsub):
                ol_t = olat_ref[a, :, hd * TQ:(hd + 1) * TQ]
                dl = dol[a * TQ:(a + 1) * TQ, :]
                dolat_ref[a, hd * TQ:(hd + 1) * TQ, :] = dl.astype(BF)
                dwuv_ref[hd] += _dot(ol_t.astype(BF), dob[a * TQ:(a + 1) * TQ, :])
                delta = jnp.sum(dl * ol_t.T, axis=-1, keepdims=True)
                delta_ref[a, :, hd * TQ:(hd + 1) * TQ] = jnp.broadcast_to(delta, (TQ, 128)).T[0:8, :]

        @pl.when(i == n - 1)
        def _():
            dwo_ref[...] = dwo_acc[...].astype(BF)
            for g in range(GROUPS):
                dwpool_ref[g] = dwpool_acc[g * GD:(g + 1) * GD, g * GD:(g + 1) * GD]

    rev = lambda c: pl.BlockSpec((ts, c), lambda i: (n - 1 - i, 0))
    rev3 = lambda r, c: pl.BlockSpec((nsub, r, c), lambda i: (n - 1 - i, 0, 0))
    out_shape = (
        jax.ShapeDtypeStruct((S, D), F32),
        jax.ShapeDtypeStruct((S, PW), BF),
        jax.ShapeDtypeStruct((S // TQ, M, KVL), BF),
        jax.ShapeDtypeStruct((S // TQ, 8, M), F32),
        jax.ShapeDtypeStruct((D, D), BF),
        jax.ShapeDtypeStruct((HEADS, KVL, 128), F32),
        jax.ShapeDtypeStruct((GROUPS, GD, GD), F32),
        jax.ShapeDtypeStruct((1, PW), F32),
        jax.ShapeDtypeStruct((1, D), F32), jax.ShapeDtypeStruct((1, D), F32), jax.ShapeDtypeStruct((1, D), F32),
        jax.ShapeDtypeStruct((1, D), F32),
    )
    in_specs = [rev(D), rev(D), rev(D), rev(D), _full(mod.shape), _full((1, D)), rev(PW), rev(PW), _full(w_o.shape),
                rev(PW), rev(PW), _full((1, PW)), _full(wpool_dc.shape), rev3(KVL, M), _full(wuv_vc.shape)]
    out_specs = (rev(D), rev(PW), rev3(M, KVL), rev3(8, M), _full((D, D)), _full((HEADS, KVL, 128)),
                 _full((GROUPS, GD, GD)), _full((1, PW)), _full((1, D)), _full((1, D)), _full((1, D)), _full((1, D)))
    return pl.pallas_call(
        body, name="mix_bwd", out_shape=out_shape, grid=(n,), in_specs=in_specs, out_specs=out_specs,
        scratch_shapes=[pltpu.VMEM((16, PW), F32), pltpu.VMEM((D, D), F32), pltpu.VMEM((PW, PW), F32),
                        pltpu.VMEM((PW, PW), BF), pltpu.VMEM((HEADS * 128, HEADS * KVL), BF)],
        compiler_params=_params(("arbitrary",)),
    )(dh2, dx3, x2, mix, mod, g_ffn, ymla, ypool, w_o, ypre, pooled, pool_scale, wpool_dc, olat, wuv_vc)


def _attn_bwd(qs, kv, dolat, lse, delta):
    nq = qs.shape[0]
    S = kv.shape[0]
    M = HEADS * TQ
    nk = S // TK

    def body(qs_ref, kv_ref, do_ref, lse_ref, delta_ref, dkv_ref, dqt_out_ref, dqt_ref, p_ref, ds_ref):
        kt = pl.program_id(0)
        k = kv_ref[...]
        v = k[:, 0:KVL]
        k_t = k.astype(F32).T.astype(BF)

        @pl.when(kt == 0)
        def _():
            dqt_ref[...] = jnp.zeros_like(dqt_ref)

        def step(qi, carry, first_chunk=None):
            dk, dv = carry
            q = qs_ref[qi]
            do = do_ref[qi]
            s = _dot_nt(k, q)
            dp = _dot_nt(v, do)
            lse_row = lse_ref[qi, 0:1, :] * LOG2_E
            delta_row = delta_ref[qi, 0:1, :]
            q_chunk = (lax.broadcasted_iota(jnp.int32, (1, M), 1) & (TQ - 1)) >> 6
            for r in range(0, TK, VPU_ROWS):
                rows = slice(r, r + VPU_ROWS)
                p = jnp.exp2(s[rows, :] * EXP2_SCALE - lse_row)
                if first_chunk is not None:
                    p = jnp.where((r >> 6) + first_chunk <= q_chunk, p, 0.0)
                p_ref[rows, :] = p.astype(BF)
                ds_ref[rows, :] = (p * (dp[rows, :] - delta_row) * SM_SCALE).astype(BF)
            ds = ds_ref[...]
            dv = dv + _dot(p_ref[...], do)
            dk = dk + _dot(ds, q)
            dqt_ref[qi] += _dot(k_t, ds)
            return dk, dv

        per = TQ // TK
        first = kt // per
        carry = step(first, (jnp.zeros((TK, QW), F32), jnp.zeros((TK, KVL), F32)), (kt % per) * (TK // 64))
        dk, dv = lax.fori_loop(first + 1, nq, step, carry)
        dkv_ref[...] = dk + jnp.concatenate([dv, jnp.zeros((TK, QW - KVL), F32)], axis=1)
        dqt_out_ref[0] = dqt_ref[first].astype(BF)

    out_shape = (jax.ShapeDtypeStruct((S, QW), F32), jax.ShapeDtypeStruct((nq, QW, M), BF))
    return pl.pallas_call(
        body, name="attn_bwd", out_shape=out_shape, grid=(nk,),
        in_specs=[_vmem(), _rows(TK, QW), _vmem(), _vmem(), _vmem()],
        out_specs=(_rows(TK, QW), pl.BlockSpec((1, QW, M), lambda kt: (kt // (TQ // TK), 0, 0))),
        scratch_shapes=[pltpu.VMEM((nq, QW, M), F32), pltpu.VMEM((TK, M), BF), pltpu.VMEM((TK, M), BF)],
        compiler_params=_params(("arbitrary",)),
    )(qs, kv, dolat, lse, delta)


def _in_bwd(dqt, dkv, du, raw, qn, h1, x, dx2, mod, g_mix, w_in, g_q, g_kv, w_uq, wuk_cd, perm_t, cos4, sin4, csk,
            snk):
    S = x.shape[0]
    ts = 512
    n = S // ts
    nsub = ts // TQ
    M = HEADS * TQ

    def body(dqt_ref, dkv_ref, du_ref, raw_ref, qn_ref, h1_ref, x_ref, dx2_ref, mod_ref, gmix_ref, win_ref, gq_ref,
             gkv_ref, wuq_ref, wuk_ref, permt_ref, cos_ref, sin_ref, csk_ref, snk_ref,
             dx_ref, dwin_ref, dwuq_ref, dwuk_ref, dgq_ref, dgkv_ref, dsc1_ref, dsh1_ref, dgmix_ref, dwin_acc,
             dwuq_acc, dwuk_acc, wuk_bd):
        i = pl.program_id(0)

        @pl.when(i == 0)
        def _():
            dwin_acc[...] = jnp.zeros_like(dwin_acc)
            dwuq_acc[...] = jnp.zeros_like(dwuq_acc)
            dwuk_acc[...] = jnp.zeros_like(dwuk_acc)
            _fill_block_diagonal(wuk_bd, wuk_ref)
            for r in (dgq_ref, dgkv_ref, dsc1_ref, dsh1_ref, dgmix_ref):
                r[...] = jnp.zeros_like(r)

        dq_blocks = [dqt_ref[a].astype(F32).T for a in range(nsub)]
        dq_heads = [jnp.concatenate([blk[hd * TQ:(hd + 1) * TQ, :] for blk in dq_blocks], axis=0)
                    for hd in range(HEADS)]
        dq_lat = jnp.concatenate([dqh[:, 0:KVL] for dqh in dq_heads], axis=1).astype(BF)
        dq_rope = jnp.concatenate([dqh[:, KVL:QW] for dqh in dq_heads], axis=1).astype(BF)
        dq_nope = _dot(dq_lat, wuk_bd[...])
        dwuk_acc[...] += _dot_tn(dq_lat, qn_ref[...])
        drope = _dot(dq_rope, permt_ref[...])
        do1 = drope[:, 0:128]
        do2 = drope[:, 128:256]
        cosv = cos_ref[...]
        sinv = sin_ref[...]
        dq = jnp.concatenate([dq_nope, do1 * cosv + do2 * sinv, do2 * cosv - do1 * sinv], axis=1).astype(BF)

        cq_raw = raw_ref[:, 0:QL]
        ckv_raw = raw_ref[:, QL:QL + KVL]
        rq = _rms(cq_raw)
        nq_ = cq_raw * rq
        gq = gq_ref[...]
        dwuq_acc[...] += _dot_tn((nq_ * gq).astype(BF), dq)
        dc_q = _dot_nt(dq, wuq_ref[...])
        dgq_ref[...] += _colsum(dc_q * nq_)
        dcq_raw = _rms_bwd(dc_q * gq, nq_, rq)

        dkv = dkv_ref[...]
        rk = _rms(ckv_raw)
        nk_ = ckv_raw * rk
        dc_kv = dkv[:, 0:KVL]
        dgkv_ref[...] += _colsum(dc_kv * nk_)
        dckv_raw = _rms_bwd(dc_kv * gkv_ref[...], nk_, rk)
        dkr_roped = dkv[:, KVL:QW]
        dkr = dkr_roped * csk_ref[...] - _swap_halves(dkr_roped) * snk_ref[...]

        dproj = jnp.concatenate([dcq_raw.astype(BF), dckv_raw.astype(BF), dkr.astype(BF), du_ref[...]], axis=1)
        dwin_acc[...] += _dot_tn(h1_ref[...], dproj)
        dh1 = _dot_nt(dproj, win_ref[...])

        sc1 = mod_ref[0:1, D:2 * D]
        gmix = gmix_ref[...]
        xv = x_ref[...]
        r1 = _rms(xv)
        xn1 = xv * r1
        along = _colsum(dh1 * xn1)
        dsc1_ref[...] += along * gmix
        dsh1_ref[...] += _colsum(dh1)
        dgmix_ref[...] += along * (1.0 + sc1)
        dx_ref[...] = dx2_ref[...] + _rms_bwd(dh1 * (gmix * (1.0 + sc1)), xn1, r1)

        @pl.when(i == n - 1)
        def _():
            dwin_ref[...] = dwin_acc[...].astype(BF)
            dwuq_ref[...] = dwuq_acc[...].astype(BF)
            for hd in range(HEADS):
                dwuk_ref[hd] = dwuk_acc[hd * KVL:(hd + 1) * KVL, hd * NOPE:(hd + 1) * NOPE]

    out_shape = (
        jax.ShapeDtypeStruct((S, D), F32),
        jax.ShapeDtypeStruct((D, D), BF),
        jax.ShapeDtypeStruct((QL, 768), BF),
        jax.ShapeDtypeStruct((HEADS, KVL, NOPE), F32),
        jax.ShapeDtypeStruct((1, QL), F32), jax.ShapeDtypeStruct((1, KVL), F32),
        jax.ShapeDtypeStruct((1, D), F32), jax.ShapeDtypeStruct((1, D), F32), jax.ShapeDtypeStruct((1, D), F32),
    )
    in_specs = [pl.BlockSpec((nsub, QW, M), lambda i: (i, 0, 0)), _rows(ts, QW), _rows(ts, PW), _rows(ts, 384),
                _rows(ts, HEADS * NOPE), _rows(ts, D), _rows(ts, D), _rows(ts, D), _full(mod.shape), _full((1, D)),
                _full(w_in.shape), _full((1, QL)), _full((1, KVL)), _full(w_uq.shape), _full(wuk_cd.shape),
                _full(perm_t.shape), _rows(ts, 128), _rows(ts, 128), _rows(ts, 128), _rows(ts, 128)]
    out_specs = (_rows(ts, D), _full((D, D)), _full((QL, 768)), _full((HEADS, KVL, NOPE)), _full((1, QL)),
                 _full((1, KVL)), _full((1, D)), _full((1, D)), _full((1, D)))
    return pl.pallas_call(
        body, name="in_bwd", out_shape=out_shape, grid=(n,), in_specs=in_specs, out_specs=out_specs,
        scratch_shapes=[pltpu.VMEM((D, D), F32), pltpu.VMEM((QL, 768), F32),
                        pltpu.VMEM((HEADS * KVL, HEADS * NOPE), F32), pltpu.VMEM((HEADS * KVL, HEADS * NOPE), BF)],
        compiler_params=_params(("arbitrary",)),
    )(dqt, dkv, du, raw, qn, h1, x, dx2, mod, g_mix, w_in, g_q, g_kv, w_uq, wuk_cd, perm_t, cos4, sin4, csk, snk)


def _rope_perm():
    p = np.zeros((HEADS, 2 * 128, 128), np.float32)
    for hd in range(HEADS):
        for t in range(HALF):
            p[hd, hd * HALF + t, t] = 1.0
            p[hd, 128 + hd * HALF + t, HALF + t] = 1.0
    return p


def _rope_tables(positions):
    freqs = jnp.power(ROPE_THETA, -jnp.arange(HALF, dtype=F32) / HALF)
    ang = positions.astype(F32)[:, None] * jnp.tile(freqs, HEADS)[None, :]
    cos4 = jnp.cos(ang)
    sin4 = jnp.sin(ang)
    lane = jnp.arange(HEADS * HALF)[None, :]
    csk = jnp.where(lane < ROPE, cos4, 0.0)
    snk = jnp.where(lane < HALF, -sin4, jnp.where(lane < ROPE, sin4, 0.0))
    return cos4, sin4, csk, snk


def _local_step(x, rope, target, mod, g_mix, w_in_p, g_q, g_kv, w_uq_p, w_uk, w_uv, w_pool, pool_scale, g_ffn,
                g_final, late, ffn_grads_exchange):
    perm = jnp.asarray(_rope_perm().transpose(1, 0, 2).reshape(2 * 128, HEADS * 128), BF)
    perm_t = jnp.asarray(_rope_perm().transpose(0, 2, 1).reshape(HEADS * 128, 2 * 128), BF)
    cos4, sin4, csk, snk = rope
    wuk_dc = w_uk.transpose(1, 2, 0).astype(BF)
    wuk_cd = w_uk.transpose(1, 0, 2).astype(BF)
    wuv_vc = w_uv.transpose(1, 2, 0).astype(BF)
    wpool = w_pool.astype(BF)
    wpool_dc = w_pool.transpose(0, 2, 1).astype(BF)

    h1, raw, qn, qs, kv, kvt, pooled, ypre, ypool = _fwd_in(
        x, mod, g_mix, w_in_p, g_q, g_kv, w_uq_p, wuk_dc, perm, cos4, sin4, csk, snk, wpool, pool_scale)
    olat, ymla, lse = _attn_fwd(qs, kv, kvt, wuv_vc)
    w_o, wg_t, wu_t, wd = late
    x2, mix, h2_t, a, b, dx3, dff, dff_t, loss, dgfin, dgt2 = _ffn_fwd(
        x, ymla, ypool, mod, w_o, g_ffn, wg_t, wu_t, wd, g_final, target)
    da, db, dh2 = _ffn_bwd_acts(dff, a, b, wg_t, wu_t, wd)
    (dx2, du, dolat, delta, dwo, dwuv, dwpool, dpscale, dgt1, dsc2, dsh2, dgffn) = _mix_bwd(
        dh2, dx3, x2, mix, mod, g_ffn, ymla, ypool, w_o, ypre, pooled, pool_scale, wpool_dc, olat, wuv_vc)
    dwg_t, dwu_t, dwd = _ffn_bwd_weights(dff_t, h2_t, da, db, a, b)
    ffn_parts = ffn_grads_exchange((dwg_t, dwu_t, dwd, dwo))
    dkv, dqt = _attn_bwd(qs, kv, dolat, lse, delta)
    dx, dwin, dwuq, dwuk, dgq, dgkv, dsc1, dsh1, dgmix = _in_bwd(
        dqt, dkv, du, raw, qn, h1, x, dx2, mod, g_mix, w_in_p, g_q, g_kv, w_uq_p, wuk_cd, perm_t, cos4, sin4, csk,
        snk)
    dmod = jnp.concatenate([dsh1, dsc1, dgt1, dsh2, dsc2, dgt2], axis=1)
    replicated = dict(
        w_uk=dwuk.transpose(1, 0, 2), w_uv=dwuv.transpose(1, 0, 2), w_pool=dwpool, g_mix=dgmix, g_q=dgq, g_kv=dgkv,
        pool_scale=dpscale, g_ffn=dgffn, g_final=dgfin)
    return loss[0, 0], dx, dmod, (dwin, dwuq), ffn_parts, replicated


def _my_pos():
    return lax.axis_index("x"), lax.axis_index("y"), lax.axis_index("c")


def _peer(pos, k):
    x, y, c = pos
    return (1 - x if k & 4 else x, 1 - y if k & 2 else y, 1 - c if k & 1 else c)


def _index(pos):
    x, y, c = pos
    return 4 * x + 2 * y + c


def _remote(src, dst, send_sem, recv_sem, to):
    return pltpu.make_async_remote_copy(src_ref=src, dst_ref=dst, send_sem=send_sem, recv_sem=recv_sem,
                                        device_id=to, device_id_type=MESH)


def _ada_mod(c, w_ada, b_ada, after):
    def body(c_ref, w_ref, b_ref, after_ref, mod_ref, call_ref, cbuf, sbuf, rbuf, send1, recv1, send2, recv2):
        me = _my_pos()
        mi = _index(me)
        cv = c_ref[...]
        cbuf[...] = jnp.broadcast_to(cv * jax.nn.sigmoid(cv), (8, D))
        call_ref[mi] = cbuf[...]
        first = [_remote(cbuf, call_ref.at[mi], send1.at[k - 1], recv1.at[k - 1], _peer(me, k)) for k in range(1, NDEV)]
        for cp in first:
            cp.start()
        for k in range(1, NDEV):
            _remote(cbuf, call_ref.at[_index(_peer(me, k))], send1.at[k - 1], recv1.at[k - 1], _peer(me, k)).wait_recv()
        c_all = jnp.concatenate([call_ref[b][0:1, :] for b in range(NDEV)], axis=0)
        blocks = _dot(c_all.astype(BF), w_ref[...].astype(BF))
        for b in range(NDEV):
            sbuf[b] = jnp.broadcast_to(blocks[b:b + 1, :], (8, MODC))
        second = []
        for k in range(1, NDEV):
            to = _peer(me, k)
            second.append(_remote(sbuf.at[_index(to)], rbuf.at[mi], send2.at[k - 1], recv2.at[k - 1], to))
        for cp in second:
            cp.start()
        rbuf[mi] = sbuf[mi]
        for k in range(1, NDEV):
            to = _peer(me, k)
            _remote(sbuf.at[_index(to)], rbuf.at[_index(to)], send2.at[k - 1], recv2.at[k - 1], to).wait_recv()
        for j in range(NDEV):
            mod_ref[:, j * MODC:(j + 1) * MODC] = rbuf[j] + b_ref[:, j * MODC:(j + 1) * MODC]
        for cp in first + second:
            cp.wait_send()

    return pl.pallas_call(
        body, name="ada_mod",
        out_shape=(jax.ShapeDtypeStruct((8, N_MOD * D), F32), jax.ShapeDtypeStruct((NDEV, 8, D), F32)),
        in_specs=[_vmem(), _vmem(), _vmem(), _any()], out_specs=(_vmem(), _vmem()),
        scratch_shapes=[pltpu.VMEM((8, D), F32), pltpu.VMEM((NDEV, 8, MODC), F32), pltpu.VMEM((NDEV, 8, MODC), F32),
                        pltpu.SemaphoreType.DMA((NDEV - 1,)), pltpu.SemaphoreType.DMA((NDEV - 1,)),
                        pltpu.SemaphoreType.DMA((NDEV - 1,)), pltpu.SemaphoreType.DMA((NDEV - 1,))],
        compiler_params=_params(),
    )(c, w_ada, b_ada, after)


def _sequencer_scatter(name, collective_id, srcs, after=()):
    n = len(srcs)

    def of(src, to_index):
        r = src.shape[0] // NDEV
        return src.at[pl.ds(pl.multiple_of(to_index * r, 16), r), :]

    def body(*refs):
        src, zone = refs[:n], refs[n + len(after):2 * n + len(after)]
        send, recv, local = refs[2 * n + len(after):]
        me = _my_pos()
        mi = _index(me)
        barrier = pltpu.get_barrier_semaphore()
        for k in range(1, NDEV):
            pl.semaphore_signal(barrier, inc=1, device_id=_peer(me, k), device_id_type=MESH)
        pl.semaphore_wait(barrier, NDEV - 1)
        own = [pltpu.make_async_copy(of(src[a], mi), zone[a].at[mi], local.at[a]) for a in range(n)]
        for cp in own:
            cp.start()
        for a in range(n):
            for k in range(1, NDEV):
                to = _peer(me, k)
                s = a * (NDEV - 1) + k - 1
                _remote(of(src[a], _index(to)), zone[a].at[mi], send.at[s], recv.at[s], to).start()
        for cp in own:
            cp.wait()
        for a in range(n):
            for k in range(1, NDEV):
                to = _peer(me, k)
                s = a * (NDEV - 1) + k - 1
                cp = _remote(of(src[a], mi), zone[a].at[_index(to)], send.at[s], recv.at[s], to)
                cp.wait_send()
                cp.wait_recv()

    return pl.kernel(
        body, name=name, mesh=plsc.ScalarSubcoreMesh(axis_name="sequencer", num_cores=1),
        out_type=tuple(jax.ShapeDtypeStruct((NDEV, s.shape[0] // NDEV, s.shape[1]), s.dtype) for s in srcs),
        scratch_types=[pltpu.SemaphoreType.DMA((n * (NDEV - 1),)), pltpu.SemaphoreType.DMA((n * (NDEV - 1),)),
                       pltpu.SemaphoreType.DMA((n,))],
        compiler_params=pltpu.CompilerParams(collective_id=collective_id),
    )(*srcs, *after)


CHIP_PEERS = (2, 4, 6)


def _sequencer_gather(name, collective_id, srcs, after=()):
    n = len(srcs)
    per = NDEV - 1

    def body(*refs):
        src, zone = refs[:n], refs[n + len(after):2 * n + len(after)]
        send, recv, local = refs[2 * n + len(after):]
        me = _my_pos()
        mi = _index(me)
        sibling = _peer(me, 1)
        talk_to = (sibling,) + tuple(_peer(me, k) for k in CHIP_PEERS)
        barrier = pltpu.get_barrier_semaphore()
        for to in talk_to:
            pl.semaphore_signal(barrier, inc=1, device_id=to, device_id_type=MESH)
        pl.semaphore_wait(barrier, len(talk_to))

        def copy(a, slot, block_of, to, from_src=False):
            rows = zone[a].at[_index(block_of)]
            return _remote(src[a] if from_src else rows, rows, send.at[a * per + slot], recv.at[a * per + slot], to)

        own = [pltpu.make_async_copy(src[a], zone[a].at[mi], local.at[a]) for a in range(n)]
        for cp in own:
            cp.start()
        started = []
        for a in range(n):
            started.append(copy(a, 0, me, sibling, from_src=True))
            started += [copy(a, 1 + j, me, _peer(me, k), from_src=True) for j, k in enumerate(CHIP_PEERS)]
        for cp in started:
            cp.start()
        for a in range(n):
            for j, k in enumerate(CHIP_PEERS):
                copy(a, 1 + j, _peer(me, k), me).wait_recv()
                passed = copy(a, 4 + j, _peer(me, k), sibling)
                passed.start()
                started.append(passed)
        for a in range(n):
            copy(a, 0, sibling, me).wait_recv()
            for j, k in enumerate(CHIP_PEERS):
                copy(a, 4 + j, _peer(me, k | 1), me).wait_recv()
        for cp in started:
            cp.wait_send()
        for cp in own:
            cp.wait()

    return pl.kernel(
        body, name=name, mesh=plsc.ScalarSubcoreMesh(axis_name="sequencer", num_cores=1),
        out_type=tuple(jax.ShapeDtypeStruct((NDEV,) + s.shape, s.dtype) for s in srcs),
        scratch_types=[pltpu.SemaphoreType.DMA((n * per,)), pltpu.SemaphoreType.DMA((n * per,)),
                       pltpu.SemaphoreType.DMA((n,))],
        compiler_params=pltpu.CompilerParams(collective_id=collective_id),
    )(*srcs, *after)


def _blocked(shape, nb, axis=0):
    block = tuple(s // nb if d == axis else s for d, s in enumerate(shape))
    return pl.BlockSpec(block, lambda i: tuple(i if d == axis else 0 for d in range(len(shape))))


def _sum_partials(name, parts, nb):
    n = len(parts)

    def body(*refs):
        for a in range(n):
            acc = refs[a][0].astype(F32)
            for p in range(1, NDEV):
                acc = acc + refs[a][p].astype(F32)
            refs[n + a][...] = acc

    return pl.pallas_call(
        body, name=name, grid=(nb,),
        out_shape=tuple(jax.ShapeDtypeStruct(p.shape[1:], F32) for p in parts),
        in_specs=[_blocked(p.shape, nb, 1) for p in parts],
        out_specs=tuple(_blocked(p.shape[1:], nb) for p in parts), compiler_params=_params(("arbitrary",)),
    )(*parts)


def _small_all_reduce(buf):
    def body(buf_ref, got_ref, red_ref, mine, send1, recv1, send2, recv2):
        me = _my_pos()
        mi = _index(me)
        first = []
        for k in range(1, NDEV):
            to = _peer(me, k)
            first.append(_remote(buf_ref.at[_index(to)], got_ref.at[mi], send1.at[k - 1], recv1.at[k - 1], to))
        for cp in first:
            cp.start()
        got_ref[mi] = buf_ref[mi]
        for k in range(1, NDEV):
            to = _peer(me, k)
            _remote(buf_ref.at[mi], got_ref.at[_index(to)], send1.at[k - 1], recv1.at[k - 1], to).wait_recv()
        acc = got_ref[0]
        for p in range(1, NDEV):
            acc = acc + got_ref[p]
        mine[...] = acc
        second = [_remote(mine, red_ref.at[mi], send2.at[k - 1], recv2.at[k - 1], _peer(me, k)) for k in range(1, NDEV)]
        for cp in second:
            cp.start()
        red_ref[mi] = acc
        for k in range(1, NDEV):
            to = _peer(me, k)
            _remote(mine, red_ref.at[_index(to)], send2.at[k - 1], recv2.at[k - 1], to).wait_recv()
        for cp in first + second:
            cp.wait_send()

    return pl.pallas_call(
        body, name="small_all_reduce",
        out_shape=(jax.ShapeDtypeStruct(buf.shape, F32), jax.ShapeDtypeStruct(buf.shape, F32)),
        in_specs=[_vmem()], out_specs=(_vmem(), _vmem()),
        scratch_shapes=[pltpu.VMEM(buf.shape[1:], F32),
                        pltpu.SemaphoreType.DMA((NDEV - 1,)), pltpu.SemaphoreType.DMA((NDEV - 1,)),
                        pltpu.SemaphoreType.DMA((NDEV - 1,)), pltpu.SemaphoreType.DMA((NDEV - 1,))],
        compiler_params=_params(),
    )(buf)


def _adamw_math(w, g, m, v):
    m = ADAM_B1 * m + (1.0 - ADAM_B1) * g
    v = ADAM_B2 * v + (1.0 - ADAM_B2) * jnp.square(g)
    m_hat = m / (1.0 - ADAM_B1 ** ADAM_STEP)
    v_hat = v / (1.0 - ADAM_B2 ** ADAM_STEP)
    delta = -ADAM_LR * (m_hat / (jnp.sqrt(v_hat) + ADAM_EPS) + ADAM_WD * w)
    return delta, m, v


def _adamw_group(name, ws, gs, ms, vs, nb):
    n = len(ws)

    def body(*refs):
        for a in range(n):
            w, g, m, v = (refs[q * n + a][...] for q in range(4))
            delta, m2, v2 = _adamw_math(w, g, m, v)
            refs[4 * n + a][...] = delta
            refs[5 * n + a][...] = m2
            refs[6 * n + a][...] = v2

    shapes = tuple(jax.ShapeDtypeStruct(w.shape, F32) for w in ws)
    specs = [_blocked(w.shape, nb) for w in ws]
    outs = pl.pallas_call(
        body, name=name, grid=(nb,), out_shape=shapes * 3, in_specs=specs * 4, out_specs=tuple(specs * 3),
        compiler_params=_params(("arbitrary",)),
    )(*ws, *gs, *ms, *vs)
    return outs[:n], outs[n:2 * n], outs[2 * n:]


def _adamw_from_partials(name, ws, parts, ms, vs, nb):
    n = len(ws)

    def body(*refs):
        for a in range(n):
            part = refs[n + a]
            g = part[0].astype(F32)
            for p in range(1, NDEV):
                g = g + part[p].astype(F32)
            delta, m2, v2 = _adamw_math(refs[a][...], g, refs[2 * n + a][...], refs[3 * n + a][...])
            refs[4 * n + a][...] = g
            refs[5 * n + a][...] = delta
            refs[6 * n + a][...] = m2
            refs[7 * n + a][...] = v2

    shapes = tuple(jax.ShapeDtypeStruct(w.shape, F32) for w in ws)
    specs = [_blocked(w.shape, nb) for w in ws]
    outs = pl.pallas_call(
        body, name=name, grid=(nb,), out_shape=shapes * 4,
        in_specs=specs + [_blocked(p.shape, nb, 1) for p in parts] + specs * 2, out_specs=tuple(specs * 4),
        compiler_params=_params(("arbitrary",)),
    )(*ws, *parts, *ms, *vs)
    return outs[:n], outs[n:2 * n], outs[2 * n:3 * n], outs[3 * n:]


def _adamw_ada(w, m, v, c_all_t, dmod_rows):
    nb = 4

    def body(w_ref, m_ref, v_ref, c_ref, dm_ref, g_ref, d_ref, m2_ref, v2_ref):
        g = _dot(c_ref[...], dm_ref[...].astype(BF))
        g_ref[...] = g
        delta, m2, v2 = _adamw_math(w_ref[...], g, m_ref[...], v_ref[...])
        d_ref[...] = delta
        m2_ref[...] = m2
        v2_ref[...] = v2

    shp = jax.ShapeDtypeStruct(w.shape, F32)
    spec = _blocked(w.shape, nb)
    return pl.pallas_call(
        body, name="adamw_ada", grid=(nb,), out_shape=(shp, shp, shp, shp),
        in_specs=[spec, spec, spec, _blocked(c_all_t.shape, nb), _full(dmod_rows.shape)],
        out_specs=(spec, spec, spec, spec), compiler_params=_params(("arbitrary",)),
    )(w, m, v, c_all_t, dmod_rows)


def _w_in_to_kernel(w):
    return jnp.concatenate([w[:, 0:448], jnp.zeros((w.shape[0], 64), w.dtype), w[:, 448:960]], axis=1)


def _w_in_from_kernel(w):
    return jnp.concatenate([w[:, 0:448], w[:, 512:1024]], axis=1)


def _w_uq_to_kernel(w):
    r = w.shape[0]
    return jnp.concatenate([w[:, :, 0:NOPE].reshape(r, HEADS * NOPE),
                            w[:, :, NOPE:NOPE + HALF].reshape(r, HEADS * HALF),
                            w[:, :, NOPE + HALF:].reshape(r, HEADS * HALF)], axis=1)


def _w_uq_from_kernel(w):
    r = w.shape[0]
    return jnp.concatenate([w[:, 0:512].reshape(r, HEADS, NOPE), w[:, 512:640].reshape(r, HEADS, HALF),
                            w[:, 640:768].reshape(r, HEADS, HALF)], axis=2)


REP_NAMES = ("w_uk", "w_uv", "w_pool", "g_mix", "g_q", "g_kv", "pool_scale", "g_ffn", "g_final")


def kernel(x, c, positions, w_ada, b_ada, g_mix, w_in, g_q, g_kv, w_uq, w_uk, w_uv, w_pool, pool_scale, w_o, g_ffn, w_gate, w_up, w_down, g_final, loss_target, m_w_ada, m_b_ada, m_g_mix, m_w_in, m_g_q, m_g_kv, m_w_uq, m_w_uk, m_w_uv, m_w_pool, m_pool_scale, m_w_o, m_g_ffn, m_w_gate, m_w_up, m_w_down, m_g_final, v_w_ada, v_b_ada, v_g_mix, v_w_in, v_g_q, v_g_kv, v_w_uq, v_w_uk, v_w_uv, v_w_pool, v_pool_scale, v_w_o, v_g_ffn, v_w_gate, v_w_up, v_w_down, v_g_final):
    given = dict(locals())

    merge = lambda g: g.reshape(NDEV * g.shape[1], g.shape[2])
    w_in_p, w_uq_p = (merge(g) for g in _sequencer_gather(
        "gather_in", 3, (_w_in_to_kernel(w_in[0]).astype(BF), _w_uq_to_kernel(w_uq[0]).astype(BF))))

    rope = _rope_tables(positions[0])
    mod, c_all8 = _ada_mod(c, w_ada[0], b_ada, rope[3][0:8, :])
    c_all = c_all8[:, 0, :]
    late = _sequencer_gather(
        "gather_late", 1, (w_o[0].astype(BF), w_gate[0].T.astype(BF), w_up[0].T.astype(BF), w_down[0].astype(BF)),
        after=(mod[:, 0:128], w_in_p[0:16, 0:128], w_uq_p[0:16, 0:128]))

    def ffn_grads_exchange(arrays):
        return _sequencer_scatter("scatter_ffn", 2, arrays)

    loss, dx, dmod, tail_grads, ffn_parts, replicated = _local_step(
        x[0], rope, loss_target[0], mod, g_mix, w_in_p, g_q, g_kv, w_uq_p, w_uk[0], w_uv[0], w_pool[0],
        pool_scale, g_ffn, g_final.reshape(1, D), tuple(merge(g) for g in late), ffn_grads_exchange)

    flat = jnp.concatenate([replicated[k].reshape(-1) for k in REP_NAMES] + [loss.reshape(1)])
    flat = jnp.pad(flat, (0, NDEV * REP_ROWS * 128 - flat.shape[0])).reshape(NDEV, REP_ROWS, 128)
    dmod_blocks = jnp.pad(dmod.reshape(NDEV, MODC // 128, 128), ((0, 0), (0, MOD_ROWS - MODC // 128), (0, 0)))
    got, red = _small_all_reduce(jnp.concatenate([dmod_blocks, flat], axis=1))

    tail_parts = _sequencer_scatter("scatter_tail", 4, tail_grads,
                                    after=(ffn_parts[0][0, 0:16, 0:128], red[0, 0:8, :]))
    g_in_p, g_uq_p = _sum_partials("sum_tail_partials", tail_parts, 1)
    as_transpose = ("w_in", "w_gate", "w_up")
    grads = dict(w_in=_w_in_from_kernel(g_in_p).T, w_uq=_w_uq_from_kernel(g_uq_p))
    partials = dict(w_gate=ffn_parts[0], w_up=ffn_parts[1], w_down=ffn_parts[2], w_o=ffn_parts[3])
    dmod_rows = got[:, 0:MODC // 128, :].reshape(NDEV, MODC)
    grads["b_ada"] = red[:, 0:MODC // 128, :].reshape(1, N_MOD * D)
    rep_flat = red[:, MOD_ROWS:, :].reshape(-1)
    off = 0
    for k in REP_NAMES:
        size = int(np.prod(given[k].shape))
        grads[k] = rep_flat[off:off + size]
        off += size

    view = {k: (given[k].shape[1:] if given[k].ndim > 2 else given[k].shape)
            for k in REP_NAMES + ("b_ada", "w_ada", "w_in", "w_uq", "w_o", "w_gate", "w_up", "w_down")}
    view.update(g_final=(1, D))
    names = ["w_ada", "b_ada", "g_mix", "w_in", "g_q", "g_kv", "w_uq", "w_uk", "w_uv", "w_pool", "pool_scale",
             "w_o", "g_ffn", "w_gate", "w_up", "w_down", "g_final"]
    g_ada, d_ada, m_ada, v_ada = _adamw_ada(w_ada[0], m_w_ada[0], v_w_ada[0], c_all.T.astype(BF), dmod_rows)
    out_g, out_d, out_m, out_v = dict(w_ada=g_ada), dict(w_ada=d_ada), dict(w_ada=m_ada), dict(w_ada=v_ada)
    groups = (("adamw_ffn", ("w_gate", "w_up", "w_down", "w_o"), 4),
              ("adamw_replicated", REP_NAMES + ("b_ada",), 1),
              ("adamw_tail", ("w_in", "w_uq"), 1))
    for gname, members, nb in groups:
        turn = lambda k, t: t.T if k in as_transpose else t
        ws = [turn(k, given[k].reshape(view[k])) for k in members]
        ms = [turn(k, given["m_" + k].reshape(view[k])) for k in members]
        vs = [turn(k, given["v_" + k].reshape(view[k])) for k in members]
        if members[0] in partials:
            gs, ds, m2, v2 = _adamw_from_partials(gname, ws, [partials[k] for k in members], ms, vs, nb)
        else:
            gs = [grads[k] if k in as_transpose else grads[k].reshape(view[k]) for k in members]
            ds, m2, v2 = _adamw_group(gname, ws, gs, ms, vs, nb)
        for k, g, d, mm, vv in zip(members, gs, ds, m2, v2):
            out_g[k], out_d[k], out_m[k], out_v[k] = turn(k, g), turn(k, d), turn(k, mm), turn(k, vv)

    total = rep_flat[off]
    shaped = lambda d: [d[k].reshape(given[k].shape) for k in names]
    return (total, dx[None], *shaped(out_g), *shaped(out_d), *shaped(out_m), *shaped(out_v))
```

```python
import numpy as np
import jax
import jax.numpy as jnp
from jax import lax
from jax.experimental import pallas as pl
from jax.experimental.pallas import tpu as pltpu
from jax.experimental.pallas import tpu_sc as plsc

D = 1024
HEADS = 4
NOPE = 128
ROPE = 64
HALF = ROPE // 2
QL = 256
KVL = 128
FF = 2816
PW = 512
GROUPS = 4
GD = 128
N_MOD = 6
EPS = 1e-6
SM_SCALE = (NOPE + ROPE) ** -0.5
LOG2_E = 1.4426950408889634
EXP2_SCALE = SM_SCALE * LOG2_E
ROPE_THETA = 10000.0
NDEV = 8
MODC = N_MOD * D // NDEV

ADAM_LR = 0.001
ADAM_B1 = 0.9
ADAM_B2 = 0.999
ADAM_EPS = 1e-08
ADAM_WD = 0.01
ADAM_STEP = 10

BF = jnp.bfloat16
F32 = jnp.float32
VMEM_LIMIT_V7X = 60 * 1024 * 1024
MESH = pl.DeviceIdType.MESH

TQ = 512
TK = 512
QW = 256
VPU_ROWS = 16
MOD_ROWS = 8
REP_ROWS = 200
SMALL_ROWS = MOD_ROWS + REP_ROWS


def _params(sem=None):
    return pltpu.CompilerParams(dimension_semantics=sem, vmem_limit_bytes=VMEM_LIMIT_V7X)


def _dot(a, b):
    return jnp.dot(a, b, preferred_element_type=F32)


def _dot_nt(a, b):
    return lax.dot_general(a, b, (((1,), (1,)), ((), ())), preferred_element_type=F32)


def _dot_tn(a, b):
    return _dot(a.astype(F32).T.astype(BF), b)


def _full(shape):
    return pl.BlockSpec(shape, lambda *_: (0,) * len(shape))


def _rows(ts, cols):
    return pl.BlockSpec((ts, cols), lambda i: (i, 0))


def _vmem():
    return pl.BlockSpec(memory_space=pltpu.VMEM)


def _any():
    return pl.BlockSpec(memory_space=pl.ANY)


def _rms(v):
    return lax.rsqrt(jnp.mean(v * v, axis=-1, keepdims=True) + EPS)


def _rms_bwd(dn, n, r):
    return r * (dn - n * jnp.mean(dn * n, axis=-1, keepdims=True))


def _colsum(v):
    return jnp.sum(v, axis=0, keepdims=True)


def _swap_halves(v):
    lane = lax.broadcasted_iota(jnp.int32, v.shape, 1)
    return jnp.where(lane < HALF, pltpu.roll(v, 128 - HALF, 1), pltpu.roll(v, HALF, 1))


def _window_lane_width():
    lane = lax.broadcasted_iota(jnp.int32, (1, PW), 1)
    return jnp.where(lane < 128, 2.0, jnp.where(lane < 256, 4.0, jnp.where(lane < 384, 8.0, 16.0))).astype(F32)


def _window_sums(ext, back):
    n = ext.shape[0]

    def sh(v, k):
        return pltpu.roll(v, k if back else n - k, 0)

    s2 = ext + sh(ext, 1)
    e4 = s2[:, 128:]
    s4 = e4 + sh(e4, 2)
    e8 = s4[:, 128:]
    s8 = e8 + sh(e8, 4)
    e16 = s8[:, 128:]
    s16 = e16 + sh(e16, 8)
    return jnp.concatenate([s2[:, :128], s4[:, :128], s8[:, :128], s16], axis=1)


def _fill_block_diagonal(dst_ref, blocks_ref):
    n, r, c = blocks_ref.shape
    dst_ref[...] = jnp.zeros_like(dst_ref)
    for b in range(n):
        dst_ref[b * r:(b + 1) * r, b * c:(b + 1) * c] = blocks_ref[b]


def _row_counts(first_row, ts):
    t1 = (first_row + lax.broadcasted_iota(jnp.int32, (ts, 1), 0) + 1).astype(F32)
    return jnp.minimum(t1, _window_lane_width())


def _fwd_in(x, mod, g_mix, w_in, g_q, g_kv, w_uq, wuk_dc, perm, cos4, sin4, csk, snk, w_pool, pool_scale):
    S = x.shape[0]
    ts = 1024
    nsub = ts // TQ

    def body(x_ref, mod_ref, gmix_ref, win_ref, gq_ref, gkv_ref, wuq_ref, wuk_ref, perm_ref, cos_ref, sin_ref,
             csk_ref, snk_ref, wpool_ref, pscale_ref,
             h1_ref, raw_ref, qn_ref, qs_ref, kv_ref, kvt_ref, pooled_ref, ypre_ref, ypool_ref, carry_ref, wuk_bd,
             wpool_bd):
        i = pl.program_id(0)

        @pl.when(i == 0)
        def _():
            carry_ref[...] = jnp.zeros_like(carry_ref)
            _fill_block_diagonal(wuk_bd, wuk_ref)
            _fill_block_diagonal(wpool_bd, wpool_ref)

        xv = x_ref[...]
        sh1 = mod_ref[0:1, 0:D]
        sc1 = mod_ref[0:1, D:2 * D]
        h = (xv * _rms(xv)) * gmix_ref[...] * (1.0 + sc1) + sh1
        hb = h.astype(BF)
        h1_ref[...] = hb
        proj = _dot(hb, win_ref[...])
        cq_raw = proj[:, 0:QL]
        ckv_raw = proj[:, QL:QL + KVL]
        kr = proj[:, 384:512]
        u = proj[:, 512:1024]
        raw_ref[...] = proj[:, 0:384]

        c_q = (cq_raw * _rms(cq_raw)) * gq_ref[...]
        c_kv = (ckv_raw * _rms(ckv_raw)) * gkv_ref[...]
        q = _dot(c_q.astype(BF), wuq_ref[...])
        qn = q[:, 0:HEADS * NOPE].astype(BF)
        qn_ref[...] = qn
        x1 = q[:, 512:640]
        x2 = q[:, 640:768]
        cosv = cos_ref[...]
        sinv = sin_ref[...]
        roped = jnp.concatenate([x1 * cosv - x2 * sinv, x1 * sinv + x2 * cosv], axis=1).astype(BF)
        q_lat = _dot(qn, wuk_bd[...])
        q_rope = _dot(roped, perm_ref[...])
        for hd in range(HEADS):
            cols = slice(hd * 128, (hd + 1) * 128)
            qh = jnp.concatenate([q_lat[:, cols], q_rope[:, cols]], axis=1).astype(BF)
            for a in range(nsub):
                qs_ref[a, hd * TQ:(hd + 1) * TQ, :] = qh[a * TQ:(a + 1) * TQ, :]
        k_rope = kr * csk_ref[...] + _swap_halves(kr) * snk_ref[...]
        keys = jnp.concatenate([c_kv, k_rope], axis=1)
        kv_ref[...] = keys.astype(BF)
        for a in range(ts // TK):
            kvt_ref[a] = keys[a * TK:(a + 1) * TK, :].T.astype(BF)

        ext = jnp.concatenate([carry_ref[...], u], axis=0)
        win = _window_sums(ext, True)[16:, :]
        pooled = (win / _row_counts(i * ts, ts) - u).astype(BF)
        pooled_ref[...] = pooled
        carry_ref[...] = u[ts - 16:ts, :]
        ypre = _dot(pooled, wpool_bd[...])
        ypre_ref[...] = ypre.astype(BF)
        ypool_ref[...] = (ypre * pscale_ref[...]).astype(BF)

    out_shape = (
        jax.ShapeDtypeStruct((S, D), BF),
        jax.ShapeDtypeStruct((S, 384), F32),
        jax.ShapeDtypeStruct((S, HEADS * NOPE), BF),
        jax.ShapeDtypeStruct((S // TQ, HEADS * TQ, QW), BF),
        jax.ShapeDtypeStruct((S, QW), BF),
        jax.ShapeDtypeStruct((S // TK, QW, TK), BF),
        jax.ShapeDtypeStruct((S, PW), BF),
        jax.ShapeDtypeStruct((S, PW), BF),
        jax.ShapeDtypeStruct((S, PW), BF),
    )
    in_specs = [
        _rows(ts, D), _full(mod.shape), _full((1, D)), _full(w_in.shape), _full((1, QL)), _full((1, KVL)),
        _full(w_uq.shape), _full(wuk_dc.shape), _full(perm.shape), _rows(ts, 128), _rows(ts, 128), _rows(ts, 128),
        _rows(ts, 128), _full(w_pool.shape), _full((1, PW)),
    ]
    out_specs = (
        _rows(ts, D), _rows(ts, 384), _rows(ts, HEADS * NOPE),
        pl.BlockSpec((nsub, HEADS * TQ, QW), lambda i: (i, 0, 0)),
        _rows(ts, QW), pl.BlockSpec((ts // TK, QW, TK), lambda i: (i, 0, 0)), _rows(ts, PW), _rows(ts, PW),
        _rows(ts, PW),
    )
    return pl.pallas_call(
        body, name="fwd_in", out_shape=out_shape, grid=(S // ts,), in_specs=in_specs, out_specs=out_specs,
        scratch_shapes=[pltpu.VMEM((16, PW), F32), pltpu.VMEM((HEADS * NOPE, HEADS * KVL), BF),
                        pltpu.VMEM((PW, PW), BF)],
        compiler_params=_params(("arbitrary",)),
    )(x, mod, g_mix, w_in, g_q, g_kv, w_uq, wuk_dc, perm, cos4, sin4, csk, snk, w_pool, pool_scale)


def _diag_mask(shape, q_axis, first_chunk):
    qi = (lax.broadcasted_iota(jnp.int32, shape, q_axis) & (TQ - 1)) >> 6
    ki = (lax.broadcasted_iota(jnp.int32, shape, 1 - q_axis) >> 6) + first_chunk
    return ki <= qi


def _attn_fwd(qs, kv, kvt, wuv_vc):
    nq = qs.shape[0]
    S = kv.shape[0]
    M = HEADS * TQ

    def body(qs_ref, kv_ref, kvt_ref, wuv_ref, olat_ref, ymla_ref, lse_ref):
        i = pl.program_id(0)
        q = qs_ref[0]

        def step(kt, carry, first_chunk=None):
            m, l, acc = carry
            k = kv_ref[pl.ds(pl.multiple_of(kt * TK, TK), TK), :]
            v_t = kvt_ref[kt][0:KVL, :]
            s = _dot_nt(k, q)
            if first_chunk is not None:
                s = jnp.where(_diag_mask((TK, M), 1, first_chunk), s, -jnp.inf)
            m_new = jnp.maximum(m, jnp.max(s, axis=0, keepdims=True))
            alpha = jnp.exp2((m - m_new) * EXP2_SCALE)
            p = jnp.exp2((s - m_new) * EXP2_SCALE)
            l = alpha * l + jnp.sum(p, axis=0, keepdims=True)
            acc = alpha * acc + _dot(v_t, p.astype(BF))
            return m_new, l, acc

        init = (jnp.full((1, M), -jnp.inf, F32), jnp.zeros((1, M), F32), jnp.zeros((KVL, M), F32))
        per = TQ // TK
        carry = lax.fori_loop(0, per * i, step, init)
        for j in range(per):
            carry = step(per * i + j, carry, j * (TK // 64))
        m, l, acc = carry
        o_lat = acc / l
        olat_ref[0] = o_lat
        lse_ref[0] = jnp.broadcast_to(m * SM_SCALE + jnp.log(l), (8, M))
        for hd in range(HEADS):
            o_t = _dot(wuv_ref[hd], o_lat[:, hd * TQ:(hd + 1) * TQ].astype(BF))
            ymla_ref[:, hd * 128:(hd + 1) * 128] = o_t.T.astype(BF)

    out_shape = (
        jax.ShapeDtypeStruct((nq, KVL, M), F32),
        jax.ShapeDtypeStruct((S, HEADS * 128), BF),
        jax.ShapeDtypeStruct((nq, 8, M), F32),
    )
    return pl.pallas_call(
        body, name="attn_fwd", out_shape=out_shape, grid=(nq,),
        in_specs=[pl.BlockSpec((1, M, QW), lambda i: (i, 0, 0)), _full(kv.shape), _full(kvt.shape),
                  _full(wuv_vc.shape)],
        out_specs=(pl.BlockSpec((1, KVL, M), lambda i: (i, 0, 0)), _rows(TQ, HEADS * 128),
                   pl.BlockSpec((1, 8, M), lambda i: (i, 0, 0))),
        compiler_params=_params(("arbitrary",)),
    )(qs, kv, kvt, wuv_vc)


def _silu_parts(a):
    sg = jax.nn.sigmoid(a)
    return sg, a * sg


def _ffn_fwd_bwd(x, ymla, ypool, mod, w_o, g_ffn, wg_t, wu_t, wd, g_final, target):
    S = x.shape[0]
    ts = 256

    def body(x_ref, ymla_ref, ypool_ref, mod_ref, wo_ref, gffn_ref, wg_ref, wu_ref, wd_ref, gfin_ref, t_ref,
             x2_ref, mix_ref, h2t_ref, a_ref, b_ref, dx3_ref, da_ref, db_ref, dh2_ref, dfft_ref, loss_ref, dgfin_ref,
             dgt2_ref, f_ref):
        i = pl.program_id(0)

        @pl.when(i == 0)
        def _():
            loss_ref[...] = jnp.zeros_like(loss_ref)
            dgfin_ref[...] = jnp.zeros_like(dgfin_ref)
            dgt2_ref[...] = jnp.zeros_like(dgt2_ref)

        gt1 = mod_ref[0:1, 2 * D:3 * D]
        sh2 = mod_ref[0:1, 3 * D:4 * D]
        sc2 = mod_ref[0:1, 4 * D:5 * D]
        gt2 = mod_ref[0:1, 5 * D:6 * D]
        cat = jnp.concatenate([ymla_ref[...], ypool_ref[...]], axis=1)
        mix = _dot(cat, wo_ref[...])
        mix_ref[...] = mix.astype(BF)
        x2 = x_ref[...] + gt1 * mix
        x2_ref[...] = x2
        h2 = (x2 * _rms(x2)) * gffn_ref[...] * (1.0 + sc2) + sh2
        h2b = h2.astype(BF)
        h2t_ref[...] = h2.T.astype(BF)

        for c in range(FF // FCHUNK):
            cols = slice(c * FCHUNK, (c + 1) * FCHUNK)
            a = _dot_nt(h2b, wg_ref[cols, :])
            b = _dot_nt(h2b, wu_ref[cols, :])
            a_ref[:, cols] = a.astype(BF)
            b_ref[:, cols] = b.astype(BF)
            f_ref[:, cols] = (_silu_parts(a)[1] * b).astype(BF)
        ff = _dot(f_ref[...], wd_ref[...])

        x3 = x2 + gt2 * ff
        r3 = _rms(x3)
        xn3 = x3 * r3
        gfin = gfin_ref[...]
        e = xn3 * gfin - t_ref[...]
        loss_ref[...] += 0.5 * jnp.sum(jnp.mean(e * e, axis=-1, keepdims=True))
        dy = e * (1.0 / D)
        dgfin_ref[...] += _colsum(dy * xn3)
        dx3 = _rms_bwd(dy * gfin, xn3, r3)
        dx3_ref[...] = dx3
        dgt2_ref[...] += _colsum(dx3 * ff)
        dff = dx3 * gt2
        dffb = dff.astype(BF)
        dfft_ref[...] = dff.T.astype(BF)

        for c in range(FF // FCHUNK):
            cols = slice(c * FCHUNK, (c + 1) * FCHUNK)
            df = _dot_nt(dffb, wd_ref[cols, :])
            av = a_ref[:, cols].astype(F32)
            bv = b_ref[:, cols].astype(F32)
            sg, sa = _silu_parts(av)
            db_ref[:, cols] = (df * sa).astype(BF)
            da_ref[:, cols] = (df * bv * (sg * (1.0 + av * (1.0 - sg)))).astype(BF)
        dh2_ref[...] = _dot(da_ref[...], wg_ref[...]) + _dot(db_ref[...], wu_ref[...])

    row = lambda c: _rows(ts, c)
    col = pl.BlockSpec((D, ts), lambda i: (0, i))
    const = _full
    out_shape = (
        jax.ShapeDtypeStruct((S, D), F32),
        jax.ShapeDtypeStruct((S, D), BF),
        jax.ShapeDtypeStruct((D, S), BF),
        jax.ShapeDtypeStruct((S, FF), BF),
        jax.ShapeDtypeStruct((S, FF), BF),
        jax.ShapeDtypeStruct((S, D), F32),
        jax.ShapeDtypeStruct((S, FF), BF),
        jax.ShapeDtypeStruct((S, FF), BF),
        jax.ShapeDtypeStruct((S, D), F32),
        jax.ShapeDtypeStruct((D, S), BF),
        jax.ShapeDtypeStruct((8, 128), F32),
        jax.ShapeDtypeStruct((1, D), F32),
        jax.ShapeDtypeStruct((1, D), F32),
    )
    return pl.pallas_call(
        body, name="ffn_fwd_bwd", out_shape=out_shape, grid=(S // ts,),
        in_specs=[row(D), row(PW), row(PW), const(mod.shape), _vmem(), const((1, D)), _vmem(), _vmem(), _vmem(),
                  const((1, D)), row(D)],
        out_specs=(row(D), row(D), col, row(FF), row(FF), row(D), row(FF), row(FF), row(D), col, const((8, 128)),
                   const((1, D)), const((1, D))),
        scratch_shapes=[pltpu.VMEM((ts, FF), BF)],
        compiler_params=_params(("arbitrary",)),
    )(x, ymla, ypool, mod, w_o, g_ffn, wg_t, wu_t, wd, g_final, target)


FCHUNK = 256


def _ffn_bwd_weights(dff_t, h2_t, da, db, a, b):
    S = da.shape[0]

    def body(dfft_ref, h2t_ref, da_ref, db_ref, a_ref, b_ref, dwg_ref, dwu_ref, dwd_ref):
        h2t = h2t_ref[...]
        dwg_ref[...] = _dot(h2t, da_ref[...]).T.astype(BF)
        dwu_ref[...] = _dot(h2t, db_ref[...]).T.astype(BF)
        f = (_silu_parts(a_ref[...].astype(F32))[1] * b_ref[...].astype(F32)).astype(BF)
        dwd_ref[...] = _dot(dfft_ref[...], f).T.astype(BF)

    act = pl.BlockSpec((S, FCHUNK), lambda j: (0, j))
    wblk = _rows(FCHUNK, D)
    shp = jax.ShapeDtypeStruct((FF, D), BF)
    return pl.pallas_call(
        body, name="ffn_bwd_weights", out_shape=(shp, shp, shp), grid=(FF // FCHUNK,),
        in_specs=[_vmem(), _vmem(), act, act, act, act], out_specs=(wblk, wblk, wblk),
        compiler_params=_params(("arbitrary",)),
    )(dff_t, h2_t, da, db, a, b)


def _mix_bwd(dh2, dx3, x2, mix, mod, g_ffn, ymla, ypool, w_o, ypre, pooled, pool_scale, wpool_dc, olat, wuv_vc):
    S = dh2.shape[0]
    ts = 512
    n = S // ts
    nsub = ts // TQ
    M = HEADS * TQ

    def body(dh2_ref, dx3_ref, x2_ref, mix_ref, mod_ref, gffn_ref, ymla_ref, ypool_ref, wo_ref, ypre_ref, pooled_ref,
             pscale_ref, wpool_ref, olat_ref, wuv_ref,
             dx2_ref, du_ref, dolat_ref, delta_ref, dwo_ref, dwuv_ref, dwpool_ref, dpscale_ref, dgt1_ref, dsc2_ref,
             dsh2_ref, dgffn_ref, carry_ref, dwo_acc, dwpool_acc, wpool_bd, wuv_bd):
        i = pl.program_id(0)

        @pl.when(i == 0)
        def _():
            carry_ref[...] = jnp.zeros_like(carry_ref)
            dwo_acc[...] = jnp.zeros_like(dwo_acc)
            dwpool_acc[...] = jnp.zeros_like(dwpool_acc)
            _fill_block_diagonal(wpool_bd, wpool_ref)
            _fill_block_diagonal(wuv_bd, wuv_ref)
            for r in (dwuv_ref, dpscale_ref, dgt1_ref, dsc2_ref, dsh2_ref, dgffn_ref):
                r[...] = jnp.zeros_like(r)

        gt1 = mod_ref[0:1, 2 * D:3 * D]
        sc2 = mod_ref[0:1, 4 * D:5 * D]
        gffn = gffn_ref[...]
        dh2 = dh2_ref[...]
        x2 = x2_ref[...]
        r2 = _rms(x2)
        xn2 = x2 * r2
        along = _colsum(dh2 * xn2)
        dsc2_ref[...] += along * gffn
        dsh2_ref[...] += _colsum(dh2)
        dgffn_ref[...] += along * (1.0 + sc2)
        dx2 = dx3_ref[...] + _rms_bwd(dh2 * (gffn * (1.0 + sc2)), xn2, r2)
        dx2_ref[...] = dx2
        dgt1_ref[...] += _colsum(dx2 * mix_ref[...].astype(F32))
        dmix = (dx2 * gt1).astype(BF)
        cat = jnp.concatenate([ymla_ref[...], ypool_ref[...]], axis=1)
        dwo_acc[...] += _dot_tn(cat, dmix)
        dcat = _dot_nt(dmix, wo_ref[...])
        dymla = dcat[:, 0:512]
        dypool = dcat[:, 512:1024]

        dpscale_ref[...] += _colsum(dypool * ypre_ref[...].astype(F32))
        dypre = (dypool * pscale_ref[...]).astype(BF)
        dwpool_acc[...] += _dot_tn(pooled_ref[...], dypre)
        dpooled = _dot(dypre, wpool_bd[...])
        tile = n - 1 - i
        e = dpooled / _row_counts(tile * ts, ts)
        ext = jnp.concatenate([e, carry_ref[...]], axis=0)
        du_ref[...] = (_window_sums(ext, False)[0:ts, :] - dpooled).astype(BF)
        carry_ref[...] = e[0:16, :]

        dob_all = dymla.astype(BF)
        dol_all = _dot(dob_all, wuv_bd[...])
        for hd in range(HEADS):
            dob = dob_all[:, hd * 128:(hd + 1) * 128]
            dol = dol_all[:, hd * 128:(hd + 1) * 128]
            for a in range(nsub):
                ol_t = olat_ref[a, :, hd * TQ:(hd + 1) * TQ]
                dl = dol[a * TQ:(a + 1) * TQ, :]
                dolat_ref[a, hd * TQ:(hd + 1) * TQ, :] = dl.astype(BF)
                dwuv_ref[hd] += _dot(ol_t.astype(BF), dob[a * TQ:(a + 1) * TQ, :])
                delta = jnp.sum(dl * ol_t.T, axis=-1, keepdims=True)
                delta_ref[a, :, hd * TQ:(hd + 1) * TQ] = jnp.broadcast_to(delta, (TQ, 128)).T[0:8, :]

        @pl.when(i == n - 1)
        def _():
            dwo_ref[...] = dwo_acc[...].astype(BF)
            for g in range(GROUPS):
                dwpool_ref[g] = dwpool_acc[g * GD:(g + 1) * GD, g * GD:(g + 1) * GD]

    rev = lambda c: pl.BlockSpec((ts, c), lambda i: (n - 1 - i, 0))
    rev3 = lambda r, c: pl.BlockSpec((nsub, r, c), lambda i: (n - 1 - i, 0, 0))
    out_shape = (
        jax.ShapeDtypeStruct((S, D), F32),
        jax.ShapeDtypeStruct((S, PW), BF),
        jax.ShapeDtypeStruct((S // TQ, M, KVL), BF),
        jax.ShapeDtypeStruct((S // TQ, 8, M), F32),
        jax.ShapeDtypeStruct((D, D), BF),
        jax.ShapeDtypeStruct((HEADS, KVL, 128), F32),
        jax.ShapeDtypeStruct((GROUPS, GD, GD), F32),
        jax.ShapeDtypeStruct((1, PW), F32),
        jax.ShapeDtypeStruct((1, D), F32), jax.ShapeDtypeStruct((1, D), F32), jax.ShapeDtypeStruct((1, D), F32),
        jax.ShapeDtypeStruct((1, D), F32),
    )
    in_specs = [rev(D), rev(D), rev(D), rev(D), _full(mod.shape), _full((1, D)), rev(PW), rev(PW), _full(w_o.shape),
                rev(PW), rev(PW), _full((1, PW)), _full(wpool_dc.shape), rev3(KVL, M), _full(wuv_vc.shape)]
    out_specs = (rev(D), rev(PW), rev3(M, KVL), rev3(8, M), _full((D, D)), _full((HEADS, KVL, 128)),
                 _full((GROUPS, GD, GD)), _full((1, PW)), _full((1, D)), _full((1, D)), _full((1, D)), _full((1, D)))
    return pl.pallas_call(
        body, name="mix_bwd", out_shape=out_shape, grid=(n,), in_specs=in_specs, out_specs=out_specs,
        scratch_shapes=[pltpu.VMEM((16, PW), F32), pltpu.VMEM((D, D), F32), pltpu.VMEM((PW, PW), F32),
                        pltpu.VMEM((PW, PW), BF), pltpu.VMEM((HEADS * 128, HEADS * KVL), BF)],
        compiler_params=_params(("arbitrary",)),
    )(dh2, dx3, x2, mix, mod, g_ffn, ymla, ypool, w_o, ypre, pooled, pool_scale, wpool_dc, olat, wuv_vc)


def _attn_bwd(qs, kv, dolat, lse, delta):
    nq = qs.shape[0]
    S = kv.shape[0]
    M = HEADS * TQ
    nk = S // TK

    def body(qs_ref, kv_ref, do_ref, lse_ref, delta_ref, dkv_ref, dqt_out_ref, dqt_ref, p_ref, ds_ref):
        kt = pl.program_id(0)
        k = kv_ref[...]
        v = k[:, 0:KVL]
        k_t = k.astype(F32).T.astype(BF)

        @pl.when(kt == 0)
        def _():
            dqt_ref[...] = jnp.zeros_like(dqt_ref)

        def step(qi, carry, first_chunk=None):
            dk, dv = carry
            q = qs_ref[qi]
            do = do_ref[qi]
            s = _dot_nt(k, q)
            dp = _dot_nt(v, do)
            lse_row = lse_ref[qi, 0:1, :] * LOG2_E
            delta_row = delta_ref[qi, 0:1, :]
            q_chunk = (lax.broadcasted_iota(jnp.int32, (1, M), 1) & (TQ - 1)) >> 6
            for r in range(0, TK, VPU_ROWS):
                rows = slice(r, r + VPU_ROWS)
                p = jnp.exp2(s[rows, :] * EXP2_SCALE - lse_row)
                if first_chunk is not None:
                    p = jnp.where((r >> 6) + first_chunk <= q_chunk, p, 0.0)
                p_ref[rows, :] = p.astype(BF)
                ds_ref[rows, :] = (p * (dp[rows, :] - delta_row) * SM_SCALE).astype(BF)
            ds = ds_ref[...]
            dv = dv + _dot(p_ref[...], do)
            dk = dk + _dot(ds, q)
            dqt_ref[qi] += _dot(k_t, ds)
            return dk, dv

        per = TQ // TK
        first = kt // per
        carry = step(first, (jnp.zeros((TK, QW), F32), jnp.zeros((TK, KVL), F32)), (kt % per) * (TK // 64))
        dk, dv = lax.fori_loop(first + 1, nq, step, carry)
        dkv_ref[...] = dk + jnp.concatenate([dv, jnp.zeros((TK, QW - KVL), F32)], axis=1)
        dqt_out_ref[0] = dqt_ref[first].astype(BF)

    out_shape = (jax.ShapeDtypeStruct((S, QW), F32), jax.ShapeDtypeStruct((nq, QW, M), BF))
    return pl.pallas_call(
        body, name="attn_bwd", out_shape=out_shape, grid=(nk,),
        in_specs=[_vmem(), _rows(TK, QW), _vmem(), _vmem(), _vmem()],
        out_specs=(_rows(TK, QW), pl.BlockSpec((1, QW, M), lambda kt: (kt // (TQ // TK), 0, 0))),
        scratch_shapes=[pltpu.VMEM((nq, QW, M), F32), pltpu.VMEM((TK, M), BF), pltpu.VMEM((TK, M), BF)],
        compiler_params=_params(("arbitrary",)),
    )(qs, kv, dolat, lse, delta)


def _in_bwd(dqt, dkv, du, raw, qn, h1, x, dx2, mod, g_mix, w_in, g_q, g_kv, w_uq, wuk_cd, perm_t, cos4, sin4, csk,
            snk):
    S = x.shape[0]
    ts = 512
    n = S // ts
    nsub = ts // TQ
    M = HEADS * TQ

    def body(dqt_ref, dkv_ref, du_ref, raw_ref, qn_ref, h1_ref, x_ref, dx2_ref, mod_ref, gmix_ref, win_ref, gq_ref,
             gkv_ref, wuq_ref, wuk_ref, permt_ref, cos_ref, sin_ref, csk_ref, snk_ref,
             dx_ref, dwin_ref, dwuq_ref, dwuk_ref, dgq_ref, dgkv_ref, dsc1_ref, dsh1_ref, dgmix_ref, dwin_acc,
             dwuq_acc, dwuk_acc, wuk_bd):
        i = pl.program_id(0)

        @pl.when(i == 0)
        def _():
            dwin_acc[...] = jnp.zeros_like(dwin_acc)
            dwuq_acc[...] = jnp.zeros_like(dwuq_acc)
            dwuk_acc[...] = jnp.zeros_like(dwuk_acc)
            _fill_block_diagonal(wuk_bd, wuk_ref)
            for r in (dgq_ref, dgkv_ref, dsc1_ref, dsh1_ref, dgmix_ref):
                r[...] = jnp.zeros_like(r)

        dq_blocks = [dqt_ref[a].astype(F32).T for a in range(nsub)]
        dq_heads = [jnp.concatenate([blk[hd * TQ:(hd + 1) * TQ, :] for blk in dq_blocks], axis=0)
                    for hd in range(HEADS)]
        dq_lat = jnp.concatenate([dqh[:, 0:KVL] for dqh in dq_heads], axis=1).astype(BF)
        dq_rope = jnp.concatenate([dqh[:, KVL:QW] for dqh in dq_heads], axis=1).astype(BF)
        dq_nope = _dot(dq_lat, wuk_bd[...])
        dwuk_acc[...] += _dot_tn(dq_lat, qn_ref[...])
        drope = _dot(dq_rope, permt_ref[...])
        do1 = drope[:, 0:128]
        do2 = drope[:, 128:256]
        cosv = cos_ref[...]
        sinv = sin_ref[...]
        dq = jnp.concatenate([dq_nope, do1 * cosv + do2 * sinv, do2 * cosv - do1 * sinv], axis=1).astype(BF)

        cq_raw = raw_ref[:, 0:QL]
        ckv_raw = raw_ref[:, QL:QL + KVL]
        rq = _rms(cq_raw)
        nq_ = cq_raw * rq
        gq = gq_ref[...]
        dwuq_acc[...] += _dot_tn((nq_ * gq).astype(BF), dq)
        dc_q = _dot_nt(dq, wuq_ref[...])
        dgq_ref[...] += _colsum(dc_q * nq_)
        dcq_raw = _rms_bwd(dc_q * gq, nq_, rq)

        dkv = dkv_ref[...]
        rk = _rms(ckv_raw)
        nk_ = ckv_raw * rk
        dc_kv = dkv[:, 0:KVL]
        dgkv_ref[...] += _colsum(dc_kv * nk_)
        dckv_raw = _rms_bwd(dc_kv * gkv_ref[...], nk_, rk)
        dkr_roped = dkv[:, KVL:QW]
        dkr = dkr_roped * csk_ref[...] - _swap_halves(dkr_roped) * snk_ref[...]

        dproj = jnp.concatenate([dcq_raw.astype(BF), dckv_raw.astype(BF), dkr.astype(BF), du_ref[...]], axis=1)
        dwin_acc[...] += _dot_tn(h1_ref[...], dproj)
        dh1 = _dot_nt(dproj, win_ref[...])

        sc1 = mod_ref[0:1, D:2 * D]
        gmix = gmix_ref[...]
        xv = x_ref[...]
        r1 = _rms(xv)
        xn1 = xv * r1
        along = _colsum(dh1 * xn1)
        dsc1_ref[...] += along * gmix
        dsh1_ref[...] += _colsum(dh1)
        dgmix_ref[...] += along * (1.0 + sc1)
        dx_ref[...] = dx2_ref[...] + _rms_bwd(dh1 * (gmix * (1.0 + sc1)), xn1, r1)

        @pl.when(i == n - 1)
        def _():
            dwin_ref[...] = dwin_acc[...].astype(BF)
            dwuq_ref[...] = dwuq_acc[...].astype(BF)
            for hd in range(HEADS):
                dwuk_ref[hd] = dwuk_acc[hd * KVL:(hd + 1) * KVL, hd * NOPE:(hd + 1) * NOPE]

    out_shape = (
        jax.ShapeDtypeStruct((S, D), F32),
        jax.ShapeDtypeStruct((D, D), BF),
        jax.ShapeDtypeStruct((QL, 768), BF),
        jax.ShapeDtypeStruct((HEADS, KVL, NOPE), F32),
        jax.ShapeDtypeStruct((1, QL), F32), jax.ShapeDtypeStruct((1, KVL), F32),
        jax.ShapeDtypeStruct((1, D), F32), jax.ShapeDtypeStruct((1, D), F32), jax.ShapeDtypeStruct((1, D), F32),
    )
    in_specs = [pl.BlockSpec((nsub, QW, M), lambda i: (i, 0, 0)), _rows(ts, QW), _rows(ts, PW), _rows(ts, 384),
                _rows(ts, HEADS * NOPE), _rows(ts, D), _rows(ts, D), _rows(ts, D), _full(mod.shape), _full((1, D)),
                _full(w_in.shape), _full((1, QL)), _full((1, KVL)), _full(w_uq.shape), _full(wuk_cd.shape),
                _full(perm_t.shape), _rows(ts, 128), _rows(ts, 128), _rows(ts, 128), _rows(ts, 128)]
    out_specs = (_rows(ts, D), _full((D, D)), _full((QL, 768)), _full((HEADS, KVL, NOPE)), _full((1, QL)),
                 _full((1, KVL)), _full((1, D)), _full((1, D)), _full((1, D)))
    return pl.pallas_call(
        body, name="in_bwd", out_shape=out_shape, grid=(n,), in_specs=in_specs, out_specs=out_specs,
        scratch_shapes=[pltpu.VMEM((D, D), F32), pltpu.VMEM((QL, 768), F32),
                        pltpu.VMEM((HEADS * KVL, HEADS * NOPE), F32), pltpu.VMEM((HEADS * KVL, HEADS * NOPE), BF)],
        compiler_params=_params(("arbitrary",)),
    )(dqt, dkv, du, raw, qn, h1, x, dx2, mod, g_mix, w_in, g_q, g_kv, w_uq, wuk_cd, perm_t, cos4, sin4, csk, snk)


def _rope_perm():
    p = np.zeros((HEADS, 2 * 128, 128), np.float32)
    for hd in range(HEADS):
        for t in range(HALF):
            p[hd, hd * HALF + t, t] = 1.0
            p[hd, 128 + hd * HALF + t, HALF + t] = 1.0
    return p


def _rope_tables(positions):
    freqs = jnp.power(ROPE_THETA, -jnp.arange(HALF, dtype=F32) / HALF)
    ang = positions.astype(F32)[:, None] * jnp.tile(freqs, HEADS)[None, :]
    cos4 = jnp.cos(ang)
    sin4 = jnp.sin(ang)
    lane = jnp.arange(HEADS * HALF)[None, :]
    csk = jnp.where(lane < ROPE, cos4, 0.0)
    snk = jnp.where(lane < HALF, -sin4, jnp.where(lane < ROPE, sin4, 0.0))
    return cos4, sin4, csk, snk


def _local_step(x, rope, target, mod, g_mix, w_in_p, g_q, g_kv, w_uq_p, w_uk, w_uv, w_pool, pool_scale, g_ffn,
                g_final, late, ffn_grads_exchange):
    perm = jnp.asarray(_rope_perm().transpose(1, 0, 2).reshape(2 * 128, HEADS * 128), BF)
    perm_t = jnp.asarray(_rope_perm().transpose(0, 2, 1).reshape(HEADS * 128, 2 * 128), BF)
    cos4, sin4, csk, snk = rope
    wuk_dc = w_uk.transpose(1, 2, 0).astype(BF)
    wuk_cd = w_uk.transpose(1, 0, 2).astype(BF)
    wuv_vc = w_uv.transpose(1, 2, 0).astype(BF)
    wpool = w_pool.astype(BF)
    wpool_dc = w_pool.transpose(0, 2, 1).astype(BF)

    h1, raw, qn, qs, kv, kvt, pooled, ypre, ypool = _fwd_in(
        x, mod, g_mix, w_in_p, g_q, g_kv, w_uq_p, wuk_dc, perm, cos4, sin4, csk, snk, wpool, pool_scale)
    olat, ymla, lse = _attn_fwd(qs, kv, kvt, wuv_vc)
    w_o, wg_t, wu_t, wd = late
    x2, mix, h2_t, a, b, dx3, da, db, dh2, dff_t, loss, dgfin, dgt2 = _ffn_fwd_bwd(
        x, ymla, ypool, mod, w_o, g_ffn, wg_t, wu_t, wd, g_final, target)
    (dx2, du, dolat, delta, dwo, dwuv, dwpool, dpscale, dgt1, dsc2, dsh2, dgffn) = _mix_bwd(
        dh2, dx3, x2, mix, mod, g_ffn, ymla, ypool, w_o, ypre, pooled, pool_scale, wpool_dc, olat, wuv_vc)
    dwg_t, dwu_t, dwd = _ffn_bwd_weights(dff_t, h2_t, da, db, a, b)
    ffn_parts = ffn_grads_exchange((dwg_t, dwu_t, dwd, dwo))
    dkv, dqt = _attn_bwd(qs, kv, dolat, lse, delta)
    dx, dwin, dwuq, dwuk, dgq, dgkv, dsc1, dsh1, dgmix = _in_bwd(
        dqt, dkv, du, raw, qn, h1, x, dx2, mod, g_mix, w_in_p, g_q, g_kv, w_uq_p, wuk_cd, perm_t, cos4, sin4, csk,
        snk)
    dmod = jnp.concatenate([dsh1, dsc1, dgt1, dsh2, dsc2, dgt2], axis=1)
    replicated = dict(
        w_uk=dwuk.transpose(1, 0, 2), w_uv=dwuv.transpose(1, 0, 2), w_pool=dwpool, g_mix=dgmix, g_q=dgq, g_kv=dgkv,
        pool_scale=dpscale, g_ffn=dgffn, g_final=dgfin)
    return loss[0, 0], dx, dmod, (dwin, dwuq), ffn_parts, replicated


def _my_pos():
    return lax.axis_index("x"), lax.axis_index("y"), lax.axis_index("c")


def _peer(pos, k):
    x, y, c = pos
    return (1 - x if k & 4 else x, 1 - y if k & 2 else y, 1 - c if k & 1 else c)


def _index(pos):
    x, y, c = pos
    return 4 * x + 2 * y + c


def _remote(src, dst, send_sem, recv_sem, to):
    return pltpu.make_async_remote_copy(src_ref=src, dst_ref=dst, send_sem=send_sem, recv_sem=recv_sem,
                                        device_id=to, device_id_type=MESH)


def _ada_mod(c, w_ada, b_ada, after):
    def body(c_ref, w_ref, b_ref, after_ref, mod_ref, call_ref, cbuf, sbuf, rbuf, send1, recv1, send2, recv2):
        me = _my_pos()
        mi = _index(me)
        cv = c_ref[...]
        cbuf[...] = jnp.broadcast_to(cv * jax.nn.sigmoid(cv), (8, D))
        call_ref[mi] = cbuf[...]
        first = [_remote(cbuf, call_ref.at[mi], send1.at[k - 1], recv1.at[k - 1], _peer(me, k)) for k in range(1, NDEV)]
        for cp in first:
            cp.start()
        for k in range(1, NDEV):
            _remote(cbuf, call_ref.at[_index(_peer(me, k))], send1.at[k - 1], recv1.at[k - 1], _peer(me, k)).wait_recv()
        c_all = jnp.concatenate([call_ref[b][0:1, :] for b in range(NDEV)], axis=0)
        blocks = _dot(c_all.astype(BF), w_ref[...].astype(BF))
        for b in range(NDEV):
            sbuf[b] = jnp.broadcast_to(blocks[b:b + 1, :], (8, MODC))
        second = []
        for k in range(1, NDEV):
            to = _peer(me, k)
            second.append(_remote(sbuf.at[_index(to)], rbuf.at[mi], send2.at[k - 1], recv2.at[k - 1], to))
        for cp in second:
            cp.start()
        rbuf[mi] = sbuf[mi]
        for k in range(1, NDEV):
            to = _peer(me, k)
            _remote(sbuf.at[_index(to)], rbuf.at[_index(to)], send2.at[k - 1], recv2.at[k - 1], to).wait_recv()
        for j in range(NDEV):
            mod_ref[:, j * MODC:(j + 1) * MODC] = rbuf[j] + b_ref[:, j * MODC:(j + 1) * MODC]
        for cp in first + second:
            cp.wait_send()

    return pl.pallas_call(
        body, name="ada_mod",
        out_shape=(jax.ShapeDtypeStruct((8, N_MOD * D), F32), jax.ShapeDtypeStruct((NDEV, 8, D), F32)),
        in_specs=[_vmem(), _vmem(), _vmem(), _any()], out_specs=(_vmem(), _vmem()),
        scratch_shapes=[pltpu.VMEM((8, D), F32), pltpu.VMEM((NDEV, 8, MODC), F32), pltpu.VMEM((NDEV, 8, MODC), F32),
                        pltpu.SemaphoreType.DMA((NDEV - 1,)), pltpu.SemaphoreType.DMA((NDEV - 1,)),
                        pltpu.SemaphoreType.DMA((NDEV - 1,)), pltpu.SemaphoreType.DMA((NDEV - 1,))],
        compiler_params=_params(),
    )(c, w_ada, b_ada, after)


def _sequencer_scatter(name, collective_id, srcs, after=()):
    n = len(srcs)

    def of(src, to_index):
        r = src.shape[0] // NDEV
        return src.at[pl.ds(pl.multiple_of(to_index * r, 16), r), :]

    def body(*refs):
        src, zone = refs[:n], refs[n + len(after):2 * n + len(after)]
        send, recv, local = refs[2 * n + len(after):]
        me = _my_pos()
        mi = _index(me)
        barrier = pltpu.get_barrier_semaphore()
        for k in range(1, NDEV):
            pl.semaphore_signal(barrier, inc=1, device_id=_peer(me, k), device_id_type=MESH)
        pl.semaphore_wait(barrier, NDEV - 1)
        own = [pltpu.make_async_copy(of(src[a], mi), zone[a].at[mi], local.at[a]) for a in range(n)]
        for cp in own:
            cp.start()
        for a in range(n):
            for k in range(1, NDEV):
                to = _peer(me, k)
                s = a * (NDEV - 1) + k - 1
                _remote(of(src[a], _index(to)), zone[a].at[mi], send.at[s], recv.at[s], to).start()
        for cp in own:
            cp.wait()
        for a in range(n):
            for k in range(1, NDEV):
                to = _peer(me, k)
                s = a * (NDEV - 1) + k - 1
                cp = _remote(of(src[a], mi), zone[a].at[_index(to)], send.at[s], recv.at[s], to)
                cp.wait_send()
                cp.wait_recv()

    return pl.kernel(
        body, name=name, mesh=plsc.ScalarSubcoreMesh(axis_name="sequencer", num_cores=1),
        out_type=tuple(jax.ShapeDtypeStruct((NDEV, s.shape[0] // NDEV, s.shape[1]), s.dtype) for s in srcs),
        scratch_types=[pltpu.SemaphoreType.DMA((n * (NDEV - 1),)), pltpu.SemaphoreType.DMA((n * (NDEV - 1),)),
                       pltpu.SemaphoreType.DMA((n,))],
        compiler_params=pltpu.CompilerParams(collective_id=collective_id),
    )(*srcs, *after)


CHIP_PEERS = (2, 4, 6)


def _sequencer_gather(name, collective_id, srcs, after=()):
    n = len(srcs)
    per = NDEV - 1

    def body(*refs):
        src, zone = refs[:n], refs[n + len(after):2 * n + len(after)]
        send, recv, local = refs[2 * n + len(after):]
        me = _my_pos()
        mi = _index(me)
        sibling = _peer(me, 1)
        talk_to = (sibling,) + tuple(_peer(me, k) for k in CHIP_PEERS)
        barrier = pltpu.get_barrier_semaphore()
        for to in talk_to:
            pl.semaphore_signal(barrier, inc=1, device_id=to, device_id_type=MESH)
        pl.semaphore_wait(barrier, len(talk_to))

        def copy(a, slot, block_of, to, from_src=False):
            rows = zone[a].at[_index(block_of)]
            return _remote(src[a] if from_src else rows, rows, send.at[a * per + slot], recv.at[a * per + slot], to)

        own = [pltpu.make_async_copy(src[a], zone[a].at[mi], local.at[a]) for a in range(n)]
        for cp in own:
            cp.start()
        started = []
        for a in range(n):
            started.append(copy(a, 0, me, sibling, from_src=True))
            started += [copy(a, 1 + j, me, _peer(me, k), from_src=True) for j, k in enumerate(CHIP_PEERS)]
        for cp in started:
            cp.start()
        for a in range(n):
            for j, k in enumerate(CHIP_PEERS):
                copy(a, 1 + j, _peer(me, k), me).wait_recv()
                passed = copy(a, 4 + j, _peer(me, k), sibling)
                passed.start()
                started.append(passed)
        for a in range(n):
            copy(a, 0, sibling, me).wait_recv()
            for j, k in enumerate(CHIP_PEERS):
                copy(a, 4 + j, _peer(me, k | 1), me).wait_recv()
        for cp in started:
            cp.wait_send()
        for cp in own:
            cp.wait()

    return pl.kernel(
        body, name=name, mesh=plsc.ScalarSubcoreMesh(axis_name="sequencer", num_cores=1),
        out_type=tuple(jax.ShapeDtypeStruct((NDEV,) + s.shape, s.dtype) for s in srcs),
        scratch_types=[pltpu.SemaphoreType.DMA((n * per,)), pltpu.SemaphoreType.DMA((n * per,)),
                       pltpu.SemaphoreType.DMA((n,))],
        compiler_params=pltpu.CompilerParams(collective_id=collective_id),
    )(*srcs, *after)


def _blocked(shape, nb, axis=0):
    block = tuple(s // nb if d == axis else s for d, s in enumerate(shape))
    return pl.BlockSpec(block, lambda i: tuple(i if d == axis else 0 for d in range(len(shape))))


def _sum_partials(name, parts, nb):
    n = len(parts)

    def body(*refs):
        for a in range(n):
            acc = refs[a][0].astype(F32)
            for p in range(1, NDEV):
                acc = acc + refs[a][p].astype(F32)
            refs[n + a][...] = acc

    return pl.pallas_call(
        body, name=name, grid=(nb,),
        out_shape=tuple(jax.ShapeDtypeStruct(p.shape[1:], F32) for p in parts),
        in_specs=[_blocked(p.shape, nb, 1) for p in parts],
        out_specs=tuple(_blocked(p.shape[1:], nb) for p in parts), compiler_params=_params(("arbitrary",)),
    )(*parts)


def _small_all_reduce(buf):
    def body(buf_ref, got_ref, red_ref, mine, send1, recv1, send2, recv2):
        me = _my_pos()
        mi = _index(me)
        first = []
        for k in range(1, NDEV):
            to = _peer(me, k)
            first.append(_remote(buf_ref.at[_index(to)], got_ref.at[mi], send1.at[k - 1], recv1.at[k - 1], to))
        for cp in first:
            cp.start()
        got_ref[mi] = buf_ref[mi]
        for k in range(1, NDEV):
            to = _peer(me, k)
            _remote(buf_ref.at[mi], got_ref.at[_index(to)], send1.at[k - 1], recv1.at[k - 1], to).wait_recv()
        acc = got_ref[0]
        for p in range(1, NDEV):
            acc = acc + got_ref[p]
        mine[...] = acc
        second = [_remote(mine, red_ref.at[mi], send2.at[k - 1], recv2.at[k - 1], _peer(me, k)) for k in range(1, NDEV)]
        for cp in second:
            cp.start()
        red_ref[mi] = acc
        for k in range(1, NDEV):
            to = _peer(me, k)
            _remote(mine, red_ref.at[_index(to)], send2.at[k - 1], recv2.at[k - 1], to).wait_recv()
        for cp in first + second:
            cp.wait_send()

    return pl.pallas_call(
        body, name="small_all_reduce",
        out_shape=(jax.ShapeDtypeStruct(buf.shape, F32), jax.ShapeDtypeStruct(buf.shape, F32)),
        in_specs=[_vmem()], out_specs=(_vmem(), _vmem()),
        scratch_shapes=[pltpu.VMEM(buf.shape[1:], F32),
                        pltpu.SemaphoreType.DMA((NDEV - 1,)), pltpu.SemaphoreType.DMA((NDEV - 1,)),
                        pltpu.SemaphoreType.DMA((NDEV - 1,)), pltpu.SemaphoreType.DMA((NDEV - 1,))],
        compiler_params=_params(),
    )(buf)


def _adamw_math(w, g, m, v):
    m = ADAM_B1 * m + (1.0 - ADAM_B1) * g
    v = ADAM_B2 * v + (1.0 - ADAM_B2) * jnp.square(g)
    m_hat = m / (1.0 - ADAM_B1 ** ADAM_STEP)
    v_hat = v / (1.0 - ADAM_B2 ** ADAM_STEP)
    delta = -ADAM_LR * (m_hat / (jnp.sqrt(v_hat) + ADAM_EPS) + ADAM_WD * w)
    return delta, m, v


def _adamw_group(name, ws, gs, ms, vs, nb):
    n = len(ws)

    def body(*refs):
        for a in range(n):
            w, g, m, v = (refs[q * n + a][...] for q in range(4))
            delta, m2, v2 = _adamw_math(w, g, m, v)
            refs[4 * n + a][...] = delta
            refs[5 * n + a][...] = m2
            refs[6 * n + a][...] = v2

    shapes = tuple(jax.ShapeDtypeStruct(w.shape, F32) for w in ws)
    specs = [_blocked(w.shape, nb) for w in ws]
    outs = pl.pallas_call(
        body, name=name, grid=(nb,), out_shape=shapes * 3, in_specs=specs * 4, out_specs=tuple(specs * 3),
        compiler_params=_params(("arbitrary",)),
    )(*ws, *gs, *ms, *vs)
    return outs[:n], outs[n:2 * n], outs[2 * n:]


def _adamw_from_partials(name, ws, parts, ms, vs, nb):
    n = len(ws)

    def body(*refs):
        for a in range(n):
            part = refs[n + a]
            g = part[0].astype(F32)
            for p in range(1, NDEV):
                g = g + part[p].astype(F32)
            delta, m2, v2 = _adamw_math(refs[a][...], g, refs[2 * n + a][...], refs[3 * n + a][...])
            refs[4 * n + a][...] = g
            refs[5 * n + a][...] = delta
            refs[6 * n + a][...] = m2
            refs[7 * n + a][...] = v2

    shapes = tuple(jax.ShapeDtypeStruct(w.shape, F32) for w in ws)
    specs = [_blocked(w.shape, nb) for w in ws]
    outs = pl.pallas_call(
        body, name=name, grid=(nb,), out_shape=shapes * 4,
        in_specs=specs + [_blocked(p.shape, nb, 1) for p in parts] + specs * 2, out_specs=tuple(specs * 4),
        compiler_params=_params(("arbitrary",)),
    )(*ws, *parts, *ms, *vs)
    return outs[:n], outs[n:2 * n], outs[2 * n:3 * n], outs[3 * n:]


def _adamw_ada(w, m, v, c_all_t, dmod_rows):
    nb = 4

    def body(w_ref, m_ref, v_ref, c_ref, dm_ref, g_ref, d_ref, m2_ref, v2_ref):
        g = _dot(c_ref[...], dm_ref[...].astype(BF))
        g_ref[...] = g
        delta, m2, v2 = _adamw_math(w_ref[...], g, m_ref[...], v_ref[...])
        d_ref[...] = delta
        m2_ref[...] = m2
        v2_ref[...] = v2

    shp = jax.ShapeDtypeStruct(w.shape, F32)
    spec = _blocked(w.shape, nb)
    return pl.pallas_call(
        body, name="adamw_ada", grid=(nb,), out_shape=(shp, shp, shp, shp),
        in_specs=[spec, spec, spec, _blocked(c_all_t.shape, nb), _full(dmod_rows.shape)],
        out_specs=(spec, spec, spec, spec), compiler_params=_params(("arbitrary",)),
    )(w, m, v, c_all_t, dmod_rows)


def _w_in_to_kernel(w):
    return jnp.concatenate([w[:, 0:448], jnp.zeros((w.shape[0], 64), w.dtype), w[:, 448:960]], axis=1)


def _w_in_from_kernel(w):
    return jnp.concatenate([w[:, 0:448], w[:, 512:1024]], axis=1)


def _w_uq_to_kernel(w):
    r = w.shape[0]
    return jnp.concatenate([w[:, :, 0:NOPE].reshape(r, HEADS * NOPE),
                            w[:, :, NOPE:NOPE + HALF].reshape(r, HEADS * HALF),
                            w[:, :, NOPE + HALF:].reshape(r, HEADS * HALF)], axis=1)


def _w_uq_from_kernel(w):
    r = w.shape[0]
    return jnp.concatenate([w[:, 0:512].reshape(r, HEADS, NOPE), w[:, 512:640].reshape(r, HEADS, HALF),
                            w[:, 640:768].reshape(r, HEADS, HALF)], axis=2)


REP_NAMES = ("w_uk", "w_uv", "w_pool", "g_mix", "g_q", "g_kv", "pool_scale", "g_ffn", "g_final")


def kernel(x, c, positions, w_ada, b_ada, g_mix, w_in, g_q, g_kv, w_uq, w_uk, w_uv, w_pool, pool_scale, w_o, g_ffn, w_gate, w_up, w_down, g_final, loss_target, m_w_ada, m_b_ada, m_g_mix, m_w_in, m_g_q, m_g_kv, m_w_uq, m_w_uk, m_w_uv, m_w_pool, m_pool_scale, m_w_o, m_g_ffn, m_w_gate, m_w_up, m_w_down, m_g_final, v_w_ada, v_b_ada, v_g_mix, v_w_in, v_g_q, v_g_kv, v_w_uq, v_w_uk, v_w_uv, v_w_pool, v_pool_scale, v_w_o, v_g_ffn, v_w_gate, v_w_up, v_w_down, v_g_final):
    given = dict(locals())

    merge = lambda g: g.reshape(NDEV * g.shape[1], g.shape[2])
    w_in_p, w_uq_p = (merge(g) for g in _sequencer_gather(
        "gather_in", 3, (_w_in_to_kernel(w_in[0]).astype(BF), _w_uq_to_kernel(w_uq[0]).astype(BF))))

    rope = _rope_tables(positions[0])
    mod, c_all8 = _ada_mod(c, w_ada[0], b_ada, rope[3][0:8, :])
    c_all = c_all8[:, 0, :]
    late = _sequencer_gather(
        "gather_late", 1, (w_o[0].astype(BF), w_gate[0].T.astype(BF), w_up[0].T.astype(BF), w_down[0].astype(BF)),
        after=(mod[:, 0:128], w_in_p[0:16, 0:128], w_uq_p[0:16, 0:128]))

    def ffn_grads_exchange(arrays):
        return _sequencer_scatter("scatter_ffn", 2, arrays)

    loss, dx, dmod, tail_grads, ffn_parts, replicated = _local_step(
        x[0], rope, loss_target[0], mod, g_mix, w_in_p, g_q, g_kv, w_uq_p, w_uk[0], w_uv[0], w_pool[0],
        pool_scale, g_ffn, g_final.reshape(1, D), tuple(merge(g) for g in late), ffn_grads_exchange)

    flat = jnp.concatenate([replicated[k].reshape(-1) for k in REP_NAMES] + [loss.reshape(1)])
    flat = jnp.pad(flat, (0, NDEV * REP_ROWS * 128 - flat.shape[0])).reshape(NDEV, REP_ROWS, 128)
    dmod_blocks = jnp.pad(dmod.reshape(NDEV, MODC // 128, 128), ((0, 0), (0, MOD_ROWS - MODC // 128), (0, 0)))
    got, red = _small_all_reduce(jnp.concatenate([dmod_blocks, flat], axis=1))

    tail_parts = _sequencer_scatter("scatter_tail", 4, tail_grads,
                                    after=(ffn_parts[0][0, 0:16, 0:128], red[0, 0:8, :]))
    g_in_p, g_uq_p = _sum_partials("sum_tail_partials", tail_parts, 1)
    as_transpose = ("w_in", "w_gate", "w_up")
    grads = dict(w_in=_w_in_from_kernel(g_in_p).T, w_uq=_w_uq_from_kernel(g_uq_p))
    partials = dict(w_gate=ffn_parts[0], w_up=ffn_parts[1], w_down=ffn_parts[2], w_o=ffn_parts[3])
    dmod_rows = got[:, 0:MODC // 128, :].reshape(NDEV, MODC)
    grads["b_ada"] = red[:, 0:MODC // 128, :].reshape(1, N_MOD * D)
    rep_flat = red[:, MOD_ROWS:, :].reshape(-1)
    off = 0
    for k in REP_NAMES:
        size = int(np.prod(given[k].shape))
        grads[k] = rep_flat[off:off + size]
        off += size

    view = {k: (given[k].shape[1:] if given[k].ndim > 2 else given[k].shape)
            for k in REP_NAMES + ("b_ada", "w_ada", "w_in", "w_uq", "w_o", "w_gate", "w_up", "w_down")}
    view.update(g_final=(1, D))
    names = ["w_ada", "b_ada", "g_mix", "w_in", "g_q", "g_kv", "w_uq", "w_uk", "w_uv", "w_pool", "pool_scale",
             "w_o", "g_ffn", "w_gate", "w_up", "w_down", "g_final"]
    g_ada, d_ada, m_ada, v_ada = _adamw_ada(w_ada[0], m_w_ada[0], v_w_ada[0], c_all.T.astype(BF), dmod_rows)
    out_g, out_d, out_m, out_v = dict(w_ada=g_ada), dict(w_ada=d_ada), dict(w_ada=m_ada), dict(w_ada=v_ada)
    groups = (("adamw_ffn", ("w_gate", "w_up", "w_down", "w_o"), 4),
              ("adamw_replicated", REP_NAMES + ("b_ada",), 1),
              ("adamw_tail", ("w_in", "w_uq"), 1))
    for gname, members, nb in groups:
        turn = lambda k, t: t.T if k in as_transpose else t
        ws = [turn(k, given[k].reshape(view[k])) for k in members]
        ms = [turn(k, given["m_" + k].reshape(view[k])) for k in members]
        vs = [turn(k, given["v_" + k].reshape(view[k])) for k in members]
        if members[0] in partials:
            gs, ds, m2, v2 = _adamw_from_partials(gname, ws, [partials[k] for k in members], ms, vs, nb)
        else:
            gs = [grads[k] if k in as_transpose else grads[k].reshape(view[k]) for k in members]
            ds, m2, v2 = _adamw_group(gname, ws, gs, ms, vs, nb)
        for k, g, d, mm, vv in zip(members, gs, ds, m2, v2):
            out_g[k], out_d[k], out_m[k], out_v[k] = turn(k, g), turn(k, d), turn(k, mm), turn(k, vv)

    total = rep_flat[off]
    shaped = lambda d: [d[k].reshape(given[k].shape) for k in names]
    return (total, dx[None], *shaped(out_g), *shaped(out_d), *shaped(out_m), *shaped(out_v))
```

```python
import numpy as np
import jax
import jax.numpy as jnp
from jax import lax
from jax.experimental import pallas as pl
from jax.experimental.pallas import tpu as pltpu
from jax.experimental.pallas import tpu_sc as plsc

D = 1024
HEADS = 4
NOPE = 128
ROPE = 64
HALF = ROPE // 2
QL = 256
KVL = 128
FF = 2816
PW = 512
GROUPS = 4
GD = 128
N_MOD = 6
EPS = 1e-6
SM_SCALE = (NOPE + ROPE) ** -0.5
LOG2_E = 1.4426950408889634
EXP2_SCALE = SM_SCALE * LOG2_E
ROPE_THETA = 10000.0
NDEV = 8
MODC = N_MOD * D // NDEV

ADAM_LR = 0.001
ADAM_B1 = 0.9
ADAM_B2 = 0.999
ADAM_EPS = 1e-08
ADAM_WD = 0.01
ADAM_STEP = 10

BF = jnp.bfloat16
F32 = jnp.float32
VMEM_LIMIT_V7X = 60 * 1024 * 1024
MESH = pl.DeviceIdType.MESH

TQ = 512
TK = 512
QW = 256
VPU_ROWS = 16
MOD_ROWS = 8
REP_ROWS = 200
SMALL_ROWS = MOD_ROWS + REP_ROWS


def _params(sem=None):
    return pltpu.CompilerParams(dimension_semantics=sem, vmem_limit_bytes=VMEM_LIMIT_V7X)


def _dot(a, b):
    return jnp.dot(a, b, preferred_element_type=F32)


def _dot_nt(a, b):
    return lax.dot_general(a, b, (((1,), (1,)), ((), ())), preferred_element_type=F32)


def _dot_tn(a, b):
    return _dot(a.astype(F32).T.astype(BF), b)


def _full(shape):
    return pl.BlockSpec(shape, lambda *_: (0,) * len(shape))


def _rows(ts, cols):
    return pl.BlockSpec((ts, cols), lambda i: (i, 0))


def _vmem():
    return pl.BlockSpec(memory_space=pltpu.VMEM)


def _any():
    return pl.BlockSpec(memory_space=pl.ANY)


def _rms(v):
    return lax.rsqrt(jnp.mean(v * v, axis=-1, keepdims=True) + EPS)


def _rms_bwd(dn, n, r):
    return r * (dn - n * jnp.mean(dn * n, axis=-1, keepdims=True))


def _colsum(v):
    return jnp.sum(v, axis=0, keepdims=True)


def _swap_halves(v):
    lane = lax.broadcasted_iota(jnp.int32, v.shape, 1)
    return jnp.where(lane < HALF, pltpu.roll(v, 128 - HALF, 1), pltpu.roll(v, HALF, 1))


def _window_lane_width():
    lane = lax.broadcasted_iota(jnp.int32, (1, PW), 1)
    return jnp.where(lane < 128, 2.0, jnp.where(lane < 256, 4.0, jnp.where(lane < 384, 8.0, 16.0))).astype(F32)


def _window_sums(ext, back):
    n = ext.shape[0]

    def sh(v, k):
        return pltpu.roll(v, k if back else n - k, 0)

    s2 = ext + sh(ext, 1)
    e4 = s2[:, 128:]
    s4 = e4 + sh(e4, 2)
    e8 = s4[:, 128:]
    s8 = e8 + sh(e8, 4)
    e16 = s8[:, 128:]
    s16 = e16 + sh(e16, 8)
    return jnp.concatenate([s2[:, :128], s4[:, :128], s8[:, :128], s16], axis=1)


def _fill_block_diagonal(dst_ref, blocks_ref):
    n, r, c = blocks_ref.shape
    dst_ref[...] = jnp.zeros_like(dst_ref)
    for b in range(n):
        dst_ref[b * r:(b + 1) * r, b * c:(b + 1) * c] = blocks_ref[b]


def _row_counts(first_row, ts):
    t1 = (first_row + lax.broadcasted_iota(jnp.int32, (ts, 1), 0) + 1).astype(F32)
    return jnp.minimum(t1, _window_lane_width())


def _fwd_in(x, mod, g_mix, w_in, g_q, g_kv, w_uq, wuk_dc, perm, cos4, sin4, csk, snk, w_pool, pool_scale):
    S = x.shape[0]
    ts = 1024
    nsub = ts // TQ

    def body(x_ref, mod_ref, gmix_ref, win_ref, gq_ref, gkv_ref, wuq_ref, wuk_ref, perm_ref, cos_ref, sin_ref,
             csk_ref, snk_ref, wpool_ref, pscale_ref,
             h1_ref, raw_ref, qn_ref, qs_ref, kv_ref, kvt_ref, pooled_ref, ypre_ref, ypool_ref, carry_ref, wuk_bd,
             wpool_bd):
        i = pl.program_id(0)

        @pl.when(i == 0)
        def _():
            carry_ref[...] = jnp.zeros_like(carry_ref)
            _fill_block_diagonal(wuk_bd, wuk_ref)
            _fill_block_diagonal(wpool_bd, wpool_ref)

        xv = x_ref[...]
        sh1 = mod_ref[0:1, 0:D]
        sc1 = mod_ref[0:1, D:2 * D]
        h = (xv * _rms(xv)) * gmix_ref[...] * (1.0 + sc1) + sh1
        hb = h.astype(BF)
        h1_ref[...] = hb
        proj = _dot(hb, win_ref[...])
        cq_raw = proj[:, 0:QL]
        ckv_raw = proj[:, QL:QL + KVL]
        kr = proj[:, 384:512]
        u = proj[:, 512:1024]
        raw_ref[...] = proj[:, 0:384]

        c_q = (cq_raw * _rms(cq_raw)) * gq_ref[...]
        c_kv = (ckv_raw * _rms(ckv_raw)) * gkv_ref[...]
        q = _dot(c_q.astype(BF), wuq_ref[...])
        qn = q[:, 0:HEADS * NOPE].astype(BF)
        qn_ref[...] = qn
        x1 = q[:, 512:640]
        x2 = q[:, 640:768]
        cosv = cos_ref[...]
        sinv = sin_ref[...]
        roped = jnp.concatenate([x1 * cosv - x2 * sinv, x1 * sinv + x2 * cosv], axis=1).astype(BF)
        q_lat = _dot(qn, wuk_bd[...])
        q_rope = _dot(roped, perm_ref[...])
        for hd in range(HEADS):
            cols = slice(hd * 128, (hd + 1) * 128)
            qh = jnp.concatenate([q_lat[:, cols], q_rope[:, cols]], axis=1).astype(BF)
            for a in range(nsub):
                qs_ref[a, hd * TQ:(hd + 1) * TQ, :] = qh[a * TQ:(a + 1) * TQ, :]
        k_rope = kr * csk_ref[...] + _swap_halves(kr) * snk_ref[...]
        keys = jnp.concatenate([c_kv, k_rope], axis=1)
        kv_ref[...] = keys.astype(BF)
        for a in range(ts // TK):
            kvt_ref[a] = keys[a * TK:(a + 1) * TK, :].T.astype(BF)

        ext = jnp.concatenate([carry_ref[...], u], axis=0)
        win = _window_sums(ext, True)[16:, :]
        pooled = (win / _row_counts(i * ts, ts) - u).astype(BF)
        pooled_ref[...] = pooled
        carry_ref[...] = u[ts - 16:ts, :]
        ypre = _dot(pooled, wpool_bd[...])
        ypre_ref[...] = ypre.astype(BF)
        ypool_ref[...] = (ypre * pscale_ref[...]).astype(BF)

    out_shape = (
        jax.ShapeDtypeStruct((S, D), BF),
        jax.ShapeDtypeStruct((S, 384), F32),
        jax.ShapeDtypeStruct((S, HEADS * NOPE), BF),
        jax.ShapeDtypeStruct((S // TQ, HEADS * TQ, QW), BF),
        jax.ShapeDtypeStruct((S, QW), BF),
        jax.ShapeDtypeStruct((S // TK, QW, TK), BF),
        jax.ShapeDtypeStruct((S, PW), BF),
        jax.ShapeDtypeStruct((S, PW), BF),
        jax.ShapeDtypeStruct((S, PW), BF),
    )
    in_specs = [
        _rows(ts, D), _full(mod.shape), _full((1, D)), _full(w_in.shape), _full((1, QL)), _full((1, KVL)),
        _full(w_uq.shape), _full(wuk_dc.shape), _full(perm.shape), _rows(ts, 128), _rows(ts, 128), _rows(ts, 128),
        _rows(ts, 128), _full(w_pool.shape), _full((1, PW)),
    ]
    out_specs = (
        _rows(ts, D), _rows(ts, 384), _rows(ts, HEADS * NOPE),
        pl.BlockSpec((nsub, HEADS * TQ, QW), lambda i: (i, 0, 0)),
        _rows(ts, QW), pl.BlockSpec((ts // TK, QW, TK), lambda i: (i, 0, 0)), _rows(ts, PW), _rows(ts, PW),
        _rows(ts, PW),
    )
    return pl.pallas_call(
        body, name="fwd_in", out_shape=out_shape, grid=(S // ts,), in_specs=in_specs, out_specs=out_specs,
        scratch_shapes=[pltpu.VMEM((16, PW), F32), pltpu.VMEM((HEADS * NOPE, HEADS * KVL), BF),
                        pltpu.VMEM((PW, PW), BF)],
        compiler_params=_params(("arbitrary",)),
    )(x, mod, g_mix, w_in, g_q, g_kv, w_uq, wuk_dc, perm, cos4, sin4, csk, snk, w_pool, pool_scale)


def _diag_mask(shape, q_axis, first_chunk):
    qi = (lax.broadcasted_iota(jnp.int32, shape, q_axis) & (TQ - 1)) >> 6
    ki = (lax.broadcasted_iota(jnp.int32, shape, 1 - q_axis) >> 6) + first_chunk
    return ki <= qi


def _attn_fwd(qs, kv, kvt, wuv_vc):
    nq = qs.shape[0]
    S = kv.shape[0]
    M = HEADS * TQ

    def body(qs_ref, kv_ref, kvt_ref, wuv_ref, olat_ref, ymla_ref, lse_ref):
        i = pl.program_id(0)
        q = qs_ref[0]

        def step(kt, carry, first_chunk=None):
            m, l, acc = carry
            k = kv_ref[pl.ds(pl.multiple_of(kt * TK, TK), TK), :]
            v_t = kvt_ref[kt][0:KVL, :]
            s = _dot_nt(k, q)
            if first_chunk is not None:
                s = jnp.where(_diag_mask((TK, M), 1, first_chunk), s, -jnp.inf)
            m_new = jnp.maximum(m, jnp.max(s, axis=0, keepdims=True))
            alpha = jnp.exp2((m - m_new) * EXP2_SCALE)
            p = jnp.exp2((s - m_new) * EXP2_SCALE)
            l = alpha * l + jnp.sum(p, axis=0, keepdims=True)
            acc = alpha * acc + _dot(v_t, p.astype(BF))
            return m_new, l, acc

        init = (jnp.full((1, M), -jnp.inf, F32), jnp.zeros((1, M), F32), jnp.zeros((KVL, M), F32))
        per = TQ // TK
        carry = lax.fori_loop(0, per * i, step, init)
        for j in range(per):
            carry = step(per * i + j, carry, j * (TK // 64))
        m, l, acc = carry
        o_lat = acc / l
        olat_ref[0] = o_lat
        lse_ref[0] = jnp.broadcast_to(m * SM_SCALE + jnp.log(l), (8, M))
        for hd in range(HEADS):
            o_t = _dot(wuv_ref[hd], o_lat[:, hd * TQ:(hd + 1) * TQ].astype(BF))
            ymla_ref[:, hd * 128:(hd + 1) * 128] = o_t.T.astype(BF)

    out_shape = (
        jax.ShapeDtypeStruct((nq, KVL, M), F32),
        jax.ShapeDtypeStruct((S, HEADS * 128), BF),
        jax.ShapeDtypeStruct((nq, 8, M), F32),
    )
    return pl.pallas_call(
        body, name="attn_fwd", out_shape=out_shape, grid=(nq,),
        in_specs=[pl.BlockSpec((1, M, QW), lambda i: (i, 0, 0)), _full(kv.shape), _full(kvt.shape),
                  _full(wuv_vc.shape)],
        out_specs=(pl.BlockSpec((1, KVL, M), lambda i: (i, 0, 0)), _rows(TQ, HEADS * 128),
                   pl.BlockSpec((1, 8, M), lambda i: (i, 0, 0))),
        compiler_params=_params(("arbitrary",)),
    )(qs, kv, kvt, wuv_vc)


def _silu_parts(a):
    sg = jax.nn.sigmoid(a)
    return sg, a * sg


def _ffn_fwd_bwd(x, ymla, ypool, mod, w_o, g_ffn, wg_t, wu_t, wd, g_final, target):
    S = x.shape[0]
    ts = 256

    def body(x_ref, ymla_ref, ypool_ref, mod_ref, wo_ref, gffn_ref, wg_ref, wu_ref, wd_ref, gfin_ref, t_ref,
             h2t_ref, a_ref, b_ref, da_ref, db_ref, dfft_ref, dx2_ref, dmix_ref, catt_ref, dymla_ref, dypool_ref,
             loss_ref, dgfin_ref, dgt2_ref, dgt1_ref, dsc2_ref, dsh2_ref, dgffn_ref, f_ref):
        i = pl.program_id(0)

        @pl.when(i == 0)
        def _():
            for r in (loss_ref, dgfin_ref, dgt2_ref, dgt1_ref, dsc2_ref, dsh2_ref, dgffn_ref):
                r[...] = jnp.zeros_like(r)

        gt1 = mod_ref[0:1, 2 * D:3 * D]
        sh2 = mod_ref[0:1, 3 * D:4 * D]
        sc2 = mod_ref[0:1, 4 * D:5 * D]
        gt2 = mod_ref[0:1, 5 * D:6 * D]
        gffn = gffn_ref[...]
        cat = jnp.concatenate([ymla_ref[...], ypool_ref[...]], axis=1)
        catt_ref[...] = cat.astype(F32).T.astype(BF)
        mix = _dot(cat, wo_ref[...])
        x2 = x_ref[...] + gt1 * mix
        r2 = _rms(x2)
        xn2 = x2 * r2
        h2 = xn2 * gffn * (1.0 + sc2) + sh2
        h2b = h2.astype(BF)
        h2t_ref[...] = h2.T.astype(BF)

        for c in range(FF // FCHUNK):
            cols = slice(c * FCHUNK, (c + 1) * FCHUNK)
            a = _dot_nt(h2b, wg_ref[cols, :])
            b = _dot_nt(h2b, wu_ref[cols, :])
            a_ref[:, cols] = a.astype(BF)
            b_ref[:, cols] = b.astype(BF)
            f_ref[:, cols] = (_silu_parts(a)[1] * b).astype(BF)
        ff = _dot(f_ref[...], wd_ref[...])

        x3 = x2 + gt2 * ff
        r3 = _rms(x3)
        xn3 = x3 * r3
        gfin = gfin_ref[...]
        e = xn3 * gfin - t_ref[...]
        loss_ref[...] += 0.5 * jnp.sum(jnp.mean(e * e, axis=-1, keepdims=True))
        dy = e * (1.0 / D)
        dgfin_ref[...] += _colsum(dy * xn3)
        dx3 = _rms_bwd(dy * gfin, xn3, r3)
        dgt2_ref[...] += _colsum(dx3 * ff)
        dff = dx3 * gt2
        dffb = dff.astype(BF)
        dfft_ref[...] = dff.T.astype(BF)

        for c in range(FF // FCHUNK):
            cols = slice(c * FCHUNK, (c + 1) * FCHUNK)
            df = _dot_nt(dffb, wd_ref[cols, :])
            av = a_ref[:, cols].astype(F32)
            bv = b_ref[:, cols].astype(F32)
            sg, sa = _silu_parts(av)
            db_ref[:, cols] = (df * sa).astype(BF)
            da_ref[:, cols] = (df * bv * (sg * (1.0 + av * (1.0 - sg)))).astype(BF)
        dh2 = _dot(da_ref[...], wg_ref[...]) + _dot(db_ref[...], wu_ref[...])

        along = _colsum(dh2 * xn2)
        dsc2_ref[...] += along * gffn
        dsh2_ref[...] += _colsum(dh2)
        dgffn_ref[...] += along * (1.0 + sc2)
        dx2 = dx3 + _rms_bwd(dh2 * (gffn * (1.0 + sc2)), xn2, r2)
        dx2_ref[...] = dx2
        dgt1_ref[...] += _colsum(dx2 * mix)
        dmix = (dx2 * gt1).astype(BF)
        dmix_ref[...] = dmix
        dcat = _dot_nt(dmix, wo_ref[...])
        dymla_ref[...] = dcat[:, 0:PW].astype(BF)
        dypool_ref[...] = dcat[:, PW:2 * PW]

    row = lambda c: _rows(ts, c)
    col = pl.BlockSpec((D, ts), lambda i: (0, i))
    const = _full
    vec = jax.ShapeDtypeStruct((1, D), F32)
    out_shape = (
        jax.ShapeDtypeStruct((D, S), BF),
        jax.ShapeDtypeStruct((S, FF), BF),
        jax.ShapeDtypeStruct((S, FF), BF),
        jax.ShapeDtypeStruct((S, FF), BF),
        jax.ShapeDtypeStruct((S, FF), BF),
        jax.ShapeDtypeStruct((D, S), BF),
        jax.ShapeDtypeStruct((S, D), F32),
        jax.ShapeDtypeStruct((S, D), BF),
        jax.ShapeDtypeStruct((D, S), BF),
        jax.ShapeDtypeStruct((S, PW), BF),
        jax.ShapeDtypeStruct((S, PW), F32),
        jax.ShapeDtypeStruct((8, 128), F32),
        vec, vec, vec, vec, vec, vec,
    )
    return pl.pallas_call(
        body, name="ffn_fwd_bwd", out_shape=out_shape, grid=(S // ts,),
        in_specs=[row(D), row(PW), row(PW), const(mod.shape), _vmem(), const((1, D)), _vmem(), _vmem(), _vmem(),
                  const((1, D)), row(D)],
        out_specs=(col, row(FF), row(FF), row(FF), row(FF), col, row(D), row(D), col, row(PW), row(PW),
                   const((8, 128))) + (const((1, D)),) * 6,
        scratch_shapes=[pltpu.VMEM((ts, FF), BF)],
        compiler_params=_params(("arbitrary",)),
    )(x, ymla, ypool, mod, w_o, g_ffn, wg_t, wu_t, wd, g_final, target)


FCHUNK = 256


def _ffn_bwd_weights(dff_t, h2_t, da, db, a, b):
    S = da.shape[0]

    def body(dfft_ref, h2t_ref, da_ref, db_ref, a_ref, b_ref, dwg_ref, dwu_ref, dwd_ref):
        h2t = h2t_ref[...]
        dwg_ref[...] = _dot(h2t, da_ref[...]).T.astype(BF)
        dwu_ref[...] = _dot(h2t, db_ref[...]).T.astype(BF)
        f = (_silu_parts(a_ref[...].astype(F32))[1] * b_ref[...].astype(F32)).astype(BF)
        dwd_ref[...] = _dot(dfft_ref[...], f).T.astype(BF)

    act = pl.BlockSpec((S, FCHUNK), lambda j: (0, j))
    wblk = _rows(FCHUNK, D)
    shp = jax.ShapeDtypeStruct((FF, D), BF)
    return pl.pallas_call(
        body, name="ffn_bwd_weights", out_shape=(shp, shp, shp), grid=(FF // FCHUNK,),
        in_specs=[_vmem(), _vmem(), act, act, act, act], out_specs=(wblk, wblk, wblk),
        compiler_params=_params(("arbitrary",)),
    )(dff_t, h2_t, da, db, a, b)


def _wo_grad(cat_t, dmix):
    S = dmix.shape[0]
    tc = 256

    def body(catt_ref, dmix_ref, dwo_ref):
        dwo_ref[...] = _dot(catt_ref[...], dmix_ref[...]).astype(BF)

    return pl.pallas_call(
        body, name="wo_grad", out_shape=jax.ShapeDtypeStruct((D, D), BF), grid=(D // tc,),
        in_specs=[_vmem(), pl.BlockSpec((S, tc), lambda j: (0, j))], out_specs=pl.BlockSpec((D, tc), lambda j: (0, j)),
        compiler_params=_params(("arbitrary",)),
    )(cat_t, dmix)


def _mix_bwd(dymla, dypool, ypre, pooled, pool_scale, wpool_dc, olat, wuv_vc):
    S = dymla.shape[0]
    ts = 512
    n = S // ts
    nsub = ts // TQ
    M = HEADS * TQ

    def body(dymla_ref, dypool_ref, ypre_ref, pooled_ref, pscale_ref, wpool_ref, olat_ref, wuv_ref,
             du_ref, dolat_ref, delta_ref, dwuv_ref, dwpool_ref, dpscale_ref, carry_ref, dwpool_acc, wpool_bd,
             wuv_bd):
        i = pl.program_id(0)

        @pl.when(i == 0)
        def _():
            carry_ref[...] = jnp.zeros_like(carry_ref)
            dwpool_acc[...] = jnp.zeros_like(dwpool_acc)
            _fill_block_diagonal(wpool_bd, wpool_ref)
            _fill_block_diagonal(wuv_bd, wuv_ref)
            for r in (dwuv_ref, dpscale_ref):
                r[...] = jnp.zeros_like(r)

        dypool = dypool_ref[...]
        dpscale_ref[...] += _colsum(dypool * ypre_ref[...].astype(F32))
        dypre = (dypool * pscale_ref[...]).astype(BF)
        dwpool_acc[...] += _dot_tn(pooled_ref[...], dypre)
        dpooled = _dot(dypre, wpool_bd[...])
        tile = n - 1 - i
        e = dpooled / _row_counts(tile * ts, ts)
        ext = jnp.concatenate([e, carry_ref[...]], axis=0)
        du_ref[...] = (_window_sums(ext, False)[0:ts, :] - dpooled).astype(BF)
        carry_ref[...] = e[0:16, :]

        dob_all = dymla_ref[...]
        dol_all = _dot(dob_all, wuv_bd[...])
        for hd in range(HEADS):
            dob = dob_all[:, hd * 128:(hd + 1) * 128]
            dol = dol_all[:, hd * 128:(hd + 1) * 128]
            for a in range(nsub):
                ol_t = olat_ref[a, :, hd * TQ:(hd + 1) * TQ]
                dl = dol[a * TQ:(a + 1) * TQ, :]
                dolat_ref[a, hd * TQ:(hd + 1) * TQ, :] = dl.astype(BF)
                dwuv_ref[hd] += _dot(ol_t.astype(BF), dob[a * TQ:(a + 1) * TQ, :])
                delta = jnp.sum(dl * ol_t.T, axis=-1, keepdims=True)
                delta_ref[a, :, hd * TQ:(hd + 1) * TQ] = jnp.broadcast_to(delta, (TQ, 128)).T[0:8, :]

        @pl.when(i == n - 1)
        def _():
            for g in range(GROUPS):
                dwpool_ref[g] = dwpool_acc[g * GD:(g + 1) * GD, g * GD:(g + 1) * GD]

    rev = lambda c: pl.BlockSpec((ts, c), lambda i: (n - 1 - i, 0))
    rev3 = lambda r, c: pl.BlockSpec((nsub, r, c), lambda i: (n - 1 - i, 0, 0))
    out_shape = (
        jax.ShapeDtypeStruct((S, PW), BF),
        jax.ShapeDtypeStruct((S // TQ, M, KVL), BF),
        jax.ShapeDtypeStruct((S // TQ, 8, M), F32),
        jax.ShapeDtypeStruct((HEADS, KVL, 128), F32),
        jax.ShapeDtypeStruct((GROUPS, GD, GD), F32),
        jax.ShapeDtypeStruct((1, PW), F32),
    )
    in_specs = [rev(PW), rev(PW), rev(PW), rev(PW), _full((1, PW)), _full(wpool_dc.shape), rev3(KVL, M),
                _full(wuv_vc.shape)]
    out_specs = (rev(PW), rev3(M, KVL), rev3(8, M), _full((HEADS, KVL, 128)), _full((GROUPS, GD, GD)),
                 _full((1, PW)))
    return pl.pallas_call(
        body, name="mix_bwd", out_shape=out_shape, grid=(n,), in_specs=in_specs, out_specs=out_specs,
        scratch_shapes=[pltpu.VMEM((16, PW), F32), pltpu.VMEM((PW, PW), F32), pltpu.VMEM((PW, PW), BF),
                        pltpu.VMEM((HEADS * 128, HEADS * KVL), BF)],
        compiler_params=_params(("arbitrary",)),
    )(dymla, dypool, ypre, pooled, pool_scale, wpool_dc, olat, wuv_vc)


def _attn_bwd(qs, kv, dolat, lse, delta):
    nq = qs.shape[0]
    S = kv.shape[0]
    M = HEADS * TQ
    nk = S // TK

    def body(qs_ref, kv_ref, do_ref, lse_ref, delta_ref, dkv_ref, dqt_out_ref, dqt_ref, p_ref, ds_ref):
        kt = pl.program_id(0)
        k = kv_ref[...]
        v = k[:, 0:KVL]
        k_t = k.astype(F32).T.astype(BF)

        @pl.when(kt == 0)
        def _():
            dqt_ref[...] = jnp.zeros_like(dqt_ref)

        def step(qi, carry, first_chunk=None):
            dk, dv = carry
            q = qs_ref[qi]
            do = do_ref[qi]
            s = _dot_nt(k, q)
            dp = _dot_nt(v, do)
            lse_row = lse_ref[qi, 0:1, :] * LOG2_E
            delta_row = delta_ref[qi, 0:1, :]
            q_chunk = (lax.broadcasted_iota(jnp.int32, (1, M), 1) & (TQ - 1)) >> 6
            for r in range(0, TK, VPU_ROWS):
                rows = slice(r, r + VPU_ROWS)
                p = jnp.exp2(s[rows, :] * EXP2_SCALE - lse_row)
                if first_chunk is not None:
                    p = jnp.where((r >> 6) + first_chunk <= q_chunk, p, 0.0)
                p_ref[rows, :] = p.astype(BF)
                ds_ref[rows, :] = (p * (dp[rows, :] - delta_row) * SM_SCALE).astype(BF)
            ds = ds_ref[...]
            dv = dv + _dot(p_ref[...], do)
            dk = dk + _dot(ds, q)
            dqt_ref[qi] += _dot(k_t, ds)
            return dk, dv

        per = TQ // TK
        first = kt // per
        carry = step(first, (jnp.zeros((TK, QW), F32), jnp.zeros((TK, KVL), F32)), (kt % per) * (TK // 64))
        dk, dv = lax.fori_loop(first + 1, nq, step, carry)
        dkv_ref[...] = dk + jnp.concatenate([dv, jnp.zeros((TK, QW - KVL), F32)], axis=1)
        dqt_out_ref[0] = dqt_ref[first].astype(BF)

    out_shape = (jax.ShapeDtypeStruct((S, QW), F32), jax.ShapeDtypeStruct((nq, QW, M), BF))
    return pl.pallas_call(
        body, name="attn_bwd", out_shape=out_shape, grid=(nk,),
        in_specs=[_vmem(), _rows(TK, QW), _vmem(), _vmem(), _vmem()],
        out_specs=(_rows(TK, QW), pl.BlockSpec((1, QW, M), lambda kt: (kt // (TQ // TK), 0, 0))),
        scratch_shapes=[pltpu.VMEM((nq, QW, M), F32), pltpu.VMEM((TK, M), BF), pltpu.VMEM((TK, M), BF)],
        compiler_params=_params(("arbitrary",)),
    )(qs, kv, dolat, lse, delta)


def _in_bwd(dqt, dkv, du, raw, qn, h1, x, dx2, mod, g_mix, w_in, g_q, g_kv, w_uq, wuk_cd, perm_t, cos4, sin4, csk,
            snk):
    S = x.shape[0]
    ts = 512
    n = S // ts
    nsub = ts // TQ
    M = HEADS * TQ

    def body(dqt_ref, dkv_ref, du_ref, raw_ref, qn_ref, h1_ref, x_ref, dx2_ref, mod_ref, gmix_ref, win_ref, gq_ref,
             gkv_ref, wuq_ref, wuk_ref, permt_ref, cos_ref, sin_ref, csk_ref, snk_ref,
             dx_ref, dwin_ref, dwuq_ref, dwuk_ref, dgq_ref, dgkv_ref, dsc1_ref, dsh1_ref, dgmix_ref, dwin_acc,
             dwuq_acc, dwuk_acc, wuk_bd):
        i = pl.program_id(0)

        @pl.when(i == 0)
        def _():
            dwin_acc[...] = jnp.zeros_like(dwin_acc)
            dwuq_acc[...] = jnp.zeros_like(dwuq_acc)
            dwuk_acc[...] = jnp.zeros_like(dwuk_acc)
            _fill_block_diagonal(wuk_bd, wuk_ref)
            for r in (dgq_ref, dgkv_ref, dsc1_ref, dsh1_ref, dgmix_ref):
                r[...] = jnp.zeros_like(r)

        dq_blocks = [dqt_ref[a].astype(F32).T for a in range(nsub)]
        dq_heads = [jnp.concatenate([blk[hd * TQ:(hd + 1) * TQ, :] for blk in dq_blocks], axis=0)
                    for hd in range(HEADS)]
        dq_lat = jnp.concatenate([dqh[:, 0:KVL] for dqh in dq_heads], axis=1).astype(BF)
        dq_rope = jnp.concatenate([dqh[:, KVL:QW] for dqh in dq_heads], axis=1).astype(BF)
        dq_nope = _dot(dq_lat, wuk_bd[...])
        dwuk_acc[...] += _dot_tn(dq_lat, qn_ref[...])
        drope = _dot(dq_rope, permt_ref[...])
        do1 = drope[:, 0:128]
        do2 = drope[:, 128:256]
        cosv = cos_ref[...]
        sinv = sin_ref[...]
        dq = jnp.concatenate([dq_nope, do1 * cosv + do2 * sinv, do2 * cosv - do1 * sinv], axis=1).astype(BF)

        cq_raw = raw_ref[:, 0:QL]
        ckv_raw = raw_ref[:, QL:QL + KVL]
        rq = _rms(cq_raw)
        nq_ = cq_raw * rq
        gq = gq_ref[...]
        dwuq_acc[...] += _dot_tn((nq_ * gq).astype(BF), dq)
        dc_q = _dot_nt(dq, wuq_ref[...])
        dgq_ref[...] += _colsum(dc_q * nq_)
        dcq_raw = _rms_bwd(dc_q * gq, nq_, rq)

        dkv = dkv_ref[...]
        rk = _rms(ckv_raw)
        nk_ = ckv_raw * rk
        dc_kv = dkv[:, 0:KVL]
        dgkv_ref[...] += _colsum(dc_kv * nk_)
        dckv_raw = _rms_bwd(dc_kv * gkv_ref[...], nk_, rk)
        dkr_roped = dkv[:, KVL:QW]
        dkr = dkr_roped * csk_ref[...] - _swap_halves(dkr_roped) * snk_ref[...]

        dproj = jnp.concatenate([dcq_raw.astype(BF), dckv_raw.astype(BF), dkr.astype(BF), du_ref[...]], axis=1)
        dwin_acc[...] += _dot_tn(h1_ref[...], dproj)
        dh1 = _dot_nt(dproj, win_ref[...])

        sc1 = mod_ref[0:1, D:2 * D]
        gmix = gmix_ref[...]
        xv = x_ref[...]
        r1 = _rms(xv)
        xn1 = xv * r1
        along = _colsum(dh1 * xn1)
        dsc1_ref[...] += along * gmix
        dsh1_ref[...] += _colsum(dh1)
        dgmix_ref[...] += along * (1.0 + sc1)
        dx_ref[...] = dx2_ref[...] + _rms_bwd(dh1 * (gmix * (1.0 + sc1)), xn1, r1)

        @pl.when(i == n - 1)
        def _():
            dwin_ref[...] = dwin_acc[...].astype(BF)
            dwuq_ref[...] = dwuq_acc[...].astype(BF)
            for hd in range(HEADS):
                dwuk_ref[hd] = dwuk_acc[hd * KVL:(hd + 1) * KVL, hd * NOPE:(hd + 1) * NOPE]

    out_shape = (
        jax.ShapeDtypeStruct((S, D), F32),
        jax.ShapeDtypeStruct((D, D), BF),
        jax.ShapeDtypeStruct((QL, 768), BF),
        jax.ShapeDtypeStruct((HEADS, KVL, NOPE), F32),
        jax.ShapeDtypeStruct((1, QL), F32), jax.ShapeDtypeStruct((1, KVL), F32),
        jax.ShapeDtypeStruct((1, D), F32), jax.ShapeDtypeStruct((1, D), F32), jax.ShapeDtypeStruct((1, D), F32),
    )
    in_specs = [pl.BlockSpec((nsub, QW, M), lambda i: (i, 0, 0)), _rows(ts, QW), _rows(ts, PW), _rows(ts, 384),
                _rows(ts, HEADS * NOPE), _rows(ts, D), _rows(ts, D), _rows(ts, D), _full(mod.shape), _full((1, D)),
                _full(w_in.shape), _full((1, QL)), _full((1, KVL)), _full(w_uq.shape), _full(wuk_cd.shape),
                _full(perm_t.shape), _rows(ts, 128), _rows(ts, 128), _rows(ts, 128), _rows(ts, 128)]
    out_specs = (_rows(ts, D), _full((D, D)), _full((QL, 768)), _full((HEADS, KVL, NOPE)), _full((1, QL)),
                 _full((1, KVL)), _full((1, D)), _full((1, D)), _full((1, D)))
    return pl.pallas_call(
        body, name="in_bwd", out_shape=out_shape, grid=(n,), in_specs=in_specs, out_specs=out_specs,
        scratch_shapes=[pltpu.VMEM((D, D), F32), pltpu.VMEM((QL, 768), F32),
                        pltpu.VMEM((HEADS * KVL, HEADS * NOPE), F32), pltpu.VMEM((HEADS * KVL, HEADS * NOPE), BF)],
        compiler_params=_params(("arbitrary",)),
    )(dqt, dkv, du, raw, qn, h1, x, dx2, mod, g_mix, w_in, g_q, g_kv, w_uq, wuk_cd, perm_t, cos4, sin4, csk, snk)


def _rope_perm():
    p = np.zeros((HEADS, 2 * 128, 128), np.float32)
    for hd in range(HEADS):
        for t in range(HALF):
            p[hd, hd * HALF + t, t] = 1.0
            p[hd, 128 + hd * HALF + t, HALF + t] = 1.0
    return p


def _rope_tables(positions):
    freqs = jnp.power(ROPE_THETA, -jnp.arange(HALF, dtype=F32) / HALF)
    ang = positions.astype(F32)[:, None] * jnp.tile(freqs, HEADS)[None, :]
    cos4 = jnp.cos(ang)
    sin4 = jnp.sin(ang)
    lane = jnp.arange(HEADS * HALF)[None, :]
    csk = jnp.where(lane < ROPE, cos4, 0.0)
    snk = jnp.where(lane < HALF, -sin4, jnp.where(lane < ROPE, sin4, 0.0))
    return cos4, sin4, csk, snk


def _local_step(x, rope, target, mod, g_mix, w_in_p, g_q, g_kv, w_uq_p, w_uk, w_uv, w_pool, pool_scale, g_ffn,
                g_final, late, ffn_grads_exchange):
    perm = jnp.asarray(_rope_perm().transpose(1, 0, 2).reshape(2 * 128, HEADS * 128), BF)
    perm_t = jnp.asarray(_rope_perm().transpose(0, 2, 1).reshape(HEADS * 128, 2 * 128), BF)
    cos4, sin4, csk, snk = rope
    wuk_dc = w_uk.transpose(1, 2, 0).astype(BF)
    wuk_cd = w_uk.transpose(1, 0, 2).astype(BF)
    wuv_vc = w_uv.transpose(1, 2, 0).astype(BF)
    wpool = w_pool.astype(BF)
    wpool_dc = w_pool.transpose(0, 2, 1).astype(BF)

    h1, raw, qn, qs, kv, kvt, pooled, ypre, ypool = _fwd_in(
        x, mod, g_mix, w_in_p, g_q, g_kv, w_uq_p, wuk_dc, perm, cos4, sin4, csk, snk, wpool, pool_scale)
    olat, ymla, lse = _attn_fwd(qs, kv, kvt, wuv_vc)
    w_o, wg_t, wu_t, wd = late
    (h2_t, a, b, da, db, dff_t, dx2, dmix, cat_t, dymla, dypool, loss, dgfin, dgt2, dgt1, dsc2, dsh2,
     dgffn) = _ffn_fwd_bwd(x, ymla, ypool, mod, w_o, g_ffn, wg_t, wu_t, wd, g_final, target)
    dwo = _wo_grad(cat_t, dmix)
    dwg_t, dwu_t, dwd = _ffn_bwd_weights(dff_t, h2_t, da, db, a, b)
    ffn_parts = ffn_grads_exchange((dwg_t, dwu_t, dwd, dwo))
    du, dolat, delta, dwuv, dwpool, dpscale = _mix_bwd(
        dymla, dypool, ypre, pooled, pool_scale, wpool_dc, olat, wuv_vc)
    dkv, dqt = _attn_bwd(qs, kv, dolat, lse, delta)
    dx, dwin, dwuq, dwuk, dgq, dgkv, dsc1, dsh1, dgmix = _in_bwd(
        dqt, dkv, du, raw, qn, h1, x, dx2, mod, g_mix, w_in_p, g_q, g_kv, w_uq_p, wuk_cd, perm_t, cos4, sin4, csk,
        snk)
    dmod = jnp.concatenate([dsh1, dsc1, dgt1, dsh2, dsc2, dgt2], axis=1)
    replicated = dict(
        w_uk=dwuk.transpose(1, 0, 2), w_uv=dwuv.transpose(1, 0, 2), w_pool=dwpool, g_mix=dgmix, g_q=dgq, g_kv=dgkv,
        pool_scale=dpscale, g_ffn=dgffn, g_final=dgfin)
    return loss[0, 0], dx, dmod, (dwin, dwuq), ffn_parts, replicated


def _my_pos():
    return lax.axis_index("x"), lax.axis_index("y"), lax.axis_index("c")


def _peer(pos, k):
    x, y, c = pos
    return (1 - x if k & 4 else x, 1 - y if k & 2 else y, 1 - c if k & 1 else c)


def _index(pos):
    x, y, c = pos
    return 4 * x + 2 * y + c


def _remote(src, dst, send_sem, recv_sem, to):
    return pltpu.make_async_remote_copy(src_ref=src, dst_ref=dst, send_sem=send_sem, recv_sem=recv_sem,
                                        device_id=to, device_id_type=MESH)


def _ada_mod(c, w_ada, b_ada, after):
    def body(c_ref, w_ref, b_ref, after_ref, mod_ref, call_ref, cbuf, sbuf, rbuf, send1, recv1, send2, recv2):
        me = _my_pos()
        mi = _index(me)
        cv = c_ref[...]
        cbuf[...] = jnp.broadcast_to(cv * jax.nn.sigmoid(cv), (8, D))
        call_ref[mi] = cbuf[...]
        first = [_remote(cbuf, call_ref.at[mi], send1.at[k - 1], recv1.at[k - 1], _peer(me, k)) for k in range(1, NDEV)]
        for cp in first:
            cp.start()
        for k in range(1, NDEV):
            _remote(cbuf, call_ref.at[_index(_peer(me, k))], send1.at[k - 1], recv1.at[k - 1], _peer(me, k)).wait_recv()
        c_all = jnp.concatenate([call_ref[b][0:1, :] for b in range(NDEV)], axis=0)
        blocks = _dot(c_all.astype(BF), w_ref[...].astype(BF))
        for b in range(NDEV):
            sbuf[b] = jnp.broadcast_to(blocks[b:b + 1, :], (8, MODC))
        second = []
        for k in range(1, NDEV):
            to = _peer(me, k)
            second.append(_remote(sbuf.at[_index(to)], rbuf.at[mi], send2.at[k - 1], recv2.at[k - 1], to))
        for cp in second:
            cp.start()
        rbuf[mi] = sbuf[mi]
        for k in range(1, NDEV):
            to = _peer(me, k)
            _remote(sbuf.at[_index(to)], rbuf.at[_index(to)], send2.at[k - 1], recv2.at[k - 1], to).wait_recv()
        for j in range(NDEV):
            mod_ref[:, j * MODC:(j + 1) * MODC] = rbuf[j] + b_ref[:, j * MODC:(j + 1) * MODC]
        for cp in first + second:
            cp.wait_send()

    return pl.pallas_call(
        body, name="ada_mod",
        out_shape=(jax.ShapeDtypeStruct((8, N_MOD * D), F32), jax.ShapeDtypeStruct((NDEV, 8, D), F32)),
        in_specs=[_vmem(), _vmem(), _vmem(), _any()], out_specs=(_vmem(), _vmem()),
        scratch_shapes=[pltpu.VMEM((8, D), F32), pltpu.VMEM((NDEV, 8, MODC), F32), pltpu.VMEM((NDEV, 8, MODC), F32),
                        pltpu.SemaphoreType.DMA((NDEV - 1,)), pltpu.SemaphoreType.DMA((NDEV - 1,)),
                        pltpu.SemaphoreType.DMA((NDEV - 1,)), pltpu.SemaphoreType.DMA((NDEV - 1,))],
        compiler_params=_params(),
    )(c, w_ada, b_ada, after)


def _sequencer_scatter(name, collective_id, srcs, after=()):
    n = len(srcs)

    def of(src, to_index):
        r = src.shape[0] // NDEV
        return src.at[pl.ds(pl.multiple_of(to_index * r, 16), r), :]

    def body(*refs):
        src, zone = refs[:n], refs[n + len(after):2 * n + len(after)]
        send, recv, local = refs[2 * n + len(after):]
        me = _my_pos()
        mi = _index(me)
        barrier = pltpu.get_barrier_semaphore()
        for k in range(1, NDEV):
            pl.semaphore_signal(barrier, inc=1, device_id=_peer(me, k), device_id_type=MESH)
        pl.semaphore_wait(barrier, NDEV - 1)
        own = [pltpu.make_async_copy(of(src[a], mi), zone[a].at[mi], local.at[a]) for a in range(n)]
        for cp in own:
            cp.start()
        for a in range(n):
            for k in range(1, NDEV):
                to = _peer(me, k)
                s = a * (NDEV - 1) + k - 1
                _remote(of(src[a], _index(to)), zone[a].at[mi], send.at[s], recv.at[s], to).start()
        for cp in own:
            cp.wait()
        for a in range(n):
            for k in range(1, NDEV):
                to = _peer(me, k)
                s = a * (NDEV - 1) + k - 1
                cp = _remote(of(src[a], mi), zone[a].at[_index(to)], send.at[s], recv.at[s], to)
                cp.wait_send()
                cp.wait_recv()

    return pl.kernel(
        body, name=name, mesh=plsc.ScalarSubcoreMesh(axis_name="sequencer", num_cores=1),
        out_type=tuple(jax.ShapeDtypeStruct((NDEV, s.shape[0] // NDEV, s.shape[1]), s.dtype) for s in srcs),
        scratch_types=[pltpu.SemaphoreType.DMA((n * (NDEV - 1),)), pltpu.SemaphoreType.DMA((n * (NDEV - 1),)),
                       pltpu.SemaphoreType.DMA((n,))],
        compiler_params=pltpu.CompilerParams(collective_id=collective_id),
    )(*srcs, *after)


CHIP_PEERS = (2, 4, 6)


def _sequencer_gather(name, collective_id, srcs, after=()):
    n = len(srcs)
    per = NDEV - 1

    def body(*refs):
        src, zone = refs[:n], refs[n + len(after):2 * n + len(after)]
        send, recv, local = refs[2 * n + len(after):]
        me = _my_pos()
        mi = _index(me)
        sibling = _peer(me, 1)
        talk_to = (sibling,) + tuple(_peer(me, k) for k in CHIP_PEERS)
        barrier = pltpu.get_barrier_semaphore()
        for to in talk_to:
            pl.semaphore_signal(barrier, inc=1, device_id=to, device_id_type=MESH)
        pl.semaphore_wait(barrier, len(talk_to))

        def copy(a, slot, block_of, to, from_src=False):
            rows = zone[a].at[_index(block_of)]
            return _remote(src[a] if from_src else rows, rows, send.at[a * per + slot], recv.at[a * per + slot], to)

        own = [pltpu.make_async_copy(src[a], zone[a].at[mi], local.at[a]) for a in range(n)]
        for cp in own:
            cp.start()
        started = []
        for a in range(n):
            started.append(copy(a, 0, me, sibling, from_src=True))
            started += [copy(a, 1 + j, me, _peer(me, k), from_src=True) for j, k in enumerate(CHIP_PEERS)]
        for cp in started:
            cp.start()
        for a in range(n):
            for j, k in enumerate(CHIP_PEERS):
                copy(a, 1 + j, _peer(me, k), me).wait_recv()
                passed = copy(a, 4 + j, _peer(me, k), sibling)
                passed.start()
                started.append(passed)
        for a in range(n):
            copy(a, 0, sibling, me).wait_recv()
            for j, k in enumerate(CHIP_PEERS):
                copy(a, 4 + j, _peer(me, k | 1), me).wait_recv()
        for cp in started:
            cp.wait_send()
        for cp in own:
            cp.wait()

    return pl.kernel(
        body, name=name, mesh=plsc.ScalarSubcoreMesh(axis_name="sequencer", num_cores=1),
        out_type=tuple(jax.ShapeDtypeStruct((NDEV,) + s.shape, s.dtype) for s in srcs),
        scratch_types=[pltpu.SemaphoreType.DMA((n * per,)), pltpu.SemaphoreType.DMA((n * per,)),
                       pltpu.SemaphoreType.DMA((n,))],
        compiler_params=pltpu.CompilerParams(collective_id=collective_id),
    )(*srcs, *after)


def _blocked(shape, nb, axis=0):
    block = tuple(s // nb if d == axis else s for d, s in enumerate(shape))
    return pl.BlockSpec(block, lambda i: tuple(i if d == axis else 0 for d in range(len(shape))))


def _sum_partials(name, parts, nb):
    n = len(parts)

    def body(*refs):
        for a in range(n):
            acc = refs[a][0].astype(F32)
            for p in range(1, NDEV):
                acc = acc + refs[a][p].astype(F32)
            refs[n + a][...] = acc

    return pl.pallas_call(
        body, name=name, grid=(nb,),
        out_shape=tuple(jax.ShapeDtypeStruct(p.shape[1:], F32) for p in parts),
        in_specs=[_blocked(p.shape, nb, 1) for p in parts],
        out_specs=tuple(_blocked(p.shape[1:], nb) for p in parts), compiler_params=_params(("arbitrary",)),
    )(*parts)


def _small_all_reduce(buf):
    def body(buf_ref, got_ref, red_ref, mine, send1, recv1, send2, recv2):
        me = _my_pos()
        mi = _index(me)
        first = []
        for k in range(1, NDEV):
            to = _peer(me, k)
            first.append(_remote(buf_ref.at[_index(to)], got_ref.at[mi], send1.at[k - 1], recv1.at[k - 1], to))
        for cp in first:
            cp.start()
        got_ref[mi] = buf_ref[mi]
        for k in range(1, NDEV):
            to = _peer(me, k)
            _remote(buf_ref.at[mi], got_ref.at[_index(to)], send1.at[k - 1], recv1.at[k - 1], to).wait_recv()
        acc = got_ref[0]
        for p in range(1, NDEV):
            acc = acc + got_ref[p]
        mine[...] = acc
        second = [_remote(mine, red_ref.at[mi], send2.at[k - 1], recv2.at[k - 1], _peer(me, k)) for k in range(1, NDEV)]
        for cp in second:
            cp.start()
        red_ref[mi] = acc
        for k in range(1, NDEV):
            to = _peer(me, k)
            _remote(mine, red_ref.at[_index(to)], send2.at[k - 1], recv2.at[k - 1], to).wait_recv()
        for cp in first + second:
            cp.wait_send()

    return pl.pallas_call(
        body, name="small_all_reduce",
        out_shape=(jax.ShapeDtypeStruct(buf.shape, F32), jax.ShapeDtypeStruct(buf.shape, F32)),
        in_specs=[_vmem()], out_specs=(_vmem(), _vmem()),
        scratch_shapes=[pltpu.VMEM(buf.shape[1:], F32),
                        pltpu.SemaphoreType.DMA((NDEV - 1,)), pltpu.SemaphoreType.DMA((NDEV - 1,)),
                        pltpu.SemaphoreType.DMA((NDEV - 1,)), pltpu.SemaphoreType.DMA((NDEV - 1,))],
        compiler_params=_params(),
    )(buf)


def _adamw_math(w, g, m, v):
    m = ADAM_B1 * m + (1.0 - ADAM_B1) * g
    v = ADAM_B2 * v + (1.0 - ADAM_B2) * jnp.square(g)
    m_hat = m / (1.0 - ADAM_B1 ** ADAM_STEP)
    v_hat = v / (1.0 - ADAM_B2 ** ADAM_STEP)
    delta = -ADAM_LR * (m_hat / (jnp.sqrt(v_hat) + ADAM_EPS) + ADAM_WD * w)
    return delta, m, v


def _adamw_group(name, ws, gs, ms, vs, nb):
    n = len(ws)

    def body(*refs):
        for a in range(n):
            w, g, m, v = (refs[q * n + a][...] for q in range(4))
            delta, m2, v2 = _adamw_math(w, g, m, v)
            refs[4 * n + a][...] = delta
            refs[5 * n + a][...] = m2
            refs[6 * n + a][...] = v2

    shapes = tuple(jax.ShapeDtypeStruct(w.shape, F32) for w in ws)
    specs = [_blocked(w.shape, nb) for w in ws]
    outs = pl.pallas_call(
        body, name=name, grid=(nb,), out_shape=shapes * 3, in_specs=specs * 4, out_specs=tuple(specs * 3),
        compiler_params=_params(("arbitrary",)),
    )(*ws, *gs, *ms, *vs)
    return outs[:n], outs[n:2 * n], outs[2 * n:]


def _adamw_from_partials(name, ws, parts, ms, vs, nb):
    n = len(ws)

    def body(*refs):
        for a in range(n):
            part = refs[n + a]
            g = part[0].astype(F32)
            for p in range(1, NDEV):
                g = g + part[p].astype(F32)
            delta, m2, v2 = _adamw_math(refs[a][...], g, refs[2 * n + a][...], refs[3 * n + a][...])
            refs[4 * n + a][...] = g
            refs[5 * n + a][...] = delta
            refs[6 * n + a][...] = m2
            refs[7 * n + a][...] = v2

    shapes = tuple(jax.ShapeDtypeStruct(w.shape, F32) for w in ws)
    specs = [_blocked(w.shape, nb) for w in ws]
    outs = pl.pallas_call(
        body, name=name, grid=(nb,), out_shape=shapes * 4,
        in_specs=specs + [_blocked(p.shape, nb, 1) for p in parts] + specs * 2, out_specs=tuple(specs * 4),
        compiler_params=_params(("arbitrary",)),
    )(*ws, *parts, *ms, *vs)
    return outs[:n], outs[n:2 * n], outs[2 * n:3 * n], outs[3 * n:]


def _adamw_ada(w, m, v, c_all_t, dmod_rows):
    nb = 4

    def body(w_ref, m_ref, v_ref, c_ref, dm_ref, g_ref, d_ref, m2_ref, v2_ref):
        g = _dot(c_ref[...], dm_ref[...].astype(BF))
        g_ref[...] = g
        delta, m2, v2 = _adamw_math(w_ref[...], g, m_ref[...], v_ref[...])
        d_ref[...] = delta
        m2_ref[...] = m2
        v2_ref[...] = v2

    shp = jax.ShapeDtypeStruct(w.shape, F32)
    spec = _blocked(w.shape, nb)
    return pl.pallas_call(
        body, name="adamw_ada", grid=(nb,), out_shape=(shp, shp, shp, shp),
        in_specs=[spec, spec, spec, _blocked(c_all_t.shape, nb), _full(dmod_rows.shape)],
        out_specs=(spec, spec, spec, spec), compiler_params=_params(("arbitrary",)),
    )(w, m, v, c_all_t, dmod_rows)


def _w_in_to_kernel(w):
    return jnp.concatenate([w[:, 0:448], jnp.zeros((w.shape[0], 64), w.dtype), w[:, 448:960]], axis=1)


def _w_in_from_kernel(w):
    return jnp.concatenate([w[:, 0:448], w[:, 512:1024]], axis=1)


def _w_uq_to_kernel(w):
    r = w.shape[0]
    return jnp.concatenate([w[:, :, 0:NOPE].reshape(r, HEADS * NOPE),
                            w[:, :, NOPE:NOPE + HALF].reshape(r, HEADS * HALF),
                            w[:, :, NOPE + HALF:].reshape(r, HEADS * HALF)], axis=1)


def _w_uq_from_kernel(w):
    r = w.shape[0]
    return jnp.concatenate([w[:, 0:512].reshape(r, HEADS, NOPE), w[:, 512:640].reshape(r, HEADS, HALF),
                            w[:, 640:768].reshape(r, HEADS, HALF)], axis=2)


REP_NAMES = ("w_uk", "w_uv", "w_pool", "g_mix", "g_q", "g_kv", "pool_scale", "g_ffn", "g_final")


def kernel(x, c, positions, w_ada, b_ada, g_mix, w_in, g_q, g_kv, w_uq, w_uk, w_uv, w_pool, pool_scale, w_o, g_ffn, w_gate, w_up, w_down, g_final, loss_target, m_w_ada, m_b_ada, m_g_mix, m_w_in, m_g_q, m_g_kv, m_w_uq, m_w_uk, m_w_uv, m_w_pool, m_pool_scale, m_w_o, m_g_ffn, m_w_gate, m_w_up, m_w_down, m_g_final, v_w_ada, v_b_ada, v_g_mix, v_w_in, v_g_q, v_g_kv, v_w_uq, v_w_uk, v_w_uv, v_w_pool, v_pool_scale, v_w_o, v_g_ffn, v_w_gate, v_w_up, v_w_down, v_g_final):
    given = dict(locals())

    merge = lambda g: g.reshape(NDEV * g.shape[1], g.shape[2])
    w_in_p, w_uq_p = (merge(g) for g in _sequencer_gather(
        "gather_in", 3, (_w_in_to_kernel(w_in[0]).astype(BF), _w_uq_to_kernel(w_uq[0]).astype(BF))))

    rope = _rope_tables(positions[0])
    mod, c_all8 = _ada_mod(c, w_ada[0], b_ada, rope[3][0:8, :])
    c_all = c_all8[:, 0, :]
    late = _sequencer_gather(
        "gather_late", 1, (w_o[0].astype(BF), w_gate[0].T.astype(BF), w_up[0].T.astype(BF), w_down[0].astype(BF)),
        after=(mod[:, 0:128], w_in_p[0:16, 0:128], w_uq_p[0:16, 0:128]))

    def ffn_grads_exchange(arrays):
        return _sequencer_scatter("scatter_ffn", 2, arrays)

    loss, dx, dmod, tail_grads, ffn_parts, replicated = _local_step(
        x[0], rope, loss_target[0], mod, g_mix, w_in_p, g_q, g_kv, w_uq_p, w_uk[0], w_uv[0], w_pool[0],
        pool_scale, g_ffn, g_final.reshape(1, D), tuple(merge(g) for g in late), ffn_grads_exchange)

    flat = jnp.concatenate([replicated[k].reshape(-1) for k in REP_NAMES] + [loss.reshape(1)])
    flat = jnp.pad(flat, (0, NDEV * REP_ROWS * 128 - flat.shape[0])).reshape(NDEV, REP_ROWS, 128)
    dmod_blocks = jnp.pad(dmod.reshape(NDEV, MODC // 128, 128), ((0, 0), (0, MOD_ROWS - MODC // 128), (0, 0)))
    got, red = _small_all_reduce(jnp.concatenate([dmod_blocks, flat], axis=1))

    tail_parts = _sequencer_scatter("scatter_tail", 4, tail_grads,
                                    after=(ffn_parts[0][0, 0:16, 0:128], red[0, 0:8, :]))
    g_in_p, g_uq_p = _sum_partials("sum_tail_partials", tail_parts, 1)
    as_transpose = ("w_in", "w_gate", "w_up")
    grads = dict(w_in=_w_in_from_kernel(g_in_p).T, w_uq=_w_uq_from_kernel(g_uq_p))
    partials = dict(w_gate=ffn_parts[0], w_up=ffn_parts[1], w_down=ffn_parts[2], w_o=ffn_parts[3])
    dmod_rows = got[:, 0:MODC // 128, :].reshape(NDEV, MODC)
    grads["b_ada"] = red[:, 0:MODC // 128, :].reshape(1, N_MOD * D)
    rep_flat = red[:, MOD_ROWS:, :].reshape(-1)
    off = 0
    for k in REP_NAMES:
        size = int(np.prod(given[k].shape))
        grads[k] = rep_flat[off:off + size]
        off += size

    view = {k: (given[k].shape[1:] if given[k].ndim > 2 else given[k].shape)
            for k in REP_NAMES + ("b_ada", "w_ada", "w_in", "w_uq", "w_o", "w_gate", "w_up", "w_down")}
    view.update(g_final=(1, D))
    names = ["w_ada", "b_ada", "g_mix", "w_in", "g_q", "g_kv", "w_uq", "w_uk", "w_uv", "w_pool", "pool_scale",
             "w_o", "g_ffn", "w_gate", "w_up", "w_down", "g_final"]
    g_ada, d_ada, m_ada, v_ada = _adamw_ada(w_ada[0], m_w_ada[0], v_w_ada[0], c_all.T.astype(BF), dmod_rows)
    out_g, out_d, out_m, out_v = dict(w_ada=g_ada), dict(w_ada=d_ada), dict(w_ada=m_ada), dict(w_ada=v_ada)
    groups = (("adamw_ffn", ("w_gate", "w_up", "w_down", "w_o"), 4),
              ("adamw_replicated", REP_NAMES + ("b_ada",), 1),
              ("adamw_tail", ("w_in", "w_uq"), 1))
    for gname, members, nb in groups:
        turn = lambda k, t: t.T if k in as_transpose else t
        ws = [turn(k, given[k].reshape(view[k])) for k in members]
        ms = [turn(k, given["m_" + k].reshape(view[k])) for k in members]
        vs = [turn(k, given["v_" + k].reshape(view[k])) for k in members]
        if members[0] in partials:
            gs, ds, m2, v2 = _adamw_from_partials(gname, ws, [partials[k] for k in members], ms, vs, nb)
        else:
            gs = [grads[k] if k in as_transpose else grads[k].reshape(view[k]) for k in members]
            ds, m2, v2 = _adamw_group(gname, ws, gs, ms, vs, nb)
        for k, g, d, mm, vv in zip(members, gs, ds, m2, v2):
            out_g[k], out_d[k], out_m[k], out_v[k] = turn(k, g), turn(k, d), turn(k, mm), turn(k, vv)

    total = rep_flat[off]
    shaped = lambda d: [d[k].reshape(given[k].shape) for k in names]
    return (total, dx[None], *shaped(out_g), *shaped(out_d), *shaped(out_m), *shaped(out_v))
```

```python
import numpy as np
import jax
import jax.numpy as jnp
from jax import lax
from jax.experimental import pallas as pl
from jax.experimental.pallas import tpu as pltpu
from jax.experimental.pallas import tpu_sc as plsc

D = 1024
HEADS = 4
NOPE = 128
ROPE = 64
HALF = ROPE // 2
QL = 256
KVL = 128
FF = 2816
PW = 512
GROUPS = 4
GD = 128
N_MOD = 6
EPS = 1e-6
SM_SCALE = (NOPE + ROPE) ** -0.5
LOG2_E = 1.4426950408889634
EXP2_SCALE = SM_SCALE * LOG2_E
ROPE_THETA = 10000.0
NDEV = 8
MODC = N_MOD * D // NDEV

ADAM_LR = 0.001
ADAM_B1 = 0.9
ADAM_B2 = 0.999
ADAM_EPS = 1e-08
ADAM_WD = 0.01
ADAM_STEP = 10

BF = jnp.bfloat16
F32 = jnp.float32
VMEM_LIMIT_V7X = 60 * 1024 * 1024
MESH = pl.DeviceIdType.MESH

TQ = 512
TK = 512
QW = 256
VPU_ROWS = 16
MOD_ROWS = 8
REP_ROWS = 200
SMALL_ROWS = MOD_ROWS + REP_ROWS


def _params(sem=None):
    return pltpu.CompilerParams(dimension_semantics=sem, vmem_limit_bytes=VMEM_LIMIT_V7X)


def _dot(a, b):
    return jnp.dot(a, b, preferred_element_type=F32)


def _dot_nt(a, b):
    return lax.dot_general(a, b, (((1,), (1,)), ((), ())), preferred_element_type=F32)


def _dot_tn(a, b):
    return _dot(a.astype(F32).T.astype(BF), b)


def _full(shape):
    return pl.BlockSpec(shape, lambda *_: (0,) * len(shape))


def _rows(ts, cols):
    return pl.BlockSpec((ts, cols), lambda i: (i, 0))


def _vmem():
    return pl.BlockSpec(memory_space=pltpu.VMEM)


def _any():
    return pl.BlockSpec(memory_space=pl.ANY)


def _rms(v):
    return lax.rsqrt(jnp.mean(v * v, axis=-1, keepdims=True) + EPS)


def _rms_bwd(dn, n, r):
    return r * (dn - n * jnp.mean(dn * n, axis=-1, keepdims=True))


def _colsum(v):
    return jnp.sum(v, axis=0, keepdims=True)


def _swap_halves(v):
    lane = lax.broadcasted_iota(jnp.int32, v.shape, 1)
    return jnp.where(lane < HALF, pltpu.roll(v, 128 - HALF, 1), pltpu.roll(v, HALF, 1))


def _window_lane_width():
    lane = lax.broadcasted_iota(jnp.int32, (1, PW), 1)
    return jnp.where(lane < 128, 2.0, jnp.where(lane < 256, 4.0, jnp.where(lane < 384, 8.0, 16.0))).astype(F32)


def _window_sums(ext, back):
    n = ext.shape[0]

    def sh(v, k):
        return pltpu.roll(v, k if back else n - k, 0)

    s2 = ext + sh(ext, 1)
    e4 = s2[:, 128:]
    s4 = e4 + sh(e4, 2)
    e8 = s4[:, 128:]
    s8 = e8 + sh(e8, 4)
    e16 = s8[:, 128:]
    s16 = e16 + sh(e16, 8)
    return jnp.concatenate([s2[:, :128], s4[:, :128], s8[:, :128], s16], axis=1)


def _fill_block_diagonal(dst_ref, blocks_ref):
    n, r, c = blocks_ref.shape
    dst_ref[...] = jnp.zeros_like(dst_ref)
    for b in range(n):
        dst_ref[b * r:(b + 1) * r, b * c:(b + 1) * c] = blocks_ref[b]


def _row_counts(first_row, ts):
    t1 = (first_row + lax.broadcasted_iota(jnp.int32, (ts, 1), 0) + 1).astype(F32)
    return jnp.minimum(t1, _window_lane_width())


def _fwd_in(x, mod, g_mix, w_in, g_q, g_kv, w_uq, wuk_dc, perm, cos4, sin4, csk, snk, w_pool, pool_scale):
    S = x.shape[0]
    ts = 1024
    nsub = ts // TQ

    def body(x_ref, mod_ref, gmix_ref, win_ref, gq_ref, gkv_ref, wuq_ref, wuk_ref, perm_ref, cos_ref, sin_ref,
             csk_ref, snk_ref, wpool_ref, pscale_ref,
             h1_ref, raw_ref, qn_ref, qs_ref, kv_ref, kvt_ref, pooled_ref, ypre_ref, ypool_ref, carry_ref, wuk_bd,
             wpool_bd):
        i = pl.program_id(0)

        @pl.when(i == 0)
        def _():
            carry_ref[...] = jnp.zeros_like(carry_ref)
            _fill_block_diagonal(wuk_bd, wuk_ref)
            _fill_block_diagonal(wpool_bd, wpool_ref)

        xv = x_ref[...]
        sh1 = mod_ref[0:1, 0:D]
        sc1 = mod_ref[0:1, D:2 * D]
        h = (xv * _rms(xv)) * gmix_ref[...] * (1.0 + sc1) + sh1
        hb = h.astype(BF)
        h1_ref[...] = hb
        proj = _dot(hb, win_ref[...])
        cq_raw = proj[:, 0:QL]
        ckv_raw = proj[:, QL:QL + KVL]
        kr = proj[:, 384:512]
        u = proj[:, 512:1024]
        raw_ref[...] = proj[:, 0:384]

        c_q = (cq_raw * _rms(cq_raw)) * gq_ref[...]
        c_kv = (ckv_raw * _rms(ckv_raw)) * gkv_ref[...]
        q = _dot(c_q.astype(BF), wuq_ref[...])
        qn = q[:, 0:HEADS * NOPE].astype(BF)
        qn_ref[...] = qn
        x1 = q[:, 512:640]
        x2 = q[:, 640:768]
        cosv = cos_ref[...]
        sinv = sin_ref[...]
        roped = jnp.concatenate([x1 * cosv - x2 * sinv, x1 * sinv + x2 * cosv], axis=1).astype(BF)
        q_lat = _dot(qn, wuk_bd[...])
        q_rope = _dot(roped, perm_ref[...])
        for hd in range(HEADS):
            cols = slice(hd * 128, (hd + 1) * 128)
            qh = jnp.concatenate([q_lat[:, cols], q_rope[:, cols]], axis=1).astype(BF)
            for a in range(nsub):
                qs_ref[a, hd * TQ:(hd + 1) * TQ, :] = qh[a * TQ:(a + 1) * TQ, :]
        k_rope = kr * csk_ref[...] + _swap_halves(kr) * snk_ref[...]
        keys = jnp.concatenate([c_kv, k_rope], axis=1)
        kv_ref[...] = keys.astype(BF)
        for a in range(ts // TK):
            kvt_ref[a] = keys[a * TK:(a + 1) * TK, :].T.astype(BF)

        ext = jnp.concatenate([carry_ref[...], u], axis=0)
        win = _window_sums(ext, True)[16:, :]
        pooled = (win / _row_counts(i * ts, ts) - u).astype(BF)
        pooled_ref[...] = pooled
        carry_ref[...] = u[ts - 16:ts, :]
        ypre = _dot(pooled, wpool_bd[...])
        ypre_ref[...] = ypre.astype(BF)
        ypool_ref[...] = (ypre * pscale_ref[...]).astype(BF)

    out_shape = (
        jax.ShapeDtypeStruct((S, D), BF),
        jax.ShapeDtypeStruct((S, 384), F32),
        jax.ShapeDtypeStruct((S, HEADS * NOPE), BF),
        jax.ShapeDtypeStruct((S // TQ, HEADS * TQ, QW), BF),
        jax.ShapeDtypeStruct((S, QW), BF),
        jax.ShapeDtypeStruct((S // TK, QW, TK), BF),
        jax.ShapeDtypeStruct((S, PW), BF),
        jax.ShapeDtypeStruct((S, PW), BF),
        jax.ShapeDtypeStruct((S, PW), BF),
    )
    in_specs = [
        _rows(ts, D), _full(mod.shape), _full((1, D)), _full(w_in.shape), _full((1, QL)), _full((1, KVL)),
        _full(w_uq.shape), _full(wuk_dc.shape), _full(perm.shape), _rows(ts, 128), _rows(ts, 128), _rows(ts, 128),
        _rows(ts, 128), _full(w_pool.shape), _full((1, PW)),
    ]
    out_specs = (
        _rows(ts, D), _rows(ts, 384), _rows(ts, HEADS * NOPE),
        pl.BlockSpec((nsub, HEADS * TQ, QW), lambda i: (i, 0, 0)),
        _rows(ts, QW), pl.BlockSpec((ts // TK, QW, TK), lambda i: (i, 0, 0)), _rows(ts, PW), _rows(ts, PW),
        _rows(ts, PW),
    )
    return pl.pallas_call(
        body, name="fwd_in", out_shape=out_shape, grid=(S // ts,), in_specs=in_specs, out_specs=out_specs,
        scratch_shapes=[pltpu.VMEM((16, PW), F32), pltpu.VMEM((HEADS * NOPE, HEADS * KVL), BF),
                        pltpu.VMEM((PW, PW), BF)],
        compiler_params=_params(("arbitrary",)),
    )(x, mod, g_mix, w_in, g_q, g_kv, w_uq, wuk_dc, perm, cos4, sin4, csk, snk, w_pool, pool_scale)


def _diag_mask(shape, q_axis, first_chunk):
    qi = (lax.broadcasted_iota(jnp.int32, shape, q_axis) & (TQ - 1)) >> 6
    ki = (lax.broadcasted_iota(jnp.int32, shape, 1 - q_axis) >> 6) + first_chunk
    return ki <= qi


def _attn_fwd(qs, kv, kvt, wuv_vc):
    nq = qs.shape[0]
    S = kv.shape[0]
    M = HEADS * TQ

    def body(qs_ref, kv_ref, kvt_ref, wuv_ref, olat_ref, ymla_ref, lse_ref):
        i = pl.program_id(0)
        q = qs_ref[0]

        def step(kt, carry, first_chunk=None):
            m, l, acc = carry
            k = kv_ref[pl.ds(pl.multiple_of(kt * TK, TK), TK), :]
            v_t = kvt_ref[kt][0:KVL, :]
            s = _dot_nt(k, q)
            if first_chunk is not None:
                s = jnp.where(_diag_mask((TK, M), 1, first_chunk), s, -jnp.inf)
            m_new = jnp.maximum(m, jnp.max(s, axis=0, keepdims=True))
            alpha = jnp.exp2((m - m_new) * EXP2_SCALE)
            p = jnp.exp2((s - m_new) * EXP2_SCALE)
            l = alpha * l + jnp.sum(p, axis=0, keepdims=True)
            acc = alpha * acc + _dot(v_t, p.astype(BF))
            return m_new, l, acc

        init = (jnp.full((1, M), -jnp.inf, F32), jnp.zeros((1, M), F32), jnp.zeros((KVL, M), F32))
        per = TQ // TK
        carry = lax.fori_loop(0, per * i, step, init)
        for j in range(per):
            carry = step(per * i + j, carry, j * (TK // 64))
        m, l, acc = carry
        o_lat = acc / l
        olat_ref[0] = o_lat
        lse_ref[0] = jnp.broadcast_to(m * SM_SCALE + jnp.log(l), (8, M))
        for hd in range(HEADS):
            o_t = _dot(wuv_ref[hd], o_lat[:, hd * TQ:(hd + 1) * TQ].astype(BF))
            ymla_ref[:, hd * 128:(hd + 1) * 128] = o_t.T.astype(BF)

    out_shape = (
        jax.ShapeDtypeStruct((nq, KVL, M), F32),
        jax.ShapeDtypeStruct((S, HEADS * 128), BF),
        jax.ShapeDtypeStruct((nq, 8, M), F32),
    )
    return pl.pallas_call(
        body, name="attn_fwd", out_shape=out_shape, grid=(nq,),
        in_specs=[pl.BlockSpec((1, M, QW), lambda i: (i, 0, 0)), _full(kv.shape), _full(kvt.shape),
                  _full(wuv_vc.shape)],
        out_specs=(pl.BlockSpec((1, KVL, M), lambda i: (i, 0, 0)), _rows(TQ, HEADS * 128),
                   pl.BlockSpec((1, 8, M), lambda i: (i, 0, 0))),
        compiler_params=_params(("arbitrary",)),
    )(qs, kv, kvt, wuv_vc)


def _silu_parts(a):
    sg = jax.nn.sigmoid(a)
    return sg, a * sg


def _ffn_fwd_bwd(x, ymla, ypool, mod, w_o, g_ffn, wg_t, wu_t, wd, g_final, target):
    S = x.shape[0]
    ts = 256

    def body(x_ref, ymla_ref, ypool_ref, mod_ref, wo_ref, gffn_ref, wg_ref, wu_ref, wd_ref, gfin_ref, t_ref,
             h2t_ref, f_ref, da_ref, db_ref, dfft_ref, dx2_ref, dmix_ref, catt_ref, dymla_ref, dypool_ref,
             loss_ref, dgfin_ref, dgt2_ref, dgt1_ref, dsc2_ref, dsh2_ref, dgffn_ref, a_ref, b_ref):
        i = pl.program_id(0)

        @pl.when(i == 0)
        def _():
            for r in (loss_ref, dgfin_ref, dgt2_ref, dgt1_ref, dsc2_ref, dsh2_ref, dgffn_ref):
                r[...] = jnp.zeros_like(r)

        gt1 = mod_ref[0:1, 2 * D:3 * D]
        sh2 = mod_ref[0:1, 3 * D:4 * D]
        sc2 = mod_ref[0:1, 4 * D:5 * D]
        gt2 = mod_ref[0:1, 5 * D:6 * D]
        gffn = gffn_ref[...]
        cat = jnp.concatenate([ymla_ref[...], ypool_ref[...]], axis=1)
        catt_ref[...] = cat.astype(F32).T.astype(BF)
        mix = _dot(cat, wo_ref[...])
        x2 = x_ref[...] + gt1 * mix
        r2 = _rms(x2)
        xn2 = x2 * r2
        h2 = xn2 * gffn * (1.0 + sc2) + sh2
        h2b = h2.astype(BF)
        h2t_ref[...] = h2.T.astype(BF)

        for c in range(FF // FCHUNK):
            cols = slice(c * FCHUNK, (c + 1) * FCHUNK)
            a = _dot_nt(h2b, wg_ref[cols, :])
            b = _dot_nt(h2b, wu_ref[cols, :])
            a_ref[:, cols] = a.astype(BF)
            b_ref[:, cols] = b.astype(BF)
            f_ref[:, cols] = (_silu_parts(a)[1] * b).astype(BF)
        ff = _dot(f_ref[...], wd_ref[...])

        x3 = x2 + gt2 * ff
        r3 = _rms(x3)
        xn3 = x3 * r3
        gfin = gfin_ref[...]
        e = xn3 * gfin - t_ref[...]
        loss_ref[...] += 0.5 * jnp.sum(jnp.mean(e * e, axis=-1, keepdims=True))
        dy = e * (1.0 / D)
        dgfin_ref[...] += _colsum(dy * xn3)
        dx3 = _rms_bwd(dy * gfin, xn3, r3)
        dgt2_ref[...] += _colsum(dx3 * ff)
        dff = dx3 * gt2
        dffb = dff.astype(BF)
        dfft_ref[...] = dff.T.astype(BF)

        for c in range(FF // FCHUNK):
            cols = slice(c * FCHUNK, (c + 1) * FCHUNK)
            df = _dot_nt(dffb, wd_ref[cols, :])
            av = a_ref[:, cols].astype(F32)
            bv = b_ref[:, cols].astype(F32)
            sg, sa = _silu_parts(av)
            db_ref[:, cols] = (df * sa).astype(BF)
            da_ref[:, cols] = (df * bv * (sg * (1.0 + av * (1.0 - sg)))).astype(BF)
        dh2 = _dot(da_ref[...], wg_ref[...]) + _dot(db_ref[...], wu_ref[...])

        along = _colsum(dh2 * xn2)
        dsc2_ref[...] += along * gffn
        dsh2_ref[...] += _colsum(dh2)
        dgffn_ref[...] += along * (1.0 + sc2)
        dx2 = dx3 + _rms_bwd(dh2 * (gffn * (1.0 + sc2)), xn2, r2)
        dx2_ref[...] = dx2
        dgt1_ref[...] += _colsum(dx2 * mix)
        dmix = (dx2 * gt1).astype(BF)
        dmix_ref[...] = dmix
        dcat = _dot_nt(dmix, wo_ref[...])
        dymla_ref[...] = dcat[:, 0:PW].astype(BF)
        dypool_ref[...] = dcat[:, PW:2 * PW]

    row = lambda c: _rows(ts, c)
    col = pl.BlockSpec((D, ts), lambda i: (0, i))
    const = _full
    vec = jax.ShapeDtypeStruct((1, D), F32)
    out_shape = (
        jax.ShapeDtypeStruct((D, S), BF),
        jax.ShapeDtypeStruct((S, FF), BF),
        jax.ShapeDtypeStruct((S, FF), BF),
        jax.ShapeDtypeStruct((S, FF), BF),
        jax.ShapeDtypeStruct((D, S), BF),
        jax.ShapeDtypeStruct((S, D), F32),
        jax.ShapeDtypeStruct((S, D), BF),
        jax.ShapeDtypeStruct((D, S), BF),
        jax.ShapeDtypeStruct((S, PW), BF),
        jax.ShapeDtypeStruct((S, PW), F32),
        jax.ShapeDtypeStruct((8, 128), F32),
        vec, vec, vec, vec, vec, vec,
    )
    return pl.pallas_call(
        body, name="ffn_fwd_bwd", out_shape=out_shape, grid=(S // ts,),
        in_specs=[row(D), row(PW), row(PW), const(mod.shape), _vmem(), const((1, D)), _vmem(), _vmem(), _vmem(),
                  const((1, D)), row(D)],
        out_specs=(col, row(FF), row(FF), row(FF), col, row(D), row(D), col, row(PW), row(PW),
                   const((8, 128))) + (const((1, D)),) * 6,
        scratch_shapes=[pltpu.VMEM((ts, FF), BF), pltpu.VMEM((ts, FF), BF)],
        compiler_params=_params(("arbitrary",)),
    )(x, ymla, ypool, mod, w_o, g_ffn, wg_t, wu_t, wd, g_final, target)


FCHUNK = 256


def _ffn_bwd_weights(dff_t, h2_t, da, db, f):
    S = da.shape[0]

    def body(dfft_ref, h2t_ref, da_ref, db_ref, f_ref, dwg_ref, dwu_ref, dwd_ref):
        h2t = h2t_ref[...]
        dwg_ref[...] = _dot(h2t, da_ref[...]).T.astype(BF)
        dwu_ref[...] = _dot(h2t, db_ref[...]).T.astype(BF)
        dwd_ref[...] = _dot(dfft_ref[...], f_ref[...]).T.astype(BF)

    act = pl.BlockSpec((S, FCHUNK), lambda j: (0, j))
    wblk = _rows(FCHUNK, D)
    shp = jax.ShapeDtypeStruct((FF, D), BF)
    return pl.pallas_call(
        body, name="ffn_bwd_weights", out_shape=(shp, shp, shp), grid=(FF // FCHUNK,),
        in_specs=[_vmem(), _vmem(), act, act, act], out_specs=(wblk, wblk, wblk),
        compiler_params=_params(("arbitrary",)),
    )(dff_t, h2_t, da, db, f)


def _wo_grad(cat_t, dmix):
    S = dmix.shape[0]
    tc = 256

    def body(catt_ref, dmix_ref, dwo_ref):
        dwo_ref[...] = _dot(catt_ref[...], dmix_ref[...]).astype(BF)

    return pl.pallas_call(
        body, name="wo_grad", out_shape=jax.ShapeDtypeStruct((D, D), BF), grid=(D // tc,),
        in_specs=[_vmem(), pl.BlockSpec((S, tc), lambda j: (0, j))], out_specs=pl.BlockSpec((D, tc), lambda j: (0, j)),
        compiler_params=_params(("arbitrary",)),
    )(cat_t, dmix)


def _mix_bwd(dymla, dypool, ypre, pooled, pool_scale, wpool_dc, olat, wuv_vc):
    S = dymla.shape[0]
    ts = 512
    n = S // ts
    nsub = ts // TQ
    M = HEADS * TQ

    def body(dymla_ref, dypool_ref, ypre_ref, pooled_ref, pscale_ref, wpool_ref, olat_ref, wuv_ref,
             du_ref, dolat_ref, delta_ref, dwuv_ref, dwpool_ref, dpscale_ref, carry_ref, dwpool_acc, wpool_bd,
             wuv_bd):
        i = pl.program_id(0)

        @pl.when(i == 0)
        def _():
            carry_ref[...] = jnp.zeros_like(carry_ref)
            dwpool_acc[...] = jnp.zeros_like(dwpool_acc)
            _fill_block_diagonal(wpool_bd, wpool_ref)
            _fill_block_diagonal(wuv_bd, wuv_ref)
            for r in (dwuv_ref, dpscale_ref):
                r[...] = jnp.zeros_like(r)

        dypool = dypool_ref[...]
        dpscale_ref[...] += _colsum(dypool * ypre_ref[...].astype(F32))
        dypre = (dypool * pscale_ref[...]).astype(BF)
        dwpool_acc[...] += _dot_tn(pooled_ref[...], dypre)
        dpooled = _dot(dypre, wpool_bd[...])
        tile = n - 1 - i
        e = dpooled / _row_counts(tile * ts, ts)
        ext = jnp.concatenate([e, carry_ref[...]], axis=0)
        du_ref[...] = (_window_sums(ext, False)[0:ts, :] - dpooled).astype(BF)
        carry_ref[...] = e[0:16, :]

        dob_all = dymla_ref[...]
        dol_all = _dot(dob_all, wuv_bd[...])
        for hd in range(HEADS):
            dob = dob_all[:, hd * 128:(hd + 1) * 128]
            dol = dol_all[:, hd * 128:(hd + 1) * 128]
            for a in range(nsub):
                ol_t = olat_ref[a, :, hd * TQ:(hd + 1) * TQ]
                dl = dol[a * TQ:(a + 1) * TQ, :]
                dolat_ref[a, hd * TQ:(hd + 1) * TQ, :] = dl.astype(BF)
                dwuv_ref[hd] += _dot(ol_t.astype(BF), dob[a * TQ:(a + 1) * TQ, :])
                delta = jnp.sum(dl * ol_t.T, axis=-1, keepdims=True)
                delta_ref[a, :, hd * TQ:(hd + 1) * TQ] = jnp.broadcast_to(delta, (TQ, 128)).T[0:8, :]

        @pl.when(i == n - 1)
        def _():
            for g in range(GROUPS):
                dwpool_ref[g] = dwpool_acc[g * GD:(g + 1) * GD, g * GD:(g + 1) * GD]

    rev = lambda c: pl.BlockSpec((ts, c), lambda i: (n - 1 - i, 0))
    rev3 = lambda r, c: pl.BlockSpec((nsub, r, c), lambda i: (n - 1 - i, 0, 0))
    out_shape = (
        jax.ShapeDtypeStruct((S, PW), BF),
        jax.ShapeDtypeStruct((S // TQ, M, KVL), BF),
        jax.ShapeDtypeStruct((S // TQ, 8, M), F32),
        jax.ShapeDtypeStruct((HEADS, KVL, 128), F32),
        jax.ShapeDtypeStruct((GROUPS, GD, GD), F32),
        jax.ShapeDtypeStruct((1, PW), F32),
    )
    in_specs = [rev(PW), rev(PW), rev(PW), rev(PW), _full((1, PW)), _full(wpool_dc.shape), rev3(KVL, M),
                _full(wuv_vc.shape)]
    out_specs = (rev(PW), rev3(M, KVL), rev3(8, M), _full((HEADS, KVL, 128)), _full((GROUPS, GD, GD)),
                 _full((1, PW)))
    return pl.pallas_call(
        body, name="mix_bwd", out_shape=out_shape, grid=(n,), in_specs=in_specs, out_specs=out_specs,
        scratch_shapes=[pltpu.VMEM((16, PW), F32), pltpu.VMEM((PW, PW), F32), pltpu.VMEM((PW, PW), BF),
                        pltpu.VMEM((HEADS * 128, HEADS * KVL), BF)],
        compiler_params=_params(("arbitrary",)),
    )(dymla, dypool, ypre, pooled, pool_scale, wpool_dc, olat, wuv_vc)


def _attn_bwd(qs, kv, dolat, lse, delta):
    nq = qs.shape[0]
    S = kv.shape[0]
    M = HEADS * TQ
    nk = S // TK

    def body(qs_ref, kv_ref, do_ref, lse_ref, delta_ref, dkv_ref, dqt_out_ref, dqt_ref, p_ref, ds_ref):
        kt = pl.program_id(0)
        k = kv_ref[...]
        v = k[:, 0:KVL]
        k_t = k.astype(F32).T.astype(BF)

        @pl.when(kt == 0)
        def _():
            dqt_ref[...] = jnp.zeros_like(dqt_ref)

        def step(qi, carry, first_chunk=None):
            dk, dv = carry
            q = qs_ref[qi]
            do = do_ref[qi]
            s = _dot_nt(k, q)
            dp = _dot_nt(v, do)
            lse_row = lse_ref[qi, 0:1, :] * LOG2_E
            delta_row = delta_ref[qi, 0:1, :]
            q_chunk = (lax.broadcasted_iota(jnp.int32, (1, M), 1) & (TQ - 1)) >> 6
            for r in range(0, TK, VPU_ROWS):
                rows = slice(r, r + VPU_ROWS)
                p = jnp.exp2(s[rows, :] * EXP2_SCALE - lse_row)
                if first_chunk is not None:
                    p = jnp.where((r >> 6) + first_chunk <= q_chunk, p, 0.0)
                p_ref[rows, :] = p.astype(BF)
                ds_ref[rows, :] = (p * (dp[rows, :] - delta_row) * SM_SCALE).astype(BF)
            ds = ds_ref[...]
            dv = dv + _dot(p_ref[...], do)
            dk = dk + _dot(ds, q)
            dqt_ref[qi] += _dot(k_t, ds)
            return dk, dv

        per = TQ // TK
        first = kt // per
        carry = step(first, (jnp.zeros((TK, QW), F32), jnp.zeros((TK, KVL), F32)), (kt % per) * (TK // 64))
        dk, dv = lax.fori_loop(first + 1, nq, step, carry)
        dkv_ref[...] = dk + jnp.concatenate([dv, jnp.zeros((TK, QW - KVL), F32)], axis=1)
        dqt_out_ref[0] = dqt_ref[first].astype(BF)

    out_shape = (jax.ShapeDtypeStruct((S, QW), F32), jax.ShapeDtypeStruct((nq, QW, M), BF))
    return pl.pallas_call(
        body, name="attn_bwd", out_shape=out_shape, grid=(nk,),
        in_specs=[_vmem(), _rows(TK, QW), _vmem(), _vmem(), _vmem()],
        out_specs=(_rows(TK, QW), pl.BlockSpec((1, QW, M), lambda kt: (kt // (TQ // TK), 0, 0))),
        scratch_shapes=[pltpu.VMEM((nq, QW, M), F32), pltpu.VMEM((TK, M), BF), pltpu.VMEM((TK, M), BF)],
        compiler_params=_params(("arbitrary",)),
    )(qs, kv, dolat, lse, delta)


def _in_bwd(dqt, dkv, du, raw, qn, h1, x, dx2, mod, g_mix, w_in, g_q, g_kv, w_uq, wuk_cd, perm_t, cos4, sin4, csk,
            snk):
    S = x.shape[0]
    ts = 512
    n = S // ts
    nsub = ts // TQ
    M = HEADS * TQ

    def body(dqt_ref, dkv_ref, du_ref, raw_ref, qn_ref, h1_ref, x_ref, dx2_ref, mod_ref, gmix_ref, win_ref, gq_ref,
             gkv_ref, wuq_ref, wuk_ref, permt_ref, cos_ref, sin_ref, csk_ref, snk_ref,
             dx_ref, dwin_ref, dwuq_ref, dwuk_ref, dgq_ref, dgkv_ref, dsc1_ref, dsh1_ref, dgmix_ref, dwin_acc,
             dwuq_acc, dwuk_acc, wuk_bd):
        i = pl.program_id(0)

        @pl.when(i == 0)
        def _():
            dwin_acc[...] = jnp.zeros_like(dwin_acc)
            dwuq_acc[...] = jnp.zeros_like(dwuq_acc)
            dwuk_acc[...] = jnp.zeros_like(dwuk_acc)
            _fill_block_diagonal(wuk_bd, wuk_ref)
            for r in (dgq_ref, dgkv_ref, dsc1_ref, dsh1_ref, dgmix_ref):
                r[...] = jnp.zeros_like(r)

        dq_blocks = [dqt_ref[a].astype(F32).T for a in range(nsub)]
        dq_heads = [jnp.concatenate([blk[hd * TQ:(hd + 1) * TQ, :] for blk in dq_blocks], axis=0)
                    for hd in range(HEADS)]
        dq_lat = jnp.concatenate([dqh[:, 0:KVL] for dqh in dq_heads], axis=1).astype(BF)
        dq_rope = jnp.concatenate([dqh[:, KVL:QW] for dqh in dq_heads], axis=1).astype(BF)
        dq_nope = _dot(dq_lat, wuk_bd[...])
        dwuk_acc[...] += _dot_tn(dq_lat, qn_ref[...])
        drope = _dot(dq_rope, permt_ref[...])
        do1 = drope[:, 0:128]
        do2 = drope[:, 128:256]
        cosv = cos_ref[...]
        sinv = sin_ref[...]
        dq = jnp.concatenate([dq_nope, do1 * cosv + do2 * sinv, do2 * cosv - do1 * sinv], axis=1).astype(BF)

        cq_raw = raw_ref[:, 0:QL]
        ckv_raw = raw_ref[:, QL:QL + KVL]
        rq = _rms(cq_raw)
        nq_ = cq_raw * rq
        gq = gq_ref[...]
        dwuq_acc[...] += _dot_tn((nq_ * gq).astype(BF), dq)
        dc_q = _dot_nt(dq, wuq_ref[...])
        dgq_ref[...] += _colsum(dc_q * nq_)
        dcq_raw = _rms_bwd(dc_q * gq, nq_, rq)

        dkv = dkv_ref[...]
        rk = _rms(ckv_raw)
        nk_ = ckv_raw * rk
        dc_kv = dkv[:, 0:KVL]
        dgkv_ref[...] += _colsum(dc_kv * nk_)
        dckv_raw = _rms_bwd(dc_kv * gkv_ref[...], nk_, rk)
        dkr_roped = dkv[:, KVL:QW]
        dkr = dkr_roped * csk_ref[...] - _swap_halves(dkr_roped) * snk_ref[...]

        dproj = jnp.concatenate([dcq_raw.astype(BF), dckv_raw.astype(BF), dkr.astype(BF), du_ref[...]], axis=1)
        dwin_acc[...] += _dot_tn(h1_ref[...], dproj)
        dh1 = _dot_nt(dproj, win_ref[...])

        sc1 = mod_ref[0:1, D:2 * D]
        gmix = gmix_ref[...]
        xv = x_ref[...]
        r1 = _rms(xv)
        xn1 = xv * r1
        along = _colsum(dh1 * xn1)
        dsc1_ref[...] += along * gmix
        dsh1_ref[...] += _colsum(dh1)
        dgmix_ref[...] += along * (1.0 + sc1)
        dx_ref[...] = dx2_ref[...] + _rms_bwd(dh1 * (gmix * (1.0 + sc1)), xn1, r1)

        @pl.when(i == n - 1)
        def _():
            dwin_ref[...] = dwin_acc[...].astype(BF)
            dwuq_ref[...] = dwuq_acc[...].astype(BF)
            for hd in range(HEADS):
                dwuk_ref[hd] = dwuk_acc[hd * KVL:(hd + 1) * KVL, hd * NOPE:(hd + 1) * NOPE]

    out_shape = (
        jax.ShapeDtypeStruct((S, D), F32),
        jax.ShapeDtypeStruct((D, D), BF),
        jax.ShapeDtypeStruct((QL, 768), BF),
        jax.ShapeDtypeStruct((HEADS, KVL, NOPE), F32),
        jax.ShapeDtypeStruct((1, QL), F32), jax.ShapeDtypeStruct((1, KVL), F32),
        jax.ShapeDtypeStruct((1, D), F32), jax.ShapeDtypeStruct((1, D), F32), jax.ShapeDtypeStruct((1, D), F32),
    )
    in_specs = [pl.BlockSpec((nsub, QW, M), lambda i: (i, 0, 0)), _rows(ts, QW), _rows(ts, PW), _rows(ts, 384),
                _rows(ts, HEADS * NOPE), _rows(ts, D), _rows(ts, D), _rows(ts, D), _full(mod.shape), _full((1, D)),
                _full(w_in.shape), _full((1, QL)), _full((1, KVL)), _full(w_uq.shape), _full(wuk_cd.shape),
                _full(perm_t.shape), _rows(ts, 128), _rows(ts, 128), _rows(ts, 128), _rows(ts, 128)]
    out_specs = (_rows(ts, D), _full((D, D)), _full((QL, 768)), _full((HEADS, KVL, NOPE)), _full((1, QL)),
                 _full((1, KVL)), _full((1, D)), _full((1, D)), _full((1, D)))
    return pl.pallas_call(
        body, name="in_bwd", out_shape=out_shape, grid=(n,), in_specs=in_specs, out_specs=out_specs,
        scratch_shapes=[pltpu.VMEM((D, D), F32), pltpu.VMEM((QL, 768), F32),
                        pltpu.VMEM((HEADS * KVL, HEADS * NOPE), F32), pltpu.VMEM((HEADS * KVL, HEADS * NOPE), BF)],
        compiler_params=_params(("arbitrary",)),
    )(dqt, dkv, du, raw, qn, h1, x, dx2, mod, g_mix, w_in, g_q, g_kv, w_uq, wuk_cd, perm_t, cos4, sin4, csk, snk)


def _rope_perm():
    p = np.zeros((HEADS, 2 * 128, 128), np.float32)
    for hd in range(HEADS):
        for t in range(HALF):
            p[hd, hd * HALF + t, t] = 1.0
            p[hd, 128 + hd * HALF + t, HALF + t] = 1.0
    return p


def _rope_tables(positions):
    freqs = jnp.power(ROPE_THETA, -jnp.arange(HALF, dtype=F32) / HALF)
    ang = positions.astype(F32)[:, None] * jnp.tile(freqs, HEADS)[None, :]
    cos4 = jnp.cos(ang)
    sin4 = jnp.sin(ang)
    lane = jnp.arange(HEADS * HALF)[None, :]
    csk = jnp.where(lane < ROPE, cos4, 0.0)
    snk = jnp.where(lane < HALF, -sin4, jnp.where(lane < ROPE, sin4, 0.0))
    return cos4, sin4, csk, snk


def _local_step(x, rope, target, mod, g_mix, w_in_p, g_q, g_kv, w_uq_p, w_uk, w_uv, w_pool, pool_scale, g_ffn,
                g_final, late, ffn_grads_exchange):
    perm = jnp.asarray(_rope_perm().transpose(1, 0, 2).reshape(2 * 128, HEADS * 128), BF)
    perm_t = jnp.asarray(_rope_perm().transpose(0, 2, 1).reshape(HEADS * 128, 2 * 128), BF)
    cos4, sin4, csk, snk = rope
    wuk_dc = w_uk.transpose(1, 2, 0).astype(BF)
    wuk_cd = w_uk.transpose(1, 0, 2).astype(BF)
    wuv_vc = w_uv.transpose(1, 2, 0).astype(BF)
    wpool = w_pool.astype(BF)
    wpool_dc = w_pool.transpose(0, 2, 1).astype(BF)

    h1, raw, qn, qs, kv, kvt, pooled, ypre, ypool = _fwd_in(
        x, mod, g_mix, w_in_p, g_q, g_kv, w_uq_p, wuk_dc, perm, cos4, sin4, csk, snk, wpool, pool_scale)
    olat, ymla, lse = _attn_fwd(qs, kv, kvt, wuv_vc)
    w_o, wg_t, wu_t, wd = late
    (h2_t, f, da, db, dff_t, dx2, dmix, cat_t, dymla, dypool, loss, dgfin, dgt2, dgt1, dsc2, dsh2,
     dgffn) = _ffn_fwd_bwd(x, ymla, ypool, mod, w_o, g_ffn, wg_t, wu_t, wd, g_final, target)
    dwo = _wo_grad(cat_t, dmix)
    dwg_t, dwu_t, dwd = _ffn_bwd_weights(dff_t, h2_t, da, db, f)
    ffn_parts = ffn_grads_exchange((dwg_t, dwu_t, dwd, dwo))
    du, dolat, delta, dwuv, dwpool, dpscale = _mix_bwd(
        dymla, dypool, ypre, pooled, pool_scale, wpool_dc, olat, wuv_vc)
    dkv, dqt = _attn_bwd(qs, kv, dolat, lse, delta)
    dx, dwin, dwuq, dwuk, dgq, dgkv, dsc1, dsh1, dgmix = _in_bwd(
        dqt, dkv, du, raw, qn, h1, x, dx2, mod, g_mix, w_in_p, g_q, g_kv, w_uq_p, wuk_cd, perm_t, cos4, sin4, csk,
        snk)
    dmod = jnp.concatenate([dsh1, dsc1, dgt1, dsh2, dsc2, dgt2], axis=1)
    replicated = dict(
        w_uk=dwuk.transpose(1, 0, 2), w_uv=dwuv.transpose(1, 0, 2), w_pool=dwpool, g_mix=dgmix, g_q=dgq, g_kv=dgkv,
        pool_scale=dpscale, g_ffn=dgffn, g_final=dgfin)
    return loss[0, 0], dx, dmod, (dwin, dwuq), ffn_parts, replicated


def _my_pos():
    return lax.axis_index("x"), lax.axis_index("y"), lax.axis_index("c")


def _peer(pos, k):
    x, y, c = pos
    return (1 - x if k & 4 else x, 1 - y if k & 2 else y, 1 - c if k & 1 else c)


def _index(pos):
    x, y, c = pos
    return 4 * x + 2 * y + c


def _remote(src, dst, send_sem, recv_sem, to):
    return pltpu.make_async_remote_copy(src_ref=src, dst_ref=dst, send_sem=send_sem, recv_sem=recv_sem,
                                        device_id=to, device_id_type=MESH)


def _ada_mod(c, w_ada, b_ada, after):
    def body(c_ref, w_ref, b_ref, after_ref, mod_ref, call_ref, cbuf, sbuf, rbuf, send1, recv1, send2, recv2):
        me = _my_pos()
        mi = _index(me)
        cv = c_ref[...]
        cbuf[...] = jnp.broadcast_to(cv * jax.nn.sigmoid(cv), (8, D))
        call_ref[mi] = cbuf[...]
        first = [_remote(cbuf, call_ref.at[mi], send1.at[k - 1], recv1.at[k - 1], _peer(me, k)) for k in range(1, NDEV)]
        for cp in first:
            cp.start()
        for k in range(1, NDEV):
            _remote(cbuf, call_ref.at[_index(_peer(me, k))], send1.at[k - 1], recv1.at[k - 1], _peer(me, k)).wait_recv()
        c_all = jnp.concatenate([call_ref[b][0:1, :] for b in range(NDEV)], axis=0)
        blocks = _dot(c_all.astype(BF), w_ref[...].astype(BF))
        for b in range(NDEV):
            sbuf[b] = jnp.broadcast_to(blocks[b:b + 1, :], (8, MODC))
        second = []
        for k in range(1, NDEV):
            to = _peer(me, k)
            second.append(_remote(sbuf.at[_index(to)], rbuf.at[mi], send2.at[k - 1], recv2.at[k - 1], to))
        for cp in second:
            cp.start()
        rbuf[mi] = sbuf[mi]
        for k in range(1, NDEV):
            to = _peer(me, k)
            _remote(sbuf.at[_index(to)], rbuf.at[_index(to)], send2.at[k - 1], recv2.at[k - 1], to).wait_recv()
        for j in range(NDEV):
            mod_ref[:, j * MODC:(j + 1) * MODC] = rbuf[j] + b_ref[:, j * MODC:(j + 1) * MODC]
        for cp in first + second:
            cp.wait_send()

    return pl.pallas_call(
        body, name="ada_mod",
        out_shape=(jax.ShapeDtypeStruct((8, N_MOD * D), F32), jax.ShapeDtypeStruct((NDEV, 8, D), F32)),
        in_specs=[_vmem(), _vmem(), _vmem(), _any()], out_specs=(_vmem(), _vmem()),
        scratch_shapes=[pltpu.VMEM((8, D), F32), pltpu.VMEM((NDEV, 8, MODC), F32), pltpu.VMEM((NDEV, 8, MODC), F32),
                        pltpu.SemaphoreType.DMA((NDEV - 1,)), pltpu.SemaphoreType.DMA((NDEV - 1,)),
                        pltpu.SemaphoreType.DMA((NDEV - 1,)), pltpu.SemaphoreType.DMA((NDEV - 1,))],
        compiler_params=_params(),
    )(c, w_ada, b_ada, after)


def _sequencer_scatter(name, collective_id, srcs, after=()):
    n = len(srcs)

    def of(src, to_index):
        r = src.shape[0] // NDEV
        return src.at[pl.ds(pl.multiple_of(to_index * r, 16), r), :]

    def body(*refs):
        src, zone = refs[:n], refs[n + len(after):2 * n + len(after)]
        send, recv, local = refs[2 * n + len(after):]
        me = _my_pos()
        mi = _index(me)
        barrier = pltpu.get_barrier_semaphore()
        for k in range(1, NDEV):
            pl.semaphore_signal(barrier, inc=1, device_id=_peer(me, k), device_id_type=MESH)
        pl.semaphore_wait(barrier, NDEV - 1)
        own = [pltpu.make_async_copy(of(src[a], mi), zone[a].at[mi], local.at[a]) for a in range(n)]
        for cp in own:
            cp.start()
        for a in range(n):
            for k in range(1, NDEV):
                to = _peer(me, k)
                s = a * (NDEV - 1) + k - 1
                _remote(of(src[a], _index(to)), zone[a].at[mi], send.at[s], recv.at[s], to).start()
        for cp in own:
            cp.wait()
        for a in range(n):
            for k in range(1, NDEV):
                to = _peer(me, k)
                s = a * (NDEV - 1) + k - 1
                cp = _remote(of(src[a], mi), zone[a].at[_index(to)], send.at[s], recv.at[s], to)
                cp.wait_send()
                cp.wait_recv()

    return pl.kernel(
        body, name=name, mesh=plsc.ScalarSubcoreMesh(axis_name="sequencer", num_cores=1),
        out_type=tuple(jax.ShapeDtypeStruct((NDEV, s.shape[0] // NDEV, s.shape[1]), s.dtype) for s in srcs),
        scratch_types=[pltpu.SemaphoreType.DMA((n * (NDEV - 1),)), pltpu.SemaphoreType.DMA((n * (NDEV - 1),)),
                       pltpu.SemaphoreType.DMA((n,))],
        compiler_params=pltpu.CompilerParams(collective_id=collective_id),
    )(*srcs, *after)


CHIP_PEERS = (2, 4, 6)


def _sequencer_gather(name, collective_id, srcs, after=()):
    n = len(srcs)
    per = NDEV - 1

    def body(*refs):
        src, zone = refs[:n], refs[n + len(after):2 * n + len(after)]
        send, recv, local = refs[2 * n + len(after):]
        me = _my_pos()
        mi = _index(me)
        sibling = _peer(me, 1)
        talk_to = (sibling,) + tuple(_peer(me, k) for k in CHIP_PEERS)
        barrier = pltpu.get_barrier_semaphore()
        for to in talk_to:
            pl.semaphore_signal(barrier, inc=1, device_id=to, device_id_type=MESH)
        pl.semaphore_wait(barrier, len(talk_to))

        def copy(a, slot, block_of, to, from_src=False):
            rows = zone[a].at[_index(block_of)]
            return _remote(src[a] if from_src else rows, rows, send.at[a * per + slot], recv.at[a * per + slot], to)

        own = [pltpu.make_async_copy(src[a], zone[a].at[mi], local.at[a]) for a in range(n)]
        for cp in own:
            cp.start()
        started = []
        for a in range(n):
            started.append(copy(a, 0, me, sibling, from_src=True))
            started += [copy(a, 1 + j, me, _peer(me, k), from_src=True) for j, k in enumerate(CHIP_PEERS)]
        for cp in started:
            cp.start()
        for a in range(n):
            for j, k in enumerate(CHIP_PEERS):
                copy(a, 1 + j, _peer(me, k), me).wait_recv()
                passed = copy(a, 4 + j, _peer(me, k), sibling)
                passed.start()
                started.append(passed)
        for a in range(n):
            copy(a, 0, sibling, me).wait_recv()
            for j, k in enumerate(CHIP_PEERS):
                copy(a, 4 + j, _peer(me, k | 1), me).wait_recv()
        for cp in started:
            cp.wait_send()
        for cp in own:
            cp.wait()

    return pl.kernel(
        body, name=name, mesh=plsc.ScalarSubcoreMesh(axis_name="sequencer", num_cores=1),
        out_type=tuple(jax.ShapeDtypeStruct((NDEV,) + s.shape, s.dtype) for s in srcs),
        scratch_types=[pltpu.SemaphoreType.DMA((n * per,)), pltpu.SemaphoreType.DMA((n * per,)),
                       pltpu.SemaphoreType.DMA((n,))],
        compiler_params=pltpu.CompilerParams(collective_id=collective_id),
    )(*srcs, *after)


def _blocked(shape, nb, axis=0):
    block = tuple(s // nb if d == axis else s for d, s in enumerate(shape))
    return pl.BlockSpec(block, lambda i: tuple(i if d == axis else 0 for d in range(len(shape))))


def _sum_partials(name, parts, nb):
    n = len(parts)

    def body(*refs):
        for a in range(n):
            acc = refs[a][0].astype(F32)
            for p in range(1, NDEV):
                acc = acc + refs[a][p].astype(F32)
            refs[n + a][...] = acc

    return pl.pallas_call(
        body, name=name, grid=(nb,),
        out_shape=tuple(jax.ShapeDtypeStruct(p.shape[1:], F32) for p in parts),
        in_specs=[_blocked(p.shape, nb, 1) for p in parts],
        out_specs=tuple(_blocked(p.shape[1:], nb) for p in parts), compiler_params=_params(("arbitrary",)),
    )(*parts)


def _small_all_reduce(buf):
    def body(buf_ref, got_ref, red_ref, mine, send1, recv1, send2, recv2):
        me = _my_pos()
        mi = _index(me)
        first = []
        for k in range(1, NDEV):
            to = _peer(me, k)
            first.append(_remote(buf_ref.at[_index(to)], got_ref.at[mi], send1.at[k - 1], recv1.at[k - 1], to))
        for cp in first:
            cp.start()
        got_ref[mi] = buf_ref[mi]
        for k in range(1, NDEV):
            to = _peer(me, k)
            _remote(buf_ref.at[mi], got_ref.at[_index(to)], send1.at[k - 1], recv1.at[k - 1], to).wait_recv()
        acc = got_ref[0]
        for p in range(1, NDEV):
            acc = acc + got_ref[p]
        mine[...] = acc
        second = [_remote(mine, red_ref.at[mi], send2.at[k - 1], recv2.at[k - 1], _peer(me, k)) for k in range(1, NDEV)]
        for cp in second:
            cp.start()
        red_ref[mi] = acc
        for k in range(1, NDEV):
            to = _peer(me, k)
            _remote(mine, red_ref.at[_index(to)], send2.at[k - 1], recv2.at[k - 1], to).wait_recv()
        for cp in first + second:
            cp.wait_send()

    return pl.pallas_call(
        body, name="small_all_reduce",
        out_shape=(jax.ShapeDtypeStruct(buf.shape, F32), jax.ShapeDtypeStruct(buf.shape, F32)),
        in_specs=[_vmem()], out_specs=(_vmem(), _vmem()),
        scratch_shapes=[pltpu.VMEM(buf.shape[1:], F32),
                        pltpu.SemaphoreType.DMA((NDEV - 1,)), pltpu.SemaphoreType.DMA((NDEV - 1,)),
                        pltpu.SemaphoreType.DMA((NDEV - 1,)), pltpu.SemaphoreType.DMA((NDEV - 1,))],
        compiler_params=_params(),
    )(buf)


def _adamw_math(w, g, m, v):
    m = ADAM_B1 * m + (1.0 - ADAM_B1) * g
    v = ADAM_B2 * v + (1.0 - ADAM_B2) * jnp.square(g)
    m_hat = m / (1.0 - ADAM_B1 ** ADAM_STEP)
    v_hat = v / (1.0 - ADAM_B2 ** ADAM_STEP)
    delta = -ADAM_LR * (m_hat / (jnp.sqrt(v_hat) + ADAM_EPS) + ADAM_WD * w)
    return delta, m, v


def _adamw_group(name, ws, gs, ms, vs, nb):
    n = len(ws)

    def body(*refs):
        for a in range(n):
            w, g, m, v = (refs[q * n + a][...] for q in range(4))
            delta, m2, v2 = _adamw_math(w, g, m, v)
            refs[4 * n + a][...] = delta
            refs[5 * n + a][...] = m2
            refs[6 * n + a][...] = v2

    shapes = tuple(jax.ShapeDtypeStruct(w.shape, F32) for w in ws)
    specs = [_blocked(w.shape, nb) for w in ws]
    outs = pl.pallas_call(
        body, name=name, grid=(nb,), out_shape=shapes * 3, in_specs=specs * 4, out_specs=tuple(specs * 3),
        compiler_params=_params(("arbitrary",)),
    )(*ws, *gs, *ms, *vs)
    return outs[:n], outs[n:2 * n], outs[2 * n:]


def _adamw_from_partials(name, ws, parts, ms, vs, nb):
    n = len(ws)

    def body(*refs):
        for a in range(n):
            part = refs[n + a]
            g = part[0].astype(F32)
            for p in range(1, NDEV):
                g = g + part[p].astype(F32)
            delta, m2, v2 = _adamw_math(refs[a][...], g, refs[2 * n + a][...], refs[3 * n + a][...])
            refs[4 * n + a][...] = g
            refs[5 * n + a][...] = delta
            refs[6 * n + a][...] = m2
            refs[7 * n + a][...] = v2

    shapes = tuple(jax.ShapeDtypeStruct(w.shape, F32) for w in ws)
    specs = [_blocked(w.shape, nb) for w in ws]
    outs = pl.pallas_call(
        body, name=name, grid=(nb,), out_shape=shapes * 4,
        in_specs=specs + [_blocked(p.shape, nb, 1) for p in parts] + specs * 2, out_specs=tuple(specs * 4),
        compiler_params=_params(("arbitrary",)),
    )(*ws, *parts, *ms, *vs)
    return outs[:n], outs[n:2 * n], outs[2 * n:3 * n], outs[3 * n:]


def _adamw_ada(w, m, v, c_all_t, dmod_rows):
    nb = 4

    def body(w_ref, m_ref, v_ref, c_ref, dm_ref, g_ref, d_ref, m2_ref, v2_ref):
        g = _dot(c_ref[...], dm_ref[...].astype(BF))
        g_ref[...] = g
        delta, m2, v2 = _adamw_math(w_ref[...], g, m_ref[...], v_ref[...])
        d_ref[...] = delta
        m2_ref[...] = m2
        v2_ref[...] = v2

    shp = jax.ShapeDtypeStruct(w.shape, F32)
    spec = _blocked(w.shape, nb)
    return pl.pallas_call(
        body, name="adamw_ada", grid=(nb,), out_shape=(shp, shp, shp, shp),
        in_specs=[spec, spec, spec, _blocked(c_all_t.shape, nb), _full(dmod_rows.shape)],
        out_specs=(spec, spec, spec, spec), compiler_params=_params(("arbitrary",)),
    )(w, m, v, c_all_t, dmod_rows)


def _w_in_to_kernel(w):
    return jnp.concatenate([w[:, 0:448], jnp.zeros((w.shape[0], 64), w.dtype), w[:, 448:960]], axis=1)


def _w_in_from_kernel(w):
    return jnp.concatenate([w[:, 0:448], w[:, 512:1024]], axis=1)


def _w_uq_to_kernel(w):
    r = w.shape[0]
    return jnp.concatenate([w[:, :, 0:NOPE].reshape(r, HEADS * NOPE),
                            w[:, :, NOPE:NOPE + HALF].reshape(r, HEADS * HALF),
                            w[:, :, NOPE + HALF:].reshape(r, HEADS * HALF)], axis=1)


def _w_uq_from_kernel(w):
    r = w.shape[0]
    return jnp.concatenate([w[:, 0:512].reshape(r, HEADS, NOPE), w[:, 512:640].reshape(r, HEADS, HALF),
                            w[:, 640:768].reshape(r, HEADS, HALF)], axis=2)


REP_NAMES = ("w_uk", "w_uv", "w_pool", "g_mix", "g_q", "g_kv", "pool_scale", "g_ffn", "g_final")


def kernel(x, c, positions, w_ada, b_ada, g_mix, w_in, g_q, g_kv, w_uq, w_uk, w_uv, w_pool, pool_scale, w_o, g_ffn, w_gate, w_up, w_down, g_final, loss_target, m_w_ada, m_b_ada, m_g_mix, m_w_in, m_g_q, m_g_kv, m_w_uq, m_w_uk, m_w_uv, m_w_pool, m_pool_scale, m_w_o, m_g_ffn, m_w_gate, m_w_up, m_w_down, m_g_final, v_w_ada, v_b_ada, v_g_mix, v_w_in, v_g_q, v_g_kv, v_w_uq, v_w_uk, v_w_uv, v_w_pool, v_pool_scale, v_w_o, v_g_ffn, v_w_gate, v_w_up, v_w_down, v_g_final):
    given = dict(locals())

    merge = lambda g: g.reshape(NDEV * g.shape[1], g.shape[2])
    w_in_p, w_uq_p = (merge(g) for g in _sequencer_gather(
        "gather_in", 3, (_w_in_to_kernel(w_in[0]).astype(BF), _w_uq_to_kernel(w_uq[0]).astype(BF))))

    rope = _rope_tables(positions[0])
    mod, c_all8 = _ada_mod(c, w_ada[0], b_ada, rope[3][0:8, :])
    c_all = c_all8[:, 0, :]
    late = _sequencer_gather(
        "gather_late", 1, (w_o[0].astype(BF), w_gate[0].T.astype(BF), w_up[0].T.astype(BF), w_down[0].astype(BF)),
        after=(mod[:, 0:128], w_in_p[0:16, 0:128], w_uq_p[0:16, 0:128]))

    def ffn_grads_exchange(arrays):
        return _sequencer_scatter("scatter_ffn", 2, arrays)

    loss, dx, dmod, tail_grads, ffn_parts, replicated = _local_step(
        x[0], rope, loss_target[0], mod, g_mix, w_in_p, g_q, g_kv, w_uq_p, w_uk[0], w_uv[0], w_pool[0],
        pool_scale, g_ffn, g_final.reshape(1, D), tuple(merge(g) for g in late), ffn_grads_exchange)

    flat = jnp.concatenate([replicated[k].reshape(-1) for k in REP_NAMES] + [loss.reshape(1)])
    flat = jnp.pad(flat, (0, NDEV * REP_ROWS * 128 - flat.shape[0])).reshape(NDEV, REP_ROWS, 128)
    dmod_blocks = jnp.pad(dmod.reshape(NDEV, MODC // 128, 128), ((0, 0), (0, MOD_ROWS - MODC // 128), (0, 0)))
    got, red = _small_all_reduce(jnp.concatenate([dmod_blocks, flat], axis=1))

    tail_parts = _sequencer_scatter("scatter_tail", 4, tail_grads,
                                    after=(ffn_parts[0][0, 0:16, 0:128], red[0, 0:8, :]))
    g_in_p, g_uq_p = _sum_partials("sum_tail_partials", tail_parts, 1)
    as_transpose = ("w_in", "w_gate", "w_up")
    grads = dict(w_in=_w_in_from_kernel(g_in_p).T, w_uq=_w_uq_from_kernel(g_uq_p))
    partials = dict(w_gate=ffn_parts[0], w_up=ffn_parts[1], w_down=ffn_parts[2], w_o=ffn_parts[3])
    dmod_rows = got[:, 0:MODC // 128, :].reshape(NDEV, MODC)
    grads["b_ada"] = red[:, 0:MODC // 128, :].reshape(1, N_MOD * D)
    rep_flat = red[:, MOD_ROWS:, :].reshape(-1)
    off = 0
    for k in REP_NAMES:
        size = int(np.prod(given[k].shape))
        grads[k] = rep_flat[off:off + size]
        off += size

    view = {k: (given[k].shape[1:] if given[k].ndim > 2 else given[k].shape)
            for k in REP_NAMES + ("b_ada", "w_ada", "w_in", "w_uq", "w_o", "w_gate", "w_up", "w_down")}
    view.update(g_final=(1, D))
    names = ["w_ada", "b_ada", "g_mix", "w_in", "g_q", "g_kv", "w_uq", "w_uk", "w_uv", "w_pool", "pool_scale",
             "w_o", "g_ffn", "w_gate", "w_up", "w_down", "g_final"]
    g_ada, d_ada, m_ada, v_ada = _adamw_ada(w_ada[0], m_w_ada[0], v_w_ada[0], c_all.T.astype(BF), dmod_rows)
    out_g, out_d, out_m, out_v = dict(w_ada=g_ada), dict(w_ada=d_ada), dict(w_ada=m_ada), dict(w_ada=v_ada)
    groups = (("adamw_ffn", ("w_gate", "w_up", "w_down", "w_o"), 4),
              ("adamw_replicated", REP_NAMES + ("b_ada",), 1),
              ("adamw_tail", ("w_in", "w_uq"), 1))
    for gname, members, nb in groups:
        turn = lambda k, t: t.T if k in as_transpose else t
        ws = [turn(k, given[k].reshape(view[k])) for k in members]
        ms = [turn(k, given["m_" + k].reshape(view[k])) for k in members]
        vs = [turn(k, given["v_" + k].reshape(view[k])) for k in members]
        if members[0] in partials:
            gs, ds, m2, v2 = _adamw_from_partials(gname, ws, [partials[k] for k in members], ms, vs, nb)
        else:
            gs = [grads[k] if k in as_transpose else grads[k].reshape(view[k]) for k in members]
            ds, m2, v2 = _adamw_group(gname, ws, gs, ms, vs, nb)
        for k, g, d, mm, vv in zip(members, gs, ds, m2, v2):
            out_g[k], out_d[k], out_m[k], out_v[k] = turn(k, g), turn(k, d), turn(k, mm), turn(k, vv)

    total = rep_flat[off]
    shaped = lambda d: [d[k].reshape(given[k].shape) for k in names]
    return (total, dx[None], *shaped(out_g), *shaped(out_d), *shaped(out_m), *shaped(out_v))
```

```python
import numpy as np
import jax
import jax.numpy as jnp
from jax import lax
from jax.experimental import pallas as pl
from jax.experimental.pallas import tpu as pltpu
from jax.experimental.pallas import tpu_sc as plsc

D = 1024
HEADS = 4
NOPE = 128
ROPE = 64
HALF = ROPE // 2
QL = 256
KVL = 128
FF = 2816
PW = 512
GROUPS = 4
GD = 128
N_MOD = 6
EPS = 1e-6
SM_SCALE = (NOPE + ROPE) ** -0.5
LOG2_E = 1.4426950408889634
EXP2_SCALE = SM_SCALE * LOG2_E
ROPE_THETA = 10000.0
NDEV = 8
MODC = N_MOD * D // NDEV

ADAM_LR = 0.001
ADAM_B1 = 0.9
ADAM_B2 = 0.999
ADAM_EPS = 1e-08
ADAM_WD = 0.01
ADAM_STEP = 10

BF = jnp.bfloat16
F32 = jnp.float32
VMEM_LIMIT_V7X = 60 * 1024 * 1024
MESH = pl.DeviceIdType.MESH

TQ = 512
TK = 512
QW = 256
VPU_ROWS = 16
MOD_ROWS = 8
REP_ROWS = 200
SMALL_ROWS = MOD_ROWS + REP_ROWS


def _params(sem=None):
    return pltpu.CompilerParams(dimension_semantics=sem, vmem_limit_bytes=VMEM_LIMIT_V7X)


def _dot(a, b):
    return jnp.dot(a, b, preferred_element_type=F32)


def _dot_nt(a, b):
    return lax.dot_general(a, b, (((1,), (1,)), ((), ())), preferred_element_type=F32)


def _dot_tn(a, b):
    return _dot(a.astype(F32).T.astype(BF), b)


def _full(shape):
    return pl.BlockSpec(shape, lambda *_: (0,) * len(shape))


def _rows(ts, cols):
    return pl.BlockSpec((ts, cols), lambda i: (i, 0))


def _vmem():
    return pl.BlockSpec(memory_space=pltpu.VMEM)


def _any():
    return pl.BlockSpec(memory_space=pl.ANY)


def _rms(v):
    return lax.rsqrt(jnp.mean(v * v, axis=-1, keepdims=True) + EPS)


def _rms_bwd(dn, n, r):
    return r * (dn - n * jnp.mean(dn * n, axis=-1, keepdims=True))


def _colsum(v):
    return jnp.sum(v, axis=0, keepdims=True)


def _swap_halves(v):
    lane = lax.broadcasted_iota(jnp.int32, v.shape, 1)
    return jnp.where(lane < HALF, pltpu.roll(v, 128 - HALF, 1), pltpu.roll(v, HALF, 1))


def _window_lane_width():
    lane = lax.broadcasted_iota(jnp.int32, (1, PW), 1)
    return jnp.where(lane < 128, 2.0, jnp.where(lane < 256, 4.0, jnp.where(lane < 384, 8.0, 16.0))).astype(F32)


def _window_sums(ext, back):
    n = ext.shape[0]

    def sh(v, k):
        return pltpu.roll(v, k if back else n - k, 0)

    s2 = ext + sh(ext, 1)
    e4 = s2[:, 128:]
    s4 = e4 + sh(e4, 2)
    e8 = s4[:, 128:]
    s8 = e8 + sh(e8, 4)
    e16 = s8[:, 128:]
    s16 = e16 + sh(e16, 8)
    return jnp.concatenate([s2[:, :128], s4[:, :128], s8[:, :128], s16], axis=1)


def _fill_block_diagonal(dst_ref, blocks_ref):
    n, r, c = blocks_ref.shape
    dst_ref[...] = jnp.zeros_like(dst_ref)
    for b in range(n):
        dst_ref[b * r:(b + 1) * r, b * c:(b + 1) * c] = blocks_ref[b]


def _row_counts(first_row, ts):
    t1 = (first_row + lax.broadcasted_iota(jnp.int32, (ts, 1), 0) + 1).astype(F32)
    return jnp.minimum(t1, _window_lane_width())


def _fwd_in(x, mod, g_mix, w_in, g_q, g_kv, w_uq, wuk_dc, perm, cos4, sin4, csk, snk, w_pool, pool_scale):
    S = x.shape[0]
    ts = 1024
    nsub = ts // TQ

    def body(x_ref, mod_ref, gmix_ref, win_ref, gq_ref, gkv_ref, wuq_ref, wuk_ref, perm_ref, cos_ref, sin_ref,
             csk_ref, snk_ref, wpool_ref, pscale_ref,
             h1_ref, raw_ref, qn_ref, qs_ref, kv_ref, kvt_ref, pooled_ref, ypre_ref, ypool_ref, carry_ref, wuk_bd,
             wpool_bd):
        i = pl.program_id(0)

        @pl.when(i == 0)
        def _():
            carry_ref[...] = jnp.zeros_like(carry_ref)
            _fill_block_diagonal(wuk_bd, wuk_ref)
            _fill_block_diagonal(wpool_bd, wpool_ref)

        xv = x_ref[...]
        sh1 = mod_ref[0:1, 0:D]
        sc1 = mod_ref[0:1, D:2 * D]
        h = (xv * _rms(xv)) * gmix_ref[...] * (1.0 + sc1) + sh1
        hb = h.astype(BF)
        h1_ref[...] = hb
        proj = _dot(hb, win_ref[...])
        cq_raw = proj[:, 0:QL]
        ckv_raw = proj[:, QL:QL + KVL]
        kr = proj[:, 384:512]
        u = proj[:, 512:1024]
        raw_ref[...] = proj[:, 0:384]

        c_q = (cq_raw * _rms(cq_raw)) * gq_ref[...]
        c_kv = (ckv_raw * _rms(ckv_raw)) * gkv_ref[...]
        q = _dot(c_q.astype(BF), wuq_ref[...])
        qn = q[:, 0:HEADS * NOPE].astype(BF)
        qn_ref[...] = qn
        x1 = q[:, 512:640]
        x2 = q[:, 640:768]
        cosv = cos_ref[...]
        sinv = sin_ref[...]
        roped = jnp.concatenate([x1 * cosv - x2 * sinv, x1 * sinv + x2 * cosv], axis=1).astype(BF)
        q_lat = _dot(qn, wuk_bd[...])
        q_rope = _dot(roped, perm_ref[...])
        for hd in range(HEADS):
            cols = slice(hd * 128, (hd + 1) * 128)
            qh = jnp.concatenate([q_lat[:, cols], q_rope[:, cols]], axis=1).astype(BF)
            for a in range(nsub):
                qs_ref[a, hd * TQ:(hd + 1) * TQ, :] = qh[a * TQ:(a + 1) * TQ, :]
        k_rope = kr * csk_ref[...] + _swap_halves(kr) * snk_ref[...]
        keys = jnp.concatenate([c_kv, k_rope], axis=1)
        kv_ref[...] = keys.astype(BF)
        for a in range(ts // TK):
            kvt_ref[a] = keys[a * TK:(a + 1) * TK, :].T.astype(BF)

        ext = jnp.concatenate([carry_ref[...], u], axis=0)
        win = _window_sums(ext, True)[16:, :]
        pooled = (win / _row_counts(i * ts, ts) - u).astype(BF)
        pooled_ref[...] = pooled
        carry_ref[...] = u[ts - 16:ts, :]
        ypre = _dot(pooled, wpool_bd[...])
        ypre_ref[...] = ypre.astype(BF)
        ypool_ref[...] = (ypre * pscale_ref[...]).astype(BF)

    out_shape = (
        jax.ShapeDtypeStruct((S, D), BF),
        jax.ShapeDtypeStruct((S, 384), F32),
        jax.ShapeDtypeStruct((S, HEADS * NOPE), BF),
        jax.ShapeDtypeStruct((S // TQ, HEADS * TQ, QW), BF),
        jax.ShapeDtypeStruct((S, QW), BF),
        jax.ShapeDtypeStruct((S // TK, QW, TK), BF),
        jax.ShapeDtypeStruct((S, PW), BF),
        jax.ShapeDtypeStruct((S, PW), BF),
        jax.ShapeDtypeStruct((S, PW), BF),
    )
    in_specs = [
        _rows(ts, D), _full(mod.shape), _full((1, D)), _full(w_in.shape), _full((1, QL)), _full((1, KVL)),
        _full(w_uq.shape), _full(wuk_dc.shape), _full(perm.shape), _rows(ts, 128), _rows(ts, 128), _rows(ts, 128),
        _rows(ts, 128), _full(w_pool.shape), _full((1, PW)),
    ]
    out_specs = (
        _rows(ts, D), _rows(ts, 384), _rows(ts, HEADS * NOPE),
        pl.BlockSpec((nsub, HEADS * TQ, QW), lambda i: (i, 0, 0)),
        _rows(ts, QW), pl.BlockSpec((ts // TK, QW, TK), lambda i: (i, 0, 0)), _rows(ts, PW), _rows(ts, PW),
        _rows(ts, PW),
    )
    return pl.pallas_call(
        body, name="fwd_in", out_shape=out_shape, grid=(S // ts,), in_specs=in_specs, out_specs=out_specs,
        scratch_shapes=[pltpu.VMEM((16, PW), F32), pltpu.VMEM((HEADS * NOPE, HEADS * KVL), BF),
                        pltpu.VMEM((PW, PW), BF)],
        compiler_params=_params(("arbitrary",)),
    )(x, mod, g_mix, w_in, g_q, g_kv, w_uq, wuk_dc, perm, cos4, sin4, csk, snk, w_pool, pool_scale)


def _diag_mask(shape, q_axis, first_chunk):
    qi = (lax.broadcasted_iota(jnp.int32, shape, q_axis) & (TQ - 1)) >> 6
    ki = (lax.broadcasted_iota(jnp.int32, shape, 1 - q_axis) >> 6) + first_chunk
    return ki <= qi


def _attn_fwd(qs, kv, kvt, wuv_vc):
    nq = qs.shape[0]
    S = kv.shape[0]
    M = HEADS * TQ

    def body(qs_ref, kv_ref, kvt_ref, wuv_ref, olat_ref, ymla_ref, lse_ref):
        i = pl.program_id(0)
        q = qs_ref[0]

        def step(kt, carry, first_chunk=None):
            m, l, acc = carry
            k = kv_ref[pl.ds(pl.multiple_of(kt * TK, TK), TK), :]
            v_t = kvt_ref[kt][0:KVL, :]
            s = _dot_nt(k, q)
            if first_chunk is not None:
                s = jnp.where(_diag_mask((TK, M), 1, first_chunk), s, -jnp.inf)
            m_new = jnp.maximum(m, jnp.max(s, axis=0, keepdims=True))
            alpha = jnp.exp2((m - m_new) * EXP2_SCALE)
            p = jnp.exp2((s - m_new) * EXP2_SCALE)
            l = alpha * l + jnp.sum(p, axis=0, keepdims=True)
            acc = alpha * acc + _dot(v_t, p.astype(BF))
            return m_new, l, acc

        init = (jnp.full((1, M), -jnp.inf, F32), jnp.zeros((1, M), F32), jnp.zeros((KVL, M), F32))
        per = TQ // TK
        carry = lax.fori_loop(0, per * i, step, init)
        for j in range(per):
            carry = step(per * i + j, carry, j * (TK // 64))
        m, l, acc = carry
        o_lat = acc / l
        olat_ref[0] = o_lat
        lse_ref[0] = jnp.broadcast_to(m * SM_SCALE + jnp.log(l), (8, M))
        for hd in range(HEADS):
            o_t = _dot(wuv_ref[hd], o_lat[:, hd * TQ:(hd + 1) * TQ].astype(BF))
            ymla_ref[:, hd * 128:(hd + 1) * 128] = o_t.T.astype(BF)

    out_shape = (
        jax.ShapeDtypeStruct((nq, KVL, M), F32),
        jax.ShapeDtypeStruct((S, HEADS * 128), BF),
        jax.ShapeDtypeStruct((nq, 8, M), F32),
    )
    return pl.pallas_call(
        body, name="attn_fwd", out_shape=out_shape, grid=(nq,),
        in_specs=[pl.BlockSpec((1, M, QW), lambda i: (i, 0, 0)), _full(kv.shape), _full(kvt.shape),
                  _full(wuv_vc.shape)],
        out_specs=(pl.BlockSpec((1, KVL, M), lambda i: (i, 0, 0)), _rows(TQ, HEADS * 128),
                   pl.BlockSpec((1, 8, M), lambda i: (i, 0, 0))),
        compiler_params=_params(("arbitrary",)),
    )(qs, kv, kvt, wuv_vc)


def _silu_parts(a):
    sg = jax.nn.sigmoid(a)
    return sg, a * sg


def _ffn_fwd_bwd(x, ymla, ypool, mod, w_o, g_ffn, wg_t, wu_t, wd, g_final, target):
    S = x.shape[0]
    ts = 256

    def body(x_ref, ymla_ref, ypool_ref, mod_ref, wo_ref, gffn_ref, wg_ref, wu_ref, wd_ref, gfin_ref, t_ref,
             h2t_ref, a_ref, b_ref, da_ref, db_ref, dfft_ref, dx2_ref, dmix_ref, catt_ref, dymla_ref, dypool_ref,
             loss_ref, dgfin_ref, dgt2_ref, dgt1_ref, dsc2_ref, dsh2_ref, dgffn_ref, f_ref):
        i = pl.program_id(0)

        @pl.when(i == 0)
        def _():
            for r in (loss_ref, dgfin_ref, dgt2_ref, dgt1_ref, dsc2_ref, dsh2_ref, dgffn_ref):
                r[...] = jnp.zeros_like(r)

        gt1 = mod_ref[0:1, 2 * D:3 * D]
        sh2 = mod_ref[0:1, 3 * D:4 * D]
        sc2 = mod_ref[0:1, 4 * D:5 * D]
        gt2 = mod_ref[0:1, 5 * D:6 * D]
        gffn = gffn_ref[...]
        cat = jnp.concatenate([ymla_ref[...], ypool_ref[...]], axis=1)
        catt_ref[...] = cat.astype(F32).T.astype(BF)
        mix = _dot(cat, wo_ref[...])
        x2 = x_ref[...] + gt1 * mix
        r2 = _rms(x2)
        xn2 = x2 * r2
        h2 = xn2 * gffn * (1.0 + sc2) + sh2
        h2b = h2.astype(BF)
        h2t_ref[...] = h2.T.astype(BF)

        for c in range(FF // FCHUNK):
            cols = slice(c * FCHUNK, (c + 1) * FCHUNK)
            a = _dot_nt(h2b, wg_ref[cols, :])
            b = _dot_nt(h2b, wu_ref[cols, :])
            a_ref[:, cols] = a.astype(BF)
            b_ref[:, cols] = b.astype(BF)
            f_ref[:, cols] = (_silu_parts(a)[1] * b).astype(BF)
        ff = _dot(f_ref[...], wd_ref[...])

        x3 = x2 + gt2 * ff
        r3 = _rms(x3)
        xn3 = x3 * r3
        gfin = gfin_ref[...]
        e = xn3 * gfin - t_ref[...]
        loss_ref[...] += 0.5 * jnp.sum(jnp.mean(e * e, axis=-1, keepdims=True))
        dy = e * (1.0 / D)
        dgfin_ref[...] += _colsum(dy * xn3)
        dx3 = _rms_bwd(dy * gfin, xn3, r3)
        dgt2_ref[...] += _colsum(dx3 * ff)
        dff = dx3 * gt2
        dffb = dff.astype(BF)
        dfft_ref[...] = dff.T.astype(BF)

        for c in range(FF // FCHUNK):
            cols = slice(c * FCHUNK, (c + 1) * FCHUNK)
            df = _dot_nt(dffb, wd_ref[cols, :])
            av = a_ref[:, cols].astype(F32)
            bv = b_ref[:, cols].astype(F32)
            sg, sa = _silu_parts(av)
            db_ref[:, cols] = (df * sa).astype(BF)
            da_ref[:, cols] = (df * bv * (sg * (1.0 + av * (1.0 - sg)))).astype(BF)
        dh2 = _dot(da_ref[...], wg_ref[...]) + _dot(db_ref[...], wu_ref[...])

        along = _colsum(dh2 * xn2)
        dsc2_ref[...] += along * gffn
        dsh2_ref[...] += _colsum(dh2)
        dgffn_ref[...] += along * (1.0 + sc2)
        dx2 = dx3 + _rms_bwd(dh2 * (gffn * (1.0 + sc2)), xn2, r2)
        dx2_ref[...] = dx2
        dgt1_ref[...] += _colsum(dx2 * mix)
        dmix = (dx2 * gt1).astype(BF)
        dmix_ref[...] = dmix
        dcat = _dot_nt(dmix, wo_ref[...])
        dymla_ref[...] = dcat[:, 0:PW].astype(BF)
        dypool_ref[...] = dcat[:, PW:2 * PW]

    row = lambda c: _rows(ts, c)
    col = pl.BlockSpec((D, ts), lambda i: (0, i))
    const = _full
    vec = jax.ShapeDtypeStruct((1, D), F32)
    out_shape = (
        jax.ShapeDtypeStruct((D, S), BF),
        jax.ShapeDtypeStruct((S, FF), BF),
        jax.ShapeDtypeStruct((S, FF), BF),
        jax.ShapeDtypeStruct((S, FF), BF),
        jax.ShapeDtypeStruct((S, FF), BF),
        jax.ShapeDtypeStruct((D, S), BF),
        jax.ShapeDtypeStruct((S, D), F32),
        jax.ShapeDtypeStruct((S, D), BF),
        jax.ShapeDtypeStruct((D, S), BF),
        jax.ShapeDtypeStruct((S, PW), BF),
        jax.ShapeDtypeStruct((S, PW), F32),
        jax.ShapeDtypeStruct((8, 128), F32),
        vec, vec, vec, vec, vec, vec,
    )
    return pl.pallas_call(
        body, name="ffn_fwd_bwd", out_shape=out_shape, grid=(S // ts,),
        in_specs=[row(D), row(PW), row(PW), const(mod.shape), _vmem(), const((1, D)), _vmem(), _vmem(), _vmem(),
                  const((1, D)), row(D)],
        out_specs=(col, row(FF), row(FF), row(FF), row(FF), col, row(D), row(D), col, row(PW), row(PW),
                   const((8, 128))) + (const((1, D)),) * 6,
        scratch_shapes=[pltpu.VMEM((ts, FF), BF)],
        compiler_params=_params(("arbitrary",)),
    )(x, ymla, ypool, mod, w_o, g_ffn, wg_t, wu_t, wd, g_final, target)


FCHUNK = 256


def _ffn_bwd_weights(dff_t, h2_t, da, db, a, b):
    S = da.shape[0]

    def body(dfft_ref, h2t_ref, da_ref, db_ref, a_ref, b_ref, dwg_ref, dwu_ref, dwd_ref):
        h2t = h2t_ref[...]
        dwg_ref[...] = _dot(h2t, da_ref[...]).T.astype(BF)
        dwu_ref[...] = _dot(h2t, db_ref[...]).T.astype(BF)
        f = (_silu_parts(a_ref[...].astype(F32))[1] * b_ref[...].astype(F32)).astype(BF)
        dwd_ref[...] = _dot(dfft_ref[...], f).T.astype(BF)

    act = pl.BlockSpec((S, FCHUNK), lambda j: (0, j))
    wblk = _rows(FCHUNK, D)
    shp = jax.ShapeDtypeStruct((FF, D), BF)
    return pl.pallas_call(
        body, name="ffn_bwd_weights", out_shape=(shp, shp, shp), grid=(FF // FCHUNK,),
        in_specs=[_vmem(), _vmem(), act, act, act, act], out_specs=(wblk, wblk, wblk),
        compiler_params=_params(("arbitrary",)),
    )(dff_t, h2_t, da, db, a, b)


def _wo_grad(cat_t, dmix):
    S = dmix.shape[0]
    tc = 256

    def body(catt_ref, dmix_ref, dwo_ref):
        dwo_ref[...] = _dot(catt_ref[...], dmix_ref[...]).astype(BF)

    return pl.pallas_call(
        body, name="wo_grad", out_shape=jax.ShapeDtypeStruct((D, D), BF), grid=(D // tc,),
        in_specs=[_vmem(), pl.BlockSpec((S, tc), lambda j: (0, j))], out_specs=pl.BlockSpec((D, tc), lambda j: (0, j)),
        compiler_params=_params(("arbitrary",)),
    )(cat_t, dmix)


def _mix_bwd(dymla, dypool, ypre, pooled, pool_scale, wpool_dc, olat, wuv_vc):
    S = dymla.shape[0]
    ts = 512
    n = S // ts
    nsub = ts // TQ
    M = HEADS * TQ

    def body(dymla_ref, dypool_ref, ypre_ref, pooled_ref, pscale_ref, wpool_ref, olat_ref, wuv_ref,
             du_ref, dolat_ref, delta_ref, dwuv_ref, dwpool_ref, dpscale_ref, carry_ref, dwpool_acc, wpool_bd,
             wuv_bd):
        i = pl.program_id(0)

        @pl.when(i == 0)
        def _():
            carry_ref[...] = jnp.zeros_like(carry_ref)
            dwpool_acc[...] = jnp.zeros_like(dwpool_acc)
            _fill_block_diagonal(wpool_bd, wpool_ref)
            _fill_block_diagonal(wuv_bd, wuv_ref)
            for r in (dwuv_ref, dpscale_ref):
                r[...] = jnp.zeros_like(r)

        dypool = dypool_ref[...]
        dpscale_ref[...] += _colsum(dypool * ypre_ref[...].astype(F32))
        dypre = (dypool * pscale_ref[...]).astype(BF)
        dwpool_acc[...] += _dot_tn(pooled_ref[...], dypre)
        dpooled = _dot(dypre, wpool_bd[...])
        tile = n - 1 - i
        e = dpooled / _row_counts(tile * ts, ts)
        ext = jnp.concatenate([e, carry_ref[...]], axis=0)
        du_ref[...] = (_window_sums(ext, False)[0:ts, :] - dpooled).astype(BF)
        carry_ref[...] = e[0:16, :]

        dob_all = dymla_ref[...]
        dol_all = _dot(dob_all, wuv_bd[...])
        for hd in range(HEADS):
            dob = dob_all[:, hd * 128:(hd + 1) * 128]
            dol = dol_all[:, hd * 128:(hd + 1) * 128]
            for a in range(nsub):
                ol_t = olat_ref[a, :, hd * TQ:(hd + 1) * TQ]
                dl = dol[a * TQ:(a + 1) * TQ, :]
                dolat_ref[a, hd * TQ:(hd + 1) * TQ, :] = dl.astype(BF)
                dwuv_ref[hd] += _dot(ol_t.astype(BF), dob[a * TQ:(a + 1) * TQ, :])
                delta = jnp.sum(dl * ol_t.T, axis=-1, keepdims=True)
                delta_ref[a, :, hd * TQ:(hd + 1) * TQ] = jnp.broadcast_to(delta, (TQ, 128)).T[0:8, :]

        @pl.when(i == n - 1)
        def _():
            for g in range(GROUPS):
                dwpool_ref[g] = dwpool_acc[g * GD:(g + 1) * GD, g * GD:(g + 1) * GD]

    rev = lambda c: pl.BlockSpec((ts, c), lambda i: (n - 1 - i, 0))
    rev3 = lambda r, c: pl.BlockSpec((nsub, r, c), lambda i: (n - 1 - i, 0, 0))
    out_shape = (
        jax.ShapeDtypeStruct((S, PW), BF),
        jax.ShapeDtypeStruct((S // TQ, M, KVL), BF),
        jax.ShapeDtypeStruct((S // TQ, 8, M), F32),
        jax.ShapeDtypeStruct((HEADS, KVL, 128), F32),
        jax.ShapeDtypeStruct((GROUPS, GD, GD), F32),
        jax.ShapeDtypeStruct((1, PW), F32),
    )
    in_specs = [rev(PW), rev(PW), rev(PW), rev(PW), _full((1, PW)), _full(wpool_dc.shape), rev3(KVL, M),
                _full(wuv_vc.shape)]
    out_specs = (rev(PW), rev3(M, KVL), rev3(8, M), _full((HEADS, KVL, 128)), _full((GROUPS, GD, GD)),
                 _full((1, PW)))
    return pl.pallas_call(
        body, name="mix_bwd", out_shape=out_shape, grid=(n,), in_specs=in_specs, out_specs=out_specs,
        scratch_shapes=[pltpu.VMEM((16, PW), F32), pltpu.VMEM((PW, PW), F32), pltpu.VMEM((PW, PW), BF),
                        pltpu.VMEM((HEADS * 128, HEADS * KVL), BF)],
        compiler_params=_params(("arbitrary",)),
    )(dymla, dypool, ypre, pooled, pool_scale, wpool_dc, olat, wuv_vc)


def _attn_bwd(qs, kv, dolat, lse, delta):
    nq = qs.shape[0]
    S = kv.shape[0]
    M = HEADS * TQ
    nk = S // TK

    def body(qs_ref, kv_ref, do_ref, lse_ref, delta_ref, dkv_ref, dqt_out_ref, dqt_ref, p_ref, ds_ref):
        kt = pl.program_id(0)
        k = kv_ref[...]
        v = k[:, 0:KVL]
        k_t = k.astype(F32).T.astype(BF)

        @pl.when(kt == 0)
        def _():
            dqt_ref[...] = jnp.zeros_like(dqt_ref)

        def step(qi, carry, first_chunk=None):
            dk, dv = carry
            q = qs_ref[qi]
            do = do_ref[qi]
            s = _dot_nt(k, q)
            dp = _dot_nt(v, do)
            lse_row = lse_ref[qi, 0:1, :] * LOG2_E
            delta_row = delta_ref[qi, 0:1, :]
            q_chunk = (lax.broadcasted_iota(jnp.int32, (1, M), 1) & (TQ - 1)) >> 6
            for r in range(0, TK, VPU_ROWS):
                rows = slice(r, r + VPU_ROWS)
                p = jnp.exp2(s[rows, :] * EXP2_SCALE - lse_row)
                if first_chunk is not None:
                    p = jnp.where((r >> 6) + first_chunk <= q_chunk, p, 0.0)
                p_ref[rows, :] = p.astype(BF)
                ds_ref[rows, :] = (p * (dp[rows, :] - delta_row) * SM_SCALE).astype(BF)
            ds = ds_ref[...]
            dv = dv + _dot(p_ref[...], do)
            dk = dk + _dot(ds, q)
            dqt_ref[qi] += _dot(k_t, ds)
            return dk, dv

        per = TQ // TK
        first = kt // per
        carry = step(first, (jnp.zeros((TK, QW), F32), jnp.zeros((TK, KVL), F32)), (kt % per) * (TK // 64))
        dk, dv = lax.fori_loop(first + 1, nq, step, carry)
        dkv_ref[...] = dk + jnp.concatenate([dv, jnp.zeros((TK, QW - KVL), F32)], axis=1)
        dqt_out_ref[0] = dqt_ref[first].astype(BF)

    out_shape = (jax.ShapeDtypeStruct((S, QW), F32), jax.ShapeDtypeStruct((nq, QW, M), BF))
    return pl.pallas_call(
        body, name="attn_bwd", out_shape=out_shape, grid=(nk,),
        in_specs=[_vmem(), _rows(TK, QW), _vmem(), _vmem(), _vmem()],
        out_specs=(_rows(TK, QW), pl.BlockSpec((1, QW, M), lambda kt: (kt // (TQ // TK), 0, 0))),
        scratch_shapes=[pltpu.VMEM((nq, QW, M), F32), pltpu.VMEM((TK, M), BF), pltpu.VMEM((TK, M), BF)],
        compiler_params=_params(("arbitrary",)),
    )(qs, kv, dolat, lse, delta)


def _in_bwd(dqt, dkv, du, raw, qn, h1, x, dx2, mod, g_mix, w_in, g_q, g_kv, w_uq, wuk_cd, perm_t, cos4, sin4, csk,
            snk):
    S = x.shape[0]
    ts = 512
    n = S // ts
    nsub = ts // TQ
    M = HEADS * TQ

    def body(dqt_ref, dkv_ref, du_ref, raw_ref, qn_ref, h1_ref, x_ref, dx2_ref, mod_ref, gmix_ref, win_ref, gq_ref,
             gkv_ref, wuq_ref, wuk_ref, permt_ref, cos_ref, sin_ref, csk_ref, snk_ref,
             dx_ref, dwin_ref, dwuq_ref, dwuk_ref, dgq_ref, dgkv_ref, dsc1_ref, dsh1_ref, dgmix_ref, dwin_acc,
             dwuq_acc, dwuk_acc, wuk_bd):
        i = pl.program_id(0)

        @pl.when(i == 0)
        def _():
            dwin_acc[...] = jnp.zeros_like(dwin_acc)
            dwuq_acc[...] = jnp.zeros_like(dwuq_acc)
            dwuk_acc[...] = jnp.zeros_like(dwuk_acc)
            _fill_block_diagonal(wuk_bd, wuk_ref)
            for r in (dgq_ref, dgkv_ref, dsc1_ref, dsh1_ref, dgmix_ref):
                r[...] = jnp.zeros_like(r)

        dq_blocks = [dqt_ref[a].astype(F32).T for a in range(nsub)]
        dq_heads = [jnp.concatenate([blk[hd * TQ:(hd + 1) * TQ, :] for blk in dq_blocks], axis=0)
                    for hd in range(HEADS)]
        dq_lat = jnp.concatenate([dqh[:, 0:KVL] for dqh in dq_heads], axis=1).astype(BF)
        dq_rope = jnp.concatenate([dqh[:, KVL:QW] for dqh in dq_heads], axis=1).astype(BF)
        dq_nope = _dot(dq_lat, wuk_bd[...])
        dwuk_acc[...] += _dot_tn(dq_lat, qn_ref[...])
        drope = _dot(dq_rope, permt_ref[...])
        do1 = drope[:, 0:128]
        do2 = drope[:, 128:256]
        cosv = cos_ref[...]
        sinv = sin_ref[...]
        dq = jnp.concatenate([dq_nope, do1 * cosv + do2 * sinv, do2 * cosv - do1 * sinv], axis=1).astype(BF)

        cq_raw = raw_ref[:, 0:QL]
        ckv_raw = raw_ref[:, QL:QL + KVL]
        rq = _rms(cq_raw)
        nq_ = cq_raw * rq
        gq = gq_ref[...]
        dwuq_acc[...] += _dot_tn((nq_ * gq).astype(BF), dq)
        dc_q = _dot_nt(dq, wuq_ref[...])
        dgq_ref[...] += _colsum(dc_q * nq_)
        dcq_raw = _rms_bwd(dc_q * gq, nq_, rq)

        dkv = dkv_ref[...]
        rk = _rms(ckv_raw)
        nk_ = ckv_raw * rk
        dc_kv = dkv[:, 0:KVL]
        dgkv_ref[...] += _colsum(dc_kv * nk_)
        dckv_raw = _rms_bwd(dc_kv * gkv_ref[...], nk_, rk)
        dkr_roped = dkv[:, KVL:QW]
        dkr = dkr_roped * csk_ref[...] - _swap_halves(dkr_roped) * snk_ref[...]

        dproj = jnp.concatenate([dcq_raw.astype(BF), dckv_raw.astype(BF), dkr.astype(BF), du_ref[...]], axis=1)
        dwin_acc[...] += _dot_tn(h1_ref[...], dproj)
        dh1 = _dot_nt(dproj, win_ref[...])

        sc1 = mod_ref[0:1, D:2 * D]
        gmix = gmix_ref[...]
        xv = x_ref[...]
        r1 = _rms(xv)
        xn1 = xv * r1
        along = _colsum(dh1 * xn1)
        dsc1_ref[...] += along * gmix
        dsh1_ref[...] += _colsum(dh1)
        dgmix_ref[...] += along * (1.0 + sc1)
        dx_ref[...] = dx2_ref[...] + _rms_bwd(dh1 * (gmix * (1.0 + sc1)), xn1, r1)

        @pl.when(i == n - 1)
        def _():
            dwin_ref[...] = dwin_acc[...].astype(BF)
            dwuq_ref[...] = dwuq_acc[...].astype(BF)
            for hd in range(HEADS):
                dwuk_ref[hd] = dwuk_acc[hd * KVL:(hd + 1) * KVL, hd * NOPE:(hd + 1) * NOPE]

    out_shape = (
        jax.ShapeDtypeStruct((S, D), F32),
        jax.ShapeDtypeStruct((D, D), BF),
        jax.ShapeDtypeStruct((QL, 768), BF),
        jax.ShapeDtypeStruct((HEADS, KVL, NOPE), F32),
        jax.ShapeDtypeStruct((1, QL), F32), jax.ShapeDtypeStruct((1, KVL), F32),
        jax.ShapeDtypeStruct((1, D), F32), jax.ShapeDtypeStruct((1, D), F32), jax.ShapeDtypeStruct((1, D), F32),
    )
    in_specs = [pl.BlockSpec((nsub, QW, M), lambda i: (i, 0, 0)), _rows(ts, QW), _rows(ts, PW), _rows(ts, 384),
                _rows(ts, HEADS * NOPE), _rows(ts, D), _rows(ts, D), _rows(ts, D), _full(mod.shape), _full((1, D)),
                _full(w_in.shape), _full((1, QL)), _full((1, KVL)), _full(w_uq.shape), _full(wuk_cd.shape),
                _full(perm_t.shape), _rows(ts, 128), _rows(ts, 128), _rows(ts, 128), _rows(ts, 128)]
    out_specs = (_rows(ts, D), _full((D, D)), _full((QL, 768)), _full((HEADS, KVL, NOPE)), _full((1, QL)),
                 _full((1, KVL)), _full((1, D)), _full((1, D)), _full((1, D)))
    return pl.pallas_call(
        body, name="in_bwd", out_shape=out_shape, grid=(n,), in_specs=in_specs, out_specs=out_specs,
        scratch_shapes=[pltpu.VMEM((D, D), F32), pltpu.VMEM((QL, 768), F32),
                        pltpu.VMEM((HEADS * KVL, HEADS * NOPE), F32), pltpu.VMEM((HEADS * KVL, HEADS * NOPE), BF)],
        compiler_params=_params(("arbitrary",)),
    )(dqt, dkv, du, raw, qn, h1, x, dx2, mod, g_mix, w_in, g_q, g_kv, w_uq, wuk_cd, perm_t, cos4, sin4, csk, snk)


def _rope_perm():
    p = np.zeros((HEADS, 2 * 128, 128), np.float32)
    for hd in range(HEADS):
        for t in range(HALF):
            p[hd, hd * HALF + t, t] = 1.0
            p[hd, 128 + hd * HALF + t, HALF + t] = 1.0
    return p


def _rope_tables(positions):
    freqs = jnp.power(ROPE_THETA, -jnp.arange(HALF, dtype=F32) / HALF)
    rows = positions.shape[0]
    ang = (positions.astype(F32).reshape(rows // HEADS, HEADS, 1) * freqs).reshape(rows // HEADS, HEADS * HALF)
    cos4 = jnp.tile(jnp.cos(ang).reshape(rows, HALF), (1, HEADS))
    sin4 = jnp.tile(jnp.sin(ang).reshape(rows, HALF), (1, HEADS))
    lane = jnp.arange(HEADS * HALF)[None, :]
    csk = jnp.where(lane < ROPE, cos4, 0.0)
    snk = jnp.where(lane < HALF, -sin4, jnp.where(lane < ROPE, sin4, 0.0))
    return cos4, sin4, csk, snk


def _local_step(x, rope, target, mod, g_mix, w_in_p, g_q, g_kv, w_uq_p, w_uk, w_uv, w_pool, pool_scale, g_ffn,
                g_final, late, ffn_grads_exchange):
    perm = jnp.asarray(_rope_perm().transpose(1, 0, 2).reshape(2 * 128, HEADS * 128), BF)
    perm_t = jnp.asarray(_rope_perm().transpose(0, 2, 1).reshape(HEADS * 128, 2 * 128), BF)
    cos4, sin4, csk, snk = rope
    wuk_dc = w_uk.transpose(1, 2, 0).astype(BF)
    wuk_cd = w_uk.transpose(1, 0, 2).astype(BF)
    wuv_vc = w_uv.transpose(1, 2, 0).astype(BF)
    wpool = w_pool.astype(BF)
    wpool_dc = w_pool.transpose(0, 2, 1).astype(BF)

    h1, raw, qn, qs, kv, kvt, pooled, ypre, ypool = _fwd_in(
        x, mod, g_mix, w_in_p, g_q, g_kv, w_uq_p, wuk_dc, perm, cos4, sin4, csk, snk, wpool, pool_scale)
    olat, ymla, lse = _attn_fwd(qs, kv, kvt, wuv_vc)
    w_o, wg_t, wu_t, wd = late
    (h2_t, a, b, da, db, dff_t, dx2, dmix, cat_t, dymla, dypool, loss, dgfin, dgt2, dgt1, dsc2, dsh2,
     dgffn) = _ffn_fwd_bwd(x, ymla, ypool, mod, w_o, g_ffn, wg_t, wu_t, wd, g_final, target)
    dwo = _wo_grad(cat_t, dmix)
    dwg_t, dwu_t, dwd = _ffn_bwd_weights(dff_t, h2_t, da, db, a, b)
    ffn_parts = ffn_grads_exchange((dwg_t, dwu_t, dwd, dwo))
    du, dolat, delta, dwuv, dwpool, dpscale = _mix_bwd(
        dymla, dypool, ypre, pooled, pool_scale, wpool_dc, olat, wuv_vc)
    dkv, dqt = _attn_bwd(qs, kv, dolat, lse, delta)
    dx, dwin, dwuq, dwuk, dgq, dgkv, dsc1, dsh1, dgmix = _in_bwd(
        dqt, dkv, du, raw, qn, h1, x, dx2, mod, g_mix, w_in_p, g_q, g_kv, w_uq_p, wuk_cd, perm_t, cos4, sin4, csk,
        snk)
    dmod = jnp.concatenate([dsh1, dsc1, dgt1, dsh2, dsc2, dgt2], axis=1)
    replicated = dict(
        w_uk=dwuk.transpose(1, 0, 2), w_uv=dwuv.transpose(1, 0, 2), w_pool=dwpool, g_mix=dgmix, g_q=dgq, g_kv=dgkv,
        pool_scale=dpscale, g_ffn=dgffn, g_final=dgfin)
    return loss[0, 0], dx, dmod, (dwin, dwuq), ffn_parts, replicated


def _my_pos():
    return lax.axis_index("x"), lax.axis_index("y"), lax.axis_index("c")


def _peer(pos, k):
    x, y, c = pos
    return (1 - x if k & 4 else x, 1 - y if k & 2 else y, 1 - c if k & 1 else c)


def _index(pos):
    x, y, c = pos
    return 4 * x + 2 * y + c


def _remote(src, dst, send_sem, recv_sem, to):
    return pltpu.make_async_remote_copy(src_ref=src, dst_ref=dst, send_sem=send_sem, recv_sem=recv_sem,
                                        device_id=to, device_id_type=MESH)


def _ada_mod(c, w_ada, b_ada, after):
    def body(c_ref, w_ref, b_ref, after_ref, mod_ref, call_ref, cbuf, sbuf, rbuf, send1, recv1, send2, recv2):
        me = _my_pos()
        mi = _index(me)
        cv = c_ref[...]
        cbuf[...] = jnp.broadcast_to(cv * jax.nn.sigmoid(cv), (8, D))
        call_ref[mi] = cbuf[...]
        first = [_remote(cbuf, call_ref.at[mi], send1.at[k - 1], recv1.at[k - 1], _peer(me, k)) for k in range(1, NDEV)]
        for cp in first:
            cp.start()
        for k in range(1, NDEV):
            _remote(cbuf, call_ref.at[_index(_peer(me, k))], send1.at[k - 1], recv1.at[k - 1], _peer(me, k)).wait_recv()
        c_all = jnp.concatenate([call_ref[b][0:1, :] for b in range(NDEV)], axis=0)
        blocks = _dot(c_all.astype(BF), w_ref[...].astype(BF))
        for b in range(NDEV):
            sbuf[b] = jnp.broadcast_to(blocks[b:b + 1, :], (8, MODC))
        second = []
        for k in range(1, NDEV):
            to = _peer(me, k)
            second.append(_remote(sbuf.at[_index(to)], rbuf.at[mi], send2.at[k - 1], recv2.at[k - 1], to))
        for cp in second:
            cp.start()
        rbuf[mi] = sbuf[mi]
        for k in range(1, NDEV):
            to = _peer(me, k)
            _remote(sbuf.at[_index(to)], rbuf.at[_index(to)], send2.at[k - 1], recv2.at[k - 1], to).wait_recv()
        for j in range(NDEV):
            mod_ref[:, j * MODC:(j + 1) * MODC] = rbuf[j] + b_ref[:, j * MODC:(j + 1) * MODC]
        for cp in first + second:
            cp.wait_send()

    return pl.pallas_call(
        body, name="ada_mod",
        out_shape=(jax.ShapeDtypeStruct((8, N_MOD * D), F32), jax.ShapeDtypeStruct((NDEV, 8, D), F32)),
        in_specs=[_vmem(), _vmem(), _vmem(), _any()], out_specs=(_vmem(), _vmem()),
        scratch_shapes=[pltpu.VMEM((8, D), F32), pltpu.VMEM((NDEV, 8, MODC), F32), pltpu.VMEM((NDEV, 8, MODC), F32),
                        pltpu.SemaphoreType.DMA((NDEV - 1,)), pltpu.SemaphoreType.DMA((NDEV - 1,)),
                        pltpu.SemaphoreType.DMA((NDEV - 1,)), pltpu.SemaphoreType.DMA((NDEV - 1,))],
        compiler_params=_params(),
    )(c, w_ada, b_ada, after)


def _sequencer_scatter(name, collective_id, srcs, after=()):
    n = len(srcs)

    def of(src, to_index):
        r = src.shape[0] // NDEV
        return src.at[pl.ds(pl.multiple_of(to_index * r, 16), r), :]

    def body(*refs):
        src, zone = refs[:n], refs[n + len(after):2 * n + len(after)]
        send, recv, local = refs[2 * n + len(after):]
        me = _my_pos()
        mi = _index(me)
        barrier = pltpu.get_barrier_semaphore()
        for k in range(1, NDEV):
            pl.semaphore_signal(barrier, inc=1, device_id=_peer(me, k), device_id_type=MESH)
        pl.semaphore_wait(barrier, NDEV - 1)
        own = [pltpu.make_async_copy(of(src[a], mi), zone[a].at[mi], local.at[a]) for a in range(n)]
        for cp in own:
            cp.start()
        for a in range(n):
            for k in range(1, NDEV):
                to = _peer(me, k)
                s = a * (NDEV - 1) + k - 1
                _remote(of(src[a], _index(to)), zone[a].at[mi], send.at[s], recv.at[s], to).start()
        for cp in own:
            cp.wait()
        for a in range(n):
            for k in range(1, NDEV):
                to = _peer(me, k)
                s = a * (NDEV - 1) + k - 1
                cp = _remote(of(src[a], mi), zone[a].at[_index(to)], send.at[s], recv.at[s], to)
                cp.wait_send()
                cp.wait_recv()

    return pl.kernel(
        body, name=name, mesh=plsc.ScalarSubcoreMesh(axis_name="sequencer", num_cores=1),
        out_type=tuple(jax.ShapeDtypeStruct((NDEV, s.shape[0] // NDEV, s.shape[1]), s.dtype) for s in srcs),
        scratch_types=[pltpu.SemaphoreType.DMA((n * (NDEV - 1),)), pltpu.SemaphoreType.DMA((n * (NDEV - 1),)),
                       pltpu.SemaphoreType.DMA((n,))],
        compiler_params=pltpu.CompilerParams(collective_id=collective_id),
    )(*srcs, *after)


CHIP_PEERS = (2, 4, 6)


def _sequencer_gather(name, collective_id, srcs, after=()):
    n = len(srcs)
    per = NDEV - 1

    def body(*refs):
        src, zone = refs[:n], refs[n + len(after):2 * n + len(after)]
        send, recv, local = refs[2 * n + len(after):]
        me = _my_pos()
        mi = _index(me)
        sibling = _peer(me, 1)
        talk_to = (sibling,) + tuple(_peer(me, k) for k in CHIP_PEERS)
        barrier = pltpu.get_barrier_semaphore()
        for to in talk_to:
            pl.semaphore_signal(barrier, inc=1, device_id=to, device_id_type=MESH)
        pl.semaphore_wait(barrier, len(talk_to))

        def copy(a, slot, block_of, to, from_src=False):
            rows = zone[a].at[_index(block_of)]
            return _remote(src[a] if from_src else rows, rows, send.at[a * per + slot], recv.at[a * per + slot], to)

        own = [pltpu.make_async_copy(src[a], zone[a].at[mi], local.at[a]) for a in range(n)]
        for cp in own:
            cp.start()
        started = []
        for a in range(n):
            started.append(copy(a, 0, me, sibling, from_src=True))
            started += [copy(a, 1 + j, me, _peer(me, k), from_src=True) for j, k in enumerate(CHIP_PEERS)]
        for cp in started:
            cp.start()
        for a in range(n):
            for j, k in enumerate(CHIP_PEERS):
                copy(a, 1 + j, _peer(me, k), me).wait_recv()
                passed = copy(a, 4 + j, _peer(me, k), sibling)
                passed.start()
                started.append(passed)
        for a in range(n):
            copy(a, 0, sibling, me).wait_recv()
            for j, k in enumerate(CHIP_PEERS):
                copy(a, 4 + j, _peer(me, k | 1), me).wait_recv()
        for cp in started:
            cp.wait_send()
        for cp in own:
            cp.wait()

    return pl.kernel(
        body, name=name, mesh=plsc.ScalarSubcoreMesh(axis_name="sequencer", num_cores=1),
        out_type=tuple(jax.ShapeDtypeStruct((NDEV,) + s.shape, s.dtype) for s in srcs),
        scratch_types=[pltpu.SemaphoreType.DMA((n * per,)), pltpu.SemaphoreType.DMA((n * per,)),
                       pltpu.SemaphoreType.DMA((n,))],
        compiler_params=pltpu.CompilerParams(collective_id=collective_id),
    )(*srcs, *after)


def _blocked(shape, nb, axis=0):
    block = tuple(s // nb if d == axis else s for d, s in enumerate(shape))
    return pl.BlockSpec(block, lambda i: tuple(i if d == axis else 0 for d in range(len(shape))))


def _sum_partials(name, parts, nb):
    n = len(parts)

    def body(*refs):
        for a in range(n):
            acc = refs[a][0].astype(F32)
            for p in range(1, NDEV):
                acc = acc + refs[a][p].astype(F32)
            refs[n + a][...] = acc

    return pl.pallas_call(
        body, name=name, grid=(nb,),
        out_shape=tuple(jax.ShapeDtypeStruct(p.shape[1:], F32) for p in parts),
        in_specs=[_blocked(p.shape, nb, 1) for p in parts],
        out_specs=tuple(_blocked(p.shape[1:], nb) for p in parts), compiler_params=_params(("arbitrary",)),
    )(*parts)


def _small_all_reduce(buf):
    def body(buf_ref, got_ref, red_ref, mine, send1, recv1, send2, recv2):
        me = _my_pos()
        mi = _index(me)
        first = []
        for k in range(1, NDEV):
            to = _peer(me, k)
            first.append(_remote(buf_ref.at[_index(to)], got_ref.at[mi], send1.at[k - 1], recv1.at[k - 1], to))
        for cp in first:
            cp.start()
        got_ref[mi] = buf_ref[mi]
        for k in range(1, NDEV):
            to = _peer(me, k)
            _remote(buf_ref.at[mi], got_ref.at[_index(to)], send1.at[k - 1], recv1.at[k - 1], to).wait_recv()
        acc = got_ref[0]
        for p in range(1, NDEV):
            acc = acc + got_ref[p]
        mine[...] = acc
        second = [_remote(mine, red_ref.at[mi], send2.at[k - 1], recv2.at[k - 1], _peer(me, k)) for k in range(1, NDEV)]
        for cp in second:
            cp.start()
        red_ref[mi] = acc
        for k in range(1, NDEV):
            to = _peer(me, k)
            _remote(mine, red_ref.at[_index(to)], send2.at[k - 1], recv2.at[k - 1], to).wait_recv()
        for cp in first + second:
            cp.wait_send()

    return pl.pallas_call(
        body, name="small_all_reduce",
        out_shape=(jax.ShapeDtypeStruct(buf.shape, F32), jax.ShapeDtypeStruct(buf.shape, F32)),
        in_specs=[_vmem()], out_specs=(_vmem(), _vmem()),
        scratch_shapes=[pltpu.VMEM(buf.shape[1:], F32),
                        pltpu.SemaphoreType.DMA((NDEV - 1,)), pltpu.SemaphoreType.DMA((NDEV - 1,)),
                        pltpu.SemaphoreType.DMA((NDEV - 1,)), pltpu.SemaphoreType.DMA((NDEV - 1,))],
        compiler_params=_params(),
    )(buf)


def _adamw_math(w, g, m, v):
    m = ADAM_B1 * m + (1.0 - ADAM_B1) * g
    v = ADAM_B2 * v + (1.0 - ADAM_B2) * jnp.square(g)
    m_hat = m / (1.0 - ADAM_B1 ** ADAM_STEP)
    v_hat = v / (1.0 - ADAM_B2 ** ADAM_STEP)
    delta = -ADAM_LR * (m_hat / (jnp.sqrt(v_hat) + ADAM_EPS) + ADAM_WD * w)
    return delta, m, v


def _adamw_group(name, ws, gs, ms, vs, nb):
    n = len(ws)

    def body(*refs):
        for a in range(n):
            w, g, m, v = (refs[q * n + a][...] for q in range(4))
            delta, m2, v2 = _adamw_math(w, g, m, v)
            refs[4 * n + a][...] = delta
            refs[5 * n + a][...] = m2
            refs[6 * n + a][...] = v2

    shapes = tuple(jax.ShapeDtypeStruct(w.shape, F32) for w in ws)
    specs = [_blocked(w.shape, nb) for w in ws]
    outs = pl.pallas_call(
        body, name=name, grid=(nb,), out_shape=shapes * 3, in_specs=specs * 4, out_specs=tuple(specs * 3),
        compiler_params=_params(("arbitrary",)),
    )(*ws, *gs, *ms, *vs)
    return outs[:n], outs[n:2 * n], outs[2 * n:]


def _adamw_from_partials(name, ws, parts, ms, vs, nb):
    n = len(ws)

    def body(*refs):
        for a in range(n):
            part = refs[n + a]
            g = part[0].astype(F32)
            for p in range(1, NDEV):
                g = g + part[p].astype(F32)
            delta, m2, v2 = _adamw_math(refs[a][...], g, refs[2 * n + a][...], refs[3 * n + a][...])
            refs[4 * n + a][...] = g
            refs[5 * n + a][...] = delta
            refs[6 * n + a][...] = m2
            refs[7 * n + a][...] = v2

    shapes = tuple(jax.ShapeDtypeStruct(w.shape, F32) for w in ws)
    specs = [_blocked(w.shape, nb) for w in ws]
    outs = pl.pallas_call(
        body, name=name, grid=(nb,), out_shape=shapes * 4,
        in_specs=specs + [_blocked(p.shape, nb, 1) for p in parts] + specs * 2, out_specs=tuple(specs * 4),
        compiler_params=_params(("arbitrary",)),
    )(*ws, *parts, *ms, *vs)
    return outs[:n], outs[n:2 * n], outs[2 * n:3 * n], outs[3 * n:]


def _adamw_ada(w, m, v, c_all_t, dmod_rows):
    nb = 4

    def body(w_ref, m_ref, v_ref, c_ref, dm_ref, g_ref, d_ref, m2_ref, v2_ref):
        g = _dot(c_ref[...], dm_ref[...].astype(BF))
        g_ref[...] = g
        delta, m2, v2 = _adamw_math(w_ref[...], g, m_ref[...], v_ref[...])
        d_ref[...] = delta
        m2_ref[...] = m2
        v2_ref[...] = v2

    shp = jax.ShapeDtypeStruct(w.shape, F32)
    spec = _blocked(w.shape, nb)
    return pl.pallas_call(
        body, name="adamw_ada", grid=(nb,), out_shape=(shp, shp, shp, shp),
        in_specs=[spec, spec, spec, _blocked(c_all_t.shape, nb), _full(dmod_rows.shape)],
        out_specs=(spec, spec, spec, spec), compiler_params=_params(("arbitrary",)),
    )(w, m, v, c_all_t, dmod_rows)


def _w_in_to_kernel(w):
    return jnp.concatenate([w[:, 0:448], jnp.zeros((w.shape[0], 64), w.dtype), w[:, 448:960]], axis=1)


def _w_in_from_kernel(w):
    return jnp.concatenate([w[:, 0:448], w[:, 512:1024]], axis=1)


def _w_uq_to_kernel(w):
    r = w.shape[0]
    return jnp.concatenate([w[:, :, 0:NOPE].reshape(r, HEADS * NOPE),
                            w[:, :, NOPE:NOPE + HALF].reshape(r, HEADS * HALF),
                            w[:, :, NOPE + HALF:].reshape(r, HEADS * HALF)], axis=1)


def _w_uq_from_kernel(w):
    r = w.shape[0]
    return jnp.concatenate([w[:, 0:512].reshape(r, HEADS, NOPE), w[:, 512:640].reshape(r, HEADS, HALF),
                            w[:, 640:768].reshape(r, HEADS, HALF)], axis=2)


REP_NAMES = ("w_uk", "w_uv", "w_pool", "g_mix", "g_q", "g_kv", "pool_scale", "g_ffn", "g_final")


def kernel(x, c, positions, w_ada, b_ada, g_mix, w_in, g_q, g_kv, w_uq, w_uk, w_uv, w_pool, pool_scale, w_o, g_ffn, w_gate, w_up, w_down, g_final, loss_target, m_w_ada, m_b_ada, m_g_mix, m_w_in, m_g_q, m_g_kv, m_w_uq, m_w_uk, m_w_uv, m_w_pool, m_pool_scale, m_w_o, m_g_ffn, m_w_gate, m_w_up, m_w_down, m_g_final, v_w_ada, v_b_ada, v_g_mix, v_w_in, v_g_q, v_g_kv, v_w_uq, v_w_uk, v_w_uv, v_w_pool, v_pool_scale, v_w_o, v_g_ffn, v_w_gate, v_w_up, v_w_down, v_g_final):
    given = dict(locals())

    merge = lambda g: g.reshape(NDEV * g.shape[1], g.shape[2])
    w_in_p, w_uq_p = (merge(g) for g in _sequencer_gather(
        "gather_in", 3, (_w_in_to_kernel(w_in[0]).astype(BF), _w_uq_to_kernel(w_uq[0]).astype(BF))))

    rope = _rope_tables(positions[0])
    mod, c_all8 = _ada_mod(c, w_ada[0], b_ada, rope[3][0:8, :])
    c_all = c_all8[:, 0, :]
    late = _sequencer_gather(
        "gather_late", 1, (w_o[0].astype(BF), w_gate[0].T.astype(BF), w_up[0].T.astype(BF), w_down[0].astype(BF)),
        after=(mod[:, 0:128], w_in_p[0:16, 0:128], w_uq_p[0:16, 0:128]))

    def ffn_grads_exchange(arrays):
        return _sequencer_scatter("scatter_ffn", 2, arrays)

    loss, dx, dmod, tail_grads, ffn_parts, replicated = _local_step(
        x[0], rope, loss_target[0], mod, g_mix, w_in_p, g_q, g_kv, w_uq_p, w_uk[0], w_uv[0], w_pool[0],
        pool_scale, g_ffn, g_final.reshape(1, D), tuple(merge(g) for g in late), ffn_grads_exchange)

    flat = jnp.concatenate([replicated[k].reshape(-1) for k in REP_NAMES] + [loss.reshape(1)])
    flat = jnp.pad(flat, (0, NDEV * REP_ROWS * 128 - flat.shape[0])).reshape(NDEV, REP_ROWS, 128)
    dmod_blocks = jnp.pad(dmod.reshape(NDEV, MODC // 128, 128), ((0, 0), (0, MOD_ROWS - MODC // 128), (0, 0)))
    got, red = _small_all_reduce(jnp.concatenate([dmod_blocks, flat], axis=1))

    tail_parts = _sequencer_scatter("scatter_tail", 4, tail_grads,
                                    after=(ffn_parts[0][0, 0:16, 0:128], red[0, 0:8, :]))
    g_in_p, g_uq_p = _sum_partials("sum_tail_partials", tail_parts, 1)
    as_transpose = ("w_in", "w_gate", "w_up")
    grads = dict(w_in=_w_in_from_kernel(g_in_p).T, w_uq=_w_uq_from_kernel(g_uq_p))
    partials = dict(w_gate=ffn_parts[0], w_up=ffn_parts[1], w_down=ffn_parts[2], w_o=ffn_parts[3])
    dmod_rows = got[:, 0:MODC // 128, :].reshape(NDEV, MODC)
    grads["b_ada"] = red[:, 0:MODC // 128, :].reshape(1, N_MOD * D)
    rep_flat = red[:, MOD_ROWS:, :].reshape(-1)
    off = 0
    for k in REP_NAMES:
        size = int(np.prod(given[k].shape))
        grads[k] = rep_flat[off:off + size]
        off += size

    view = {k: (given[k].shape[1:] if given[k].ndim > 2 else given[k].shape)
            for k in REP_NAMES + ("b_ada", "w_ada", "w_in", "w_uq", "w_o", "w_gate", "w_up", "w_down")}
    view.update(g_final=(1, D))
    names = ["w_ada", "b_ada", "g_mix", "w_in", "g_q", "g_kv", "w_uq", "w_uk", "w_uv", "w_pool", "pool_scale",
             "w_o", "g_ffn", "w_gate", "w_up", "w_down", "g_final"]
    g_ada, d_ada, m_ada, v_ada = _adamw_ada(w_ada[0], m_w_ada[0], v_w_ada[0], c_all.T.astype(BF), dmod_rows)
    out_g, out_d, out_m, out_v = dict(w_ada=g_ada), dict(w_ada=d_ada), dict(w_ada=m_ada), dict(w_ada=v_ada)
    groups = (("adamw_ffn", ("w_gate", "w_up", "w_down", "w_o"), 4),
              ("adamw_replicated", REP_NAMES + ("b_ada",), 1),
              ("adamw_tail", ("w_in", "w_uq"), 1))
    for gname, members, nb in groups:
        turn = lambda k, t: t.T if k in as_transpose else t
        ws = [turn(k, given[k].reshape(view[k])) for k in members]
        ms = [turn(k, given["m_" + k].reshape(view[k])) for k in members]
        vs = [turn(k, given["v_" + k].reshape(view[k])) for k in members]
        if members[0] in partials:
            gs, ds, m2, v2 = _adamw_from_partials(gname, ws, [partials[k] for k in members], ms, vs, nb)
        else:
            gs = [grads[k] if k in as_transpose else grads[k].reshape(view[k]) for k in members]
            ds, m2, v2 = _adamw_group(gname, ws, gs, ms, vs, nb)
        for k, g, d, mm, vv in zip(members, gs, ds, m2, v2):
            out_g[k], out_d[k], out_m[k], out_v[k] = turn(k, g), turn(k, d), turn(k, mm), turn(k, vv)

    total = rep_flat[off]
    shaped = lambda d: [d[k].reshape(given[k].shape) for k in names]
    return (total, dx[None], *shaped(out_g), *shaped(out_d), *shaped(out_m), *shaped(out_v))
```

```python
import numpy as np
import jax
import jax.numpy as jnp
from jax import lax
from jax.experimental import pallas as pl
from jax.experimental.pallas import tpu as pltpu
from jax.experimental.pallas import tpu_sc as plsc

D = 1024
HEADS = 4
NOPE = 128
ROPE = 64
HALF = ROPE // 2
QL = 256
KVL = 128
FF = 2816
PW = 512
GROUPS = 4
GD = 128
N_MOD = 6
EPS = 1e-6
SM_SCALE = (NOPE + ROPE) ** -0.5
LOG2_E = 1.4426950408889634
EXP2_SCALE = SM_SCALE * LOG2_E
ROPE_THETA = 10000.0
NDEV = 8
MODC = N_MOD * D // NDEV

ADAM_LR = 0.001
ADAM_B1 = 0.9
ADAM_B2 = 0.999
ADAM_EPS = 1e-08
ADAM_WD = 0.01
ADAM_STEP = 10

BF = jnp.bfloat16
F32 = jnp.float32
VMEM_LIMIT_V7X = 60 * 1024 * 1024
MESH = pl.DeviceIdType.MESH

TQ = 512
TK = 512
QW = 256
VPU_ROWS = 16
MOD_ROWS = 8
REP_ROWS = 200
SMALL_ROWS = MOD_ROWS + REP_ROWS


def _params(sem=None):
    return pltpu.CompilerParams(dimension_semantics=sem, vmem_limit_bytes=VMEM_LIMIT_V7X)


def _dot(a, b):
    return jnp.dot(a, b, preferred_element_type=F32)


def _dot_nt(a, b):
    return lax.dot_general(a, b, (((1,), (1,)), ((), ())), preferred_element_type=F32)


def _dot_tn(a, b):
    return _dot(a.astype(F32).T.astype(BF), b)


def _full(shape):
    return pl.BlockSpec(shape, lambda *_: (0,) * len(shape))


def _rows(ts, cols):
    return pl.BlockSpec((ts, cols), lambda i: (i, 0))


def _vmem():
    return pl.BlockSpec(memory_space=pltpu.VMEM)


def _any():
    return pl.BlockSpec(memory_space=pl.ANY)


def _rms(v):
    return lax.rsqrt(jnp.mean(v * v, axis=-1, keepdims=True) + EPS)


def _rms_bwd(dn, n, r):
    return r * (dn - n * jnp.mean(dn * n, axis=-1, keepdims=True))


def _colsum(v):
    return jnp.sum(v, axis=0, keepdims=True)


def _swap_halves(v):
    lane = lax.broadcasted_iota(jnp.int32, v.shape, 1)
    return jnp.where(lane < HALF, pltpu.roll(v, 128 - HALF, 1), pltpu.roll(v, HALF, 1))


def _window_lane_width():
    lane = lax.broadcasted_iota(jnp.int32, (1, PW), 1)
    return jnp.where(lane < 128, 2.0, jnp.where(lane < 256, 4.0, jnp.where(lane < 384, 8.0, 16.0))).astype(F32)


def _window_sums(ext, back):
    n = ext.shape[0]

    def sh(v, k):
        return pltpu.roll(v, k if back else n - k, 0)

    s2 = ext + sh(ext, 1)
    e4 = s2[:, 128:]
    s4 = e4 + sh(e4, 2)
    e8 = s4[:, 128:]
    s8 = e8 + sh(e8, 4)
    e16 = s8[:, 128:]
    s16 = e16 + sh(e16, 8)
    return jnp.concatenate([s2[:, :128], s4[:, :128], s8[:, :128], s16], axis=1)


def _fill_block_diagonal(dst_ref, blocks_ref):
    n, r, c = blocks_ref.shape
    dst_ref[...] = jnp.zeros_like(dst_ref)
    for b in range(n):
        dst_ref[b * r:(b + 1) * r, b * c:(b + 1) * c] = blocks_ref[b]


def _row_counts(first_row, ts):
    t1 = (first_row + lax.broadcasted_iota(jnp.int32, (ts, 1), 0) + 1).astype(F32)
    return jnp.minimum(t1, _window_lane_width())


def _fwd_in(x, mod, g_mix, w_in, g_q, g_kv, w_uq, wuk_dc, perm, cos4, sin4, csk, snk, w_pool, pool_scale):
    S = x.shape[0]
    ts = 1024
    nsub = ts // TQ

    def body(x_ref, mod_ref, gmix_ref, win_ref, gq_ref, gkv_ref, wuq_ref, wuk_ref, perm_ref, cos_ref, sin_ref,
             csk_ref, snk_ref, wpool_ref, pscale_ref,
             h1_ref, raw_ref, qn_ref, qs_ref, kv_ref, kvt_ref, pooled_ref, ypre_ref, ypool_ref, carry_ref, wuk_bd,
             wpool_bd):
        i = pl.program_id(0)

        @pl.when(i == 0)
        def _():
            carry_ref[...] = jnp.zeros_like(carry_ref)
            _fill_block_diagonal(wuk_bd, wuk_ref)
            _fill_block_diagonal(wpool_bd, wpool_ref)

        xv = x_ref[...]
        sh1 = mod_ref[0:1, 0:D]
        sc1 = mod_ref[0:1, D:2 * D]
        h = (xv * _rms(xv)) * gmix_ref[...] * (1.0 + sc1) + sh1
        hb = h.astype(BF)
        h1_ref[...] = hb
        proj = _dot(hb, win_ref[...])
        cq_raw = proj[:, 0:QL]
        ckv_raw = proj[:, QL:QL + KVL]
        kr = proj[:, 384:512]
        u = proj[:, 512:1024]
        raw_ref[...] = proj[:, 0:384]

        c_q = (cq_raw * _rms(cq_raw)) * gq_ref[...]
        c_kv = (ckv_raw * _rms(ckv_raw)) * gkv_ref[...]
        q = _dot(c_q.astype(BF), wuq_ref[...])
        qn = q[:, 0:HEADS * NOPE].astype(BF)
        qn_ref[...] = qn
        x1 = q[:, 512:640]
        x2 = q[:, 640:768]
        cosv = cos_ref[...]
        sinv = sin_ref[...]
        roped = jnp.concatenate([x1 * cosv - x2 * sinv, x1 * sinv + x2 * cosv], axis=1).astype(BF)
        q_lat = _dot(qn, wuk_bd[...])
        q_rope = _dot(roped, perm_ref[...])
        for hd in range(HEADS):
            cols = slice(hd * 128, (hd + 1) * 128)
            qh = jnp.concatenate([q_lat[:, cols], q_rope[:, cols]], axis=1).astype(BF)
            for a in range(nsub):
                qs_ref[a, hd * TQ:(hd + 1) * TQ, :] = qh[a * TQ:(a + 1) * TQ, :]
        k_rope = kr * csk_ref[...] + _swap_halves(kr) * snk_ref[...]
        keys = jnp.concatenate([c_kv, k_rope], axis=1)
        kv_ref[...] = keys.astype(BF)
        for a in range(ts // TK):
            kvt_ref[a] = keys[a * TK:(a + 1) * TK, :].T.astype(BF)

        ext = jnp.concatenate([carry_ref[...], u], axis=0)
        win = _window_sums(ext, True)[16:, :]
        pooled = (win / _row_counts(i * ts, ts) - u).astype(BF)
        pooled_ref[...] = pooled
        carry_ref[...] = u[ts - 16:ts, :]
        ypre = _dot(pooled, wpool_bd[...])
        ypre_ref[...] = ypre.astype(BF)
        ypool_ref[...] = (ypre * pscale_ref[...]).astype(BF)

    out_shape = (
        jax.ShapeDtypeStruct((S, D), BF),
        jax.ShapeDtypeStruct((S, 384), F32),
        jax.ShapeDtypeStruct((S, HEADS * NOPE), BF),
        jax.ShapeDtypeStruct((S // TQ, HEADS * TQ, QW), BF),
        jax.ShapeDtypeStruct((S, QW), BF),
        jax.ShapeDtypeStruct((S // TK, QW, TK), BF),
        jax.ShapeDtypeStruct((S, PW), BF),
        jax.ShapeDtypeStruct((S, PW), BF),
        jax.ShapeDtypeStruct((S, PW), BF),
    )
    in_specs = [
        _rows(ts, D), _full(mod.shape), _full((1, D)), _full(w_in.shape), _full((1, QL)), _full((1, KVL)),
        _full(w_uq.shape), _full(wuk_dc.shape), _full(perm.shape), _rows(ts, 128), _rows(ts, 128), _rows(ts, 128),
        _rows(ts, 128), _full(w_pool.shape), _full((1, PW)),
    ]
    out_specs = (
        _rows(ts, D), _rows(ts, 384), _rows(ts, HEADS * NOPE),
        pl.BlockSpec((nsub, HEADS * TQ, QW), lambda i: (i, 0, 0)),
        _rows(ts, QW), pl.BlockSpec((ts // TK, QW, TK), lambda i: (i, 0, 0)), _rows(ts, PW), _rows(ts, PW),
        _rows(ts, PW),
    )
    return pl.pallas_call(
        body, name="fwd_in", out_shape=out_shape, grid=(S // ts,), in_specs=in_specs, out_specs=out_specs,
        scratch_shapes=[pltpu.VMEM((16, PW), F32), pltpu.VMEM((HEADS * NOPE, HEADS * KVL), BF),
                        pltpu.VMEM((PW, PW), BF)],
        compiler_params=_params(("arbitrary",)),
    )(x, mod, g_mix, w_in, g_q, g_kv, w_uq, wuk_dc, perm, cos4, sin4, csk, snk, w_pool, pool_scale)


def _diag_mask(shape, q_axis, first_chunk):
    qi = (lax.broadcasted_iota(jnp.int32, shape, q_axis) & (TQ - 1)) >> 6
    ki = (lax.broadcasted_iota(jnp.int32, shape, 1 - q_axis) >> 6) + first_chunk
    return ki <= qi


def _attn_fwd(qs, kv, kvt, wuv_vc):
    nq = qs.shape[0]
    S = kv.shape[0]
    M = HEADS * TQ

    def body(qs_ref, kv_ref, kvt_ref, wuv_ref, olat_ref, ymla_ref, lse_ref):
        i = pl.program_id(0)
        q = qs_ref[0]

        def step(kt, carry, first_chunk=None):
            m, l, acc = carry
            k = kv_ref[pl.ds(pl.multiple_of(kt * TK, TK), TK), :]
            v_t = kvt_ref[kt][0:KVL, :]
            s = _dot_nt(k, q)
            if first_chunk is not None:
                s = jnp.where(_diag_mask((TK, M), 1, first_chunk), s, -jnp.inf)
            m_new = jnp.maximum(m, jnp.max(s, axis=0, keepdims=True))
            alpha = jnp.exp2((m - m_new) * EXP2_SCALE)
            p = jnp.exp2((s - m_new) * EXP2_SCALE)
            l = alpha * l + jnp.sum(p, axis=0, keepdims=True)
            acc = alpha * acc + _dot(v_t, p.astype(BF))
            return m_new, l, acc

        init = (jnp.full((1, M), -jnp.inf, F32), jnp.zeros((1, M), F32), jnp.zeros((KVL, M), F32))
        per = TQ // TK
        carry = lax.fori_loop(0, per * i, step, init)
        for j in range(per):
            carry = step(per * i + j, carry, j * (TK // 64))
        m, l, acc = carry
        o_lat = acc / l
        olat_ref[0] = o_lat
        lse_ref[0] = jnp.broadcast_to(m * SM_SCALE + jnp.log(l), (8, M))
        for hd in range(HEADS):
            o_t = _dot(wuv_ref[hd], o_lat[:, hd * TQ:(hd + 1) * TQ].astype(BF))
            ymla_ref[:, hd * 128:(hd + 1) * 128] = o_t.T.astype(BF)

    out_shape = (
        jax.ShapeDtypeStruct((nq, KVL, M), F32),
        jax.ShapeDtypeStruct((S, HEADS * 128), BF),
        jax.ShapeDtypeStruct((nq, 8, M), F32),
    )
    return pl.pallas_call(
        body, name="attn_fwd", out_shape=out_shape, grid=(nq,),
        in_specs=[pl.BlockSpec((1, M, QW), lambda i: (i, 0, 0)), _full(kv.shape), _full(kvt.shape),
                  _full(wuv_vc.shape)],
        out_specs=(pl.BlockSpec((1, KVL, M), lambda i: (i, 0, 0)), _rows(TQ, HEADS * 128),
                   pl.BlockSpec((1, 8, M), lambda i: (i, 0, 0))),
        compiler_params=_params(("arbitrary",)),
    )(qs, kv, kvt, wuv_vc)


def _silu_parts(a):
    sg = jax.nn.sigmoid(a)
    return sg, a * sg


def _ffn_fwd_bwd(x, ymla, ypool, mod, w_o, g_ffn, wg_t, wu_t, wd, g_final, target):
    S = x.shape[0]
    ts = 256

    def body(x_ref, ymla_ref, ypool_ref, mod_ref, wo_ref, gffn_ref, wg_ref, wu_ref, wd_ref, gfin_ref, t_ref,
             h2t_ref, a_ref, b_ref, da_ref, db_ref, dfft_ref, dx2_ref, dmix_ref, catt_ref, dymla_ref, dypool_ref,
             loss_ref, dgfin_ref, dgt2_ref, dgt1_ref, dsc2_ref, dsh2_ref, dgffn_ref, f_ref):
        i = pl.program_id(0)

        @pl.when(i == 0)
        def _():
            for r in (loss_ref, dgfin_ref, dgt2_ref, dgt1_ref, dsc2_ref, dsh2_ref, dgffn_ref):
                r[...] = jnp.zeros_like(r)

        gt1 = mod_ref[0:1, 2 * D:3 * D]
        sh2 = mod_ref[0:1, 3 * D:4 * D]
        sc2 = mod_ref[0:1, 4 * D:5 * D]
        gt2 = mod_ref[0:1, 5 * D:6 * D]
        gffn = gffn_ref[...]
        cat = jnp.concatenate([ymla_ref[...], ypool_ref[...]], axis=1)
        catt_ref[...] = cat.astype(F32).T.astype(BF)
        mix = _dot(cat, wo_ref[...])
        x2 = x_ref[...] + gt1 * mix
        r2 = _rms(x2)
        xn2 = x2 * r2
        h2 = xn2 * gffn * (1.0 + sc2) + sh2
        h2b = h2.astype(BF)
        h2t_ref[...] = h2.T.astype(BF)

        for c in range(FF // FCHUNK):
            cols = slice(c * FCHUNK, (c + 1) * FCHUNK)
            a = _dot_nt(h2b, wg_ref[cols, :])
            b = _dot_nt(h2b, wu_ref[cols, :])
            a_ref[:, cols] = a.astype(BF)
            b_ref[:, cols] = b.astype(BF)
            f_ref[:, cols] = (_silu_parts(a)[1] * b).astype(BF)
        ff = _dot(f_ref[...], wd_ref[...])

        x3 = x2 + gt2 * ff
        r3 = _rms(x3)
        xn3 = x3 * r3
        gfin = gfin_ref[...]
        e = xn3 * gfin - t_ref[...]
        loss_ref[...] += 0.5 * jnp.sum(jnp.mean(e * e, axis=-1, keepdims=True))
        dy = e * (1.0 / D)
        dgfin_ref[...] += _colsum(dy * xn3)
        dx3 = _rms_bwd(dy * gfin, xn3, r3)
        dgt2_ref[...] += _colsum(dx3 * ff)
        dff = dx3 * gt2
        dffb = dff.astype(BF)
        dfft_ref[...] = dff.T.astype(BF)

        for c in range(FF // FCHUNK):
            cols = slice(c * FCHUNK, (c + 1) * FCHUNK)
            df = _dot_nt(dffb, wd_ref[cols, :])
            av = a_ref[:, cols].astype(F32)
            bv = b_ref[:, cols].astype(F32)
            sg, sa = _silu_parts(av)
            db_ref[:, cols] = (df * sa).astype(BF)
            da_ref[:, cols] = (df * bv * (sg * (1.0 + av * (1.0 - sg)))).astype(BF)
        dh2 = _dot(da_ref[...], wg_ref[...]) + _dot(db_ref[...], wu_ref[...])

        along = _colsum(dh2 * xn2)
        dsc2_ref[...] += along * gffn
        dsh2_ref[...] += _colsum(dh2)
        dgffn_ref[...] += along * (1.0 + sc2)
        dx2 = dx3 + _rms_bwd(dh2 * (gffn * (1.0 + sc2)), xn2, r2)
        dx2_ref[...] = dx2
        dgt1_ref[...] += _colsum(dx2 * mix)
        dmix = (dx2 * gt1).astype(BF)
        dmix_ref[...] = dmix
        dcat = _dot_nt(dmix, wo_ref[...])
        dymla_ref[...] = dcat[:, 0:PW].astype(BF)
        dypool_ref[...] = dcat[:, PW:2 * PW]

    row = lambda c: _rows(ts, c)
    col = pl.BlockSpec((D, ts), lambda i: (0, i))
    const = _full
    vec = jax.ShapeDtypeStruct((1, D), F32)
    out_shape = (
        jax.ShapeDtypeStruct((D, S), BF),
        jax.ShapeDtypeStruct((S, FF), BF),
        jax.ShapeDtypeStruct((S, FF), BF),
        jax.ShapeDtypeStruct((S, FF), BF),
        jax.ShapeDtypeStruct((S, FF), BF),
        jax.ShapeDtypeStruct((D, S), BF),
        jax.ShapeDtypeStruct((S, D), F32),
        jax.ShapeDtypeStruct((S, D), BF),
        jax.ShapeDtypeStruct((D, S), BF),
        jax.ShapeDtypeStruct((S, PW), BF),
        jax.ShapeDtypeStruct((S, PW), F32),
        jax.ShapeDtypeStruct((8, 128), F32),
        vec, vec, vec, vec, vec, vec,
    )
    return pl.pallas_call(
        body, name="ffn_fwd_bwd", out_shape=out_shape, grid=(S // ts,),
        in_specs=[row(D), row(PW), row(PW), const(mod.shape), _vmem(), const((1, D)), _vmem(), _vmem(), _vmem(),
                  const((1, D)), row(D)],
        out_specs=(col, row(FF), row(FF), row(FF), row(FF), col, row(D), row(D), col, row(PW), row(PW),
                   const((8, 128))) + (const((1, D)),) * 6,
        scratch_shapes=[pltpu.VMEM((ts, FF), BF)],
        compiler_params=_params(("arbitrary",)),
    )(x, ymla, ypool, mod, w_o, g_ffn, wg_t, wu_t, wd, g_final, target)


FCHUNK = 256


def _row_contracted_grads(dff_t, h2_t, cat_t, da, db, a, b, dmix):
    S = da.shape[0]
    nf = FF // FCHUNK
    nw = D // FCHUNK

    def body(dfft_ref, h2t_ref, catt_ref, da_ref, db_ref, a_ref, b_ref, dmix_ref, dwg_ref, dwu_ref, dwd_ref,
             dwo_ref):
        j = pl.program_id(0)

        @pl.when(j < nf)
        def _():
            h2t = h2t_ref[...]
            dwg_ref[...] = _dot(h2t, da_ref[...]).T.astype(BF)
            dwu_ref[...] = _dot(h2t, db_ref[...]).T.astype(BF)
            f = (_silu_parts(a_ref[...].astype(F32))[1] * b_ref[...].astype(F32)).astype(BF)
            dwd_ref[...] = _dot(dfft_ref[...], f).T.astype(BF)

        @pl.when(j >= nf)
        def _():
            dwo_ref[...] = _dot(catt_ref[...], dmix_ref[...]).astype(BF)

    ffn_j = lambda j: jnp.minimum(j, nf - 1)
    wo_j = lambda j: jnp.maximum(j - nf, 0)
    act = pl.BlockSpec((S, FCHUNK), lambda j: (0, ffn_j(j)))
    wblk = pl.BlockSpec((FCHUNK, D), lambda j: (ffn_j(j), 0))
    shp = jax.ShapeDtypeStruct((FF, D), BF)
    return pl.pallas_call(
        body, name="row_contracted_grads", out_shape=(shp, shp, shp, jax.ShapeDtypeStruct((D, D), BF)),
        grid=(nf + nw,),
        in_specs=[_vmem(), _vmem(), _vmem(), act, act, act, act, pl.BlockSpec((S, FCHUNK), lambda j: (0, wo_j(j)))],
        out_specs=(wblk, wblk, wblk, pl.BlockSpec((D, FCHUNK), lambda j: (0, wo_j(j)))),
        compiler_params=_params(("arbitrary",)),
    )(dff_t, h2_t, cat_t, da, db, a, b, dmix)


def _wo_grad(cat_t, dmix):
    S = dmix.shape[0]
    tc = 256

    def body(catt_ref, dmix_ref, dwo_ref):
        dwo_ref[...] = _dot(catt_ref[...], dmix_ref[...]).astype(BF)

    return pl.pallas_call(
        body, name="wo_grad", out_shape=jax.ShapeDtypeStruct((D, D), BF), grid=(D // tc,),
        in_specs=[_vmem(), pl.BlockSpec((S, tc), lambda j: (0, j))], out_specs=pl.BlockSpec((D, tc), lambda j: (0, j)),
        compiler_params=_params(("arbitrary",)),
    )(cat_t, dmix)


def _mix_bwd(dymla, dypool, ypre, pooled, pool_scale, wpool_dc, olat, wuv_vc):
    S = dymla.shape[0]
    ts = 512
    n = S // ts
    nsub = ts // TQ
    M = HEADS * TQ

    def body(dymla_ref, dypool_ref, ypre_ref, pooled_ref, pscale_ref, wpool_ref, olat_ref, wuv_ref,
             du_ref, dolat_ref, delta_ref, dwuv_ref, dwpool_ref, dpscale_ref, carry_ref, dwpool_acc, wpool_bd,
             wuv_bd):
        i = pl.program_id(0)

        @pl.when(i == 0)
        def _():
            carry_ref[...] = jnp.zeros_like(carry_ref)
            dwpool_acc[...] = jnp.zeros_like(dwpool_acc)
            _fill_block_diagonal(wpool_bd, wpool_ref)
            _fill_block_diagonal(wuv_bd, wuv_ref)
            for r in (dwuv_ref, dpscale_ref):
                r[...] = jnp.zeros_like(r)

        dypool = dypool_ref[...]
        dpscale_ref[...] += _colsum(dypool * ypre_ref[...].astype(F32))
        dypre = (dypool * pscale_ref[...]).astype(BF)
        dwpool_acc[...] += _dot_tn(pooled_ref[...], dypre)
        dpooled = _dot(dypre, wpool_bd[...])
        tile = n - 1 - i
        e = dpooled / _row_counts(tile * ts, ts)
        ext = jnp.concatenate([e, carry_ref[...]], axis=0)
        du_ref[...] = (_window_sums(ext, False)[0:ts, :] - dpooled).astype(BF)
        carry_ref[...] = e[0:16, :]

        dob_all = dymla_ref[...]
        dol_all = _dot(dob_all, wuv_bd[...])
        for hd in range(HEADS):
            dob = dob_all[:, hd * 128:(hd + 1) * 128]
            dol = dol_all[:, hd * 128:(hd + 1) * 128]
            for a in range(nsub):
                ol_t = olat_ref[a, :, hd * TQ:(hd + 1) * TQ]
                dl = dol[a * TQ:(a + 1) * TQ, :]
                dolat_ref[a, hd * TQ:(hd + 1) * TQ, :] = dl.astype(BF)
                dwuv_ref[hd] += _dot(ol_t.astype(BF), dob[a * TQ:(a + 1) * TQ, :])
                delta = jnp.sum(dl * ol_t.T, axis=-1, keepdims=True)
                delta_ref[a, :, hd * TQ:(hd + 1) * TQ] = jnp.broadcast_to(delta, (TQ, 128)).T[0:8, :]

        @pl.when(i == n - 1)
        def _():
            for g in range(GROUPS):
                dwpool_ref[g] = dwpool_acc[g * GD:(g + 1) * GD, g * GD:(g + 1) * GD]

    rev = lambda c: pl.BlockSpec((ts, c), lambda i: (n - 1 - i, 0))
    rev3 = lambda r, c: pl.BlockSpec((nsub, r, c), lambda i: (n - 1 - i, 0, 0))
    out_shape = (
        jax.ShapeDtypeStruct((S, PW), BF),
        jax.ShapeDtypeStruct((S // TQ, M, KVL), BF),
        jax.ShapeDtypeStruct((S // TQ, 8, M), F32),
        jax.ShapeDtypeStruct((HEADS, KVL, 128), F32),
        jax.ShapeDtypeStruct((GROUPS, GD, GD), F32),
        jax.ShapeDtypeStruct((1, PW), F32),
    )
    in_specs = [rev(PW), rev(PW), rev(PW), rev(PW), _full((1, PW)), _full(wpool_dc.shape), rev3(KVL, M),
                _full(wuv_vc.shape)]
    out_specs = (rev(PW), rev3(M, KVL), rev3(8, M), _full((HEADS, KVL, 128)), _full((GROUPS, GD, GD)),
                 _full((1, PW)))
    return pl.pallas_call(
        body, name="mix_bwd", out_shape=out_shape, grid=(n,), in_specs=in_specs, out_specs=out_specs,
        scratch_shapes=[pltpu.VMEM((16, PW), F32), pltpu.VMEM((PW, PW), F32), pltpu.VMEM((PW, PW), BF),
                        pltpu.VMEM((HEADS * 128, HEADS * KVL), BF)],
        compiler_params=_params(("arbitrary",)),
    )(dymla, dypool, ypre, pooled, pool_scale, wpool_dc, olat, wuv_vc)


def _attn_bwd(qs, kv, dolat, lse, delta):
    nq = qs.shape[0]
    S = kv.shape[0]
    M = HEADS * TQ
    nk = S // TK

    def body(qs_ref, kv_ref, do_ref, lse_ref, delta_ref, dkv_ref, dqt_out_ref, dqt_ref, p_ref, ds_ref):
        kt = pl.program_id(0)
        k = kv_ref[...]
        v = k[:, 0:KVL]
        k_t = k.astype(F32).T.astype(BF)

        @pl.when(kt == 0)
        def _():
            dqt_ref[...] = jnp.zeros_like(dqt_ref)

        def step(qi, carry, first_chunk=None):
            dk, dv = carry
            q = qs_ref[qi]
            do = do_ref[qi]
            s = _dot_nt(k, q)
            dp = _dot_nt(v, do)
            lse_row = lse_ref[qi, 0:1, :] * LOG2_E
            delta_row = delta_ref[qi, 0:1, :]
            q_chunk = (lax.broadcasted_iota(jnp.int32, (1, M), 1) & (TQ - 1)) >> 6
            for r in range(0, TK, VPU_ROWS):
                rows = slice(r, r + VPU_ROWS)
                p = jnp.exp2(s[rows, :] * EXP2_SCALE - lse_row)
                if first_chunk is not None:
                    p = jnp.where((r >> 6) + first_chunk <= q_chunk, p, 0.0)
                p_ref[rows, :] = p.astype(BF)
                ds_ref[rows, :] = (p * (dp[rows, :] - delta_row) * SM_SCALE).astype(BF)
            ds = ds_ref[...]
            dv = dv + _dot(p_ref[...], do)
            dk = dk + _dot(ds, q)
            dqt_ref[qi] += _dot(k_t, ds)
            return dk, dv

        per = TQ // TK
        first = kt // per
        carry = step(first, (jnp.zeros((TK, QW), F32), jnp.zeros((TK, KVL), F32)), (kt % per) * (TK // 64))
        dk, dv = lax.fori_loop(first + 1, nq, step, carry)
        dkv_ref[...] = dk + jnp.concatenate([dv, jnp.zeros((TK, QW - KVL), F32)], axis=1)
        dqt_out_ref[0] = dqt_ref[first].astype(BF)

    out_shape = (jax.ShapeDtypeStruct((S, QW), F32), jax.ShapeDtypeStruct((nq, QW, M), BF))
    return pl.pallas_call(
        body, name="attn_bwd", out_shape=out_shape, grid=(nk,),
        in_specs=[_vmem(), _rows(TK, QW), _vmem(), _vmem(), _vmem()],
        out_specs=(_rows(TK, QW), pl.BlockSpec((1, QW, M), lambda kt: (kt // (TQ // TK), 0, 0))),
        scratch_shapes=[pltpu.VMEM((nq, QW, M), F32), pltpu.VMEM((TK, M), BF), pltpu.VMEM((TK, M), BF)],
        compiler_params=_params(("arbitrary",)),
    )(qs, kv, dolat, lse, delta)


def _in_bwd(dqt, dkv, du, raw, qn, h1, x, dx2, mod, g_mix, w_in, g_q, g_kv, w_uq, wuk_cd, perm_t, cos4, sin4, csk,
            snk):
    S = x.shape[0]
    ts = 512
    n = S // ts
    nsub = ts // TQ
    M = HEADS * TQ

    def body(dqt_ref, dkv_ref, du_ref, raw_ref, qn_ref, h1_ref, x_ref, dx2_ref, mod_ref, gmix_ref, win_ref, gq_ref,
             gkv_ref, wuq_ref, wuk_ref, permt_ref, cos_ref, sin_ref, csk_ref, snk_ref,
             dx_ref, dwin_ref, dwuq_ref, dwuk_ref, dgq_ref, dgkv_ref, dsc1_ref, dsh1_ref, dgmix_ref, dwin_acc,
             dwuq_acc, dwuk_acc, wuk_bd):
        i = pl.program_id(0)

        @pl.when(i == 0)
        def _():
            dwin_acc[...] = jnp.zeros_like(dwin_acc)
            dwuq_acc[...] = jnp.zeros_like(dwuq_acc)
            dwuk_acc[...] = jnp.zeros_like(dwuk_acc)
            _fill_block_diagonal(wuk_bd, wuk_ref)
            for r in (dgq_ref, dgkv_ref, dsc1_ref, dsh1_ref, dgmix_ref):
                r[...] = jnp.zeros_like(r)

        dq_blocks = [dqt_ref[a].astype(F32).T for a in range(nsub)]
        dq_heads = [jnp.concatenate([blk[hd * TQ:(hd + 1) * TQ, :] for blk in dq_blocks], axis=0)
                    for hd in range(HEADS)]
        dq_lat = jnp.concatenate([dqh[:, 0:KVL] for dqh in dq_heads], axis=1).astype(BF)
        dq_rope = jnp.concatenate([dqh[:, KVL:QW] for dqh in dq_heads], axis=1).astype(BF)
        dq_nope = _dot(dq_lat, wuk_bd[...])
        dwuk_acc[...] += _dot_tn(dq_lat, qn_ref[...])
        drope = _dot(dq_rope, permt_ref[...])
        do1 = drope[:, 0:128]
        do2 = drope[:, 128:256]
        cosv = cos_ref[...]
        sinv = sin_ref[...]
        dq = jnp.concatenate([dq_nope, do1 * cosv + do2 * sinv, do2 * cosv - do1 * sinv], axis=1).astype(BF)

        cq_raw = raw_ref[:, 0:QL]
        ckv_raw = raw_ref[:, QL:QL + KVL]
        rq = _rms(cq_raw)
        nq_ = cq_raw * rq
        gq = gq_ref[...]
        dwuq_acc[...] += _dot_tn((nq_ * gq).astype(BF), dq)
        dc_q = _dot_nt(dq, wuq_ref[...])
        dgq_ref[...] += _colsum(dc_q * nq_)
        dcq_raw = _rms_bwd(dc_q * gq, nq_, rq)

        dkv = dkv_ref[...]
        rk = _rms(ckv_raw)
        nk_ = ckv_raw * rk
        dc_kv = dkv[:, 0:KVL]
        dgkv_ref[...] += _colsum(dc_kv * nk_)
        dckv_raw = _rms_bwd(dc_kv * gkv_ref[...], nk_, rk)
        dkr_roped = dkv[:, KVL:QW]
        dkr = dkr_roped * csk_ref[...] - _swap_halves(dkr_roped) * snk_ref[...]

        dproj = jnp.concatenate([dcq_raw.astype(BF), dckv_raw.astype(BF), dkr.astype(BF), du_ref[...]], axis=1)
        dwin_acc[...] += _dot_tn(h1_ref[...], dproj)
        dh1 = _dot_nt(dproj, win_ref[...])

        sc1 = mod_ref[0:1, D:2 * D]
        gmix = gmix_ref[...]
        xv = x_ref[...]
        r1 = _rms(xv)
        xn1 = xv * r1
        along = _colsum(dh1 * xn1)
        dsc1_ref[...] += along * gmix
        dsh1_ref[...] += _colsum(dh1)
        dgmix_ref[...] += along * (1.0 + sc1)
        dx_ref[...] = dx2_ref[...] + _rms_bwd(dh1 * (gmix * (1.0 + sc1)), xn1, r1)

        @pl.when(i == n - 1)
        def _():
            dwin_ref[...] = dwin_acc[...].astype(BF)
            dwuq_ref[...] = dwuq_acc[...].astype(BF)
            for hd in range(HEADS):
                dwuk_ref[hd] = dwuk_acc[hd * KVL:(hd + 1) * KVL, hd * NOPE:(hd + 1) * NOPE]

    out_shape = (
        jax.ShapeDtypeStruct((S, D), F32),
        jax.ShapeDtypeStruct((D, D), BF),
        jax.ShapeDtypeStruct((QL, 768), BF),
        jax.ShapeDtypeStruct((HEADS, KVL, NOPE), F32),
        jax.ShapeDtypeStruct((1, QL), F32), jax.ShapeDtypeStruct((1, KVL), F32),
        jax.ShapeDtypeStruct((1, D), F32), jax.ShapeDtypeStruct((1, D), F32), jax.ShapeDtypeStruct((1, D), F32),
    )
    in_specs = [pl.BlockSpec((nsub, QW, M), lambda i: (i, 0, 0)), _rows(ts, QW), _rows(ts, PW), _rows(ts, 384),
                _rows(ts, HEADS * NOPE), _rows(ts, D), _rows(ts, D), _rows(ts, D), _full(mod.shape), _full((1, D)),
                _full(w_in.shape), _full((1, QL)), _full((1, KVL)), _full(w_uq.shape), _full(wuk_cd.shape),
                _full(perm_t.shape), _rows(ts, 128), _rows(ts, 128), _rows(ts, 128), _rows(ts, 128)]
    out_specs = (_rows(ts, D), _full((D, D)), _full((QL, 768)), _full((HEADS, KVL, NOPE)), _full((1, QL)),
                 _full((1, KVL)), _full((1, D)), _full((1, D)), _full((1, D)))
    return pl.pallas_call(
        body, name="in_bwd", out_shape=out_shape, grid=(n,), in_specs=in_specs, out_specs=out_specs,
        scratch_shapes=[pltpu.VMEM((D, D), F32), pltpu.VMEM((QL, 768), F32),
                        pltpu.VMEM((HEADS * KVL, HEADS * NOPE), F32), pltpu.VMEM((HEADS * KVL, HEADS * NOPE), BF)],
        compiler_params=_params(("arbitrary",)),
    )(dqt, dkv, du, raw, qn, h1, x, dx2, mod, g_mix, w_in, g_q, g_kv, w_uq, wuk_cd, perm_t, cos4, sin4, csk, snk)


def _rope_perm():
    p = np.zeros((HEADS, 2 * 128, 128), np.float32)
    for hd in range(HEADS):
        for t in range(HALF):
            p[hd, hd * HALF + t, t] = 1.0
            p[hd, 128 + hd * HALF + t, HALF + t] = 1.0
    return p


def _rope_tables(positions):
    freqs = jnp.power(ROPE_THETA, -jnp.arange(HALF, dtype=F32) / HALF)
    ang = positions.astype(F32)[:, None] * jnp.tile(freqs, HEADS)[None, :]
    cos4 = jnp.cos(ang)
    sin4 = jnp.sin(ang)
    lane = jnp.arange(HEADS * HALF)[None, :]
    csk = jnp.where(lane < ROPE, cos4, 0.0)
    snk = jnp.where(lane < HALF, -sin4, jnp.where(lane < ROPE, sin4, 0.0))
    return cos4, sin4, csk, snk


def _local_step(x, rope, target, mod, g_mix, w_in_p, g_q, g_kv, w_uq_p, w_uk, w_uv, w_pool, pool_scale, g_ffn,
                g_final, late, ffn_grads_exchange):
    perm = jnp.asarray(_rope_perm().transpose(1, 0, 2).reshape(2 * 128, HEADS * 128), BF)
    perm_t = jnp.asarray(_rope_perm().transpose(0, 2, 1).reshape(HEADS * 128, 2 * 128), BF)
    cos4, sin4, csk, snk = rope
    wuk_dc = w_uk.transpose(1, 2, 0).astype(BF)
    wuk_cd = w_uk.transpose(1, 0, 2).astype(BF)
    wuv_vc = w_uv.transpose(1, 2, 0).astype(BF)
    wpool = w_pool.astype(BF)
    wpool_dc = w_pool.transpose(0, 2, 1).astype(BF)

    h1, raw, qn, qs, kv, kvt, pooled, ypre, ypool = _fwd_in(
        x, mod, g_mix, w_in_p, g_q, g_kv, w_uq_p, wuk_dc, perm, cos4, sin4, csk, snk, wpool, pool_scale)
    olat, ymla, lse = _attn_fwd(qs, kv, kvt, wuv_vc)
    w_o, wg_t, wu_t, wd = late
    (h2_t, a, b, da, db, dff_t, dx2, dmix, cat_t, dymla, dypool, loss, dgfin, dgt2, dgt1, dsc2, dsh2,
     dgffn) = _ffn_fwd_bwd(x, ymla, ypool, mod, w_o, g_ffn, wg_t, wu_t, wd, g_final, target)
    dwg_t, dwu_t, dwd, dwo = _row_contracted_grads(dff_t, h2_t, cat_t, da, db, a, b, dmix)
    ffn_parts = ffn_grads_exchange((dwg_t, dwu_t, dwd, dwo))
    du, dolat, delta, dwuv, dwpool, dpscale = _mix_bwd(
        dymla, dypool, ypre, pooled, pool_scale, wpool_dc, olat, wuv_vc)
    dkv, dqt = _attn_bwd(qs, kv, dolat, lse, delta)
    dx, dwin, dwuq, dwuk, dgq, dgkv, dsc1, dsh1, dgmix = _in_bwd(
        dqt, dkv, du, raw, qn, h1, x, dx2, mod, g_mix, w_in_p, g_q, g_kv, w_uq_p, wuk_cd, perm_t, cos4, sin4, csk,
        snk)
    dmod = jnp.concatenate([dsh1, dsc1, dgt1, dsh2, dsc2, dgt2], axis=1)
    replicated = dict(
        w_uk=dwuk.transpose(1, 0, 2), w_uv=dwuv.transpose(1, 0, 2), w_pool=dwpool, g_mix=dgmix, g_q=dgq, g_kv=dgkv,
        pool_scale=dpscale, g_ffn=dgffn, g_final=dgfin)
    return loss[0, 0], dx, dmod, (dwin, dwuq), ffn_parts, replicated


def _my_pos():
    return lax.axis_index("x"), lax.axis_index("y"), lax.axis_index("c")


def _peer(pos, k):
    x, y, c = pos
    return (1 - x if k & 4 else x, 1 - y if k & 2 else y, 1 - c if k & 1 else c)


def _index(pos):
    x, y, c = pos
    return 4 * x + 2 * y + c


def _remote(src, dst, send_sem, recv_sem, to):
    return pltpu.make_async_remote_copy(src_ref=src, dst_ref=dst, send_sem=send_sem, recv_sem=recv_sem,
                                        device_id=to, device_id_type=MESH)


def _ada_mod(c, w_ada, b_ada, after):
    def body(c_ref, w_ref, b_ref, after_ref, mod_ref, call_ref, cbuf, sbuf, rbuf, send1, recv1, send2, recv2):
        me = _my_pos()
        mi = _index(me)
        cv = c_ref[...]
        cbuf[...] = jnp.broadcast_to(cv * jax.nn.sigmoid(cv), (8, D))
        call_ref[mi] = cbuf[...]
        first = [_remote(cbuf, call_ref.at[mi], send1.at[k - 1], recv1.at[k - 1], _peer(me, k)) for k in range(1, NDEV)]
        for cp in first:
            cp.start()
        for k in range(1, NDEV):
            _remote(cbuf, call_ref.at[_index(_peer(me, k))], send1.at[k - 1], recv1.at[k - 1], _peer(me, k)).wait_recv()
        c_all = jnp.concatenate([call_ref[b][0:1, :] for b in range(NDEV)], axis=0)
        blocks = _dot(c_all.astype(BF), w_ref[...].astype(BF))
        for b in range(NDEV):
            sbuf[b] = jnp.broadcast_to(blocks[b:b + 1, :], (8, MODC))
        second = []
        for k in range(1, NDEV):
            to = _peer(me, k)
            second.append(_remote(sbuf.at[_index(to)], rbuf.at[mi], send2.at[k - 1], recv2.at[k - 1], to))
        for cp in second:
            cp.start()
        rbuf[mi] = sbuf[mi]
        for k in range(1, NDEV):
            to = _peer(me, k)
            _remote(sbuf.at[_index(to)], rbuf.at[_index(to)], send2.at[k - 1], recv2.at[k - 1], to).wait_recv()
        for j in range(NDEV):
            mod_ref[:, j * MODC:(j + 1) * MODC] = rbuf[j] + b_ref[:, j * MODC:(j + 1) * MODC]
        for cp in first + second:
            cp.wait_send()

    return pl.pallas_call(
        body, name="ada_mod",
        out_shape=(jax.ShapeDtypeStruct((8, N_MOD * D), F32), jax.ShapeDtypeStruct((NDEV, 8, D), F32)),
        in_specs=[_vmem(), _vmem(), _vmem(), _any()], out_specs=(_vmem(), _vmem()),
        scratch_shapes=[pltpu.VMEM((8, D), F32), pltpu.VMEM((NDEV, 8, MODC), F32), pltpu.VMEM((NDEV, 8, MODC), F32),
                        pltpu.SemaphoreType.DMA((NDEV - 1,)), pltpu.SemaphoreType.DMA((NDEV - 1,)),
                        pltpu.SemaphoreType.DMA((NDEV - 1,)), pltpu.SemaphoreType.DMA((NDEV - 1,))],
        compiler_params=_params(),
    )(c, w_ada, b_ada, after)


def _sequencer_scatter(name, collective_id, srcs, after=()):
    n = len(srcs)

    def of(src, to_index):
        r = src.shape[0] // NDEV
        return src.at[pl.ds(pl.multiple_of(to_index * r, 16), r), :]

    def body(*refs):
        src, zone = refs[:n], refs[n + len(after):2 * n + len(after)]
        send, recv, local = refs[2 * n + len(after):]
        me = _my_pos()
        mi = _index(me)
        barrier = pltpu.get_barrier_semaphore()
        for k in range(1, NDEV):
            pl.semaphore_signal(barrier, inc=1, device_id=_peer(me, k), device_id_type=MESH)
        pl.semaphore_wait(barrier, NDEV - 1)
        own = [pltpu.make_async_copy(of(src[a], mi), zone[a].at[mi], local.at[a]) for a in range(n)]
        for cp in own:
            cp.start()
        for a in range(n):
            for k in range(1, NDEV):
                to = _peer(me, k)
                s = a * (NDEV - 1) + k - 1
                _remote(of(src[a], _index(to)), zone[a].at[mi], send.at[s], recv.at[s], to).start()
        for cp in own:
            cp.wait()
        for a in range(n):
            for k in range(1, NDEV):
                to = _peer(me, k)
                s = a * (NDEV - 1) + k - 1
                cp = _remote(of(src[a], mi), zone[a].at[_index(to)], send.at[s], recv.at[s], to)
                cp.wait_send()
                cp.wait_recv()

    return pl.kernel(
        body, name=name, mesh=plsc.ScalarSubcoreMesh(axis_name="sequencer", num_cores=1),
        out_type=tuple(jax.ShapeDtypeStruct((NDEV, s.shape[0] // NDEV, s.shape[1]), s.dtype) for s in srcs),
        scratch_types=[pltpu.SemaphoreType.DMA((n * (NDEV - 1),)), pltpu.SemaphoreType.DMA((n * (NDEV - 1),)),
                       pltpu.SemaphoreType.DMA((n,))],
        compiler_params=pltpu.CompilerParams(collective_id=collective_id),
    )(*srcs, *after)


CHIP_PEERS = (2, 4, 6)


def _sequencer_gather(name, collective_id, srcs, after=()):
    n = len(srcs)
    per = NDEV - 1

    def body(*refs):
        src, zone = refs[:n], refs[n + len(after):2 * n + len(after)]
        send, recv, local = refs[2 * n + len(after):]
        me = _my_pos()
        mi = _index(me)
        sibling = _peer(me, 1)
        talk_to = (sibling,) + tuple(_peer(me, k) for k in CHIP_PEERS)
        barrier = pltpu.get_barrier_semaphore()
        for to in talk_to:
            pl.semaphore_signal(barrier, inc=1, device_id=to, device_id_type=MESH)
        pl.semaphore_wait(barrier, len(talk_to))

        def copy(a, slot, block_of, to, from_src=False):
            rows = zone[a].at[_index(block_of)]
            return _remote(src[a] if from_src else rows, rows, send.at[a * per + slot], recv.at[a * per + slot], to)

        own = [pltpu.make_async_copy(src[a], zone[a].at[mi], local.at[a]) for a in range(n)]
        for cp in own:
            cp.start()
        started = []
        for a in range(n):
            started.append(copy(a, 0, me, sibling, from_src=True))
            started += [copy(a, 1 + j, me, _peer(me, k), from_src=True) for j, k in enumerate(CHIP_PEERS)]
        for cp in started:
            cp.start()
        for a in range(n):
            for j, k in enumerate(CHIP_PEERS):
                copy(a, 1 + j, _peer(me, k), me).wait_recv()
                passed = copy(a, 4 + j, _peer(me, k), sibling)
                passed.start()
                started.append(passed)
        for a in range(n):
            copy(a, 0, sibling, me).wait_recv()
            for j, k in enumerate(CHIP_PEERS):
                copy(a, 4 + j, _peer(me, k | 1), me).wait_recv()
        for cp in started:
            cp.wait_send()
        for cp in own:
            cp.wait()

    return pl.kernel(
        body, name=name, mesh=plsc.ScalarSubcoreMesh(axis_name="sequencer", num_cores=1),
        out_type=tuple(jax.ShapeDtypeStruct((NDEV,) + s.shape, s.dtype) for s in srcs),
        scratch_types=[pltpu.SemaphoreType.DMA((n * per,)), pltpu.SemaphoreType.DMA((n * per,)),
                       pltpu.SemaphoreType.DMA((n,))],
        compiler_params=pltpu.CompilerParams(collective_id=collective_id),
    )(*srcs, *after)


def _blocked(shape, nb, axis=0):
    block = tuple(s // nb if d == axis else s for d, s in enumerate(shape))
    return pl.BlockSpec(block, lambda i: tuple(i if d == axis else 0 for d in range(len(shape))))


def _sum_partials(name, parts, nb):
    n = len(parts)

    def body(*refs):
        for a in range(n):
            acc = refs[a][0].astype(F32)
            for p in range(1, NDEV):
                acc = acc + refs[a][p].astype(F32)
            refs[n + a][...] = acc

    return pl.pallas_call(
        body, name=name, grid=(nb,),
        out_shape=tuple(jax.ShapeDtypeStruct(p.shape[1:], F32) for p in parts),
        in_specs=[_blocked(p.shape, nb, 1) for p in parts],
        out_specs=tuple(_blocked(p.shape[1:], nb) for p in parts), compiler_params=_params(("arbitrary",)),
    )(*parts)


def _small_all_reduce(buf):
    def body(buf_ref, got_ref, red_ref, mine, send1, recv1, send2, recv2):
        me = _my_pos()
        mi = _index(me)
        first = []
        for k in range(1, NDEV):
            to = _peer(me, k)
            first.append(_remote(buf_ref.at[_index(to)], got_ref.at[mi], send1.at[k - 1], recv1.at[k - 1], to))
        for cp in first:
            cp.start()
        got_ref[mi] = buf_ref[mi]
        for k in range(1, NDEV):
            to = _peer(me, k)
            _remote(buf_ref.at[mi], got_ref.at[_index(to)], send1.at[k - 1], recv1.at[k - 1], to).wait_recv()
        acc = got_ref[0]
        for p in range(1, NDEV):
            acc = acc + got_ref[p]
        mine[...] = acc
        second = [_remote(mine, red_ref.at[mi], send2.at[k - 1], recv2.at[k - 1], _peer(me, k)) for k in range(1, NDEV)]
        for cp in second:
            cp.start()
        red_ref[mi] = acc
        for k in range(1, NDEV):
            to = _peer(me, k)
            _remote(mine, red_ref.at[_index(to)], send2.at[k - 1], recv2.at[k - 1], to).wait_recv()
        for cp in first + second:
            cp.wait_send()

    return pl.pallas_call(
        body, name="small_all_reduce",
        out_shape=(jax.ShapeDtypeStruct(buf.shape, F32), jax.ShapeDtypeStruct(buf.shape, F32)),
        in_specs=[_vmem()], out_specs=(_vmem(), _vmem()),
        scratch_shapes=[pltpu.VMEM(buf.shape[1:], F32),
                        pltpu.SemaphoreType.DMA((NDEV - 1,)), pltpu.SemaphoreType.DMA((NDEV - 1,)),
                        pltpu.SemaphoreType.DMA((NDEV - 1,)), pltpu.SemaphoreType.DMA((NDEV - 1,))],
        compiler_params=_params(),
    )(buf)


def _adamw_math(w, g, m, v):
    m = ADAM_B1 * m + (1.0 - ADAM_B1) * g
    v = ADAM_B2 * v + (1.0 - ADAM_B2) * jnp.square(g)
    m_hat = m / (1.0 - ADAM_B1 ** ADAM_STEP)
    v_hat = v / (1.0 - ADAM_B2 ** ADAM_STEP)
    delta = -ADAM_LR * (m_hat / (jnp.sqrt(v_hat) + ADAM_EPS) + ADAM_WD * w)
    return delta, m, v


def _adamw_group(name, ws, gs, ms, vs, nb):
    n = len(ws)

    def body(*refs):
        for a in range(n):
            w, g, m, v = (refs[q * n + a][...] for q in range(4))
            delta, m2, v2 = _adamw_math(w, g, m, v)
            refs[4 * n + a][...] = delta
            refs[5 * n + a][...] = m2
            refs[6 * n + a][...] = v2

    shapes = tuple(jax.ShapeDtypeStruct(w.shape, F32) for w in ws)
    specs = [_blocked(w.shape, nb) for w in ws]
    outs = pl.pallas_call(
        body, name=name, grid=(nb,), out_shape=shapes * 3, in_specs=specs * 4, out_specs=tuple(specs * 3),
        compiler_params=_params(("arbitrary",)),
    )(*ws, *gs, *ms, *vs)
    return outs[:n], outs[n:2 * n], outs[2 * n:]


def _adamw_from_partials(name, ws, parts, ms, vs, nb):
    n = len(ws)

    def body(*refs):
        for a in range(n):
            part = refs[n + a]
            g = part[0].astype(F32)
            for p in range(1, NDEV):
                g = g + part[p].astype(F32)
            delta, m2, v2 = _adamw_math(refs[a][...], g, refs[2 * n + a][...], refs[3 * n + a][...])
            refs[4 * n + a][...] = g
            refs[5 * n + a][...] = delta
            refs[6 * n + a][...] = m2
            refs[7 * n + a][...] = v2

    shapes = tuple(jax.ShapeDtypeStruct(w.shape, F32) for w in ws)
    specs = [_blocked(w.shape, nb) for w in ws]
    outs = pl.pallas_call(
        body, name=name, grid=(nb,), out_shape=shapes * 4,
        in_specs=specs + [_blocked(p.shape, nb, 1) for p in parts] + specs * 2, out_specs=tuple(specs * 4),
        compiler_params=_params(("arbitrary",)),
    )(*ws, *parts, *ms, *vs)
    return outs[:n], outs[n:2 * n], outs[2 * n:3 * n], outs[3 * n:]


def _adamw_ada(w, m, v, c_all_t, dmod_rows):
    nb = 4

    def body(w_ref, m_ref, v_ref, c_ref, dm_ref, g_ref, d_ref, m2_ref, v2_ref):
        g = _dot(c_ref[...], dm_ref[...].astype(BF))
        g_ref[...] = g
        delta, m2, v2 = _adamw_math(w_ref[...], g, m_ref[...], v_ref[...])
        d_ref[...] = delta
        m2_ref[...] = m2
        v2_ref[...] = v2

    shp = jax.ShapeDtypeStruct(w.shape, F32)
    spec = _blocked(w.shape, nb)
    return pl.pallas_call(
        body, name="adamw_ada", grid=(nb,), out_shape=(shp, shp, shp, shp),
        in_specs=[spec, spec, spec, _blocked(c_all_t.shape, nb), _full(dmod_rows.shape)],
        out_specs=(spec, spec, spec, spec), compiler_params=_params(("arbitrary",)),
    )(w, m, v, c_all_t, dmod_rows)


def _w_in_to_kernel(w):
    return jnp.concatenate([w[:, 0:448], jnp.zeros((w.shape[0], 64), w.dtype), w[:, 448:960]], axis=1)


def _w_in_from_kernel(w):
    return jnp.concatenate([w[:, 0:448], w[:, 512:1024]], axis=1)


def _w_uq_to_kernel(w):
    r = w.shape[0]
    return jnp.concatenate([w[:, :, 0:NOPE].reshape(r, HEADS * NOPE),
                            w[:, :, NOPE:NOPE + HALF].reshape(r, HEADS * HALF),
                            w[:, :, NOPE + HALF:].reshape(r, HEADS * HALF)], axis=1)


def _w_uq_from_kernel(w):
    r = w.shape[0]
    return jnp.concatenate([w[:, 0:512].reshape(r, HEADS, NOPE), w[:, 512:640].reshape(r, HEADS, HALF),
                            w[:, 640:768].reshape(r, HEADS, HALF)], axis=2)


REP_NAMES = ("w_uk", "w_uv", "w_pool", "g_mix", "g_q", "g_kv", "pool_scale", "g_ffn", "g_final")


def kernel(x, c, positions, w_ada, b_ada, g_mix, w_in, g_q, g_kv, w_uq, w_uk, w_uv, w_pool, pool_scale, w_o, g_ffn, w_gate, w_up, w_down, g_final, loss_target, m_w_ada, m_b_ada, m_g_mix, m_w_in, m_g_q, m_g_kv, m_w_uq, m_w_uk, m_w_uv, m_w_pool, m_pool_scale, m_w_o, m_g_ffn, m_w_gate, m_w_up, m_w_down, m_g_final, v_w_ada, v_b_ada, v_g_mix, v_w_in, v_g_q, v_g_kv, v_w_uq, v_w_uk, v_w_uv, v_w_pool, v_pool_scale, v_w_o, v_g_ffn, v_w_gate, v_w_up, v_w_down, v_g_final):
    given = dict(locals())

    merge = lambda g: g.reshape(NDEV * g.shape[1], g.shape[2])
    w_in_p, w_uq_p = (merge(g) for g in _sequencer_gather(
        "gather_in", 3, (_w_in_to_kernel(w_in[0]).astype(BF), _w_uq_to_kernel(w_uq[0]).astype(BF))))

    rope = _rope_tables(positions[0])
    mod, c_all8 = _ada_mod(c, w_ada[0], b_ada, rope[3][0:8, :])
    c_all = c_all8[:, 0, :]
    late = _sequencer_gather(
        "gather_late", 1, (w_o[0].astype(BF), w_gate[0].T.astype(BF), w_up[0].T.astype(BF), w_down[0].astype(BF)),
        after=(mod[:, 0:128], w_in_p[0:16, 0:128], w_uq_p[0:16, 0:128]))

    def ffn_grads_exchange(arrays):
        return _sequencer_scatter("scatter_ffn", 2, arrays)

    loss, dx, dmod, tail_grads, ffn_parts, replicated = _local_step(
        x[0], rope, loss_target[0], mod, g_mix, w_in_p, g_q, g_kv, w_uq_p, w_uk[0], w_uv[0], w_pool[0],
        pool_scale, g_ffn, g_final.reshape(1, D), tuple(merge(g) for g in late), ffn_grads_exchange)

    flat = jnp.concatenate([replicated[k].reshape(-1) for k in REP_NAMES] + [loss.reshape(1)])
    flat = jnp.pad(flat, (0, NDEV * REP_ROWS * 128 - flat.shape[0])).reshape(NDEV, REP_ROWS, 128)
    dmod_blocks = jnp.pad(dmod.reshape(NDEV, MODC // 128, 128), ((0, 0), (0, MOD_ROWS - MODC // 128), (0, 0)))
    got, red = _small_all_reduce(jnp.concatenate([dmod_blocks, flat], axis=1))

    tail_parts = _sequencer_scatter("scatter_tail", 4, tail_grads,
                                    after=(ffn_parts[0][0, 0:16, 0:128], red[0, 0:8, :]))
    g_in_p, g_uq_p = _sum_partials("sum_tail_partials", tail_parts, 1)
    as_transpose = ("w_in", "w_gate", "w_up")
    grads = dict(w_in=_w_in_from_kernel(g_in_p).T, w_uq=_w_uq_from_kernel(g_uq_p))
    partials = dict(w_gate=ffn_parts[0], w_up=ffn_parts[1], w_down=ffn_parts[2], w_o=ffn_parts[3])
    dmod_rows = got[:, 0:MODC // 128, :].reshape(NDEV, MODC)
    grads["b_ada"] = red[:, 0:MODC // 128, :].reshape(1, N_MOD * D)
    rep_flat = red[:, MOD_ROWS:, :].reshape(-1)
    off = 0
    for k in REP_NAMES:
        size = int(np.prod(given[k].shape))
        grads[k] = rep_flat[off:off + size]
        off += size

    view = {k: (given[k].shape[1:] if given[k].ndim > 2 else given[k].shape)
            for k in REP_NAMES + ("b_ada", "w_ada", "w_in", "w_uq", "w_o", "w_gate", "w_up", "w_down")}
    view.update(g_final=(1, D))
    names = ["w_ada", "b_ada", "g_mix", "w_in", "g_q", "g_kv", "w_uq", "w_uk", "w_uv", "w_pool", "pool_scale",
             "w_o", "g_ffn", "w_gate", "w_up", "w_down", "g_final"]
    g_ada, d_ada, m_ada, v_ada = _adamw_ada(w_ada[0], m_w_ada[0], v_w_ada[0], c_all.T.astype(BF), dmod_rows)
    out_g, out_d, out_m, out_v = dict(w_ada=g_ada), dict(w_ada=d_ada), dict(w_ada=m_ada), dict(w_ada=v_ada)
    groups = (("adamw_ffn", ("w_gate", "w_up", "w_down", "w_o"), 4),
              ("adamw_replicated", REP_NAMES + ("b_ada",), 1),
              ("adamw_tail", ("w_in", "w_uq"), 1))
    for gname, members, nb in groups:
        turn = lambda k, t: t.T if k in as_transpose else t
        ws = [turn(k, given[k].reshape(view[k])) for k in members]
        ms = [turn(k, given["m_" + k].reshape(view[k])) for k in members]
        vs = [turn(k, given["v_" + k].reshape(view[k])) for k in members]
        if members[0] in partials:
            gs, ds, m2, v2 = _adamw_from_partials(gname, ws, [partials[k] for k in members], ms, vs, nb)
        else:
            gs = [grads[k] if k in as_transpose else grads[k].reshape(view[k]) for k in members]
            ds, m2, v2 = _adamw_group(gname, ws, gs, ms, vs, nb)
        for k, g, d, mm, vv in zip(members, gs, ds, m2, v2):
            out_g[k], out_d[k], out_m[k], out_v[k] = turn(k, g), turn(k, d), turn(k, mm), turn(k, vv)

    total = rep_flat[off]
    shaped = lambda d: [d[k].reshape(given[k].shape) for k in names]
    return (total, dx[None], *shaped(out_g), *shaped(out_d), *shaped(out_m), *shaped(out_v))
```

```python
import numpy as np
import jax
import jax.numpy as jnp
from jax import lax
from jax.experimental import pallas as pl
from jax.experimental.pallas import tpu as pltpu
from jax.experimental.pallas import tpu_sc as plsc

D = 1024
HEADS = 4
NOPE = 128
ROPE = 64
HALF = ROPE // 2
QL = 256
KVL = 128
FF = 2816
PW = 512
GROUPS = 4
GD = 128
N_MOD = 6
EPS = 1e-6
SM_SCALE = (NOPE + ROPE) ** -0.5
LOG2_E = 1.4426950408889634
EXP2_SCALE = SM_SCALE * LOG2_E
ROPE_THETA = 10000.0
NDEV = 8
MODC = N_MOD * D // NDEV

ADAM_LR = 0.001
ADAM_B1 = 0.9
ADAM_B2 = 0.999
ADAM_EPS = 1e-08
ADAM_WD = 0.01
ADAM_STEP = 10

BF = jnp.bfloat16
F32 = jnp.float32
VMEM_LIMIT_V7X = 60 * 1024 * 1024
MESH = pl.DeviceIdType.MESH

TQ = 512
TK = 512
QW = 256
VPU_ROWS = 16
MOD_ROWS = 8
REP_ROWS = 200
SMALL_ROWS = MOD_ROWS + REP_ROWS


def _params(sem=None):
    return pltpu.CompilerParams(dimension_semantics=sem, vmem_limit_bytes=VMEM_LIMIT_V7X)


def _dot(a, b):
    return jnp.dot(a, b, preferred_element_type=F32)


def _dot_nt(a, b):
    return lax.dot_general(a, b, (((1,), (1,)), ((), ())), preferred_element_type=F32)


def _dot_tn(a, b):
    return _dot(a.astype(F32).T.astype(BF), b)


def _full(shape):
    return pl.BlockSpec(shape, lambda *_: (0,) * len(shape))


def _rows(ts, cols):
    return pl.BlockSpec((ts, cols), lambda i: (i, 0))


def _vmem():
    return pl.BlockSpec(memory_space=pltpu.VMEM)


def _any():
    return pl.BlockSpec(memory_space=pl.ANY)


def _rms(v):
    return lax.rsqrt(jnp.mean(v * v, axis=-1, keepdims=True) + EPS)


def _rms_bwd(dn, n, r):
    return r * (dn - n * jnp.mean(dn * n, axis=-1, keepdims=True))


def _colsum(v):
    return jnp.sum(v, axis=0, keepdims=True)


def _swap_halves(v):
    lane = lax.broadcasted_iota(jnp.int32, v.shape, 1)
    return jnp.where(lane < HALF, pltpu.roll(v, 128 - HALF, 1), pltpu.roll(v, HALF, 1))


def _window_lane_width():
    lane = lax.broadcasted_iota(jnp.int32, (1, PW), 1)
    return jnp.where(lane < 128, 2.0, jnp.where(lane < 256, 4.0, jnp.where(lane < 384, 8.0, 16.0))).astype(F32)


def _window_sums(ext, back):
    n = ext.shape[0]

    def sh(v, k):
        return pltpu.roll(v, k if back else n - k, 0)

    s2 = ext + sh(ext, 1)
    e4 = s2[:, 128:]
    s4 = e4 + sh(e4, 2)
    e8 = s4[:, 128:]
    s8 = e8 + sh(e8, 4)
    e16 = s8[:, 128:]
    s16 = e16 + sh(e16, 8)
    return jnp.concatenate([s2[:, :128], s4[:, :128], s8[:, :128], s16], axis=1)


def _fill_block_diagonal(dst_ref, blocks_ref):
    n, r, c = blocks_ref.shape
    dst_ref[...] = jnp.zeros_like(dst_ref)
    for b in range(n):
        dst_ref[b * r:(b + 1) * r, b * c:(b + 1) * c] = blocks_ref[b]


def _row_counts(first_row, ts):
    t1 = (first_row + lax.broadcasted_iota(jnp.int32, (ts, 1), 0) + 1).astype(F32)
    return jnp.minimum(t1, _window_lane_width())


def _fwd_in(x, mod, g_mix, w_in, g_q, g_kv, w_uq, wuk_dc, perm, cos4, sin4, csk, snk, w_pool, pool_scale):
    S = x.shape[0]
    ts = 1024
    nsub = ts // TQ

    def body(x_ref, mod_ref, gmix_ref, win_ref, gq_ref, gkv_ref, wuq_ref, wuk_ref, perm_ref, cos_ref, sin_ref,
             csk_ref, snk_ref, wpool_ref, pscale_ref,
             h1_ref, raw_ref, qn_ref, qs_ref, kv_ref, kvt_ref, pooled_ref, ypre_ref, ypool_ref, carry_ref, wuk_bd,
             wpool_bd):
        i = pl.program_id(0)

        @pl.when(i == 0)
        def _():
            carry_ref[...] = jnp.zeros_like(carry_ref)
            _fill_block_diagonal(wuk_bd, wuk_ref)
            _fill_block_diagonal(wpool_bd, wpool_ref)

        xv = x_ref[...]
        sh1 = mod_ref[0:1, 0:D]
        sc1 = mod_ref[0:1, D:2 * D]
        h = (xv * _rms(xv)) * gmix_ref[...] * (1.0 + sc1) + sh1
        hb = h.astype(BF)
        h1_ref[...] = hb
        proj = _dot(hb, win_ref[...])
        cq_raw = proj[:, 0:QL]
        ckv_raw = proj[:, QL:QL + KVL]
        kr = proj[:, 384:512]
        u = proj[:, 512:1024]
        raw_ref[...] = proj[:, 0:384]

        c_q = (cq_raw * _rms(cq_raw)) * gq_ref[...]
        c_kv = (ckv_raw * _rms(ckv_raw)) * gkv_ref[...]
        q = _dot(c_q.astype(BF), wuq_ref[...])
        qn = q[:, 0:HEADS * NOPE].astype(BF)
        qn_ref[...] = qn
        x1 = q[:, 512:640]
        x2 = q[:, 640:768]
        cosv = cos_ref[...]
        sinv = sin_ref[...]
        roped = jnp.concatenate([x1 * cosv - x2 * sinv, x1 * sinv + x2 * cosv], axis=1).astype(BF)
        q_lat = _dot(qn, wuk_bd[...])
        q_rope = _dot(roped, perm_ref[...])
        for hd in range(HEADS):
            cols = slice(hd * 128, (hd + 1) * 128)
            qh = jnp.concatenate([q_lat[:, cols], q_rope[:, cols]], axis=1).astype(BF)
            for a in range(nsub):
                qs_ref[a, hd * TQ:(hd + 1) * TQ, :] = qh[a * TQ:(a + 1) * TQ, :]
        k_rope = kr * csk_ref[...] + _swap_halves(kr) * snk_ref[...]
        keys = jnp.concatenate([c_kv, k_rope], axis=1)
        kv_ref[...] = keys.astype(BF)
        for a in range(ts // TK):
            kvt_ref[a] = keys[a * TK:(a + 1) * TK, :].T.astype(BF)

        ext = jnp.concatenate([carry_ref[...], u], axis=0)
        win = _window_sums(ext, True)[16:, :]
        pooled = (win / _row_counts(i * ts, ts) - u).astype(BF)
        pooled_ref[...] = pooled
        carry_ref[...] = u[ts - 16:ts, :]
        ypre = _dot(pooled, wpool_bd[...])
        ypre_ref[...] = ypre.astype(BF)
        ypool_ref[...] = (ypre * pscale_ref[...]).astype(BF)

    out_shape = (
        jax.ShapeDtypeStruct((S, D), BF),
        jax.ShapeDtypeStruct((S, 384), F32),
        jax.ShapeDtypeStruct((S, HEADS * NOPE), BF),
        jax.ShapeDtypeStruct((S // TQ, HEADS * TQ, QW), BF),
        jax.ShapeDtypeStruct((S, QW), BF),
        jax.ShapeDtypeStruct((S // TK, QW, TK), BF),
        jax.ShapeDtypeStruct((S, PW), BF),
        jax.ShapeDtypeStruct((S, PW), BF),
        jax.ShapeDtypeStruct((S, PW), BF),
    )
    in_specs = [
        _rows(ts, D), _full(mod.shape), _full((1, D)), _full(w_in.shape), _full((1, QL)), _full((1, KVL)),
        _full(w_uq.shape), _full(wuk_dc.shape), _full(perm.shape), _rows(ts, 128), _rows(ts, 128), _rows(ts, 128),
        _rows(ts, 128), _full(w_pool.shape), _full((1, PW)),
    ]
    out_specs = (
        _rows(ts, D), _rows(ts, 384), _rows(ts, HEADS * NOPE),
        pl.BlockSpec((nsub, HEADS * TQ, QW), lambda i: (i, 0, 0)),
        _rows(ts, QW), pl.BlockSpec((ts // TK, QW, TK), lambda i: (i, 0, 0)), _rows(ts, PW), _rows(ts, PW),
        _rows(ts, PW),
    )
    return pl.pallas_call(
        body, name="fwd_in", out_shape=out_shape, grid=(S // ts,), in_specs=in_specs, out_specs=out_specs,
        scratch_shapes=[pltpu.VMEM((16, PW), F32), pltpu.VMEM((HEADS * NOPE, HEADS * KVL), BF),
                        pltpu.VMEM((PW, PW), BF)],
        compiler_params=_params(("arbitrary",)),
    )(x, mod, g_mix, w_in, g_q, g_kv, w_uq, wuk_dc, perm, cos4, sin4, csk, snk, w_pool, pool_scale)


def _diag_mask(shape, q_axis, first_chunk):
    qi = (lax.broadcasted_iota(jnp.int32, shape, q_axis) & (TQ - 1)) >> 6
    ki = (lax.broadcasted_iota(jnp.int32, shape, 1 - q_axis) >> 6) + first_chunk
    return ki <= qi


def _attn_fwd(qs, kv, kvt, wuv_vc):
    nq = qs.shape[0]
    S = kv.shape[0]
    M = HEADS * TQ

    def body(qs_ref, kv_ref, kvt_ref, wuv_ref, olat_ref, ymla_ref, lse_ref):
        i = pl.program_id(0)
        q = qs_ref[0]

        def step(kt, carry, first_chunk=None):
            m, l, acc = carry
            k = kv_ref[pl.ds(pl.multiple_of(kt * TK, TK), TK), :]
            v_t = kvt_ref[kt][0:KVL, :]
            s = _dot_nt(k, q)
            if first_chunk is not None:
                s = jnp.where(_diag_mask((TK, M), 1, first_chunk), s, -jnp.inf)
            m_new = jnp.maximum(m, jnp.max(s, axis=0, keepdims=True))
            alpha = jnp.exp2((m - m_new) * EXP2_SCALE)
            p = jnp.exp2((s - m_new) * EXP2_SCALE)
            l = alpha * l + jnp.sum(p, axis=0, keepdims=True)
            acc = alpha * acc + _dot(v_t, p.astype(BF))
            return m_new, l, acc

        init = (jnp.full((1, M), -jnp.inf, F32), jnp.zeros((1, M), F32), jnp.zeros((KVL, M), F32))
        per = TQ // TK
        carry = lax.fori_loop(0, per * i, step, init)
        for j in range(per):
            carry = step(per * i + j, carry, j * (TK // 64))
        m, l, acc = carry
        o_lat = acc / l
        olat_ref[0] = o_lat
        lse_ref[0] = jnp.broadcast_to(m * SM_SCALE + jnp.log(l), (8, M))
        for hd in range(HEADS):
            o_t = _dot(wuv_ref[hd], o_lat[:, hd * TQ:(hd + 1) * TQ].astype(BF))
            ymla_ref[:, hd * 128:(hd + 1) * 128] = o_t.T.astype(BF)

    out_shape = (
        jax.ShapeDtypeStruct((nq, KVL, M), F32),
        jax.ShapeDtypeStruct((S, HEADS * 128), BF),
        jax.ShapeDtypeStruct((nq, 8, M), F32),
    )
    return pl.pallas_call(
        body, name="attn_fwd", out_shape=out_shape, grid=(nq,),
        in_specs=[pl.BlockSpec((1, M, QW), lambda i: (i, 0, 0)), _full(kv.shape), _full(kvt.shape),
                  _full(wuv_vc.shape)],
        out_specs=(pl.BlockSpec((1, KVL, M), lambda i: (i, 0, 0)), _rows(TQ, HEADS * 128),
                   pl.BlockSpec((1, 8, M), lambda i: (i, 0, 0))),
        compiler_params=_params(("arbitrary",)),
    )(qs, kv, kvt, wuv_vc)


def _silu_parts(a):
    sg = jax.nn.sigmoid(a)
    return sg, a * sg


def _ffn_fwd_bwd(x, ymla, ypool, mod, w_o, g_ffn, wg_t, wu_t, wd, g_final, target):
    S = x.shape[0]
    ts = 256

    def body(x_ref, ymla_ref, ypool_ref, mod_ref, wo_ref, gffn_ref, wg_ref, wu_ref, wd_ref, gfin_ref, t_ref,
             h2t_ref, a_ref, b_ref, da_ref, db_ref, dfft_ref, dx2_ref, dmix_ref, catt_ref, dymla_ref, dypool_ref,
             loss_ref, dgfin_ref, dgt2_ref, dgt1_ref, dsc2_ref, dsh2_ref, dgffn_ref, f_ref):
        i = pl.program_id(0)

        @pl.when(i == 0)
        def _():
            for r in (loss_ref, dgfin_ref, dgt2_ref, dgt1_ref, dsc2_ref, dsh2_ref, dgffn_ref):
                r[...] = jnp.zeros_like(r)

        gt1 = mod_ref[0:1, 2 * D:3 * D]
        sh2 = mod_ref[0:1, 3 * D:4 * D]
        sc2 = mod_ref[0:1, 4 * D:5 * D]
        gt2 = mod_ref[0:1, 5 * D:6 * D]
        gffn = gffn_ref[...]
        cat = jnp.concatenate([ymla_ref[...], ypool_ref[...]], axis=1)
        catt_ref[...] = cat.astype(F32).T.astype(BF)
        mix = _dot(cat, wo_ref[...])
        x2 = x_ref[...] + gt1 * mix
        r2 = _rms(x2)
        xn2 = x2 * r2
        h2 = xn2 * gffn * (1.0 + sc2) + sh2
        h2b = h2.astype(BF)
        h2t_ref[...] = h2.T.astype(BF)

        for c in range(FF // FCHUNK):
            cols = slice(c * FCHUNK, (c + 1) * FCHUNK)
            a = _dot_nt(h2b, wg_ref[cols, :])
            b = _dot_nt(h2b, wu_ref[cols, :])
            a_ref[:, cols] = a.astype(BF)
            b_ref[:, cols] = b.astype(BF)
            f_ref[:, cols] = (_silu_parts(a)[1] * b).astype(BF)
        ff = _dot(f_ref[...], wd_ref[...])

        x3 = x2 + gt2 * ff
        r3 = _rms(x3)
        xn3 = x3 * r3
        gfin = gfin_ref[...]
        e = xn3 * gfin - t_ref[...]
        loss_ref[...] += 0.5 * jnp.sum(jnp.mean(e * e, axis=-1, keepdims=True))
        dy = e * (1.0 / D)
        dgfin_ref[...] += _colsum(dy * xn3)
        dx3 = _rms_bwd(dy * gfin, xn3, r3)
        dgt2_ref[...] += _colsum(dx3 * ff)
        dff = dx3 * gt2
        dffb = dff.astype(BF)
        dfft_ref[...] = dff.T.astype(BF)

        for c in range(FF // FCHUNK):
            cols = slice(c * FCHUNK, (c + 1) * FCHUNK)
            df = _dot_nt(dffb, wd_ref[cols, :])
            av = a_ref[:, cols].astype(F32)
            bv = b_ref[:, cols].astype(F32)
            sg, sa = _silu_parts(av)
            db_ref[:, cols] = (df * sa).astype(BF)
            da_ref[:, cols] = (df * bv * (sg * (1.0 + av * (1.0 - sg)))).astype(BF)
        dh2 = _dot(da_ref[...], wg_ref[...]) + _dot(db_ref[...], wu_ref[...])

        along = _colsum(dh2 * xn2)
        dsc2_ref[...] += along * gffn
        dsh2_ref[...] += _colsum(dh2)
        dgffn_ref[...] += along * (1.0 + sc2)
        dx2 = dx3 + _rms_bwd(dh2 * (gffn * (1.0 + sc2)), xn2, r2)
        dx2_ref[...] = dx2
        dgt1_ref[...] += _colsum(dx2 * mix)
        dmix = (dx2 * gt1).astype(BF)
        dmix_ref[...] = dmix
        dcat = _dot_nt(dmix, wo_ref[...])
        dymla_ref[...] = dcat[:, 0:PW].astype(BF)
        dypool_ref[...] = dcat[:, PW:2 * PW]

    row = lambda c: _rows(ts, c)
    col = pl.BlockSpec((D, ts), lambda i: (0, i))
    const = _full
    vec = jax.ShapeDtypeStruct((1, D), F32)
    out_shape = (
        jax.ShapeDtypeStruct((D, S), BF),
        jax.ShapeDtypeStruct((S, FF), BF),
        jax.ShapeDtypeStruct((S, FF), BF),
        jax.ShapeDtypeStruct((S, FF), BF),
        jax.ShapeDtypeStruct((S, FF), BF),
        jax.ShapeDtypeStruct((D, S), BF),
        jax.ShapeDtypeStruct((S, D), F32),
        jax.ShapeDtypeStruct((S, D), BF),
        jax.ShapeDtypeStruct((D, S), BF),
        jax.ShapeDtypeStruct((S, PW), BF),
        jax.ShapeDtypeStruct((S, PW), F32),
        jax.ShapeDtypeStruct((8, 128), F32),
        vec, vec, vec, vec, vec, vec,
    )
    return pl.pallas_call(
        body, name="ffn_fwd_bwd", out_shape=out_shape, grid=(S // ts,),
        in_specs=[row(D), row(PW), row(PW), const(mod.shape), _vmem(), const((1, D)), _vmem(), _vmem(), _vmem(),
                  const((1, D)), row(D)],
        out_specs=(col, row(FF), row(FF), row(FF), row(FF), col, row(D), row(D), col, row(PW), row(PW),
                   const((8, 128))) + (const((1, D)),) * 6,
        scratch_shapes=[pltpu.VMEM((ts, FF), BF)],
        compiler_params=_params(("arbitrary",)),
    )(x, ymla, ypool, mod, w_o, g_ffn, wg_t, wu_t, wd, g_final, target)


FCHUNK = 256


def _row_contracted_grads(dff_t, h2_t, cat_t, da, db, a, b, dmix):
    S = da.shape[0]
    nf = FF // FCHUNK
    nw = D // FCHUNK

    def body(dfft_ref, h2t_ref, catt_ref, da_ref, db_ref, a_ref, b_ref, dmix_ref, dwg_ref, dwu_ref, dwd_ref,
             dwo_ref):
        j = pl.program_id(0)

        @pl.when(j < nf)
        def _():
            h2t = h2t_ref[...]
            dwg_ref[...] = _dot(h2t, da_ref[...]).T.astype(BF)
            dwu_ref[...] = _dot(h2t, db_ref[...]).T.astype(BF)
            f = (_silu_parts(a_ref[...].astype(F32))[1] * b_ref[...].astype(F32)).astype(BF)
            dwd_ref[...] = _dot(dfft_ref[...], f).T.astype(BF)

        @pl.when(j >= nf)
        def _():
            dwo_ref[...] = _dot(catt_ref[...], dmix_ref[...]).astype(BF)

    ffn_j = lambda j: jnp.minimum(j, nf - 1)
    wo_j = lambda j: jnp.maximum(j - nf, 0)
    act = pl.BlockSpec((S, FCHUNK), lambda j: (0, ffn_j(j)))
    wblk = pl.BlockSpec((FCHUNK, D), lambda j: (ffn_j(j), 0))
    shp = jax.ShapeDtypeStruct((FF, D), BF)
    return pl.pallas_call(
        body, name="row_contracted_grads", out_shape=(shp, shp, shp, jax.ShapeDtypeStruct((D, D), BF)),
        grid=(nf + nw,),
        in_specs=[_vmem(), _vmem(), _vmem(), act, act, act, act, pl.BlockSpec((S, FCHUNK), lambda j: (0, wo_j(j)))],
        out_specs=(wblk, wblk, wblk, pl.BlockSpec((D, FCHUNK), lambda j: (0, wo_j(j)))),
        compiler_params=_params(("arbitrary",)),
    )(dff_t, h2_t, cat_t, da, db, a, b, dmix)


def _wo_grad(cat_t, dmix):
    S = dmix.shape[0]
    tc = 256

    def body(catt_ref, dmix_ref, dwo_ref):
        dwo_ref[...] = _dot(catt_ref[...], dmix_ref[...]).astype(BF)

    return pl.pallas_call(
        body, name="wo_grad", out_shape=jax.ShapeDtypeStruct((D, D), BF), grid=(D // tc,),
        in_specs=[_vmem(), pl.BlockSpec((S, tc), lambda j: (0, j))], out_specs=pl.BlockSpec((D, tc), lambda j: (0, j)),
        compiler_params=_params(("arbitrary",)),
    )(cat_t, dmix)


def _mix_bwd(dymla, dypool, ypre, pooled, pool_scale, wpool_dc, olat, wuv_vc):
    S = dymla.shape[0]
    ts = 512
    n = S // ts
    nsub = ts // TQ
    M = HEADS * TQ

    def body(dymla_ref, dypool_ref, ypre_ref, pooled_ref, pscale_ref, wpool_ref, olat_ref, wuv_ref,
             du_ref, dolat_ref, delta_ref, dwuv_ref, dwpool_ref, dpscale_ref, carry_ref, dwpool_acc, wpool_bd,
             wuv_bd):
        i = pl.program_id(0)

        @pl.when(i == 0)
        def _():
            carry_ref[...] = jnp.zeros_like(carry_ref)
            dwpool_acc[...] = jnp.zeros_like(dwpool_acc)
            _fill_block_diagonal(wpool_bd, wpool_ref)
            _fill_block_diagonal(wuv_bd, wuv_ref)
            for r in (dwuv_ref, dpscale_ref):
                r[...] = jnp.zeros_like(r)

        dypool = dypool_ref[...]
        dpscale_ref[...] += _colsum(dypool * ypre_ref[...].astype(F32))
        dypre = (dypool * pscale_ref[...]).astype(BF)
        dwpool_acc[...] += _dot_tn(pooled_ref[...], dypre)
        dpooled = _dot(dypre, wpool_bd[...])
        tile = n - 1 - i
        e = dpooled / _row_counts(tile * ts, ts)
        ext = jnp.concatenate([e, carry_ref[...]], axis=0)
        du_ref[...] = (_window_sums(ext, False)[0:ts, :] - dpooled).astype(BF)
        carry_ref[...] = e[0:16, :]

        dob_all = dymla_ref[...]
        dol_all = _dot(dob_all, wuv_bd[...])
        for hd in range(HEADS):
            dob = dob_all[:, hd * 128:(hd + 1) * 128]
            dol = dol_all[:, hd * 128:(hd + 1) * 128]
            for a in range(nsub):
                ol_t = olat_ref[a, :, hd * TQ:(hd + 1) * TQ]
                dl = dol[a * TQ:(a + 1) * TQ, :]
                dolat_ref[a, hd * TQ:(hd + 1) * TQ, :] = dl.astype(BF)
                dwuv_ref[hd] += _dot(ol_t.astype(BF), dob[a * TQ:(a + 1) * TQ, :])
                delta = jnp.sum(dl * ol_t.T, axis=-1, keepdims=True)
                delta_ref[a, :, hd * TQ:(hd + 1) * TQ] = jnp.broadcast_to(delta, (TQ, 128)).T[0:8, :]

        @pl.when(i == n - 1)
        def _():
            for g in range(GROUPS):
                dwpool_ref[g] = dwpool_acc[g * GD:(g + 1) * GD, g * GD:(g + 1) * GD]

    rev = lambda c: pl.BlockSpec((ts, c), lambda i: (n - 1 - i, 0))
    rev3 = lambda r, c: pl.BlockSpec((nsub, r, c), lambda i: (n - 1 - i, 0, 0))
    out_shape = (
        jax.ShapeDtypeStruct((S, PW), BF),
        jax.ShapeDtypeStruct((S // TQ, M, KVL), BF),
        jax.ShapeDtypeStruct((S // TQ, 8, M), F32),
        jax.ShapeDtypeStruct((HEADS, KVL, 128), F32),
        jax.ShapeDtypeStruct((GROUPS, GD, GD), F32),
        jax.ShapeDtypeStruct((1, PW), F32),
    )
    in_specs = [rev(PW), rev(PW), rev(PW), rev(PW), _full((1, PW)), _full(wpool_dc.shape), rev3(KVL, M),
                _full(wuv_vc.shape)]
    out_specs = (rev(PW), rev3(M, KVL), rev3(8, M), _full((HEADS, KVL, 128)), _full((GROUPS, GD, GD)),
                 _full((1, PW)))
    return pl.pallas_call(
        body, name="mix_bwd", out_shape=out_shape, grid=(n,), in_specs=in_specs, out_specs=out_specs,
        scratch_shapes=[pltpu.VMEM((16, PW), F32), pltpu.VMEM((PW, PW), F32), pltpu.VMEM((PW, PW), BF),
                        pltpu.VMEM((HEADS * 128, HEADS * KVL), BF)],
        compiler_params=_params(("arbitrary",)),
    )(dymla, dypool, ypre, pooled, pool_scale, wpool_dc, olat, wuv_vc)


def _attn_bwd(qs, kv, dolat, lse, delta):
    nq = qs.shape[0]
    S = kv.shape[0]
    M = HEADS * TQ
    nk = S // TK

    def body(qs_ref, kv_ref, do_ref, lse_ref, delta_ref, dkv_ref, dqt_out_ref, dqt_ref, p_ref, ds_ref):
        kt = pl.program_id(0)
        k = kv_ref[...]
        v = k[:, 0:KVL]
        k_t = k.astype(F32).T.astype(BF)

        @pl.when(kt == 0)
        def _():
            dqt_ref[...] = jnp.zeros_like(dqt_ref)

        def step(qi, carry, first_chunk=None):
            dk, dv = carry
            q = qs_ref[qi]
            do = do_ref[qi]
            s = _dot_nt(k, q)
            dp = _dot_nt(v, do)
            lse_row = lse_ref[qi, 0:1, :] * LOG2_E
            delta_row = delta_ref[qi, 0:1, :]
            q_chunk = (lax.broadcasted_iota(jnp.int32, (1, M), 1) & (TQ - 1)) >> 6
            for r in range(0, TK, VPU_ROWS):
                rows = slice(r, r + VPU_ROWS)
                p = jnp.exp2(s[rows, :] * EXP2_SCALE - lse_row)
                if first_chunk is not None:
                    p = jnp.where((r >> 6) + first_chunk <= q_chunk, p, 0.0)
                p_ref[rows, :] = p.astype(BF)
                ds_ref[rows, :] = (p * (dp[rows, :] - delta_row) * SM_SCALE).astype(BF)
            ds = ds_ref[...]
            dv = dv + _dot(p_ref[...], do)
            dk = dk + _dot(ds, q)
            dqt_ref[qi] += _dot(k_t, ds)
            return dk, dv

        per = TQ // TK
        first = kt // per
        carry = step(first, (jnp.zeros((TK, QW), F32), jnp.zeros((TK, KVL), F32)), (kt % per) * (TK // 64))
        dk, dv = lax.fori_loop(first + 1, nq, step, carry)
        dkv_ref[...] = dk + jnp.concatenate([dv, jnp.zeros((TK, QW - KVL), F32)], axis=1)
        dqt_out_ref[0] = dqt_ref[first].astype(BF)

    out_shape = (jax.ShapeDtypeStruct((S, QW), F32), jax.ShapeDtypeStruct((nq, QW, M), BF))
    return pl.pallas_call(
        body, name="attn_bwd", out_shape=out_shape, grid=(nk,),
        in_specs=[_vmem(), _rows(TK, QW), _vmem(), _vmem(), _vmem()],
        out_specs=(_rows(TK, QW), pl.BlockSpec((1, QW, M), lambda kt: (kt // (TQ // TK), 0, 0))),
        scratch_shapes=[pltpu.VMEM((nq, QW, M), F32), pltpu.VMEM((TK, M), BF), pltpu.VMEM((TK, M), BF)],
        compiler_params=_params(("arbitrary",)),
    )(qs, kv, dolat, lse, delta)


def _in_bwd(dqt, dkv, du, raw, qn, h1, x, dx2, mod, g_mix, w_in, g_q, g_kv, w_uq, wuk_cd, perm_t, cos4, sin4, csk,
            snk):
    S = x.shape[0]
    ts = 512
    n = S // ts
    nsub = ts // TQ
    M = HEADS * TQ

    def body(dqt_ref, dkv_ref, du_ref, raw_ref, qn_ref, h1_ref, x_ref, dx2_ref, mod_ref, gmix_ref, win_ref, gq_ref,
             gkv_ref, wuq_ref, wuk_ref, permt_ref, cos_ref, sin_ref, csk_ref, snk_ref,
             dx_ref, dwin_ref, dwuq_ref, dwuk_ref, dgq_ref, dgkv_ref, dsc1_ref, dsh1_ref, dgmix_ref, dwin_acc,
             dwuq_acc, dwuk_acc, wuk_bd):
        i = pl.program_id(0)

        @pl.when(i == 0)
        def _():
            dwin_acc[...] = jnp.zeros_like(dwin_acc)
            dwuq_acc[...] = jnp.zeros_like(dwuq_acc)
            dwuk_acc[...] = jnp.zeros_like(dwuk_acc)
            _fill_block_diagonal(wuk_bd, wuk_ref)
            for r in (dgq_ref, dgkv_ref, dsc1_ref, dsh1_ref, dgmix_ref):
                r[...] = jnp.zeros_like(r)

        dq_blocks = [dqt_ref[a].astype(F32).T for a in range(nsub)]
        dq_heads = [jnp.concatenate([blk[hd * TQ:(hd + 1) * TQ, :] for blk in dq_blocks], axis=0)
                    for hd in range(HEADS)]
        dq_lat = jnp.concatenate([dqh[:, 0:KVL] for dqh in dq_heads], axis=1).astype(BF)
        dq_rope = jnp.concatenate([dqh[:, KVL:QW] for dqh in dq_heads], axis=1).astype(BF)
        dq_nope = _dot(dq_lat, wuk_bd[...])
        dwuk_acc[...] += _dot_tn(dq_lat, qn_ref[...])
        drope = _dot(dq_rope, permt_ref[...])
        do1 = drope[:, 0:128]
        do2 = drope[:, 128:256]
        cosv = cos_ref[...]
        sinv = sin_ref[...]
        dq = jnp.concatenate([dq_nope, do1 * cosv + do2 * sinv, do2 * cosv - do1 * sinv], axis=1).astype(BF)

        cq_raw = raw_ref[:, 0:QL]
        ckv_raw = raw_ref[:, QL:QL + KVL]
        rq = _rms(cq_raw)
        nq_ = cq_raw * rq
        gq = gq_ref[...]
        dwuq_acc[...] += _dot_tn((nq_ * gq).astype(BF), dq)
        dc_q = _dot_nt(dq, wuq_ref[...])
        dgq_ref[...] += _colsum(dc_q * nq_)
        dcq_raw = _rms_bwd(dc_q * gq, nq_, rq)

        dkv = dkv_ref[...]
        rk = _rms(ckv_raw)
        nk_ = ckv_raw * rk
        dc_kv = dkv[:, 0:KVL]
        dgkv_ref[...] += _colsum(dc_kv * nk_)
        dckv_raw = _rms_bwd(dc_kv * gkv_ref[...], nk_, rk)
        dkr_roped = dkv[:, KVL:QW]
        dkr = dkr_roped * csk_ref[...] - _swap_halves(dkr_roped) * snk_ref[...]

        dproj = jnp.concatenate([dcq_raw.astype(BF), dckv_raw.astype(BF), dkr.astype(BF), du_ref[...]], axis=1)
        dwin_acc[...] += _dot_tn(h1_ref[...], dproj)
        dh1 = _dot_nt(dproj, win_ref[...])

        sc1 = mod_ref[0:1, D:2 * D]
        gmix = gmix_ref[...]
        xv = x_ref[...]
        r1 = _rms(xv)
        xn1 = xv * r1
        along = _colsum(dh1 * xn1)
        dsc1_ref[...] += along * gmix
        dsh1_ref[...] += _colsum(dh1)
        dgmix_ref[...] += along * (1.0 + sc1)
        dx_ref[...] = dx2_ref[...] + _rms_bwd(dh1 * (gmix * (1.0 + sc1)), xn1, r1)

        @pl.when(i == n - 1)
        def _():
            dwin_ref[...] = dwin_acc[...].astype(BF)
            dwuq_ref[...] = dwuq_acc[...].astype(BF)
            for hd in range(HEADS):
                dwuk_ref[hd] = dwuk_acc[hd * KVL:(hd + 1) * KVL, hd * NOPE:(hd + 1) * NOPE]

    out_shape = (
        jax.ShapeDtypeStruct((S, D), F32),
        jax.ShapeDtypeStruct((D, D), BF),
        jax.ShapeDtypeStruct((QL, 768), BF),
        jax.ShapeDtypeStruct((HEADS, KVL, NOPE), F32),
        jax.ShapeDtypeStruct((1, QL), F32), jax.ShapeDtypeStruct((1, KVL), F32),
        jax.ShapeDtypeStruct((1, D), F32), jax.ShapeDtypeStruct((1, D), F32), jax.ShapeDtypeStruct((1, D), F32),
    )
    in_specs = [pl.BlockSpec((nsub, QW, M), lambda i: (i, 0, 0)), _rows(ts, QW), _rows(ts, PW), _rows(ts, 384),
                _rows(ts, HEADS * NOPE), _rows(ts, D), _rows(ts, D), _rows(ts, D), _full(mod.shape), _full((1, D)),
                _full(w_in.shape), _full((1, QL)), _full((1, KVL)), _full(w_uq.shape), _full(wuk_cd.shape),
                _full(perm_t.shape), _rows(ts, 128), _rows(ts, 128), _rows(ts, 128), _rows(ts, 128)]
    out_specs = (_rows(ts, D), _full((D, D)), _full((QL, 768)), _full((HEADS, KVL, NOPE)), _full((1, QL)),
                 _full((1, KVL)), _full((1, D)), _full((1, D)), _full((1, D)))
    return pl.pallas_call(
        body, name="in_bwd", out_shape=out_shape, grid=(n,), in_specs=in_specs, out_specs=out_specs,
        scratch_shapes=[pltpu.VMEM((D, D), F32), pltpu.VMEM((QL, 768), F32),
                        pltpu.VMEM((HEADS * KVL, HEADS * NOPE), F32), pltpu.VMEM((HEADS * KVL, HEADS * NOPE), BF)],
        compiler_params=_params(("arbitrary",)),
    )(dqt, dkv, du, raw, qn, h1, x, dx2, mod, g_mix, w_in, g_q, g_kv, w_uq, wuk_cd, perm_t, cos4, sin4, csk, snk)


def _rope_perm():
    p = np.zeros((HEADS, 2 * 128, 128), np.float32)
    for hd in range(HEADS):
        for t in range(HALF):
            p[hd, hd * HALF + t, t] = 1.0
            p[hd, 128 + hd * HALF + t, HALF + t] = 1.0
    return p


def _rope_tables(positions):
    freqs = jnp.power(ROPE_THETA, -jnp.arange(HALF, dtype=F32) / HALF)
    ang = positions.astype(F32)[:, None] * jnp.tile(freqs, HEADS)[None, :]
    cos4 = jnp.cos(ang)
    sin4 = jnp.sin(ang)
    lane = jnp.arange(HEADS * HALF)[None, :]
    csk = jnp.where(lane < ROPE, cos4, 0.0)
    snk = jnp.where(lane < HALF, -sin4, jnp.where(lane < ROPE, sin4, 0.0))
    return cos4, sin4, csk, snk


def _local_step(x, rope, target, mod, g_mix, w_in_p, g_q, g_kv, w_uq_p, w_uk, w_uv, w_pool, pool_scale, g_ffn,
                g_final, late, ffn_grads_exchange):
    perm = jnp.asarray(_rope_perm().transpose(1, 0, 2).reshape(2 * 128, HEADS * 128), BF)
    perm_t = jnp.asarray(_rope_perm().transpose(0, 2, 1).reshape(HEADS * 128, 2 * 128), BF)
    cos4, sin4, csk, snk = rope
    wuk_dc = w_uk.transpose(1, 2, 0).astype(BF)
    wuk_cd = w_uk.transpose(1, 0, 2).astype(BF)
    wuv_vc = w_uv.transpose(1, 2, 0).astype(BF)
    wpool = w_pool.astype(BF)
    wpool_dc = w_pool.transpose(0, 2, 1).astype(BF)

    h1, raw, qn, qs, kv, kvt, pooled, ypre, ypool = _fwd_in(
        x, mod, g_mix, w_in_p, g_q, g_kv, w_uq_p, wuk_dc, perm, cos4, sin4, csk, snk, wpool, pool_scale)
    olat, ymla, lse = _attn_fwd(qs, kv, kvt, wuv_vc)
    w_o, wg_t, wu_t, wd = late
    (h2_t, a, b, da, db, dff_t, dx2, dmix, cat_t, dymla, dypool, loss, dgfin, dgt2, dgt1, dsc2, dsh2,
     dgffn) = _ffn_fwd_bwd(x, ymla, ypool, mod, w_o, g_ffn, wg_t, wu_t, wd, g_final, target)
    dwg_t, dwu_t, dwd, dwo = _row_contracted_grads(dff_t, h2_t, cat_t, da, db, a, b, dmix)
    ffn_parts = ffn_grads_exchange((dwg_t, dwu_t, dwd, dwo))
    du, dolat, delta, dwuv, dwpool, dpscale = _mix_bwd(
        dymla, dypool, ypre, pooled, pool_scale, wpool_dc, olat, wuv_vc)
    dkv, dqt = _attn_bwd(qs, kv, dolat, lse, delta)
    dx, dwin, dwuq, dwuk, dgq, dgkv, dsc1, dsh1, dgmix = _in_bwd(
        dqt, dkv, du, raw, qn, h1, x, dx2, mod, g_mix, w_in_p, g_q, g_kv, w_uq_p, wuk_cd, perm_t, cos4, sin4, csk,
        snk)
    dmod = jnp.concatenate([dsh1, dsc1, dgt1, dsh2, dsc2, dgt2], axis=1)
    replicated = dict(
        w_uk=dwuk.transpose(1, 0, 2), w_uv=dwuv.transpose(1, 0, 2), w_pool=dwpool, g_mix=dgmix, g_q=dgq, g_kv=dgkv,
        pool_scale=dpscale, g_ffn=dgffn, g_final=dgfin)
    return loss[0, 0], dx, dmod, (dwin, dwuq), ffn_parts, replicated


def _my_pos():
    return lax.axis_index("x"), lax.axis_index("y"), lax.axis_index("c")


def _peer(pos, k):
    x, y, c = pos
    return (1 - x if k & 4 else x, 1 - y if k & 2 else y, 1 - c if k & 1 else c)


def _index(pos):
    x, y, c = pos
    return 4 * x + 2 * y + c


def _remote(src, dst, send_sem, recv_sem, to):
    return pltpu.make_async_remote_copy(src_ref=src, dst_ref=dst, send_sem=send_sem, recv_sem=recv_sem,
                                        device_id=to, device_id_type=MESH)


def _ada_mod(c, w_ada, b_ada, after):
    def body(c_ref, w_ref, b_ref, after_ref, mod_ref, call_ref, cbuf, sbuf, rbuf, send1, recv1, send2, recv2):
        me = _my_pos()
        mi = _index(me)
        cv = c_ref[...]
        cbuf[...] = jnp.broadcast_to(cv * jax.nn.sigmoid(cv), (8, D))
        call_ref[mi] = cbuf[...]
        first = [_remote(cbuf, call_ref.at[mi], send1.at[k - 1], recv1.at[k - 1], _peer(me, k)) for k in range(1, NDEV)]
        for cp in first:
            cp.start()
        for k in range(1, NDEV):
            _remote(cbuf, call_ref.at[_index(_peer(me, k))], send1.at[k - 1], recv1.at[k - 1], _peer(me, k)).wait_recv()
        c_all = jnp.concatenate([call_ref[b][0:1, :] for b in range(NDEV)], axis=0)
        blocks = _dot(c_all.astype(BF), w_ref[...].astype(BF))
        for b in range(NDEV):
            sbuf[b] = jnp.broadcast_to(blocks[b:b + 1, :], (8, MODC))
        second = []
        for k in range(1, NDEV):
            to = _peer(me, k)
            second.append(_remote(sbuf.at[_index(to)], rbuf.at[mi], send2.at[k - 1], recv2.at[k - 1], to))
        for cp in second:
            cp.start()
        rbuf[mi] = sbuf[mi]
        for k in range(1, NDEV):
            to = _peer(me, k)
            _remote(sbuf.at[_index(to)], rbuf.at[_index(to)], send2.at[k - 1], recv2.at[k - 1], to).wait_recv()
        for j in range(NDEV):
            mod_ref[:, j * MODC:(j + 1) * MODC] = rbuf[j] + b_ref[:, j * MODC:(j + 1) * MODC]
        for cp in first + second:
            cp.wait_send()

    return pl.pallas_call(
        body, name="ada_mod",
        out_shape=(jax.ShapeDtypeStruct((8, N_MOD * D), F32), jax.ShapeDtypeStruct((NDEV, 8, D), F32)),
        in_specs=[_vmem(), _vmem(), _vmem(), _any()], out_specs=(_vmem(), _vmem()),
        scratch_shapes=[pltpu.VMEM((8, D), F32), pltpu.VMEM((NDEV, 8, MODC), F32), pltpu.VMEM((NDEV, 8, MODC), F32),
                        pltpu.SemaphoreType.DMA((NDEV - 1,)), pltpu.SemaphoreType.DMA((NDEV - 1,)),
                        pltpu.SemaphoreType.DMA((NDEV - 1,)), pltpu.SemaphoreType.DMA((NDEV - 1,))],
        compiler_params=_params(),
    )(c, w_ada, b_ada, after)


def _sequencer_scatter(name, collective_id, srcs, after=()):
    n = len(srcs)

    def of(src, to_index):
        r = src.shape[0] // NDEV
        return src.at[pl.ds(pl.multiple_of(to_index * r, 16), r), :]

    def body(*refs):
        src, zone = refs[:n], refs[n + len(after):2 * n + len(after)]
        send, recv, local = refs[2 * n + len(after):]
        me = _my_pos()
        mi = _index(me)
        barrier = pltpu.get_barrier_semaphore()
        for k in range(1, NDEV):
            pl.semaphore_signal(barrier, inc=1, device_id=_peer(me, k), device_id_type=MESH)
        pl.semaphore_wait(barrier, NDEV - 1)
        own = [pltpu.make_async_copy(of(src[a], mi), zone[a].at[mi], local.at[a]) for a in range(n)]
        for cp in own:
            cp.start()
        for a in range(n):
            for k in range(1, NDEV):
                to = _peer(me, k)
                s = a * (NDEV - 1) + k - 1
                _remote(of(src[a], _index(to)), zone[a].at[mi], send.at[s], recv.at[s], to).start()
        for cp in own:
            cp.wait()
        for a in range(n):
            for k in range(1, NDEV):
                to = _peer(me, k)
                s = a * (NDEV - 1) + k - 1
                cp = _remote(of(src[a], mi), zone[a].at[_index(to)], send.at[s], recv.at[s], to)
                cp.wait_send()
                cp.wait_recv()

    return pl.kernel(
        body, name=name, mesh=plsc.ScalarSubcoreMesh(axis_name="sequencer", num_cores=1),
        out_type=tuple(jax.ShapeDtypeStruct((NDEV, s.shape[0] // NDEV, s.shape[1]), s.dtype) for s in srcs),
        scratch_types=[pltpu.SemaphoreType.DMA((n * (NDEV - 1),)), pltpu.SemaphoreType.DMA((n * (NDEV - 1),)),
                       pltpu.SemaphoreType.DMA((n,))],
        compiler_params=pltpu.CompilerParams(collective_id=collective_id),
    )(*srcs, *after)


CHIP_PEERS = (2, 4, 6)


def _sequencer_gather(name, collective_id, srcs, after=()):
    n = len(srcs)
    per = NDEV - 1

    def body(*refs):
        src, zone = refs[:n], refs[n + len(after):2 * n + len(after)]
        send, recv, local = refs[2 * n + len(after):]
        me = _my_pos()
        mi = _index(me)
        sibling = _peer(me, 1)
        talk_to = (sibling,) + tuple(_peer(me, k) for k in CHIP_PEERS)
        barrier = pltpu.get_barrier_semaphore()
        for to in talk_to:
            pl.semaphore_signal(barrier, inc=1, device_id=to, device_id_type=MESH)
        pl.semaphore_wait(barrier, len(talk_to))

        def copy(a, slot, block_of, to, from_src=False):
            rows = zone[a].at[_index(block_of)]
            return _remote(src[a] if from_src else rows, rows, send.at[a * per + slot], recv.at[a * per + slot], to)

        own = [pltpu.make_async_copy(src[a], zone[a].at[mi], local.at[a]) for a in range(n)]
        for cp in own:
            cp.start()
        started = []
        for a in range(n):
            started.append(copy(a, 0, me, sibling, from_src=True))
            started += [copy(a, 1 + j, me, _peer(me, k), from_src=True) for j, k in enumerate(CHIP_PEERS)]
        for cp in started:
            cp.start()
        for a in range(n):
            for j, k in enumerate(CHIP_PEERS):
                copy(a, 1 + j, _peer(me, k), me).wait_recv()
                passed = copy(a, 4 + j, _peer(me, k), sibling)
                passed.start()
                started.append(passed)
        for a in range(n):
            copy(a, 0, sibling, me).wait_recv()
            for j, k in enumerate(CHIP_PEERS):
                copy(a, 4 + j, _peer(me, k | 1), me).wait_recv()
        for cp in started:
            cp.wait_send()
        for cp in own:
            cp.wait()

    return pl.kernel(
        body, name=name, mesh=plsc.ScalarSubcoreMesh(axis_name="sequencer", num_cores=1),
        out_type=tuple(jax.ShapeDtypeStruct((NDEV,) + s.shape, s.dtype) for s in srcs),
        scratch_types=[pltpu.SemaphoreType.DMA((n * per,)), pltpu.SemaphoreType.DMA((n * per,)),
                       pltpu.SemaphoreType.DMA((n,))],
        compiler_params=pltpu.CompilerParams(collective_id=collective_id),
    )(*srcs, *after)


def _blocked(shape, nb, axis=0):
    block = tuple(s // nb if d == axis else s for d, s in enumerate(shape))
    return pl.BlockSpec(block, lambda i: tuple(i if d == axis else 0 for d in range(len(shape))))


def _sum_partials(name, parts, nb):
    n = len(parts)

    def body(*refs):
        for a in range(n):
            acc = refs[a][0].astype(F32)
            for p in range(1, NDEV):
                acc = acc + refs[a][p].astype(F32)
            refs[n + a][...] = acc

    return pl.pallas_call(
        body, name=name, grid=(nb,),
        out_shape=tuple(jax.ShapeDtypeStruct(p.shape[1:], F32) for p in parts),
        in_specs=[_blocked(p.shape, nb, 1) for p in parts],
        out_specs=tuple(_blocked(p.shape[1:], nb) for p in parts), compiler_params=_params(("arbitrary",)),
    )(*parts)


def _small_all_reduce(buf):
    def body(buf_ref, got_ref, red_ref, mine, send1, recv1, send2, recv2):
        me = _my_pos()
        mi = _index(me)
        first = []
        for k in range(1, NDEV):
            to = _peer(me, k)
            first.append(_remote(buf_ref.at[_index(to)], got_ref.at[mi], send1.at[k - 1], recv1.at[k - 1], to))
        for cp in first:
            cp.start()
        got_ref[mi] = buf_ref[mi]
        for k in range(1, NDEV):
            to = _peer(me, k)
            _remote(buf_ref.at[mi], got_ref.at[_index(to)], send1.at[k - 1], recv1.at[k - 1], to).wait_recv()
        acc = got_ref[0]
        for p in range(1, NDEV):
            acc = acc + got_ref[p]
        mine[...] = acc
        second = [_remote(mine, red_ref.at[mi], send2.at[k - 1], recv2.at[k - 1], _peer(me, k)) for k in range(1, NDEV)]
        for cp in second:
            cp.start()
        red_ref[mi] = acc
        for k in range(1, NDEV):
            to = _peer(me, k)
            _remote(mine, red_ref.at[_index(to)], send2.at[k - 1], recv2.at[k - 1], to).wait_recv()
        for cp in first + second:
            cp.wait_send()

    return pl.pallas_call(
        body, name="small_all_reduce",
        out_shape=(jax.ShapeDtypeStruct(buf.shape, F32), jax.ShapeDtypeStruct(buf.shape, F32)),
        in_specs=[_vmem()], out_specs=(_vmem(), _vmem()),
        scratch_shapes=[pltpu.VMEM(buf.shape[1:], F32),
                        pltpu.SemaphoreType.DMA((NDEV - 1,)), pltpu.SemaphoreType.DMA((NDEV - 1,)),
                        pltpu.SemaphoreType.DMA((NDEV - 1,)), pltpu.SemaphoreType.DMA((NDEV - 1,))],
        compiler_params=_params(),
    )(buf)


def _adamw_math(w, g, m, v):
    m = ADAM_B1 * m + (1.0 - ADAM_B1) * g
    v = ADAM_B2 * v + (1.0 - ADAM_B2) * jnp.square(g)
    m_hat = m / (1.0 - ADAM_B1 ** ADAM_STEP)
    v_hat = v / (1.0 - ADAM_B2 ** ADAM_STEP)
    delta = -ADAM_LR * (m_hat / (jnp.sqrt(v_hat) + ADAM_EPS) + ADAM_WD * w)
    return delta, m, v


def _adamw_group(name, ws, gs, ms, vs, nb):
    n = len(ws)

    def body(*refs):
        for a in range(n):
            w, g, m, v = (refs[q * n + a][...] for q in range(4))
            delta, m2, v2 = _adamw_math(w, g, m, v)
            refs[4 * n + a][...] = delta
            refs[5 * n + a][...] = m2
            refs[6 * n + a][...] = v2

    shapes = tuple(jax.ShapeDtypeStruct(w.shape, F32) for w in ws)
    specs = [_blocked(w.shape, nb) for w in ws]
    outs = pl.pallas_call(
        body, name=name, grid=(nb,), out_shape=shapes * 3, in_specs=specs * 4, out_specs=tuple(specs * 3),
        compiler_params=_params(("arbitrary",)),
    )(*ws, *gs, *ms, *vs)
    return outs[:n], outs[n:2 * n], outs[2 * n:]


def _adamw_from_partials(name, ws, parts, ms, vs, nb):
    n = len(ws)

    def body(*refs):
        for a in range(n):
            part = refs[n + a]
            g = part[0].astype(F32)
            for p in range(1, NDEV):
                g = g + part[p].astype(F32)
            delta, m2, v2 = _adamw_math(refs[a][...], g, refs[2 * n + a][...], refs[3 * n + a][...])
            refs[4 * n + a][...] = g
            refs[5 * n + a][...] = delta
            refs[6 * n + a][...] = m2
            refs[7 * n + a][...] = v2

    shapes = tuple(jax.ShapeDtypeStruct(w.shape, F32) for w in ws)
    specs = [_blocked(w.shape, nb) for w in ws]
    outs = pl.pallas_call(
        body, name=name, grid=(nb,), out_shape=shapes * 4,
        in_specs=specs + [_blocked(p.shape, nb, 1) for p in parts] + specs * 2, out_specs=tuple(specs * 4),
        compiler_params=_params(("arbitrary",)),
    )(*ws, *parts, *ms, *vs)
    return outs[:n], outs[n:2 * n], outs[2 * n:3 * n], outs[3 * n:]


def _adamw_ada(w, m, v, c_all_t, dmod_rows):
    nb = 4

    def body(w_ref, m_ref, v_ref, c_ref, dm_ref, g_ref, d_ref, m2_ref, v2_ref):
        g = _dot(c_ref[...], dm_ref[...].astype(BF))
        g_ref[...] = g
        delta, m2, v2 = _adamw_math(w_ref[...], g, m_ref[...], v_ref[...])
        d_ref[...] = delta
        m2_ref[...] = m2
        v2_ref[...] = v2

    shp = jax.ShapeDtypeStruct(w.shape, F32)
    spec = _blocked(w.shape, nb)
    return pl.pallas_call(
        body, name="adamw_ada", grid=(nb,), out_shape=(shp, shp, shp, shp),
        in_specs=[spec, spec, spec, _blocked(c_all_t.shape, nb), _full(dmod_rows.shape)],
        out_specs=(spec, spec, spec, spec), compiler_params=_params(("arbitrary",)),
    )(w, m, v, c_all_t, dmod_rows)


def _w_in_to_kernel(w):
    return jnp.concatenate([w[:, 0:448], jnp.zeros((w.shape[0], 64), w.dtype), w[:, 448:960]], axis=1)


def _w_in_from_kernel(w):
    return jnp.concatenate([w[:, 0:448], w[:, 512:1024]], axis=1)


def _w_uq_to_kernel(w):
    r = w.shape[0]
    return jnp.concatenate([w[:, :, 0:NOPE].reshape(r, HEADS * NOPE),
                            w[:, :, NOPE:NOPE + HALF].reshape(r, HEADS * HALF),
                            w[:, :, NOPE + HALF:].reshape(r, HEADS * HALF)], axis=1)


def _w_uq_from_kernel(w):
    r = w.shape[0]
    return jnp.concatenate([w[:, 0:512].reshape(r, HEADS, NOPE), w[:, 512:640].reshape(r, HEADS, HALF),
                            w[:, 640:768].reshape(r, HEADS, HALF)], axis=2)


REP_NAMES = ("w_uk", "w_uv", "w_pool", "g_mix", "g_q", "g_kv", "pool_scale", "g_ffn", "g_final")


def kernel(x, c, positions, w_ada, b_ada, g_mix, w_in, g_q, g_kv, w_uq, w_uk, w_uv, w_pool, pool_scale, w_o, g_ffn, w_gate, w_up, w_down, g_final, loss_target, m_w_ada, m_b_ada, m_g_mix, m_w_in, m_g_q, m_g_kv, m_w_uq, m_w_uk, m_w_uv, m_w_pool, m_pool_scale, m_w_o, m_g_ffn, m_w_gate, m_w_up, m_w_down, m_g_final, v_w_ada, v_b_ada, v_g_mix, v_w_in, v_g_q, v_g_kv, v_w_uq, v_w_uk, v_w_uv, v_w_pool, v_pool_scale, v_w_o, v_g_ffn, v_w_gate, v_w_up, v_w_down, v_g_final):
    given = dict(locals())

    merge = lambda g: g.reshape(NDEV * g.shape[1], g.shape[2])
    w_in_p, w_uq_p = (merge(g) for g in _sequencer_gather(
        "gather_in", 3, (_w_in_to_kernel(w_in[0]).astype(BF), _w_uq_to_kernel(w_uq[0]).astype(BF))))

    rope = _rope_tables(positions[0])
    mod, c_all8 = _ada_mod(c, w_ada[0], b_ada, rope[3][0:8, :])
    c_all = c_all8[:, 0, :]
    late = _sequencer_gather(
        "gather_late", 1, (w_o[0].astype(BF), w_gate[0].T.astype(BF), w_up[0].T.astype(BF), w_down[0].astype(BF)),
        after=(mod[:, 0:128], w_in_p[0:16, 0:128], w_uq_p[0:16, 0:128]))

    def ffn_grads_exchange(arrays):
        return _sequencer_scatter("scatter_ffn", 2, arrays)

    loss, dx, dmod, tail_grads, ffn_parts, replicated = _local_step(
        x[0], rope, loss_target[0], mod, g_mix, w_in_p, g_q, g_kv, w_uq_p, w_uk[0], w_uv[0], w_pool[0],
        pool_scale, g_ffn, g_final.reshape(1, D), tuple(merge(g) for g in late), ffn_grads_exchange)

    flat = jnp.concatenate([replicated[k].reshape(-1) for k in REP_NAMES] + [loss.reshape(1)])
    flat = jnp.pad(flat, (0, NDEV * REP_ROWS * 128 - flat.shape[0])).reshape(NDEV, REP_ROWS, 128)
    dmod_blocks = jnp.pad(dmod.reshape(NDEV, MODC // 128, 128), ((0, 0), (0, MOD_ROWS - MODC // 128), (0, 0)))
    got, red = _small_all_reduce(jnp.concatenate([dmod_blocks, flat], axis=1))

    tail_parts = _sequencer_scatter("scatter_tail", 4, tail_grads,
                                    after=(ffn_parts[0][0, 0:16, 0:128], red[0, 0:8, :]))
    g_in_p, g_uq_p = _sum_partials("sum_tail_partials", tail_parts, 1)
    as_transpose = ("w_in", "w_gate", "w_up")
    grads = dict(w_in=_w_in_from_kernel(g_in_p).T, w_uq=_w_uq_from_kernel(g_uq_p))
    partials = dict(w_gate=ffn_parts[0], w_up=ffn_parts[1], w_down=ffn_parts[2], w_o=ffn_parts[3])
    dmod_rows = got[:, 0:MODC // 128, :].reshape(NDEV, MODC)
    grads["b_ada"] = red[:, 0:MODC // 128, :].reshape(1, N_MOD * D)
    rep_flat = red[:, MOD_ROWS:, :].reshape(-1)
    off = 0
    for k in REP_NAMES:
        size = int(np.prod(given[k].shape))
        grads[k] = rep_flat[off:off + size]
        off += size

    view = {k: (given[k].shape[1:] if given[k].ndim > 2 else given[k].shape)
            for k in REP_NAMES + ("b_ada", "w_ada", "w_in", "w_uq", "w_o", "w_gate", "w_up", "w_down")}
    view.update(g_final=(1, D))
    names = ["w_ada", "b_ada", "g_mix", "w_in", "g_q", "g_kv", "w_uq", "w_uk", "w_uv", "w_pool", "pool_scale",
             "w_o", "g_ffn", "w_gate", "w_up", "w_down", "g_final"]
    g_ada, d_ada, m_ada, v_ada = _adamw_ada(w_ada[0], m_w_ada[0], v_w_ada[0], c_all.T.astype(BF), dmod_rows)
    out_g, out_d, out_m, out_v = dict(w_ada=g_ada), dict(w_ada=d_ada), dict(w_ada=m_ada), dict(w_ada=v_ada)
    groups = (("adamw_ffn", ("w_gate", "w_up", "w_down", "w_o"), 4),
              ("adamw_tail", REP_NAMES + ("b_ada", "w_in", "w_uq"), 1))
    for gname, members, nb in groups:
        turn = lambda k, t: t.T if k in as_transpose else t
        ws = [turn(k, given[k].reshape(view[k])) for k in members]
        ms = [turn(k, given["m_" + k].reshape(view[k])) for k in members]
        vs = [turn(k, given["v_" + k].reshape(view[k])) for k in members]
        if members[0] in partials:
            gs, ds, m2, v2 = _adamw_from_partials(gname, ws, [partials[k] for k in members], ms, vs, nb)
        else:
            gs = [grads[k] if k in as_transpose else grads[k].reshape(view[k]) for k in members]
            ds, m2, v2 = _adamw_group(gname, ws, gs, ms, vs, nb)
        for k, g, d, mm, vv in zip(members, gs, ds, m2, v2):
            out_g[k], out_d[k], out_m[k], out_v[k] = turn(k, g), turn(k, d), turn(k, mm), turn(k, vv)

    total = rep_flat[off]
    shaped = lambda d: [d[k].reshape(given[k].shape) for k in names]
    return (total, dx[None], *shaped(out_g), *shaped(out_d), *shaped(out_m), *shaped(out_v))
```
